```python
import jax, jax.numpy as jnp
from jax import lax
import numpy as np

D_MODEL = 1024
BATCH = 8
SEQ = 4096
DEPTH = 1

CHUNK = 64
Q_BLOCK = 128
ML_HEADS = 4
ML_DQK = D_MODEL // 8
ML_DV = D_MODEL // 4
ML_QK = ML_HEADS * ML_DQK
ML_V = ML_HEADS * ML_DV
FOX_HEADS = 8
FOX_DH = D_MODEL // 8
FOX_W = FOX_HEADS * FOX_DH
D_FF = 2816
CONV_W = 3
GATE_CAP = 15.0
EPS = 1e-6
SPLITS = (ML_QK, ML_QK, ML_V, ML_HEADS, ML_HEADS, ML_V,
          FOX_W, FOX_W, FOX_W, FOX_HEADS, D_MODEL, D_MODEL)
D_IN = 2 * ML_QK + 2 * ML_V + 2 * ML_HEADS + 3 * FOX_W + FOX_HEADS + 2 * D_MODEL

kernel_name = "hybrid_mlstm_fox_convffn"


def rmsnorm(x, w):
    xf = x.astype(jnp.float32)
    y = xf * lax.rsqrt(jnp.mean(xf * xf, axis=-1, keepdims=True) + EPS)
    return (y * w.astype(jnp.float32)).astype(x.dtype)


def softcap(z):
    return GATE_CAP * jnp.tanh(z / GATE_CAP)


def causal_dwconv(u, w, b):
    S = u.shape[1]
    up = jnp.pad(u, ((0, 0), (CONV_W - 1, 0), (0, 0)))
    y = b
    for j in range(CONV_W):
        y = y + w[j] * up[:, j:j + S]
    return y


def mlstm_chunkwise(q, k, v, log_i, log_f):
    B, S, H, DK = q.shape
    DV = v.shape[-1]
    NC = S // CHUNK
    q = q.astype(jnp.float32) * (DK ** -0.5)
    k = k.astype(jnp.float32)
    v = v.astype(jnp.float32)

    def to_chunks(t):
        t = t.reshape((B, NC, CHUNK, H) + t.shape[3:])
        return jnp.moveaxis(t, (1, 3), (0, 2))

    causal = jnp.tril(jnp.ones((CHUNK, CHUNK), dtype=bool))

    def step(carry, xs):
        C, n, m = carry
        qc, kc, vc, li, lf = xs
        b = jnp.cumsum(lf, axis=-1)
        dmat = b[..., :, None] - b[..., None, :] + li[..., None, :]
        dmat = jnp.where(causal, dmat, -jnp.inf)
        inter = b + m[..., None]
        m_t = jnp.maximum(inter, jnp.max(dmat, axis=-1))
        w_intra = jnp.exp(dmat - m_t[..., None])
        w_inter = jnp.exp(inter - m_t)
        s = jnp.einsum('bhtd,bhsd->bhts', qc, kc) * w_intra
        num = (jnp.einsum('bhts,bhsv->bhtv', s, vc)
               + w_inter[..., None] * jnp.einsum('bhvd,bhtd->bhtv', C, qc))
        den = jnp.sum(s, axis=-1) + w_inter * jnp.einsum('bhd,bhtd->bht', n, qc)
        h = num / jnp.maximum(jnp.abs(den), jnp.exp(-m_t))[..., None]
        b_last = b[..., -1]
        g = b_last[..., None] - b + li
        m_new = jnp.maximum(b_last + m, jnp.max(g, axis=-1))
        a = jnp.exp(b_last + m - m_new)
        wk = jnp.exp(g - m_new[..., None])
        C_new = a[..., None, None] * C + jnp.einsum('bhs,bhsv,bhsd->bhvd', wk, vc, kc)
        n_new = a[..., None] * n + jnp.einsum('bhs,bhsd->bhd', wk, kc)
        return (C_new, n_new, m_new), h

    init = (jnp.zeros((B, H, DV, DK), jnp.float32),
            jnp.zeros((B, H, DK), jnp.float32),
            jnp.zeros((B, H), jnp.float32))
    xs = (to_chunks(q), to_chunks(k), to_chunks(v), to_chunks(log_i), to_chunks(log_f))
    _, hs = lax.scan(step, init, xs)
    return jnp.moveaxis(hs, (0, 2), (1, 3)).reshape(B, S, H, DV)


def forgetting_attention(q, k, v, log_f):
    B, S, H, D = q.shape
    nb = S // Q_BLOCK
    F = jnp.moveaxis(jnp.cumsum(log_f, axis=1), 1, 2)
    qb = q.reshape(B, nb, Q_BLOCK, H, D).transpose(1, 0, 3, 2, 4)
    Fb = F.reshape(B, H, nb, Q_BLOCK).transpose(2, 0, 1, 3)
    kpos = jnp.arange(S)
    scale = D ** -0.5

    def block(args):
        qi, Fi, i = args
        logits = (jnp.einsum('bhqd,bshd->bhqs', qi, k).astype(jnp.float32) * scale
                  + Fi[..., None] - F[:, :, None, :])
        qpos = i * Q_BLOCK + jnp.arange(Q_BLOCK)
        logits = jnp.where(kpos[None, :] <= qpos[:, None], logits, -jnp.inf)
        p = jax.nn.softmax(logits, axis=-1)
        return jnp.einsum('bhqs,bshd->bqhd', p.astype(v.dtype), v)

    out = lax.map(block, (qb, Fb, jnp.arange(nb)))
    return out.transpose(1, 0, 2, 3, 4).reshape(B, S, H, D)


def _fwd_setup_inputs(seed: int = 0) -> dict:
    key = jax.random.key(seed)
    ks = jax.random.split(key, 19)
    f32 = jnp.float32

    def dense(k, fan_in, shape):
        return jax.random.normal(k, (DEPTH,) + shape, f32) * fan_in ** -0.5

    def around(k, n, center):
        return center + 0.05 * jax.random.normal(k, (DEPTH, n), f32)

    return {
        "x": jax.random.normal(ks[0], (BATCH, SEQ, D_MODEL), f32),
        "norm_mix_pre": around(ks[1], D_MODEL, 1.0),
        "w_in": dense(ks[2], D_MODEL, (D_MODEL, D_IN)),
        "b_ml_i": around(ks[3], ML_HEADS, 0.0),
        "b_ml_f": around(ks[4], ML_HEADS, jnp.linspace(3.0, 6.0, ML_HEADS)),
        "ml_head_norm": around(ks[5], ML_V, 1.0),
        "b_fox_f": around(ks[6], FOX_HEADS, 2.0),
        "b_gate_a": around(ks[7], D_MODEL, 0.0),
        "b_gate_b": around(ks[8], D_MODEL, 0.0),
        "w_branch_a": dense(ks[9], ML_V, (ML_V, D_MODEL)),
        "w_branch_b": dense(ks[10], FOX_W, (FOX_W, D_MODEL)),
        "w_out": dense(ks[11], D_MODEL, (D_MODEL, D_MODEL)),
        "norm_mix_post": around(ks[12], D_MODEL, 1.0),
        "norm_ffn_pre": around(ks[13], D_MODEL, 1.0),
        "w_up": dense(ks[14], D_MODEL, (D_MODEL, 2 * D_FF)),
        "conv_w": dense(ks[15], CONV_W, (CONV_W, 2 * D_FF)),
        "conv_b": around(ks[16], 2 * D_FF, 0.0),
        "w_down": dense(ks[17], D_FF, (D_FF, D_MODEL)),
        "norm_ffn_post": around(ks[18], D_MODEL, 1.0),
    }


def _fwd_reference(x, norm_mix_pre, w_in, b_ml_i, b_ml_f, ml_head_norm, b_fox_f, b_gate_a, b_gate_b,
              w_branch_a, w_branch_b, w_out, norm_mix_post, norm_ffn_pre, w_up, conv_w, conv_b,
              w_down, norm_ffn_post):
    B, S, _ = x.shape
    f32 = jnp.float32
    offsets = np.cumsum(SPLITS)[:-1].tolist()
    for l in range(DEPTH):
        h = rmsnorm(x, norm_mix_pre[l])
        proj = h @ w_in[l]
        (q_m, k_m, v_m, i_m, f_m, o_m,
         q_f, k_f, v_f, f_f, g_a, g_b) = jnp.split(proj, offsets, axis=-1)

        log_i = softcap(i_m.astype(f32) + b_ml_i[l])
        log_f = jax.nn.log_sigmoid(softcap(f_m.astype(f32) + b_ml_f[l]))
        h_a = mlstm_chunkwise(q_m.reshape(B, S, ML_HEADS, ML_DQK),
                              k_m.reshape(B, S, ML_HEADS, ML_DQK),
                              v_m.reshape(B, S, ML_HEADS, ML_DV), log_i, log_f)
        h_a = rmsnorm(h_a, ml_head_norm[l].reshape(ML_HEADS, ML_DV))
        h_a = h_a.reshape(B, S, ML_V).astype(x.dtype) * jax.nn.sigmoid(o_m)
        y_a = h_a @ w_branch_a[l]

        log_fg = jax.nn.log_sigmoid(f_f.astype(f32) + b_fox_f[l])
        h_b = forgetting_attention(q_f.reshape(B, S, FOX_HEADS, FOX_DH),
                                   k_f.reshape(B, S, FOX_HEADS, FOX_DH),
                                   v_f.reshape(B, S, FOX_HEADS, FOX_DH), log_fg)
        y_b = h_b.reshape(B, S, FOX_W) @ w_branch_b[l]

        merged = jax.nn.sigmoid(g_a + b_gate_a[l]) * y_a + jax.nn.sigmoid(g_b + b_gate_b[l]) * y_b
        x = x + rmsnorm(merged @ w_out[l], norm_mix_post[l])

        h = rmsnorm(x, norm_ffn_pre[l])
        u = causal_dwconv(h @ w_up[l], conv_w[l], conv_b[l])
        a, g = jnp.split(u, 2, axis=-1)
        x = x + rmsnorm((jax.nn.gelu(g) * a) @ w_down[l], norm_ffn_post[l])
    return x


import jax as _jax
import jax.numpy as _jnp

TWIN_FORMAT = 'train_step'
FWD_PARAMS = ['x', 'norm_mix_pre', 'w_in', 'b_ml_i', 'b_ml_f', 'ml_head_norm', 'b_fox_f', 'b_gate_a', 'b_gate_b', 'w_branch_a', 'w_branch_b', 'w_out', 'norm_mix_post', 'norm_ffn_pre', 'w_up', 'conv_w', 'conv_b', 'w_down', 'norm_ffn_post']
TWIN_WEIGHTS = ['norm_mix_pre', 'w_in', 'b_ml_i', 'b_ml_f', 'ml_head_norm', 'b_fox_f', 'b_gate_a', 'b_gate_b', 'w_branch_a', 'w_branch_b', 'w_out', 'norm_mix_post', 'norm_ffn_pre', 'w_up', 'conv_w', 'conv_b', 'w_down', 'norm_ffn_post']
TWIN_DIFF_INPUT = 'x'
TWIN_INPUTS = ['x', 'norm_mix_pre', 'w_in', 'b_ml_i', 'b_ml_f', 'ml_head_norm', 'b_fox_f', 'b_gate_a', 'b_gate_b', 'w_branch_a', 'w_branch_b', 'w_out', 'norm_mix_post', 'norm_ffn_pre', 'w_up', 'conv_w', 'conv_b', 'w_down', 'norm_ffn_post', 'loss_target', 'm_norm_mix_pre', 'm_w_in', 'm_b_ml_i', 'm_b_ml_f', 'm_ml_head_norm', 'm_b_fox_f', 'm_b_gate_a', 'm_b_gate_b', 'm_w_branch_a', 'm_w_branch_b', 'm_w_out', 'm_norm_mix_post', 'm_norm_ffn_pre', 'm_w_up', 'm_conv_w', 'm_conv_b', 'm_w_down', 'm_norm_ffn_post', 'v_norm_mix_pre', 'v_w_in', 'v_b_ml_i', 'v_b_ml_f', 'v_ml_head_norm', 'v_b_fox_f', 'v_b_gate_a', 'v_b_gate_b', 'v_w_branch_a', 'v_w_branch_b', 'v_w_out', 'v_norm_mix_post', 'v_norm_ffn_pre', 'v_w_up', 'v_conv_w', 'v_conv_b', 'v_w_down', 'v_norm_ffn_post']
TWIN_OUTPUTS = ['loss', 'grad_x', 'grad_norm_mix_pre', 'grad_w_in', 'grad_b_ml_i', 'grad_b_ml_f', 'grad_ml_head_norm', 'grad_b_fox_f', 'grad_b_gate_a', 'grad_b_gate_b', 'grad_w_branch_a', 'grad_w_branch_b', 'grad_w_out', 'grad_norm_mix_post', 'grad_norm_ffn_pre', 'grad_w_up', 'grad_conv_w', 'grad_conv_b', 'grad_w_down', 'grad_norm_ffn_post', 'delta_norm_mix_pre', 'delta_w_in', 'delta_b_ml_i', 'delta_b_ml_f', 'delta_ml_head_norm', 'delta_b_fox_f', 'delta_b_gate_a', 'delta_b_gate_b', 'delta_w_branch_a', 'delta_w_branch_b', 'delta_w_out', 'delta_norm_mix_post', 'delta_norm_ffn_pre', 'delta_w_up', 'delta_conv_w', 'delta_conv_b', 'delta_w_down', 'delta_norm_ffn_post', 'new_m_norm_mix_pre', 'new_m_w_in', 'new_m_b_ml_i', 'new_m_b_ml_f', 'new_m_ml_head_norm', 'new_m_b_fox_f', 'new_m_b_gate_a', 'new_m_b_gate_b', 'new_m_w_branch_a', 'new_m_w_branch_b', 'new_m_w_out', 'new_m_norm_mix_post', 'new_m_norm_ffn_pre', 'new_m_w_up', 'new_m_conv_w', 'new_m_conv_b', 'new_m_w_down', 'new_m_norm_ffn_post', 'new_v_norm_mix_pre', 'new_v_w_in', 'new_v_b_ml_i', 'new_v_b_ml_f', 'new_v_ml_head_norm', 'new_v_b_fox_f', 'new_v_b_gate_a', 'new_v_b_gate_b', 'new_v_w_branch_a', 'new_v_w_branch_b', 'new_v_w_out', 'new_v_norm_mix_post', 'new_v_norm_ffn_pre', 'new_v_w_up', 'new_v_conv_w', 'new_v_conv_b', 'new_v_w_down', 'new_v_norm_ffn_post']
TWIN_LEAF_KINDS = {'loss': 'loss', 'grad_x': 'grad_x', 'grad_norm_mix_pre': 'grad_w', 'grad_w_in': 'grad_w', 'grad_b_ml_i': 'grad_w', 'grad_b_ml_f': 'grad_w', 'grad_ml_head_norm': 'grad_w', 'grad_b_fox_f': 'grad_w', 'grad_b_gate_a': 'grad_w', 'grad_b_gate_b': 'grad_w', 'grad_w_branch_a': 'grad_w', 'grad_w_branch_b': 'grad_w', 'grad_w_out': 'grad_w', 'grad_norm_mix_post': 'grad_w', 'grad_norm_ffn_pre': 'grad_w', 'grad_w_up': 'grad_w', 'grad_conv_w': 'grad_w', 'grad_conv_b': 'grad_w', 'grad_w_down': 'grad_w', 'grad_norm_ffn_post': 'grad_w', 'delta_norm_mix_pre': 'delta_w', 'delta_w_in': 'delta_w', 'delta_b_ml_i': 'delta_w', 'delta_b_ml_f': 'delta_w', 'delta_ml_head_norm': 'delta_w', 'delta_b_fox_f': 'delta_w', 'delta_b_gate_a': 'delta_w', 'delta_b_gate_b': 'delta_w', 'delta_w_branch_a': 'delta_w', 'delta_w_branch_b': 'delta_w', 'delta_w_out': 'delta_w', 'delta_norm_mix_post': 'delta_w', 'delta_norm_ffn_pre': 'delta_w', 'delta_w_up': 'delta_w', 'delta_conv_w': 'delta_w', 'delta_conv_b': 'delta_w', 'delta_w_down': 'delta_w', 'delta_norm_ffn_post': 'delta_w', 'new_m_norm_mix_pre': 'new_m', 'new_m_w_in': 'new_m', 'new_m_b_ml_i': 'new_m', 'new_m_b_ml_f': 'new_m', 'new_m_ml_head_norm': 'new_m', 'new_m_b_fox_f': 'new_m', 'new_m_b_gate_a': 'new_m', 'new_m_b_gate_b': 'new_m', 'new_m_w_branch_a': 'new_m', 'new_m_w_branch_b': 'new_m', 'new_m_w_out': 'new_m', 'new_m_norm_mix_post': 'new_m', 'new_m_norm_ffn_pre': 'new_m', 'new_m_w_up': 'new_m', 'new_m_conv_w': 'new_m', 'new_m_conv_b': 'new_m', 'new_m_w_down': 'new_m', 'new_m_norm_ffn_post': 'new_m', 'new_v_norm_mix_pre': 'new_v', 'new_v_w_in': 'new_v', 'new_v_b_ml_i': 'new_v', 'new_v_b_ml_f': 'new_v', 'new_v_ml_head_norm': 'new_v', 'new_v_b_fox_f': 'new_v', 'new_v_b_gate_a': 'new_v', 'new_v_b_gate_b': 'new_v', 'new_v_w_branch_a': 'new_v', 'new_v_w_branch_b': 'new_v', 'new_v_w_out': 'new_v', 'new_v_norm_mix_post': 'new_v', 'new_v_norm_ffn_pre': 'new_v', 'new_v_w_up': 'new_v', 'new_v_conv_w': 'new_v', 'new_v_conv_b': 'new_v', 'new_v_w_down': 'new_v', 'new_v_norm_ffn_post': 'new_v'}


def _forward(args):
    return _fwd_reference(*[args[k] for k in FWD_PARAMS])


def _output_shape():
    def fwd():
        inp = _fwd_setup_inputs(0)
        return _fwd_reference(*[inp[k] for k in FWD_PARAMS])
    out = _jax.eval_shape(fwd)
    return out.shape, out.dtype

N_MICROBATCH = 1
ADAM_LR = 0.001
ADAM_B1 = 0.9
ADAM_B2 = 0.999
ADAM_EPS = 1e-08
ADAM_WD = 0.01
ADAM_STEP = 10
PER_EXAMPLE_BATCH_AXIS = {'x': 0, 'loss_target': 0}
SHARED_INPUTS = []
_WEIGHT_DTYPES = {'norm_mix_pre': _jnp.float32, 'w_in': _jnp.float32, 'b_ml_i': _jnp.float32, 'b_ml_f': _jnp.float32, 'ml_head_norm': _jnp.float32, 'b_fox_f': _jnp.float32, 'b_gate_a': _jnp.float32, 'b_gate_b': _jnp.float32, 'w_branch_a': _jnp.float32, 'w_branch_b': _jnp.float32, 'w_out': _jnp.float32, 'norm_mix_post': _jnp.float32, 'norm_ffn_pre': _jnp.float32, 'w_up': _jnp.float32, 'conv_w': _jnp.float32, 'conv_b': _jnp.float32, 'w_down': _jnp.float32, 'norm_ffn_post': _jnp.float32}
MOMENT_SCALE = {'norm_mix_pre': 9.065874e-01, 'w_in': 3.123380e-01, 'b_ml_i': 5.857810e-02, 'b_ml_f': 4.042737e+00, 'ml_head_norm': 3.836350e-01, 'b_fox_f': 2.278208e+00, 'b_gate_a': 1.699007e-01, 'b_gate_b': 1.334434e-01, 'w_branch_a': 3.952824e-01, 'w_branch_b': 3.127605e-01, 'w_out': 5.395969e-01, 'norm_mix_post': 3.221064e+01, 'norm_ffn_pre': 5.395533e-01, 'w_up': 2.297381e-01, 'conv_w': 2.547077e-01, 'conv_b': 3.955475e-01, 'w_down': 4.443762e-01, 'norm_ffn_post': 3.208424e+01}


def _to_microbatches(a, axis):
    t = _jnp.moveaxis(a, axis, 0)
    t = t.reshape((N_MICROBATCH, t.shape[0] // N_MICROBATCH) + t.shape[1:])
    return _jnp.moveaxis(t, 1, axis + 1)


def setup_inputs(seed: int = 0) -> dict:
    inp = _fwd_setup_inputs(seed)
    key = _jax.random.fold_in(_jax.random.key(seed), 7919)
    shape, _ = _output_shape()
    out = dict(inp)
    out["loss_target"] = _jax.random.normal(_jax.random.fold_in(key, 0), shape, _jnp.float32)
    for i, name in enumerate(TWIN_WEIGHTS):
        w = inp[name].astype(_jnp.float32)
        if MOMENT_SCALE is None:
            s = _jnp.sqrt(_jnp.mean(_jnp.square(w)) + 1e-30)
        else:
            s = MOMENT_SCALE[name]
        km, kv = _jax.random.split(_jax.random.fold_in(key, i + 1))
        out[name] = w
        out["m_" + name] = s * _jax.random.normal(km, w.shape, _jnp.float32)
        out["v_" + name] = (s * s) * _jax.random.uniform(kv, w.shape, _jnp.float32, 0.5, 1.5)
    if N_MICROBATCH > 1:
        for name, axis in PER_EXAMPLE_BATCH_AXIS.items():
            out[name] = _to_microbatches(out[name], axis)
    return {'x': out['x'], 'norm_mix_pre': out['norm_mix_pre'], 'w_in': out['w_in'], 'b_ml_i': out['b_ml_i'], 'b_ml_f': out['b_ml_f'], 'ml_head_norm': out['ml_head_norm'], 'b_fox_f': out['b_fox_f'], 'b_gate_a': out['b_gate_a'], 'b_gate_b': out['b_gate_b'], 'w_branch_a': out['w_branch_a'], 'w_branch_b': out['w_branch_b'], 'w_out': out['w_out'], 'norm_mix_post': out['norm_mix_post'], 'norm_ffn_pre': out['norm_ffn_pre'], 'w_up': out['w_up'], 'conv_w': out['conv_w'], 'conv_b': out['conv_b'], 'w_down': out['w_down'], 'norm_ffn_post': out['norm_ffn_post'], 'loss_target': out['loss_target'], 'm_norm_mix_pre': out['m_norm_mix_pre'], 'm_w_in': out['m_w_in'], 'm_b_ml_i': out['m_b_ml_i'], 'm_b_ml_f': out['m_b_ml_f'], 'm_ml_head_norm': out['m_ml_head_norm'], 'm_b_fox_f': out['m_b_fox_f'], 'm_b_gate_a': out['m_b_gate_a'], 'm_b_gate_b': out['m_b_gate_b'], 'm_w_branch_a': out['m_w_branch_a'], 'm_w_branch_b': out['m_w_branch_b'], 'm_w_out': out['m_w_out'], 'm_norm_mix_post': out['m_norm_mix_post'], 'm_norm_ffn_pre': out['m_norm_ffn_pre'], 'm_w_up': out['m_w_up'], 'm_conv_w': out['m_conv_w'], 'm_conv_b': out['m_conv_b'], 'm_w_down': out['m_w_down'], 'm_norm_ffn_post': out['m_norm_ffn_post'], 'v_norm_mix_pre': out['v_norm_mix_pre'], 'v_w_in': out['v_w_in'], 'v_b_ml_i': out['v_b_ml_i'], 'v_b_ml_f': out['v_b_ml_f'], 'v_ml_head_norm': out['v_ml_head_norm'], 'v_b_fox_f': out['v_b_fox_f'], 'v_b_gate_a': out['v_b_gate_a'], 'v_b_gate_b': out['v_b_gate_b'], 'v_w_branch_a': out['v_w_branch_a'], 'v_w_branch_b': out['v_w_branch_b'], 'v_w_out': out['v_w_out'], 'v_norm_mix_post': out['v_norm_mix_post'], 'v_norm_ffn_pre': out['v_norm_ffn_pre'], 'v_w_up': out['v_w_up'], 'v_conv_w': out['v_conv_w'], 'v_conv_b': out['v_conv_b'], 'v_w_down': out['v_w_down'], 'v_norm_ffn_post': out['v_norm_ffn_post']}


def _loss(weights, diff, rest, loss_target):
    with _jax.named_scope("forward"):
        args = {**rest, TWIN_DIFF_INPUT: diff, **{k: w.astype(_WEIGHT_DTYPES[k]) for k, w in weights.items()}}
        y = _forward(args)
    with _jax.named_scope("loss_head"):
        err = _jnp.square(y.astype(_jnp.float32) - loss_target)
        return 0.5 * _jnp.sum(_jnp.mean(err, axis=-1)) if err.ndim else 0.5 * err


def _adamw(w, g, m, v):
    m = ADAM_B1 * m + (1.0 - ADAM_B1) * g
    v = ADAM_B2 * v + (1.0 - ADAM_B2) * _jnp.square(g)
    m_hat = m / (1.0 - ADAM_B1 ** ADAM_STEP)
    v_hat = v / (1.0 - ADAM_B2 ** ADAM_STEP)
    delta = -ADAM_LR * (m_hat / (_jnp.sqrt(v_hat) + ADAM_EPS) + ADAM_WD * w)
    return delta, m, v


def reference(x, norm_mix_pre, w_in, b_ml_i, b_ml_f, ml_head_norm, b_fox_f, b_gate_a, b_gate_b, w_branch_a, w_branch_b, w_out, norm_mix_post, norm_ffn_pre, w_up, conv_w, conv_b, w_down, norm_ffn_post, loss_target, m_norm_mix_pre, m_w_in, m_b_ml_i, m_b_ml_f, m_ml_head_norm, m_b_fox_f, m_b_gate_a, m_b_gate_b, m_w_branch_a, m_w_branch_b, m_w_out, m_norm_mix_post, m_norm_ffn_pre, m_w_up, m_conv_w, m_conv_b, m_w_down, m_norm_ffn_post, v_norm_mix_pre, v_w_in, v_b_ml_i, v_b_ml_f, v_ml_head_norm, v_b_fox_f, v_b_gate_a, v_b_gate_b, v_w_branch_a, v_w_branch_b, v_w_out, v_norm_mix_post, v_norm_ffn_pre, v_w_up, v_conv_w, v_conv_b, v_w_down, v_norm_ffn_post):
    given = dict(x=x, norm_mix_pre=norm_mix_pre, w_in=w_in, b_ml_i=b_ml_i, b_ml_f=b_ml_f, ml_head_norm=ml_head_norm, b_fox_f=b_fox_f, b_gate_a=b_gate_a, b_gate_b=b_gate_b, w_branch_a=w_branch_a, w_branch_b=w_branch_b, w_out=w_out, norm_mix_post=norm_mix_post, norm_ffn_pre=norm_ffn_pre, w_up=w_up, conv_w=conv_w, conv_b=conv_b, w_down=w_down, norm_ffn_post=norm_ffn_post, loss_target=loss_target, m_norm_mix_pre=m_norm_mix_pre, m_w_in=m_w_in, m_b_ml_i=m_b_ml_i, m_b_ml_f=m_b_ml_f, m_ml_head_norm=m_ml_head_norm, m_b_fox_f=m_b_fox_f, m_b_gate_a=m_b_gate_a, m_b_gate_b=m_b_gate_b, m_w_branch_a=m_w_branch_a, m_w_branch_b=m_w_branch_b, m_w_out=m_w_out, m_norm_mix_post=m_norm_mix_post, m_norm_ffn_pre=m_norm_ffn_pre, m_w_up=m_w_up, m_conv_w=m_conv_w, m_conv_b=m_conv_b, m_w_down=m_w_down, m_norm_ffn_post=m_norm_ffn_post, v_norm_mix_pre=v_norm_mix_pre, v_w_in=v_w_in, v_b_ml_i=v_b_ml_i, v_b_ml_f=v_b_ml_f, v_ml_head_norm=v_ml_head_norm, v_b_fox_f=v_b_fox_f, v_b_gate_a=v_b_gate_a, v_b_gate_b=v_b_gate_b, v_w_branch_a=v_w_branch_a, v_w_branch_b=v_w_branch_b, v_w_out=v_w_out, v_norm_mix_post=v_norm_mix_post, v_norm_ffn_pre=v_norm_ffn_pre, v_w_up=v_w_up, v_conv_w=v_conv_w, v_conv_b=v_conv_b, v_w_down=v_w_down, v_norm_ffn_post=v_norm_ffn_post)
    weights = {n: given[n] for n in TWIN_WEIGHTS}
    shared = {n: given[n] for n in SHARED_INPUTS}
    per_example = {n: given[n] for n in ['x']}
    grad_fn = _jax.value_and_grad(_loss, argnums=(0, 1))

    def one_microbatch(ex, loss_target):
        ex = dict(ex)
        diff = ex.pop(TWIN_DIFF_INPUT)
        return grad_fn(weights, diff, {**shared, **ex}, loss_target)

    if N_MICROBATCH == 1:
        loss, (grad_w, grad_x) = one_microbatch(per_example, given["loss_target"])
    else:
        def body(carry, xs):
            loss_sum, grad_sum = carry
            l_k, (gw_k, gx_k) = one_microbatch(xs[0], xs[1])
            with _jax.named_scope("update"):
                return (loss_sum + l_k, _jax.tree.map(_jnp.add, grad_sum, gw_k)), gx_k

        init = (_jnp.zeros((), _jnp.float32), _jax.tree.map(_jnp.zeros_like, weights))
        (loss, grad_w), grad_x = _jax.lax.scan(body, init, (per_example, given["loss_target"]))
    with _jax.named_scope("update"):
        delta_w, new_m, new_v = {}, {}, {}
        for n in TWIN_WEIGHTS:
            delta_w[n], new_m[n], new_v[n] = _adamw(weights[n], grad_w[n], given["m_" + n], given["v_" + n])
    return (loss, grad_x, *[grad_w[n] for n in TWIN_WEIGHTS], *[delta_w[n] for n in TWIN_WEIGHTS],
            *[new_m[n] for n in TWIN_WEIGHTS], *[new_v[n] for n in TWIN_WEIGHTS])
```

```python
import functools
import math

import jax
import jax.numpy as jnp
from jax import lax
from jax.experimental import pallas as pl
from jax.experimental.pallas import tpu as pltpu

F32 = jnp.float32
BF16 = jnp.bfloat16
MESH = pl.DeviceIdType.MESH

D_MODEL = 1024
ML_HEADS = 4
ML_DQK = 128
ML_DV = 256
FOX_HEADS = 8
FOX_DH = 128
D_FF = 2816
GATE_CAP = 15.0
EPS = 1e-6
ADAM_LR, ADAM_B1, ADAM_B2, ADAM_EPS, ADAM_WD, ADAM_STEP = 0.001, 0.9, 0.999, 1e-08, 0.01, 10

LANES = 128
MLC = 128
FOX_T = 256
ROW_T = 512
VMEM_LIMIT = 56 * 1024 * 1024

C_QM, C_KM, C_VM, C_OM, C_QF, C_KF, C_VF, C_GA, C_GB = 0, 512, 1024, 2048, 3072, 4096, 5120, 6144, 7168
N_MAIN = 8192
N_SMALL = 384
R_QM, R_KM, R_VM, R_I, R_F, R_OM, R_QF, R_KF, R_VF, R_FF, R_GA, R_GB, R_END = (
    0, 512, 1024, 2048, 2052, 2056, 3080, 4104, 5128, 6152, 6160, 7184, 8208)


def _cparams(sem=None):
    return pltpu.CompilerParams(dimension_semantics=sem, vmem_limit_bytes=VMEM_LIMIT)


def _tile(n, target):
    if n <= target:
        return n
    best = None
    for t in range(LANES, target + 1, LANES):
        if n % t == 0:
            best = t
    assert best is not None, (n, target)
    return best


def _dot(a, b, dims):
    return lax.dot_general(a, b, (dims, ((), ())), preferred_element_type=F32)


def _dot_nn(a, b):
    return _dot(a, b, ((1,), (0,)))


def _dot_nt(a, b):
    return _dot(a, b, ((1,), (1,)))


def _dot_tn(a, b):
    return _dot(a, b, ((0,), (0,)))


_DOTS = {"nn": _dot_nn, "nt": _dot_nt, "tn": _dot_tn}


def _mm(a, b, mode, out_dtype, name, tm=1024, tn=1408, tk=1024):
    if mode == "nn":
        (M, K), (K2, N) = a.shape, b.shape
    elif mode == "nt":
        (M, K), (N, K2) = a.shape, b.shape
    else:
        (K, M), (K2, N) = a.shape, b.shape
    assert K == K2, (name, a.shape, b.shape)
    tm, tn, tk = _tile(M, tm), _tile(N, tn), _tile(K, tk)
    nk = K // tk
    dot = _DOTS[mode]

    def body(a_ref, b_ref, o_ref, *acc):
        part = dot(a_ref[...], b_ref[...])
        if nk == 1:
            o_ref[...] = part.astype(o_ref.dtype)
        else:
            acc_ref, = acc
            k = pl.program_id(2)

            @pl.when(k == 0)
            def _():
                acc_ref[...] = part

            @pl.when(k > 0)
            def _():
                acc_ref[...] += part

            @pl.when(k == nk - 1)
            def _():
                o_ref[...] = acc_ref[...].astype(o_ref.dtype)

    if mode == "nn":
        a_spec = pl.BlockSpec((tm, tk), lambda i, j, k: (i, k))
        b_spec = pl.BlockSpec((tk, tn), lambda i, j, k: (k, j))
    elif mode == "nt":
        a_spec = pl.BlockSpec((tm, tk), lambda i, j, k: (i, k))
        b_spec = pl.BlockSpec((tn, tk), lambda i, j, k: (j, k))
    else:
        a_spec = pl.BlockSpec((tk, tm), lambda i, j, k: (k, i))
        b_spec = pl.BlockSpec((tk, tn), lambda i, j, k: (k, j))
    return pl.pallas_call(
        body, name=name,
        out_shape=jax.ShapeDtypeStruct((M, N), out_dtype),
        grid=(M // tm, N // tn, nk),
        in_specs=[a_spec, b_spec],
        out_specs=pl.BlockSpec((tm, tn), lambda i, j, k: (i, j)),
        scratch_shapes=[pltpu.VMEM((tm, tn), F32)] if nk > 1 else [],
        compiler_params=_cparams(("parallel", "parallel", "arbitrary")),
    )(a, b)


def _rstd(x):
    return lax.rsqrt(jnp.mean(x * x, axis=-1, keepdims=True) + EPS)


def _rmsnorm_fwd(x, g, name):
    S, D = x.shape
    T = _tile(S, ROW_T)

    def body(x_ref, g_ref, o_ref):
        xv = x_ref[...]
        o_ref[...] = (xv * _rstd(xv) * g_ref[...]).astype(o_ref.dtype)

    return pl.pallas_call(
        body, name=name, out_shape=jax.ShapeDtypeStruct((S, D), BF16), grid=(S // T,),
        in_specs=[pl.BlockSpec((T, D), lambda i: (i, 0)), pl.BlockSpec((1, D), lambda i: (0, 0))],
        out_specs=pl.BlockSpec((T, D), lambda i: (i, 0)),
        compiler_params=_cparams(("parallel",)),
    )(x, g)


def _resid_norm_fwd(x, z, g, name):
    S, D = x.shape
    T = _tile(S, ROW_T)

    def body(x_ref, z_ref, g_ref, o_ref):
        zv = z_ref[...]
        o_ref[...] = x_ref[...] + zv * _rstd(zv) * g_ref[...]

    row = pl.BlockSpec((T, D), lambda i: (i, 0))
    return pl.pallas_call(
        body, name=name, out_shape=jax.ShapeDtypeStruct((S, D), F32), grid=(S // T,),
        in_specs=[row, row, pl.BlockSpec((1, D), lambda i: (0, 0))],
        out_specs=row, compiler_params=_cparams(("parallel",)),
    )(x, z, g)


def _rmsnorm_bwd_math(dy, xv, g):
    r = _rstd(xv)
    u = dy * g
    dx = r * u - xv * (r * r * r) * jnp.mean(u * xv, axis=-1, keepdims=True)
    return dx, dy * xv * r


def _rmsnorm_bwd(dys, xin, g, resid, out_dtype, name):
    S, D = xin.shape
    T = _tile(S, ROW_T)
    has_resid = resid is not None
    ndy = len(dys)

    def body(*refs):
        dy_refs, (x_ref, g_ref) = refs[:ndy], refs[ndy:ndy + 2]
        dx_ref, dg_ref = refs[-2:]
        dy = dy_refs[0][...]
        for r in dy_refs[1:]:
            dy = dy + r[...]
        dx, dgt = _rmsnorm_bwd_math(dy, x_ref[...], g_ref[...])
        if has_resid:
            dx = dx + refs[ndy + 2][...]
        dx_ref[...] = dx.astype(dx_ref.dtype)

        @pl.when(pl.program_id(0) == 0)
        def _():
            dg_ref[...] = jnp.zeros_like(dg_ref)

        dg_ref[...] += jnp.sum(dgt, axis=0, keepdims=True)

    row = pl.BlockSpec((T, D), lambda i: (i, 0))
    vec = pl.BlockSpec((1, D), lambda i: (0, 0))
    ins = list(dys) + [xin, g] + ([resid] if has_resid else [])
    return pl.pallas_call(
        body, name=name,
        out_shape=(jax.ShapeDtypeStruct((S, D), out_dtype), jax.ShapeDtypeStruct((1, D), F32)),
        grid=(S // T,), in_specs=[row] * ndy + [row, vec] + ([row] if has_resid else []),
        out_specs=(row, vec), compiler_params=_cparams(("arbitrary",)),
    )(*ins)


def _loss_head(x1, d, g, target, name):
    S, D = x1.shape
    T = _tile(S, ROW_T)

    def body(x_ref, d_ref, g_ref, t_ref, loss_ref, dy_ref, dd_ref, dg_ref):
        dv, gv = d_ref[...], g_ref[...]
        y = x_ref[...] + dv * _rstd(dv) * gv
        diff = y - t_ref[...]
        dy = diff * (1.0 / D)
        dy_ref[...] = dy
        dd, dgt = _rmsnorm_bwd_math(dy, dv, gv)
        dd_ref[...] = dd.astype(dd_ref.dtype)

        @pl.when(pl.program_id(0) == 0)
        def _():
            dg_ref[...] = jnp.zeros_like(dg_ref)
            loss_ref[...] = jnp.zeros_like(loss_ref)

        dg_ref[...] += jnp.sum(dgt, axis=0, keepdims=True)
        part = jnp.sum(jnp.sum(diff * diff, axis=1, keepdims=True), axis=0, keepdims=True)
        loss_ref[...] += (0.5 / D) * part

    row = pl.BlockSpec((T, D), lambda i: (i, 0))
    vec = pl.BlockSpec((1, D), lambda i: (0, 0))
    return pl.pallas_call(
        body, name=name,
        out_shape=(jax.ShapeDtypeStruct((1, LANES), F32), jax.ShapeDtypeStruct((S, D), F32),
                   jax.ShapeDtypeStruct((S, D), BF16), jax.ShapeDtypeStruct((1, D), F32)),
        grid=(S // T,), in_specs=[row, row, vec, row],
        out_specs=(pl.BlockSpec((1, LANES), lambda i: (0, 0)), row, row, vec),
        compiler_params=_cparams(("arbitrary",)),
    )(x1, d, g, target)


def _merge_fwd(ya, yb, pm, ba, bb, name):
    S, D = ya.shape
    T = _tile(S, ROW_T)

    def body(ya_ref, yb_ref, ga_ref, gb_ref, ba_ref, bb_ref, o_ref):
        sa = jax.nn.sigmoid(ga_ref[...] + ba_ref[...])
        sb = jax.nn.sigmoid(gb_ref[...] + bb_ref[...])
        o_ref[...] = (sa * ya_ref[...] + sb * yb_ref[...]).astype(o_ref.dtype)

    row = pl.BlockSpec((T, D), lambda i: (i, 0))
    vec = pl.BlockSpec((1, D), lambda i: (0, 0))
    return pl.pallas_call(
        body, name=name, out_shape=jax.ShapeDtypeStruct((S, D), BF16), grid=(S // T,),
        in_specs=[row, row, pl.BlockSpec((T, D), lambda i: (i, C_GA // D)),
                  pl.BlockSpec((T, D), lambda i: (i, C_GB // D)), vec, vec],
        out_specs=row, compiler_params=_cparams(("parallel",)),
    )(ya, yb, pm, pm, ba, bb)


def _merge_bwd(dmerged, ya, yb, pm, ba, bb, name):
    S, D = ya.shape
    T = _tile(S, ROW_T)

    def body(dm_ref, ya_ref, yb_ref, ga_ref, gb_ref, ba_ref, bb_ref,
             dya_ref, dyb_ref, dga_ref, dgb_ref, dba_ref, dbb_ref):
        dm = dm_ref[...]
        sa = jax.nn.sigmoid(ga_ref[...] + ba_ref[...])
        sb = jax.nn.sigmoid(gb_ref[...] + bb_ref[...])
        dya_ref[...] = (dm * sa).astype(dya_ref.dtype)
        dyb_ref[...] = (dm * sb).astype(dyb_ref.dtype)
        dga = dm * ya_ref[...] * sa * (1.0 - sa)
        dgb = dm * yb_ref[...] * sb * (1.0 - sb)
        dga_ref[...] = dga.astype(dga_ref.dtype)
        dgb_ref[...] = dgb.astype(dgb_ref.dtype)

        @pl.when(pl.program_id(0) == 0)
        def _():
            dba_ref[...] = jnp.zeros_like(dba_ref)
            dbb_ref[...] = jnp.zeros_like(dbb_ref)

        dba_ref[...] += jnp.sum(dga, axis=0, keepdims=True)
        dbb_ref[...] += jnp.sum(dgb, axis=0, keepdims=True)

    row = pl.BlockSpec((T, D), lambda i: (i, 0))
    vec = pl.BlockSpec((1, D), lambda i: (0, 0))
    act = jax.ShapeDtypeStruct((S, D), BF16)
    v1 = jax.ShapeDtypeStruct((1, D), F32)
    return pl.pallas_call(
        body, name=name, out_shape=(act, act, act, act, v1, v1), grid=(S // T,),
        in_specs=[row, row, row, pl.BlockSpec((T, D), lambda i: (i, C_GA // D)),
                  pl.BlockSpec((T, D), lambda i: (i, C_GB // D)), vec, vec],
        out_specs=(row, row, row, row, vec, vec), compiler_params=_cparams(("arbitrary",)),
    )(dmerged, ya, yb, pm, pm, ba, bb)


_GELU_C = math.sqrt(2.0 / math.pi)


def _gelu(g):
    t = jnp.tanh(_GELU_C * (g + 0.044715 * g * g * g))
    return 0.5 * g * (1.0 + t), t


def _gelu_grad(g, t):
    return 0.5 * (1.0 + t) + 0.5 * g * (1.0 - t * t) * _GELU_C * (1.0 + 3 * 0.044715 * g * g)


def _shift_down(v, halo_ref, first, rows):
    T = v.shape[0]
    keep = jnp.where(first, 0.0, 1.0)
    h7 = halo_ref[7:8, :] * keep
    h6 = halo_ref[6:7, :] * keep
    m1 = jnp.where(rows == 0, h7, pltpu.roll(v, 1, 0))
    m2 = jnp.where(rows == 0, h6, jnp.where(rows == 1, h7, pltpu.roll(v, 2, 0)))
    return m1, m2


def _conv_act_fwd(up, cw, cb, name):
    S, F2 = up.shape
    Fh = F2 // 2
    T = _tile(S, ROW_T)
    tc = _tile(Fh, 256)
    ncol = Fh // tc
    hb = T // 8

    def body(ua_ref, ug_ref, ha_ref, hg_ref, wa_ref, wg_ref, ba_ref, bg_ref, o_ref):
        first = pl.program_id(0) == 0
        rows = lax.broadcasted_iota(jnp.int32, (T, tc), 0)

        def conv(u_ref, h_ref, w_ref, b_ref):
            v = u_ref[...]
            m1, m2 = _shift_down(v, h_ref, first, rows)
            return b_ref[...] + w_ref[0:1, :] * m2 + w_ref[1:2, :] * m1 + w_ref[2:3, :] * v

        a = conv(ua_ref, ha_ref, wa_ref, ba_ref)
        g = conv(ug_ref, hg_ref, wg_ref, bg_ref)
        o_ref[...] = (_gelu(g)[0] * a).astype(o_ref.dtype)

    halo = lambda off: pl.BlockSpec((8, tc), lambda i, j: (jnp.maximum(i * hb - 1, 0), j + off))
    return pl.pallas_call(
        body, name=name, out_shape=jax.ShapeDtypeStruct((S, Fh), BF16), grid=(S // T, ncol),
        in_specs=[pl.BlockSpec((T, tc), lambda i, j: (i, j)), pl.BlockSpec((T, tc), lambda i, j: (i, j + ncol)),
                  halo(0), halo(ncol),
                  pl.BlockSpec((3, tc), lambda i, j: (0, j)), pl.BlockSpec((3, tc), lambda i, j: (0, j + ncol)),
                  pl.BlockSpec((1, tc), lambda i, j: (0, j)), pl.BlockSpec((1, tc), lambda i, j: (0, j + ncol))],
        out_specs=pl.BlockSpec((T, tc), lambda i, j: (i, j)),
        compiler_params=_cparams(("parallel", "parallel")),
    )(up, up, up, up, cw, cw, cb, cb)


def _conv_act_bwd_du(up, dact, cw, cb, name):
    S, F2 = up.shape
    Fh = F2 // 2
    T = _tile(S, ROW_T)
    tc = _tile(Fh, 256)
    ncol = Fh // tc
    hb = T // 8

    def body(ua_ref, ug_ref, ha_ref, hg_ref, wa_ref, wg_ref, ba_ref, bg_ref, da_ref,
             dua_ref, dug_ref, dwa_ref, dwg_ref, dba_ref, dbg_ref):
        first = pl.program_id(1) == 0
        rows = lax.broadcasted_iota(jnp.int32, (T, tc), 0)

        def conv(u_ref, h_ref, w_ref, b_ref):
            v = u_ref[...]
            m1, m2 = _shift_down(v, h_ref, first, rows)
            return b_ref[...] + w_ref[0:1, :] * m2 + w_ref[1:2, :] * m1 + w_ref[2:3, :] * v, (m2, m1, v)

        a, taps_a = conv(ua_ref, ha_ref, wa_ref, ba_ref)
        g, taps_g = conv(ug_ref, hg_ref, wg_ref, bg_ref)
        gel, t = _gelu(g)
        dact_v = da_ref[...]
        dua = dact_v * gel
        dug = dact_v * a * _gelu_grad(g, t)
        dua_ref[...] = dua
        dug_ref[...] = dug

        @pl.when(first)
        def _():
            for r in (dwa_ref, dwg_ref, dba_ref, dbg_ref):
                r[...] = jnp.zeros_like(r)

        for du, taps, dw_ref, db_ref in ((dua, taps_a, dwa_ref, dba_ref), (dug, taps_g, dwg_ref, dbg_ref)):
            db_ref[...] += jnp.sum(du, axis=0, keepdims=True)
            for j in range(3):
                dw_ref[j:j + 1, :] += jnp.sum(du * taps[j], axis=0, keepdims=True)

    halo = lambda off: pl.BlockSpec((8, tc), lambda j, i: (jnp.maximum(i * hb - 1, 0), j + off))
    outs = pl.pallas_call(
        body, name=name,
        out_shape=(jax.ShapeDtypeStruct((S, Fh), F32), jax.ShapeDtypeStruct((S, Fh), F32),
                   jax.ShapeDtypeStruct((3, Fh), F32), jax.ShapeDtypeStruct((3, Fh), F32),
                   jax.ShapeDtypeStruct((1, Fh), F32), jax.ShapeDtypeStruct((1, Fh), F32)),
        grid=(ncol, S // T),
        in_specs=[pl.BlockSpec((T, tc), lambda j, i: (i, j)), pl.BlockSpec((T, tc), lambda j, i: (i, j + ncol)),
                  halo(0), halo(ncol),
                  pl.BlockSpec((3, tc), lambda j, i: (0, j)), pl.BlockSpec((3, tc), lambda j, i: (0, j + ncol)),
                  pl.BlockSpec((1, tc), lambda j, i: (0, j)), pl.BlockSpec((1, tc), lambda j, i: (0, j + ncol)),
                  pl.BlockSpec((T, tc), lambda j, i: (i, j))],
        out_specs=(pl.BlockSpec((T, tc), lambda j, i: (i, j)), pl.BlockSpec((T, tc), lambda j, i: (i, j)),
                   pl.BlockSpec((3, tc), lambda j, i: (0, j)), pl.BlockSpec((3, tc), lambda j, i: (0, j)),
                   pl.BlockSpec((1, tc), lambda j, i: (0, j)), pl.BlockSpec((1, tc), lambda j, i: (0, j))),
        compiler_params=_cparams(("parallel", "arbitrary")),
    )(up, up, up, up, cw, cw, cb, cb, dact)
    return outs


def _conv_bwd_dup(du, cw, col_off, name):
    S, Fh = du.shape
    T = _tile(S, ROW_T)
    tc = _tile(Fh, 256)
    ncol, nrow, hb, nhb = Fh // tc, S // T, T // 8, S // 8

    def body(du_ref, h_ref, w_ref, o_ref):
        keep = jnp.where(pl.program_id(0) == nrow - 1, 0.0, 1.0)
        rows = lax.broadcasted_iota(jnp.int32, (T, tc), 0)
        v = du_ref[...]
        h0 = h_ref[0:1, :] * keep
        h1 = h_ref[1:2, :] * keep
        p1 = jnp.where(rows == T - 1, h0, pltpu.roll(v, T - 1, 0))
        p2 = jnp.where(rows == T - 2, h0, jnp.where(rows == T - 1, h1, pltpu.roll(v, T - 2, 0)))
        o_ref[...] = (w_ref[2:3, :] * v + w_ref[1:2, :] * p1 + w_ref[0:1, :] * p2).astype(o_ref.dtype)

    return pl.pallas_call(
        body, name=name, out_shape=jax.ShapeDtypeStruct((S, Fh), BF16), grid=(nrow, ncol),
        in_specs=[pl.BlockSpec((T, tc), lambda i, j: (i, j)),
                  pl.BlockSpec((8, tc), lambda i, j: (jnp.minimum((i + 1) * hb, nhb - 1), j)),
                  pl.BlockSpec((3, tc), lambda i, j: (0, j + col_off))],
        out_specs=pl.BlockSpec((T, tc), lambda i, j: (i, j)),
        compiler_params=_cparams(("parallel", "parallel")),
    )(du, du, cw)


def _split3(x):
    hi = x.astype(BF16)
    r1 = x - hi.astype(F32)
    mid = r1.astype(BF16)
    lo = (r1 - mid.astype(F32)).astype(BF16)
    return hi, mid, lo


def _tri_dot(tri, x):
    hi, mid, lo = _split3(x)
    return _dot_nn(tri, hi) + _dot_nn(tri, mid) + _dot_nn(tri, lo)


def _log_sigmoid(x):
    return jnp.minimum(x, 0.0) - jnp.log(1.0 + jnp.exp(-jnp.abs(x)))


def _tri_mask(n, lower):
    r = lax.broadcasted_iota(jnp.int32, (n, n), 0)
    c = lax.broadcasted_iota(jnp.int32, (n, n), 1)
    return (r >= c) if lower else (r <= c)


def _gates_fwd(ps, bi, bf, bff, name):
    S = ps.shape[0]
    NC = S // MLC

    def body(ps_ref, bi_ref, bf_ref, bff_ref, a_ref, A_ref, wi_ref, em_ref, wk_ref, dec_ref, F_ref, m_scr, f_scr):
        @pl.when(pl.program_id(0) == 0)
        def _():
            m_scr[...] = jnp.zeros_like(m_scr)
            f_scr[...] = jnp.zeros_like(f_scr)

        rows = lax.broadcasted_iota(jnp.int32, (MLC, LANES), 0)
        ltri = _tri_mask(MLC, True).astype(BF16)
        li = GATE_CAP * jnp.tanh((ps_ref[:, 0:LANES] + bi_ref[...]) / GATE_CAP)
        lf = _log_sigmoid(GATE_CAP * jnp.tanh((ps_ref[:, LANES:2 * LANES] + bf_ref[...]) / GATE_CAP))
        b = _tri_dot(ltri, lf)
        a = li - b
        cm = a
        sh = 1
        while sh < MLC:
            cm = jnp.where(rows >= sh, jnp.maximum(cm, pltpu.roll(cm, sh, 0)), cm)
            sh *= 2
        m0 = m_scr[...]
        A = jnp.maximum(cm, m0)
        a_ref[...] = a
        A_ref[...] = A
        A_last = A_ref[MLC - 1:MLC, :]
        wi_ref[...] = jnp.exp(m0 - A)
        em_ref[...] = jnp.exp(-(b + A))
        wk_ref[...] = jnp.exp(a - A_last)
        dec_ref[0] = jnp.exp(m0 - A_last)
        F_ref[...] = b
        m_scr[...] = F_ref[MLC - 1:MLC, :] + A_last
        lfg = _log_sigmoid(ps_ref[:, 2 * LANES:3 * LANES] + bff_ref[...])
        F_ref[...] = _tri_dot(ltri, lfg) + f_scr[...]
        f_scr[...] = F_ref[MLC - 1:MLC, :]

    col = pl.BlockSpec((MLC, LANES), lambda c: (c, 0))
    vec = pl.BlockSpec((1, LANES), lambda c: (0, 0))
    cs = jax.ShapeDtypeStruct((S, LANES), F32)
    return pl.pallas_call(
        body, name=name,
        out_shape=(cs, cs, cs, cs, cs, jax.ShapeDtypeStruct((NC, 1, LANES), F32), cs),
        grid=(NC,), in_specs=[pl.BlockSpec((MLC, N_SMALL), lambda c: (c, 0)), vec, vec, vec],
        out_specs=(col, col, col, col, col, pl.BlockSpec((1, 1, LANES), lambda c: (c, 0, 0)), col),
        scratch_shapes=[pltpu.VMEM((1, LANES), F32), pltpu.VMEM((1, LANES), F32)],
        compiler_params=_cparams(("arbitrary",)),
    )(ps, bi, bf, bff)


def _gates_bwd(ps, bi, bf, bff, rk, kc, tch, dF, name):
    S = ps.shape[0]
    NC = S // MLC

    def body(ps_ref, bi_ref, bf_ref, bff_ref, rk_ref, kc_ref, t_ref, dF_ref, dps_ref, db_ref, carry):
        @pl.when(pl.program_id(0) == 0)
        def _():
            carry[...] = jnp.zeros_like(carry)
            db_ref[...] = jnp.zeros_like(db_ref)

        lanes = lax.broadcasted_iota(jnp.int32, (MLC, LANES), 1)
        utri = _tri_mask(MLC, False).astype(BF16)
        ti = jnp.tanh((ps_ref[:, 0:LANES] + bi_ref[...]) / GATE_CAP)
        t_end, t_start = t_ref[0, 0:1, :], t_ref[0, 1:2, :]
        rk = rk_ref[...]
        rk = rk - (jnp.sum(rk, axis=0, keepdims=True) - (t_start - t_end)) * (1.0 / MLC)
        dpi = jnp.where(lanes < ML_HEADS, (kc_ref[...] - rk) * (1.0 - ti * ti), 0.0)
        tf = jnp.tanh((ps_ref[:, LANES:2 * LANES] + bf_ref[...]) / GATE_CAP)
        dlf = _tri_dot(utri, rk) + t_end
        dpf = jnp.where(lanes < ML_HEADS, dlf * jax.nn.sigmoid(-GATE_CAP * tf) * (1.0 - tf * tf), 0.0)
        dFv = dF_ref[...]
        dlfg = _tri_dot(utri, dFv) + carry[...]
        carry[...] += jnp.sum(dFv, axis=0, keepdims=True)
        dpff = jnp.where(lanes < FOX_HEADS, dlfg * jax.nn.sigmoid(-(ps_ref[:, 2 * LANES:3 * LANES] + bff_ref[...])), 0.0)
        for n, dp in enumerate((dpi, dpf, dpff)):
            dps_ref[:, n * LANES:(n + 1) * LANES] = dp.astype(dps_ref.dtype)
            db_ref[:, n * LANES:(n + 1) * LANES] += jnp.sum(dp, axis=0, keepdims=True)

    rev = lambda c: (NC - 1 - c, 0)
    col = pl.BlockSpec((MLC, LANES), rev)
    vec = pl.BlockSpec((1, LANES), lambda c: (0, 0))
    wide = pl.BlockSpec((MLC, N_SMALL), rev)
    return pl.pallas_call(
        body, name=name,
        out_shape=(jax.ShapeDtypeStruct((S, N_SMALL), BF16), jax.ShapeDtypeStruct((1, N_SMALL), F32)),
        grid=(NC,),
        in_specs=[wide, vec, vec, vec, col, col, pl.BlockSpec((1, 2, LANES), lambda c: (NC - 1 - c, 0, 0)), col],
        out_specs=(wide, pl.BlockSpec((1, N_SMALL), lambda c: (0, 0))),
        scratch_shapes=[pltpu.VMEM((1, LANES), F32)],
        compiler_params=_cparams(("arbitrary",)),
    )(ps, bi, bf, bff, rk, kc, tch, dF)


_ML_SCALE = ML_DQK ** -0.5


def _ml_specs(rev, NC):
    idx = (lambda c: NC - 1 - c) if rev else (lambda c: c)
    qk = lambda blk: pl.BlockSpec((MLC, ML_HEADS * ML_DQK), lambda c: (idx(c), blk))
    wide = lambda blk: pl.BlockSpec((MLC, D_MODEL), lambda c: (idx(c), blk))
    col = pl.BlockSpec((MLC, LANES), lambda c: (idx(c), 0))
    return idx, qk, wide, col


def _ml_intra(q_ref, k_ref, arow_ref, A_ref, h):
    hs = slice(h * ML_DQK, (h + 1) * ML_DQK)
    qf = q_ref[:, hs] * _ML_SCALE
    kf = k_ref[:, hs]
    qb, kb = qf.astype(BF16), kf.astype(BF16)
    qk = _dot_nt(qb, kb)
    logw = arow_ref[h:h + 1, :] - A_ref[:, h:h + 1]
    W = jnp.exp(jnp.where(_tri_mask(MLC, True), logw, -1e30))
    return qb, kb, qf, kf, qk, W


def _mlstm_fwd(pm, a_row, A, wi, em, wk, dec, w_hn, name):
    S = pm.shape[0]
    NC = S // MLC
    _, qk, wide, col = _ml_specs(False, NC)

    def body(q_ref, k_ref, v_ref, o_ref, arow_ref, A_ref, wi_ref, em_ref, wk_ref, dec_ref, whn_ref,
             ha_ref, hp_ref, den_ref, cst_ref, nst_ref, C_scr, n_scr):
        @pl.when(pl.program_id(0) == 0)
        def _():
            C_scr[...] = jnp.zeros_like(C_scr)
            n_scr[...] = jnp.zeros_like(n_scr)

        lanes = lax.broadcasted_iota(jnp.int32, (MLC, LANES), 1)
        den_tile = jnp.zeros((MLC, LANES), F32)
        for h in range(ML_HEADS):
            vs = slice(h * ML_DV, (h + 1) * ML_DV)
            qb, kb, qf, kf, qk_, W = _ml_intra(q_ref, k_ref, arow_ref, A_ref, h)
            vb = v_ref[:, vs].astype(BF16)
            Cf = C_scr[h]
            Cb = Cf.astype(BF16)
            nrow = n_scr[h]
            cst_ref[0, h] = Cb
            nst_ref[0, h] = nrow
            s = qk_ * W
            wic = wi_ref[:, h:h + 1]
            num = _dot_nn(s.astype(BF16), vb) + wic * _dot_nt(qb, Cb)
            den = jnp.sum(s, axis=1, keepdims=True) + wic * jnp.sum(qf * nrow, axis=1, keepdims=True)
            hp = num / jnp.maximum(jnp.abs(den), em_ref[:, h:h + 1])
            hp_ref[:, vs] = hp
            den_tile = jnp.where(lanes == h, den, den_tile)
            hn = hp * _rstd(hp) * whn_ref[:, vs]
            ha_ref[:, vs] = (hn * jax.nn.sigmoid(o_ref[:, vs])).astype(ha_ref.dtype)
            wkc = wk_ref[:, h:h + 1]
            kw = kf * wkc
            d = dec_ref[0, :, h:h + 1]
            C_scr[h] = d * Cf + _dot_tn(vb, kw.astype(BF16))
            n_scr[h] = d * nrow + jnp.sum(kw, axis=0, keepdims=True)
        den_ref[...] = den_tile

    return pl.pallas_call(
        body, name=name,
        out_shape=(jax.ShapeDtypeStruct((S, D_MODEL), BF16), jax.ShapeDtypeStruct((S, D_MODEL), F32),
                   jax.ShapeDtypeStruct((S, LANES), F32),
                   jax.ShapeDtypeStruct((NC, ML_HEADS, ML_DV, ML_DQK), BF16),
                   jax.ShapeDtypeStruct((NC, ML_HEADS, 1, ML_DQK), F32)),
        grid=(NC,),
        in_specs=[qk(C_QM // 512), qk(C_KM // 512), wide(C_VM // D_MODEL), wide(C_OM // D_MODEL),
                  pl.BlockSpec((8, MLC), lambda c: (0, c)), col, col, col, col,
                  pl.BlockSpec((1, 1, LANES), lambda c: (c, 0, 0)), pl.BlockSpec((1, D_MODEL), lambda c: (0, 0))],
        out_specs=(pl.BlockSpec((MLC, D_MODEL), lambda c: (c, 0)), pl.BlockSpec((MLC, D_MODEL), lambda c: (c, 0)),
                   col, pl.BlockSpec((1, ML_HEADS, ML_DV, ML_DQK), lambda c: (c, 0, 0, 0)),
                   pl.BlockSpec((1, ML_HEADS, 1, ML_DQK), lambda c: (c, 0, 0, 0))),
        scratch_shapes=[pltpu.VMEM((ML_HEADS, ML_DV, ML_DQK), F32), pltpu.VMEM((ML_HEADS, 1, ML_DQK), F32)],
        compiler_params=_cparams(("arbitrary",)),
    )(pm, pm, pm, pm, a_row, A, wi, em, wk, dec, w_hn)


def _mlstm_bwd(dha, pm, hp_all, den_all, a_row, A, wi, em, wk, dec, cst, nst, w_hn, name):
    S = pm.shape[0]
    NC = S // MLC
    idx, qk, wide, col = _ml_specs(True, NC)

    def body(dha_ref, q_ref, k_ref, v_ref, o_ref, hp_ref, den_ref, arow_ref, A_ref, wi_ref, em_ref, wk_ref,
             dec_ref, cst_ref, nst_ref, whn_ref,
             dq_ref, dk_ref, dv_ref, do_ref, rk_ref, kc_ref, t_ref, dwhn_ref, dC_scr, dn_scr, t_scr):
        @pl.when(pl.program_id(0) == 0)
        def _():
            dC_scr[...] = jnp.zeros_like(dC_scr)
            dn_scr[...] = jnp.zeros_like(dn_scr)
            t_scr[...] = jnp.zeros_like(t_scr)
            dwhn_ref[...] = jnp.zeros_like(dwhn_ref)

        lanes = lax.broadcasted_iota(jnp.int32, (MLC, LANES), 1)
        lane1 = lax.broadcasted_iota(jnp.int32, (1, LANES), 1)
        t_ref[0, 0:1, :] = t_scr[...]
        rk_tile = jnp.zeros((MLC, LANES), F32)
        kc_tile = jnp.zeros((MLC, LANES), F32)
        t_new = jnp.zeros((1, LANES), F32)
        for h in range(ML_HEADS):
            hs = slice(h * ML_DQK, (h + 1) * ML_DQK)
            vs = slice(h * ML_DV, (h + 1) * ML_DV)
            hp = hp_ref[:, vs]
            sig = jax.nn.sigmoid(o_ref[:, vs])
            whn = whn_ref[:, vs]
            r = _rstd(hp)
            dga = dha_ref[:, vs]
            do_ref[:, vs] = (dga * (hp * r * whn) * sig * (1.0 - sig)).astype(do_ref.dtype)
            dhn = dga * sig
            dhp, dwt = _rmsnorm_bwd_math(dhn, hp, whn)
            dwhn_ref[:, vs] += jnp.sum(dwt, axis=0, keepdims=True)
            den = den_ref[:, h:h + 1]
            floor = em_ref[:, h:h + 1]
            D = jnp.maximum(jnp.abs(den), floor)
            dnum = dhp / D
            dh_h = jnp.sum(dhp * hp, axis=1, keepdims=True)
            active = jnp.abs(den) >= floor
            dden = -dh_h / D * jnp.where(active, jnp.sign(den), 0.0)
            phi = jnp.where(active, 0.0, dh_h)
            qb, kb, qf, kf, qk_, W = _ml_intra(q_ref, k_ref, arow_ref, A_ref, h)
            vf = v_ref[:, vs]
            vb = vf.astype(BF16)
            Cb = cst_ref[0, h]
            nrow = nst_ref[0, h]
            wic = wi_ref[:, h:h + 1]
            wkc = wk_ref[:, h:h + 1]
            d = dec_ref[0, :, h:h + 1]
            dCn = dC_scr[h]
            dCb = dCn.astype(BF16)
            dnn = dn_scr[h]
            dnumb = dnum.astype(BF16)
            s = qk_ * W
            ds = (_dot_nt(dnumb, vb) + dden) * W
            dsb = ds.astype(BF16)
            dnw = (wic * dnum).astype(BF16)
            wd = wic * dden
            kw = kf * wkc
            dv_state = _dot_nt(kw.astype(BF16), dCb)
            dq = _dot_nn(dsb, kb) + _dot_nn(dnw, Cb) + wd * nrow
            dk_state = wkc * (_dot_nn(vb, dCb) + dnn)
            dk = _dot_tn(dsb, qb) + dk_state
            dv = _dot_tn(s.astype(BF16), dnumb) + dv_state
            dC = d * dCn + _dot_tn(dnw, qb)
            dn = d * dnn + jnp.sum(wd * qf, axis=0, keepdims=True)
            dC_scr[h] = dC
            dn_scr[h] = dn
            dq_ref[:, hs] = (dq * _ML_SCALE).astype(dq_ref.dtype)
            dk_ref[:, hs] = dk.astype(dk_ref.dtype)
            dv_ref[:, vs] = dv.astype(dv_ref.dtype)
            G = ds * qk_
            inter = _dot_nt(qb, Cb)
            qn = jnp.sum(qf * nrow, axis=1, keepdims=True)
            R = (jnp.sum(G, axis=1, keepdims=True)
                 + wic * (jnp.sum(dnum * inter, axis=1, keepdims=True) + dden * qn))
            K = jnp.sum(G.T, axis=1, keepdims=True) + jnp.sum(kf * dk_state, axis=1, keepdims=True)
            rk_tile = jnp.where(lanes == h, R - K, rk_tile)
            kc_tile = jnp.where(lanes == h, phi, kc_tile)
            tt = (jnp.sum(jnp.sum(dC * Cb.astype(F32), axis=1, keepdims=True), axis=0, keepdims=True)
                  + jnp.sum(dn * nrow, axis=1, keepdims=True))
            t_new = jnp.where(lane1 == h, tt, t_new)
        rk_ref[...] = rk_tile
        kc_ref[...] = kc_tile
        t_ref[0, 1:2, :] = t_new
        t_scr[...] = t_new

    act = lambda n: jax.ShapeDtypeStruct((S, n), BF16)
    cs = jax.ShapeDtypeStruct((S, LANES), F32)
    rowblk = lambda n: pl.BlockSpec((MLC, n), lambda c: (idx(c), 0))
    return pl.pallas_call(
        body, name=name,
        out_shape=(act(512), act(512), act(D_MODEL), act(D_MODEL), cs, cs,
                   jax.ShapeDtypeStruct((NC, 2, LANES), F32), jax.ShapeDtypeStruct((1, D_MODEL), F32)),
        grid=(NC,),
        in_specs=[rowblk(D_MODEL), qk(C_QM // 512), qk(C_KM // 512), wide(C_VM // D_MODEL), wide(C_OM // D_MODEL),
                  rowblk(D_MODEL), col, pl.BlockSpec((8, MLC), lambda c: (0, idx(c))), col, col, col, col,
                  pl.BlockSpec((1, 1, LANES), lambda c: (idx(c), 0, 0)),
                  pl.BlockSpec((1, ML_HEADS, ML_DV, ML_DQK), lambda c: (idx(c), 0, 0, 0)),
                  pl.BlockSpec((1, ML_HEADS, 1, ML_DQK), lambda c: (idx(c), 0, 0, 0)),
                  pl.BlockSpec((1, D_MODEL), lambda c: (0, 0))],
        out_specs=(rowblk(512), rowblk(512), rowblk(D_MODEL), rowblk(D_MODEL), col, col,
                   pl.BlockSpec((1, 2, LANES), lambda c: (idx(c), 0, 0)), pl.BlockSpec((1, D_MODEL), lambda c: (0, 0))),
        scratch_shapes=[pltpu.VMEM((ML_HEADS, ML_DV, ML_DQK), F32), pltpu.VMEM((ML_HEADS, 1, ML_DQK), F32),
                        pltpu.VMEM((1, LANES), F32)],
        compiler_params=_cparams(("arbitrary",)),
    )(dha, pm, pm, pm, pm, hp_all, den_all, a_row, A, wi, em, wk, dec, cst, nst, w_hn)


_FOX_SCALE = FOX_DH ** -0.5
_NEG = -1e30


def _fox_fwd(pm, fcol, frow, name):
    S = pm.shape[0]
    T = FOX_T
    nb = S // T

    def body(q_ref, k_ref, v_ref, fc_ref, fr_ref, o_ref, lse_ref):
        i = pl.program_id(1)
        qb = q_ref[...].astype(BF16)
        fq = fc_ref[0]

        def step(j, carry, masked):
            m, l, acc = carry
            off = pl.multiple_of(j * T, T)
            kb = k_ref[pl.ds(off, T), :].astype(BF16)
            vb = v_ref[pl.ds(off, T), :].astype(BF16)
            s = _dot_nt(qb, kb) * _FOX_SCALE + fq - fr_ref[0, j]
            if masked:
                s = jnp.where(_tri_mask(T, True), s, _NEG)
            m_new = jnp.maximum(m, jnp.max(s, axis=1, keepdims=True))
            alpha = jnp.exp(m - m_new)
            p = jnp.exp(s - m_new)
            l = alpha * l + jnp.sum(p, axis=1, keepdims=True)
            acc = alpha * acc + _dot_nn(p.astype(BF16), vb)
            return m_new, l, acc

        init = (jnp.full((T, 1), _NEG, F32), jnp.zeros((T, 1), F32), jnp.zeros((T, FOX_DH), F32))
        carry = lax.fori_loop(0, i, lambda j, c: step(j, c, False), init)
        m, l, acc = step(i, carry, True)
        o_ref[...] = (acc / l).astype(o_ref.dtype)
        lse_ref[0] = m + jnp.log(l)

    head = lambda base: pl.BlockSpec((S, FOX_DH), lambda h, i: (0, base // FOX_DH + h))
    return pl.pallas_call(
        body, name=name,
        out_shape=(jax.ShapeDtypeStruct((S, D_MODEL), BF16), jax.ShapeDtypeStruct((FOX_HEADS, S, 1), F32)),
        grid=(FOX_HEADS, nb),
        in_specs=[pl.BlockSpec((T, FOX_DH), lambda h, i: (i, C_QF // FOX_DH + h)), head(C_KF), head(C_VF),
                  pl.BlockSpec((1, T, 1), lambda h, i: (h, i, 0)),
                  pl.BlockSpec((1, nb, 1, T), lambda h, i: (h, 0, 0, 0))],
        out_specs=(pl.BlockSpec((T, FOX_DH), lambda h, i: (i, h)), pl.BlockSpec((1, T, 1), lambda h, i: (h, i, 0))),
        compiler_params=_cparams(("parallel", "arbitrary")),
    )(pm, pm, pm, fcol, frow)


def _fox_bwd(dhb, hb, pm, lse, fcol, frow, name):
    S = pm.shape[0]
    T = FOX_T
    nb = S // T

    def body(q_ref, k_ref, v_ref, do_ref, o_ref, lse_ref, fc_ref, fr_ref,
             dq_ref, dk_ref, dv_ref, dF_ref, dFq_ref, dq_acc, delta, dk_acc, dv_acc, cs_acc):
        j = pl.program_id(1)

        @pl.when(j == 0)
        def _():
            dq_acc[...] = jnp.zeros_like(dq_acc)
            dFq_ref[...] = jnp.zeros_like(dFq_ref)

            def fill(b, _):
                off = pl.multiple_of(b * T, T)
                delta[pl.ds(off, T), :] = jnp.sum(do_ref[pl.ds(off, T), :] * o_ref[pl.ds(off, T), :].astype(F32),
                                                  axis=1, keepdims=True)
                return 0

            lax.fori_loop(0, nb, fill, 0)

        kb = k_ref[...].astype(BF16)
        vb = v_ref[...].astype(BF16)
        fk = fr_ref[0, 0]
        dk_acc[...] = jnp.zeros_like(dk_acc)
        dv_acc[...] = jnp.zeros_like(dv_acc)
        cs_acc[...] = jnp.zeros_like(cs_acc)

        def step(i, masked):
            off = pl.multiple_of(i * T, T)
            qb = q_ref[pl.ds(off, T), :].astype(BF16)
            dob = do_ref[pl.ds(off, T), :].astype(BF16)
            s = _dot_nt(qb, kb) * _FOX_SCALE + fc_ref[0, pl.ds(off, T), :] - fk
            if masked:
                s = jnp.where(_tri_mask(T, True), s, _NEG)
            p = jnp.exp(s - lse_ref[0, pl.ds(off, T), :])
            dv_acc[...] += _dot_tn(p.astype(BF16), dob)
            ds = p * (_dot_nt(dob, vb) - delta[pl.ds(off, T), :])
            dsb = ds.astype(BF16)
            dk_acc[...] += _dot_tn(dsb, qb) * _FOX_SCALE
            dq_acc[pl.ds(off, T), :] += _dot_nn(dsb, kb) * _FOX_SCALE
            cs_acc[...] += jnp.sum(ds, axis=0, keepdims=True)
            dFq_ref[0, pl.ds(off, T), :] += jnp.sum(ds, axis=1, keepdims=True)

        step(j, True)

        def rest(i, _):
            step(i, False)
            return 0

        lax.fori_loop(j + 1, nb, rest, 0)
        dk_ref[...] = dk_acc[...].astype(dk_ref.dtype)
        dv_ref[...] = dv_acc[...].astype(dv_ref.dtype)
        dF_ref[0, 0] = -cs_acc[...]

        @pl.when(j == nb - 1)
        def _():
            dq_ref[...] = dq_acc[...].astype(dq_ref.dtype)

    head = lambda base: pl.BlockSpec((S, FOX_DH), lambda h, j: (0, base // FOX_DH + h))
    blk = lambda base: pl.BlockSpec((T, FOX_DH), lambda h, j: (j, base // FOX_DH + h))
    whole = pl.BlockSpec((1, S, 1), lambda h, j: (h, 0, 0))
    act = jax.ShapeDtypeStruct((S, D_MODEL), BF16)
    return pl.pallas_call(
        body, name=name,
        out_shape=(act, act, act, jax.ShapeDtypeStruct((FOX_HEADS, nb, 1, T), F32),
                   jax.ShapeDtypeStruct((FOX_HEADS, S, 1), F32)),
        grid=(FOX_HEADS, nb),
        in_specs=[head(C_QF), blk(C_KF), blk(C_VF), head(0), head(0), whole, whole,
                  pl.BlockSpec((1, 1, 1, T), lambda h, j: (h, j, 0, 0))],
        out_specs=(head(0), blk(0), blk(0), pl.BlockSpec((1, 1, 1, T), lambda h, j: (h, j, 0, 0)), whole),
        scratch_shapes=[pltpu.VMEM((S, FOX_DH), F32), pltpu.VMEM((S, 1), F32), pltpu.VMEM((T, FOX_DH), F32),
                        pltpu.VMEM((T, FOX_DH), F32), pltpu.VMEM((1, T), F32)],
        compiler_params=_cparams(("parallel", "arbitrary")),
    )(pm, pm, pm, dhb, hb, lse, fcol, frow)


def _pad_lanes(v):
    return jnp.pad(v, ((0, 0), (0, LANES - v.shape[1])))


def _local_step(x, target, wmain, wsmall, wa, wb, wout, wup, wdown, p):
    S = x.shape[0]
    nb = S // FOX_T
    bi, bf, bff = _pad_lanes(p["b_ml_i"]), _pad_lanes(p["b_ml_f"]), _pad_lanes(p["b_fox_f"])

    h0 = _rmsnorm_fwd(x, p["norm_mix_pre"], "norm_mix_pre")
    pm = _mm(h0, wmain, "nn", F32, "proj_main")
    ps = _mm(h0, wsmall, "nn", F32, "proj_gates")
    a, A, wi, em, wk, dec, Fc = _gates_fwd(ps, bi, bf, bff, "gates_fwd")
    a_row = a[:, :8].T
    ha, hp, den, cst, nst = _mlstm_fwd(pm, a_row, A, wi, em, wk, dec, p["ml_head_norm"], "mlstm_fwd")
    ft = Fc[:, :FOX_HEADS].T
    fcol = ft.reshape(FOX_HEADS, S, 1)
    frow = ft.reshape(FOX_HEADS, nb, 1, FOX_T)
    hb, lse = _fox_fwd(pm, fcol, frow, "fox_fwd")
    ya = _mm(ha, wa, "nn", F32, "branch_a")
    yb = _mm(hb, wb, "nn", F32, "branch_b")
    merged = _merge_fwd(ya, yb, pm, p["b_gate_a"], p["b_gate_b"], "merge_fwd")
    z = _mm(merged, wout, "nn", F32, "out_proj")
    x1 = _resid_norm_fwd(x, z, p["norm_mix_post"], "resid_mix")
    h2 = _rmsnorm_fwd(x1, p["norm_ffn_pre"], "norm_ffn_pre")
    up = _mm(h2, wup, "nn", F32, "ffn_up")
    act = _conv_act_fwd(up, p["conv_w"], p["conv_b"], "conv_act_fwd")
    d = _mm(act, wdown, "nn", F32, "ffn_down")
    loss_row, dy, dd, g_norm_ffn_post = _loss_head(x1, d, p["norm_ffn_post"], target, "loss_head")
    dact = _mm(dd, wdown, "nt", F32, "d_act")
    g_wdown = _mm(act, dd, "tn", F32, "dw_down")
    dua, dug, dcwa, dcwg, dcba, dcbg = _conv_act_bwd_du(up, dact, p["conv_w"], p["conv_b"], "conv_act_bwd")
    g_conv_w = jnp.concatenate([dcwa, dcwg], axis=1)
    g_conv_b = jnp.concatenate([dcba, dcbg], axis=1)
    dup = jnp.concatenate([_conv_bwd_dup(dua, p["conv_w"], 0, "conv_bwd_a"),
                           _conv_bwd_dup(dug, p["conv_w"], (D_FF // _tile(D_FF, 256)), "conv_bwd_g")], axis=1)
    dh2 = _mm(dup, wup, "nt", F32, "d_h2")
    g_wup = _mm(h2, dup, "tn", F32, "dw_up")
    dx1, g_norm_ffn_pre = _rmsnorm_bwd([dh2], x1, p["norm_ffn_pre"], dy, F32, "norm_ffn_pre_bwd")
    dz, g_norm_mix_post = _rmsnorm_bwd([dx1], z, p["norm_mix_post"], None, BF16, "norm_mix_post_bwd")
    dmerged = _mm(dz, wout, "nt", F32, "d_merged")
    g_wout = _mm(merged, dz, "tn", F32, "dw_out")
    dya, dyb, dga, dgb, g_b_gate_a, g_b_gate_b = _merge_bwd(dmerged, ya, yb, pm, p["b_gate_a"], p["b_gate_b"], "merge_bwd")
    dha = _mm(dya, wa, "nt", F32, "d_ha")
    g_wa = _mm(ha, dya, "tn", F32, "dw_a")
    dhb = _mm(dyb, wb, "nt", F32, "d_hb")
    g_wb = _mm(hb, dyb, "tn", F32, "dw_b")
    dqm, dkm, dvm, dom, rk, kc, tch, g_ml_head_norm = _mlstm_bwd(
        dha, pm, hp, den, a_row, A, wi, em, wk, dec, cst, nst, p["ml_head_norm"], "mlstm_bwd")
    dqf, dkf, dvf, dFk, dFq = _fox_bwd(dhb, hb, pm, lse, fcol, frow, "fox_bwd")
    dF = jnp.pad((dFk.reshape(FOX_HEADS, S) + dFq.reshape(FOX_HEADS, S)).T, ((0, 0), (0, LANES - FOX_HEADS)))
    dps, dbias = _gates_bwd(ps, bi, bf, bff, rk, kc, tch, dF, "gates_bwd")
    dpm = jnp.concatenate([dqm, dkm, dvm, dom, dqf, dkf, dvf, dga, dgb], axis=1)
    dh0 = _mm(dpm, wmain, "nt", F32, "d_h0_main")
    dh0s = _mm(dps, wsmall, "nt", F32, "d_h0_gates")
    g_wmain = _mm(h0, dpm, "tn", F32, "dw_main")
    g_wsmall = _mm(h0, dps, "tn", F32, "dw_gates")
    grad_x, g_norm_mix_pre = _rmsnorm_bwd([dh0, dh0s], x, p["norm_mix_pre"], dx1, F32, "norm_mix_pre_bwd")

    big = dict(wmain=g_wmain, wsmall=g_wsmall, w_branch_a=g_wa, w_branch_b=g_wb, w_out=g_wout, w_up=g_wup, w_down=g_wdown)
    small = dict(norm_mix_pre=g_norm_mix_pre, ml_head_norm=g_ml_head_norm, b_gate_a=g_b_gate_a, b_gate_b=g_b_gate_b,
                 norm_mix_post=g_norm_mix_post, norm_ffn_pre=g_norm_ffn_pre, norm_ffn_post=g_norm_ffn_post,
                 conv_b=g_conv_b, b_ml_i=dbias[:, 0:ML_HEADS], b_ml_f=dbias[:, LANES:LANES + ML_HEADS],
                 b_fox_f=dbias[:, 2 * LANES:2 * LANES + FOX_HEADS], conv_w=g_conv_w)
    return loss_row, grad_x, big, small


def _split_w_in(w):
    seg = lambda lo, hi: w[:, lo:hi]
    main = jnp.concatenate([seg(R_QM, R_KM), seg(R_KM, R_VM), seg(R_VM, R_I), seg(R_OM, R_QF), seg(R_QF, R_KF),
                            seg(R_KF, R_VF), seg(R_VF, R_FF), seg(R_GA, R_GB), seg(R_GB, R_END)], axis=1)
    padto = lambda v: jnp.pad(v, ((0, 0), (0, LANES - v.shape[1])))
    small = jnp.concatenate([padto(seg(R_I, R_F)), padto(seg(R_F, R_OM)), padto(seg(R_FF, R_GA))], axis=1)
    return main, small


def _join_w_in(main, small):
    m = lambda lo, hi: main[:, lo:hi]
    return jnp.concatenate([
        m(C_QM, C_KM), m(C_KM, C_VM), m(C_VM, C_OM), small[:, 0:ML_HEADS], small[:, LANES:LANES + ML_HEADS],
        m(C_OM, C_QF), m(C_QF, C_KF), m(C_KF, C_VF), m(C_VF, C_GA), small[:, 2 * LANES:2 * LANES + FOX_HEADS],
        m(C_GA, C_GB), m(C_GB, N_MAIN)], axis=1)


def _row_tile(r, target=256):
    best = None
    for t in range(8, min(r, target) + 1, 8):
        if r % t == 0:
            best = t
    return best if best is not None else r


def _adamw(w, g, m, v, name):
    R, C = w.shape
    tr = _row_tile(R)

    def body(w_ref, g_ref, m_ref, v_ref, d_ref, mo_ref, vo_ref):
        gv = g_ref[...]
        mn = ADAM_B1 * m_ref[...] + (1.0 - ADAM_B1) * gv
        vn = ADAM_B2 * v_ref[...] + (1.0 - ADAM_B2) * (gv * gv)
        m_hat = mn / (1.0 - ADAM_B1 ** ADAM_STEP)
        v_hat = vn / (1.0 - ADAM_B2 ** ADAM_STEP)
        d_ref[...] = -ADAM_LR * (m_hat / (jnp.sqrt(v_hat) + ADAM_EPS) + ADAM_WD * w_ref[...])
        mo_ref[...] = mn
        vo_ref[...] = vn

    blk = pl.BlockSpec((tr, C), lambda i: (i, 0))
    o = jax.ShapeDtypeStruct((R, C), F32)
    return pl.pallas_call(
        body, name=name, out_shape=(o, o, o), grid=(R // tr,), in_specs=[blk] * 4, out_specs=(blk,) * 3,
        compiler_params=_cparams(("parallel",)),
    )(w, g, m, v)


ANY = pl.BlockSpec(memory_space=pl.ANY)


def _place():
    x, y, c = lax.axis_index("x"), lax.axis_index("y"), lax.axis_index("c")
    chips = [(1 - x, y), (x, 1 - y), (1 - x, 1 - y)]
    return x, y, c, chips


def _gather_weights(shards, convw):
    n = len(shards)

    def body(*refs):
        ins, cw_in = refs[:n], refs[n]
        outs, cw_out = refs[n + 1:2 * n + 1], refs[2 * n + 1]
        send_sems, recv_sems, cw_send, cw_recv, local_sems = refs[2 * n + 2:]
        x, y, c, chips = _place()
        sibling = (x, y, 1 - c)
        kme = 2 * x + y

        def half(a, k, hc):
            h = ins[a].shape[0] // 2
            return outs[a].at[k, pl.ds(hc * h, h), :]

        def remote(a, slot, src, dst, to):
            return pltpu.make_async_remote_copy(src_ref=src, dst_ref=dst, send_sem=send_sems.at[a * 6 + slot],
                                                recv_sem=recv_sems.at[a * 6 + slot], device_id=to, device_id_type=MESH)

        def cw_copy(j, k, to):
            return pltpu.make_async_remote_copy(src_ref=cw_in, dst_ref=cw_out.at[k], send_sem=cw_send.at[j],
                                                recv_sem=cw_recv.at[j], device_id=to, device_id_type=MESH)

        local = [pltpu.make_async_copy(ins[a], outs[a].at[kme], local_sems.at[a]) for a in range(n)]
        local.append(pltpu.make_async_copy(cw_in, cw_out.at[kme], local_sems.at[n]))
        for cp in local:
            cp.start()
        sends = []
        for a in range(n):
            h = ins[a].shape[0] // 2
            for j, chip in enumerate(chips):
                sends.append(remote(a, j, ins[a].at[pl.ds(c * h, h), :], half(a, kme, c), (*chip, c)))
        for j, chip in enumerate(chips):
            sends.append(cw_copy(j, kme, (*chip, c)))
        for cp in sends:
            cp.start()
        for a in range(n):
            for j, chip in enumerate(chips):
                kj = 2 * chip[0] + chip[1]
                remote(a, j, half(a, kj, c), half(a, kj, c), (*chip, c)).wait_recv()
                fwd = remote(a, 3 + j, half(a, kj, c), half(a, kj, c), sibling)
                fwd.start()
                sends.append(fwd)
        for a in range(n):
            for j, chip in enumerate(chips):
                kj = 2 * chip[0] + chip[1]
                remote(a, 3 + j, half(a, kj, 1 - c), half(a, kj, 1 - c), sibling).wait_recv()
        for j, chip in enumerate(chips):
            cw_copy(j, 2 * chip[0] + chip[1], (*chip, c)).wait_recv()
        for cp in sends:
            cp.wait_send()
        for cp in local:
            cp.wait()

    outs = pl.pallas_call(
        body, name="gather_weights",
        out_shape=tuple(jax.ShapeDtypeStruct((4,) + s.shape, s.dtype) for s in list(shards) + [convw]),
        in_specs=[ANY] * (n + 1), out_specs=tuple([ANY] * (n + 1)),
        scratch_shapes=[pltpu.SemaphoreType.DMA((6 * n,)), pltpu.SemaphoreType.DMA((6 * n,)),
                        pltpu.SemaphoreType.DMA((3,)), pltpu.SemaphoreType.DMA((3,)),
                        pltpu.SemaphoreType.DMA((n + 1,))],
    )(*shards, convw)
    return outs[:n], outs[n]


def _exchange_sibling_halves(gs):
    n = len(gs)

    def body(*refs):
        ins, outs, send_sems, recv_sems = refs[:n], refs[n:2 * n], refs[2 * n], refs[2 * n + 1]
        x, y, c, _ = _place()
        cps = []
        for a in range(n):
            h = ins[a].shape[1] // 2
            cps.append(pltpu.make_async_remote_copy(
                src_ref=ins[a].at[:, pl.ds((1 - c) * h, h), :], dst_ref=outs[a], send_sem=send_sems.at[a],
                recv_sem=recv_sems.at[a], device_id=(x, y, 1 - c), device_id_type=MESH))
        for cp in cps:
            cp.start()
        for cp in cps:
            cp.wait()

    return pl.pallas_call(
        body, name="grads_to_sibling",
        out_shape=tuple(jax.ShapeDtypeStruct((4, g.shape[1] // 2, g.shape[2]), g.dtype) for g in gs),
        in_specs=[ANY] * n, out_specs=tuple([ANY] * n),
        scratch_shapes=[pltpu.SemaphoreType.DMA((n,)), pltpu.SemaphoreType.DMA((n,))],
    )(*gs)


def _add_halves(g, r1, cvec, name):
    _, R, C = g.shape
    h = R // 2
    tr = _row_tile(h)
    nt = h // tr

    def body(c_ref, g_ref, r_ref, o_ref):
        o_ref[...] = (g_ref[...] + r_ref[...]).astype(o_ref.dtype)

    return pl.pallas_call(
        body, name=name, out_shape=jax.ShapeDtypeStruct((4, h, C), BF16),
        grid_spec=pltpu.PrefetchScalarGridSpec(
            num_scalar_prefetch=1, grid=(4, nt),
            in_specs=[pl.BlockSpec((1, tr, C), lambda k, i, c_ref: (k, c_ref[0] * nt + i, 0)),
                      pl.BlockSpec((1, tr, C), lambda k, i, c_ref: (k, i, 0))],
            out_specs=pl.BlockSpec((1, tr, C), lambda k, i, c_ref: (k, i, 0))),
        compiler_params=_cparams(("parallel", "parallel")),
    )(cvec, g, r1)


def _exchange_chips(ss):
    n = len(ss)

    def body(*refs):
        ins, outs, send_sems, recv_sems = refs[:n], refs[n:2 * n], refs[2 * n], refs[2 * n + 1]
        x, y, c, chips = _place()
        cps = []
        for a in range(n):
            for j, chip in enumerate(chips):
                cps.append(pltpu.make_async_remote_copy(
                    src_ref=ins[a].at[2 * chip[0] + chip[1]], dst_ref=outs[a].at[j], send_sem=send_sems.at[3 * a + j],
                    recv_sem=recv_sems.at[3 * a + j], device_id=(*chip, c), device_id_type=MESH))
        for cp in cps:
            cp.start()
        for cp in cps:
            cp.wait()

    return pl.pallas_call(
        body, name="grads_to_chips",
        out_shape=tuple(jax.ShapeDtypeStruct((3,) + s.shape[1:], s.dtype) for s in ss),
        in_specs=[ANY] * n, out_specs=tuple([ANY] * n),
        scratch_shapes=[pltpu.SemaphoreType.DMA((3 * n,)), pltpu.SemaphoreType.DMA((3 * n,))],
    )(*ss)


def _add_chips(s1, r2, kvec, name):
    _, h, C = s1.shape
    tr = _row_tile(h)

    def body(k_ref, s_ref, r0_ref, r1_ref, r2_ref, o_ref):
        o_ref[...] = ((s_ref[0].astype(F32) + r0_ref[0].astype(F32)) + r1_ref[0].astype(F32)) + r2_ref[0].astype(F32)

    peer = lambda j: pl.BlockSpec((1, tr, C), lambda i, k_ref: (j, i, 0))
    return pl.pallas_call(
        body, name=name, out_shape=jax.ShapeDtypeStruct((h, C), F32),
        grid_spec=pltpu.PrefetchScalarGridSpec(
            num_scalar_prefetch=1, grid=(h // tr,),
            in_specs=[pl.BlockSpec((1, tr, C), lambda i, k_ref: (k_ref[0], i, 0)), peer(0), peer(1), peer(2)],
            out_specs=pl.BlockSpec((tr, C), lambda i, k_ref: (i, 0))),
        compiler_params=_cparams(("parallel",)),
    )(kvec, s1, r2, r2, r2)


def _join_sibling_halves(ss):
    n = len(ss)

    def body(*refs):
        ins, outs, send_sems, recv_sems, local_sems = refs[:n], refs[n:2 * n], refs[2 * n], refs[2 * n + 1], refs[2 * n + 2]
        x, y, c, _ = _place()
        cps, loc = [], []
        for a in range(n):
            h = ins[a].shape[0]
            mine = outs[a].at[pl.ds(c * h, h), :]
            loc.append(pltpu.make_async_copy(ins[a], mine, local_sems.at[a]))
            cps.append(pltpu.make_async_remote_copy(src_ref=ins[a], dst_ref=mine, send_sem=send_sems.at[a],
                                                    recv_sem=recv_sems.at[a], device_id=(x, y, 1 - c), device_id_type=MESH))
        for cp in loc + cps:
            cp.start()
        for a in range(n):
            h = ins[a].shape[0]
            theirs = outs[a].at[pl.ds((1 - c) * h, h), :]
            pltpu.make_async_remote_copy(src_ref=ins[a], dst_ref=theirs, send_sem=send_sems.at[a],
                                         recv_sem=recv_sems.at[a], device_id=(x, y, 1 - c), device_id_type=MESH).wait_recv()
        for cp in cps:
            cp.wait_send()
        for cp in loc:
            cp.wait()

    return pl.pallas_call(
        body, name="grads_join",
        out_shape=tuple(jax.ShapeDtypeStruct((2 * s.shape[0], s.shape[1]), s.dtype) for s in ss),
        in_specs=[ANY] * n, out_specs=tuple([ANY] * n),
        scratch_shapes=[pltpu.SemaphoreType.DMA((n,)), pltpu.SemaphoreType.DMA((n,)), pltpu.SemaphoreType.DMA((n,))],
    )(*ss)


N_DEV = 8


def _allreduce_small(pack):
    P = pack.shape[0]

    def body(p_ref, o_ref, gath, send_sems, recv_sems):
        x, y, c, _ = _place()
        me = 4 * x + 2 * y + c
        cps = []
        for mask in range(1, N_DEV):
            px = 1 - x if mask & 4 else x
            py = 1 - y if mask & 2 else y
            pc = 1 - c if mask & 1 else c
            cps.append((pltpu.make_async_remote_copy(
                src_ref=p_ref, dst_ref=gath.at[me], send_sem=send_sems.at[mask - 1], recv_sem=recv_sems.at[mask - 1],
                device_id=(px, py, pc), device_id_type=MESH), 4 * px + 2 * py + pc, mask))
        for cp, _, _ in cps:
            cp.start()
        gath[me] = p_ref[...]
        for _, peer, mask in cps:
            pltpu.make_async_remote_copy(
                src_ref=p_ref, dst_ref=gath.at[peer], send_sem=send_sems.at[mask - 1], recv_sem=recv_sems.at[mask - 1],
                device_id=(x, y, c), device_id_type=MESH).wait_recv()
        for cp, _, _ in cps:
            cp.wait_send()
        acc = gath[0]
        for i in range(1, N_DEV):
            acc = acc + gath[i]
        o_ref[...] = acc

    return pl.pallas_call(
        body, name="allreduce_small", out_shape=jax.ShapeDtypeStruct((P, LANES), F32),
        in_specs=[pl.BlockSpec(memory_space=pltpu.VMEM)], out_specs=pl.BlockSpec(memory_space=pltpu.VMEM),
        scratch_shapes=[pltpu.VMEM((N_DEV, P, LANES), F32), pltpu.SemaphoreType.DMA((N_DEV - 1,)),
                        pltpu.SemaphoreType.DMA((N_DEV - 1,))],
    )(pack)


def _pack_rows(arrs):
    rows = []
    for a in arrs:
        f = a.reshape(-1)
        f = jnp.pad(f, (0, (-f.shape[0]) % LANES))
        rows.append(f.reshape(-1, LANES))
    return jnp.concatenate(rows, axis=0)


def _unpack_rows(pack, shapes):
    out, r = [], 0
    for s in shapes:
        n = math.prod(s)
        nr = -(-n // LANES)
        out.append(pack[r:r + nr].reshape(-1)[:n].reshape(s))
        r += nr
    return out


_SMALL = ["norm_mix_pre", "ml_head_norm", "b_gate_a", "b_gate_b", "norm_mix_post", "norm_ffn_pre", "norm_ffn_post",
          "conv_b", "b_ml_i", "b_ml_f", "b_fox_f"]
_BIG = ["w_in", "w_branch_a", "w_branch_b", "w_out", "w_up", "w_down"]
_WEIGHTS = ['norm_mix_pre', 'w_in', 'b_ml_i', 'b_ml_f', 'ml_head_norm', 'b_fox_f', 'b_gate_a', 'b_gate_b', 'w_branch_a',
            'w_branch_b', 'w_out', 'norm_mix_post', 'norm_ffn_pre', 'w_up', 'conv_w', 'conv_b', 'w_down', 'norm_ffn_post']


def _stack_cols(g):
    R = g.shape[0]
    return jnp.transpose(g.reshape(R, 4, -1), (1, 0, 2))


def _unstack_cols(g):
    return jnp.transpose(g, (1, 0, 2)).reshape(g.shape[1], -1)


def kernel(x, norm_mix_pre, w_in, b_ml_i, b_ml_f, ml_head_norm, b_fox_f, b_gate_a, b_gate_b, w_branch_a, w_branch_b, w_out, norm_mix_post, norm_ffn_pre, w_up, conv_w, conv_b, w_down, norm_ffn_post, loss_target, m_norm_mix_pre, m_w_in, m_b_ml_i, m_b_ml_f, m_ml_head_norm, m_b_fox_f, m_b_gate_a, m_b_gate_b, m_w_branch_a, m_w_branch_b, m_w_out, m_norm_mix_post, m_norm_ffn_pre, m_w_up, m_conv_w, m_conv_b, m_w_down, m_norm_ffn_post, v_norm_mix_pre, v_w_in, v_b_ml_i, v_b_ml_f, v_ml_head_norm, v_b_fox_f, v_b_gate_a, v_b_gate_b, v_w_branch_a, v_w_branch_b, v_w_out, v_norm_mix_post, v_norm_ffn_pre, v_w_up, v_conv_w, v_conv_b, v_w_down, v_norm_ffn_post):
    args = dict(locals())
    w = {n: args[n] for n in _WEIGHTS}
    mom = {n: args["m_" + n] for n in _WEIGHTS}
    var = {n: args["v_" + n] for n in _WEIGHTS}
    cx, cy, cc = lax.axis_index("x"), lax.axis_index("y"), lax.axis_index("c")
    kme = 2 * cx + cy
    cvec = jnp.reshape(cc, (1,)).astype(jnp.int32)
    kvec = jnp.reshape(kme, (1,)).astype(jnp.int32)

    shards = [w[n][0].astype(BF16) for n in _BIG]
    (g_in, g_a, g_b, g_out, g_up, g_down), g_cw = _gather_weights(shards, w["conv_w"][0])
    wmain, wsmall = _split_w_in(_unstack_cols(g_in))
    full = lambda g: g.reshape(-1, g.shape[2])
    p = {n: w[n] for n in _SMALL}
    p["conv_w"] = _unstack_cols(g_cw)

    loss_row, grad_x, big, small = _local_step(x[0], loss_target[0], wmain, wsmall, full(g_a), full(g_b), full(g_out),
                                               _unstack_cols(g_up), full(g_down), p)

    stacked = [_stack_cols(_join_w_in(big["wmain"], big["wsmall"])),
               big["w_branch_a"].reshape(4, -1, D_MODEL), big["w_branch_b"].reshape(4, -1, D_MODEL),
               big["w_out"].reshape(4, -1, D_MODEL), _stack_cols(big["w_up"]), big["w_down"].reshape(4, -1, D_MODEL)]
    from_sibling = _exchange_sibling_halves(stacked)
    chip_sums = [_add_halves(g, r, cvec, "add_sibling_" + n) for g, r, n in zip(stacked, from_sibling, _BIG)]
    from_chips = _exchange_chips(chip_sums)
    mine = [_add_chips(s, r, kvec, "add_chips_" + n) for s, r, n in zip(chip_sums, from_chips, _BIG)]
    grads = dict(zip(_BIG, _join_sibling_halves(mine)))

    small_names = _SMALL + ["conv_w"]
    pack = _pack_rows([small[n] for n in small_names] + [loss_row])
    pack = jnp.pad(pack, ((0, (-pack.shape[0]) % 8), (0, 0)))
    full_shapes = [small["conv_w"].shape if n == "conv_w" else w[n][0].shape for n in small_names]
    total = _unpack_rows(_allreduce_small(pack), full_shapes + [loss_row.shape])
    for n, t in zip(small_names, total):
        grads[n] = t
    loss = total[-1][0, 0]
    grads["conv_w"] = lax.dynamic_slice_in_dim(grads["conv_w"], kme * conv_w.shape[2], conv_w.shape[2], axis=1)

    delta, new_m, new_v = {}, {}, {}
    for n in _BIG:
        delta[n], new_m[n], new_v[n] = _adamw(w[n][0], grads[n], mom[n][0], var[n][0], "adamw_" + n)
    sw = _pack_rows([w[n][0] for n in small_names])
    pad = ((0, (-sw.shape[0]) % 8), (0, 0))
    packs = [jnp.pad(_pack_rows([d[n][0] for n in small_names]), pad) for d in (w, mom, var)]
    gp = jnp.pad(_pack_rows([grads[n] for n in small_names]), pad)
    shapes = [w[n][0].shape for n in small_names]
    for dst, res in zip((delta, new_m, new_v), _adamw(packs[0], gp, packs[1], packs[2], "adamw_small")):
        for n, t in zip(small_names, _unpack_rows(res, shapes)):
            dst[n] = t

    lead = lambda t: t[None]
    return (loss, grad_x[None], *[lead(grads[n]) for n in _WEIGHTS], *[lead(delta[n]) for n in _WEIGHTS],
            *[lead(new_m[n]) for n in _WEIGHTS], *[lead(new_v[n]) for n in _WEIGHTS])
```

```python
import functools
import math

import jax
import jax.numpy as jnp
from jax import lax
from jax.experimental import pallas as pl
from jax.experimental.pallas import tpu as pltpu

F32 = jnp.float32
BF16 = jnp.bfloat16
MESH = pl.DeviceIdType.MESH

D_MODEL = 1024
ML_HEADS = 4
ML_DQK = 128
ML_DV = 256
FOX_HEADS = 8
FOX_DH = 128
D_FF = 2816
GATE_CAP = 15.0
EPS = 1e-6
ADAM_LR, ADAM_B1, ADAM_B2, ADAM_EPS, ADAM_WD, ADAM_STEP = 0.001, 0.9, 0.999, 1e-08, 0.01, 10

LANES = 128
MLC = 128
FOX_T = 256
ROW_T = 512
VMEM_LIMIT = 56 * 1024 * 1024

C_QM, C_KM, C_VM, C_OM, C_QF, C_KF, C_VF, C_GA, C_GB = 0, 512, 1024, 2048, 3072, 4096, 5120, 6144, 7168
N_MAIN = 8192
N_SMALL = 384
R_QM, R_KM, R_VM, R_I, R_F, R_OM, R_QF, R_KF, R_VF, R_FF, R_GA, R_GB, R_END = (
    0, 512, 1024, 2048, 2052, 2056, 3080, 4104, 5128, 6152, 6160, 7184, 8208)


def _cparams(sem=None):
    return pltpu.CompilerParams(dimension_semantics=sem, vmem_limit_bytes=VMEM_LIMIT)


def _tile(n, target):
    if n <= target:
        return n
    best = None
    for t in range(LANES, target + 1, LANES):
        if n % t == 0:
            best = t
    assert best is not None, (n, target)
    return best


def _dot(a, b, dims):
    return lax.dot_general(a, b, (dims, ((), ())), preferred_element_type=F32)


def _dot_nn(a, b):
    return _dot(a, b, ((1,), (0,)))


def _dot_nt(a, b):
    return _dot(a, b, ((1,), (1,)))


def _dot_tn(a, b):
    return _dot(a, b, ((0,), (0,)))


_DOTS = {"nn": _dot_nn, "nt": _dot_nt, "tn": _dot_tn}


def _mm(a, b, mode, out_dtype, name, tm=1024, tn=1408, tk=1024):
    if mode == "nn":
        (M, K), (K2, N) = a.shape, b.shape
    elif mode == "nt":
        (M, K), (N, K2) = a.shape, b.shape
    else:
        (K, M), (K2, N) = a.shape, b.shape
    assert K == K2, (name, a.shape, b.shape)
    tm, tn, tk = _tile(M, tm), _tile(N, tn), _tile(K, tk)
    nk = K // tk
    dot = _DOTS[mode]

    def body(a_ref, b_ref, o_ref, *acc):
        part = dot(a_ref[...], b_ref[...])
        if nk == 1:
            o_ref[...] = part.astype(o_ref.dtype)
        else:
            acc_ref, = acc
            k = pl.program_id(2)

            @pl.when(k == 0)
            def _():
                acc_ref[...] = part

            @pl.when(k > 0)
            def _():
                acc_ref[...] += part

            @pl.when(k == nk - 1)
            def _():
                o_ref[...] = acc_ref[...].astype(o_ref.dtype)

    if mode == "nn":
        a_spec = pl.BlockSpec((tm, tk), lambda i, j, k: (i, k))
        b_spec = pl.BlockSpec((tk, tn), lambda i, j, k: (k, j))
    elif mode == "nt":
        a_spec = pl.BlockSpec((tm, tk), lambda i, j, k: (i, k))
        b_spec = pl.BlockSpec((tn, tk), lambda i, j, k: (j, k))
    else:
        a_spec = pl.BlockSpec((tk, tm), lambda i, j, k: (k, i))
        b_spec = pl.BlockSpec((tk, tn), lambda i, j, k: (k, j))
    return pl.pallas_call(
        body, name=name,
        out_shape=jax.ShapeDtypeStruct((M, N), out_dtype),
        grid=(M // tm, N // tn, nk),
        in_specs=[a_spec, b_spec],
        out_specs=pl.BlockSpec((tm, tn), lambda i, j, k: (i, j)),
        scratch_shapes=[pltpu.VMEM((tm, tn), F32)] if nk > 1 else [],
        compiler_params=_cparams(("parallel", "parallel", "arbitrary")),
    )(a, b)


def _rstd(x):
    return lax.rsqrt(jnp.mean(x * x, axis=-1, keepdims=True) + EPS)


def _rmsnorm_fwd(x, g, name):
    S, D = x.shape
    T = _tile(S, ROW_T)

    def body(x_ref, g_ref, o_ref):
        xv = x_ref[...]
        o_ref[...] = (xv * _rstd(xv) * g_ref[...]).astype(o_ref.dtype)

    return pl.pallas_call(
        body, name=name, out_shape=jax.ShapeDtypeStruct((S, D), BF16), grid=(S // T,),
        in_specs=[pl.BlockSpec((T, D), lambda i: (i, 0)), pl.BlockSpec((1, D), lambda i: (0, 0))],
        out_specs=pl.BlockSpec((T, D), lambda i: (i, 0)),
        compiler_params=_cparams(("parallel",)),
    )(x, g)


def _resid_norm_fwd(x, z, g, name):
    S, D = x.shape
    T = _tile(S, ROW_T)

    def body(x_ref, z_ref, g_ref, o_ref):
        zv = z_ref[...]
        o_ref[...] = x_ref[...] + zv * _rstd(zv) * g_ref[...]

    row = pl.BlockSpec((T, D), lambda i: (i, 0))
    return pl.pallas_call(
        body, name=name, out_shape=jax.ShapeDtypeStruct((S, D), F32), grid=(S // T,),
        in_specs=[row, row, pl.BlockSpec((1, D), lambda i: (0, 0))],
        out_specs=row, compiler_params=_cparams(("parallel",)),
    )(x, z, g)


def _rmsnorm_bwd_math(dy, xv, g):
    r = _rstd(xv)
    u = dy * g
    dx = r * u - xv * (r * r * r) * jnp.mean(u * xv, axis=-1, keepdims=True)
    return dx, dy * xv * r


def _rmsnorm_bwd(dys, xin, g, resid, out_dtype, name):
    S, D = xin.shape
    T = _tile(S, ROW_T)
    has_resid = resid is not None
    ndy = len(dys)

    def body(*refs):
        dy_refs, (x_ref, g_ref) = refs[:ndy], refs[ndy:ndy + 2]
        dx_ref, dg_ref = refs[-2:]
        dy = dy_refs[0][...]
        for r in dy_refs[1:]:
            dy = dy + r[...]
        dx, dgt = _rmsnorm_bwd_math(dy, x_ref[...], g_ref[...])
        if has_resid:
            dx = dx + refs[ndy + 2][...]
        dx_ref[...] = dx.astype(dx_ref.dtype)

        @pl.when(pl.program_id(0) == 0)
        def _():
            dg_ref[...] = jnp.zeros_like(dg_ref)

        dg_ref[...] += jnp.sum(dgt, axis=0, keepdims=True)

    row = pl.BlockSpec((T, D), lambda i: (i, 0))
    vec = pl.BlockSpec((1, D), lambda i: (0, 0))
    ins = list(dys) + [xin, g] + ([resid] if has_resid else [])
    return pl.pallas_call(
        body, name=name,
        out_shape=(jax.ShapeDtypeStruct((S, D), out_dtype), jax.ShapeDtypeStruct((1, D), F32)),
        grid=(S // T,), in_specs=[row] * ndy + [row, vec] + ([row] if has_resid else []),
        out_specs=(row, vec), compiler_params=_cparams(("arbitrary",)),
    )(*ins)


def _loss_head(x1, d, g, target, name):
    S, D = x1.shape
    T = _tile(S, ROW_T)

    def body(x_ref, d_ref, g_ref, t_ref, loss_ref, dy_ref, dd_ref, dg_ref):
        dv, gv = d_ref[...], g_ref[...]
        y = x_ref[...] + dv * _rstd(dv) * gv
        diff = y - t_ref[...]
        dy = diff * (1.0 / D)
        dy_ref[...] = dy
        dd, dgt = _rmsnorm_bwd_math(dy, dv, gv)
        dd_ref[...] = dd.astype(dd_ref.dtype)

        @pl.when(pl.program_id(0) == 0)
        def _():
            dg_ref[...] = jnp.zeros_like(dg_ref)
            loss_ref[...] = jnp.zeros_like(loss_ref)

        dg_ref[...] += jnp.sum(dgt, axis=0, keepdims=True)
        part = jnp.sum(jnp.sum(diff * diff, axis=1, keepdims=True), axis=0, keepdims=True)
        loss_ref[...] += (0.5 / D) * part

    row = pl.BlockSpec((T, D), lambda i: (i, 0))
    vec = pl.BlockSpec((1, D), lambda i: (0, 0))
    return pl.pallas_call(
        body, name=name,
        out_shape=(jax.ShapeDtypeStruct((1, LANES), F32), jax.ShapeDtypeStruct((S, D), F32),
                   jax.ShapeDtypeStruct((S, D), BF16), jax.ShapeDtypeStruct((1, D), F32)),
        grid=(S // T,), in_specs=[row, row, vec, row],
        out_specs=(pl.BlockSpec((1, LANES), lambda i: (0, 0)), row, row, vec),
        compiler_params=_cparams(("arbitrary",)),
    )(x1, d, g, target)


def _merge_fwd(ya, yb, pm, ba, bb, name):
    S, D = ya.shape
    T = _tile(S, ROW_T)

    def body(ya_ref, yb_ref, ga_ref, gb_ref, ba_ref, bb_ref, o_ref):
        sa = jax.nn.sigmoid(ga_ref[...] + ba_ref[...])
        sb = jax.nn.sigmoid(gb_ref[...] + bb_ref[...])
        o_ref[...] = (sa * ya_ref[...] + sb * yb_ref[...]).astype(o_ref.dtype)

    row = pl.BlockSpec((T, D), lambda i: (i, 0))
    vec = pl.BlockSpec((1, D), lambda i: (0, 0))
    return pl.pallas_call(
        body, name=name, out_shape=jax.ShapeDtypeStruct((S, D), BF16), grid=(S // T,),
        in_specs=[row, row, pl.BlockSpec((T, D), lambda i: (i, C_GA // D)),
                  pl.BlockSpec((T, D), lambda i: (i, C_GB // D)), vec, vec],
        out_specs=row, compiler_params=_cparams(("parallel",)),
    )(ya, yb, pm, pm, ba, bb)


def _merge_bwd(dmerged, ya, yb, pm, ba, bb, name):
    S, D = ya.shape
    T = _tile(S, ROW_T)

    def body(dm_ref, ya_ref, yb_ref, ga_ref, gb_ref, ba_ref, bb_ref,
             dya_ref, dyb_ref, dga_ref, dgb_ref, dba_ref, dbb_ref):
        dm = dm_ref[...]
        sa = jax.nn.sigmoid(ga_ref[...] + ba_ref[...])
        sb = jax.nn.sigmoid(gb_ref[...] + bb_ref[...])
        dya_ref[...] = (dm * sa).astype(dya_ref.dtype)
        dyb_ref[...] = (dm * sb).astype(dyb_ref.dtype)
        dga = dm * ya_ref[...] * sa * (1.0 - sa)
        dgb = dm * yb_ref[...] * sb * (1.0 - sb)
        dga_ref[...] = dga.astype(dga_ref.dtype)
        dgb_ref[...] = dgb.astype(dgb_ref.dtype)

        @pl.when(pl.program_id(0) == 0)
        def _():
            dba_ref[...] = jnp.zeros_like(dba_ref)
            dbb_ref[...] = jnp.zeros_like(dbb_ref)

        dba_ref[...] += jnp.sum(dga, axis=0, keepdims=True)
        dbb_ref[...] += jnp.sum(dgb, axis=0, keepdims=True)

    row = pl.BlockSpec((T, D), lambda i: (i, 0))
    vec = pl.BlockSpec((1, D), lambda i: (0, 0))
    act = jax.ShapeDtypeStruct((S, D), BF16)
    v1 = jax.ShapeDtypeStruct((1, D), F32)
    return pl.pallas_call(
        body, name=name, out_shape=(act, act, act, act, v1, v1), grid=(S // T,),
        in_specs=[row, row, row, pl.BlockSpec((T, D), lambda i: (i, C_GA // D)),
                  pl.BlockSpec((T, D), lambda i: (i, C_GB // D)), vec, vec],
        out_specs=(row, row, row, row, vec, vec), compiler_params=_cparams(("arbitrary",)),
    )(dmerged, ya, yb, pm, pm, ba, bb)


_GELU_C = math.sqrt(2.0 / math.pi)


def _gelu(g):
    t = jnp.tanh(_GELU_C * (g + 0.044715 * g * g * g))
    return 0.5 * g * (1.0 + t), t


def _gelu_grad(g, t):
    return 0.5 * (1.0 + t) + 0.5 * g * (1.0 - t * t) * _GELU_C * (1.0 + 3 * 0.044715 * g * g)


def _shift_down(v, halo_ref, first, rows):
    T = v.shape[0]
    keep = jnp.where(first, 0.0, 1.0)
    h7 = halo_ref[7:8, :] * keep
    h6 = halo_ref[6:7, :] * keep
    m1 = jnp.where(rows == 0, h7, pltpu.roll(v, 1, 0))
    m2 = jnp.where(rows == 0, h6, jnp.where(rows == 1, h7, pltpu.roll(v, 2, 0)))
    return m1, m2


def _conv_act_fwd(up, cw, cb, name):
    S, F2 = up.shape
    Fh = F2 // 2
    T = _tile(S, ROW_T)
    tc = _tile(Fh, 256)
    ncol = Fh // tc
    hb = T // 8

    def body(ua_ref, ug_ref, ha_ref, hg_ref, wa_ref, wg_ref, ba_ref, bg_ref, o_ref):
        first = pl.program_id(0) == 0
        rows = lax.broadcasted_iota(jnp.int32, (T, tc), 0)

        def conv(u_ref, h_ref, w_ref, b_ref):
            v = u_ref[...]
            m1, m2 = _shift_down(v, h_ref, first, rows)
            return b_ref[...] + w_ref[0:1, :] * m2 + w_ref[1:2, :] * m1 + w_ref[2:3, :] * v

        a = conv(ua_ref, ha_ref, wa_ref, ba_ref)
        g = conv(ug_ref, hg_ref, wg_ref, bg_ref)
        o_ref[...] = (_gelu(g)[0] * a).astype(o_ref.dtype)

    halo = lambda off: pl.BlockSpec((8, tc), lambda i, j: (jnp.maximum(i * hb - 1, 0), j + off))
    return pl.pallas_call(
        body, name=name, out_shape=jax.ShapeDtypeStruct((S, Fh), BF16), grid=(S // T, ncol),
        in_specs=[pl.BlockSpec((T, tc), lambda i, j: (i, j)), pl.BlockSpec((T, tc), lambda i, j: (i, j + ncol)),
                  halo(0), halo(ncol),
                  pl.BlockSpec((3, tc), lambda i, j: (0, j)), pl.BlockSpec((3, tc), lambda i, j: (0, j + ncol)),
                  pl.BlockSpec((1, tc), lambda i, j: (0, j)), pl.BlockSpec((1, tc), lambda i, j: (0, j + ncol))],
        out_specs=pl.BlockSpec((T, tc), lambda i, j: (i, j)),
        compiler_params=_cparams(("parallel", "parallel")),
    )(up, up, up, up, cw, cw, cb, cb)


def _conv_act_bwd_du(up, dact, cw, cb, name):
    S, F2 = up.shape
    Fh = F2 // 2
    T = _tile(S, ROW_T)
    tc = _tile(Fh, 256)
    ncol = Fh // tc
    hb = T // 8

    def body(ua_ref, ug_ref, ha_ref, hg_ref, wa_ref, wg_ref, ba_ref, bg_ref, da_ref,
             dua_ref, dug_ref, dwa_ref, dwg_ref, dba_ref, dbg_ref):
        first = pl.program_id(1) == 0
        rows = lax.broadcasted_iota(jnp.int32, (T, tc), 0)

        def conv(u_ref, h_ref, w_ref, b_ref):
            v = u_ref[...]
            m1, m2 = _shift_down(v, h_ref, first, rows)
            return b_ref[...] + w_ref[0:1, :] * m2 + w_ref[1:2, :] * m1 + w_ref[2:3, :] * v, (m2, m1, v)

        a, taps_a = conv(ua_ref, ha_ref, wa_ref, ba_ref)
        g, taps_g = conv(ug_ref, hg_ref, wg_ref, bg_ref)
        gel, t = _gelu(g)
        dact_v = da_ref[...]
        dua = dact_v * gel
        dug = dact_v * a * _gelu_grad(g, t)
        dua_ref[...] = dua
        dug_ref[...] = dug

        @pl.when(first)
        def _():
            for r in (dwa_ref, dwg_ref, dba_ref, dbg_ref):
                r[...] = jnp.zeros_like(r)

        for du, taps, dw_ref, db_ref in ((dua, taps_a, dwa_ref, dba_ref), (dug, taps_g, dwg_ref, dbg_ref)):
            db_ref[...] += jnp.sum(du, axis=0, keepdims=True)
            for j in range(3):
                dw_ref[j:j + 1, :] += jnp.sum(du * taps[j], axis=0, keepdims=True)

    halo = lambda off: pl.BlockSpec((8, tc), lambda j, i: (jnp.maximum(i * hb - 1, 0), j + off))
    outs = pl.pallas_call(
        body, name=name,
        out_shape=(jax.ShapeDtypeStruct((S, Fh), F32), jax.ShapeDtypeStruct((S, Fh), F32),
                   jax.ShapeDtypeStruct((3, Fh), F32), jax.ShapeDtypeStruct((3, Fh), F32),
                   jax.ShapeDtypeStruct((1, Fh), F32), jax.ShapeDtypeStruct((1, Fh), F32)),
        grid=(ncol, S // T),
        in_specs=[pl.BlockSpec((T, tc), lambda j, i: (i, j)), pl.BlockSpec((T, tc), lambda j, i: (i, j + ncol)),
                  halo(0), halo(ncol),
                  pl.BlockSpec((3, tc), lambda j, i: (0, j)), pl.BlockSpec((3, tc), lambda j, i: (0, j + ncol)),
                  pl.BlockSpec((1, tc), lambda j, i: (0, j)), pl.BlockSpec((1, tc), lambda j, i: (0, j + ncol)),
                  pl.BlockSpec((T, tc), lambda j, i: (i, j))],
        out_specs=(pl.BlockSpec((T, tc), lambda j, i: (i, j)), pl.BlockSpec((T, tc), lambda j, i: (i, j)),
                   pl.BlockSpec((3, tc), lambda j, i: (0, j)), pl.BlockSpec((3, tc), lambda j, i: (0, j)),
                   pl.BlockSpec((1, tc), lambda j, i: (0, j)), pl.BlockSpec((1, tc), lambda j, i: (0, j))),
        compiler_params=_cparams(("parallel", "arbitrary")),
    )(up, up, up, up, cw, cw, cb, cb, dact)
    return outs


def _conv_bwd_dup(du, cw, col_off, name):
    S, Fh = du.shape
    T = _tile(S, ROW_T)
    tc = _tile(Fh, 256)
    ncol, nrow, hb, nhb = Fh // tc, S // T, T // 8, S // 8

    def body(du_ref, h_ref, w_ref, o_ref):
        keep = jnp.where(pl.program_id(0) == nrow - 1, 0.0, 1.0)
        rows = lax.broadcasted_iota(jnp.int32, (T, tc), 0)
        v = du_ref[...]
        h0 = h_ref[0:1, :] * keep
        h1 = h_ref[1:2, :] * keep
        p1 = jnp.where(rows == T - 1, h0, pltpu.roll(v, T - 1, 0))
        p2 = jnp.where(rows == T - 2, h0, jnp.where(rows == T - 1, h1, pltpu.roll(v, T - 2, 0)))
        o_ref[...] = (w_ref[2:3, :] * v + w_ref[1:2, :] * p1 + w_ref[0:1, :] * p2).astype(o_ref.dtype)

    return pl.pallas_call(
        body, name=name, out_shape=jax.ShapeDtypeStruct((S, Fh), BF16), grid=(nrow, ncol),
        in_specs=[pl.BlockSpec((T, tc), lambda i, j: (i, j)),
                  pl.BlockSpec((8, tc), lambda i, j: (jnp.minimum((i + 1) * hb, nhb - 1), j)),
                  pl.BlockSpec((3, tc), lambda i, j: (0, j + col_off))],
        out_specs=pl.BlockSpec((T, tc), lambda i, j: (i, j)),
        compiler_params=_cparams(("parallel", "parallel")),
    )(du, du, cw)


def _split3(x):
    hi = x.astype(BF16)
    r1 = x - hi.astype(F32)
    mid = r1.astype(BF16)
    lo = (r1 - mid.astype(F32)).astype(BF16)
    return hi, mid, lo


def _tri_dot(tri, x):
    hi, mid, lo = _split3(x)
    return _dot_nn(tri, hi) + _dot_nn(tri, mid) + _dot_nn(tri, lo)


def _log_sigmoid(x):
    return jnp.minimum(x, 0.0) - jnp.log(1.0 + jnp.exp(-jnp.abs(x)))


def _tri_mask(n, lower):
    r = lax.broadcasted_iota(jnp.int32, (n, n), 0)
    c = lax.broadcasted_iota(jnp.int32, (n, n), 1)
    return (r >= c) if lower else (r <= c)


def _gates_fwd(ps, bi, bf, bff, name):
    S = ps.shape[0]
    NC = S // MLC

    def body(ps_ref, bi_ref, bf_ref, bff_ref, a_ref, A_ref, wi_ref, em_ref, wk_ref, dec_ref, F_ref, m_scr, f_scr):
        @pl.when(pl.program_id(0) == 0)
        def _():
            m_scr[...] = jnp.zeros_like(m_scr)
            f_scr[...] = jnp.zeros_like(f_scr)

        rows = lax.broadcasted_iota(jnp.int32, (MLC, LANES), 0)
        ltri = _tri_mask(MLC, True).astype(BF16)
        li = GATE_CAP * jnp.tanh((ps_ref[:, 0:LANES] + bi_ref[...]) / GATE_CAP)
        lf = _log_sigmoid(GATE_CAP * jnp.tanh((ps_ref[:, LANES:2 * LANES] + bf_ref[...]) / GATE_CAP))
        b = _tri_dot(ltri, lf)
        a = li - b
        cm = a
        sh = 1
        while sh < MLC:
            cm = jnp.where(rows >= sh, jnp.maximum(cm, pltpu.roll(cm, sh, 0)), cm)
            sh *= 2
        m0 = m_scr[...]
        A = jnp.maximum(cm, m0)
        a_ref[...] = a
        A_ref[...] = A
        A_last = A_ref[MLC - 1:MLC, :]
        wi_ref[...] = jnp.exp(m0 - A)
        em_ref[...] = jnp.exp(-(b + A))
        wk_ref[...] = jnp.exp(a - A_last)
        dec_ref[0] = jnp.exp(m0 - A_last)
        F_ref[...] = b
        m_scr[...] = F_ref[MLC - 1:MLC, :] + A_last
        lfg = _log_sigmoid(ps_ref[:, 2 * LANES:3 * LANES] + bff_ref[...])
        F_ref[...] = _tri_dot(ltri, lfg) + f_scr[...]
        f_scr[...] = F_ref[MLC - 1:MLC, :]

    col = pl.BlockSpec((MLC, LANES), lambda c: (c, 0))
    vec = pl.BlockSpec((1, LANES), lambda c: (0, 0))
    cs = jax.ShapeDtypeStruct((S, LANES), F32)
    return pl.pallas_call(
        body, name=name,
        out_shape=(cs, cs, cs, cs, cs, jax.ShapeDtypeStruct((NC, 1, LANES), F32), cs),
        grid=(NC,), in_specs=[pl.BlockSpec((MLC, N_SMALL), lambda c: (c, 0)), vec, vec, vec],
        out_specs=(col, col, col, col, col, pl.BlockSpec((1, 1, LANES), lambda c: (c, 0, 0)), col),
        scratch_shapes=[pltpu.VMEM((1, LANES), F32), pltpu.VMEM((1, LANES), F32)],
        compiler_params=_cparams(("arbitrary",)),
    )(ps, bi, bf, bff)


def _gates_bwd(ps, bi, bf, bff, rk, kc, tch, dF, name):
    S = ps.shape[0]
    NC = S // MLC

    def body(ps_ref, bi_ref, bf_ref, bff_ref, rk_ref, kc_ref, t_ref, dF_ref, dps_ref, db_ref, carry):
        @pl.when(pl.program_id(0) == 0)
        def _():
            carry[...] = jnp.zeros_like(carry)
            db_ref[...] = jnp.zeros_like(db_ref)

        lanes = lax.broadcasted_iota(jnp.int32, (MLC, LANES), 1)
        utri = _tri_mask(MLC, False).astype(BF16)
        ti = jnp.tanh((ps_ref[:, 0:LANES] + bi_ref[...]) / GATE_CAP)
        t_end, t_start = t_ref[0, 0:1, :], t_ref[0, 1:2, :]
        rk = rk_ref[...]
        rk = rk - (jnp.sum(rk, axis=0, keepdims=True) - (t_start - t_end)) * (1.0 / MLC)
        dpi = jnp.where(lanes < ML_HEADS, (kc_ref[...] - rk) * (1.0 - ti * ti), 0.0)
        tf = jnp.tanh((ps_ref[:, LANES:2 * LANES] + bf_ref[...]) / GATE_CAP)
        dlf = _tri_dot(utri, rk) + t_end
        dpf = jnp.where(lanes < ML_HEADS, dlf * jax.nn.sigmoid(-GATE_CAP * tf) * (1.0 - tf * tf), 0.0)
        dFv = dF_ref[...]
        dlfg = _tri_dot(utri, dFv) + carry[...]
        carry[...] += jnp.sum(dFv, axis=0, keepdims=True)
        dpff = jnp.where(lanes < FOX_HEADS, dlfg * jax.nn.sigmoid(-(ps_ref[:, 2 * LANES:3 * LANES] + bff_ref[...])), 0.0)
        for n, dp in enumerate((dpi, dpf, dpff)):
            dps_ref[:, n * LANES:(n + 1) * LANES] = dp.astype(dps_ref.dtype)
            db_ref[:, n * LANES:(n + 1) * LANES] += jnp.sum(dp, axis=0, keepdims=True)

    rev = lambda c: (NC - 1 - c, 0)
    col = pl.BlockSpec((MLC, LANES), rev)
    vec = pl.BlockSpec((1, LANES), lambda c: (0, 0))
    wide = pl.BlockSpec((MLC, N_SMALL), rev)
    return pl.pallas_call(
        body, name=name,
        out_shape=(jax.ShapeDtypeStruct((S, N_SMALL), BF16), jax.ShapeDtypeStruct((1, N_SMALL), F32)),
        grid=(NC,),
        in_specs=[wide, vec, vec, vec, col, col, pl.BlockSpec((1, 2, LANES), lambda c: (NC - 1 - c, 0, 0)), col],
        out_specs=(wide, pl.BlockSpec((1, N_SMALL), lambda c: (0, 0))),
        scratch_shapes=[pltpu.VMEM((1, LANES), F32)],
        compiler_params=_cparams(("arbitrary",)),
    )(ps, bi, bf, bff, rk, kc, tch, dF)


_ML_SCALE = ML_DQK ** -0.5


def _ml_specs(rev, NC):
    idx = (lambda c: NC - 1 - c) if rev else (lambda c: c)
    qk = lambda blk: pl.BlockSpec((MLC, ML_HEADS * ML_DQK), lambda c: (idx(c), blk))
    wide = lambda blk: pl.BlockSpec((MLC, D_MODEL), lambda c: (idx(c), blk))
    col = pl.BlockSpec((MLC, LANES), lambda c: (idx(c), 0))
    return idx, qk, wide, col


def _ml_intra(q_ref, k_ref, arow_ref, A_ref, h):
    hs = slice(h * ML_DQK, (h + 1) * ML_DQK)
    qf = q_ref[:, hs] * _ML_SCALE
    kf = k_ref[:, hs]
    qb, kb = qf.astype(BF16), kf.astype(BF16)
    qk = _dot_nt(qb, kb)
    logw = arow_ref[h:h + 1, :] - A_ref[:, h:h + 1]
    W = jnp.exp(jnp.where(_tri_mask(MLC, True), logw, -1e30))
    return qb, kb, qf, kf, qk, W


def _mlstm_fwd(pm, a_row, A, wi, em, wk, dec, w_hn, name):
    S = pm.shape[0]
    NC = S // MLC
    _, qk, wide, col = _ml_specs(False, NC)

    def body(q_ref, k_ref, v_ref, o_ref, arow_ref, A_ref, wi_ref, em_ref, wk_ref, dec_ref, whn_ref,
             ha_ref, hp_ref, den_ref, cst_ref, nst_ref, C_scr, n_scr):
        @pl.when(pl.program_id(0) == 0)
        def _():
            C_scr[...] = jnp.zeros_like(C_scr)
            n_scr[...] = jnp.zeros_like(n_scr)

        lanes = lax.broadcasted_iota(jnp.int32, (MLC, LANES), 1)
        den_tile = jnp.zeros((MLC, LANES), F32)
        for h in range(ML_HEADS):
            vs = slice(h * ML_DV, (h + 1) * ML_DV)
            qb, kb, qf, kf, qk_, W = _ml_intra(q_ref, k_ref, arow_ref, A_ref, h)
            vb = v_ref[:, vs].astype(BF16)
            Cf = C_scr[h]
            Cb = Cf.astype(BF16)
            nrow = n_scr[h]
            cst_ref[0, h] = Cb
            nst_ref[0, h] = nrow
            s = qk_ * W
            wic = wi_ref[:, h:h + 1]
            num = _dot_nn(s.astype(BF16), vb) + wic * _dot_nt(qb, Cb)
            den = jnp.sum(s, axis=1, keepdims=True) + wic * jnp.sum(qf * nrow, axis=1, keepdims=True)
            hp = num / jnp.maximum(jnp.abs(den), em_ref[:, h:h + 1])
            hp_ref[:, vs] = hp
            den_tile = jnp.where(lanes == h, den, den_tile)
            hn = hp * _rstd(hp) * whn_ref[:, vs]
            ha_ref[:, vs] = (hn * jax.nn.sigmoid(o_ref[:, vs])).astype(ha_ref.dtype)
            wkc = wk_ref[:, h:h + 1]
            kw = kf * wkc
            d = dec_ref[0, :, h:h + 1]
            C_scr[h] = d * Cf + _dot_tn(vb, kw.astype(BF16))
            n_scr[h] = d * nrow + jnp.sum(kw, axis=0, keepdims=True)
        den_ref[...] = den_tile

    return pl.pallas_call(
        body, name=name,
        out_shape=(jax.ShapeDtypeStruct((S, D_MODEL), BF16), jax.ShapeDtypeStruct((S, D_MODEL), F32),
                   jax.ShapeDtypeStruct((S, LANES), F32),
                   jax.ShapeDtypeStruct((NC, ML_HEADS, ML_DV, ML_DQK), BF16),
                   jax.ShapeDtypeStruct((NC, ML_HEADS, 1, ML_DQK), F32)),
        grid=(NC,),
        in_specs=[qk(C_QM // 512), qk(C_KM // 512), wide(C_VM // D_MODEL), wide(C_OM // D_MODEL),
                  pl.BlockSpec((8, MLC), lambda c: (0, c)), col, col, col, col,
                  pl.BlockSpec((1, 1, LANES), lambda c: (c, 0, 0)), pl.BlockSpec((1, D_MODEL), lambda c: (0, 0))],
        out_specs=(pl.BlockSpec((MLC, D_MODEL), lambda c: (c, 0)), pl.BlockSpec((MLC, D_MODEL), lambda c: (c, 0)),
                   col, pl.BlockSpec((1, ML_HEADS, ML_DV, ML_DQK), lambda c: (c, 0, 0, 0)),
                   pl.BlockSpec((1, ML_HEADS, 1, ML_DQK), lambda c: (c, 0, 0, 0))),
        scratch_shapes=[pltpu.VMEM((ML_HEADS, ML_DV, ML_DQK), F32), pltpu.VMEM((ML_HEADS, 1, ML_DQK), F32)],
        compiler_params=_cparams(("arbitrary",)),
    )(pm, pm, pm, pm, a_row, A, wi, em, wk, dec, w_hn)


def _mlstm_bwd(dha, pm, hp_all, den_all, a_row, A, wi, em, wk, dec, cst, nst, w_hn, name):
    S = pm.shape[0]
    NC = S // MLC
    idx, qk, wide, col = _ml_specs(True, NC)

    def body(dha_ref, q_ref, k_ref, v_ref, o_ref, hp_ref, den_ref, arow_ref, A_ref, wi_ref, em_ref, wk_ref,
             dec_ref, cst_ref, nst_ref, whn_ref,
             dq_ref, dk_ref, dv_ref, do_ref, rk_ref, kc_ref, t_ref, dwhn_ref, dC_scr, dn_scr, t_scr):
        @pl.when(pl.program_id(0) == 0)
        def _():
            dC_scr[...] = jnp.zeros_like(dC_scr)
            dn_scr[...] = jnp.zeros_like(dn_scr)
            t_scr[...] = jnp.zeros_like(t_scr)
            dwhn_ref[...] = jnp.zeros_like(dwhn_ref)

        lanes = lax.broadcasted_iota(jnp.int32, (MLC, LANES), 1)
        lane1 = lax.broadcasted_iota(jnp.int32, (1, LANES), 1)
        t_ref[0, 0:1, :] = t_scr[...]
        rk_tile = jnp.zeros((MLC, LANES), F32)
        kc_tile = jnp.zeros((MLC, LANES), F32)
        t_new = jnp.zeros((1, LANES), F32)
        for h in range(ML_HEADS):
            hs = slice(h * ML_DQK, (h + 1) * ML_DQK)
            vs = slice(h * ML_DV, (h + 1) * ML_DV)
            hp = hp_ref[:, vs]
            sig = jax.nn.sigmoid(o_ref[:, vs])
            whn = whn_ref[:, vs]
            r = _rstd(hp)
            dga = dha_ref[:, vs]
            do_ref[:, vs] = (dga * (hp * r * whn) * sig * (1.0 - sig)).astype(do_ref.dtype)
            dhn = dga * sig
            dhp, dwt = _rmsnorm_bwd_math(dhn, hp, whn)
            dwhn_ref[:, vs] += jnp.sum(dwt, axis=0, keepdims=True)
            den = den_ref[:, h:h + 1]
            floor = em_ref[:, h:h + 1]
            D = jnp.maximum(jnp.abs(den), floor)
            dnum = dhp / D
            dh_h = jnp.sum(dhp * hp, axis=1, keepdims=True)
            active = jnp.abs(den) >= floor
            dden = -dh_h / D * jnp.where(active, jnp.sign(den), 0.0)
            phi = jnp.where(active, 0.0, dh_h)
            qb, kb, qf, kf, qk_, W = _ml_intra(q_ref, k_ref, arow_ref, A_ref, h)
            vf = v_ref[:, vs]
            vb = vf.astype(BF16)
            Cb = cst_ref[0, h]
            nrow = nst_ref[0, h]
            wic = wi_ref[:, h:h + 1]
            wkc = wk_ref[:, h:h + 1]
            d = dec_ref[0, :, h:h + 1]
            dCn = dC_scr[h]
            dCb = dCn.astype(BF16)
            dnn = dn_scr[h]
            dnumb = dnum.astype(BF16)
            s = qk_ * W
            ds = (_dot_nt(dnumb, vb) + dden) * W
            dsb = ds.astype(BF16)
            dnw = (wic * dnum).astype(BF16)
            wd = wic * dden
            kw = kf * wkc
            dv_state = _dot_nt(kw.astype(BF16), dCb)
            dq = _dot_nn(dsb, kb) + _dot_nn(dnw, Cb) + wd * nrow
            dk_state = wkc * (_dot_nn(vb, dCb) + dnn)
            dk = _dot_tn(dsb, qb) + dk_state
            dv = _dot_tn(s.astype(BF16), dnumb) + dv_state
            dC = d * dCn + _dot_tn(dnw, qb)
            dn = d * dnn + jnp.sum(wd * qf, axis=0, keepdims=True)
            dC_scr[h] = dC
            dn_scr[h] = dn
            dq_ref[:, hs] = (dq * _ML_SCALE).astype(dq_ref.dtype)
            dk_ref[:, hs] = dk.astype(dk_ref.dtype)
            dv_ref[:, vs] = dv.astype(dv_ref.dtype)
            G = ds * qk_
            inter = _dot_nt(qb, Cb)
            qn = jnp.sum(qf * nrow, axis=1, keepdims=True)
            R = (jnp.sum(G, axis=1, keepdims=True)
                 + wic * (jnp.sum(dnum * inter, axis=1, keepdims=True) + dden * qn))
            K = jnp.sum(G.T, axis=1, keepdims=True) + jnp.sum(kf * dk_state, axis=1, keepdims=True)
            rk_tile = jnp.where(lanes == h, R - K, rk_tile)
            kc_tile = jnp.where(lanes == h, phi, kc_tile)
            tt = (jnp.sum(jnp.sum(dC * Cb.astype(F32), axis=1, keepdims=True), axis=0, keepdims=True)
                  + jnp.sum(dn * nrow, axis=1, keepdims=True))
            t_new = jnp.where(lane1 == h, tt, t_new)
        rk_ref[...] = rk_tile
        kc_ref[...] = kc_tile
        t_ref[0, 1:2, :] = t_new
        t_scr[...] = t_new

    act = lambda n: jax.ShapeDtypeStruct((S, n), BF16)
    cs = jax.ShapeDtypeStruct((S, LANES), F32)
    rowblk = lambda n: pl.BlockSpec((MLC, n), lambda c: (idx(c), 0))
    return pl.pallas_call(
        body, name=name,
        out_shape=(act(512), act(512), act(D_MODEL), act(D_MODEL), cs, cs,
                   jax.ShapeDtypeStruct((NC, 2, LANES), F32), jax.ShapeDtypeStruct((1, D_MODEL), F32)),
        grid=(NC,),
        in_specs=[rowblk(D_MODEL), qk(C_QM // 512), qk(C_KM // 512), wide(C_VM // D_MODEL), wide(C_OM // D_MODEL),
                  rowblk(D_MODEL), col, pl.BlockSpec((8, MLC), lambda c: (0, idx(c))), col, col, col, col,
                  pl.BlockSpec((1, 1, LANES), lambda c: (idx(c), 0, 0)),
                  pl.BlockSpec((1, ML_HEADS, ML_DV, ML_DQK), lambda c: (idx(c), 0, 0, 0)),
                  pl.BlockSpec((1, ML_HEADS, 1, ML_DQK), lambda c: (idx(c), 0, 0, 0)),
                  pl.BlockSpec((1, D_MODEL), lambda c: (0, 0))],
        out_specs=(rowblk(512), rowblk(512), rowblk(D_MODEL), rowblk(D_MODEL), col, col,
                   pl.BlockSpec((1, 2, LANES), lambda c: (idx(c), 0, 0)), pl.BlockSpec((1, D_MODEL), lambda c: (0, 0))),
        scratch_shapes=[pltpu.VMEM((ML_HEADS, ML_DV, ML_DQK), F32), pltpu.VMEM((ML_HEADS, 1, ML_DQK), F32),
                        pltpu.VMEM((1, LANES), F32)],
        compiler_params=_cparams(("arbitrary",)),
    )(dha, pm, pm, pm, pm, hp_all, den_all, a_row, A, wi, em, wk, dec, cst, nst, w_hn)


_FOX_SCALE = FOX_DH ** -0.5
_NEG = -1e30


def _fox_fwd(pm, fcol, frow, name):
    S = pm.shape[0]
    T = FOX_T
    nb = S // T

    def body(q_ref, k_ref, v_ref, fc_ref, fr_ref, o_ref, lse_ref):
        i = pl.program_id(1)
        qb = q_ref[...].astype(BF16)
        fq = fc_ref[0]

        def step(j, carry, masked):
            m, l, acc = carry
            off = pl.multiple_of(j * T, T)
            kb = k_ref[pl.ds(off, T), :].astype(BF16)
            vb = v_ref[pl.ds(off, T), :].astype(BF16)
            s = _dot_nt(qb, kb) * _FOX_SCALE + fq - fr_ref[0, j]
            if masked:
                s = jnp.where(_tri_mask(T, True), s, _NEG)
            m_new = jnp.maximum(m, jnp.max(s, axis=1, keepdims=True))
            alpha = jnp.exp(m - m_new)
            p = jnp.exp(s - m_new)
            l = alpha * l + jnp.sum(p, axis=1, keepdims=True)
            acc = alpha * acc + _dot_nn(p.astype(BF16), vb)
            return m_new, l, acc

        init = (jnp.full((T, 1), _NEG, F32), jnp.zeros((T, 1), F32), jnp.zeros((T, FOX_DH), F32))
        carry = lax.fori_loop(0, i, lambda j, c: step(j, c, False), init)
        m, l, acc = step(i, carry, True)
        o_ref[...] = (acc / l).astype(o_ref.dtype)
        lse_ref[0] = m + jnp.log(l)

    head = lambda base: pl.BlockSpec((S, FOX_DH), lambda h, i: (0, base // FOX_DH + h))
    return pl.pallas_call(
        body, name=name,
        out_shape=(jax.ShapeDtypeStruct((S, D_MODEL), BF16), jax.ShapeDtypeStruct((FOX_HEADS, S, 1), F32)),
        grid=(FOX_HEADS, nb),
        in_specs=[pl.BlockSpec((T, FOX_DH), lambda h, i: (i, C_QF // FOX_DH + h)), head(C_KF), head(C_VF),
                  pl.BlockSpec((1, T, 1), lambda h, i: (h, i, 0)),
                  pl.BlockSpec((1, nb, 1, T), lambda h, i: (h, 0, 0, 0))],
        out_specs=(pl.BlockSpec((T, FOX_DH), lambda h, i: (i, h)), pl.BlockSpec((1, T, 1), lambda h, i: (h, i, 0))),
        compiler_params=_cparams(("parallel", "arbitrary")),
    )(pm, pm, pm, fcol, frow)


def _fox_bwd(dhb, hb, pm, lse, fcol, frow, name):
    S = pm.shape[0]
    T = FOX_T
    nb = S // T

    def body(q_ref, k_ref, v_ref, do_ref, o_ref, lse_ref, fc_ref, fr_ref,
             dq_ref, dk_ref, dv_ref, dF_ref, dFq_ref, dq_acc, delta, dk_acc, dv_acc, cs_acc):
        j = pl.program_id(1)

        @pl.when(j == 0)
        def _():
            dq_acc[...] = jnp.zeros_like(dq_acc)
            dFq_ref[...] = jnp.zeros_like(dFq_ref)

            def fill(b, _):
                off = pl.multiple_of(b * T, T)
                delta[pl.ds(off, T), :] = jnp.sum(do_ref[pl.ds(off, T), :] * o_ref[pl.ds(off, T), :].astype(F32),
                                                  axis=1, keepdims=True)
                return 0

            lax.fori_loop(0, nb, fill, 0)

        kb = k_ref[...].astype(BF16)
        vb = v_ref[...].astype(BF16)
        fk = fr_ref[0, 0]
        dk_acc[...] = jnp.zeros_like(dk_acc)
        dv_acc[...] = jnp.zeros_like(dv_acc)
        cs_acc[...] = jnp.zeros_like(cs_acc)

        def step(i, masked):
            off = pl.multiple_of(i * T, T)
            qb = q_ref[pl.ds(off, T), :].astype(BF16)
            dob = do_ref[pl.ds(off, T), :].astype(BF16)
            s = _dot_nt(qb, kb) * _FOX_SCALE + fc_ref[0, pl.ds(off, T), :] - fk
            if masked:
                s = jnp.where(_tri_mask(T, True), s, _NEG)
            p = jnp.exp(s - lse_ref[0, pl.ds(off, T), :])
            dv_acc[...] += _dot_tn(p.astype(BF16), dob)
            ds = p * (_dot_nt(dob, vb) - delta[pl.ds(off, T), :])
            dsb = ds.astype(BF16)
            dk_acc[...] += _dot_tn(dsb, qb) * _FOX_SCALE
            dq_acc[pl.ds(off, T), :] += _dot_nn(dsb, kb) * _FOX_SCALE
            cs_acc[...] += jnp.sum(ds, axis=0, keepdims=True)
            dFq_ref[0, pl.ds(off, T), :] += jnp.sum(ds, axis=1, keepdims=True)

        step(j, True)

        def rest(i, _):
            step(i, False)
            return 0

        lax.fori_loop(j + 1, nb, rest, 0)
        dk_ref[...] = dk_acc[...].astype(dk_ref.dtype)
        dv_ref[...] = dv_acc[...].astype(dv_ref.dtype)
        dF_ref[0, 0] = -cs_acc[...]

        @pl.when(j == nb - 1)
        def _():
            dq_ref[...] = dq_acc[...].astype(dq_ref.dtype)

    head = lambda base: pl.BlockSpec((S, FOX_DH), lambda h, j: (0, base // FOX_DH + h))
    blk = lambda base: pl.BlockSpec((T, FOX_DH), lambda h, j: (j, base // FOX_DH + h))
    whole = pl.BlockSpec((1, S, 1), lambda h, j: (h, 0, 0))
    act = jax.ShapeDtypeStruct((S, D_MODEL), BF16)
    return pl.pallas_call(
        body, name=name,
        out_shape=(act, act, act, jax.ShapeDtypeStruct((FOX_HEADS, nb, 1, T), F32),
                   jax.ShapeDtypeStruct((FOX_HEADS, S, 1), F32)),
        grid=(FOX_HEADS, nb),
        in_specs=[head(C_QF), blk(C_KF), blk(C_VF), head(0), head(0), whole, whole,
                  pl.BlockSpec((1, 1, 1, T), lambda h, j: (h, j, 0, 0))],
        out_specs=(head(0), blk(0), blk(0), pl.BlockSpec((1, 1, 1, T), lambda h, j: (h, j, 0, 0)), whole),
        scratch_shapes=[pltpu.VMEM((S, FOX_DH), F32), pltpu.VMEM((S, 1), F32), pltpu.VMEM((T, FOX_DH), F32),
                        pltpu.VMEM((T, FOX_DH), F32), pltpu.VMEM((1, T), F32)],
        compiler_params=_cparams(("parallel", "arbitrary")),
    )(pm, pm, pm, dhb, hb, lse, fcol, frow)


def _pad_lanes(v):
    return jnp.pad(v, ((0, 0), (0, LANES - v.shape[1])))


def _local_step(x, target, wmain, wsmall_t, wa, wb, wout, wup, wdown, p):
    S = x.shape[0]
    nb = S // FOX_T
    bi, bf, bff = _pad_lanes(p["b_ml_i"]), _pad_lanes(p["b_ml_f"]), _pad_lanes(p["b_fox_f"])

    h0 = _rmsnorm_fwd(x, p["norm_mix_pre"], "norm_mix_pre")
    pm = _mm(h0, wmain, "nn", F32, "proj_main")
    ps = _mm(h0, wsmall_t, "nt", F32, "proj_gates")
    a, A, wi, em, wk, dec, Fc = _gates_fwd(ps, bi, bf, bff, "gates_fwd")
    a_row = a[:, :8].T
    ha, hp, den, cst, nst = _mlstm_fwd(pm, a_row, A, wi, em, wk, dec, p["ml_head_norm"], "mlstm_fwd")
    ft = Fc[:, :FOX_HEADS].T
    fcol = ft.reshape(FOX_HEADS, S, 1)
    frow = ft.reshape(FOX_HEADS, nb, 1, FOX_T)
    hb, lse = _fox_fwd(pm, fcol, frow, "fox_fwd")
    ya = _mm(ha, wa, "nn", F32, "branch_a")
    yb = _mm(hb, wb, "nn", F32, "branch_b")
    merged = _merge_fwd(ya, yb, pm, p["b_gate_a"], p["b_gate_b"], "merge_fwd")
    z = _mm(merged, wout, "nn", F32, "out_proj")
    x1 = _resid_norm_fwd(x, z, p["norm_mix_post"], "resid_mix")
    h2 = _rmsnorm_fwd(x1, p["norm_ffn_pre"], "norm_ffn_pre")
    up = _mm(h2, wup, "nn", F32, "ffn_up")
    act = _conv_act_fwd(up, p["conv_w"], p["conv_b"], "conv_act_fwd")
    d = _mm(act, wdown, "nn", F32, "ffn_down")
    loss_row, dy, dd, g_norm_ffn_post = _loss_head(x1, d, p["norm_ffn_post"], target, "loss_head")
    dact = _mm(dd, wdown, "nt", F32, "d_act")
    g_wdown = _mm(act, dd, "tn", F32, "dw_down")
    dua, dug, dcwa, dcwg, dcba, dcbg = _conv_act_bwd_du(up, dact, p["conv_w"], p["conv_b"], "conv_act_bwd")
    g_conv_w = jnp.concatenate([dcwa, dcwg], axis=1)
    g_conv_b = jnp.concatenate([dcba, dcbg], axis=1)
    dup = jnp.concatenate([_conv_bwd_dup(dua, p["conv_w"], 0, "conv_bwd_a"),
                           _conv_bwd_dup(dug, p["conv_w"], (D_FF // _tile(D_FF, 256)), "conv_bwd_g")], axis=1)
    dh2 = _mm(dup, wup, "nt", F32, "d_h2")
    g_wup = _mm(h2, dup, "tn", F32, "dw_up")
    dx1, g_norm_ffn_pre = _rmsnorm_bwd([dh2], x1, p["norm_ffn_pre"], dy, F32, "norm_ffn_pre_bwd")
    dz, g_norm_mix_post = _rmsnorm_bwd([dx1], z, p["norm_mix_post"], None, BF16, "norm_mix_post_bwd")
    dmerged = _mm(dz, wout, "nt", F32, "d_merged")
    g_wout = _mm(merged, dz, "tn", F32, "dw_out")
    dya, dyb, dga, dgb, g_b_gate_a, g_b_gate_b = _merge_bwd(dmerged, ya, yb, pm, p["b_gate_a"], p["b_gate_b"], "merge_bwd")
    dha = _mm(dya, wa, "nt", F32, "d_ha")
    g_wa = _mm(ha, dya, "tn", F32, "dw_a")
    dhb = _mm(dyb, wb, "nt", F32, "d_hb")
    g_wb = _mm(hb, dyb, "tn", F32, "dw_b")
    dqm, dkm, dvm, dom, rk, kc, tch, g_ml_head_norm = _mlstm_bwd(
        dha, pm, hp, den, a_row, A, wi, em, wk, dec, cst, nst, p["ml_head_norm"], "mlstm_bwd")
    dqf, dkf, dvf, dFk, dFq = _fox_bwd(dhb, hb, pm, lse, fcol, frow, "fox_bwd")
    dF = jnp.pad((dFk.reshape(FOX_HEADS, S) + dFq.reshape(FOX_HEADS, S)).T, ((0, 0), (0, LANES - FOX_HEADS)))
    dps, dbias = _gates_bwd(ps, bi, bf, bff, rk, kc, tch, dF, "gates_bwd")
    dpm = jnp.concatenate([dqm, dkm, dvm, dom, dqf, dkf, dvf, dga, dgb], axis=1)
    dh0 = _mm(dpm, wmain, "nt", F32, "d_h0_main")
    dh0s = _mm(dps, wsmall_t, "nn", F32, "d_h0_gates")
    g_wmain = _mm(h0, dpm, "tn", F32, "dw_main")
    g_wsmall_t = _mm(dps, h0, "tn", F32, "dw_gates")
    grad_x, g_norm_mix_pre = _rmsnorm_bwd([dh0, dh0s], x, p["norm_mix_pre"], dx1, F32, "norm_mix_pre_bwd")

    big = dict(wmain=g_wmain, wsmall_t=g_wsmall_t, w_branch_a=g_wa, w_branch_b=g_wb, w_out=g_wout, w_up=g_wup, w_down=g_wdown)
    small = dict(norm_mix_pre=g_norm_mix_pre, ml_head_norm=g_ml_head_norm, b_gate_a=g_b_gate_a, b_gate_b=g_b_gate_b,
                 norm_mix_post=g_norm_mix_post, norm_ffn_pre=g_norm_ffn_pre, norm_ffn_post=g_norm_ffn_post,
                 conv_b=g_conv_b, b_ml_i=dbias[:, 0:ML_HEADS], b_ml_f=dbias[:, LANES:LANES + ML_HEADS],
                 b_fox_f=dbias[:, 2 * LANES:2 * LANES + FOX_HEADS], conv_w=g_conv_w)
    return loss_row, grad_x, big, small


def _row_tile(r, target=256):
    best = None
    for t in range(8, min(r, target) + 1, 8):
        if r % t == 0:
            best = t
    return best if best is not None else r


def _adamw(w, g, m, v, name):
    _, R, C = w.shape
    tr = _row_tile(R)

    def body(w_ref, g_ref, m_ref, v_ref, d_ref, mo_ref, vo_ref):
        gv = g_ref[...]
        mn = ADAM_B1 * m_ref[0] + (1.0 - ADAM_B1) * gv
        vn = ADAM_B2 * v_ref[0] + (1.0 - ADAM_B2) * (gv * gv)
        m_hat = mn / (1.0 - ADAM_B1 ** ADAM_STEP)
        v_hat = vn / (1.0 - ADAM_B2 ** ADAM_STEP)
        d_ref[0] = -ADAM_LR * (m_hat / (jnp.sqrt(v_hat) + ADAM_EPS) + ADAM_WD * w_ref[0])
        mo_ref[0] = mn
        vo_ref[0] = vn

    blk = pl.BlockSpec((1, tr, C), lambda i: (0, i, 0))
    o = jax.ShapeDtypeStruct((1, R, C), F32)
    return pl.pallas_call(
        body, name=name, out_shape=(o, o, o), grid=(R // tr,),
        in_specs=[blk, pl.BlockSpec((tr, C), lambda i: (i, 0)), blk, blk], out_specs=(blk,) * 3,
        compiler_params=_cparams(("parallel",)),
    )(w, g, m, v)


ANY = pl.BlockSpec(memory_space=pl.ANY)


def _place():
    x, y, c = lax.axis_index("x"), lax.axis_index("y"), lax.axis_index("c")
    chips = [(1 - x, y), (x, 1 - y), (1 - x, 1 - y)]
    return x, y, c, chips


def _block(ref, kind, k, rows=None):
    if kind == "rows":
        return ref.at[k] if rows is None else ref.at[k, pl.ds(*rows), :]
    cb = ref.shape[1] // 4
    return ref.at[:, pl.ds(k * cb, cb)] if rows is None else ref.at[pl.ds(*rows), pl.ds(k * cb, cb)]


def _gathered_shape(s, kind):
    return (4,) + s.shape if kind == "rows" else (s.shape[0], 4 * s.shape[1])


def _gather_weights(shards, kinds, smalls):
    n, ns = len(shards), len(smalls)

    def body(*refs):
        ins, sm_in = refs[:n], refs[n:n + ns]
        outs, sm_out = refs[n + ns:2 * n + ns], refs[2 * n + ns:2 * (n + ns)]
        send_sems, recv_sems, sm_send, sm_recv, local_sems = refs[2 * (n + ns):]
        x, y, c, chips = _place()
        sibling = (x, y, 1 - c)
        kme = 2 * x + y

        def half(a, k, hc):
            h = ins[a].shape[0] // 2
            return _block(outs[a], kinds[a], k, (hc * h, h))

        def remote(a, slot, src, dst, to):
            return pltpu.make_async_remote_copy(src_ref=src, dst_ref=dst, send_sem=send_sems.at[a * 7 + slot],
                                                recv_sem=recv_sems.at[a * 7 + slot], device_id=to, device_id_type=MESH)

        def sm_copy(b, j, k, to):
            return pltpu.make_async_remote_copy(src_ref=sm_in[b], dst_ref=sm_out[b].at[k], send_sem=sm_send.at[3 * b + j],
                                                recv_sem=sm_recv.at[3 * b + j], device_id=to, device_id_type=MESH)

        local = [pltpu.make_async_copy(sm_in[b], sm_out[b].at[kme], local_sems.at[b]) for b in range(ns)]
        for cp in local:
            cp.start()
        sends = [remote(a, 6, ins[a], _block(outs[a], kinds[a], kme), sibling) for a in range(n)]
        for a in range(n):
            h = ins[a].shape[0] // 2
            for j, chip in enumerate(chips):
                sends.append(remote(a, j, ins[a].at[pl.ds(c * h, h), :], half(a, kme, c), (*chip, c)))
        for b in range(ns):
            for j, chip in enumerate(chips):
                sends.append(sm_copy(b, j, kme, (*chip, c)))
        for cp in sends:
            cp.start()
        for a in range(n):
            for j, chip in enumerate(chips):
                kj = 2 * chip[0] + chip[1]
                remote(a, j, half(a, kj, c), half(a, kj, c), (*chip, c)).wait_recv()
                fwd = remote(a, 3 + j, half(a, kj, c), half(a, kj, c), sibling)
                fwd.start()
                sends.append(fwd)
        for a in range(n):
            for j, chip in enumerate(chips):
                kj = 2 * chip[0] + chip[1]
                remote(a, 3 + j, half(a, kj, 1 - c), half(a, kj, 1 - c), sibling).wait_recv()
        for b in range(ns):
            for j, chip in enumerate(chips):
                sm_copy(b, j, 2 * chip[0] + chip[1], (*chip, c)).wait_recv()
        for a in range(n):
            remote(a, 6, ins[a], _block(outs[a], kinds[a], kme), sibling).wait_recv()
        for cp in sends:
            cp.wait_send()
        for cp in local:
            cp.wait()

    outs = pl.pallas_call(
        body, name="gather_weights",
        out_shape=tuple([jax.ShapeDtypeStruct(_gathered_shape(s, k), s.dtype) for s, k in zip(shards, kinds)]
                        + [jax.ShapeDtypeStruct((4,) + s.shape, s.dtype) for s in smalls]),
        in_specs=[ANY] * (n + ns), out_specs=tuple([ANY] * (n + ns)),
        scratch_shapes=[pltpu.SemaphoreType.DMA((7 * n,)), pltpu.SemaphoreType.DMA((7 * n,)),
                        pltpu.SemaphoreType.DMA((3 * ns,)), pltpu.SemaphoreType.DMA((3 * ns,)),
                        pltpu.SemaphoreType.DMA((ns,))],
    )(*shards, *smalls)
    return outs[:n], outs[n:]


def _exchange_sibling_halves(gs, kinds):
    n = len(gs)
    hshape = lambda g, kind: (4, g.shape[1] // 2, g.shape[2]) if kind == "rows" else (g.shape[0] // 2, g.shape[1])

    def body(*refs):
        ins, outs, send_sems, recv_sems = refs[:n], refs[n:2 * n], refs[2 * n], refs[2 * n + 1]
        x, y, c, _ = _place()
        cps = []
        for a in range(n):
            h = outs[a].shape[-2]
            src = ins[a].at[:, pl.ds((1 - c) * h, h), :] if kinds[a] == "rows" else ins[a].at[pl.ds((1 - c) * h, h), :]
            cps.append(pltpu.make_async_remote_copy(
                src_ref=src, dst_ref=outs[a], send_sem=send_sems.at[a],
                recv_sem=recv_sems.at[a], device_id=(x, y, 1 - c), device_id_type=MESH))
        for cp in cps:
            cp.start()
        for cp in cps:
            cp.wait()

    return pl.pallas_call(
        body, name="grads_to_sibling",
        out_shape=tuple(jax.ShapeDtypeStruct(hshape(g, k), g.dtype) for g, k in zip(gs, kinds)),
        in_specs=[ANY] * n, out_specs=tuple([ANY] * n),
        scratch_shapes=[pltpu.SemaphoreType.DMA((n,)), pltpu.SemaphoreType.DMA((n,))],
    )(*gs)


def _add_halves(g, r1, cvec, kind, name):
    def body(c_ref, g_ref, r_ref, o_ref):
        o_ref[...] = (g_ref[...] + r_ref[...]).astype(o_ref.dtype)

    if kind == "rows":
        _, h, C = r1.shape
        tr = _row_tile(h)
        nt = h // tr
        grid = (4, nt)
        g_spec = pl.BlockSpec((1, tr, C), lambda k, i, c_ref: (k, c_ref[0] * nt + i, 0))
        r_spec = pl.BlockSpec((1, tr, C), lambda k, i, c_ref: (k, i, 0))
    else:
        h, C4 = r1.shape
        tr, tc = _row_tile(h), C4 // 4
        nt = h // tr
        grid = (nt, 4)
        g_spec = pl.BlockSpec((tr, tc), lambda i, k, c_ref: (c_ref[0] * nt + i, k))
        r_spec = pl.BlockSpec((tr, tc), lambda i, k, c_ref: (i, k))
    return pl.pallas_call(
        body, name=name, out_shape=jax.ShapeDtypeStruct(r1.shape, BF16),
        grid_spec=pltpu.PrefetchScalarGridSpec(num_scalar_prefetch=1, grid=grid, in_specs=[g_spec, r_spec],
                                               out_specs=r_spec),
        compiler_params=_cparams(("parallel", "parallel")),
    )(cvec, g, r1)


def _exchange_chips(ss, kinds):
    n = len(ss)
    bshape = lambda s, kind: s.shape[1:] if kind == "rows" else (s.shape[0], s.shape[1] // 4)

    def body(*refs):
        ins, outs, send_sems, recv_sems = refs[:n], refs[n:2 * n], refs[2 * n], refs[2 * n + 1]
        x, y, c, chips = _place()
        cps = []
        for a in range(n):
            for j, chip in enumerate(chips):
                cps.append(pltpu.make_async_remote_copy(
                    src_ref=_block(ins[a], kinds[a], 2 * chip[0] + chip[1]), dst_ref=outs[a].at[j],
                    send_sem=send_sems.at[3 * a + j], recv_sem=recv_sems.at[3 * a + j], device_id=(*chip, c),
                    device_id_type=MESH))
        for cp in cps:
            cp.start()
        for cp in cps:
            cp.wait()

    return pl.pallas_call(
        body, name="grads_to_chips",
        out_shape=tuple(jax.ShapeDtypeStruct((3,) + bshape(s, k), s.dtype) for s, k in zip(ss, kinds)),
        in_specs=[ANY] * n, out_specs=tuple([ANY] * n),
        scratch_shapes=[pltpu.SemaphoreType.DMA((3 * n,)), pltpu.SemaphoreType.DMA((3 * n,))],
    )(*ss)


def _add_chips(s1, r2, kcvec, kind, name):
    _, h, C = r2.shape
    tr = _row_tile(h)
    nt = h // tr

    def body(kc_ref, s_ref, r0_ref, r1_ref, r2_ref, o_ref):
        s = s_ref[0] if kind == "rows" else s_ref[...]
        o_ref[...] = ((s.astype(F32) + r0_ref[0].astype(F32)) + r1_ref[0].astype(F32)) + r2_ref[0].astype(F32)

    peer = lambda j: pl.BlockSpec((1, tr, C), lambda i, kc_ref: (j, i, 0))
    if kind == "rows":
        s_spec = pl.BlockSpec((1, tr, C), lambda i, kc_ref: (kc_ref[0], i, 0))
    else:
        s_spec = pl.BlockSpec((tr, C), lambda i, kc_ref: (i, kc_ref[0]))
    return pl.pallas_call(
        body, name=name, out_shape=jax.ShapeDtypeStruct((2 * h, C), F32),
        grid_spec=pltpu.PrefetchScalarGridSpec(
            num_scalar_prefetch=1, grid=(nt,),
            in_specs=[s_spec, peer(0), peer(1), peer(2)],
            out_specs=pl.BlockSpec((tr, C), lambda i, kc_ref: (kc_ref[1] * nt + i, 0))),
        compiler_params=_cparams(("parallel",)),
    )(kcvec, s1, r2, r2, r2)


def _join_sibling_halves(bufs):
    n = len(bufs)

    def body(*refs):
        ins, outs, send_sems, recv_sems = refs[:n], refs[n:2 * n], refs[2 * n], refs[2 * n + 1]
        x, y, c, _ = _place()
        cps = []
        for a in range(n):
            h = ins[a].shape[0] // 2
            cps.append(pltpu.make_async_remote_copy(
                src_ref=ins[a].at[pl.ds(c * h, h), :], dst_ref=outs[a].at[pl.ds(c * h, h), :], send_sem=send_sems.at[a],
                recv_sem=recv_sems.at[a], device_id=(x, y, 1 - c), device_id_type=MESH))
        for cp in cps:
            cp.start()
        for a in range(n):
            h = ins[a].shape[0] // 2
            theirs = outs[a].at[pl.ds((1 - c) * h, h), :]
            pltpu.make_async_remote_copy(src_ref=theirs, dst_ref=theirs, send_sem=send_sems.at[a],
                                         recv_sem=recv_sems.at[a], device_id=(x, y, 1 - c), device_id_type=MESH).wait_recv()
        for cp in cps:
            cp.wait_send()

    return pl.pallas_call(
        body, name="grads_join",
        out_shape=tuple(jax.ShapeDtypeStruct(b.shape, b.dtype) for b in bufs),
        in_specs=[ANY] * n, out_specs=tuple([ANY] * n), input_output_aliases={a: a for a in range(n)},
        scratch_shapes=[pltpu.SemaphoreType.DMA((n,)), pltpu.SemaphoreType.DMA((n,))],
    )(*bufs)


N_DEV = 8


def _allreduce_small(pack):
    P = pack.shape[0]

    def body(p_ref, o_ref, gath, send_sems, recv_sems):
        x, y, c, _ = _place()
        me = 4 * x + 2 * y + c
        cps = []
        for mask in range(1, N_DEV):
            px = 1 - x if mask & 4 else x
            py = 1 - y if mask & 2 else y
            pc = 1 - c if mask & 1 else c
            cps.append((pltpu.make_async_remote_copy(
                src_ref=p_ref, dst_ref=gath.at[me], send_sem=send_sems.at[mask - 1], recv_sem=recv_sems.at[mask - 1],
                device_id=(px, py, pc), device_id_type=MESH), 4 * px + 2 * py + pc, mask))
        for cp, _, _ in cps:
            cp.start()
        gath[me] = p_ref[...]
        for _, peer, mask in cps:
            pltpu.make_async_remote_copy(
                src_ref=p_ref, dst_ref=gath.at[peer], send_sem=send_sems.at[mask - 1], recv_sem=recv_sems.at[mask - 1],
                device_id=(x, y, c), device_id_type=MESH).wait_recv()
        for cp, _, _ in cps:
            cp.wait_send()
        acc = gath[0]
        for i in range(1, N_DEV):
            acc = acc + gath[i]
        o_ref[...] = acc

    return pl.pallas_call(
        body, name="allreduce_small", out_shape=jax.ShapeDtypeStruct((P, LANES), F32),
        in_specs=[pl.BlockSpec(memory_space=pltpu.VMEM)], out_specs=pl.BlockSpec(memory_space=pltpu.VMEM),
        scratch_shapes=[pltpu.VMEM((N_DEV, P, LANES), F32), pltpu.SemaphoreType.DMA((N_DEV - 1,)),
                        pltpu.SemaphoreType.DMA((N_DEV - 1,))],
    )(pack)


def _pack_rows(arrs):
    rows = []
    for a in arrs:
        f = a.reshape(-1)
        f = jnp.pad(f, (0, (-f.shape[0]) % LANES))
        rows.append(f.reshape(-1, LANES))
    return jnp.concatenate(rows, axis=0)


def _unpack_rows(pack, shapes):
    out, r = [], 0
    for s in shapes:
        n = math.prod(s)
        nr = -(-n // LANES)
        out.append(pack[r:r + nr].reshape(-1)[:n].reshape(s))
        r += nr
    return out


_SMALL = ["norm_mix_pre", "ml_head_norm", "b_gate_a", "b_gate_b", "norm_mix_post", "norm_ffn_pre", "norm_ffn_post",
          "conv_b", "b_ml_i", "b_ml_f", "b_fox_f"]
_BIG = ["w_in", "w_branch_a", "w_branch_b", "w_out", "w_up", "w_down"]
_WEIGHTS = ['norm_mix_pre', 'w_in', 'b_ml_i', 'b_ml_f', 'ml_head_norm', 'b_fox_f', 'b_gate_a', 'b_gate_b', 'w_branch_a',
            'w_branch_b', 'w_out', 'norm_mix_post', 'norm_ffn_pre', 'w_up', 'conv_w', 'conv_b', 'w_down', 'norm_ffn_post']


_KINDS = ["cols", "rows", "rows", "rows", "cols", "rows"]


def kernel(x, norm_mix_pre, w_in, b_ml_i, b_ml_f, ml_head_norm, b_fox_f, b_gate_a, b_gate_b, w_branch_a, w_branch_b, w_out, norm_mix_post, norm_ffn_pre, w_up, conv_w, conv_b, w_down, norm_ffn_post, loss_target, m_norm_mix_pre, m_w_in, m_b_ml_i, m_b_ml_f, m_ml_head_norm, m_b_fox_f, m_b_gate_a, m_b_gate_b, m_w_branch_a, m_w_branch_b, m_w_out, m_norm_mix_post, m_norm_ffn_pre, m_w_up, m_conv_w, m_conv_b, m_w_down, m_norm_ffn_post, v_norm_mix_pre, v_w_in, v_b_ml_i, v_b_ml_f, v_ml_head_norm, v_b_fox_f, v_b_gate_a, v_b_gate_b, v_w_branch_a, v_w_branch_b, v_w_out, v_norm_mix_post, v_norm_ffn_pre, v_w_up, v_conv_w, v_conv_b, v_w_down, v_norm_ffn_post):
    args = dict(locals())
    w = {n: args[n] for n in _WEIGHTS}
    mom = {n: args["m_" + n] for n in _WEIGHTS}
    var = {n: args["v_" + n] for n in _WEIGHTS}
    cx, cy, cc = lax.axis_index("x"), lax.axis_index("y"), lax.axis_index("c")
    kme = 2 * cx + cy
    cvec = jnp.reshape(cc, (1,)).astype(jnp.int32)
    kcvec = jnp.stack([kme, cc]).astype(jnp.int32)
    odd = kme % 2

    w_in_main = lax.dynamic_slice_in_dim(w_in[0], 4 * odd, 2048, axis=1).astype(BF16)
    w_in_gates = lax.dynamic_slice_in_dim(w_in[0], 2048 * (1 - odd), 4, axis=1).T.astype(BF16)
    shards = [w_in_main] + [w[n][0].astype(BF16) for n in _BIG[1:]]
    (wmain, g_a, g_b, g_out, wup, g_down), (g_cw, g_gates) = _gather_weights(
        shards, _KINDS, [w["conv_w"][0], w_in_gates])
    gate_rows = g_gates.reshape(16, D_MODEL)
    wsmall_t = jnp.zeros((N_SMALL, D_MODEL), BF16)
    for blk, (lo, hi) in enumerate(((0, 4), (4, 8), (8, 16))):
        wsmall_t = wsmall_t.at[blk * LANES:blk * LANES + hi - lo].set(gate_rows[lo:hi])
    full = lambda g: g.reshape(-1, g.shape[2])
    p = {n: w[n] for n in _SMALL}
    p["conv_w"] = jnp.transpose(g_cw, (1, 0, 2)).reshape(3, -1)

    loss_row, grad_x, big, small = _local_step(x[0], loss_target[0], wmain, wsmall_t, full(g_a), full(g_b), full(g_out),
                                               wup, full(g_down), p)

    whole = [big["wmain"], big["w_branch_a"].reshape(4, -1, D_MODEL), big["w_branch_b"].reshape(4, -1, D_MODEL),
             big["w_out"].reshape(4, -1, D_MODEL), big["w_up"], big["w_down"].reshape(4, -1, D_MODEL)]
    from_sibling = _exchange_sibling_halves(whole, _KINDS)
    chip_sums = [_add_halves(g, r, cvec, k, "add_sibling_" + n) for g, r, k, n in zip(whole, from_sibling, _KINDS, _BIG)]
    from_chips = _exchange_chips(chip_sums, _KINDS)
    mine = [_add_chips(s, r, kcvec, k, "add_chips_" + n) for s, r, k, n in zip(chip_sums, from_chips, _KINDS, _BIG)]
    grads = dict(zip(_BIG, _join_sibling_halves(mine)))

    gt = big["wsmall_t"]
    small["w_in_gates"] = jnp.concatenate([gt[0:4], gt[LANES:LANES + 4], gt[2 * LANES:2 * LANES + 8]], axis=0)
    small_names = _SMALL + ["conv_w"]
    packed_names = small_names + ["w_in_gates"]
    pack = _pack_rows([small[n] for n in packed_names] + [loss_row])
    pack = jnp.pad(pack, ((0, (-pack.shape[0]) % 8), (0, 0)))
    full_shapes = [small[n].shape if n in ("conv_w", "w_in_gates") else w[n][0].shape for n in packed_names]
    total = _unpack_rows(_allreduce_small(pack), full_shapes + [loss_row.shape])
    for n, t in zip(packed_names, total):
        grads[n] = t
    loss = total[-1][0, 0]
    grads["conv_w"] = lax.dynamic_slice_in_dim(grads["conv_w"], kme * conv_w.shape[2], conv_w.shape[2], axis=1)
    my_gates = lax.dynamic_slice_in_dim(grads.pop("w_in_gates"), 4 * kme, 4, axis=0).T
    g_in = jnp.zeros(w_in.shape[1:], F32)
    g_in = lax.dynamic_update_slice_in_dim(g_in, grads["w_in"], 4 * odd, axis=1)
    grads["w_in"] = lax.dynamic_update_slice_in_dim(g_in, my_gates, 2048 * (1 - odd), axis=1)

    delta, new_m, new_v = {}, {}, {}
    for n in _BIG:
        delta[n], new_m[n], new_v[n] = _adamw(w[n], grads[n], mom[n], var[n], "adamw_" + n)
        grads[n] = grads[n][None]
    packs = [_pack_rows([d[n][0] for n in small_names]) for d in (w, mom, var)]
    pad = ((0, (-packs[0].shape[0]) % 8), (0, 0))
    packs = [jnp.pad(t, pad)[None] for t in packs]
    gp = jnp.pad(_pack_rows([grads[n] for n in small_names]), pad)
    shapes = [w[n][0].shape for n in small_names]
    for dst, res in zip((delta, new_m, new_v), _adamw(packs[0], gp, packs[1], packs[2], "adamw_small")):
        for n, t in zip(small_names, _unpack_rows(res[0], shapes)):
            dst[n] = t[None]
    for n in small_names:
        grads[n] = grads[n][None]

    return (loss, grad_x[None], *[grads[n] for n in _WEIGHTS], *[delta[n] for n in _WEIGHTS],
            *[new_m[n] for n in _WEIGHTS], *[new_v[n] for n in _WEIGHTS])
```

```python
import functools
import math

import jax
import jax.numpy as jnp
from jax import lax
from jax.experimental import pallas as pl
from jax.experimental.pallas import tpu as pltpu

F32 = jnp.float32
BF16 = jnp.bfloat16
MESH = pl.DeviceIdType.MESH

D_MODEL = 1024
ML_HEADS = 4
ML_DQK = 128
ML_DV = 256
FOX_HEADS = 8
FOX_DH = 128
D_FF = 2816
GATE_CAP = 15.0
EPS = 1e-6
ADAM_LR, ADAM_B1, ADAM_B2, ADAM_EPS, ADAM_WD, ADAM_STEP = 0.001, 0.9, 0.999, 1e-08, 0.01, 10

LANES = 128
MLC = 128
FOX_TQ = 256
FOX_TK = 512
ROW_T = 512
VMEM_LIMIT = 56 * 1024 * 1024

C_QM, C_KM, C_VM, C_OM = 0, 512, 1024, 2048
N_ML, N_FOX, N_GATE = 3072, 3072, 2048
N_SMALL = 384


def _cparams(sem=None):
    return pltpu.CompilerParams(dimension_semantics=sem, vmem_limit_bytes=VMEM_LIMIT)


def _tile(n, target):
    if n <= target:
        return n
    best = None
    for t in range(LANES, target + 1, LANES):
        if n % t == 0:
            best = t
    assert best is not None, (n, target)
    return best


def _dot(a, b, dims):
    return lax.dot_general(a, b, (dims, ((), ())), preferred_element_type=F32)


def _dot_nn(a, b):
    return _dot(a, b, ((1,), (0,)))


def _dot_nt(a, b):
    return _dot(a, b, ((1,), (1,)))


def _dot_tn(a, b):
    return _dot(a, b, ((0,), (0,)))


_DOTS = {"nn": _dot_nn, "nt": _dot_nt, "tn": _dot_tn}


def _mm(a, b, mode, out_dtype, name, tm=1024, tn=1408, tk=1024):
    if mode == "nn":
        (M, K), (K2, N) = a.shape, b.shape
    elif mode == "nt":
        (M, K), (N, K2) = a.shape, b.shape
    else:
        (K, M), (K2, N) = a.shape, b.shape
    assert K == K2, (name, a.shape, b.shape)
    tm, tn, tk = _tile(M, tm), _tile(N, tn), _tile(K, tk)
    nk = K // tk
    dot = _DOTS[mode]

    def body(a_ref, b_ref, o_ref, *acc):
        part = dot(a_ref[...], b_ref[...])
        if nk == 1:
            o_ref[...] = part.astype(o_ref.dtype)
        else:
            acc_ref, = acc
            k = pl.program_id(2)

            @pl.when(k == 0)
            def _():
                acc_ref[...] = part

            @pl.when(k > 0)
            def _():
                acc_ref[...] += part

            @pl.when(k == nk - 1)
            def _():
                o_ref[...] = acc_ref[...].astype(o_ref.dtype)

    if mode == "nn":
        a_spec = pl.BlockSpec((tm, tk), lambda i, j, k: (i, k))
        b_spec = pl.BlockSpec((tk, tn), lambda i, j, k: (k, j))
    elif mode == "nt":
        a_spec = pl.BlockSpec((tm, tk), lambda i, j, k: (i, k))
        b_spec = pl.BlockSpec((tn, tk), lambda i, j, k: (j, k))
    else:
        a_spec = pl.BlockSpec((tk, tm), lambda i, j, k: (k, i))
        b_spec = pl.BlockSpec((tk, tn), lambda i, j, k: (k, j))
    return pl.pallas_call(
        body, name=name,
        out_shape=jax.ShapeDtypeStruct((M, N), out_dtype),
        grid=(M // tm, N // tn, nk),
        in_specs=[a_spec, b_spec],
        out_specs=pl.BlockSpec((tm, tn), lambda i, j, k: (i, j)),
        scratch_shapes=[pltpu.VMEM((tm, tn), F32)] if nk > 1 else [],
        compiler_params=_cparams(("parallel", "parallel", "arbitrary")),
    )(a, b)


def _rstd(x):
    return lax.rsqrt(jnp.mean(x * x, axis=-1, keepdims=True) + EPS)


def _rmsnorm_fwd(x, g, name):
    S, D = x.shape
    T = _tile(S, ROW_T)

    def body(x_ref, g_ref, o_ref):
        xv = x_ref[...]
        o_ref[...] = (xv * _rstd(xv) * g_ref[...]).astype(o_ref.dtype)

    return pl.pallas_call(
        body, name=name, out_shape=jax.ShapeDtypeStruct((S, D), BF16), grid=(S // T,),
        in_specs=[pl.BlockSpec((T, D), lambda i: (i, 0)), pl.BlockSpec((1, D), lambda i: (0, 0))],
        out_specs=pl.BlockSpec((T, D), lambda i: (i, 0)),
        compiler_params=_cparams(("parallel",)),
    )(x, g)


def _resid_norm_fwd(x, z, g, name):
    S, D = x.shape
    T = _tile(S, ROW_T)

    def body(x_ref, z_ref, g_ref, o_ref):
        zv = z_ref[...]
        o_ref[...] = x_ref[...] + zv * _rstd(zv) * g_ref[...]

    row = pl.BlockSpec((T, D), lambda i: (i, 0))
    return pl.pallas_call(
        body, name=name, out_shape=jax.ShapeDtypeStruct((S, D), F32), grid=(S // T,),
        in_specs=[row, row, pl.BlockSpec((1, D), lambda i: (0, 0))],
        out_specs=row, compiler_params=_cparams(("parallel",)),
    )(x, z, g)


def _rmsnorm_bwd_math(dy, xv, g):
    r = _rstd(xv)
    u = dy * g
    dx = r * u - xv * (r * r * r) * jnp.mean(u * xv, axis=-1, keepdims=True)
    return dx, dy * xv * r


def _rmsnorm_bwd(dys, xin, g, resid, out_dtype, name):
    S, D = xin.shape
    T = _tile(S, ROW_T)
    has_resid = resid is not None
    ndy = len(dys)

    def body(*refs):
        dy_refs, (x_ref, g_ref) = refs[:ndy], refs[ndy:ndy + 2]
        dx_ref, dg_ref = refs[-2:]
        dy = dy_refs[0][...]
        for r in dy_refs[1:]:
            dy = dy + r[...]
        dx, dgt = _rmsnorm_bwd_math(dy, x_ref[...], g_ref[...])
        if has_resid:
            dx = dx + refs[ndy + 2][...]
        dx_ref[...] = dx.astype(dx_ref.dtype)

        @pl.when(pl.program_id(0) == 0)
        def _():
            dg_ref[...] = jnp.zeros_like(dg_ref)

        dg_ref[...] += jnp.sum(dgt, axis=0, keepdims=True)

    row = pl.BlockSpec((T, D), lambda i: (i, 0))
    vec = pl.BlockSpec((1, D), lambda i: (0, 0))
    ins = list(dys) + [xin, g] + ([resid] if has_resid else [])
    return pl.pallas_call(
        body, name=name,
        out_shape=(jax.ShapeDtypeStruct((S, D), out_dtype), jax.ShapeDtypeStruct((1, D), F32)),
        grid=(S // T,), in_specs=[row] * ndy + [row, vec] + ([row] if has_resid else []),
        out_specs=(row, vec), compiler_params=_cparams(("arbitrary",)),
    )(*ins)


def _loss_head(x1, d, g, target, name):
    S, D = x1.shape
    T = _tile(S, ROW_T)

    def body(x_ref, d_ref, g_ref, t_ref, loss_ref, dy_ref, dd_ref, dg_ref):
        dv, gv = d_ref[...], g_ref[...]
        y = x_ref[...] + dv * _rstd(dv) * gv
        diff = y - t_ref[...]
        dy = diff * (1.0 / D)
        dy_ref[...] = dy
        dd, dgt = _rmsnorm_bwd_math(dy, dv, gv)
        dd_ref[...] = dd.astype(dd_ref.dtype)

        @pl.when(pl.program_id(0) == 0)
        def _():
            dg_ref[...] = jnp.zeros_like(dg_ref)
            loss_ref[...] = jnp.zeros_like(loss_ref)

        dg_ref[...] += jnp.sum(dgt, axis=0, keepdims=True)
        part = jnp.sum(jnp.sum(diff * diff, axis=1, keepdims=True), axis=0, keepdims=True)
        loss_ref[...] += (0.5 / D) * part

    row = pl.BlockSpec((T, D), lambda i: (i, 0))
    vec = pl.BlockSpec((1, D), lambda i: (0, 0))
    return pl.pallas_call(
        body, name=name,
        out_shape=(jax.ShapeDtypeStruct((1, LANES), F32), jax.ShapeDtypeStruct((S, D), F32),
                   jax.ShapeDtypeStruct((S, D), BF16), jax.ShapeDtypeStruct((1, D), F32)),
        grid=(S // T,), in_specs=[row, row, vec, row],
        out_specs=(pl.BlockSpec((1, LANES), lambda i: (0, 0)), row, row, vec),
        compiler_params=_cparams(("arbitrary",)),
    )(x1, d, g, target)


def _merge_fwd(ya, yb, pm, ba, bb, name):
    S, D = ya.shape
    T = _tile(S, ROW_T)

    def body(ya_ref, yb_ref, ga_ref, gb_ref, ba_ref, bb_ref, o_ref):
        sa = jax.nn.sigmoid(ga_ref[...] + ba_ref[...])
        sb = jax.nn.sigmoid(gb_ref[...] + bb_ref[...])
        o_ref[...] = (sa * ya_ref[...] + sb * yb_ref[...]).astype(o_ref.dtype)

    row = pl.BlockSpec((T, D), lambda i: (i, 0))
    vec = pl.BlockSpec((1, D), lambda i: (0, 0))
    return pl.pallas_call(
        body, name=name, out_shape=jax.ShapeDtypeStruct((S, D), BF16), grid=(S // T,),
        in_specs=[row, row, pl.BlockSpec((T, D), lambda i: (i, 0)),
                  pl.BlockSpec((T, D), lambda i: (i, 1)), vec, vec],
        out_specs=row, compiler_params=_cparams(("parallel",)),
    )(ya, yb, pm, pm, ba, bb)


def _merge_bwd(dmerged, ya, yb, pm, ba, bb, name):
    S, D = ya.shape
    T = _tile(S, ROW_T)

    def body(dm_ref, ya_ref, yb_ref, ga_ref, gb_ref, ba_ref, bb_ref,
             dya_ref, dyb_ref, dga_ref, dgb_ref, dba_ref, dbb_ref):
        dm = dm_ref[...]
        sa = jax.nn.sigmoid(ga_ref[...] + ba_ref[...])
        sb = jax.nn.sigmoid(gb_ref[...] + bb_ref[...])
        dya_ref[...] = (dm * sa).astype(dya_ref.dtype)
        dyb_ref[...] = (dm * sb).astype(dyb_ref.dtype)
        dga = dm * ya_ref[...] * sa * (1.0 - sa)
        dgb = dm * yb_ref[...] * sb * (1.0 - sb)
        dga_ref[...] = dga.astype(dga_ref.dtype)
        dgb_ref[...] = dgb.astype(dgb_ref.dtype)

        @pl.when(pl.program_id(0) == 0)
        def _():
            dba_ref[...] = jnp.zeros_like(dba_ref)
            dbb_ref[...] = jnp.zeros_like(dbb_ref)

        dba_ref[...] += jnp.sum(dga, axis=0, keepdims=True)
        dbb_ref[...] += jnp.sum(dgb, axis=0, keepdims=True)

    row = pl.BlockSpec((T, D), lambda i: (i, 0))
    vec = pl.BlockSpec((1, D), lambda i: (0, 0))
    act = jax.ShapeDtypeStruct((S, D), BF16)
    v1 = jax.ShapeDtypeStruct((1, D), F32)
    return pl.pallas_call(
        body, name=name, out_shape=(act, act, act, act, v1, v1), grid=(S // T,),
        in_specs=[row, row, row, pl.BlockSpec((T, D), lambda i: (i, 0)),
                  pl.BlockSpec((T, D), lambda i: (i, 1)), vec, vec],
        out_specs=(row, row, row, row, vec, vec), compiler_params=_cparams(("arbitrary",)),
    )(dmerged, ya, yb, pm, pm, ba, bb)


_GELU_C = math.sqrt(2.0 / math.pi)


def _gelu(g):
    t = jnp.tanh(_GELU_C * (g + 0.044715 * g * g * g))
    return 0.5 * g * (1.0 + t), t


def _gelu_grad(g, t):
    return 0.5 * (1.0 + t) + 0.5 * g * (1.0 - t * t) * _GELU_C * (1.0 + 3 * 0.044715 * g * g)


def _shift_down(v, halo_ref, first, rows):
    T = v.shape[0]
    keep = jnp.where(first, 0.0, 1.0)
    h7 = halo_ref[7:8, :] * keep
    h6 = halo_ref[6:7, :] * keep
    m1 = jnp.where(rows == 0, h7, pltpu.roll(v, 1, 0))
    m2 = jnp.where(rows == 0, h6, jnp.where(rows == 1, h7, pltpu.roll(v, 2, 0)))
    return m1, m2


def _conv_act_fwd(up, cw, cb, name):
    S, F2 = up.shape
    Fh = F2 // 2
    T = _tile(S, ROW_T)
    tc = _tile(Fh, 256)
    ncol = Fh // tc
    hb = T // 8

    def body(ua_ref, ug_ref, ha_ref, hg_ref, wa_ref, wg_ref, ba_ref, bg_ref, o_ref):
        first = pl.program_id(0) == 0
        rows = lax.broadcasted_iota(jnp.int32, (T, tc), 0)

        def conv(u_ref, h_ref, w_ref, b_ref):
            v = u_ref[...]
            m1, m2 = _shift_down(v, h_ref, first, rows)
            return b_ref[...] + w_ref[0:1, :] * m2 + w_ref[1:2, :] * m1 + w_ref[2:3, :] * v

        a = conv(ua_ref, ha_ref, wa_ref, ba_ref)
        g = conv(ug_ref, hg_ref, wg_ref, bg_ref)
        o_ref[...] = (_gelu(g)[0] * a).astype(o_ref.dtype)

    halo = lambda off: pl.BlockSpec((8, tc), lambda i, j: (jnp.maximum(i * hb - 1, 0), j + off))
    return pl.pallas_call(
        body, name=name, out_shape=jax.ShapeDtypeStruct((S, Fh), BF16), grid=(S // T, ncol),
        in_specs=[pl.BlockSpec((T, tc), lambda i, j: (i, j)), pl.BlockSpec((T, tc), lambda i, j: (i, j + ncol)),
                  halo(0), halo(ncol),
                  pl.BlockSpec((3, tc), lambda i, j: (0, j)), pl.BlockSpec((3, tc), lambda i, j: (0, j + ncol)),
                  pl.BlockSpec((1, tc), lambda i, j: (0, j)), pl.BlockSpec((1, tc), lambda i, j: (0, j + ncol))],
        out_specs=pl.BlockSpec((T, tc), lambda i, j: (i, j)),
        compiler_params=_cparams(("parallel", "parallel")),
    )(up, up, up, up, cw, cw, cb, cb)


def _conv_act_bwd_du(up, dact, cw, cb, name):
    S, F2 = up.shape
    Fh = F2 // 2
    T = _tile(S, ROW_T)
    tc = _tile(Fh, 256)
    ncol = Fh // tc
    hb = T // 8

    def body(ua_ref, ug_ref, ha_ref, hg_ref, wa_ref, wg_ref, ba_ref, bg_ref, da_ref,
             dua_ref, dug_ref, dwa_ref, dwg_ref, dba_ref, dbg_ref):
        first = pl.program_id(1) == 0
        rows = lax.broadcasted_iota(jnp.int32, (T, tc), 0)

        def conv(u_ref, h_ref, w_ref, b_ref):
            v = u_ref[...]
            m1, m2 = _shift_down(v, h_ref, first, rows)
            return b_ref[...] + w_ref[0:1, :] * m2 + w_ref[1:2, :] * m1 + w_ref[2:3, :] * v, (m2, m1, v)

        a, taps_a = conv(ua_ref, ha_ref, wa_ref, ba_ref)
        g, taps_g = conv(ug_ref, hg_ref, wg_ref, bg_ref)
        gel, t = _gelu(g)
        dact_v = da_ref[...]
        dua = dact_v * gel
        dug = dact_v * a * _gelu_grad(g, t)
        dua_ref[...] = dua
        dug_ref[...] = dug

        @pl.when(first)
        def _():
            for r in (dwa_ref, dwg_ref, dba_ref, dbg_ref):
                r[...] = jnp.zeros_like(r)

        for du, taps, dw_ref, db_ref in ((dua, taps_a, dwa_ref, dba_ref), (dug, taps_g, dwg_ref, dbg_ref)):
            db_ref[...] += jnp.sum(du, axis=0, keepdims=True)
            for j in range(3):
                dw_ref[j:j + 1, :] += jnp.sum(du * taps[j], axis=0, keepdims=True)

    halo = lambda off: pl.BlockSpec((8, tc), lambda j, i: (jnp.maximum(i * hb - 1, 0), j + off))
    outs = pl.pallas_call(
        body, name=name,
        out_shape=(jax.ShapeDtypeStruct((S, Fh), F32), jax.ShapeDtypeStruct((S, Fh), F32),
                   jax.ShapeDtypeStruct((3, Fh), F32), jax.ShapeDtypeStruct((3, Fh), F32),
                   jax.ShapeDtypeStruct((1, Fh), F32), jax.ShapeDtypeStruct((1, Fh), F32)),
        grid=(ncol, S // T),
        in_specs=[pl.BlockSpec((T, tc), lambda j, i: (i, j)), pl.BlockSpec((T, tc), lambda j, i: (i, j + ncol)),
                  halo(0), halo(ncol),
                  pl.BlockSpec((3, tc), lambda j, i: (0, j)), pl.BlockSpec((3, tc), lambda j, i: (0, j + ncol)),
                  pl.BlockSpec((1, tc), lambda j, i: (0, j)), pl.BlockSpec((1, tc), lambda j, i: (0, j + ncol)),
                  pl.BlockSpec((T, tc), lambda j, i: (i, j))],
        out_specs=(pl.BlockSpec((T, tc), lambda j, i: (i, j)), pl.BlockSpec((T, tc), lambda j, i: (i, j)),
                   pl.BlockSpec((3, tc), lambda j, i: (0, j)), pl.BlockSpec((3, tc), lambda j, i: (0, j)),
                   pl.BlockSpec((1, tc), lambda j, i: (0, j)), pl.BlockSpec((1, tc), lambda j, i: (0, j))),
        compiler_params=_cparams(("parallel", "arbitrary")),
    )(up, up, up, up, cw, cw, cb, cb, dact)
    return outs


def _conv_bwd_dup(du, cw, col_off, name):
    S, Fh = du.shape
    T = _tile(S, ROW_T)
    tc = _tile(Fh, 256)
    ncol, nrow, hb, nhb = Fh // tc, S // T, T // 8, S // 8

    def body(du_ref, h_ref, w_ref, o_ref):
        keep = jnp.where(pl.program_id(0) == nrow - 1, 0.0, 1.0)
        rows = lax.broadcasted_iota(jnp.int32, (T, tc), 0)
        v = du_ref[...]
        h0 = h_ref[0:1, :] * keep
        h1 = h_ref[1:2, :] * keep
        p1 = jnp.where(rows == T - 1, h0, pltpu.roll(v, T - 1, 0))
        p2 = jnp.where(rows == T - 2, h0, jnp.where(rows == T - 1, h1, pltpu.roll(v, T - 2, 0)))
        o_ref[...] = (w_ref[2:3, :] * v + w_ref[1:2, :] * p1 + w_ref[0:1, :] * p2).astype(o_ref.dtype)

    return pl.pallas_call(
        body, name=name, out_shape=jax.ShapeDtypeStruct((S, Fh), BF16), grid=(nrow, ncol),
        in_specs=[pl.BlockSpec((T, tc), lambda i, j: (i, j)),
                  pl.BlockSpec((8, tc), lambda i, j: (jnp.minimum((i + 1) * hb, nhb - 1), j)),
                  pl.BlockSpec((3, tc), lambda i, j: (0, j + col_off))],
        out_specs=pl.BlockSpec((T, tc), lambda i, j: (i, j)),
        compiler_params=_cparams(("parallel", "parallel")),
    )(du, du, cw)


def _split3(x):
    hi = x.astype(BF16)
    r1 = x - hi.astype(F32)
    mid = r1.astype(BF16)
    lo = (r1 - mid.astype(F32)).astype(BF16)
    return hi, mid, lo


def _tri_dot(tri, x):
    hi, mid, lo = _split3(x)
    return _dot_nn(tri, hi) + _dot_nn(tri, mid) + _dot_nn(tri, lo)


def _log_sigmoid(x):
    return jnp.minimum(x, 0.0) - jnp.log(1.0 + jnp.exp(-jnp.abs(x)))


def _tri_mask(n, lower):
    r = lax.broadcasted_iota(jnp.int32, (n, n), 0)
    c = lax.broadcasted_iota(jnp.int32, (n, n), 1)
    return (r >= c) if lower else (r <= c)


def _gates_fwd(ps, bi, bf, bff, name):
    S = ps.shape[0]
    NC = S // MLC

    def body(ps_ref, bi_ref, bf_ref, bff_ref, a_ref, A_ref, wi_ref, em_ref, wk_ref, dec_ref, F_ref, m_scr, f_scr):
        @pl.when(pl.program_id(0) == 0)
        def _():
            m_scr[...] = jnp.zeros_like(m_scr)
            f_scr[...] = jnp.zeros_like(f_scr)

        rows = lax.broadcasted_iota(jnp.int32, (MLC, LANES), 0)
        ltri = _tri_mask(MLC, True).astype(BF16)
        li = GATE_CAP * jnp.tanh((ps_ref[:, 0:LANES] + bi_ref[...]) / GATE_CAP)
        lf = _log_sigmoid(GATE_CAP * jnp.tanh((ps_ref[:, LANES:2 * LANES] + bf_ref[...]) / GATE_CAP))
        b = _tri_dot(ltri, lf)
        a = li - b
        cm = a
        sh = 1
        while sh < MLC:
            cm = jnp.where(rows >= sh, jnp.maximum(cm, pltpu.roll(cm, sh, 0)), cm)
            sh *= 2
        m0 = m_scr[...]
        A = jnp.maximum(cm, m0)
        a_ref[...] = a
        A_ref[...] = A
        A_last = A_ref[MLC - 1:MLC, :]
        wi_ref[...] = jnp.exp(m0 - A)
        em_ref[...] = jnp.exp(-(b + A))
        wk_ref[...] = jnp.exp(a - A_last)
        dec_ref[0] = jnp.exp(m0 - A_last)
        F_ref[...] = b
        m_scr[...] = F_ref[MLC - 1:MLC, :] + A_last
        lfg = _log_sigmoid(ps_ref[:, 2 * LANES:3 * LANES] + bff_ref[...])
        F_ref[...] = _tri_dot(ltri, lfg) + f_scr[...]
        f_scr[...] = F_ref[MLC - 1:MLC, :]

    col = pl.BlockSpec((MLC, LANES), lambda c: (c, 0))
    vec = pl.BlockSpec((1, LANES), lambda c: (0, 0))
    cs = jax.ShapeDtypeStruct((S, LANES), F32)
    return pl.pallas_call(
        body, name=name,
        out_shape=(cs, cs, cs, cs, cs, jax.ShapeDtypeStruct((NC, 1, LANES), F32), cs),
        grid=(NC,), in_specs=[pl.BlockSpec((MLC, N_SMALL), lambda c: (c, 0)), vec, vec, vec],
        out_specs=(col, col, col, col, col, pl.BlockSpec((1, 1, LANES), lambda c: (c, 0, 0)), col),
        scratch_shapes=[pltpu.VMEM((1, LANES), F32), pltpu.VMEM((1, LANES), F32)],
        compiler_params=_cparams(("arbitrary",)),
    )(ps, bi, bf, bff)


def _gates_bwd(ps, bi, bf, bff, rk, kc, tch, dF, name):
    S = ps.shape[0]
    NC = S // MLC

    def body(ps_ref, bi_ref, bf_ref, bff_ref, rk_ref, kc_ref, t_ref, dF_ref, dps_ref, db_ref, carry):
        @pl.when(pl.program_id(0) == 0)
        def _():
            carry[...] = jnp.zeros_like(carry)
            db_ref[...] = jnp.zeros_like(db_ref)

        lanes = lax.broadcasted_iota(jnp.int32, (MLC, LANES), 1)
        utri = _tri_mask(MLC, False).astype(BF16)
        ti = jnp.tanh((ps_ref[:, 0:LANES] + bi_ref[...]) / GATE_CAP)
        t_end, t_start = t_ref[0, 0:1, :], t_ref[0, 1:2, :]
        rk = rk_ref[...]
        rk = rk - (jnp.sum(rk, axis=0, keepdims=True) - (t_start - t_end)) * (1.0 / MLC)
        dpi = jnp.where(lanes < ML_HEADS, (kc_ref[...] - rk) * (1.0 - ti * ti), 0.0)
        tf = jnp.tanh((ps_ref[:, LANES:2 * LANES] + bf_ref[...]) / GATE_CAP)
        dlf = _tri_dot(utri, rk) + t_end
        dpf = jnp.where(lanes < ML_HEADS, dlf * jax.nn.sigmoid(-GATE_CAP * tf) * (1.0 - tf * tf), 0.0)
        dFv = dF_ref[...]
        dlfg = _tri_dot(utri, dFv) + carry[...]
        carry[...] += jnp.sum(dFv, axis=0, keepdims=True)
        dpff = jnp.where(lanes < FOX_HEADS, dlfg * jax.nn.sigmoid(-(ps_ref[:, 2 * LANES:3 * LANES] + bff_ref[...])), 0.0)
        for n, dp in enumerate((dpi, dpf, dpff)):
            dps_ref[:, n * LANES:(n + 1) * LANES] = dp.astype(dps_ref.dtype)
            db_ref[:, n * LANES:(n + 1) * LANES] += jnp.sum(dp, axis=0, keepdims=True)

    rev = lambda c: (NC - 1 - c, 0)
    col = pl.BlockSpec((MLC, LANES), rev)
    vec = pl.BlockSpec((1, LANES), lambda c: (0, 0))
    wide = pl.BlockSpec((MLC, N_SMALL), rev)
    return pl.pallas_call(
        body, name=name,
        out_shape=(jax.ShapeDtypeStruct((S, N_SMALL), BF16), jax.ShapeDtypeStruct((1, N_SMALL), F32)),
        grid=(NC,),
        in_specs=[wide, vec, vec, vec, col, col, pl.BlockSpec((1, 2, LANES), lambda c: (NC - 1 - c, 0, 0)), col],
        out_specs=(wide, pl.BlockSpec((1, N_SMALL), lambda c: (0, 0))),
        scratch_shapes=[pltpu.VMEM((1, LANES), F32)],
        compiler_params=_cparams(("arbitrary",)),
    )(ps, bi, bf, bff, rk, kc, tch, dF)


_ML_SCALE = ML_DQK ** -0.5


def _ml_specs(rev, NC):
    idx = (lambda c: NC - 1 - c) if rev else (lambda c: c)
    qk = lambda blk: pl.BlockSpec((MLC, ML_HEADS * ML_DQK), lambda c: (idx(c), blk))
    wide = lambda blk: pl.BlockSpec((MLC, D_MODEL), lambda c: (idx(c), blk))
    col = pl.BlockSpec((MLC, LANES), lambda c: (idx(c), 0))
    return idx, qk, wide, col


def _ml_intra(q_ref, k_ref, arow_ref, A_ref, h):
    hs = slice(h * ML_DQK, (h + 1) * ML_DQK)
    qf = q_ref[:, hs] * _ML_SCALE
    kf = k_ref[:, hs]
    qb, kb = qf.astype(BF16), kf.astype(BF16)
    qk = _dot_nt(qb, kb)
    logw = arow_ref[h:h + 1, :] - A_ref[:, h:h + 1]
    W = jnp.exp(jnp.where(_tri_mask(MLC, True), logw, -1e30))
    return qb, kb, qf, kf, qk, W


def _mlstm_fwd(pm, a_row, A, wi, em, wk, dec, w_hn, name):
    S = pm.shape[0]
    NC = S // MLC
    _, qk, wide, col = _ml_specs(False, NC)

    def body(q_ref, k_ref, v_ref, o_ref, arow_ref, A_ref, wi_ref, em_ref, wk_ref, dec_ref, whn_ref,
             ha_ref, hp_ref, den_ref, cst_ref, nst_ref, C_scr, n_scr):
        @pl.when(pl.program_id(0) == 0)
        def _():
            C_scr[...] = jnp.zeros_like(C_scr)
            n_scr[...] = jnp.zeros_like(n_scr)

        lanes = lax.broadcasted_iota(jnp.int32, (MLC, LANES), 1)
        den_tile = jnp.zeros((MLC, LANES), F32)
        for h in range(ML_HEADS):
            vs = slice(h * ML_DV, (h + 1) * ML_DV)
            qb, kb, qf, kf, qk_, W = _ml_intra(q_ref, k_ref, arow_ref, A_ref, h)
            vb = v_ref[:, vs].astype(BF16)
            Cf = C_scr[h]
            Cb = Cf.astype(BF16)
            nrow = n_scr[h]
            cst_ref[0, h] = Cb
            nst_ref[0, h] = nrow
            s = qk_ * W
            wic = wi_ref[:, h:h + 1]
            num = _dot_nn(s.astype(BF16), vb) + wic * _dot_nt(qb, Cb)
            den = jnp.sum(s, axis=1, keepdims=True) + wic * jnp.sum(qf * nrow, axis=1, keepdims=True)
            hp = num / jnp.maximum(jnp.abs(den), em_ref[:, h:h + 1])
            hp_ref[:, vs] = hp
            den_tile = jnp.where(lanes == h, den, den_tile)
            hn = hp * _rstd(hp) * whn_ref[:, vs]
            ha_ref[:, vs] = (hn * jax.nn.sigmoid(o_ref[:, vs])).astype(ha_ref.dtype)
            wkc = wk_ref[:, h:h + 1]
            kw = kf * wkc
            d = dec_ref[0, :, h:h + 1]
            C_scr[h] = d * Cf + _dot_tn(vb, kw.astype(BF16))
            n_scr[h] = d * nrow + jnp.sum(kw, axis=0, keepdims=True)
        den_ref[...] = den_tile

    return pl.pallas_call(
        body, name=name,
        out_shape=(jax.ShapeDtypeStruct((S, D_MODEL), BF16), jax.ShapeDtypeStruct((S, D_MODEL), F32),
                   jax.ShapeDtypeStruct((S, LANES), F32),
                   jax.ShapeDtypeStruct((NC, ML_HEADS, ML_DV, ML_DQK), BF16),
                   jax.ShapeDtypeStruct((NC, ML_HEADS, 1, ML_DQK), F32)),
        grid=(NC,),
        in_specs=[qk(C_QM // 512), qk(C_KM // 512), wide(C_VM // D_MODEL), wide(C_OM // D_MODEL),
                  pl.BlockSpec((8, MLC), lambda c: (0, c)), col, col, col, col,
                  pl.BlockSpec((1, 1, LANES), lambda c: (c, 0, 0)), pl.BlockSpec((1, D_MODEL), lambda c: (0, 0))],
        out_specs=(pl.BlockSpec((MLC, D_MODEL), lambda c: (c, 0)), pl.BlockSpec((MLC, D_MODEL), lambda c: (c, 0)),
                   col, pl.BlockSpec((1, ML_HEADS, ML_DV, ML_DQK), lambda c: (c, 0, 0, 0)),
                   pl.BlockSpec((1, ML_HEADS, 1, ML_DQK), lambda c: (c, 0, 0, 0))),
        scratch_shapes=[pltpu.VMEM((ML_HEADS, ML_DV, ML_DQK), F32), pltpu.VMEM((ML_HEADS, 1, ML_DQK), F32)],
        compiler_params=_cparams(("arbitrary",)),
    )(pm, pm, pm, pm, a_row, A, wi, em, wk, dec, w_hn)


def _mlstm_bwd(dha, pm, hp_all, den_all, a_row, A, wi, em, wk, dec, cst, nst, w_hn, name):
    S = pm.shape[0]
    NC = S // MLC
    idx, qk, wide, col = _ml_specs(True, NC)

    def body(dha_ref, q_ref, k_ref, v_ref, o_ref, hp_ref, den_ref, arow_ref, A_ref, wi_ref, em_ref, wk_ref,
             dec_ref, cst_ref, nst_ref, whn_ref,
             dq_ref, dk_ref, dv_ref, do_ref, rk_ref, kc_ref, t_ref, dwhn_ref, dC_scr, dn_scr, t_scr):
        @pl.when(pl.program_id(0) == 0)
        def _():
            dC_scr[...] = jnp.zeros_like(dC_scr)
            dn_scr[...] = jnp.zeros_like(dn_scr)
            t_scr[...] = jnp.zeros_like(t_scr)
            dwhn_ref[...] = jnp.zeros_like(dwhn_ref)

        lanes = lax.broadcasted_iota(jnp.int32, (MLC, LANES), 1)
        lane1 = lax.broadcasted_iota(jnp.int32, (1, LANES), 1)
        t_ref[0, 0:1, :] = t_scr[...]
        rk_tile = jnp.zeros((MLC, LANES), F32)
        kc_tile = jnp.zeros((MLC, LANES), F32)
        t_new = jnp.zeros((1, LANES), F32)
        for h in range(ML_HEADS):
            hs = slice(h * ML_DQK, (h + 1) * ML_DQK)
            vs = slice(h * ML_DV, (h + 1) * ML_DV)
            hp = hp_ref[:, vs]
            sig = jax.nn.sigmoid(o_ref[:, vs])
            whn = whn_ref[:, vs]
            r = _rstd(hp)
            dga = dha_ref[:, vs]
            do_ref[:, vs] = (dga * (hp * r * whn) * sig * (1.0 - sig)).astype(do_ref.dtype)
            dhn = dga * sig
            dhp, dwt = _rmsnorm_bwd_math(dhn, hp, whn)
            dwhn_ref[:, vs] += jnp.sum(dwt, axis=0, keepdims=True)
            den = den_ref[:, h:h + 1]
            floor = em_ref[:, h:h + 1]
            D = jnp.maximum(jnp.abs(den), floor)
            dnum = dhp / D
            dh_h = jnp.sum(dhp * hp, axis=1, keepdims=True)
            active = jnp.abs(den) >= floor
            dden = -dh_h / D * jnp.where(active, jnp.sign(den), 0.0)
            phi = jnp.where(active, 0.0, dh_h)
            qb, kb, qf, kf, qk_, W = _ml_intra(q_ref, k_ref, arow_ref, A_ref, h)
            vf = v_ref[:, vs]
            vb = vf.astype(BF16)
            Cb = cst_ref[0, h]
            nrow = nst_ref[0, h]
            wic = wi_ref[:, h:h + 1]
            wkc = wk_ref[:, h:h + 1]
            d = dec_ref[0, :, h:h + 1]
            dCn = dC_scr[h]
            dCb = dCn.astype(BF16)
            dnn = dn_scr[h]
            dnumb = dnum.astype(BF16)
            s = qk_ * W
            ds = (_dot_nt(dnumb, vb) + dden) * W
            dsb = ds.astype(BF16)
            dnw = (wic * dnum).astype(BF16)
            wd = wic * dden
            kw = kf * wkc
            dv_state = _dot_nt(kw.astype(BF16), dCb)
            dq = _dot_nn(dsb, kb) + _dot_nn(dnw, Cb) + wd * nrow
            dk_state = wkc * (_dot_nn(vb, dCb) + dnn)
            dk = _dot_tn(dsb, qb) + dk_state
            dv = _dot_tn(s.astype(BF16), dnumb) + dv_state
            dC = d * dCn + _dot_tn(dnw, qb)
            dn = d * dnn + jnp.sum(wd * qf, axis=0, keepdims=True)
            dC_scr[h] = dC
            dn_scr[h] = dn
            dq_ref[:, hs] = (dq * _ML_SCALE).astype(dq_ref.dtype)
            dk_ref[:, hs] = dk.astype(dk_ref.dtype)
            dv_ref[:, vs] = dv.astype(dv_ref.dtype)
            G = ds * qk_
            inter = _dot_nt(qb, Cb)
            qn = jnp.sum(qf * nrow, axis=1, keepdims=True)
            R = (jnp.sum(G, axis=1, keepdims=True)
                 + wic * (jnp.sum(dnum * inter, axis=1, keepdims=True) + dden * qn))
            K = jnp.sum(G.T, axis=1, keepdims=True) + jnp.sum(kf * dk_state, axis=1, keepdims=True)
            rk_tile = jnp.where(lanes == h, R - K, rk_tile)
            kc_tile = jnp.where(lanes == h, phi, kc_tile)
            tt = (jnp.sum(jnp.sum(dC * Cb.astype(F32), axis=1, keepdims=True), axis=0, keepdims=True)
                  + jnp.sum(dn * nrow, axis=1, keepdims=True))
            t_new = jnp.where(lane1 == h, tt, t_new)
        rk_ref[...] = rk_tile
        kc_ref[...] = kc_tile
        t_ref[0, 1:2, :] = t_new
        t_scr[...] = t_new

    act = lambda n: jax.ShapeDtypeStruct((S, n), BF16)
    cs = jax.ShapeDtypeStruct((S, LANES), F32)
    rowblk = lambda n: pl.BlockSpec((MLC, n), lambda c: (idx(c), 0))
    return pl.pallas_call(
        body, name=name,
        out_shape=(act(512), act(512), act(D_MODEL), act(D_MODEL), cs, cs,
                   jax.ShapeDtypeStruct((NC, 2, LANES), F32), jax.ShapeDtypeStruct((1, D_MODEL), F32)),
        grid=(NC,),
        in_specs=[rowblk(D_MODEL), qk(C_QM // 512), qk(C_KM // 512), wide(C_VM // D_MODEL), wide(C_OM // D_MODEL),
                  rowblk(D_MODEL), col, pl.BlockSpec((8, MLC), lambda c: (0, idx(c))), col, col, col, col,
                  pl.BlockSpec((1, 1, LANES), lambda c: (idx(c), 0, 0)),
                  pl.BlockSpec((1, ML_HEADS, ML_DV, ML_DQK), lambda c: (idx(c), 0, 0, 0)),
                  pl.BlockSpec((1, ML_HEADS, 1, ML_DQK), lambda c: (idx(c), 0, 0, 0)),
                  pl.BlockSpec((1, D_MODEL), lambda c: (0, 0))],
        out_specs=(rowblk(512), rowblk(512), rowblk(D_MODEL), rowblk(D_MODEL), col, col,
                   pl.BlockSpec((1, 2, LANES), lambda c: (idx(c), 0, 0)), pl.BlockSpec((1, D_MODEL), lambda c: (0, 0))),
        scratch_shapes=[pltpu.VMEM((ML_HEADS, ML_DV, ML_DQK), F32), pltpu.VMEM((ML_HEADS, 1, ML_DQK), F32),
                        pltpu.VMEM((1, LANES), F32)],
        compiler_params=_cparams(("arbitrary",)),
    )(dha, pm, pm, pm, pm, hp_all, den_all, a_row, A, wi, em, wk, dec, cst, nst, w_hn)


_FOX_SCALE = FOX_DH ** -0.5
_NEG = -1e30
_LOG2E = 1.4426950408889634
_LN2 = 0.6931471805599453
_QF_BLK, _KF_BLK, _VF_BLK = 0, FOX_HEADS, 2 * FOX_HEADS


def _lane_pick(tile, lane):
    lanes = lax.broadcasted_iota(jnp.int32, tile.shape, 1)
    return jnp.sum(jnp.where(lanes == lane, tile, 0.0), axis=1, keepdims=True)


def _col_to_row(col):
    return jnp.max(jnp.broadcast_to(col, (col.shape[0], LANES)).T, axis=0, keepdims=True)


def _causal(q0, k0, shape, q_axis):
    qpos = q0 + lax.broadcasted_iota(jnp.int32, shape, q_axis)
    kpos = k0 + lax.broadcasted_iota(jnp.int32, shape, 1 - q_axis)
    return kpos <= qpos


def _fox_fwd(pf, fc, fk_row, name):
    S = pf.shape[0]
    TQ, TK = FOX_TQ, FOX_TK
    nq, nk = S // TQ, S // TK
    c1 = _FOX_SCALE * _LOG2E

    def body(q_ref, k_ref, v_ref, fc_ref, fr_ref, o_ref, lse_ref):
        h, i = pl.program_id(0), pl.program_id(1)
        qb = q_ref[...]
        fq2 = _lane_pick(fc_ref[...], h) * _LOG2E

        def step(j, carry, masked):
            m, l, acc = carry
            off = pl.multiple_of(j * TK, TK)
            t = _dot_nt(qb, k_ref[pl.ds(off, TK), :]) * c1 - fr_ref[0, j] * _LOG2E
            if masked:
                t = jnp.where(_causal(i * TQ, j * TK, (TQ, TK), 0), t, _NEG)
            m_new = jnp.maximum(m, jnp.max(t, axis=1, keepdims=True) + fq2)
            alpha = jnp.exp2(m - m_new)
            p = jnp.exp2(t + (fq2 - m_new))
            l = alpha * l + jnp.sum(p, axis=1, keepdims=True)
            acc = alpha * acc + _dot_nn(p.astype(BF16), v_ref[pl.ds(off, TK), :])
            return m_new, l, acc

        init = (jnp.full((TQ, 1), _NEG, F32), jnp.zeros((TQ, 1), F32), jnp.zeros((TQ, FOX_DH), F32))
        last = (i * TQ) // TK
        carry = lax.fori_loop(0, last, lambda j, c: step(j, c, False), init)
        m, l, acc = step(last, carry, True)
        o_ref[...] = (acc / l).astype(o_ref.dtype)
        lse_ref[0, 0] = _col_to_row((m + jnp.log2(l)) * _LN2)

    head = lambda blk: pl.BlockSpec((S, FOX_DH), lambda h, i: (0, blk + h))
    return pl.pallas_call(
        body, name=name,
        out_shape=(jax.ShapeDtypeStruct((S, D_MODEL), BF16), jax.ShapeDtypeStruct((FOX_HEADS, nq, 1, TQ), F32)),
        grid=(FOX_HEADS, nq),
        in_specs=[pl.BlockSpec((TQ, FOX_DH), lambda h, i: (i, _QF_BLK + h)), head(_KF_BLK), head(_VF_BLK),
                  pl.BlockSpec((TQ, LANES), lambda h, i: (i, 0)),
                  pl.BlockSpec((1, nk, 1, TK), lambda h, i: (h, 0, 0, 0))],
        out_specs=(pl.BlockSpec((TQ, FOX_DH), lambda h, i: (i, h)),
                   pl.BlockSpec((1, 1, 1, TQ), lambda h, i: (h, i, 0, 0))),
        compiler_params=_cparams(("parallel", "arbitrary")),
    )(pf, pf, pf, fc, fk_row)


def _fox_bwd(dhb, hb, pf, lse_row, fq_row, fc, name):
    S = pf.shape[0]
    TQ, TK = FOX_TQ, FOX_TK
    nq, nk, r = S // TQ, S // TK, TK // TQ
    c1 = _FOX_SCALE * _LOG2E

    def body(q_ref, k_ref, v_ref, do_ref, o_ref, lse_ref, fq_ref, fc_ref,
             dq_ref, dk_ref, dv_ref, dFk_ref, dFq_ref, dq_acc, qside, delta, dk_acc, dv_acc, cs_acc):
        h, j = pl.program_id(0), pl.program_id(1)

        @pl.when(j == 0)
        def _():
            dq_acc[...] = jnp.zeros_like(dq_acc)
            dFq_ref[...] = jnp.zeros_like(dFq_ref)

            def fill(b, _):
                off = pl.multiple_of(b * TQ, TQ)
                prod = do_ref[pl.ds(off, TQ), :].astype(F32) * o_ref[pl.ds(off, TQ), :].astype(F32)
                delta[b] = jnp.sum(prod.T, axis=0, keepdims=True)
                qside[b] = (fq_ref[0, b] - lse_ref[0, b]) * _LOG2E
                return 0

            lax.fori_loop(0, nq, fill, 0)

        kb = k_ref[...]
        vb = v_ref[...]
        fk2 = _lane_pick(fc_ref[...], h) * _LOG2E
        dk_acc[...] = jnp.zeros_like(dk_acc)
        dv_acc[...] = jnp.zeros_like(dv_acc)
        cs_acc[...] = jnp.zeros_like(cs_acc)

        def step(i, masked):
            off = pl.multiple_of(i * TQ, TQ)
            qb = q_ref[pl.ds(off, TQ), :]
            dob = do_ref[pl.ds(off, TQ), :]
            t = _dot_nt(kb, qb) * c1 + qside[i] - fk2
            if masked:
                t = jnp.where(_causal(i * TQ, j * TK, (TK, TQ), 1), t, _NEG)
            p = jnp.exp2(t)
            dv_acc[...] += _dot_nn(p.astype(BF16), dob)
            ds = p * (_dot_nt(vb, dob) - delta[i])
            dsb = ds.astype(BF16)
            dk_acc[...] += _dot_nn(dsb, qb)
            dq_acc[pl.ds(off, TQ), :] += _dot_tn(dsb, kb)
            cs_acc[...] += jnp.sum(ds, axis=1, keepdims=True)
            dFq_ref[0, i] += jnp.sum(ds, axis=0, keepdims=True)

        for d in range(r):
            step(r * j + d, True)

        def rest(i, _):
            step(i, False)
            return 0

        lax.fori_loop(r * j + r, nq, rest, 0)
        dk_ref[...] = (dk_acc[...] * _FOX_SCALE).astype(dk_ref.dtype)
        dv_ref[...] = dv_acc[...].astype(dv_ref.dtype)
        dFk_ref[0, 0] = -_col_to_row(cs_acc[...])

        @pl.when(j == nk - 1)
        def _():
            dq_ref[...] = (dq_acc[...] * _FOX_SCALE).astype(dq_ref.dtype)

    head = lambda blk: pl.BlockSpec((S, FOX_DH), lambda h, j: (0, blk + h))
    kblk = lambda blk: pl.BlockSpec((TK, FOX_DH), lambda h, j: (j, blk + h))
    qrows = pl.BlockSpec((1, nq, 1, TQ), lambda h, j: (h, 0, 0, 0))
    act = jax.ShapeDtypeStruct((S, D_MODEL), BF16)
    return pl.pallas_call(
        body, name=name,
        out_shape=(act, act, act, jax.ShapeDtypeStruct((FOX_HEADS, nk, 1, TK), F32),
                   jax.ShapeDtypeStruct((FOX_HEADS, nq, 1, TQ), F32)),
        grid=(FOX_HEADS, nk),
        in_specs=[head(_QF_BLK), kblk(_KF_BLK), kblk(_VF_BLK), head(0), head(0), qrows, qrows,
                  pl.BlockSpec((TK, LANES), lambda h, j: (j, 0))],
        out_specs=(head(0), kblk(0), kblk(0), pl.BlockSpec((1, 1, 1, TK), lambda h, j: (h, j, 0, 0)), qrows),
        scratch_shapes=[pltpu.VMEM((S, FOX_DH), F32), pltpu.VMEM((nq, 1, TQ), F32), pltpu.VMEM((nq, 1, TQ), F32),
                        pltpu.VMEM((TK, FOX_DH), F32), pltpu.VMEM((TK, FOX_DH), F32), pltpu.VMEM((TK, 1), F32)],
        compiler_params=_cparams(("parallel", "arbitrary")),
    )(pf, pf, pf, dhb, hb, lse_row, fq_row, fc)


def _pad_lanes(v):
    return jnp.pad(v, ((0, 0), (0, LANES - v.shape[1])))


def _local_step(x, target, wmain_t, wsmall_t, wa, wb, wout, wup, wdown, p):
    S = x.shape[0]
    bi, bf, bff = _pad_lanes(p["b_ml_i"]), _pad_lanes(p["b_ml_f"]), _pad_lanes(p["b_fox_f"])

    h0 = _rmsnorm_fwd(x, p["norm_mix_pre"], "norm_mix_pre")
    pm = _mm(h0, wmain_t[:N_ML], "nt", F32, "proj_mlstm")
    pf = _mm(h0, wmain_t[N_ML:N_ML + N_FOX], "nt", BF16, "proj_fox")
    pg = _mm(h0, wmain_t[N_ML + N_FOX:], "nt", F32, "proj_merge")
    ps = _mm(h0, wsmall_t, "nt", F32, "proj_gates")
    a, A, wi, em, wk, dec, Fc = _gates_fwd(ps, bi, bf, bff, "gates_fwd")
    a_row = a[:, :8].T
    ha, hp, den, cst, nst = _mlstm_fwd(pm, a_row, A, wi, em, wk, dec, p["ml_head_norm"], "mlstm_fwd")
    ft = Fc[:, :FOX_HEADS].T
    fq_row = ft.reshape(FOX_HEADS, S // FOX_TQ, 1, FOX_TQ)
    fk_row = ft.reshape(FOX_HEADS, S // FOX_TK, 1, FOX_TK)
    hb, lse_row = _fox_fwd(pf, Fc, fk_row, "fox_fwd")
    ya = _mm(ha, wa, "nn", F32, "branch_a")
    yb = _mm(hb, wb, "nn", F32, "branch_b")
    merged = _merge_fwd(ya, yb, pg, p["b_gate_a"], p["b_gate_b"], "merge_fwd")
    z = _mm(merged, wout, "nn", F32, "out_proj")
    x1 = _resid_norm_fwd(x, z, p["norm_mix_post"], "resid_mix")
    h2 = _rmsnorm_fwd(x1, p["norm_ffn_pre"], "norm_ffn_pre")
    up = _mm(h2, wup, "nn", F32, "ffn_up")
    act = _conv_act_fwd(up, p["conv_w"], p["conv_b"], "conv_act_fwd")
    d = _mm(act, wdown, "nn", F32, "ffn_down")
    loss_row, dy, dd, g_norm_ffn_post = _loss_head(x1, d, p["norm_ffn_post"], target, "loss_head")
    dact = _mm(dd, wdown, "nt", F32, "d_act")
    g_wdown = _mm(act, dd, "tn", F32, "dw_down")
    dua, dug, dcwa, dcwg, dcba, dcbg = _conv_act_bwd_du(up, dact, p["conv_w"], p["conv_b"], "conv_act_bwd")
    g_conv_w = jnp.concatenate([dcwa, dcwg], axis=1)
    g_conv_b = jnp.concatenate([dcba, dcbg], axis=1)
    dup = jnp.concatenate([_conv_bwd_dup(dua, p["conv_w"], 0, "conv_bwd_a"),
                           _conv_bwd_dup(dug, p["conv_w"], (D_FF // _tile(D_FF, 256)), "conv_bwd_g")], axis=1)
    dh2 = _mm(dup, wup, "nt", F32, "d_h2")
    g_wup = _mm(h2, dup, "tn", F32, "dw_up")
    dx1, g_norm_ffn_pre = _rmsnorm_bwd([dh2], x1, p["norm_ffn_pre"], dy, F32, "norm_ffn_pre_bwd")
    dz, g_norm_mix_post = _rmsnorm_bwd([dx1], z, p["norm_mix_post"], None, BF16, "norm_mix_post_bwd")
    dmerged = _mm(dz, wout, "nt", F32, "d_merged")
    g_wout = _mm(merged, dz, "tn", F32, "dw_out")
    dya, dyb, dga, dgb, g_b_gate_a, g_b_gate_b = _merge_bwd(dmerged, ya, yb, pg, p["b_gate_a"], p["b_gate_b"], "merge_bwd")
    dha = _mm(dya, wa, "nt", F32, "d_ha")
    g_wa = _mm(ha, dya, "tn", F32, "dw_a")
    dhb = _mm(dyb, wb, "nt", BF16, "d_hb")
    g_wb = _mm(hb, dyb, "tn", F32, "dw_b")
    dqm, dkm, dvm, dom, rk, kc, tch, g_ml_head_norm = _mlstm_bwd(
        dha, pm, hp, den, a_row, A, wi, em, wk, dec, cst, nst, p["ml_head_norm"], "mlstm_bwd")
    dqf, dkf, dvf, dFk, dFq = _fox_bwd(dhb, hb, pf, lse_row, fq_row, Fc, "fox_bwd")
    dF = jnp.pad((dFk.reshape(FOX_HEADS, S) + dFq.reshape(FOX_HEADS, S)).T, ((0, 0), (0, LANES - FOX_HEADS)))
    dps, dbias = _gates_bwd(ps, bi, bf, bff, rk, kc, tch, dF, "gates_bwd")
    dpm = jnp.concatenate([dqm, dkm, dvm, dom, dqf, dkf, dvf, dga, dgb], axis=1)
    dh0 = _mm(dpm, wmain_t, "nn", F32, "d_h0_main")
    dh0s = _mm(dps, wsmall_t, "nn", F32, "d_h0_gates")
    g_wmain_t = _mm(dpm, h0, "tn", F32, "dw_main")
    g_wsmall_t = _mm(dps, h0, "tn", F32, "dw_gates")
    grad_x, g_norm_mix_pre = _rmsnorm_bwd([dh0, dh0s], x, p["norm_mix_pre"], dx1, F32, "norm_mix_pre_bwd")

    big = dict(wmain_t=g_wmain_t, wsmall_t=g_wsmall_t, w_branch_a=g_wa, w_branch_b=g_wb, w_out=g_wout, w_up=g_wup, w_down=g_wdown)
    small = dict(norm_mix_pre=g_norm_mix_pre, ml_head_norm=g_ml_head_norm, b_gate_a=g_b_gate_a, b_gate_b=g_b_gate_b,
                 norm_mix_post=g_norm_mix_post, norm_ffn_pre=g_norm_ffn_pre, norm_ffn_post=g_norm_ffn_post,
                 conv_b=g_conv_b, b_ml_i=dbias[:, 0:ML_HEADS], b_ml_f=dbias[:, LANES:LANES + ML_HEADS],
                 b_fox_f=dbias[:, 2 * LANES:2 * LANES + FOX_HEADS], conv_w=g_conv_w)
    return loss_row, grad_x, big, small


def _row_tile(r, target=256):
    best = None
    for t in range(8, min(r, target) + 1, 8):
        if r % t == 0:
            best = t
    return best if best is not None else r


def _adamw(w, g, m, v, name):
    _, R, C = w.shape
    tr = _row_tile(R)
    tc = C
    if tr == R and R > 256:
        tc = 256

    def body(w_ref, g_ref, m_ref, v_ref, d_ref, mo_ref, vo_ref):
        gv = g_ref[...]
        mn = ADAM_B1 * m_ref[0] + (1.0 - ADAM_B1) * gv
        vn = ADAM_B2 * v_ref[0] + (1.0 - ADAM_B2) * (gv * gv)
        m_hat = mn / (1.0 - ADAM_B1 ** ADAM_STEP)
        v_hat = vn / (1.0 - ADAM_B2 ** ADAM_STEP)
        d_ref[0] = -ADAM_LR * (m_hat / (jnp.sqrt(v_hat) + ADAM_EPS) + ADAM_WD * w_ref[0])
        mo_ref[0] = mn
        vo_ref[0] = vn

    blk = pl.BlockSpec((1, tr, tc), lambda i, j: (0, i, j))
    o = jax.ShapeDtypeStruct((1, R, C), F32)
    return pl.pallas_call(
        body, name=name, out_shape=(o, o, o), grid=(R // tr, C // tc),
        in_specs=[blk, pl.BlockSpec((tr, tc), lambda i, j: (i, j)), blk, blk], out_specs=(blk,) * 3,
        compiler_params=_cparams(("parallel", "parallel")),
    )(w, g, m, v)


ANY = pl.BlockSpec(memory_space=pl.ANY)


def _place():
    x, y, c = lax.axis_index("x"), lax.axis_index("y"), lax.axis_index("c")
    chips = [(1 - x, y), (x, 1 - y), (1 - x, 1 - y)]
    return x, y, c, chips


def _block(ref, kind, k, rows=None):
    if kind == "rows":
        return ref.at[k] if rows is None else ref.at[k, pl.ds(*rows), :]
    cb = ref.shape[1] // 4
    return ref.at[:, pl.ds(k * cb, cb)] if rows is None else ref.at[pl.ds(*rows), pl.ds(k * cb, cb)]


def _gathered_shape(s, kind):
    return (4,) + s.shape if kind == "rows" else (s.shape[0], 4 * s.shape[1])


def _gather_weights(shards, kinds, smalls):
    n, ns = len(shards), len(smalls)

    def body(*refs):
        ins, sm_in = refs[:n], refs[n:n + ns]
        outs, sm_out = refs[n + ns:2 * n + ns], refs[2 * n + ns:2 * (n + ns)]
        send_sems, recv_sems, sm_send, sm_recv, local_sems = refs[2 * (n + ns):]
        x, y, c, chips = _place()
        sibling = (x, y, 1 - c)
        kme = 2 * x + y

        def half(a, k, hc):
            h = ins[a].shape[0] // 2
            return _block(outs[a], kinds[a], k, (hc * h, h))

        def remote(a, slot, src, dst, to):
            return pltpu.make_async_remote_copy(src_ref=src, dst_ref=dst, send_sem=send_sems.at[a * 7 + slot],
                                                recv_sem=recv_sems.at[a * 7 + slot], device_id=to, device_id_type=MESH)

        def sm_copy(b, j, k, to):
            return pltpu.make_async_remote_copy(src_ref=sm_in[b], dst_ref=sm_out[b].at[k], send_sem=sm_send.at[3 * b + j],
                                                recv_sem=sm_recv.at[3 * b + j], device_id=to, device_id_type=MESH)

        local = [pltpu.make_async_copy(sm_in[b], sm_out[b].at[kme], local_sems.at[b]) for b in range(ns)]
        for cp in local:
            cp.start()
        sends = [remote(a, 6, ins[a], _block(outs[a], kinds[a], kme), sibling) for a in range(n)]
        for a in range(n):
            h = ins[a].shape[0] // 2
            for j, chip in enumerate(chips):
                sends.append(remote(a, j, ins[a].at[pl.ds(c * h, h), :], half(a, kme, c), (*chip, c)))
        for b in range(ns):
            for j, chip in enumerate(chips):
                sends.append(sm_copy(b, j, kme, (*chip, c)))
        for cp in sends:
            cp.start()
        for a in range(n):
            for j, chip in enumerate(chips):
                kj = 2 * chip[0] + chip[1]
                remote(a, j, half(a, kj, c), half(a, kj, c), (*chip, c)).wait_recv()
                fwd = remote(a, 3 + j, half(a, kj, c), half(a, kj, c), sibling)
                fwd.start()
                sends.append(fwd)
        for a in range(n):
            for j, chip in enumerate(chips):
                kj = 2 * chip[0] + chip[1]
                remote(a, 3 + j, half(a, kj, 1 - c), half(a, kj, 1 - c), sibling).wait_recv()
        for b in range(ns):
            for j, chip in enumerate(chips):
                sm_copy(b, j, 2 * chip[0] + chip[1], (*chip, c)).wait_recv()
        for a in range(n):
            remote(a, 6, ins[a], _block(outs[a], kinds[a], kme), sibling).wait_recv()
        for cp in sends:
            cp.wait_send()
        for cp in local:
            cp.wait()

    outs = pl.pallas_call(
        body, name="gather_weights",
        out_shape=tuple([jax.ShapeDtypeStruct(_gathered_shape(s, k), s.dtype) for s, k in zip(shards, kinds)]
                        + [jax.ShapeDtypeStruct((4,) + s.shape, s.dtype) for s in smalls]),
        in_specs=[ANY] * (n + ns), out_specs=tuple([ANY] * (n + ns)),
        scratch_shapes=[pltpu.SemaphoreType.DMA((7 * n,)), pltpu.SemaphoreType.DMA((7 * n,)),
                        pltpu.SemaphoreType.DMA((3 * ns,)), pltpu.SemaphoreType.DMA((3 * ns,)),
                        pltpu.SemaphoreType.DMA((ns,))],
    )(*shards, *smalls)
    return outs[:n], outs[n:]


def _exchange_sibling_halves(gs, kinds):
    n = len(gs)
    hshape = lambda g, kind: (4, g.shape[1] // 2, g.shape[2]) if kind == "rows" else (g.shape[0] // 2, g.shape[1])

    def body(*refs):
        ins, outs, send_sems, recv_sems = refs[:n], refs[n:2 * n], refs[2 * n], refs[2 * n + 1]
        x, y, c, _ = _place()
        cps = []
        for a in range(n):
            h = outs[a].shape[-2]
            src = ins[a].at[:, pl.ds((1 - c) * h, h), :] if kinds[a] == "rows" else ins[a].at[pl.ds((1 - c) * h, h), :]
            cps.append(pltpu.make_async_remote_copy(
                src_ref=src, dst_ref=outs[a], send_sem=send_sems.at[a],
                recv_sem=recv_sems.at[a], device_id=(x, y, 1 - c), device_id_type=MESH))
        for cp in cps:
            cp.start()
        for cp in cps:
            cp.wait()

    return pl.pallas_call(
        body, name="grads_to_sibling",
        out_shape=tuple(jax.ShapeDtypeStruct(hshape(g, k), g.dtype) for g, k in zip(gs, kinds)),
        in_specs=[ANY] * n, out_specs=tuple([ANY] * n),
        scratch_shapes=[pltpu.SemaphoreType.DMA((n,)), pltpu.SemaphoreType.DMA((n,))],
    )(*gs)


def _add_halves(g, r1, cvec, kind, name):
    def body(c_ref, g_ref, r_ref, o_ref):
        o_ref[...] = (g_ref[...] + r_ref[...]).astype(o_ref.dtype)

    if kind == "rows":
        _, h, C = r1.shape
        tr = _row_tile(h)
        nt = h // tr
        grid = (4, nt)
        g_spec = pl.BlockSpec((1, tr, C), lambda k, i, c_ref: (k, c_ref[0] * nt + i, 0))
        r_spec = pl.BlockSpec((1, tr, C), lambda k, i, c_ref: (k, i, 0))
    else:
        h, C4 = r1.shape
        tr, tc = _row_tile(h), C4 // 4
        nt = h // tr
        grid = (nt, 4)
        g_spec = pl.BlockSpec((tr, tc), lambda i, k, c_ref: (c_ref[0] * nt + i, k))
        r_spec = pl.BlockSpec((tr, tc), lambda i, k, c_ref: (i, k))
    return pl.pallas_call(
        body, name=name, out_shape=jax.ShapeDtypeStruct(r1.shape, BF16),
        grid_spec=pltpu.PrefetchScalarGridSpec(num_scalar_prefetch=1, grid=grid, in_specs=[g_spec, r_spec],
                                               out_specs=r_spec),
        compiler_params=_cparams(("parallel", "parallel")),
    )(cvec, g, r1)


def _exchange_chips(ss, kinds):
    n = len(ss)
    bshape = lambda s, kind: s.shape[1:] if kind == "rows" else (s.shape[0], s.shape[1] // 4)

    def body(*refs):
        ins, outs, send_sems, recv_sems = refs[:n], refs[n:2 * n], refs[2 * n], refs[2 * n + 1]
        x, y, c, chips = _place()
        cps = []
        for a in range(n):
            for j, chip in enumerate(chips):
                cps.append(pltpu.make_async_remote_copy(
                    src_ref=_block(ins[a], kinds[a], 2 * chip[0] + chip[1]), dst_ref=outs[a].at[j],
                    send_sem=send_sems.at[3 * a + j], recv_sem=recv_sems.at[3 * a + j], device_id=(*chip, c),
                    device_id_type=MESH))
        for cp in cps:
            cp.start()
        for cp in cps:
            cp.wait()

    return pl.pallas_call(
        body, name="grads_to_chips",
        out_shape=tuple(jax.ShapeDtypeStruct((3,) + bshape(s, k), s.dtype) for s, k in zip(ss, kinds)),
        in_specs=[ANY] * n, out_specs=tuple([ANY] * n),
        scratch_shapes=[pltpu.SemaphoreType.DMA((3 * n,)), pltpu.SemaphoreType.DMA((3 * n,))],
    )(*ss)


def _add_chips(s1, r2, kcvec, kind, name):
    _, h, C = r2.shape
    tr = _row_tile(h)
    nt = h // tr

    def body(kc_ref, s_ref, r0_ref, r1_ref, r2_ref, o_ref):
        s = s_ref[0] if kind == "rows" else s_ref[...]
        o_ref[...] = ((s.astype(F32) + r0_ref[0].astype(F32)) + r1_ref[0].astype(F32)) + r2_ref[0].astype(F32)

    peer = lambda j: pl.BlockSpec((1, tr, C), lambda i, kc_ref: (j, i, 0))
    if kind == "rows":
        s_spec = pl.BlockSpec((1, tr, C), lambda i, kc_ref: (kc_ref[0], i, 0))
    else:
        s_spec = pl.BlockSpec((tr, C), lambda i, kc_ref: (i, kc_ref[0]))
    return pl.pallas_call(
        body, name=name, out_shape=jax.ShapeDtypeStruct((2 * h, C), F32),
        grid_spec=pltpu.PrefetchScalarGridSpec(
            num_scalar_prefetch=1, grid=(nt,),
            in_specs=[s_spec, peer(0), peer(1), peer(2)],
            out_specs=pl.BlockSpec((tr, C), lambda i, kc_ref: (kc_ref[1] * nt + i, 0))),
        compiler_params=_cparams(("parallel",)),
    )(kcvec, s1, r2, r2, r2)


def _join_sibling_halves(bufs):
    n = len(bufs)

    def body(*refs):
        ins, outs, send_sems, recv_sems = refs[:n], refs[n:2 * n], refs[2 * n], refs[2 * n + 1]
        x, y, c, _ = _place()
        cps = []
        for a in range(n):
            h = ins[a].shape[0] // 2
            cps.append(pltpu.make_async_remote_copy(
                src_ref=ins[a].at[pl.ds(c * h, h), :], dst_ref=outs[a].at[pl.ds(c * h, h), :], send_sem=send_sems.at[a],
                recv_sem=recv_sems.at[a], device_id=(x, y, 1 - c), device_id_type=MESH))
        for cp in cps:
            cp.start()
        for a in range(n):
            h = ins[a].shape[0] // 2
            theirs = outs[a].at[pl.ds((1 - c) * h, h), :]
            pltpu.make_async_remote_copy(src_ref=theirs, dst_ref=theirs, send_sem=send_sems.at[a],
                                         recv_sem=recv_sems.at[a], device_id=(x, y, 1 - c), device_id_type=MESH).wait_recv()
        for cp in cps:
            cp.wait_send()

    return pl.pallas_call(
        body, name="grads_join",
        out_shape=tuple(jax.ShapeDtypeStruct(b.shape, b.dtype) for b in bufs),
        in_specs=[ANY] * n, out_specs=tuple([ANY] * n), input_output_aliases={a: a for a in range(n)},
        scratch_shapes=[pltpu.SemaphoreType.DMA((n,)), pltpu.SemaphoreType.DMA((n,))],
    )(*bufs)


N_DEV = 8


def _allreduce_small(pack):
    P = pack.shape[0]

    def body(p_ref, o_ref, gath, send_sems, recv_sems):
        x, y, c, _ = _place()
        me = 4 * x + 2 * y + c
        cps = []
        for mask in range(1, N_DEV):
            px = 1 - x if mask & 4 else x
            py = 1 - y if mask & 2 else y
            pc = 1 - c if mask & 1 else c
            cps.append((pltpu.make_async_remote_copy(
                src_ref=p_ref, dst_ref=gath.at[me], send_sem=send_sems.at[mask - 1], recv_sem=recv_sems.at[mask - 1],
                device_id=(px, py, pc), device_id_type=MESH), 4 * px + 2 * py + pc, mask))
        for cp, _, _ in cps:
            cp.start()
        gath[me] = p_ref[...]
        for _, peer, mask in cps:
            pltpu.make_async_remote_copy(
                src_ref=p_ref, dst_ref=gath.at[peer], send_sem=send_sems.at[mask - 1], recv_sem=recv_sems.at[mask - 1],
                device_id=(x, y, c), device_id_type=MESH).wait_recv()
        for cp, _, _ in cps:
            cp.wait_send()
        acc = gath[0]
        for i in range(1, N_DEV):
            acc = acc + gath[i]
        o_ref[...] = acc

    return pl.pallas_call(
        body, name="allreduce_small", out_shape=jax.ShapeDtypeStruct((P, LANES), F32),
        in_specs=[pl.BlockSpec(memory_space=pltpu.VMEM)], out_specs=pl.BlockSpec(memory_space=pltpu.VMEM),
        scratch_shapes=[pltpu.VMEM((N_DEV, P, LANES), F32), pltpu.SemaphoreType.DMA((N_DEV - 1,)),
                        pltpu.SemaphoreType.DMA((N_DEV - 1,))],
    )(pack)


def _pack_rows(arrs):
    rows = []
    for a in arrs:
        f = a.reshape(-1)
        f = jnp.pad(f, (0, (-f.shape[0]) % LANES))
        rows.append(f.reshape(-1, LANES))
    return jnp.concatenate(rows, axis=0)


def _unpack_rows(pack, shapes):
    out, r = [], 0
    for s in shapes:
        n = math.prod(s)
        nr = -(-n // LANES)
        out.append(pack[r:r + nr].reshape(-1)[:n].reshape(s))
        r += nr
    return out


_SMALL = ["norm_mix_pre", "ml_head_norm", "b_gate_a", "b_gate_b", "norm_mix_post", "norm_ffn_pre", "norm_ffn_post",
          "conv_b", "b_ml_i", "b_ml_f", "b_fox_f"]
_BIG = ["w_in", "w_branch_a", "w_branch_b", "w_out", "w_up", "w_down"]
_WEIGHTS = ['norm_mix_pre', 'w_in', 'b_ml_i', 'b_ml_f', 'ml_head_norm', 'b_fox_f', 'b_gate_a', 'b_gate_b', 'w_branch_a',
            'w_branch_b', 'w_out', 'norm_mix_post', 'norm_ffn_pre', 'w_up', 'conv_w', 'conv_b', 'w_down', 'norm_ffn_post']


_KINDS = ["rows", "rows", "rows", "rows", "cols", "rows"]


def kernel(x, norm_mix_pre, w_in, b_ml_i, b_ml_f, ml_head_norm, b_fox_f, b_gate_a, b_gate_b, w_branch_a, w_branch_b, w_out, norm_mix_post, norm_ffn_pre, w_up, conv_w, conv_b, w_down, norm_ffn_post, loss_target, m_norm_mix_pre, m_w_in, m_b_ml_i, m_b_ml_f, m_ml_head_norm, m_b_fox_f, m_b_gate_a, m_b_gate_b, m_w_branch_a, m_w_branch_b, m_w_out, m_norm_mix_post, m_norm_ffn_pre, m_w_up, m_conv_w, m_conv_b, m_w_down, m_norm_ffn_post, v_norm_mix_pre, v_w_in, v_b_ml_i, v_b_ml_f, v_ml_head_norm, v_b_fox_f, v_b_gate_a, v_b_gate_b, v_w_branch_a, v_w_branch_b, v_w_out, v_norm_mix_post, v_norm_ffn_pre, v_w_up, v_conv_w, v_conv_b, v_w_down, v_norm_ffn_post):
    args = dict(locals())
    w = {n: args[n] for n in _WEIGHTS}
    mom = {n: args["m_" + n] for n in _WEIGHTS}
    var = {n: args["v_" + n] for n in _WEIGHTS}
    cx, cy, cc = lax.axis_index("x"), lax.axis_index("y"), lax.axis_index("c")
    kme = 2 * cx + cy
    cvec = jnp.reshape(cc, (1,)).astype(jnp.int32)
    kcvec = jnp.stack([kme, cc]).astype(jnp.int32)
    odd = kme % 2

    tr3 = lambda t: jnp.transpose(t, (0, 2, 1))
    w["w_in"], mom["w_in"], var["w_in"] = tr3(w_in), tr3(m_w_in), tr3(v_w_in)
    w_in_main = lax.dynamic_slice_in_dim(w["w_in"][0], 4 * odd, 2048, axis=0).astype(BF16)
    w_in_gates = lax.dynamic_slice_in_dim(w["w_in"][0], 2048 * (1 - odd), 4, axis=0).astype(BF16)
    shards = [w_in_main] + [w[n][0].astype(BF16) for n in _BIG[1:]]
    (wmain_t, g_a, g_b, g_out, wup, g_down), (g_cw, g_gates) = _gather_weights(
        shards, _KINDS, [w["conv_w"][0], w_in_gates])
    gate_rows = g_gates.reshape(16, D_MODEL)
    wsmall_t = jnp.zeros((N_SMALL, D_MODEL), BF16)
    for blk, (lo, hi) in enumerate(((0, 4), (4, 8), (8, 16))):
        wsmall_t = wsmall_t.at[blk * LANES:blk * LANES + hi - lo].set(gate_rows[lo:hi])
    full = lambda g: g.reshape(-1, g.shape[2])
    p = {n: w[n] for n in _SMALL}
    p["conv_w"] = jnp.transpose(g_cw, (1, 0, 2)).reshape(3, -1)

    loss_row, grad_x, big, small = _local_step(x[0], loss_target[0], full(wmain_t), wsmall_t, full(g_a), full(g_b),
                                               full(g_out), wup, full(g_down), p)

    whole = [big["wmain_t"].reshape(4, -1, D_MODEL), big["w_branch_a"].reshape(4, -1, D_MODEL),
             big["w_branch_b"].reshape(4, -1, D_MODEL),
             big["w_out"].reshape(4, -1, D_MODEL), big["w_up"], big["w_down"].reshape(4, -1, D_MODEL)]
    from_sibling = _exchange_sibling_halves(whole, _KINDS)
    chip_sums = [_add_halves(g, r, cvec, k, "add_sibling_" + n) for g, r, k, n in zip(whole, from_sibling, _KINDS, _BIG)]
    from_chips = _exchange_chips(chip_sums, _KINDS)
    mine = [_add_chips(s, r, kcvec, k, "add_chips_" + n) for s, r, k, n in zip(chip_sums, from_chips, _KINDS, _BIG)]
    grads = dict(zip(_BIG, _join_sibling_halves(mine)))

    gt = big["wsmall_t"]
    small["w_in_gates"] = jnp.concatenate([gt[0:4], gt[LANES:LANES + 4], gt[2 * LANES:2 * LANES + 8]], axis=0)
    small_names = _SMALL + ["conv_w"]
    packed_names = small_names + ["w_in_gates"]
    pack = _pack_rows([small[n] for n in packed_names] + [loss_row])
    pack = jnp.pad(pack, ((0, (-pack.shape[0]) % 8), (0, 0)))
    full_shapes = [small[n].shape if n in ("conv_w", "w_in_gates") else w[n][0].shape for n in packed_names]
    total = _unpack_rows(_allreduce_small(pack), full_shapes + [loss_row.shape])
    for n, t in zip(packed_names, total):
        grads[n] = t
    loss = total[-1][0, 0]
    grads["conv_w"] = lax.dynamic_slice_in_dim(grads["conv_w"], kme * conv_w.shape[2], conv_w.shape[2], axis=1)
    my_gates = lax.dynamic_slice_in_dim(grads.pop("w_in_gates"), 4 * kme, 4, axis=0)
    g_in = jnp.zeros(w["w_in"].shape[1:], F32)
    g_in = lax.dynamic_update_slice_in_dim(g_in, grads["w_in"], 4 * odd, axis=0)
    grads["w_in"] = lax.dynamic_update_slice_in_dim(g_in, my_gates, 2048 * (1 - odd), axis=0)

    delta, new_m, new_v = {}, {}, {}
    for n in _BIG:
        delta[n], new_m[n], new_v[n] = _adamw(w[n], grads[n], mom[n], var[n], "adamw_" + n)
        grads[n] = grads[n][None]
    for d in (grads, delta, new_m, new_v):
        d["w_in"] = tr3(d["w_in"])
    packs = [_pack_rows([d[n][0] for n in small_names]) for d in (w, mom, var)]
    pad = ((0, (-packs[0].shape[0]) % 8), (0, 0))
    packs = [jnp.pad(t, pad)[None] for t in packs]
    gp = jnp.pad(_pack_rows([grads[n] for n in small_names]), pad)
    shapes = [w[n][0].shape for n in small_names]
    for dst, res in zip((delta, new_m, new_v), _adamw(packs[0], gp, packs[1], packs[2], "adamw_small")):
        for n, t in zip(small_names, _unpack_rows(res[0], shapes)):
            dst[n] = t[None]
    for n in small_names:
        grads[n] = grads[n][None]

    return (loss, grad_x[None], *[grads[n] for n in _WEIGHTS], *[delta[n] for n in _WEIGHTS],
            *[new_m[n] for n in _WEIGHTS], *[new_v[n] for n in _WEIGHTS])
```

```python
import functools
import math

import jax
import jax.numpy as jnp
from jax import lax
from jax.experimental import pallas as pl
from jax.experimental.pallas import tpu as pltpu

F32 = jnp.float32
BF16 = jnp.bfloat16
MESH = pl.DeviceIdType.MESH

D_MODEL = 1024
ML_HEADS = 4
ML_DQK = 128
ML_DV = 256
FOX_HEADS = 8
FOX_DH = 128
D_FF = 2816
GATE_CAP = 15.0
EPS = 1e-6
ADAM_LR, ADAM_B1, ADAM_B2, ADAM_EPS, ADAM_WD, ADAM_STEP = 0.001, 0.9, 0.999, 1e-08, 0.01, 10

LANES = 128
MLC = 128
FOX_TQ = 256
FOX_TK = 512
ROW_T = 512
VMEM_LIMIT = 56 * 1024 * 1024

C_QM, C_KM, C_VM, C_OM = 0, 512, 1024, 2048
N_ML, N_FOX, N_GATE = 3072, 3072, 2048
N_SMALL = 384


def _cparams(sem=None):
    return pltpu.CompilerParams(dimension_semantics=sem, vmem_limit_bytes=VMEM_LIMIT)


def _tile(n, target):
    if n <= target:
        return n
    best = None
    for t in range(LANES, target + 1, LANES):
        if n % t == 0:
            best = t
    assert best is not None, (n, target)
    return best


def _dot(a, b, dims):
    return lax.dot_general(a, b, (dims, ((), ())), preferred_element_type=F32)


def _dot_nn(a, b):
    return _dot(a, b, ((1,), (0,)))


def _dot_nt(a, b):
    return _dot(a, b, ((1,), (1,)))


def _dot_tn(a, b):
    return _dot(a, b, ((0,), (0,)))


_DOTS = {"nn": _dot_nn, "nt": _dot_nt, "tn": _dot_tn}


def _mm(a, b, mode, out_dtype, name, tm=1024, tn=1408, tk=1408):
    a_parts = list(a) if isinstance(a, (list, tuple)) else [a]
    b_parts = list(b) if isinstance(b, (list, tuple)) else [b]
    assert len(a_parts) == 1 or len(b_parts) == 1, name
    a_axes = {"nn": "ik", "nt": "ik", "tn": "ki"}[mode]
    b_axes = {"nn": "kj", "nt": "jk", "tn": "kj"}[mode]
    size, target = {}, dict(i=tm, j=tn, k=tk)
    for parts, axes in ((a_parts, a_axes), (b_parts, b_axes)):
        dims = (parts[0].shape[0], parts[0].shape[1] * len(parts))
        for ax, n in zip(axes, dims):
            assert size.setdefault(ax, n) == n, (name, ax, n, size)
    tile = {}
    for parts, axes in ((a_parts, a_axes), (b_parts, b_axes)):
        if len(parts) > 1:
            tile[axes[1]] = _tile(parts[0].shape[1], target[axes[1]])
    for ax in "ijk":
        tile.setdefault(ax, _tile(size[ax], target[ax]))
    M, N, nk = size["i"], size["j"], size["k"] // tile["k"]
    grid_pos = dict(i=0, j=1, k=2)
    dot = _DOTS[mode]

    def specs(parts, axes):
        blk = (tile[axes[0]], tile[axes[1]])
        if len(parts) == 1:
            return [pl.BlockSpec(blk, lambda *g: (g[grid_pos[axes[0]]], g[grid_pos[axes[1]]]))], None
        bpp = parts[0].shape[1] // blk[1]

        def index(p):
            def f(*g):
                g0, g1 = g[grid_pos[axes[0]]], g[grid_pos[axes[1]]]
                on = g1 // bpp == p
                return jnp.where(on, g0, 0), jnp.where(on, g1 % bpp, 0)
            return f

        return [pl.BlockSpec(blk, index(p)) for p in range(len(parts))], (axes[1], bpp)

    a_specs, a_sel = specs(a_parts, a_axes)
    b_specs, b_sel = specs(b_parts, b_axes)
    na, nb = len(a_parts), len(b_parts)

    def body(*refs):
        a_refs, b_refs, o_ref, acc = refs[:na], refs[na:na + nb], refs[na + nb], refs[na + nb + 1:]

        def accumulate(part):
            if nk == 1:
                o_ref[...] = part.astype(o_ref.dtype)
                return
            acc_ref, = acc
            k = pl.program_id(2)

            @pl.when(k == 0)
            def _():
                acc_ref[...] = part

            @pl.when(k > 0)
            def _():
                acc_ref[...] += part

            @pl.when(k == nk - 1)
            def _():
                o_ref[...] = acc_ref[...].astype(o_ref.dtype)

        sel = a_sel or b_sel
        if sel is None:
            accumulate(dot(a_refs[0][...], b_refs[0][...]))
        else:
            which = pl.program_id(grid_pos[sel[0]]) // sel[1]
            for p in range(max(na, nb)):
                @pl.when(which == p)
                def _(p=p):
                    accumulate(dot(a_refs[p if a_sel else 0][...], b_refs[p if b_sel else 0][...]))

    return pl.pallas_call(
        body, name=name,
        out_shape=jax.ShapeDtypeStruct((M, N), out_dtype),
        grid=(M // tile["i"], N // tile["j"], nk),
        in_specs=a_specs + b_specs,
        out_specs=pl.BlockSpec((tile["i"], tile["j"]), lambda i, j, k: (i, j)),
        scratch_shapes=[pltpu.VMEM((tile["i"], tile["j"]), F32)] if nk > 1 else [],
        compiler_params=_cparams(("parallel", "parallel", "arbitrary")),
    )(*a_parts, *b_parts)


def _rstd(x):
    return lax.rsqrt(jnp.mean(x * x, axis=-1, keepdims=True) + EPS)


def _rmsnorm_fwd(x, g, name):
    S, D = x.shape
    T = _tile(S, ROW_T)

    def body(x_ref, g_ref, o_ref):
        xv = x_ref[...]
        o_ref[...] = (xv * _rstd(xv) * g_ref[...]).astype(o_ref.dtype)

    return pl.pallas_call(
        body, name=name, out_shape=jax.ShapeDtypeStruct((S, D), BF16), grid=(S // T,),
        in_specs=[pl.BlockSpec((T, D), lambda i: (i, 0)), pl.BlockSpec((1, D), lambda i: (0, 0))],
        out_specs=pl.BlockSpec((T, D), lambda i: (i, 0)),
        compiler_params=_cparams(("parallel",)),
    )(x, g)


def _resid_norm_fwd(x, z, g, name):
    S, D = x.shape
    T = _tile(S, ROW_T)

    def body(x_ref, z_ref, g_ref, o_ref):
        zv = z_ref[...]
        o_ref[...] = x_ref[...] + zv * _rstd(zv) * g_ref[...]

    row = pl.BlockSpec((T, D), lambda i: (i, 0))
    return pl.pallas_call(
        body, name=name, out_shape=jax.ShapeDtypeStruct((S, D), F32), grid=(S // T,),
        in_specs=[row, row, pl.BlockSpec((1, D), lambda i: (0, 0))],
        out_specs=row, compiler_params=_cparams(("parallel",)),
    )(x, z, g)


def _rmsnorm_bwd_math(dy, xv, g):
    r = _rstd(xv)
    u = dy * g
    dx = r * u - xv * (r * r * r) * jnp.mean(u * xv, axis=-1, keepdims=True)
    return dx, dy * xv * r


def _rmsnorm_bwd(dys, xin, g, resid, out_dtype, name):
    S, D = xin.shape
    T = _tile(S, ROW_T)
    has_resid = resid is not None
    ndy = len(dys)

    def body(*refs):
        dy_refs, (x_ref, g_ref) = refs[:ndy], refs[ndy:ndy + 2]
        dx_ref, dg_ref = refs[-2:]
        dy = dy_refs[0][...]
        for r in dy_refs[1:]:
            dy = dy + r[...]
        dx, dgt = _rmsnorm_bwd_math(dy, x_ref[...], g_ref[...])
        if has_resid:
            dx = dx + refs[ndy + 2][...]
        dx_ref[...] = dx.astype(dx_ref.dtype)

        @pl.when(pl.program_id(0) == 0)
        def _():
            dg_ref[...] = jnp.zeros_like(dg_ref)

        dg_ref[...] += jnp.sum(dgt, axis=0, keepdims=True)

    row = pl.BlockSpec((T, D), lambda i: (i, 0))
    vec = pl.BlockSpec((1, D), lambda i: (0, 0))
    ins = list(dys) + [xin, g] + ([resid] if has_resid else [])
    return pl.pallas_call(
        body, name=name,
        out_shape=(jax.ShapeDtypeStruct((S, D), out_dtype), jax.ShapeDtypeStruct((1, D), F32)),
        grid=(S // T,), in_specs=[row] * ndy + [row, vec] + ([row] if has_resid else []),
        out_specs=(row, vec), compiler_params=_cparams(("arbitrary",)),
    )(*ins)


def _loss_head(x1, d, g, target, name):
    S, D = x1.shape
    T = _tile(S, ROW_T)

    def body(x_ref, d_ref, g_ref, t_ref, loss_ref, dy_ref, dd_ref, dg_ref):
        dv, gv = d_ref[...], g_ref[...]
        y = x_ref[...] + dv * _rstd(dv) * gv
        diff = y - t_ref[...]
        dy = diff * (1.0 / D)
        dy_ref[...] = dy
        dd, dgt = _rmsnorm_bwd_math(dy, dv, gv)
        dd_ref[...] = dd.astype(dd_ref.dtype)

        @pl.when(pl.program_id(0) == 0)
        def _():
            dg_ref[...] = jnp.zeros_like(dg_ref)
            loss_ref[...] = jnp.zeros_like(loss_ref)

        dg_ref[...] += jnp.sum(dgt, axis=0, keepdims=True)
        part = jnp.sum(jnp.sum(diff * diff, axis=1, keepdims=True), axis=0, keepdims=True)
        loss_ref[...] += (0.5 / D) * part

    row = pl.BlockSpec((T, D), lambda i: (i, 0))
    vec = pl.BlockSpec((1, D), lambda i: (0, 0))
    return pl.pallas_call(
        body, name=name,
        out_shape=(jax.ShapeDtypeStruct((1, LANES), F32), jax.ShapeDtypeStruct((S, D), F32),
                   jax.ShapeDtypeStruct((S, D), BF16), jax.ShapeDtypeStruct((1, D), F32)),
        grid=(S // T,), in_specs=[row, row, vec, row],
        out_specs=(pl.BlockSpec((1, LANES), lambda i: (0, 0)), row, row, vec),
        compiler_params=_cparams(("arbitrary",)),
    )(x1, d, g, target)


def _merge_fwd(ya, yb, pm, ba, bb, name):
    S, D = ya.shape
    T = _tile(S, ROW_T)

    def body(ya_ref, yb_ref, ga_ref, gb_ref, ba_ref, bb_ref, o_ref):
        sa = jax.nn.sigmoid(ga_ref[...] + ba_ref[...])
        sb = jax.nn.sigmoid(gb_ref[...] + bb_ref[...])
        o_ref[...] = (sa * ya_ref[...] + sb * yb_ref[...]).astype(o_ref.dtype)

    row = pl.BlockSpec((T, D), lambda i: (i, 0))
    vec = pl.BlockSpec((1, D), lambda i: (0, 0))
    return pl.pallas_call(
        body, name=name, out_shape=jax.ShapeDtypeStruct((S, D), BF16), grid=(S // T,),
        in_specs=[row, row, pl.BlockSpec((T, D), lambda i: (i, 0)),
                  pl.BlockSpec((T, D), lambda i: (i, 1)), vec, vec],
        out_specs=row, compiler_params=_cparams(("parallel",)),
    )(ya, yb, pm, pm, ba, bb)


def _merge_bwd(dmerged, ya, yb, pm, ba, bb, name):
    S, D = ya.shape
    T = _tile(S, ROW_T)

    def body(dm_ref, ya_ref, yb_ref, ga_ref, gb_ref, ba_ref, bb_ref,
             dya_ref, dyb_ref, dga_ref, dgb_ref, dba_ref, dbb_ref):
        dm = dm_ref[...]
        sa = jax.nn.sigmoid(ga_ref[...] + ba_ref[...])
        sb = jax.nn.sigmoid(gb_ref[...] + bb_ref[...])
        dya_ref[...] = (dm * sa).astype(dya_ref.dtype)
        dyb_ref[...] = (dm * sb).astype(dyb_ref.dtype)
        dga = dm * ya_ref[...] * sa * (1.0 - sa)
        dgb = dm * yb_ref[...] * sb * (1.0 - sb)
        dga_ref[...] = dga.astype(dga_ref.dtype)
        dgb_ref[...] = dgb.astype(dgb_ref.dtype)

        @pl.when(pl.program_id(0) == 0)
        def _():
            dba_ref[...] = jnp.zeros_like(dba_ref)
            dbb_ref[...] = jnp.zeros_like(dbb_ref)

        dba_ref[...] += jnp.sum(dga, axis=0, keepdims=True)
        dbb_ref[...] += jnp.sum(dgb, axis=0, keepdims=True)

    row = pl.BlockSpec((T, D), lambda i: (i, 0))
    vec = pl.BlockSpec((1, D), lambda i: (0, 0))
    act = jax.ShapeDtypeStruct((S, D), BF16)
    v1 = jax.ShapeDtypeStruct((1, D), F32)
    return pl.pallas_call(
        body, name=name, out_shape=(act, act, act, act, v1, v1), grid=(S // T,),
        in_specs=[row, row, row, pl.BlockSpec((T, D), lambda i: (i, 0)),
                  pl.BlockSpec((T, D), lambda i: (i, 1)), vec, vec],
        out_specs=(row, row, row, row, vec, vec), compiler_params=_cparams(("arbitrary",)),
    )(dmerged, ya, yb, pm, pm, ba, bb)


_GELU_C = math.sqrt(2.0 / math.pi)


def _gelu(g):
    t = jnp.tanh(_GELU_C * (g + 0.044715 * g * g * g))
    return 0.5 * g * (1.0 + t), t


def _gelu_grad(g, t):
    return 0.5 * (1.0 + t) + 0.5 * g * (1.0 - t * t) * _GELU_C * (1.0 + 3 * 0.044715 * g * g)


def _shift_down(v, halo_ref, first, rows):
    T = v.shape[0]
    keep = jnp.where(first, 0.0, 1.0)
    h7 = halo_ref[7:8, :] * keep
    h6 = halo_ref[6:7, :] * keep
    m1 = jnp.where(rows == 0, h7, pltpu.roll(v, 1, 0))
    m2 = jnp.where(rows == 0, h6, jnp.where(rows == 1, h7, pltpu.roll(v, 2, 0)))
    return m1, m2


def _conv_act_fwd(up, cw, cb, name):
    S, F2 = up.shape
    Fh = F2 // 2
    T = _tile(S, ROW_T)
    tc = _tile(Fh, 256)
    ncol = Fh // tc
    hb = T // 8

    def body(ua_ref, ug_ref, ha_ref, hg_ref, wa_ref, wg_ref, ba_ref, bg_ref, o_ref):
        first = pl.program_id(0) == 0
        rows = lax.broadcasted_iota(jnp.int32, (T, tc), 0)

        def conv(u_ref, h_ref, w_ref, b_ref):
            v = u_ref[...]
            m1, m2 = _shift_down(v, h_ref, first, rows)
            return b_ref[...] + w_ref[0:1, :] * m2 + w_ref[1:2, :] * m1 + w_ref[2:3, :] * v

        a = conv(ua_ref, ha_ref, wa_ref, ba_ref)
        g = conv(ug_ref, hg_ref, wg_ref, bg_ref)
        o_ref[...] = (_gelu(g)[0] * a).astype(o_ref.dtype)

    halo = lambda off: pl.BlockSpec((8, tc), lambda i, j: (jnp.maximum(i * hb - 1, 0), j + off))
    return pl.pallas_call(
        body, name=name, out_shape=jax.ShapeDtypeStruct((S, Fh), BF16), grid=(S // T, ncol),
        in_specs=[pl.BlockSpec((T, tc), lambda i, j: (i, j)), pl.BlockSpec((T, tc), lambda i, j: (i, j + ncol)),
                  halo(0), halo(ncol),
                  pl.BlockSpec((3, tc), lambda i, j: (0, j)), pl.BlockSpec((3, tc), lambda i, j: (0, j + ncol)),
                  pl.BlockSpec((1, tc), lambda i, j: (0, j)), pl.BlockSpec((1, tc), lambda i, j: (0, j + ncol))],
        out_specs=pl.BlockSpec((T, tc), lambda i, j: (i, j)),
        compiler_params=_cparams(("parallel", "parallel")),
    )(up, up, up, up, cw, cw, cb, cb)


def _conv_act_bwd(up, dact, cw, cb, name):
    S, F2 = up.shape
    Fh = F2 // 2
    T = _tile(S, ROW_T)
    tc = _tile(Fh, 256)
    ncol, nrow, hb, nhb = Fh // tc, S // T, T // 8, S // 8

    def body(ua_ref, ug_ref, ha_ref, hg_ref, na_ref, ng_ref, wa_ref, wg_ref, ba_ref, bg_ref, da_ref, dn_ref,
             dpa_ref, dpg_ref, dwa_ref, dwg_ref, dba_ref, dbg_ref, dua_n, dug_n):
        i = pl.program_id(1)
        first = i == 0
        rows = lax.broadcasted_iota(jnp.int32, (T, tc), 0)
        rows8 = lax.broadcasted_iota(jnp.int32, (8, tc), 0)

        def conv(v, m1, m2, w_ref, b_ref):
            return b_ref[...] + w_ref[0:1, :] * m2 + w_ref[1:2, :] * m1 + w_ref[2:3, :] * v

        def du_of(a, g, dact_v):
            gel, t = _gelu(g)
            return dact_v * gel, dact_v * a * _gelu_grad(g, t)

        va, vg = ua_ref[...], ug_ref[...]
        a1, a2 = _shift_down(va, ha_ref, first, rows)
        g1, g2 = _shift_down(vg, hg_ref, first, rows)
        dua, dug = du_of(conv(va, a1, a2, wa_ref, ba_ref), conv(vg, g1, g2, wg_ref, bg_ref), da_ref[...])

        @pl.when(first)
        def _():
            for r in (dwa_ref, dwg_ref, dba_ref, dbg_ref):
                r[...] = jnp.zeros_like(r)

        for du, taps, dw_ref, db_ref in ((dua, (a2, a1, va), dwa_ref, dba_ref), (dug, (g2, g1, vg), dwg_ref, dbg_ref)):
            db_ref[...] += jnp.sum(du, axis=0, keepdims=True)
            for j in range(3):
                dw_ref[j:j + 1, :] += jnp.sum(du * taps[j], axis=0, keepdims=True)

        def below(n_ref, u_ref):
            v = n_ref[...]
            l1, l2 = u_ref[T - 1:T, :], u_ref[T - 2:T - 1, :]
            m1 = jnp.where(rows8 == 0, l1, pltpu.roll(v, 1, 0))
            m2 = jnp.where(rows8 == 0, l2, jnp.where(rows8 == 1, l1, pltpu.roll(v, 2, 0)))
            return v, m1, m2

        keep = jnp.where(i == nrow - 1, 0.0, 1.0)
        na, ng = below(na_ref, ua_ref), below(ng_ref, ug_ref)
        dna, dng = du_of(conv(*na, wa_ref, ba_ref), conv(*ng, wg_ref, bg_ref), dn_ref[...] * keep)
        dua_n[...] = dna
        dug_n[...] = dng

        for du, n_ref, w_ref, o_ref in ((dua, dua_n, wa_ref, dpa_ref), (dug, dug_n, wg_ref, dpg_ref)):
            n0, n1 = n_ref[0:1, :], n_ref[1:2, :]
            p1 = jnp.where(rows == T - 1, n0, pltpu.roll(du, T - 1, 0))
            p2 = jnp.where(rows == T - 2, n0, jnp.where(rows == T - 1, n1, pltpu.roll(du, T - 2, 0)))
            o_ref[...] = (w_ref[2:3, :] * du + w_ref[1:2, :] * p1 + w_ref[0:1, :] * p2).astype(o_ref.dtype)

    tile = lambda off: pl.BlockSpec((T, tc), lambda j, i: (i, j + off))
    above = lambda off: pl.BlockSpec((8, tc), lambda j, i: (jnp.maximum(i * hb - 1, 0), j + off))
    under = lambda off: pl.BlockSpec((8, tc), lambda j, i: (jnp.minimum((i + 1) * hb, nhb - 1), j + off))
    vec = lambda n, off: pl.BlockSpec((n, tc), lambda j, i: (0, j + off))
    act = jax.ShapeDtypeStruct((S, Fh), BF16)
    return pl.pallas_call(
        body, name=name,
        out_shape=(act, act, jax.ShapeDtypeStruct((3, Fh), F32), jax.ShapeDtypeStruct((3, Fh), F32),
                   jax.ShapeDtypeStruct((1, Fh), F32), jax.ShapeDtypeStruct((1, Fh), F32)),
        grid=(ncol, nrow),
        in_specs=[tile(0), tile(ncol), above(0), above(ncol), under(0), under(ncol),
                  vec(3, 0), vec(3, ncol), vec(1, 0), vec(1, ncol), tile(0), under(0)],
        out_specs=(tile(0), tile(0), vec(3, 0), vec(3, 0), vec(1, 0), vec(1, 0)),
        scratch_shapes=[pltpu.VMEM((8, tc), F32), pltpu.VMEM((8, tc), F32)],
        compiler_params=_cparams(("parallel", "arbitrary")),
    )(up, up, up, up, up, up, cw, cw, cb, cb, dact, dact)


def _split3(x):
    hi = x.astype(BF16)
    r1 = x - hi.astype(F32)
    mid = r1.astype(BF16)
    lo = (r1 - mid.astype(F32)).astype(BF16)
    return hi, mid, lo


def _tri_dot(tri, x):
    hi, mid, lo = _split3(x)
    return _dot_nn(tri, hi) + _dot_nn(tri, mid) + _dot_nn(tri, lo)


def _log_sigmoid(x):
    return jnp.minimum(x, 0.0) - jnp.log(1.0 + jnp.exp(-jnp.abs(x)))


def _tri_mask(n, lower):
    r = lax.broadcasted_iota(jnp.int32, (n, n), 0)
    c = lax.broadcasted_iota(jnp.int32, (n, n), 1)
    return (r >= c) if lower else (r <= c)


def _gates_fwd(ps, bi, bf, bff, name):
    S = ps.shape[0]
    NC = S // MLC

    def body(ps_ref, bi_ref, bf_ref, bff_ref, a_ref, A_ref, wi_ref, em_ref, wk_ref, dec_ref, F_ref, m_scr, f_scr):
        @pl.when(pl.program_id(0) == 0)
        def _():
            m_scr[...] = jnp.zeros_like(m_scr)
            f_scr[...] = jnp.zeros_like(f_scr)

        rows = lax.broadcasted_iota(jnp.int32, (MLC, LANES), 0)
        ltri = _tri_mask(MLC, True).astype(BF16)
        li = GATE_CAP * jnp.tanh((ps_ref[:, 0:LANES] + bi_ref[...]) / GATE_CAP)
        lf = _log_sigmoid(GATE_CAP * jnp.tanh((ps_ref[:, LANES:2 * LANES] + bf_ref[...]) / GATE_CAP))
        b = _tri_dot(ltri, lf)
        a = li - b
        cm = a
        sh = 1
        while sh < MLC:
            cm = jnp.where(rows >= sh, jnp.maximum(cm, pltpu.roll(cm, sh, 0)), cm)
            sh *= 2
        m0 = m_scr[...]
        A = jnp.maximum(cm, m0)
        a_ref[...] = a
        A_ref[...] = A
        A_last = A_ref[MLC - 1:MLC, :]
        wi_ref[...] = jnp.exp(m0 - A)
        em_ref[...] = jnp.exp(-(b + A))
        wk_ref[...] = jnp.exp(a - A_last)
        dec_ref[0] = jnp.exp(m0 - A_last)
        F_ref[...] = b
        m_scr[...] = F_ref[MLC - 1:MLC, :] + A_last
        lfg = _log_sigmoid(ps_ref[:, 2 * LANES:3 * LANES] + bff_ref[...])
        F_ref[...] = _tri_dot(ltri, lfg) + f_scr[...]
        f_scr[...] = F_ref[MLC - 1:MLC, :]

    col = pl.BlockSpec((MLC, LANES), lambda c: (c, 0))
    vec = pl.BlockSpec((1, LANES), lambda c: (0, 0))
    cs = jax.ShapeDtypeStruct((S, LANES), F32)
    return pl.pallas_call(
        body, name=name,
        out_shape=(cs, cs, cs, cs, cs, jax.ShapeDtypeStruct((NC, 1, LANES), F32), cs),
        grid=(NC,), in_specs=[pl.BlockSpec((MLC, N_SMALL), lambda c: (c, 0)), vec, vec, vec],
        out_specs=(col, col, col, col, col, pl.BlockSpec((1, 1, LANES), lambda c: (c, 0, 0)), col),
        scratch_shapes=[pltpu.VMEM((1, LANES), F32), pltpu.VMEM((1, LANES), F32)],
        compiler_params=_cparams(("arbitrary",)),
    )(ps, bi, bf, bff)


def _gates_bwd(ps, bi, bf, bff, rk, kc, tch, dF, name):
    S = ps.shape[0]
    NC = S // MLC

    def body(ps_ref, bi_ref, bf_ref, bff_ref, rk_ref, kc_ref, t_ref, dF_ref, dps_ref, db_ref, carry):
        @pl.when(pl.program_id(0) == 0)
        def _():
            carry[...] = jnp.zeros_like(carry)
            db_ref[...] = jnp.zeros_like(db_ref)

        lanes = lax.broadcasted_iota(jnp.int32, (MLC, LANES), 1)
        utri = _tri_mask(MLC, False).astype(BF16)
        ti = jnp.tanh((ps_ref[:, 0:LANES] + bi_ref[...]) / GATE_CAP)
        t_end, t_start = t_ref[0, 0:1, :], t_ref[0, 1:2, :]
        rk = rk_ref[...]
        rk = rk - (jnp.sum(rk, axis=0, keepdims=True) - (t_start - t_end)) * (1.0 / MLC)
        dpi = jnp.where(lanes < ML_HEADS, (kc_ref[...] - rk) * (1.0 - ti * ti), 0.0)
        tf = jnp.tanh((ps_ref[:, LANES:2 * LANES] + bf_ref[...]) / GATE_CAP)
        dlf = _tri_dot(utri, rk) + t_end
        dpf = jnp.where(lanes < ML_HEADS, dlf * jax.nn.sigmoid(-GATE_CAP * tf) * (1.0 - tf * tf), 0.0)
        dFv = dF_ref[...]
        dlfg = _tri_dot(utri, dFv) + carry[...]
        carry[...] += jnp.sum(dFv, axis=0, keepdims=True)
        dpff = jnp.where(lanes < FOX_HEADS, dlfg * jax.nn.sigmoid(-(ps_ref[:, 2 * LANES:3 * LANES] + bff_ref[...])), 0.0)
        for n, dp in enumerate((dpi, dpf, dpff)):
            dps_ref[:, n * LANES:(n + 1) * LANES] = dp.astype(dps_ref.dtype)
            db_ref[:, n * LANES:(n + 1) * LANES] += jnp.sum(dp, axis=0, keepdims=True)

    rev = lambda c: (NC - 1 - c, 0)
    col = pl.BlockSpec((MLC, LANES), rev)
    vec = pl.BlockSpec((1, LANES), lambda c: (0, 0))
    wide = pl.BlockSpec((MLC, N_SMALL), rev)
    return pl.pallas_call(
        body, name=name,
        out_shape=(jax.ShapeDtypeStruct((S, N_SMALL), BF16), jax.ShapeDtypeStruct((1, N_SMALL), F32)),
        grid=(NC,),
        in_specs=[wide, vec, vec, vec, col, col, pl.BlockSpec((1, 2, LANES), lambda c: (NC - 1 - c, 0, 0)), col],
        out_specs=(wide, pl.BlockSpec((1, N_SMALL), lambda c: (0, 0))),
        scratch_shapes=[pltpu.VMEM((1, LANES), F32)],
        compiler_params=_cparams(("arbitrary",)),
    )(ps, bi, bf, bff, rk, kc, tch, dF)


_ML_SCALE = ML_DQK ** -0.5


def _ml_specs(rev, NC):
    idx = (lambda c: NC - 1 - c) if rev else (lambda c: c)
    qk = lambda blk: pl.BlockSpec((MLC, ML_HEADS * ML_DQK), lambda c: (idx(c), blk))
    wide = lambda blk: pl.BlockSpec((MLC, D_MODEL), lambda c: (idx(c), blk))
    col = pl.BlockSpec((MLC, LANES), lambda c: (idx(c), 0))
    return idx, qk, wide, col


def _ml_intra(q_ref, k_ref, arow_ref, A_ref, h):
    hs = slice(h * ML_DQK, (h + 1) * ML_DQK)
    qf = q_ref[:, hs] * _ML_SCALE
    kf = k_ref[:, hs]
    qb, kb = qf.astype(BF16), kf.astype(BF16)
    qk = _dot_nt(qb, kb)
    logw = arow_ref[h:h + 1, :] - A_ref[:, h:h + 1]
    W = jnp.exp(jnp.where(_tri_mask(MLC, True), logw, -1e30))
    return qb, kb, qf, kf, qk, W


def _mlstm_fwd(pm, a_row, A, wi, em, wk, dec, w_hn, name):
    S = pm.shape[0]
    NC = S // MLC
    _, qk, wide, col = _ml_specs(False, NC)

    def body(q_ref, k_ref, v_ref, o_ref, arow_ref, A_ref, wi_ref, em_ref, wk_ref, dec_ref, whn_ref,
             ha_ref, hp_ref, den_ref, cst_ref, nst_ref, C_scr, n_scr):
        @pl.when(pl.program_id(0) == 0)
        def _():
            C_scr[...] = jnp.zeros_like(C_scr)
            n_scr[...] = jnp.zeros_like(n_scr)

        lanes = lax.broadcasted_iota(jnp.int32, (MLC, LANES), 1)
        den_tile = jnp.zeros((MLC, LANES), F32)
        for h in range(ML_HEADS):
            vs = slice(h * ML_DV, (h + 1) * ML_DV)
            qb, kb, qf, kf, qk_, W = _ml_intra(q_ref, k_ref, arow_ref, A_ref, h)
            vb = v_ref[:, vs].astype(BF16)
            Cf = C_scr[h]
            Cb = Cf.astype(BF16)
            nrow = n_scr[h]
            cst_ref[0, h] = Cb
            nst_ref[0, h] = nrow
            s = qk_ * W
            wic = wi_ref[:, h:h + 1]
            num = _dot_nn(s.astype(BF16), vb) + wic * _dot_nt(qb, Cb)
            den = jnp.sum(s, axis=1, keepdims=True) + wic * jnp.sum(qf * nrow, axis=1, keepdims=True)
            hp = num / jnp.maximum(jnp.abs(den), em_ref[:, h:h + 1])
            hp_ref[:, vs] = hp
            den_tile = jnp.where(lanes == h, den, den_tile)
            hn = hp * _rstd(hp) * whn_ref[:, vs]
            ha_ref[:, vs] = (hn * jax.nn.sigmoid(o_ref[:, vs])).astype(ha_ref.dtype)
            wkc = wk_ref[:, h:h + 1]
            kw = kf * wkc
            d = dec_ref[0, :, h:h + 1]
            C_scr[h] = d * Cf + _dot_tn(vb, kw.astype(BF16))
            n_scr[h] = d * nrow + jnp.sum(kw, axis=0, keepdims=True)
        den_ref[...] = den_tile

    return pl.pallas_call(
        body, name=name,
        out_shape=(jax.ShapeDtypeStruct((S, D_MODEL), BF16), jax.ShapeDtypeStruct((S, D_MODEL), F32),
                   jax.ShapeDtypeStruct((S, LANES), F32),
                   jax.ShapeDtypeStruct((NC, ML_HEADS, ML_DV, ML_DQK), BF16),
                   jax.ShapeDtypeStruct((NC, ML_HEADS, 1, ML_DQK), F32)),
        grid=(NC,),
        in_specs=[qk(C_QM // 512), qk(C_KM // 512), wide(C_VM // D_MODEL), wide(C_OM // D_MODEL),
                  pl.BlockSpec((8, MLC), lambda c: (0, c)), col, col, col, col,
                  pl.BlockSpec((1, 1, LANES), lambda c: (c, 0, 0)), pl.BlockSpec((1, D_MODEL), lambda c: (0, 0))],
        out_specs=(pl.BlockSpec((MLC, D_MODEL), lambda c: (c, 0)), pl.BlockSpec((MLC, D_MODEL), lambda c: (c, 0)),
                   col, pl.BlockSpec((1, ML_HEADS, ML_DV, ML_DQK), lambda c: (c, 0, 0, 0)),
                   pl.BlockSpec((1, ML_HEADS, 1, ML_DQK), lambda c: (c, 0, 0, 0))),
        scratch_shapes=[pltpu.VMEM((ML_HEADS, ML_DV, ML_DQK), F32), pltpu.VMEM((ML_HEADS, 1, ML_DQK), F32)],
        compiler_params=_cparams(("arbitrary",)),
    )(pm, pm, pm, pm, a_row, A, wi, em, wk, dec, w_hn)


def _mlstm_bwd(dha, pm, hp_all, den_all, a_row, A, wi, em, wk, dec, cst, nst, w_hn, name):
    S = pm.shape[0]
    NC = S // MLC
    idx, qk, wide, col = _ml_specs(True, NC)

    def body(dha_ref, q_ref, k_ref, v_ref, o_ref, hp_ref, den_ref, arow_ref, A_ref, wi_ref, em_ref, wk_ref,
             dec_ref, cst_ref, nst_ref, whn_ref,
             dqk_ref, dv_ref, do_ref, rk_ref, kc_ref, t_ref, dwhn_ref, dC_scr, dn_scr, t_scr):
        @pl.when(pl.program_id(0) == 0)
        def _():
            dC_scr[...] = jnp.zeros_like(dC_scr)
            dn_scr[...] = jnp.zeros_like(dn_scr)
            t_scr[...] = jnp.zeros_like(t_scr)
            dwhn_ref[...] = jnp.zeros_like(dwhn_ref)

        lanes = lax.broadcasted_iota(jnp.int32, (MLC, LANES), 1)
        lane1 = lax.broadcasted_iota(jnp.int32, (1, LANES), 1)
        t_ref[0, 0:1, :] = t_scr[...]
        rk_tile = jnp.zeros((MLC, LANES), F32)
        kc_tile = jnp.zeros((MLC, LANES), F32)
        t_new = jnp.zeros((1, LANES), F32)
        for h in range(ML_HEADS):
            hs = slice(h * ML_DQK, (h + 1) * ML_DQK)
            vs = slice(h * ML_DV, (h + 1) * ML_DV)
            hp = hp_ref[:, vs]
            sig = jax.nn.sigmoid(o_ref[:, vs])
            whn = whn_ref[:, vs]
            r = _rstd(hp)
            dga = dha_ref[:, vs]
            do_ref[:, vs] = (dga * (hp * r * whn) * sig * (1.0 - sig)).astype(do_ref.dtype)
            dhn = dga * sig
            dhp, dwt = _rmsnorm_bwd_math(dhn, hp, whn)
            dwhn_ref[:, vs] += jnp.sum(dwt, axis=0, keepdims=True)
            den = den_ref[:, h:h + 1]
            floor = em_ref[:, h:h + 1]
            D = jnp.maximum(jnp.abs(den), floor)
            dnum = dhp / D
            dh_h = jnp.sum(dhp * hp, axis=1, keepdims=True)
            active = jnp.abs(den) >= floor
            dden = -dh_h / D * jnp.where(active, jnp.sign(den), 0.0)
            phi = jnp.where(active, 0.0, dh_h)
            qb, kb, qf, kf, qk_, W = _ml_intra(q_ref, k_ref, arow_ref, A_ref, h)
            vf = v_ref[:, vs]
            vb = vf.astype(BF16)
            Cb = cst_ref[0, h]
            nrow = nst_ref[0, h]
            wic = wi_ref[:, h:h + 1]
            wkc = wk_ref[:, h:h + 1]
            d = dec_ref[0, :, h:h + 1]
            dCn = dC_scr[h]
            dCb = dCn.astype(BF16)
            dnn = dn_scr[h]
            dnumb = dnum.astype(BF16)
            s = qk_ * W
            ds = (_dot_nt(dnumb, vb) + dden) * W
            dsb = ds.astype(BF16)
            dnw = (wic * dnum).astype(BF16)
            wd = wic * dden
            kw = kf * wkc
            dv_state = _dot_nt(kw.astype(BF16), dCb)
            dq = _dot_nn(dsb, kb) + _dot_nn(dnw, Cb) + wd * nrow
            dk_state = wkc * (_dot_nn(vb, dCb) + dnn)
            dk = _dot_tn(dsb, qb) + dk_state
            dv = _dot_tn(s.astype(BF16), dnumb) + dv_state
            dC = d * dCn + _dot_tn(dnw, qb)
            dn = d * dnn + jnp.sum(wd * qf, axis=0, keepdims=True)
            dC_scr[h] = dC
            dn_scr[h] = dn
            dqk_ref[:, hs] = (dq * _ML_SCALE).astype(dqk_ref.dtype)
            dqk_ref[:, C_KM + h * ML_DQK:C_KM + (h + 1) * ML_DQK] = dk.astype(dqk_ref.dtype)
            dv_ref[:, vs] = dv.astype(dv_ref.dtype)
            G = ds * qk_
            inter = _dot_nt(qb, Cb)
            qn = jnp.sum(qf * nrow, axis=1, keepdims=True)
            R = (jnp.sum(G, axis=1, keepdims=True)
                 + wic * (jnp.sum(dnum * inter, axis=1, keepdims=True) + dden * qn))
            K = jnp.sum(G.T, axis=1, keepdims=True) + jnp.sum(kf * dk_state, axis=1, keepdims=True)
            rk_tile = jnp.where(lanes == h, R - K, rk_tile)
            kc_tile = jnp.where(lanes == h, phi, kc_tile)
            tt = (jnp.sum(jnp.sum(dC * Cb.astype(F32), axis=1, keepdims=True), axis=0, keepdims=True)
                  + jnp.sum(dn * nrow, axis=1, keepdims=True))
            t_new = jnp.where(lane1 == h, tt, t_new)
        rk_ref[...] = rk_tile
        kc_ref[...] = kc_tile
        t_ref[0, 1:2, :] = t_new
        t_scr[...] = t_new

    act = lambda n: jax.ShapeDtypeStruct((S, n), BF16)
    cs = jax.ShapeDtypeStruct((S, LANES), F32)
    rowblk = lambda n: pl.BlockSpec((MLC, n), lambda c: (idx(c), 0))
    return pl.pallas_call(
        body, name=name,
        out_shape=(act(D_MODEL), act(D_MODEL), act(D_MODEL), cs, cs,
                   jax.ShapeDtypeStruct((NC, 2, LANES), F32), jax.ShapeDtypeStruct((1, D_MODEL), F32)),
        grid=(NC,),
        in_specs=[rowblk(D_MODEL), qk(C_QM // 512), qk(C_KM // 512), wide(C_VM // D_MODEL), wide(C_OM // D_MODEL),
                  rowblk(D_MODEL), col, pl.BlockSpec((8, MLC), lambda c: (0, idx(c))), col, col, col, col,
                  pl.BlockSpec((1, 1, LANES), lambda c: (idx(c), 0, 0)),
                  pl.BlockSpec((1, ML_HEADS, ML_DV, ML_DQK), lambda c: (idx(c), 0, 0, 0)),
                  pl.BlockSpec((1, ML_HEADS, 1, ML_DQK), lambda c: (idx(c), 0, 0, 0)),
                  pl.BlockSpec((1, D_MODEL), lambda c: (0, 0))],
        out_specs=(rowblk(D_MODEL), rowblk(D_MODEL), rowblk(D_MODEL), col, col,
                   pl.BlockSpec((1, 2, LANES), lambda c: (idx(c), 0, 0)), pl.BlockSpec((1, D_MODEL), lambda c: (0, 0))),
        scratch_shapes=[pltpu.VMEM((ML_HEADS, ML_DV, ML_DQK), F32), pltpu.VMEM((ML_HEADS, 1, ML_DQK), F32),
                        pltpu.VMEM((1, LANES), F32)],
        compiler_params=_cparams(("arbitrary",)),
    )(dha, pm, pm, pm, pm, hp_all, den_all, a_row, A, wi, em, wk, dec, cst, nst, w_hn)


_FOX_SCALE = FOX_DH ** -0.5
_NEG = -1e30
_LOG2E = 1.4426950408889634
_LN2 = 0.6931471805599453
_QF_BLK, _KF_BLK, _VF_BLK = 0, FOX_HEADS, 2 * FOX_HEADS


def _lane_pick(tile, lane):
    lanes = lax.broadcasted_iota(jnp.int32, tile.shape, 1)
    return jnp.sum(jnp.where(lanes == lane, tile, 0.0), axis=1, keepdims=True)


def _col_to_row(col):
    return jnp.max(jnp.broadcast_to(col, (col.shape[0], LANES)).T, axis=0, keepdims=True)


def _causal(q0, k0, shape, q_axis):
    qpos = q0 + lax.broadcasted_iota(jnp.int32, shape, q_axis)
    kpos = k0 + lax.broadcasted_iota(jnp.int32, shape, 1 - q_axis)
    return kpos <= qpos


def _fox_fwd(pf, fc, fk_row, name):
    S = pf.shape[0]
    TQ, TK = FOX_TQ, FOX_TK
    nq, nk = S // TQ, S // TK
    c1 = _FOX_SCALE * _LOG2E

    def body(q_ref, k_ref, v_ref, fc_ref, fr_ref, o_ref, lse_ref):
        h, i = pl.program_id(0), pl.program_id(1)
        qb = q_ref[...]
        fq2 = _lane_pick(fc_ref[...], h) * _LOG2E

        def step(j, carry, masked):
            m, l, acc = carry
            off = pl.multiple_of(j * TK, TK)
            t = _dot_nt(qb, k_ref[pl.ds(off, TK), :]) * c1 - fr_ref[0, j] * _LOG2E
            if masked:
                t = jnp.where(_causal(i * TQ, j * TK, (TQ, TK), 0), t, _NEG)
            m_new = jnp.maximum(m, jnp.max(t, axis=1, keepdims=True) + fq2)
            alpha = jnp.exp2(m - m_new)
            p = jnp.exp2(t + (fq2 - m_new))
            l = alpha * l + jnp.sum(p, axis=1, keepdims=True)
            acc = alpha * acc + _dot_nn(p.astype(BF16), v_ref[pl.ds(off, TK), :])
            return m_new, l, acc

        init = (jnp.full((TQ, 1), _NEG, F32), jnp.zeros((TQ, 1), F32), jnp.zeros((TQ, FOX_DH), F32))
        last = (i * TQ) // TK
        carry = lax.fori_loop(0, last, lambda j, c: step(j, c, False), init)
        m, l, acc = step(last, carry, True)
        o_ref[...] = (acc / l).astype(o_ref.dtype)
        lse_ref[0, 0] = _col_to_row((m + jnp.log2(l)) * _LN2)

    head = lambda blk: pl.BlockSpec((S, FOX_DH), lambda h, i: (0, blk + h))
    return pl.pallas_call(
        body, name=name,
        out_shape=(jax.ShapeDtypeStruct((S, D_MODEL), BF16), jax.ShapeDtypeStruct((FOX_HEADS, nq, 1, TQ), F32)),
        grid=(FOX_HEADS, nq),
        in_specs=[pl.BlockSpec((TQ, FOX_DH), lambda h, i: (i, _QF_BLK + h)), head(_KF_BLK), head(_VF_BLK),
                  pl.BlockSpec((TQ, LANES), lambda h, i: (i, 0)),
                  pl.BlockSpec((1, nk, 1, TK), lambda h, i: (h, 0, 0, 0))],
        out_specs=(pl.BlockSpec((TQ, FOX_DH), lambda h, i: (i, h)),
                   pl.BlockSpec((1, 1, 1, TQ), lambda h, i: (h, i, 0, 0))),
        compiler_params=_cparams(("parallel", "arbitrary")),
    )(pf, pf, pf, fc, fk_row)


def _fox_bwd(dhb, hb, pf, lse_row, fq_row, fc, name):
    S = pf.shape[0]
    TQ, TK = FOX_TQ, FOX_TK
    nq, nk, r = S // TQ, S // TK, TK // TQ
    c1 = _FOX_SCALE * _LOG2E

    def body(q_ref, k_ref, v_ref, do_ref, o_ref, lse_ref, fq_ref, fc_ref,
             dq_ref, dk_ref, dv_ref, dFk_ref, dFq_ref, dq_acc, qside, delta, dk_acc, dv_acc, cs_acc):
        h, j = pl.program_id(0), pl.program_id(1)

        @pl.when(j == 0)
        def _():
            dq_acc[...] = jnp.zeros_like(dq_acc)
            dFq_ref[...] = jnp.zeros_like(dFq_ref)

            def fill(b, _):
                off = pl.multiple_of(b * TQ, TQ)
                prod = do_ref[pl.ds(off, TQ), :].astype(F32) * o_ref[pl.ds(off, TQ), :].astype(F32)
                delta[b] = jnp.sum(prod.T, axis=0, keepdims=True)
                qside[b] = (fq_ref[0, b] - lse_ref[0, b]) * _LOG2E
                return 0

            lax.fori_loop(0, nq, fill, 0)

        kb = k_ref[...]
        vb = v_ref[...]
        fk2 = _lane_pick(fc_ref[...], h) * _LOG2E
        dk_acc[...] = jnp.zeros_like(dk_acc)
        dv_acc[...] = jnp.zeros_like(dv_acc)
        cs_acc[...] = jnp.zeros_like(cs_acc)

        def step(i, masked):
            off = pl.multiple_of(i * TQ, TQ)
            qb = q_ref[pl.ds(off, TQ), :]
            dob = do_ref[pl.ds(off, TQ), :]
            t = _dot_nt(kb, qb) * c1 + qside[i] - fk2
            if masked:
                t = jnp.where(_causal(i * TQ, j * TK, (TK, TQ), 1), t, _NEG)
            p = jnp.exp2(t)
            dv_acc[...] += _dot_nn(p.astype(BF16), dob)
            ds = p * (_dot_nt(vb, dob) - delta[i])
            dsb = ds.astype(BF16)
            dk_acc[...] += _dot_nn(dsb, qb)
            dq_acc[pl.ds(off, TQ), :] += _dot_tn(dsb, kb)
            cs_acc[...] += jnp.sum(ds, axis=1, keepdims=True)
            dFq_ref[0, i] += jnp.sum(ds, axis=0, keepdims=True)

        for d in range(r):
            step(r * j + d, True)

        def rest(i, _):
            step(i, False)
            return 0

        lax.fori_loop(r * j + r, nq, rest, 0)
        dk_ref[...] = (dk_acc[...] * _FOX_SCALE).astype(dk_ref.dtype)
        dv_ref[...] = dv_acc[...].astype(dv_ref.dtype)
        dFk_ref[0, 0] = -_col_to_row(cs_acc[...])

        @pl.when(j == nk - 1)
        def _():
            dq_ref[...] = (dq_acc[...] * _FOX_SCALE).astype(dq_ref.dtype)

    head = lambda blk: pl.BlockSpec((S, FOX_DH), lambda h, j: (0, blk + h))
    kblk = lambda blk: pl.BlockSpec((TK, FOX_DH), lambda h, j: (j, blk + h))
    qrows = pl.BlockSpec((1, nq, 1, TQ), lambda h, j: (h, 0, 0, 0))
    act = jax.ShapeDtypeStruct((S, D_MODEL), BF16)
    return pl.pallas_call(
        body, name=name,
        out_shape=(act, act, act, jax.ShapeDtypeStruct((FOX_HEADS, nk, 1, TK), F32),
                   jax.ShapeDtypeStruct((FOX_HEADS, nq, 1, TQ), F32)),
        grid=(FOX_HEADS, nk),
        in_specs=[head(_QF_BLK), kblk(_KF_BLK), kblk(_VF_BLK), head(0), head(0), qrows, qrows,
                  pl.BlockSpec((TK, LANES), lambda h, j: (j, 0))],
        out_specs=(head(0), kblk(0), kblk(0), pl.BlockSpec((1, 1, 1, TK), lambda h, j: (h, j, 0, 0)), qrows),
        scratch_shapes=[pltpu.VMEM((S, FOX_DH), F32), pltpu.VMEM((nq, 1, TQ), F32), pltpu.VMEM((nq, 1, TQ), F32),
                        pltpu.VMEM((TK, FOX_DH), F32), pltpu.VMEM((TK, FOX_DH), F32), pltpu.VMEM((TK, 1), F32)],
        compiler_params=_cparams(("parallel", "arbitrary")),
    )(pf, pf, pf, dhb, hb, lse_row, fq_row, fc)


def _pad_lanes(v):
    return jnp.pad(v, ((0, 0), (0, LANES - v.shape[1])))


def _local_step(x, target, wmain_t, wsmall_t, wa, wb, wout, wup, wdown, p):
    S = x.shape[0]
    bi, bf, bff = _pad_lanes(p["b_ml_i"]), _pad_lanes(p["b_ml_f"]), _pad_lanes(p["b_fox_f"])

    h0 = _rmsnorm_fwd(x, p["norm_mix_pre"], "norm_mix_pre")
    pm = _mm(h0, wmain_t[:N_ML], "nt", F32, "proj_mlstm")
    pf = _mm(h0, wmain_t[N_ML:N_ML + N_FOX], "nt", BF16, "proj_fox")
    pg = _mm(h0, wmain_t[N_ML + N_FOX:], "nt", F32, "proj_merge")
    ps = _mm(h0, wsmall_t, "nt", F32, "proj_gates")
    a, A, wi, em, wk, dec, Fc = _gates_fwd(ps, bi, bf, bff, "gates_fwd")
    a_row = a[:, :8].T
    ha, hp, den, cst, nst = _mlstm_fwd(pm, a_row, A, wi, em, wk, dec, p["ml_head_norm"], "mlstm_fwd")
    ft = Fc[:, :FOX_HEADS].T
    fq_row = ft.reshape(FOX_HEADS, S // FOX_TQ, 1, FOX_TQ)
    fk_row = ft.reshape(FOX_HEADS, S // FOX_TK, 1, FOX_TK)
    hb, lse_row = _fox_fwd(pf, Fc, fk_row, "fox_fwd")
    ya = _mm(ha, wa, "nn", F32, "branch_a")
    yb = _mm(hb, wb, "nn", F32, "branch_b")
    merged = _merge_fwd(ya, yb, pg, p["b_gate_a"], p["b_gate_b"], "merge_fwd")
    z = _mm(merged, wout, "nn", F32, "out_proj")
    x1 = _resid_norm_fwd(x, z, p["norm_mix_post"], "resid_mix")
    h2 = _rmsnorm_fwd(x1, p["norm_ffn_pre"], "norm_ffn_pre")
    up = _mm(h2, wup, "nn", F32, "ffn_up")
    act = _conv_act_fwd(up, p["conv_w"], p["conv_b"], "conv_act_fwd")
    d = _mm(act, wdown, "nn", F32, "ffn_down")
    loss_row, dy, dd, g_norm_ffn_post = _loss_head(x1, d, p["norm_ffn_post"], target, "loss_head")
    dact = _mm(dd, wdown, "nt", F32, "d_act")
    g_wdown = _mm(act, dd, "tn", F32, "dw_down")
    dupa, dupg, dcwa, dcwg, dcba, dcbg = _conv_act_bwd(up, dact, p["conv_w"], p["conv_b"], "conv_act_bwd")
    g_conv_w = jnp.concatenate([dcwa, dcwg], axis=1)
    g_conv_b = jnp.concatenate([dcba, dcbg], axis=1)
    dh2 = _mm([dupa, dupg], wup, "nt", F32, "d_h2")
    g_wup = _mm(h2, [dupa, dupg], "tn", F32, "dw_up")
    dx1, g_norm_ffn_pre = _rmsnorm_bwd([dh2], x1, p["norm_ffn_pre"], dy, F32, "norm_ffn_pre_bwd")
    dz, g_norm_mix_post = _rmsnorm_bwd([dx1], z, p["norm_mix_post"], None, BF16, "norm_mix_post_bwd")
    dmerged = _mm(dz, wout, "nt", F32, "d_merged")
    g_wout = _mm(merged, dz, "tn", F32, "dw_out")
    dya, dyb, dga, dgb, g_b_gate_a, g_b_gate_b = _merge_bwd(dmerged, ya, yb, pg, p["b_gate_a"], p["b_gate_b"], "merge_bwd")
    dha = _mm(dya, wa, "nt", F32, "d_ha")
    g_wa = _mm(ha, dya, "tn", F32, "dw_a")
    dhb = _mm(dyb, wb, "nt", BF16, "d_hb")
    g_wb = _mm(hb, dyb, "tn", F32, "dw_b")
    dqkm, dvm, dom, rk, kc, tch, g_ml_head_norm = _mlstm_bwd(
        dha, pm, hp, den, a_row, A, wi, em, wk, dec, cst, nst, p["ml_head_norm"], "mlstm_bwd")
    dqf, dkf, dvf, dFk, dFq = _fox_bwd(dhb, hb, pf, lse_row, fq_row, Fc, "fox_bwd")
    dF = jnp.pad((dFk.reshape(FOX_HEADS, S) + dFq.reshape(FOX_HEADS, S)).T, ((0, 0), (0, LANES - FOX_HEADS)))
    dps, dbias = _gates_bwd(ps, bi, bf, bff, rk, kc, tch, dF, "gates_bwd")
    dpm = [dqkm, dvm, dom, dqf, dkf, dvf, dga, dgb]
    dh0 = _mm(dpm, wmain_t, "nn", F32, "d_h0_main", tm=512)
    dh0s = _mm(dps, wsmall_t, "nn", F32, "d_h0_gates")
    g_wmain_t = _mm(dpm, h0, "tn", F32, "dw_main", tk=512)
    g_wsmall_t = _mm(dps, h0, "tn", F32, "dw_gates")
    grad_x, g_norm_mix_pre = _rmsnorm_bwd([dh0, dh0s], x, p["norm_mix_pre"], dx1, F32, "norm_mix_pre_bwd")

    big = dict(wmain_t=g_wmain_t, wsmall_t=g_wsmall_t, w_branch_a=g_wa, w_branch_b=g_wb, w_out=g_wout, w_up=g_wup, w_down=g_wdown)
    small = dict(norm_mix_pre=g_norm_mix_pre, ml_head_norm=g_ml_head_norm, b_gate_a=g_b_gate_a, b_gate_b=g_b_gate_b,
                 norm_mix_post=g_norm_mix_post, norm_ffn_pre=g_norm_ffn_pre, norm_ffn_post=g_norm_ffn_post,
                 conv_b=g_conv_b, b_ml_i=dbias[:, 0:ML_HEADS], b_ml_f=dbias[:, LANES:LANES + ML_HEADS],
                 b_fox_f=dbias[:, 2 * LANES:2 * LANES + FOX_HEADS], conv_w=g_conv_w)
    return loss_row, grad_x, big, small


def _row_tile(r, target=256):
    best = None
    for t in range(8, min(r, target) + 1, 8):
        if r % t == 0:
            best = t
    return best if best is not None else r


def _adamw(w, g, m, v, name):
    _, R, C = w.shape
    tr = _row_tile(R)
    tc = C
    if tr == R and R > 256:
        tc = 256

    def body(w_ref, g_ref, m_ref, v_ref, d_ref, mo_ref, vo_ref):
        gv = g_ref[...]
        mn = ADAM_B1 * m_ref[0] + (1.0 - ADAM_B1) * gv
        vn = ADAM_B2 * v_ref[0] + (1.0 - ADAM_B2) * (gv * gv)
        m_hat = mn / (1.0 - ADAM_B1 ** ADAM_STEP)
        v_hat = vn / (1.0 - ADAM_B2 ** ADAM_STEP)
        d_ref[0] = -ADAM_LR * (m_hat / (jnp.sqrt(v_hat) + ADAM_EPS) + ADAM_WD * w_ref[0])
        mo_ref[0] = mn
        vo_ref[0] = vn

    blk = pl.BlockSpec((1, tr, tc), lambda i, j: (0, i, j))
    o = jax.ShapeDtypeStruct((1, R, C), F32)
    return pl.pallas_call(
        body, name=name, out_shape=(o, o, o), grid=(R // tr, C // tc),
        in_specs=[blk, pl.BlockSpec((tr, tc), lambda i, j: (i, j)), blk, blk], out_specs=(blk,) * 3,
        compiler_params=_cparams(("parallel", "parallel")),
    )(w, g, m, v)


ANY = pl.BlockSpec(memory_space=pl.ANY)


def _place():
    x, y, c = lax.axis_index("x"), lax.axis_index("y"), lax.axis_index("c")
    chips = [(1 - x, y), (x, 1 - y), (1 - x, 1 - y)]
    return x, y, c, chips


def _block(ref, kind, k, rows=None):
    if kind == "rows":
        return ref.at[k] if rows is None else ref.at[k, pl.ds(*rows), :]
    cb = ref.shape[1] // 4
    return ref.at[:, pl.ds(k * cb, cb)] if rows is None else ref.at[pl.ds(*rows), pl.ds(k * cb, cb)]


def _gathered_shape(s, kind):
    return (4,) + s.shape if kind == "rows" else (s.shape[0], 4 * s.shape[1])


def _gather_weights(shards, kinds, smalls):
    n, ns = len(shards), len(smalls)

    def body(*refs):
        ins, sm_in = refs[:n], refs[n:n + ns]
        outs, sm_out = refs[n + ns:2 * n + ns], refs[2 * n + ns:2 * (n + ns)]
        send_sems, recv_sems, sm_send, sm_recv, local_sems = refs[2 * (n + ns):]
        x, y, c, chips = _place()
        sibling = (x, y, 1 - c)
        kme = 2 * x + y

        def half(a, k, hc):
            h = ins[a].shape[0] // 2
            return _block(outs[a], kinds[a], k, (hc * h, h))

        def remote(a, slot, src, dst, to):
            return pltpu.make_async_remote_copy(src_ref=src, dst_ref=dst, send_sem=send_sems.at[a * 7 + slot],
                                                recv_sem=recv_sems.at[a * 7 + slot], device_id=to, device_id_type=MESH)

        def sm_copy(b, j, k, to):
            return pltpu.make_async_remote_copy(src_ref=sm_in[b], dst_ref=sm_out[b].at[k], send_sem=sm_send.at[3 * b + j],
                                                recv_sem=sm_recv.at[3 * b + j], device_id=to, device_id_type=MESH)

        local = [pltpu.make_async_copy(sm_in[b], sm_out[b].at[kme], local_sems.at[b]) for b in range(ns)]
        for cp in local:
            cp.start()
        sends = [remote(a, 6, ins[a], _block(outs[a], kinds[a], kme), sibling) for a in range(n)]
        for a in range(n):
            h = ins[a].shape[0] // 2
            for j, chip in enumerate(chips):
                sends.append(remote(a, j, ins[a].at[pl.ds(c * h, h), :], half(a, kme, c), (*chip, c)))
        for b in range(ns):
            for j, chip in enumerate(chips):
                sends.append(sm_copy(b, j, kme, (*chip, c)))
        for cp in sends:
            cp.start()
        for a in range(n):
            for j, chip in enumerate(chips):
                kj = 2 * chip[0] + chip[1]
                remote(a, j, half(a, kj, c), half(a, kj, c), (*chip, c)).wait_recv()
                fwd = remote(a, 3 + j, half(a, kj, c), half(a, kj, c), sibling)
                fwd.start()
                sends.append(fwd)
        for a in range(n):
            for j, chip in enumerate(chips):
                kj = 2 * chip[0] + chip[1]
                remote(a, 3 + j, half(a, kj, 1 - c), half(a, kj, 1 - c), sibling).wait_recv()
        for b in range(ns):
            for j, chip in enumerate(chips):
                sm_copy(b, j, 2 * chip[0] + chip[1], (*chip, c)).wait_recv()
        for a in range(n):
            remote(a, 6, ins[a], _block(outs[a], kinds[a], kme), sibling).wait_recv()
        for cp in sends:
            cp.wait_send()
        for cp in local:
            cp.wait()

    outs = pl.pallas_call(
        body, name="gather_weights",
        out_shape=tuple([jax.ShapeDtypeStruct(_gathered_shape(s, k), s.dtype) for s, k in zip(shards, kinds)]
                        + [jax.ShapeDtypeStruct((4,) + s.shape, s.dtype) for s in smalls]),
        in_specs=[ANY] * (n + ns), out_specs=tuple([ANY] * (n + ns)),
        scratch_shapes=[pltpu.SemaphoreType.DMA((7 * n,)), pltpu.SemaphoreType.DMA((7 * n,)),
                        pltpu.SemaphoreType.DMA((3 * ns,)), pltpu.SemaphoreType.DMA((3 * ns,)),
                        pltpu.SemaphoreType.DMA((ns,))],
    )(*shards, *smalls)
    return outs[:n], outs[n:]


def _exchange_sibling_halves(gs, kinds):
    n = len(gs)
    hshape = lambda g, kind: (4, g.shape[1] // 2, g.shape[2]) if kind == "rows" else (g.shape[0] // 2, g.shape[1])

    def body(*refs):
        ins, outs, send_sems, recv_sems = refs[:n], refs[n:2 * n], refs[2 * n], refs[2 * n + 1]
        x, y, c, _ = _place()
        cps = []
        for a in range(n):
            h = outs[a].shape[-2]
            src = ins[a].at[:, pl.ds((1 - c) * h, h), :] if kinds[a] == "rows" else ins[a].at[pl.ds((1 - c) * h, h), :]
            cps.append(pltpu.make_async_remote_copy(
                src_ref=src, dst_ref=outs[a], send_sem=send_sems.at[a],
                recv_sem=recv_sems.at[a], device_id=(x, y, 1 - c), device_id_type=MESH))
        for cp in cps:
            cp.start()
        for cp in cps:
            cp.wait()

    return pl.pallas_call(
        body, name="grads_to_sibling",
        out_shape=tuple(jax.ShapeDtypeStruct(hshape(g, k), g.dtype) for g, k in zip(gs, kinds)),
        in_specs=[ANY] * n, out_specs=tuple([ANY] * n),
        scratch_shapes=[pltpu.SemaphoreType.DMA((n,)), pltpu.SemaphoreType.DMA((n,))],
    )(*gs)


def _add_halves(g, r1, cvec, kind, name):
    def body(c_ref, g_ref, r_ref, o_ref):
        o_ref[...] = (g_ref[...] + r_ref[...]).astype(o_ref.dtype)

    if kind == "rows":
        _, h, C = r1.shape
        tr = _row_tile(h)
        nt = h // tr
        grid = (4, nt)
        g_spec = pl.BlockSpec((1, tr, C), lambda k, i, c_ref: (k, c_ref[0] * nt + i, 0))
        r_spec = pl.BlockSpec((1, tr, C), lambda k, i, c_ref: (k, i, 0))
    else:
        h, C4 = r1.shape
        tr, tc = _row_tile(h), C4 // 4
        nt = h // tr
        grid = (nt, 4)
        g_spec = pl.BlockSpec((tr, tc), lambda i, k, c_ref: (c_ref[0] * nt + i, k))
        r_spec = pl.BlockSpec((tr, tc), lambda i, k, c_ref: (i, k))
    return pl.pallas_call(
        body, name=name, out_shape=jax.ShapeDtypeStruct(r1.shape, BF16),
        grid_spec=pltpu.PrefetchScalarGridSpec(num_scalar_prefetch=1, grid=grid, in_specs=[g_spec, r_spec],
                                               out_specs=r_spec),
        compiler_params=_cparams(("parallel", "parallel")),
    )(cvec, g, r1)


def _exchange_chips(ss, kinds):
    n = len(ss)
    bshape = lambda s, kind: s.shape[1:] if kind == "rows" else (s.shape[0], s.shape[1] // 4)

    def body(*refs):
        ins, outs, send_sems, recv_sems = refs[:n], refs[n:2 * n], refs[2 * n], refs[2 * n + 1]
        x, y, c, chips = _place()
        cps = []
        for a in range(n):
            for j, chip in enumerate(chips):
                cps.append(pltpu.make_async_remote_copy(
                    src_ref=_block(ins[a], kinds[a], 2 * chip[0] + chip[1]), dst_ref=outs[a].at[j],
                    send_sem=send_sems.at[3 * a + j], recv_sem=recv_sems.at[3 * a + j], device_id=(*chip, c),
                    device_id_type=MESH))
        for cp in cps:
            cp.start()
        for cp in cps:
            cp.wait()

    return pl.pallas_call(
        body, name="grads_to_chips",
        out_shape=tuple(jax.ShapeDtypeStruct((3,) + bshape(s, k), s.dtype) for s, k in zip(ss, kinds)),
        in_specs=[ANY] * n, out_specs=tuple([ANY] * n),
        scratch_shapes=[pltpu.SemaphoreType.DMA((3 * n,)), pltpu.SemaphoreType.DMA((3 * n,))],
    )(*ss)


def _add_chips(s1, r2, kcvec, kind, name):
    _, h, C = r2.shape
    tr = _row_tile(h)
    nt = h // tr

    def body(kc_ref, s_ref, r0_ref, r1_ref, r2_ref, o_ref):
        s = s_ref[0] if kind == "rows" else s_ref[...]
        o_ref[...] = ((s.astype(F32) + r0_ref[0].astype(F32)) + r1_ref[0].astype(F32)) + r2_ref[0].astype(F32)

    peer = lambda j: pl.BlockSpec((1, tr, C), lambda i, kc_ref: (j, i, 0))
    if kind == "rows":
        s_spec = pl.BlockSpec((1, tr, C), lambda i, kc_ref: (kc_ref[0], i, 0))
    else:
        s_spec = pl.BlockSpec((tr, C), lambda i, kc_ref: (i, kc_ref[0]))
    return pl.pallas_call(
        body, name=name, out_shape=jax.ShapeDtypeStruct((2 * h, C), F32),
        grid_spec=pltpu.PrefetchScalarGridSpec(
            num_scalar_prefetch=1, grid=(nt,),
            in_specs=[s_spec, peer(0), peer(1), peer(2)],
            out_specs=pl.BlockSpec((tr, C), lambda i, kc_ref: (kc_ref[1] * nt + i, 0))),
        compiler_params=_cparams(("parallel",)),
    )(kcvec, s1, r2, r2, r2)


def _join_sibling_halves(bufs):
    n = len(bufs)

    def body(*refs):
        ins, outs, send_sems, recv_sems = refs[:n], refs[n:2 * n], refs[2 * n], refs[2 * n + 1]
        x, y, c, _ = _place()
        cps = []
        for a in range(n):
            h = ins[a].shape[0] // 2
            cps.append(pltpu.make_async_remote_copy(
                src_ref=ins[a].at[pl.ds(c * h, h), :], dst_ref=outs[a].at[pl.ds(c * h, h), :], send_sem=send_sems.at[a],
                recv_sem=recv_sems.at[a], device_id=(x, y, 1 - c), device_id_type=MESH))
        for cp in cps:
            cp.start()
        for a in range(n):
            h = ins[a].shape[0] // 2
            theirs = outs[a].at[pl.ds((1 - c) * h, h), :]
            pltpu.make_async_remote_copy(src_ref=theirs, dst_ref=theirs, send_sem=send_sems.at[a],
                                         recv_sem=recv_sems.at[a], device_id=(x, y, 1 - c), device_id_type=MESH).wait_recv()
        for cp in cps:
            cp.wait_send()

    return pl.pallas_call(
        body, name="grads_join",
        out_shape=tuple(jax.ShapeDtypeStruct(b.shape, b.dtype) for b in bufs),
        in_specs=[ANY] * n, out_specs=tuple([ANY] * n), input_output_aliases={a: a for a in range(n)},
        scratch_shapes=[pltpu.SemaphoreType.DMA((n,)), pltpu.SemaphoreType.DMA((n,))],
    )(*bufs)


N_DEV = 8


def _allreduce_small(pack):
    P = pack.shape[0]

    def body(p_ref, o_ref, gath, send_sems, recv_sems):
        x, y, c, _ = _place()
        me = 4 * x + 2 * y + c
        cps = []
        for mask in range(1, N_DEV):
            px = 1 - x if mask & 4 else x
            py = 1 - y if mask & 2 else y
            pc = 1 - c if mask & 1 else c
            cps.append((pltpu.make_async_remote_copy(
                src_ref=p_ref, dst_ref=gath.at[me], send_sem=send_sems.at[mask - 1], recv_sem=recv_sems.at[mask - 1],
                device_id=(px, py, pc), device_id_type=MESH), 4 * px + 2 * py + pc, mask))
        for cp, _, _ in cps:
            cp.start()
        gath[me] = p_ref[...]
        for _, peer, mask in cps:
            pltpu.make_async_remote_copy(
                src_ref=p_ref, dst_ref=gath.at[peer], send_sem=send_sems.at[mask - 1], recv_sem=recv_sems.at[mask - 1],
                device_id=(x, y, c), device_id_type=MESH).wait_recv()
        for cp, _, _ in cps:
            cp.wait_send()
        acc = gath[0]
        for i in range(1, N_DEV):
            acc = acc + gath[i]
        o_ref[...] = acc

    return pl.pallas_call(
        body, name="allreduce_small", out_shape=jax.ShapeDtypeStruct((P, LANES), F32),
        in_specs=[pl.BlockSpec(memory_space=pltpu.VMEM)], out_specs=pl.BlockSpec(memory_space=pltpu.VMEM),
        scratch_shapes=[pltpu.VMEM((N_DEV, P, LANES), F32), pltpu.SemaphoreType.DMA((N_DEV - 1,)),
                        pltpu.SemaphoreType.DMA((N_DEV - 1,))],
    )(pack)


def _pack_rows(arrs):
    rows = []
    for a in arrs:
        f = a.reshape(-1)
        f = jnp.pad(f, (0, (-f.shape[0]) % (8 * LANES)))
        rows.append(f.reshape(-1, LANES))
    return jnp.concatenate(rows, axis=0)


def _unpack_rows(pack, shapes):
    out, r = [], 0
    for s in shapes:
        n = math.prod(s)
        out.append(pack[r:r + -(-n // LANES)].reshape(-1)[:n].reshape(s))
        r += 8 * -(-n // (8 * LANES))
    return out


_SMALL = ["norm_mix_pre", "ml_head_norm", "b_gate_a", "b_gate_b", "norm_mix_post", "norm_ffn_pre", "norm_ffn_post",
          "conv_b", "b_ml_i", "b_ml_f", "b_fox_f"]
_BIG = ["w_in", "w_branch_a", "w_branch_b", "w_out", "w_up", "w_down"]
_WEIGHTS = ['norm_mix_pre', 'w_in', 'b_ml_i', 'b_ml_f', 'ml_head_norm', 'b_fox_f', 'b_gate_a', 'b_gate_b', 'w_branch_a',
            'w_branch_b', 'w_out', 'norm_mix_post', 'norm_ffn_pre', 'w_up', 'conv_w', 'conv_b', 'w_down', 'norm_ffn_post']


_KINDS = ["rows", "rows", "rows", "rows", "cols", "rows"]


def kernel(x, norm_mix_pre, w_in, b_ml_i, b_ml_f, ml_head_norm, b_fox_f, b_gate_a, b_gate_b, w_branch_a, w_branch_b, w_out, norm_mix_post, norm_ffn_pre, w_up, conv_w, conv_b, w_down, norm_ffn_post, loss_target, m_norm_mix_pre, m_w_in, m_b_ml_i, m_b_ml_f, m_ml_head_norm, m_b_fox_f, m_b_gate_a, m_b_gate_b, m_w_branch_a, m_w_branch_b, m_w_out, m_norm_mix_post, m_norm_ffn_pre, m_w_up, m_conv_w, m_conv_b, m_w_down, m_norm_ffn_post, v_norm_mix_pre, v_w_in, v_b_ml_i, v_b_ml_f, v_ml_head_norm, v_b_fox_f, v_b_gate_a, v_b_gate_b, v_w_branch_a, v_w_branch_b, v_w_out, v_norm_mix_post, v_norm_ffn_pre, v_w_up, v_conv_w, v_conv_b, v_w_down, v_norm_ffn_post):
    args = dict(locals())
    w = {n: args[n] for n in _WEIGHTS}
    mom = {n: args["m_" + n] for n in _WEIGHTS}
    var = {n: args["v_" + n] for n in _WEIGHTS}
    cx, cy, cc = lax.axis_index("x"), lax.axis_index("y"), lax.axis_index("c")
    kme = 2 * cx + cy
    cvec = jnp.reshape(cc, (1,)).astype(jnp.int32)
    kcvec = jnp.stack([kme, cc]).astype(jnp.int32)
    odd = kme % 2

    tr3 = lambda t: jnp.transpose(t, (0, 2, 1))
    w["w_in"], mom["w_in"], var["w_in"] = tr3(w_in), tr3(m_w_in), tr3(v_w_in)
    w_in_main = lax.dynamic_slice_in_dim(w["w_in"][0], 4 * odd, 2048, axis=0).astype(BF16)
    w_in_gates = lax.dynamic_slice_in_dim(w["w_in"][0], 2048 * (1 - odd), 4, axis=0).astype(BF16)
    shards = [w_in_main] + [w[n][0].astype(BF16) for n in _BIG[1:]]
    (wmain_t, g_a, g_b, g_out, wup, g_down), (g_cw, g_gates) = _gather_weights(
        shards, _KINDS, [w["conv_w"][0], w_in_gates])
    gate_rows = g_gates.reshape(16, D_MODEL)
    wsmall_t = jnp.zeros((N_SMALL, D_MODEL), BF16)
    for blk, (lo, hi) in enumerate(((0, 4), (4, 8), (8, 16))):
        wsmall_t = wsmall_t.at[blk * LANES:blk * LANES + hi - lo].set(gate_rows[lo:hi])
    full = lambda g: g.reshape(-1, g.shape[2])
    p = {n: w[n] for n in _SMALL}
    p["conv_w"] = jnp.transpose(g_cw, (1, 0, 2)).reshape(3, -1)

    loss_row, grad_x, big, small = _local_step(x[0], loss_target[0], full(wmain_t), wsmall_t, full(g_a), full(g_b),
                                               full(g_out), wup, full(g_down), p)

    whole = [big["wmain_t"].reshape(4, -1, D_MODEL), big["w_branch_a"].reshape(4, -1, D_MODEL),
             big["w_branch_b"].reshape(4, -1, D_MODEL),
             big["w_out"].reshape(4, -1, D_MODEL), big["w_up"], big["w_down"].reshape(4, -1, D_MODEL)]
    from_sibling = _exchange_sibling_halves(whole, _KINDS)
    chip_sums = [_add_halves(g, r, cvec, k, "add_sibling_" + n) for g, r, k, n in zip(whole, from_sibling, _KINDS, _BIG)]
    from_chips = _exchange_chips(chip_sums, _KINDS)
    mine = [_add_chips(s, r, kcvec, k, "add_chips_" + n) for s, r, k, n in zip(chip_sums, from_chips, _KINDS, _BIG)]
    grads = dict(zip(_BIG, _join_sibling_halves(mine)))

    gt = big["wsmall_t"]
    small["w_in_gates"] = jnp.concatenate([gt[0:4], gt[LANES:LANES + 4], gt[2 * LANES:2 * LANES + 8]], axis=0)
    small_names = _SMALL + ["conv_w"]
    packed_names = small_names + ["w_in_gates"]
    pack = _pack_rows([small[n] for n in packed_names] + [loss_row])
    pack = jnp.pad(pack, ((0, (-pack.shape[0]) % 8), (0, 0)))
    full_shapes = [small[n].shape if n in ("conv_w", "w_in_gates") else w[n][0].shape for n in packed_names]
    total = _unpack_rows(_allreduce_small(pack), full_shapes + [loss_row.shape])
    for n, t in zip(packed_names, total):
        grads[n] = t
    loss = total[-1][0, 0]
    grads["conv_w"] = lax.dynamic_slice_in_dim(grads["conv_w"], kme * conv_w.shape[2], conv_w.shape[2], axis=1)
    my_gates = lax.dynamic_slice_in_dim(grads.pop("w_in_gates"), 4 * kme, 4, axis=0)
    g_in = jnp.zeros(w["w_in"].shape[1:], F32)
    g_in = lax.dynamic_update_slice_in_dim(g_in, grads["w_in"], 4 * odd, axis=0)
    grads["w_in"] = lax.dynamic_update_slice_in_dim(g_in, my_gates, 2048 * (1 - odd), axis=0)

    delta, new_m, new_v = {}, {}, {}
    for n in _BIG:
        delta[n], new_m[n], new_v[n] = _adamw(w[n], grads[n], mom[n], var[n], "adamw_" + n)
        grads[n] = grads[n][None]
    for d in (grads, delta, new_m, new_v):
        d["w_in"] = tr3(d["w_in"])
    packs = [_pack_rows([d[n][0] for n in small_names]) for d in (w, mom, var)]
    pad = ((0, (-packs[0].shape[0]) % 8), (0, 0))
    packs = [jnp.pad(t, pad)[None] for t in packs]
    gp = jnp.pad(_pack_rows([grads[n] for n in small_names]), pad)
    shapes = [w[n][0].shape for n in small_names]
    for dst, res in zip((delta, new_m, new_v), _adamw(packs[0], gp, packs[1], packs[2], "adamw_small")):
        for n, t in zip(small_names, _unpack_rows(res[0], shapes)):
            dst[n] = t[None]
    for n in small_names:
        grads[n] = grads[n][None]

    return (loss, grad_x[None], *[grads[n] for n in _WEIGHTS], *[delta[n] for n in _WEIGHTS],
            *[new_m[n] for n in _WEIGHTS], *[new_v[n] for n in _WEIGHTS])
```

```python
import functools
import math

import jax
import jax.numpy as jnp
from jax import lax
from jax.experimental import pallas as pl
from jax.experimental.pallas import tpu as pltpu

F32 = jnp.float32
BF16 = jnp.bfloat16
MESH = pl.DeviceIdType.MESH

D_MODEL = 1024
ML_HEADS = 4
ML_DQK = 128
ML_DV = 256
FOX_HEADS = 8
FOX_DH = 128
D_FF = 2816
GATE_CAP = 15.0
EPS = 1e-6
ADAM_LR, ADAM_B1, ADAM_B2, ADAM_EPS, ADAM_WD, ADAM_STEP = 0.001, 0.9, 0.999, 1e-08, 0.01, 10

LANES = 128
MLC = 128
FOX_TQ = 256
FOX_TK = 512
ROW_T = 512
VMEM_LIMIT = 56 * 1024 * 1024

C_QM, C_KM, C_VM, C_OM = 0, 512, 1024, 2048
N_ML, N_FOX, N_GATE = 3072, 3072, 2048
N_SMALL = 384


def _cparams(sem=None):
    return pltpu.CompilerParams(dimension_semantics=sem, vmem_limit_bytes=VMEM_LIMIT)


def _tile(n, target):
    if n <= target:
        return n
    best = None
    for t in range(LANES, target + 1, LANES):
        if n % t == 0:
            best = t
    assert best is not None, (n, target)
    return best


def _dot(a, b, dims):
    return lax.dot_general(a, b, (dims, ((), ())), preferred_element_type=F32)


def _dot_nn(a, b):
    return _dot(a, b, ((1,), (0,)))


def _dot_nt(a, b):
    return _dot(a, b, ((1,), (1,)))


def _dot_tn(a, b):
    return _dot(a, b, ((0,), (0,)))


_DOTS = {"nn": _dot_nn, "nt": _dot_nt, "tn": _dot_tn}


def _mm(a, b, mode, out_dtype, name, tm=1024, tn=1408, tk=1408):
    a_parts = list(a) if isinstance(a, (list, tuple)) else [a]
    b_parts = list(b) if isinstance(b, (list, tuple)) else [b]
    assert len(a_parts) == 1 or len(b_parts) == 1, name
    a_axes = {"nn": "ik", "nt": "ik", "tn": "ki"}[mode]
    b_axes = {"nn": "kj", "nt": "jk", "tn": "kj"}[mode]
    size, target = {}, dict(i=tm, j=tn, k=tk)
    for parts, axes in ((a_parts, a_axes), (b_parts, b_axes)):
        dims = (parts[0].shape[0], parts[0].shape[1] * len(parts))
        for ax, n in zip(axes, dims):
            assert size.setdefault(ax, n) == n, (name, ax, n, size)
    tile = {}
    for parts, axes in ((a_parts, a_axes), (b_parts, b_axes)):
        if len(parts) > 1:
            tile[axes[1]] = _tile(parts[0].shape[1], target[axes[1]])
    for ax in "ijk":
        tile.setdefault(ax, _tile(size[ax], target[ax]))
    M, N, nk = size["i"], size["j"], size["k"] // tile["k"]
    grid_pos = dict(i=0, j=1, k=2)
    dot = _DOTS[mode]

    def specs(parts, axes):
        blk = (tile[axes[0]], tile[axes[1]])
        if len(parts) == 1:
            return [pl.BlockSpec(blk, lambda *g: (g[grid_pos[axes[0]]], g[grid_pos[axes[1]]]))], None
        bpp = parts[0].shape[1] // blk[1]

        def index(p):
            def f(*g):
                g0, g1 = g[grid_pos[axes[0]]], g[grid_pos[axes[1]]]
                on = g1 // bpp == p
                return jnp.where(on, g0, 0), jnp.where(on, g1 % bpp, 0)
            return f

        return [pl.BlockSpec(blk, index(p)) for p in range(len(parts))], (axes[1], bpp)

    a_specs, a_sel = specs(a_parts, a_axes)
    b_specs, b_sel = specs(b_parts, b_axes)
    na, nb = len(a_parts), len(b_parts)

    def body(*refs):
        a_refs, b_refs, o_ref, acc = refs[:na], refs[na:na + nb], refs[na + nb], refs[na + nb + 1:]

        def accumulate(part):
            if nk == 1:
                o_ref[...] = part.astype(o_ref.dtype)
                return
            acc_ref, = acc
            k = pl.program_id(2)

            @pl.when(k == 0)
            def _():
                acc_ref[...] = part

            @pl.when(k > 0)
            def _():
                acc_ref[...] += part

            @pl.when(k == nk - 1)
            def _():
                o_ref[...] = acc_ref[...].astype(o_ref.dtype)

        sel = a_sel or b_sel
        if sel is None:
            accumulate(dot(a_refs[0][...], b_refs[0][...]))
        else:
            which = pl.program_id(grid_pos[sel[0]]) // sel[1]
            for p in range(max(na, nb)):
                @pl.when(which == p)
                def _(p=p):
                    accumulate(dot(a_refs[p if a_sel else 0][...], b_refs[p if b_sel else 0][...]))

    return pl.pallas_call(
        body, name=name,
        out_shape=jax.ShapeDtypeStruct((M, N), out_dtype),
        grid=(M // tile["i"], N // tile["j"], nk),
        in_specs=a_specs + b_specs,
        out_specs=pl.BlockSpec((tile["i"], tile["j"]), lambda i, j, k: (i, j)),
        scratch_shapes=[pltpu.VMEM((tile["i"], tile["j"]), F32)] if nk > 1 else [],
        compiler_params=_cparams(("parallel", "parallel", "arbitrary")),
    )(*a_parts, *b_parts)


def _rstd(x):
    return lax.rsqrt(jnp.mean(x * x, axis=-1, keepdims=True) + EPS)


def _rmsnorm_fwd(x, g, name):
    S, D = x.shape
    T = _tile(S, ROW_T)

    def body(x_ref, g_ref, o_ref):
        xv = x_ref[...]
        o_ref[...] = (xv * _rstd(xv) * g_ref[...]).astype(o_ref.dtype)

    return pl.pallas_call(
        body, name=name, out_shape=jax.ShapeDtypeStruct((S, D), BF16), grid=(S // T,),
        in_specs=[pl.BlockSpec((T, D), lambda i: (i, 0)), pl.BlockSpec((1, D), lambda i: (0, 0))],
        out_specs=pl.BlockSpec((T, D), lambda i: (i, 0)),
        compiler_params=_cparams(("parallel",)),
    )(x, g)


def _resid_norm_fwd(x, z, g, name):
    S, D = x.shape
    T = _tile(S, ROW_T)

    def body(x_ref, z_ref, g_ref, o_ref):
        zv = z_ref[...]
        o_ref[...] = x_ref[...] + zv * _rstd(zv) * g_ref[...]

    row = pl.BlockSpec((T, D), lambda i: (i, 0))
    return pl.pallas_call(
        body, name=name, out_shape=jax.ShapeDtypeStruct((S, D), F32), grid=(S // T,),
        in_specs=[row, row, pl.BlockSpec((1, D), lambda i: (0, 0))],
        out_specs=row, compiler_params=_cparams(("parallel",)),
    )(x, z, g)


def _rmsnorm_bwd_math(dy, xv, g):
    r = _rstd(xv)
    u = dy * g
    dx = r * u - xv * (r * r * r) * jnp.mean(u * xv, axis=-1, keepdims=True)
    return dx, dy * xv * r


def _rmsnorm_bwd(dys, xin, g, resid, out_dtype, name):
    S, D = xin.shape
    T = _tile(S, ROW_T)
    has_resid = resid is not None
    ndy = len(dys)

    def body(*refs):
        dy_refs, (x_ref, g_ref) = refs[:ndy], refs[ndy:ndy + 2]
        dx_ref, dg_ref = refs[-2:]
        dy = dy_refs[0][...]
        for r in dy_refs[1:]:
            dy = dy + r[...]
        dx, dgt = _rmsnorm_bwd_math(dy, x_ref[...], g_ref[...])
        if has_resid:
            dx = dx + refs[ndy + 2][...]
        dx_ref[...] = dx.astype(dx_ref.dtype)

        @pl.when(pl.program_id(0) == 0)
        def _():
            dg_ref[...] = jnp.zeros_like(dg_ref)

        dg_ref[...] += jnp.sum(dgt, axis=0, keepdims=True)

    row = pl.BlockSpec((T, D), lambda i: (i, 0))
    vec = pl.BlockSpec((1, D), lambda i: (0, 0))
    ins = list(dys) + [xin, g] + ([resid] if has_resid else [])
    return pl.pallas_call(
        body, name=name,
        out_shape=(jax.ShapeDtypeStruct((S, D), out_dtype), jax.ShapeDtypeStruct((1, D), F32)),
        grid=(S // T,), in_specs=[row] * ndy + [row, vec] + ([row] if has_resid else []),
        out_specs=(row, vec), compiler_params=_cparams(("arbitrary",)),
    )(*ins)


def _loss_head(x1, d, g, target, name):
    S, D = x1.shape
    T = _tile(S, ROW_T)

    def body(x_ref, d_ref, g_ref, t_ref, loss_ref, dy_ref, dd_ref, dg_ref):
        dv, gv = d_ref[...], g_ref[...]
        y = x_ref[...] + dv * _rstd(dv) * gv
        diff = y - t_ref[...]
        dy = diff * (1.0 / D)
        dy_ref[...] = dy
        dd, dgt = _rmsnorm_bwd_math(dy, dv, gv)
        dd_ref[...] = dd.astype(dd_ref.dtype)

        @pl.when(pl.program_id(0) == 0)
        def _():
            dg_ref[...] = jnp.zeros_like(dg_ref)
            loss_ref[...] = jnp.zeros_like(loss_ref)

        dg_ref[...] += jnp.sum(dgt, axis=0, keepdims=True)
        part = jnp.sum(jnp.sum(diff * diff, axis=1, keepdims=True), axis=0, keepdims=True)
        loss_ref[...] += (0.5 / D) * part

    row = pl.BlockSpec((T, D), lambda i: (i, 0))
    vec = pl.BlockSpec((1, D), lambda i: (0, 0))
    return pl.pallas_call(
        body, name=name,
        out_shape=(jax.ShapeDtypeStruct((1, LANES), F32), jax.ShapeDtypeStruct((S, D), F32),
                   jax.ShapeDtypeStruct((S, D), BF16), jax.ShapeDtypeStruct((1, D), F32)),
        grid=(S // T,), in_specs=[row, row, vec, row],
        out_specs=(pl.BlockSpec((1, LANES), lambda i: (0, 0)), row, row, vec),
        compiler_params=_cparams(("arbitrary",)),
    )(x1, d, g, target)


def _merge_fwd(ya, yb, pm, ba, bb, name):
    S, D = ya.shape
    T = _tile(S, ROW_T)

    def body(ya_ref, yb_ref, ga_ref, gb_ref, ba_ref, bb_ref, o_ref):
        sa = jax.nn.sigmoid(ga_ref[...] + ba_ref[...])
        sb = jax.nn.sigmoid(gb_ref[...] + bb_ref[...])
        o_ref[...] = (sa * ya_ref[...] + sb * yb_ref[...]).astype(o_ref.dtype)

    row = pl.BlockSpec((T, D), lambda i: (i, 0))
    vec = pl.BlockSpec((1, D), lambda i: (0, 0))
    return pl.pallas_call(
        body, name=name, out_shape=jax.ShapeDtypeStruct((S, D), BF16), grid=(S // T,),
        in_specs=[row, row, pl.BlockSpec((T, D), lambda i: (i, 0)),
                  pl.BlockSpec((T, D), lambda i: (i, 1)), vec, vec],
        out_specs=row, compiler_params=_cparams(("parallel",)),
    )(ya, yb, pm, pm, ba, bb)


def _merge_bwd(dmerged, ya, yb, pm, ba, bb, name):
    S, D = ya.shape
    T = _tile(S, ROW_T)

    def body(dm_ref, ya_ref, yb_ref, ga_ref, gb_ref, ba_ref, bb_ref,
             dya_ref, dyb_ref, dga_ref, dgb_ref, dba_ref, dbb_ref):
        dm = dm_ref[...]
        sa = jax.nn.sigmoid(ga_ref[...] + ba_ref[...])
        sb = jax.nn.sigmoid(gb_ref[...] + bb_ref[...])
        dya_ref[...] = (dm * sa).astype(dya_ref.dtype)
        dyb_ref[...] = (dm * sb).astype(dyb_ref.dtype)
        dga = dm * ya_ref[...] * sa * (1.0 - sa)
        dgb = dm * yb_ref[...] * sb * (1.0 - sb)
        dga_ref[...] = dga.astype(dga_ref.dtype)
        dgb_ref[...] = dgb.astype(dgb_ref.dtype)

        @pl.when(pl.program_id(0) == 0)
        def _():
            dba_ref[...] = jnp.zeros_like(dba_ref)
            dbb_ref[...] = jnp.zeros_like(dbb_ref)

        dba_ref[...] += jnp.sum(dga, axis=0, keepdims=True)
        dbb_ref[...] += jnp.sum(dgb, axis=0, keepdims=True)

    row = pl.BlockSpec((T, D), lambda i: (i, 0))
    vec = pl.BlockSpec((1, D), lambda i: (0, 0))
    act = jax.ShapeDtypeStruct((S, D), BF16)
    v1 = jax.ShapeDtypeStruct((1, D), F32)
    return pl.pallas_call(
        body, name=name, out_shape=(act, act, act, act, v1, v1), grid=(S // T,),
        in_specs=[row, row, row, pl.BlockSpec((T, D), lambda i: (i, 0)),
                  pl.BlockSpec((T, D), lambda i: (i, 1)), vec, vec],
        out_specs=(row, row, row, row, vec, vec), compiler_params=_cparams(("arbitrary",)),
    )(dmerged, ya, yb, pm, pm, ba, bb)


_GELU_C = math.sqrt(2.0 / math.pi)


def _gelu(g):
    t = jnp.tanh(_GELU_C * (g + 0.044715 * g * g * g))
    return 0.5 * g * (1.0 + t), t


def _gelu_grad(g, t):
    return 0.5 * (1.0 + t) + 0.5 * g * (1.0 - t * t) * _GELU_C * (1.0 + 3 * 0.044715 * g * g)


def _shift_down(v, halo_ref, first, rows):
    T = v.shape[0]
    keep = jnp.where(first, 0.0, 1.0)
    h7 = halo_ref[7:8, :] * keep
    h6 = halo_ref[6:7, :] * keep
    m1 = jnp.where(rows == 0, h7, pltpu.roll(v, 1, 0))
    m2 = jnp.where(rows == 0, h6, jnp.where(rows == 1, h7, pltpu.roll(v, 2, 0)))
    return m1, m2


def _conv_act_fwd(up, cw, cb, name):
    S, F2 = up.shape
    Fh = F2 // 2
    T = _tile(S, ROW_T)
    tc = _tile(Fh, 256)
    ncol = Fh // tc
    hb = T // 8

    def body(ua_ref, ug_ref, ha_ref, hg_ref, wa_ref, wg_ref, ba_ref, bg_ref, o_ref):
        first = pl.program_id(0) == 0
        rows = lax.broadcasted_iota(jnp.int32, (T, tc), 0)

        def conv(u_ref, h_ref, w_ref, b_ref):
            v = u_ref[...]
            m1, m2 = _shift_down(v, h_ref, first, rows)
            return b_ref[...] + w_ref[0:1, :] * m2 + w_ref[1:2, :] * m1 + w_ref[2:3, :] * v

        a = conv(ua_ref, ha_ref, wa_ref, ba_ref)
        g = conv(ug_ref, hg_ref, wg_ref, bg_ref)
        o_ref[...] = (_gelu(g)[0] * a).astype(o_ref.dtype)

    halo = lambda off: pl.BlockSpec((8, tc), lambda i, j: (jnp.maximum(i * hb - 1, 0), j + off))
    return pl.pallas_call(
        body, name=name, out_shape=jax.ShapeDtypeStruct((S, Fh), BF16), grid=(S // T, ncol),
        in_specs=[pl.BlockSpec((T, tc), lambda i, j: (i, j)), pl.BlockSpec((T, tc), lambda i, j: (i, j + ncol)),
                  halo(0), halo(ncol),
                  pl.BlockSpec((3, tc), lambda i, j: (0, j)), pl.BlockSpec((3, tc), lambda i, j: (0, j + ncol)),
                  pl.BlockSpec((1, tc), lambda i, j: (0, j)), pl.BlockSpec((1, tc), lambda i, j: (0, j + ncol))],
        out_specs=pl.BlockSpec((T, tc), lambda i, j: (i, j)),
        compiler_params=_cparams(("parallel", "parallel")),
    )(up, up, up, up, cw, cw, cb, cb)


def _conv_act_bwd(up, dact, cw, cb, name):
    S, F2 = up.shape
    Fh = F2 // 2
    T = _tile(S, ROW_T)
    tc = _tile(Fh, 256)
    ncol, nrow, hb, nhb = Fh // tc, S // T, T // 8, S // 8

    def body(ua_ref, ug_ref, ha_ref, hg_ref, na_ref, ng_ref, wa_ref, wg_ref, ba_ref, bg_ref, da_ref, dn_ref,
             dpa_ref, dpg_ref, dwa_ref, dwg_ref, dba_ref, dbg_ref, dua_n, dug_n):
        i = pl.program_id(1)
        first = i == 0
        rows = lax.broadcasted_iota(jnp.int32, (T, tc), 0)
        rows8 = lax.broadcasted_iota(jnp.int32, (8, tc), 0)

        def conv(v, m1, m2, w_ref, b_ref):
            return b_ref[...] + w_ref[0:1, :] * m2 + w_ref[1:2, :] * m1 + w_ref[2:3, :] * v

        def du_of(a, g, dact_v):
            gel, t = _gelu(g)
            return dact_v * gel, dact_v * a * _gelu_grad(g, t)

        va, vg = ua_ref[...], ug_ref[...]
        a1, a2 = _shift_down(va, ha_ref, first, rows)
        g1, g2 = _shift_down(vg, hg_ref, first, rows)
        dua, dug = du_of(conv(va, a1, a2, wa_ref, ba_ref), conv(vg, g1, g2, wg_ref, bg_ref), da_ref[...])

        @pl.when(first)
        def _():
            for r in (dwa_ref, dwg_ref, dba_ref, dbg_ref):
                r[...] = jnp.zeros_like(r)

        for du, taps, dw_ref, db_ref in ((dua, (a2, a1, va), dwa_ref, dba_ref), (dug, (g2, g1, vg), dwg_ref, dbg_ref)):
            db_ref[...] += jnp.sum(du, axis=0, keepdims=True)
            for j in range(3):
                dw_ref[j:j + 1, :] += jnp.sum(du * taps[j], axis=0, keepdims=True)

        def below(n_ref, u_ref):
            v = n_ref[...]
            l1, l2 = u_ref[T - 1:T, :], u_ref[T - 2:T - 1, :]
            m1 = jnp.where(rows8 == 0, l1, pltpu.roll(v, 1, 0))
            m2 = jnp.where(rows8 == 0, l2, jnp.where(rows8 == 1, l1, pltpu.roll(v, 2, 0)))
            return v, m1, m2

        keep = jnp.where(i == nrow - 1, 0.0, 1.0)
        na, ng = below(na_ref, ua_ref), below(ng_ref, ug_ref)
        dna, dng = du_of(conv(*na, wa_ref, ba_ref), conv(*ng, wg_ref, bg_ref), dn_ref[...] * keep)
        dua_n[...] = dna
        dug_n[...] = dng

        for du, n_ref, w_ref, o_ref in ((dua, dua_n, wa_ref, dpa_ref), (dug, dug_n, wg_ref, dpg_ref)):
            n0, n1 = n_ref[0:1, :], n_ref[1:2, :]
            p1 = jnp.where(rows == T - 1, n0, pltpu.roll(du, T - 1, 0))
            p2 = jnp.where(rows == T - 2, n0, jnp.where(rows == T - 1, n1, pltpu.roll(du, T - 2, 0)))
            o_ref[...] = (w_ref[2:3, :] * du + w_ref[1:2, :] * p1 + w_ref[0:1, :] * p2).astype(o_ref.dtype)

    tile = lambda off: pl.BlockSpec((T, tc), lambda j, i: (i, j + off))
    above = lambda off: pl.BlockSpec((8, tc), lambda j, i: (jnp.maximum(i * hb - 1, 0), j + off))
    under = lambda off: pl.BlockSpec((8, tc), lambda j, i: (jnp.minimum((i + 1) * hb, nhb - 1), j + off))
    vec = lambda n, off: pl.BlockSpec((n, tc), lambda j, i: (0, j + off))
    act = jax.ShapeDtypeStruct((S, Fh), BF16)
    return pl.pallas_call(
        body, name=name,
        out_shape=(act, act, jax.ShapeDtypeStruct((3, Fh), F32), jax.ShapeDtypeStruct((3, Fh), F32),
                   jax.ShapeDtypeStruct((1, Fh), F32), jax.ShapeDtypeStruct((1, Fh), F32)),
        grid=(ncol, nrow),
        in_specs=[tile(0), tile(ncol), above(0), above(ncol), under(0), under(ncol),
                  vec(3, 0), vec(3, ncol), vec(1, 0), vec(1, ncol), tile(0), under(0)],
        out_specs=(tile(0), tile(0), vec(3, 0), vec(3, 0), vec(1, 0), vec(1, 0)),
        scratch_shapes=[pltpu.VMEM((8, tc), F32), pltpu.VMEM((8, tc), F32)],
        compiler_params=_cparams(("parallel", "arbitrary")),
    )(up, up, up, up, up, up, cw, cw, cb, cb, dact, dact)


def _split3(x):
    hi = x.astype(BF16)
    r1 = x - hi.astype(F32)
    mid = r1.astype(BF16)
    lo = (r1 - mid.astype(F32)).astype(BF16)
    return hi, mid, lo


def _tri_dot(tri, x):
    hi, mid, lo = _split3(x)
    return _dot_nn(tri, hi) + _dot_nn(tri, mid) + _dot_nn(tri, lo)


def _log_sigmoid(x):
    return jnp.minimum(x, 0.0) - jnp.log(1.0 + jnp.exp(-jnp.abs(x)))


def _tri_mask(n, lower):
    r = lax.broadcasted_iota(jnp.int32, (n, n), 0)
    c = lax.broadcasted_iota(jnp.int32, (n, n), 1)
    return (r >= c) if lower else (r <= c)


def _gates_fwd(ps, bi, bf, bff, name):
    S = ps.shape[0]
    NC = S // MLC

    def body(ps_ref, bi_ref, bf_ref, bff_ref, a_ref, A_ref, wi_ref, em_ref, wk_ref, dec_ref, F_ref, m_scr, f_scr):
        @pl.when(pl.program_id(0) == 0)
        def _():
            m_scr[...] = jnp.zeros_like(m_scr)
            f_scr[...] = jnp.zeros_like(f_scr)

        rows = lax.broadcasted_iota(jnp.int32, (MLC, LANES), 0)
        ltri = _tri_mask(MLC, True).astype(BF16)
        li = GATE_CAP * jnp.tanh((ps_ref[:, 0:LANES] + bi_ref[...]) / GATE_CAP)
        lf = _log_sigmoid(GATE_CAP * jnp.tanh((ps_ref[:, LANES:2 * LANES] + bf_ref[...]) / GATE_CAP))
        b = _tri_dot(ltri, lf)
        a = li - b
        cm = a
        sh = 1
        while sh < MLC:
            cm = jnp.where(rows >= sh, jnp.maximum(cm, pltpu.roll(cm, sh, 0)), cm)
            sh *= 2
        m0 = m_scr[...]
        A = jnp.maximum(cm, m0)
        a_ref[...] = a
        A_ref[...] = A
        A_last = A_ref[MLC - 1:MLC, :]
        wi_ref[...] = jnp.exp(m0 - A)
        em_ref[...] = jnp.exp(-(b + A))
        wk_ref[...] = jnp.exp(a - A_last)
        dec_ref[0] = jnp.exp(m0 - A_last)
        F_ref[...] = b
        m_scr[...] = F_ref[MLC - 1:MLC, :] + A_last
        lfg = _log_sigmoid(ps_ref[:, 2 * LANES:3 * LANES] + bff_ref[...])
        F_ref[...] = _tri_dot(ltri, lfg) + f_scr[...]
        f_scr[...] = F_ref[MLC - 1:MLC, :]

    col = pl.BlockSpec((MLC, LANES), lambda c: (c, 0))
    vec = pl.BlockSpec((1, LANES), lambda c: (0, 0))
    cs = jax.ShapeDtypeStruct((S, LANES), F32)
    return pl.pallas_call(
        body, name=name,
        out_shape=(cs, cs, cs, cs, cs, jax.ShapeDtypeStruct((NC, 1, LANES), F32), cs),
        grid=(NC,), in_specs=[pl.BlockSpec((MLC, N_SMALL), lambda c: (c, 0)), vec, vec, vec],
        out_specs=(col, col, col, col, col, pl.BlockSpec((1, 1, LANES), lambda c: (c, 0, 0)), col),
        scratch_shapes=[pltpu.VMEM((1, LANES), F32), pltpu.VMEM((1, LANES), F32)],
        compiler_params=_cparams(("arbitrary",)),
    )(ps, bi, bf, bff)


def _gates_bwd(ps, bi, bf, bff, rk, kc, tch, dF, name):
    S = ps.shape[0]
    NC = S // MLC

    def body(ps_ref, bi_ref, bf_ref, bff_ref, rk_ref, kc_ref, t_ref, dF_ref, dps_ref, db_ref, carry):
        @pl.when(pl.program_id(0) == 0)
        def _():
            carry[...] = jnp.zeros_like(carry)
            db_ref[...] = jnp.zeros_like(db_ref)

        lanes = lax.broadcasted_iota(jnp.int32, (MLC, LANES), 1)
        utri = _tri_mask(MLC, False).astype(BF16)
        ti = jnp.tanh((ps_ref[:, 0:LANES] + bi_ref[...]) / GATE_CAP)
        t_end, t_start = t_ref[0, 0:1, :], t_ref[0, 1:2, :]
        rk = rk_ref[...]
        rk = rk - (jnp.sum(rk, axis=0, keepdims=True) - (t_start - t_end)) * (1.0 / MLC)
        dpi = jnp.where(lanes < ML_HEADS, (kc_ref[...] - rk) * (1.0 - ti * ti), 0.0)
        tf = jnp.tanh((ps_ref[:, LANES:2 * LANES] + bf_ref[...]) / GATE_CAP)
        dlf = _tri_dot(utri, rk) + t_end
        dpf = jnp.where(lanes < ML_HEADS, dlf * jax.nn.sigmoid(-GATE_CAP * tf) * (1.0 - tf * tf), 0.0)
        dFv = dF_ref[...]
        dlfg = _tri_dot(utri, dFv) + carry[...]
        carry[...] += jnp.sum(dFv, axis=0, keepdims=True)
        dpff = jnp.where(lanes < FOX_HEADS, dlfg * jax.nn.sigmoid(-(ps_ref[:, 2 * LANES:3 * LANES] + bff_ref[...])), 0.0)
        for n, dp in enumerate((dpi, dpf, dpff)):
            dps_ref[:, n * LANES:(n + 1) * LANES] = dp.astype(dps_ref.dtype)
            db_ref[:, n * LANES:(n + 1) * LANES] += jnp.sum(dp, axis=0, keepdims=True)

    rev = lambda c: (NC - 1 - c, 0)
    col = pl.BlockSpec((MLC, LANES), rev)
    vec = pl.BlockSpec((1, LANES), lambda c: (0, 0))
    wide = pl.BlockSpec((MLC, N_SMALL), rev)
    return pl.pallas_call(
        body, name=name,
        out_shape=(jax.ShapeDtypeStruct((S, N_SMALL), BF16), jax.ShapeDtypeStruct((1, N_SMALL), F32)),
        grid=(NC,),
        in_specs=[wide, vec, vec, vec, col, col, pl.BlockSpec((1, 2, LANES), lambda c: (NC - 1 - c, 0, 0)), col],
        out_specs=(wide, pl.BlockSpec((1, N_SMALL), lambda c: (0, 0))),
        scratch_shapes=[pltpu.VMEM((1, LANES), F32)],
        compiler_params=_cparams(("arbitrary",)),
    )(ps, bi, bf, bff, rk, kc, tch, dF)


_ML_SCALE = ML_DQK ** -0.5


def _ml_specs(rev, NC):
    idx = (lambda c: NC - 1 - c) if rev else (lambda c: c)
    qk = lambda blk: pl.BlockSpec((MLC, ML_HEADS * ML_DQK), lambda c: (idx(c), blk))
    wide = lambda blk: pl.BlockSpec((MLC, D_MODEL), lambda c: (idx(c), blk))
    col = pl.BlockSpec((MLC, LANES), lambda c: (idx(c), 0))
    return idx, qk, wide, col


def _ml_intra(q_ref, k_ref, arow_ref, A_ref, h):
    hs = slice(h * ML_DQK, (h + 1) * ML_DQK)
    qf = q_ref[:, hs] * _ML_SCALE
    kf = k_ref[:, hs]
    qb, kb = qf.astype(BF16), kf.astype(BF16)
    qk = _dot_nt(qb, kb)
    logw = arow_ref[h:h + 1, :] - A_ref[:, h:h + 1]
    W = jnp.exp(jnp.where(_tri_mask(MLC, True), logw, -1e30))
    return qb, kb, qf, kf, qk, W


def _mlstm_fwd(pm, a_row, A, wi, em, wk, dec, w_hn, name):
    S = pm.shape[0]
    NC = S // MLC
    _, qk, wide, col = _ml_specs(False, NC)

    def body(q_ref, k_ref, v_ref, o_ref, arow_ref, A_ref, wi_ref, em_ref, wk_ref, dec_ref, whn_ref,
             ha_ref, hp_ref, den_ref, cst_ref, nst_ref, C_scr, n_scr):
        @pl.when(pl.program_id(0) == 0)
        def _():
            C_scr[...] = jnp.zeros_like(C_scr)
            n_scr[...] = jnp.zeros_like(n_scr)

        lanes = lax.broadcasted_iota(jnp.int32, (MLC, LANES), 1)
        den_tile = jnp.zeros((MLC, LANES), F32)
        for h in range(ML_HEADS):
            vs = slice(h * ML_DV, (h + 1) * ML_DV)
            qb, kb, qf, kf, qk_, W = _ml_intra(q_ref, k_ref, arow_ref, A_ref, h)
            vb = v_ref[:, vs].astype(BF16)
            Cf = C_scr[h]
            Cb = Cf.astype(BF16)
            nrow = n_scr[h]
            cst_ref[0, h] = Cb
            nst_ref[0, h] = nrow
            s = qk_ * W
            wic = wi_ref[:, h:h + 1]
            num = _dot_nn(s.astype(BF16), vb) + wic * _dot_nt(qb, Cb)
            den = jnp.sum(s, axis=1, keepdims=True) + wic * jnp.sum(qf * nrow, axis=1, keepdims=True)
            hp = num / jnp.maximum(jnp.abs(den), em_ref[:, h:h + 1])
            hp_ref[:, vs] = hp
            den_tile = jnp.where(lanes == h, den, den_tile)
            hn = hp * _rstd(hp) * whn_ref[:, vs]
            ha_ref[:, vs] = (hn * jax.nn.sigmoid(o_ref[:, vs])).astype(ha_ref.dtype)
            wkc = wk_ref[:, h:h + 1]
            kw = kf * wkc
            d = dec_ref[0, :, h:h + 1]
            C_scr[h] = d * Cf + _dot_tn(vb, kw.astype(BF16))
            n_scr[h] = d * nrow + jnp.sum(kw, axis=0, keepdims=True)
        den_ref[...] = den_tile

    return pl.pallas_call(
        body, name=name,
        out_shape=(jax.ShapeDtypeStruct((S, D_MODEL), BF16), jax.ShapeDtypeStruct((S, D_MODEL), F32),
                   jax.ShapeDtypeStruct((S, LANES), F32),
                   jax.ShapeDtypeStruct((NC, ML_HEADS, ML_DV, ML_DQK), BF16),
                   jax.ShapeDtypeStruct((NC, ML_HEADS, 1, ML_DQK), F32)),
        grid=(NC,),
        in_specs=[qk(C_QM // 512), qk(C_KM // 512), wide(C_VM // D_MODEL), wide(C_OM // D_MODEL),
                  pl.BlockSpec((8, MLC), lambda c: (0, c)), col, col, col, col,
                  pl.BlockSpec((1, 1, LANES), lambda c: (c, 0, 0)), pl.BlockSpec((1, D_MODEL), lambda c: (0, 0))],
        out_specs=(pl.BlockSpec((MLC, D_MODEL), lambda c: (c, 0)), pl.BlockSpec((MLC, D_MODEL), lambda c: (c, 0)),
                   col, pl.BlockSpec((1, ML_HEADS, ML_DV, ML_DQK), lambda c: (c, 0, 0, 0)),
                   pl.BlockSpec((1, ML_HEADS, 1, ML_DQK), lambda c: (c, 0, 0, 0))),
        scratch_shapes=[pltpu.VMEM((ML_HEADS, ML_DV, ML_DQK), F32), pltpu.VMEM((ML_HEADS, 1, ML_DQK), F32)],
        compiler_params=_cparams(("arbitrary",)),
    )(pm, pm, pm, pm, a_row, A, wi, em, wk, dec, w_hn)


def _mlstm_bwd(dha, pm, hp_all, den_all, a_row, A, wi, em, wk, dec, cst, nst, w_hn, name):
    S = pm.shape[0]
    NC = S // MLC
    idx, qk, wide, col = _ml_specs(True, NC)

    def body(dha_ref, q_ref, k_ref, v_ref, o_ref, hp_ref, den_ref, arow_ref, A_ref, wi_ref, em_ref, wk_ref,
             dec_ref, cst_ref, nst_ref, whn_ref,
             dqk_ref, dv_ref, do_ref, rk_ref, kc_ref, t_ref, dwhn_ref, dC_scr, dn_scr, t_scr):
        @pl.when(pl.program_id(0) == 0)
        def _():
            dC_scr[...] = jnp.zeros_like(dC_scr)
            dn_scr[...] = jnp.zeros_like(dn_scr)
            t_scr[...] = jnp.zeros_like(t_scr)
            dwhn_ref[...] = jnp.zeros_like(dwhn_ref)

        lanes = lax.broadcasted_iota(jnp.int32, (MLC, LANES), 1)
        lane1 = lax.broadcasted_iota(jnp.int32, (1, LANES), 1)
        t_ref[0, 0:1, :] = t_scr[...]
        rk_tile = jnp.zeros((MLC, LANES), F32)
        kc_tile = jnp.zeros((MLC, LANES), F32)
        t_new = jnp.zeros((1, LANES), F32)
        for h in range(ML_HEADS):
            hs = slice(h * ML_DQK, (h + 1) * ML_DQK)
            vs = slice(h * ML_DV, (h + 1) * ML_DV)
            hp = hp_ref[:, vs]
            sig = jax.nn.sigmoid(o_ref[:, vs])
            whn = whn_ref[:, vs]
            r = _rstd(hp)
            dga = dha_ref[:, vs]
            do_ref[:, vs] = (dga * (hp * r * whn) * sig * (1.0 - sig)).astype(do_ref.dtype)
            dhn = dga * sig
            dhp, dwt = _rmsnorm_bwd_math(dhn, hp, whn)
            dwhn_ref[:, vs] += jnp.sum(dwt, axis=0, keepdims=True)
            den = den_ref[:, h:h + 1]
            floor = em_ref[:, h:h + 1]
            D = jnp.maximum(jnp.abs(den), floor)
            dnum = dhp / D
            dh_h = jnp.sum(dhp * hp, axis=1, keepdims=True)
            active = jnp.abs(den) >= floor
            dden = -dh_h / D * jnp.where(active, jnp.sign(den), 0.0)
            phi = jnp.where(active, 0.0, dh_h)
            qb, kb, qf, kf, qk_, W = _ml_intra(q_ref, k_ref, arow_ref, A_ref, h)
            vf = v_ref[:, vs]
            vb = vf.astype(BF16)
            Cb = cst_ref[0, h]
            nrow = nst_ref[0, h]
            wic = wi_ref[:, h:h + 1]
            wkc = wk_ref[:, h:h + 1]
            d = dec_ref[0, :, h:h + 1]
            dCn = dC_scr[h]
            dCb = dCn.astype(BF16)
            dnn = dn_scr[h]
            dnumb = dnum.astype(BF16)
            s = qk_ * W
            ds = (_dot_nt(dnumb, vb) + dden) * W
            dsb = ds.astype(BF16)
            dnw = (wic * dnum).astype(BF16)
            wd = wic * dden
            kw = kf * wkc
            dv_state = _dot_nt(kw.astype(BF16), dCb)
            dq = _dot_nn(dsb, kb) + _dot_nn(dnw, Cb) + wd * nrow
            dk_state = wkc * (_dot_nn(vb, dCb) + dnn)
            dk = _dot_tn(dsb, qb) + dk_state
            dv = _dot_tn(s.astype(BF16), dnumb) + dv_state
            dC = d * dCn + _dot_tn(dnw, qb)
            dn = d * dnn + jnp.sum(wd * qf, axis=0, keepdims=True)
            dC_scr[h] = dC
            dn_scr[h] = dn
            dqk_ref[:, hs] = (dq * _ML_SCALE).astype(dqk_ref.dtype)
            dqk_ref[:, C_KM + h * ML_DQK:C_KM + (h + 1) * ML_DQK] = dk.astype(dqk_ref.dtype)
            dv_ref[:, vs] = dv.astype(dv_ref.dtype)
            G = ds * qk_
            inter = _dot_nt(qb, Cb)
            qn = jnp.sum(qf * nrow, axis=1, keepdims=True)
            R = (jnp.sum(G, axis=1, keepdims=True)
                 + wic * (jnp.sum(dnum * inter, axis=1, keepdims=True) + dden * qn))
            K = jnp.sum(G.T, axis=1, keepdims=True) + jnp.sum(kf * dk_state, axis=1, keepdims=True)
            rk_tile = jnp.where(lanes == h, R - K, rk_tile)
            kc_tile = jnp.where(lanes == h, phi, kc_tile)
            tt = (jnp.sum(jnp.sum(dC * Cb.astype(F32), axis=1, keepdims=True), axis=0, keepdims=True)
                  + jnp.sum(dn * nrow, axis=1, keepdims=True))
            t_new = jnp.where(lane1 == h, tt, t_new)
        rk_ref[...] = rk_tile
        kc_ref[...] = kc_tile
        t_ref[0, 1:2, :] = t_new
        t_scr[...] = t_new

    act = lambda n: jax.ShapeDtypeStruct((S, n), BF16)
    cs = jax.ShapeDtypeStruct((S, LANES), F32)
    rowblk = lambda n: pl.BlockSpec((MLC, n), lambda c: (idx(c), 0))
    return pl.pallas_call(
        body, name=name,
        out_shape=(act(D_MODEL), act(D_MODEL), act(D_MODEL), cs, cs,
                   jax.ShapeDtypeStruct((NC, 2, LANES), F32), jax.ShapeDtypeStruct((1, D_MODEL), F32)),
        grid=(NC,),
        in_specs=[rowblk(D_MODEL), qk(C_QM // 512), qk(C_KM // 512), wide(C_VM // D_MODEL), wide(C_OM // D_MODEL),
                  rowblk(D_MODEL), col, pl.BlockSpec((8, MLC), lambda c: (0, idx(c))), col, col, col, col,
                  pl.BlockSpec((1, 1, LANES), lambda c: (idx(c), 0, 0)),
                  pl.BlockSpec((1, ML_HEADS, ML_DV, ML_DQK), lambda c: (idx(c), 0, 0, 0)),
                  pl.BlockSpec((1, ML_HEADS, 1, ML_DQK), lambda c: (idx(c), 0, 0, 0)),
                  pl.BlockSpec((1, D_MODEL), lambda c: (0, 0))],
        out_specs=(rowblk(D_MODEL), rowblk(D_MODEL), rowblk(D_MODEL), col, col,
                   pl.BlockSpec((1, 2, LANES), lambda c: (idx(c), 0, 0)), pl.BlockSpec((1, D_MODEL), lambda c: (0, 0))),
        scratch_shapes=[pltpu.VMEM((ML_HEADS, ML_DV, ML_DQK), F32), pltpu.VMEM((ML_HEADS, 1, ML_DQK), F32),
                        pltpu.VMEM((1, LANES), F32)],
        compiler_params=_cparams(("arbitrary",)),
    )(dha, pm, pm, pm, pm, hp_all, den_all, a_row, A, wi, em, wk, dec, cst, nst, w_hn)


_FOX_SCALE = FOX_DH ** -0.5
_NEG = -1e30
_LOG2E = 1.4426950408889634
_LN2 = 0.6931471805599453
_QF_BLK, _KF_BLK, _VF_BLK = 0, FOX_HEADS, 2 * FOX_HEADS


def _lane_pick(tile, lane):
    lanes = lax.broadcasted_iota(jnp.int32, tile.shape, 1)
    return jnp.sum(jnp.where(lanes == lane, tile, 0.0), axis=1, keepdims=True)


def _col_to_row(col):
    return jnp.max(jnp.broadcast_to(col, (col.shape[0], LANES)).T, axis=0, keepdims=True)


def _causal(q0, k0, shape, q_axis):
    qpos = q0 + lax.broadcasted_iota(jnp.int32, shape, q_axis)
    kpos = k0 + lax.broadcasted_iota(jnp.int32, shape, 1 - q_axis)
    return kpos <= qpos


def _fox_fwd(pf, fc, fk_row, name):
    S = pf.shape[0]
    TQ, TK = FOX_TQ, FOX_TK
    nq, nk = S // TQ, S // TK
    c1 = _FOX_SCALE * _LOG2E

    def body(q_ref, k_ref, v_ref, fc_ref, fr_ref, o_ref, lse_ref):
        h, i = pl.program_id(0), pl.program_id(1)
        qb = q_ref[...]
        fq2 = _lane_pick(fc_ref[...], h) * _LOG2E

        def step(j, carry, masked):
            m, l, acc = carry
            off = pl.multiple_of(j * TK, TK)
            t = _dot_nt(qb, k_ref[pl.ds(off, TK), :]) * c1 - fr_ref[0, j] * _LOG2E
            if masked:
                t = jnp.where(_causal(i * TQ, j * TK, (TQ, TK), 0), t, _NEG)
            m_new = jnp.maximum(m, jnp.max(t, axis=1, keepdims=True) + fq2)
            alpha = jnp.exp2(m - m_new)
            p = jnp.exp2(t + (fq2 - m_new))
            l = alpha * l + jnp.sum(p, axis=1, keepdims=True)
            acc = alpha * acc + _dot_nn(p.astype(BF16), v_ref[pl.ds(off, TK), :])
            return m_new, l, acc

        init = (jnp.full((TQ, 1), _NEG, F32), jnp.zeros((TQ, 1), F32), jnp.zeros((TQ, FOX_DH), F32))
        last = (i * TQ) // TK
        carry = lax.fori_loop(0, last, lambda j, c: step(j, c, False), init)
        m, l, acc = step(last, carry, True)
        o_ref[...] = (acc / l).astype(o_ref.dtype)
        lse_ref[0, 0] = _col_to_row((m + jnp.log2(l)) * _LN2)

    head = lambda blk: pl.BlockSpec((S, FOX_DH), lambda h, i: (0, blk + h))
    return pl.pallas_call(
        body, name=name,
        out_shape=(jax.ShapeDtypeStruct((S, D_MODEL), BF16), jax.ShapeDtypeStruct((FOX_HEADS, nq, 1, TQ), F32)),
        grid=(FOX_HEADS, nq),
        in_specs=[pl.BlockSpec((TQ, FOX_DH), lambda h, i: (i, _QF_BLK + h)), head(_KF_BLK), head(_VF_BLK),
                  pl.BlockSpec((TQ, LANES), lambda h, i: (i, 0)),
                  pl.BlockSpec((1, nk, 1, TK), lambda h, i: (h, 0, 0, 0))],
        out_specs=(pl.BlockSpec((TQ, FOX_DH), lambda h, i: (i, h)),
                   pl.BlockSpec((1, 1, 1, TQ), lambda h, i: (h, i, 0, 0))),
        compiler_params=_cparams(("parallel", "arbitrary")),
    )(pf, pf, pf, fc, fk_row)


def _fox_bwd(dhb, hb, pf, lse_row, fq_row, fc, name):
    S = pf.shape[0]
    TQ, TK = FOX_TQ, FOX_TK
    nq, nk, r = S // TQ, S // TK, TK // TQ
    c1 = _FOX_SCALE * _LOG2E

    def body(q_ref, k_ref, v_ref, do_ref, o_ref, lse_ref, fq_ref, fc_ref,
             dq_ref, dk_ref, dv_ref, dFk_ref, dFq_ref, dq_acc, qside, delta, dk_acc, dv_acc, cs_acc):
        h, j = pl.program_id(0), pl.program_id(1)

        @pl.when(j == 0)
        def _():
            dq_acc[...] = jnp.zeros_like(dq_acc)
            dFq_ref[...] = jnp.zeros_like(dFq_ref)

            def fill(b, _):
                off = pl.multiple_of(b * TQ, TQ)
                prod = do_ref[pl.ds(off, TQ), :].astype(F32) * o_ref[pl.ds(off, TQ), :].astype(F32)
                delta[b] = jnp.sum(prod.T, axis=0, keepdims=True)
                qside[b] = (fq_ref[0, b] - lse_ref[0, b]) * _LOG2E
                return 0

            lax.fori_loop(0, nq, fill, 0)

        kb = k_ref[...]
        vb = v_ref[...]
        fk2 = _lane_pick(fc_ref[...], h) * _LOG2E
        dk_acc[...] = jnp.zeros_like(dk_acc)
        dv_acc[...] = jnp.zeros_like(dv_acc)
        cs_acc[...] = jnp.zeros_like(cs_acc)

        def step(i, masked):
            off = pl.multiple_of(i * TQ, TQ)
            qb = q_ref[pl.ds(off, TQ), :]
            dob = do_ref[pl.ds(off, TQ), :]
            t = _dot_nt(kb, qb) * c1 + qside[i] - fk2
            if masked:
                t = jnp.where(_causal(i * TQ, j * TK, (TK, TQ), 1), t, _NEG)
            p = jnp.exp2(t)
            dv_acc[...] += _dot_nn(p.astype(BF16), dob)
            ds = p * (_dot_nt(vb, dob) - delta[i])
            dsb = ds.astype(BF16)
            dk_acc[...] += _dot_nn(dsb, qb)
            dq_acc[pl.ds(off, TQ), :] += _dot_tn(dsb, kb)
            cs_acc[...] += jnp.sum(ds, axis=1, keepdims=True)
            dFq_ref[0, i] += jnp.sum(ds, axis=0, keepdims=True)

        for d in range(r):
            step(r * j + d, True)

        def rest(i, _):
            step(i, False)
            return 0

        lax.fori_loop(r * j + r, nq, rest, 0)
        dk_ref[...] = (dk_acc[...] * _FOX_SCALE).astype(dk_ref.dtype)
        dv_ref[...] = dv_acc[...].astype(dv_ref.dtype)
        dFk_ref[0, 0] = -_col_to_row(cs_acc[...])

        @pl.when(j == nk - 1)
        def _():
            dq_ref[...] = (dq_acc[...] * _FOX_SCALE).astype(dq_ref.dtype)

    head = lambda blk: pl.BlockSpec((S, FOX_DH), lambda h, j: (0, blk + h))
    kblk = lambda blk: pl.BlockSpec((TK, FOX_DH), lambda h, j: (j, blk + h))
    qrows = pl.BlockSpec((1, nq, 1, TQ), lambda h, j: (h, 0, 0, 0))
    act = jax.ShapeDtypeStruct((S, D_MODEL), BF16)
    return pl.pallas_call(
        body, name=name,
        out_shape=(act, act, act, jax.ShapeDtypeStruct((FOX_HEADS, nk, 1, TK), F32),
                   jax.ShapeDtypeStruct((FOX_HEADS, nq, 1, TQ), F32)),
        grid=(FOX_HEADS, nk),
        in_specs=[head(_QF_BLK), kblk(_KF_BLK), kblk(_VF_BLK), head(0), head(0), qrows, qrows,
                  pl.BlockSpec((TK, LANES), lambda h, j: (j, 0))],
        out_specs=(head(0), kblk(0), kblk(0), pl.BlockSpec((1, 1, 1, TK), lambda h, j: (h, j, 0, 0)), qrows),
        scratch_shapes=[pltpu.VMEM((S, FOX_DH), F32), pltpu.VMEM((nq, 1, TQ), F32), pltpu.VMEM((nq, 1, TQ), F32),
                        pltpu.VMEM((TK, FOX_DH), F32), pltpu.VMEM((TK, FOX_DH), F32), pltpu.VMEM((TK, 1), F32)],
        compiler_params=_cparams(("parallel", "arbitrary")),
    )(pf, pf, pf, dhb, hb, lse_row, fq_row, fc)


def _pad_lanes(v):
    return jnp.pad(v, ((0, 0), (0, LANES - v.shape[1])))


def _local_step(x, target, wmain_t, wsmall_t, wa, wb, wout, wup, wdown, p, on_ffn_grads):
    S = x.shape[0]
    bi, bf, bff = _pad_lanes(p["b_ml_i"]), _pad_lanes(p["b_ml_f"]), _pad_lanes(p["b_fox_f"])

    h0 = _rmsnorm_fwd(x, p["norm_mix_pre"], "norm_mix_pre")
    pm = _mm(h0, wmain_t[:N_ML], "nt", F32, "proj_mlstm")
    pf = _mm(h0, wmain_t[N_ML:N_ML + N_FOX], "nt", BF16, "proj_fox")
    pg = _mm(h0, wmain_t[N_ML + N_FOX:], "nt", F32, "proj_merge")
    ps = _mm(h0, wsmall_t, "nt", F32, "proj_gates")
    a, A, wi, em, wk, dec, Fc = _gates_fwd(ps, bi, bf, bff, "gates_fwd")
    a_row = a[:, :8].T
    ha, hp, den, cst, nst = _mlstm_fwd(pm, a_row, A, wi, em, wk, dec, p["ml_head_norm"], "mlstm_fwd")
    ft = Fc[:, :FOX_HEADS].T
    fq_row = ft.reshape(FOX_HEADS, S // FOX_TQ, 1, FOX_TQ)
    fk_row = ft.reshape(FOX_HEADS, S // FOX_TK, 1, FOX_TK)
    hb, lse_row = _fox_fwd(pf, Fc, fk_row, "fox_fwd")
    ya = _mm(ha, wa, "nn", F32, "branch_a")
    yb = _mm(hb, wb, "nn", F32, "branch_b")
    merged = _merge_fwd(ya, yb, pg, p["b_gate_a"], p["b_gate_b"], "merge_fwd")
    z = _mm(merged, wout, "nn", F32, "out_proj")
    x1 = _resid_norm_fwd(x, z, p["norm_mix_post"], "resid_mix")
    h2 = _rmsnorm_fwd(x1, p["norm_ffn_pre"], "norm_ffn_pre")
    up = _mm(h2, wup, "nn", F32, "ffn_up")
    act = _conv_act_fwd(up, p["conv_w"], p["conv_b"], "conv_act_fwd")
    d = _mm(act, wdown, "nn", F32, "ffn_down")
    loss_row, dy, dd, g_norm_ffn_post = _loss_head(x1, d, p["norm_ffn_post"], target, "loss_head")
    dact = _mm(dd, wdown, "nt", F32, "d_act")
    g_wdown = _mm(act, dd, "tn", F32, "dw_down")
    dupa, dupg, dcwa, dcwg, dcba, dcbg = _conv_act_bwd(up, dact, p["conv_w"], p["conv_b"], "conv_act_bwd")
    g_conv_w = jnp.concatenate([dcwa, dcwg], axis=1)
    g_conv_b = jnp.concatenate([dcba, dcbg], axis=1)
    dh2 = _mm([dupa, dupg], wup, "nt", F32, "d_h2")
    g_wup = _mm(h2, [dupa, dupg], "tn", F32, "dw_up")
    token = on_ffn_grads(g_wdown, g_wup)
    dx1, g_norm_ffn_pre = _rmsnorm_bwd([dh2], x1, p["norm_ffn_pre"] + token[0:1, 0:1], dy, F32, "norm_ffn_pre_bwd")
    dz, g_norm_mix_post = _rmsnorm_bwd([dx1], z, p["norm_mix_post"], None, BF16, "norm_mix_post_bwd")
    dmerged = _mm(dz, wout, "nt", F32, "d_merged")
    g_wout = _mm(merged, dz, "tn", F32, "dw_out")
    dya, dyb, dga, dgb, g_b_gate_a, g_b_gate_b = _merge_bwd(dmerged, ya, yb, pg, p["b_gate_a"], p["b_gate_b"], "merge_bwd")
    dha = _mm(dya, wa, "nt", F32, "d_ha")
    g_wa = _mm(ha, dya, "tn", F32, "dw_a")
    dhb = _mm(dyb, wb, "nt", BF16, "d_hb")
    g_wb = _mm(hb, dyb, "tn", F32, "dw_b")
    dqkm, dvm, dom, rk, kc, tch, g_ml_head_norm = _mlstm_bwd(
        dha, pm, hp, den, a_row, A, wi, em, wk, dec, cst, nst, p["ml_head_norm"], "mlstm_bwd")
    dqf, dkf, dvf, dFk, dFq = _fox_bwd(dhb, hb, pf, lse_row, fq_row, Fc, "fox_bwd")
    dF = jnp.pad((dFk.reshape(FOX_HEADS, S) + dFq.reshape(FOX_HEADS, S)).T, ((0, 0), (0, LANES - FOX_HEADS)))
    dps, dbias = _gates_bwd(ps, bi, bf, bff, rk, kc, tch, dF, "gates_bwd")
    dpm = [dqkm, dvm, dom, dqf, dkf, dvf, dga, dgb]
    dh0 = _mm(dpm, wmain_t, "nn", F32, "d_h0_main")
    dh0s = _mm(dps, wsmall_t, "nn", F32, "d_h0_gates")
    g_wmain_t = _mm(dpm, h0, "tn", F32, "dw_main")
    g_wsmall_t = _mm(dps, h0, "tn", F32, "dw_gates")
    grad_x, g_norm_mix_pre = _rmsnorm_bwd([dh0, dh0s], x, p["norm_mix_pre"], dx1, F32, "norm_mix_pre_bwd")

    big = dict(wmain_t=g_wmain_t, wsmall_t=g_wsmall_t, w_branch_a=g_wa, w_branch_b=g_wb, w_out=g_wout, w_up=g_wup, w_down=g_wdown)
    small = dict(norm_mix_pre=g_norm_mix_pre, ml_head_norm=g_ml_head_norm, b_gate_a=g_b_gate_a, b_gate_b=g_b_gate_b,
                 norm_mix_post=g_norm_mix_post, norm_ffn_pre=g_norm_ffn_pre, norm_ffn_post=g_norm_ffn_post,
                 conv_b=g_conv_b, b_ml_i=dbias[:, 0:ML_HEADS], b_ml_f=dbias[:, LANES:LANES + ML_HEADS],
                 b_fox_f=dbias[:, 2 * LANES:2 * LANES + FOX_HEADS], conv_w=g_conv_w)
    return loss_row, grad_x, big, small


def _row_tile(r, target=256):
    best = None
    for t in range(8, min(r, target) + 1, 8):
        if r % t == 0:
            best = t
    return best if best is not None else r


def _adamw(w, g, m, v, name):
    _, R, C = w.shape
    tr = _row_tile(R)
    tc = C
    if tr == R and R > 256:
        tc = 256

    def body(w_ref, g_ref, m_ref, v_ref, d_ref, mo_ref, vo_ref):
        gv = g_ref[...]
        mn = ADAM_B1 * m_ref[0] + (1.0 - ADAM_B1) * gv
        vn = ADAM_B2 * v_ref[0] + (1.0 - ADAM_B2) * (gv * gv)
        m_hat = mn / (1.0 - ADAM_B1 ** ADAM_STEP)
        v_hat = vn / (1.0 - ADAM_B2 ** ADAM_STEP)
        d_ref[0] = -ADAM_LR * (m_hat / (jnp.sqrt(v_hat) + ADAM_EPS) + ADAM_WD * w_ref[0])
        mo_ref[0] = mn
        vo_ref[0] = vn

    blk = pl.BlockSpec((1, tr, tc), lambda i, j: (0, i, j))
    o = jax.ShapeDtypeStruct((1, R, C), F32)
    return pl.pallas_call(
        body, name=name, out_shape=(o, o, o), grid=(R // tr, C // tc),
        in_specs=[blk, pl.BlockSpec((tr, tc), lambda i, j: (i, j)), blk, blk], out_specs=(blk,) * 3,
        compiler_params=_cparams(("parallel", "parallel")),
    )(w, g, m, v)


ANY = pl.BlockSpec(memory_space=pl.ANY)


def _place():
    x, y, c = lax.axis_index("x"), lax.axis_index("y"), lax.axis_index("c")
    chips = [(1 - x, y), (x, 1 - y), (1 - x, 1 - y)]
    return x, y, c, chips


def _block(ref, kind, k, rows=None):
    if kind == "rows":
        return ref.at[k] if rows is None else ref.at[k, pl.ds(*rows), :]
    cb = ref.shape[1] // 4
    return ref.at[:, pl.ds(k * cb, cb)] if rows is None else ref.at[pl.ds(*rows), pl.ds(k * cb, cb)]


def _gathered_shape(s, kind):
    return (4,) + s.shape if kind == "rows" else (s.shape[0], 4 * s.shape[1])


def _gather_weights(shards, kinds, smalls):
    n, ns = len(shards), len(smalls)

    def body(*refs):
        ins, sm_in = refs[:n], refs[n:n + ns]
        outs, sm_out = refs[n + ns:2 * n + ns], refs[2 * n + ns:2 * (n + ns)]
        send_sems, recv_sems, sm_send, sm_recv, local_sems = refs[2 * (n + ns):]
        x, y, c, chips = _place()
        sibling = (x, y, 1 - c)
        kme = 2 * x + y

        def half(a, k, hc):
            h = ins[a].shape[0] // 2
            return _block(outs[a], kinds[a], k, (hc * h, h))

        def remote(a, slot, src, dst, to):
            return pltpu.make_async_remote_copy(src_ref=src, dst_ref=dst, send_sem=send_sems.at[a * 7 + slot],
                                                recv_sem=recv_sems.at[a * 7 + slot], device_id=to, device_id_type=MESH)

        def sm_copy(b, j, k, to):
            return pltpu.make_async_remote_copy(src_ref=sm_in[b], dst_ref=sm_out[b].at[k], send_sem=sm_send.at[3 * b + j],
                                                recv_sem=sm_recv.at[3 * b + j], device_id=to, device_id_type=MESH)

        local = [pltpu.make_async_copy(sm_in[b], sm_out[b].at[kme], local_sems.at[b]) for b in range(ns)]
        for cp in local:
            cp.start()
        sends = [remote(a, 6, ins[a], _block(outs[a], kinds[a], kme), sibling) for a in range(n)]
        for a in range(n):
            h = ins[a].shape[0] // 2
            for j, chip in enumerate(chips):
                sends.append(remote(a, j, ins[a].at[pl.ds(c * h, h), :], half(a, kme, c), (*chip, c)))
        for b in range(ns):
            for j, chip in enumerate(chips):
                sends.append(sm_copy(b, j, kme, (*chip, c)))
        for cp in sends:
            cp.start()
        for a in range(n):
            for j, chip in enumerate(chips):
                kj = 2 * chip[0] + chip[1]
                remote(a, j, half(a, kj, c), half(a, kj, c), (*chip, c)).wait_recv()
                fwd = remote(a, 3 + j, half(a, kj, c), half(a, kj, c), sibling)
                fwd.start()
                sends.append(fwd)
        for a in range(n):
            for j, chip in enumerate(chips):
                kj = 2 * chip[0] + chip[1]
                remote(a, 3 + j, half(a, kj, 1 - c), half(a, kj, 1 - c), sibling).wait_recv()
        for b in range(ns):
            for j, chip in enumerate(chips):
                sm_copy(b, j, 2 * chip[0] + chip[1], (*chip, c)).wait_recv()
        for a in range(n):
            remote(a, 6, ins[a], _block(outs[a], kinds[a], kme), sibling).wait_recv()
        for cp in sends:
            cp.wait_send()
        for cp in local:
            cp.wait()

    outs = pl.pallas_call(
        body, name="gather_weights",
        out_shape=tuple([jax.ShapeDtypeStruct(_gathered_shape(s, k), s.dtype) for s, k in zip(shards, kinds)]
                        + [jax.ShapeDtypeStruct((4,) + s.shape, s.dtype) for s in smalls]),
        in_specs=[ANY] * (n + ns), out_specs=tuple([ANY] * (n + ns)),
        scratch_shapes=[pltpu.SemaphoreType.DMA((7 * n,)), pltpu.SemaphoreType.DMA((7 * n,)),
                        pltpu.SemaphoreType.DMA((3 * ns,)), pltpu.SemaphoreType.DMA((3 * ns,)),
                        pltpu.SemaphoreType.DMA((ns,))],
    )(*shards, *smalls)
    return outs[:n], outs[n:]


def _exchange_sibling_halves(gs, kinds, name):
    n = len(gs)
    hshape = lambda g, kind: (4, g.shape[1] // 2, g.shape[2]) if kind == "rows" else (g.shape[0] // 2, g.shape[1])

    def body(*refs):
        ins, outs, send_sems, recv_sems = refs[:n], refs[n:2 * n], refs[2 * n], refs[2 * n + 1]
        x, y, c, _ = _place()
        cps = []
        for a in range(n):
            h = outs[a].shape[-2]
            src = ins[a].at[:, pl.ds((1 - c) * h, h), :] if kinds[a] == "rows" else ins[a].at[pl.ds((1 - c) * h, h), :]
            cps.append(pltpu.make_async_remote_copy(
                src_ref=src, dst_ref=outs[a], send_sem=send_sems.at[a],
                recv_sem=recv_sems.at[a], device_id=(x, y, 1 - c), device_id_type=MESH))
        for cp in cps:
            cp.start()
        for cp in cps:
            cp.wait()

    return pl.pallas_call(
        body, name=name,
        out_shape=tuple(jax.ShapeDtypeStruct(hshape(g, k), g.dtype) for g, k in zip(gs, kinds)),
        in_specs=[ANY] * n, out_specs=tuple([ANY] * n),
        scratch_shapes=[pltpu.SemaphoreType.DMA((n,)), pltpu.SemaphoreType.DMA((n,))],
    )(*gs)


def _add_halves(g, r1, cvec, kind, name):
    def body(c_ref, g_ref, r_ref, o_ref):
        o_ref[...] = (g_ref[...] + r_ref[...]).astype(o_ref.dtype)

    if kind == "rows":
        _, h, C = r1.shape
        tr = _row_tile(h)
        nt = h // tr
        grid = (4, nt)
        g_spec = pl.BlockSpec((1, tr, C), lambda k, i, c_ref: (k, c_ref[0] * nt + i, 0))
        r_spec = pl.BlockSpec((1, tr, C), lambda k, i, c_ref: (k, i, 0))
    else:
        h, C4 = r1.shape
        tr, tc = _row_tile(h), C4 // 4
        nt = h // tr
        grid = (nt, 4)
        g_spec = pl.BlockSpec((tr, tc), lambda i, k, c_ref: (c_ref[0] * nt + i, k))
        r_spec = pl.BlockSpec((tr, tc), lambda i, k, c_ref: (i, k))
    return pl.pallas_call(
        body, name=name, out_shape=jax.ShapeDtypeStruct(r1.shape, BF16),
        grid_spec=pltpu.PrefetchScalarGridSpec(num_scalar_prefetch=1, grid=grid, in_specs=[g_spec, r_spec],
                                               out_specs=r_spec),
        compiler_params=_cparams(("parallel", "parallel")),
    )(cvec, g, r1)


def _chip_copies(ins, lands, send_sems, recv_sems, kinds):
    x, y, c, chips = _place()
    return [pltpu.make_async_remote_copy(
        src_ref=_block(ins[a], kinds[a], 2 * chip[0] + chip[1]), dst_ref=lands[a].at[j],
        send_sem=send_sems.at[3 * a + j], recv_sem=recv_sems.at[3 * a + j], device_id=(*chip, c), device_id_type=MESH)
        for a in range(len(ins)) for j, chip in enumerate(chips)]


def _land_shape(s, kind):
    return (3,) + (s.shape[1:] if kind == "rows" else (s.shape[0], s.shape[1] // 4))


def _exchange_chips(ss, kinds, name):
    n = len(ss)

    def body(*refs):
        cps = _chip_copies(refs[:n], refs[n:2 * n], refs[2 * n], refs[2 * n + 1], kinds)
        for cp in cps:
            cp.start()
        for cp in cps:
            cp.wait()

    return pl.pallas_call(
        body, name=name,
        out_shape=tuple(jax.ShapeDtypeStruct(_land_shape(s, k), s.dtype) for s, k in zip(ss, kinds)),
        in_specs=[ANY] * n, out_specs=tuple([ANY] * n),
        scratch_shapes=[pltpu.SemaphoreType.DMA((3 * n,)), pltpu.SemaphoreType.DMA((3 * n,))],
    )(*ss)


_IN_HBM = pl.BlockSpec(memory_space=pltpu.HBM)
_SEMS = pl.BlockSpec(memory_space=pltpu.SEMAPHORE)
_DATAFLOW = pltpu.SideEffectType.DATAFLOW_SIDE_EFFECTING


def _exchange_chips_start(ss, kinds, name):
    n = len(ss)
    lands = [lax.empty(_land_shape(s, k), s.dtype) for s, k in zip(ss, kinds)]

    def body(*refs):
        for cp in _chip_copies(refs[:n], refs[n:2 * n], refs[2 * n], refs[2 * n + 1], kinds):
            cp.start()
        refs[-1][...] = jnp.zeros_like(refs[-1])

    hbm = lambda t: pltpu.HBM(t.shape, t.dtype)
    return pl.pallas_call(
        body, name=name,
        out_shape=(pltpu.SemaphoreType.DMA((3 * n,)), pltpu.SemaphoreType.DMA((3 * n,)),
                   *[hbm(t) for t in ss], *[hbm(t) for t in lands], jax.ShapeDtypeStruct((8, LANES), F32)),
        in_specs=[_IN_HBM] * (2 * n),
        out_specs=(_SEMS, _SEMS, *[_IN_HBM] * (2 * n), pl.BlockSpec(memory_space=pltpu.VMEM)),
        input_output_aliases={a: 2 + a for a in range(2 * n)},
        compiler_params=pltpu.CompilerParams(has_side_effects=_DATAFLOW),
    )(*[pltpu.with_memory_space_constraint(t, pltpu.HBM) for t in list(ss) + lands])


def _exchange_chips_wait(started, after, kinds, name):
    send_sems, recv_sems = started[0], started[1]
    n = (len(started) - 3) // 2
    bufs = started[2:2 + 2 * n]

    def body(*refs):
        for cp in _chip_copies(refs[:n], refs[n:2 * n], refs[2 * n], refs[2 * n + 1], kinds):
            cp.wait_send()
            cp.wait_recv()

    hbm = lambda t: pltpu.HBM(t.shape, t.dtype)
    outs = pl.pallas_call(
        body, name=name, out_shape=tuple(hbm(t) for t in bufs),
        in_specs=[_IN_HBM] * (2 * n) + [_SEMS, _SEMS, ANY], out_specs=tuple([_IN_HBM] * (2 * n)),
        input_output_aliases={a: a for a in range(2 * n)},
        compiler_params=pltpu.CompilerParams(has_side_effects=_DATAFLOW),
    )(*bufs, send_sems, recv_sems, after)
    return outs[:n], outs[n:]


def _add_chips(s1, r2, kcvec, kind, name):
    _, h, C = r2.shape
    tr = _row_tile(h)
    nt = h // tr

    def body(kc_ref, s_ref, r0_ref, r1_ref, r2_ref, o_ref):
        s = s_ref[0] if kind == "rows" else s_ref[...]
        o_ref[...] = ((s.astype(F32) + r0_ref[0].astype(F32)) + r1_ref[0].astype(F32)) + r2_ref[0].astype(F32)

    peer = lambda j: pl.BlockSpec((1, tr, C), lambda i, kc_ref: (j, i, 0))
    if kind == "rows":
        s_spec = pl.BlockSpec((1, tr, C), lambda i, kc_ref: (kc_ref[0], i, 0))
    else:
        s_spec = pl.BlockSpec((tr, C), lambda i, kc_ref: (i, kc_ref[0]))
    return pl.pallas_call(
        body, name=name, out_shape=jax.ShapeDtypeStruct((2 * h, C), F32),
        grid_spec=pltpu.PrefetchScalarGridSpec(
            num_scalar_prefetch=1, grid=(nt,),
            in_specs=[s_spec, peer(0), peer(1), peer(2)],
            out_specs=pl.BlockSpec((tr, C), lambda i, kc_ref: (kc_ref[1] * nt + i, 0))),
        compiler_params=_cparams(("parallel",)),
    )(kcvec, s1, r2, r2, r2)


def _join_sibling_halves(bufs):
    n = len(bufs)

    def body(*refs):
        ins, outs, send_sems, recv_sems = refs[:n], refs[n:2 * n], refs[2 * n], refs[2 * n + 1]
        x, y, c, _ = _place()
        cps = []
        for a in range(n):
            h = ins[a].shape[0] // 2
            cps.append(pltpu.make_async_remote_copy(
                src_ref=ins[a].at[pl.ds(c * h, h), :], dst_ref=outs[a].at[pl.ds(c * h, h), :], send_sem=send_sems.at[a],
                recv_sem=recv_sems.at[a], device_id=(x, y, 1 - c), device_id_type=MESH))
        for cp in cps:
            cp.start()
        for a in range(n):
            h = ins[a].shape[0] // 2
            theirs = outs[a].at[pl.ds((1 - c) * h, h), :]
            pltpu.make_async_remote_copy(src_ref=theirs, dst_ref=theirs, send_sem=send_sems.at[a],
                                         recv_sem=recv_sems.at[a], device_id=(x, y, 1 - c), device_id_type=MESH).wait_recv()
        for cp in cps:
            cp.wait_send()

    return pl.pallas_call(
        body, name="grads_join",
        out_shape=tuple(jax.ShapeDtypeStruct(b.shape, b.dtype) for b in bufs),
        in_specs=[ANY] * n, out_specs=tuple([ANY] * n), input_output_aliases={a: a for a in range(n)},
        scratch_shapes=[pltpu.SemaphoreType.DMA((n,)), pltpu.SemaphoreType.DMA((n,))],
    )(*bufs)


N_DEV = 8


def _allreduce_small(pack):
    P = pack.shape[0]

    def body(p_ref, o_ref, gath, send_sems, recv_sems):
        x, y, c, _ = _place()
        me = 4 * x + 2 * y + c
        cps = []
        for mask in range(1, N_DEV):
            px = 1 - x if mask & 4 else x
            py = 1 - y if mask & 2 else y
            pc = 1 - c if mask & 1 else c
            cps.append((pltpu.make_async_remote_copy(
                src_ref=p_ref, dst_ref=gath.at[me], send_sem=send_sems.at[mask - 1], recv_sem=recv_sems.at[mask - 1],
                device_id=(px, py, pc), device_id_type=MESH), 4 * px + 2 * py + pc, mask))
        for cp, _, _ in cps:
            cp.start()
        gath[me] = p_ref[...]
        for _, peer, mask in cps:
            pltpu.make_async_remote_copy(
                src_ref=p_ref, dst_ref=gath.at[peer], send_sem=send_sems.at[mask - 1], recv_sem=recv_sems.at[mask - 1],
                device_id=(x, y, c), device_id_type=MESH).wait_recv()
        for cp, _, _ in cps:
            cp.wait_send()
        acc = gath[0]
        for i in range(1, N_DEV):
            acc = acc + gath[i]
        o_ref[...] = acc

    return pl.pallas_call(
        body, name="allreduce_small", out_shape=jax.ShapeDtypeStruct((P, LANES), F32),
        in_specs=[pl.BlockSpec(memory_space=pltpu.VMEM)], out_specs=pl.BlockSpec(memory_space=pltpu.VMEM),
        scratch_shapes=[pltpu.VMEM((N_DEV, P, LANES), F32), pltpu.SemaphoreType.DMA((N_DEV - 1,)),
                        pltpu.SemaphoreType.DMA((N_DEV - 1,))],
    )(pack)


def _pack_rows(arrs):
    rows = []
    for a in arrs:
        f = a.reshape(-1)
        f = jnp.pad(f, (0, (-f.shape[0]) % (8 * LANES)))
        rows.append(f.reshape(-1, LANES))
    return jnp.concatenate(rows, axis=0)


def _unpack_rows(pack, shapes):
    out, r = [], 0
    for s in shapes:
        n = math.prod(s)
        out.append(pack[r:r + -(-n // LANES)].reshape(-1)[:n].reshape(s))
        r += 8 * -(-n // (8 * LANES))
    return out


_SMALL = ["norm_mix_pre", "ml_head_norm", "b_gate_a", "b_gate_b", "norm_mix_post", "norm_ffn_pre", "norm_ffn_post",
          "conv_b", "b_ml_i", "b_ml_f", "b_fox_f"]
_BIG = ["w_in", "w_branch_a", "w_branch_b", "w_out", "w_up", "w_down"]
_WEIGHTS = ['norm_mix_pre', 'w_in', 'b_ml_i', 'b_ml_f', 'ml_head_norm', 'b_fox_f', 'b_gate_a', 'b_gate_b', 'w_branch_a',
            'w_branch_b', 'w_out', 'norm_mix_post', 'norm_ffn_pre', 'w_up', 'conv_w', 'conv_b', 'w_down', 'norm_ffn_post']


_KINDS = ["rows", "rows", "rows", "rows", "cols", "rows"]


def kernel(x, norm_mix_pre, w_in, b_ml_i, b_ml_f, ml_head_norm, b_fox_f, b_gate_a, b_gate_b, w_branch_a, w_branch_b, w_out, norm_mix_post, norm_ffn_pre, w_up, conv_w, conv_b, w_down, norm_ffn_post, loss_target, m_norm_mix_pre, m_w_in, m_b_ml_i, m_b_ml_f, m_ml_head_norm, m_b_fox_f, m_b_gate_a, m_b_gate_b, m_w_branch_a, m_w_branch_b, m_w_out, m_norm_mix_post, m_norm_ffn_pre, m_w_up, m_conv_w, m_conv_b, m_w_down, m_norm_ffn_post, v_norm_mix_pre, v_w_in, v_b_ml_i, v_b_ml_f, v_ml_head_norm, v_b_fox_f, v_b_gate_a, v_b_gate_b, v_w_branch_a, v_w_branch_b, v_w_out, v_norm_mix_post, v_norm_ffn_pre, v_w_up, v_conv_w, v_conv_b, v_w_down, v_norm_ffn_post):
    args = dict(locals())
    w = {n: args[n] for n in _WEIGHTS}
    mom = {n: args["m_" + n] for n in _WEIGHTS}
    var = {n: args["v_" + n] for n in _WEIGHTS}
    cx, cy, cc = lax.axis_index("x"), lax.axis_index("y"), lax.axis_index("c")
    kme = 2 * cx + cy
    cvec = jnp.reshape(cc, (1,)).astype(jnp.int32)
    kcvec = jnp.stack([kme, cc]).astype(jnp.int32)
    odd = kme % 2

    tr3 = lambda t: jnp.transpose(t, (0, 2, 1))
    w["w_in"], mom["w_in"], var["w_in"] = tr3(w_in), tr3(m_w_in), tr3(v_w_in)
    w_in_main = lax.dynamic_slice_in_dim(w["w_in"][0], 4 * odd, 2048, axis=0).astype(BF16)
    w_in_gates = lax.dynamic_slice_in_dim(w["w_in"][0], 2048 * (1 - odd), 4, axis=0).astype(BF16)
    shards = [w_in_main] + [w[n][0].astype(BF16) for n in _BIG[1:]]
    (wmain_t, g_a, g_b, g_out, wup, g_down), (g_cw, g_gates) = _gather_weights(
        shards, _KINDS, [w["conv_w"][0], w_in_gates])
    gate_rows = g_gates.reshape(16, D_MODEL)
    wsmall_t = jnp.zeros((N_SMALL, D_MODEL), BF16)
    for blk, (lo, hi) in enumerate(((0, 4), (4, 8), (8, 16))):
        wsmall_t = wsmall_t.at[blk * LANES:blk * LANES + hi - lo].set(gate_rows[lo:hi])
    full = lambda g: g.reshape(-1, g.shape[2])
    p = {n: w[n] for n in _SMALL}
    p["conv_w"] = jnp.transpose(g_cw, (1, 0, 2)).reshape(3, -1)

    def chip_sums_of(names, whole):
        kinds = [_KINDS[_BIG.index(n)] for n in names]
        from_sibling = _exchange_sibling_halves(whole, kinds, "grads_to_sibling_" + names[0])
        return kinds, [_add_halves(g, r, cvec, k, "add_sibling_" + n) for g, r, k, n in zip(whole, from_sibling, kinds, names)]

    early_names, early = ["w_up", "w_down"], {}

    def on_ffn_grads(g_wdown, g_wup):
        kinds, sums = chip_sums_of(early_names, [g_wup, g_wdown.reshape(4, -1, D_MODEL)])
        early["kinds"] = kinds
        early["started"] = _exchange_chips_start(sums, kinds, "grads_to_chips_ffn_start")
        return early["started"][-1]

    loss_row, grad_x, big, small = _local_step(x[0], loss_target[0], full(wmain_t), wsmall_t, full(g_a), full(g_b),
                                               full(g_out), wup, full(g_down), p, on_ffn_grads)

    late_names = ["w_in", "w_branch_a", "w_branch_b", "w_out"]
    late_kinds, late_sums = chip_sums_of(late_names, [big[n].reshape(4, -1, D_MODEL) for n in (
        "wmain_t", "w_branch_a", "w_branch_b", "w_out")])
    late_got = _exchange_chips(late_sums, late_kinds, "grads_to_chips")
    early_sums, early_got = _exchange_chips_wait(early["started"], late_got[0], early["kinds"], "grads_to_chips_ffn_wait")
    mine = [_add_chips(s, r, kcvec, k, "add_chips_" + n) for s, r, k, n in zip(
        late_sums + list(early_sums), list(late_got) + list(early_got), late_kinds + early["kinds"],
        late_names + early_names)]
    grads = dict(zip(late_names + early_names, _join_sibling_halves(mine)))

    gt = big["wsmall_t"]
    small["w_in_gates"] = jnp.concatenate([gt[0:4], gt[LANES:LANES + 4], gt[2 * LANES:2 * LANES + 8]], axis=0)
    small_names = _SMALL + ["conv_w"]
    packed_names = small_names + ["w_in_gates"]
    pack = _pack_rows([small[n] for n in packed_names] + [loss_row])
    pack = jnp.pad(pack, ((0, (-pack.shape[0]) % 8), (0, 0)))
    full_shapes = [small[n].shape if n in ("conv_w", "w_in_gates") else w[n][0].shape for n in packed_names]
    total = _unpack_rows(_allreduce_small(pack), full_shapes + [loss_row.shape])
    for n, t in zip(packed_names, total):
        grads[n] = t
    loss = total[-1][0, 0]
    grads["conv_w"] = lax.dynamic_slice_in_dim(grads["conv_w"], kme * conv_w.shape[2], conv_w.shape[2], axis=1)
    my_gates = lax.dynamic_slice_in_dim(grads.pop("w_in_gates"), 4 * kme, 4, axis=0)
    g_in = jnp.zeros(w["w_in"].shape[1:], F32)
    g_in = lax.dynamic_update_slice_in_dim(g_in, grads["w_in"], 4 * odd, axis=0)
    grads["w_in"] = lax.dynamic_update_slice_in_dim(g_in, my_gates, 2048 * (1 - odd), axis=0)

    delta, new_m, new_v = {}, {}, {}
    for n in _BIG:
        delta[n], new_m[n], new_v[n] = _adamw(w[n], grads[n], mom[n], var[n], "adamw_" + n)
        grads[n] = grads[n][None]
    for d in (grads, delta, new_m, new_v):
        d["w_in"] = tr3(d["w_in"])
    packs = [_pack_rows([d[n][0] for n in small_names]) for d in (w, mom, var)]
    pad = ((0, (-packs[0].shape[0]) % 8), (0, 0))
    packs = [jnp.pad(t, pad)[None] for t in packs]
    gp = jnp.pad(_pack_rows([grads[n] for n in small_names]), pad)
    shapes = [w[n][0].shape for n in small_names]
    for dst, res in zip((delta, new_m, new_v), _adamw(packs[0], gp, packs[1], packs[2], "adamw_small")):
        for n, t in zip(small_names, _unpack_rows(res[0], shapes)):
            dst[n] = t[None]
    for n in small_names:
        grads[n] = grads[n][None]

    return (loss, grad_x[None], *[grads[n] for n in _WEIGHTS], *[delta[n] for n in _WEIGHTS],
            *[new_m[n] for n in _WEIGHTS], *[new_v[n] for n in _WEIGHTS])
```

```python
import functools
import math

import jax
import jax.numpy as jnp
from jax import lax
from jax.experimental import pallas as pl
from jax.experimental.pallas import tpu as pltpu

F32 = jnp.float32
BF16 = jnp.bfloat16
MESH = pl.DeviceIdType.MESH

D_MODEL = 1024
ML_HEADS = 4
ML_DQK = 128
ML_DV = 256
FOX_HEADS = 8
FOX_DH = 128
D_FF = 2816
GATE_CAP = 15.0
EPS = 1e-6
ADAM_LR, ADAM_B1, ADAM_B2, ADAM_EPS, ADAM_WD, ADAM_STEP = 0.001, 0.9, 0.999, 1e-08, 0.01, 10

LANES = 128
MLC = 128
FOX_TQ = 256
FOX_TK = 512
ROW_T = 512
VMEM_LIMIT = 56 * 1024 * 1024

C_QM, C_KM, C_VM, C_OM = 0, 512, 1024, 2048
N_ML, N_FOX, N_GATE = 3072, 3072, 2048
N_SMALL = 384


def _cparams(sem=None):
    return pltpu.CompilerParams(dimension_semantics=sem, vmem_limit_bytes=VMEM_LIMIT)


def _tile(n, target):
    if n <= target:
        return n
    best = None
    for t in range(LANES, target + 1, LANES):
        if n % t == 0:
            best = t
    assert best is not None, (n, target)
    return best


def _dot(a, b, dims):
    return lax.dot_general(a, b, (dims, ((), ())), preferred_element_type=F32)


def _dot_nn(a, b):
    return _dot(a, b, ((1,), (0,)))


def _dot_nt(a, b):
    return _dot(a, b, ((1,), (1,)))


def _dot_tn(a, b):
    return _dot(a, b, ((0,), (0,)))


_DOTS = {"nn": _dot_nn, "nt": _dot_nt, "tn": _dot_tn}


def _mm(a, b, mode, out_dtype, name, tm=1024, tn=1408, tk=1408):
    a_parts = list(a) if isinstance(a, (list, tuple)) else [a]
    b_parts = list(b) if isinstance(b, (list, tuple)) else [b]
    assert len(a_parts) == 1 or len(b_parts) == 1, name
    a_axes = {"nn": "ik", "nt": "ik", "tn": "ki"}[mode]
    b_axes = {"nn": "kj", "nt": "jk", "tn": "kj"}[mode]
    size, target = {}, dict(i=tm, j=tn, k=tk)
    for parts, axes in ((a_parts, a_axes), (b_parts, b_axes)):
        dims = (parts[0].shape[0], parts[0].shape[1] * len(parts))
        for ax, n in zip(axes, dims):
            assert size.setdefault(ax, n) == n, (name, ax, n, size)
    tile = {}
    for parts, axes in ((a_parts, a_axes), (b_parts, b_axes)):
        if len(parts) > 1:
            tile[axes[1]] = _tile(parts[0].shape[1], target[axes[1]])
    for ax in "ijk":
        tile.setdefault(ax, _tile(size[ax], target[ax]))
    M, N, nk = size["i"], size["j"], size["k"] // tile["k"]
    grid_pos = dict(i=0, j=1, k=2)
    dot = _DOTS[mode]

    def specs(parts, axes):
        blk = (tile[axes[0]], tile[axes[1]])
        if len(parts) == 1:
            return [pl.BlockSpec(blk, lambda *g: (g[grid_pos[axes[0]]], g[grid_pos[axes[1]]]))], None
        bpp = parts[0].shape[1] // blk[1]

        def index(p):
            def f(*g):
                g0, g1 = g[grid_pos[axes[0]]], g[grid_pos[axes[1]]]
                on = g1 // bpp == p
                return jnp.where(on, g0, 0), jnp.where(on, g1 % bpp, 0)
            return f

        return [pl.BlockSpec(blk, index(p)) for p in range(len(parts))], (axes[1], bpp)

    a_specs, a_sel = specs(a_parts, a_axes)
    b_specs, b_sel = specs(b_parts, b_axes)
    na, nb = len(a_parts), len(b_parts)

    def body(*refs):
        a_refs, b_refs, o_ref, acc = refs[:na], refs[na:na + nb], refs[na + nb], refs[na + nb + 1:]

        def accumulate(part):
            if nk == 1:
                o_ref[...] = part.astype(o_ref.dtype)
                return
            acc_ref, = acc
            k = pl.program_id(2)

            @pl.when(k == 0)
            def _():
                acc_ref[...] = part

            @pl.when(k > 0)
            def _():
                acc_ref[...] += part

            @pl.when(k == nk - 1)
            def _():
                o_ref[...] = acc_ref[...].astype(o_ref.dtype)

        sel = a_sel or b_sel
        if sel is None:
            accumulate(dot(a_refs[0][...], b_refs[0][...]))
        else:
            which = pl.program_id(grid_pos[sel[0]]) // sel[1]
            for p in range(max(na, nb)):
                @pl.when(which == p)
                def _(p=p):
                    accumulate(dot(a_refs[p if a_sel else 0][...], b_refs[p if b_sel else 0][...]))

    return pl.pallas_call(
        body, name=name,
        out_shape=jax.ShapeDtypeStruct((M, N), out_dtype),
        grid=(M // tile["i"], N // tile["j"], nk),
        in_specs=a_specs + b_specs,
        out_specs=pl.BlockSpec((tile["i"], tile["j"]), lambda i, j, k: (i, j)),
        scratch_shapes=[pltpu.VMEM((tile["i"], tile["j"]), F32)] if nk > 1 else [],
        compiler_params=_cparams(("parallel", "parallel", "arbitrary")),
    )(*a_parts, *b_parts)


def _rstd(x):
    return lax.rsqrt(jnp.mean(x * x, axis=-1, keepdims=True) + EPS)


def _rmsnorm_fwd(x, g, name):
    S, D = x.shape
    T = _tile(S, ROW_T)

    def body(x_ref, g_ref, o_ref):
        xv = x_ref[...]
        o_ref[...] = (xv * _rstd(xv) * g_ref[...]).astype(o_ref.dtype)

    return pl.pallas_call(
        body, name=name, out_shape=jax.ShapeDtypeStruct((S, D), BF16), grid=(S // T,),
        in_specs=[pl.BlockSpec((T, D), lambda i: (i, 0)), pl.BlockSpec((1, D), lambda i: (0, 0))],
        out_specs=pl.BlockSpec((T, D), lambda i: (i, 0)),
        compiler_params=_cparams(("parallel",)),
    )(x, g)


def _resid_norm_fwd(x, z, g, name):
    S, D = x.shape
    T = _tile(S, ROW_T)

    def body(x_ref, z_ref, g_ref, o_ref):
        zv = z_ref[...]
        o_ref[...] = x_ref[...] + zv * _rstd(zv) * g_ref[...]

    row = pl.BlockSpec((T, D), lambda i: (i, 0))
    return pl.pallas_call(
        body, name=name, out_shape=jax.ShapeDtypeStruct((S, D), F32), grid=(S // T,),
        in_specs=[row, row, pl.BlockSpec((1, D), lambda i: (0, 0))],
        out_specs=row, compiler_params=_cparams(("parallel",)),
    )(x, z, g)


def _rmsnorm_bwd_math(dy, xv, g):
    r = _rstd(xv)
    u = dy * g
    dx = r * u - xv * (r * r * r) * jnp.mean(u * xv, axis=-1, keepdims=True)
    return dx, dy * xv * r


def _rmsnorm_bwd(dys, xin, g, resid, out_dtype, name):
    S, D = xin.shape
    T = _tile(S, ROW_T)
    has_resid = resid is not None
    ndy = len(dys)

    def body(*refs):
        dy_refs, (x_ref, g_ref) = refs[:ndy], refs[ndy:ndy + 2]
        dx_ref, dg_ref = refs[-2:]
        dy = dy_refs[0][...]
        for r in dy_refs[1:]:
            dy = dy + r[...]
        dx, dgt = _rmsnorm_bwd_math(dy, x_ref[...], g_ref[...])
        if has_resid:
            dx = dx + refs[ndy + 2][...]
        dx_ref[...] = dx.astype(dx_ref.dtype)

        @pl.when(pl.program_id(0) == 0)
        def _():
            dg_ref[...] = jnp.zeros_like(dg_ref)

        dg_ref[...] += jnp.sum(dgt, axis=0, keepdims=True)

    row = pl.BlockSpec((T, D), lambda i: (i, 0))
    vec = pl.BlockSpec((1, D), lambda i: (0, 0))
    ins = list(dys) + [xin, g] + ([resid] if has_resid else [])
    return pl.pallas_call(
        body, name=name,
        out_shape=(jax.ShapeDtypeStruct((S, D), out_dtype), jax.ShapeDtypeStruct((1, D), F32)),
        grid=(S // T,), in_specs=[row] * ndy + [row, vec] + ([row] if has_resid else []),
        out_specs=(row, vec), compiler_params=_cparams(("arbitrary",)),
    )(*ins)


def _loss_head(x1, d, g, target, name):
    S, D = x1.shape
    T = _tile(S, ROW_T)

    def body(x_ref, d_ref, g_ref, t_ref, loss_ref, dy_ref, dd_ref, dg_ref):
        dv, gv = d_ref[...], g_ref[...]
        y = x_ref[...] + dv * _rstd(dv) * gv
        diff = y - t_ref[...]
        dy = diff * (1.0 / D)
        dy_ref[...] = dy
        dd, dgt = _rmsnorm_bwd_math(dy, dv, gv)
        dd_ref[...] = dd.astype(dd_ref.dtype)

        @pl.when(pl.program_id(0) == 0)
        def _():
            dg_ref[...] = jnp.zeros_like(dg_ref)
            loss_ref[...] = jnp.zeros_like(loss_ref)

        dg_ref[...] += jnp.sum(dgt, axis=0, keepdims=True)
        part = jnp.sum(jnp.sum(diff * diff, axis=1, keepdims=True), axis=0, keepdims=True)
        loss_ref[...] += (0.5 / D) * part

    row = pl.BlockSpec((T, D), lambda i: (i, 0))
    vec = pl.BlockSpec((1, D), lambda i: (0, 0))
    return pl.pallas_call(
        body, name=name,
        out_shape=(jax.ShapeDtypeStruct((1, LANES), F32), jax.ShapeDtypeStruct((S, D), F32),
                   jax.ShapeDtypeStruct((S, D), BF16), jax.ShapeDtypeStruct((1, D), F32)),
        grid=(S // T,), in_specs=[row, row, vec, row],
        out_specs=(pl.BlockSpec((1, LANES), lambda i: (0, 0)), row, row, vec),
        compiler_params=_cparams(("arbitrary",)),
    )(x1, d, g, target)


def _merge_fwd(ya, yb, pm, ba, bb, name):
    S, D = ya.shape
    T = _tile(S, ROW_T)

    def body(ya_ref, yb_ref, ga_ref, gb_ref, ba_ref, bb_ref, o_ref):
        sa = jax.nn.sigmoid(ga_ref[...] + ba_ref[...])
        sb = jax.nn.sigmoid(gb_ref[...] + bb_ref[...])
        o_ref[...] = (sa * ya_ref[...] + sb * yb_ref[...]).astype(o_ref.dtype)

    row = pl.BlockSpec((T, D), lambda i: (i, 0))
    vec = pl.BlockSpec((1, D), lambda i: (0, 0))
    return pl.pallas_call(
        body, name=name, out_shape=jax.ShapeDtypeStruct((S, D), BF16), grid=(S // T,),
        in_specs=[row, row, pl.BlockSpec((T, D), lambda i: (i, 0)),
                  pl.BlockSpec((T, D), lambda i: (i, 1)), vec, vec],
        out_specs=row, compiler_params=_cparams(("parallel",)),
    )(ya, yb, pm, pm, ba, bb)


def _merge_bwd(dmerged, ya, yb, pm, ba, bb, name):
    S, D = ya.shape
    T = _tile(S, ROW_T)

    def body(dm_ref, ya_ref, yb_ref, ga_ref, gb_ref, ba_ref, bb_ref,
             dya_ref, dyb_ref, dga_ref, dgb_ref, dba_ref, dbb_ref):
        dm = dm_ref[...]
        sa = jax.nn.sigmoid(ga_ref[...] + ba_ref[...])
        sb = jax.nn.sigmoid(gb_ref[...] + bb_ref[...])
        dya_ref[...] = (dm * sa).astype(dya_ref.dtype)
        dyb_ref[...] = (dm * sb).astype(dyb_ref.dtype)
        dga = dm * ya_ref[...] * sa * (1.0 - sa)
        dgb = dm * yb_ref[...] * sb * (1.0 - sb)
        dga_ref[...] = dga.astype(dga_ref.dtype)
        dgb_ref[...] = dgb.astype(dgb_ref.dtype)

        @pl.when(pl.program_id(0) == 0)
        def _():
            dba_ref[...] = jnp.zeros_like(dba_ref)
            dbb_ref[...] = jnp.zeros_like(dbb_ref)

        dba_ref[...] += jnp.sum(dga, axis=0, keepdims=True)
        dbb_ref[...] += jnp.sum(dgb, axis=0, keepdims=True)

    row = pl.BlockSpec((T, D), lambda i: (i, 0))
    vec = pl.BlockSpec((1, D), lambda i: (0, 0))
    act = jax.ShapeDtypeStruct((S, D), BF16)
    v1 = jax.ShapeDtypeStruct((1, D), F32)
    return pl.pallas_call(
        body, name=name, out_shape=(act, act, act, act, v1, v1), grid=(S // T,),
        in_specs=[row, row, row, pl.BlockSpec((T, D), lambda i: (i, 0)),
                  pl.BlockSpec((T, D), lambda i: (i, 1)), vec, vec],
        out_specs=(row, row, row, row, vec, vec), compiler_params=_cparams(("arbitrary",)),
    )(dmerged, ya, yb, pm, pm, ba, bb)


_GELU_C = math.sqrt(2.0 / math.pi)


def _gelu(g):
    t = jnp.tanh(_GELU_C * (g + 0.044715 * g * g * g))
    return 0.5 * g * (1.0 + t), t


def _gelu_grad(g, t):
    return 0.5 * (1.0 + t) + 0.5 * g * (1.0 - t * t) * _GELU_C * (1.0 + 3 * 0.044715 * g * g)


def _shift_down(v, halo_ref, first, rows):
    T = v.shape[0]
    keep = jnp.where(first, 0.0, 1.0)
    h7 = halo_ref[7:8, :] * keep
    h6 = halo_ref[6:7, :] * keep
    m1 = jnp.where(rows == 0, h7, pltpu.roll(v, 1, 0))
    m2 = jnp.where(rows == 0, h6, jnp.where(rows == 1, h7, pltpu.roll(v, 2, 0)))
    return m1, m2


def _conv_act_fwd(up, cw, cb, name):
    S, F2 = up.shape
    Fh = F2 // 2
    T = _tile(S, ROW_T)
    tc = _tile(Fh, 256)
    ncol = Fh // tc
    hb = T // 8

    def body(ua_ref, ug_ref, ha_ref, hg_ref, wa_ref, wg_ref, ba_ref, bg_ref, o_ref):
        first = pl.program_id(0) == 0
        rows = lax.broadcasted_iota(jnp.int32, (T, tc), 0)

        def conv(u_ref, h_ref, w_ref, b_ref):
            v = u_ref[...]
            m1, m2 = _shift_down(v, h_ref, first, rows)
            return b_ref[...] + w_ref[0:1, :] * m2 + w_ref[1:2, :] * m1 + w_ref[2:3, :] * v

        a = conv(ua_ref, ha_ref, wa_ref, ba_ref)
        g = conv(ug_ref, hg_ref, wg_ref, bg_ref)
        o_ref[...] = (_gelu(g)[0] * a).astype(o_ref.dtype)

    halo = lambda off: pl.BlockSpec((8, tc), lambda i, j: (jnp.maximum(i * hb - 1, 0), j + off))
    return pl.pallas_call(
        body, name=name, out_shape=jax.ShapeDtypeStruct((S, Fh), BF16), grid=(S // T, ncol),
        in_specs=[pl.BlockSpec((T, tc), lambda i, j: (i, j)), pl.BlockSpec((T, tc), lambda i, j: (i, j + ncol)),
                  halo(0), halo(ncol),
                  pl.BlockSpec((3, tc), lambda i, j: (0, j)), pl.BlockSpec((3, tc), lambda i, j: (0, j + ncol)),
                  pl.BlockSpec((1, tc), lambda i, j: (0, j)), pl.BlockSpec((1, tc), lambda i, j: (0, j + ncol))],
        out_specs=pl.BlockSpec((T, tc), lambda i, j: (i, j)),
        compiler_params=_cparams(("parallel", "parallel")),
    )(up, up, up, up, cw, cw, cb, cb)


def _conv_act_bwd(up, dact, cw, cb, name):
    S, F2 = up.shape
    Fh = F2 // 2
    T = _tile(S, ROW_T)
    tc = _tile(Fh, 256)
    ncol, nrow, hb, nhb = Fh // tc, S // T, T // 8, S // 8

    def body(ua_ref, ug_ref, ha_ref, hg_ref, na_ref, ng_ref, wa_ref, wg_ref, ba_ref, bg_ref, da_ref, dn_ref,
             dpa_ref, dpg_ref, dwa_ref, dwg_ref, dba_ref, dbg_ref, dua_n, dug_n):
        i = pl.program_id(1)
        first = i == 0
        rows = lax.broadcasted_iota(jnp.int32, (T, tc), 0)
        rows8 = lax.broadcasted_iota(jnp.int32, (8, tc), 0)

        def conv(v, m1, m2, w_ref, b_ref):
            return b_ref[...] + w_ref[0:1, :] * m2 + w_ref[1:2, :] * m1 + w_ref[2:3, :] * v

        def du_of(a, g, dact_v):
            gel, t = _gelu(g)
            return dact_v * gel, dact_v * a * _gelu_grad(g, t)

        va, vg = ua_ref[...], ug_ref[...]
        a1, a2 = _shift_down(va, ha_ref, first, rows)
        g1, g2 = _shift_down(vg, hg_ref, first, rows)
        dua, dug = du_of(conv(va, a1, a2, wa_ref, ba_ref), conv(vg, g1, g2, wg_ref, bg_ref), da_ref[...])

        @pl.when(first)
        def _():
            for r in (dwa_ref, dwg_ref, dba_ref, dbg_ref):
                r[...] = jnp.zeros_like(r)

        for du, taps, dw_ref, db_ref in ((dua, (a2, a1, va), dwa_ref, dba_ref), (dug, (g2, g1, vg), dwg_ref, dbg_ref)):
            db_ref[...] += jnp.sum(du, axis=0, keepdims=True)
            for j in range(3):
                dw_ref[j:j + 1, :] += jnp.sum(du * taps[j], axis=0, keepdims=True)

        def below(n_ref, u_ref):
            v = n_ref[...]
            l1, l2 = u_ref[T - 1:T, :], u_ref[T - 2:T - 1, :]
            m1 = jnp.where(rows8 == 0, l1, pltpu.roll(v, 1, 0))
            m2 = jnp.where(rows8 == 0, l2, jnp.where(rows8 == 1, l1, pltpu.roll(v, 2, 0)))
            return v, m1, m2

        keep = jnp.where(i == nrow - 1, 0.0, 1.0)
        na, ng = below(na_ref, ua_ref), below(ng_ref, ug_ref)
        dna, dng = du_of(conv(*na, wa_ref, ba_ref), conv(*ng, wg_ref, bg_ref), dn_ref[...] * keep)
        dua_n[...] = dna
        dug_n[...] = dng

        for du, n_ref, w_ref, o_ref in ((dua, dua_n, wa_ref, dpa_ref), (dug, dug_n, wg_ref, dpg_ref)):
            n0, n1 = n_ref[0:1, :], n_ref[1:2, :]
            p1 = jnp.where(rows == T - 1, n0, pltpu.roll(du, T - 1, 0))
            p2 = jnp.where(rows == T - 2, n0, jnp.where(rows == T - 1, n1, pltpu.roll(du, T - 2, 0)))
            o_ref[...] = (w_ref[2:3, :] * du + w_ref[1:2, :] * p1 + w_ref[0:1, :] * p2).astype(o_ref.dtype)

    tile = lambda off: pl.BlockSpec((T, tc), lambda j, i: (i, j + off))
    above = lambda off: pl.BlockSpec((8, tc), lambda j, i: (jnp.maximum(i * hb - 1, 0), j + off))
    under = lambda off: pl.BlockSpec((8, tc), lambda j, i: (jnp.minimum((i + 1) * hb, nhb - 1), j + off))
    vec = lambda n, off: pl.BlockSpec((n, tc), lambda j, i: (0, j + off))
    act = jax.ShapeDtypeStruct((S, Fh), BF16)
    return pl.pallas_call(
        body, name=name,
        out_shape=(act, act, jax.ShapeDtypeStruct((3, Fh), F32), jax.ShapeDtypeStruct((3, Fh), F32),
                   jax.ShapeDtypeStruct((1, Fh), F32), jax.ShapeDtypeStruct((1, Fh), F32)),
        grid=(ncol, nrow),
        in_specs=[tile(0), tile(ncol), above(0), above(ncol), under(0), under(ncol),
                  vec(3, 0), vec(3, ncol), vec(1, 0), vec(1, ncol), tile(0), under(0)],
        out_specs=(tile(0), tile(0), vec(3, 0), vec(3, 0), vec(1, 0), vec(1, 0)),
        scratch_shapes=[pltpu.VMEM((8, tc), F32), pltpu.VMEM((8, tc), F32)],
        compiler_params=_cparams(("parallel", "arbitrary")),
    )(up, up, up, up, up, up, cw, cw, cb, cb, dact, dact)


def _split3(x):
    hi = x.astype(BF16)
    r1 = x - hi.astype(F32)
    mid = r1.astype(BF16)
    lo = (r1 - mid.astype(F32)).astype(BF16)
    return hi, mid, lo


def _tri_dot(tri, x):
    hi, mid, lo = _split3(x)
    return _dot_nn(tri, hi) + _dot_nn(tri, mid) + _dot_nn(tri, lo)


def _log_sigmoid(x):
    return jnp.minimum(x, 0.0) - jnp.log(1.0 + jnp.exp(-jnp.abs(x)))


def _tri_mask(n, lower):
    r = lax.broadcasted_iota(jnp.int32, (n, n), 0)
    c = lax.broadcasted_iota(jnp.int32, (n, n), 1)
    return (r >= c) if lower else (r <= c)


def _gates_fwd(ps, bi, bf, bff, name):
    S = ps.shape[0]
    NC = S // MLC

    def body(ps_ref, bi_ref, bf_ref, bff_ref, a_ref, A_ref, wi_ref, em_ref, wk_ref, dec_ref, F_ref, m_scr, f_scr):
        @pl.when(pl.program_id(0) == 0)
        def _():
            m_scr[...] = jnp.zeros_like(m_scr)
            f_scr[...] = jnp.zeros_like(f_scr)

        rows = lax.broadcasted_iota(jnp.int32, (MLC, LANES), 0)
        ltri = _tri_mask(MLC, True).astype(BF16)
        li = GATE_CAP * jnp.tanh((ps_ref[:, 0:LANES] + bi_ref[...]) / GATE_CAP)
        lf = _log_sigmoid(GATE_CAP * jnp.tanh((ps_ref[:, LANES:2 * LANES] + bf_ref[...]) / GATE_CAP))
        b = _tri_dot(ltri, lf)
        a = li - b
        cm = a
        sh = 1
        while sh < MLC:
            cm = jnp.where(rows >= sh, jnp.maximum(cm, pltpu.roll(cm, sh, 0)), cm)
            sh *= 2
        m0 = m_scr[...]
        A = jnp.maximum(cm, m0)
        a_ref[...] = a
        A_ref[...] = A
        A_last = A_ref[MLC - 1:MLC, :]
        wi_ref[...] = jnp.exp(m0 - A)
        em_ref[...] = jnp.exp(-(b + A))
        wk_ref[...] = jnp.exp(a - A_last)
        dec_ref[0] = jnp.exp(m0 - A_last)
        F_ref[...] = b
        m_scr[...] = F_ref[MLC - 1:MLC, :] + A_last
        lfg = _log_sigmoid(ps_ref[:, 2 * LANES:3 * LANES] + bff_ref[...])
        F_ref[...] = _tri_dot(ltri, lfg) + f_scr[...]
        f_scr[...] = F_ref[MLC - 1:MLC, :]

    col = pl.BlockSpec((MLC, LANES), lambda c: (c, 0))
    vec = pl.BlockSpec((1, LANES), lambda c: (0, 0))
    cs = jax.ShapeDtypeStruct((S, LANES), F32)
    return pl.pallas_call(
        body, name=name,
        out_shape=(cs, cs, cs, cs, cs, jax.ShapeDtypeStruct((NC, 1, LANES), F32), cs),
        grid=(NC,), in_specs=[pl.BlockSpec((MLC, N_SMALL), lambda c: (c, 0)), vec, vec, vec],
        out_specs=(col, col, col, col, col, pl.BlockSpec((1, 1, LANES), lambda c: (c, 0, 0)), col),
        scratch_shapes=[pltpu.VMEM((1, LANES), F32), pltpu.VMEM((1, LANES), F32)],
        compiler_params=_cparams(("arbitrary",)),
    )(ps, bi, bf, bff)


def _gates_bwd(ps, bi, bf, bff, rk, kc, tch, dF, name):
    S = ps.shape[0]
    NC = S // MLC

    def body(ps_ref, bi_ref, bf_ref, bff_ref, rk_ref, kc_ref, t_ref, dF_ref, dps_ref, db_ref, carry):
        @pl.when(pl.program_id(0) == 0)
        def _():
            carry[...] = jnp.zeros_like(carry)
            db_ref[...] = jnp.zeros_like(db_ref)

        lanes = lax.broadcasted_iota(jnp.int32, (MLC, LANES), 1)
        utri = _tri_mask(MLC, False).astype(BF16)
        ti = jnp.tanh((ps_ref[:, 0:LANES] + bi_ref[...]) / GATE_CAP)
        t_end, t_start = t_ref[0, 0:1, :], t_ref[0, 1:2, :]
        rk = rk_ref[...]
        rk = rk - (jnp.sum(rk, axis=0, keepdims=True) - (t_start - t_end)) * (1.0 / MLC)
        dpi = jnp.where(lanes < ML_HEADS, (kc_ref[...] - rk) * (1.0 - ti * ti), 0.0)
        tf = jnp.tanh((ps_ref[:, LANES:2 * LANES] + bf_ref[...]) / GATE_CAP)
        dlf = _tri_dot(utri, rk) + t_end
        dpf = jnp.where(lanes < ML_HEADS, dlf * jax.nn.sigmoid(-GATE_CAP * tf) * (1.0 - tf * tf), 0.0)
        dFv = dF_ref[...]
        dlfg = _tri_dot(utri, dFv) + carry[...]
        carry[...] += jnp.sum(dFv, axis=0, keepdims=True)
        dpff = jnp.where(lanes < FOX_HEADS, dlfg * jax.nn.sigmoid(-(ps_ref[:, 2 * LANES:3 * LANES] + bff_ref[...])), 0.0)
        for n, dp in enumerate((dpi, dpf, dpff)):
            dps_ref[:, n * LANES:(n + 1) * LANES] = dp.astype(dps_ref.dtype)
            db_ref[:, n * LANES:(n + 1) * LANES] += jnp.sum(dp, axis=0, keepdims=True)

    rev = lambda c: (NC - 1 - c, 0)
    col = pl.BlockSpec((MLC, LANES), rev)
    vec = pl.BlockSpec((1, LANES), lambda c: (0, 0))
    wide = pl.BlockSpec((MLC, N_SMALL), rev)
    return pl.pallas_call(
        body, name=name,
        out_shape=(jax.ShapeDtypeStruct((S, N_SMALL), BF16), jax.ShapeDtypeStruct((1, N_SMALL), F32)),
        grid=(NC,),
        in_specs=[wide, vec, vec, vec, col, col, pl.BlockSpec((1, 2, LANES), lambda c: (NC - 1 - c, 0, 0)), col],
        out_specs=(wide, pl.BlockSpec((1, N_SMALL), lambda c: (0, 0))),
        scratch_shapes=[pltpu.VMEM((1, LANES), F32)],
        compiler_params=_cparams(("arbitrary",)),
    )(ps, bi, bf, bff, rk, kc, tch, dF)


_ML_SCALE = ML_DQK ** -0.5


def _ml_specs(rev, NC):
    idx = (lambda c: NC - 1 - c) if rev else (lambda c: c)
    qk = lambda blk: pl.BlockSpec((MLC, ML_HEADS * ML_DQK), lambda c: (idx(c), blk))
    wide = lambda blk: pl.BlockSpec((MLC, D_MODEL), lambda c: (idx(c), blk))
    col = pl.BlockSpec((MLC, LANES), lambda c: (idx(c), 0))
    return idx, qk, wide, col


def _ml_intra(q_ref, k_ref, arow_ref, A_ref, h):
    hs = slice(h * ML_DQK, (h + 1) * ML_DQK)
    qf = q_ref[:, hs] * _ML_SCALE
    kf = k_ref[:, hs]
    qb, kb = qf.astype(BF16), kf.astype(BF16)
    qk = _dot_nt(qb, kb)
    logw = arow_ref[h:h + 1, :] - A_ref[:, h:h + 1]
    W = jnp.exp(jnp.where(_tri_mask(MLC, True), logw, -1e30))
    return qb, kb, qf, kf, qk, W


def _mlstm_fwd(pm, a_row, A, wi, em, wk, dec, w_hn, name):
    S = pm.shape[0]
    NC = S // MLC
    _, qk, wide, col = _ml_specs(False, NC)

    def body(q_ref, k_ref, v_ref, o_ref, arow_ref, A_ref, wi_ref, em_ref, wk_ref, dec_ref, whn_ref,
             ha_ref, hp_ref, den_ref, cst_ref, nst_ref, C_scr, n_scr):
        @pl.when(pl.program_id(0) == 0)
        def _():
            C_scr[...] = jnp.zeros_like(C_scr)
            n_scr[...] = jnp.zeros_like(n_scr)

        lanes = lax.broadcasted_iota(jnp.int32, (MLC, LANES), 1)
        den_tile = jnp.zeros((MLC, LANES), F32)
        for h in range(ML_HEADS):
            vs = slice(h * ML_DV, (h + 1) * ML_DV)
            qb, kb, qf, kf, qk_, W = _ml_intra(q_ref, k_ref, arow_ref, A_ref, h)
            vb = v_ref[:, vs].astype(BF16)
            Cf = C_scr[h]
            Cb = Cf.astype(BF16)
            nrow = n_scr[h]
            cst_ref[0, h] = Cb
            nst_ref[0, h] = nrow
            s = qk_ * W
            wic = wi_ref[:, h:h + 1]
            num = _dot_nn(s.astype(BF16), vb) + wic * _dot_nt(qb, Cb)
            den = jnp.sum(s, axis=1, keepdims=True) + wic * jnp.sum(qf * nrow, axis=1, keepdims=True)
            hp = num / jnp.maximum(jnp.abs(den), em_ref[:, h:h + 1])
            hp_ref[:, vs] = hp
            den_tile = jnp.where(lanes == h, den, den_tile)
            hn = hp * _rstd(hp) * whn_ref[:, vs]
            ha_ref[:, vs] = (hn * jax.nn.sigmoid(o_ref[:, vs])).astype(ha_ref.dtype)
            wkc = wk_ref[:, h:h + 1]
            kw = kf * wkc
            d = dec_ref[0, :, h:h + 1]
            C_scr[h] = d * Cf + _dot_tn(vb, kw.astype(BF16))
            n_scr[h] = d * nrow + jnp.sum(kw, axis=0, keepdims=True)
        den_ref[...] = den_tile

    return pl.pallas_call(
        body, name=name,
        out_shape=(jax.ShapeDtypeStruct((S, D_MODEL), BF16), jax.ShapeDtypeStruct((S, D_MODEL), F32),
                   jax.ShapeDtypeStruct((S, LANES), F32),
                   jax.ShapeDtypeStruct((NC, ML_HEADS, ML_DV, ML_DQK), BF16),
                   jax.ShapeDtypeStruct((NC, ML_HEADS, 1, ML_DQK), F32)),
        grid=(NC,),
        in_specs=[qk(C_QM // 512), qk(C_KM // 512), wide(C_VM // D_MODEL), wide(C_OM // D_MODEL),
                  pl.BlockSpec((8, MLC), lambda c: (0, c)), col, col, col, col,
                  pl.BlockSpec((1, 1, LANES), lambda c: (c, 0, 0)), pl.BlockSpec((1, D_MODEL), lambda c: (0, 0))],
        out_specs=(pl.BlockSpec((MLC, D_MODEL), lambda c: (c, 0)), pl.BlockSpec((MLC, D_MODEL), lambda c: (c, 0)),
                   col, pl.BlockSpec((1, ML_HEADS, ML_DV, ML_DQK), lambda c: (c, 0, 0, 0)),
                   pl.BlockSpec((1, ML_HEADS, 1, ML_DQK), lambda c: (c, 0, 0, 0))),
        scratch_shapes=[pltpu.VMEM((ML_HEADS, ML_DV, ML_DQK), F32), pltpu.VMEM((ML_HEADS, 1, ML_DQK), F32)],
        compiler_params=_cparams(("arbitrary",)),
    )(pm, pm, pm, pm, a_row, A, wi, em, wk, dec, w_hn)


def _mlstm_bwd(dha, pm, hp_all, den_all, a_row, A, wi, em, wk, dec, cst, nst, w_hn, name):
    S = pm.shape[0]
    NC = S // MLC
    idx, qk, wide, col = _ml_specs(True, NC)

    def body(dha_ref, q_ref, k_ref, v_ref, o_ref, hp_ref, den_ref, arow_ref, A_ref, wi_ref, em_ref, wk_ref,
             dec_ref, cst_ref, nst_ref, whn_ref,
             dqk_ref, dv_ref, do_ref, rk_ref, kc_ref, t_ref, dwhn_ref, dC_scr, dn_scr, t_scr):
        @pl.when(pl.program_id(0) == 0)
        def _():
            dC_scr[...] = jnp.zeros_like(dC_scr)
            dn_scr[...] = jnp.zeros_like(dn_scr)
            t_scr[...] = jnp.zeros_like(t_scr)
            dwhn_ref[...] = jnp.zeros_like(dwhn_ref)

        lanes = lax.broadcasted_iota(jnp.int32, (MLC, LANES), 1)
        lane1 = lax.broadcasted_iota(jnp.int32, (1, LANES), 1)
        t_ref[0, 0:1, :] = t_scr[...]
        rk_tile = jnp.zeros((MLC, LANES), F32)
        kc_tile = jnp.zeros((MLC, LANES), F32)
        t_new = jnp.zeros((1, LANES), F32)
        for h in range(ML_HEADS):
            hs = slice(h * ML_DQK, (h + 1) * ML_DQK)
            vs = slice(h * ML_DV, (h + 1) * ML_DV)
            hp = hp_ref[:, vs]
            sig = jax.nn.sigmoid(o_ref[:, vs])
            whn = whn_ref[:, vs]
            r = _rstd(hp)
            dga = dha_ref[:, vs]
            do_ref[:, vs] = (dga * (hp * r * whn) * sig * (1.0 - sig)).astype(do_ref.dtype)
            dhn = dga * sig
            dhp, dwt = _rmsnorm_bwd_math(dhn, hp, whn)
            dwhn_ref[:, vs] += jnp.sum(dwt, axis=0, keepdims=True)
            den = den_ref[:, h:h + 1]
            floor = em_ref[:, h:h + 1]
            D = jnp.maximum(jnp.abs(den), floor)
            dnum = dhp / D
            dh_h = jnp.sum(dhp * hp, axis=1, keepdims=True)
            active = jnp.abs(den) >= floor
            dden = -dh_h / D * jnp.where(active, jnp.sign(den), 0.0)
            phi = jnp.where(active, 0.0, dh_h)
            qb, kb, qf, kf, qk_, W = _ml_intra(q_ref, k_ref, arow_ref, A_ref, h)
            vf = v_ref[:, vs]
            vb = vf.astype(BF16)
            Cb = cst_ref[0, h]
            nrow = nst_ref[0, h]
            wic = wi_ref[:, h:h + 1]
            wkc = wk_ref[:, h:h + 1]
            d = dec_ref[0, :, h:h + 1]
            dCn = dC_scr[h]
            dCb = dCn.astype(BF16)
            dnn = dn_scr[h]
            dnumb = dnum.astype(BF16)
            s = qk_ * W
            ds = (_dot_nt(dnumb, vb) + dden) * W
            dsb = ds.astype(BF16)
            dnw = (wic * dnum).astype(BF16)
            wd = wic * dden
            kw = kf * wkc
            dv_state = _dot_nt(kw.astype(BF16), dCb)
            dq = _dot_nn(dsb, kb) + _dot_nn(dnw, Cb) + wd * nrow
            dk_state = wkc * (_dot_nn(vb, dCb) + dnn)
            dk = _dot_tn(dsb, qb) + dk_state
            dv = _dot_tn(s.astype(BF16), dnumb) + dv_state
            dC = d * dCn + _dot_tn(dnw, qb)
            dn = d * dnn + jnp.sum(wd * qf, axis=0, keepdims=True)
            dC_scr[h] = dC
            dn_scr[h] = dn
            dqk_ref[:, hs] = (dq * _ML_SCALE).astype(dqk_ref.dtype)
            dqk_ref[:, C_KM + h * ML_DQK:C_KM + (h + 1) * ML_DQK] = dk.astype(dqk_ref.dtype)
            dv_ref[:, vs] = dv.astype(dv_ref.dtype)
            G = ds * qk_
            inter = _dot_nt(qb, Cb)
            qn = jnp.sum(qf * nrow, axis=1, keepdims=True)
            R = (jnp.sum(G, axis=1, keepdims=True)
                 + wic * (jnp.sum(dnum * inter, axis=1, keepdims=True) + dden * qn))
            K = jnp.sum(G.T, axis=1, keepdims=True) + jnp.sum(kf * dk_state, axis=1, keepdims=True)
            rk_tile = jnp.where(lanes == h, R - K, rk_tile)
            kc_tile = jnp.where(lanes == h, phi, kc_tile)
            tt = (jnp.sum(jnp.sum(dC * Cb.astype(F32), axis=1, keepdims=True), axis=0, keepdims=True)
                  + jnp.sum(dn * nrow, axis=1, keepdims=True))
            t_new = jnp.where(lane1 == h, tt, t_new)
        rk_ref[...] = rk_tile
        kc_ref[...] = kc_tile
        t_ref[0, 1:2, :] = t_new
        t_scr[...] = t_new

    act = lambda n: jax.ShapeDtypeStruct((S, n), BF16)
    cs = jax.ShapeDtypeStruct((S, LANES), F32)
    rowblk = lambda n: pl.BlockSpec((MLC, n), lambda c: (idx(c), 0))
    return pl.pallas_call(
        body, name=name,
        out_shape=(act(D_MODEL), act(D_MODEL), act(D_MODEL), cs, cs,
                   jax.ShapeDtypeStruct((NC, 2, LANES), F32), jax.ShapeDtypeStruct((1, D_MODEL), F32)),
        grid=(NC,),
        in_specs=[rowblk(D_MODEL), qk(C_QM // 512), qk(C_KM // 512), wide(C_VM // D_MODEL), wide(C_OM // D_MODEL),
                  rowblk(D_MODEL), col, pl.BlockSpec((8, MLC), lambda c: (0, idx(c))), col, col, col, col,
                  pl.BlockSpec((1, 1, LANES), lambda c: (idx(c), 0, 0)),
                  pl.BlockSpec((1, ML_HEADS, ML_DV, ML_DQK), lambda c: (idx(c), 0, 0, 0)),
                  pl.BlockSpec((1, ML_HEADS, 1, ML_DQK), lambda c: (idx(c), 0, 0, 0)),
                  pl.BlockSpec((1, D_MODEL), lambda c: (0, 0))],
        out_specs=(rowblk(D_MODEL), rowblk(D_MODEL), rowblk(D_MODEL), col, col,
                   pl.BlockSpec((1, 2, LANES), lambda c: (idx(c), 0, 0)), pl.BlockSpec((1, D_MODEL), lambda c: (0, 0))),
        scratch_shapes=[pltpu.VMEM((ML_HEADS, ML_DV, ML_DQK), F32), pltpu.VMEM((ML_HEADS, 1, ML_DQK), F32),
                        pltpu.VMEM((1, LANES), F32)],
        compiler_params=_cparams(("arbitrary",)),
    )(dha, pm, pm, pm, pm, hp_all, den_all, a_row, A, wi, em, wk, dec, cst, nst, w_hn)


_FOX_SCALE = FOX_DH ** -0.5
_NEG = -1e30
_LOG2E = 1.4426950408889634
_LN2 = 0.6931471805599453
_QF_BLK, _KF_BLK, _VF_BLK = 0, FOX_HEADS, 2 * FOX_HEADS


def _lane_pick(tile, lane):
    lanes = lax.broadcasted_iota(jnp.int32, tile.shape, 1)
    return jnp.sum(jnp.where(lanes == lane, tile, 0.0), axis=1, keepdims=True)


def _col_to_row(col):
    return jnp.max(jnp.broadcast_to(col, (col.shape[0], LANES)).T, axis=0, keepdims=True)


def _causal(q0, k0, shape, q_axis):
    qpos = q0 + lax.broadcasted_iota(jnp.int32, shape, q_axis)
    kpos = k0 + lax.broadcasted_iota(jnp.int32, shape, 1 - q_axis)
    return kpos <= qpos


def _fox_fwd(pf, fc, fk_row, name):
    S = pf.shape[0]
    TQ, TK = FOX_TQ, FOX_TK
    nq, nk = S // TQ, S // TK
    c1 = _FOX_SCALE * _LOG2E

    def body(q_ref, k_ref, v_ref, fc_ref, fr_ref, o_ref, lse_ref):
        h, i = pl.program_id(0), pl.program_id(1)
        qb = q_ref[...]
        fq2 = _lane_pick(fc_ref[...], h) * _LOG2E

        def step(j, carry, masked):
            m, l, acc = carry
            off = pl.multiple_of(j * TK, TK)
            t = _dot_nt(qb, k_ref[pl.ds(off, TK), :]) * c1 - fr_ref[0, j] * _LOG2E
            if masked:
                t = jnp.where(_causal(i * TQ, j * TK, (TQ, TK), 0), t, _NEG)
            m_new = jnp.maximum(m, jnp.max(t, axis=1, keepdims=True) + fq2)
            alpha = jnp.exp2(m - m_new)
            p = jnp.exp2(t + (fq2 - m_new))
            l = alpha * l + jnp.sum(p, axis=1, keepdims=True)
            acc = alpha * acc + _dot_nn(p.astype(BF16), v_ref[pl.ds(off, TK), :])
            return m_new, l, acc

        init = (jnp.full((TQ, 1), _NEG, F32), jnp.zeros((TQ, 1), F32), jnp.zeros((TQ, FOX_DH), F32))
        last = (i * TQ) // TK
        carry = lax.fori_loop(0, last, lambda j, c: step(j, c, False), init)
        m, l, acc = step(last, carry, True)
        o_ref[...] = (acc / l).astype(o_ref.dtype)
        lse_ref[0, 0] = _col_to_row((m + jnp.log2(l)) * _LN2)

    head = lambda blk: pl.BlockSpec((S, FOX_DH), lambda h, i: (0, blk + h))
    return pl.pallas_call(
        body, name=name,
        out_shape=(jax.ShapeDtypeStruct((S, D_MODEL), BF16), jax.ShapeDtypeStruct((FOX_HEADS, nq, 1, TQ), F32)),
        grid=(FOX_HEADS, nq),
        in_specs=[pl.BlockSpec((TQ, FOX_DH), lambda h, i: (i, _QF_BLK + h)), head(_KF_BLK), head(_VF_BLK),
                  pl.BlockSpec((TQ, LANES), lambda h, i: (i, 0)),
                  pl.BlockSpec((1, nk, 1, TK), lambda h, i: (h, 0, 0, 0))],
        out_specs=(pl.BlockSpec((TQ, FOX_DH), lambda h, i: (i, h)),
                   pl.BlockSpec((1, 1, 1, TQ), lambda h, i: (h, i, 0, 0))),
        compiler_params=_cparams(("parallel", "arbitrary")),
    )(pf, pf, pf, fc, fk_row)


def _fox_bwd(dhb, hb, pf, lse_row, fq_row, fc, name):
    S = pf.shape[0]
    TQ, TK = FOX_TQ, FOX_TK
    nq, nk, r = S // TQ, S // TK, TK // TQ
    c1 = _FOX_SCALE * _LOG2E

    def body(q_ref, k_ref, v_ref, do_ref, o_ref, lse_ref, fq_ref, fc_ref,
             dq_ref, dk_ref, dv_ref, dFk_ref, dFq_ref, dq_acc, qside, delta, dk_acc, dv_acc, cs_acc):
        h, j = pl.program_id(0), pl.program_id(1)

        @pl.when(j == 0)
        def _():
            dq_acc[...] = jnp.zeros_like(dq_acc)
            dFq_ref[...] = jnp.zeros_like(dFq_ref)

            def fill(b, _):
                off = pl.multiple_of(b * TQ, TQ)
                prod = do_ref[pl.ds(off, TQ), :].astype(F32) * o_ref[pl.ds(off, TQ), :].astype(F32)
                delta[b] = jnp.sum(prod.T, axis=0, keepdims=True)
                qside[b] = (fq_ref[0, b] - lse_ref[0, b]) * _LOG2E
                return 0

            lax.fori_loop(0, nq, fill, 0)

        kb = k_ref[...]
        vb = v_ref[...]
        fk2 = _lane_pick(fc_ref[...], h) * _LOG2E
        dk_acc[...] = jnp.zeros_like(dk_acc)
        dv_acc[...] = jnp.zeros_like(dv_acc)
        cs_acc[...] = jnp.zeros_like(cs_acc)

        def step(i, masked):
            off = pl.multiple_of(i * TQ, TQ)
            qb = q_ref[pl.ds(off, TQ), :]
            dob = do_ref[pl.ds(off, TQ), :]
            t = _dot_nt(kb, qb) * c1 + qside[i] - fk2
            if masked:
                t = jnp.where(_causal(i * TQ, j * TK, (TK, TQ), 1), t, _NEG)
            p = jnp.exp2(t)
            dv_acc[...] += _dot_nn(p.astype(BF16), dob)
            ds = p * (_dot_nt(vb, dob) - delta[i])
            dsb = ds.astype(BF16)
            dk_acc[...] += _dot_nn(dsb, qb)
            dq_acc[pl.ds(off, TQ), :] += _dot_tn(dsb, kb)
            cs_acc[...] += jnp.sum(ds, axis=1, keepdims=True)
            dFq_ref[0, i] += jnp.sum(ds, axis=0, keepdims=True)

        for d in range(r):
            step(r * j + d, True)

        def rest(i, _):
            step(i, False)
            return 0

        lax.fori_loop(r * j + r, nq, rest, 0)
        dk_ref[...] = (dk_acc[...] * _FOX_SCALE).astype(dk_ref.dtype)
        dv_ref[...] = dv_acc[...].astype(dv_ref.dtype)
        dFk_ref[0, 0] = -_col_to_row(cs_acc[...])

        @pl.when(j == nk - 1)
        def _():
            dq_ref[...] = (dq_acc[...] * _FOX_SCALE).astype(dq_ref.dtype)

    head = lambda blk: pl.BlockSpec((S, FOX_DH), lambda h, j: (0, blk + h))
    kblk = lambda blk: pl.BlockSpec((TK, FOX_DH), lambda h, j: (j, blk + h))
    qrows = pl.BlockSpec((1, nq, 1, TQ), lambda h, j: (h, 0, 0, 0))
    act = jax.ShapeDtypeStruct((S, D_MODEL), BF16)
    return pl.pallas_call(
        body, name=name,
        out_shape=(act, act, act, jax.ShapeDtypeStruct((FOX_HEADS, nk, 1, TK), F32),
                   jax.ShapeDtypeStruct((FOX_HEADS, nq, 1, TQ), F32)),
        grid=(FOX_HEADS, nk),
        in_specs=[head(_QF_BLK), kblk(_KF_BLK), kblk(_VF_BLK), head(0), head(0), qrows, qrows,
                  pl.BlockSpec((TK, LANES), lambda h, j: (j, 0))],
        out_specs=(head(0), kblk(0), kblk(0), pl.BlockSpec((1, 1, 1, TK), lambda h, j: (h, j, 0, 0)), qrows),
        scratch_shapes=[pltpu.VMEM((S, FOX_DH), F32), pltpu.VMEM((nq, 1, TQ), F32), pltpu.VMEM((nq, 1, TQ), F32),
                        pltpu.VMEM((TK, FOX_DH), F32), pltpu.VMEM((TK, FOX_DH), F32), pltpu.VMEM((TK, 1), F32)],
        compiler_params=_cparams(("parallel", "arbitrary")),
    )(pf, pf, pf, dhb, hb, lse_row, fq_row, fc)


def _pad_lanes(v):
    return jnp.pad(v, ((0, 0), (0, LANES - v.shape[1])))


def _local_step(x, target, wmain_t, wsmall_t, rest_weights, p, on_grads, token):
    S = x.shape[0]
    bi, bf, bff = _pad_lanes(p["b_ml_i"]), _pad_lanes(p["b_ml_f"]), _pad_lanes(p["b_fox_f"])

    h0 = _rmsnorm_fwd(x, p["norm_mix_pre"] + token[0:1, 0:1], "norm_mix_pre")
    pm = _mm(h0, wmain_t[:N_ML], "nt", F32, "proj_mlstm")
    pf = _mm(h0, wmain_t[N_ML:N_ML + N_FOX], "nt", BF16, "proj_fox")
    pg = _mm(h0, wmain_t[N_ML + N_FOX:], "nt", F32, "proj_merge")
    ps = _mm(h0, wsmall_t, "nt", F32, "proj_gates")
    a, A, wi, em, wk, dec, Fc = _gates_fwd(ps, bi, bf, bff, "gates_fwd")
    a_row = a[:, :8].T
    ha, hp, den, cst, nst = _mlstm_fwd(pm, a_row, A, wi, em, wk, dec, p["ml_head_norm"], "mlstm_fwd")
    ft = Fc[:, :FOX_HEADS].T
    fq_row = ft.reshape(FOX_HEADS, S // FOX_TQ, 1, FOX_TQ)
    fk_row = ft.reshape(FOX_HEADS, S // FOX_TK, 1, FOX_TK)
    hb, lse_row = _fox_fwd(pf, Fc, fk_row, "fox_fwd")
    wa, wb, wout, wup, wdown = rest_weights(hb)
    ya = _mm(ha, wa, "nn", F32, "branch_a")
    yb = _mm(hb, wb, "nn", F32, "branch_b")
    merged = _merge_fwd(ya, yb, pg, p["b_gate_a"], p["b_gate_b"], "merge_fwd")
    z = _mm(merged, wout, "nn", F32, "out_proj")
    x1 = _resid_norm_fwd(x, z, p["norm_mix_post"], "resid_mix")
    h2 = _rmsnorm_fwd(x1, p["norm_ffn_pre"], "norm_ffn_pre")
    up = _mm(h2, wup, "nn", F32, "ffn_up")
    act = _conv_act_fwd(up, p["conv_w"], p["conv_b"], "conv_act_fwd")
    d = _mm(act, wdown, "nn", F32, "ffn_down")
    loss_row, dy, dd, g_norm_ffn_post = _loss_head(x1, d, p["norm_ffn_post"], target, "loss_head")
    dact = _mm(dd, wdown, "nt", F32, "d_act")
    g_wdown = _mm(act, dd, "tn", F32, "dw_down")
    dupa, dupg, dcwa, dcwg, dcba, dcbg = _conv_act_bwd(up, dact, p["conv_w"], p["conv_b"], "conv_act_bwd")
    g_conv_w = jnp.concatenate([dcwa, dcwg], axis=1)
    g_conv_b = jnp.concatenate([dcba, dcbg], axis=1)
    dh2 = _mm([dupa, dupg], wup, "nt", F32, "d_h2")
    g_wup = _mm(h2, [dupa, dupg], "tn", F32, "dw_up")
    token = on_grads("ffn", dict(w_up=g_wup, w_down=g_wdown))
    dx1, g_norm_ffn_pre = _rmsnorm_bwd([dh2], x1, p["norm_ffn_pre"] + token[0:1, 0:1], dy, F32, "norm_ffn_pre_bwd")
    dz, g_norm_mix_post = _rmsnorm_bwd([dx1], z, p["norm_mix_post"], None, BF16, "norm_mix_post_bwd")
    dmerged = _mm(dz, wout, "nt", F32, "d_merged")
    g_wout = _mm(merged, dz, "tn", F32, "dw_out")
    dya, dyb, dga, dgb, g_b_gate_a, g_b_gate_b = _merge_bwd(dmerged, ya, yb, pg, p["b_gate_a"], p["b_gate_b"], "merge_bwd")
    dha = _mm(dya, wa, "nt", F32, "d_ha")
    g_wa = _mm(ha, dya, "tn", F32, "dw_a")
    dhb = _mm(dyb, wb, "nt", BF16, "d_hb")
    g_wb = _mm(hb, dyb, "tn", F32, "dw_b")
    token = on_grads("mix", dict(w_out=g_wout, w_branch_a=g_wa, w_branch_b=g_wb))
    dqkm, dvm, dom, rk, kc, tch, g_ml_head_norm = _mlstm_bwd(
        dha, pm, hp, den, a_row, A, wi, em, wk, dec, cst, nst, p["ml_head_norm"] + token[0:1, 0:1], "mlstm_bwd")
    dqf, dkf, dvf, dFk, dFq = _fox_bwd(dhb, hb, pf, lse_row, fq_row, Fc, "fox_bwd")
    dF = jnp.pad((dFk.reshape(FOX_HEADS, S) + dFq.reshape(FOX_HEADS, S)).T, ((0, 0), (0, LANES - FOX_HEADS)))
    dps, dbias = _gates_bwd(ps, bi, bf, bff, rk, kc, tch, dF, "gates_bwd")
    dpm = [dqkm, dvm, dom, dqf, dkf, dvf, dga, dgb]
    g_wmain_t = _mm(dpm, h0, "tn", F32, "dw_main")
    token = on_grads("in", dict(w_in=g_wmain_t))
    g_wsmall_t = _mm(dps, h0, "tn", F32, "dw_gates")
    dh0s = _mm(dps, wsmall_t + token[0:1, 0:1].astype(BF16), "nn", F32, "d_h0_gates")
    dh0 = _mm(dpm, wmain_t, "nn", F32, "d_h0_main")
    grad_x, g_norm_mix_pre = _rmsnorm_bwd([dh0, dh0s], x, p["norm_mix_pre"], dx1, F32, "norm_mix_pre_bwd")

    big = dict(wsmall_t=g_wsmall_t)
    small = dict(norm_mix_pre=g_norm_mix_pre, ml_head_norm=g_ml_head_norm, b_gate_a=g_b_gate_a, b_gate_b=g_b_gate_b,
                 norm_mix_post=g_norm_mix_post, norm_ffn_pre=g_norm_ffn_pre, norm_ffn_post=g_norm_ffn_post,
                 conv_b=g_conv_b, b_ml_i=dbias[:, 0:ML_HEADS], b_ml_f=dbias[:, LANES:LANES + ML_HEADS],
                 b_fox_f=dbias[:, 2 * LANES:2 * LANES + FOX_HEADS], conv_w=g_conv_w)
    return loss_row, grad_x, big, small


def _row_tile(r, target=256):
    best = None
    for t in range(8, min(r, target) + 1, 8):
        if r % t == 0:
            best = t
    return best if best is not None else r


def _adamw(w, g, m, v, name):
    _, R, C = w.shape
    tr = _row_tile(R)
    tc = C
    if tr == R and R > 256:
        tc = 256

    def body(w_ref, g_ref, m_ref, v_ref, d_ref, mo_ref, vo_ref):
        gv = g_ref[...]
        mn = ADAM_B1 * m_ref[0] + (1.0 - ADAM_B1) * gv
        vn = ADAM_B2 * v_ref[0] + (1.0 - ADAM_B2) * (gv * gv)
        m_hat = mn / (1.0 - ADAM_B1 ** ADAM_STEP)
        v_hat = vn / (1.0 - ADAM_B2 ** ADAM_STEP)
        d_ref[0] = -ADAM_LR * (m_hat / (jnp.sqrt(v_hat) + ADAM_EPS) + ADAM_WD * w_ref[0])
        mo_ref[0] = mn
        vo_ref[0] = vn

    blk = pl.BlockSpec((1, tr, tc), lambda i, j: (0, i, j))
    o = jax.ShapeDtypeStruct((1, R, C), F32)
    return pl.pallas_call(
        body, name=name, out_shape=(o, o, o), grid=(R // tr, C // tc),
        in_specs=[blk, pl.BlockSpec((tr, tc), lambda i, j: (i, j)), blk, blk], out_specs=(blk,) * 3,
        compiler_params=_cparams(("parallel", "parallel")),
    )(w, g, m, v)


ANY = pl.BlockSpec(memory_space=pl.ANY)


def _place():
    x, y, c = lax.axis_index("x"), lax.axis_index("y"), lax.axis_index("c")
    chips = [(1 - x, y), (x, 1 - y), (1 - x, 1 - y)]
    return x, y, c, chips


def _block(ref, kind, k, rows=None):
    if kind == "rows":
        return ref.at[k] if rows is None else ref.at[k, pl.ds(*rows), :]
    cb = ref.shape[1] // 4
    return ref.at[:, pl.ds(k * cb, cb)] if rows is None else ref.at[pl.ds(*rows), pl.ds(k * cb, cb)]


def _gathered_shape(s, kind):
    return (4,) + s.shape if kind == "rows" else (s.shape[0], 4 * s.shape[1])


def _gather_weights(shards, kinds, smalls):
    n, ns = len(shards), len(smalls)

    def body(*refs):
        ins, sm_in = refs[:n], refs[n:n + ns]
        outs, sm_out = refs[n + ns:2 * n + ns], refs[2 * n + ns:2 * (n + ns)]
        send_sems, recv_sems, sm_send, sm_recv, local_sems = refs[2 * (n + ns):]
        x, y, c, chips = _place()
        sibling = (x, y, 1 - c)
        kme = 2 * x + y

        def half(a, k, hc):
            h = ins[a].shape[0] // 2
            return _block(outs[a], kinds[a], k, (hc * h, h))

        def remote(a, slot, src, dst, to):
            return pltpu.make_async_remote_copy(src_ref=src, dst_ref=dst, send_sem=send_sems.at[a * 7 + slot],
                                                recv_sem=recv_sems.at[a * 7 + slot], device_id=to, device_id_type=MESH)

        def sm_copy(b, j, k, to):
            return pltpu.make_async_remote_copy(src_ref=sm_in[b], dst_ref=sm_out[b].at[k], send_sem=sm_send.at[3 * b + j],
                                                recv_sem=sm_recv.at[3 * b + j], device_id=to, device_id_type=MESH)

        local = [pltpu.make_async_copy(sm_in[b], sm_out[b].at[kme], local_sems.at[b]) for b in range(ns)]
        for cp in local:
            cp.start()
        sends = [remote(a, 6, ins[a], _block(outs[a], kinds[a], kme), sibling) for a in range(n)]
        for a in range(n):
            h = ins[a].shape[0] // 2
            for j, chip in enumerate(chips):
                sends.append(remote(a, j, ins[a].at[pl.ds(c * h, h), :], half(a, kme, c), (*chip, c)))
        for b in range(ns):
            for j, chip in enumerate(chips):
                sends.append(sm_copy(b, j, kme, (*chip, c)))
        for cp in sends:
            cp.start()
        for a in range(n):
            for j, chip in enumerate(chips):
                kj = 2 * chip[0] + chip[1]
                remote(a, j, half(a, kj, c), half(a, kj, c), (*chip, c)).wait_recv()
                fwd = remote(a, 3 + j, half(a, kj, c), half(a, kj, c), sibling)
                fwd.start()
                sends.append(fwd)
        for a in range(n):
            for j, chip in enumerate(chips):
                kj = 2 * chip[0] + chip[1]
                remote(a, 3 + j, half(a, kj, 1 - c), half(a, kj, 1 - c), sibling).wait_recv()
        for b in range(ns):
            for j, chip in enumerate(chips):
                sm_copy(b, j, 2 * chip[0] + chip[1], (*chip, c)).wait_recv()
        for a in range(n):
            remote(a, 6, ins[a], _block(outs[a], kinds[a], kme), sibling).wait_recv()
        for cp in sends:
            cp.wait_send()
        for cp in local:
            cp.wait()

    outs = pl.pallas_call(
        body, name="gather_weights",
        out_shape=tuple([jax.ShapeDtypeStruct(_gathered_shape(s, k), s.dtype) for s, k in zip(shards, kinds)]
                        + [jax.ShapeDtypeStruct((4,) + s.shape, s.dtype) for s in smalls]),
        in_specs=[ANY] * (n + ns), out_specs=tuple([ANY] * (n + ns)),
        scratch_shapes=[pltpu.SemaphoreType.DMA((7 * n,)), pltpu.SemaphoreType.DMA((7 * n,)),
                        pltpu.SemaphoreType.DMA((3 * ns,)), pltpu.SemaphoreType.DMA((3 * ns,)),
                        pltpu.SemaphoreType.DMA((ns,))],
    )(*shards, *smalls)
    return outs[:n], outs[n:]


_IN_HBM = pl.BlockSpec(memory_space=pltpu.HBM)
_SEMS = pl.BlockSpec(memory_space=pltpu.SEMAPHORE)
_DATAFLOW = pltpu.SideEffectType.DATAFLOW_SIDE_EFFECTING


def _hbm(t):
    return pltpu.HBM(t.shape, t.dtype)


def _gather_copies(ins, outs, send_sems, recv_sems, kinds):
    x, y, c, chips = _place()
    kme = 2 * x + y
    cps = []
    for a in range(len(ins)):
        h = ins[a].shape[0] // 2
        for j, chip in enumerate(chips + [None]):
            to = (x, y, 1 - c) if chip is None else (*chip, c)
            src = ins[a] if chip is None else ins[a].at[pl.ds(c * h, h), :]
            dst = _block(outs[a], kinds[a], kme, None if chip is None else (c * h, h))
            cps.append(pltpu.make_async_remote_copy(src_ref=src, dst_ref=dst, send_sem=send_sems.at[4 * a + j],
                                                    recv_sem=recv_sems.at[4 * a + j], device_id=to, device_id_type=MESH))
    return cps


def _gather_start(shards, kinds, name):
    n = len(shards)
    outs = [lax.empty(_gathered_shape(s, k), s.dtype) for s, k in zip(shards, kinds)]

    def body(*refs):
        for cp in _gather_copies(refs[:n], refs[n:2 * n], refs[2 * n], refs[2 * n + 1], kinds):
            cp.start()
        refs[-1][...] = jnp.zeros_like(refs[-1])

    return pl.pallas_call(
        body, name=name,
        out_shape=(pltpu.SemaphoreType.DMA((4 * n,)), pltpu.SemaphoreType.DMA((4 * n,)),
                   *[_hbm(t) for t in shards], *[_hbm(t) for t in outs], jax.ShapeDtypeStruct((8, LANES), F32)),
        in_specs=[_IN_HBM] * (2 * n),
        out_specs=(_SEMS, _SEMS, *[_IN_HBM] * (2 * n), pl.BlockSpec(memory_space=pltpu.VMEM)),
        input_output_aliases={a: 2 + a for a in range(2 * n)},
        compiler_params=pltpu.CompilerParams(has_side_effects=_DATAFLOW),
    )(*[pltpu.with_memory_space_constraint(t, pltpu.HBM) for t in list(shards) + outs])


def _gather_wait(started, after, kinds, name):
    n = (len(started) - 3) // 2
    bufs = started[2:2 + 2 * n]

    def body(*refs):
        for cp in _gather_copies(refs[:n], refs[n:2 * n], refs[2 * n], refs[2 * n + 1], kinds):
            cp.wait_send()
            cp.wait_recv()

    outs = pl.pallas_call(
        body, name=name, out_shape=tuple(_hbm(t) for t in bufs),
        in_specs=[_IN_HBM] * (2 * n) + [_SEMS, _SEMS, ANY], out_specs=tuple([_IN_HBM] * (2 * n)),
        input_output_aliases={a: a for a in range(2 * n)},
        compiler_params=pltpu.CompilerParams(has_side_effects=_DATAFLOW),
    )(*bufs, started[0], started[1], after)
    return outs[n:]


def _gather_relay(bufs, kinds, name):
    n = len(bufs)

    def body(*refs):
        ins, outs, send_sems, recv_sems = refs[:n], refs[n:2 * n], refs[2 * n], refs[2 * n + 1]
        x, y, c, chips = _place()
        cps = []
        for a in range(n):
            h = (ins[a].shape[1] if kinds[a] == "rows" else ins[a].shape[0]) // 2
            for j, chip in enumerate(chips):
                kj = 2 * chip[0] + chip[1]
                cps.append((pltpu.make_async_remote_copy(
                    src_ref=_block(ins[a], kinds[a], kj, (c * h, h)), dst_ref=_block(outs[a], kinds[a], kj, (c * h, h)),
                    send_sem=send_sems.at[3 * a + j], recv_sem=recv_sems.at[3 * a + j], device_id=(x, y, 1 - c),
                    device_id_type=MESH), a, kj, h))
        for cp, _, _, _ in cps:
            cp.start()
        for a_cp, (cp, a, kj, h) in enumerate(cps):
            theirs = _block(outs[a], kinds[a], kj, ((1 - c) * h, h))
            pltpu.make_async_remote_copy(src_ref=theirs, dst_ref=theirs, send_sem=send_sems.at[a_cp],
                                         recv_sem=recv_sems.at[a_cp], device_id=(x, y, 1 - c), device_id_type=MESH).wait_recv()
        for cp, _, _, _ in cps:
            cp.wait_send()

    return pl.pallas_call(
        body, name=name, out_shape=tuple(jax.ShapeDtypeStruct(b.shape, b.dtype) for b in bufs),
        in_specs=[ANY] * n, out_specs=tuple([ANY] * n), input_output_aliases={a: a for a in range(n)},
        scratch_shapes=[pltpu.SemaphoreType.DMA((3 * n,)), pltpu.SemaphoreType.DMA((3 * n,))],
    )(*bufs)


def _exchange_sibling_halves(gs, kinds, name):
    n = len(gs)
    hshape = lambda g, kind: (4, g.shape[1] // 2, g.shape[2]) if kind == "rows" else (g.shape[0] // 2, g.shape[1])

    def body(*refs):
        ins, outs, send_sems, recv_sems = refs[:n], refs[n:2 * n], refs[2 * n], refs[2 * n + 1]
        x, y, c, _ = _place()
        cps = []
        for a in range(n):
            h = outs[a].shape[-2]
            src = ins[a].at[:, pl.ds((1 - c) * h, h), :] if kinds[a] == "rows" else ins[a].at[pl.ds((1 - c) * h, h), :]
            cps.append(pltpu.make_async_remote_copy(
                src_ref=src, dst_ref=outs[a], send_sem=send_sems.at[a],
                recv_sem=recv_sems.at[a], device_id=(x, y, 1 - c), device_id_type=MESH))
        for cp in cps:
            cp.start()
        for cp in cps:
            cp.wait()

    return pl.pallas_call(
        body, name=name,
        out_shape=tuple(jax.ShapeDtypeStruct(hshape(g, k), g.dtype) for g, k in zip(gs, kinds)),
        in_specs=[ANY] * n, out_specs=tuple([ANY] * n),
        scratch_shapes=[pltpu.SemaphoreType.DMA((n,)), pltpu.SemaphoreType.DMA((n,))],
    )(*gs)


def _add_halves(g, r1, cvec, kind, name):
    def body(c_ref, g_ref, r_ref, o_ref):
        o_ref[...] = (g_ref[...] + r_ref[...]).astype(o_ref.dtype)

    if kind == "rows":
        _, h, C = r1.shape
        tr = _row_tile(h)
        nt = h // tr
        grid = (4, nt)
        g_spec = pl.BlockSpec((1, tr, C), lambda k, i, c_ref: (k, c_ref[0] * nt + i, 0))
        r_spec = pl.BlockSpec((1, tr, C), lambda k, i, c_ref: (k, i, 0))
    else:
        h, C4 = r1.shape
        tr, tc = _row_tile(h), C4 // 4
        nt = h // tr
        grid = (nt, 4)
        g_spec = pl.BlockSpec((tr, tc), lambda i, k, c_ref: (c_ref[0] * nt + i, k))
        r_spec = pl.BlockSpec((tr, tc), lambda i, k, c_ref: (i, k))
    return pl.pallas_call(
        body, name=name, out_shape=jax.ShapeDtypeStruct(r1.shape, BF16),
        grid_spec=pltpu.PrefetchScalarGridSpec(num_scalar_prefetch=1, grid=grid, in_specs=[g_spec, r_spec],
                                               out_specs=r_spec),
        compiler_params=_cparams(("parallel", "parallel")),
    )(cvec, g, r1)


def _chip_copies(ins, lands, send_sems, recv_sems, kinds):
    x, y, c, chips = _place()
    return [pltpu.make_async_remote_copy(
        src_ref=_block(ins[a], kinds[a], 2 * chip[0] + chip[1]), dst_ref=lands[a].at[j],
        send_sem=send_sems.at[3 * a + j], recv_sem=recv_sems.at[3 * a + j], device_id=(*chip, c), device_id_type=MESH)
        for a in range(len(ins)) for j, chip in enumerate(chips)]


def _land_shape(s, kind):
    return (3,) + (s.shape[1:] if kind == "rows" else (s.shape[0], s.shape[1] // 4))


def _exchange_chips_start(ss, kinds, name):
    n = len(ss)
    lands = [lax.empty(_land_shape(s, k), s.dtype) for s, k in zip(ss, kinds)]

    def body(*refs):
        for cp in _chip_copies(refs[:n], refs[n:2 * n], refs[2 * n], refs[2 * n + 1], kinds):
            cp.start()
        refs[-1][...] = jnp.zeros_like(refs[-1])

    hbm = _hbm
    return pl.pallas_call(
        body, name=name,
        out_shape=(pltpu.SemaphoreType.DMA((3 * n,)), pltpu.SemaphoreType.DMA((3 * n,)),
                   *[hbm(t) for t in ss], *[hbm(t) for t in lands], jax.ShapeDtypeStruct((8, LANES), F32)),
        in_specs=[_IN_HBM] * (2 * n),
        out_specs=(_SEMS, _SEMS, *[_IN_HBM] * (2 * n), pl.BlockSpec(memory_space=pltpu.VMEM)),
        input_output_aliases={a: 2 + a for a in range(2 * n)},
        compiler_params=pltpu.CompilerParams(has_side_effects=_DATAFLOW),
    )(*[pltpu.with_memory_space_constraint(t, pltpu.HBM) for t in list(ss) + lands])


def _exchange_chips_wait(started, after, kinds, name):
    send_sems, recv_sems = started[0], started[1]
    n = (len(started) - 3) // 2
    bufs = started[2:2 + 2 * n]

    def body(*refs):
        for cp in _chip_copies(refs[:n], refs[n:2 * n], refs[2 * n], refs[2 * n + 1], kinds):
            cp.wait_send()
            cp.wait_recv()

    hbm = _hbm
    outs = pl.pallas_call(
        body, name=name, out_shape=tuple(hbm(t) for t in bufs),
        in_specs=[_IN_HBM] * (2 * n) + [_SEMS, _SEMS, ANY], out_specs=tuple([_IN_HBM] * (2 * n)),
        input_output_aliases={a: a for a in range(2 * n)},
        compiler_params=pltpu.CompilerParams(has_side_effects=_DATAFLOW),
    )(*bufs, send_sems, recv_sems, after)
    return outs[:n], outs[n:]


def _add_chips(s1, r2, kcvec, kind, name):
    _, h, C = r2.shape
    tr = _row_tile(h)
    nt = h // tr

    def body(kc_ref, s_ref, r0_ref, r1_ref, r2_ref, o_ref):
        s = s_ref[0] if kind == "rows" else s_ref[...]
        o_ref[...] = ((s.astype(F32) + r0_ref[0].astype(F32)) + r1_ref[0].astype(F32)) + r2_ref[0].astype(F32)

    peer = lambda j: pl.BlockSpec((1, tr, C), lambda i, kc_ref: (j, i, 0))
    if kind == "rows":
        s_spec = pl.BlockSpec((1, tr, C), lambda i, kc_ref: (kc_ref[0], i, 0))
    else:
        s_spec = pl.BlockSpec((tr, C), lambda i, kc_ref: (i, kc_ref[0]))
    return pl.pallas_call(
        body, name=name, out_shape=jax.ShapeDtypeStruct((2 * h, C), F32),
        grid_spec=pltpu.PrefetchScalarGridSpec(
            num_scalar_prefetch=1, grid=(nt,),
            in_specs=[s_spec, peer(0), peer(1), peer(2)],
            out_specs=pl.BlockSpec((tr, C), lambda i, kc_ref: (kc_ref[1] * nt + i, 0))),
        compiler_params=_cparams(("parallel",)),
    )(kcvec, s1, r2, r2, r2)


def _join_sibling_halves(bufs):
    n = len(bufs)

    def body(*refs):
        ins, outs, send_sems, recv_sems = refs[:n], refs[n:2 * n], refs[2 * n], refs[2 * n + 1]
        x, y, c, _ = _place()
        cps = []
        for a in range(n):
            h = ins[a].shape[0] // 2
            cps.append(pltpu.make_async_remote_copy(
                src_ref=ins[a].at[pl.ds(c * h, h), :], dst_ref=outs[a].at[pl.ds(c * h, h), :], send_sem=send_sems.at[a],
                recv_sem=recv_sems.at[a], device_id=(x, y, 1 - c), device_id_type=MESH))
        for cp in cps:
            cp.start()
        for a in range(n):
            h = ins[a].shape[0] // 2
            theirs = outs[a].at[pl.ds((1 - c) * h, h), :]
            pltpu.make_async_remote_copy(src_ref=theirs, dst_ref=theirs, send_sem=send_sems.at[a],
                                         recv_sem=recv_sems.at[a], device_id=(x, y, 1 - c), device_id_type=MESH).wait_recv()
        for cp in cps:
            cp.wait_send()

    return pl.pallas_call(
        body, name="grads_join",
        out_shape=tuple(jax.ShapeDtypeStruct(b.shape, b.dtype) for b in bufs),
        in_specs=[ANY] * n, out_specs=tuple([ANY] * n), input_output_aliases={a: a for a in range(n)},
        scratch_shapes=[pltpu.SemaphoreType.DMA((n,)), pltpu.SemaphoreType.DMA((n,))],
    )(*bufs)


N_DEV = 8


def _allreduce_small(pack):
    P = pack.shape[0]

    def body(p_ref, o_ref, gath, send_sems, recv_sems):
        x, y, c, _ = _place()
        me = 4 * x + 2 * y + c
        cps = []
        for mask in range(1, N_DEV):
            px = 1 - x if mask & 4 else x
            py = 1 - y if mask & 2 else y
            pc = 1 - c if mask & 1 else c
            cps.append((pltpu.make_async_remote_copy(
                src_ref=p_ref, dst_ref=gath.at[me], send_sem=send_sems.at[mask - 1], recv_sem=recv_sems.at[mask - 1],
                device_id=(px, py, pc), device_id_type=MESH), 4 * px + 2 * py + pc, mask))
        for cp, _, _ in cps:
            cp.start()
        gath[me] = p_ref[...]
        for _, peer, mask in cps:
            pltpu.make_async_remote_copy(
                src_ref=p_ref, dst_ref=gath.at[peer], send_sem=send_sems.at[mask - 1], recv_sem=recv_sems.at[mask - 1],
                device_id=(x, y, c), device_id_type=MESH).wait_recv()
        for cp, _, _ in cps:
            cp.wait_send()
        acc = gath[0]
        for i in range(1, N_DEV):
            acc = acc + gath[i]
        o_ref[...] = acc

    return pl.pallas_call(
        body, name="allreduce_small", out_shape=jax.ShapeDtypeStruct((P, LANES), F32),
        in_specs=[pl.BlockSpec(memory_space=pltpu.VMEM)], out_specs=pl.BlockSpec(memory_space=pltpu.VMEM),
        scratch_shapes=[pltpu.VMEM((N_DEV, P, LANES), F32), pltpu.SemaphoreType.DMA((N_DEV - 1,)),
                        pltpu.SemaphoreType.DMA((N_DEV - 1,))],
    )(pack)


def _pack_rows(arrs):
    rows = []
    for a in arrs:
        f = a.reshape(-1)
        f = jnp.pad(f, (0, (-f.shape[0]) % (8 * LANES)))
        rows.append(f.reshape(-1, LANES))
    return jnp.concatenate(rows, axis=0)


def _unpack_rows(pack, shapes):
    out, r = [], 0
    for s in shapes:
        n = math.prod(s)
        out.append(pack[r:r + -(-n // LANES)].reshape(-1)[:n].reshape(s))
        r += 8 * -(-n // (8 * LANES))
    return out


_SMALL = ["norm_mix_pre", "ml_head_norm", "b_gate_a", "b_gate_b", "norm_mix_post", "norm_ffn_pre", "norm_ffn_post",
          "conv_b", "b_ml_i", "b_ml_f", "b_fox_f"]
_BIG = ["w_in", "w_branch_a", "w_branch_b", "w_out", "w_up", "w_down"]
_WEIGHTS = ['norm_mix_pre', 'w_in', 'b_ml_i', 'b_ml_f', 'ml_head_norm', 'b_fox_f', 'b_gate_a', 'b_gate_b', 'w_branch_a',
            'w_branch_b', 'w_out', 'norm_mix_post', 'norm_ffn_pre', 'w_up', 'conv_w', 'conv_b', 'w_down', 'norm_ffn_post']


_KINDS = ["rows", "rows", "rows", "rows", "cols", "rows"]


def kernel(x, norm_mix_pre, w_in, b_ml_i, b_ml_f, ml_head_norm, b_fox_f, b_gate_a, b_gate_b, w_branch_a, w_branch_b, w_out, norm_mix_post, norm_ffn_pre, w_up, conv_w, conv_b, w_down, norm_ffn_post, loss_target, m_norm_mix_pre, m_w_in, m_b_ml_i, m_b_ml_f, m_ml_head_norm, m_b_fox_f, m_b_gate_a, m_b_gate_b, m_w_branch_a, m_w_branch_b, m_w_out, m_norm_mix_post, m_norm_ffn_pre, m_w_up, m_conv_w, m_conv_b, m_w_down, m_norm_ffn_post, v_norm_mix_pre, v_w_in, v_b_ml_i, v_b_ml_f, v_ml_head_norm, v_b_fox_f, v_b_gate_a, v_b_gate_b, v_w_branch_a, v_w_branch_b, v_w_out, v_norm_mix_post, v_norm_ffn_pre, v_w_up, v_conv_w, v_conv_b, v_w_down, v_norm_ffn_post):
    args = dict(locals())
    w = {n: args[n] for n in _WEIGHTS}
    mom = {n: args["m_" + n] for n in _WEIGHTS}
    var = {n: args["v_" + n] for n in _WEIGHTS}
    cx, cy, cc = lax.axis_index("x"), lax.axis_index("y"), lax.axis_index("c")
    kme = 2 * cx + cy
    cvec = jnp.reshape(cc, (1,)).astype(jnp.int32)
    kcvec = jnp.stack([kme, cc]).astype(jnp.int32)
    odd = kme % 2

    tr3 = lambda t: jnp.transpose(t, (0, 2, 1))
    w["w_in"], mom["w_in"], var["w_in"] = tr3(w_in), tr3(m_w_in), tr3(v_w_in)
    w_in_main = lax.dynamic_slice_in_dim(w["w_in"][0], 4 * odd, 2048, axis=0).astype(BF16)
    w_in_gates = lax.dynamic_slice_in_dim(w["w_in"][0], 2048 * (1 - odd), 4, axis=0).astype(BF16)
    (wmain_t,), (g_cw, g_gates) = _gather_weights([w_in_main], _KINDS[:1], [w["conv_w"][0], w_in_gates])
    rest_started = _gather_start([w[n][0].astype(BF16) for n in _BIG[1:]], _KINDS[1:], "gather_rest_start")

    def rest_weights(after):
        bufs = _gather_wait(rest_started, after, _KINDS[1:], "gather_rest_wait")
        g_a, g_b, g_out, wup, g_down = _gather_relay(bufs, _KINDS[1:], "gather_rest_relay")
        return full(g_a), full(g_b), full(g_out), wup, full(g_down)
    gate_rows = g_gates.reshape(16, D_MODEL)
    wsmall_t = jnp.zeros((N_SMALL, D_MODEL), BF16)
    for blk, (lo, hi) in enumerate(((0, 4), (4, 8), (8, 16))):
        wsmall_t = wsmall_t.at[blk * LANES:blk * LANES + hi - lo].set(gate_rows[lo:hi])
    full = lambda g: g.reshape(-1, g.shape[2])
    p = {n: w[n] for n in _SMALL}
    p["conv_w"] = jnp.transpose(g_cw, (1, 0, 2)).reshape(3, -1)

    groups = []

    def on_grads(group, gs):
        names = list(gs)
        kinds = [_KINDS[_BIG.index(n)] for n in names]
        whole = [g if k == "cols" else g.reshape(4, -1, g.shape[1]) for g, k in zip(gs.values(), kinds)]
        from_sibling = _exchange_sibling_halves(whole, kinds, "grads_to_sibling_" + group)
        sums = [_add_halves(g, r, cvec, k, "add_sibling_" + n) for g, r, k, n in zip(whole, from_sibling, kinds, names)]
        started = _exchange_chips_start(sums, kinds, "grads_to_chips_start_" + group)
        groups.append((group, names, kinds, started))
        return started[-1]

    loss_row, grad_x, big, small = _local_step(x[0], loss_target[0], full(wmain_t), wsmall_t, rest_weights, p, on_grads,
                                               rest_started[-1])
    grads = {}
    mine, mine_names = [], []
    for group, names, kinds, started in groups:
        sums, got = _exchange_chips_wait(started, grad_x, kinds, "grads_to_chips_wait_" + group)
        mine += [_add_chips(s, r, kcvec, k, "add_chips_" + n) for s, r, k, n in zip(sums, got, kinds, names)]
        mine_names += names
    grads.update(zip(mine_names, _join_sibling_halves(mine)))

    gt = big["wsmall_t"]
    small["w_in_gates"] = jnp.concatenate([gt[0:4], gt[LANES:LANES + 4], gt[2 * LANES:2 * LANES + 8]], axis=0)
    small_names = _SMALL + ["conv_w"]
    packed_names = small_names + ["w_in_gates"]
    pack = _pack_rows([small[n] for n in packed_names] + [loss_row])
    pack = jnp.pad(pack, ((0, (-pack.shape[0]) % 8), (0, 0)))
    full_shapes = [small[n].shape if n in ("conv_w", "w_in_gates") else w[n][0].shape for n in packed_names]
    total = _unpack_rows(_allreduce_small(pack), full_shapes + [loss_row.shape])
    for n, t in zip(packed_names, total):
        grads[n] = t
    loss = total[-1][0, 0]
    grads["conv_w"] = lax.dynamic_slice_in_dim(grads["conv_w"], kme * conv_w.shape[2], conv_w.shape[2], axis=1)
    my_gates = lax.dynamic_slice_in_dim(grads.pop("w_in_gates"), 4 * kme, 4, axis=0)
    g_in = jnp.zeros(w["w_in"].shape[1:], F32)
    g_in = lax.dynamic_update_slice_in_dim(g_in, grads["w_in"], 4 * odd, axis=0)
    grads["w_in"] = lax.dynamic_update_slice_in_dim(g_in, my_gates, 2048 * (1 - odd), axis=0)

    delta, new_m, new_v = {}, {}, {}
    for n in _BIG:
        delta[n], new_m[n], new_v[n] = _adamw(w[n], grads[n], mom[n], var[n], "adamw_" + n)
        grads[n] = grads[n][None]
    for d in (grads, delta, new_m, new_v):
        d["w_in"] = tr3(d["w_in"])
    packs = [_pack_rows([d[n][0] for n in small_names]) for d in (w, mom, var)]
    pad = ((0, (-packs[0].shape[0]) % 8), (0, 0))
    packs = [jnp.pad(t, pad)[None] for t in packs]
    gp = jnp.pad(_pack_rows([grads[n] for n in small_names]), pad)
    shapes = [w[n][0].shape for n in small_names]
    for dst, res in zip((delta, new_m, new_v), _adamw(packs[0], gp, packs[1], packs[2], "adamw_small")):
        for n, t in zip(small_names, _unpack_rows(res[0], shapes)):
            dst[n] = t[None]
    for n in small_names:
        grads[n] = grads[n][None]

    return (loss, grad_x[None], *[grads[n] for n in _WEIGHTS], *[delta[n] for n in _WEIGHTS],
            *[new_m[n] for n in _WEIGHTS], *[new_v[n] for n in _WEIGHTS])
```

```python
import functools
import math

import jax
import jax.numpy as jnp
from jax import lax
from jax.experimental import pallas as pl
from jax.experimental.pallas import tpu as pltpu

F32 = jnp.float32
BF16 = jnp.bfloat16
MESH = pl.DeviceIdType.MESH

D_MODEL = 1024
ML_HEADS = 4
ML_DQK = 128
ML_DV = 256
FOX_HEADS = 8
FOX_DH = 128
D_FF = 2816
GATE_CAP = 15.0
EPS = 1e-6
ADAM_LR, ADAM_B1, ADAM_B2, ADAM_EPS, ADAM_WD, ADAM_STEP = 0.001, 0.9, 0.999, 1e-08, 0.01, 10

LANES = 128
MLC = 128
FOX_TQ = 512
FOX_TQ_FWD = 512
FOX_TK = 512
ROW_T = 512
VMEM_LIMIT = 56 * 1024 * 1024

C_QM, C_KM, C_VM, C_OM = 0, 512, 1024, 2048
N_ML, N_FOX, N_GATE = 3072, 3072, 2048
N_SMALL = 384


def _cparams(sem=None):
    return pltpu.CompilerParams(dimension_semantics=sem, vmem_limit_bytes=VMEM_LIMIT)


def _tile(n, target):
    if n <= target:
        return n
    best = None
    for t in range(LANES, target + 1, LANES):
        if n % t == 0:
            best = t
    assert best is not None, (n, target)
    return best


def _dot(a, b, dims):
    return lax.dot_general(a, b, (dims, ((), ())), preferred_element_type=F32)


def _dot_nn(a, b):
    return _dot(a, b, ((1,), (0,)))


def _dot_nt(a, b):
    return _dot(a, b, ((1,), (1,)))


def _dot_tn(a, b):
    return _dot(a, b, ((0,), (0,)))


_DOTS = {"nn": _dot_nn, "nt": _dot_nt, "tn": _dot_tn}


def _mm(a, b, mode, out_dtype, name, tm=1024, tn=1408, tk=1408):
    a_parts = list(a) if isinstance(a, (list, tuple)) else [a]
    b_parts = list(b) if isinstance(b, (list, tuple)) else [b]
    assert len(a_parts) == 1 or len(b_parts) == 1, name
    a_axes = {"nn": "ik", "nt": "ik", "tn": "ki"}[mode]
    b_axes = {"nn": "kj", "nt": "jk", "tn": "kj"}[mode]
    size, target = {}, dict(i=tm, j=tn, k=tk)
    for parts, axes in ((a_parts, a_axes), (b_parts, b_axes)):
        dims = (parts[0].shape[0], parts[0].shape[1] * len(parts))
        for ax, n in zip(axes, dims):
            assert size.setdefault(ax, n) == n, (name, ax, n, size)
    tile = {}
    for parts, axes in ((a_parts, a_axes), (b_parts, b_axes)):
        if len(parts) > 1:
            tile[axes[1]] = _tile(parts[0].shape[1], target[axes[1]])
    for ax in "ijk":
        tile.setdefault(ax, _tile(size[ax], target[ax]))
    M, N, nk = size["i"], size["j"], size["k"] // tile["k"]
    grid_pos = dict(i=0, j=1, k=2)
    dot = _DOTS[mode]

    def specs(parts, axes):
        blk = (tile[axes[0]], tile[axes[1]])
        if len(parts) == 1:
            return [pl.BlockSpec(blk, lambda *g: (g[grid_pos[axes[0]]], g[grid_pos[axes[1]]]))], None
        bpp = parts[0].shape[1] // blk[1]

        def index(p):
            def f(*g):
                g0, g1 = g[grid_pos[axes[0]]], g[grid_pos[axes[1]]]
                on = g1 // bpp == p
                return jnp.where(on, g0, 0), jnp.where(on, g1 % bpp, 0)
            return f

        return [pl.BlockSpec(blk, index(p)) for p in range(len(parts))], (axes[1], bpp)

    a_specs, a_sel = specs(a_parts, a_axes)
    b_specs, b_sel = specs(b_parts, b_axes)
    na, nb = len(a_parts), len(b_parts)

    def body(*refs):
        a_refs, b_refs, o_ref, acc = refs[:na], refs[na:na + nb], refs[na + nb], refs[na + nb + 1:]

        def accumulate(part):
            if nk == 1:
                o_ref[...] = part.astype(o_ref.dtype)
                return
            acc_ref, = acc
            k = pl.program_id(2)

            @pl.when(k == 0)
            def _():
                acc_ref[...] = part

            @pl.when(k > 0)
            def _():
                acc_ref[...] += part

            @pl.when(k == nk - 1)
            def _():
                o_ref[...] = acc_ref[...].astype(o_ref.dtype)

        sel = a_sel or b_sel
        if sel is None:
            accumulate(dot(a_refs[0][...], b_refs[0][...]))
        else:
            which = pl.program_id(grid_pos[sel[0]]) // sel[1]
            for p in range(max(na, nb)):
                @pl.when(which == p)
                def _(p=p):
                    accumulate(dot(a_refs[p if a_sel else 0][...], b_refs[p if b_sel else 0][...]))

    return pl.pallas_call(
        body, name=name,
        out_shape=jax.ShapeDtypeStruct((M, N), out_dtype),
        grid=(M // tile["i"], N // tile["j"], nk),
        in_specs=a_specs + b_specs,
        out_specs=pl.BlockSpec((tile["i"], tile["j"]), lambda i, j, k: (i, j)),
        scratch_shapes=[pltpu.VMEM((tile["i"], tile["j"]), F32)] if nk > 1 else [],
        compiler_params=_cparams(("parallel", "parallel", "arbitrary")),
    )(*a_parts, *b_parts)


def _rstd(x):
    return lax.rsqrt(jnp.mean(x * x, axis=-1, keepdims=True) + EPS)


def _rmsnorm_fwd(x, g, name):
    S, D = x.shape
    T = _tile(S, ROW_T)

    def body(x_ref, g_ref, o_ref):
        xv = x_ref[...]
        o_ref[...] = (xv * _rstd(xv) * g_ref[...]).astype(o_ref.dtype)

    return pl.pallas_call(
        body, name=name, out_shape=jax.ShapeDtypeStruct((S, D), BF16), grid=(S // T,),
        in_specs=[pl.BlockSpec((T, D), lambda i: (i, 0)), pl.BlockSpec((1, D), lambda i: (0, 0))],
        out_specs=pl.BlockSpec((T, D), lambda i: (i, 0)),
        compiler_params=_cparams(("parallel",)),
    )(x, g)


def _resid_norm_fwd(x, z, g, name):
    S, D = x.shape
    T = _tile(S, ROW_T)

    def body(x_ref, z_ref, g_ref, o_ref):
        zv = z_ref[...]
        o_ref[...] = x_ref[...] + zv * _rstd(zv) * g_ref[...]

    row = pl.BlockSpec((T, D), lambda i: (i, 0))
    return pl.pallas_call(
        body, name=name, out_shape=jax.ShapeDtypeStruct((S, D), F32), grid=(S // T,),
        in_specs=[row, row, pl.BlockSpec((1, D), lambda i: (0, 0))],
        out_specs=row, compiler_params=_cparams(("parallel",)),
    )(x, z, g)


def _rmsnorm_bwd_math(dy, xv, g):
    r = _rstd(xv)
    u = dy * g
    dx = r * u - xv * (r * r * r) * jnp.mean(u * xv, axis=-1, keepdims=True)
    return dx, dy * xv * r


def _rmsnorm_bwd(dys, xin, g, resid, out_dtype, name):
    S, D = xin.shape
    T = _tile(S, ROW_T)
    has_resid = resid is not None
    ndy = len(dys)

    def body(*refs):
        dy_refs, (x_ref, g_ref) = refs[:ndy], refs[ndy:ndy + 2]
        dx_ref, dg_ref = refs[-2:]
        dy = dy_refs[0][...]
        for r in dy_refs[1:]:
            dy = dy + r[...]
        dx, dgt = _rmsnorm_bwd_math(dy, x_ref[...], g_ref[...])
        if has_resid:
            dx = dx + refs[ndy + 2][...]
        dx_ref[...] = dx.astype(dx_ref.dtype)

        @pl.when(pl.program_id(0) == 0)
        def _():
            dg_ref[...] = jnp.zeros_like(dg_ref)

        dg_ref[...] += jnp.sum(dgt, axis=0, keepdims=True)

    row = pl.BlockSpec((T, D), lambda i: (i, 0))
    vec = pl.BlockSpec((1, D), lambda i: (0, 0))
    ins = list(dys) + [xin, g] + ([resid] if has_resid else [])
    return pl.pallas_call(
        body, name=name,
        out_shape=(jax.ShapeDtypeStruct((S, D), out_dtype), jax.ShapeDtypeStruct((1, D), F32)),
        grid=(S // T,), in_specs=[row] * ndy + [row, vec] + ([row] if has_resid else []),
        out_specs=(row, vec), compiler_params=_cparams(("arbitrary",)),
    )(*ins)


def _loss_head(x1, d, g, target, name):
    S, D = x1.shape
    T = _tile(S, ROW_T)

    def body(x_ref, d_ref, g_ref, t_ref, loss_ref, dy_ref, dd_ref, dg_ref):
        dv, gv = d_ref[...], g_ref[...]
        y = x_ref[...] + dv * _rstd(dv) * gv
        diff = y - t_ref[...]
        dy = diff * (1.0 / D)
        dy_ref[...] = dy
        dd, dgt = _rmsnorm_bwd_math(dy, dv, gv)
        dd_ref[...] = dd.astype(dd_ref.dtype)

        @pl.when(pl.program_id(0) == 0)
        def _():
            dg_ref[...] = jnp.zeros_like(dg_ref)
            loss_ref[...] = jnp.zeros_like(loss_ref)

        dg_ref[...] += jnp.sum(dgt, axis=0, keepdims=True)
        part = jnp.sum(jnp.sum(diff * diff, axis=1, keepdims=True), axis=0, keepdims=True)
        loss_ref[...] += (0.5 / D) * part

    row = pl.BlockSpec((T, D), lambda i: (i, 0))
    vec = pl.BlockSpec((1, D), lambda i: (0, 0))
    return pl.pallas_call(
        body, name=name,
        out_shape=(jax.ShapeDtypeStruct((1, LANES), F32), jax.ShapeDtypeStruct((S, D), F32),
                   jax.ShapeDtypeStruct((S, D), BF16), jax.ShapeDtypeStruct((1, D), F32)),
        grid=(S // T,), in_specs=[row, row, vec, row],
        out_specs=(pl.BlockSpec((1, LANES), lambda i: (0, 0)), row, row, vec),
        compiler_params=_cparams(("arbitrary",)),
    )(x1, d, g, target)


def _merge_fwd(ya, yb, pm, ba, bb, name):
    S, D = ya.shape
    T = _tile(S, ROW_T)

    def body(ya_ref, yb_ref, ga_ref, gb_ref, ba_ref, bb_ref, o_ref):
        sa = jax.nn.sigmoid(ga_ref[...] + ba_ref[...])
        sb = jax.nn.sigmoid(gb_ref[...] + bb_ref[...])
        o_ref[...] = (sa * ya_ref[...] + sb * yb_ref[...]).astype(o_ref.dtype)

    row = pl.BlockSpec((T, D), lambda i: (i, 0))
    vec = pl.BlockSpec((1, D), lambda i: (0, 0))
    return pl.pallas_call(
        body, name=name, out_shape=jax.ShapeDtypeStruct((S, D), BF16), grid=(S // T,),
        in_specs=[row, row, pl.BlockSpec((T, D), lambda i: (i, 0)),
                  pl.BlockSpec((T, D), lambda i: (i, 1)), vec, vec],
        out_specs=row, compiler_params=_cparams(("parallel",)),
    )(ya, yb, pm, pm, ba, bb)


def _merge_bwd(dmerged, ya, yb, pm, ba, bb, name):
    S, D = ya.shape
    T = _tile(S, ROW_T)

    def body(dm_ref, ya_ref, yb_ref, ga_ref, gb_ref, ba_ref, bb_ref,
             dya_ref, dyb_ref, dga_ref, dgb_ref, dba_ref, dbb_ref):
        dm = dm_ref[...]
        sa = jax.nn.sigmoid(ga_ref[...] + ba_ref[...])
        sb = jax.nn.sigmoid(gb_ref[...] + bb_ref[...])
        dya_ref[...] = (dm * sa).astype(dya_ref.dtype)
        dyb_ref[...] = (dm * sb).astype(dyb_ref.dtype)
        dga = dm * ya_ref[...] * sa * (1.0 - sa)
        dgb = dm * yb_ref[...] * sb * (1.0 - sb)
        dga_ref[...] = dga.astype(dga_ref.dtype)
        dgb_ref[...] = dgb.astype(dgb_ref.dtype)

        @pl.when(pl.program_id(0) == 0)
        def _():
            dba_ref[...] = jnp.zeros_like(dba_ref)
            dbb_ref[...] = jnp.zeros_like(dbb_ref)

        dba_ref[...] += jnp.sum(dga, axis=0, keepdims=True)
        dbb_ref[...] += jnp.sum(dgb, axis=0, keepdims=True)

    row = pl.BlockSpec((T, D), lambda i: (i, 0))
    vec = pl.BlockSpec((1, D), lambda i: (0, 0))
    act = jax.ShapeDtypeStruct((S, D), BF16)
    v1 = jax.ShapeDtypeStruct((1, D), F32)
    return pl.pallas_call(
        body, name=name, out_shape=(act, act, act, act, v1, v1), grid=(S // T,),
        in_specs=[row, row, row, pl.BlockSpec((T, D), lambda i: (i, 0)),
                  pl.BlockSpec((T, D), lambda i: (i, 1)), vec, vec],
        out_specs=(row, row, row, row, vec, vec), compiler_params=_cparams(("arbitrary",)),
    )(dmerged, ya, yb, pm, pm, ba, bb)


_GELU_C = math.sqrt(2.0 / math.pi)


def _gelu(g):
    t = jnp.tanh(_GELU_C * (g + 0.044715 * g * g * g))
    return 0.5 * g * (1.0 + t), t


def _gelu_grad(g, t):
    return 0.5 * (1.0 + t) + 0.5 * g * (1.0 - t * t) * _GELU_C * (1.0 + 3 * 0.044715 * g * g)


def _shift_down(v, halo_ref, first, rows):
    T = v.shape[0]
    keep = jnp.where(first, 0.0, 1.0)
    h7 = halo_ref[7:8, :] * keep
    h6 = halo_ref[6:7, :] * keep
    m1 = jnp.where(rows == 0, h7, pltpu.roll(v, 1, 0))
    m2 = jnp.where(rows == 0, h6, jnp.where(rows == 1, h7, pltpu.roll(v, 2, 0)))
    return m1, m2


def _conv_act_fwd(up, cw, cb, name):
    S, F2 = up.shape
    Fh = F2 // 2
    T = _tile(S, ROW_T)
    tc = _tile(Fh, 256)
    ncol = Fh // tc
    hb = T // 8

    def body(ua_ref, ug_ref, ha_ref, hg_ref, wa_ref, wg_ref, ba_ref, bg_ref, o_ref):
        first = pl.program_id(0) == 0
        rows = lax.broadcasted_iota(jnp.int32, (T, tc), 0)

        def conv(u_ref, h_ref, w_ref, b_ref):
            v = u_ref[...]
            m1, m2 = _shift_down(v, h_ref, first, rows)
            return b_ref[...] + w_ref[0:1, :] * m2 + w_ref[1:2, :] * m1 + w_ref[2:3, :] * v

        a = conv(ua_ref, ha_ref, wa_ref, ba_ref)
        g = conv(ug_ref, hg_ref, wg_ref, bg_ref)
        o_ref[...] = (_gelu(g)[0] * a).astype(o_ref.dtype)

    halo = lambda off: pl.BlockSpec((8, tc), lambda i, j: (jnp.maximum(i * hb - 1, 0), j + off))
    return pl.pallas_call(
        body, name=name, out_shape=jax.ShapeDtypeStruct((S, Fh), BF16), grid=(S // T, ncol),
        in_specs=[pl.BlockSpec((T, tc), lambda i, j: (i, j)), pl.BlockSpec((T, tc), lambda i, j: (i, j + ncol)),
                  halo(0), halo(ncol),
                  pl.BlockSpec((3, tc), lambda i, j: (0, j)), pl.BlockSpec((3, tc), lambda i, j: (0, j + ncol)),
                  pl.BlockSpec((1, tc), lambda i, j: (0, j)), pl.BlockSpec((1, tc), lambda i, j: (0, j + ncol))],
        out_specs=pl.BlockSpec((T, tc), lambda i, j: (i, j)),
        compiler_params=_cparams(("parallel", "parallel")),
    )(up, up, up, up, cw, cw, cb, cb)


def _conv_act_bwd(up, dact, cw, cb, name):
    S, F2 = up.shape
    Fh = F2 // 2
    T = _tile(S, ROW_T)
    tc = _tile(Fh, 256)
    ncol, nrow, hb, nhb = Fh // tc, S // T, T // 8, S // 8

    def body(ua_ref, ug_ref, ha_ref, hg_ref, na_ref, ng_ref, wa_ref, wg_ref, ba_ref, bg_ref, da_ref, dn_ref,
             dpa_ref, dpg_ref, dwa_ref, dwg_ref, dba_ref, dbg_ref, dua_n, dug_n):
        i = pl.program_id(1)
        first = i == 0
        rows = lax.broadcasted_iota(jnp.int32, (T, tc), 0)
        rows8 = lax.broadcasted_iota(jnp.int32, (8, tc), 0)

        def conv(v, m1, m2, w_ref, b_ref):
            return b_ref[...] + w_ref[0:1, :] * m2 + w_ref[1:2, :] * m1 + w_ref[2:3, :] * v

        def du_of(a, g, dact_v):
            gel, t = _gelu(g)
            return dact_v * gel, dact_v * a * _gelu_grad(g, t)

        va, vg = ua_ref[...], ug_ref[...]
        a1, a2 = _shift_down(va, ha_ref, first, rows)
        g1, g2 = _shift_down(vg, hg_ref, first, rows)
        dua, dug = du_of(conv(va, a1, a2, wa_ref, ba_ref), conv(vg, g1, g2, wg_ref, bg_ref), da_ref[...])

        @pl.when(first)
        def _():
            for r in (dwa_ref, dwg_ref, dba_ref, dbg_ref):
                r[...] = jnp.zeros_like(r)

        for du, taps, dw_ref, db_ref in ((dua, (a2, a1, va), dwa_ref, dba_ref), (dug, (g2, g1, vg), dwg_ref, dbg_ref)):
            db_ref[...] += jnp.sum(du, axis=0, keepdims=True)
            for j in range(3):
                dw_ref[j:j + 1, :] += jnp.sum(du * taps[j], axis=0, keepdims=True)

        def below(n_ref, u_ref):
            v = n_ref[...]
            l1, l2 = u_ref[T - 1:T, :], u_ref[T - 2:T - 1, :]
            m1 = jnp.where(rows8 == 0, l1, pltpu.roll(v, 1, 0))
            m2 = jnp.where(rows8 == 0, l2, jnp.where(rows8 == 1, l1, pltpu.roll(v, 2, 0)))
            return v, m1, m2

        keep = jnp.where(i == nrow - 1, 0.0, 1.0)
        na, ng = below(na_ref, ua_ref), below(ng_ref, ug_ref)
        dna, dng = du_of(conv(*na, wa_ref, ba_ref), conv(*ng, wg_ref, bg_ref), dn_ref[...] * keep)
        dua_n[...] = dna
        dug_n[...] = dng

        for du, n_ref, w_ref, o_ref in ((dua, dua_n, wa_ref, dpa_ref), (dug, dug_n, wg_ref, dpg_ref)):
            n0, n1 = n_ref[0:1, :], n_ref[1:2, :]
            p1 = jnp.where(rows == T - 1, n0, pltpu.roll(du, T - 1, 0))
            p2 = jnp.where(rows == T - 2, n0, jnp.where(rows == T - 1, n1, pltpu.roll(du, T - 2, 0)))
            o_ref[...] = (w_ref[2:3, :] * du + w_ref[1:2, :] * p1 + w_ref[0:1, :] * p2).astype(o_ref.dtype)

    tile = lambda off: pl.BlockSpec((T, tc), lambda j, i: (i, j + off))
    above = lambda off: pl.BlockSpec((8, tc), lambda j, i: (jnp.maximum(i * hb - 1, 0), j + off))
    under = lambda off: pl.BlockSpec((8, tc), lambda j, i: (jnp.minimum((i + 1) * hb, nhb - 1), j + off))
    vec = lambda n, off: pl.BlockSpec((n, tc), lambda j, i: (0, j + off))
    act = jax.ShapeDtypeStruct((S, Fh), BF16)
    return pl.pallas_call(
        body, name=name,
        out_shape=(act, act, jax.ShapeDtypeStruct((3, Fh), F32), jax.ShapeDtypeStruct((3, Fh), F32),
                   jax.ShapeDtypeStruct((1, Fh), F32), jax.ShapeDtypeStruct((1, Fh), F32)),
        grid=(ncol, nrow),
        in_specs=[tile(0), tile(ncol), above(0), above(ncol), under(0), under(ncol),
                  vec(3, 0), vec(3, ncol), vec(1, 0), vec(1, ncol), tile(0), under(0)],
        out_specs=(tile(0), tile(0), vec(3, 0), vec(3, 0), vec(1, 0), vec(1, 0)),
        scratch_shapes=[pltpu.VMEM((8, tc), F32), pltpu.VMEM((8, tc), F32)],
        compiler_params=_cparams(("parallel", "arbitrary")),
    )(up, up, up, up, up, up, cw, cw, cb, cb, dact, dact)


def _split3(x):
    hi = x.astype(BF16)
    r1 = x - hi.astype(F32)
    mid = r1.astype(BF16)
    lo = (r1 - mid.astype(F32)).astype(BF16)
    return hi, mid, lo


def _tri_dot(tri, x):
    hi, mid, lo = _split3(x)
    return _dot_nn(tri, hi) + _dot_nn(tri, mid) + _dot_nn(tri, lo)


def _log_sigmoid(x):
    return jnp.minimum(x, 0.0) - jnp.log(1.0 + jnp.exp(-jnp.abs(x)))


def _tri_mask(n, lower):
    r = lax.broadcasted_iota(jnp.int32, (n, n), 0)
    c = lax.broadcasted_iota(jnp.int32, (n, n), 1)
    return (r >= c) if lower else (r <= c)


def _gates_fwd(ps, bi, bf, bff, name):
    S = ps.shape[0]
    NC = S // MLC

    def body(ps_ref, bi_ref, bf_ref, bff_ref, a_ref, A_ref, wi_ref, em_ref, wk_ref, dec_ref, F_ref, m_scr, f_scr):
        @pl.when(pl.program_id(0) == 0)
        def _():
            m_scr[...] = jnp.zeros_like(m_scr)
            f_scr[...] = jnp.zeros_like(f_scr)

        rows = lax.broadcasted_iota(jnp.int32, (MLC, LANES), 0)
        ltri = _tri_mask(MLC, True).astype(BF16)
        li = GATE_CAP * jnp.tanh((ps_ref[:, 0:LANES] + bi_ref[...]) / GATE_CAP)
        lf = _log_sigmoid(GATE_CAP * jnp.tanh((ps_ref[:, LANES:2 * LANES] + bf_ref[...]) / GATE_CAP))
        b = _tri_dot(ltri, lf)
        a = li - b
        cm = a
        sh = 1
        while sh < MLC:
            cm = jnp.where(rows >= sh, jnp.maximum(cm, pltpu.roll(cm, sh, 0)), cm)
            sh *= 2
        m0 = m_scr[...]
        A = jnp.maximum(cm, m0)
        a_ref[...] = a
        A_ref[...] = A
        A_last = A_ref[MLC - 1:MLC, :]
        wi_ref[...] = jnp.exp(m0 - A)
        em_ref[...] = jnp.exp(-(b + A))
        wk_ref[...] = jnp.exp(a - A_last)
        dec_ref[0] = jnp.exp(m0 - A_last)
        F_ref[...] = b
        m_scr[...] = F_ref[MLC - 1:MLC, :] + A_last
        lfg = _log_sigmoid(ps_ref[:, 2 * LANES:3 * LANES] + bff_ref[...])
        F_ref[...] = _tri_dot(ltri, lfg) + f_scr[...]
        f_scr[...] = F_ref[MLC - 1:MLC, :]

    col = pl.BlockSpec((MLC, LANES), lambda c: (c, 0))
    vec = pl.BlockSpec((1, LANES), lambda c: (0, 0))
    cs = jax.ShapeDtypeStruct((S, LANES), F32)
    return pl.pallas_call(
        body, name=name,
        out_shape=(cs, cs, cs, cs, cs, jax.ShapeDtypeStruct((NC, 1, LANES), F32), cs),
        grid=(NC,), in_specs=[pl.BlockSpec((MLC, N_SMALL), lambda c: (c, 0)), vec, vec, vec],
        out_specs=(col, col, col, col, col, pl.BlockSpec((1, 1, LANES), lambda c: (c, 0, 0)), col),
        scratch_shapes=[pltpu.VMEM((1, LANES), F32), pltpu.VMEM((1, LANES), F32)],
        compiler_params=_cparams(("arbitrary",)),
    )(ps, bi, bf, bff)


def _gates_bwd(ps, bi, bf, bff, rk, kc, tch, dF, name):
    S = ps.shape[0]
    NC = S // MLC

    def body(ps_ref, bi_ref, bf_ref, bff_ref, rk_ref, kc_ref, t_ref, dF_ref, dps_ref, db_ref, carry):
        @pl.when(pl.program_id(0) == 0)
        def _():
            carry[...] = jnp.zeros_like(carry)
            db_ref[...] = jnp.zeros_like(db_ref)

        lanes = lax.broadcasted_iota(jnp.int32, (MLC, LANES), 1)
        utri = _tri_mask(MLC, False).astype(BF16)
        ti = jnp.tanh((ps_ref[:, 0:LANES] + bi_ref[...]) / GATE_CAP)
        t_end, t_start = t_ref[0, 0:1, :], t_ref[0, 1:2, :]
        rk = rk_ref[...]
        rk = rk - (jnp.sum(rk, axis=0, keepdims=True) - (t_start - t_end)) * (1.0 / MLC)
        dpi = jnp.where(lanes < ML_HEADS, (kc_ref[...] - rk) * (1.0 - ti * ti), 0.0)
        tf = jnp.tanh((ps_ref[:, LANES:2 * LANES] + bf_ref[...]) / GATE_CAP)
        dlf = _tri_dot(utri, rk) + t_end
        dpf = jnp.where(lanes < ML_HEADS, dlf * jax.nn.sigmoid(-GATE_CAP * tf) * (1.0 - tf * tf), 0.0)
        dFv = dF_ref[...]
        dlfg = _tri_dot(utri, dFv) + carry[...]
        carry[...] += jnp.sum(dFv, axis=0, keepdims=True)
        dpff = jnp.where(lanes < FOX_HEADS, dlfg * jax.nn.sigmoid(-(ps_ref[:, 2 * LANES:3 * LANES] + bff_ref[...])), 0.0)
        for n, dp in enumerate((dpi, dpf, dpff)):
            dps_ref[:, n * LANES:(n + 1) * LANES] = dp.astype(dps_ref.dtype)
            db_ref[:, n * LANES:(n + 1) * LANES] += jnp.sum(dp, axis=0, keepdims=True)

    rev = lambda c: (NC - 1 - c, 0)
    col = pl.BlockSpec((MLC, LANES), rev)
    vec = pl.BlockSpec((1, LANES), lambda c: (0, 0))
    wide = pl.BlockSpec((MLC, N_SMALL), rev)
    return pl.pallas_call(
        body, name=name,
        out_shape=(jax.ShapeDtypeStruct((S, N_SMALL), BF16), jax.ShapeDtypeStruct((1, N_SMALL), F32)),
        grid=(NC,),
        in_specs=[wide, vec, vec, vec, col, col, pl.BlockSpec((1, 2, LANES), lambda c: (NC - 1 - c, 0, 0)), col],
        out_specs=(wide, pl.BlockSpec((1, N_SMALL), lambda c: (0, 0))),
        scratch_shapes=[pltpu.VMEM((1, LANES), F32)],
        compiler_params=_cparams(("arbitrary",)),
    )(ps, bi, bf, bff, rk, kc, tch, dF)


_ML_SCALE = ML_DQK ** -0.5


def _ml_specs(rev, NC):
    idx = (lambda c: NC - 1 - c) if rev else (lambda c: c)
    qk = lambda blk: pl.BlockSpec((MLC, ML_HEADS * ML_DQK), lambda c: (idx(c), blk))
    wide = lambda blk: pl.BlockSpec((MLC, D_MODEL), lambda c: (idx(c), blk))
    col = pl.BlockSpec((MLC, LANES), lambda c: (idx(c), 0))
    return idx, qk, wide, col


def _ml_intra(q_ref, k_ref, arow_ref, A_ref, h):
    hs = slice(h * ML_DQK, (h + 1) * ML_DQK)
    qf = q_ref[:, hs] * _ML_SCALE
    kf = k_ref[:, hs]
    qb, kb = qf.astype(BF16), kf.astype(BF16)
    qk = _dot_nt(qb, kb)
    logw = arow_ref[h:h + 1, :] - A_ref[:, h:h + 1]
    W = jnp.exp(jnp.where(_tri_mask(MLC, True), logw, -1e30))
    return qb, kb, qf, kf, qk, W


def _mlstm_fwd(pm, a_row, A, wi, em, wk, dec, w_hn, name):
    S = pm.shape[0]
    NC = S // MLC
    _, qk, wide, col = _ml_specs(False, NC)

    def body(q_ref, k_ref, v_ref, o_ref, arow_ref, A_ref, wi_ref, em_ref, wk_ref, dec_ref, whn_ref,
             ha_ref, hp_ref, den_ref, cst_ref, nst_ref, C_scr, n_scr):
        @pl.when(pl.program_id(0) == 0)
        def _():
            C_scr[...] = jnp.zeros_like(C_scr)
            n_scr[...] = jnp.zeros_like(n_scr)

        lanes = lax.broadcasted_iota(jnp.int32, (MLC, LANES), 1)
        den_tile = jnp.zeros((MLC, LANES), F32)
        for h in range(ML_HEADS):
            vs = slice(h * ML_DV, (h + 1) * ML_DV)
            qb, kb, qf, kf, qk_, W = _ml_intra(q_ref, k_ref, arow_ref, A_ref, h)
            vb = v_ref[:, vs].astype(BF16)
            Cf = C_scr[h]
            Cb = Cf.astype(BF16)
            nrow = n_scr[h]
            cst_ref[0, h] = Cb
            nst_ref[0, h] = nrow
            s = qk_ * W
            wic = wi_ref[:, h:h + 1]
            num = _dot_nn(s.astype(BF16), vb) + wic * _dot_nt(qb, Cb)
            den = jnp.sum(s, axis=1, keepdims=True) + wic * jnp.sum(qf * nrow, axis=1, keepdims=True)
            hp = num / jnp.maximum(jnp.abs(den), em_ref[:, h:h + 1])
            hp_ref[:, vs] = hp
            den_tile = jnp.where(lanes == h, den, den_tile)
            hn = hp * _rstd(hp) * whn_ref[:, vs]
            ha_ref[:, vs] = (hn * jax.nn.sigmoid(o_ref[:, vs])).astype(ha_ref.dtype)
            wkc = wk_ref[:, h:h + 1]
            kw = kf * wkc
            d = dec_ref[0, :, h:h + 1]
            C_scr[h] = d * Cf + _dot_tn(vb, kw.astype(BF16))
            n_scr[h] = d * nrow + jnp.sum(kw, axis=0, keepdims=True)
        den_ref[...] = den_tile

    return pl.pallas_call(
        body, name=name,
        out_shape=(jax.ShapeDtypeStruct((S, D_MODEL), BF16), jax.ShapeDtypeStruct((S, D_MODEL), F32),
                   jax.ShapeDtypeStruct((S, LANES), F32),
                   jax.ShapeDtypeStruct((NC, ML_HEADS, ML_DV, ML_DQK), BF16),
                   jax.ShapeDtypeStruct((NC, ML_HEADS, 1, ML_DQK), F32)),
        grid=(NC,),
        in_specs=[qk(C_QM // 512), qk(C_KM // 512), wide(C_VM // D_MODEL), wide(C_OM // D_MODEL),
                  pl.BlockSpec((8, MLC), lambda c: (0, c)), col, col, col, col,
                  pl.BlockSpec((1, 1, LANES), lambda c: (c, 0, 0)), pl.BlockSpec((1, D_MODEL), lambda c: (0, 0))],
        out_specs=(pl.BlockSpec((MLC, D_MODEL), lambda c: (c, 0)), pl.BlockSpec((MLC, D_MODEL), lambda c: (c, 0)),
                   col, pl.BlockSpec((1, ML_HEADS, ML_DV, ML_DQK), lambda c: (c, 0, 0, 0)),
                   pl.BlockSpec((1, ML_HEADS, 1, ML_DQK), lambda c: (c, 0, 0, 0))),
        scratch_shapes=[pltpu.VMEM((ML_HEADS, ML_DV, ML_DQK), F32), pltpu.VMEM((ML_HEADS, 1, ML_DQK), F32)],
        compiler_params=_cparams(("arbitrary",)),
    )(pm, pm, pm, pm, a_row, A, wi, em, wk, dec, w_hn)


def _mlstm_bwd(dha, pm, hp_all, den_all, a_row, A, wi, em, wk, dec, cst, nst, w_hn, name):
    S = pm.shape[0]
    NC = S // MLC
    idx, qk, wide, col = _ml_specs(True, NC)

    def body(dha_ref, q_ref, k_ref, v_ref, o_ref, hp_ref, den_ref, arow_ref, A_ref, wi_ref, em_ref, wk_ref,
             dec_ref, cst_ref, nst_ref, whn_ref,
             dqk_ref, dv_ref, do_ref, rk_ref, kc_ref, t_ref, dwhn_ref, dC_scr, dn_scr, t_scr):
        @pl.when(pl.program_id(0) == 0)
        def _():
            dC_scr[...] = jnp.zeros_like(dC_scr)
            dn_scr[...] = jnp.zeros_like(dn_scr)
            t_scr[...] = jnp.zeros_like(t_scr)
            dwhn_ref[...] = jnp.zeros_like(dwhn_ref)

        lanes = lax.broadcasted_iota(jnp.int32, (MLC, LANES), 1)
        lane1 = lax.broadcasted_iota(jnp.int32, (1, LANES), 1)
        t_ref[0, 0:1, :] = t_scr[...]
        rk_tile = jnp.zeros((MLC, LANES), F32)
        kc_tile = jnp.zeros((MLC, LANES), F32)
        t_new = jnp.zeros((1, LANES), F32)
        for h in range(ML_HEADS):
            hs = slice(h * ML_DQK, (h + 1) * ML_DQK)
            vs = slice(h * ML_DV, (h + 1) * ML_DV)
            hp = hp_ref[:, vs]
            sig = jax.nn.sigmoid(o_ref[:, vs])
            whn = whn_ref[:, vs]
            r = _rstd(hp)
            dga = dha_ref[:, vs]
            do_ref[:, vs] = (dga * (hp * r * whn) * sig * (1.0 - sig)).astype(do_ref.dtype)
            dhn = dga * sig
            dhp, dwt = _rmsnorm_bwd_math(dhn, hp, whn)
            dwhn_ref[:, vs] += jnp.sum(dwt, axis=0, keepdims=True)
            den = den_ref[:, h:h + 1]
            floor = em_ref[:, h:h + 1]
            D = jnp.maximum(jnp.abs(den), floor)
            dnum = dhp / D
            dh_h = jnp.sum(dhp * hp, axis=1, keepdims=True)
            active = jnp.abs(den) >= floor
            dden = -dh_h / D * jnp.where(active, jnp.sign(den), 0.0)
            phi = jnp.where(active, 0.0, dh_h)
            qb, kb, qf, kf, qk_, W = _ml_intra(q_ref, k_ref, arow_ref, A_ref, h)
            vf = v_ref[:, vs]
            vb = vf.astype(BF16)
            Cb = cst_ref[0, h]
            nrow = nst_ref[0, h]
            wic = wi_ref[:, h:h + 1]
            wkc = wk_ref[:, h:h + 1]
            d = dec_ref[0, :, h:h + 1]
            dCn = dC_scr[h]
            dCb = dCn.astype(BF16)
            dnn = dn_scr[h]
            dnumb = dnum.astype(BF16)
            s = qk_ * W
            ds = (_dot_nt(dnumb, vb) + dden) * W
            dsb = ds.astype(BF16)
            dnw = (wic * dnum).astype(BF16)
            wd = wic * dden
            kw = kf * wkc
            dv_state = _dot_nt(kw.astype(BF16), dCb)
            dq = _dot_nn(dsb, kb) + _dot_nn(dnw, Cb) + wd * nrow
            dk_state = wkc * (_dot_nn(vb, dCb) + dnn)
            dk = _dot_tn(dsb, qb) + dk_state
            dv = _dot_tn(s.astype(BF16), dnumb) + dv_state
            dC = d * dCn + _dot_tn(dnw, qb)
            dn = d * dnn + jnp.sum(wd * qf, axis=0, keepdims=True)
            dC_scr[h] = dC
            dn_scr[h] = dn
            dqk_ref[:, hs] = (dq * _ML_SCALE).astype(dqk_ref.dtype)
            dqk_ref[:, C_KM + h * ML_DQK:C_KM + (h + 1) * ML_DQK] = dk.astype(dqk_ref.dtype)
            dv_ref[:, vs] = dv.astype(dv_ref.dtype)
            G = ds * qk_
            inter = _dot_nt(qb, Cb)
            qn = jnp.sum(qf * nrow, axis=1, keepdims=True)
            R = (jnp.sum(G, axis=1, keepdims=True)
                 + wic * (jnp.sum(dnum * inter, axis=1, keepdims=True) + dden * qn))
            K = jnp.sum(G.T, axis=1, keepdims=True) + jnp.sum(kf * dk_state, axis=1, keepdims=True)
            rk_tile = jnp.where(lanes == h, R - K, rk_tile)
            kc_tile = jnp.where(lanes == h, phi, kc_tile)
            tt = (jnp.sum(jnp.sum(dC * Cb.astype(F32), axis=1, keepdims=True), axis=0, keepdims=True)
                  + jnp.sum(dn * nrow, axis=1, keepdims=True))
            t_new = jnp.where(lane1 == h, tt, t_new)
        rk_ref[...] = rk_tile
        kc_ref[...] = kc_tile
        t_ref[0, 1:2, :] = t_new
        t_scr[...] = t_new

    act = lambda n: jax.ShapeDtypeStruct((S, n), BF16)
    cs = jax.ShapeDtypeStruct((S, LANES), F32)
    rowblk = lambda n: pl.BlockSpec((MLC, n), lambda c: (idx(c), 0))
    return pl.pallas_call(
        body, name=name,
        out_shape=(act(D_MODEL), act(D_MODEL), act(D_MODEL), cs, cs,
                   jax.ShapeDtypeStruct((NC, 2, LANES), F32), jax.ShapeDtypeStruct((1, D_MODEL), F32)),
        grid=(NC,),
        in_specs=[rowblk(D_MODEL), qk(C_QM // 512), qk(C_KM // 512), wide(C_VM // D_MODEL), wide(C_OM // D_MODEL),
                  rowblk(D_MODEL), col, pl.BlockSpec((8, MLC), lambda c: (0, idx(c))), col, col, col, col,
                  pl.BlockSpec((1, 1, LANES), lambda c: (idx(c), 0, 0)),
                  pl.BlockSpec((1, ML_HEADS, ML_DV, ML_DQK), lambda c: (idx(c), 0, 0, 0)),
                  pl.BlockSpec((1, ML_HEADS, 1, ML_DQK), lambda c: (idx(c), 0, 0, 0)),
                  pl.BlockSpec((1, D_MODEL), lambda c: (0, 0))],
        out_specs=(rowblk(D_MODEL), rowblk(D_MODEL), rowblk(D_MODEL), col, col,
                   pl.BlockSpec((1, 2, LANES), lambda c: (idx(c), 0, 0)), pl.BlockSpec((1, D_MODEL), lambda c: (0, 0))),
        scratch_shapes=[pltpu.VMEM((ML_HEADS, ML_DV, ML_DQK), F32), pltpu.VMEM((ML_HEADS, 1, ML_DQK), F32),
                        pltpu.VMEM((1, LANES), F32)],
        compiler_params=_cparams(("arbitrary",)),
    )(dha, pm, pm, pm, pm, hp_all, den_all, a_row, A, wi, em, wk, dec, cst, nst, w_hn)


_FOX_SCALE = FOX_DH ** -0.5
_NEG = -1e30
_LOG2E = 1.4426950408889634
_LN2 = 0.6931471805599453
_QF_BLK, _KF_BLK, _VF_BLK = 0, FOX_HEADS, 2 * FOX_HEADS


def _lane_pick(tile, lane):
    lanes = lax.broadcasted_iota(jnp.int32, tile.shape, 1)
    return jnp.sum(jnp.where(lanes == lane, tile, 0.0), axis=1, keepdims=True)


def _col_to_row(col):
    return jnp.max(jnp.broadcast_to(col, (col.shape[0], LANES)).T, axis=0, keepdims=True)


def _causal(q0, k0, shape, q_axis):
    qpos = q0 + lax.broadcasted_iota(jnp.int32, shape, q_axis)
    kpos = k0 + lax.broadcasted_iota(jnp.int32, shape, 1 - q_axis)
    return kpos <= qpos


def _fox_fwd(pf, fc, fk_row, name):
    S = pf.shape[0]
    TQ, TK = FOX_TQ_FWD, FOX_TK
    nq, nk = S // TQ, S // TK
    c1 = _FOX_SCALE * _LOG2E

    def body(q_ref, k_ref, v_ref, fc_ref, fr_ref, o_ref, lse_ref):
        h, i = pl.program_id(0), pl.program_id(1)
        qb = q_ref[...]
        fq2 = _lane_pick(fc_ref[...], h) * _LOG2E

        def step(j, carry, masked):
            m, l, acc = carry
            off = pl.multiple_of(j * TK, TK)
            t = _dot_nt(qb, k_ref[pl.ds(off, TK), :]) * c1 - fr_ref[0, j] * _LOG2E
            if masked:
                t = jnp.where(_causal(i * TQ, j * TK, (TQ, TK), 0), t, _NEG)
            m_new = jnp.maximum(m, jnp.max(t, axis=1, keepdims=True) + fq2)
            alpha = jnp.exp2(m - m_new)
            p = jnp.exp2(t + (fq2 - m_new))
            l = alpha * l + jnp.sum(p, axis=1, keepdims=True)
            acc = alpha * acc + _dot_nn(p.astype(BF16), v_ref[pl.ds(off, TK), :])
            return m_new, l, acc

        init = (jnp.full((TQ, 1), _NEG, F32), jnp.zeros((TQ, 1), F32), jnp.zeros((TQ, FOX_DH), F32))
        last = (i * TQ) // TK
        carry = lax.fori_loop(0, last, lambda j, c: step(j, c, False), init)
        m, l, acc = step(last, carry, True)
        o_ref[...] = (acc / l).astype(o_ref.dtype)
        lse_ref[0, 0] = _col_to_row((m + jnp.log2(l)) * _LN2)

    head = lambda blk: pl.BlockSpec((S, FOX_DH), lambda h, i: (0, blk + h))
    return pl.pallas_call(
        body, name=name,
        out_shape=(jax.ShapeDtypeStruct((S, D_MODEL), BF16), jax.ShapeDtypeStruct((FOX_HEADS, nq, 1, TQ), F32)),
        grid=(FOX_HEADS, nq),
        in_specs=[pl.BlockSpec((TQ, FOX_DH), lambda h, i: (i, _QF_BLK + h)), head(_KF_BLK), head(_VF_BLK),
                  pl.BlockSpec((TQ, LANES), lambda h, i: (i, 0)),
                  pl.BlockSpec((1, nk, 1, TK), lambda h, i: (h, 0, 0, 0))],
        out_specs=(pl.BlockSpec((TQ, FOX_DH), lambda h, i: (i, h)),
                   pl.BlockSpec((1, 1, 1, TQ), lambda h, i: (h, i, 0, 0))),
        compiler_params=_cparams(("parallel", "arbitrary")),
    )(pf, pf, pf, fc, fk_row)


def _fox_bwd(dhb, hb, pf, lse_row, fq_row, fc, name):
    S = pf.shape[0]
    TQ, TK = FOX_TQ, FOX_TK
    nq, nk, r = S // TQ, S // TK, TK // TQ
    c1 = _FOX_SCALE * _LOG2E

    def body(q_ref, k_ref, v_ref, do_ref, o_ref, lse_ref, fq_ref, fc_ref,
             dq_ref, dk_ref, dv_ref, dFk_ref, dFq_ref, dq_acc, qside, delta, dk_acc, dv_acc, cs_acc):
        h, j = pl.program_id(0), pl.program_id(1)

        @pl.when(j == 0)
        def _():
            dq_acc[...] = jnp.zeros_like(dq_acc)
            dFq_ref[...] = jnp.zeros_like(dFq_ref)

            def fill(b, _):
                off = pl.multiple_of(b * TQ, TQ)
                prod = do_ref[pl.ds(off, TQ), :].astype(F32) * o_ref[pl.ds(off, TQ), :].astype(F32)
                delta[b] = jnp.sum(prod.T, axis=0, keepdims=True)
                qside[b] = (fq_ref[0, b] - lse_ref[0, b]) * _LOG2E
                return 0

            lax.fori_loop(0, nq, fill, 0)

        kb = k_ref[...]
        vb = v_ref[...]
        fk2 = _lane_pick(fc_ref[...], h) * _LOG2E
        dk_acc[...] = jnp.zeros_like(dk_acc)
        dv_acc[...] = jnp.zeros_like(dv_acc)
        cs_acc[...] = jnp.zeros_like(cs_acc)

        def step(i, masked):
            off = pl.multiple_of(i * TQ, TQ)
            qb = q_ref[pl.ds(off, TQ), :]
            dob = do_ref[pl.ds(off, TQ), :]
            t = _dot_nt(kb, qb) * c1 + qside[i] - fk2
            if masked:
                t = jnp.where(_causal(i * TQ, j * TK, (TK, TQ), 1), t, _NEG)
            p = jnp.exp2(t)
            dv_acc[...] += _dot_nn(p.astype(BF16), dob)
            ds = p * (_dot_nt(vb, dob) - delta[i])
            dsb = ds.astype(BF16)
            dk_acc[...] += _dot_nn(dsb, qb)
            dq_acc[pl.ds(off, TQ), :] += _dot_tn(dsb, kb)
            cs_acc[...] += jnp.sum(ds, axis=1, keepdims=True)
            dFq_ref[0, i] += jnp.sum(ds, axis=0, keepdims=True)

        for d in range(r):
            step(r * j + d, True)

        def rest(i, _):
            step(i, False)
            return 0

        lax.fori_loop(r * j + r, nq, rest, 0)
        dk_ref[...] = (dk_acc[...] * _FOX_SCALE).astype(dk_ref.dtype)
        dv_ref[...] = dv_acc[...].astype(dv_ref.dtype)
        dFk_ref[0, 0] = -_col_to_row(cs_acc[...])

        @pl.when(j == nk - 1)
        def _():
            dq_ref[...] = (dq_acc[...] * _FOX_SCALE).astype(dq_ref.dtype)

    head = lambda blk: pl.BlockSpec((S, FOX_DH), lambda h, j: (0, blk + h))
    kblk = lambda blk: pl.BlockSpec((TK, FOX_DH), lambda h, j: (j, blk + h))
    qrows = pl.BlockSpec((1, nq, 1, TQ), lambda h, j: (h, 0, 0, 0))
    act = jax.ShapeDtypeStruct((S, D_MODEL), BF16)
    return pl.pallas_call(
        body, name=name,
        out_shape=(act, act, act, jax.ShapeDtypeStruct((FOX_HEADS, nk, 1, TK), F32),
                   jax.ShapeDtypeStruct((FOX_HEADS, nq, 1, TQ), F32)),
        grid=(FOX_HEADS, nk),
        in_specs=[head(_QF_BLK), kblk(_KF_BLK), kblk(_VF_BLK), head(0), head(0), qrows, qrows,
                  pl.BlockSpec((TK, LANES), lambda h, j: (j, 0))],
        out_specs=(head(0), kblk(0), kblk(0), pl.BlockSpec((1, 1, 1, TK), lambda h, j: (h, j, 0, 0)), qrows),
        scratch_shapes=[pltpu.VMEM((S, FOX_DH), F32), pltpu.VMEM((nq, 1, TQ), F32), pltpu.VMEM((nq, 1, TQ), F32),
                        pltpu.VMEM((TK, FOX_DH), F32), pltpu.VMEM((TK, FOX_DH), F32), pltpu.VMEM((TK, 1), F32)],
        compiler_params=_cparams(("parallel", "arbitrary")),
    )(pf, pf, pf, dhb, hb, lse_row, fq_row, fc)


def _pad_lanes(v):
    return jnp.pad(v, ((0, 0), (0, LANES - v.shape[1])))


def _local_step(x, target, wmain_t, wsmall_t, rest_weights, p, on_grads, token):
    S = x.shape[0]
    bi, bf, bff = _pad_lanes(p["b_ml_i"]), _pad_lanes(p["b_ml_f"]), _pad_lanes(p["b_fox_f"])

    h0 = _rmsnorm_fwd(x, p["norm_mix_pre"] + token[0:1, 0:1], "norm_mix_pre")
    pm = _mm(h0, wmain_t[:N_ML], "nt", F32, "proj_mlstm")
    pf = _mm(h0, wmain_t[N_ML:N_ML + N_FOX], "nt", BF16, "proj_fox")
    pg = _mm(h0, wmain_t[N_ML + N_FOX:], "nt", F32, "proj_merge")
    ps = _mm(h0, wsmall_t, "nt", F32, "proj_gates")
    a, A, wi, em, wk, dec, Fc = _gates_fwd(ps, bi, bf, bff, "gates_fwd")
    a_row = a[:, :8].T
    ha, hp, den, cst, nst = _mlstm_fwd(pm, a_row, A, wi, em, wk, dec, p["ml_head_norm"], "mlstm_fwd")
    ft = Fc[:, :FOX_HEADS].T
    fq_row = ft.reshape(FOX_HEADS, S // FOX_TQ, 1, FOX_TQ)
    fk_row = ft.reshape(FOX_HEADS, S // FOX_TK, 1, FOX_TK)
    hb, lse_row = _fox_fwd(pf, Fc, fk_row, "fox_fwd")
    wa, wb, wout, wup, wdown = rest_weights(hb)
    ya = _mm(ha, wa, "nn", F32, "branch_a")
    yb = _mm(hb, wb, "nn", F32, "branch_b")
    merged = _merge_fwd(ya, yb, pg, p["b_gate_a"], p["b_gate_b"], "merge_fwd")
    z = _mm(merged, wout, "nn", F32, "out_proj")
    x1 = _resid_norm_fwd(x, z, p["norm_mix_post"], "resid_mix")
    h2 = _rmsnorm_fwd(x1, p["norm_ffn_pre"], "norm_ffn_pre")
    up = _mm(h2, wup, "nn", F32, "ffn_up")
    act = _conv_act_fwd(up, p["conv_w"], p["conv_b"], "conv_act_fwd")
    d = _mm(act, wdown, "nn", F32, "ffn_down")
    loss_row, dy, dd, g_norm_ffn_post = _loss_head(x1, d, p["norm_ffn_post"], target, "loss_head")
    dact = _mm(dd, wdown, "nt", F32, "d_act")
    g_wdown = _mm(act, dd, "tn", F32, "dw_down", tm=1408)
    dupa, dupg, dcwa, dcwg, dcba, dcbg = _conv_act_bwd(up, dact, p["conv_w"], p["conv_b"], "conv_act_bwd")
    g_conv_w = jnp.concatenate([dcwa, dcwg], axis=1)
    g_conv_b = jnp.concatenate([dcba, dcbg], axis=1)
    dh2 = _mm([dupa, dupg], wup, "nt", F32, "d_h2")
    g_wup = _mm(h2, [dupa, dupg], "tn", F32, "dw_up")
    token = on_grads("ffn", dict(w_up=g_wup, w_down=g_wdown))
    dx1, g_norm_ffn_pre = _rmsnorm_bwd([dh2], x1, p["norm_ffn_pre"] + token[0:1, 0:1], dy, F32, "norm_ffn_pre_bwd")
    dz, g_norm_mix_post = _rmsnorm_bwd([dx1], z, p["norm_mix_post"], None, BF16, "norm_mix_post_bwd")
    dmerged = _mm(dz, wout, "nt", F32, "d_merged")
    g_wout = _mm(merged, dz, "tn", F32, "dw_out")
    dya, dyb, dga, dgb, g_b_gate_a, g_b_gate_b = _merge_bwd(dmerged, ya, yb, pg, p["b_gate_a"], p["b_gate_b"], "merge_bwd")
    dha = _mm(dya, wa, "nt", F32, "d_ha")
    g_wa = _mm(ha, dya, "tn", F32, "dw_a")
    dhb = _mm(dyb, wb, "nt", BF16, "d_hb")
    g_wb = _mm(hb, dyb, "tn", F32, "dw_b")
    token = on_grads("mix", dict(w_out=g_wout, w_branch_a=g_wa, w_branch_b=g_wb))
    dqkm, dvm, dom, rk, kc, tch, g_ml_head_norm = _mlstm_bwd(
        dha, pm, hp, den, a_row, A, wi, em, wk, dec, cst, nst, p["ml_head_norm"] + token[0:1, 0:1], "mlstm_bwd")
    dqf, dkf, dvf, dFk, dFq = _fox_bwd(dhb, hb, pf, lse_row.reshape(fq_row.shape), fq_row, Fc, "fox_bwd")
    dF = jnp.pad((dFk.reshape(FOX_HEADS, S) + dFq.reshape(FOX_HEADS, S)).T, ((0, 0), (0, LANES - FOX_HEADS)))
    dps, dbias = _gates_bwd(ps, bi, bf, bff, rk, kc, tch, dF, "gates_bwd")
    dpm = [dqkm, dvm, dom, dqf, dkf, dvf, dga, dgb]
    g_wmain_t = _mm(dpm, h0, "tn", F32, "dw_main")
    token = on_grads("in", dict(w_in=g_wmain_t))
    g_wsmall_t = _mm(dps, h0, "tn", F32, "dw_gates")
    dh0s = _mm(dps, wsmall_t + token[0:1, 0:1].astype(BF16), "nn", F32, "d_h0_gates")
    dh0 = _mm(dpm, wmain_t, "nn", F32, "d_h0_main")
    grad_x, g_norm_mix_pre = _rmsnorm_bwd([dh0, dh0s], x, p["norm_mix_pre"], dx1, F32, "norm_mix_pre_bwd")

    big = dict(wsmall_t=g_wsmall_t)
    small = dict(norm_mix_pre=g_norm_mix_pre, ml_head_norm=g_ml_head_norm, b_gate_a=g_b_gate_a, b_gate_b=g_b_gate_b,
                 norm_mix_post=g_norm_mix_post, norm_ffn_pre=g_norm_ffn_pre, norm_ffn_post=g_norm_ffn_post,
                 conv_b=g_conv_b, b_ml_i=dbias[:, 0:ML_HEADS], b_ml_f=dbias[:, LANES:LANES + ML_HEADS],
                 b_fox_f=dbias[:, 2 * LANES:2 * LANES + FOX_HEADS], conv_w=g_conv_w)
    return loss_row, grad_x, big, small


def _row_tile(r, target=256):
    best = None
    for t in range(8, min(r, target) + 1, 8):
        if r % t == 0:
            best = t
    return best if best is not None else r


def _adamw(w, g, m, v, name):
    _, R, C = w.shape
    tr = _row_tile(R)
    tc = C
    if tr == R and R > 256:
        tc = 256

    def body(w_ref, g_ref, m_ref, v_ref, d_ref, mo_ref, vo_ref):
        gv = g_ref[...]
        mn = ADAM_B1 * m_ref[0] + (1.0 - ADAM_B1) * gv
        vn = ADAM_B2 * v_ref[0] + (1.0 - ADAM_B2) * (gv * gv)
        m_hat = mn / (1.0 - ADAM_B1 ** ADAM_STEP)
        v_hat = vn / (1.0 - ADAM_B2 ** ADAM_STEP)
        d_ref[0] = -ADAM_LR * (m_hat / (jnp.sqrt(v_hat) + ADAM_EPS) + ADAM_WD * w_ref[0])
        mo_ref[0] = mn
        vo_ref[0] = vn

    blk = pl.BlockSpec((1, tr, tc), lambda i, j: (0, i, j))
    o = jax.ShapeDtypeStruct((1, R, C), F32)
    return pl.pallas_call(
        body, name=name, out_shape=(o, o, o), grid=(R // tr, C // tc),
        in_specs=[blk, pl.BlockSpec((tr, tc), lambda i, j: (i, j)), blk, blk], out_specs=(blk,) * 3,
        compiler_params=_cparams(("parallel", "parallel")),
    )(w, g, m, v)


ANY = pl.BlockSpec(memory_space=pl.ANY)


def _place():
    x, y, c = lax.axis_index("x"), lax.axis_index("y"), lax.axis_index("c")
    chips = [(1 - x, y), (x, 1 - y), (1 - x, 1 - y)]
    return x, y, c, chips


def _block(ref, kind, k, rows=None):
    if kind == "rows":
        return ref.at[k] if rows is None else ref.at[k, pl.ds(*rows), :]
    cb = ref.shape[1] // 4
    return ref.at[:, pl.ds(k * cb, cb)] if rows is None else ref.at[pl.ds(*rows), pl.ds(k * cb, cb)]


def _gathered_shape(s, kind):
    return (4,) + s.shape if kind == "rows" else (s.shape[0], 4 * s.shape[1])


def _gather_weights(shards, kinds, smalls):
    n, ns = len(shards), len(smalls)

    def body(*refs):
        ins, sm_in = refs[:n], refs[n:n + ns]
        outs, sm_out = refs[n + ns:2 * n + ns], refs[2 * n + ns:2 * (n + ns)]
        send_sems, recv_sems, sm_send, sm_recv, local_sems = refs[2 * (n + ns):]
        x, y, c, chips = _place()
        sibling = (x, y, 1 - c)
        kme = 2 * x + y

        def half(a, k, hc):
            h = ins[a].shape[0] // 2
            return _block(outs[a], kinds[a], k, (hc * h, h))

        def remote(a, slot, src, dst, to):
            return pltpu.make_async_remote_copy(src_ref=src, dst_ref=dst, send_sem=send_sems.at[a * 7 + slot],
                                                recv_sem=recv_sems.at[a * 7 + slot], device_id=to, device_id_type=MESH)

        def sm_copy(b, j, k, to):
            return pltpu.make_async_remote_copy(src_ref=sm_in[b], dst_ref=sm_out[b].at[k], send_sem=sm_send.at[3 * b + j],
                                                recv_sem=sm_recv.at[3 * b + j], device_id=to, device_id_type=MESH)

        local = [pltpu.make_async_copy(sm_in[b], sm_out[b].at[kme], local_sems.at[b]) for b in range(ns)]
        for cp in local:
            cp.start()
        sends = [remote(a, 6, ins[a], _block(outs[a], kinds[a], kme), sibling) for a in range(n)]
        for a in range(n):
            h = ins[a].shape[0] // 2
            for j, chip in enumerate(chips):
                sends.append(remote(a, j, ins[a].at[pl.ds(c * h, h), :], half(a, kme, c), (*chip, c)))
        for b in range(ns):
            for j, chip in enumerate(chips):
                sends.append(sm_copy(b, j, kme, (*chip, c)))
        for cp in sends:
            cp.start()
        for a in range(n):
            for j, chip in enumerate(chips):
                kj = 2 * chip[0] + chip[1]
                remote(a, j, half(a, kj, c), half(a, kj, c), (*chip, c)).wait_recv()
                fwd = remote(a, 3 + j, half(a, kj, c), half(a, kj, c), sibling)
                fwd.start()
                sends.append(fwd)
        for a in range(n):
            for j, chip in enumerate(chips):
                kj = 2 * chip[0] + chip[1]
                remote(a, 3 + j, half(a, kj, 1 - c), half(a, kj, 1 - c), sibling).wait_recv()
        for b in range(ns):
            for j, chip in enumerate(chips):
                sm_copy(b, j, 2 * chip[0] + chip[1], (*chip, c)).wait_recv()
        for a in range(n):
            remote(a, 6, ins[a], _block(outs[a], kinds[a], kme), sibling).wait_recv()
        for cp in sends:
            cp.wait_send()
        for cp in local:
            cp.wait()

    outs = pl.pallas_call(
        body, name="gather_weights",
        out_shape=tuple([jax.ShapeDtypeStruct(_gathered_shape(s, k), s.dtype) for s, k in zip(shards, kinds)]
                        + [jax.ShapeDtypeStruct((4,) + s.shape, s.dtype) for s in smalls]),
        in_specs=[ANY] * (n + ns), out_specs=tuple([ANY] * (n + ns)),
        scratch_shapes=[pltpu.SemaphoreType.DMA((7 * n,)), pltpu.SemaphoreType.DMA((7 * n,)),
                        pltpu.SemaphoreType.DMA((3 * ns,)), pltpu.SemaphoreType.DMA((3 * ns,)),
                        pltpu.SemaphoreType.DMA((ns,))],
    )(*shards, *smalls)
    return outs[:n], outs[n:]


_IN_HBM = pl.BlockSpec(memory_space=pltpu.HBM)
_SEMS = pl.BlockSpec(memory_space=pltpu.SEMAPHORE)
_DATAFLOW = pltpu.SideEffectType.DATAFLOW_SIDE_EFFECTING


def _hbm(t):
    return pltpu.HBM(t.shape, t.dtype)


def _gather_copies(ins, outs, send_sems, recv_sems, kinds):
    x, y, c, chips = _place()
    kme = 2 * x + y
    cps = []
    for a in range(len(ins)):
        h = ins[a].shape[0] // 2
        for j, chip in enumerate(chips + [None]):
            to = (x, y, 1 - c) if chip is None else (*chip, c)
            src = ins[a] if chip is None else ins[a].at[pl.ds(c * h, h), :]
            dst = _block(outs[a], kinds[a], kme, None if chip is None else (c * h, h))
            cps.append(pltpu.make_async_remote_copy(src_ref=src, dst_ref=dst, send_sem=send_sems.at[4 * a + j],
                                                    recv_sem=recv_sems.at[4 * a + j], device_id=to, device_id_type=MESH))
    return cps


def _gather_start(shards, kinds, name):
    n = len(shards)
    outs = [lax.empty(_gathered_shape(s, k), s.dtype) for s, k in zip(shards, kinds)]

    def body(*refs):
        for cp in _gather_copies(refs[:n], refs[n:2 * n], refs[2 * n], refs[2 * n + 1], kinds):
            cp.start()
        refs[-1][...] = jnp.zeros_like(refs[-1])

    return pl.pallas_call(
        body, name=name,
        out_shape=(pltpu.SemaphoreType.DMA((4 * n,)), pltpu.SemaphoreType.DMA((4 * n,)),
                   *[_hbm(t) for t in shards], *[_hbm(t) for t in outs], jax.ShapeDtypeStruct((8, LANES), F32)),
        in_specs=[_IN_HBM] * (2 * n),
        out_specs=(_SEMS, _SEMS, *[_IN_HBM] * (2 * n), pl.BlockSpec(memory_space=pltpu.VMEM)),
        input_output_aliases={a: 2 + a for a in range(2 * n)},
        compiler_params=pltpu.CompilerParams(has_side_effects=_DATAFLOW),
    )(*[pltpu.with_memory_space_constraint(t, pltpu.HBM) for t in list(shards) + outs])


def _gather_wait(started, after, kinds, name):
    n = (len(started) - 3) // 2
    bufs = started[2:2 + 2 * n]

    def body(*refs):
        for cp in _gather_copies(refs[:n], refs[n:2 * n], refs[2 * n], refs[2 * n + 1], kinds):
            cp.wait_send()
            cp.wait_recv()

    outs = pl.pallas_call(
        body, name=name, out_shape=tuple(_hbm(t) for t in bufs),
        in_specs=[_IN_HBM] * (2 * n) + [_SEMS, _SEMS, ANY], out_specs=tuple([_IN_HBM] * (2 * n)),
        input_output_aliases={a: a for a in range(2 * n)},
        compiler_params=pltpu.CompilerParams(has_side_effects=_DATAFLOW),
    )(*bufs, started[0], started[1], after)
    return outs[n:]


def _gather_relay(bufs, kinds, name):
    n = len(bufs)

    def body(*refs):
        ins, outs, send_sems, recv_sems = refs[:n], refs[n:2 * n], refs[2 * n], refs[2 * n + 1]
        x, y, c, chips = _place()
        cps = []
        for a in range(n):
            h = (ins[a].shape[1] if kinds[a] == "rows" else ins[a].shape[0]) // 2
            for j, chip in enumerate(chips):
                kj = 2 * chip[0] + chip[1]
                cps.append((pltpu.make_async_remote_copy(
                    src_ref=_block(ins[a], kinds[a], kj, (c * h, h)), dst_ref=_block(outs[a], kinds[a], kj, (c * h, h)),
                    send_sem=send_sems.at[3 * a + j], recv_sem=recv_sems.at[3 * a + j], device_id=(x, y, 1 - c),
                    device_id_type=MESH), a, kj, h))
        for cp, _, _, _ in cps:
            cp.start()
        for a_cp, (cp, a, kj, h) in enumerate(cps):
            theirs = _block(outs[a], kinds[a], kj, ((1 - c) * h, h))
            pltpu.make_async_remote_copy(src_ref=theirs, dst_ref=theirs, send_sem=send_sems.at[a_cp],
                                         recv_sem=recv_sems.at[a_cp], device_id=(x, y, 1 - c), device_id_type=MESH).wait_recv()
        for cp, _, _, _ in cps:
            cp.wait_send()

    return pl.pallas_call(
        body, name=name, out_shape=tuple(jax.ShapeDtypeStruct(b.shape, b.dtype) for b in bufs),
        in_specs=[ANY] * n, out_specs=tuple([ANY] * n), input_output_aliases={a: a for a in range(n)},
        scratch_shapes=[pltpu.SemaphoreType.DMA((3 * n,)), pltpu.SemaphoreType.DMA((3 * n,))],
    )(*bufs)


def _exchange_sibling_halves(gs, kinds, name):
    n = len(gs)
    hshape = lambda g, kind: (4, g.shape[1] // 2, g.shape[2]) if kind == "rows" else (g.shape[0] // 2, g.shape[1])

    def body(*refs):
        ins, outs, send_sems, recv_sems = refs[:n], refs[n:2 * n], refs[2 * n], refs[2 * n + 1]
        x, y, c, _ = _place()
        cps = []
        for a in range(n):
            h = outs[a].shape[-2]
            src = ins[a].at[:, pl.ds((1 - c) * h, h), :] if kinds[a] == "rows" else ins[a].at[pl.ds((1 - c) * h, h), :]
            cps.append(pltpu.make_async_remote_copy(
                src_ref=src, dst_ref=outs[a], send_sem=send_sems.at[a],
                recv_sem=recv_sems.at[a], device_id=(x, y, 1 - c), device_id_type=MESH))
        for cp in cps:
            cp.start()
        for cp in cps:
            cp.wait()

    return pl.pallas_call(
        body, name=name,
        out_shape=tuple(jax.ShapeDtypeStruct(hshape(g, k), g.dtype) for g, k in zip(gs, kinds)),
        in_specs=[ANY] * n, out_specs=tuple([ANY] * n),
        scratch_shapes=[pltpu.SemaphoreType.DMA((n,)), pltpu.SemaphoreType.DMA((n,))],
    )(*gs)


def _add_halves(g, r1, cvec, kind, name):
    def body(c_ref, g_ref, r_ref, o_ref):
        o_ref[...] = (g_ref[...] + r_ref[...]).astype(o_ref.dtype)

    if kind == "rows":
        _, h, C = r1.shape
        tr = _row_tile(h)
        nt = h // tr
        grid = (4, nt)
        g_spec = pl.BlockSpec((1, tr, C), lambda k, i, c_ref: (k, c_ref[0] * nt + i, 0))
        r_spec = pl.BlockSpec((1, tr, C), lambda k, i, c_ref: (k, i, 0))
    else:
        h, C4 = r1.shape
        tr, tc = _row_tile(h), C4 // 4
        nt = h // tr
        grid = (nt, 4)
        g_spec = pl.BlockSpec((tr, tc), lambda i, k, c_ref: (c_ref[0] * nt + i, k))
        r_spec = pl.BlockSpec((tr, tc), lambda i, k, c_ref: (i, k))
    return pl.pallas_call(
        body, name=name, out_shape=jax.ShapeDtypeStruct(r1.shape, BF16),
        grid_spec=pltpu.PrefetchScalarGridSpec(num_scalar_prefetch=1, grid=grid, in_specs=[g_spec, r_spec],
                                               out_specs=r_spec),
        compiler_params=_cparams(("parallel", "parallel")),
    )(cvec, g, r1)


def _chip_copies(ins, lands, send_sems, recv_sems, kinds):
    x, y, c, chips = _place()
    return [pltpu.make_async_remote_copy(
        src_ref=_block(ins[a], kinds[a], 2 * chip[0] + chip[1]), dst_ref=lands[a].at[j],
        send_sem=send_sems.at[3 * a + j], recv_sem=recv_sems.at[3 * a + j], device_id=(*chip, c), device_id_type=MESH)
        for a in range(len(ins)) for j, chip in enumerate(chips)]


def _land_shape(s, kind):
    return (3,) + (s.shape[1:] if kind == "rows" else (s.shape[0], s.shape[1] // 4))


def _exchange_chips_start(ss, kinds, name):
    n = len(ss)
    lands = [lax.empty(_land_shape(s, k), s.dtype) for s, k in zip(ss, kinds)]

    def body(*refs):
        for cp in _chip_copies(refs[:n], refs[n:2 * n], refs[2 * n], refs[2 * n + 1], kinds):
            cp.start()
        refs[-1][...] = jnp.zeros_like(refs[-1])

    hbm = _hbm
    return pl.pallas_call(
        body, name=name,
        out_shape=(pltpu.SemaphoreType.DMA((3 * n,)), pltpu.SemaphoreType.DMA((3 * n,)),
                   *[hbm(t) for t in ss], *[hbm(t) for t in lands], jax.ShapeDtypeStruct((8, LANES), F32)),
        in_specs=[_IN_HBM] * (2 * n),
        out_specs=(_SEMS, _SEMS, *[_IN_HBM] * (2 * n), pl.BlockSpec(memory_space=pltpu.VMEM)),
        input_output_aliases={a: 2 + a for a in range(2 * n)},
        compiler_params=pltpu.CompilerParams(has_side_effects=_DATAFLOW),
    )(*[pltpu.with_memory_space_constraint(t, pltpu.HBM) for t in list(ss) + lands])


def _exchange_chips_wait(started, after, kinds, name):
    send_sems, recv_sems = started[0], started[1]
    n = (len(started) - 3) // 2
    bufs = started[2:2 + 2 * n]

    def body(*refs):
        for cp in _chip_copies(refs[:n], refs[n:2 * n], refs[2 * n], refs[2 * n + 1], kinds):
            cp.wait_send()
            cp.wait_recv()

    hbm = _hbm
    outs = pl.pallas_call(
        body, name=name, out_shape=tuple(hbm(t) for t in bufs),
        in_specs=[_IN_HBM] * (2 * n) + [_SEMS, _SEMS, ANY], out_specs=tuple([_IN_HBM] * (2 * n)),
        input_output_aliases={a: a for a in range(2 * n)},
        compiler_params=pltpu.CompilerParams(has_side_effects=_DATAFLOW),
    )(*bufs, send_sems, recv_sems, after)
    return outs[:n], outs[n:]


def _add_chips(s1, r2, kcvec, kind, name):
    _, h, C = r2.shape
    tr = _row_tile(h)
    nt = h // tr

    def body(kc_ref, s_ref, r0_ref, r1_ref, r2_ref, o_ref):
        s = s_ref[0] if kind == "rows" else s_ref[...]
        o_ref[...] = ((s.astype(F32) + r0_ref[0].astype(F32)) + r1_ref[0].astype(F32)) + r2_ref[0].astype(F32)

    peer = lambda j: pl.BlockSpec((1, tr, C), lambda i, kc_ref: (j, i, 0))
    if kind == "rows":
        s_spec = pl.BlockSpec((1, tr, C), lambda i, kc_ref: (kc_ref[0], i, 0))
    else:
        s_spec = pl.BlockSpec((tr, C), lambda i, kc_ref: (i, kc_ref[0]))
    return pl.pallas_call(
        body, name=name, out_shape=jax.ShapeDtypeStruct((2 * h, C), F32),
        grid_spec=pltpu.PrefetchScalarGridSpec(
            num_scalar_prefetch=1, grid=(nt,),
            in_specs=[s_spec, peer(0), peer(1), peer(2)],
            out_specs=pl.BlockSpec((tr, C), lambda i, kc_ref: (kc_ref[1] * nt + i, 0))),
        compiler_params=_cparams(("parallel",)),
    )(kcvec, s1, r2, r2, r2)


def _join_sibling_halves(bufs):
    n = len(bufs)

    def body(*refs):
        ins, outs, send_sems, recv_sems = refs[:n], refs[n:2 * n], refs[2 * n], refs[2 * n + 1]
        x, y, c, _ = _place()
        cps = []
        for a in range(n):
            h = ins[a].shape[0] // 2
            cps.append(pltpu.make_async_remote_copy(
                src_ref=ins[a].at[pl.ds(c * h, h), :], dst_ref=outs[a].at[pl.ds(c * h, h), :], send_sem=send_sems.at[a],
                recv_sem=recv_sems.at[a], device_id=(x, y, 1 - c), device_id_type=MESH))
        for cp in cps:
            cp.start()
        for a in range(n):
            h = ins[a].shape[0] // 2
            theirs = outs[a].at[pl.ds((1 - c) * h, h), :]
            pltpu.make_async_remote_copy(src_ref=theirs, dst_ref=theirs, send_sem=send_sems.at[a],
                                         recv_sem=recv_sems.at[a], device_id=(x, y, 1 - c), device_id_type=MESH).wait_recv()
        for cp in cps:
            cp.wait_send()

    return pl.pallas_call(
        body, name="grads_join",
        out_shape=tuple(jax.ShapeDtypeStruct(b.shape, b.dtype) for b in bufs),
        in_specs=[ANY] * n, out_specs=tuple([ANY] * n), input_output_aliases={a: a for a in range(n)},
        scratch_shapes=[pltpu.SemaphoreType.DMA((n,)), pltpu.SemaphoreType.DMA((n,))],
    )(*bufs)


N_DEV = 8


def _allreduce_small(pack):
    P = pack.shape[0]

    def body(p_ref, o_ref, gath, send_sems, recv_sems):
        x, y, c, _ = _place()
        me = 4 * x + 2 * y + c
        cps = []
        for mask in range(1, N_DEV):
            px = 1 - x if mask & 4 else x
            py = 1 - y if mask & 2 else y
            pc = 1 - c if mask & 1 else c
            cps.append((pltpu.make_async_remote_copy(
                src_ref=p_ref, dst_ref=gath.at[me], send_sem=send_sems.at[mask - 1], recv_sem=recv_sems.at[mask - 1],
                device_id=(px, py, pc), device_id_type=MESH), 4 * px + 2 * py + pc, mask))
        for cp, _, _ in cps:
            cp.start()
        gath[me] = p_ref[...]
        for _, peer, mask in cps:
            pltpu.make_async_remote_copy(
                src_ref=p_ref, dst_ref=gath.at[peer], send_sem=send_sems.at[mask - 1], recv_sem=recv_sems.at[mask - 1],
                device_id=(x, y, c), device_id_type=MESH).wait_recv()
        for cp, _, _ in cps:
            cp.wait_send()
        acc = gath[0]
        for i in range(1, N_DEV):
            acc = acc + gath[i]
        o_ref[...] = acc

    return pl.pallas_call(
        body, name="allreduce_small", out_shape=jax.ShapeDtypeStruct((P, LANES), F32),
        in_specs=[pl.BlockSpec(memory_space=pltpu.VMEM)], out_specs=pl.BlockSpec(memory_space=pltpu.VMEM),
        scratch_shapes=[pltpu.VMEM((N_DEV, P, LANES), F32), pltpu.SemaphoreType.DMA((N_DEV - 1,)),
                        pltpu.SemaphoreType.DMA((N_DEV - 1,))],
    )(pack)


def _pack_rows(arrs):
    rows = []
    for a in arrs:
        f = a.reshape(-1)
        f = jnp.pad(f, (0, (-f.shape[0]) % (8 * LANES)))
        rows.append(f.reshape(-1, LANES))
    return jnp.concatenate(rows, axis=0)


def _unpack_rows(pack, shapes):
    out, r = [], 0
    for s in shapes:
        n = math.prod(s)
        out.append(pack[r:r + -(-n // LANES)].reshape(-1)[:n].reshape(s))
        r += 8 * -(-n // (8 * LANES))
    return out


_SMALL = ["norm_mix_pre", "ml_head_norm", "b_gate_a", "b_gate_b", "norm_mix_post", "norm_ffn_pre", "norm_ffn_post",
          "conv_b", "b_ml_i", "b_ml_f", "b_fox_f"]
_BIG = ["w_in", "w_branch_a", "w_branch_b", "w_out", "w_up", "w_down"]
_WEIGHTS = ['norm_mix_pre', 'w_in', 'b_ml_i', 'b_ml_f', 'ml_head_norm', 'b_fox_f', 'b_gate_a', 'b_gate_b', 'w_branch_a',
            'w_branch_b', 'w_out', 'norm_mix_post', 'norm_ffn_pre', 'w_up', 'conv_w', 'conv_b', 'w_down', 'norm_ffn_post']


_KINDS = ["rows", "rows", "rows", "rows", "cols", "rows"]


def kernel(x, norm_mix_pre, w_in, b_ml_i, b_ml_f, ml_head_norm, b_fox_f, b_gate_a, b_gate_b, w_branch_a, w_branch_b, w_out, norm_mix_post, norm_ffn_pre, w_up, conv_w, conv_b, w_down, norm_ffn_post, loss_target, m_norm_mix_pre, m_w_in, m_b_ml_i, m_b_ml_f, m_ml_head_norm, m_b_fox_f, m_b_gate_a, m_b_gate_b, m_w_branch_a, m_w_branch_b, m_w_out, m_norm_mix_post, m_norm_ffn_pre, m_w_up, m_conv_w, m_conv_b, m_w_down, m_norm_ffn_post, v_norm_mix_pre, v_w_in, v_b_ml_i, v_b_ml_f, v_ml_head_norm, v_b_fox_f, v_b_gate_a, v_b_gate_b, v_w_branch_a, v_w_branch_b, v_w_out, v_norm_mix_post, v_norm_ffn_pre, v_w_up, v_conv_w, v_conv_b, v_w_down, v_norm_ffn_post):
    args = dict(locals())
    w = {n: args[n] for n in _WEIGHTS}
    mom = {n: args["m_" + n] for n in _WEIGHTS}
    var = {n: args["v_" + n] for n in _WEIGHTS}
    cx, cy, cc = lax.axis_index("x"), lax.axis_index("y"), lax.axis_index("c")
    kme = 2 * cx + cy
    cvec = jnp.reshape(cc, (1,)).astype(jnp.int32)
    kcvec = jnp.stack([kme, cc]).astype(jnp.int32)
    odd = kme % 2

    tr3 = lambda t: jnp.transpose(t, (0, 2, 1))
    w["w_in"], mom["w_in"], var["w_in"] = tr3(w_in), tr3(m_w_in), tr3(v_w_in)
    w_in_main = lax.dynamic_slice_in_dim(w["w_in"][0], 4 * odd, 2048, axis=0).astype(BF16)
    w_in_gates = lax.dynamic_slice_in_dim(w["w_in"][0], 2048 * (1 - odd), 4, axis=0).astype(BF16)
    (wmain_t,), (g_cw, g_gates) = _gather_weights([w_in_main], _KINDS[:1], [w["conv_w"][0], w_in_gates])
    rest_started = _gather_start([w[n][0].astype(BF16) for n in _BIG[1:]], _KINDS[1:], "gather_rest_start")

    def rest_weights(after):
        bufs = _gather_wait(rest_started, after, _KINDS[1:], "gather_rest_wait")
        g_a, g_b, g_out, wup, g_down = _gather_relay(bufs, _KINDS[1:], "gather_rest_relay")
        return full(g_a), full(g_b), full(g_out), wup, full(g_down)
    gate_rows = g_gates.reshape(16, D_MODEL)
    wsmall_t = jnp.zeros((N_SMALL, D_MODEL), BF16)
    for blk, (lo, hi) in enumerate(((0, 4), (4, 8), (8, 16))):
        wsmall_t = wsmall_t.at[blk * LANES:blk * LANES + hi - lo].set(gate_rows[lo:hi])
    full = lambda g: g.reshape(-1, g.shape[2])
    p = {n: w[n] for n in _SMALL}
    p["conv_w"] = jnp.transpose(g_cw, (1, 0, 2)).reshape(3, -1)

    groups = []

    def on_grads(group, gs):
        names = list(gs)
        kinds = [_KINDS[_BIG.index(n)] for n in names]
        whole = [g if k == "cols" else g.reshape(4, -1, g.shape[1]) for g, k in zip(gs.values(), kinds)]
        from_sibling = _exchange_sibling_halves(whole, kinds, "grads_to_sibling_" + group)
        sums = [_add_halves(g, r, cvec, k, "add_sibling_" + n) for g, r, k, n in zip(whole, from_sibling, kinds, names)]
        started = _exchange_chips_start(sums, kinds, "grads_to_chips_start_" + group)
        groups.append((group, names, kinds, started))
        return started[-1]

    loss_row, grad_x, big, small = _local_step(x[0], loss_target[0], full(wmain_t), wsmall_t, rest_weights, p, on_grads,
                                               rest_started[-1])
    grads = {}
    mine, mine_names = [], []
    for group, names, kinds, started in groups:
        sums, got = _exchange_chips_wait(started, grad_x, kinds, "grads_to_chips_wait_" + group)
        mine += [_add_chips(s, r, kcvec, k, "add_chips_" + n) for s, r, k, n in zip(sums, got, kinds, names)]
        mine_names += names
    grads.update(zip(mine_names, _join_sibling_halves(mine)))

    gt = big["wsmall_t"]
    small["w_in_gates"] = jnp.concatenate([gt[0:4], gt[LANES:LANES + 4], gt[2 * LANES:2 * LANES + 8]], axis=0)
    small_names = _SMALL + ["conv_w"]
    packed_names = small_names + ["w_in_gates"]
    pack = _pack_rows([small[n] for n in packed_names] + [loss_row])
    pack = jnp.pad(pack, ((0, (-pack.shape[0]) % 8), (0, 0)))
    full_shapes = [small[n].shape if n in ("conv_w", "w_in_gates") else w[n][0].shape for n in packed_names]
    total = _unpack_rows(_allreduce_small(pack), full_shapes + [loss_row.shape])
    for n, t in zip(packed_names, total):
        grads[n] = t
    loss = total[-1][0, 0]
    grads["conv_w"] = lax.dynamic_slice_in_dim(grads["conv_w"], kme * conv_w.shape[2], conv_w.shape[2], axis=1)
    my_gates = lax.dynamic_slice_in_dim(grads.pop("w_in_gates"), 4 * kme, 4, axis=0)
    g_in = jnp.zeros(w["w_in"].shape[1:], F32)
    g_in = lax.dynamic_update_slice_in_dim(g_in, grads["w_in"], 4 * odd, axis=0)
    grads["w_in"] = lax.dynamic_update_slice_in_dim(g_in, my_gates, 2048 * (1 - odd), axis=0)

    delta, new_m, new_v = {}, {}, {}
    for n in _BIG:
        delta[n], new_m[n], new_v[n] = _adamw(w[n], grads[n], mom[n], var[n], "adamw_" + n)
        grads[n] = grads[n][None]
    for d in (grads, delta, new_m, new_v):
        d["w_in"] = tr3(d["w_in"])
    packs = [_pack_rows([d[n][0] for n in small_names]) for d in (w, mom, var)]
    pad = ((0, (-packs[0].shape[0]) % 8), (0, 0))
    packs = [jnp.pad(t, pad)[None] for t in packs]
    gp = jnp.pad(_pack_rows([grads[n] for n in small_names]), pad)
    shapes = [w[n][0].shape for n in small_names]
    for dst, res in zip((delta, new_m, new_v), _adamw(packs[0], gp, packs[1], packs[2], "adamw_small")):
        for n, t in zip(small_names, _unpack_rows(res[0], shapes)):
            dst[n] = t[None]
    for n in small_names:
        grads[n] = grads[n][None]

    return (loss, grad_x[None], *[grads[n] for n in _WEIGHTS], *[delta[n] for n in _WEIGHTS],
            *[new_m[n] for n in _WEIGHTS], *[new_v[n] for n in _WEIGHTS])
```

```python
import functools
import math

import jax
import jax.numpy as jnp
from jax import lax
from jax.experimental import pallas as pl
from jax.experimental.pallas import tpu as pltpu

F32 = jnp.float32
BF16 = jnp.bfloat16
MESH = pl.DeviceIdType.MESH

D_MODEL = 1024
ML_HEADS = 4
ML_DQK = 128
ML_DV = 256
FOX_HEADS = 8
FOX_DH = 128
D_FF = 2816
GATE_CAP = 15.0
EPS = 1e-6
ADAM_LR, ADAM_B1, ADAM_B2, ADAM_EPS, ADAM_WD, ADAM_STEP = 0.001, 0.9, 0.999, 1e-08, 0.01, 10

LANES = 128
MLC = 128
FOX_TQ = 512
FOX_TQ_FWD = 512
FOX_TK = 512
FOX_TK_FWD = 512
ROW_T = 512
VMEM_LIMIT = 56 * 1024 * 1024

C_QM, C_KM, C_VM, C_OM = 0, 512, 1024, 2048
N_ML, N_FOX, N_GATE = 3072, 3072, 2048
N_SMALL = 384


def _cparams(sem=None):
    return pltpu.CompilerParams(dimension_semantics=sem, vmem_limit_bytes=VMEM_LIMIT)


def _tile(n, target):
    if n <= target:
        return n
    best = None
    for t in range(LANES, target + 1, LANES):
        if n % t == 0:
            best = t
    assert best is not None, (n, target)
    return best


def _dot(a, b, dims):
    return lax.dot_general(a, b, (dims, ((), ())), preferred_element_type=F32)


def _dot_nn(a, b):
    return _dot(a, b, ((1,), (0,)))


def _dot_nt(a, b):
    return _dot(a, b, ((1,), (1,)))


def _dot_tn(a, b):
    return _dot(a, b, ((0,), (0,)))


_DOTS = {"nn": _dot_nn, "nt": _dot_nt, "tn": _dot_tn}


def _mm(a, b, mode, out_dtype, name, tm=1024, tn=1408, tk=1408):
    a_parts = list(a) if isinstance(a, (list, tuple)) else [a]
    b_parts = list(b) if isinstance(b, (list, tuple)) else [b]
    assert len(a_parts) == 1 or len(b_parts) == 1, name
    a_axes = {"nn": "ik", "nt": "ik", "tn": "ki"}[mode]
    b_axes = {"nn": "kj", "nt": "jk", "tn": "kj"}[mode]
    size, target = {}, dict(i=tm, j=tn, k=tk)
    for parts, axes in ((a_parts, a_axes), (b_parts, b_axes)):
        dims = (parts[0].shape[0], parts[0].shape[1] * len(parts))
        for ax, n in zip(axes, dims):
            assert size.setdefault(ax, n) == n, (name, ax, n, size)
    tile = {}
    for parts, axes in ((a_parts, a_axes), (b_parts, b_axes)):
        if len(parts) > 1:
            tile[axes[1]] = _tile(parts[0].shape[1], target[axes[1]])
    for ax in "ijk":
        tile.setdefault(ax, _tile(size[ax], target[ax]))
    M, N, nk = size["i"], size["j"], size["k"] // tile["k"]
    grid_pos = dict(i=0, j=1, k=2)
    dot = _DOTS[mode]

    def specs(parts, axes):
        blk = (tile[axes[0]], tile[axes[1]])
        if len(parts) == 1:
            return [pl.BlockSpec(blk, lambda *g: (g[grid_pos[axes[0]]], g[grid_pos[axes[1]]]))], None
        bpp = parts[0].shape[1] // blk[1]

        def index(p):
            def f(*g):
                g0, g1 = g[grid_pos[axes[0]]], g[grid_pos[axes[1]]]
                on = g1 // bpp == p
                return jnp.where(on, g0, 0), jnp.where(on, g1 % bpp, 0)
            return f

        return [pl.BlockSpec(blk, index(p)) for p in range(len(parts))], (axes[1], bpp)

    a_specs, a_sel = specs(a_parts, a_axes)
    b_specs, b_sel = specs(b_parts, b_axes)
    na, nb = len(a_parts), len(b_parts)

    def body(*refs):
        a_refs, b_refs, o_ref, acc = refs[:na], refs[na:na + nb], refs[na + nb], refs[na + nb + 1:]

        def accumulate(part):
            if nk == 1:
                o_ref[...] = part.astype(o_ref.dtype)
                return
            acc_ref, = acc
            k = pl.program_id(2)

            @pl.when(k == 0)
            def _():
                acc_ref[...] = part

            @pl.when(k > 0)
            def _():
                acc_ref[...] += part

            @pl.when(k == nk - 1)
            def _():
                o_ref[...] = acc_ref[...].astype(o_ref.dtype)

        sel = a_sel or b_sel
        if sel is None:
            accumulate(dot(a_refs[0][...], b_refs[0][...]))
        else:
            which = pl.program_id(grid_pos[sel[0]]) // sel[1]
            for p in range(max(na, nb)):
                @pl.when(which == p)
                def _(p=p):
                    accumulate(dot(a_refs[p if a_sel else 0][...], b_refs[p if b_sel else 0][...]))

    return pl.pallas_call(
        body, name=name,
        out_shape=jax.ShapeDtypeStruct((M, N), out_dtype),
        grid=(M // tile["i"], N // tile["j"], nk),
        in_specs=a_specs + b_specs,
        out_specs=pl.BlockSpec((tile["i"], tile["j"]), lambda i, j, k: (i, j)),
        scratch_shapes=[pltpu.VMEM((tile["i"], tile["j"]), F32)] if nk > 1 else [],
        compiler_params=_cparams(("parallel", "parallel", "arbitrary")),
    )(*a_parts, *b_parts)


def _rstd(x):
    return lax.rsqrt(jnp.mean(x * x, axis=-1, keepdims=True) + EPS)


def _rmsnorm_fwd(x, g, name):
    S, D = x.shape
    T = _tile(S, ROW_T)

    def body(x_ref, g_ref, o_ref):
        xv = x_ref[...]
        o_ref[...] = (xv * _rstd(xv) * g_ref[...]).astype(o_ref.dtype)

    return pl.pallas_call(
        body, name=name, out_shape=jax.ShapeDtypeStruct((S, D), BF16), grid=(S // T,),
        in_specs=[pl.BlockSpec((T, D), lambda i: (i, 0)), pl.BlockSpec((1, D), lambda i: (0, 0))],
        out_specs=pl.BlockSpec((T, D), lambda i: (i, 0)),
        compiler_params=_cparams(("parallel",)),
    )(x, g)


def _resid_norm_fwd(x, z, g, name):
    S, D = x.shape
    T = _tile(S, ROW_T)

    def body(x_ref, z_ref, g_ref, o_ref):
        zv = z_ref[...]
        o_ref[...] = x_ref[...] + zv * _rstd(zv) * g_ref[...]

    row = pl.BlockSpec((T, D), lambda i: (i, 0))
    return pl.pallas_call(
        body, name=name, out_shape=jax.ShapeDtypeStruct((S, D), F32), grid=(S // T,),
        in_specs=[row, row, pl.BlockSpec((1, D), lambda i: (0, 0))],
        out_specs=row, compiler_params=_cparams(("parallel",)),
    )(x, z, g)


def _rmsnorm_bwd_math(dy, xv, g):
    r = _rstd(xv)
    u = dy * g
    dx = r * u - xv * (r * r * r) * jnp.mean(u * xv, axis=-1, keepdims=True)
    return dx, dy * xv * r


def _rmsnorm_bwd(dys, xin, g, resid, out_dtype, name):
    S, D = xin.shape
    T = _tile(S, ROW_T)
    has_resid = resid is not None
    ndy = len(dys)

    def body(*refs):
        dy_refs, (x_ref, g_ref) = refs[:ndy], refs[ndy:ndy + 2]
        dx_ref, dg_ref = refs[-2:]
        dy = dy_refs[0][...]
        for r in dy_refs[1:]:
            dy = dy + r[...]
        dx, dgt = _rmsnorm_bwd_math(dy, x_ref[...], g_ref[...])
        if has_resid:
            dx = dx + refs[ndy + 2][...]
        dx_ref[...] = dx.astype(dx_ref.dtype)

        @pl.when(pl.program_id(0) == 0)
        def _():
            dg_ref[...] = jnp.zeros_like(dg_ref)

        dg_ref[...] += jnp.sum(dgt, axis=0, keepdims=True)

    row = pl.BlockSpec((T, D), lambda i: (i, 0))
    vec = pl.BlockSpec((1, D), lambda i: (0, 0))
    ins = list(dys) + [xin, g] + ([resid] if has_resid else [])
    return pl.pallas_call(
        body, name=name,
        out_shape=(jax.ShapeDtypeStruct((S, D), out_dtype), jax.ShapeDtypeStruct((1, D), F32)),
        grid=(S // T,), in_specs=[row] * ndy + [row, vec] + ([row] if has_resid else []),
        out_specs=(row, vec), compiler_params=_cparams(("arbitrary",)),
    )(*ins)


def _loss_head(x1, d, g, target, name):
    S, D = x1.shape
    T = _tile(S, ROW_T)

    def body(x_ref, d_ref, g_ref, t_ref, loss_ref, dy_ref, dd_ref, dg_ref):
        dv, gv = d_ref[...], g_ref[...]
        y = x_ref[...] + dv * _rstd(dv) * gv
        diff = y - t_ref[...]
        dy = diff * (1.0 / D)
        dy_ref[...] = dy
        dd, dgt = _rmsnorm_bwd_math(dy, dv, gv)
        dd_ref[...] = dd.astype(dd_ref.dtype)

        @pl.when(pl.program_id(0) == 0)
        def _():
            dg_ref[...] = jnp.zeros_like(dg_ref)
            loss_ref[...] = jnp.zeros_like(loss_ref)

        dg_ref[...] += jnp.sum(dgt, axis=0, keepdims=True)
        part = jnp.sum(jnp.sum(diff * diff, axis=1, keepdims=True), axis=0, keepdims=True)
        loss_ref[...] += (0.5 / D) * part

    row = pl.BlockSpec((T, D), lambda i: (i, 0))
    vec = pl.BlockSpec((1, D), lambda i: (0, 0))
    return pl.pallas_call(
        body, name=name,
        out_shape=(jax.ShapeDtypeStruct((1, LANES), F32), jax.ShapeDtypeStruct((S, D), F32),
                   jax.ShapeDtypeStruct((S, D), BF16), jax.ShapeDtypeStruct((1, D), F32)),
        grid=(S // T,), in_specs=[row, row, vec, row],
        out_specs=(pl.BlockSpec((1, LANES), lambda i: (0, 0)), row, row, vec),
        compiler_params=_cparams(("arbitrary",)),
    )(x1, d, g, target)


def _merge_fwd(ya, yb, pm, ba, bb, name):
    S, D = ya.shape
    T = _tile(S, ROW_T)

    def body(ya_ref, yb_ref, ga_ref, gb_ref, ba_ref, bb_ref, o_ref):
        sa = jax.nn.sigmoid(ga_ref[...] + ba_ref[...])
        sb = jax.nn.sigmoid(gb_ref[...] + bb_ref[...])
        o_ref[...] = (sa * ya_ref[...] + sb * yb_ref[...]).astype(o_ref.dtype)

    row = pl.BlockSpec((T, D), lambda i: (i, 0))
    vec = pl.BlockSpec((1, D), lambda i: (0, 0))
    return pl.pallas_call(
        body, name=name, out_shape=jax.ShapeDtypeStruct((S, D), BF16), grid=(S // T,),
        in_specs=[row, row, pl.BlockSpec((T, D), lambda i: (i, 0)),
                  pl.BlockSpec((T, D), lambda i: (i, 1)), vec, vec],
        out_specs=row, compiler_params=_cparams(("parallel",)),
    )(ya, yb, pm, pm, ba, bb)


def _merge_bwd(dmerged, ya, yb, pm, ba, bb, name):
    S, D = ya.shape
    T = _tile(S, ROW_T)

    def body(dm_ref, ya_ref, yb_ref, ga_ref, gb_ref, ba_ref, bb_ref,
             dya_ref, dyb_ref, dga_ref, dgb_ref, dba_ref, dbb_ref):
        dm = dm_ref[...]
        sa = jax.nn.sigmoid(ga_ref[...] + ba_ref[...])
        sb = jax.nn.sigmoid(gb_ref[...] + bb_ref[...])
        dya_ref[...] = (dm * sa).astype(dya_ref.dtype)
        dyb_ref[...] = (dm * sb).astype(dyb_ref.dtype)
        dga = dm * ya_ref[...] * sa * (1.0 - sa)
        dgb = dm * yb_ref[...] * sb * (1.0 - sb)
        dga_ref[...] = dga.astype(dga_ref.dtype)
        dgb_ref[...] = dgb.astype(dgb_ref.dtype)

        @pl.when(pl.program_id(0) == 0)
        def _():
            dba_ref[...] = jnp.zeros_like(dba_ref)
            dbb_ref[...] = jnp.zeros_like(dbb_ref)

        dba_ref[...] += jnp.sum(dga, axis=0, keepdims=True)
        dbb_ref[...] += jnp.sum(dgb, axis=0, keepdims=True)

    row = pl.BlockSpec((T, D), lambda i: (i, 0))
    vec = pl.BlockSpec((1, D), lambda i: (0, 0))
    act = jax.ShapeDtypeStruct((S, D), BF16)
    v1 = jax.ShapeDtypeStruct((1, D), F32)
    return pl.pallas_call(
        body, name=name, out_shape=(act, act, act, act, v1, v1), grid=(S // T,),
        in_specs=[row, row, row, pl.BlockSpec((T, D), lambda i: (i, 0)),
                  pl.BlockSpec((T, D), lambda i: (i, 1)), vec, vec],
        out_specs=(row, row, row, row, vec, vec), compiler_params=_cparams(("arbitrary",)),
    )(dmerged, ya, yb, pm, pm, ba, bb)


_GELU_C = math.sqrt(2.0 / math.pi)


def _gelu(g):
    t = jnp.tanh(_GELU_C * (g + 0.044715 * g * g * g))
    return 0.5 * g * (1.0 + t), t


def _gelu_grad(g, t):
    return 0.5 * (1.0 + t) + 0.5 * g * (1.0 - t * t) * _GELU_C * (1.0 + 3 * 0.044715 * g * g)


def _shift_down(v, halo_ref, first, rows):
    T = v.shape[0]
    keep = jnp.where(first, 0.0, 1.0)
    h7 = halo_ref[7:8, :] * keep
    h6 = halo_ref[6:7, :] * keep
    m1 = jnp.where(rows == 0, h7, pltpu.roll(v, 1, 0))
    m2 = jnp.where(rows == 0, h6, jnp.where(rows == 1, h7, pltpu.roll(v, 2, 0)))
    return m1, m2


def _conv_act_fwd(up, cw, cb, name):
    S, F2 = up.shape
    Fh = F2 // 2
    T = _tile(S, ROW_T)
    tc = _tile(Fh, 256)
    ncol = Fh // tc
    hb = T // 8

    def body(ua_ref, ug_ref, ha_ref, hg_ref, wa_ref, wg_ref, ba_ref, bg_ref, o_ref):
        first = pl.program_id(0) == 0
        rows = lax.broadcasted_iota(jnp.int32, (T, tc), 0)

        def conv(u_ref, h_ref, w_ref, b_ref):
            v = u_ref[...]
            m1, m2 = _shift_down(v, h_ref, first, rows)
            return b_ref[...] + w_ref[0:1, :] * m2 + w_ref[1:2, :] * m1 + w_ref[2:3, :] * v

        a = conv(ua_ref, ha_ref, wa_ref, ba_ref)
        g = conv(ug_ref, hg_ref, wg_ref, bg_ref)
        o_ref[...] = (_gelu(g)[0] * a).astype(o_ref.dtype)

    halo = lambda off: pl.BlockSpec((8, tc), lambda i, j: (jnp.maximum(i * hb - 1, 0), j + off))
    return pl.pallas_call(
        body, name=name, out_shape=jax.ShapeDtypeStruct((S, Fh), BF16), grid=(S // T, ncol),
        in_specs=[pl.BlockSpec((T, tc), lambda i, j: (i, j)), pl.BlockSpec((T, tc), lambda i, j: (i, j + ncol)),
                  halo(0), halo(ncol),
                  pl.BlockSpec((3, tc), lambda i, j: (0, j)), pl.BlockSpec((3, tc), lambda i, j: (0, j + ncol)),
                  pl.BlockSpec((1, tc), lambda i, j: (0, j)), pl.BlockSpec((1, tc), lambda i, j: (0, j + ncol))],
        out_specs=pl.BlockSpec((T, tc), lambda i, j: (i, j)),
        compiler_params=_cparams(("parallel", "parallel")),
    )(up, up, up, up, cw, cw, cb, cb)


def _conv_act_bwd(up, dact, cw, cb, name):
    S, F2 = up.shape
    Fh = F2 // 2
    T = _tile(S, ROW_T)
    tc = _tile(Fh, 256)
    ncol, nrow, hb, nhb = Fh // tc, S // T, T // 8, S // 8

    def body(ua_ref, ug_ref, ha_ref, hg_ref, na_ref, ng_ref, wa_ref, wg_ref, ba_ref, bg_ref, da_ref, dn_ref,
             dpa_ref, dpg_ref, dwa_ref, dwg_ref, dba_ref, dbg_ref, dua_n, dug_n):
        i = pl.program_id(1)
        first = i == 0
        rows = lax.broadcasted_iota(jnp.int32, (T, tc), 0)
        rows8 = lax.broadcasted_iota(jnp.int32, (8, tc), 0)

        def conv(v, m1, m2, w_ref, b_ref):
            return b_ref[...] + w_ref[0:1, :] * m2 + w_ref[1:2, :] * m1 + w_ref[2:3, :] * v

        def du_of(a, g, dact_v):
            gel, t = _gelu(g)
            return dact_v * gel, dact_v * a * _gelu_grad(g, t)

        va, vg = ua_ref[...], ug_ref[...]
        a1, a2 = _shift_down(va, ha_ref, first, rows)
        g1, g2 = _shift_down(vg, hg_ref, first, rows)
        dua, dug = du_of(conv(va, a1, a2, wa_ref, ba_ref), conv(vg, g1, g2, wg_ref, bg_ref), da_ref[...])

        @pl.when(first)
        def _():
            for r in (dwa_ref, dwg_ref, dba_ref, dbg_ref):
                r[...] = jnp.zeros_like(r)

        for du, taps, dw_ref, db_ref in ((dua, (a2, a1, va), dwa_ref, dba_ref), (dug, (g2, g1, vg), dwg_ref, dbg_ref)):
            db_ref[...] += jnp.sum(du, axis=0, keepdims=True)
            for j in range(3):
                dw_ref[j:j + 1, :] += jnp.sum(du * taps[j], axis=0, keepdims=True)

        def below(n_ref, u_ref):
            v = n_ref[...]
            l1, l2 = u_ref[T - 1:T, :], u_ref[T - 2:T - 1, :]
            m1 = jnp.where(rows8 == 0, l1, pltpu.roll(v, 1, 0))
            m2 = jnp.where(rows8 == 0, l2, jnp.where(rows8 == 1, l1, pltpu.roll(v, 2, 0)))
            return v, m1, m2

        keep = jnp.where(i == nrow - 1, 0.0, 1.0)
        na, ng = below(na_ref, ua_ref), below(ng_ref, ug_ref)
        dna, dng = du_of(conv(*na, wa_ref, ba_ref), conv(*ng, wg_ref, bg_ref), dn_ref[...] * keep)
        dua_n[...] = dna
        dug_n[...] = dng

        for du, n_ref, w_ref, o_ref in ((dua, dua_n, wa_ref, dpa_ref), (dug, dug_n, wg_ref, dpg_ref)):
            n0, n1 = n_ref[0:1, :], n_ref[1:2, :]
            p1 = jnp.where(rows == T - 1, n0, pltpu.roll(du, T - 1, 0))
            p2 = jnp.where(rows == T - 2, n0, jnp.where(rows == T - 1, n1, pltpu.roll(du, T - 2, 0)))
            o_ref[...] = (w_ref[2:3, :] * du + w_ref[1:2, :] * p1 + w_ref[0:1, :] * p2).astype(o_ref.dtype)

    tile = lambda off: pl.BlockSpec((T, tc), lambda j, i: (i, j + off))
    above = lambda off: pl.BlockSpec((8, tc), lambda j, i: (jnp.maximum(i * hb - 1, 0), j + off))
    under = lambda off: pl.BlockSpec((8, tc), lambda j, i: (jnp.minimum((i + 1) * hb, nhb - 1), j + off))
    vec = lambda n, off: pl.BlockSpec((n, tc), lambda j, i: (0, j + off))
    act = jax.ShapeDtypeStruct((S, Fh), BF16)
    return pl.pallas_call(
        body, name=name,
        out_shape=(act, act, jax.ShapeDtypeStruct((3, Fh), F32), jax.ShapeDtypeStruct((3, Fh), F32),
                   jax.ShapeDtypeStruct((1, Fh), F32), jax.ShapeDtypeStruct((1, Fh), F32)),
        grid=(ncol, nrow),
        in_specs=[tile(0), tile(ncol), above(0), above(ncol), under(0), under(ncol),
                  vec(3, 0), vec(3, ncol), vec(1, 0), vec(1, ncol), tile(0), under(0)],
        out_specs=(tile(0), tile(0), vec(3, 0), vec(3, 0), vec(1, 0), vec(1, 0)),
        scratch_shapes=[pltpu.VMEM((8, tc), F32), pltpu.VMEM((8, tc), F32)],
        compiler_params=_cparams(("parallel", "arbitrary")),
    )(up, up, up, up, up, up, cw, cw, cb, cb, dact, dact)


def _split3(x):
    hi = x.astype(BF16)
    r1 = x - hi.astype(F32)
    mid = r1.astype(BF16)
    lo = (r1 - mid.astype(F32)).astype(BF16)
    return hi, mid, lo


def _tri_dot(tri, x):
    hi, mid, lo = _split3(x)
    return _dot_nn(tri, hi) + _dot_nn(tri, mid) + _dot_nn(tri, lo)


def _log_sigmoid(x):
    return jnp.minimum(x, 0.0) - jnp.log(1.0 + jnp.exp(-jnp.abs(x)))


def _tri_mask(n, lower):
    r = lax.broadcasted_iota(jnp.int32, (n, n), 0)
    c = lax.broadcasted_iota(jnp.int32, (n, n), 1)
    return (r >= c) if lower else (r <= c)


def _gates_fwd(ps, bi, bf, bff, name):
    S = ps.shape[0]
    NC = S // MLC

    def body(ps_ref, bi_ref, bf_ref, bff_ref, a_ref, A_ref, wi_ref, em_ref, wk_ref, dec_ref, F_ref, m_scr, f_scr):
        @pl.when(pl.program_id(0) == 0)
        def _():
            m_scr[...] = jnp.zeros_like(m_scr)
            f_scr[...] = jnp.zeros_like(f_scr)

        rows = lax.broadcasted_iota(jnp.int32, (MLC, LANES), 0)
        ltri = _tri_mask(MLC, True).astype(BF16)
        li = GATE_CAP * jnp.tanh((ps_ref[:, 0:LANES] + bi_ref[...]) / GATE_CAP)
        lf = _log_sigmoid(GATE_CAP * jnp.tanh((ps_ref[:, LANES:2 * LANES] + bf_ref[...]) / GATE_CAP))
        b = _tri_dot(ltri, lf)
        a = li - b
        cm = a
        sh = 1
        while sh < MLC:
            cm = jnp.where(rows >= sh, jnp.maximum(cm, pltpu.roll(cm, sh, 0)), cm)
            sh *= 2
        m0 = m_scr[...]
        A = jnp.maximum(cm, m0)
        a_ref[...] = a
        A_ref[...] = A
        A_last = A_ref[MLC - 1:MLC, :]
        wi_ref[...] = jnp.exp(m0 - A)
        em_ref[...] = jnp.exp(-(b + A))
        wk_ref[...] = jnp.exp(a - A_last)
        dec_ref[0] = jnp.exp(m0 - A_last)
        F_ref[...] = b
        m_scr[...] = F_ref[MLC - 1:MLC, :] + A_last
        lfg = _log_sigmoid(ps_ref[:, 2 * LANES:3 * LANES] + bff_ref[...])
        F_ref[...] = _tri_dot(ltri, lfg) + f_scr[...]
        f_scr[...] = F_ref[MLC - 1:MLC, :]

    col = pl.BlockSpec((MLC, LANES), lambda c: (c, 0))
    vec = pl.BlockSpec((1, LANES), lambda c: (0, 0))
    cs = jax.ShapeDtypeStruct((S, LANES), F32)
    return pl.pallas_call(
        body, name=name,
        out_shape=(cs, cs, cs, cs, cs, jax.ShapeDtypeStruct((NC, 1, LANES), F32), cs),
        grid=(NC,), in_specs=[pl.BlockSpec((MLC, N_SMALL), lambda c: (c, 0)), vec, vec, vec],
        out_specs=(col, col, col, col, col, pl.BlockSpec((1, 1, LANES), lambda c: (c, 0, 0)), col),
        scratch_shapes=[pltpu.VMEM((1, LANES), F32), pltpu.VMEM((1, LANES), F32)],
        compiler_params=_cparams(("arbitrary",)),
    )(ps, bi, bf, bff)


def _gates_bwd(ps, bi, bf, bff, rk, kc, tch, dF, name):
    S = ps.shape[0]
    NC = S // MLC

    def body(ps_ref, bi_ref, bf_ref, bff_ref, rk_ref, kc_ref, t_ref, dF_ref, dps_ref, db_ref, carry):
        @pl.when(pl.program_id(0) == 0)
        def _():
            carry[...] = jnp.zeros_like(carry)
            db_ref[...] = jnp.zeros_like(db_ref)

        lanes = lax.broadcasted_iota(jnp.int32, (MLC, LANES), 1)
        utri = _tri_mask(MLC, False).astype(BF16)
        ti = jnp.tanh((ps_ref[:, 0:LANES] + bi_ref[...]) / GATE_CAP)
        t_end, t_start = t_ref[0, 0:1, :], t_ref[0, 1:2, :]
        rk = rk_ref[...]
        rk = rk - (jnp.sum(rk, axis=0, keepdims=True) - (t_start - t_end)) * (1.0 / MLC)
        dpi = jnp.where(lanes < ML_HEADS, (kc_ref[...] - rk) * (1.0 - ti * ti), 0.0)
        tf = jnp.tanh((ps_ref[:, LANES:2 * LANES] + bf_ref[...]) / GATE_CAP)
        dlf = _tri_dot(utri, rk) + t_end
        dpf = jnp.where(lanes < ML_HEADS, dlf * jax.nn.sigmoid(-GATE_CAP * tf) * (1.0 - tf * tf), 0.0)
        dFv = dF_ref[...]
        dlfg = _tri_dot(utri, dFv) + carry[...]
        carry[...] += jnp.sum(dFv, axis=0, keepdims=True)
        dpff = jnp.where(lanes < FOX_HEADS, dlfg * jax.nn.sigmoid(-(ps_ref[:, 2 * LANES:3 * LANES] + bff_ref[...])), 0.0)
        for n, dp in enumerate((dpi, dpf, dpff)):
            dps_ref[:, n * LANES:(n + 1) * LANES] = dp.astype(dps_ref.dtype)
            db_ref[:, n * LANES:(n + 1) * LANES] += jnp.sum(dp, axis=0, keepdims=True)

    rev = lambda c: (NC - 1 - c, 0)
    col = pl.BlockSpec((MLC, LANES), rev)
    vec = pl.BlockSpec((1, LANES), lambda c: (0, 0))
    wide = pl.BlockSpec((MLC, N_SMALL), rev)
    return pl.pallas_call(
        body, name=name,
        out_shape=(jax.ShapeDtypeStruct((S, N_SMALL), BF16), jax.ShapeDtypeStruct((1, N_SMALL), F32)),
        grid=(NC,),
        in_specs=[wide, vec, vec, vec, col, col, pl.BlockSpec((1, 2, LANES), lambda c: (NC - 1 - c, 0, 0)), col],
        out_specs=(wide, pl.BlockSpec((1, N_SMALL), lambda c: (0, 0))),
        scratch_shapes=[pltpu.VMEM((1, LANES), F32)],
        compiler_params=_cparams(("arbitrary",)),
    )(ps, bi, bf, bff, rk, kc, tch, dF)


_ML_SCALE = ML_DQK ** -0.5


def _ml_specs(rev, NC):
    idx = (lambda c: NC - 1 - c) if rev else (lambda c: c)
    qk = lambda blk: pl.BlockSpec((MLC, ML_HEADS * ML_DQK), lambda c: (idx(c), blk))
    wide = lambda blk: pl.BlockSpec((MLC, D_MODEL), lambda c: (idx(c), blk))
    col = pl.BlockSpec((MLC, LANES), lambda c: (idx(c), 0))
    return idx, qk, wide, col


def _ml_intra(q_ref, k_ref, arow_ref, A_ref, h):
    hs = slice(h * ML_DQK, (h + 1) * ML_DQK)
    qf = q_ref[:, hs] * _ML_SCALE
    kf = k_ref[:, hs]
    qb, kb = qf.astype(BF16), kf.astype(BF16)
    qk = _dot_nt(qb, kb)
    logw = arow_ref[h:h + 1, :] - A_ref[:, h:h + 1]
    W = jnp.exp(jnp.where(_tri_mask(MLC, True), logw, -1e30))
    return qb, kb, qf, kf, qk, W


def _mlstm_fwd(pm, a_row, A, wi, em, wk, dec, w_hn, name):
    S = pm.shape[0]
    NC = S // MLC
    _, qk, wide, col = _ml_specs(False, NC)

    def body(q_ref, k_ref, v_ref, o_ref, arow_ref, A_ref, wi_ref, em_ref, wk_ref, dec_ref, whn_ref,
             ha_ref, hp_ref, den_ref, cst_ref, nst_ref, C_scr, n_scr):
        @pl.when(pl.program_id(0) == 0)
        def _():
            C_scr[...] = jnp.zeros_like(C_scr)
            n_scr[...] = jnp.zeros_like(n_scr)

        lanes = lax.broadcasted_iota(jnp.int32, (MLC, LANES), 1)
        den_tile = jnp.zeros((MLC, LANES), F32)
        for h in range(ML_HEADS):
            vs = slice(h * ML_DV, (h + 1) * ML_DV)
            qb, kb, qf, kf, qk_, W = _ml_intra(q_ref, k_ref, arow_ref, A_ref, h)
            vb = v_ref[:, vs].astype(BF16)
            Cf = C_scr[h]
            Cb = Cf.astype(BF16)
            nrow = n_scr[h]
            cst_ref[0, h] = Cb
            nst_ref[0, h] = nrow
            s = qk_ * W
            wic = wi_ref[:, h:h + 1]
            num = _dot_nn(s.astype(BF16), vb) + wic * _dot_nt(qb, Cb)
            den = jnp.sum(s, axis=1, keepdims=True) + wic * jnp.sum(qf * nrow, axis=1, keepdims=True)
            hp = num / jnp.maximum(jnp.abs(den), em_ref[:, h:h + 1])
            hp_ref[:, vs] = hp
            den_tile = jnp.where(lanes == h, den, den_tile)
            hn = hp * _rstd(hp) * whn_ref[:, vs]
            ha_ref[:, vs] = (hn * jax.nn.sigmoid(o_ref[:, vs])).astype(ha_ref.dtype)
            wkc = wk_ref[:, h:h + 1]
            kw = kf * wkc
            d = dec_ref[0, :, h:h + 1]
            C_scr[h] = d * Cf + _dot_tn(vb, kw.astype(BF16))
            n_scr[h] = d * nrow + jnp.sum(kw, axis=0, keepdims=True)
        den_ref[...] = den_tile

    return pl.pallas_call(
        body, name=name,
        out_shape=(jax.ShapeDtypeStruct((S, D_MODEL), BF16), jax.ShapeDtypeStruct((S, D_MODEL), F32),
                   jax.ShapeDtypeStruct((S, LANES), F32),
                   jax.ShapeDtypeStruct((NC, ML_HEADS, ML_DV, ML_DQK), BF16),
                   jax.ShapeDtypeStruct((NC, ML_HEADS, 1, ML_DQK), F32)),
        grid=(NC,),
        in_specs=[qk(C_QM // 512), qk(C_KM // 512), wide(C_VM // D_MODEL), wide(C_OM // D_MODEL),
                  pl.BlockSpec((8, MLC), lambda c: (0, c)), col, col, col, col,
                  pl.BlockSpec((1, 1, LANES), lambda c: (c, 0, 0)), pl.BlockSpec((1, D_MODEL), lambda c: (0, 0))],
        out_specs=(pl.BlockSpec((MLC, D_MODEL), lambda c: (c, 0)), pl.BlockSpec((MLC, D_MODEL), lambda c: (c, 0)),
                   col, pl.BlockSpec((1, ML_HEADS, ML_DV, ML_DQK), lambda c: (c, 0, 0, 0)),
                   pl.BlockSpec((1, ML_HEADS, 1, ML_DQK), lambda c: (c, 0, 0, 0))),
        scratch_shapes=[pltpu.VMEM((ML_HEADS, ML_DV, ML_DQK), F32), pltpu.VMEM((ML_HEADS, 1, ML_DQK), F32)],
        compiler_params=_cparams(("arbitrary",)),
    )(pm, pm, pm, pm, a_row, A, wi, em, wk, dec, w_hn)


def _mlstm_bwd(dha, pm, hp_all, den_all, a_row, A, wi, em, wk, dec, cst, nst, w_hn, name):
    S = pm.shape[0]
    NC = S // MLC
    idx, qk, wide, col = _ml_specs(True, NC)

    def body(dha_ref, q_ref, k_ref, v_ref, o_ref, hp_ref, den_ref, arow_ref, A_ref, wi_ref, em_ref, wk_ref,
             dec_ref, cst_ref, nst_ref, whn_ref,
             dqk_ref, dv_ref, do_ref, rk_ref, kc_ref, t_ref, dwhn_ref, dC_scr, dn_scr, t_scr):
        @pl.when(pl.program_id(0) == 0)
        def _():
            dC_scr[...] = jnp.zeros_like(dC_scr)
            dn_scr[...] = jnp.zeros_like(dn_scr)
            t_scr[...] = jnp.zeros_like(t_scr)
            dwhn_ref[...] = jnp.zeros_like(dwhn_ref)

        lanes = lax.broadcasted_iota(jnp.int32, (MLC, LANES), 1)
        lane1 = lax.broadcasted_iota(jnp.int32, (1, LANES), 1)
        t_ref[0, 0:1, :] = t_scr[...]
        rk_tile = jnp.zeros((MLC, LANES), F32)
        kc_tile = jnp.zeros((MLC, LANES), F32)
        t_new = jnp.zeros((1, LANES), F32)
        for h in range(ML_HEADS):
            hs = slice(h * ML_DQK, (h + 1) * ML_DQK)
            vs = slice(h * ML_DV, (h + 1) * ML_DV)
            hp = hp_ref[:, vs]
            sig = jax.nn.sigmoid(o_ref[:, vs])
            whn = whn_ref[:, vs]
            r = _rstd(hp)
            dga = dha_ref[:, vs]
            do_ref[:, vs] = (dga * (hp * r * whn) * sig * (1.0 - sig)).astype(do_ref.dtype)
            dhn = dga * sig
            dhp, dwt = _rmsnorm_bwd_math(dhn, hp, whn)
            dwhn_ref[:, vs] += jnp.sum(dwt, axis=0, keepdims=True)
            den = den_ref[:, h:h + 1]
            floor = em_ref[:, h:h + 1]
            D = jnp.maximum(jnp.abs(den), floor)
            dnum = dhp / D
            dh_h = jnp.sum(dhp * hp, axis=1, keepdims=True)
            active = jnp.abs(den) >= floor
            dden = -dh_h / D * jnp.where(active, jnp.sign(den), 0.0)
            phi = jnp.where(active, 0.0, dh_h)
            qb, kb, qf, kf, qk_, W = _ml_intra(q_ref, k_ref, arow_ref, A_ref, h)
            vf = v_ref[:, vs]
            vb = vf.astype(BF16)
            Cb = cst_ref[0, h]
            nrow = nst_ref[0, h]
            wic = wi_ref[:, h:h + 1]
            wkc = wk_ref[:, h:h + 1]
            d = dec_ref[0, :, h:h + 1]
            dCn = dC_scr[h]
            dCb = dCn.astype(BF16)
            dnn = dn_scr[h]
            dnumb = dnum.astype(BF16)
            s = qk_ * W
            ds = (_dot_nt(dnumb, vb) + dden) * W
            dsb = ds.astype(BF16)
            dnw = (wic * dnum).astype(BF16)
            wd = wic * dden
            kw = kf * wkc
            dv_state = _dot_nt(kw.astype(BF16), dCb)
            dq = _dot_nn(dsb, kb) + _dot_nn(dnw, Cb) + wd * nrow
            dk_state = wkc * (_dot_nn(vb, dCb) + dnn)
            dk = _dot_tn(dsb, qb) + dk_state
            dv = _dot_tn(s.astype(BF16), dnumb) + dv_state
            dC = d * dCn + _dot_tn(dnw, qb)
            dn = d * dnn + jnp.sum(wd * qf, axis=0, keepdims=True)
            dC_scr[h] = dC
            dn_scr[h] = dn
            dqk_ref[:, hs] = (dq * _ML_SCALE).astype(dqk_ref.dtype)
            dqk_ref[:, C_KM + h * ML_DQK:C_KM + (h + 1) * ML_DQK] = dk.astype(dqk_ref.dtype)
            dv_ref[:, vs] = dv.astype(dv_ref.dtype)
            G = ds * qk_
            inter = _dot_nt(qb, Cb)
            qn = jnp.sum(qf * nrow, axis=1, keepdims=True)
            R = (jnp.sum(G, axis=1, keepdims=True)
                 + wic * (jnp.sum(dnum * inter, axis=1, keepdims=True) + dden * qn))
            K = jnp.sum(G.T, axis=1, keepdims=True) + jnp.sum(kf * dk_state, axis=1, keepdims=True)
            rk_tile = jnp.where(lanes == h, R - K, rk_tile)
            kc_tile = jnp.where(lanes == h, phi, kc_tile)
            tt = (jnp.sum(jnp.sum(dC * Cb.astype(F32), axis=1, keepdims=True), axis=0, keepdims=True)
                  + jnp.sum(dn * nrow, axis=1, keepdims=True))
            t_new = jnp.where(lane1 == h, tt, t_new)
        rk_ref[...] = rk_tile
        kc_ref[...] = kc_tile
        t_ref[0, 1:2, :] = t_new
        t_scr[...] = t_new

    act = lambda n: jax.ShapeDtypeStruct((S, n), BF16)
    cs = jax.ShapeDtypeStruct((S, LANES), F32)
    rowblk = lambda n: pl.BlockSpec((MLC, n), lambda c: (idx(c), 0))
    return pl.pallas_call(
        body, name=name,
        out_shape=(act(D_MODEL), act(D_MODEL), act(D_MODEL), cs, cs,
                   jax.ShapeDtypeStruct((NC, 2, LANES), F32), jax.ShapeDtypeStruct((1, D_MODEL), F32)),
        grid=(NC,),
        in_specs=[rowblk(D_MODEL), qk(C_QM // 512), qk(C_KM // 512), wide(C_VM // D_MODEL), wide(C_OM // D_MODEL),
                  rowblk(D_MODEL), col, pl.BlockSpec((8, MLC), lambda c: (0, idx(c))), col, col, col, col,
                  pl.BlockSpec((1, 1, LANES), lambda c: (idx(c), 0, 0)),
                  pl.BlockSpec((1, ML_HEADS, ML_DV, ML_DQK), lambda c: (idx(c), 0, 0, 0)),
                  pl.BlockSpec((1, ML_HEADS, 1, ML_DQK), lambda c: (idx(c), 0, 0, 0)),
                  pl.BlockSpec((1, D_MODEL), lambda c: (0, 0))],
        out_specs=(rowblk(D_MODEL), rowblk(D_MODEL), rowblk(D_MODEL), col, col,
                   pl.BlockSpec((1, 2, LANES), lambda c: (idx(c), 0, 0)), pl.BlockSpec((1, D_MODEL), lambda c: (0, 0))),
        scratch_shapes=[pltpu.VMEM((ML_HEADS, ML_DV, ML_DQK), F32), pltpu.VMEM((ML_HEADS, 1, ML_DQK), F32),
                        pltpu.VMEM((1, LANES), F32)],
        compiler_params=_cparams(("arbitrary",)),
    )(dha, pm, pm, pm, pm, hp_all, den_all, a_row, A, wi, em, wk, dec, cst, nst, w_hn)


_FOX_SCALE = FOX_DH ** -0.5
_NEG = -1e30
_LOG2E = 1.4426950408889634
_LN2 = 0.6931471805599453
_QF_BLK, _KF_BLK, _VF_BLK = 0, FOX_HEADS, 2 * FOX_HEADS


def _lane_pick(tile, lane):
    lanes = lax.broadcasted_iota(jnp.int32, tile.shape, 1)
    return jnp.sum(jnp.where(lanes == lane, tile, 0.0), axis=1, keepdims=True)


def _col_to_row(col):
    return jnp.max(jnp.broadcast_to(col, (col.shape[0], LANES)).T, axis=0, keepdims=True)


def _causal(q0, k0, shape, q_axis):
    qpos = q0 + lax.broadcasted_iota(jnp.int32, shape, q_axis)
    kpos = k0 + lax.broadcasted_iota(jnp.int32, shape, 1 - q_axis)
    return kpos <= qpos


def _fox_fwd(pf, fc, fk_row, name):
    S = pf.shape[0]
    TQ, TK = FOX_TQ_FWD, FOX_TK_FWD
    nq, nk = S // TQ, S // TK
    c1 = _FOX_SCALE * _LOG2E

    def body(q_ref, k_ref, v_ref, fc_ref, fr_ref, o_ref, lse_ref):
        h, i = pl.program_id(0), pl.program_id(1)
        qb = q_ref[...]
        fq2 = _lane_pick(fc_ref[...], h) * _LOG2E

        def step(j, carry, masked):
            m, l, acc = carry
            off = pl.multiple_of(j * TK, TK)
            t = _dot_nt(qb, k_ref[pl.ds(off, TK), :]) * c1 - fr_ref[0, j] * _LOG2E
            if masked:
                t = jnp.where(_causal(i * TQ, j * TK, (TQ, TK), 0), t, _NEG)
            m_new = jnp.maximum(m, jnp.max(t, axis=1, keepdims=True) + fq2)
            alpha = jnp.exp2(m - m_new)
            p = jnp.exp2(t + (fq2 - m_new))
            l = alpha * l + jnp.sum(p, axis=1, keepdims=True)
            acc = alpha * acc + _dot_nn(p.astype(BF16), v_ref[pl.ds(off, TK), :])
            return m_new, l, acc

        init = (jnp.full((TQ, 1), _NEG, F32), jnp.zeros((TQ, 1), F32), jnp.zeros((TQ, FOX_DH), F32))
        last = (i * TQ) // TK
        carry = lax.fori_loop(0, last, lambda j, c: step(j, c, False), init)
        m, l, acc = step(last, carry, True)
        o_ref[...] = (acc / l).astype(o_ref.dtype)
        lse_ref[0, 0] = _col_to_row((m + jnp.log2(l)) * _LN2)

    head = lambda blk: pl.BlockSpec((S, FOX_DH), lambda h, i: (0, blk + h))
    return pl.pallas_call(
        body, name=name,
        out_shape=(jax.ShapeDtypeStruct((S, D_MODEL), BF16), jax.ShapeDtypeStruct((FOX_HEADS, nq, 1, TQ), F32)),
        grid=(FOX_HEADS, nq),
        in_specs=[pl.BlockSpec((TQ, FOX_DH), lambda h, i: (i, _QF_BLK + h)), head(_KF_BLK), head(_VF_BLK),
                  pl.BlockSpec((TQ, LANES), lambda h, i: (i, 0)),
                  pl.BlockSpec((1, nk, 1, TK), lambda h, i: (h, 0, 0, 0))],
        out_specs=(pl.BlockSpec((TQ, FOX_DH), lambda h, i: (i, h)),
                   pl.BlockSpec((1, 1, 1, TQ), lambda h, i: (h, i, 0, 0))),
        compiler_params=_cparams(("parallel", "arbitrary")),
    )(pf, pf, pf, fc, fk_row)


def _fox_bwd(dhb, hb, pf, lse_row, fq_row, fc, name):
    S = pf.shape[0]
    TQ, TK = FOX_TQ, FOX_TK
    nq, nk, r = S // TQ, S // TK, TK // TQ
    c1 = _FOX_SCALE * _LOG2E

    def body(q_ref, k_ref, v_ref, do_ref, o_ref, lse_ref, fq_ref, fc_ref,
             dq_ref, dk_ref, dv_ref, dFk_ref, dFq_ref, dq_acc, qside, delta, dk_acc, dv_acc, cs_acc):
        h, j = pl.program_id(0), pl.program_id(1)

        @pl.when(j == 0)
        def _():
            dq_acc[...] = jnp.zeros_like(dq_acc)
            dFq_ref[...] = jnp.zeros_like(dFq_ref)

            def fill(b, _):
                off = pl.multiple_of(b * TQ, TQ)
                prod = do_ref[pl.ds(off, TQ), :].astype(F32) * o_ref[pl.ds(off, TQ), :].astype(F32)
                delta[b] = jnp.sum(prod.T, axis=0, keepdims=True)
                qside[b] = (fq_ref[0, b] - lse_ref[0, b]) * _LOG2E
                return 0

            lax.fori_loop(0, nq, fill, 0)

        kb = k_ref[...]
        vb = v_ref[...]
        fk2 = _lane_pick(fc_ref[...], h) * _LOG2E
        dk_acc[...] = jnp.zeros_like(dk_acc)
        dv_acc[...] = jnp.zeros_like(dv_acc)
        cs_acc[...] = jnp.zeros_like(cs_acc)

        def step(i, masked):
            off = pl.multiple_of(i * TQ, TQ)
            qb = q_ref[pl.ds(off, TQ), :]
            dob = do_ref[pl.ds(off, TQ), :]
            t = _dot_nt(kb, qb) * c1 + qside[i] - fk2
            if masked:
                t = jnp.where(_causal(i * TQ, j * TK, (TK, TQ), 1), t, _NEG)
            p = jnp.exp2(t)
            dv_acc[...] += _dot_nn(p.astype(BF16), dob)
            ds = p * (_dot_nt(vb, dob) - delta[i])
            dsb = ds.astype(BF16)
            dk_acc[...] += _dot_nn(dsb, qb)
            dq_acc[pl.ds(off, TQ), :] += _dot_tn(dsb, kb)
            cs_acc[...] += jnp.sum(ds, axis=1, keepdims=True)
            dFq_ref[0, i] += jnp.sum(ds, axis=0, keepdims=True)

        for d in range(r):
            step(r * j + d, True)

        def rest(i, _):
            step(i, False)
            return 0

        lax.fori_loop(r * j + r, nq, rest, 0)
        dk_ref[...] = (dk_acc[...] * _FOX_SCALE).astype(dk_ref.dtype)
        dv_ref[...] = dv_acc[...].astype(dv_ref.dtype)
        dFk_ref[0, 0] = -_col_to_row(cs_acc[...])

        @pl.when(j == nk - 1)
        def _():
            dq_ref[...] = (dq_acc[...] * _FOX_SCALE).astype(dq_ref.dtype)

    head = lambda blk: pl.BlockSpec((S, FOX_DH), lambda h, j: (0, blk + h))
    kblk = lambda blk: pl.BlockSpec((TK, FOX_DH), lambda h, j: (j, blk + h))
    qrows = pl.BlockSpec((1, nq, 1, TQ), lambda h, j: (h, 0, 0, 0))
    act = jax.ShapeDtypeStruct((S, D_MODEL), BF16)
    return pl.pallas_call(
        body, name=name,
        out_shape=(act, act, act, jax.ShapeDtypeStruct((FOX_HEADS, nk, 1, TK), F32),
                   jax.ShapeDtypeStruct((FOX_HEADS, nq, 1, TQ), F32)),
        grid=(FOX_HEADS, nk),
        in_specs=[head(_QF_BLK), kblk(_KF_BLK), kblk(_VF_BLK), head(0), head(0), qrows, qrows,
                  pl.BlockSpec((TK, LANES), lambda h, j: (j, 0))],
        out_specs=(head(0), kblk(0), kblk(0), pl.BlockSpec((1, 1, 1, TK), lambda h, j: (h, j, 0, 0)), qrows),
        scratch_shapes=[pltpu.VMEM((S, FOX_DH), F32), pltpu.VMEM((nq, 1, TQ), F32), pltpu.VMEM((nq, 1, TQ), F32),
                        pltpu.VMEM((TK, FOX_DH), F32), pltpu.VMEM((TK, FOX_DH), F32), pltpu.VMEM((TK, 1), F32)],
        compiler_params=_cparams(("parallel", "arbitrary")),
    )(pf, pf, pf, dhb, hb, lse_row, fq_row, fc)


def _pad_lanes(v):
    return jnp.pad(v, ((0, 0), (0, LANES - v.shape[1])))


def _local_step(x, target, wmain_t, wsmall_t, rest_weights, p, on_grads, token):
    S = x.shape[0]
    bi, bf, bff = _pad_lanes(p["b_ml_i"]), _pad_lanes(p["b_ml_f"]), _pad_lanes(p["b_fox_f"])

    h0 = _rmsnorm_fwd(x, p["norm_mix_pre"] + token[0:1, 0:1], "norm_mix_pre")
    pm = _mm(h0, wmain_t[:N_ML], "nt", F32, "proj_mlstm")
    pf = _mm(h0, wmain_t[N_ML:N_ML + N_FOX], "nt", BF16, "proj_fox")
    pg = _mm(h0, wmain_t[N_ML + N_FOX:], "nt", F32, "proj_merge")
    ps = _mm(h0, wsmall_t, "nt", F32, "proj_gates")
    a, A, wi, em, wk, dec, Fc = _gates_fwd(ps, bi, bf, bff, "gates_fwd")
    a_row = a[:, :8].T
    ha, hp, den, cst, nst = _mlstm_fwd(pm, a_row, A, wi, em, wk, dec, p["ml_head_norm"], "mlstm_fwd")
    ft = Fc[:, :FOX_HEADS].T
    fq_row = ft.reshape(FOX_HEADS, S // FOX_TQ, 1, FOX_TQ)
    fk_row = ft.reshape(FOX_HEADS, S // FOX_TK, 1, FOX_TK)
    hb, lse_row = _fox_fwd(pf, Fc, ft.reshape(FOX_HEADS, S // FOX_TK_FWD, 1, FOX_TK_FWD), "fox_fwd")
    wa, wb, wout, wup, wdown = rest_weights(hb)
    ya = _mm(ha, wa, "nn", F32, "branch_a")
    yb = _mm(hb, wb, "nn", F32, "branch_b")
    merged = _merge_fwd(ya, yb, pg, p["b_gate_a"], p["b_gate_b"], "merge_fwd")
    z = _mm(merged, wout, "nn", F32, "out_proj")
    x1 = _resid_norm_fwd(x, z, p["norm_mix_post"], "resid_mix")
    h2 = _rmsnorm_fwd(x1, p["norm_ffn_pre"], "norm_ffn_pre")
    up = _mm(h2, wup, "nn", F32, "ffn_up")
    act = _conv_act_fwd(up, p["conv_w"], p["conv_b"], "conv_act_fwd")
    d = _mm(act, wdown, "nn", F32, "ffn_down")
    loss_row, dy, dd, g_norm_ffn_post = _loss_head(x1, d, p["norm_ffn_post"], target, "loss_head")
    dact = _mm(dd, wdown, "nt", F32, "d_act")
    g_wdown = _mm(act, dd, "tn", F32, "dw_down", tm=1408)
    dupa, dupg, dcwa, dcwg, dcba, dcbg = _conv_act_bwd(up, dact, p["conv_w"], p["conv_b"], "conv_act_bwd")
    g_conv_w = jnp.concatenate([dcwa, dcwg], axis=1)
    g_conv_b = jnp.concatenate([dcba, dcbg], axis=1)
    dh2 = _mm([dupa, dupg], wup, "nt", F32, "d_h2")
    g_wup = _mm(h2, [dupa, dupg], "tn", F32, "dw_up")
    token = on_grads("ffn", dict(w_up=g_wup, w_down=g_wdown))
    dx1, g_norm_ffn_pre = _rmsnorm_bwd([dh2], x1, p["norm_ffn_pre"] + token[0:1, 0:1], dy, F32, "norm_ffn_pre_bwd")
    dz, g_norm_mix_post = _rmsnorm_bwd([dx1], z, p["norm_mix_post"], None, BF16, "norm_mix_post_bwd")
    dmerged = _mm(dz, wout, "nt", F32, "d_merged")
    g_wout = _mm(merged, dz, "tn", F32, "dw_out")
    dya, dyb, dga, dgb, g_b_gate_a, g_b_gate_b = _merge_bwd(dmerged, ya, yb, pg, p["b_gate_a"], p["b_gate_b"], "merge_bwd")
    dha = _mm(dya, wa, "nt", F32, "d_ha")
    g_wa = _mm(ha, dya, "tn", F32, "dw_a")
    dhb = _mm(dyb, wb, "nt", BF16, "d_hb")
    g_wb = _mm(hb, dyb, "tn", F32, "dw_b")
    token = on_grads("mix", dict(w_out=g_wout, w_branch_a=g_wa, w_branch_b=g_wb))
    dqkm, dvm, dom, rk, kc, tch, g_ml_head_norm = _mlstm_bwd(
        dha, pm, hp, den, a_row, A, wi, em, wk, dec, cst, nst, p["ml_head_norm"] + token[0:1, 0:1], "mlstm_bwd")
    dqf, dkf, dvf, dFk, dFq = _fox_bwd(dhb, hb, pf, lse_row.reshape(fq_row.shape), fq_row, Fc, "fox_bwd")
    dF = jnp.pad((dFk.reshape(FOX_HEADS, S) + dFq.reshape(FOX_HEADS, S)).T, ((0, 0), (0, LANES - FOX_HEADS)))
    dps, dbias = _gates_bwd(ps, bi, bf, bff, rk, kc, tch, dF, "gates_bwd")
    dpm = [dqkm, dvm, dom, dqf, dkf, dvf, dga, dgb]
    g_wmain_t = _mm(dpm, h0, "tn", F32, "dw_main")
    token = on_grads("in", dict(w_in=g_wmain_t))
    g_wsmall_t = _mm(dps, h0, "tn", F32, "dw_gates")
    dh0s = _mm(dps, wsmall_t + token[0:1, 0:1].astype(BF16), "nn", F32, "d_h0_gates")
    dpm = list(lax.optimization_barrier((token, *dpm))[1:])
    dh0 = _mm(dpm, wmain_t, "nn", F32, "d_h0_main")
    grad_x, g_norm_mix_pre = _rmsnorm_bwd([dh0, dh0s], x, p["norm_mix_pre"], dx1, F32, "norm_mix_pre_bwd")

    big = dict(wsmall_t=g_wsmall_t)
    small = dict(norm_mix_pre=g_norm_mix_pre, ml_head_norm=g_ml_head_norm, b_gate_a=g_b_gate_a, b_gate_b=g_b_gate_b,
                 norm_mix_post=g_norm_mix_post, norm_ffn_pre=g_norm_ffn_pre, norm_ffn_post=g_norm_ffn_post,
                 conv_b=g_conv_b, b_ml_i=dbias[:, 0:ML_HEADS], b_ml_f=dbias[:, LANES:LANES + ML_HEADS],
                 b_fox_f=dbias[:, 2 * LANES:2 * LANES + FOX_HEADS], conv_w=g_conv_w)
    return loss_row, grad_x, big, small


def _row_tile(r, target=256):
    best = None
    for t in range(8, min(r, target) + 1, 8):
        if r % t == 0:
            best = t
    return best if best is not None else r


def _adamw(w, g, m, v, name):
    _, R, C = w.shape
    tr = _row_tile(R)
    tc = C
    if tr == R and R > 256:
        tc = 256

    def body(w_ref, g_ref, m_ref, v_ref, d_ref, mo_ref, vo_ref):
        gv = g_ref[...]
        mn = ADAM_B1 * m_ref[0] + (1.0 - ADAM_B1) * gv
        vn = ADAM_B2 * v_ref[0] + (1.0 - ADAM_B2) * (gv * gv)
        m_hat = mn / (1.0 - ADAM_B1 ** ADAM_STEP)
        v_hat = vn / (1.0 - ADAM_B2 ** ADAM_STEP)
        d_ref[0] = -ADAM_LR * (m_hat / (jnp.sqrt(v_hat) + ADAM_EPS) + ADAM_WD * w_ref[0])
        mo_ref[0] = mn
        vo_ref[0] = vn

    blk = pl.BlockSpec((1, tr, tc), lambda i, j: (0, i, j))
    o = jax.ShapeDtypeStruct((1, R, C), F32)
    return pl.pallas_call(
        body, name=name, out_shape=(o, o, o), grid=(R // tr, C // tc),
        in_specs=[blk, pl.BlockSpec((tr, tc), lambda i, j: (i, j)), blk, blk], out_specs=(blk,) * 3,
        compiler_params=_cparams(("parallel", "parallel")),
    )(w, g, m, v)


ANY = pl.BlockSpec(memory_space=pl.ANY)


def _place():
    x, y, c = lax.axis_index("x"), lax.axis_index("y"), lax.axis_index("c")
    chips = [(1 - x, y), (x, 1 - y), (1 - x, 1 - y)]
    return x, y, c, chips


def _block(ref, kind, k, rows=None):
    if kind == "rows":
        return ref.at[k] if rows is None else ref.at[k, pl.ds(*rows), :]
    cb = ref.shape[1] // 4
    return ref.at[:, pl.ds(k * cb, cb)] if rows is None else ref.at[pl.ds(*rows), pl.ds(k * cb, cb)]


def _gathered_shape(s, kind):
    return (4,) + s.shape if kind == "rows" else (s.shape[0], 4 * s.shape[1])


def _gather_weights(shards, kinds, smalls):
    n, ns = len(shards), len(smalls)

    def body(*refs):
        ins, sm_in = refs[:n], refs[n:n + ns]
        outs, sm_out = refs[n + ns:2 * n + ns], refs[2 * n + ns:2 * (n + ns)]
        send_sems, recv_sems, sm_send, sm_recv, local_sems = refs[2 * (n + ns):]
        x, y, c, chips = _place()
        sibling = (x, y, 1 - c)
        kme = 2 * x + y

        def half(a, k, hc):
            h = ins[a].shape[0] // 2
            return _block(outs[a], kinds[a], k, (hc * h, h))

        def remote(a, slot, src, dst, to):
            return pltpu.make_async_remote_copy(src_ref=src, dst_ref=dst, send_sem=send_sems.at[a * 7 + slot],
                                                recv_sem=recv_sems.at[a * 7 + slot], device_id=to, device_id_type=MESH)

        def sm_copy(b, j, k, to):
            return pltpu.make_async_remote_copy(src_ref=sm_in[b], dst_ref=sm_out[b].at[k], send_sem=sm_send.at[3 * b + j],
                                                recv_sem=sm_recv.at[3 * b + j], device_id=to, device_id_type=MESH)

        local = [pltpu.make_async_copy(sm_in[b], sm_out[b].at[kme], local_sems.at[b]) for b in range(ns)]
        for cp in local:
            cp.start()
        sends = [remote(a, 6, ins[a], _block(outs[a], kinds[a], kme), sibling) for a in range(n)]
        for a in range(n):
            h = ins[a].shape[0] // 2
            for j, chip in enumerate(chips):
                sends.append(remote(a, j, ins[a].at[pl.ds(c * h, h), :], half(a, kme, c), (*chip, c)))
        for b in range(ns):
            for j, chip in enumerate(chips):
                sends.append(sm_copy(b, j, kme, (*chip, c)))
        for cp in sends:
            cp.start()
        for a in range(n):
            for j, chip in enumerate(chips):
                kj = 2 * chip[0] + chip[1]
                remote(a, j, half(a, kj, c), half(a, kj, c), (*chip, c)).wait_recv()
                fwd = remote(a, 3 + j, half(a, kj, c), half(a, kj, c), sibling)
                fwd.start()
                sends.append(fwd)
        for a in range(n):
            for j, chip in enumerate(chips):
                kj = 2 * chip[0] + chip[1]
                remote(a, 3 + j, half(a, kj, 1 - c), half(a, kj, 1 - c), sibling).wait_recv()
        for b in range(ns):
            for j, chip in enumerate(chips):
                sm_copy(b, j, 2 * chip[0] + chip[1], (*chip, c)).wait_recv()
        for a in range(n):
            remote(a, 6, ins[a], _block(outs[a], kinds[a], kme), sibling).wait_recv()
        for cp in sends:
            cp.wait_send()
        for cp in local:
            cp.wait()

    outs = pl.pallas_call(
        body, name="gather_weights",
        out_shape=tuple([jax.ShapeDtypeStruct(_gathered_shape(s, k), s.dtype) for s, k in zip(shards, kinds)]
                        + [jax.ShapeDtypeStruct((4,) + s.shape, s.dtype) for s in smalls]),
        in_specs=[ANY] * (n + ns), out_specs=tuple([ANY] * (n + ns)),
        scratch_shapes=[pltpu.SemaphoreType.DMA((7 * n,)), pltpu.SemaphoreType.DMA((7 * n,)),
                        pltpu.SemaphoreType.DMA((3 * ns,)), pltpu.SemaphoreType.DMA((3 * ns,)),
                        pltpu.SemaphoreType.DMA((ns,))],
    )(*shards, *smalls)
    return outs[:n], outs[n:]


_IN_HBM = pl.BlockSpec(memory_space=pltpu.HBM)
_SEMS = pl.BlockSpec(memory_space=pltpu.SEMAPHORE)
_DATAFLOW = pltpu.SideEffectType.DATAFLOW_SIDE_EFFECTING


def _hbm(t):
    return pltpu.HBM(t.shape, t.dtype)


def _gather_copies(ins, outs, send_sems, recv_sems, kinds):
    x, y, c, chips = _place()
    kme = 2 * x + y
    cps = []
    for a in range(len(ins)):
        h = ins[a].shape[0] // 2
        for j, chip in enumerate(chips + [None]):
            to = (x, y, 1 - c) if chip is None else (*chip, c)
            src = ins[a] if chip is None else ins[a].at[pl.ds(c * h, h), :]
            dst = _block(outs[a], kinds[a], kme, None if chip is None else (c * h, h))
            cps.append(pltpu.make_async_remote_copy(src_ref=src, dst_ref=dst, send_sem=send_sems.at[4 * a + j],
                                                    recv_sem=recv_sems.at[4 * a + j], device_id=to, device_id_type=MESH))
    return cps


def _gather_start(shards, kinds, name):
    n = len(shards)
    outs = [lax.empty(_gathered_shape(s, k), s.dtype) for s, k in zip(shards, kinds)]

    def body(*refs):
        for cp in _gather_copies(refs[:n], refs[n:2 * n], refs[2 * n], refs[2 * n + 1], kinds):
            cp.start()
        refs[-1][...] = jnp.zeros_like(refs[-1])

    return pl.pallas_call(
        body, name=name,
        out_shape=(pltpu.SemaphoreType.DMA((4 * n,)), pltpu.SemaphoreType.DMA((4 * n,)),
                   *[_hbm(t) for t in shards], *[_hbm(t) for t in outs], jax.ShapeDtypeStruct((8, LANES), F32)),
        in_specs=[_IN_HBM] * (2 * n),
        out_specs=(_SEMS, _SEMS, *[_IN_HBM] * (2 * n), pl.BlockSpec(memory_space=pltpu.VMEM)),
        input_output_aliases={a: 2 + a for a in range(2 * n)},
        compiler_params=pltpu.CompilerParams(has_side_effects=_DATAFLOW),
    )(*[pltpu.with_memory_space_constraint(t, pltpu.HBM) for t in list(shards) + outs])


def _gather_wait(started, after, kinds, name):
    n = (len(started) - 3) // 2
    bufs = started[2:2 + 2 * n]

    def body(*refs):
        for cp in _gather_copies(refs[:n], refs[n:2 * n], refs[2 * n], refs[2 * n + 1], kinds):
            cp.wait_send()
            cp.wait_recv()

    outs = pl.pallas_call(
        body, name=name, out_shape=tuple(_hbm(t) for t in bufs),
        in_specs=[_IN_HBM] * (2 * n) + [_SEMS, _SEMS, ANY], out_specs=tuple([_IN_HBM] * (2 * n)),
        input_output_aliases={a: a for a in range(2 * n)},
        compiler_params=pltpu.CompilerParams(has_side_effects=_DATAFLOW),
    )(*bufs, started[0], started[1], after)
    return outs[n:]


def _gather_relay(bufs, kinds, name):
    n = len(bufs)

    def body(*refs):
        ins, outs, send_sems, recv_sems = refs[:n], refs[n:2 * n], refs[2 * n], refs[2 * n + 1]
        x, y, c, chips = _place()
        cps = []
        for a in range(n):
            h = (ins[a].shape[1] if kinds[a] == "rows" else ins[a].shape[0]) // 2
            for j, chip in enumerate(chips):
                kj = 2 * chip[0] + chip[1]
                cps.append((pltpu.make_async_remote_copy(
                    src_ref=_block(ins[a], kinds[a], kj, (c * h, h)), dst_ref=_block(outs[a], kinds[a], kj, (c * h, h)),
                    send_sem=send_sems.at[3 * a + j], recv_sem=recv_sems.at[3 * a + j], device_id=(x, y, 1 - c),
                    device_id_type=MESH), a, kj, h))
        for cp, _, _, _ in cps:
            cp.start()
        for a_cp, (cp, a, kj, h) in enumerate(cps):
            theirs = _block(outs[a], kinds[a], kj, ((1 - c) * h, h))
            pltpu.make_async_remote_copy(src_ref=theirs, dst_ref=theirs, send_sem=send_sems.at[a_cp],
                                         recv_sem=recv_sems.at[a_cp], device_id=(x, y, 1 - c), device_id_type=MESH).wait_recv()
        for cp, _, _, _ in cps:
            cp.wait_send()

    return pl.pallas_call(
        body, name=name, out_shape=tuple(jax.ShapeDtypeStruct(b.shape, b.dtype) for b in bufs),
        in_specs=[ANY] * n, out_specs=tuple([ANY] * n), input_output_aliases={a: a for a in range(n)},
        scratch_shapes=[pltpu.SemaphoreType.DMA((3 * n,)), pltpu.SemaphoreType.DMA((3 * n,))],
    )(*bufs)


def _exchange_sibling_halves(gs, kinds, name):
    n = len(gs)
    hshape = lambda g, kind: (4, g.shape[1] // 2, g.shape[2]) if kind == "rows" else (g.shape[0] // 2, g.shape[1])

    def body(*refs):
        ins, outs, send_sems, recv_sems = refs[:n], refs[n:2 * n], refs[2 * n], refs[2 * n + 1]
        x, y, c, _ = _place()
        cps = []
        for a in range(n):
            h = outs[a].shape[-2]
            src = ins[a].at[:, pl.ds((1 - c) * h, h), :] if kinds[a] == "rows" else ins[a].at[pl.ds((1 - c) * h, h), :]
            cps.append(pltpu.make_async_remote_copy(
                src_ref=src, dst_ref=outs[a], send_sem=send_sems.at[a],
                recv_sem=recv_sems.at[a], device_id=(x, y, 1 - c), device_id_type=MESH))
        for cp in cps:
            cp.start()
        for cp in cps:
            cp.wait()

    return pl.pallas_call(
        body, name=name,
        out_shape=tuple(jax.ShapeDtypeStruct(hshape(g, k), g.dtype) for g, k in zip(gs, kinds)),
        in_specs=[ANY] * n, out_specs=tuple([ANY] * n),
        scratch_shapes=[pltpu.SemaphoreType.DMA((n,)), pltpu.SemaphoreType.DMA((n,))],
    )(*gs)


def _add_halves(g, r1, cvec, kind, name):
    def body(c_ref, g_ref, r_ref, o_ref):
        o_ref[...] = (g_ref[...] + r_ref[...]).astype(o_ref.dtype)

    if kind == "rows":
        _, h, C = r1.shape
        tr = _row_tile(h)
        nt = h // tr
        grid = (4, nt)
        g_spec = pl.BlockSpec((1, tr, C), lambda k, i, c_ref: (k, c_ref[0] * nt + i, 0))
        r_spec = pl.BlockSpec((1, tr, C), lambda k, i, c_ref: (k, i, 0))
    else:
        h, C4 = r1.shape
        tr, tc = _row_tile(h), C4 // 4
        nt = h // tr
        grid = (nt, 4)
        g_spec = pl.BlockSpec((tr, tc), lambda i, k, c_ref: (c_ref[0] * nt + i, k))
        r_spec = pl.BlockSpec((tr, tc), lambda i, k, c_ref: (i, k))
    return pl.pallas_call(
        body, name=name, out_shape=jax.ShapeDtypeStruct(r1.shape, BF16),
        grid_spec=pltpu.PrefetchScalarGridSpec(num_scalar_prefetch=1, grid=grid, in_specs=[g_spec, r_spec],
                                               out_specs=r_spec),
        compiler_params=_cparams(("parallel", "parallel")),
    )(cvec, g, r1)


def _chip_copies(ins, lands, send_sems, recv_sems, kinds):
    x, y, c, chips = _place()
    return [pltpu.make_async_remote_copy(
        src_ref=_block(ins[a], kinds[a], 2 * chip[0] + chip[1]), dst_ref=lands[a].at[j],
        send_sem=send_sems.at[3 * a + j], recv_sem=recv_sems.at[3 * a + j], device_id=(*chip, c), device_id_type=MESH)
        for a in range(len(ins)) for j, chip in enumerate(chips)]


def _land_shape(s, kind):
    return (3,) + (s.shape[1:] if kind == "rows" else (s.shape[0], s.shape[1] // 4))


def _exchange_chips_start(ss, kinds, name):
    n = len(ss)
    lands = [lax.empty(_land_shape(s, k), s.dtype) for s, k in zip(ss, kinds)]

    def body(*refs):
        for cp in _chip_copies(refs[:n], refs[n:2 * n], refs[2 * n], refs[2 * n + 1], kinds):
            cp.start()
        refs[-1][...] = jnp.zeros_like(refs[-1])

    hbm = _hbm
    return pl.pallas_call(
        body, name=name,
        out_shape=(pltpu.SemaphoreType.DMA((3 * n,)), pltpu.SemaphoreType.DMA((3 * n,)),
                   *[hbm(t) for t in ss], *[hbm(t) for t in lands], jax.ShapeDtypeStruct((8, LANES), F32)),
        in_specs=[_IN_HBM] * (2 * n),
        out_specs=(_SEMS, _SEMS, *[_IN_HBM] * (2 * n), pl.BlockSpec(memory_space=pltpu.VMEM)),
        input_output_aliases={a: 2 + a for a in range(2 * n)},
        compiler_params=pltpu.CompilerParams(has_side_effects=_DATAFLOW),
    )(*[pltpu.with_memory_space_constraint(t, pltpu.HBM) for t in list(ss) + lands])


def _exchange_chips_wait(started, after, kinds, name):
    send_sems, recv_sems = started[0], started[1]
    n = (len(started) - 3) // 2
    bufs = started[2:2 + 2 * n]

    def body(*refs):
        for cp in _chip_copies(refs[:n], refs[n:2 * n], refs[2 * n], refs[2 * n + 1], kinds):
            cp.wait_send()
            cp.wait_recv()

    hbm = _hbm
    outs = pl.pallas_call(
        body, name=name, out_shape=tuple(hbm(t) for t in bufs),
        in_specs=[_IN_HBM] * (2 * n) + [_SEMS, _SEMS, ANY], out_specs=tuple([_IN_HBM] * (2 * n)),
        input_output_aliases={a: a for a in range(2 * n)},
        compiler_params=pltpu.CompilerParams(has_side_effects=_DATAFLOW),
    )(*bufs, send_sems, recv_sems, after)
    return outs[:n], outs[n:]


def _add_chips(s1, r2, kcvec, kind, name):
    _, h, C = r2.shape
    tr = _row_tile(h)
    nt = h // tr

    def body(kc_ref, s_ref, r0_ref, r1_ref, r2_ref, o_ref):
        s = s_ref[0] if kind == "rows" else s_ref[...]
        o_ref[...] = ((s.astype(F32) + r0_ref[0].astype(F32)) + r1_ref[0].astype(F32)) + r2_ref[0].astype(F32)

    peer = lambda j: pl.BlockSpec((1, tr, C), lambda i, kc_ref: (j, i, 0))
    if kind == "rows":
        s_spec = pl.BlockSpec((1, tr, C), lambda i, kc_ref: (kc_ref[0], i, 0))
    else:
        s_spec = pl.BlockSpec((tr, C), lambda i, kc_ref: (i, kc_ref[0]))
    return pl.pallas_call(
        body, name=name, out_shape=jax.ShapeDtypeStruct((2 * h, C), F32),
        grid_spec=pltpu.PrefetchScalarGridSpec(
            num_scalar_prefetch=1, grid=(nt,),
            in_specs=[s_spec, peer(0), peer(1), peer(2)],
            out_specs=pl.BlockSpec((tr, C), lambda i, kc_ref: (kc_ref[1] * nt + i, 0))),
        compiler_params=_cparams(("parallel",)),
    )(kcvec, s1, r2, r2, r2)


def _join_sibling_halves(bufs):
    n = len(bufs)

    def body(*refs):
        ins, outs, send_sems, recv_sems = refs[:n], refs[n:2 * n], refs[2 * n], refs[2 * n + 1]
        x, y, c, _ = _place()
        cps = []
        for a in range(n):
            h = ins[a].shape[0] // 2
            cps.append(pltpu.make_async_remote_copy(
                src_ref=ins[a].at[pl.ds(c * h, h), :], dst_ref=outs[a].at[pl.ds(c * h, h), :], send_sem=send_sems.at[a],
                recv_sem=recv_sems.at[a], device_id=(x, y, 1 - c), device_id_type=MESH))
        for cp in cps:
            cp.start()
        for a in range(n):
            h = ins[a].shape[0] // 2
            theirs = outs[a].at[pl.ds((1 - c) * h, h), :]
            pltpu.make_async_remote_copy(src_ref=theirs, dst_ref=theirs, send_sem=send_sems.at[a],
                                         recv_sem=recv_sems.at[a], device_id=(x, y, 1 - c), device_id_type=MESH).wait_recv()
        for cp in cps:
            cp.wait_send()

    return pl.pallas_call(
        body, name="grads_join",
        out_shape=tuple(jax.ShapeDtypeStruct(b.shape, b.dtype) for b in bufs),
        in_specs=[ANY] * n, out_specs=tuple([ANY] * n), input_output_aliases={a: a for a in range(n)},
        scratch_shapes=[pltpu.SemaphoreType.DMA((n,)), pltpu.SemaphoreType.DMA((n,))],
    )(*bufs)


N_DEV = 8


def _allreduce_small(pack):
    P = pack.shape[0]

    def body(p_ref, o_ref, gath, send_sems, recv_sems):
        x, y, c, _ = _place()
        me = 4 * x + 2 * y + c
        cps = []
        for mask in range(1, N_DEV):
            px = 1 - x if mask & 4 else x
            py = 1 - y if mask & 2 else y
            pc = 1 - c if mask & 1 else c
            cps.append((pltpu.make_async_remote_copy(
                src_ref=p_ref, dst_ref=gath.at[me], send_sem=send_sems.at[mask - 1], recv_sem=recv_sems.at[mask - 1],
                device_id=(px, py, pc), device_id_type=MESH), 4 * px + 2 * py + pc, mask))
        for cp, _, _ in cps:
            cp.start()
        gath[me] = p_ref[...]
        for _, peer, mask in cps:
            pltpu.make_async_remote_copy(
                src_ref=p_ref, dst_ref=gath.at[peer], send_sem=send_sems.at[mask - 1], recv_sem=recv_sems.at[mask - 1],
                device_id=(x, y, c), device_id_type=MESH).wait_recv()
        for cp, _, _ in cps:
            cp.wait_send()
        acc = gath[0]
        for i in range(1, N_DEV):
            acc = acc + gath[i]
        o_ref[...] = acc

    return pl.pallas_call(
        body, name="allreduce_small", out_shape=jax.ShapeDtypeStruct((P, LANES), F32),
        in_specs=[pl.BlockSpec(memory_space=pltpu.VMEM)], out_specs=pl.BlockSpec(memory_space=pltpu.VMEM),
        scratch_shapes=[pltpu.VMEM((N_DEV, P, LANES), F32), pltpu.SemaphoreType.DMA((N_DEV - 1,)),
                        pltpu.SemaphoreType.DMA((N_DEV - 1,))],
    )(pack)


def _pack_rows(arrs):
    rows = []
    for a in arrs:
        f = a.reshape(-1)
        f = jnp.pad(f, (0, (-f.shape[0]) % (8 * LANES)))
        rows.append(f.reshape(-1, LANES))
    return jnp.concatenate(rows, axis=0)


def _unpack_rows(pack, shapes):
    out, r = [], 0
    for s in shapes:
        n = math.prod(s)
        out.append(pack[r:r + -(-n // LANES)].reshape(-1)[:n].reshape(s))
        r += 8 * -(-n // (8 * LANES))
    return out


_SMALL = ["norm_mix_pre", "ml_head_norm", "b_gate_a", "b_gate_b", "norm_mix_post", "norm_ffn_pre", "norm_ffn_post",
          "conv_b", "b_ml_i", "b_ml_f", "b_fox_f"]
_BIG = ["w_in", "w_branch_a", "w_branch_b", "w_out", "w_up", "w_down"]
_WEIGHTS = ['norm_mix_pre', 'w_in', 'b_ml_i', 'b_ml_f', 'ml_head_norm', 'b_fox_f', 'b_gate_a', 'b_gate_b', 'w_branch_a',
            'w_branch_b', 'w_out', 'norm_mix_post', 'norm_ffn_pre', 'w_up', 'conv_w', 'conv_b', 'w_down', 'norm_ffn_post']


_KINDS = ["rows", "rows", "rows", "rows", "cols", "rows"]


def kernel(x, norm_mix_pre, w_in, b_ml_i, b_ml_f, ml_head_norm, b_fox_f, b_gate_a, b_gate_b, w_branch_a, w_branch_b, w_out, norm_mix_post, norm_ffn_pre, w_up, conv_w, conv_b, w_down, norm_ffn_post, loss_target, m_norm_mix_pre, m_w_in, m_b_ml_i, m_b_ml_f, m_ml_head_norm, m_b_fox_f, m_b_gate_a, m_b_gate_b, m_w_branch_a, m_w_branch_b, m_w_out, m_norm_mix_post, m_norm_ffn_pre, m_w_up, m_conv_w, m_conv_b, m_w_down, m_norm_ffn_post, v_norm_mix_pre, v_w_in, v_b_ml_i, v_b_ml_f, v_ml_head_norm, v_b_fox_f, v_b_gate_a, v_b_gate_b, v_w_branch_a, v_w_branch_b, v_w_out, v_norm_mix_post, v_norm_ffn_pre, v_w_up, v_conv_w, v_conv_b, v_w_down, v_norm_ffn_post):
    args = dict(locals())
    w = {n: args[n] for n in _WEIGHTS}
    mom = {n: args["m_" + n] for n in _WEIGHTS}
    var = {n: args["v_" + n] for n in _WEIGHTS}
    cx, cy, cc = lax.axis_index("x"), lax.axis_index("y"), lax.axis_index("c")
    kme = 2 * cx + cy
    cvec = jnp.reshape(cc, (1,)).astype(jnp.int32)
    kcvec = jnp.stack([kme, cc]).astype(jnp.int32)
    odd = kme % 2

    tr3 = lambda t: jnp.transpose(t, (0, 2, 1))
    w["w_in"], mom["w_in"], var["w_in"] = tr3(w_in), tr3(m_w_in), tr3(v_w_in)
    w_in_main = lax.dynamic_slice_in_dim(w["w_in"][0], 4 * odd, 2048, axis=0).astype(BF16)
    w_in_gates = lax.dynamic_slice_in_dim(w["w_in"][0], 2048 * (1 - odd), 4, axis=0).astype(BF16)
    (wmain_t,), (g_cw, g_gates) = _gather_weights([w_in_main], _KINDS[:1], [w["conv_w"][0], w_in_gates])
    rest_started = _gather_start([w[n][0].astype(BF16) for n in _BIG[1:]], _KINDS[1:], "gather_rest_start")

    def rest_weights(after):
        bufs = _gather_wait(rest_started, after, _KINDS[1:], "gather_rest_wait")
        g_a, g_b, g_out, wup, g_down = _gather_relay(bufs, _KINDS[1:], "gather_rest_relay")
        return full(g_a), full(g_b), full(g_out), wup, full(g_down)
    gate_rows = g_gates.reshape(16, D_MODEL)
    wsmall_t = jnp.zeros((N_SMALL, D_MODEL), BF16)
    for blk, (lo, hi) in enumerate(((0, 4), (4, 8), (8, 16))):
        wsmall_t = wsmall_t.at[blk * LANES:blk * LANES + hi - lo].set(gate_rows[lo:hi])
    full = lambda g: g.reshape(-1, g.shape[2])
    p = {n: w[n] for n in _SMALL}
    p["conv_w"] = jnp.transpose(g_cw, (1, 0, 2)).reshape(3, -1)

    groups = []

    def on_grads(group, gs):
        names = list(gs)
        kinds = [_KINDS[_BIG.index(n)] for n in names]
        whole = [g if k == "cols" else g.reshape(4, -1, g.shape[1]) for g, k in zip(gs.values(), kinds)]
        from_sibling = _exchange_sibling_halves(whole, kinds, "grads_to_sibling_" + group)
        sums = [_add_halves(g, r, cvec, k, "add_sibling_" + n) for g, r, k, n in zip(whole, from_sibling, kinds, names)]
        started = _exchange_chips_start(sums, kinds, "grads_to_chips_start_" + group)
        groups.append((group, names, kinds, started))
        return started[-1]

    loss_row, grad_x, big, small = _local_step(x[0], loss_target[0], full(wmain_t), wsmall_t, rest_weights, p, on_grads,
                                               rest_started[-1])
    grads = {}
    mine, mine_names = [], []
    for group, names, kinds, started in groups:
        sums, got = _exchange_chips_wait(started, grad_x, kinds, "grads_to_chips_wait_" + group)
        mine += [_add_chips(s, r, kcvec, k, "add_chips_" + n) for s, r, k, n in zip(sums, got, kinds, names)]
        mine_names += names
    grads.update(zip(mine_names, _join_sibling_halves(mine)))

    gt = big["wsmall_t"]
    small["w_in_gates"] = jnp.concatenate([gt[0:4], gt[LANES:LANES + 4], gt[2 * LANES:2 * LANES + 8]], axis=0)
    small_names = _SMALL + ["conv_w"]
    packed_names = small_names + ["w_in_gates"]
    pack = _pack_rows([small[n] for n in packed_names] + [loss_row])
    pack = jnp.pad(pack, ((0, (-pack.shape[0]) % 8), (0, 0)))
    full_shapes = [small[n].shape if n in ("conv_w", "w_in_gates") else w[n][0].shape for n in packed_names]
    total = _unpack_rows(_allreduce_small(pack), full_shapes + [loss_row.shape])
    for n, t in zip(packed_names, total):
        grads[n] = t
    loss = total[-1][0, 0]
    grads["conv_w"] = lax.dynamic_slice_in_dim(grads["conv_w"], kme * conv_w.shape[2], conv_w.shape[2], axis=1)
    my_gates = lax.dynamic_slice_in_dim(grads.pop("w_in_gates"), 4 * kme, 4, axis=0)
    g_in = jnp.zeros(w["w_in"].shape[1:], F32)
    g_in = lax.dynamic_update_slice_in_dim(g_in, grads["w_in"], 4 * odd, axis=0)
    grads["w_in"] = lax.dynamic_update_slice_in_dim(g_in, my_gates, 2048 * (1 - odd), axis=0)

    delta, new_m, new_v = {}, {}, {}
    for n in _BIG:
        delta[n], new_m[n], new_v[n] = _adamw(w[n], grads[n], mom[n], var[n], "adamw_" + n)
        grads[n] = grads[n][None]
    for d in (grads, delta, new_m, new_v):
        d["w_in"] = tr3(d["w_in"])
    packs = [_pack_rows([d[n][0] for n in small_names]) for d in (w, mom, var)]
    pad = ((0, (-packs[0].shape[0]) % 8), (0, 0))
    packs = [jnp.pad(t, pad)[None] for t in packs]
    gp = jnp.pad(_pack_rows([grads[n] for n in small_names]), pad)
    shapes = [w[n][0].shape for n in small_names]
    for dst, res in zip((delta, new_m, new_v), _adamw(packs[0], gp, packs[1], packs[2], "adamw_small")):
        for n, t in zip(small_names, _unpack_rows(res[0], shapes)):
            dst[n] = t[None]
    for n in small_names:
        grads[n] = grads[n][None]

    return (loss, grad_x[None], *[grads[n] for n in _WEIGHTS], *[delta[n] for n in _WEIGHTS],
            *[new_m[n] for n in _WEIGHTS], *[new_v[n] for n in _WEIGHTS])
```

```python
import functools
import math

import jax
import jax.numpy as jnp
from jax import lax
from jax.experimental import pallas as pl
from jax.experimental.pallas import tpu as pltpu

F32 = jnp.float32
BF16 = jnp.bfloat16
MESH = pl.DeviceIdType.MESH

D_MODEL = 1024
ML_HEADS = 4
ML_DQK = 128
ML_DV = 256
FOX_HEADS = 8
FOX_DH = 128
D_FF = 2816
GATE_CAP = 15.0
EPS = 1e-6
ADAM_LR, ADAM_B1, ADAM_B2, ADAM_EPS, ADAM_WD, ADAM_STEP = 0.001, 0.9, 0.999, 1e-08, 0.01, 10

LANES = 128
MLC = 128
FOX_TQ = 512
FOX_TQ_FWD = 512
FOX_TK = 512
FOX_TK_FWD = 512
ROW_T = 512
VMEM_LIMIT = 56 * 1024 * 1024

C_QM, C_KM, C_VM, C_OM = 0, 512, 1024, 2048
N_ML, N_FOX, N_GATE = 3072, 3072, 2048
N_SMALL = 384


def _cparams(sem=None):
    return pltpu.CompilerParams(dimension_semantics=sem, vmem_limit_bytes=VMEM_LIMIT)


def _tile(n, target):
    if n <= target:
        return n
    best = None
    for t in range(LANES, target + 1, LANES):
        if n % t == 0:
            best = t
    assert best is not None, (n, target)
    return best


def _dot(a, b, dims):
    return lax.dot_general(a, b, (dims, ((), ())), preferred_element_type=F32)


def _dot_nn(a, b):
    return _dot(a, b, ((1,), (0,)))


def _dot_nt(a, b):
    return _dot(a, b, ((1,), (1,)))


def _dot_tn(a, b):
    return _dot(a, b, ((0,), (0,)))


_DOTS = {"nn": _dot_nn, "nt": _dot_nt, "tn": _dot_tn}


def _mm(a, b, mode, out_dtype, name, tm=1024, tn=1408, tk=1408):
    a_parts = list(a) if isinstance(a, (list, tuple)) else [a]
    b_parts = list(b) if isinstance(b, (list, tuple)) else [b]
    assert len(a_parts) == 1 or len(b_parts) == 1, name
    a_axes = {"nn": "ik", "nt": "ik", "tn": "ki"}[mode]
    b_axes = {"nn": "kj", "nt": "jk", "tn": "kj"}[mode]
    size, target = {}, dict(i=tm, j=tn, k=tk)
    for parts, axes in ((a_parts, a_axes), (b_parts, b_axes)):
        dims = (parts[0].shape[0], parts[0].shape[1] * len(parts))
        for ax, n in zip(axes, dims):
            assert size.setdefault(ax, n) == n, (name, ax, n, size)
    tile = {}
    for parts, axes in ((a_parts, a_axes), (b_parts, b_axes)):
        if len(parts) > 1:
            tile[axes[1]] = _tile(parts[0].shape[1], target[axes[1]])
    for ax in "ijk":
        tile.setdefault(ax, _tile(size[ax], target[ax]))
    M, N, nk = size["i"], size["j"], size["k"] // tile["k"]
    grid_pos = dict(i=0, j=1, k=2)
    dot = _DOTS[mode]

    def specs(parts, axes):
        blk = (tile[axes[0]], tile[axes[1]])
        if len(parts) == 1:
            return [pl.BlockSpec(blk, lambda *g: (g[grid_pos[axes[0]]], g[grid_pos[axes[1]]]))], None
        bpp = parts[0].shape[1] // blk[1]

        def index(p):
            def f(*g):
                g0, g1 = g[grid_pos[axes[0]]], g[grid_pos[axes[1]]]
                on = g1 // bpp == p
                return jnp.where(on, g0, 0), jnp.where(on, g1 % bpp, 0)
            return f

        return [pl.BlockSpec(blk, index(p)) for p in range(len(parts))], (axes[1], bpp)

    a_specs, a_sel = specs(a_parts, a_axes)
    b_specs, b_sel = specs(b_parts, b_axes)
    na, nb = len(a_parts), len(b_parts)

    def body(*refs):
        a_refs, b_refs, o_ref, acc = refs[:na], refs[na:na + nb], refs[na + nb], refs[na + nb + 1:]

        def accumulate(part):
            if nk == 1:
                o_ref[...] = part.astype(o_ref.dtype)
                return
            acc_ref, = acc
            k = pl.program_id(2)

            @pl.when(k == 0)
            def _():
                acc_ref[...] = part

            @pl.when(k > 0)
            def _():
                acc_ref[...] += part

            @pl.when(k == nk - 1)
            def _():
                o_ref[...] = acc_ref[...].astype(o_ref.dtype)

        sel = a_sel or b_sel
        if sel is None:
            accumulate(dot(a_refs[0][...], b_refs[0][...]))
        else:
            which = pl.program_id(grid_pos[sel[0]]) // sel[1]
            for p in range(max(na, nb)):
                @pl.when(which == p)
                def _(p=p):
                    accumulate(dot(a_refs[p if a_sel else 0][...], b_refs[p if b_sel else 0][...]))

    return pl.pallas_call(
        body, name=name,
        out_shape=jax.ShapeDtypeStruct((M, N), out_dtype),
        grid=(M // tile["i"], N // tile["j"], nk),
        in_specs=a_specs + b_specs,
        out_specs=pl.BlockSpec((tile["i"], tile["j"]), lambda i, j, k: (i, j)),
        scratch_shapes=[pltpu.VMEM((tile["i"], tile["j"]), F32)] if nk > 1 else [],
        compiler_params=_cparams(("parallel", "parallel", "arbitrary")),
    )(*a_parts, *b_parts)


def _rstd(x):
    return lax.rsqrt(jnp.mean(x * x, axis=-1, keepdims=True) + EPS)


def _rmsnorm_fwd(x, g, name):
    S, D = x.shape
    T = _tile(S, ROW_T)

    def body(x_ref, g_ref, o_ref):
        xv = x_ref[...]
        o_ref[...] = (xv * _rstd(xv) * g_ref[...]).astype(o_ref.dtype)

    return pl.pallas_call(
        body, name=name, out_shape=jax.ShapeDtypeStruct((S, D), BF16), grid=(S // T,),
        in_specs=[pl.BlockSpec((T, D), lambda i: (i, 0)), pl.BlockSpec((1, D), lambda i: (0, 0))],
        out_specs=pl.BlockSpec((T, D), lambda i: (i, 0)),
        compiler_params=_cparams(("parallel",)),
    )(x, g)


def _resid_norm_fwd(x, z, g, name):
    S, D = x.shape
    T = _tile(S, ROW_T)

    def body(x_ref, z_ref, g_ref, o_ref):
        zv = z_ref[...]
        o_ref[...] = x_ref[...] + zv * _rstd(zv) * g_ref[...]

    row = pl.BlockSpec((T, D), lambda i: (i, 0))
    return pl.pallas_call(
        body, name=name, out_shape=jax.ShapeDtypeStruct((S, D), F32), grid=(S // T,),
        in_specs=[row, row, pl.BlockSpec((1, D), lambda i: (0, 0))],
        out_specs=row, compiler_params=_cparams(("parallel",)),
    )(x, z, g)


def _rmsnorm_bwd_math(dy, xv, g):
    r = _rstd(xv)
    u = dy * g
    dx = r * u - xv * (r * r * r) * jnp.mean(u * xv, axis=-1, keepdims=True)
    return dx, dy * xv * r


def _rmsnorm_bwd(dys, xin, g, resid, out_dtype, name):
    S, D = xin.shape
    T = _tile(S, ROW_T)
    has_resid = resid is not None
    ndy = len(dys)

    def body(*refs):
        dy_refs, (x_ref, g_ref) = refs[:ndy], refs[ndy:ndy + 2]
        dx_ref, dg_ref = refs[-2:]
        dy = dy_refs[0][...]
        for r in dy_refs[1:]:
            dy = dy + r[...]
        dx, dgt = _rmsnorm_bwd_math(dy, x_ref[...], g_ref[...])
        if has_resid:
            dx = dx + refs[ndy + 2][...]
        dx_ref[...] = dx.astype(dx_ref.dtype)

        @pl.when(pl.program_id(0) == 0)
        def _():
            dg_ref[...] = jnp.zeros_like(dg_ref)

        dg_ref[...] += jnp.sum(dgt, axis=0, keepdims=True)

    row = pl.BlockSpec((T, D), lambda i: (i, 0))
    vec = pl.BlockSpec((1, D), lambda i: (0, 0))
    ins = list(dys) + [xin, g] + ([resid] if has_resid else [])
    return pl.pallas_call(
        body, name=name,
        out_shape=(jax.ShapeDtypeStruct((S, D), out_dtype), jax.ShapeDtypeStruct((1, D), F32)),
        grid=(S // T,), in_specs=[row] * ndy + [row, vec] + ([row] if has_resid else []),
        out_specs=(row, vec), compiler_params=_cparams(("arbitrary",)),
    )(*ins)


def _loss_head(x1, d, g, target, name):
    S, D = x1.shape
    T = _tile(S, ROW_T)

    def body(x_ref, d_ref, g_ref, t_ref, loss_ref, dy_ref, dd_ref, dg_ref):
        dv, gv = d_ref[...], g_ref[...]
        y = x_ref[...] + dv * _rstd(dv) * gv
        diff = y - t_ref[...]
        dy = diff * (1.0 / D)
        dy_ref[...] = dy
        dd, dgt = _rmsnorm_bwd_math(dy, dv, gv)
        dd_ref[...] = dd.astype(dd_ref.dtype)

        @pl.when(pl.program_id(0) == 0)
        def _():
            dg_ref[...] = jnp.zeros_like(dg_ref)
            loss_ref[...] = jnp.zeros_like(loss_ref)

        dg_ref[...] += jnp.sum(dgt, axis=0, keepdims=True)
        part = jnp.sum(jnp.sum(diff * diff, axis=1, keepdims=True), axis=0, keepdims=True)
        loss_ref[...] += (0.5 / D) * part

    row = pl.BlockSpec((T, D), lambda i: (i, 0))
    vec = pl.BlockSpec((1, D), lambda i: (0, 0))
    return pl.pallas_call(
        body, name=name,
        out_shape=(jax.ShapeDtypeStruct((1, LANES), F32), jax.ShapeDtypeStruct((S, D), F32),
                   jax.ShapeDtypeStruct((S, D), BF16), jax.ShapeDtypeStruct((1, D), F32)),
        grid=(S // T,), in_specs=[row, row, vec, row],
        out_specs=(pl.BlockSpec((1, LANES), lambda i: (0, 0)), row, row, vec),
        compiler_params=_cparams(("arbitrary",)),
    )(x1, d, g, target)


def _merge_fwd(ya, yb, pm, ba, bb, name):
    S, D = ya.shape
    T = _tile(S, ROW_T)

    def body(ya_ref, yb_ref, ga_ref, gb_ref, ba_ref, bb_ref, o_ref):
        sa = jax.nn.sigmoid(ga_ref[...] + ba_ref[...])
        sb = jax.nn.sigmoid(gb_ref[...] + bb_ref[...])
        o_ref[...] = (sa * ya_ref[...] + sb * yb_ref[...]).astype(o_ref.dtype)

    row = pl.BlockSpec((T, D), lambda i: (i, 0))
    vec = pl.BlockSpec((1, D), lambda i: (0, 0))
    return pl.pallas_call(
        body, name=name, out_shape=jax.ShapeDtypeStruct((S, D), BF16), grid=(S // T,),
        in_specs=[row, row, pl.BlockSpec((T, D), lambda i: (i, 0)),
                  pl.BlockSpec((T, D), lambda i: (i, 1)), vec, vec],
        out_specs=row, compiler_params=_cparams(("parallel",)),
    )(ya, yb, pm, pm, ba, bb)


def _merge_bwd(dmerged, ya, yb, pm, ba, bb, name):
    S, D = ya.shape
    T = _tile(S, ROW_T)

    def body(dm_ref, ya_ref, yb_ref, ga_ref, gb_ref, ba_ref, bb_ref,
             dya_ref, dyb_ref, dga_ref, dgb_ref, dba_ref, dbb_ref):
        dm = dm_ref[...]
        sa = jax.nn.sigmoid(ga_ref[...] + ba_ref[...])
        sb = jax.nn.sigmoid(gb_ref[...] + bb_ref[...])
        dya_ref[...] = (dm * sa).astype(dya_ref.dtype)
        dyb_ref[...] = (dm * sb).astype(dyb_ref.dtype)
        dga = dm * ya_ref[...] * sa * (1.0 - sa)
        dgb = dm * yb_ref[...] * sb * (1.0 - sb)
        dga_ref[...] = dga.astype(dga_ref.dtype)
        dgb_ref[...] = dgb.astype(dgb_ref.dtype)

        @pl.when(pl.program_id(0) == 0)
        def _():
            dba_ref[...] = jnp.zeros_like(dba_ref)
            dbb_ref[...] = jnp.zeros_like(dbb_ref)

        dba_ref[...] += jnp.sum(dga, axis=0, keepdims=True)
        dbb_ref[...] += jnp.sum(dgb, axis=0, keepdims=True)

    row = pl.BlockSpec((T, D), lambda i: (i, 0))
    vec = pl.BlockSpec((1, D), lambda i: (0, 0))
    act = jax.ShapeDtypeStruct((S, D), BF16)
    v1 = jax.ShapeDtypeStruct((1, D), F32)
    return pl.pallas_call(
        body, name=name, out_shape=(act, act, act, act, v1, v1), grid=(S // T,),
        in_specs=[row, row, row, pl.BlockSpec((T, D), lambda i: (i, 0)),
                  pl.BlockSpec((T, D), lambda i: (i, 1)), vec, vec],
        out_specs=(row, row, row, row, vec, vec), compiler_params=_cparams(("arbitrary",)),
    )(dmerged, ya, yb, pm, pm, ba, bb)


_GELU_C = math.sqrt(2.0 / math.pi)


def _gelu(g):
    t = jnp.tanh(_GELU_C * (g + 0.044715 * g * g * g))
    return 0.5 * g * (1.0 + t), t


def _gelu_grad(g, t):
    return 0.5 * (1.0 + t) + 0.5 * g * (1.0 - t * t) * _GELU_C * (1.0 + 3 * 0.044715 * g * g)


def _shift_down(v, halo_ref, first, rows):
    T = v.shape[0]
    keep = jnp.where(first, 0.0, 1.0)
    h7 = halo_ref[7:8, :] * keep
    h6 = halo_ref[6:7, :] * keep
    m1 = jnp.where(rows == 0, h7, pltpu.roll(v, 1, 0))
    m2 = jnp.where(rows == 0, h6, jnp.where(rows == 1, h7, pltpu.roll(v, 2, 0)))
    return m1, m2


def _conv_act_fwd(up, cw, cb, name):
    S, F2 = up.shape
    Fh = F2 // 2
    T = _tile(S, ROW_T)
    tc = _tile(Fh, 256)
    ncol = Fh // tc
    hb = T // 8

    def body(ua_ref, ug_ref, ha_ref, hg_ref, wa_ref, wg_ref, ba_ref, bg_ref, o_ref, a_ref, g_ref):
        first = pl.program_id(0) == 0
        rows = lax.broadcasted_iota(jnp.int32, (T, tc), 0)

        def conv(u_ref, h_ref, w_ref, b_ref):
            v = u_ref[...]
            m1, m2 = _shift_down(v, h_ref, first, rows)
            return b_ref[...] + w_ref[0:1, :] * m2 + w_ref[1:2, :] * m1 + w_ref[2:3, :] * v

        a = conv(ua_ref, ha_ref, wa_ref, ba_ref)
        g = conv(ug_ref, hg_ref, wg_ref, bg_ref)
        a_ref[...] = a
        g_ref[...] = g
        o_ref[...] = (_gelu(g)[0] * a).astype(o_ref.dtype)

    halo = lambda off: pl.BlockSpec((8, tc), lambda i, j: (jnp.maximum(i * hb - 1, 0), j + off))
    blk = pl.BlockSpec((T, tc), lambda i, j: (i, j))
    f32 = jax.ShapeDtypeStruct((S, Fh), F32)
    return pl.pallas_call(
        body, name=name, out_shape=(jax.ShapeDtypeStruct((S, Fh), BF16), f32, f32), grid=(S // T, ncol),
        in_specs=[blk, pl.BlockSpec((T, tc), lambda i, j: (i, j + ncol)),
                  halo(0), halo(ncol),
                  pl.BlockSpec((3, tc), lambda i, j: (0, j)), pl.BlockSpec((3, tc), lambda i, j: (0, j + ncol)),
                  pl.BlockSpec((1, tc), lambda i, j: (0, j)), pl.BlockSpec((1, tc), lambda i, j: (0, j + ncol))],
        out_specs=(blk, blk, blk),
        compiler_params=_cparams(("parallel", "parallel")),
    )(up, up, up, up, cw, cw, cb, cb)


def _conv_act_bwd(up, a, g, dact, cw, name):
    S, F2 = up.shape
    Fh = F2 // 2
    T = _tile(S, ROW_T)
    tc = _tile(Fh, 256)
    ncol, nrow, hb, nhb = Fh // tc, S // T, T // 8, S // 8

    def body(ua_ref, ug_ref, a_ref, g_ref, an_ref, gn_ref, wa_ref, wg_ref, da_ref, dn_ref,
             dpa_ref, dpg_ref, dwa_ref, dwg_ref, dba_ref, dbg_ref, dua_n, dug_n):
        i = pl.program_id(1)
        rows = lax.broadcasted_iota(jnp.int32, (T, tc), 0)

        def du_of(a, g, dact_v):
            gel, t = _gelu(g)
            return dact_v * gel, dact_v * a * _gelu_grad(g, t)

        dua, dug = du_of(a_ref[...], g_ref[...], da_ref[...])
        keep = jnp.where(i == nrow - 1, 0.0, 1.0)
        dua_n[...], dug_n[...] = du_of(an_ref[...], gn_ref[...], dn_ref[...] * keep)

        @pl.when(i == 0)
        def _():
            for r in (dwa_ref, dwg_ref, dba_ref, dbg_ref):
                r[...] = jnp.zeros_like(r)

        for du, n_ref, u_ref, w_ref, o_ref, dw_ref, db_ref in ((dua, dua_n, ua_ref, wa_ref, dpa_ref, dwa_ref, dba_ref),
                                                               (dug, dug_n, ug_ref, wg_ref, dpg_ref, dwg_ref, dbg_ref)):
            n0, n1 = n_ref[0:1, :], n_ref[1:2, :]
            du1 = jnp.where(rows == T - 1, n0, pltpu.roll(du, T - 1, 0))
            du2 = jnp.where(rows == T - 2, n0, jnp.where(rows == T - 1, n1, pltpu.roll(du, T - 2, 0)))
            o_ref[...] = (w_ref[2:3, :] * du + w_ref[1:2, :] * du1 + w_ref[0:1, :] * du2).astype(o_ref.dtype)
            u = u_ref[...]
            db_ref[...] += jnp.sum(du, axis=0, keepdims=True)
            for j, d in enumerate((du2, du1, du)):
                dw_ref[j:j + 1, :] += jnp.sum(d * u, axis=0, keepdims=True)

    tile = lambda off: pl.BlockSpec((T, tc), lambda j, i: (i, j + off))
    under = pl.BlockSpec((8, tc), lambda j, i: (jnp.minimum((i + 1) * hb, nhb - 1), j))
    vec = lambda n, off: pl.BlockSpec((n, tc), lambda j, i: (0, j + off))
    act = jax.ShapeDtypeStruct((S, Fh), BF16)
    return pl.pallas_call(
        body, name=name,
        out_shape=(act, act, jax.ShapeDtypeStruct((3, Fh), F32), jax.ShapeDtypeStruct((3, Fh), F32),
                   jax.ShapeDtypeStruct((1, Fh), F32), jax.ShapeDtypeStruct((1, Fh), F32)),
        grid=(ncol, nrow),
        in_specs=[tile(0), tile(ncol), tile(0), tile(0), under, under, vec(3, 0), vec(3, ncol), tile(0), under],
        out_specs=(tile(0), tile(0), vec(3, 0), vec(3, 0), vec(1, 0), vec(1, 0)),
        scratch_shapes=[pltpu.VMEM((8, tc), F32), pltpu.VMEM((8, tc), F32)],
        compiler_params=_cparams(("parallel", "arbitrary")),
    )(up, up, a, g, a, g, cw, cw, dact, dact)


def _split3(x):
    hi = x.astype(BF16)
    r1 = x - hi.astype(F32)
    mid = r1.astype(BF16)
    lo = (r1 - mid.astype(F32)).astype(BF16)
    return hi, mid, lo


def _tri_dot(tri, x):
    hi, mid, lo = _split3(x)
    return _dot_nn(tri, hi) + _dot_nn(tri, mid) + _dot_nn(tri, lo)


def _log_sigmoid(x):
    return jnp.minimum(x, 0.0) - jnp.log(1.0 + jnp.exp(-jnp.abs(x)))


def _tri_mask(n, lower):
    r = lax.broadcasted_iota(jnp.int32, (n, n), 0)
    c = lax.broadcasted_iota(jnp.int32, (n, n), 1)
    return (r >= c) if lower else (r <= c)


def _gates_fwd(ps, bi, bf, bff, name):
    S = ps.shape[0]
    NC = S // MLC

    def body(ps_ref, bi_ref, bf_ref, bff_ref, a_ref, A_ref, wi_ref, em_ref, wk_ref, dec_ref, F_ref, m_scr, f_scr):
        @pl.when(pl.program_id(0) == 0)
        def _():
            m_scr[...] = jnp.zeros_like(m_scr)
            f_scr[...] = jnp.zeros_like(f_scr)

        rows = lax.broadcasted_iota(jnp.int32, (MLC, LANES), 0)
        ltri = _tri_mask(MLC, True).astype(BF16)
        li = GATE_CAP * jnp.tanh((ps_ref[:, 0:LANES] + bi_ref[...]) / GATE_CAP)
        lf = _log_sigmoid(GATE_CAP * jnp.tanh((ps_ref[:, LANES:2 * LANES] + bf_ref[...]) / GATE_CAP))
        b = _tri_dot(ltri, lf)
        a = li - b
        cm = a
        sh = 1
        while sh < MLC:
            cm = jnp.where(rows >= sh, jnp.maximum(cm, pltpu.roll(cm, sh, 0)), cm)
            sh *= 2
        m0 = m_scr[...]
        A = jnp.maximum(cm, m0)
        a_ref[...] = a
        A_ref[...] = A
        A_last = A_ref[MLC - 1:MLC, :]
        wi_ref[...] = jnp.exp(m0 - A)
        em_ref[...] = jnp.exp(-(b + A))
        wk_ref[...] = jnp.exp(a - A_last)
        dec_ref[0] = jnp.exp(m0 - A_last)
        F_ref[...] = b
        m_scr[...] = F_ref[MLC - 1:MLC, :] + A_last
        lfg = _log_sigmoid(ps_ref[:, 2 * LANES:3 * LANES] + bff_ref[...])
        F_ref[...] = _tri_dot(ltri, lfg) + f_scr[...]
        f_scr[...] = F_ref[MLC - 1:MLC, :]

    col = pl.BlockSpec((MLC, LANES), lambda c: (c, 0))
    vec = pl.BlockSpec((1, LANES), lambda c: (0, 0))
    cs = jax.ShapeDtypeStruct((S, LANES), F32)
    return pl.pallas_call(
        body, name=name,
        out_shape=(cs, cs, cs, cs, cs, jax.ShapeDtypeStruct((NC, 1, LANES), F32), cs),
        grid=(NC,), in_specs=[pl.BlockSpec((MLC, N_SMALL), lambda c: (c, 0)), vec, vec, vec],
        out_specs=(col, col, col, col, col, pl.BlockSpec((1, 1, LANES), lambda c: (c, 0, 0)), col),
        scratch_shapes=[pltpu.VMEM((1, LANES), F32), pltpu.VMEM((1, LANES), F32)],
        compiler_params=_cparams(("arbitrary",)),
    )(ps, bi, bf, bff)


def _gates_bwd(ps, bi, bf, bff, rk, kc, tch, dF, name):
    S = ps.shape[0]
    NC = S // MLC

    def body(ps_ref, bi_ref, bf_ref, bff_ref, rk_ref, kc_ref, t_ref, dF_ref, dps_ref, db_ref, carry):
        @pl.when(pl.program_id(0) == 0)
        def _():
            carry[...] = jnp.zeros_like(carry)
            db_ref[...] = jnp.zeros_like(db_ref)

        lanes = lax.broadcasted_iota(jnp.int32, (MLC, LANES), 1)
        utri = _tri_mask(MLC, False).astype(BF16)
        ti = jnp.tanh((ps_ref[:, 0:LANES] + bi_ref[...]) / GATE_CAP)
        t_end, t_start = t_ref[0, 0:1, :], t_ref[0, 1:2, :]
        rk = rk_ref[...]
        rk = rk - (jnp.sum(rk, axis=0, keepdims=True) - (t_start - t_end)) * (1.0 / MLC)
        dpi = jnp.where(lanes < ML_HEADS, (kc_ref[...] - rk) * (1.0 - ti * ti), 0.0)
        tf = jnp.tanh((ps_ref[:, LANES:2 * LANES] + bf_ref[...]) / GATE_CAP)
        dlf = _tri_dot(utri, rk) + t_end
        dpf = jnp.where(lanes < ML_HEADS, dlf * jax.nn.sigmoid(-GATE_CAP * tf) * (1.0 - tf * tf), 0.0)
        dFv = dF_ref[...]
        dlfg = _tri_dot(utri, dFv) + carry[...]
        carry[...] += jnp.sum(dFv, axis=0, keepdims=True)
        dpff = jnp.where(lanes < FOX_HEADS, dlfg * jax.nn.sigmoid(-(ps_ref[:, 2 * LANES:3 * LANES] + bff_ref[...])), 0.0)
        for n, dp in enumerate((dpi, dpf, dpff)):
            dps_ref[:, n * LANES:(n + 1) * LANES] = dp.astype(dps_ref.dtype)
            db_ref[:, n * LANES:(n + 1) * LANES] += jnp.sum(dp, axis=0, keepdims=True)

    rev = lambda c: (NC - 1 - c, 0)
    col = pl.BlockSpec((MLC, LANES), rev)
    vec = pl.BlockSpec((1, LANES), lambda c: (0, 0))
    wide = pl.BlockSpec((MLC, N_SMALL), rev)
    return pl.pallas_call(
        body, name=name,
        out_shape=(jax.ShapeDtypeStruct((S, N_SMALL), BF16), jax.ShapeDtypeStruct((1, N_SMALL), F32)),
        grid=(NC,),
        in_specs=[wide, vec, vec, vec, col, col, pl.BlockSpec((1, 2, LANES), lambda c: (NC - 1 - c, 0, 0)), col],
        out_specs=(wide, pl.BlockSpec((1, N_SMALL), lambda c: (0, 0))),
        scratch_shapes=[pltpu.VMEM((1, LANES), F32)],
        compiler_params=_cparams(("arbitrary",)),
    )(ps, bi, bf, bff, rk, kc, tch, dF)


_ML_SCALE = ML_DQK ** -0.5


def _ml_specs(rev, NC):
    idx = (lambda c: NC - 1 - c) if rev else (lambda c: c)
    qk = lambda blk: pl.BlockSpec((MLC, ML_HEADS * ML_DQK), lambda c: (idx(c), blk))
    wide = lambda blk: pl.BlockSpec((MLC, D_MODEL), lambda c: (idx(c), blk))
    col = pl.BlockSpec((MLC, LANES), lambda c: (idx(c), 0))
    return idx, qk, wide, col


def _ml_intra(q_ref, k_ref, arow_ref, A_ref, h):
    hs = slice(h * ML_DQK, (h + 1) * ML_DQK)
    qf = q_ref[:, hs] * _ML_SCALE
    kf = k_ref[:, hs]
    qb, kb = qf.astype(BF16), kf.astype(BF16)
    qk = _dot_nt(qb, kb)
    logw = arow_ref[h:h + 1, :] - A_ref[:, h:h + 1]
    W = jnp.exp(jnp.where(_tri_mask(MLC, True), logw, -1e30))
    return qb, kb, qf, kf, qk, W


def _mlstm_fwd(pm, a_row, A, wi, em, wk, dec, w_hn, name):
    S = pm.shape[0]
    NC = S // MLC
    _, qk, wide, col = _ml_specs(False, NC)

    def body(q_ref, k_ref, v_ref, o_ref, arow_ref, A_ref, wi_ref, em_ref, wk_ref, dec_ref, whn_ref,
             ha_ref, hp_ref, den_ref, cst_ref, nst_ref, C_scr, n_scr):
        @pl.when(pl.program_id(0) == 0)
        def _():
            C_scr[...] = jnp.zeros_like(C_scr)
            n_scr[...] = jnp.zeros_like(n_scr)

        lanes = lax.broadcasted_iota(jnp.int32, (MLC, LANES), 1)
        den_tile = jnp.zeros((MLC, LANES), F32)
        for h in range(ML_HEADS):
            vs = slice(h * ML_DV, (h + 1) * ML_DV)
            qb, kb, qf, kf, qk_, W = _ml_intra(q_ref, k_ref, arow_ref, A_ref, h)
            vb = v_ref[:, vs].astype(BF16)
            Cf = C_scr[h]
            Cb = Cf.astype(BF16)
            nrow = n_scr[h]
            cst_ref[0, h] = Cb
            nst_ref[0, h] = nrow
            s = qk_ * W
            wic = wi_ref[:, h:h + 1]
            num = _dot_nn(s.astype(BF16), vb) + wic * _dot_nt(qb, Cb)
            den = jnp.sum(s, axis=1, keepdims=True) + wic * jnp.sum(qf * nrow, axis=1, keepdims=True)
            hp = num / jnp.maximum(jnp.abs(den), em_ref[:, h:h + 1])
            hp_ref[:, vs] = hp
            den_tile = jnp.where(lanes == h, den, den_tile)
            hn = hp * _rstd(hp) * whn_ref[:, vs]
            ha_ref[:, vs] = (hn * jax.nn.sigmoid(o_ref[:, vs])).astype(ha_ref.dtype)
            wkc = wk_ref[:, h:h + 1]
            kw = kf * wkc
            d = dec_ref[0, :, h:h + 1]
            C_scr[h] = d * Cf + _dot_tn(vb, kw.astype(BF16))
            n_scr[h] = d * nrow + jnp.sum(kw, axis=0, keepdims=True)
        den_ref[...] = den_tile

    return pl.pallas_call(
        body, name=name,
        out_shape=(jax.ShapeDtypeStruct((S, D_MODEL), BF16), jax.ShapeDtypeStruct((S, D_MODEL), F32),
                   jax.ShapeDtypeStruct((S, LANES), F32),
                   jax.ShapeDtypeStruct((NC, ML_HEADS, ML_DV, ML_DQK), BF16),
                   jax.ShapeDtypeStruct((NC, ML_HEADS, 1, ML_DQK), F32)),
        grid=(NC,),
        in_specs=[qk(C_QM // 512), qk(C_KM // 512), wide(C_VM // D_MODEL), wide(C_OM // D_MODEL),
                  pl.BlockSpec((8, MLC), lambda c: (0, c)), col, col, col, col,
                  pl.BlockSpec((1, 1, LANES), lambda c: (c, 0, 0)), pl.BlockSpec((1, D_MODEL), lambda c: (0, 0))],
        out_specs=(pl.BlockSpec((MLC, D_MODEL), lambda c: (c, 0)), pl.BlockSpec((MLC, D_MODEL), lambda c: (c, 0)),
                   col, pl.BlockSpec((1, ML_HEADS, ML_DV, ML_DQK), lambda c: (c, 0, 0, 0)),
                   pl.BlockSpec((1, ML_HEADS, 1, ML_DQK), lambda c: (c, 0, 0, 0))),
        scratch_shapes=[pltpu.VMEM((ML_HEADS, ML_DV, ML_DQK), F32), pltpu.VMEM((ML_HEADS, 1, ML_DQK), F32)],
        compiler_params=_cparams(("arbitrary",)),
    )(pm, pm, pm, pm, a_row, A, wi, em, wk, dec, w_hn)


def _mlstm_bwd(dha, pm, hp_all, den_all, a_row, A, wi, em, wk, dec, cst, nst, w_hn, name):
    S = pm.shape[0]
    NC = S // MLC
    idx, qk, wide, col = _ml_specs(True, NC)

    def body(dha_ref, q_ref, k_ref, v_ref, o_ref, hp_ref, den_ref, arow_ref, A_ref, wi_ref, em_ref, wk_ref,
             dec_ref, cst_ref, nst_ref, whn_ref,
             dqk_ref, dv_ref, do_ref, rk_ref, kc_ref, t_ref, dwhn_ref, dC_scr, dn_scr, t_scr):
        @pl.when(pl.program_id(0) == 0)
        def _():
            dC_scr[...] = jnp.zeros_like(dC_scr)
            dn_scr[...] = jnp.zeros_like(dn_scr)
            t_scr[...] = jnp.zeros_like(t_scr)
            dwhn_ref[...] = jnp.zeros_like(dwhn_ref)

        lanes = lax.broadcasted_iota(jnp.int32, (MLC, LANES), 1)
        lane1 = lax.broadcasted_iota(jnp.int32, (1, LANES), 1)
        t_ref[0, 0:1, :] = t_scr[...]
        rk_tile = jnp.zeros((MLC, LANES), F32)
        kc_tile = jnp.zeros((MLC, LANES), F32)
        t_new = jnp.zeros((1, LANES), F32)
        for h in range(ML_HEADS):
            hs = slice(h * ML_DQK, (h + 1) * ML_DQK)
            vs = slice(h * ML_DV, (h + 1) * ML_DV)
            hp = hp_ref[:, vs]
            sig = jax.nn.sigmoid(o_ref[:, vs])
            whn = whn_ref[:, vs]
            r = _rstd(hp)
            dga = dha_ref[:, vs]
            do_ref[:, vs] = (dga * (hp * r * whn) * sig * (1.0 - sig)).astype(do_ref.dtype)
            dhn = dga * sig
            dhp, dwt = _rmsnorm_bwd_math(dhn, hp, whn)
            dwhn_ref[:, vs] += jnp.sum(dwt, axis=0, keepdims=True)
            den = den_ref[:, h:h + 1]
            floor = em_ref[:, h:h + 1]
            D = jnp.maximum(jnp.abs(den), floor)
            dnum = dhp / D
            dh_h = jnp.sum(dhp * hp, axis=1, keepdims=True)
            active = jnp.abs(den) >= floor
            dden = -dh_h / D * jnp.where(active, jnp.sign(den), 0.0)
            phi = jnp.where(active, 0.0, dh_h)
            qb, kb, qf, kf, qk_, W = _ml_intra(q_ref, k_ref, arow_ref, A_ref, h)
            vf = v_ref[:, vs]
            vb = vf.astype(BF16)
            Cb = cst_ref[0, h]
            nrow = nst_ref[0, h]
            wic = wi_ref[:, h:h + 1]
            wkc = wk_ref[:, h:h + 1]
            d = dec_ref[0, :, h:h + 1]
            dCn = dC_scr[h]
            dCb = dCn.astype(BF16)
            dnn = dn_scr[h]
            dnumb = dnum.astype(BF16)
            s = qk_ * W
            ds = (_dot_nt(dnumb, vb) + dden) * W
            dsb = ds.astype(BF16)
            dnw = (wic * dnum).astype(BF16)
            wd = wic * dden
            kw = kf * wkc
            dv_state = _dot_nt(kw.astype(BF16), dCb)
            dq = _dot_nn(dsb, kb) + _dot_nn(dnw, Cb) + wd * nrow
            dk_state = wkc * (_dot_nn(vb, dCb) + dnn)
            dk = _dot_tn(dsb, qb) + dk_state
            dv = _dot_tn(s.astype(BF16), dnumb) + dv_state
            dC = d * dCn + _dot_tn(dnw, qb)
            dn = d * dnn + jnp.sum(wd * qf, axis=0, keepdims=True)
            dC_scr[h] = dC
            dn_scr[h] = dn
            dqk_ref[:, hs] = (dq * _ML_SCALE).astype(dqk_ref.dtype)
            dqk_ref[:, C_KM + h * ML_DQK:C_KM + (h + 1) * ML_DQK] = dk.astype(dqk_ref.dtype)
            dv_ref[:, vs] = dv.astype(dv_ref.dtype)
            G = ds * qk_
            inter = _dot_nt(qb, Cb)
            qn = jnp.sum(qf * nrow, axis=1, keepdims=True)
            R = (jnp.sum(G, axis=1, keepdims=True)
                 + wic * (jnp.sum(dnum * inter, axis=1, keepdims=True) + dden * qn))
            K = jnp.sum(G.T, axis=1, keepdims=True) + jnp.sum(kf * dk_state, axis=1, keepdims=True)
            rk_tile = jnp.where(lanes == h, R - K, rk_tile)
            kc_tile = jnp.where(lanes == h, phi, kc_tile)
            tt = (jnp.sum(jnp.sum(dC * Cb.astype(F32), axis=1, keepdims=True), axis=0, keepdims=True)
                  + jnp.sum(dn * nrow, axis=1, keepdims=True))
            t_new = jnp.where(lane1 == h, tt, t_new)
        rk_ref[...] = rk_tile
        kc_ref[...] = kc_tile
        t_ref[0, 1:2, :] = t_new
        t_scr[...] = t_new

    act = lambda n: jax.ShapeDtypeStruct((S, n), BF16)
    cs = jax.ShapeDtypeStruct((S, LANES), F32)
    rowblk = lambda n: pl.BlockSpec((MLC, n), lambda c: (idx(c), 0))
    return pl.pallas_call(
        body, name=name,
        out_shape=(act(D_MODEL), act(D_MODEL), act(D_MODEL), cs, cs,
                   jax.ShapeDtypeStruct((NC, 2, LANES), F32), jax.ShapeDtypeStruct((1, D_MODEL), F32)),
        grid=(NC,),
        in_specs=[rowblk(D_MODEL), qk(C_QM // 512), qk(C_KM // 512), wide(C_VM // D_MODEL), wide(C_OM // D_MODEL),
                  rowblk(D_MODEL), col, pl.BlockSpec((8, MLC), lambda c: (0, idx(c))), col, col, col, col,
                  pl.BlockSpec((1, 1, LANES), lambda c: (idx(c), 0, 0)),
                  pl.BlockSpec((1, ML_HEADS, ML_DV, ML_DQK), lambda c: (idx(c), 0, 0, 0)),
                  pl.BlockSpec((1, ML_HEADS, 1, ML_DQK), lambda c: (idx(c), 0, 0, 0)),
                  pl.BlockSpec((1, D_MODEL), lambda c: (0, 0))],
        out_specs=(rowblk(D_MODEL), rowblk(D_MODEL), rowblk(D_MODEL), col, col,
                   pl.BlockSpec((1, 2, LANES), lambda c: (idx(c), 0, 0)), pl.BlockSpec((1, D_MODEL), lambda c: (0, 0))),
        scratch_shapes=[pltpu.VMEM((ML_HEADS, ML_DV, ML_DQK), F32), pltpu.VMEM((ML_HEADS, 1, ML_DQK), F32),
                        pltpu.VMEM((1, LANES), F32)],
        compiler_params=_cparams(("arbitrary",)),
    )(dha, pm, pm, pm, pm, hp_all, den_all, a_row, A, wi, em, wk, dec, cst, nst, w_hn)


_FOX_SCALE = FOX_DH ** -0.5
_NEG = -1e30
_LOG2E = 1.4426950408889634
_LN2 = 0.6931471805599453
_QF_BLK, _KF_BLK, _VF_BLK = 0, FOX_HEADS, 2 * FOX_HEADS


def _lane_pick(tile, lane):
    lanes = lax.broadcasted_iota(jnp.int32, tile.shape, 1)
    return jnp.sum(jnp.where(lanes == lane, tile, 0.0), axis=1, keepdims=True)


def _col_to_row(col):
    return jnp.max(jnp.broadcast_to(col, (col.shape[0], LANES)).T, axis=0, keepdims=True)


def _causal(q0, k0, shape, q_axis):
    qpos = q0 + lax.broadcasted_iota(jnp.int32, shape, q_axis)
    kpos = k0 + lax.broadcasted_iota(jnp.int32, shape, 1 - q_axis)
    return kpos <= qpos


def _fox_fwd(pf, fc, fk_row, name):
    S = pf.shape[0]
    TQ, TK = FOX_TQ_FWD, FOX_TK_FWD
    nq, nk = S // TQ, S // TK
    c1 = _FOX_SCALE * _LOG2E

    def body(q_ref, k_ref, v_ref, fc_ref, fr_ref, o_ref, lse_ref):
        h, i = pl.program_id(0), pl.program_id(1)
        qb = q_ref[...]
        fq2 = _lane_pick(fc_ref[...], h) * _LOG2E

        def step(j, carry, masked):
            m, l, acc = carry
            off = pl.multiple_of(j * TK, TK)
            t = _dot_nt(qb, k_ref[pl.ds(off, TK), :]) * c1 - fr_ref[0, j] * _LOG2E
            if masked:
                t = jnp.where(_causal(i * TQ, j * TK, (TQ, TK), 0), t, _NEG)
            m_new = jnp.maximum(m, jnp.max(t, axis=1, keepdims=True) + fq2)
            alpha = jnp.exp2(m - m_new)
            p = jnp.exp2(t + (fq2 - m_new))
            l = alpha * l + jnp.sum(p, axis=1, keepdims=True)
            acc = alpha * acc + _dot_nn(p.astype(BF16), v_ref[pl.ds(off, TK), :])
            return m_new, l, acc

        init = (jnp.full((TQ, 1), _NEG, F32), jnp.zeros((TQ, 1), F32), jnp.zeros((TQ, FOX_DH), F32))
        last = (i * TQ) // TK
        carry = lax.fori_loop(0, last, lambda j, c: step(j, c, False), init)
        m, l, acc = step(last, carry, True)
        o_ref[...] = (acc / l).astype(o_ref.dtype)
        lse_ref[0, 0] = _col_to_row((m + jnp.log2(l)) * _LN2)

    head = lambda blk: pl.BlockSpec((S, FOX_DH), lambda h, i: (0, blk + h))
    return pl.pallas_call(
        body, name=name,
        out_shape=(jax.ShapeDtypeStruct((S, D_MODEL), BF16), jax.ShapeDtypeStruct((FOX_HEADS, nq, 1, TQ), F32)),
        grid=(FOX_HEADS, nq),
        in_specs=[pl.BlockSpec((TQ, FOX_DH), lambda h, i: (i, _QF_BLK + h)), head(_KF_BLK), head(_VF_BLK),
                  pl.BlockSpec((TQ, LANES), lambda h, i: (i, 0)),
                  pl.BlockSpec((1, nk, 1, TK), lambda h, i: (h, 0, 0, 0))],
        out_specs=(pl.BlockSpec((TQ, FOX_DH), lambda h, i: (i, h)),
                   pl.BlockSpec((1, 1, 1, TQ), lambda h, i: (h, i, 0, 0))),
        compiler_params=_cparams(("parallel", "arbitrary")),
    )(pf, pf, pf, fc, fk_row)


def _fox_bwd(dhb, hb, pf, lse_row, fq_row, fc, name):
    S = pf.shape[0]
    TQ, TK = FOX_TQ, FOX_TK
    nq, nk, r = S // TQ, S // TK, TK // TQ
    c1 = _FOX_SCALE * _LOG2E

    def body(q_ref, k_ref, v_ref, do_ref, o_ref, lse_ref, fq_ref, fc_ref,
             dq_ref, dk_ref, dv_ref, dFk_ref, dFq_ref, dq_acc, qside, delta, dk_acc, dv_acc, cs_acc):
        h, j = pl.program_id(0), pl.program_id(1)

        @pl.when(j == 0)
        def _():
            dq_acc[...] = jnp.zeros_like(dq_acc)
            dFq_ref[...] = jnp.zeros_like(dFq_ref)

            def fill(b, _):
                off = pl.multiple_of(b * TQ, TQ)
                prod = do_ref[pl.ds(off, TQ), :].astype(F32) * o_ref[pl.ds(off, TQ), :].astype(F32)
                delta[b] = jnp.sum(prod.T, axis=0, keepdims=True)
                qside[b] = (fq_ref[0, b] - lse_ref[0, b]) * _LOG2E
                return 0

            lax.fori_loop(0, nq, fill, 0)

        kb = k_ref[...]
        vb = v_ref[...]
        fk2 = _lane_pick(fc_ref[...], h) * _LOG2E
        dk_acc[...] = jnp.zeros_like(dk_acc)
        dv_acc[...] = jnp.zeros_like(dv_acc)
        cs_acc[...] = jnp.zeros_like(cs_acc)

        def step(i, masked):
            off = pl.multiple_of(i * TQ, TQ)
            qb = q_ref[pl.ds(off, TQ), :]
            dob = do_ref[pl.ds(off, TQ), :]
            t = _dot_nt(kb, qb) * c1 + qside[i] - fk2
            if masked:
                t = jnp.where(_causal(i * TQ, j * TK, (TK, TQ), 1), t, _NEG)
            p = jnp.exp2(t)
            dv_acc[...] += _dot_nn(p.astype(BF16), dob)
            ds = p * (_dot_nt(vb, dob) - delta[i])
            dsb = ds.astype(BF16)
            dk_acc[...] += _dot_nn(dsb, qb)
            dq_acc[pl.ds(off, TQ), :] += _dot_tn(dsb, kb)
            cs_acc[...] += jnp.sum(ds, axis=1, keepdims=True)
            dFq_ref[0, i] += jnp.sum(ds, axis=0, keepdims=True)

        for d in range(r):
            step(r * j + d, True)

        def rest(i, _):
            step(i, False)
            return 0

        lax.fori_loop(r * j + r, nq, rest, 0)
        dk_ref[...] = (dk_acc[...] * _FOX_SCALE).astype(dk_ref.dtype)
        dv_ref[...] = dv_acc[...].astype(dv_ref.dtype)
        dFk_ref[0, 0] = -_col_to_row(cs_acc[...])

        @pl.when(j == nk - 1)
        def _():
            dq_ref[...] = (dq_acc[...] * _FOX_SCALE).astype(dq_ref.dtype)

    head = lambda blk: pl.BlockSpec((S, FOX_DH), lambda h, j: (0, blk + h))
    kblk = lambda blk: pl.BlockSpec((TK, FOX_DH), lambda h, j: (j, blk + h))
    qrows = pl.BlockSpec((1, nq, 1, TQ), lambda h, j: (h, 0, 0, 0))
    act = jax.ShapeDtypeStruct((S, D_MODEL), BF16)
    return pl.pallas_call(
        body, name=name,
        out_shape=(act, act, act, jax.ShapeDtypeStruct((FOX_HEADS, nk, 1, TK), F32),
                   jax.ShapeDtypeStruct((FOX_HEADS, nq, 1, TQ), F32)),
        grid=(FOX_HEADS, nk),
        in_specs=[head(_QF_BLK), kblk(_KF_BLK), kblk(_VF_BLK), head(0), head(0), qrows, qrows,
                  pl.BlockSpec((TK, LANES), lambda h, j: (j, 0))],
        out_specs=(head(0), kblk(0), kblk(0), pl.BlockSpec((1, 1, 1, TK), lambda h, j: (h, j, 0, 0)), qrows),
        scratch_shapes=[pltpu.VMEM((S, FOX_DH), F32), pltpu.VMEM((nq, 1, TQ), F32), pltpu.VMEM((nq, 1, TQ), F32),
                        pltpu.VMEM((TK, FOX_DH), F32), pltpu.VMEM((TK, FOX_DH), F32), pltpu.VMEM((TK, 1), F32)],
        compiler_params=_cparams(("parallel", "arbitrary")),
    )(pf, pf, pf, dhb, hb, lse_row, fq_row, fc)


def _pad_lanes(v):
    return jnp.pad(v, ((0, 0), (0, LANES - v.shape[1])))


def _local_step(x, target, wmain_t, wsmall_t, rest_weights, p, on_grads, token):
    S = x.shape[0]
    bi, bf, bff = _pad_lanes(p["b_ml_i"]), _pad_lanes(p["b_ml_f"]), _pad_lanes(p["b_fox_f"])

    h0 = _rmsnorm_fwd(x, p["norm_mix_pre"] + token[0:1, 0:1], "norm_mix_pre")
    pm = _mm(h0, wmain_t[:N_ML], "nt", F32, "proj_mlstm")
    pf = _mm(h0, wmain_t[N_ML:N_ML + N_FOX], "nt", BF16, "proj_fox")
    pg = _mm(h0, wmain_t[N_ML + N_FOX:], "nt", F32, "proj_merge")
    ps = _mm(h0, wsmall_t, "nt", F32, "proj_gates")
    a, A, wi, em, wk, dec, Fc = _gates_fwd(ps, bi, bf, bff, "gates_fwd")
    a_row = a[:, :8].T
    ha, hp, den, cst, nst = _mlstm_fwd(pm, a_row, A, wi, em, wk, dec, p["ml_head_norm"], "mlstm_fwd")
    ft = Fc[:, :FOX_HEADS].T
    fq_row = ft.reshape(FOX_HEADS, S // FOX_TQ, 1, FOX_TQ)
    fk_row = ft.reshape(FOX_HEADS, S // FOX_TK, 1, FOX_TK)
    hb, lse_row = _fox_fwd(pf, Fc, ft.reshape(FOX_HEADS, S // FOX_TK_FWD, 1, FOX_TK_FWD), "fox_fwd")
    wa, wb, wout, wup, wdown = rest_weights(hb)
    ya = _mm(ha, wa, "nn", F32, "branch_a")
    yb = _mm(hb, wb, "nn", F32, "branch_b")
    merged = _merge_fwd(ya, yb, pg, p["b_gate_a"], p["b_gate_b"], "merge_fwd")
    z = _mm(merged, wout, "nn", F32, "out_proj")
    x1 = _resid_norm_fwd(x, z, p["norm_mix_post"], "resid_mix")
    h2 = _rmsnorm_fwd(x1, p["norm_ffn_pre"], "norm_ffn_pre")
    up = _mm(h2, wup, "nn", F32, "ffn_up")
    act, conv_a, conv_g = _conv_act_fwd(up, p["conv_w"], p["conv_b"], "conv_act_fwd")
    d = _mm(act, wdown, "nn", F32, "ffn_down")
    loss_row, dy, dd, g_norm_ffn_post = _loss_head(x1, d, p["norm_ffn_post"], target, "loss_head")
    dact = _mm(dd, wdown, "nt", F32, "d_act")
    g_wdown = _mm(act, dd, "tn", F32, "dw_down", tm=1408)
    dupa, dupg, dcwa, dcwg, dcba, dcbg = _conv_act_bwd(up, conv_a, conv_g, dact, p["conv_w"], "conv_act_bwd")
    g_conv_w = jnp.concatenate([dcwa, dcwg], axis=1)
    g_conv_b = jnp.concatenate([dcba, dcbg], axis=1)
    dh2 = _mm([dupa, dupg], wup, "nt", F32, "d_h2")
    g_wup = _mm(h2, [dupa, dupg], "tn", F32, "dw_up")
    token = on_grads("ffn", dict(w_up=g_wup, w_down=g_wdown))
    dx1, g_norm_ffn_pre = _rmsnorm_bwd([dh2], x1, p["norm_ffn_pre"] + token[0:1, 0:1], dy, F32, "norm_ffn_pre_bwd")
    dz, g_norm_mix_post = _rmsnorm_bwd([dx1], z, p["norm_mix_post"], None, BF16, "norm_mix_post_bwd")
    dmerged = _mm(dz, wout, "nt", F32, "d_merged")
    g_wout = _mm(merged, dz, "tn", F32, "dw_out")
    dya, dyb, dga, dgb, g_b_gate_a, g_b_gate_b = _merge_bwd(dmerged, ya, yb, pg, p["b_gate_a"], p["b_gate_b"], "merge_bwd")
    dha = _mm(dya, wa, "nt", F32, "d_ha")
    g_wa = _mm(ha, dya, "tn", F32, "dw_a")
    dhb = _mm(dyb, wb, "nt", BF16, "d_hb")
    g_wb = _mm(hb, dyb, "tn", F32, "dw_b")
    token = on_grads("mix", dict(w_out=g_wout, w_branch_a=g_wa, w_branch_b=g_wb))
    dqkm, dvm, dom, rk, kc, tch, g_ml_head_norm = _mlstm_bwd(
        dha, pm, hp, den, a_row, A, wi, em, wk, dec, cst, nst, p["ml_head_norm"] + token[0:1, 0:1], "mlstm_bwd")
    dqf, dkf, dvf, dFk, dFq = _fox_bwd(dhb, hb, pf, lse_row.reshape(fq_row.shape), fq_row, Fc, "fox_bwd")
    dF = jnp.pad((dFk.reshape(FOX_HEADS, S) + dFq.reshape(FOX_HEADS, S)).T, ((0, 0), (0, LANES - FOX_HEADS)))
    dps, dbias = _gates_bwd(ps, bi, bf, bff, rk, kc, tch, dF, "gates_bwd")
    dpm = [dqkm, dvm, dom, dqf, dkf, dvf, dga, dgb]
    g_wmain_t = _mm(dpm, h0, "tn", F32, "dw_main")
    token = on_grads("in", dict(w_in=g_wmain_t))
    g_wsmall_t = _mm(dps, h0, "tn", F32, "dw_gates")
    dh0s = _mm(dps, wsmall_t + token[0:1, 0:1].astype(BF16), "nn", F32, "d_h0_gates")
    dpm = list(lax.optimization_barrier((token, *dpm))[1:])
    dh0 = _mm(dpm, wmain_t, "nn", F32, "d_h0_main")
    grad_x, g_norm_mix_pre = _rmsnorm_bwd([dh0, dh0s], x, p["norm_mix_pre"], dx1, F32, "norm_mix_pre_bwd")

    big = dict(wsmall_t=g_wsmall_t)
    small = dict(norm_mix_pre=g_norm_mix_pre, ml_head_norm=g_ml_head_norm, b_gate_a=g_b_gate_a, b_gate_b=g_b_gate_b,
                 norm_mix_post=g_norm_mix_post, norm_ffn_pre=g_norm_ffn_pre, norm_ffn_post=g_norm_ffn_post,
                 conv_b=g_conv_b, b_ml_i=dbias[:, 0:ML_HEADS], b_ml_f=dbias[:, LANES:LANES + ML_HEADS],
                 b_fox_f=dbias[:, 2 * LANES:2 * LANES + FOX_HEADS], conv_w=g_conv_w)
    return loss_row, grad_x, big, small


def _row_tile(r, target=256):
    best = None
    for t in range(8, min(r, target) + 1, 8):
        if r % t == 0:
            best = t
    return best if best is not None else r


def _adamw(w, g, m, v, name):
    _, R, C = w.shape
    tr = _row_tile(R)
    tc = C
    if tr == R and R > 256:
        tc = 256

    def body(w_ref, g_ref, m_ref, v_ref, d_ref, mo_ref, vo_ref):
        gv = g_ref[...]
        mn = ADAM_B1 * m_ref[0] + (1.0 - ADAM_B1) * gv
        vn = ADAM_B2 * v_ref[0] + (1.0 - ADAM_B2) * (gv * gv)
        m_hat = mn / (1.0 - ADAM_B1 ** ADAM_STEP)
        v_hat = vn / (1.0 - ADAM_B2 ** ADAM_STEP)
        d_ref[0] = -ADAM_LR * (m_hat / (jnp.sqrt(v_hat) + ADAM_EPS) + ADAM_WD * w_ref[0])
        mo_ref[0] = mn
        vo_ref[0] = vn

    blk = pl.BlockSpec((1, tr, tc), lambda i, j: (0, i, j))
    o = jax.ShapeDtypeStruct((1, R, C), F32)
    return pl.pallas_call(
        body, name=name, out_shape=(o, o, o), grid=(R // tr, C // tc),
        in_specs=[blk, pl.BlockSpec((tr, tc), lambda i, j: (i, j)), blk, blk], out_specs=(blk,) * 3,
        compiler_params=_cparams(("parallel", "parallel")),
    )(w, g, m, v)


ANY = pl.BlockSpec(memory_space=pl.ANY)


def _place():
    x, y, c = lax.axis_index("x"), lax.axis_index("y"), lax.axis_index("c")
    chips = [(1 - x, y), (x, 1 - y), (1 - x, 1 - y)]
    return x, y, c, chips


def _block(ref, kind, k, rows=None):
    if kind == "rows":
        return ref.at[k] if rows is None else ref.at[k, pl.ds(*rows), :]
    cb = ref.shape[1] // 4
    return ref.at[:, pl.ds(k * cb, cb)] if rows is None else ref.at[pl.ds(*rows), pl.ds(k * cb, cb)]


def _gathered_shape(s, kind):
    return (4,) + s.shape if kind == "rows" else (s.shape[0], 4 * s.shape[1])


def _gather_weights(shards, kinds, smalls):
    n, ns = len(shards), len(smalls)

    def body(*refs):
        ins, sm_in = refs[:n], refs[n:n + ns]
        outs, sm_out = refs[n + ns:2 * n + ns], refs[2 * n + ns:2 * (n + ns)]
        send_sems, recv_sems, sm_send, sm_recv, local_sems = refs[2 * (n + ns):]
        x, y, c, chips = _place()
        sibling = (x, y, 1 - c)
        kme = 2 * x + y

        def half(a, k, hc):
            h = ins[a].shape[0] // 2
            return _block(outs[a], kinds[a], k, (hc * h, h))

        def remote(a, slot, src, dst, to):
            return pltpu.make_async_remote_copy(src_ref=src, dst_ref=dst, send_sem=send_sems.at[a * 7 + slot],
                                                recv_sem=recv_sems.at[a * 7 + slot], device_id=to, device_id_type=MESH)

        def sm_copy(b, j, k, to):
            return pltpu.make_async_remote_copy(src_ref=sm_in[b], dst_ref=sm_out[b].at[k], send_sem=sm_send.at[3 * b + j],
                                                recv_sem=sm_recv.at[3 * b + j], device_id=to, device_id_type=MESH)

        local = [pltpu.make_async_copy(sm_in[b], sm_out[b].at[kme], local_sems.at[b]) for b in range(ns)]
        for cp in local:
            cp.start()
        sends = [remote(a, 6, ins[a], _block(outs[a], kinds[a], kme), sibling) for a in range(n)]
        for a in range(n):
            h = ins[a].shape[0] // 2
            for j, chip in enumerate(chips):
                sends.append(remote(a, j, ins[a].at[pl.ds(c * h, h), :], half(a, kme, c), (*chip, c)))
        for b in range(ns):
            for j, chip in enumerate(chips):
                sends.append(sm_copy(b, j, kme, (*chip, c)))
        for cp in sends:
            cp.start()
        for a in range(n):
            for j, chip in enumerate(chips):
                kj = 2 * chip[0] + chip[1]
                remote(a, j, half(a, kj, c), half(a, kj, c), (*chip, c)).wait_recv()
                fwd = remote(a, 3 + j, half(a, kj, c), half(a, kj, c), sibling)
                fwd.start()
                sends.append(fwd)
        for a in range(n):
            for j, chip in enumerate(chips):
                kj = 2 * chip[0] + chip[1]
                remote(a, 3 + j, half(a, kj, 1 - c), half(a, kj, 1 - c), sibling).wait_recv()
        for b in range(ns):
            for j, chip in enumerate(chips):
                sm_copy(b, j, 2 * chip[0] + chip[1], (*chip, c)).wait_recv()
        for a in range(n):
            remote(a, 6, ins[a], _block(outs[a], kinds[a], kme), sibling).wait_recv()
        for cp in sends:
            cp.wait_send()
        for cp in local:
            cp.wait()

    outs = pl.pallas_call(
        body, name="gather_weights",
        out_shape=tuple([jax.ShapeDtypeStruct(_gathered_shape(s, k), s.dtype) for s, k in zip(shards, kinds)]
                        + [jax.ShapeDtypeStruct((4,) + s.shape, s.dtype) for s in smalls]),
        in_specs=[ANY] * (n + ns), out_specs=tuple([ANY] * (n + ns)),
        scratch_shapes=[pltpu.SemaphoreType.DMA((7 * n,)), pltpu.SemaphoreType.DMA((7 * n,)),
                        pltpu.SemaphoreType.DMA((3 * ns,)), pltpu.SemaphoreType.DMA((3 * ns,)),
                        pltpu.SemaphoreType.DMA((ns,))],
    )(*shards, *smalls)
    return outs[:n], outs[n:]


_IN_HBM = pl.BlockSpec(memory_space=pltpu.HBM)
_SEMS = pl.BlockSpec(memory_space=pltpu.SEMAPHORE)
_DATAFLOW = pltpu.SideEffectType.DATAFLOW_SIDE_EFFECTING


def _hbm(t):
    return pltpu.HBM(t.shape, t.dtype)


def _gather_copies(ins, outs, send_sems, recv_sems, kinds):
    x, y, c, chips = _place()
    kme = 2 * x + y
    cps = []
    for a in range(len(ins)):
        h = ins[a].shape[0] // 2
        for j, chip in enumerate(chips + [None]):
            to = (x, y, 1 - c) if chip is None else (*chip, c)
            src = ins[a] if chip is None else ins[a].at[pl.ds(c * h, h), :]
            dst = _block(outs[a], kinds[a], kme, None if chip is None else (c * h, h))
            cps.append(pltpu.make_async_remote_copy(src_ref=src, dst_ref=dst, send_sem=send_sems.at[4 * a + j],
                                                    recv_sem=recv_sems.at[4 * a + j], device_id=to, device_id_type=MESH))
    return cps


def _gather_start(shards, kinds, name):
    n = len(shards)
    outs = [lax.empty(_gathered_shape(s, k), s.dtype) for s, k in zip(shards, kinds)]

    def body(*refs):
        for cp in _gather_copies(refs[:n], refs[n:2 * n], refs[2 * n], refs[2 * n + 1], kinds):
            cp.start()
        refs[-1][...] = jnp.zeros_like(refs[-1])

    return pl.pallas_call(
        body, name=name,
        out_shape=(pltpu.SemaphoreType.DMA((4 * n,)), pltpu.SemaphoreType.DMA((4 * n,)),
                   *[_hbm(t) for t in shards], *[_hbm(t) for t in outs], jax.ShapeDtypeStruct((8, LANES), F32)),
        in_specs=[_IN_HBM] * (2 * n),
        out_specs=(_SEMS, _SEMS, *[_IN_HBM] * (2 * n), pl.BlockSpec(memory_space=pltpu.VMEM)),
        input_output_aliases={a: 2 + a for a in range(2 * n)},
        compiler_params=pltpu.CompilerParams(has_side_effects=_DATAFLOW),
    )(*[pltpu.with_memory_space_constraint(t, pltpu.HBM) for t in list(shards) + outs])


def _gather_wait(started, after, kinds, name):
    n = (len(started) - 3) // 2
    bufs = started[2:2 + 2 * n]

    def body(*refs):
        for cp in _gather_copies(refs[:n], refs[n:2 * n], refs[2 * n], refs[2 * n + 1], kinds):
            cp.wait_send()
            cp.wait_recv()

    outs = pl.pallas_call(
        body, name=name, out_shape=tuple(_hbm(t) for t in bufs),
        in_specs=[_IN_HBM] * (2 * n) + [_SEMS, _SEMS, ANY], out_specs=tuple([_IN_HBM] * (2 * n)),
        input_output_aliases={a: a for a in range(2 * n)},
        compiler_params=pltpu.CompilerParams(has_side_effects=_DATAFLOW),
    )(*bufs, started[0], started[1], after)
    return outs[n:]


def _gather_relay(bufs, kinds, name):
    n = len(bufs)

    def body(*refs):
        ins, outs, send_sems, recv_sems = refs[:n], refs[n:2 * n], refs[2 * n], refs[2 * n + 1]
        x, y, c, chips = _place()
        cps = []
        for a in range(n):
            h = (ins[a].shape[1] if kinds[a] == "rows" else ins[a].shape[0]) // 2
            for j, chip in enumerate(chips):
                kj = 2 * chip[0] + chip[1]
                cps.append((pltpu.make_async_remote_copy(
                    src_ref=_block(ins[a], kinds[a], kj, (c * h, h)), dst_ref=_block(outs[a], kinds[a], kj, (c * h, h)),
                    send_sem=send_sems.at[3 * a + j], recv_sem=recv_sems.at[3 * a + j], device_id=(x, y, 1 - c),
                    device_id_type=MESH), a, kj, h))
        for cp, _, _, _ in cps:
            cp.start()
        for a_cp, (cp, a, kj, h) in enumerate(cps):
            theirs = _block(outs[a], kinds[a], kj, ((1 - c) * h, h))
            pltpu.make_async_remote_copy(src_ref=theirs, dst_ref=theirs, send_sem=send_sems.at[a_cp],
                                         recv_sem=recv_sems.at[a_cp], device_id=(x, y, 1 - c), device_id_type=MESH).wait_recv()
        for cp, _, _, _ in cps:
            cp.wait_send()

    return pl.pallas_call(
        body, name=name, out_shape=tuple(jax.ShapeDtypeStruct(b.shape, b.dtype) for b in bufs),
        in_specs=[ANY] * n, out_specs=tuple([ANY] * n), input_output_aliases={a: a for a in range(n)},
        scratch_shapes=[pltpu.SemaphoreType.DMA((3 * n,)), pltpu.SemaphoreType.DMA((3 * n,))],
    )(*bufs)


def _exchange_sibling_halves(gs, kinds, name):
    n = len(gs)
    hshape = lambda g, kind: (4, g.shape[1] // 2, g.shape[2]) if kind == "rows" else (g.shape[0] // 2, g.shape[1])

    def body(*refs):
        ins, outs, send_sems, recv_sems = refs[:n], refs[n:2 * n], refs[2 * n], refs[2 * n + 1]
        x, y, c, _ = _place()
        cps = []
        for a in range(n):
            h = outs[a].shape[-2]
            src = ins[a].at[:, pl.ds((1 - c) * h, h), :] if kinds[a] == "rows" else ins[a].at[pl.ds((1 - c) * h, h), :]
            cps.append(pltpu.make_async_remote_copy(
                src_ref=src, dst_ref=outs[a], send_sem=send_sems.at[a],
                recv_sem=recv_sems.at[a], device_id=(x, y, 1 - c), device_id_type=MESH))
        for cp in cps:
            cp.start()
        for cp in cps:
            cp.wait()

    return pl.pallas_call(
        body, name=name,
        out_shape=tuple(jax.ShapeDtypeStruct(hshape(g, k), g.dtype) for g, k in zip(gs, kinds)),
        in_specs=[ANY] * n, out_specs=tuple([ANY] * n),
        scratch_shapes=[pltpu.SemaphoreType.DMA((n,)), pltpu.SemaphoreType.DMA((n,))],
    )(*gs)


def _add_halves(g, r1, cvec, kind, name):
    def body(c_ref, g_ref, r_ref, o_ref):
        o_ref[...] = (g_ref[...] + r_ref[...]).astype(o_ref.dtype)

    if kind == "rows":
        _, h, C = r1.shape
        tr = _row_tile(h)
        nt = h // tr
        grid = (4, nt)
        g_spec = pl.BlockSpec((1, tr, C), lambda k, i, c_ref: (k, c_ref[0] * nt + i, 0))
        r_spec = pl.BlockSpec((1, tr, C), lambda k, i, c_ref: (k, i, 0))
    else:
        h, C4 = r1.shape
        tr, tc = _row_tile(h), C4 // 4
        nt = h // tr
        grid = (nt, 4)
        g_spec = pl.BlockSpec((tr, tc), lambda i, k, c_ref: (c_ref[0] * nt + i, k))
        r_spec = pl.BlockSpec((tr, tc), lambda i, k, c_ref: (i, k))
    return pl.pallas_call(
        body, name=name, out_shape=jax.ShapeDtypeStruct(r1.shape, BF16),
        grid_spec=pltpu.PrefetchScalarGridSpec(num_scalar_prefetch=1, grid=grid, in_specs=[g_spec, r_spec],
                                               out_specs=r_spec),
        compiler_params=_cparams(("parallel", "parallel")),
    )(cvec, g, r1)


def _chip_copies(ins, lands, send_sems, recv_sems, kinds):
    x, y, c, chips = _place()
    return [pltpu.make_async_remote_copy(
        src_ref=_block(ins[a], kinds[a], 2 * chip[0] + chip[1]), dst_ref=lands[a].at[j],
        send_sem=send_sems.at[3 * a + j], recv_sem=recv_sems.at[3 * a + j], device_id=(*chip, c), device_id_type=MESH)
        for a in range(len(ins)) for j, chip in enumerate(chips)]


def _land_shape(s, kind):
    return (3,) + (s.shape[1:] if kind == "rows" else (s.shape[0], s.shape[1] // 4))


def _exchange_chips_start(ss, kinds, name):
    n = len(ss)
    lands = [lax.empty(_land_shape(s, k), s.dtype) for s, k in zip(ss, kinds)]

    def body(*refs):
        for cp in _chip_copies(refs[:n], refs[n:2 * n], refs[2 * n], refs[2 * n + 1], kinds):
            cp.start()
        refs[-1][...] = jnp.zeros_like(refs[-1])

    hbm = _hbm
    return pl.pallas_call(
        body, name=name,
        out_shape=(pltpu.SemaphoreType.DMA((3 * n,)), pltpu.SemaphoreType.DMA((3 * n,)),
                   *[hbm(t) for t in ss], *[hbm(t) for t in lands], jax.ShapeDtypeStruct((8, LANES), F32)),
        in_specs=[_IN_HBM] * (2 * n),
        out_specs=(_SEMS, _SEMS, *[_IN_HBM] * (2 * n), pl.BlockSpec(memory_space=pltpu.VMEM)),
        input_output_aliases={a: 2 + a for a in range(2 * n)},
        compiler_params=pltpu.CompilerParams(has_side_effects=_DATAFLOW),
    )(*[pltpu.with_memory_space_constraint(t, pltpu.HBM) for t in list(ss) + lands])


def _exchange_chips_wait(started, after, kinds, name):
    send_sems, recv_sems = started[0], started[1]
    n = (len(started) - 3) // 2
    bufs = started[2:2 + 2 * n]

    def body(*refs):
        for cp in _chip_copies(refs[:n], refs[n:2 * n], refs[2 * n], refs[2 * n + 1], kinds):
            cp.wait_send()
            cp.wait_recv()

    hbm = _hbm
    outs = pl.pallas_call(
        body, name=name, out_shape=tuple(hbm(t) for t in bufs),
        in_specs=[_IN_HBM] * (2 * n) + [_SEMS, _SEMS, ANY], out_specs=tuple([_IN_HBM] * (2 * n)),
        input_output_aliases={a: a for a in range(2 * n)},
        compiler_params=pltpu.CompilerParams(has_side_effects=_DATAFLOW),
    )(*bufs, send_sems, recv_sems, after)
    return outs[:n], outs[n:]


def _add_chips(s1, r2, kcvec, kind, name):
    _, h, C = r2.shape
    tr = _row_tile(h)
    nt = h // tr

    def body(kc_ref, s_ref, r0_ref, r1_ref, r2_ref, o_ref):
        s = s_ref[0] if kind == "rows" else s_ref[...]
        o_ref[...] = ((s.astype(F32) + r0_ref[0].astype(F32)) + r1_ref[0].astype(F32)) + r2_ref[0].astype(F32)

    peer = lambda j: pl.BlockSpec((1, tr, C), lambda i, kc_ref: (j, i, 0))
    if kind == "rows":
        s_spec = pl.BlockSpec((1, tr, C), lambda i, kc_ref: (kc_ref[0], i, 0))
    else:
        s_spec = pl.BlockSpec((tr, C), lambda i, kc_ref: (i, kc_ref[0]))
    return pl.pallas_call(
        body, name=name, out_shape=jax.ShapeDtypeStruct((2 * h, C), F32),
        grid_spec=pltpu.PrefetchScalarGridSpec(
            num_scalar_prefetch=1, grid=(nt,),
            in_specs=[s_spec, peer(0), peer(1), peer(2)],
            out_specs=pl.BlockSpec((tr, C), lambda i, kc_ref: (kc_ref[1] * nt + i, 0))),
        compiler_params=_cparams(("parallel",)),
    )(kcvec, s1, r2, r2, r2)


def _join_sibling_halves(bufs):
    n = len(bufs)

    def body(*refs):
        ins, outs, send_sems, recv_sems = refs[:n], refs[n:2 * n], refs[2 * n], refs[2 * n + 1]
        x, y, c, _ = _place()
        cps = []
        for a in range(n):
            h = ins[a].shape[0] // 2
            cps.append(pltpu.make_async_remote_copy(
                src_ref=ins[a].at[pl.ds(c * h, h), :], dst_ref=outs[a].at[pl.ds(c * h, h), :], send_sem=send_sems.at[a],
                recv_sem=recv_sems.at[a], device_id=(x, y, 1 - c), device_id_type=MESH))
        for cp in cps:
            cp.start()
        for a in range(n):
            h = ins[a].shape[0] // 2
            theirs = outs[a].at[pl.ds((1 - c) * h, h), :]
            pltpu.make_async_remote_copy(src_ref=theirs, dst_ref=theirs, send_sem=send_sems.at[a],
                                         recv_sem=recv_sems.at[a], device_id=(x, y, 1 - c), device_id_type=MESH).wait_recv()
        for cp in cps:
            cp.wait_send()

    return pl.pallas_call(
        body, name="grads_join",
        out_shape=tuple(jax.ShapeDtypeStruct(b.shape, b.dtype) for b in bufs),
        in_specs=[ANY] * n, out_specs=tuple([ANY] * n), input_output_aliases={a: a for a in range(n)},
        scratch_shapes=[pltpu.SemaphoreType.DMA((n,)), pltpu.SemaphoreType.DMA((n,))],
    )(*bufs)


N_DEV = 8


def _allreduce_small(pack):
    P = pack.shape[0]

    def body(p_ref, o_ref, gath, send_sems, recv_sems):
        x, y, c, _ = _place()
        me = 4 * x + 2 * y + c
        cps = []
        for mask in range(1, N_DEV):
            px = 1 - x if mask & 4 else x
            py = 1 - y if mask & 2 else y
            pc = 1 - c if mask & 1 else c
            cps.append((pltpu.make_async_remote_copy(
                src_ref=p_ref, dst_ref=gath.at[me], send_sem=send_sems.at[mask - 1], recv_sem=recv_sems.at[mask - 1],
                device_id=(px, py, pc), device_id_type=MESH), 4 * px + 2 * py + pc, mask))
        for cp, _, _ in cps:
            cp.start()
        gath[me] = p_ref[...]
        for _, peer, mask in cps:
            pltpu.make_async_remote_copy(
                src_ref=p_ref, dst_ref=gath.at[peer], send_sem=send_sems.at[mask - 1], recv_sem=recv_sems.at[mask - 1],
                device_id=(x, y, c), device_id_type=MESH).wait_recv()
        for cp, _, _ in cps:
            cp.wait_send()
        acc = gath[0]
        for i in range(1, N_DEV):
            acc = acc + gath[i]
        o_ref[...] = acc

    return pl.pallas_call(
        body, name="allreduce_small", out_shape=jax.ShapeDtypeStruct((P, LANES), F32),
        in_specs=[pl.BlockSpec(memory_space=pltpu.VMEM)], out_specs=pl.BlockSpec(memory_space=pltpu.VMEM),
        scratch_shapes=[pltpu.VMEM((N_DEV, P, LANES), F32), pltpu.SemaphoreType.DMA((N_DEV - 1,)),
                        pltpu.SemaphoreType.DMA((N_DEV - 1,))],
    )(pack)


def _pack_rows(arrs):
    rows = []
    for a in arrs:
        f = a.reshape(-1)
        f = jnp.pad(f, (0, (-f.shape[0]) % (8 * LANES)))
        rows.append(f.reshape(-1, LANES))
    return jnp.concatenate(rows, axis=0)


def _unpack_rows(pack, shapes):
    out, r = [], 0
    for s in shapes:
        n = math.prod(s)
        out.append(pack[r:r + -(-n // LANES)].reshape(-1)[:n].reshape(s))
        r += 8 * -(-n // (8 * LANES))
    return out


_SMALL = ["norm_mix_pre", "ml_head_norm", "b_gate_a", "b_gate_b", "norm_mix_post", "norm_ffn_pre", "norm_ffn_post",
          "conv_b", "b_ml_i", "b_ml_f", "b_fox_f"]
_BIG = ["w_in", "w_branch_a", "w_branch_b", "w_out", "w_up", "w_down"]
_WEIGHTS = ['norm_mix_pre', 'w_in', 'b_ml_i', 'b_ml_f', 'ml_head_norm', 'b_fox_f', 'b_gate_a', 'b_gate_b', 'w_branch_a',
            'w_branch_b', 'w_out', 'norm_mix_post', 'norm_ffn_pre', 'w_up', 'conv_w', 'conv_b', 'w_down', 'norm_ffn_post']


_KINDS = ["rows", "rows", "rows", "rows", "cols", "rows"]


def kernel(x, norm_mix_pre, w_in, b_ml_i, b_ml_f, ml_head_norm, b_fox_f, b_gate_a, b_gate_b, w_branch_a, w_branch_b, w_out, norm_mix_post, norm_ffn_pre, w_up, conv_w, conv_b, w_down, norm_ffn_post, loss_target, m_norm_mix_pre, m_w_in, m_b_ml_i, m_b_ml_f, m_ml_head_norm, m_b_fox_f, m_b_gate_a, m_b_gate_b, m_w_branch_a, m_w_branch_b, m_w_out, m_norm_mix_post, m_norm_ffn_pre, m_w_up, m_conv_w, m_conv_b, m_w_down, m_norm_ffn_post, v_norm_mix_pre, v_w_in, v_b_ml_i, v_b_ml_f, v_ml_head_norm, v_b_fox_f, v_b_gate_a, v_b_gate_b, v_w_branch_a, v_w_branch_b, v_w_out, v_norm_mix_post, v_norm_ffn_pre, v_w_up, v_conv_w, v_conv_b, v_w_down, v_norm_ffn_post):
    args = dict(locals())
    w = {n: args[n] for n in _WEIGHTS}
    mom = {n: args["m_" + n] for n in _WEIGHTS}
    var = {n: args["v_" + n] for n in _WEIGHTS}
    cx, cy, cc = lax.axis_index("x"), lax.axis_index("y"), lax.axis_index("c")
    kme = 2 * cx + cy
    cvec = jnp.reshape(cc, (1,)).astype(jnp.int32)
    kcvec = jnp.stack([kme, cc]).astype(jnp.int32)
    odd = kme % 2

    tr3 = lambda t: jnp.transpose(t, (0, 2, 1))
    w["w_in"], mom["w_in"], var["w_in"] = tr3(w_in), tr3(m_w_in), tr3(v_w_in)
    w_in_main = lax.dynamic_slice_in_dim(w["w_in"][0], 4 * odd, 2048, axis=0).astype(BF16)
    w_in_gates = lax.dynamic_slice_in_dim(w["w_in"][0], 2048 * (1 - odd), 4, axis=0).astype(BF16)
    (wmain_t,), (g_cw, g_gates) = _gather_weights([w_in_main], _KINDS[:1], [w["conv_w"][0], w_in_gates])
    rest_started = _gather_start([w[n][0].astype(BF16) for n in _BIG[1:]], _KINDS[1:], "gather_rest_start")

    def rest_weights(after):
        bufs = _gather_wait(rest_started, after, _KINDS[1:], "gather_rest_wait")
        g_a, g_b, g_out, wup, g_down = _gather_relay(bufs, _KINDS[1:], "gather_rest_relay")
        return full(g_a), full(g_b), full(g_out), wup, full(g_down)
    gate_rows = g_gates.reshape(16, D_MODEL)
    wsmall_t = jnp.zeros((N_SMALL, D_MODEL), BF16)
    for blk, (lo, hi) in enumerate(((0, 4), (4, 8), (8, 16))):
        wsmall_t = wsmall_t.at[blk * LANES:blk * LANES + hi - lo].set(gate_rows[lo:hi])
    full = lambda g: g.reshape(-1, g.shape[2])
    p = {n: w[n] for n in _SMALL}
    p["conv_w"] = jnp.transpose(g_cw, (1, 0, 2)).reshape(3, -1)

    groups = []

    def on_grads(group, gs):
        names = list(gs)
        kinds = [_KINDS[_BIG.index(n)] for n in names]
        whole = [g if k == "cols" else g.reshape(4, -1, g.shape[1]) for g, k in zip(gs.values(), kinds)]
        from_sibling = _exchange_sibling_halves(whole, kinds, "grads_to_sibling_" + group)
        sums = [_add_halves(g, r, cvec, k, "add_sibling_" + n) for g, r, k, n in zip(whole, from_sibling, kinds, names)]
        started = _exchange_chips_start(sums, kinds, "grads_to_chips_start_" + group)
        groups.append((group, names, kinds, started))
        return started[-1]

    loss_row, grad_x, big, small = _local_step(x[0], loss_target[0], full(wmain_t), wsmall_t, rest_weights, p, on_grads,
                                               rest_started[-1])
    grads = {}
    mine, mine_names = [], []
    for group, names, kinds, started in groups:
        sums, got = _exchange_chips_wait(started, grad_x, kinds, "grads_to_chips_wait_" + group)
        mine += [_add_chips(s, r, kcvec, k, "add_chips_" + n) for s, r, k, n in zip(sums, got, kinds, names)]
        mine_names += names
    grads.update(zip(mine_names, _join_sibling_halves(mine)))

    gt = big["wsmall_t"]
    small["w_in_gates"] = jnp.concatenate([gt[0:4], gt[LANES:LANES + 4], gt[2 * LANES:2 * LANES + 8]], axis=0)
    small_names = _SMALL + ["conv_w"]
    packed_names = small_names + ["w_in_gates"]
    pack = _pack_rows([small[n] for n in packed_names] + [loss_row])
    pack = jnp.pad(pack, ((0, (-pack.shape[0]) % 8), (0, 0)))
    full_shapes = [small[n].shape if n in ("conv_w", "w_in_gates") else w[n][0].shape for n in packed_names]
    total = _unpack_rows(_allreduce_small(pack), full_shapes + [loss_row.shape])
    for n, t in zip(packed_names, total):
        grads[n] = t
    loss = total[-1][0, 0]
    grads["conv_w"] = lax.dynamic_slice_in_dim(grads["conv_w"], kme * conv_w.shape[2], conv_w.shape[2], axis=1)
    my_gates = lax.dynamic_slice_in_dim(grads.pop("w_in_gates"), 4 * kme, 4, axis=0)
    g_in = jnp.zeros(w["w_in"].shape[1:], F32)
    g_in = lax.dynamic_update_slice_in_dim(g_in, grads["w_in"], 4 * odd, axis=0)
    grads["w_in"] = lax.dynamic_update_slice_in_dim(g_in, my_gates, 2048 * (1 - odd), axis=0)

    delta, new_m, new_v = {}, {}, {}
    for n in _BIG:
        delta[n], new_m[n], new_v[n] = _adamw(w[n], grads[n], mom[n], var[n], "adamw_" + n)
        grads[n] = grads[n][None]
    for d in (grads, delta, new_m, new_v):
        d["w_in"] = tr3(d["w_in"])
    packs = [_pack_rows([d[n][0] for n in small_names]) for d in (w, mom, var)]
    pad = ((0, (-packs[0].shape[0]) % 8), (0, 0))
    packs = [jnp.pad(t, pad)[None] for t in packs]
    gp = jnp.pad(_pack_rows([grads[n] for n in small_names]), pad)
    shapes = [w[n][0].shape for n in small_names]
    for dst, res in zip((delta, new_m, new_v), _adamw(packs[0], gp, packs[1], packs[2], "adamw_small")):
        for n, t in zip(small_names, _unpack_rows(res[0], shapes)):
            dst[n] = t[None]
    for n in small_names:
        grads[n] = grads[n][None]

    return (loss, grad_x[None], *[grads[n] for n in _WEIGHTS], *[delta[n] for n in _WEIGHTS],
            *[new_m[n] for n in _WEIGHTS], *[new_v[n] for n in _WEIGHTS])
```

```python
import functools
import math

import jax
import jax.numpy as jnp
from jax import lax
from jax.experimental import pallas as pl
from jax.experimental.pallas import tpu as pltpu

F32 = jnp.float32
BF16 = jnp.bfloat16
MESH = pl.DeviceIdType.MESH

D_MODEL = 1024
ML_HEADS = 4
ML_DQK = 128
ML_DV = 256
FOX_HEADS = 8
FOX_DH = 128
D_FF = 2816
GATE_CAP = 15.0
EPS = 1e-6
ADAM_LR, ADAM_B1, ADAM_B2, ADAM_EPS, ADAM_WD, ADAM_STEP = 0.001, 0.9, 0.999, 1e-08, 0.01, 10

LANES = 128
MLC = 128
FOX_TQ = 512
FOX_TQ_FWD = 512
FOX_TK = 512
FOX_TK_FWD = 512
ROW_T = 512
VMEM_LIMIT = 56 * 1024 * 1024

C_QM, C_KM, C_VM, C_OM = 0, 512, 1024, 2048
N_ML, N_FOX, N_GATE = 3072, 3072, 2048
N_SMALL = 384


def _cparams(sem=None):
    return pltpu.CompilerParams(dimension_semantics=sem, vmem_limit_bytes=VMEM_LIMIT)


def _tile(n, target):
    if n <= target:
        return n
    best = None
    for t in range(LANES, target + 1, LANES):
        if n % t == 0:
            best = t
    assert best is not None, (n, target)
    return best


def _dot(a, b, dims):
    return lax.dot_general(a, b, (dims, ((), ())), preferred_element_type=F32)


def _dot_nn(a, b):
    return _dot(a, b, ((1,), (0,)))


def _dot_nt(a, b):
    return _dot(a, b, ((1,), (1,)))


def _dot_tn(a, b):
    return _dot(a, b, ((0,), (0,)))


_DOTS = {"nn": _dot_nn, "nt": _dot_nt, "tn": _dot_tn}


def _mm(a, b, mode, out_dtype, name, tm=1024, tn=1408, tk=1408):
    a_parts = list(a) if isinstance(a, (list, tuple)) else [a]
    b_parts = list(b) if isinstance(b, (list, tuple)) else [b]
    assert len(a_parts) == 1 or len(b_parts) == 1, name
    a_axes = {"nn": "ik", "nt": "ik", "tn": "ki"}[mode]
    b_axes = {"nn": "kj", "nt": "jk", "tn": "kj"}[mode]
    size, target = {}, dict(i=tm, j=tn, k=tk)
    for parts, axes in ((a_parts, a_axes), (b_parts, b_axes)):
        dims = (parts[0].shape[0], parts[0].shape[1] * len(parts))
        for ax, n in zip(axes, dims):
            assert size.setdefault(ax, n) == n, (name, ax, n, size)
    tile = {}
    for parts, axes in ((a_parts, a_axes), (b_parts, b_axes)):
        if len(parts) > 1:
            tile[axes[1]] = _tile(parts[0].shape[1], target[axes[1]])
    for ax in "ijk":
        tile.setdefault(ax, _tile(size[ax], target[ax]))
    M, N, nk = size["i"], size["j"], size["k"] // tile["k"]
    grid_pos = dict(i=0, j=1, k=2)
    dot = _DOTS[mode]

    def specs(parts, axes):
        blk = (tile[axes[0]], tile[axes[1]])
        if len(parts) == 1:
            return [pl.BlockSpec(blk, lambda *g: (g[grid_pos[axes[0]]], g[grid_pos[axes[1]]]))], None
        bpp = parts[0].shape[1] // blk[1]

        def index(p):
            def f(*g):
                g0, g1 = g[grid_pos[axes[0]]], g[grid_pos[axes[1]]]
                on = g1 // bpp == p
                return jnp.where(on, g0, 0), jnp.where(on, g1 % bpp, 0)
            return f

        return [pl.BlockSpec(blk, index(p)) for p in range(len(parts))], (axes[1], bpp)

    a_specs, a_sel = specs(a_parts, a_axes)
    b_specs, b_sel = specs(b_parts, b_axes)
    na, nb = len(a_parts), len(b_parts)

    def body(*refs):
        a_refs, b_refs, o_ref, acc = refs[:na], refs[na:na + nb], refs[na + nb], refs[na + nb + 1:]

        def accumulate(part):
            if nk == 1:
                o_ref[...] = part.astype(o_ref.dtype)
                return
            acc_ref, = acc
            k = pl.program_id(2)

            @pl.when(k == 0)
            def _():
                acc_ref[...] = part

            @pl.when(k > 0)
            def _():
                acc_ref[...] += part

            @pl.when(k == nk - 1)
            def _():
                o_ref[...] = acc_ref[...].astype(o_ref.dtype)

        sel = a_sel or b_sel
        if sel is None:
            accumulate(dot(a_refs[0][...], b_refs[0][...]))
        else:
            which = pl.program_id(grid_pos[sel[0]]) // sel[1]
            for p in range(max(na, nb)):
                @pl.when(which == p)
                def _(p=p):
                    accumulate(dot(a_refs[p if a_sel else 0][...], b_refs[p if b_sel else 0][...]))

    return pl.pallas_call(
        body, name=name,
        out_shape=jax.ShapeDtypeStruct((M, N), out_dtype),
        grid=(M // tile["i"], N // tile["j"], nk),
        in_specs=a_specs + b_specs,
        out_specs=pl.BlockSpec((tile["i"], tile["j"]), lambda i, j, k: (i, j)),
        scratch_shapes=[pltpu.VMEM((tile["i"], tile["j"]), F32)] if nk > 1 else [],
        compiler_params=_cparams(("parallel", "parallel", "arbitrary")),
    )(*a_parts, *b_parts)


def _rstd(x):
    return lax.rsqrt(jnp.mean(x * x, axis=-1, keepdims=True) + EPS)


def _rmsnorm_fwd(x, g, name):
    S, D = x.shape
    T = _tile(S, ROW_T)

    def body(x_ref, g_ref, o_ref):
        xv = x_ref[...]
        o_ref[...] = (xv * _rstd(xv) * g_ref[...]).astype(o_ref.dtype)

    return pl.pallas_call(
        body, name=name, out_shape=jax.ShapeDtypeStruct((S, D), BF16), grid=(S // T,),
        in_specs=[pl.BlockSpec((T, D), lambda i: (i, 0)), pl.BlockSpec((1, D), lambda i: (0, 0))],
        out_specs=pl.BlockSpec((T, D), lambda i: (i, 0)),
        compiler_params=_cparams(("parallel",)),
    )(x, g)


def _resid_norm_fwd(x, z, g, name):
    S, D = x.shape
    T = _tile(S, ROW_T)

    def body(x_ref, z_ref, g_ref, o_ref):
        zv = z_ref[...]
        o_ref[...] = x_ref[...] + zv * _rstd(zv) * g_ref[...]

    row = pl.BlockSpec((T, D), lambda i: (i, 0))
    return pl.pallas_call(
        body, name=name, out_shape=jax.ShapeDtypeStruct((S, D), F32), grid=(S // T,),
        in_specs=[row, row, pl.BlockSpec((1, D), lambda i: (0, 0))],
        out_specs=row, compiler_params=_cparams(("parallel",)),
    )(x, z, g)


def _rmsnorm_bwd_math(dy, xv, g):
    r = _rstd(xv)
    u = dy * g
    dx = r * u - xv * (r * r * r) * jnp.mean(u * xv, axis=-1, keepdims=True)
    return dx, dy * xv * r


def _rmsnorm_bwd(dys, xin, g, resid, out_dtype, name):
    S, D = xin.shape
    T = _tile(S, ROW_T)
    has_resid = resid is not None
    ndy = len(dys)

    def body(*refs):
        dy_refs, (x_ref, g_ref) = refs[:ndy], refs[ndy:ndy + 2]
        dx_ref, dg_ref = refs[-2:]
        dy = dy_refs[0][...]
        for r in dy_refs[1:]:
            dy = dy + r[...]
        dx, dgt = _rmsnorm_bwd_math(dy, x_ref[...], g_ref[...])
        if has_resid:
            dx = dx + refs[ndy + 2][...]
        dx_ref[...] = dx.astype(dx_ref.dtype)

        @pl.when(pl.program_id(0) == 0)
        def _():
            dg_ref[...] = jnp.zeros_like(dg_ref)

        dg_ref[...] += jnp.sum(dgt, axis=0, keepdims=True)

    row = pl.BlockSpec((T, D), lambda i: (i, 0))
    vec = pl.BlockSpec((1, D), lambda i: (0, 0))
    ins = list(dys) + [xin, g] + ([resid] if has_resid else [])
    return pl.pallas_call(
        body, name=name,
        out_shape=(jax.ShapeDtypeStruct((S, D), out_dtype), jax.ShapeDtypeStruct((1, D), F32)),
        grid=(S // T,), in_specs=[row] * ndy + [row, vec] + ([row] if has_resid else []),
        out_specs=(row, vec), compiler_params=_cparams(("arbitrary",)),
    )(*ins)


def _loss_head(x1, d, g, target, name):
    S, D = x1.shape
    T = _tile(S, ROW_T)

    def body(x_ref, d_ref, g_ref, t_ref, loss_ref, dy_ref, dd_ref, dg_ref):
        dv, gv = d_ref[...], g_ref[...]
        y = x_ref[...] + dv * _rstd(dv) * gv
        diff = y - t_ref[...]
        dy = diff * (1.0 / D)
        dy_ref[...] = dy
        dd, dgt = _rmsnorm_bwd_math(dy, dv, gv)
        dd_ref[...] = dd.astype(dd_ref.dtype)

        @pl.when(pl.program_id(0) == 0)
        def _():
            dg_ref[...] = jnp.zeros_like(dg_ref)
            loss_ref[...] = jnp.zeros_like(loss_ref)

        dg_ref[...] += jnp.sum(dgt, axis=0, keepdims=True)
        part = jnp.sum(jnp.sum(diff * diff, axis=1, keepdims=True), axis=0, keepdims=True)
        loss_ref[...] += (0.5 / D) * part

    row = pl.BlockSpec((T, D), lambda i: (i, 0))
    vec = pl.BlockSpec((1, D), lambda i: (0, 0))
    return pl.pallas_call(
        body, name=name,
        out_shape=(jax.ShapeDtypeStruct((1, LANES), F32), jax.ShapeDtypeStruct((S, D), F32),
                   jax.ShapeDtypeStruct((S, D), BF16), jax.ShapeDtypeStruct((1, D), F32)),
        grid=(S // T,), in_specs=[row, row, vec, row],
        out_specs=(pl.BlockSpec((1, LANES), lambda i: (0, 0)), row, row, vec),
        compiler_params=_cparams(("arbitrary",)),
    )(x1, d, g, target)


def _merge_fwd(ya, yb, pm, ba, bb, name):
    S, D = ya.shape
    T = _tile(S, ROW_T)

    def body(ya_ref, yb_ref, ga_ref, gb_ref, ba_ref, bb_ref, o_ref):
        sa = jax.nn.sigmoid(ga_ref[...] + ba_ref[...])
        sb = jax.nn.sigmoid(gb_ref[...] + bb_ref[...])
        o_ref[...] = (sa * ya_ref[...] + sb * yb_ref[...]).astype(o_ref.dtype)

    row = pl.BlockSpec((T, D), lambda i: (i, 0))
    vec = pl.BlockSpec((1, D), lambda i: (0, 0))
    return pl.pallas_call(
        body, name=name, out_shape=jax.ShapeDtypeStruct((S, D), BF16), grid=(S // T,),
        in_specs=[row, row, pl.BlockSpec((T, D), lambda i: (i, 0)),
                  pl.BlockSpec((T, D), lambda i: (i, 1)), vec, vec],
        out_specs=row, compiler_params=_cparams(("parallel",)),
    )(ya, yb, pm, pm, ba, bb)


def _merge_bwd(dmerged, ya, yb, pm, ba, bb, name):
    S, D = ya.shape
    T = _tile(S, ROW_T)

    def body(dm_ref, ya_ref, yb_ref, ga_ref, gb_ref, ba_ref, bb_ref,
             dya_ref, dyb_ref, dga_ref, dgb_ref, dba_ref, dbb_ref):
        dm = dm_ref[...]
        sa = jax.nn.sigmoid(ga_ref[...] + ba_ref[...])
        sb = jax.nn.sigmoid(gb_ref[...] + bb_ref[...])
        dya_ref[...] = (dm * sa).astype(dya_ref.dtype)
        dyb_ref[...] = (dm * sb).astype(dyb_ref.dtype)
        dga = dm * ya_ref[...] * sa * (1.0 - sa)
        dgb = dm * yb_ref[...] * sb * (1.0 - sb)
        dga_ref[...] = dga.astype(dga_ref.dtype)
        dgb_ref[...] = dgb.astype(dgb_ref.dtype)

        @pl.when(pl.program_id(0) == 0)
        def _():
            dba_ref[...] = jnp.zeros_like(dba_ref)
            dbb_ref[...] = jnp.zeros_like(dbb_ref)

        dba_ref[...] += jnp.sum(dga, axis=0, keepdims=True)
        dbb_ref[...] += jnp.sum(dgb, axis=0, keepdims=True)

    row = pl.BlockSpec((T, D), lambda i: (i, 0))
    vec = pl.BlockSpec((1, D), lambda i: (0, 0))
    act = jax.ShapeDtypeStruct((S, D), BF16)
    v1 = jax.ShapeDtypeStruct((1, D), F32)
    return pl.pallas_call(
        body, name=name, out_shape=(act, act, act, act, v1, v1), grid=(S // T,),
        in_specs=[row, row, row, pl.BlockSpec((T, D), lambda i: (i, 0)),
                  pl.BlockSpec((T, D), lambda i: (i, 1)), vec, vec],
        out_specs=(row, row, row, row, vec, vec), compiler_params=_cparams(("arbitrary",)),
    )(dmerged, ya, yb, pm, pm, ba, bb)


_GELU_C = math.sqrt(2.0 / math.pi)


def _gelu(g):
    t = jnp.tanh(_GELU_C * (g + 0.044715 * g * g * g))
    return 0.5 * g * (1.0 + t), t


def _gelu_grad(g, t):
    return 0.5 * (1.0 + t) + 0.5 * g * (1.0 - t * t) * _GELU_C * (1.0 + 3 * 0.044715 * g * g)


def _shift_down(v, halo_ref, first, rows):
    T = v.shape[0]
    keep = jnp.where(first, 0.0, 1.0)
    h7 = halo_ref[7:8, :] * keep
    h6 = halo_ref[6:7, :] * keep
    m1 = jnp.where(rows == 0, h7, pltpu.roll(v, 1, 0))
    m2 = jnp.where(rows == 0, h6, jnp.where(rows == 1, h7, pltpu.roll(v, 2, 0)))
    return m1, m2


def _conv_act_fwd(up, cw, cb, name):
    S, F2 = up.shape
    Fh = F2 // 2
    T = _tile(S, ROW_T)
    tc = _tile(Fh, 256)
    ncol = Fh // tc
    hb = T // 8

    def body(ua_ref, ug_ref, ha_ref, hg_ref, wa_ref, wg_ref, ba_ref, bg_ref, o_ref, a_ref, g_ref):
        first = pl.program_id(0) == 0
        rows = lax.broadcasted_iota(jnp.int32, (T, tc), 0)

        def conv(u_ref, h_ref, w_ref, b_ref):
            v = u_ref[...]
            m1, m2 = _shift_down(v, h_ref, first, rows)
            return b_ref[...] + w_ref[0:1, :] * m2 + w_ref[1:2, :] * m1 + w_ref[2:3, :] * v

        a = conv(ua_ref, ha_ref, wa_ref, ba_ref)
        g = conv(ug_ref, hg_ref, wg_ref, bg_ref)
        a_ref[...] = a
        g_ref[...] = g
        o_ref[...] = (_gelu(g)[0] * a).astype(o_ref.dtype)

    halo = lambda off: pl.BlockSpec((8, tc), lambda i, j: (jnp.maximum(i * hb - 1, 0), j + off))
    blk = pl.BlockSpec((T, tc), lambda i, j: (i, j))
    f32 = jax.ShapeDtypeStruct((S, Fh), F32)
    return pl.pallas_call(
        body, name=name, out_shape=(jax.ShapeDtypeStruct((S, Fh), BF16), f32, f32), grid=(S // T, ncol),
        in_specs=[blk, pl.BlockSpec((T, tc), lambda i, j: (i, j + ncol)),
                  halo(0), halo(ncol),
                  pl.BlockSpec((3, tc), lambda i, j: (0, j)), pl.BlockSpec((3, tc), lambda i, j: (0, j + ncol)),
                  pl.BlockSpec((1, tc), lambda i, j: (0, j)), pl.BlockSpec((1, tc), lambda i, j: (0, j + ncol))],
        out_specs=(blk, blk, blk),
        compiler_params=_cparams(("parallel", "parallel")),
    )(up, up, up, up, cw, cw, cb, cb)


def _conv_act_bwd(up, a, g, dact, cw, name):
    S, F2 = up.shape
    Fh = F2 // 2
    T = _tile(S, ROW_T)
    tc = _tile(Fh, 256)
    ncol, nrow, hb, nhb = Fh // tc, S // T, T // 8, S // 8

    def body(ua_ref, ug_ref, a_ref, g_ref, an_ref, gn_ref, wa_ref, wg_ref, da_ref, dn_ref,
             dpa_ref, dpg_ref, dwa_ref, dwg_ref, dba_ref, dbg_ref, dua_n, dug_n):
        i = pl.program_id(1)
        rows = lax.broadcasted_iota(jnp.int32, (T, tc), 0)

        def du_of(a, g, dact_v):
            gel, t = _gelu(g)
            return dact_v * gel, dact_v * a * _gelu_grad(g, t)

        dua, dug = du_of(a_ref[...], g_ref[...], da_ref[...])
        keep = jnp.where(i == nrow - 1, 0.0, 1.0)
        dua_n[...], dug_n[...] = du_of(an_ref[...], gn_ref[...], dn_ref[...] * keep)

        @pl.when(i == 0)
        def _():
            for r in (dwa_ref, dwg_ref, dba_ref, dbg_ref):
                r[...] = jnp.zeros_like(r)

        for du, n_ref, u_ref, w_ref, o_ref, dw_ref, db_ref in ((dua, dua_n, ua_ref, wa_ref, dpa_ref, dwa_ref, dba_ref),
                                                               (dug, dug_n, ug_ref, wg_ref, dpg_ref, dwg_ref, dbg_ref)):
            n0, n1 = n_ref[0:1, :], n_ref[1:2, :]
            du1 = jnp.where(rows == T - 1, n0, pltpu.roll(du, T - 1, 0))
            du2 = jnp.where(rows == T - 2, n0, jnp.where(rows == T - 1, n1, pltpu.roll(du, T - 2, 0)))
            o_ref[...] = (w_ref[2:3, :] * du + w_ref[1:2, :] * du1 + w_ref[0:1, :] * du2).astype(o_ref.dtype)
            u = u_ref[...]
            db_ref[...] += jnp.sum(du, axis=0, keepdims=True)
            for j, d in enumerate((du2, du1, du)):
                dw_ref[j:j + 1, :] += jnp.sum(d * u, axis=0, keepdims=True)

    tile = lambda off: pl.BlockSpec((T, tc), lambda j, i: (i, j + off))
    under = pl.BlockSpec((8, tc), lambda j, i: (jnp.minimum((i + 1) * hb, nhb - 1), j))
    vec = lambda n, off: pl.BlockSpec((n, tc), lambda j, i: (0, j + off))
    act = jax.ShapeDtypeStruct((S, Fh), BF16)
    return pl.pallas_call(
        body, name=name,
        out_shape=(act, act, jax.ShapeDtypeStruct((3, Fh), F32), jax.ShapeDtypeStruct((3, Fh), F32),
                   jax.ShapeDtypeStruct((1, Fh), F32), jax.ShapeDtypeStruct((1, Fh), F32)),
        grid=(ncol, nrow),
        in_specs=[tile(0), tile(ncol), tile(0), tile(0), under, under, vec(3, 0), vec(3, ncol), tile(0), under],
        out_specs=(tile(0), tile(0), vec(3, 0), vec(3, 0), vec(1, 0), vec(1, 0)),
        scratch_shapes=[pltpu.VMEM((8, tc), F32), pltpu.VMEM((8, tc), F32)],
        compiler_params=_cparams(("parallel", "arbitrary")),
    )(up, up, a, g, a, g, cw, cw, dact, dact)


def _split3(x):
    hi = x.astype(BF16)
    r1 = x - hi.astype(F32)
    mid = r1.astype(BF16)
    lo = (r1 - mid.astype(F32)).astype(BF16)
    return hi, mid, lo


def _tri_dot(tri, x):
    hi, mid, lo = _split3(x)
    return _dot_nn(tri, hi) + _dot_nn(tri, mid) + _dot_nn(tri, lo)


def _log_sigmoid(x):
    return jnp.minimum(x, 0.0) - jnp.log(1.0 + jnp.exp(-jnp.abs(x)))


def _tri_mask(n, lower):
    r = lax.broadcasted_iota(jnp.int32, (n, n), 0)
    c = lax.broadcasted_iota(jnp.int32, (n, n), 1)
    return (r >= c) if lower else (r <= c)


def _gates_fwd(ps, bi, bf, bff, name):
    S = ps.shape[0]
    NC = S // MLC

    def body(ps_ref, bi_ref, bf_ref, bff_ref, a_ref, A_ref, wi_ref, em_ref, wk_ref, dec_ref, F_ref, m_scr, f_scr):
        @pl.when(pl.program_id(0) == 0)
        def _():
            m_scr[...] = jnp.zeros_like(m_scr)
            f_scr[...] = jnp.zeros_like(f_scr)

        rows = lax.broadcasted_iota(jnp.int32, (MLC, LANES), 0)
        ltri = _tri_mask(MLC, True).astype(BF16)
        li = GATE_CAP * jnp.tanh((ps_ref[:, 0:LANES] + bi_ref[...]) / GATE_CAP)
        lf = _log_sigmoid(GATE_CAP * jnp.tanh((ps_ref[:, LANES:2 * LANES] + bf_ref[...]) / GATE_CAP))
        b = _tri_dot(ltri, lf)
        a = li - b
        cm = a
        sh = 1
        while sh < MLC:
            cm = jnp.where(rows >= sh, jnp.maximum(cm, pltpu.roll(cm, sh, 0)), cm)
            sh *= 2
        m0 = m_scr[...]
        A = jnp.maximum(cm, m0)
        a_ref[...] = a
        A_ref[...] = A
        A_last = A_ref[MLC - 1:MLC, :]
        wi_ref[...] = jnp.exp(m0 - A)
        em_ref[...] = jnp.exp(-(b + A))
        wk_ref[...] = jnp.exp(a - A_last)
        dec_ref[0] = jnp.exp(m0 - A_last)
        F_ref[...] = b
        m_scr[...] = F_ref[MLC - 1:MLC, :] + A_last
        lfg = _log_sigmoid(ps_ref[:, 2 * LANES:3 * LANES] + bff_ref[...])
        F_ref[...] = _tri_dot(ltri, lfg) + f_scr[...]
        f_scr[...] = F_ref[MLC - 1:MLC, :]

    col = pl.BlockSpec((MLC, LANES), lambda c: (c, 0))
    vec = pl.BlockSpec((1, LANES), lambda c: (0, 0))
    cs = jax.ShapeDtypeStruct((S, LANES), F32)
    return pl.pallas_call(
        body, name=name,
        out_shape=(cs, cs, cs, cs, cs, jax.ShapeDtypeStruct((NC, 1, LANES), F32), cs),
        grid=(NC,), in_specs=[pl.BlockSpec((MLC, N_SMALL), lambda c: (c, 0)), vec, vec, vec],
        out_specs=(col, col, col, col, col, pl.BlockSpec((1, 1, LANES), lambda c: (c, 0, 0)), col),
        scratch_shapes=[pltpu.VMEM((1, LANES), F32), pltpu.VMEM((1, LANES), F32)],
        compiler_params=_cparams(("arbitrary",)),
    )(ps, bi, bf, bff)


def _gates_bwd(ps, bi, bf, bff, rk, kc, tch, dF, name):
    S = ps.shape[0]
    NC = S // MLC

    def body(ps_ref, bi_ref, bf_ref, bff_ref, rk_ref, kc_ref, t_ref, dF_ref, dps_ref, db_ref, carry):
        @pl.when(pl.program_id(0) == 0)
        def _():
            carry[...] = jnp.zeros_like(carry)
            db_ref[...] = jnp.zeros_like(db_ref)

        lanes = lax.broadcasted_iota(jnp.int32, (MLC, LANES), 1)
        utri = _tri_mask(MLC, False).astype(BF16)
        ti = jnp.tanh((ps_ref[:, 0:LANES] + bi_ref[...]) / GATE_CAP)
        t_end, t_start = t_ref[0, 0:1, :], t_ref[0, 1:2, :]
        rk = rk_ref[...]
        rk = rk - (jnp.sum(rk, axis=0, keepdims=True) - (t_start - t_end)) * (1.0 / MLC)
        dpi = jnp.where(lanes < ML_HEADS, (kc_ref[...] - rk) * (1.0 - ti * ti), 0.0)
        tf = jnp.tanh((ps_ref[:, LANES:2 * LANES] + bf_ref[...]) / GATE_CAP)
        dlf = _tri_dot(utri, rk) + t_end
        dpf = jnp.where(lanes < ML_HEADS, dlf * jax.nn.sigmoid(-GATE_CAP * tf) * (1.0 - tf * tf), 0.0)
        dFv = dF_ref[...]
        dlfg = _tri_dot(utri, dFv) + carry[...]
        carry[...] += jnp.sum(dFv, axis=0, keepdims=True)
        dpff = jnp.where(lanes < FOX_HEADS, dlfg * jax.nn.sigmoid(-(ps_ref[:, 2 * LANES:3 * LANES] + bff_ref[...])), 0.0)
        for n, dp in enumerate((dpi, dpf, dpff)):
            dps_ref[:, n * LANES:(n + 1) * LANES] = dp.astype(dps_ref.dtype)
            db_ref[:, n * LANES:(n + 1) * LANES] += jnp.sum(dp, axis=0, keepdims=True)

    rev = lambda c: (NC - 1 - c, 0)
    col = pl.BlockSpec((MLC, LANES), rev)
    vec = pl.BlockSpec((1, LANES), lambda c: (0, 0))
    wide = pl.BlockSpec((MLC, N_SMALL), rev)
    return pl.pallas_call(
        body, name=name,
        out_shape=(jax.ShapeDtypeStruct((S, N_SMALL), BF16), jax.ShapeDtypeStruct((1, N_SMALL), F32)),
        grid=(NC,),
        in_specs=[wide, vec, vec, vec, col, col, pl.BlockSpec((1, 2, LANES), lambda c: (NC - 1 - c, 0, 0)), col],
        out_specs=(wide, pl.BlockSpec((1, N_SMALL), lambda c: (0, 0))),
        scratch_shapes=[pltpu.VMEM((1, LANES), F32)],
        compiler_params=_cparams(("arbitrary",)),
    )(ps, bi, bf, bff, rk, kc, tch, dF)


_ML_SCALE = ML_DQK ** -0.5


def _ml_specs(rev, NC):
    idx = (lambda c: NC - 1 - c) if rev else (lambda c: c)
    qk = lambda blk: pl.BlockSpec((MLC, ML_HEADS * ML_DQK), lambda c: (idx(c), blk))
    wide = lambda blk: pl.BlockSpec((MLC, D_MODEL), lambda c: (idx(c), blk))
    col = pl.BlockSpec((MLC, LANES), lambda c: (idx(c), 0))
    return idx, qk, wide, col


def _ml_intra(q_ref, k_ref, arow_ref, A_ref, h):
    hs = slice(h * ML_DQK, (h + 1) * ML_DQK)
    qf = q_ref[:, hs] * _ML_SCALE
    kf = k_ref[:, hs]
    qb, kb = qf.astype(BF16), kf.astype(BF16)
    qk = _dot_nt(qb, kb)
    logw = arow_ref[h:h + 1, :] - A_ref[:, h:h + 1]
    W = jnp.exp(jnp.where(_tri_mask(MLC, True), logw, -1e30))
    return qb, kb, qf, kf, qk, W


def _mlstm_fwd(pm, a_row, A, wi, em, wk, dec, w_hn, name):
    S = pm.shape[0]
    NC = S // MLC
    _, qk, wide, col = _ml_specs(False, NC)

    def body(q_ref, k_ref, v_ref, o_ref, arow_ref, A_ref, wi_ref, em_ref, wk_ref, dec_ref, whn_ref,
             ha_ref, hp_ref, den_ref, cst_ref, nst_ref, C_scr, n_scr):
        @pl.when(pl.program_id(0) == 0)
        def _():
            C_scr[...] = jnp.zeros_like(C_scr)
            n_scr[...] = jnp.zeros_like(n_scr)

        lanes = lax.broadcasted_iota(jnp.int32, (MLC, LANES), 1)
        den_tile = jnp.zeros((MLC, LANES), F32)
        for h in range(ML_HEADS):
            vs = slice(h * ML_DV, (h + 1) * ML_DV)
            qb, kb, qf, kf, qk_, W = _ml_intra(q_ref, k_ref, arow_ref, A_ref, h)
            vb = v_ref[:, vs].astype(BF16)
            Cf = C_scr[h]
            Cb = Cf.astype(BF16)
            nrow = n_scr[h]
            cst_ref[0, h] = Cb
            nst_ref[0, h] = nrow
            s = qk_ * W
            wic = wi_ref[:, h:h + 1]
            num = _dot_nn(s.astype(BF16), vb) + wic * _dot_nt(qb, Cb)
            den = jnp.sum(s, axis=1, keepdims=True) + wic * jnp.sum(qf * nrow, axis=1, keepdims=True)
            hp = num / jnp.maximum(jnp.abs(den), em_ref[:, h:h + 1])
            hp_ref[:, vs] = hp
            den_tile = jnp.where(lanes == h, den, den_tile)
            hn = hp * _rstd(hp) * whn_ref[:, vs]
            ha_ref[:, vs] = (hn * jax.nn.sigmoid(o_ref[:, vs])).astype(ha_ref.dtype)
            wkc = wk_ref[:, h:h + 1]
            kw = kf * wkc
            d = dec_ref[0, :, h:h + 1]
            C_scr[h] = d * Cf + _dot_tn(vb, kw.astype(BF16))
            n_scr[h] = d * nrow + jnp.sum(kw, axis=0, keepdims=True)
        den_ref[...] = den_tile

    return pl.pallas_call(
        body, name=name,
        out_shape=(jax.ShapeDtypeStruct((S, D_MODEL), BF16), jax.ShapeDtypeStruct((S, D_MODEL), F32),
                   jax.ShapeDtypeStruct((S, LANES), F32),
                   jax.ShapeDtypeStruct((NC, ML_HEADS, ML_DV, ML_DQK), BF16),
                   jax.ShapeDtypeStruct((NC, ML_HEADS, 1, ML_DQK), F32)),
        grid=(NC,),
        in_specs=[qk(C_QM // 512), qk(C_KM // 512), wide(C_VM // D_MODEL), wide(C_OM // D_MODEL),
                  pl.BlockSpec((8, MLC), lambda c: (0, c)), col, col, col, col,
                  pl.BlockSpec((1, 1, LANES), lambda c: (c, 0, 0)), pl.BlockSpec((1, D_MODEL), lambda c: (0, 0))],
        out_specs=(pl.BlockSpec((MLC, D_MODEL), lambda c: (c, 0)), pl.BlockSpec((MLC, D_MODEL), lambda c: (c, 0)),
                   col, pl.BlockSpec((1, ML_HEADS, ML_DV, ML_DQK), lambda c: (c, 0, 0, 0)),
                   pl.BlockSpec((1, ML_HEADS, 1, ML_DQK), lambda c: (c, 0, 0, 0))),
        scratch_shapes=[pltpu.VMEM((ML_HEADS, ML_DV, ML_DQK), F32), pltpu.VMEM((ML_HEADS, 1, ML_DQK), F32)],
        compiler_params=_cparams(("arbitrary",)),
    )(pm, pm, pm, pm, a_row, A, wi, em, wk, dec, w_hn)


def _mlstm_bwd(dha, pm, hp_all, den_all, a_row, A, wi, em, wk, dec, cst, nst, w_hn, name):
    S = pm.shape[0]
    NC = S // MLC
    idx, qk, wide, col = _ml_specs(True, NC)

    def body(dha_ref, q_ref, k_ref, v_ref, o_ref, hp_ref, den_ref, arow_ref, A_ref, wi_ref, em_ref, wk_ref,
             dec_ref, cst_ref, nst_ref, whn_ref,
             dqk_ref, dv_ref, do_ref, rk_ref, kc_ref, t_ref, dwhn_ref, dC_scr, dn_scr, t_scr):
        @pl.when(pl.program_id(0) == 0)
        def _():
            dC_scr[...] = jnp.zeros_like(dC_scr)
            dn_scr[...] = jnp.zeros_like(dn_scr)
            t_scr[...] = jnp.zeros_like(t_scr)
            dwhn_ref[...] = jnp.zeros_like(dwhn_ref)

        lanes = lax.broadcasted_iota(jnp.int32, (MLC, LANES), 1)
        lane1 = lax.broadcasted_iota(jnp.int32, (1, LANES), 1)
        t_ref[0, 0:1, :] = t_scr[...]
        rk_tile = jnp.zeros((MLC, LANES), F32)
        kc_tile = jnp.zeros((MLC, LANES), F32)
        t_new = jnp.zeros((1, LANES), F32)
        for h in range(ML_HEADS):
            hs = slice(h * ML_DQK, (h + 1) * ML_DQK)
            vs = slice(h * ML_DV, (h + 1) * ML_DV)
            hp = hp_ref[:, vs]
            sig = jax.nn.sigmoid(o_ref[:, vs])
            whn = whn_ref[:, vs]
            r = _rstd(hp)
            dga = dha_ref[:, vs]
            do_ref[:, vs] = (dga * (hp * r * whn) * sig * (1.0 - sig)).astype(do_ref.dtype)
            dhn = dga * sig
            dhp, dwt = _rmsnorm_bwd_math(dhn, hp, whn)
            dwhn_ref[:, vs] += jnp.sum(dwt, axis=0, keepdims=True)
            den = den_ref[:, h:h + 1]
            floor = em_ref[:, h:h + 1]
            D = jnp.maximum(jnp.abs(den), floor)
            dnum = dhp / D
            dh_h = jnp.sum(dhp * hp, axis=1, keepdims=True)
            active = jnp.abs(den) >= floor
            dden = -dh_h / D * jnp.where(active, jnp.sign(den), 0.0)
            phi = jnp.where(active, 0.0, dh_h)
            qb, kb, qf, kf, qk_, W = _ml_intra(q_ref, k_ref, arow_ref, A_ref, h)
            vf = v_ref[:, vs]
            vb = vf.astype(BF16)
            Cb = cst_ref[0, h]
            nrow = nst_ref[0, h]
            wic = wi_ref[:, h:h + 1]
            wkc = wk_ref[:, h:h + 1]
            d = dec_ref[0, :, h:h + 1]
            dCn = dC_scr[h]
            dCb = dCn.astype(BF16)
            dnn = dn_scr[h]
            dnumb = dnum.astype(BF16)
            s = qk_ * W
            ds = (_dot_nt(dnumb, vb) + dden) * W
            dsb = ds.astype(BF16)
            dnw = (wic * dnum).astype(BF16)
            wd = wic * dden
            kw = kf * wkc
            dv_state = _dot_nt(kw.astype(BF16), dCb)
            dq = _dot_nn(dsb, kb) + _dot_nn(dnw, Cb) + wd * nrow
            dk_state = wkc * (_dot_nn(vb, dCb) + dnn)
            dk = _dot_tn(dsb, qb) + dk_state
            dv = _dot_tn(s.astype(BF16), dnumb) + dv_state
            dC = d * dCn + _dot_tn(dnw, qb)
            dn = d * dnn + jnp.sum(wd * qf, axis=0, keepdims=True)
            dC_scr[h] = dC
            dn_scr[h] = dn
            dqk_ref[:, hs] = (dq * _ML_SCALE).astype(dqk_ref.dtype)
            dqk_ref[:, C_KM + h * ML_DQK:C_KM + (h + 1) * ML_DQK] = dk.astype(dqk_ref.dtype)
            dv_ref[:, vs] = dv.astype(dv_ref.dtype)
            G = ds * qk_
            inter = _dot_nt(qb, Cb)
            qn = jnp.sum(qf * nrow, axis=1, keepdims=True)
            R = (jnp.sum(G, axis=1, keepdims=True)
                 + wic * (jnp.sum(dnum * inter, axis=1, keepdims=True) + dden * qn))
            K = jnp.sum(G.T, axis=1, keepdims=True) + jnp.sum(kf * dk_state, axis=1, keepdims=True)
            rk_tile = jnp.where(lanes == h, R - K, rk_tile)
            kc_tile = jnp.where(lanes == h, phi, kc_tile)
            tt = (jnp.sum(jnp.sum(dC * Cb.astype(F32), axis=1, keepdims=True), axis=0, keepdims=True)
                  + jnp.sum(dn * nrow, axis=1, keepdims=True))
            t_new = jnp.where(lane1 == h, tt, t_new)
        rk_ref[...] = rk_tile
        kc_ref[...] = kc_tile
        t_ref[0, 1:2, :] = t_new
        t_scr[...] = t_new

    act = lambda n: jax.ShapeDtypeStruct((S, n), BF16)
    cs = jax.ShapeDtypeStruct((S, LANES), F32)
    rowblk = lambda n: pl.BlockSpec((MLC, n), lambda c: (idx(c), 0))
    return pl.pallas_call(
        body, name=name,
        out_shape=(act(D_MODEL), act(D_MODEL), act(D_MODEL), cs, cs,
                   jax.ShapeDtypeStruct((NC, 2, LANES), F32), jax.ShapeDtypeStruct((1, D_MODEL), F32)),
        grid=(NC,),
        in_specs=[rowblk(D_MODEL), qk(C_QM // 512), qk(C_KM // 512), wide(C_VM // D_MODEL), wide(C_OM // D_MODEL),
                  rowblk(D_MODEL), col, pl.BlockSpec((8, MLC), lambda c: (0, idx(c))), col, col, col, col,
                  pl.BlockSpec((1, 1, LANES), lambda c: (idx(c), 0, 0)),
                  pl.BlockSpec((1, ML_HEADS, ML_DV, ML_DQK), lambda c: (idx(c), 0, 0, 0)),
                  pl.BlockSpec((1, ML_HEADS, 1, ML_DQK), lambda c: (idx(c), 0, 0, 0)),
                  pl.BlockSpec((1, D_MODEL), lambda c: (0, 0))],
        out_specs=(rowblk(D_MODEL), rowblk(D_MODEL), rowblk(D_MODEL), col, col,
                   pl.BlockSpec((1, 2, LANES), lambda c: (idx(c), 0, 0)), pl.BlockSpec((1, D_MODEL), lambda c: (0, 0))),
        scratch_shapes=[pltpu.VMEM((ML_HEADS, ML_DV, ML_DQK), F32), pltpu.VMEM((ML_HEADS, 1, ML_DQK), F32),
                        pltpu.VMEM((1, LANES), F32)],
        compiler_params=_cparams(("arbitrary",)),
    )(dha, pm, pm, pm, pm, hp_all, den_all, a_row, A, wi, em, wk, dec, cst, nst, w_hn)


_FOX_SCALE = FOX_DH ** -0.5
_NEG = -1e30
_LOG2E = 1.4426950408889634
_LN2 = 0.6931471805599453
_QF_BLK, _KF_BLK, _VF_BLK = 0, FOX_HEADS, 2 * FOX_HEADS


def _lane_pick(tile, lane):
    lanes = lax.broadcasted_iota(jnp.int32, tile.shape, 1)
    return jnp.sum(jnp.where(lanes == lane, tile, 0.0), axis=1, keepdims=True)


def _col_to_row(col):
    return jnp.max(jnp.broadcast_to(col, (col.shape[0], LANES)).T, axis=0, keepdims=True)


def _causal(q0, k0, shape, q_axis):
    qpos = q0 + lax.broadcasted_iota(jnp.int32, shape, q_axis)
    kpos = k0 + lax.broadcasted_iota(jnp.int32, shape, 1 - q_axis)
    return kpos <= qpos


def _fox_fwd(pf, fc, fk_row, name):
    S = pf.shape[0]
    TQ, TK = FOX_TQ_FWD, FOX_TK_FWD
    nq, nk = S // TQ, S // TK
    c1 = _FOX_SCALE * _LOG2E

    def body(q_ref, k_ref, v_ref, fc_ref, fr_ref, o_ref, lse_ref):
        h, i = pl.program_id(0), pl.program_id(1)
        qb = q_ref[...]
        fq2 = _lane_pick(fc_ref[...], h) * _LOG2E

        def step(j, carry, masked):
            m, l, acc = carry
            off = pl.multiple_of(j * TK, TK)
            t = _dot_nt(qb, k_ref[pl.ds(off, TK), :]) * c1 - fr_ref[0, j] * _LOG2E
            if masked:
                t = jnp.where(_causal(i * TQ, j * TK, (TQ, TK), 0), t, _NEG)
            m_new = jnp.maximum(m, jnp.max(t, axis=1, keepdims=True) + fq2)
            alpha = jnp.exp2(m - m_new)
            p = jnp.exp2(t + (fq2 - m_new))
            l = alpha * l + jnp.sum(p, axis=1, keepdims=True)
            acc = alpha * acc + _dot_nn(p.astype(BF16), v_ref[pl.ds(off, TK), :])
            return m_new, l, acc

        init = (jnp.full((TQ, 1), _NEG, F32), jnp.zeros((TQ, 1), F32), jnp.zeros((TQ, FOX_DH), F32))
        last = (i * TQ) // TK
        carry = lax.fori_loop(0, last, lambda j, c: step(j, c, False), init)
        m, l, acc = step(last, carry, True)
        o_ref[...] = (acc / l).astype(o_ref.dtype)
        lse_ref[0, 0] = _col_to_row((m + jnp.log2(l)) * _LN2)

    head = lambda blk: pl.BlockSpec((S, FOX_DH), lambda h, i: (0, blk + h))
    return pl.pallas_call(
        body, name=name,
        out_shape=(jax.ShapeDtypeStruct((S, D_MODEL), BF16), jax.ShapeDtypeStruct((FOX_HEADS, nq, 1, TQ), F32)),
        grid=(FOX_HEADS, nq),
        in_specs=[pl.BlockSpec((TQ, FOX_DH), lambda h, i: (i, _QF_BLK + h)), head(_KF_BLK), head(_VF_BLK),
                  pl.BlockSpec((TQ, LANES), lambda h, i: (i, 0)),
                  pl.BlockSpec((1, nk, 1, TK), lambda h, i: (h, 0, 0, 0))],
        out_specs=(pl.BlockSpec((TQ, FOX_DH), lambda h, i: (i, h)),
                   pl.BlockSpec((1, 1, 1, TQ), lambda h, i: (h, i, 0, 0))),
        compiler_params=_cparams(("parallel", "arbitrary")),
    )(pf, pf, pf, fc, fk_row)


def _fox_bwd(dhb, hb, pf, lse_row, fq_row, fc, name):
    S = pf.shape[0]
    TQ, TK = FOX_TQ, FOX_TK
    nq, nk, r = S // TQ, S // TK, TK // TQ
    c1 = _FOX_SCALE * _LOG2E

    def body(q_ref, k_ref, v_ref, do_ref, o_ref, lse_ref, fq_ref, fc_ref,
             dq_ref, dk_ref, dv_ref, dFk_ref, dFq_ref, dq_acc, qside, delta, dk_acc, dv_acc, cs_acc):
        h, j = pl.program_id(0), pl.program_id(1)

        @pl.when(j == 0)
        def _():
            dq_acc[...] = jnp.zeros_like(dq_acc)
            dFq_ref[...] = jnp.zeros_like(dFq_ref)

            def fill(b, _):
                off = pl.multiple_of(b * TQ, TQ)
                prod = do_ref[pl.ds(off, TQ), :].astype(F32) * o_ref[pl.ds(off, TQ), :].astype(F32)
                delta[b] = jnp.sum(prod.T, axis=0, keepdims=True)
                qside[b] = (fq_ref[0, b] - lse_ref[0, b]) * _LOG2E
                return 0

            lax.fori_loop(0, nq, fill, 0)

        kb = k_ref[...]
        vb = v_ref[...]
        fk2 = _lane_pick(fc_ref[...], h) * _LOG2E
        dk_acc[...] = jnp.zeros_like(dk_acc)
        dv_acc[...] = jnp.zeros_like(dv_acc)
        cs_acc[...] = jnp.zeros_like(cs_acc)

        def step(i, masked):
            off = pl.multiple_of(i * TQ, TQ)
            qb = q_ref[pl.ds(off, TQ), :]
            dob = do_ref[pl.ds(off, TQ), :]
            t = _dot_nt(kb, qb) * c1 + qside[i] - fk2
            if masked:
                t = jnp.where(_causal(i * TQ, j * TK, (TK, TQ), 1), t, _NEG)
            p = jnp.exp2(t)
            dv_acc[...] += _dot_nn(p.astype(BF16), dob)
            ds = p * (_dot_nt(vb, dob) - delta[i])
            dsb = ds.astype(BF16)
            dk_acc[...] += _dot_nn(dsb, qb)
            dq_acc[pl.ds(off, TQ), :] += _dot_tn(dsb, kb)
            cs_acc[...] += jnp.sum(ds, axis=1, keepdims=True)
            dFq_ref[0, i] += jnp.sum(ds, axis=0, keepdims=True)

        for d in range(r):
            step(r * j + d, True)

        def rest(i, _):
            step(i, False)
            return 0

        lax.fori_loop(r * j + r, nq, rest, 0)
        dk_ref[...] = (dk_acc[...] * _FOX_SCALE).astype(dk_ref.dtype)
        dv_ref[...] = dv_acc[...].astype(dv_ref.dtype)
        dFk_ref[0, 0] = -_col_to_row(cs_acc[...])

        @pl.when(j == nk - 1)
        def _():
            dq_ref[...] = (dq_acc[...] * _FOX_SCALE).astype(dq_ref.dtype)

    head = lambda blk: pl.BlockSpec((S, FOX_DH), lambda h, j: (0, blk + h))
    kblk = lambda blk: pl.BlockSpec((TK, FOX_DH), lambda h, j: (j, blk + h))
    qrows = pl.BlockSpec((1, nq, 1, TQ), lambda h, j: (h, 0, 0, 0))
    act = jax.ShapeDtypeStruct((S, D_MODEL), BF16)
    return pl.pallas_call(
        body, name=name,
        out_shape=(act, act, act, jax.ShapeDtypeStruct((FOX_HEADS, nk, 1, TK), F32),
                   jax.ShapeDtypeStruct((FOX_HEADS, nq, 1, TQ), F32)),
        grid=(FOX_HEADS, nk),
        in_specs=[head(_QF_BLK), kblk(_KF_BLK), kblk(_VF_BLK), head(0), head(0), qrows, qrows,
                  pl.BlockSpec((TK, LANES), lambda h, j: (j, 0))],
        out_specs=(head(0), kblk(0), kblk(0), pl.BlockSpec((1, 1, 1, TK), lambda h, j: (h, j, 0, 0)), qrows),
        scratch_shapes=[pltpu.VMEM((S, FOX_DH), F32), pltpu.VMEM((nq, 1, TQ), F32), pltpu.VMEM((nq, 1, TQ), F32),
                        pltpu.VMEM((TK, FOX_DH), F32), pltpu.VMEM((TK, FOX_DH), F32), pltpu.VMEM((TK, 1), F32)],
        compiler_params=_cparams(("parallel", "arbitrary")),
    )(pf, pf, pf, dhb, hb, lse_row, fq_row, fc)


def _pad_lanes(v):
    return jnp.pad(v, ((0, 0), (0, LANES - v.shape[1])))


def _local_step(x, target, wmain_t, wsmall_t, rest_weights, p, on_grads, advance, token):
    S = x.shape[0]
    bi, bf, bff = _pad_lanes(p["b_ml_i"]), _pad_lanes(p["b_ml_f"]), _pad_lanes(p["b_fox_f"])

    h0 = _rmsnorm_fwd(x, p["norm_mix_pre"] + token[0:1, 0:1], "norm_mix_pre")
    pm = _mm(h0, wmain_t[:N_ML], "nt", F32, "proj_mlstm")
    pf = _mm(h0, wmain_t[N_ML:N_ML + N_FOX], "nt", BF16, "proj_fox")
    pg = _mm(h0, wmain_t[N_ML + N_FOX:], "nt", F32, "proj_merge")
    ps = _mm(h0, wsmall_t, "nt", F32, "proj_gates")
    a, A, wi, em, wk, dec, Fc = _gates_fwd(ps, bi, bf, bff, "gates_fwd")
    a_row = a[:, :8].T
    ha, hp, den, cst, nst = _mlstm_fwd(pm, a_row, A, wi, em, wk, dec, p["ml_head_norm"], "mlstm_fwd")
    ft = Fc[:, :FOX_HEADS].T
    fq_row = ft.reshape(FOX_HEADS, S // FOX_TQ, 1, FOX_TQ)
    fk_row = ft.reshape(FOX_HEADS, S // FOX_TK, 1, FOX_TK)
    hb, lse_row = _fox_fwd(pf, Fc, ft.reshape(FOX_HEADS, S // FOX_TK_FWD, 1, FOX_TK_FWD), "fox_fwd")
    wa, wb, wout, wup, wdown = rest_weights(hb)
    ya = _mm(ha, wa, "nn", F32, "branch_a")
    yb = _mm(hb, wb, "nn", F32, "branch_b")
    merged = _merge_fwd(ya, yb, pg, p["b_gate_a"], p["b_gate_b"], "merge_fwd")
    z = _mm(merged, wout, "nn", F32, "out_proj")
    x1 = _resid_norm_fwd(x, z, p["norm_mix_post"], "resid_mix")
    h2 = _rmsnorm_fwd(x1, p["norm_ffn_pre"], "norm_ffn_pre")
    up = _mm(h2, wup, "nn", F32, "ffn_up")
    act, conv_a, conv_g = _conv_act_fwd(up, p["conv_w"], p["conv_b"], "conv_act_fwd")
    d = _mm(act, wdown, "nn", F32, "ffn_down")
    loss_row, dy, dd, g_norm_ffn_post = _loss_head(x1, d, p["norm_ffn_post"], target, "loss_head")
    dact = _mm(dd, wdown, "nt", F32, "d_act")
    g_wdown = _mm(act, dd, "tn", F32, "dw_down", tm=1408)
    dupa, dupg, dcwa, dcwg, dcba, dcbg = _conv_act_bwd(up, conv_a, conv_g, dact, p["conv_w"], "conv_act_bwd")
    g_conv_w = jnp.concatenate([dcwa, dcwg], axis=1)
    g_conv_b = jnp.concatenate([dcba, dcbg], axis=1)
    dh2 = _mm([dupa, dupg], wup, "nt", F32, "d_h2")
    g_wup = _mm(h2, [dupa, dupg], "tn", F32, "dw_up")
    token = on_grads("ffn", dict(w_up=g_wup, w_down=g_wdown))
    dx1, g_norm_ffn_pre = _rmsnorm_bwd([dh2], x1, p["norm_ffn_pre"] + token[0:1, 0:1], dy, F32, "norm_ffn_pre_bwd")
    dz, g_norm_mix_post = _rmsnorm_bwd([dx1], z, p["norm_mix_post"], None, BF16, "norm_mix_post_bwd")
    dmerged = _mm(dz, wout, "nt", F32, "d_merged")
    g_wout = _mm(merged, dz, "tn", F32, "dw_out")
    dya, dyb, dga, dgb, g_b_gate_a, g_b_gate_b = _merge_bwd(dmerged, ya, yb, pg, p["b_gate_a"], p["b_gate_b"], "merge_bwd")
    dha = _mm(dya, wa, "nt", F32, "d_ha")
    g_wa = _mm(ha, dya, "tn", F32, "dw_a")
    dhb = _mm(dyb, wb, "nt", BF16, "d_hb")
    g_wb = _mm(hb, dyb, "tn", F32, "dw_b")
    token = advance("ffn", g_wb) + on_grads("mix", dict(w_out=g_wout, w_branch_a=g_wa, w_branch_b=g_wb))
    dqkm, dvm, dom, rk, kc, tch, g_ml_head_norm = _mlstm_bwd(
        dha, pm, hp, den, a_row, A, wi, em, wk, dec, cst, nst, p["ml_head_norm"] + token[0:1, 0:1], "mlstm_bwd")
    token = advance("mix", dqkm)
    dqf, dkf, dvf, dFk, dFq = _fox_bwd(dhb, hb, pf, lse_row.reshape(fq_row.shape), fq_row + token[0, 0], Fc, "fox_bwd")
    dF = jnp.pad((dFk.reshape(FOX_HEADS, S) + dFq.reshape(FOX_HEADS, S)).T, ((0, 0), (0, LANES - FOX_HEADS)))
    dps, dbias = _gates_bwd(ps, bi, bf, bff, rk, kc, tch, dF, "gates_bwd")
    dpm = [dqkm, dvm, dom, dqf, dkf, dvf, dga, dgb]
    g_wmain_t = _mm(dpm, h0, "tn", F32, "dw_main")
    token = on_grads("in", dict(w_in=g_wmain_t))
    g_wsmall_t = _mm(dps, h0, "tn", F32, "dw_gates")
    dh0s = _mm(dps, wsmall_t + token[0:1, 0:1].astype(BF16), "nn", F32, "d_h0_gates")
    token = advance("in", dh0s)
    dpm = list(lax.optimization_barrier((token, *dpm))[1:])
    dh0 = _mm(dpm, wmain_t, "nn", F32, "d_h0_main")
    grad_x, g_norm_mix_pre = _rmsnorm_bwd([dh0, dh0s], x, p["norm_mix_pre"], dx1, F32, "norm_mix_pre_bwd")

    big = dict(wsmall_t=g_wsmall_t)
    small = dict(norm_mix_pre=g_norm_mix_pre, ml_head_norm=g_ml_head_norm, b_gate_a=g_b_gate_a, b_gate_b=g_b_gate_b,
                 norm_mix_post=g_norm_mix_post, norm_ffn_pre=g_norm_ffn_pre, norm_ffn_post=g_norm_ffn_post,
                 conv_b=g_conv_b, b_ml_i=dbias[:, 0:ML_HEADS], b_ml_f=dbias[:, LANES:LANES + ML_HEADS],
                 b_fox_f=dbias[:, 2 * LANES:2 * LANES + FOX_HEADS], conv_w=g_conv_w)
    return loss_row, grad_x, big, small


def _row_tile(r, target=256):
    best = None
    for t in range(8, min(r, target) + 1, 8):
        if r % t == 0:
            best = t
    return best if best is not None else r


def _adamw(w, g, m, v, name):
    _, R, C = w.shape
    tr = _row_tile(R)
    tc = C
    if tr == R and R > 256:
        tc = 256

    def body(w_ref, g_ref, m_ref, v_ref, d_ref, mo_ref, vo_ref):
        gv = g_ref[...]
        mn = ADAM_B1 * m_ref[0] + (1.0 - ADAM_B1) * gv
        vn = ADAM_B2 * v_ref[0] + (1.0 - ADAM_B2) * (gv * gv)
        m_hat = mn / (1.0 - ADAM_B1 ** ADAM_STEP)
        v_hat = vn / (1.0 - ADAM_B2 ** ADAM_STEP)
        d_ref[0] = -ADAM_LR * (m_hat / (jnp.sqrt(v_hat) + ADAM_EPS) + ADAM_WD * w_ref[0])
        mo_ref[0] = mn
        vo_ref[0] = vn

    blk = pl.BlockSpec((1, tr, tc), lambda i, j: (0, i, j))
    o = jax.ShapeDtypeStruct((1, R, C), F32)
    return pl.pallas_call(
        body, name=name, out_shape=(o, o, o), grid=(R // tr, C // tc),
        in_specs=[blk, pl.BlockSpec((tr, tc), lambda i, j: (i, j)), blk, blk], out_specs=(blk,) * 3,
        compiler_params=_cparams(("parallel", "parallel")),
    )(w, g, m, v)


ANY = pl.BlockSpec(memory_space=pl.ANY)


def _place():
    x, y, c = lax.axis_index("x"), lax.axis_index("y"), lax.axis_index("c")
    chips = [(1 - x, y), (x, 1 - y), (1 - x, 1 - y)]
    return x, y, c, chips


def _block(ref, kind, k, rows=None):
    if kind == "rows":
        return ref.at[k] if rows is None else ref.at[k, pl.ds(*rows), :]
    cb = ref.shape[1] // 4
    return ref.at[:, pl.ds(k * cb, cb)] if rows is None else ref.at[pl.ds(*rows), pl.ds(k * cb, cb)]


def _gathered_shape(s, kind):
    return (4,) + s.shape if kind == "rows" else (s.shape[0], 4 * s.shape[1])


def _gather_weights(shards, kinds, smalls):
    n, ns = len(shards), len(smalls)

    def body(*refs):
        ins, sm_in = refs[:n], refs[n:n + ns]
        outs, sm_out = refs[n + ns:2 * n + ns], refs[2 * n + ns:2 * (n + ns)]
        send_sems, recv_sems, sm_send, sm_recv, local_sems = refs[2 * (n + ns):]
        x, y, c, chips = _place()
        sibling = (x, y, 1 - c)
        kme = 2 * x + y

        def half(a, k, hc):
            h = ins[a].shape[0] // 2
            return _block(outs[a], kinds[a], k, (hc * h, h))

        def remote(a, slot, src, dst, to):
            return pltpu.make_async_remote_copy(src_ref=src, dst_ref=dst, send_sem=send_sems.at[a * 7 + slot],
                                                recv_sem=recv_sems.at[a * 7 + slot], device_id=to, device_id_type=MESH)

        def sm_copy(b, j, k, to):
            return pltpu.make_async_remote_copy(src_ref=sm_in[b], dst_ref=sm_out[b].at[k], send_sem=sm_send.at[3 * b + j],
                                                recv_sem=sm_recv.at[3 * b + j], device_id=to, device_id_type=MESH)

        local = [pltpu.make_async_copy(sm_in[b], sm_out[b].at[kme], local_sems.at[b]) for b in range(ns)]
        for cp in local:
            cp.start()
        sends = [remote(a, 6, ins[a], _block(outs[a], kinds[a], kme), sibling) for a in range(n)]
        for a in range(n):
            h = ins[a].shape[0] // 2
            for j, chip in enumerate(chips):
                sends.append(remote(a, j, ins[a].at[pl.ds(c * h, h), :], half(a, kme, c), (*chip, c)))
        for b in range(ns):
            for j, chip in enumerate(chips):
                sends.append(sm_copy(b, j, kme, (*chip, c)))
        for cp in sends:
            cp.start()
        for a in range(n):
            for j, chip in enumerate(chips):
                kj = 2 * chip[0] + chip[1]
                remote(a, j, half(a, kj, c), half(a, kj, c), (*chip, c)).wait_recv()
                fwd = remote(a, 3 + j, half(a, kj, c), half(a, kj, c), sibling)
                fwd.start()
                sends.append(fwd)
        for a in range(n):
            for j, chip in enumerate(chips):
                kj = 2 * chip[0] + chip[1]
                remote(a, 3 + j, half(a, kj, 1 - c), half(a, kj, 1 - c), sibling).wait_recv()
        for b in range(ns):
            for j, chip in enumerate(chips):
                sm_copy(b, j, 2 * chip[0] + chip[1], (*chip, c)).wait_recv()
        for a in range(n):
            remote(a, 6, ins[a], _block(outs[a], kinds[a], kme), sibling).wait_recv()
        for cp in sends:
            cp.wait_send()
        for cp in local:
            cp.wait()

    outs = pl.pallas_call(
        body, name="gather_weights",
        out_shape=tuple([jax.ShapeDtypeStruct(_gathered_shape(s, k), s.dtype) for s, k in zip(shards, kinds)]
                        + [jax.ShapeDtypeStruct((4,) + s.shape, s.dtype) for s in smalls]),
        in_specs=[ANY] * (n + ns), out_specs=tuple([ANY] * (n + ns)),
        scratch_shapes=[pltpu.SemaphoreType.DMA((7 * n,)), pltpu.SemaphoreType.DMA((7 * n,)),
                        pltpu.SemaphoreType.DMA((3 * ns,)), pltpu.SemaphoreType.DMA((3 * ns,)),
                        pltpu.SemaphoreType.DMA((ns,))],
    )(*shards, *smalls)
    return outs[:n], outs[n:]


_IN_HBM = pl.BlockSpec(memory_space=pltpu.HBM)
_SEMS = pl.BlockSpec(memory_space=pltpu.SEMAPHORE)
_DATAFLOW = pltpu.SideEffectType.DATAFLOW_SIDE_EFFECTING


def _hbm(t):
    return pltpu.HBM(t.shape, t.dtype)


def _gather_copies(ins, outs, send_sems, recv_sems, kinds):
    x, y, c, chips = _place()
    kme = 2 * x + y
    cps = []
    for a in range(len(ins)):
        h = ins[a].shape[0] // 2
        for j, chip in enumerate(chips + [None]):
            to = (x, y, 1 - c) if chip is None else (*chip, c)
            src = ins[a] if chip is None else ins[a].at[pl.ds(c * h, h), :]
            dst = _block(outs[a], kinds[a], kme, None if chip is None else (c * h, h))
            cps.append(pltpu.make_async_remote_copy(src_ref=src, dst_ref=dst, send_sem=send_sems.at[4 * a + j],
                                                    recv_sem=recv_sems.at[4 * a + j], device_id=to, device_id_type=MESH))
    return cps


def _gather_start(shards, kinds, name):
    n = len(shards)
    outs = [lax.empty(_gathered_shape(s, k), s.dtype) for s, k in zip(shards, kinds)]

    def body(*refs):
        for cp in _gather_copies(refs[:n], refs[n:2 * n], refs[2 * n], refs[2 * n + 1], kinds):
            cp.start()
        refs[-1][...] = jnp.zeros_like(refs[-1])

    return pl.pallas_call(
        body, name=name,
        out_shape=(pltpu.SemaphoreType.DMA((4 * n,)), pltpu.SemaphoreType.DMA((4 * n,)),
                   *[_hbm(t) for t in shards], *[_hbm(t) for t in outs], jax.ShapeDtypeStruct((8, LANES), F32)),
        in_specs=[_IN_HBM] * (2 * n),
        out_specs=(_SEMS, _SEMS, *[_IN_HBM] * (2 * n), pl.BlockSpec(memory_space=pltpu.VMEM)),
        input_output_aliases={a: 2 + a for a in range(2 * n)},
        compiler_params=pltpu.CompilerParams(has_side_effects=_DATAFLOW),
    )(*[pltpu.with_memory_space_constraint(t, pltpu.HBM) for t in list(shards) + outs])


def _gather_wait(started, after, kinds, name):
    n = (len(started) - 3) // 2
    bufs = started[2:2 + 2 * n]

    def body(*refs):
        for cp in _gather_copies(refs[:n], refs[n:2 * n], refs[2 * n], refs[2 * n + 1], kinds):
            cp.wait_send()
            cp.wait_recv()

    outs = pl.pallas_call(
        body, name=name, out_shape=tuple(_hbm(t) for t in bufs),
        in_specs=[_IN_HBM] * (2 * n) + [_SEMS, _SEMS, ANY], out_specs=tuple([_IN_HBM] * (2 * n)),
        input_output_aliases={a: a for a in range(2 * n)},
        compiler_params=pltpu.CompilerParams(has_side_effects=_DATAFLOW),
    )(*bufs, started[0], started[1], after)
    return outs[n:]


def _gather_relay(bufs, kinds, name):
    n = len(bufs)

    def body(*refs):
        ins, outs, send_sems, recv_sems = refs[:n], refs[n:2 * n], refs[2 * n], refs[2 * n + 1]
        x, y, c, chips = _place()
        cps = []
        for a in range(n):
            h = (ins[a].shape[1] if kinds[a] == "rows" else ins[a].shape[0]) // 2
            for j, chip in enumerate(chips):
                kj = 2 * chip[0] + chip[1]
                cps.append((pltpu.make_async_remote_copy(
                    src_ref=_block(ins[a], kinds[a], kj, (c * h, h)), dst_ref=_block(outs[a], kinds[a], kj, (c * h, h)),
                    send_sem=send_sems.at[3 * a + j], recv_sem=recv_sems.at[3 * a + j], device_id=(x, y, 1 - c),
                    device_id_type=MESH), a, kj, h))
        for cp, _, _, _ in cps:
            cp.start()
        for a_cp, (cp, a, kj, h) in enumerate(cps):
            theirs = _block(outs[a], kinds[a], kj, ((1 - c) * h, h))
            pltpu.make_async_remote_copy(src_ref=theirs, dst_ref=theirs, send_sem=send_sems.at[a_cp],
                                         recv_sem=recv_sems.at[a_cp], device_id=(x, y, 1 - c), device_id_type=MESH).wait_recv()
        for cp, _, _, _ in cps:
            cp.wait_send()

    return pl.pallas_call(
        body, name=name, out_shape=tuple(jax.ShapeDtypeStruct(b.shape, b.dtype) for b in bufs),
        in_specs=[ANY] * n, out_specs=tuple([ANY] * n), input_output_aliases={a: a for a in range(n)},
        scratch_shapes=[pltpu.SemaphoreType.DMA((3 * n,)), pltpu.SemaphoreType.DMA((3 * n,))],
    )(*bufs)


def _add_halves(g, r1, cvec, kind, name):
    def body(c_ref, g_ref, r_ref, o_ref):
        o_ref[...] = (g_ref[...] + r_ref[...]).astype(o_ref.dtype)

    if kind == "rows":
        _, h, C = r1.shape
        tr = _row_tile(h)
        nt = h // tr
        grid = (4, nt)
        g_spec = pl.BlockSpec((1, tr, C), lambda k, i, c_ref: (k, c_ref[0] * nt + i, 0))
        r_spec = pl.BlockSpec((1, tr, C), lambda k, i, c_ref: (k, i, 0))
    else:
        h, C4 = r1.shape
        tr, tc = _row_tile(h), C4 // 4
        nt = h // tr
        grid = (nt, 4)
        g_spec = pl.BlockSpec((tr, tc), lambda i, k, c_ref: (c_ref[0] * nt + i, k))
        r_spec = pl.BlockSpec((tr, tc), lambda i, k, c_ref: (i, k))
    return pl.pallas_call(
        body, name=name, out_shape=jax.ShapeDtypeStruct(r1.shape, BF16),
        grid_spec=pltpu.PrefetchScalarGridSpec(num_scalar_prefetch=1, grid=grid, in_specs=[g_spec, r_spec],
                                               out_specs=r_spec),
        compiler_params=_cparams(("parallel", "parallel")),
    )(cvec, g, r1)


def _chip_copies(ins, lands, send_sems, recv_sems, kinds):
    x, y, c, chips = _place()
    return [pltpu.make_async_remote_copy(
        src_ref=_block(ins[a], kinds[a], 2 * chip[0] + chip[1]), dst_ref=lands[a].at[j],
        send_sem=send_sems.at[3 * a + j], recv_sem=recv_sems.at[3 * a + j], device_id=(*chip, c), device_id_type=MESH)
        for a in range(len(ins)) for j, chip in enumerate(chips)]


def _land_shape(s, kind):
    return (3,) + (s.shape[1:] if kind == "rows" else (s.shape[0], s.shape[1] // 4))


def _sibling_copies(ins, lands, send_sems, recv_sems, kinds):
    x, y, c, _ = _place()
    cps = []
    for a in range(len(ins)):
        h = lands[a].shape[-2]
        src = ins[a].at[:, pl.ds((1 - c) * h, h), :] if kinds[a] == "rows" else ins[a].at[pl.ds((1 - c) * h, h), :]
        cps.append(pltpu.make_async_remote_copy(src_ref=src, dst_ref=lands[a], send_sem=send_sems.at[a],
                                                recv_sem=recv_sems.at[a], device_id=(x, y, 1 - c), device_id_type=MESH))
    return cps


def _half_shape(g, kind):
    return (4, g.shape[1] // 2, g.shape[2]) if kind == "rows" else (g.shape[0] // 2, g.shape[1])


def _exchange_start(copies, per_array, srcs, land_shapes, kinds, name):
    n = len(srcs)
    lands = [lax.empty(shape, s.dtype) for shape, s in zip(land_shapes, srcs)]

    def body(*refs):
        for cp in copies(refs[:n], refs[n:2 * n], refs[2 * n], refs[2 * n + 1], kinds):
            cp.start()
        refs[-1][...] = jnp.zeros_like(refs[-1])

    return pl.pallas_call(
        body, name=name,
        out_shape=(pltpu.SemaphoreType.DMA((per_array * n,)), pltpu.SemaphoreType.DMA((per_array * n,)),
                   *[_hbm(t) for t in srcs], *[_hbm(t) for t in lands], jax.ShapeDtypeStruct((8, LANES), F32)),
        in_specs=[_IN_HBM] * (2 * n),
        out_specs=(_SEMS, _SEMS, *[_IN_HBM] * (2 * n), pl.BlockSpec(memory_space=pltpu.VMEM)),
        input_output_aliases={a: 2 + a for a in range(2 * n)},
        compiler_params=pltpu.CompilerParams(has_side_effects=_DATAFLOW),
    )(*[pltpu.with_memory_space_constraint(t, pltpu.HBM) for t in list(srcs) + lands])


def _exchange_wait(copies, started, after, kinds, name):
    n = (len(started) - 3) // 2
    bufs = started[2:2 + 2 * n]

    def body(*refs):
        for cp in copies(refs[:n], refs[n:2 * n], refs[2 * n], refs[2 * n + 1], kinds):
            cp.wait_send()
            cp.wait_recv()

    outs = pl.pallas_call(
        body, name=name, out_shape=tuple(_hbm(t) for t in bufs),
        in_specs=[_IN_HBM] * (2 * n) + [_SEMS, _SEMS, ANY], out_specs=tuple([_IN_HBM] * (2 * n)),
        input_output_aliases={a: a for a in range(2 * n)},
        compiler_params=pltpu.CompilerParams(has_side_effects=_DATAFLOW),
    )(*bufs, started[0], started[1], after)
    return outs[:n], outs[n:]


def _add_chips(s1, r2, kcvec, kind, name):
    _, h, C = r2.shape
    tr = _row_tile(h)
    nt = h // tr

    def body(kc_ref, s_ref, r0_ref, r1_ref, r2_ref, o_ref):
        s = s_ref[0] if kind == "rows" else s_ref[...]
        o_ref[...] = ((s.astype(F32) + r0_ref[0].astype(F32)) + r1_ref[0].astype(F32)) + r2_ref[0].astype(F32)

    peer = lambda j: pl.BlockSpec((1, tr, C), lambda i, kc_ref: (j, i, 0))
    if kind == "rows":
        s_spec = pl.BlockSpec((1, tr, C), lambda i, kc_ref: (kc_ref[0], i, 0))
    else:
        s_spec = pl.BlockSpec((tr, C), lambda i, kc_ref: (i, kc_ref[0]))
    return pl.pallas_call(
        body, name=name, out_shape=jax.ShapeDtypeStruct((2 * h, C), F32),
        grid_spec=pltpu.PrefetchScalarGridSpec(
            num_scalar_prefetch=1, grid=(nt,),
            in_specs=[s_spec, peer(0), peer(1), peer(2)],
            out_specs=pl.BlockSpec((tr, C), lambda i, kc_ref: (kc_ref[1] * nt + i, 0))),
        compiler_params=_cparams(("parallel",)),
    )(kcvec, s1, r2, r2, r2)


def _join_sibling_halves(bufs):
    n = len(bufs)

    def body(*refs):
        ins, outs, send_sems, recv_sems = refs[:n], refs[n:2 * n], refs[2 * n], refs[2 * n + 1]
        x, y, c, _ = _place()
        cps = []
        for a in range(n):
            h = ins[a].shape[0] // 2
            cps.append(pltpu.make_async_remote_copy(
                src_ref=ins[a].at[pl.ds(c * h, h), :], dst_ref=outs[a].at[pl.ds(c * h, h), :], send_sem=send_sems.at[a],
                recv_sem=recv_sems.at[a], device_id=(x, y, 1 - c), device_id_type=MESH))
        for cp in cps:
            cp.start()
        for a in range(n):
            h = ins[a].shape[0] // 2
            theirs = outs[a].at[pl.ds((1 - c) * h, h), :]
            pltpu.make_async_remote_copy(src_ref=theirs, dst_ref=theirs, send_sem=send_sems.at[a],
                                         recv_sem=recv_sems.at[a], device_id=(x, y, 1 - c), device_id_type=MESH).wait_recv()
        for cp in cps:
            cp.wait_send()

    return pl.pallas_call(
        body, name="grads_join",
        out_shape=tuple(jax.ShapeDtypeStruct(b.shape, b.dtype) for b in bufs),
        in_specs=[ANY] * n, out_specs=tuple([ANY] * n), input_output_aliases={a: a for a in range(n)},
        scratch_shapes=[pltpu.SemaphoreType.DMA((n,)), pltpu.SemaphoreType.DMA((n,))],
    )(*bufs)


N_DEV = 8


def _allreduce_small(pack):
    P = pack.shape[0]

    def body(p_ref, o_ref, gath, send_sems, recv_sems):
        x, y, c, _ = _place()
        me = 4 * x + 2 * y + c
        cps = []
        for mask in range(1, N_DEV):
            px = 1 - x if mask & 4 else x
            py = 1 - y if mask & 2 else y
            pc = 1 - c if mask & 1 else c
            cps.append((pltpu.make_async_remote_copy(
                src_ref=p_ref, dst_ref=gath.at[me], send_sem=send_sems.at[mask - 1], recv_sem=recv_sems.at[mask - 1],
                device_id=(px, py, pc), device_id_type=MESH), 4 * px + 2 * py + pc, mask))
        for cp, _, _ in cps:
            cp.start()
        gath[me] = p_ref[...]
        for _, peer, mask in cps:
            pltpu.make_async_remote_copy(
                src_ref=p_ref, dst_ref=gath.at[peer], send_sem=send_sems.at[mask - 1], recv_sem=recv_sems.at[mask - 1],
                device_id=(x, y, c), device_id_type=MESH).wait_recv()
        for cp, _, _ in cps:
            cp.wait_send()
        acc = gath[0]
        for i in range(1, N_DEV):
            acc = acc + gath[i]
        o_ref[...] = acc

    return pl.pallas_call(
        body, name="allreduce_small", out_shape=jax.ShapeDtypeStruct((P, LANES), F32),
        in_specs=[pl.BlockSpec(memory_space=pltpu.VMEM)], out_specs=pl.BlockSpec(memory_space=pltpu.VMEM),
        scratch_shapes=[pltpu.VMEM((N_DEV, P, LANES), F32), pltpu.SemaphoreType.DMA((N_DEV - 1,)),
                        pltpu.SemaphoreType.DMA((N_DEV - 1,))],
    )(pack)


def _pack_rows(arrs):
    rows = []
    for a in arrs:
        f = a.reshape(-1)
        f = jnp.pad(f, (0, (-f.shape[0]) % (8 * LANES)))
        rows.append(f.reshape(-1, LANES))
    return jnp.concatenate(rows, axis=0)


def _unpack_rows(pack, shapes):
    out, r = [], 0
    for s in shapes:
        n = math.prod(s)
        out.append(pack[r:r + -(-n // LANES)].reshape(-1)[:n].reshape(s))
        r += 8 * -(-n // (8 * LANES))
    return out


_SMALL = ["norm_mix_pre", "ml_head_norm", "b_gate_a", "b_gate_b", "norm_mix_post", "norm_ffn_pre", "norm_ffn_post",
          "conv_b", "b_ml_i", "b_ml_f", "b_fox_f"]
_BIG = ["w_in", "w_branch_a", "w_branch_b", "w_out", "w_up", "w_down"]
_WEIGHTS = ['norm_mix_pre', 'w_in', 'b_ml_i', 'b_ml_f', 'ml_head_norm', 'b_fox_f', 'b_gate_a', 'b_gate_b', 'w_branch_a',
            'w_branch_b', 'w_out', 'norm_mix_post', 'norm_ffn_pre', 'w_up', 'conv_w', 'conv_b', 'w_down', 'norm_ffn_post']


_KINDS = ["rows", "rows", "rows", "rows", "cols", "rows"]


def kernel(x, norm_mix_pre, w_in, b_ml_i, b_ml_f, ml_head_norm, b_fox_f, b_gate_a, b_gate_b, w_branch_a, w_branch_b, w_out, norm_mix_post, norm_ffn_pre, w_up, conv_w, conv_b, w_down, norm_ffn_post, loss_target, m_norm_mix_pre, m_w_in, m_b_ml_i, m_b_ml_f, m_ml_head_norm, m_b_fox_f, m_b_gate_a, m_b_gate_b, m_w_branch_a, m_w_branch_b, m_w_out, m_norm_mix_post, m_norm_ffn_pre, m_w_up, m_conv_w, m_conv_b, m_w_down, m_norm_ffn_post, v_norm_mix_pre, v_w_in, v_b_ml_i, v_b_ml_f, v_ml_head_norm, v_b_fox_f, v_b_gate_a, v_b_gate_b, v_w_branch_a, v_w_branch_b, v_w_out, v_norm_mix_post, v_norm_ffn_pre, v_w_up, v_conv_w, v_conv_b, v_w_down, v_norm_ffn_post):
    args = dict(locals())
    w = {n: args[n] for n in _WEIGHTS}
    mom = {n: args["m_" + n] for n in _WEIGHTS}
    var = {n: args["v_" + n] for n in _WEIGHTS}
    cx, cy, cc = lax.axis_index("x"), lax.axis_index("y"), lax.axis_index("c")
    kme = 2 * cx + cy
    cvec = jnp.reshape(cc, (1,)).astype(jnp.int32)
    kcvec = jnp.stack([kme, cc]).astype(jnp.int32)
    odd = kme % 2

    tr3 = lambda t: jnp.transpose(t, (0, 2, 1))
    w["w_in"], mom["w_in"], var["w_in"] = tr3(w_in), tr3(m_w_in), tr3(v_w_in)
    w_in_main = lax.dynamic_slice_in_dim(w["w_in"][0], 4 * odd, 2048, axis=0).astype(BF16)
    w_in_gates = lax.dynamic_slice_in_dim(w["w_in"][0], 2048 * (1 - odd), 4, axis=0).astype(BF16)
    (wmain_t,), (g_cw, g_gates) = _gather_weights([w_in_main], _KINDS[:1], [w["conv_w"][0], w_in_gates])
    rest_started = _gather_start([w[n][0].astype(BF16) for n in _BIG[1:]], _KINDS[1:], "gather_rest_start")

    def rest_weights(after):
        bufs = _gather_wait(rest_started, after, _KINDS[1:], "gather_rest_wait")
        g_a, g_b, g_out, wup, g_down = _gather_relay(bufs, _KINDS[1:], "gather_rest_relay")
        return full(g_a), full(g_b), full(g_out), wup, full(g_down)
    gate_rows = g_gates.reshape(16, D_MODEL)
    wsmall_t = jnp.zeros((N_SMALL, D_MODEL), BF16)
    for blk, (lo, hi) in enumerate(((0, 4), (4, 8), (8, 16))):
        wsmall_t = wsmall_t.at[blk * LANES:blk * LANES + hi - lo].set(gate_rows[lo:hi])
    full = lambda g: g.reshape(-1, g.shape[2])
    p = {n: w[n] for n in _SMALL}
    p["conv_w"] = jnp.transpose(g_cw, (1, 0, 2)).reshape(3, -1)

    groups = {}

    def on_grads(group, gs):
        names = list(gs)
        kinds = [_KINDS[_BIG.index(n)] for n in names]
        whole = [g if k == "cols" else g.reshape(4, -1, g.shape[1]) for g, k in zip(gs.values(), kinds)]
        started = _exchange_start(_sibling_copies, 1, whole, [_half_shape(g, k) for g, k in zip(whole, kinds)], kinds,
                                  "grads_to_sibling_start_" + group)
        groups[group] = dict(names=names, kinds=kinds, sibling=started)
        return started[-1]

    def advance(group, after):
        G = groups[group]
        whole, got = _exchange_wait(_sibling_copies, G["sibling"], after, G["kinds"], "grads_to_sibling_wait_" + group)
        sums = [_add_halves(g, r, cvec, k, "add_sibling_" + n) for g, r, k, n in zip(whole, got, G["kinds"], G["names"])]
        G["chips"] = _exchange_start(_chip_copies, 3, sums, [_land_shape(s, k) for s, k in zip(sums, G["kinds"])],
                                     G["kinds"], "grads_to_chips_start_" + group)
        return G["chips"][-1]

    loss_row, grad_x, big, small = _local_step(x[0], loss_target[0], full(wmain_t), wsmall_t, rest_weights, p, on_grads,
                                               advance, rest_started[-1])
    grads = {}
    mine, mine_names = [], []
    for group, G in groups.items():
        sums, got = _exchange_wait(_chip_copies, G["chips"], grad_x, G["kinds"], "grads_to_chips_wait_" + group)
        mine += [_add_chips(s, r, kcvec, k, "add_chips_" + n) for s, r, k, n in zip(sums, got, G["kinds"], G["names"])]
        mine_names += G["names"]
    grads.update(zip(mine_names, _join_sibling_halves(mine)))

    gt = big["wsmall_t"]
    small["w_in_gates"] = jnp.concatenate([gt[0:4], gt[LANES:LANES + 4], gt[2 * LANES:2 * LANES + 8]], axis=0)
    small_names = _SMALL + ["conv_w"]
    packed_names = small_names + ["w_in_gates"]
    pack = _pack_rows([small[n] for n in packed_names] + [loss_row])
    pack = jnp.pad(pack, ((0, (-pack.shape[0]) % 8), (0, 0)))
    full_shapes = [small[n].shape if n in ("conv_w", "w_in_gates") else w[n][0].shape for n in packed_names]
    total = _unpack_rows(_allreduce_small(pack), full_shapes + [loss_row.shape])
    for n, t in zip(packed_names, total):
        grads[n] = t
    loss = total[-1][0, 0]
    grads["conv_w"] = lax.dynamic_slice_in_dim(grads["conv_w"], kme * conv_w.shape[2], conv_w.shape[2], axis=1)
    my_gates = lax.dynamic_slice_in_dim(grads.pop("w_in_gates"), 4 * kme, 4, axis=0)
    g_in = jnp.zeros(w["w_in"].shape[1:], F32)
    g_in = lax.dynamic_update_slice_in_dim(g_in, grads["w_in"], 4 * odd, axis=0)
    grads["w_in"] = lax.dynamic_update_slice_in_dim(g_in, my_gates, 2048 * (1 - odd), axis=0)

    delta, new_m, new_v = {}, {}, {}
    for n in _BIG:
        delta[n], new_m[n], new_v[n] = _adamw(w[n], grads[n], mom[n], var[n], "adamw_" + n)
        grads[n] = grads[n][None]
    for d in (grads, delta, new_m, new_v):
        d["w_in"] = tr3(d["w_in"])
    packs = [_pack_rows([d[n][0] for n in small_names]) for d in (w, mom, var)]
    pad = ((0, (-packs[0].shape[0]) % 8), (0, 0))
    packs = [jnp.pad(t, pad)[None] for t in packs]
    gp = jnp.pad(_pack_rows([grads[n] for n in small_names]), pad)
    shapes = [w[n][0].shape for n in small_names]
    for dst, res in zip((delta, new_m, new_v), _adamw(packs[0], gp, packs[1], packs[2], "adamw_small")):
        for n, t in zip(small_names, _unpack_rows(res[0], shapes)):
            dst[n] = t[None]
    for n in small_names:
        grads[n] = grads[n][None]

    return (loss, grad_x[None], *[grads[n] for n in _WEIGHTS], *[delta[n] for n in _WEIGHTS],
            *[new_m[n] for n in _WEIGHTS], *[new_v[n] for n in _WEIGHTS])
```

```python
import functools
import math

import jax
import jax.numpy as jnp
from jax import lax
from jax.experimental import pallas as pl
from jax.experimental.pallas import tpu as pltpu

F32 = jnp.float32
BF16 = jnp.bfloat16
MESH = pl.DeviceIdType.MESH

D_MODEL = 1024
ML_HEADS = 4
ML_DQK = 128
ML_DV = 256
FOX_HEADS = 8
FOX_DH = 128
D_FF = 2816
GATE_CAP = 15.0
EPS = 1e-6
ADAM_LR, ADAM_B1, ADAM_B2, ADAM_EPS, ADAM_WD, ADAM_STEP = 0.001, 0.9, 0.999, 1e-08, 0.01, 10

LANES = 128
MLC = 128
FOX_TQ = 512
FOX_TQ_FWD = 512
FOX_TK = 512
FOX_TK_FWD = 512
ROW_T = 512
VMEM_LIMIT = 56 * 1024 * 1024

C_QM, C_KM, C_VM, C_OM = 0, 512, 1024, 2048
N_ML, N_FOX, N_GATE = 3072, 3072, 2048
N_SMALL = 384


def _cparams(sem=None):
    return pltpu.CompilerParams(dimension_semantics=sem, vmem_limit_bytes=VMEM_LIMIT)


def _tile(n, target):
    if n <= target:
        return n
    best = None
    for t in range(LANES, target + 1, LANES):
        if n % t == 0:
            best = t
    assert best is not None, (n, target)
    return best


def _dot(a, b, dims):
    return lax.dot_general(a, b, (dims, ((), ())), preferred_element_type=F32)


def _dot_nn(a, b):
    return _dot(a, b, ((1,), (0,)))


def _dot_nt(a, b):
    return _dot(a, b, ((1,), (1,)))


def _dot_tn(a, b):
    return _dot(a, b, ((0,), (0,)))


_DOTS = {"nn": _dot_nn, "nt": _dot_nt, "tn": _dot_tn}


def _mm(a, b, mode, out_dtype, name, tm=1024, tn=1408, tk=1408, after=None):
    a_parts = list(a) if isinstance(a, (list, tuple)) else [a]
    b_parts = list(b) if isinstance(b, (list, tuple)) else [b]
    extra = [] if after is None else [after]
    assert len(a_parts) == 1 or len(b_parts) == 1, name
    a_axes = {"nn": "ik", "nt": "ik", "tn": "ki"}[mode]
    b_axes = {"nn": "kj", "nt": "jk", "tn": "kj"}[mode]
    size, target = {}, dict(i=tm, j=tn, k=tk)
    for parts, axes in ((a_parts, a_axes), (b_parts, b_axes)):
        dims = (parts[0].shape[0], parts[0].shape[1] * len(parts))
        for ax, n in zip(axes, dims):
            assert size.setdefault(ax, n) == n, (name, ax, n, size)
    tile = {}
    for parts, axes in ((a_parts, a_axes), (b_parts, b_axes)):
        if len(parts) > 1:
            tile[axes[1]] = _tile(parts[0].shape[1], target[axes[1]])
    for ax in "ijk":
        tile.setdefault(ax, _tile(size[ax], target[ax]))
    M, N, nk = size["i"], size["j"], size["k"] // tile["k"]
    grid_pos = dict(i=0, j=1, k=2)
    dot = _DOTS[mode]

    def specs(parts, axes):
        blk = (tile[axes[0]], tile[axes[1]])
        if len(parts) == 1:
            return [pl.BlockSpec(blk, lambda *g: (g[grid_pos[axes[0]]], g[grid_pos[axes[1]]]))], None
        bpp = parts[0].shape[1] // blk[1]

        def index(p):
            def f(*g):
                g0, g1 = g[grid_pos[axes[0]]], g[grid_pos[axes[1]]]
                on = g1 // bpp == p
                return jnp.where(on, g0, 0), jnp.where(on, g1 % bpp, 0)
            return f

        return [pl.BlockSpec(blk, index(p)) for p in range(len(parts))], (axes[1], bpp)

    a_specs, a_sel = specs(a_parts, a_axes)
    b_specs, b_sel = specs(b_parts, b_axes)
    na, nb = len(a_parts), len(b_parts)

    def body(*refs):
        a_refs, b_refs = refs[:na], refs[na:na + nb]
        o_ref, acc = refs[na + nb + len(extra)], refs[na + nb + len(extra) + 1:]

        def accumulate(part):
            if nk == 1:
                o_ref[...] = part.astype(o_ref.dtype)
                return
            acc_ref, = acc
            k = pl.program_id(2)

            @pl.when(k == 0)
            def _():
                acc_ref[...] = part

            @pl.when(k > 0)
            def _():
                acc_ref[...] += part

            @pl.when(k == nk - 1)
            def _():
                o_ref[...] = acc_ref[...].astype(o_ref.dtype)

        sel = a_sel or b_sel
        if sel is None:
            accumulate(dot(a_refs[0][...], b_refs[0][...]))
        else:
            which = pl.program_id(grid_pos[sel[0]]) // sel[1]
            for p in range(max(na, nb)):
                @pl.when(which == p)
                def _(p=p):
                    accumulate(dot(a_refs[p if a_sel else 0][...], b_refs[p if b_sel else 0][...]))

    return pl.pallas_call(
        body, name=name,
        out_shape=jax.ShapeDtypeStruct((M, N), out_dtype),
        grid=(M // tile["i"], N // tile["j"], nk),
        in_specs=a_specs + b_specs + [pl.BlockSpec(memory_space=pl.ANY)] * len(extra),
        out_specs=pl.BlockSpec((tile["i"], tile["j"]), lambda i, j, k: (i, j)),
        scratch_shapes=[pltpu.VMEM((tile["i"], tile["j"]), F32)] if nk > 1 else [],
        compiler_params=_cparams(("parallel", "parallel", "arbitrary")),
    )(*a_parts, *b_parts, *extra)


def _rstd(x):
    return lax.rsqrt(jnp.mean(x * x, axis=-1, keepdims=True) + EPS)


def _rmsnorm_fwd(x, g, name):
    S, D = x.shape
    T = _tile(S, ROW_T)

    def body(x_ref, g_ref, o_ref):
        xv = x_ref[...]
        o_ref[...] = (xv * _rstd(xv) * g_ref[...]).astype(o_ref.dtype)

    return pl.pallas_call(
        body, name=name, out_shape=jax.ShapeDtypeStruct((S, D), BF16), grid=(S // T,),
        in_specs=[pl.BlockSpec((T, D), lambda i: (i, 0)), pl.BlockSpec((1, D), lambda i: (0, 0))],
        out_specs=pl.BlockSpec((T, D), lambda i: (i, 0)),
        compiler_params=_cparams(("parallel",)),
    )(x, g)


def _resid_norm_fwd(x, z, g, name):
    S, D = x.shape
    T = _tile(S, ROW_T)

    def body(x_ref, z_ref, g_ref, o_ref):
        zv = z_ref[...]
        o_ref[...] = x_ref[...] + zv * _rstd(zv) * g_ref[...]

    row = pl.BlockSpec((T, D), lambda i: (i, 0))
    return pl.pallas_call(
        body, name=name, out_shape=jax.ShapeDtypeStruct((S, D), F32), grid=(S // T,),
        in_specs=[row, row, pl.BlockSpec((1, D), lambda i: (0, 0))],
        out_specs=row, compiler_params=_cparams(("parallel",)),
    )(x, z, g)


def _rmsnorm_bwd_math(dy, xv, g):
    r = _rstd(xv)
    u = dy * g
    dx = r * u - xv * (r * r * r) * jnp.mean(u * xv, axis=-1, keepdims=True)
    return dx, dy * xv * r


def _rmsnorm_bwd(dys, xin, g, resid, out_dtype, name):
    S, D = xin.shape
    T = _tile(S, ROW_T)
    has_resid = resid is not None
    ndy = len(dys)

    def body(*refs):
        dy_refs, (x_ref, g_ref) = refs[:ndy], refs[ndy:ndy + 2]
        dx_ref, dg_ref = refs[-2:]
        dy = dy_refs[0][...]
        for r in dy_refs[1:]:
            dy = dy + r[...]
        dx, dgt = _rmsnorm_bwd_math(dy, x_ref[...], g_ref[...])
        if has_resid:
            dx = dx + refs[ndy + 2][...]
        dx_ref[...] = dx.astype(dx_ref.dtype)

        @pl.when(pl.program_id(0) == 0)
        def _():
            dg_ref[...] = jnp.zeros_like(dg_ref)

        dg_ref[...] += jnp.sum(dgt, axis=0, keepdims=True)

    row = pl.BlockSpec((T, D), lambda i: (i, 0))
    vec = pl.BlockSpec((1, D), lambda i: (0, 0))
    ins = list(dys) + [xin, g] + ([resid] if has_resid else [])
    return pl.pallas_call(
        body, name=name,
        out_shape=(jax.ShapeDtypeStruct((S, D), out_dtype), jax.ShapeDtypeStruct((1, D), F32)),
        grid=(S // T,), in_specs=[row] * ndy + [row, vec] + ([row] if has_resid else []),
        out_specs=(row, vec), compiler_params=_cparams(("arbitrary",)),
    )(*ins)


def _loss_head(x1, d, g, target, name):
    S, D = x1.shape
    T = _tile(S, ROW_T)

    def body(x_ref, d_ref, g_ref, t_ref, loss_ref, dy_ref, dd_ref, dg_ref):
        dv, gv = d_ref[...], g_ref[...]
        y = x_ref[...] + dv * _rstd(dv) * gv
        diff = y - t_ref[...]
        dy = diff * (1.0 / D)
        dy_ref[...] = dy
        dd, dgt = _rmsnorm_bwd_math(dy, dv, gv)
        dd_ref[...] = dd.astype(dd_ref.dtype)

        @pl.when(pl.program_id(0) == 0)
        def _():
            dg_ref[...] = jnp.zeros_like(dg_ref)
            loss_ref[...] = jnp.zeros_like(loss_ref)

        dg_ref[...] += jnp.sum(dgt, axis=0, keepdims=True)
        part = jnp.sum(jnp.sum(diff * diff, axis=1, keepdims=True), axis=0, keepdims=True)
        loss_ref[...] += (0.5 / D) * part

    row = pl.BlockSpec((T, D), lambda i: (i, 0))
    vec = pl.BlockSpec((1, D), lambda i: (0, 0))
    return pl.pallas_call(
        body, name=name,
        out_shape=(jax.ShapeDtypeStruct((1, LANES), F32), jax.ShapeDtypeStruct((S, D), F32),
                   jax.ShapeDtypeStruct((S, D), BF16), jax.ShapeDtypeStruct((1, D), F32)),
        grid=(S // T,), in_specs=[row, row, vec, row],
        out_specs=(pl.BlockSpec((1, LANES), lambda i: (0, 0)), row, row, vec),
        compiler_params=_cparams(("arbitrary",)),
    )(x1, d, g, target)


def _merge_fwd(ya, yb, pm, ba, bb, name):
    S, D = ya.shape
    T = _tile(S, ROW_T)

    def body(ya_ref, yb_ref, ga_ref, gb_ref, ba_ref, bb_ref, o_ref):
        sa = jax.nn.sigmoid(ga_ref[...] + ba_ref[...])
        sb = jax.nn.sigmoid(gb_ref[...] + bb_ref[...])
        o_ref[...] = (sa * ya_ref[...] + sb * yb_ref[...]).astype(o_ref.dtype)

    row = pl.BlockSpec((T, D), lambda i: (i, 0))
    vec = pl.BlockSpec((1, D), lambda i: (0, 0))
    return pl.pallas_call(
        body, name=name, out_shape=jax.ShapeDtypeStruct((S, D), BF16), grid=(S // T,),
        in_specs=[row, row, pl.BlockSpec((T, D), lambda i: (i, 0)),
                  pl.BlockSpec((T, D), lambda i: (i, 1)), vec, vec],
        out_specs=row, compiler_params=_cparams(("parallel",)),
    )(ya, yb, pm, pm, ba, bb)


def _merge_bwd(dmerged, ya, yb, pm, ba, bb, name):
    S, D = ya.shape
    T = _tile(S, ROW_T)

    def body(dm_ref, ya_ref, yb_ref, ga_ref, gb_ref, ba_ref, bb_ref,
             dya_ref, dyb_ref, dga_ref, dgb_ref, dba_ref, dbb_ref):
        dm = dm_ref[...]
        sa = jax.nn.sigmoid(ga_ref[...] + ba_ref[...])
        sb = jax.nn.sigmoid(gb_ref[...] + bb_ref[...])
        dya_ref[...] = (dm * sa).astype(dya_ref.dtype)
        dyb_ref[...] = (dm * sb).astype(dyb_ref.dtype)
        dga = dm * ya_ref[...] * sa * (1.0 - sa)
        dgb = dm * yb_ref[...] * sb * (1.0 - sb)
        dga_ref[...] = dga.astype(dga_ref.dtype)
        dgb_ref[...] = dgb.astype(dgb_ref.dtype)

        @pl.when(pl.program_id(0) == 0)
        def _():
            dba_ref[...] = jnp.zeros_like(dba_ref)
            dbb_ref[...] = jnp.zeros_like(dbb_ref)

        dba_ref[...] += jnp.sum(dga, axis=0, keepdims=True)
        dbb_ref[...] += jnp.sum(dgb, axis=0, keepdims=True)

    row = pl.BlockSpec((T, D), lambda i: (i, 0))
    vec = pl.BlockSpec((1, D), lambda i: (0, 0))
    act = jax.ShapeDtypeStruct((S, D), BF16)
    v1 = jax.ShapeDtypeStruct((1, D), F32)
    return pl.pallas_call(
        body, name=name, out_shape=(act, act, act, act, v1, v1), grid=(S // T,),
        in_specs=[row, row, row, pl.BlockSpec((T, D), lambda i: (i, 0)),
                  pl.BlockSpec((T, D), lambda i: (i, 1)), vec, vec],
        out_specs=(row, row, row, row, vec, vec), compiler_params=_cparams(("arbitrary",)),
    )(dmerged, ya, yb, pm, pm, ba, bb)


_GELU_C = math.sqrt(2.0 / math.pi)


def _gelu(g):
    t = jnp.tanh(_GELU_C * (g + 0.044715 * g * g * g))
    return 0.5 * g * (1.0 + t), t


def _gelu_grad(g, t):
    return 0.5 * (1.0 + t) + 0.5 * g * (1.0 - t * t) * _GELU_C * (1.0 + 3 * 0.044715 * g * g)


def _shift_down(v, halo_ref, first, rows):
    T = v.shape[0]
    keep = jnp.where(first, 0.0, 1.0)
    h7 = halo_ref[7:8, :] * keep
    h6 = halo_ref[6:7, :] * keep
    m1 = jnp.where(rows == 0, h7, pltpu.roll(v, 1, 0))
    m2 = jnp.where(rows == 0, h6, jnp.where(rows == 1, h7, pltpu.roll(v, 2, 0)))
    return m1, m2


def _conv_act_fwd(up, cw, cb, name):
    S, F2 = up.shape
    Fh = F2 // 2
    T = _tile(S, ROW_T)
    tc = _tile(Fh, 256)
    ncol = Fh // tc
    hb = T // 8

    def body(ua_ref, ug_ref, ha_ref, hg_ref, wa_ref, wg_ref, ba_ref, bg_ref, o_ref, a_ref, g_ref):
        first = pl.program_id(0) == 0
        rows = lax.broadcasted_iota(jnp.int32, (T, tc), 0)

        def conv(u_ref, h_ref, w_ref, b_ref):
            v = u_ref[...]
            m1, m2 = _shift_down(v, h_ref, first, rows)
            return b_ref[...] + w_ref[0:1, :] * m2 + w_ref[1:2, :] * m1 + w_ref[2:3, :] * v

        a = conv(ua_ref, ha_ref, wa_ref, ba_ref)
        g = conv(ug_ref, hg_ref, wg_ref, bg_ref)
        a_ref[...] = a
        g_ref[...] = g
        o_ref[...] = (_gelu(g)[0] * a).astype(o_ref.dtype)

    halo = lambda off: pl.BlockSpec((8, tc), lambda i, j: (jnp.maximum(i * hb - 1, 0), j + off))
    blk = pl.BlockSpec((T, tc), lambda i, j: (i, j))
    f32 = jax.ShapeDtypeStruct((S, Fh), F32)
    return pl.pallas_call(
        body, name=name, out_shape=(jax.ShapeDtypeStruct((S, Fh), BF16), f32, f32), grid=(S // T, ncol),
        in_specs=[blk, pl.BlockSpec((T, tc), lambda i, j: (i, j + ncol)),
                  halo(0), halo(ncol),
                  pl.BlockSpec((3, tc), lambda i, j: (0, j)), pl.BlockSpec((3, tc), lambda i, j: (0, j + ncol)),
                  pl.BlockSpec((1, tc), lambda i, j: (0, j)), pl.BlockSpec((1, tc), lambda i, j: (0, j + ncol))],
        out_specs=(blk, blk, blk),
        compiler_params=_cparams(("parallel", "parallel")),
    )(up, up, up, up, cw, cw, cb, cb)


def _conv_act_bwd(up, a, g, dact, cw, name):
    S, F2 = up.shape
    Fh = F2 // 2
    T = _tile(S, ROW_T)
    tc = _tile(Fh, 256)
    ncol, nrow, hb, nhb = Fh // tc, S // T, T // 8, S // 8

    def body(ua_ref, ug_ref, a_ref, g_ref, an_ref, gn_ref, wa_ref, wg_ref, da_ref, dn_ref,
             dpa_ref, dpg_ref, dwa_ref, dwg_ref, dba_ref, dbg_ref, dua_n, dug_n):
        i = pl.program_id(1)
        rows = lax.broadcasted_iota(jnp.int32, (T, tc), 0)

        def du_of(a, g, dact_v):
            gel, t = _gelu(g)
            return dact_v * gel, dact_v * a * _gelu_grad(g, t)

        dua, dug = du_of(a_ref[...], g_ref[...], da_ref[...])
        keep = jnp.where(i == nrow - 1, 0.0, 1.0)
        dua_n[...], dug_n[...] = du_of(an_ref[...], gn_ref[...], dn_ref[...] * keep)

        @pl.when(i == 0)
        def _():
            for r in (dwa_ref, dwg_ref, dba_ref, dbg_ref):
                r[...] = jnp.zeros_like(r)

        for du, n_ref, u_ref, w_ref, o_ref, dw_ref, db_ref in ((dua, dua_n, ua_ref, wa_ref, dpa_ref, dwa_ref, dba_ref),
                                                               (dug, dug_n, ug_ref, wg_ref, dpg_ref, dwg_ref, dbg_ref)):
            n0, n1 = n_ref[0:1, :], n_ref[1:2, :]
            du1 = jnp.where(rows == T - 1, n0, pltpu.roll(du, T - 1, 0))
            du2 = jnp.where(rows == T - 2, n0, jnp.where(rows == T - 1, n1, pltpu.roll(du, T - 2, 0)))
            o_ref[...] = (w_ref[2:3, :] * du + w_ref[1:2, :] * du1 + w_ref[0:1, :] * du2).astype(o_ref.dtype)
            u = u_ref[...]
            db_ref[...] += jnp.sum(du, axis=0, keepdims=True)
            for j, d in enumerate((du2, du1, du)):
                dw_ref[j:j + 1, :] += jnp.sum(d * u, axis=0, keepdims=True)

    tile = lambda off: pl.BlockSpec((T, tc), lambda j, i: (i, j + off))
    under = pl.BlockSpec((8, tc), lambda j, i: (jnp.minimum((i + 1) * hb, nhb - 1), j))
    vec = lambda n, off: pl.BlockSpec((n, tc), lambda j, i: (0, j + off))
    act = jax.ShapeDtypeStruct((S, Fh), BF16)
    return pl.pallas_call(
        body, name=name,
        out_shape=(act, act, jax.ShapeDtypeStruct((3, Fh), F32), jax.ShapeDtypeStruct((3, Fh), F32),
                   jax.ShapeDtypeStruct((1, Fh), F32), jax.ShapeDtypeStruct((1, Fh), F32)),
        grid=(ncol, nrow),
        in_specs=[tile(0), tile(ncol), tile(0), tile(0), under, under, vec(3, 0), vec(3, ncol), tile(0), under],
        out_specs=(tile(0), tile(0), vec(3, 0), vec(3, 0), vec(1, 0), vec(1, 0)),
        scratch_shapes=[pltpu.VMEM((8, tc), F32), pltpu.VMEM((8, tc), F32)],
        compiler_params=_cparams(("parallel", "arbitrary")),
    )(up, up, a, g, a, g, cw, cw, dact, dact)


def _split3(x):
    hi = x.astype(BF16)
    r1 = x - hi.astype(F32)
    mid = r1.astype(BF16)
    lo = (r1 - mid.astype(F32)).astype(BF16)
    return hi, mid, lo


def _tri_dot(tri, x):
    hi, mid, lo = _split3(x)
    return _dot_nn(tri, hi) + _dot_nn(tri, mid) + _dot_nn(tri, lo)


def _log_sigmoid(x):
    return jnp.minimum(x, 0.0) - jnp.log(1.0 + jnp.exp(-jnp.abs(x)))


def _tri_mask(n, lower):
    r = lax.broadcasted_iota(jnp.int32, (n, n), 0)
    c = lax.broadcasted_iota(jnp.int32, (n, n), 1)
    return (r >= c) if lower else (r <= c)


def _gates_fwd(ps, bi, bf, bff, name):
    S = ps.shape[0]
    NC = S // MLC

    def body(ps_ref, bi_ref, bf_ref, bff_ref, a_ref, A_ref, wi_ref, em_ref, wk_ref, dec_ref, F_ref, m_scr, f_scr):
        @pl.when(pl.program_id(0) == 0)
        def _():
            m_scr[...] = jnp.zeros_like(m_scr)
            f_scr[...] = jnp.zeros_like(f_scr)

        rows = lax.broadcasted_iota(jnp.int32, (MLC, LANES), 0)
        ltri = _tri_mask(MLC, True).astype(BF16)
        li = GATE_CAP * jnp.tanh((ps_ref[:, 0:LANES] + bi_ref[...]) / GATE_CAP)
        lf = _log_sigmoid(GATE_CAP * jnp.tanh((ps_ref[:, LANES:2 * LANES] + bf_ref[...]) / GATE_CAP))
        b = _tri_dot(ltri, lf)
        a = li - b
        cm = a
        sh = 1
        while sh < MLC:
            cm = jnp.where(rows >= sh, jnp.maximum(cm, pltpu.roll(cm, sh, 0)), cm)
            sh *= 2
        m0 = m_scr[...]
        A = jnp.maximum(cm, m0)
        a_ref[...] = a
        A_ref[...] = A
        A_last = A_ref[MLC - 1:MLC, :]
        wi_ref[...] = jnp.exp(m0 - A)
        em_ref[...] = jnp.exp(-(b + A))
        wk_ref[...] = jnp.exp(a - A_last)
        dec_ref[0] = jnp.exp(m0 - A_last)
        F_ref[...] = b
        m_scr[...] = F_ref[MLC - 1:MLC, :] + A_last
        lfg = _log_sigmoid(ps_ref[:, 2 * LANES:3 * LANES] + bff_ref[...])
        F_ref[...] = _tri_dot(ltri, lfg) + f_scr[...]
        f_scr[...] = F_ref[MLC - 1:MLC, :]

    col = pl.BlockSpec((MLC, LANES), lambda c: (c, 0))
    vec = pl.BlockSpec((1, LANES), lambda c: (0, 0))
    cs = jax.ShapeDtypeStruct((S, LANES), F32)
    return pl.pallas_call(
        body, name=name,
        out_shape=(cs, cs, cs, cs, cs, jax.ShapeDtypeStruct((NC, 1, LANES), F32), cs),
        grid=(NC,), in_specs=[pl.BlockSpec((MLC, N_SMALL), lambda c: (c, 0)), vec, vec, vec],
        out_specs=(col, col, col, col, col, pl.BlockSpec((1, 1, LANES), lambda c: (c, 0, 0)), col),
        scratch_shapes=[pltpu.VMEM((1, LANES), F32), pltpu.VMEM((1, LANES), F32)],
        compiler_params=_cparams(("arbitrary",)),
    )(ps, bi, bf, bff)


def _gates_bwd(ps, bi, bf, bff, rk, kc, tch, dF, name):
    S = ps.shape[0]
    NC = S // MLC

    def body(ps_ref, bi_ref, bf_ref, bff_ref, rk_ref, kc_ref, t_ref, dF_ref, dps_ref, db_ref, carry):
        @pl.when(pl.program_id(0) == 0)
        def _():
            carry[...] = jnp.zeros_like(carry)
            db_ref[...] = jnp.zeros_like(db_ref)

        lanes = lax.broadcasted_iota(jnp.int32, (MLC, LANES), 1)
        utri = _tri_mask(MLC, False).astype(BF16)
        ti = jnp.tanh((ps_ref[:, 0:LANES] + bi_ref[...]) / GATE_CAP)
        t_end, t_start = t_ref[0, 0:1, :], t_ref[0, 1:2, :]
        rk = rk_ref[...]
        rk = rk - (jnp.sum(rk, axis=0, keepdims=True) - (t_start - t_end)) * (1.0 / MLC)
        dpi = jnp.where(lanes < ML_HEADS, (kc_ref[...] - rk) * (1.0 - ti * ti), 0.0)
        tf = jnp.tanh((ps_ref[:, LANES:2 * LANES] + bf_ref[...]) / GATE_CAP)
        dlf = _tri_dot(utri, rk) + t_end
        dpf = jnp.where(lanes < ML_HEADS, dlf * jax.nn.sigmoid(-GATE_CAP * tf) * (1.0 - tf * tf), 0.0)
        dFv = dF_ref[...]
        dlfg = _tri_dot(utri, dFv) + carry[...]
        carry[...] += jnp.sum(dFv, axis=0, keepdims=True)
        dpff = jnp.where(lanes < FOX_HEADS, dlfg * jax.nn.sigmoid(-(ps_ref[:, 2 * LANES:3 * LANES] + bff_ref[...])), 0.0)
        for n, dp in enumerate((dpi, dpf, dpff)):
            dps_ref[:, n * LANES:(n + 1) * LANES] = dp.astype(dps_ref.dtype)
            db_ref[:, n * LANES:(n + 1) * LANES] += jnp.sum(dp, axis=0, keepdims=True)

    rev = lambda c: (NC - 1 - c, 0)
    col = pl.BlockSpec((MLC, LANES), rev)
    vec = pl.BlockSpec((1, LANES), lambda c: (0, 0))
    wide = pl.BlockSpec((MLC, N_SMALL), rev)
    return pl.pallas_call(
        body, name=name,
        out_shape=(jax.ShapeDtypeStruct((S, N_SMALL), BF16), jax.ShapeDtypeStruct((1, N_SMALL), F32)),
        grid=(NC,),
        in_specs=[wide, vec, vec, vec, col, col, pl.BlockSpec((1, 2, LANES), lambda c: (NC - 1 - c, 0, 0)), col],
        out_specs=(wide, pl.BlockSpec((1, N_SMALL), lambda c: (0, 0))),
        scratch_shapes=[pltpu.VMEM((1, LANES), F32)],
        compiler_params=_cparams(("arbitrary",)),
    )(ps, bi, bf, bff, rk, kc, tch, dF)


_ML_SCALE = ML_DQK ** -0.5


def _ml_specs(rev, NC):
    idx = (lambda c: NC - 1 - c) if rev else (lambda c: c)
    qk = lambda blk: pl.BlockSpec((MLC, ML_HEADS * ML_DQK), lambda c: (idx(c), blk))
    wide = lambda blk: pl.BlockSpec((MLC, D_MODEL), lambda c: (idx(c), blk))
    col = pl.BlockSpec((MLC, LANES), lambda c: (idx(c), 0))
    return idx, qk, wide, col


def _ml_intra(q_ref, k_ref, arow_ref, A_ref, h):
    hs = slice(h * ML_DQK, (h + 1) * ML_DQK)
    qf = q_ref[:, hs] * _ML_SCALE
    kf = k_ref[:, hs]
    qb, kb = qf.astype(BF16), kf.astype(BF16)
    qk = _dot_nt(qb, kb)
    logw = arow_ref[h:h + 1, :] - A_ref[:, h:h + 1]
    W = jnp.exp(jnp.where(_tri_mask(MLC, True), logw, -1e30))
    return qb, kb, qf, kf, qk, W


def _mlstm_fwd(pm, a_row, A, wi, em, wk, dec, w_hn, name):
    S = pm.shape[0]
    NC = S // MLC
    _, qk, wide, col = _ml_specs(False, NC)

    def body(q_ref, k_ref, v_ref, o_ref, arow_ref, A_ref, wi_ref, em_ref, wk_ref, dec_ref, whn_ref,
             ha_ref, hp_ref, den_ref, cst_ref, nst_ref, C_scr, n_scr):
        @pl.when(pl.program_id(0) == 0)
        def _():
            C_scr[...] = jnp.zeros_like(C_scr)
            n_scr[...] = jnp.zeros_like(n_scr)

        lanes = lax.broadcasted_iota(jnp.int32, (MLC, LANES), 1)
        den_tile = jnp.zeros((MLC, LANES), F32)
        for h in range(ML_HEADS):
            vs = slice(h * ML_DV, (h + 1) * ML_DV)
            qb, kb, qf, kf, qk_, W = _ml_intra(q_ref, k_ref, arow_ref, A_ref, h)
            vb = v_ref[:, vs].astype(BF16)
            Cf = C_scr[h]
            Cb = Cf.astype(BF16)
            nrow = n_scr[h]
            cst_ref[0, h] = Cb
            nst_ref[0, h] = nrow
            s = qk_ * W
            wic = wi_ref[:, h:h + 1]
            num = _dot_nn(s.astype(BF16), vb) + wic * _dot_nt(qb, Cb)
            den = jnp.sum(s, axis=1, keepdims=True) + wic * jnp.sum(qf * nrow, axis=1, keepdims=True)
            hp = num / jnp.maximum(jnp.abs(den), em_ref[:, h:h + 1])
            hp_ref[:, vs] = hp
            den_tile = jnp.where(lanes == h, den, den_tile)
            hn = hp * _rstd(hp) * whn_ref[:, vs]
            ha_ref[:, vs] = (hn * jax.nn.sigmoid(o_ref[:, vs])).astype(ha_ref.dtype)
            wkc = wk_ref[:, h:h + 1]
            kw = kf * wkc
            d = dec_ref[0, :, h:h + 1]
            C_scr[h] = d * Cf + _dot_tn(vb, kw.astype(BF16))
            n_scr[h] = d * nrow + jnp.sum(kw, axis=0, keepdims=True)
        den_ref[...] = den_tile

    return pl.pallas_call(
        body, name=name,
        out_shape=(jax.ShapeDtypeStruct((S, D_MODEL), BF16), jax.ShapeDtypeStruct((S, D_MODEL), F32),
                   jax.ShapeDtypeStruct((S, LANES), F32),
                   jax.ShapeDtypeStruct((NC, ML_HEADS, ML_DV, ML_DQK), BF16),
                   jax.ShapeDtypeStruct((NC, ML_HEADS, 1, ML_DQK), F32)),
        grid=(NC,),
        in_specs=[qk(C_QM // 512), qk(C_KM // 512), wide(C_VM // D_MODEL), wide(C_OM // D_MODEL),
                  pl.BlockSpec((8, MLC), lambda c: (0, c)), col, col, col, col,
                  pl.BlockSpec((1, 1, LANES), lambda c: (c, 0, 0)), pl.BlockSpec((1, D_MODEL), lambda c: (0, 0))],
        out_specs=(pl.BlockSpec((MLC, D_MODEL), lambda c: (c, 0)), pl.BlockSpec((MLC, D_MODEL), lambda c: (c, 0)),
                   col, pl.BlockSpec((1, ML_HEADS, ML_DV, ML_DQK), lambda c: (c, 0, 0, 0)),
                   pl.BlockSpec((1, ML_HEADS, 1, ML_DQK), lambda c: (c, 0, 0, 0))),
        scratch_shapes=[pltpu.VMEM((ML_HEADS, ML_DV, ML_DQK), F32), pltpu.VMEM((ML_HEADS, 1, ML_DQK), F32)],
        compiler_params=_cparams(("arbitrary",)),
    )(pm, pm, pm, pm, a_row, A, wi, em, wk, dec, w_hn)


def _mlstm_bwd(dha, pm, hp_all, den_all, a_row, A, wi, em, wk, dec, cst, nst, w_hn, name):
    S = pm.shape[0]
    NC = S // MLC
    idx, qk, wide, col = _ml_specs(True, NC)

    def body(dha_ref, q_ref, k_ref, v_ref, o_ref, hp_ref, den_ref, arow_ref, A_ref, wi_ref, em_ref, wk_ref,
             dec_ref, cst_ref, nst_ref, whn_ref,
             dqk_ref, dv_ref, do_ref, rk_ref, kc_ref, t_ref, dwhn_ref, dC_scr, dn_scr, t_scr):
        @pl.when(pl.program_id(0) == 0)
        def _():
            dC_scr[...] = jnp.zeros_like(dC_scr)
            dn_scr[...] = jnp.zeros_like(dn_scr)
            t_scr[...] = jnp.zeros_like(t_scr)
            dwhn_ref[...] = jnp.zeros_like(dwhn_ref)

        lanes = lax.broadcasted_iota(jnp.int32, (MLC, LANES), 1)
        lane1 = lax.broadcasted_iota(jnp.int32, (1, LANES), 1)
        t_ref[0, 0:1, :] = t_scr[...]
        rk_tile = jnp.zeros((MLC, LANES), F32)
        kc_tile = jnp.zeros((MLC, LANES), F32)
        t_new = jnp.zeros((1, LANES), F32)
        for h in range(ML_HEADS):
            hs = slice(h * ML_DQK, (h + 1) * ML_DQK)
            vs = slice(h * ML_DV, (h + 1) * ML_DV)
            hp = hp_ref[:, vs]
            sig = jax.nn.sigmoid(o_ref[:, vs])
            whn = whn_ref[:, vs]
            r = _rstd(hp)
            dga = dha_ref[:, vs]
            do_ref[:, vs] = (dga * (hp * r * whn) * sig * (1.0 - sig)).astype(do_ref.dtype)
            dhn = dga * sig
            dhp, dwt = _rmsnorm_bwd_math(dhn, hp, whn)
            dwhn_ref[:, vs] += jnp.sum(dwt, axis=0, keepdims=True)
            den = den_ref[:, h:h + 1]
            floor = em_ref[:, h:h + 1]
            D = jnp.maximum(jnp.abs(den), floor)
            dnum = dhp / D
            dh_h = jnp.sum(dhp * hp, axis=1, keepdims=True)
            active = jnp.abs(den) >= floor
            dden = -dh_h / D * jnp.where(active, jnp.sign(den), 0.0)
            phi = jnp.where(active, 0.0, dh_h)
            qb, kb, qf, kf, qk_, W = _ml_intra(q_ref, k_ref, arow_ref, A_ref, h)
            vf = v_ref[:, vs]
            vb = vf.astype(BF16)
            Cb = cst_ref[0, h]
            nrow = nst_ref[0, h]
            wic = wi_ref[:, h:h + 1]
            wkc = wk_ref[:, h:h + 1]
            d = dec_ref[0, :, h:h + 1]
            dCn = dC_scr[h]
            dCb = dCn.astype(BF16)
            dnn = dn_scr[h]
            dnumb = dnum.astype(BF16)
            s = qk_ * W
            ds = (_dot_nt(dnumb, vb) + dden) * W
            dsb = ds.astype(BF16)
            dnw = (wic * dnum).astype(BF16)
            wd = wic * dden
            kw = kf * wkc
            dv_state = _dot_nt(kw.astype(BF16), dCb)
            dq = _dot_nn(dsb, kb) + _dot_nn(dnw, Cb) + wd * nrow
            dk_state = wkc * (_dot_nn(vb, dCb) + dnn)
            dk = _dot_tn(dsb, qb) + dk_state
            dv = _dot_tn(s.astype(BF16), dnumb) + dv_state
            dC = d * dCn + _dot_tn(dnw, qb)
            dn = d * dnn + jnp.sum(wd * qf, axis=0, keepdims=True)
            dC_scr[h] = dC
            dn_scr[h] = dn
            dqk_ref[:, hs] = (dq * _ML_SCALE).astype(dqk_ref.dtype)
            dqk_ref[:, C_KM + h * ML_DQK:C_KM + (h + 1) * ML_DQK] = dk.astype(dqk_ref.dtype)
            dv_ref[:, vs] = dv.astype(dv_ref.dtype)
            G = ds * qk_
            inter = _dot_nt(qb, Cb)
            qn = jnp.sum(qf * nrow, axis=1, keepdims=True)
            R = (jnp.sum(G, axis=1, keepdims=True)
                 + wic * (jnp.sum(dnum * inter, axis=1, keepdims=True) + dden * qn))
            K = jnp.sum(G.T, axis=1, keepdims=True) + jnp.sum(kf * dk_state, axis=1, keepdims=True)
            rk_tile = jnp.where(lanes == h, R - K, rk_tile)
            kc_tile = jnp.where(lanes == h, phi, kc_tile)
            tt = (jnp.sum(jnp.sum(dC * Cb.astype(F32), axis=1, keepdims=True), axis=0, keepdims=True)
                  + jnp.sum(dn * nrow, axis=1, keepdims=True))
            t_new = jnp.where(lane1 == h, tt, t_new)
        rk_ref[...] = rk_tile
        kc_ref[...] = kc_tile
        t_ref[0, 1:2, :] = t_new
        t_scr[...] = t_new

    act = lambda n: jax.ShapeDtypeStruct((S, n), BF16)
    cs = jax.ShapeDtypeStruct((S, LANES), F32)
    rowblk = lambda n: pl.BlockSpec((MLC, n), lambda c: (idx(c), 0))
    return pl.pallas_call(
        body, name=name,
        out_shape=(act(D_MODEL), act(D_MODEL), act(D_MODEL), cs, cs,
                   jax.ShapeDtypeStruct((NC, 2, LANES), F32), jax.ShapeDtypeStruct((1, D_MODEL), F32)),
        grid=(NC,),
        in_specs=[rowblk(D_MODEL), qk(C_QM // 512), qk(C_KM // 512), wide(C_VM // D_MODEL), wide(C_OM // D_MODEL),
                  rowblk(D_MODEL), col, pl.BlockSpec((8, MLC), lambda c: (0, idx(c))), col, col, col, col,
                  pl.BlockSpec((1, 1, LANES), lambda c: (idx(c), 0, 0)),
                  pl.BlockSpec((1, ML_HEADS, ML_DV, ML_DQK), lambda c: (idx(c), 0, 0, 0)),
                  pl.BlockSpec((1, ML_HEADS, 1, ML_DQK), lambda c: (idx(c), 0, 0, 0)),
                  pl.BlockSpec((1, D_MODEL), lambda c: (0, 0))],
        out_specs=(rowblk(D_MODEL), rowblk(D_MODEL), rowblk(D_MODEL), col, col,
                   pl.BlockSpec((1, 2, LANES), lambda c: (idx(c), 0, 0)), pl.BlockSpec((1, D_MODEL), lambda c: (0, 0))),
        scratch_shapes=[pltpu.VMEM((ML_HEADS, ML_DV, ML_DQK), F32), pltpu.VMEM((ML_HEADS, 1, ML_DQK), F32),
                        pltpu.VMEM((1, LANES), F32)],
        compiler_params=_cparams(("arbitrary",)),
    )(dha, pm, pm, pm, pm, hp_all, den_all, a_row, A, wi, em, wk, dec, cst, nst, w_hn)


_FOX_SCALE = FOX_DH ** -0.5
_NEG = -1e30
_LOG2E = 1.4426950408889634
_LN2 = 0.6931471805599453
_QF_BLK, _KF_BLK, _VF_BLK = 0, FOX_HEADS, 2 * FOX_HEADS


def _lane_pick(tile, lane):
    lanes = lax.broadcasted_iota(jnp.int32, tile.shape, 1)
    return jnp.sum(jnp.where(lanes == lane, tile, 0.0), axis=1, keepdims=True)


def _col_to_row(col):
    return jnp.max(jnp.broadcast_to(col, (col.shape[0], LANES)).T, axis=0, keepdims=True)


def _causal(q0, k0, shape, q_axis):
    qpos = q0 + lax.broadcasted_iota(jnp.int32, shape, q_axis)
    kpos = k0 + lax.broadcasted_iota(jnp.int32, shape, 1 - q_axis)
    return kpos <= qpos


def _fox_fwd(pf, fc, fk_row, name):
    S = pf.shape[0]
    TQ, TK = FOX_TQ_FWD, FOX_TK_FWD
    nq, nk = S // TQ, S // TK
    c1 = _FOX_SCALE * _LOG2E

    def body(q_ref, k_ref, v_ref, fc_ref, fr_ref, o_ref, lse_ref):
        h, i = pl.program_id(0), pl.program_id(1)
        qb = q_ref[...]
        fq2 = _lane_pick(fc_ref[...], h) * _LOG2E

        def step(j, carry, masked):
            m, l, acc = carry
            off = pl.multiple_of(j * TK, TK)
            t = _dot_nt(qb, k_ref[pl.ds(off, TK), :]) * c1 - fr_ref[0, j] * _LOG2E
            if masked:
                t = jnp.where(_causal(i * TQ, j * TK, (TQ, TK), 0), t, _NEG)
            m_new = jnp.maximum(m, jnp.max(t, axis=1, keepdims=True) + fq2)
            alpha = jnp.exp2(m - m_new)
            p = jnp.exp2(t + (fq2 - m_new))
            l = alpha * l + jnp.sum(p, axis=1, keepdims=True)
            acc = alpha * acc + _dot_nn(p.astype(BF16), v_ref[pl.ds(off, TK), :])
            return m_new, l, acc

        init = (jnp.full((TQ, 1), _NEG, F32), jnp.zeros((TQ, 1), F32), jnp.zeros((TQ, FOX_DH), F32))
        last = (i * TQ) // TK
        carry = lax.fori_loop(0, last, lambda j, c: step(j, c, False), init)
        m, l, acc = step(last, carry, True)
        o_ref[...] = (acc / l).astype(o_ref.dtype)
        lse_ref[0, 0] = _col_to_row((m + jnp.log2(l)) * _LN2)

    head = lambda blk: pl.BlockSpec((S, FOX_DH), lambda h, i: (0, blk + h))
    return pl.pallas_call(
        body, name=name,
        out_shape=(jax.ShapeDtypeStruct((S, D_MODEL), BF16), jax.ShapeDtypeStruct((FOX_HEADS, nq, 1, TQ), F32)),
        grid=(FOX_HEADS, nq),
        in_specs=[pl.BlockSpec((TQ, FOX_DH), lambda h, i: (i, _QF_BLK + h)), head(_KF_BLK), head(_VF_BLK),
                  pl.BlockSpec((TQ, LANES), lambda h, i: (i, 0)),
                  pl.BlockSpec((1, nk, 1, TK), lambda h, i: (h, 0, 0, 0))],
        out_specs=(pl.BlockSpec((TQ, FOX_DH), lambda h, i: (i, h)),
                   pl.BlockSpec((1, 1, 1, TQ), lambda h, i: (h, i, 0, 0))),
        compiler_params=_cparams(("parallel", "arbitrary")),
    )(pf, pf, pf, fc, fk_row)


def _fox_bwd(dhb, hb, pf, lse_row, fq_row, fc, name):
    S = pf.shape[0]
    TQ, TK = FOX_TQ, FOX_TK
    nq, nk, r = S // TQ, S // TK, TK // TQ
    c1 = _FOX_SCALE * _LOG2E

    def body(q_ref, k_ref, v_ref, do_ref, o_ref, lse_ref, fq_ref, fc_ref,
             dq_ref, dk_ref, dv_ref, dFk_ref, dFq_ref, dq_acc, qside, delta, dk_acc, dv_acc, cs_acc):
        h, j = pl.program_id(0), pl.program_id(1)

        @pl.when(j == 0)
        def _():
            dq_acc[...] = jnp.zeros_like(dq_acc)
            dFq_ref[...] = jnp.zeros_like(dFq_ref)

            def fill(b, _):
                off = pl.multiple_of(b * TQ, TQ)
                prod = do_ref[pl.ds(off, TQ), :].astype(F32) * o_ref[pl.ds(off, TQ), :].astype(F32)
                delta[b] = jnp.sum(prod.T, axis=0, keepdims=True)
                qside[b] = (fq_ref[0, b] - lse_ref[0, b]) * _LOG2E
                return 0

            lax.fori_loop(0, nq, fill, 0)

        kb = k_ref[...]
        vb = v_ref[...]
        fk2 = _lane_pick(fc_ref[...], h) * _LOG2E
        dk_acc[...] = jnp.zeros_like(dk_acc)
        dv_acc[...] = jnp.zeros_like(dv_acc)
        cs_acc[...] = jnp.zeros_like(cs_acc)

        def step(i, masked):
            off = pl.multiple_of(i * TQ, TQ)
            qb = q_ref[pl.ds(off, TQ), :]
            dob = do_ref[pl.ds(off, TQ), :]
            t = _dot_nt(kb, qb) * c1 + qside[i] - fk2
            if masked:
                t = jnp.where(_causal(i * TQ, j * TK, (TK, TQ), 1), t, _NEG)
            p = jnp.exp2(t)
            dv_acc[...] += _dot_nn(p.astype(BF16), dob)
            ds = p * (_dot_nt(vb, dob) - delta[i])
            dsb = ds.astype(BF16)
            dk_acc[...] += _dot_nn(dsb, qb)
            dq_acc[pl.ds(off, TQ), :] += _dot_tn(dsb, kb)
            cs_acc[...] += jnp.sum(ds, axis=1, keepdims=True)
            dFq_ref[0, i] += jnp.sum(ds, axis=0, keepdims=True)

        for d in range(r):
            step(r * j + d, True)

        def rest(i, _):
            step(i, False)
            return 0

        lax.fori_loop(r * j + r, nq, rest, 0)
        dk_ref[...] = (dk_acc[...] * _FOX_SCALE).astype(dk_ref.dtype)
        dv_ref[...] = dv_acc[...].astype(dv_ref.dtype)
        dFk_ref[0, 0] = -_col_to_row(cs_acc[...])

        @pl.when(j == nk - 1)
        def _():
            dq_ref[...] = (dq_acc[...] * _FOX_SCALE).astype(dq_ref.dtype)

    head = lambda blk: pl.BlockSpec((S, FOX_DH), lambda h, j: (0, blk + h))
    kblk = lambda blk: pl.BlockSpec((TK, FOX_DH), lambda h, j: (j, blk + h))
    qrows = pl.BlockSpec((1, nq, 1, TQ), lambda h, j: (h, 0, 0, 0))
    act = jax.ShapeDtypeStruct((S, D_MODEL), BF16)
    return pl.pallas_call(
        body, name=name,
        out_shape=(act, act, act, jax.ShapeDtypeStruct((FOX_HEADS, nk, 1, TK), F32),
                   jax.ShapeDtypeStruct((FOX_HEADS, nq, 1, TQ), F32)),
        grid=(FOX_HEADS, nk),
        in_specs=[head(_QF_BLK), kblk(_KF_BLK), kblk(_VF_BLK), head(0), head(0), qrows, qrows,
                  pl.BlockSpec((TK, LANES), lambda h, j: (j, 0))],
        out_specs=(head(0), kblk(0), kblk(0), pl.BlockSpec((1, 1, 1, TK), lambda h, j: (h, j, 0, 0)), qrows),
        scratch_shapes=[pltpu.VMEM((S, FOX_DH), F32), pltpu.VMEM((nq, 1, TQ), F32), pltpu.VMEM((nq, 1, TQ), F32),
                        pltpu.VMEM((TK, FOX_DH), F32), pltpu.VMEM((TK, FOX_DH), F32), pltpu.VMEM((TK, 1), F32)],
        compiler_params=_cparams(("parallel", "arbitrary")),
    )(pf, pf, pf, dhb, hb, lse_row, fq_row, fc)


def _pad_lanes(v):
    return jnp.pad(v, ((0, 0), (0, LANES - v.shape[1])))


def _local_step(x, target, wmain_t, wsmall_t, rest_weights, p, on_grads, advance, token):
    S = x.shape[0]
    bi, bf, bff = _pad_lanes(p["b_ml_i"]), _pad_lanes(p["b_ml_f"]), _pad_lanes(p["b_fox_f"])

    h0 = _rmsnorm_fwd(x, p["norm_mix_pre"] + token[0:1, 0:1], "norm_mix_pre")
    pm = _mm(h0, wmain_t[:N_ML], "nt", F32, "proj_mlstm")
    pf = _mm(h0, wmain_t[N_ML:N_ML + N_FOX], "nt", BF16, "proj_fox")
    pg = _mm(h0, wmain_t[N_ML + N_FOX:], "nt", F32, "proj_merge")
    ps = _mm(h0, wsmall_t, "nt", F32, "proj_gates")
    a, A, wi, em, wk, dec, Fc = _gates_fwd(ps, bi, bf, bff, "gates_fwd")
    a_row = a[:, :8].T
    ha, hp, den, cst, nst = _mlstm_fwd(pm, a_row, A, wi, em, wk, dec, p["ml_head_norm"], "mlstm_fwd")
    ft = Fc[:, :FOX_HEADS].T
    fq_row = ft.reshape(FOX_HEADS, S // FOX_TQ, 1, FOX_TQ)
    fk_row = ft.reshape(FOX_HEADS, S // FOX_TK, 1, FOX_TK)
    hb, lse_row = _fox_fwd(pf, Fc, ft.reshape(FOX_HEADS, S // FOX_TK_FWD, 1, FOX_TK_FWD), "fox_fwd")
    wa, wb, wout, wup, wdown = rest_weights(hb)
    ya = _mm(ha, wa, "nn", F32, "branch_a")
    yb = _mm(hb, wb, "nn", F32, "branch_b")
    merged = _merge_fwd(ya, yb, pg, p["b_gate_a"], p["b_gate_b"], "merge_fwd")
    z = _mm(merged, wout, "nn", F32, "out_proj")
    x1 = _resid_norm_fwd(x, z, p["norm_mix_post"], "resid_mix")
    h2 = _rmsnorm_fwd(x1, p["norm_ffn_pre"], "norm_ffn_pre")
    up = _mm(h2, wup, "nn", F32, "ffn_up")
    act, conv_a, conv_g = _conv_act_fwd(up, p["conv_w"], p["conv_b"], "conv_act_fwd")
    d = _mm(act, wdown, "nn", F32, "ffn_down")
    loss_row, dy, dd, g_norm_ffn_post = _loss_head(x1, d, p["norm_ffn_post"], target, "loss_head")
    dact = _mm(dd, wdown, "nt", F32, "d_act")
    g_wdown = _mm(act, dd, "tn", F32, "dw_down", tm=1408)
    dupa, dupg, dcwa, dcwg, dcba, dcbg = _conv_act_bwd(up, conv_a, conv_g, dact, p["conv_w"], "conv_act_bwd")
    g_conv_w = jnp.concatenate([dcwa, dcwg], axis=1)
    g_conv_b = jnp.concatenate([dcba, dcbg], axis=1)
    dh2 = _mm([dupa, dupg], wup, "nt", F32, "d_h2")
    g_wup = _mm(h2, [dupa, dupg], "tn", F32, "dw_up")
    token = on_grads("ffn", dict(w_up=g_wup, w_down=g_wdown))
    dx1, g_norm_ffn_pre = _rmsnorm_bwd([dh2], x1, p["norm_ffn_pre"] + token[0:1, 0:1], dy, F32, "norm_ffn_pre_bwd")
    dz, g_norm_mix_post = _rmsnorm_bwd([dx1], z, p["norm_mix_post"], None, BF16, "norm_mix_post_bwd")
    dmerged = _mm(dz, wout, "nt", F32, "d_merged")
    g_wout = _mm(merged, dz, "tn", F32, "dw_out")
    dya, dyb, dga, dgb, g_b_gate_a, g_b_gate_b = _merge_bwd(dmerged, ya, yb, pg, p["b_gate_a"], p["b_gate_b"], "merge_bwd")
    dha = _mm(dya, wa, "nt", F32, "d_ha")
    g_wa = _mm(ha, dya, "tn", F32, "dw_a")
    dhb = _mm(dyb, wb, "nt", BF16, "d_hb")
    g_wb = _mm(hb, dyb, "tn", F32, "dw_b")
    token = advance("ffn", g_wb) + on_grads("mix", dict(w_out=g_wout, w_branch_a=g_wa, w_branch_b=g_wb))
    dqkm, dvm, dom, rk, kc, tch, g_ml_head_norm = _mlstm_bwd(
        dha, pm, hp, den, a_row, A, wi, em, wk, dec, cst, nst, p["ml_head_norm"] + token[0:1, 0:1], "mlstm_bwd")
    token = advance("mix", dqkm)
    dqf, dkf, dvf, dFk, dFq = _fox_bwd(dhb, hb, pf, lse_row.reshape(fq_row.shape), fq_row + token[0, 0], Fc, "fox_bwd")
    dF = jnp.pad((dFk.reshape(FOX_HEADS, S) + dFq.reshape(FOX_HEADS, S)).T, ((0, 0), (0, LANES - FOX_HEADS)))
    dps, dbias = _gates_bwd(ps, bi, bf, bff, rk, kc, tch, dF, "gates_bwd")
    dpm = [dqkm, dvm, dom, dqf, dkf, dvf, dga, dgb]
    g_wmain_t = _mm(dpm, h0, "tn", F32, "dw_main")
    token = on_grads("in", dict(w_in=g_wmain_t))
    g_wsmall_t = _mm(dps, h0, "tn", F32, "dw_gates")
    dh0s = _mm(dps, wsmall_t + token[0:1, 0:1].astype(BF16), "nn", F32, "d_h0_gates")
    token = advance("in", dh0s)
    dh0 = _mm(dpm, wmain_t, "nn", F32, "d_h0_main", after=token)
    grad_x, g_norm_mix_pre = _rmsnorm_bwd([dh0, dh0s], x, p["norm_mix_pre"], dx1, F32, "norm_mix_pre_bwd")

    big = dict(wsmall_t=g_wsmall_t)
    small = dict(norm_mix_pre=g_norm_mix_pre, ml_head_norm=g_ml_head_norm, b_gate_a=g_b_gate_a, b_gate_b=g_b_gate_b,
                 norm_mix_post=g_norm_mix_post, norm_ffn_pre=g_norm_ffn_pre, norm_ffn_post=g_norm_ffn_post,
                 conv_b=g_conv_b, b_ml_i=dbias[:, 0:ML_HEADS], b_ml_f=dbias[:, LANES:LANES + ML_HEADS],
                 b_fox_f=dbias[:, 2 * LANES:2 * LANES + FOX_HEADS], conv_w=g_conv_w)
    return loss_row, grad_x, big, small


def _row_tile(r, target=256):
    best = None
    for t in range(8, min(r, target) + 1, 8):
        if r % t == 0:
            best = t
    return best if best is not None else r


def _adamw(w, g, m, v, name):
    _, R, C = w.shape
    tr = _row_tile(R)
    tc = C
    if tr == R and R > 256:
        tc = 256

    def body(w_ref, g_ref, m_ref, v_ref, d_ref, mo_ref, vo_ref):
        gv = g_ref[...]
        mn = ADAM_B1 * m_ref[0] + (1.0 - ADAM_B1) * gv
        vn = ADAM_B2 * v_ref[0] + (1.0 - ADAM_B2) * (gv * gv)
        m_hat = mn / (1.0 - ADAM_B1 ** ADAM_STEP)
        v_hat = vn / (1.0 - ADAM_B2 ** ADAM_STEP)
        d_ref[0] = -ADAM_LR * (m_hat / (jnp.sqrt(v_hat) + ADAM_EPS) + ADAM_WD * w_ref[0])
        mo_ref[0] = mn
        vo_ref[0] = vn

    blk = pl.BlockSpec((1, tr, tc), lambda i, j: (0, i, j))
    o = jax.ShapeDtypeStruct((1, R, C), F32)
    return pl.pallas_call(
        body, name=name, out_shape=(o, o, o), grid=(R // tr, C // tc),
        in_specs=[blk, pl.BlockSpec((tr, tc), lambda i, j: (i, j)), blk, blk], out_specs=(blk,) * 3,
        compiler_params=_cparams(("parallel", "parallel")),
    )(w, g, m, v)


ANY = pl.BlockSpec(memory_space=pl.ANY)


def _place():
    x, y, c = lax.axis_index("x"), lax.axis_index("y"), lax.axis_index("c")
    chips = [(1 - x, y), (x, 1 - y), (1 - x, 1 - y)]
    return x, y, c, chips


def _block(ref, kind, k, rows=None):
    if kind == "rows":
        return ref.at[k] if rows is None else ref.at[k, pl.ds(*rows), :]
    cb = ref.shape[1] // 4
    return ref.at[:, pl.ds(k * cb, cb)] if rows is None else ref.at[pl.ds(*rows), pl.ds(k * cb, cb)]


def _gathered_shape(s, kind):
    return (4,) + s.shape if kind == "rows" else (s.shape[0], 4 * s.shape[1])


def _gather_weights(shards, kinds, smalls):
    n, ns = len(shards), len(smalls)

    def body(*refs):
        ins, sm_in = refs[:n], refs[n:n + ns]
        outs, sm_out = refs[n + ns:2 * n + ns], refs[2 * n + ns:2 * (n + ns)]
        send_sems, recv_sems, sm_send, sm_recv, local_sems = refs[2 * (n + ns):]
        x, y, c, chips = _place()
        sibling = (x, y, 1 - c)
        kme = 2 * x + y

        def half(a, k, hc):
            h = ins[a].shape[0] // 2
            return _block(outs[a], kinds[a], k, (hc * h, h))

        def remote(a, slot, src, dst, to):
            return pltpu.make_async_remote_copy(src_ref=src, dst_ref=dst, send_sem=send_sems.at[a * 7 + slot],
                                                recv_sem=recv_sems.at[a * 7 + slot], device_id=to, device_id_type=MESH)

        def sm_copy(b, j, k, to):
            return pltpu.make_async_remote_copy(src_ref=sm_in[b], dst_ref=sm_out[b].at[k], send_sem=sm_send.at[3 * b + j],
                                                recv_sem=sm_recv.at[3 * b + j], device_id=to, device_id_type=MESH)

        local = [pltpu.make_async_copy(sm_in[b], sm_out[b].at[kme], local_sems.at[b]) for b in range(ns)]
        for cp in local:
            cp.start()
        sends = [remote(a, 6, ins[a], _block(outs[a], kinds[a], kme), sibling) for a in range(n)]
        for a in range(n):
            h = ins[a].shape[0] // 2
            for j, chip in enumerate(chips):
                sends.append(remote(a, j, ins[a].at[pl.ds(c * h, h), :], half(a, kme, c), (*chip, c)))
        for b in range(ns):
            for j, chip in enumerate(chips):
                sends.append(sm_copy(b, j, kme, (*chip, c)))
        for cp in sends:
            cp.start()
        for a in range(n):
            for j, chip in enumerate(chips):
                kj = 2 * chip[0] + chip[1]
                remote(a, j, half(a, kj, c), half(a, kj, c), (*chip, c)).wait_recv()
                fwd = remote(a, 3 + j, half(a, kj, c), half(a, kj, c), sibling)
                fwd.start()
                sends.append(fwd)
        for a in range(n):
            for j, chip in enumerate(chips):
                kj = 2 * chip[0] + chip[1]
                remote(a, 3 + j, half(a, kj, 1 - c), half(a, kj, 1 - c), sibling).wait_recv()
        for b in range(ns):
            for j, chip in enumerate(chips):
                sm_copy(b, j, 2 * chip[0] + chip[1], (*chip, c)).wait_recv()
        for a in range(n):
            remote(a, 6, ins[a], _block(outs[a], kinds[a], kme), sibling).wait_recv()
        for cp in sends:
            cp.wait_send()
        for cp in local:
            cp.wait()

    outs = pl.pallas_call(
        body, name="gather_weights",
        out_shape=tuple([jax.ShapeDtypeStruct(_gathered_shape(s, k), s.dtype) for s, k in zip(shards, kinds)]
                        + [jax.ShapeDtypeStruct((4,) + s.shape, s.dtype) for s in smalls]),
        in_specs=[ANY] * (n + ns), out_specs=tuple([ANY] * (n + ns)),
        scratch_shapes=[pltpu.SemaphoreType.DMA((7 * n,)), pltpu.SemaphoreType.DMA((7 * n,)),
                        pltpu.SemaphoreType.DMA((3 * ns,)), pltpu.SemaphoreType.DMA((3 * ns,)),
                        pltpu.SemaphoreType.DMA((ns,))],
    )(*shards, *smalls)
    return outs[:n], outs[n:]


_IN_HBM = pl.BlockSpec(memory_space=pltpu.HBM)
_SEMS = pl.BlockSpec(memory_space=pltpu.SEMAPHORE)
_DATAFLOW = pltpu.SideEffectType.DATAFLOW_SIDE_EFFECTING


def _hbm(t):
    return pltpu.HBM(t.shape, t.dtype)


def _gather_copies(ins, outs, send_sems, recv_sems, kinds):
    x, y, c, chips = _place()
    kme = 2 * x + y
    cps = []
    for a in range(len(ins)):
        h = ins[a].shape[0] // 2
        for j, chip in enumerate(chips + [None]):
            to = (x, y, 1 - c) if chip is None else (*chip, c)
            src = ins[a] if chip is None else ins[a].at[pl.ds(c * h, h), :]
            dst = _block(outs[a], kinds[a], kme, None if chip is None else (c * h, h))
            cps.append(pltpu.make_async_remote_copy(src_ref=src, dst_ref=dst, send_sem=send_sems.at[4 * a + j],
                                                    recv_sem=recv_sems.at[4 * a + j], device_id=to, device_id_type=MESH))
    return cps


def _gather_start(shards, kinds, name):
    n = len(shards)
    outs = [lax.empty(_gathered_shape(s, k), s.dtype) for s, k in zip(shards, kinds)]

    def body(*refs):
        for cp in _gather_copies(refs[:n], refs[n:2 * n], refs[2 * n], refs[2 * n + 1], kinds):
            cp.start()
        refs[-1][...] = jnp.zeros_like(refs[-1])

    return pl.pallas_call(
        body, name=name,
        out_shape=(pltpu.SemaphoreType.DMA((4 * n,)), pltpu.SemaphoreType.DMA((4 * n,)),
                   *[_hbm(t) for t in shards], *[_hbm(t) for t in outs], jax.ShapeDtypeStruct((8, LANES), F32)),
        in_specs=[_IN_HBM] * (2 * n),
        out_specs=(_SEMS, _SEMS, *[_IN_HBM] * (2 * n), pl.BlockSpec(memory_space=pltpu.VMEM)),
        input_output_aliases={a: 2 + a for a in range(2 * n)},
        compiler_params=pltpu.CompilerParams(has_side_effects=_DATAFLOW),
    )(*[pltpu.with_memory_space_constraint(t, pltpu.HBM) for t in list(shards) + outs])


def _gather_wait(started, after, kinds, name):
    n = (len(started) - 3) // 2
    bufs = started[2:2 + 2 * n]

    def body(*refs):
        for cp in _gather_copies(refs[:n], refs[n:2 * n], refs[2 * n], refs[2 * n + 1], kinds):
            cp.wait_send()
            cp.wait_recv()

    outs = pl.pallas_call(
        body, name=name, out_shape=tuple(_hbm(t) for t in bufs),
        in_specs=[_IN_HBM] * (2 * n) + [_SEMS, _SEMS, ANY], out_specs=tuple([_IN_HBM] * (2 * n)),
        input_output_aliases={a: a for a in range(2 * n)},
        compiler_params=pltpu.CompilerParams(has_side_effects=_DATAFLOW),
    )(*bufs, started[0], started[1], after)
    return outs[n:]


def _gather_relay(bufs, kinds, name):
    n = len(bufs)

    def body(*refs):
        ins, outs, send_sems, recv_sems = refs[:n], refs[n:2 * n], refs[2 * n], refs[2 * n + 1]
        x, y, c, chips = _place()
        cps = []
        for a in range(n):
            h = (ins[a].shape[1] if kinds[a] == "rows" else ins[a].shape[0]) // 2
            for j, chip in enumerate(chips):
                kj = 2 * chip[0] + chip[1]
                cps.append((pltpu.make_async_remote_copy(
                    src_ref=_block(ins[a], kinds[a], kj, (c * h, h)), dst_ref=_block(outs[a], kinds[a], kj, (c * h, h)),
                    send_sem=send_sems.at[3 * a + j], recv_sem=recv_sems.at[3 * a + j], device_id=(x, y, 1 - c),
                    device_id_type=MESH), a, kj, h))
        for cp, _, _, _ in cps:
            cp.start()
        for a_cp, (cp, a, kj, h) in enumerate(cps):
            theirs = _block(outs[a], kinds[a], kj, ((1 - c) * h, h))
            pltpu.make_async_remote_copy(src_ref=theirs, dst_ref=theirs, send_sem=send_sems.at[a_cp],
                                         recv_sem=recv_sems.at[a_cp], device_id=(x, y, 1 - c), device_id_type=MESH).wait_recv()
        for cp, _, _, _ in cps:
            cp.wait_send()

    return pl.pallas_call(
        body, name=name, out_shape=tuple(jax.ShapeDtypeStruct(b.shape, b.dtype) for b in bufs),
        in_specs=[ANY] * n, out_specs=tuple([ANY] * n), input_output_aliases={a: a for a in range(n)},
        scratch_shapes=[pltpu.SemaphoreType.DMA((3 * n,)), pltpu.SemaphoreType.DMA((3 * n,))],
    )(*bufs)


def _add_halves(g, r1, cvec, kind, name):
    def body(c_ref, g_ref, r_ref, o_ref):
        o_ref[...] = (g_ref[...] + r_ref[...]).astype(o_ref.dtype)

    if kind == "rows":
        _, h, C = r1.shape
        tr = _row_tile(h)
        nt = h // tr
        grid = (4, nt)
        g_spec = pl.BlockSpec((1, tr, C), lambda k, i, c_ref: (k, c_ref[0] * nt + i, 0))
        r_spec = pl.BlockSpec((1, tr, C), lambda k, i, c_ref: (k, i, 0))
    else:
        h, C4 = r1.shape
        tr, tc = _row_tile(h), C4 // 4
        nt = h // tr
        grid = (nt, 4)
        g_spec = pl.BlockSpec((tr, tc), lambda i, k, c_ref: (c_ref[0] * nt + i, k))
        r_spec = pl.BlockSpec((tr, tc), lambda i, k, c_ref: (i, k))
    return pl.pallas_call(
        body, name=name, out_shape=jax.ShapeDtypeStruct(r1.shape, BF16),
        grid_spec=pltpu.PrefetchScalarGridSpec(num_scalar_prefetch=1, grid=grid, in_specs=[g_spec, r_spec],
                                               out_specs=r_spec),
        compiler_params=_cparams(("parallel", "parallel")),
    )(cvec, g, r1)


def _chip_copies(ins, lands, send_sems, recv_sems, kinds):
    x, y, c, chips = _place()
    return [pltpu.make_async_remote_copy(
        src_ref=_block(ins[a], kinds[a], 2 * chip[0] + chip[1]), dst_ref=lands[a].at[j],
        send_sem=send_sems.at[3 * a + j], recv_sem=recv_sems.at[3 * a + j], device_id=(*chip, c), device_id_type=MESH)
        for a in range(len(ins)) for j, chip in enumerate(chips)]


def _land_shape(s, kind):
    return (3,) + (s.shape[1:] if kind == "rows" else (s.shape[0], s.shape[1] // 4))


def _sibling_copies(ins, lands, send_sems, recv_sems, kinds):
    x, y, c, _ = _place()
    cps = []
    for a in range(len(ins)):
        h = lands[a].shape[-2]
        src = ins[a].at[:, pl.ds((1 - c) * h, h), :] if kinds[a] == "rows" else ins[a].at[pl.ds((1 - c) * h, h), :]
        cps.append(pltpu.make_async_remote_copy(src_ref=src, dst_ref=lands[a], send_sem=send_sems.at[a],
                                                recv_sem=recv_sems.at[a], device_id=(x, y, 1 - c), device_id_type=MESH))
    return cps


def _half_shape(g, kind):
    return (4, g.shape[1] // 2, g.shape[2]) if kind == "rows" else (g.shape[0] // 2, g.shape[1])


def _exchange_start(copies, per_array, srcs, land_shapes, kinds, name):
    n = len(srcs)
    lands = [lax.empty(shape, s.dtype) for shape, s in zip(land_shapes, srcs)]

    def body(*refs):
        for cp in copies(refs[:n], refs[n:2 * n], refs[2 * n], refs[2 * n + 1], kinds):
            cp.start()
        refs[-1][...] = jnp.zeros_like(refs[-1])

    return pl.pallas_call(
        body, name=name,
        out_shape=(pltpu.SemaphoreType.DMA((per_array * n,)), pltpu.SemaphoreType.DMA((per_array * n,)),
                   *[_hbm(t) for t in srcs], *[_hbm(t) for t in lands], jax.ShapeDtypeStruct((8, LANES), F32)),
        in_specs=[_IN_HBM] * (2 * n),
        out_specs=(_SEMS, _SEMS, *[_IN_HBM] * (2 * n), pl.BlockSpec(memory_space=pltpu.VMEM)),
        input_output_aliases={a: 2 + a for a in range(2 * n)},
        compiler_params=pltpu.CompilerParams(has_side_effects=_DATAFLOW),
    )(*[pltpu.with_memory_space_constraint(t, pltpu.HBM) for t in list(srcs) + lands])


def _exchange_wait(copies, started, after, kinds, name):
    n = (len(started) - 3) // 2
    bufs = started[2:2 + 2 * n]

    def body(*refs):
        for cp in copies(refs[:n], refs[n:2 * n], refs[2 * n], refs[2 * n + 1], kinds):
            cp.wait_send()
            cp.wait_recv()

    outs = pl.pallas_call(
        body, name=name, out_shape=tuple(_hbm(t) for t in bufs),
        in_specs=[_IN_HBM] * (2 * n) + [_SEMS, _SEMS, ANY], out_specs=tuple([_IN_HBM] * (2 * n)),
        input_output_aliases={a: a for a in range(2 * n)},
        compiler_params=pltpu.CompilerParams(has_side_effects=_DATAFLOW),
    )(*bufs, started[0], started[1], after)
    return outs[:n], outs[n:]


def _add_chips(s1, r2, kcvec, kind, name):
    _, h, C = r2.shape
    tr = _row_tile(h)
    nt = h // tr

    def body(kc_ref, s_ref, r0_ref, r1_ref, r2_ref, o_ref):
        s = s_ref[0] if kind == "rows" else s_ref[...]
        o_ref[...] = ((s.astype(F32) + r0_ref[0].astype(F32)) + r1_ref[0].astype(F32)) + r2_ref[0].astype(F32)

    peer = lambda j: pl.BlockSpec((1, tr, C), lambda i, kc_ref: (j, i, 0))
    if kind == "rows":
        s_spec = pl.BlockSpec((1, tr, C), lambda i, kc_ref: (kc_ref[0], i, 0))
    else:
        s_spec = pl.BlockSpec((tr, C), lambda i, kc_ref: (i, kc_ref[0]))
    return pl.pallas_call(
        body, name=name, out_shape=jax.ShapeDtypeStruct((2 * h, C), F32),
        grid_spec=pltpu.PrefetchScalarGridSpec(
            num_scalar_prefetch=1, grid=(nt,),
            in_specs=[s_spec, peer(0), peer(1), peer(2)],
            out_specs=pl.BlockSpec((tr, C), lambda i, kc_ref: (kc_ref[1] * nt + i, 0))),
        compiler_params=_cparams(("parallel",)),
    )(kcvec, s1, r2, r2, r2)


def _join_sibling_halves(bufs):
    n = len(bufs)

    def body(*refs):
        ins, outs, send_sems, recv_sems = refs[:n], refs[n:2 * n], refs[2 * n], refs[2 * n + 1]
        x, y, c, _ = _place()
        cps = []
        for a in range(n):
            h = ins[a].shape[0] // 2
            cps.append(pltpu.make_async_remote_copy(
                src_ref=ins[a].at[pl.ds(c * h, h), :], dst_ref=outs[a].at[pl.ds(c * h, h), :], send_sem=send_sems.at[a],
                recv_sem=recv_sems.at[a], device_id=(x, y, 1 - c), device_id_type=MESH))
        for cp in cps:
            cp.start()
        for a in range(n):
            h = ins[a].shape[0] // 2
            theirs = outs[a].at[pl.ds((1 - c) * h, h), :]
            pltpu.make_async_remote_copy(src_ref=theirs, dst_ref=theirs, send_sem=send_sems.at[a],
                                         recv_sem=recv_sems.at[a], device_id=(x, y, 1 - c), device_id_type=MESH).wait_recv()
        for cp in cps:
            cp.wait_send()

    return pl.pallas_call(
        body, name="grads_join",
        out_shape=tuple(jax.ShapeDtypeStruct(b.shape, b.dtype) for b in bufs),
        in_specs=[ANY] * n, out_specs=tuple([ANY] * n), input_output_aliases={a: a for a in range(n)},
        scratch_shapes=[pltpu.SemaphoreType.DMA((n,)), pltpu.SemaphoreType.DMA((n,))],
    )(*bufs)


N_DEV = 8


def _allreduce_small(pack):
    P = pack.shape[0]

    def body(p_ref, o_ref, gath, send_sems, recv_sems):
        x, y, c, _ = _place()
        me = 4 * x + 2 * y + c
        cps = []
        for mask in range(1, N_DEV):
            px = 1 - x if mask & 4 else x
            py = 1 - y if mask & 2 else y
            pc = 1 - c if mask & 1 else c
            cps.append((pltpu.make_async_remote_copy(
                src_ref=p_ref, dst_ref=gath.at[me], send_sem=send_sems.at[mask - 1], recv_sem=recv_sems.at[mask - 1],
                device_id=(px, py, pc), device_id_type=MESH), 4 * px + 2 * py + pc, mask))
        for cp, _, _ in cps:
            cp.start()
        gath[me] = p_ref[...]
        for _, peer, mask in cps:
            pltpu.make_async_remote_copy(
                src_ref=p_ref, dst_ref=gath.at[peer], send_sem=send_sems.at[mask - 1], recv_sem=recv_sems.at[mask - 1],
                device_id=(x, y, c), device_id_type=MESH).wait_recv()
        for cp, _, _ in cps:
            cp.wait_send()
        acc = gath[0]
        for i in range(1, N_DEV):
            acc = acc + gath[i]
        o_ref[...] = acc

    return pl.pallas_call(
        body, name="allreduce_small", out_shape=jax.ShapeDtypeStruct((P, LANES), F32),
        in_specs=[pl.BlockSpec(memory_space=pltpu.VMEM)], out_specs=pl.BlockSpec(memory_space=pltpu.VMEM),
        scratch_shapes=[pltpu.VMEM((N_DEV, P, LANES), F32), pltpu.SemaphoreType.DMA((N_DEV - 1,)),
                        pltpu.SemaphoreType.DMA((N_DEV - 1,))],
    )(pack)


def _pack_rows(arrs):
    rows = []
    for a in arrs:
        f = a.reshape(-1)
        f = jnp.pad(f, (0, (-f.shape[0]) % (8 * LANES)))
        rows.append(f.reshape(-1, LANES))
    return jnp.concatenate(rows, axis=0)


def _unpack_rows(pack, shapes):
    out, r = [], 0
    for s in shapes:
        n = math.prod(s)
        out.append(pack[r:r + -(-n // LANES)].reshape(-1)[:n].reshape(s))
        r += 8 * -(-n // (8 * LANES))
    return out


_SMALL = ["norm_mix_pre", "ml_head_norm", "b_gate_a", "b_gate_b", "norm_mix_post", "norm_ffn_pre", "norm_ffn_post",
          "conv_b", "b_ml_i", "b_ml_f", "b_fox_f"]
_BIG = ["w_in", "w_branch_a", "w_branch_b", "w_out", "w_up", "w_down"]
_WEIGHTS = ['norm_mix_pre', 'w_in', 'b_ml_i', 'b_ml_f', 'ml_head_norm', 'b_fox_f', 'b_gate_a', 'b_gate_b', 'w_branch_a',
            'w_branch_b', 'w_out', 'norm_mix_post', 'norm_ffn_pre', 'w_up', 'conv_w', 'conv_b', 'w_down', 'norm_ffn_post']


_KINDS = ["rows", "rows", "rows", "rows", "cols", "rows"]


def kernel(x, norm_mix_pre, w_in, b_ml_i, b_ml_f, ml_head_norm, b_fox_f, b_gate_a, b_gate_b, w_branch_a, w_branch_b, w_out, norm_mix_post, norm_ffn_pre, w_up, conv_w, conv_b, w_down, norm_ffn_post, loss_target, m_norm_mix_pre, m_w_in, m_b_ml_i, m_b_ml_f, m_ml_head_norm, m_b_fox_f, m_b_gate_a, m_b_gate_b, m_w_branch_a, m_w_branch_b, m_w_out, m_norm_mix_post, m_norm_ffn_pre, m_w_up, m_conv_w, m_conv_b, m_w_down, m_norm_ffn_post, v_norm_mix_pre, v_w_in, v_b_ml_i, v_b_ml_f, v_ml_head_norm, v_b_fox_f, v_b_gate_a, v_b_gate_b, v_w_branch_a, v_w_branch_b, v_w_out, v_norm_mix_post, v_norm_ffn_pre, v_w_up, v_conv_w, v_conv_b, v_w_down, v_norm_ffn_post):
    args = dict(locals())
    w = {n: args[n] for n in _WEIGHTS}
    mom = {n: args["m_" + n] for n in _WEIGHTS}
    var = {n: args["v_" + n] for n in _WEIGHTS}
    cx, cy, cc = lax.axis_index("x"), lax.axis_index("y"), lax.axis_index("c")
    kme = 2 * cx + cy
    cvec = jnp.reshape(cc, (1,)).astype(jnp.int32)
    kcvec = jnp.stack([kme, cc]).astype(jnp.int32)
    odd = kme % 2

    tr3 = lambda t: jnp.transpose(t, (0, 2, 1))
    w["w_in"], mom["w_in"], var["w_in"] = tr3(w_in), tr3(m_w_in), tr3(v_w_in)
    w_in_main = lax.dynamic_slice_in_dim(w["w_in"][0], 4 * odd, 2048, axis=0).astype(BF16)
    w_in_gates = lax.dynamic_slice_in_dim(w["w_in"][0], 2048 * (1 - odd), 4, axis=0).astype(BF16)
    (wmain_t,), (g_cw, g_gates) = _gather_weights([w_in_main], _KINDS[:1], [w["conv_w"][0], w_in_gates])
    rest_started = _gather_start([w[n][0].astype(BF16) for n in _BIG[1:]], _KINDS[1:], "gather_rest_start")

    def rest_weights(after):
        bufs = _gather_wait(rest_started, after, _KINDS[1:], "gather_rest_wait")
        g_a, g_b, g_out, wup, g_down = _gather_relay(bufs, _KINDS[1:], "gather_rest_relay")
        return full(g_a), full(g_b), full(g_out), wup, full(g_down)
    gate_rows = g_gates.reshape(16, D_MODEL)
    wsmall_t = jnp.zeros((N_SMALL, D_MODEL), BF16)
    for blk, (lo, hi) in enumerate(((0, 4), (4, 8), (8, 16))):
        wsmall_t = wsmall_t.at[blk * LANES:blk * LANES + hi - lo].set(gate_rows[lo:hi])
    full = lambda g: g.reshape(-1, g.shape[2])
    p = {n: w[n] for n in _SMALL}
    p["conv_w"] = jnp.transpose(g_cw, (1, 0, 2)).reshape(3, -1)

    groups = {}

    def on_grads(group, gs):
        names = list(gs)
        kinds = [_KINDS[_BIG.index(n)] for n in names]
        whole = [g if k == "cols" else g.reshape(4, -1, g.shape[1]) for g, k in zip(gs.values(), kinds)]
        started = _exchange_start(_sibling_copies, 1, whole, [_half_shape(g, k) for g, k in zip(whole, kinds)], kinds,
                                  "grads_to_sibling_start_" + group)
        groups[group] = dict(names=names, kinds=kinds, sibling=started)
        return started[-1]

    def advance(group, after):
        G = groups[group]
        whole, got = _exchange_wait(_sibling_copies, G["sibling"], after, G["kinds"], "grads_to_sibling_wait_" + group)
        sums = [_add_halves(g, r, cvec, k, "add_sibling_" + n) for g, r, k, n in zip(whole, got, G["kinds"], G["names"])]
        G["chips"] = _exchange_start(_chip_copies, 3, sums, [_land_shape(s, k) for s, k in zip(sums, G["kinds"])],
                                     G["kinds"], "grads_to_chips_start_" + group)
        return G["chips"][-1]

    loss_row, grad_x, big, small = _local_step(x[0], loss_target[0], full(wmain_t), wsmall_t, rest_weights, p, on_grads,
                                               advance, rest_started[-1])
    grads = {}
    mine, mine_names = [], []
    for group, G in groups.items():
        sums, got = _exchange_wait(_chip_copies, G["chips"], grad_x, G["kinds"], "grads_to_chips_wait_" + group)
        mine += [_add_chips(s, r, kcvec, k, "add_chips_" + n) for s, r, k, n in zip(sums, got, G["kinds"], G["names"])]
        mine_names += G["names"]
    grads.update(zip(mine_names, _join_sibling_halves(mine)))

    gt = big["wsmall_t"]
    small["w_in_gates"] = jnp.concatenate([gt[0:4], gt[LANES:LANES + 4], gt[2 * LANES:2 * LANES + 8]], axis=0)
    small_names = _SMALL + ["conv_w"]
    packed_names = small_names + ["w_in_gates"]
    pack = _pack_rows([small[n] for n in packed_names] + [loss_row])
    pack = jnp.pad(pack, ((0, (-pack.shape[0]) % 8), (0, 0)))
    full_shapes = [small[n].shape if n in ("conv_w", "w_in_gates") else w[n][0].shape for n in packed_names]
    total = _unpack_rows(_allreduce_small(pack), full_shapes + [loss_row.shape])
    for n, t in zip(packed_names, total):
        grads[n] = t
    loss = total[-1][0, 0]
    grads["conv_w"] = lax.dynamic_slice_in_dim(grads["conv_w"], kme * conv_w.shape[2], conv_w.shape[2], axis=1)
    my_gates = lax.dynamic_slice_in_dim(grads.pop("w_in_gates"), 4 * kme, 4, axis=0)
    g_in = jnp.zeros(w["w_in"].shape[1:], F32)
    g_in = lax.dynamic_update_slice_in_dim(g_in, grads["w_in"], 4 * odd, axis=0)
    grads["w_in"] = lax.dynamic_update_slice_in_dim(g_in, my_gates, 2048 * (1 - odd), axis=0)

    delta, new_m, new_v = {}, {}, {}
    for n in _BIG:
        delta[n], new_m[n], new_v[n] = _adamw(w[n], grads[n], mom[n], var[n], "adamw_" + n)
        grads[n] = grads[n][None]
    for d in (grads, delta, new_m, new_v):
        d["w_in"] = tr3(d["w_in"])
    packs = [_pack_rows([d[n][0] for n in small_names]) for d in (w, mom, var)]
    pad = ((0, (-packs[0].shape[0]) % 8), (0, 0))
    packs = [jnp.pad(t, pad)[None] for t in packs]
    gp = jnp.pad(_pack_rows([grads[n] for n in small_names]), pad)
    shapes = [w[n][0].shape for n in small_names]
    for dst, res in zip((delta, new_m, new_v), _adamw(packs[0], gp, packs[1], packs[2], "adamw_small")):
        for n, t in zip(small_names, _unpack_rows(res[0], shapes)):
            dst[n] = t[None]
    for n in small_names:
        grads[n] = grads[n][None]

    return (loss, grad_x[None], *[grads[n] for n in _WEIGHTS], *[delta[n] for n in _WEIGHTS],
            *[new_m[n] for n in _WEIGHTS], *[new_v[n] for n in _WEIGHTS])
```

```python
import functools
import math

import jax
import jax.numpy as jnp
from jax import lax
from jax.experimental import pallas as pl
from jax.experimental.pallas import tpu as pltpu

F32 = jnp.float32
BF16 = jnp.bfloat16
MESH = pl.DeviceIdType.MESH

D_MODEL = 1024
ML_HEADS = 4
ML_DQK = 128
ML_DV = 256
FOX_HEADS = 8
FOX_DH = 128
D_FF = 2816
GATE_CAP = 15.0
EPS = 1e-6
ADAM_LR, ADAM_B1, ADAM_B2, ADAM_EPS, ADAM_WD, ADAM_STEP = 0.001, 0.9, 0.999, 1e-08, 0.01, 10

LANES = 128
MLC = 128
FOX_TQ = 512
FOX_TQ_FWD = 512
FOX_TK = 512
FOX_TK_FWD = 512
ROW_T = 512
CONV_TC = 1408
VMEM_LIMIT = 56 * 1024 * 1024

C_QM, C_KM, C_VM, C_OM = 0, 512, 1024, 2048
N_ML, N_FOX, N_GATE = 3072, 3072, 2048
N_SMALL = 384


def _cparams(sem=None):
    return pltpu.CompilerParams(dimension_semantics=sem, vmem_limit_bytes=VMEM_LIMIT)


def _tile(n, target):
    if n <= target:
        return n
    best = None
    for t in range(LANES, target + 1, LANES):
        if n % t == 0:
            best = t
    assert best is not None, (n, target)
    return best


def _dot(a, b, dims):
    return lax.dot_general(a, b, (dims, ((), ())), preferred_element_type=F32)


def _dot_nn(a, b):
    return _dot(a, b, ((1,), (0,)))


def _dot_nt(a, b):
    return _dot(a, b, ((1,), (1,)))


def _dot_tn(a, b):
    return _dot(a, b, ((0,), (0,)))


_DOTS = {"nn": _dot_nn, "nt": _dot_nt, "tn": _dot_tn}


def _mm(a, b, mode, out_dtype, name, tm=1024, tn=1408, tk=1408, after=None):
    a_parts = list(a) if isinstance(a, (list, tuple)) else [a]
    b_parts = list(b) if isinstance(b, (list, tuple)) else [b]
    extra = [] if after is None else [after]
    assert len(a_parts) == 1 or len(b_parts) == 1, name
    a_axes = {"nn": "ik", "nt": "ik", "tn": "ki"}[mode]
    b_axes = {"nn": "kj", "nt": "jk", "tn": "kj"}[mode]
    size, target = {}, dict(i=tm, j=tn, k=tk)
    for parts, axes in ((a_parts, a_axes), (b_parts, b_axes)):
        dims = (parts[0].shape[0], parts[0].shape[1] * len(parts))
        for ax, n in zip(axes, dims):
            assert size.setdefault(ax, n) == n, (name, ax, n, size)
    tile = {}
    for parts, axes in ((a_parts, a_axes), (b_parts, b_axes)):
        if len(parts) > 1:
            tile[axes[1]] = _tile(parts[0].shape[1], target[axes[1]])
    for ax in "ijk":
        tile.setdefault(ax, _tile(size[ax], target[ax]))
    M, N, nk = size["i"], size["j"], size["k"] // tile["k"]
    grid_pos = dict(i=0, j=1, k=2)
    dot = _DOTS[mode]

    def specs(parts, axes):
        blk = (tile[axes[0]], tile[axes[1]])
        if len(parts) == 1:
            return [pl.BlockSpec(blk, lambda *g: (g[grid_pos[axes[0]]], g[grid_pos[axes[1]]]))], None
        bpp = parts[0].shape[1] // blk[1]

        def index(p):
            def f(*g):
                g0, g1 = g[grid_pos[axes[0]]], g[grid_pos[axes[1]]]
                on = g1 // bpp == p
                return jnp.where(on, g0, 0), jnp.where(on, g1 % bpp, 0)
            return f

        return [pl.BlockSpec(blk, index(p)) for p in range(len(parts))], (axes[1], bpp)

    a_specs, a_sel = specs(a_parts, a_axes)
    b_specs, b_sel = specs(b_parts, b_axes)
    na, nb = len(a_parts), len(b_parts)

    def body(*refs):
        a_refs, b_refs = refs[:na], refs[na:na + nb]
        o_ref, acc = refs[na + nb + len(extra)], refs[na + nb + len(extra) + 1:]

        def accumulate(part):
            if nk == 1:
                o_ref[...] = part.astype(o_ref.dtype)
                return
            acc_ref, = acc
            k = pl.program_id(2)

            @pl.when(k == 0)
            def _():
                acc_ref[...] = part

            @pl.when(k > 0)
            def _():
                acc_ref[...] += part

            @pl.when(k == nk - 1)
            def _():
                o_ref[...] = acc_ref[...].astype(o_ref.dtype)

        sel = a_sel or b_sel
        if sel is None:
            accumulate(dot(a_refs[0][...], b_refs[0][...]))
        else:
            which = pl.program_id(grid_pos[sel[0]]) // sel[1]
            for p in range(max(na, nb)):
                @pl.when(which == p)
                def _(p=p):
                    accumulate(dot(a_refs[p if a_sel else 0][...], b_refs[p if b_sel else 0][...]))

    return pl.pallas_call(
        body, name=name,
        out_shape=jax.ShapeDtypeStruct((M, N), out_dtype),
        grid=(M // tile["i"], N // tile["j"], nk),
        in_specs=a_specs + b_specs + [pl.BlockSpec(memory_space=pl.ANY)] * len(extra),
        out_specs=pl.BlockSpec((tile["i"], tile["j"]), lambda i, j, k: (i, j)),
        scratch_shapes=[pltpu.VMEM((tile["i"], tile["j"]), F32)] if nk > 1 else [],
        compiler_params=_cparams(("parallel", "parallel", "arbitrary")),
    )(*a_parts, *b_parts, *extra)


def _rstd(x):
    return lax.rsqrt(jnp.mean(x * x, axis=-1, keepdims=True) + EPS)


def _rmsnorm_fwd(x, g, name):
    S, D = x.shape
    T = _tile(S, ROW_T)

    def body(x_ref, g_ref, o_ref):
        xv = x_ref[...]
        o_ref[...] = (xv * _rstd(xv) * g_ref[...]).astype(o_ref.dtype)

    return pl.pallas_call(
        body, name=name, out_shape=jax.ShapeDtypeStruct((S, D), BF16), grid=(S // T,),
        in_specs=[pl.BlockSpec((T, D), lambda i: (i, 0)), pl.BlockSpec((1, D), lambda i: (0, 0))],
        out_specs=pl.BlockSpec((T, D), lambda i: (i, 0)),
        compiler_params=_cparams(("parallel",)),
    )(x, g)


def _resid_norm_fwd(x, z, g, name):
    S, D = x.shape
    T = _tile(S, ROW_T)

    def body(x_ref, z_ref, g_ref, o_ref):
        zv = z_ref[...]
        o_ref[...] = x_ref[...] + zv * _rstd(zv) * g_ref[...]

    row = pl.BlockSpec((T, D), lambda i: (i, 0))
    return pl.pallas_call(
        body, name=name, out_shape=jax.ShapeDtypeStruct((S, D), F32), grid=(S // T,),
        in_specs=[row, row, pl.BlockSpec((1, D), lambda i: (0, 0))],
        out_specs=row, compiler_params=_cparams(("parallel",)),
    )(x, z, g)


def _rmsnorm_bwd_math(dy, xv, g):
    r = _rstd(xv)
    u = dy * g
    dx = r * u - xv * (r * r * r) * jnp.mean(u * xv, axis=-1, keepdims=True)
    return dx, dy * xv * r


def _rmsnorm_bwd(dys, xin, g, resid, out_dtype, name):
    S, D = xin.shape
    T = _tile(S, ROW_T)
    has_resid = resid is not None
    ndy = len(dys)

    def body(*refs):
        dy_refs, (x_ref, g_ref) = refs[:ndy], refs[ndy:ndy + 2]
        dx_ref, dg_ref = refs[-2:]
        dy = dy_refs[0][...]
        for r in dy_refs[1:]:
            dy = dy + r[...]
        dx, dgt = _rmsnorm_bwd_math(dy, x_ref[...], g_ref[...])
        if has_resid:
            dx = dx + refs[ndy + 2][...]
        dx_ref[...] = dx.astype(dx_ref.dtype)

        @pl.when(pl.program_id(0) == 0)
        def _():
            dg_ref[...] = jnp.zeros_like(dg_ref)

        dg_ref[...] += jnp.sum(dgt, axis=0, keepdims=True)

    row = pl.BlockSpec((T, D), lambda i: (i, 0))
    vec = pl.BlockSpec((1, D), lambda i: (0, 0))
    ins = list(dys) + [xin, g] + ([resid] if has_resid else [])
    return pl.pallas_call(
        body, name=name,
        out_shape=(jax.ShapeDtypeStruct((S, D), out_dtype), jax.ShapeDtypeStruct((1, D), F32)),
        grid=(S // T,), in_specs=[row] * ndy + [row, vec] + ([row] if has_resid else []),
        out_specs=(row, vec), compiler_params=_cparams(("arbitrary",)),
    )(*ins)


def _loss_head(x1, d, g, target, name):
    S, D = x1.shape
    T = _tile(S, ROW_T)

    def body(x_ref, d_ref, g_ref, t_ref, loss_ref, dy_ref, dd_ref, dg_ref):
        dv, gv = d_ref[...], g_ref[...]
        y = x_ref[...] + dv * _rstd(dv) * gv
        diff = y - t_ref[...]
        dy = diff * (1.0 / D)
        dy_ref[...] = dy
        dd, dgt = _rmsnorm_bwd_math(dy, dv, gv)
        dd_ref[...] = dd.astype(dd_ref.dtype)

        @pl.when(pl.program_id(0) == 0)
        def _():
            dg_ref[...] = jnp.zeros_like(dg_ref)
            loss_ref[...] = jnp.zeros_like(loss_ref)

        dg_ref[...] += jnp.sum(dgt, axis=0, keepdims=True)
        part = jnp.sum(jnp.sum(diff * diff, axis=1, keepdims=True), axis=0, keepdims=True)
        loss_ref[...] += (0.5 / D) * part

    row = pl.BlockSpec((T, D), lambda i: (i, 0))
    vec = pl.BlockSpec((1, D), lambda i: (0, 0))
    return pl.pallas_call(
        body, name=name,
        out_shape=(jax.ShapeDtypeStruct((1, LANES), F32), jax.ShapeDtypeStruct((S, D), F32),
                   jax.ShapeDtypeStruct((S, D), BF16), jax.ShapeDtypeStruct((1, D), F32)),
        grid=(S // T,), in_specs=[row, row, vec, row],
        out_specs=(pl.BlockSpec((1, LANES), lambda i: (0, 0)), row, row, vec),
        compiler_params=_cparams(("arbitrary",)),
    )(x1, d, g, target)


def _merge_fwd(ya, yb, pm, ba, bb, name):
    S, D = ya.shape
    T = _tile(S, ROW_T)

    def body(ya_ref, yb_ref, ga_ref, gb_ref, ba_ref, bb_ref, o_ref):
        sa = jax.nn.sigmoid(ga_ref[...] + ba_ref[...])
        sb = jax.nn.sigmoid(gb_ref[...] + bb_ref[...])
        o_ref[...] = (sa * ya_ref[...] + sb * yb_ref[...]).astype(o_ref.dtype)

    row = pl.BlockSpec((T, D), lambda i: (i, 0))
    vec = pl.BlockSpec((1, D), lambda i: (0, 0))
    return pl.pallas_call(
        body, name=name, out_shape=jax.ShapeDtypeStruct((S, D), BF16), grid=(S // T,),
        in_specs=[row, row, pl.BlockSpec((T, D), lambda i: (i, 0)),
                  pl.BlockSpec((T, D), lambda i: (i, 1)), vec, vec],
        out_specs=row, compiler_params=_cparams(("parallel",)),
    )(ya, yb, pm, pm, ba, bb)


def _merge_bwd(dmerged, ya, yb, pm, ba, bb, name):
    S, D = ya.shape
    T = _tile(S, ROW_T)

    def body(dm_ref, ya_ref, yb_ref, ga_ref, gb_ref, ba_ref, bb_ref,
             dya_ref, dyb_ref, dga_ref, dgb_ref, dba_ref, dbb_ref):
        dm = dm_ref[...]
        sa = jax.nn.sigmoid(ga_ref[...] + ba_ref[...])
        sb = jax.nn.sigmoid(gb_ref[...] + bb_ref[...])
        dya_ref[...] = (dm * sa).astype(dya_ref.dtype)
        dyb_ref[...] = (dm * sb).astype(dyb_ref.dtype)
        dga = dm * ya_ref[...] * sa * (1.0 - sa)
        dgb = dm * yb_ref[...] * sb * (1.0 - sb)
        dga_ref[...] = dga.astype(dga_ref.dtype)
        dgb_ref[...] = dgb.astype(dgb_ref.dtype)

        @pl.when(pl.program_id(0) == 0)
        def _():
            dba_ref[...] = jnp.zeros_like(dba_ref)
            dbb_ref[...] = jnp.zeros_like(dbb_ref)

        dba_ref[...] += jnp.sum(dga, axis=0, keepdims=True)
        dbb_ref[...] += jnp.sum(dgb, axis=0, keepdims=True)

    row = pl.BlockSpec((T, D), lambda i: (i, 0))
    vec = pl.BlockSpec((1, D), lambda i: (0, 0))
    act = jax.ShapeDtypeStruct((S, D), BF16)
    v1 = jax.ShapeDtypeStruct((1, D), F32)
    return pl.pallas_call(
        body, name=name, out_shape=(act, act, act, act, v1, v1), grid=(S // T,),
        in_specs=[row, row, row, pl.BlockSpec((T, D), lambda i: (i, 0)),
                  pl.BlockSpec((T, D), lambda i: (i, 1)), vec, vec],
        out_specs=(row, row, row, row, vec, vec), compiler_params=_cparams(("arbitrary",)),
    )(dmerged, ya, yb, pm, pm, ba, bb)


_GELU_C = math.sqrt(2.0 / math.pi)


def _gelu(g):
    t = jnp.tanh(_GELU_C * (g + 0.044715 * g * g * g))
    return 0.5 * g * (1.0 + t), t


def _gelu_grad(g, t):
    return 0.5 * (1.0 + t) + 0.5 * g * (1.0 - t * t) * _GELU_C * (1.0 + 3 * 0.044715 * g * g)


def _shift_down(v, halo_ref, first, rows):
    T = v.shape[0]
    keep = jnp.where(first, 0.0, 1.0)
    h7 = halo_ref[7:8, :] * keep
    h6 = halo_ref[6:7, :] * keep
    m1 = jnp.where(rows == 0, h7, pltpu.roll(v, 1, 0))
    m2 = jnp.where(rows == 0, h6, jnp.where(rows == 1, h7, pltpu.roll(v, 2, 0)))
    return m1, m2


def _conv_act_fwd(up, cw, cb, name):
    S, F2 = up.shape
    Fh = F2 // 2
    T = _tile(S, ROW_T)
    tc = _tile(Fh, CONV_TC)
    ncol = Fh // tc
    hb = T // 8

    def body(ua_ref, ug_ref, ha_ref, hg_ref, wa_ref, wg_ref, ba_ref, bg_ref, o_ref, a_ref, g_ref):
        first = pl.program_id(0) == 0
        rows = lax.broadcasted_iota(jnp.int32, (T, tc), 0)

        def conv(u_ref, h_ref, w_ref, b_ref):
            v = u_ref[...]
            m1, m2 = _shift_down(v, h_ref, first, rows)
            return b_ref[...] + w_ref[0:1, :] * m2 + w_ref[1:2, :] * m1 + w_ref[2:3, :] * v

        a = conv(ua_ref, ha_ref, wa_ref, ba_ref)
        g = conv(ug_ref, hg_ref, wg_ref, bg_ref)
        a_ref[...] = a
        g_ref[...] = g
        o_ref[...] = (_gelu(g)[0] * a).astype(o_ref.dtype)

    halo = lambda off: pl.BlockSpec((8, tc), lambda i, j: (jnp.maximum(i * hb - 1, 0), j + off))
    blk = pl.BlockSpec((T, tc), lambda i, j: (i, j))
    f32 = jax.ShapeDtypeStruct((S, Fh), F32)
    return pl.pallas_call(
        body, name=name, out_shape=(jax.ShapeDtypeStruct((S, Fh), BF16), f32, f32), grid=(S // T, ncol),
        in_specs=[blk, pl.BlockSpec((T, tc), lambda i, j: (i, j + ncol)),
                  halo(0), halo(ncol),
                  pl.BlockSpec((3, tc), lambda i, j: (0, j)), pl.BlockSpec((3, tc), lambda i, j: (0, j + ncol)),
                  pl.BlockSpec((1, tc), lambda i, j: (0, j)), pl.BlockSpec((1, tc), lambda i, j: (0, j + ncol))],
        out_specs=(blk, blk, blk),
        compiler_params=_cparams(("parallel", "parallel")),
    )(up, up, up, up, cw, cw, cb, cb)


def _conv_act_bwd(up, a, g, dact, cw, name):
    S, F2 = up.shape
    Fh = F2 // 2
    T = _tile(S, ROW_T)
    tc = _tile(Fh, CONV_TC)
    ncol, nrow, hb, nhb = Fh // tc, S // T, T // 8, S // 8

    def body(ua_ref, ug_ref, a_ref, g_ref, an_ref, gn_ref, wa_ref, wg_ref, da_ref, dn_ref,
             dpa_ref, dpg_ref, dwa_ref, dwg_ref, dba_ref, dbg_ref, dua_n, dug_n):
        i = pl.program_id(1)
        rows = lax.broadcasted_iota(jnp.int32, (T, tc), 0)

        def du_of(a, g, dact_v):
            gel, t = _gelu(g)
            return dact_v * gel, dact_v * a * _gelu_grad(g, t)

        dua, dug = du_of(a_ref[...], g_ref[...], da_ref[...])
        keep = jnp.where(i == nrow - 1, 0.0, 1.0)
        dua_n[...], dug_n[...] = du_of(an_ref[...], gn_ref[...], dn_ref[...] * keep)

        @pl.when(i == 0)
        def _():
            for r in (dwa_ref, dwg_ref, dba_ref, dbg_ref):
                r[...] = jnp.zeros_like(r)

        for du, n_ref, u_ref, w_ref, o_ref, dw_ref, db_ref in ((dua, dua_n, ua_ref, wa_ref, dpa_ref, dwa_ref, dba_ref),
                                                               (dug, dug_n, ug_ref, wg_ref, dpg_ref, dwg_ref, dbg_ref)):
            n0, n1 = n_ref[0:1, :], n_ref[1:2, :]
            du1 = jnp.where(rows == T - 1, n0, pltpu.roll(du, T - 1, 0))
            du2 = jnp.where(rows == T - 2, n0, jnp.where(rows == T - 1, n1, pltpu.roll(du, T - 2, 0)))
            o_ref[...] = (w_ref[2:3, :] * du + w_ref[1:2, :] * du1 + w_ref[0:1, :] * du2).astype(o_ref.dtype)
            u = u_ref[...]
            db_ref[...] += jnp.sum(du, axis=0, keepdims=True)
            for j, d in enumerate((du2, du1, du)):
                dw_ref[j:j + 1, :] += jnp.sum(d * u, axis=0, keepdims=True)

    tile = lambda off: pl.BlockSpec((T, tc), lambda j, i: (i, j + off))
    under = pl.BlockSpec((8, tc), lambda j, i: (jnp.minimum((i + 1) * hb, nhb - 1), j))
    vec = lambda n, off: pl.BlockSpec((n, tc), lambda j, i: (0, j + off))
    act = jax.ShapeDtypeStruct((S, Fh), BF16)
    return pl.pallas_call(
        body, name=name,
        out_shape=(act, act, jax.ShapeDtypeStruct((3, Fh), F32), jax.ShapeDtypeStruct((3, Fh), F32),
                   jax.ShapeDtypeStruct((1, Fh), F32), jax.ShapeDtypeStruct((1, Fh), F32)),
        grid=(ncol, nrow),
        in_specs=[tile(0), tile(ncol), tile(0), tile(0), under, under, vec(3, 0), vec(3, ncol), tile(0), under],
        out_specs=(tile(0), tile(0), vec(3, 0), vec(3, 0), vec(1, 0), vec(1, 0)),
        scratch_shapes=[pltpu.VMEM((8, tc), F32), pltpu.VMEM((8, tc), F32)],
        compiler_params=_cparams(("parallel", "arbitrary")),
    )(up, up, a, g, a, g, cw, cw, dact, dact)


def _split3(x):
    hi = x.astype(BF16)
    r1 = x - hi.astype(F32)
    mid = r1.astype(BF16)
    lo = (r1 - mid.astype(F32)).astype(BF16)
    return hi, mid, lo


def _tri_dot(tri, x):
    hi, mid, lo = _split3(x)
    return _dot_nn(tri, hi) + _dot_nn(tri, mid) + _dot_nn(tri, lo)


def _log_sigmoid(x):
    return jnp.minimum(x, 0.0) - jnp.log(1.0 + jnp.exp(-jnp.abs(x)))


def _tri_mask(n, lower):
    r = lax.broadcasted_iota(jnp.int32, (n, n), 0)
    c = lax.broadcasted_iota(jnp.int32, (n, n), 1)
    return (r >= c) if lower else (r <= c)


def _gates_fwd(ps, bi, bf, bff, name):
    S = ps.shape[0]
    NC = S // MLC

    def body(ps_ref, bi_ref, bf_ref, bff_ref, a_ref, A_ref, wi_ref, em_ref, wk_ref, dec_ref, F_ref, m_scr, f_scr):
        @pl.when(pl.program_id(0) == 0)
        def _():
            m_scr[...] = jnp.zeros_like(m_scr)
            f_scr[...] = jnp.zeros_like(f_scr)

        rows = lax.broadcasted_iota(jnp.int32, (MLC, LANES), 0)
        ltri = _tri_mask(MLC, True).astype(BF16)
        li = GATE_CAP * jnp.tanh((ps_ref[:, 0:LANES] + bi_ref[...]) / GATE_CAP)
        lf = _log_sigmoid(GATE_CAP * jnp.tanh((ps_ref[:, LANES:2 * LANES] + bf_ref[...]) / GATE_CAP))
        b = _tri_dot(ltri, lf)
        a = li - b
        cm = a
        sh = 1
        while sh < MLC:
            cm = jnp.where(rows >= sh, jnp.maximum(cm, pltpu.roll(cm, sh, 0)), cm)
            sh *= 2
        m0 = m_scr[...]
        A = jnp.maximum(cm, m0)
        a_ref[...] = a
        A_ref[...] = A
        A_last = A_ref[MLC - 1:MLC, :]
        wi_ref[...] = jnp.exp(m0 - A)
        em_ref[...] = jnp.exp(-(b + A))
        wk_ref[...] = jnp.exp(a - A_last)
        dec_ref[0] = jnp.exp(m0 - A_last)
        F_ref[...] = b
        m_scr[...] = F_ref[MLC - 1:MLC, :] + A_last
        lfg = _log_sigmoid(ps_ref[:, 2 * LANES:3 * LANES] + bff_ref[...])
        F_ref[...] = _tri_dot(ltri, lfg) + f_scr[...]
        f_scr[...] = F_ref[MLC - 1:MLC, :]

    col = pl.BlockSpec((MLC, LANES), lambda c: (c, 0))
    vec = pl.BlockSpec((1, LANES), lambda c: (0, 0))
    cs = jax.ShapeDtypeStruct((S, LANES), F32)
    return pl.pallas_call(
        body, name=name,
        out_shape=(cs, cs, cs, cs, cs, jax.ShapeDtypeStruct((NC, 1, LANES), F32), cs),
        grid=(NC,), in_specs=[pl.BlockSpec((MLC, N_SMALL), lambda c: (c, 0)), vec, vec, vec],
        out_specs=(col, col, col, col, col, pl.BlockSpec((1, 1, LANES), lambda c: (c, 0, 0)), col),
        scratch_shapes=[pltpu.VMEM((1, LANES), F32), pltpu.VMEM((1, LANES), F32)],
        compiler_params=_cparams(("arbitrary",)),
    )(ps, bi, bf, bff)


def _gates_bwd(ps, bi, bf, bff, rk, kc, tch, dF, name):
    S = ps.shape[0]
    NC = S // MLC

    def body(ps_ref, bi_ref, bf_ref, bff_ref, rk_ref, kc_ref, t_ref, dF_ref, dps_ref, db_ref, carry):
        @pl.when(pl.program_id(0) == 0)
        def _():
            carry[...] = jnp.zeros_like(carry)
            db_ref[...] = jnp.zeros_like(db_ref)

        lanes = lax.broadcasted_iota(jnp.int32, (MLC, LANES), 1)
        utri = _tri_mask(MLC, False).astype(BF16)
        ti = jnp.tanh((ps_ref[:, 0:LANES] + bi_ref[...]) / GATE_CAP)
        t_end, t_start = t_ref[0, 0:1, :], t_ref[0, 1:2, :]
        rk = rk_ref[...]
        rk = rk - (jnp.sum(rk, axis=0, keepdims=True) - (t_start - t_end)) * (1.0 / MLC)
        dpi = jnp.where(lanes < ML_HEADS, (kc_ref[...] - rk) * (1.0 - ti * ti), 0.0)
        tf = jnp.tanh((ps_ref[:, LANES:2 * LANES] + bf_ref[...]) / GATE_CAP)
        dlf = _tri_dot(utri, rk) + t_end
        dpf = jnp.where(lanes < ML_HEADS, dlf * jax.nn.sigmoid(-GATE_CAP * tf) * (1.0 - tf * tf), 0.0)
        dFv = dF_ref[...]
        dlfg = _tri_dot(utri, dFv) + carry[...]
        carry[...] += jnp.sum(dFv, axis=0, keepdims=True)
        dpff = jnp.where(lanes < FOX_HEADS, dlfg * jax.nn.sigmoid(-(ps_ref[:, 2 * LANES:3 * LANES] + bff_ref[...])), 0.0)
        for n, dp in enumerate((dpi, dpf, dpff)):
            dps_ref[:, n * LANES:(n + 1) * LANES] = dp.astype(dps_ref.dtype)
            db_ref[:, n * LANES:(n + 1) * LANES] += jnp.sum(dp, axis=0, keepdims=True)

    rev = lambda c: (NC - 1 - c, 0)
    col = pl.BlockSpec((MLC, LANES), rev)
    vec = pl.BlockSpec((1, LANES), lambda c: (0, 0))
    wide = pl.BlockSpec((MLC, N_SMALL), rev)
    return pl.pallas_call(
        body, name=name,
        out_shape=(jax.ShapeDtypeStruct((S, N_SMALL), BF16), jax.ShapeDtypeStruct((1, N_SMALL), F32)),
        grid=(NC,),
        in_specs=[wide, vec, vec, vec, col, col, pl.BlockSpec((1, 2, LANES), lambda c: (NC - 1 - c, 0, 0)), col],
        out_specs=(wide, pl.BlockSpec((1, N_SMALL), lambda c: (0, 0))),
        scratch_shapes=[pltpu.VMEM((1, LANES), F32)],
        compiler_params=_cparams(("arbitrary",)),
    )(ps, bi, bf, bff, rk, kc, tch, dF)


_ML_SCALE = ML_DQK ** -0.5


def _ml_specs(rev, NC):
    idx = (lambda c: NC - 1 - c) if rev else (lambda c: c)
    qk = lambda blk: pl.BlockSpec((MLC, ML_HEADS * ML_DQK), lambda c: (idx(c), blk))
    wide = lambda blk: pl.BlockSpec((MLC, D_MODEL), lambda c: (idx(c), blk))
    col = pl.BlockSpec((MLC, LANES), lambda c: (idx(c), 0))
    return idx, qk, wide, col


def _ml_intra(q_ref, k_ref, arow_ref, A_ref, h):
    hs = slice(h * ML_DQK, (h + 1) * ML_DQK)
    qf = q_ref[:, hs] * _ML_SCALE
    kf = k_ref[:, hs]
    qb, kb = qf.astype(BF16), kf.astype(BF16)
    qk = _dot_nt(qb, kb)
    logw = arow_ref[h:h + 1, :] - A_ref[:, h:h + 1]
    W = jnp.exp(jnp.where(_tri_mask(MLC, True), logw, -1e30))
    return qb, kb, qf, kf, qk, W


def _mlstm_fwd(pm, a_row, A, wi, em, wk, dec, w_hn, name):
    S = pm.shape[0]
    NC = S // MLC
    _, qk, wide, col = _ml_specs(False, NC)

    def body(q_ref, k_ref, v_ref, o_ref, arow_ref, A_ref, wi_ref, em_ref, wk_ref, dec_ref, whn_ref,
             ha_ref, hp_ref, den_ref, cst_ref, nst_ref, C_scr, n_scr):
        @pl.when(pl.program_id(0) == 0)
        def _():
            C_scr[...] = jnp.zeros_like(C_scr)
            n_scr[...] = jnp.zeros_like(n_scr)

        lanes = lax.broadcasted_iota(jnp.int32, (MLC, LANES), 1)
        den_tile = jnp.zeros((MLC, LANES), F32)
        for h in range(ML_HEADS):
            vs = slice(h * ML_DV, (h + 1) * ML_DV)
            qb, kb, qf, kf, qk_, W = _ml_intra(q_ref, k_ref, arow_ref, A_ref, h)
            vb = v_ref[:, vs].astype(BF16)
            Cf = C_scr[h]
            Cb = Cf.astype(BF16)
            nrow = n_scr[h]
            cst_ref[0, h] = Cb
            nst_ref[0, h] = nrow
            s = qk_ * W
            wic = wi_ref[:, h:h + 1]
            num = _dot_nn(s.astype(BF16), vb) + wic * _dot_nt(qb, Cb)
            den = jnp.sum(s, axis=1, keepdims=True) + wic * jnp.sum(qf * nrow, axis=1, keepdims=True)
            hp = num / jnp.maximum(jnp.abs(den), em_ref[:, h:h + 1])
            hp_ref[:, vs] = hp
            den_tile = jnp.where(lanes == h, den, den_tile)
            hn = hp * _rstd(hp) * whn_ref[:, vs]
            ha_ref[:, vs] = (hn * jax.nn.sigmoid(o_ref[:, vs])).astype(ha_ref.dtype)
            wkc = wk_ref[:, h:h + 1]
            kw = kf * wkc
            d = dec_ref[0, :, h:h + 1]
            C_scr[h] = d * Cf + _dot_tn(vb, kw.astype(BF16))
            n_scr[h] = d * nrow + jnp.sum(kw, axis=0, keepdims=True)
        den_ref[...] = den_tile

    return pl.pallas_call(
        body, name=name,
        out_shape=(jax.ShapeDtypeStruct((S, D_MODEL), BF16), jax.ShapeDtypeStruct((S, D_MODEL), F32),
                   jax.ShapeDtypeStruct((S, LANES), F32),
                   jax.ShapeDtypeStruct((NC, ML_HEADS, ML_DV, ML_DQK), BF16),
                   jax.ShapeDtypeStruct((NC, ML_HEADS, 1, ML_DQK), F32)),
        grid=(NC,),
        in_specs=[qk(C_QM // 512), qk(C_KM // 512), wide(C_VM // D_MODEL), wide(C_OM // D_MODEL),
                  pl.BlockSpec((8, MLC), lambda c: (0, c)), col, col, col, col,
                  pl.BlockSpec((1, 1, LANES), lambda c: (c, 0, 0)), pl.BlockSpec((1, D_MODEL), lambda c: (0, 0))],
        out_specs=(pl.BlockSpec((MLC, D_MODEL), lambda c: (c, 0)), pl.BlockSpec((MLC, D_MODEL), lambda c: (c, 0)),
                   col, pl.BlockSpec((1, ML_HEADS, ML_DV, ML_DQK), lambda c: (c, 0, 0, 0)),
                   pl.BlockSpec((1, ML_HEADS, 1, ML_DQK), lambda c: (c, 0, 0, 0))),
        scratch_shapes=[pltpu.VMEM((ML_HEADS, ML_DV, ML_DQK), F32), pltpu.VMEM((ML_HEADS, 1, ML_DQK), F32)],
        compiler_params=_cparams(("arbitrary",)),
    )(pm, pm, pm, pm, a_row, A, wi, em, wk, dec, w_hn)


def _mlstm_bwd(dha, pm, hp_all, den_all, a_row, A, wi, em, wk, dec, cst, nst, w_hn, name):
    S = pm.shape[0]
    NC = S // MLC
    idx, qk, wide, col = _ml_specs(True, NC)

    def body(dha_ref, q_ref, k_ref, v_ref, o_ref, hp_ref, den_ref, arow_ref, A_ref, wi_ref, em_ref, wk_ref,
             dec_ref, cst_ref, nst_ref, whn_ref,
             dqk_ref, dv_ref, do_ref, rk_ref, kc_ref, t_ref, dwhn_ref, dC_scr, dn_scr, t_scr):
        @pl.when(pl.program_id(0) == 0)
        def _():
            dC_scr[...] = jnp.zeros_like(dC_scr)
            dn_scr[...] = jnp.zeros_like(dn_scr)
            t_scr[...] = jnp.zeros_like(t_scr)
            dwhn_ref[...] = jnp.zeros_like(dwhn_ref)

        lanes = lax.broadcasted_iota(jnp.int32, (MLC, LANES), 1)
        lane1 = lax.broadcasted_iota(jnp.int32, (1, LANES), 1)
        t_ref[0, 0:1, :] = t_scr[...]
        rk_tile = jnp.zeros((MLC, LANES), F32)
        kc_tile = jnp.zeros((MLC, LANES), F32)
        t_new = jnp.zeros((1, LANES), F32)
        for h in range(ML_HEADS):
            hs = slice(h * ML_DQK, (h + 1) * ML_DQK)
            vs = slice(h * ML_DV, (h + 1) * ML_DV)
            hp = hp_ref[:, vs]
            sig = jax.nn.sigmoid(o_ref[:, vs])
            whn = whn_ref[:, vs]
            r = _rstd(hp)
            dga = dha_ref[:, vs]
            do_ref[:, vs] = (dga * (hp * r * whn) * sig * (1.0 - sig)).astype(do_ref.dtype)
            dhn = dga * sig
            dhp, dwt = _rmsnorm_bwd_math(dhn, hp, whn)
            dwhn_ref[:, vs] += jnp.sum(dwt, axis=0, keepdims=True)
            den = den_ref[:, h:h + 1]
            floor = em_ref[:, h:h + 1]
            D = jnp.maximum(jnp.abs(den), floor)
            dnum = dhp / D
            dh_h = jnp.sum(dhp * hp, axis=1, keepdims=True)
            active = jnp.abs(den) >= floor
            dden = -dh_h / D * jnp.where(active, jnp.sign(den), 0.0)
            phi = jnp.where(active, 0.0, dh_h)
            qb, kb, qf, kf, qk_, W = _ml_intra(q_ref, k_ref, arow_ref, A_ref, h)
            vf = v_ref[:, vs]
            vb = vf.astype(BF16)
            Cb = cst_ref[0, h]
            nrow = nst_ref[0, h]
            wic = wi_ref[:, h:h + 1]
            wkc = wk_ref[:, h:h + 1]
            d = dec_ref[0, :, h:h + 1]
            dCn = dC_scr[h]
            dCb = dCn.astype(BF16)
            dnn = dn_scr[h]
            dnumb = dnum.astype(BF16)
            s = qk_ * W
            ds = (_dot_nt(dnumb, vb) + dden) * W
            dsb = ds.astype(BF16)
            dnw = (wic * dnum).astype(BF16)
            wd = wic * dden
            kw = kf * wkc
            dv_state = _dot_nt(kw.astype(BF16), dCb)
            dq = _dot_nn(dsb, kb) + _dot_nn(dnw, Cb) + wd * nrow
            dk_state = wkc * (_dot_nn(vb, dCb) + dnn)
            dk = _dot_tn(dsb, qb) + dk_state
            dv = _dot_tn(s.astype(BF16), dnumb) + dv_state
            dC = d * dCn + _dot_tn(dnw, qb)
            dn = d * dnn + jnp.sum(wd * qf, axis=0, keepdims=True)
            dC_scr[h] = dC
            dn_scr[h] = dn
            dqk_ref[:, hs] = (dq * _ML_SCALE).astype(dqk_ref.dtype)
            dqk_ref[:, C_KM + h * ML_DQK:C_KM + (h + 1) * ML_DQK] = dk.astype(dqk_ref.dtype)
            dv_ref[:, vs] = dv.astype(dv_ref.dtype)
            G = ds * qk_
            inter = _dot_nt(qb, Cb)
            qn = jnp.sum(qf * nrow, axis=1, keepdims=True)
            R = (jnp.sum(G, axis=1, keepdims=True)
                 + wic * (jnp.sum(dnum * inter, axis=1, keepdims=True) + dden * qn))
            K = jnp.sum(G.T, axis=1, keepdims=True) + jnp.sum(kf * dk_state, axis=1, keepdims=True)
            rk_tile = jnp.where(lanes == h, R - K, rk_tile)
            kc_tile = jnp.where(lanes == h, phi, kc_tile)
            tt = (jnp.sum(jnp.sum(dC * Cb.astype(F32), axis=1, keepdims=True), axis=0, keepdims=True)
                  + jnp.sum(dn * nrow, axis=1, keepdims=True))
            t_new = jnp.where(lane1 == h, tt, t_new)
        rk_ref[...] = rk_tile
        kc_ref[...] = kc_tile
        t_ref[0, 1:2, :] = t_new
        t_scr[...] = t_new

    act = lambda n: jax.ShapeDtypeStruct((S, n), BF16)
    cs = jax.ShapeDtypeStruct((S, LANES), F32)
    rowblk = lambda n: pl.BlockSpec((MLC, n), lambda c: (idx(c), 0))
    return pl.pallas_call(
        body, name=name,
        out_shape=(act(D_MODEL), act(D_MODEL), act(D_MODEL), cs, cs,
                   jax.ShapeDtypeStruct((NC, 2, LANES), F32), jax.ShapeDtypeStruct((1, D_MODEL), F32)),
        grid=(NC,),
        in_specs=[rowblk(D_MODEL), qk(C_QM // 512), qk(C_KM // 512), wide(C_VM // D_MODEL), wide(C_OM // D_MODEL),
                  rowblk(D_MODEL), col, pl.BlockSpec((8, MLC), lambda c: (0, idx(c))), col, col, col, col,
                  pl.BlockSpec((1, 1, LANES), lambda c: (idx(c), 0, 0)),
                  pl.BlockSpec((1, ML_HEADS, ML_DV, ML_DQK), lambda c: (idx(c), 0, 0, 0)),
                  pl.BlockSpec((1, ML_HEADS, 1, ML_DQK), lambda c: (idx(c), 0, 0, 0)),
                  pl.BlockSpec((1, D_MODEL), lambda c: (0, 0))],
        out_specs=(rowblk(D_MODEL), rowblk(D_MODEL), rowblk(D_MODEL), col, col,
                   pl.BlockSpec((1, 2, LANES), lambda c: (idx(c), 0, 0)), pl.BlockSpec((1, D_MODEL), lambda c: (0, 0))),
        scratch_shapes=[pltpu.VMEM((ML_HEADS, ML_DV, ML_DQK), F32), pltpu.VMEM((ML_HEADS, 1, ML_DQK), F32),
                        pltpu.VMEM((1, LANES), F32)],
        compiler_params=_cparams(("arbitrary",)),
    )(dha, pm, pm, pm, pm, hp_all, den_all, a_row, A, wi, em, wk, dec, cst, nst, w_hn)


_FOX_SCALE = FOX_DH ** -0.5
_NEG = -1e30
_LOG2E = 1.4426950408889634
_LN2 = 0.6931471805599453
_QF_BLK, _KF_BLK, _VF_BLK = 0, FOX_HEADS, 2 * FOX_HEADS


def _lane_pick(tile, lane):
    lanes = lax.broadcasted_iota(jnp.int32, tile.shape, 1)
    return jnp.sum(jnp.where(lanes == lane, tile, 0.0), axis=1, keepdims=True)


def _col_to_row(col):
    return jnp.max(jnp.broadcast_to(col, (col.shape[0], LANES)).T, axis=0, keepdims=True)


def _causal(q0, k0, shape, q_axis):
    qpos = q0 + lax.broadcasted_iota(jnp.int32, shape, q_axis)
    kpos = k0 + lax.broadcasted_iota(jnp.int32, shape, 1 - q_axis)
    return kpos <= qpos


def _fox_fwd(pf, fc, fk_row, name):
    S = pf.shape[0]
    TQ, TK = FOX_TQ_FWD, FOX_TK_FWD
    nq, nk = S // TQ, S // TK
    c1 = _FOX_SCALE * _LOG2E

    def body(q_ref, k_ref, v_ref, fc_ref, fr_ref, o_ref, lse_ref):
        h, i = pl.program_id(0), pl.program_id(1)
        qb = q_ref[...]
        fq2 = _lane_pick(fc_ref[...], h) * _LOG2E

        def step(j, carry, masked):
            m, l, acc = carry
            off = pl.multiple_of(j * TK, TK)
            t = _dot_nt(qb, k_ref[pl.ds(off, TK), :]) * c1 - fr_ref[0, j] * _LOG2E
            if masked:
                t = jnp.where(_causal(i * TQ, j * TK, (TQ, TK), 0), t, _NEG)
            m_new = jnp.maximum(m, jnp.max(t, axis=1, keepdims=True) + fq2)
            alpha = jnp.exp2(m - m_new)
            p = jnp.exp2(t + (fq2 - m_new))
            l = alpha * l + jnp.sum(p, axis=1, keepdims=True)
            acc = alpha * acc + _dot_nn(p.astype(BF16), v_ref[pl.ds(off, TK), :])
            return m_new, l, acc

        init = (jnp.full((TQ, 1), _NEG, F32), jnp.zeros((TQ, 1), F32), jnp.zeros((TQ, FOX_DH), F32))
        last = (i * TQ) // TK
        carry = lax.fori_loop(0, last, lambda j, c: step(j, c, False), init)
        for d in range(TQ // TK):
            carry = step(last + d, carry, True)
        m, l, acc = carry
        o_ref[...] = (acc / l).astype(o_ref.dtype)
        lse_ref[0, 0] = _col_to_row((m + jnp.log2(l)) * _LN2)

    head = lambda blk: pl.BlockSpec((S, FOX_DH), lambda h, i: (0, blk + h))
    return pl.pallas_call(
        body, name=name,
        out_shape=(jax.ShapeDtypeStruct((S, D_MODEL), BF16), jax.ShapeDtypeStruct((FOX_HEADS, nq, 1, TQ), F32)),
        grid=(FOX_HEADS, nq),
        in_specs=[pl.BlockSpec((TQ, FOX_DH), lambda h, i: (i, _QF_BLK + h)), head(_KF_BLK), head(_VF_BLK),
                  pl.BlockSpec((TQ, LANES), lambda h, i: (i, 0)),
                  pl.BlockSpec((1, nk, 1, TK), lambda h, i: (h, 0, 0, 0))],
        out_specs=(pl.BlockSpec((TQ, FOX_DH), lambda h, i: (i, h)),
                   pl.BlockSpec((1, 1, 1, TQ), lambda h, i: (h, i, 0, 0))),
        compiler_params=_cparams(("parallel", "arbitrary")),
    )(pf, pf, pf, fc, fk_row)


def _fox_bwd(dhb, hb, pf, lse_row, fq_row, fc, name):
    S = pf.shape[0]
    TQ, TK = FOX_TQ, FOX_TK
    nq, nk, r = S // TQ, S // TK, TK // TQ
    c1 = _FOX_SCALE * _LOG2E

    def body(q_ref, k_ref, v_ref, do_ref, o_ref, lse_ref, fq_ref, fc_ref,
             dq_ref, dk_ref, dv_ref, dFk_ref, dFq_ref, dq_acc, qside, delta, dk_acc, dv_acc, cs_acc):
        h, j = pl.program_id(0), pl.program_id(1)

        @pl.when(j == 0)
        def _():
            dq_acc[...] = jnp.zeros_like(dq_acc)
            dFq_ref[...] = jnp.zeros_like(dFq_ref)

            def fill(b, _):
                off = pl.multiple_of(b * TQ, TQ)
                prod = do_ref[pl.ds(off, TQ), :].astype(F32) * o_ref[pl.ds(off, TQ), :].astype(F32)
                delta[b] = jnp.sum(prod.T, axis=0, keepdims=True)
                qside[b] = (fq_ref[0, b] - lse_ref[0, b]) * _LOG2E
                return 0

            lax.fori_loop(0, nq, fill, 0)

        kb = k_ref[...]
        vb = v_ref[...]
        fk2 = _lane_pick(fc_ref[...], h) * _LOG2E
        dk_acc[...] = jnp.zeros_like(dk_acc)
        dv_acc[...] = jnp.zeros_like(dv_acc)
        cs_acc[...] = jnp.zeros_like(cs_acc)

        def step(i, masked):
            off = pl.multiple_of(i * TQ, TQ)
            qb = q_ref[pl.ds(off, TQ), :]
            dob = do_ref[pl.ds(off, TQ), :]
            t = _dot_nt(kb, qb) * c1 + qside[i] - fk2
            if masked:
                t = jnp.where(_causal(i * TQ, j * TK, (TK, TQ), 1), t, _NEG)
            p = jnp.exp2(t)
            dv_acc[...] += _dot_nn(p.astype(BF16), dob)
            ds = p * (_dot_nt(vb, dob) - delta[i])
            dsb = ds.astype(BF16)
            dk_acc[...] += _dot_nn(dsb, qb)
            dq_acc[pl.ds(off, TQ), :] += _dot_tn(dsb, kb)
            cs_acc[...] += jnp.sum(ds, axis=1, keepdims=True)
            dFq_ref[0, i] += jnp.sum(ds, axis=0, keepdims=True)

        for d in range(r):
            step(r * j + d, True)

        def rest(i, _):
            step(i, False)
            return 0

        lax.fori_loop(r * j + r, nq, rest, 0)
        dk_ref[...] = (dk_acc[...] * _FOX_SCALE).astype(dk_ref.dtype)
        dv_ref[...] = dv_acc[...].astype(dv_ref.dtype)
        dFk_ref[0, 0] = -_col_to_row(cs_acc[...])

        @pl.when(j == nk - 1)
        def _():
            dq_ref[...] = (dq_acc[...] * _FOX_SCALE).astype(dq_ref.dtype)

    head = lambda blk: pl.BlockSpec((S, FOX_DH), lambda h, j: (0, blk + h))
    kblk = lambda blk: pl.BlockSpec((TK, FOX_DH), lambda h, j: (j, blk + h))
    qrows = pl.BlockSpec((1, nq, 1, TQ), lambda h, j: (h, 0, 0, 0))
    act = jax.ShapeDtypeStruct((S, D_MODEL), BF16)
    return pl.pallas_call(
        body, name=name,
        out_shape=(act, act, act, jax.ShapeDtypeStruct((FOX_HEADS, nk, 1, TK), F32),
                   jax.ShapeDtypeStruct((FOX_HEADS, nq, 1, TQ), F32)),
        grid=(FOX_HEADS, nk),
        in_specs=[head(_QF_BLK), kblk(_KF_BLK), kblk(_VF_BLK), head(0), head(0), qrows, qrows,
                  pl.BlockSpec((TK, LANES), lambda h, j: (j, 0))],
        out_specs=(head(0), kblk(0), kblk(0), pl.BlockSpec((1, 1, 1, TK), lambda h, j: (h, j, 0, 0)), qrows),
        scratch_shapes=[pltpu.VMEM((S, FOX_DH), F32), pltpu.VMEM((nq, 1, TQ), F32), pltpu.VMEM((nq, 1, TQ), F32),
                        pltpu.VMEM((TK, FOX_DH), F32), pltpu.VMEM((TK, FOX_DH), F32), pltpu.VMEM((TK, 1), F32)],
        compiler_params=_cparams(("parallel", "arbitrary")),
    )(pf, pf, pf, dhb, hb, lse_row, fq_row, fc)


def _pad_lanes(v):
    return jnp.pad(v, ((0, 0), (0, LANES - v.shape[1])))


def _local_step(x, target, wmain_t, wsmall_t, rest_weights, p, on_grads, advance, token):
    S = x.shape[0]
    bi, bf, bff = _pad_lanes(p["b_ml_i"]), _pad_lanes(p["b_ml_f"]), _pad_lanes(p["b_fox_f"])

    h0 = _rmsnorm_fwd(x, p["norm_mix_pre"] + token[0:1, 0:1], "norm_mix_pre")
    pm = _mm(h0, wmain_t[:N_ML], "nt", F32, "proj_mlstm")
    pf = _mm(h0, wmain_t[N_ML:N_ML + N_FOX], "nt", BF16, "proj_fox")
    pg = _mm(h0, wmain_t[N_ML + N_FOX:], "nt", F32, "proj_merge")
    ps = _mm(h0, wsmall_t, "nt", F32, "proj_gates")
    a, A, wi, em, wk, dec, Fc = _gates_fwd(ps, bi, bf, bff, "gates_fwd")
    a_row = a[:, :8].T
    ha, hp, den, cst, nst = _mlstm_fwd(pm, a_row, A, wi, em, wk, dec, p["ml_head_norm"], "mlstm_fwd")
    ft = Fc[:, :FOX_HEADS].T
    fq_row = ft.reshape(FOX_HEADS, S // FOX_TQ, 1, FOX_TQ)
    fk_row = ft.reshape(FOX_HEADS, S // FOX_TK, 1, FOX_TK)
    hb, lse_row = _fox_fwd(pf, Fc, ft.reshape(FOX_HEADS, S // FOX_TK_FWD, 1, FOX_TK_FWD), "fox_fwd")
    wa, wb, wout, wup, wdown = rest_weights(hb)
    ya = _mm(ha, wa, "nn", F32, "branch_a")
    yb = _mm(hb, wb, "nn", F32, "branch_b")
    merged = _merge_fwd(ya, yb, pg, p["b_gate_a"], p["b_gate_b"], "merge_fwd")
    z = _mm(merged, wout, "nn", F32, "out_proj")
    x1 = _resid_norm_fwd(x, z, p["norm_mix_post"], "resid_mix")
    h2 = _rmsnorm_fwd(x1, p["norm_ffn_pre"], "norm_ffn_pre")
    up = _mm(h2, wup, "nn", F32, "ffn_up")
    act, conv_a, conv_g = _conv_act_fwd(up, p["conv_w"], p["conv_b"], "conv_act_fwd")
    d = _mm(act, wdown, "nn", F32, "ffn_down")
    loss_row, dy, dd, g_norm_ffn_post = _loss_head(x1, d, p["norm_ffn_post"], target, "loss_head")
    dact = _mm(dd, wdown, "nt", F32, "d_act")
    g_wdown = _mm(act, dd, "tn", F32, "dw_down", tm=1408)
    dupa, dupg, dcwa, dcwg, dcba, dcbg = _conv_act_bwd(up, conv_a, conv_g, dact, p["conv_w"], "conv_act_bwd")
    g_conv_w = jnp.concatenate([dcwa, dcwg], axis=1)
    g_conv_b = jnp.concatenate([dcba, dcbg], axis=1)
    dh2 = _mm([dupa, dupg], wup, "nt", F32, "d_h2")
    g_wup = _mm(h2, [dupa, dupg], "tn", F32, "dw_up")
    token = on_grads("ffn", dict(w_up=g_wup, w_down=g_wdown))
    dx1, g_norm_ffn_pre = _rmsnorm_bwd([dh2], x1, p["norm_ffn_pre"] + token[0:1, 0:1], dy, F32, "norm_ffn_pre_bwd")
    dz, g_norm_mix_post = _rmsnorm_bwd([dx1], z, p["norm_mix_post"], None, BF16, "norm_mix_post_bwd")
    dmerged = _mm(dz, wout, "nt", F32, "d_merged")
    g_wout = _mm(merged, dz, "tn", F32, "dw_out")
    dya, dyb, dga, dgb, g_b_gate_a, g_b_gate_b = _merge_bwd(dmerged, ya, yb, pg, p["b_gate_a"], p["b_gate_b"], "merge_bwd")
    dha = _mm(dya, wa, "nt", F32, "d_ha")
    g_wa = _mm(ha, dya, "tn", F32, "dw_a")
    dhb = _mm(dyb, wb, "nt", BF16, "d_hb")
    g_wb = _mm(hb, dyb, "tn", F32, "dw_b")
    token = advance("ffn", g_wb) + on_grads("mix", dict(w_out=g_wout, w_branch_a=g_wa, w_branch_b=g_wb))
    dqkm, dvm, dom, rk, kc, tch, g_ml_head_norm = _mlstm_bwd(
        dha, pm, hp, den, a_row, A, wi, em, wk, dec, cst, nst, p["ml_head_norm"] + token[0:1, 0:1], "mlstm_bwd")
    token = advance("mix", dqkm)
    dqf, dkf, dvf, dFk, dFq = _fox_bwd(dhb, hb, pf, lse_row.reshape(fq_row.shape), fq_row + token[0, 0], Fc, "fox_bwd")
    dF = jnp.pad((dFk.reshape(FOX_HEADS, S) + dFq.reshape(FOX_HEADS, S)).T, ((0, 0), (0, LANES - FOX_HEADS)))
    dps, dbias = _gates_bwd(ps, bi, bf, bff, rk, kc, tch, dF, "gates_bwd")
    dpm = [dqkm, dvm, dom, dqf, dkf, dvf, dga, dgb]
    g_wmain_t = _mm(dpm, h0, "tn", F32, "dw_main")
    token = on_grads("in", dict(w_in=g_wmain_t))
    g_wsmall_t = _mm(dps, h0, "tn", F32, "dw_gates")
    dh0s = _mm(dps, wsmall_t + token[0:1, 0:1].astype(BF16), "nn", F32, "d_h0_gates")
    token = advance("in", dh0s)
    dh0 = _mm(dpm, wmain_t, "nn", F32, "d_h0_main", after=token)
    grad_x, g_norm_mix_pre = _rmsnorm_bwd([dh0, dh0s], x, p["norm_mix_pre"], dx1, F32, "norm_mix_pre_bwd")

    big = dict(wsmall_t=g_wsmall_t)
    small = dict(norm_mix_pre=g_norm_mix_pre, ml_head_norm=g_ml_head_norm, b_gate_a=g_b_gate_a, b_gate_b=g_b_gate_b,
                 norm_mix_post=g_norm_mix_post, norm_ffn_pre=g_norm_ffn_pre, norm_ffn_post=g_norm_ffn_post,
                 conv_b=g_conv_b, b_ml_i=dbias[:, 0:ML_HEADS], b_ml_f=dbias[:, LANES:LANES + ML_HEADS],
                 b_fox_f=dbias[:, 2 * LANES:2 * LANES + FOX_HEADS], conv_w=g_conv_w)
    return loss_row, grad_x, big, small


def _row_tile(r, target=256):
    best = None
    for t in range(8, min(r, target) + 1, 8):
        if r % t == 0:
            best = t
    return best if best is not None else r


def _adamw(w, g, m, v, name):
    _, R, C = w.shape
    tr = _row_tile(R)
    tc = C
    if tr == R and R > 256:
        tc = 256

    def body(w_ref, g_ref, m_ref, v_ref, d_ref, mo_ref, vo_ref):
        gv = g_ref[...]
        mn = ADAM_B1 * m_ref[0] + (1.0 - ADAM_B1) * gv
        vn = ADAM_B2 * v_ref[0] + (1.0 - ADAM_B2) * (gv * gv)
        m_hat = mn / (1.0 - ADAM_B1 ** ADAM_STEP)
        v_hat = vn / (1.0 - ADAM_B2 ** ADAM_STEP)
        d_ref[0] = -ADAM_LR * (m_hat / (jnp.sqrt(v_hat) + ADAM_EPS) + ADAM_WD * w_ref[0])
        mo_ref[0] = mn
        vo_ref[0] = vn

    blk = pl.BlockSpec((1, tr, tc), lambda i, j: (0, i, j))
    o = jax.ShapeDtypeStruct((1, R, C), F32)
    return pl.pallas_call(
        body, name=name, out_shape=(o, o, o), grid=(R // tr, C // tc),
        in_specs=[blk, pl.BlockSpec((tr, tc), lambda i, j: (i, j)), blk, blk], out_specs=(blk,) * 3,
        compiler_params=_cparams(("parallel", "parallel")),
    )(w, g, m, v)


ANY = pl.BlockSpec(memory_space=pl.ANY)


def _place():
    x, y, c = lax.axis_index("x"), lax.axis_index("y"), lax.axis_index("c")
    chips = [(1 - x, y), (x, 1 - y), (1 - x, 1 - y)]
    return x, y, c, chips


def _block(ref, kind, k, rows=None):
    if kind == "rows":
        return ref.at[k] if rows is None else ref.at[k, pl.ds(*rows), :]
    cb = ref.shape[1] // 4
    return ref.at[:, pl.ds(k * cb, cb)] if rows is None else ref.at[pl.ds(*rows), pl.ds(k * cb, cb)]


def _gathered_shape(s, kind):
    return (4,) + s.shape if kind == "rows" else (s.shape[0], 4 * s.shape[1])


def _gather_weights(shards, kinds, smalls):
    n, ns = len(shards), len(smalls)

    def body(*refs):
        ins, sm_in = refs[:n], refs[n:n + ns]
        outs, sm_out = refs[n + ns:2 * n + ns], refs[2 * n + ns:2 * (n + ns)]
        send_sems, recv_sems, sm_send, sm_recv, local_sems = refs[2 * (n + ns):]
        x, y, c, chips = _place()
        sibling = (x, y, 1 - c)
        kme = 2 * x + y

        def half(a, k, hc):
            h = ins[a].shape[0] // 2
            return _block(outs[a], kinds[a], k, (hc * h, h))

        def remote(a, slot, src, dst, to):
            return pltpu.make_async_remote_copy(src_ref=src, dst_ref=dst, send_sem=send_sems.at[a * 7 + slot],
                                                recv_sem=recv_sems.at[a * 7 + slot], device_id=to, device_id_type=MESH)

        def sm_copy(b, j, k, to):
            return pltpu.make_async_remote_copy(src_ref=sm_in[b], dst_ref=sm_out[b].at[k], send_sem=sm_send.at[3 * b + j],
                                                recv_sem=sm_recv.at[3 * b + j], device_id=to, device_id_type=MESH)

        local = [pltpu.make_async_copy(sm_in[b], sm_out[b].at[kme], local_sems.at[b]) for b in range(ns)]
        for cp in local:
            cp.start()
        sends = [remote(a, 6, ins[a], _block(outs[a], kinds[a], kme), sibling) for a in range(n)]
        for a in range(n):
            h = ins[a].shape[0] // 2
            for j, chip in enumerate(chips):
                sends.append(remote(a, j, ins[a].at[pl.ds(c * h, h), :], half(a, kme, c), (*chip, c)))
        for b in range(ns):
            for j, chip in enumerate(chips):
                sends.append(sm_copy(b, j, kme, (*chip, c)))
        for cp in sends:
            cp.start()
        for a in range(n):
            for j, chip in enumerate(chips):
                kj = 2 * chip[0] + chip[1]
                remote(a, j, half(a, kj, c), half(a, kj, c), (*chip, c)).wait_recv()
                fwd = remote(a, 3 + j, half(a, kj, c), half(a, kj, c), sibling)
                fwd.start()
                sends.append(fwd)
        for a in range(n):
            for j, chip in enumerate(chips):
                kj = 2 * chip[0] + chip[1]
                remote(a, 3 + j, half(a, kj, 1 - c), half(a, kj, 1 - c), sibling).wait_recv()
        for b in range(ns):
            for j, chip in enumerate(chips):
                sm_copy(b, j, 2 * chip[0] + chip[1], (*chip, c)).wait_recv()
        for a in range(n):
            remote(a, 6, ins[a], _block(outs[a], kinds[a], kme), sibling).wait_recv()
        for cp in sends:
            cp.wait_send()
        for cp in local:
            cp.wait()

    outs = pl.pallas_call(
        body, name="gather_weights",
        out_shape=tuple([jax.ShapeDtypeStruct(_gathered_shape(s, k), s.dtype) for s, k in zip(shards, kinds)]
                        + [jax.ShapeDtypeStruct((4,) + s.shape, s.dtype) for s in smalls]),
        in_specs=[ANY] * (n + ns), out_specs=tuple([ANY] * (n + ns)),
        scratch_shapes=[pltpu.SemaphoreType.DMA((7 * n,)), pltpu.SemaphoreType.DMA((7 * n,)),
                        pltpu.SemaphoreType.DMA((3 * ns,)), pltpu.SemaphoreType.DMA((3 * ns,)),
                        pltpu.SemaphoreType.DMA((ns,))],
    )(*shards, *smalls)
    return outs[:n], outs[n:]


_IN_HBM = pl.BlockSpec(memory_space=pltpu.HBM)
_SEMS = pl.BlockSpec(memory_space=pltpu.SEMAPHORE)
_DATAFLOW = pltpu.SideEffectType.DATAFLOW_SIDE_EFFECTING


def _hbm(t):
    return pltpu.HBM(t.shape, t.dtype)


def _gather_copies(ins, outs, send_sems, recv_sems, kinds):
    x, y, c, chips = _place()
    kme = 2 * x + y
    cps = []
    for a in range(len(ins)):
        h = ins[a].shape[0] // 2
        for j, chip in enumerate(chips + [None]):
            to = (x, y, 1 - c) if chip is None else (*chip, c)
            src = ins[a] if chip is None else ins[a].at[pl.ds(c * h, h), :]
            dst = _block(outs[a], kinds[a], kme, None if chip is None else (c * h, h))
            cps.append(pltpu.make_async_remote_copy(src_ref=src, dst_ref=dst, send_sem=send_sems.at[4 * a + j],
                                                    recv_sem=recv_sems.at[4 * a + j], device_id=to, device_id_type=MESH))
    return cps


def _gather_start(shards, kinds, name):
    n = len(shards)
    outs = [lax.empty(_gathered_shape(s, k), s.dtype) for s, k in zip(shards, kinds)]

    def body(*refs):
        for cp in _gather_copies(refs[:n], refs[n:2 * n], refs[2 * n], refs[2 * n + 1], kinds):
            cp.start()
        refs[-1][...] = jnp.zeros_like(refs[-1])

    return pl.pallas_call(
        body, name=name,
        out_shape=(pltpu.SemaphoreType.DMA((4 * n,)), pltpu.SemaphoreType.DMA((4 * n,)),
                   *[_hbm(t) for t in shards], *[_hbm(t) for t in outs], jax.ShapeDtypeStruct((8, LANES), F32)),
        in_specs=[_IN_HBM] * (2 * n),
        out_specs=(_SEMS, _SEMS, *[_IN_HBM] * (2 * n), pl.BlockSpec(memory_space=pltpu.VMEM)),
        input_output_aliases={a: 2 + a for a in range(2 * n)},
        compiler_params=pltpu.CompilerParams(has_side_effects=_DATAFLOW),
    )(*[pltpu.with_memory_space_constraint(t, pltpu.HBM) for t in list(shards) + outs])


def _gather_wait(started, after, kinds, name):
    n = (len(started) - 3) // 2
    bufs = started[2:2 + 2 * n]

    def body(*refs):
        for cp in _gather_copies(refs[:n], refs[n:2 * n], refs[2 * n], refs[2 * n + 1], kinds):
            cp.wait_send()
            cp.wait_recv()

    outs = pl.pallas_call(
        body, name=name, out_shape=tuple(_hbm(t) for t in bufs),
        in_specs=[_IN_HBM] * (2 * n) + [_SEMS, _SEMS, ANY], out_specs=tuple([_IN_HBM] * (2 * n)),
        input_output_aliases={a: a for a in range(2 * n)},
        compiler_params=pltpu.CompilerParams(has_side_effects=_DATAFLOW),
    )(*bufs, started[0], started[1], after)
    return outs[n:]


def _gather_relay(bufs, kinds, name):
    n = len(bufs)

    def body(*refs):
        ins, outs, send_sems, recv_sems = refs[:n], refs[n:2 * n], refs[2 * n], refs[2 * n + 1]
        x, y, c, chips = _place()
        cps = []
        for a in range(n):
            h = (ins[a].shape[1] if kinds[a] == "rows" else ins[a].shape[0]) // 2
            for j, chip in enumerate(chips):
                kj = 2 * chip[0] + chip[1]
                cps.append((pltpu.make_async_remote_copy(
                    src_ref=_block(ins[a], kinds[a], kj, (c * h, h)), dst_ref=_block(outs[a], kinds[a], kj, (c * h, h)),
                    send_sem=send_sems.at[3 * a + j], recv_sem=recv_sems.at[3 * a + j], device_id=(x, y, 1 - c),
                    device_id_type=MESH), a, kj, h))
        for cp, _, _, _ in cps:
            cp.start()
        for a_cp, (cp, a, kj, h) in enumerate(cps):
            theirs = _block(outs[a], kinds[a], kj, ((1 - c) * h, h))
            pltpu.make_async_remote_copy(src_ref=theirs, dst_ref=theirs, send_sem=send_sems.at[a_cp],
                                         recv_sem=recv_sems.at[a_cp], device_id=(x, y, 1 - c), device_id_type=MESH).wait_recv()
        for cp, _, _, _ in cps:
            cp.wait_send()

    return pl.pallas_call(
        body, name=name, out_shape=tuple(jax.ShapeDtypeStruct(b.shape, b.dtype) for b in bufs),
        in_specs=[ANY] * n, out_specs=tuple([ANY] * n), input_output_aliases={a: a for a in range(n)},
        scratch_shapes=[pltpu.SemaphoreType.DMA((3 * n,)), pltpu.SemaphoreType.DMA((3 * n,))],
    )(*bufs)


def _add_halves(g, r1, cvec, kind, name):
    def body(c_ref, g_ref, r_ref, o_ref):
        o_ref[...] = (g_ref[...] + r_ref[...]).astype(o_ref.dtype)

    if kind == "rows":
        _, h, C = r1.shape
        tr = _row_tile(h, 512)
        nt = h // tr
        grid = (4, nt)
        g_spec = pl.BlockSpec((1, tr, C), lambda k, i, c_ref: (k, c_ref[0] * nt + i, 0))
        r_spec = pl.BlockSpec((1, tr, C), lambda k, i, c_ref: (k, i, 0))
    else:
        h, C4 = r1.shape
        tr, tc = _row_tile(h, 512), C4 // 4
        nt = h // tr
        grid = (nt, 4)
        g_spec = pl.BlockSpec((tr, tc), lambda i, k, c_ref: (c_ref[0] * nt + i, k))
        r_spec = pl.BlockSpec((tr, tc), lambda i, k, c_ref: (i, k))
    return pl.pallas_call(
        body, name=name, out_shape=jax.ShapeDtypeStruct(r1.shape, BF16),
        grid_spec=pltpu.PrefetchScalarGridSpec(num_scalar_prefetch=1, grid=grid, in_specs=[g_spec, r_spec],
                                               out_specs=r_spec),
        compiler_params=_cparams(("parallel", "parallel")),
    )(cvec, g, r1)


def _chip_copies(ins, lands, send_sems, recv_sems, kinds):
    x, y, c, chips = _place()
    return [pltpu.make_async_remote_copy(
        src_ref=_block(ins[a], kinds[a], 2 * chip[0] + chip[1]), dst_ref=lands[a].at[j],
        send_sem=send_sems.at[3 * a + j], recv_sem=recv_sems.at[3 * a + j], device_id=(*chip, c), device_id_type=MESH)
        for a in range(len(ins)) for j, chip in enumerate(chips)]


def _land_shape(s, kind):
    return (3,) + (s.shape[1:] if kind == "rows" else (s.shape[0], s.shape[1] // 4))


def _sibling_copies(ins, lands, send_sems, recv_sems, kinds):
    x, y, c, _ = _place()
    cps = []
    for a in range(len(ins)):
        h = lands[a].shape[-2]
        src = ins[a].at[:, pl.ds((1 - c) * h, h), :] if kinds[a] == "rows" else ins[a].at[pl.ds((1 - c) * h, h), :]
        cps.append(pltpu.make_async_remote_copy(src_ref=src, dst_ref=lands[a], send_sem=send_sems.at[a],
                                                recv_sem=recv_sems.at[a], device_id=(x, y, 1 - c), device_id_type=MESH))
    return cps


def _half_shape(g, kind):
    return (4, g.shape[1] // 2, g.shape[2]) if kind == "rows" else (g.shape[0] // 2, g.shape[1])


def _exchange_start(copies, per_array, srcs, land_shapes, kinds, name):
    n = len(srcs)
    lands = [lax.empty(shape, s.dtype) for shape, s in zip(land_shapes, srcs)]

    def body(*refs):
        for cp in copies(refs[:n], refs[n:2 * n], refs[2 * n], refs[2 * n + 1], kinds):
            cp.start()
        refs[-1][...] = jnp.zeros_like(refs[-1])

    return pl.pallas_call(
        body, name=name,
        out_shape=(pltpu.SemaphoreType.DMA((per_array * n,)), pltpu.SemaphoreType.DMA((per_array * n,)),
                   *[_hbm(t) for t in srcs], *[_hbm(t) for t in lands], jax.ShapeDtypeStruct((8, LANES), F32)),
        in_specs=[_IN_HBM] * (2 * n),
        out_specs=(_SEMS, _SEMS, *[_IN_HBM] * (2 * n), pl.BlockSpec(memory_space=pltpu.VMEM)),
        input_output_aliases={a: 2 + a for a in range(2 * n)},
        compiler_params=pltpu.CompilerParams(has_side_effects=_DATAFLOW),
    )(*[pltpu.with_memory_space_constraint(t, pltpu.HBM) for t in list(srcs) + lands])


def _exchange_wait(copies, started, after, kinds, name):
    n = (len(started) - 3) // 2
    bufs = started[2:2 + 2 * n]

    def body(*refs):
        for cp in copies(refs[:n], refs[n:2 * n], refs[2 * n], refs[2 * n + 1], kinds):
            cp.wait_send()
            cp.wait_recv()

    outs = pl.pallas_call(
        body, name=name, out_shape=tuple(_hbm(t) for t in bufs),
        in_specs=[_IN_HBM] * (2 * n) + [_SEMS, _SEMS, ANY], out_specs=tuple([_IN_HBM] * (2 * n)),
        input_output_aliases={a: a for a in range(2 * n)},
        compiler_params=pltpu.CompilerParams(has_side_effects=_DATAFLOW),
    )(*bufs, started[0], started[1], after)
    return outs[:n], outs[n:]


def _add_chips(s1, r2, kcvec, kind, name):
    _, h, C = r2.shape
    tr = _row_tile(h, 512)
    nt = h // tr

    def body(kc_ref, s_ref, r0_ref, r1_ref, r2_ref, o_ref):
        s = s_ref[0] if kind == "rows" else s_ref[...]
        o_ref[...] = ((s.astype(F32) + r0_ref[0].astype(F32)) + r1_ref[0].astype(F32)) + r2_ref[0].astype(F32)

    peer = lambda j: pl.BlockSpec((1, tr, C), lambda i, kc_ref: (j, i, 0))
    if kind == "rows":
        s_spec = pl.BlockSpec((1, tr, C), lambda i, kc_ref: (kc_ref[0], i, 0))
    else:
        s_spec = pl.BlockSpec((tr, C), lambda i, kc_ref: (i, kc_ref[0]))
    return pl.pallas_call(
        body, name=name, out_shape=jax.ShapeDtypeStruct((2 * h, C), F32),
        grid_spec=pltpu.PrefetchScalarGridSpec(
            num_scalar_prefetch=1, grid=(nt,),
            in_specs=[s_spec, peer(0), peer(1), peer(2)],
            out_specs=pl.BlockSpec((tr, C), lambda i, kc_ref: (kc_ref[1] * nt + i, 0))),
        compiler_params=_cparams(("parallel",)),
    )(kcvec, s1, r2, r2, r2)


def _join_sibling_halves(bufs):
    n = len(bufs)

    def body(*refs):
        ins, outs, send_sems, recv_sems = refs[:n], refs[n:2 * n], refs[2 * n], refs[2 * n + 1]
        x, y, c, _ = _place()
        cps = []
        for a in range(n):
            h = ins[a].shape[0] // 2
            cps.append(pltpu.make_async_remote_copy(
                src_ref=ins[a].at[pl.ds(c * h, h), :], dst_ref=outs[a].at[pl.ds(c * h, h), :], send_sem=send_sems.at[a],
                recv_sem=recv_sems.at[a], device_id=(x, y, 1 - c), device_id_type=MESH))
        for cp in cps:
            cp.start()
        for a in range(n):
            h = ins[a].shape[0] // 2
            theirs = outs[a].at[pl.ds((1 - c) * h, h), :]
            pltpu.make_async_remote_copy(src_ref=theirs, dst_ref=theirs, send_sem=send_sems.at[a],
                                         recv_sem=recv_sems.at[a], device_id=(x, y, 1 - c), device_id_type=MESH).wait_recv()
        for cp in cps:
            cp.wait_send()

    return pl.pallas_call(
        body, name="grads_join",
        out_shape=tuple(jax.ShapeDtypeStruct(b.shape, b.dtype) for b in bufs),
        in_specs=[ANY] * n, out_specs=tuple([ANY] * n), input_output_aliases={a: a for a in range(n)},
        scratch_shapes=[pltpu.SemaphoreType.DMA((n,)), pltpu.SemaphoreType.DMA((n,))],
    )(*bufs)


N_DEV = 8


def _allreduce_small(pack):
    P = pack.shape[0]

    def body(p_ref, o_ref, gath, send_sems, recv_sems):
        x, y, c, _ = _place()
        me = 4 * x + 2 * y + c
        cps = []
        for mask in range(1, N_DEV):
            px = 1 - x if mask & 4 else x
            py = 1 - y if mask & 2 else y
            pc = 1 - c if mask & 1 else c
            cps.append((pltpu.make_async_remote_copy(
                src_ref=p_ref, dst_ref=gath.at[me], send_sem=send_sems.at[mask - 1], recv_sem=recv_sems.at[mask - 1],
                device_id=(px, py, pc), device_id_type=MESH), 4 * px + 2 * py + pc, mask))
        for cp, _, _ in cps:
            cp.start()
        gath[me] = p_ref[...]
        for _, peer, mask in cps:
            pltpu.make_async_remote_copy(
                src_ref=p_ref, dst_ref=gath.at[peer], send_sem=send_sems.at[mask - 1], recv_sem=recv_sems.at[mask - 1],
                device_id=(x, y, c), device_id_type=MESH).wait_recv()
        for cp, _, _ in cps:
            cp.wait_send()
        acc = gath[0]
        for i in range(1, N_DEV):
            acc = acc + gath[i]
        o_ref[...] = acc

    return pl.pallas_call(
        body, name="allreduce_small", out_shape=jax.ShapeDtypeStruct((P, LANES), F32),
        in_specs=[pl.BlockSpec(memory_space=pltpu.VMEM)], out_specs=pl.BlockSpec(memory_space=pltpu.VMEM),
        scratch_shapes=[pltpu.VMEM((N_DEV, P, LANES), F32), pltpu.SemaphoreType.DMA((N_DEV - 1,)),
                        pltpu.SemaphoreType.DMA((N_DEV - 1,))],
    )(pack)


def _pack_rows(arrs):
    rows = []
    for a in arrs:
        f = a.reshape(-1)
        f = jnp.pad(f, (0, (-f.shape[0]) % (8 * LANES)))
        rows.append(f.reshape(-1, LANES))
    return jnp.concatenate(rows, axis=0)


def _unpack_rows(pack, shapes):
    out, r = [], 0
    for s in shapes:
        n = math.prod(s)
        out.append(pack[r:r + -(-n // LANES)].reshape(-1)[:n].reshape(s))
        r += 8 * -(-n // (8 * LANES))
    return out


_SMALL = ["norm_mix_pre", "ml_head_norm", "b_gate_a", "b_gate_b", "norm_mix_post", "norm_ffn_pre", "norm_ffn_post",
          "conv_b", "b_ml_i", "b_ml_f", "b_fox_f"]
_BIG = ["w_in", "w_branch_a", "w_branch_b", "w_out", "w_up", "w_down"]
_WEIGHTS = ['norm_mix_pre', 'w_in', 'b_ml_i', 'b_ml_f', 'ml_head_norm', 'b_fox_f', 'b_gate_a', 'b_gate_b', 'w_branch_a',
            'w_branch_b', 'w_out', 'norm_mix_post', 'norm_ffn_pre', 'w_up', 'conv_w', 'conv_b', 'w_down', 'norm_ffn_post']


_KINDS = ["rows", "rows", "rows", "rows", "cols", "rows"]


def kernel(x, norm_mix_pre, w_in, b_ml_i, b_ml_f, ml_head_norm, b_fox_f, b_gate_a, b_gate_b, w_branch_a, w_branch_b, w_out, norm_mix_post, norm_ffn_pre, w_up, conv_w, conv_b, w_down, norm_ffn_post, loss_target, m_norm_mix_pre, m_w_in, m_b_ml_i, m_b_ml_f, m_ml_head_norm, m_b_fox_f, m_b_gate_a, m_b_gate_b, m_w_branch_a, m_w_branch_b, m_w_out, m_norm_mix_post, m_norm_ffn_pre, m_w_up, m_conv_w, m_conv_b, m_w_down, m_norm_ffn_post, v_norm_mix_pre, v_w_in, v_b_ml_i, v_b_ml_f, v_ml_head_norm, v_b_fox_f, v_b_gate_a, v_b_gate_b, v_w_branch_a, v_w_branch_b, v_w_out, v_norm_mix_post, v_norm_ffn_pre, v_w_up, v_conv_w, v_conv_b, v_w_down, v_norm_ffn_post):
    args = dict(locals())
    w = {n: args[n] for n in _WEIGHTS}
    mom = {n: args["m_" + n] for n in _WEIGHTS}
    var = {n: args["v_" + n] for n in _WEIGHTS}
    cx, cy, cc = lax.axis_index("x"), lax.axis_index("y"), lax.axis_index("c")
    kme = 2 * cx + cy
    cvec = jnp.reshape(cc, (1,)).astype(jnp.int32)
    kcvec = jnp.stack([kme, cc]).astype(jnp.int32)
    odd = kme % 2

    tr3 = lambda t: jnp.transpose(t, (0, 2, 1))
    w["w_in"], mom["w_in"], var["w_in"] = tr3(w_in), tr3(m_w_in), tr3(v_w_in)
    w_in_main = lax.dynamic_slice_in_dim(w["w_in"][0], 4 * odd, 2048, axis=0).astype(BF16)
    w_in_gates = lax.dynamic_slice_in_dim(w["w_in"][0], 2048 * (1 - odd), 4, axis=0).astype(BF16)
    (wmain_t,), (g_cw, g_gates) = _gather_weights([w_in_main], _KINDS[:1], [w["conv_w"][0], w_in_gates])
    rest_started = _gather_start([w[n][0].astype(BF16) for n in _BIG[1:]], _KINDS[1:], "gather_rest_start")

    def rest_weights(after):
        bufs = _gather_wait(rest_started, after, _KINDS[1:], "gather_rest_wait")
        g_a, g_b, g_out, wup, g_down = _gather_relay(bufs, _KINDS[1:], "gather_rest_relay")
        return full(g_a), full(g_b), full(g_out), wup, full(g_down)
    gate_rows = g_gates.reshape(16, D_MODEL)
    wsmall_t = jnp.zeros((N_SMALL, D_MODEL), BF16)
    for blk, (lo, hi) in enumerate(((0, 4), (4, 8), (8, 16))):
        wsmall_t = wsmall_t.at[blk * LANES:blk * LANES + hi - lo].set(gate_rows[lo:hi])
    full = lambda g: g.reshape(-1, g.shape[2])
    p = {n: w[n] for n in _SMALL}
    p["conv_w"] = jnp.transpose(g_cw, (1, 0, 2)).reshape(3, -1)

    groups = {}

    def on_grads(group, gs):
        names = list(gs)
        kinds = [_KINDS[_BIG.index(n)] for n in names]
        whole = [g if k == "cols" else g.reshape(4, -1, g.shape[1]) for g, k in zip(gs.values(), kinds)]
        started = _exchange_start(_sibling_copies, 1, whole, [_half_shape(g, k) for g, k in zip(whole, kinds)], kinds,
                                  "grads_to_sibling_start_" + group)
        groups[group] = dict(names=names, kinds=kinds, sibling=started)
        return started[-1]

    def advance(group, after):
        G = groups[group]
        whole, got = _exchange_wait(_sibling_copies, G["sibling"], after, G["kinds"], "grads_to_sibling_wait_" + group)
        sums = [_add_halves(g, r, cvec, k, "add_sibling_" + n) for g, r, k, n in zip(whole, got, G["kinds"], G["names"])]
        G["chips"] = _exchange_start(_chip_copies, 3, sums, [_land_shape(s, k) for s, k in zip(sums, G["kinds"])],
                                     G["kinds"], "grads_to_chips_start_" + group)
        return G["chips"][-1]

    loss_row, grad_x, big, small = _local_step(x[0], loss_target[0], full(wmain_t), wsmall_t, rest_weights, p, on_grads,
                                               advance, rest_started[-1])
    grads = {}
    mine, mine_names = [], []
    for group, G in groups.items():
        sums, got = _exchange_wait(_chip_copies, G["chips"], grad_x, G["kinds"], "grads_to_chips_wait_" + group)
        mine += [_add_chips(s, r, kcvec, k, "add_chips_" + n) for s, r, k, n in zip(sums, got, G["kinds"], G["names"])]
        mine_names += G["names"]
    grads.update(zip(mine_names, _join_sibling_halves(mine)))

    gt = big["wsmall_t"]
    small["w_in_gates"] = jnp.concatenate([gt[0:4], gt[LANES:LANES + 4], gt[2 * LANES:2 * LANES + 8]], axis=0)
    small_names = _SMALL + ["conv_w"]
    packed_names = small_names + ["w_in_gates"]
    pack = _pack_rows([small[n] for n in packed_names] + [loss_row])
    pack = jnp.pad(pack, ((0, (-pack.shape[0]) % 8), (0, 0)))
    full_shapes = [small[n].shape if n in ("conv_w", "w_in_gates") else w[n][0].shape for n in packed_names]
    total = _unpack_rows(_allreduce_small(pack), full_shapes + [loss_row.shape])
    for n, t in zip(packed_names, total):
        grads[n] = t
    loss = total[-1][0, 0]
    grads["conv_w"] = lax.dynamic_slice_in_dim(grads["conv_w"], kme * conv_w.shape[2], conv_w.shape[2], axis=1)
    my_gates = lax.dynamic_slice_in_dim(grads.pop("w_in_gates"), 4 * kme, 4, axis=0)
    g_in = jnp.zeros(w["w_in"].shape[1:], F32)
    g_in = lax.dynamic_update_slice_in_dim(g_in, grads["w_in"], 4 * odd, axis=0)
    grads["w_in"] = lax.dynamic_update_slice_in_dim(g_in, my_gates, 2048 * (1 - odd), axis=0)

    delta, new_m, new_v = {}, {}, {}
    for n in _BIG:
        delta[n], new_m[n], new_v[n] = _adamw(w[n], grads[n], mom[n], var[n], "adamw_" + n)
        grads[n] = grads[n][None]
    for d in (grads, delta, new_m, new_v):
        d["w_in"] = tr3(d["w_in"])
    packs = [_pack_rows([d[n][0] for n in small_names]) for d in (w, mom, var)]
    pad = ((0, (-packs[0].shape[0]) % 8), (0, 0))
    packs = [jnp.pad(t, pad)[None] for t in packs]
    gp = jnp.pad(_pack_rows([grads[n] for n in small_names]), pad)
    shapes = [w[n][0].shape for n in small_names]
    for dst, res in zip((delta, new_m, new_v), _adamw(packs[0], gp, packs[1], packs[2], "adamw_small")):
        for n, t in zip(small_names, _unpack_rows(res[0], shapes)):
            dst[n] = t[None]
    for n in small_names:
        grads[n] = grads[n][None]

    return (loss, grad_x[None], *[grads[n] for n in _WEIGHTS], *[delta[n] for n in _WEIGHTS],
            *[new_m[n] for n in _WEIGHTS], *[new_v[n] for n in _WEIGHTS])
```

```python
import functools
import math

import jax
import jax.numpy as jnp
from jax import lax
from jax.experimental import pallas as pl
from jax.experimental.pallas import tpu as pltpu

F32 = jnp.float32
BF16 = jnp.bfloat16
MESH = pl.DeviceIdType.MESH

D_MODEL = 1024
ML_HEADS = 4
ML_DQK = 128
ML_DV = 256
FOX_HEADS = 8
FOX_DH = 128
D_FF = 2816
GATE_CAP = 15.0
EPS = 1e-6
ADAM_LR, ADAM_B1, ADAM_B2, ADAM_EPS, ADAM_WD, ADAM_STEP = 0.001, 0.9, 0.999, 1e-08, 0.01, 10

LANES = 128
MLC = 256
FOX_TQ = 512
FOX_TQ_FWD = 512
FOX_TK = 512
FOX_TK_FWD = 512
ROW_T = 512
CONV_TC = 1408
VMEM_LIMIT = 56 * 1024 * 1024

C_QM, C_KM, C_VM, C_OM = 0, 512, 1024, 2048
N_ML, N_FOX, N_GATE = 3072, 3072, 2048
N_SMALL = 384


def _cparams(sem=None):
    return pltpu.CompilerParams(dimension_semantics=sem, vmem_limit_bytes=VMEM_LIMIT)


def _tile(n, target):
    if n <= target:
        return n
    best = None
    for t in range(LANES, target + 1, LANES):
        if n % t == 0:
            best = t
    assert best is not None, (n, target)
    return best


def _dot(a, b, dims):
    return lax.dot_general(a, b, (dims, ((), ())), preferred_element_type=F32)


def _dot_nn(a, b):
    return _dot(a, b, ((1,), (0,)))


def _dot_nt(a, b):
    return _dot(a, b, ((1,), (1,)))


def _dot_tn(a, b):
    return _dot(a, b, ((0,), (0,)))


_DOTS = {"nn": _dot_nn, "nt": _dot_nt, "tn": _dot_tn}


def _mm(a, b, mode, out_dtype, name, tm=1024, tn=1408, tk=1408, after=None):
    a_parts = list(a) if isinstance(a, (list, tuple)) else [a]
    b_parts = list(b) if isinstance(b, (list, tuple)) else [b]
    extra = [] if after is None else [after]
    assert len(a_parts) == 1 or len(b_parts) == 1, name
    a_axes = {"nn": "ik", "nt": "ik", "tn": "ki"}[mode]
    b_axes = {"nn": "kj", "nt": "jk", "tn": "kj"}[mode]
    size, target = {}, dict(i=tm, j=tn, k=tk)
    for parts, axes in ((a_parts, a_axes), (b_parts, b_axes)):
        dims = (parts[0].shape[0], parts[0].shape[1] * len(parts))
        for ax, n in zip(axes, dims):
            assert size.setdefault(ax, n) == n, (name, ax, n, size)
    tile = {}
    for parts, axes in ((a_parts, a_axes), (b_parts, b_axes)):
        if len(parts) > 1:
            tile[axes[1]] = _tile(parts[0].shape[1], target[axes[1]])
    for ax in "ijk":
        tile.setdefault(ax, _tile(size[ax], target[ax]))
    M, N, nk = size["i"], size["j"], size["k"] // tile["k"]
    grid_pos = dict(i=0, j=1, k=2)
    dot = _DOTS[mode]

    def specs(parts, axes):
        blk = (tile[axes[0]], tile[axes[1]])
        if len(parts) == 1:
            return [pl.BlockSpec(blk, lambda *g: (g[grid_pos[axes[0]]], g[grid_pos[axes[1]]]))], None
        bpp = parts[0].shape[1] // blk[1]

        def index(p):
            def f(*g):
                g0, g1 = g[grid_pos[axes[0]]], g[grid_pos[axes[1]]]
                on = g1 // bpp == p
                return jnp.where(on, g0, 0), jnp.where(on, g1 % bpp, 0)
            return f

        return [pl.BlockSpec(blk, index(p)) for p in range(len(parts))], (axes[1], bpp)

    a_specs, a_sel = specs(a_parts, a_axes)
    b_specs, b_sel = specs(b_parts, b_axes)
    na, nb = len(a_parts), len(b_parts)

    def body(*refs):
        a_refs, b_refs = refs[:na], refs[na:na + nb]
        o_ref, acc = refs[na + nb + len(extra)], refs[na + nb + len(extra) + 1:]

        def accumulate(part):
            if nk == 1:
                o_ref[...] = part.astype(o_ref.dtype)
                return
            acc_ref, = acc
            k = pl.program_id(2)

            @pl.when(k == 0)
            def _():
                acc_ref[...] = part

            @pl.when(k > 0)
            def _():
                acc_ref[...] += part

            @pl.when(k == nk - 1)
            def _():
                o_ref[...] = acc_ref[...].astype(o_ref.dtype)

        sel = a_sel or b_sel
        if sel is None:
            accumulate(dot(a_refs[0][...], b_refs[0][...]))
        else:
            which = pl.program_id(grid_pos[sel[0]]) // sel[1]
            for p in range(max(na, nb)):
                @pl.when(which == p)
                def _(p=p):
                    accumulate(dot(a_refs[p if a_sel else 0][...], b_refs[p if b_sel else 0][...]))

    return pl.pallas_call(
        body, name=name,
        out_shape=jax.ShapeDtypeStruct((M, N), out_dtype),
        grid=(M // tile["i"], N // tile["j"], nk),
        in_specs=a_specs + b_specs + [pl.BlockSpec(memory_space=pl.ANY)] * len(extra),
        out_specs=pl.BlockSpec((tile["i"], tile["j"]), lambda i, j, k: (i, j)),
        scratch_shapes=[pltpu.VMEM((tile["i"], tile["j"]), F32)] if nk > 1 else [],
        compiler_params=_cparams(("parallel", "parallel", "arbitrary")),
    )(*a_parts, *b_parts, *extra)


def _rstd(x):
    return lax.rsqrt(jnp.mean(x * x, axis=-1, keepdims=True) + EPS)


def _rmsnorm_fwd(x, g, name):
    S, D = x.shape
    T = _tile(S, ROW_T)

    def body(x_ref, g_ref, o_ref):
        xv = x_ref[...]
        o_ref[...] = (xv * _rstd(xv) * g_ref[...]).astype(o_ref.dtype)

    return pl.pallas_call(
        body, name=name, out_shape=jax.ShapeDtypeStruct((S, D), BF16), grid=(S // T,),
        in_specs=[pl.BlockSpec((T, D), lambda i: (i, 0)), pl.BlockSpec((1, D), lambda i: (0, 0))],
        out_specs=pl.BlockSpec((T, D), lambda i: (i, 0)),
        compiler_params=_cparams(("parallel",)),
    )(x, g)


def _resid_norm_fwd(x, z, g, name):
    S, D = x.shape
    T = _tile(S, ROW_T)

    def body(x_ref, z_ref, g_ref, o_ref):
        zv = z_ref[...]
        o_ref[...] = x_ref[...] + zv * _rstd(zv) * g_ref[...]

    row = pl.BlockSpec((T, D), lambda i: (i, 0))
    return pl.pallas_call(
        body, name=name, out_shape=jax.ShapeDtypeStruct((S, D), F32), grid=(S // T,),
        in_specs=[row, row, pl.BlockSpec((1, D), lambda i: (0, 0))],
        out_specs=row, compiler_params=_cparams(("parallel",)),
    )(x, z, g)


def _rmsnorm_bwd_math(dy, xv, g):
    r = _rstd(xv)
    u = dy * g
    dx = r * u - xv * (r * r * r) * jnp.mean(u * xv, axis=-1, keepdims=True)
    return dx, dy * xv * r


def _rmsnorm_bwd(dys, xin, g, resid, out_dtype, name):
    S, D = xin.shape
    T = _tile(S, ROW_T)
    has_resid = resid is not None
    ndy = len(dys)

    def body(*refs):
        dy_refs, (x_ref, g_ref) = refs[:ndy], refs[ndy:ndy + 2]
        dx_ref, dg_ref = refs[-2:]
        dy = dy_refs[0][...]
        for r in dy_refs[1:]:
            dy = dy + r[...]
        dx, dgt = _rmsnorm_bwd_math(dy, x_ref[...], g_ref[...])
        if has_resid:
            dx = dx + refs[ndy + 2][...]
        dx_ref[...] = dx.astype(dx_ref.dtype)

        @pl.when(pl.program_id(0) == 0)
        def _():
            dg_ref[...] = jnp.zeros_like(dg_ref)

        dg_ref[...] += jnp.sum(dgt, axis=0, keepdims=True)

    row = pl.BlockSpec((T, D), lambda i: (i, 0))
    vec = pl.BlockSpec((1, D), lambda i: (0, 0))
    ins = list(dys) + [xin, g] + ([resid] if has_resid else [])
    return pl.pallas_call(
        body, name=name,
        out_shape=(jax.ShapeDtypeStruct((S, D), out_dtype), jax.ShapeDtypeStruct((1, D), F32)),
        grid=(S // T,), in_specs=[row] * ndy + [row, vec] + ([row] if has_resid else []),
        out_specs=(row, vec), compiler_params=_cparams(("arbitrary",)),
    )(*ins)


def _loss_head(x1, d, g, target, name):
    S, D = x1.shape
    T = _tile(S, ROW_T)

    def body(x_ref, d_ref, g_ref, t_ref, loss_ref, dy_ref, dd_ref, dg_ref):
        dv, gv = d_ref[...], g_ref[...]
        y = x_ref[...] + dv * _rstd(dv) * gv
        diff = y - t_ref[...]
        dy = diff * (1.0 / D)
        dy_ref[...] = dy
        dd, dgt = _rmsnorm_bwd_math(dy, dv, gv)
        dd_ref[...] = dd.astype(dd_ref.dtype)

        @pl.when(pl.program_id(0) == 0)
        def _():
            dg_ref[...] = jnp.zeros_like(dg_ref)
            loss_ref[...] = jnp.zeros_like(loss_ref)

        dg_ref[...] += jnp.sum(dgt, axis=0, keepdims=True)
        part = jnp.sum(jnp.sum(diff * diff, axis=1, keepdims=True), axis=0, keepdims=True)
        loss_ref[...] += (0.5 / D) * part

    row = pl.BlockSpec((T, D), lambda i: (i, 0))
    vec = pl.BlockSpec((1, D), lambda i: (0, 0))
    return pl.pallas_call(
        body, name=name,
        out_shape=(jax.ShapeDtypeStruct((1, LANES), F32), jax.ShapeDtypeStruct((S, D), F32),
                   jax.ShapeDtypeStruct((S, D), BF16), jax.ShapeDtypeStruct((1, D), F32)),
        grid=(S // T,), in_specs=[row, row, vec, row],
        out_specs=(pl.BlockSpec((1, LANES), lambda i: (0, 0)), row, row, vec),
        compiler_params=_cparams(("arbitrary",)),
    )(x1, d, g, target)


def _merge_fwd(ya, yb, pm, ba, bb, name):
    S, D = ya.shape
    T = _tile(S, ROW_T)

    def body(ya_ref, yb_ref, ga_ref, gb_ref, ba_ref, bb_ref, o_ref):
        sa = jax.nn.sigmoid(ga_ref[...] + ba_ref[...])
        sb = jax.nn.sigmoid(gb_ref[...] + bb_ref[...])
        o_ref[...] = (sa * ya_ref[...] + sb * yb_ref[...]).astype(o_ref.dtype)

    row = pl.BlockSpec((T, D), lambda i: (i, 0))
    vec = pl.BlockSpec((1, D), lambda i: (0, 0))
    return pl.pallas_call(
        body, name=name, out_shape=jax.ShapeDtypeStruct((S, D), BF16), grid=(S // T,),
        in_specs=[row, row, pl.BlockSpec((T, D), lambda i: (i, 0)),
                  pl.BlockSpec((T, D), lambda i: (i, 1)), vec, vec],
        out_specs=row, compiler_params=_cparams(("parallel",)),
    )(ya, yb, pm, pm, ba, bb)


def _merge_bwd(dmerged, ya, yb, pm, ba, bb, name):
    S, D = ya.shape
    T = _tile(S, ROW_T)

    def body(dm_ref, ya_ref, yb_ref, ga_ref, gb_ref, ba_ref, bb_ref,
             dya_ref, dyb_ref, dga_ref, dgb_ref, dba_ref, dbb_ref):
        dm = dm_ref[...]
        sa = jax.nn.sigmoid(ga_ref[...] + ba_ref[...])
        sb = jax.nn.sigmoid(gb_ref[...] + bb_ref[...])
        dya_ref[...] = (dm * sa).astype(dya_ref.dtype)
        dyb_ref[...] = (dm * sb).astype(dyb_ref.dtype)
        dga = dm * ya_ref[...] * sa * (1.0 - sa)
        dgb = dm * yb_ref[...] * sb * (1.0 - sb)
        dga_ref[...] = dga.astype(dga_ref.dtype)
        dgb_ref[...] = dgb.astype(dgb_ref.dtype)

        @pl.when(pl.program_id(0) == 0)
        def _():
            dba_ref[...] = jnp.zeros_like(dba_ref)
            dbb_ref[...] = jnp.zeros_like(dbb_ref)

        dba_ref[...] += jnp.sum(dga, axis=0, keepdims=True)
        dbb_ref[...] += jnp.sum(dgb, axis=0, keepdims=True)

    row = pl.BlockSpec((T, D), lambda i: (i, 0))
    vec = pl.BlockSpec((1, D), lambda i: (0, 0))
    act = jax.ShapeDtypeStruct((S, D), BF16)
    v1 = jax.ShapeDtypeStruct((1, D), F32)
    return pl.pallas_call(
        body, name=name, out_shape=(act, act, act, act, v1, v1), grid=(S // T,),
        in_specs=[row, row, row, pl.BlockSpec((T, D), lambda i: (i, 0)),
                  pl.BlockSpec((T, D), lambda i: (i, 1)), vec, vec],
        out_specs=(row, row, row, row, vec, vec), compiler_params=_cparams(("arbitrary",)),
    )(dmerged, ya, yb, pm, pm, ba, bb)


_GELU_C = math.sqrt(2.0 / math.pi)


def _gelu(g):
    t = jnp.tanh(_GELU_C * (g + 0.044715 * g * g * g))
    return 0.5 * g * (1.0 + t), t


def _gelu_grad(g, t):
    return 0.5 * (1.0 + t) + 0.5 * g * (1.0 - t * t) * _GELU_C * (1.0 + 3 * 0.044715 * g * g)


def _shift_down(v, halo_ref, first, rows):
    T = v.shape[0]
    keep = jnp.where(first, 0.0, 1.0)
    h7 = halo_ref[7:8, :] * keep
    h6 = halo_ref[6:7, :] * keep
    m1 = jnp.where(rows == 0, h7, pltpu.roll(v, 1, 0))
    m2 = jnp.where(rows == 0, h6, jnp.where(rows == 1, h7, pltpu.roll(v, 2, 0)))
    return m1, m2


def _conv_act_fwd(up, cw, cb, name):
    S, F2 = up.shape
    Fh = F2 // 2
    T = _tile(S, ROW_T)
    tc = _tile(Fh, CONV_TC)
    ncol = Fh // tc
    hb = T // 8

    def body(ua_ref, ug_ref, ha_ref, hg_ref, wa_ref, wg_ref, ba_ref, bg_ref, o_ref, a_ref, g_ref):
        first = pl.program_id(0) == 0
        rows = lax.broadcasted_iota(jnp.int32, (T, tc), 0)

        def conv(u_ref, h_ref, w_ref, b_ref):
            v = u_ref[...]
            m1, m2 = _shift_down(v, h_ref, first, rows)
            return b_ref[...] + w_ref[0:1, :] * m2 + w_ref[1:2, :] * m1 + w_ref[2:3, :] * v

        a = conv(ua_ref, ha_ref, wa_ref, ba_ref)
        g = conv(ug_ref, hg_ref, wg_ref, bg_ref)
        a_ref[...] = a
        g_ref[...] = g
        o_ref[...] = (_gelu(g)[0] * a).astype(o_ref.dtype)

    halo = lambda off: pl.BlockSpec((8, tc), lambda i, j: (jnp.maximum(i * hb - 1, 0), j + off))
    blk = pl.BlockSpec((T, tc), lambda i, j: (i, j))
    f32 = jax.ShapeDtypeStruct((S, Fh), F32)
    return pl.pallas_call(
        body, name=name, out_shape=(jax.ShapeDtypeStruct((S, Fh), BF16), f32, f32), grid=(S // T, ncol),
        in_specs=[blk, pl.BlockSpec((T, tc), lambda i, j: (i, j + ncol)),
                  halo(0), halo(ncol),
                  pl.BlockSpec((3, tc), lambda i, j: (0, j)), pl.BlockSpec((3, tc), lambda i, j: (0, j + ncol)),
                  pl.BlockSpec((1, tc), lambda i, j: (0, j)), pl.BlockSpec((1, tc), lambda i, j: (0, j + ncol))],
        out_specs=(blk, blk, blk),
        compiler_params=_cparams(("parallel", "parallel")),
    )(up, up, up, up, cw, cw, cb, cb)


def _conv_act_bwd(up, a, g, dact, cw, name):
    S, F2 = up.shape
    Fh = F2 // 2
    T = _tile(S, ROW_T)
    tc = _tile(Fh, CONV_TC)
    ncol, nrow, hb, nhb = Fh // tc, S // T, T // 8, S // 8

    def body(ua_ref, ug_ref, a_ref, g_ref, an_ref, gn_ref, wa_ref, wg_ref, da_ref, dn_ref,
             dpa_ref, dpg_ref, dwa_ref, dwg_ref, dba_ref, dbg_ref, dua_n, dug_n):
        i = pl.program_id(1)
        rows = lax.broadcasted_iota(jnp.int32, (T, tc), 0)

        def du_of(a, g, dact_v):
            gel, t = _gelu(g)
            return dact_v * gel, dact_v * a * _gelu_grad(g, t)

        dua, dug = du_of(a_ref[...], g_ref[...], da_ref[...])
        keep = jnp.where(i == nrow - 1, 0.0, 1.0)
        dua_n[...], dug_n[...] = du_of(an_ref[...], gn_ref[...], dn_ref[...] * keep)

        @pl.when(i == 0)
        def _():
            for r in (dwa_ref, dwg_ref, dba_ref, dbg_ref):
                r[...] = jnp.zeros_like(r)

        for du, n_ref, u_ref, w_ref, o_ref, dw_ref, db_ref in ((dua, dua_n, ua_ref, wa_ref, dpa_ref, dwa_ref, dba_ref),
                                                               (dug, dug_n, ug_ref, wg_ref, dpg_ref, dwg_ref, dbg_ref)):
            n0, n1 = n_ref[0:1, :], n_ref[1:2, :]
            du1 = jnp.where(rows == T - 1, n0, pltpu.roll(du, T - 1, 0))
            du2 = jnp.where(rows == T - 2, n0, jnp.where(rows == T - 1, n1, pltpu.roll(du, T - 2, 0)))
            o_ref[...] = (w_ref[2:3, :] * du + w_ref[1:2, :] * du1 + w_ref[0:1, :] * du2).astype(o_ref.dtype)
            u = u_ref[...]
            db_ref[...] += jnp.sum(du, axis=0, keepdims=True)
            for j, d in enumerate((du2, du1, du)):
                dw_ref[j:j + 1, :] += jnp.sum(d * u, axis=0, keepdims=True)

    tile = lambda off: pl.BlockSpec((T, tc), lambda j, i: (i, j + off))
    under = pl.BlockSpec((8, tc), lambda j, i: (jnp.minimum((i + 1) * hb, nhb - 1), j))
    vec = lambda n, off: pl.BlockSpec((n, tc), lambda j, i: (0, j + off))
    act = jax.ShapeDtypeStruct((S, Fh), BF16)
    return pl.pallas_call(
        body, name=name,
        out_shape=(act, act, jax.ShapeDtypeStruct((3, Fh), F32), jax.ShapeDtypeStruct((3, Fh), F32),
                   jax.ShapeDtypeStruct((1, Fh), F32), jax.ShapeDtypeStruct((1, Fh), F32)),
        grid=(ncol, nrow),
        in_specs=[tile(0), tile(ncol), tile(0), tile(0), under, under, vec(3, 0), vec(3, ncol), tile(0), under],
        out_specs=(tile(0), tile(0), vec(3, 0), vec(3, 0), vec(1, 0), vec(1, 0)),
        scratch_shapes=[pltpu.VMEM((8, tc), F32), pltpu.VMEM((8, tc), F32)],
        compiler_params=_cparams(("parallel", "arbitrary")),
    )(up, up, a, g, a, g, cw, cw, dact, dact)


def _split3(x):
    hi = x.astype(BF16)
    r1 = x - hi.astype(F32)
    mid = r1.astype(BF16)
    lo = (r1 - mid.astype(F32)).astype(BF16)
    return hi, mid, lo


def _tri_dot(tri, x):
    hi, mid, lo = _split3(x)
    return _dot_nn(tri, hi) + _dot_nn(tri, mid) + _dot_nn(tri, lo)


def _log_sigmoid(x):
    return jnp.minimum(x, 0.0) - jnp.log(1.0 + jnp.exp(-jnp.abs(x)))


def _tri_mask(n, lower):
    r = lax.broadcasted_iota(jnp.int32, (n, n), 0)
    c = lax.broadcasted_iota(jnp.int32, (n, n), 1)
    return (r >= c) if lower else (r <= c)


def _gates_fwd(ps, bi, bf, bff, name):
    S = ps.shape[0]
    NC = S // MLC

    def body(ps_ref, bi_ref, bf_ref, bff_ref, a_ref, A_ref, wi_ref, em_ref, wk_ref, dec_ref, F_ref, m_scr, f_scr):
        @pl.when(pl.program_id(0) == 0)
        def _():
            m_scr[...] = jnp.zeros_like(m_scr)
            f_scr[...] = jnp.zeros_like(f_scr)

        rows = lax.broadcasted_iota(jnp.int32, (MLC, LANES), 0)
        ltri = _tri_mask(MLC, True).astype(BF16)
        li = GATE_CAP * jnp.tanh((ps_ref[:, 0:LANES] + bi_ref[...]) / GATE_CAP)
        lf = _log_sigmoid(GATE_CAP * jnp.tanh((ps_ref[:, LANES:2 * LANES] + bf_ref[...]) / GATE_CAP))
        b = _tri_dot(ltri, lf)
        a = li - b
        cm = a
        sh = 1
        while sh < MLC:
            cm = jnp.where(rows >= sh, jnp.maximum(cm, pltpu.roll(cm, sh, 0)), cm)
            sh *= 2
        m0 = m_scr[...]
        A = jnp.maximum(cm, m0)
        a_ref[...] = a
        A_ref[...] = A
        A_last = A_ref[MLC - 1:MLC, :]
        wi_ref[...] = jnp.exp(m0 - A)
        em_ref[...] = jnp.exp(-(b + A))
        wk_ref[...] = jnp.exp(a - A_last)
        dec_ref[0] = jnp.exp(m0 - A_last)
        F_ref[...] = b
        m_scr[...] = F_ref[MLC - 1:MLC, :] + A_last
        lfg = _log_sigmoid(ps_ref[:, 2 * LANES:3 * LANES] + bff_ref[...])
        F_ref[...] = _tri_dot(ltri, lfg) + f_scr[...]
        f_scr[...] = F_ref[MLC - 1:MLC, :]

    col = pl.BlockSpec((MLC, LANES), lambda c: (c, 0))
    vec = pl.BlockSpec((1, LANES), lambda c: (0, 0))
    cs = jax.ShapeDtypeStruct((S, LANES), F32)
    return pl.pallas_call(
        body, name=name,
        out_shape=(cs, cs, cs, cs, cs, jax.ShapeDtypeStruct((NC, 1, LANES), F32), cs),
        grid=(NC,), in_specs=[pl.BlockSpec((MLC, N_SMALL), lambda c: (c, 0)), vec, vec, vec],
        out_specs=(col, col, col, col, col, pl.BlockSpec((1, 1, LANES), lambda c: (c, 0, 0)), col),
        scratch_shapes=[pltpu.VMEM((1, LANES), F32), pltpu.VMEM((1, LANES), F32)],
        compiler_params=_cparams(("arbitrary",)),
    )(ps, bi, bf, bff)


def _gates_bwd(ps, bi, bf, bff, rk, kc, tch, dF, name):
    S = ps.shape[0]
    NC = S // MLC

    def body(ps_ref, bi_ref, bf_ref, bff_ref, rk_ref, kc_ref, t_ref, dF_ref, dps_ref, db_ref, carry):
        @pl.when(pl.program_id(0) == 0)
        def _():
            carry[...] = jnp.zeros_like(carry)
            db_ref[...] = jnp.zeros_like(db_ref)

        lanes = lax.broadcasted_iota(jnp.int32, (MLC, LANES), 1)
        utri = _tri_mask(MLC, False).astype(BF16)
        ti = jnp.tanh((ps_ref[:, 0:LANES] + bi_ref[...]) / GATE_CAP)
        t_end, t_start = t_ref[0, 0:1, :], t_ref[0, 1:2, :]
        rk = rk_ref[...]
        rk = rk - (jnp.sum(rk, axis=0, keepdims=True) - (t_start - t_end)) * (1.0 / MLC)
        dpi = jnp.where(lanes < ML_HEADS, (kc_ref[...] - rk) * (1.0 - ti * ti), 0.0)
        tf = jnp.tanh((ps_ref[:, LANES:2 * LANES] + bf_ref[...]) / GATE_CAP)
        dlf = _tri_dot(utri, rk) + t_end
        dpf = jnp.where(lanes < ML_HEADS, dlf * jax.nn.sigmoid(-GATE_CAP * tf) * (1.0 - tf * tf), 0.0)
        dFv = dF_ref[...]
        dlfg = _tri_dot(utri, dFv) + carry[...]
        carry[...] += jnp.sum(dFv, axis=0, keepdims=True)
        dpff = jnp.where(lanes < FOX_HEADS, dlfg * jax.nn.sigmoid(-(ps_ref[:, 2 * LANES:3 * LANES] + bff_ref[...])), 0.0)
        for n, dp in enumerate((dpi, dpf, dpff)):
            dps_ref[:, n * LANES:(n + 1) * LANES] = dp.astype(dps_ref.dtype)
            db_ref[:, n * LANES:(n + 1) * LANES] += jnp.sum(dp, axis=0, keepdims=True)

    rev = lambda c: (NC - 1 - c, 0)
    col = pl.BlockSpec((MLC, LANES), rev)
    vec = pl.BlockSpec((1, LANES), lambda c: (0, 0))
    wide = pl.BlockSpec((MLC, N_SMALL), rev)
    return pl.pallas_call(
        body, name=name,
        out_shape=(jax.ShapeDtypeStruct((S, N_SMALL), BF16), jax.ShapeDtypeStruct((1, N_SMALL), F32)),
        grid=(NC,),
        in_specs=[wide, vec, vec, vec, col, col, pl.BlockSpec((1, 2, LANES), lambda c: (NC - 1 - c, 0, 0)), col],
        out_specs=(wide, pl.BlockSpec((1, N_SMALL), lambda c: (0, 0))),
        scratch_shapes=[pltpu.VMEM((1, LANES), F32)],
        compiler_params=_cparams(("arbitrary",)),
    )(ps, bi, bf, bff, rk, kc, tch, dF)


_ML_SCALE = ML_DQK ** -0.5


def _ml_specs(rev, NC):
    idx = (lambda c: NC - 1 - c) if rev else (lambda c: c)
    qk = lambda blk: pl.BlockSpec((MLC, ML_HEADS * ML_DQK), lambda c: (idx(c), blk))
    wide = lambda blk: pl.BlockSpec((MLC, D_MODEL), lambda c: (idx(c), blk))
    col = pl.BlockSpec((MLC, LANES), lambda c: (idx(c), 0))
    return idx, qk, wide, col


def _ml_intra(q_ref, k_ref, arow_ref, A_ref, h):
    hs = slice(h * ML_DQK, (h + 1) * ML_DQK)
    qf = q_ref[:, hs] * _ML_SCALE
    kf = k_ref[:, hs]
    qb, kb = qf.astype(BF16), kf.astype(BF16)
    qk = _dot_nt(qb, kb)
    logw = arow_ref[h:h + 1, :] - A_ref[:, h:h + 1]
    W = jnp.exp(jnp.where(_tri_mask(MLC, True), logw, -1e30))
    return qb, kb, qf, kf, qk, W


def _mlstm_fwd(pm, a_row, A, wi, em, wk, dec, w_hn, name):
    S = pm.shape[0]
    NC = S // MLC
    _, qk, wide, col = _ml_specs(False, NC)

    def body(q_ref, k_ref, v_ref, o_ref, arow_ref, A_ref, wi_ref, em_ref, wk_ref, dec_ref, whn_ref,
             ha_ref, hp_ref, den_ref, cst_ref, nst_ref, C_scr, n_scr):
        @pl.when(pl.program_id(0) == 0)
        def _():
            C_scr[...] = jnp.zeros_like(C_scr)
            n_scr[...] = jnp.zeros_like(n_scr)

        lanes = lax.broadcasted_iota(jnp.int32, (MLC, LANES), 1)
        den_tile = jnp.zeros((MLC, LANES), F32)
        for h in range(ML_HEADS):
            vs = slice(h * ML_DV, (h + 1) * ML_DV)
            qb, kb, qf, kf, qk_, W = _ml_intra(q_ref, k_ref, arow_ref, A_ref, h)
            vb = v_ref[:, vs].astype(BF16)
            Cf = C_scr[h]
            Cb = Cf.astype(BF16)
            nrow = n_scr[h]
            cst_ref[0, h] = Cb
            nst_ref[0, h] = nrow
            s = qk_ * W
            wic = wi_ref[:, h:h + 1]
            num = _dot_nn(s.astype(BF16), vb) + wic * _dot_nt(qb, Cb)
            den = jnp.sum(s, axis=1, keepdims=True) + wic * jnp.sum(qf * nrow, axis=1, keepdims=True)
            hp = num / jnp.maximum(jnp.abs(den), em_ref[:, h:h + 1])
            hp_ref[:, vs] = hp
            den_tile = jnp.where(lanes == h, den, den_tile)
            hn = hp * _rstd(hp) * whn_ref[:, vs]
            ha_ref[:, vs] = (hn * jax.nn.sigmoid(o_ref[:, vs])).astype(ha_ref.dtype)
            wkc = wk_ref[:, h:h + 1]
            kw = kf * wkc
            d = dec_ref[0, :, h:h + 1]
            C_scr[h] = d * Cf + _dot_tn(vb, kw.astype(BF16))
            n_scr[h] = d * nrow + jnp.sum(kw, axis=0, keepdims=True)
        den_ref[...] = den_tile

    return pl.pallas_call(
        body, name=name,
        out_shape=(jax.ShapeDtypeStruct((S, D_MODEL), BF16), jax.ShapeDtypeStruct((S, D_MODEL), F32),
                   jax.ShapeDtypeStruct((S, LANES), F32),
                   jax.ShapeDtypeStruct((NC, ML_HEADS, ML_DV, ML_DQK), BF16),
                   jax.ShapeDtypeStruct((NC, ML_HEADS, 1, ML_DQK), F32)),
        grid=(NC,),
        in_specs=[qk(C_QM // 512), qk(C_KM // 512), wide(C_VM // D_MODEL), wide(C_OM // D_MODEL),
                  pl.BlockSpec((8, MLC), lambda c: (0, c)), col, col, col, col,
                  pl.BlockSpec((1, 1, LANES), lambda c: (c, 0, 0)), pl.BlockSpec((1, D_MODEL), lambda c: (0, 0))],
        out_specs=(pl.BlockSpec((MLC, D_MODEL), lambda c: (c, 0)), pl.BlockSpec((MLC, D_MODEL), lambda c: (c, 0)),
                   col, pl.BlockSpec((1, ML_HEADS, ML_DV, ML_DQK), lambda c: (c, 0, 0, 0)),
                   pl.BlockSpec((1, ML_HEADS, 1, ML_DQK), lambda c: (c, 0, 0, 0))),
        scratch_shapes=[pltpu.VMEM((ML_HEADS, ML_DV, ML_DQK), F32), pltpu.VMEM((ML_HEADS, 1, ML_DQK), F32)],
        compiler_params=_cparams(("arbitrary",)),
    )(pm, pm, pm, pm, a_row, A, wi, em, wk, dec, w_hn)


def _mlstm_bwd(dha, pm, hp_all, den_all, a_row, A, wi, em, wk, dec, cst, nst, w_hn, name):
    S = pm.shape[0]
    NC = S // MLC
    idx, qk, wide, col = _ml_specs(True, NC)

    def body(dha_ref, q_ref, k_ref, v_ref, o_ref, hp_ref, den_ref, arow_ref, A_ref, wi_ref, em_ref, wk_ref,
             dec_ref, cst_ref, nst_ref, whn_ref,
             dqk_ref, dv_ref, do_ref, rk_ref, kc_ref, t_ref, dwhn_ref, dC_scr, dn_scr, t_scr):
        @pl.when(pl.program_id(0) == 0)
        def _():
            dC_scr[...] = jnp.zeros_like(dC_scr)
            dn_scr[...] = jnp.zeros_like(dn_scr)
            t_scr[...] = jnp.zeros_like(t_scr)
            dwhn_ref[...] = jnp.zeros_like(dwhn_ref)

        lanes = lax.broadcasted_iota(jnp.int32, (MLC, LANES), 1)
        lane1 = lax.broadcasted_iota(jnp.int32, (1, LANES), 1)
        t_ref[0, 0:1, :] = t_scr[...]
        rk_tile = jnp.zeros((MLC, LANES), F32)
        kc_tile = jnp.zeros((MLC, LANES), F32)
        t_new = jnp.zeros((1, LANES), F32)
        for h in range(ML_HEADS):
            hs = slice(h * ML_DQK, (h + 1) * ML_DQK)
            vs = slice(h * ML_DV, (h + 1) * ML_DV)
            hp = hp_ref[:, vs]
            sig = jax.nn.sigmoid(o_ref[:, vs])
            whn = whn_ref[:, vs]
            r = _rstd(hp)
            dga = dha_ref[:, vs]
            do_ref[:, vs] = (dga * (hp * r * whn) * sig * (1.0 - sig)).astype(do_ref.dtype)
            dhn = dga * sig
            dhp, dwt = _rmsnorm_bwd_math(dhn, hp, whn)
            dwhn_ref[:, vs] += jnp.sum(dwt, axis=0, keepdims=True)
            den = den_ref[:, h:h + 1]
            floor = em_ref[:, h:h + 1]
            D = jnp.maximum(jnp.abs(den), floor)
            dnum = dhp / D
            dh_h = jnp.sum(dhp * hp, axis=1, keepdims=True)
            active = jnp.abs(den) >= floor
            dden = -dh_h / D * jnp.where(active, jnp.sign(den), 0.0)
            phi = jnp.where(active, 0.0, dh_h)
            qb, kb, qf, kf, qk_, W = _ml_intra(q_ref, k_ref, arow_ref, A_ref, h)
            vf = v_ref[:, vs]
            vb = vf.astype(BF16)
            Cb = cst_ref[0, h]
            nrow = nst_ref[0, h]
            wic = wi_ref[:, h:h + 1]
            wkc = wk_ref[:, h:h + 1]
            d = dec_ref[0, :, h:h + 1]
            dCn = dC_scr[h]
            dCb = dCn.astype(BF16)
            dnn = dn_scr[h]
            dnumb = dnum.astype(BF16)
            s = qk_ * W
            ds = (_dot_nt(dnumb, vb) + dden) * W
            dsb = ds.astype(BF16)
            dnw = (wic * dnum).astype(BF16)
            wd = wic * dden
            kw = kf * wkc
            dv_state = _dot_nt(kw.astype(BF16), dCb)
            dq = _dot_nn(dsb, kb) + _dot_nn(dnw, Cb) + wd * nrow
            dk_state = wkc * (_dot_nn(vb, dCb) + dnn)
            dk = _dot_tn(dsb, qb) + dk_state
            dv = _dot_tn(s.astype(BF16), dnumb) + dv_state
            dC = d * dCn + _dot_tn(dnw, qb)
            dn = d * dnn + jnp.sum(wd * qf, axis=0, keepdims=True)
            dC_scr[h] = dC
            dn_scr[h] = dn
            dqk_ref[:, hs] = (dq * _ML_SCALE).astype(dqk_ref.dtype)
            dqk_ref[:, C_KM + h * ML_DQK:C_KM + (h + 1) * ML_DQK] = dk.astype(dqk_ref.dtype)
            dv_ref[:, vs] = dv.astype(dv_ref.dtype)
            G = ds * qk_
            inter = _dot_nt(qb, Cb)
            qn = jnp.sum(qf * nrow, axis=1, keepdims=True)
            R = (jnp.sum(G, axis=1, keepdims=True)
                 + wic * (jnp.sum(dnum * inter, axis=1, keepdims=True) + dden * qn))
            K = jnp.sum(G.T, axis=1, keepdims=True) + jnp.sum(kf * dk_state, axis=1, keepdims=True)
            rk_tile = jnp.where(lanes == h, R - K, rk_tile)
            kc_tile = jnp.where(lanes == h, phi, kc_tile)
            tt = (jnp.sum(jnp.sum(dC * Cb.astype(F32), axis=1, keepdims=True), axis=0, keepdims=True)
                  + jnp.sum(dn * nrow, axis=1, keepdims=True))
            t_new = jnp.where(lane1 == h, tt, t_new)
        rk_ref[...] = rk_tile
        kc_ref[...] = kc_tile
        t_ref[0, 1:2, :] = t_new
        t_scr[...] = t_new

    act = lambda n: jax.ShapeDtypeStruct((S, n), BF16)
    cs = jax.ShapeDtypeStruct((S, LANES), F32)
    rowblk = lambda n: pl.BlockSpec((MLC, n), lambda c: (idx(c), 0))
    return pl.pallas_call(
        body, name=name,
        out_shape=(act(D_MODEL), act(D_MODEL), act(D_MODEL), cs, cs,
                   jax.ShapeDtypeStruct((NC, 2, LANES), F32), jax.ShapeDtypeStruct((1, D_MODEL), F32)),
        grid=(NC,),
        in_specs=[rowblk(D_MODEL), qk(C_QM // 512), qk(C_KM // 512), wide(C_VM // D_MODEL), wide(C_OM // D_MODEL),
                  rowblk(D_MODEL), col, pl.BlockSpec((8, MLC), lambda c: (0, idx(c))), col, col, col, col,
                  pl.BlockSpec((1, 1, LANES), lambda c: (idx(c), 0, 0)),
                  pl.BlockSpec((1, ML_HEADS, ML_DV, ML_DQK), lambda c: (idx(c), 0, 0, 0)),
                  pl.BlockSpec((1, ML_HEADS, 1, ML_DQK), lambda c: (idx(c), 0, 0, 0)),
                  pl.BlockSpec((1, D_MODEL), lambda c: (0, 0))],
        out_specs=(rowblk(D_MODEL), rowblk(D_MODEL), rowblk(D_MODEL), col, col,
                   pl.BlockSpec((1, 2, LANES), lambda c: (idx(c), 0, 0)), pl.BlockSpec((1, D_MODEL), lambda c: (0, 0))),
        scratch_shapes=[pltpu.VMEM((ML_HEADS, ML_DV, ML_DQK), F32), pltpu.VMEM((ML_HEADS, 1, ML_DQK), F32),
                        pltpu.VMEM((1, LANES), F32)],
        compiler_params=_cparams(("arbitrary",)),
    )(dha, pm, pm, pm, pm, hp_all, den_all, a_row, A, wi, em, wk, dec, cst, nst, w_hn)


_FOX_SCALE = FOX_DH ** -0.5
_NEG = -1e30
_LOG2E = 1.4426950408889634
_LN2 = 0.6931471805599453
_QF_BLK, _KF_BLK, _VF_BLK = 0, FOX_HEADS, 2 * FOX_HEADS


def _lane_pick(tile, lane):
    lanes = lax.broadcasted_iota(jnp.int32, tile.shape, 1)
    return jnp.sum(jnp.where(lanes == lane, tile, 0.0), axis=1, keepdims=True)


def _col_to_row(col):
    return jnp.max(jnp.broadcast_to(col, (col.shape[0], LANES)).T, axis=0, keepdims=True)


def _causal(q0, k0, shape, q_axis):
    qpos = q0 + lax.broadcasted_iota(jnp.int32, shape, q_axis)
    kpos = k0 + lax.broadcasted_iota(jnp.int32, shape, 1 - q_axis)
    return kpos <= qpos


def _fox_fwd(pf, fc, fk_row, name):
    S = pf.shape[0]
    TQ, TK = FOX_TQ_FWD, FOX_TK_FWD
    nq, nk = S // TQ, S // TK
    c1 = _FOX_SCALE * _LOG2E

    def body(q_ref, k_ref, v_ref, fc_ref, fr_ref, o_ref, lse_ref):
        h, i = pl.program_id(0), pl.program_id(1)
        qb = q_ref[...]
        fq2 = _lane_pick(fc_ref[...], h) * _LOG2E

        def step(j, carry, masked):
            m, l, acc = carry
            off = pl.multiple_of(j * TK, TK)
            t = _dot_nt(qb, k_ref[pl.ds(off, TK), :]) * c1 - fr_ref[0, j] * _LOG2E
            if masked:
                t = jnp.where(_causal(i * TQ, j * TK, (TQ, TK), 0), t, _NEG)
            m_new = jnp.maximum(m, jnp.max(t, axis=1, keepdims=True) + fq2)
            alpha = jnp.exp2(m - m_new)
            p = jnp.exp2(t + (fq2 - m_new))
            l = alpha * l + jnp.sum(p, axis=1, keepdims=True)
            acc = alpha * acc + _dot_nn(p.astype(BF16), v_ref[pl.ds(off, TK), :])
            return m_new, l, acc

        init = (jnp.full((TQ, 1), _NEG, F32), jnp.zeros((TQ, 1), F32), jnp.zeros((TQ, FOX_DH), F32))
        last = (i * TQ) // TK
        carry = lax.fori_loop(0, last, lambda j, c: step(j, c, False), init)
        for d in range(TQ // TK):
            carry = step(last + d, carry, True)
        m, l, acc = carry
        o_ref[...] = (acc / l).astype(o_ref.dtype)
        lse_ref[0, 0] = _col_to_row((m + jnp.log2(l)) * _LN2)

    head = lambda blk: pl.BlockSpec((S, FOX_DH), lambda h, i: (0, blk + h))
    return pl.pallas_call(
        body, name=name,
        out_shape=(jax.ShapeDtypeStruct((S, D_MODEL), BF16), jax.ShapeDtypeStruct((FOX_HEADS, nq, 1, TQ), F32)),
        grid=(FOX_HEADS, nq),
        in_specs=[pl.BlockSpec((TQ, FOX_DH), lambda h, i: (i, _QF_BLK + h)), head(_KF_BLK), head(_VF_BLK),
                  pl.BlockSpec((TQ, LANES), lambda h, i: (i, 0)),
                  pl.BlockSpec((1, nk, 1, TK), lambda h, i: (h, 0, 0, 0))],
        out_specs=(pl.BlockSpec((TQ, FOX_DH), lambda h, i: (i, h)),
                   pl.BlockSpec((1, 1, 1, TQ), lambda h, i: (h, i, 0, 0))),
        compiler_params=_cparams(("parallel", "arbitrary")),
    )(pf, pf, pf, fc, fk_row)


def _fox_bwd(dhb, hb, pf, lse_row, fq_row, fc, name):
    S = pf.shape[0]
    TQ, TK = FOX_TQ, FOX_TK
    nq, nk, r = S // TQ, S // TK, TK // TQ
    c1 = _FOX_SCALE * _LOG2E

    def body(q_ref, k_ref, v_ref, do_ref, o_ref, lse_ref, fq_ref, fc_ref,
             dq_ref, dk_ref, dv_ref, dFk_ref, dFq_ref, dq_acc, qside, delta, dk_acc, dv_acc, cs_acc):
        h, j = pl.program_id(0), pl.program_id(1)

        @pl.when(j == 0)
        def _():
            dq_acc[...] = jnp.zeros_like(dq_acc)
            dFq_ref[...] = jnp.zeros_like(dFq_ref)

            def fill(b, _):
                off = pl.multiple_of(b * TQ, TQ)
                prod = do_ref[pl.ds(off, TQ), :].astype(F32) * o_ref[pl.ds(off, TQ), :].astype(F32)
                delta[b] = jnp.sum(prod.T, axis=0, keepdims=True)
                qside[b] = (fq_ref[0, b] - lse_ref[0, b]) * _LOG2E
                return 0

            lax.fori_loop(0, nq, fill, 0)

        kb = k_ref[...]
        vb = v_ref[...]
        fk2 = _lane_pick(fc_ref[...], h) * _LOG2E
        dk_acc[...] = jnp.zeros_like(dk_acc)
        dv_acc[...] = jnp.zeros_like(dv_acc)
        cs_acc[...] = jnp.zeros_like(cs_acc)

        def step(i, masked):
            off = pl.multiple_of(i * TQ, TQ)
            qb = q_ref[pl.ds(off, TQ), :]
            dob = do_ref[pl.ds(off, TQ), :]
            t = _dot_nt(kb, qb) * c1 + qside[i] - fk2
            if masked:
                t = jnp.where(_causal(i * TQ, j * TK, (TK, TQ), 1), t, _NEG)
            p = jnp.exp2(t)
            dv_acc[...] += _dot_nn(p.astype(BF16), dob)
            ds = p * (_dot_nt(vb, dob) - delta[i])
            dsb = ds.astype(BF16)
            dk_acc[...] += _dot_nn(dsb, qb)
            dq_acc[pl.ds(off, TQ), :] += _dot_tn(dsb, kb)
            cs_acc[...] += jnp.sum(ds, axis=1, keepdims=True)
            dFq_ref[0, i] += jnp.sum(ds, axis=0, keepdims=True)

        for d in range(r):
            step(r * j + d, True)

        def rest(i, _):
            step(i, False)
            return 0

        lax.fori_loop(r * j + r, nq, rest, 0)
        dk_ref[...] = (dk_acc[...] * _FOX_SCALE).astype(dk_ref.dtype)
        dv_ref[...] = dv_acc[...].astype(dv_ref.dtype)
        dFk_ref[0, 0] = -_col_to_row(cs_acc[...])

        @pl.when(j == nk - 1)
        def _():
            dq_ref[...] = (dq_acc[...] * _FOX_SCALE).astype(dq_ref.dtype)

    head = lambda blk: pl.BlockSpec((S, FOX_DH), lambda h, j: (0, blk + h))
    kblk = lambda blk: pl.BlockSpec((TK, FOX_DH), lambda h, j: (j, blk + h))
    qrows = pl.BlockSpec((1, nq, 1, TQ), lambda h, j: (h, 0, 0, 0))
    act = jax.ShapeDtypeStruct((S, D_MODEL), BF16)
    return pl.pallas_call(
        body, name=name,
        out_shape=(act, act, act, jax.ShapeDtypeStruct((FOX_HEADS, nk, 1, TK), F32),
                   jax.ShapeDtypeStruct((FOX_HEADS, nq, 1, TQ), F32)),
        grid=(FOX_HEADS, nk),
        in_specs=[head(_QF_BLK), kblk(_KF_BLK), kblk(_VF_BLK), head(0), head(0), qrows, qrows,
                  pl.BlockSpec((TK, LANES), lambda h, j: (j, 0))],
        out_specs=(head(0), kblk(0), kblk(0), pl.BlockSpec((1, 1, 1, TK), lambda h, j: (h, j, 0, 0)), qrows),
        scratch_shapes=[pltpu.VMEM((S, FOX_DH), F32), pltpu.VMEM((nq, 1, TQ), F32), pltpu.VMEM((nq, 1, TQ), F32),
                        pltpu.VMEM((TK, FOX_DH), F32), pltpu.VMEM((TK, FOX_DH), F32), pltpu.VMEM((TK, 1), F32)],
        compiler_params=_cparams(("parallel", "arbitrary")),
    )(pf, pf, pf, dhb, hb, lse_row, fq_row, fc)


def _pad_lanes(v):
    return jnp.pad(v, ((0, 0), (0, LANES - v.shape[1])))


def _local_step(x, target, wmain_t, wsmall_t, rest_weights, p, on_grads, advance, token):
    S = x.shape[0]
    bi, bf, bff = _pad_lanes(p["b_ml_i"]), _pad_lanes(p["b_ml_f"]), _pad_lanes(p["b_fox_f"])

    h0 = _rmsnorm_fwd(x, p["norm_mix_pre"] + token[0:1, 0:1], "norm_mix_pre")
    pm = _mm(h0, wmain_t[:N_ML], "nt", F32, "proj_mlstm")
    pf = _mm(h0, wmain_t[N_ML:N_ML + N_FOX], "nt", BF16, "proj_fox")
    pg = _mm(h0, wmain_t[N_ML + N_FOX:], "nt", F32, "proj_merge")
    ps = _mm(h0, wsmall_t, "nt", F32, "proj_gates")
    a, A, wi, em, wk, dec, Fc = _gates_fwd(ps, bi, bf, bff, "gates_fwd")
    a_row = a[:, :8].T
    ha, hp, den, cst, nst = _mlstm_fwd(pm, a_row, A, wi, em, wk, dec, p["ml_head_norm"], "mlstm_fwd")
    ft = Fc[:, :FOX_HEADS].T
    fq_row = ft.reshape(FOX_HEADS, S // FOX_TQ, 1, FOX_TQ)
    fk_row = ft.reshape(FOX_HEADS, S // FOX_TK, 1, FOX_TK)
    hb, lse_row = _fox_fwd(pf, Fc, ft.reshape(FOX_HEADS, S // FOX_TK_FWD, 1, FOX_TK_FWD), "fox_fwd")
    wa, wb, wout, wup, wdown = rest_weights(hb)
    ya = _mm(ha, wa, "nn", F32, "branch_a")
    yb = _mm(hb, wb, "nn", F32, "branch_b")
    merged = _merge_fwd(ya, yb, pg, p["b_gate_a"], p["b_gate_b"], "merge_fwd")
    z = _mm(merged, wout, "nn", F32, "out_proj")
    x1 = _resid_norm_fwd(x, z, p["norm_mix_post"], "resid_mix")
    h2 = _rmsnorm_fwd(x1, p["norm_ffn_pre"], "norm_ffn_pre")
    up = _mm(h2, wup, "nn", F32, "ffn_up")
    act, conv_a, conv_g = _conv_act_fwd(up, p["conv_w"], p["conv_b"], "conv_act_fwd")
    d = _mm(act, wdown, "nn", F32, "ffn_down")
    loss_row, dy, dd, g_norm_ffn_post = _loss_head(x1, d, p["norm_ffn_post"], target, "loss_head")
    dact = _mm(dd, wdown, "nt", F32, "d_act")
    g_wdown = _mm(act, dd, "tn", F32, "dw_down", tm=1408)
    dupa, dupg, dcwa, dcwg, dcba, dcbg = _conv_act_bwd(up, conv_a, conv_g, dact, p["conv_w"], "conv_act_bwd")
    g_conv_w = jnp.concatenate([dcwa, dcwg], axis=1)
    g_conv_b = jnp.concatenate([dcba, dcbg], axis=1)
    dh2 = _mm([dupa, dupg], wup, "nt", F32, "d_h2")
    g_wup = _mm(h2, [dupa, dupg], "tn", F32, "dw_up")
    token = on_grads("ffn", dict(w_up=g_wup, w_down=g_wdown))
    dx1, g_norm_ffn_pre = _rmsnorm_bwd([dh2], x1, p["norm_ffn_pre"] + token[0:1, 0:1], dy, F32, "norm_ffn_pre_bwd")
    dz, g_norm_mix_post = _rmsnorm_bwd([dx1], z, p["norm_mix_post"], None, BF16, "norm_mix_post_bwd")
    dmerged = _mm(dz, wout, "nt", F32, "d_merged")
    g_wout = _mm(merged, dz, "tn", F32, "dw_out")
    dya, dyb, dga, dgb, g_b_gate_a, g_b_gate_b = _merge_bwd(dmerged, ya, yb, pg, p["b_gate_a"], p["b_gate_b"], "merge_bwd")
    dha = _mm(dya, wa, "nt", F32, "d_ha")
    g_wa = _mm(ha, dya, "tn", F32, "dw_a")
    dhb = _mm(dyb, wb, "nt", BF16, "d_hb")
    g_wb = _mm(hb, dyb, "tn", F32, "dw_b")
    token = advance("ffn", g_wb) + on_grads("mix", dict(w_out=g_wout, w_branch_a=g_wa, w_branch_b=g_wb))
    dqkm, dvm, dom, rk, kc, tch, g_ml_head_norm = _mlstm_bwd(
        dha, pm, hp, den, a_row, A, wi, em, wk, dec, cst, nst, p["ml_head_norm"] + token[0:1, 0:1], "mlstm_bwd")
    token = advance("mix", dqkm)
    dqf, dkf, dvf, dFk, dFq = _fox_bwd(dhb, hb, pf, lse_row.reshape(fq_row.shape), fq_row + token[0, 0], Fc, "fox_bwd")
    dF = jnp.pad((dFk.reshape(FOX_HEADS, S) + dFq.reshape(FOX_HEADS, S)).T, ((0, 0), (0, LANES - FOX_HEADS)))
    dps, dbias = _gates_bwd(ps, bi, bf, bff, rk, kc, tch, dF, "gates_bwd")
    dpm = [dqkm, dvm, dom, dqf, dkf, dvf, dga, dgb]
    g_wmain_t = _mm(dpm, h0, "tn", F32, "dw_main")
    token = on_grads("in", dict(w_in=g_wmain_t))
    g_wsmall_t = _mm(dps, h0, "tn", F32, "dw_gates")
    dh0s = _mm(dps, wsmall_t + token[0:1, 0:1].astype(BF16), "nn", F32, "d_h0_gates")
    token = advance("in", dh0s)
    dh0 = _mm(dpm, wmain_t, "nn", F32, "d_h0_main", after=token)
    grad_x, g_norm_mix_pre = _rmsnorm_bwd([dh0, dh0s], x, p["norm_mix_pre"], dx1, F32, "norm_mix_pre_bwd")

    big = dict(wsmall_t=g_wsmall_t)
    small = dict(norm_mix_pre=g_norm_mix_pre, ml_head_norm=g_ml_head_norm, b_gate_a=g_b_gate_a, b_gate_b=g_b_gate_b,
                 norm_mix_post=g_norm_mix_post, norm_ffn_pre=g_norm_ffn_pre, norm_ffn_post=g_norm_ffn_post,
                 conv_b=g_conv_b, b_ml_i=dbias[:, 0:ML_HEADS], b_ml_f=dbias[:, LANES:LANES + ML_HEADS],
                 b_fox_f=dbias[:, 2 * LANES:2 * LANES + FOX_HEADS], conv_w=g_conv_w)
    return loss_row, grad_x, big, small


def _row_tile(r, target=256):
    best = None
    for t in range(8, min(r, target) + 1, 8):
        if r % t == 0:
            best = t
    return best if best is not None else r


def _adamw(w, g, m, v, name):
    _, R, C = w.shape
    tr = _row_tile(R)
    tc = C
    if tr == R and R > 256:
        tc = 256

    def body(w_ref, g_ref, m_ref, v_ref, d_ref, mo_ref, vo_ref):
        gv = g_ref[...]
        mn = ADAM_B1 * m_ref[0] + (1.0 - ADAM_B1) * gv
        vn = ADAM_B2 * v_ref[0] + (1.0 - ADAM_B2) * (gv * gv)
        m_hat = mn / (1.0 - ADAM_B1 ** ADAM_STEP)
        v_hat = vn / (1.0 - ADAM_B2 ** ADAM_STEP)
        d_ref[0] = -ADAM_LR * (m_hat / (jnp.sqrt(v_hat) + ADAM_EPS) + ADAM_WD * w_ref[0])
        mo_ref[0] = mn
        vo_ref[0] = vn

    blk = pl.BlockSpec((1, tr, tc), lambda i, j: (0, i, j))
    o = jax.ShapeDtypeStruct((1, R, C), F32)
    return pl.pallas_call(
        body, name=name, out_shape=(o, o, o), grid=(R // tr, C // tc),
        in_specs=[blk, pl.BlockSpec((tr, tc), lambda i, j: (i, j)), blk, blk], out_specs=(blk,) * 3,
        compiler_params=_cparams(("parallel", "parallel")),
    )(w, g, m, v)


ANY = pl.BlockSpec(memory_space=pl.ANY)


def _place():
    x, y, c = lax.axis_index("x"), lax.axis_index("y"), lax.axis_index("c")
    chips = [(1 - x, y), (x, 1 - y), (1 - x, 1 - y)]
    return x, y, c, chips


def _block(ref, kind, k, rows=None):
    if kind == "rows":
        return ref.at[k] if rows is None else ref.at[k, pl.ds(*rows), :]
    cb = ref.shape[1] // 4
    return ref.at[:, pl.ds(k * cb, cb)] if rows is None else ref.at[pl.ds(*rows), pl.ds(k * cb, cb)]


def _gathered_shape(s, kind):
    return (4,) + s.shape if kind == "rows" else (s.shape[0], 4 * s.shape[1])


def _gather_weights(shards, kinds, smalls):
    n, ns = len(shards), len(smalls)

    def body(*refs):
        ins, sm_in = refs[:n], refs[n:n + ns]
        outs, sm_out = refs[n + ns:2 * n + ns], refs[2 * n + ns:2 * (n + ns)]
        send_sems, recv_sems, sm_send, sm_recv, local_sems = refs[2 * (n + ns):]
        x, y, c, chips = _place()
        sibling = (x, y, 1 - c)
        kme = 2 * x + y

        def half(a, k, hc):
            h = ins[a].shape[0] // 2
            return _block(outs[a], kinds[a], k, (hc * h, h))

        def remote(a, slot, src, dst, to):
            return pltpu.make_async_remote_copy(src_ref=src, dst_ref=dst, send_sem=send_sems.at[a * 7 + slot],
                                                recv_sem=recv_sems.at[a * 7 + slot], device_id=to, device_id_type=MESH)

        def sm_copy(b, j, k, to):
            return pltpu.make_async_remote_copy(src_ref=sm_in[b], dst_ref=sm_out[b].at[k], send_sem=sm_send.at[3 * b + j],
                                                recv_sem=sm_recv.at[3 * b + j], device_id=to, device_id_type=MESH)

        local = [pltpu.make_async_copy(sm_in[b], sm_out[b].at[kme], local_sems.at[b]) for b in range(ns)]
        for cp in local:
            cp.start()
        sends = [remote(a, 6, ins[a], _block(outs[a], kinds[a], kme), sibling) for a in range(n)]
        for a in range(n):
            h = ins[a].shape[0] // 2
            for j, chip in enumerate(chips):
                sends.append(remote(a, j, ins[a].at[pl.ds(c * h, h), :], half(a, kme, c), (*chip, c)))
        for b in range(ns):
            for j, chip in enumerate(chips):
                sends.append(sm_copy(b, j, kme, (*chip, c)))
        for cp in sends:
            cp.start()
        for a in range(n):
            for j, chip in enumerate(chips):
                kj = 2 * chip[0] + chip[1]
                remote(a, j, half(a, kj, c), half(a, kj, c), (*chip, c)).wait_recv()
                fwd = remote(a, 3 + j, half(a, kj, c), half(a, kj, c), sibling)
                fwd.start()
                sends.append(fwd)
        for a in range(n):
            for j, chip in enumerate(chips):
                kj = 2 * chip[0] + chip[1]
                remote(a, 3 + j, half(a, kj, 1 - c), half(a, kj, 1 - c), sibling).wait_recv()
        for b in range(ns):
            for j, chip in enumerate(chips):
                sm_copy(b, j, 2 * chip[0] + chip[1], (*chip, c)).wait_recv()
        for a in range(n):
            remote(a, 6, ins[a], _block(outs[a], kinds[a], kme), sibling).wait_recv()
        for cp in sends:
            cp.wait_send()
        for cp in local:
            cp.wait()

    outs = pl.pallas_call(
        body, name="gather_weights",
        out_shape=tuple([jax.ShapeDtypeStruct(_gathered_shape(s, k), s.dtype) for s, k in zip(shards, kinds)]
                        + [jax.ShapeDtypeStruct((4,) + s.shape, s.dtype) for s in smalls]),
        in_specs=[ANY] * (n + ns), out_specs=tuple([ANY] * (n + ns)),
        scratch_shapes=[pltpu.SemaphoreType.DMA((7 * n,)), pltpu.SemaphoreType.DMA((7 * n,)),
                        pltpu.SemaphoreType.DMA((3 * ns,)), pltpu.SemaphoreType.DMA((3 * ns,)),
                        pltpu.SemaphoreType.DMA((ns,))],
    )(*shards, *smalls)
    return outs[:n], outs[n:]


_IN_HBM = pl.BlockSpec(memory_space=pltpu.HBM)
_SEMS = pl.BlockSpec(memory_space=pltpu.SEMAPHORE)
_DATAFLOW = pltpu.SideEffectType.DATAFLOW_SIDE_EFFECTING


def _hbm(t):
    return pltpu.HBM(t.shape, t.dtype)


def _gather_copies(ins, outs, send_sems, recv_sems, kinds):
    x, y, c, chips = _place()
    kme = 2 * x + y
    cps = []
    for a in range(len(ins)):
        h = ins[a].shape[0] // 2
        for j, chip in enumerate(chips + [None]):
            to = (x, y, 1 - c) if chip is None else (*chip, c)
            src = ins[a] if chip is None else ins[a].at[pl.ds(c * h, h), :]
            dst = _block(outs[a], kinds[a], kme, None if chip is None else (c * h, h))
            cps.append(pltpu.make_async_remote_copy(src_ref=src, dst_ref=dst, send_sem=send_sems.at[4 * a + j],
                                                    recv_sem=recv_sems.at[4 * a + j], device_id=to, device_id_type=MESH))
    return cps


def _gather_start(shards, kinds, name):
    n = len(shards)
    outs = [lax.empty(_gathered_shape(s, k), s.dtype) for s, k in zip(shards, kinds)]

    def body(*refs):
        for cp in _gather_copies(refs[:n], refs[n:2 * n], refs[2 * n], refs[2 * n + 1], kinds):
            cp.start()
        refs[-1][...] = jnp.zeros_like(refs[-1])

    return pl.pallas_call(
        body, name=name,
        out_shape=(pltpu.SemaphoreType.DMA((4 * n,)), pltpu.SemaphoreType.DMA((4 * n,)),
                   *[_hbm(t) for t in shards], *[_hbm(t) for t in outs], jax.ShapeDtypeStruct((8, LANES), F32)),
        in_specs=[_IN_HBM] * (2 * n),
        out_specs=(_SEMS, _SEMS, *[_IN_HBM] * (2 * n), pl.BlockSpec(memory_space=pltpu.VMEM)),
        input_output_aliases={a: 2 + a for a in range(2 * n)},
        compiler_params=pltpu.CompilerParams(has_side_effects=_DATAFLOW),
    )(*[pltpu.with_memory_space_constraint(t, pltpu.HBM) for t in list(shards) + outs])


def _gather_wait(started, after, kinds, name):
    n = (len(started) - 3) // 2
    bufs = started[2:2 + 2 * n]

    def body(*refs):
        for cp in _gather_copies(refs[:n], refs[n:2 * n], refs[2 * n], refs[2 * n + 1], kinds):
            cp.wait_send()
            cp.wait_recv()

    outs = pl.pallas_call(
        body, name=name, out_shape=tuple(_hbm(t) for t in bufs),
        in_specs=[_IN_HBM] * (2 * n) + [_SEMS, _SEMS, ANY], out_specs=tuple([_IN_HBM] * (2 * n)),
        input_output_aliases={a: a for a in range(2 * n)},
        compiler_params=pltpu.CompilerParams(has_side_effects=_DATAFLOW),
    )(*bufs, started[0], started[1], after)
    return outs[n:]


def _gather_relay(bufs, kinds, name):
    n = len(bufs)

    def body(*refs):
        ins, outs, send_sems, recv_sems = refs[:n], refs[n:2 * n], refs[2 * n], refs[2 * n + 1]
        x, y, c, chips = _place()
        cps = []
        for a in range(n):
            h = (ins[a].shape[1] if kinds[a] == "rows" else ins[a].shape[0]) // 2
            for j, chip in enumerate(chips):
                kj = 2 * chip[0] + chip[1]
                cps.append((pltpu.make_async_remote_copy(
                    src_ref=_block(ins[a], kinds[a], kj, (c * h, h)), dst_ref=_block(outs[a], kinds[a], kj, (c * h, h)),
                    send_sem=send_sems.at[3 * a + j], recv_sem=recv_sems.at[3 * a + j], device_id=(x, y, 1 - c),
                    device_id_type=MESH), a, kj, h))
        for cp, _, _, _ in cps:
            cp.start()
        for a_cp, (cp, a, kj, h) in enumerate(cps):
            theirs = _block(outs[a], kinds[a], kj, ((1 - c) * h, h))
            pltpu.make_async_remote_copy(src_ref=theirs, dst_ref=theirs, send_sem=send_sems.at[a_cp],
                                         recv_sem=recv_sems.at[a_cp], device_id=(x, y, 1 - c), device_id_type=MESH).wait_recv()
        for cp, _, _, _ in cps:
            cp.wait_send()

    return pl.pallas_call(
        body, name=name, out_shape=tuple(jax.ShapeDtypeStruct(b.shape, b.dtype) for b in bufs),
        in_specs=[ANY] * n, out_specs=tuple([ANY] * n), input_output_aliases={a: a for a in range(n)},
        scratch_shapes=[pltpu.SemaphoreType.DMA((3 * n,)), pltpu.SemaphoreType.DMA((3 * n,))],
    )(*bufs)


def _add_halves(g, r1, cvec, kind, name):
    def body(c_ref, g_ref, r_ref, o_ref):
        o_ref[...] = (g_ref[...] + r_ref[...]).astype(o_ref.dtype)

    if kind == "rows":
        _, h, C = r1.shape
        tr = _row_tile(h, 512)
        nt = h // tr
        grid = (4, nt)
        g_spec = pl.BlockSpec((1, tr, C), lambda k, i, c_ref: (k, c_ref[0] * nt + i, 0))
        r_spec = pl.BlockSpec((1, tr, C), lambda k, i, c_ref: (k, i, 0))
    else:
        h, C4 = r1.shape
        tr, tc = _row_tile(h, 512), C4 // 4
        nt = h // tr
        grid = (nt, 4)
        g_spec = pl.BlockSpec((tr, tc), lambda i, k, c_ref: (c_ref[0] * nt + i, k))
        r_spec = pl.BlockSpec((tr, tc), lambda i, k, c_ref: (i, k))
    return pl.pallas_call(
        body, name=name, out_shape=jax.ShapeDtypeStruct(r1.shape, BF16),
        grid_spec=pltpu.PrefetchScalarGridSpec(num_scalar_prefetch=1, grid=grid, in_specs=[g_spec, r_spec],
                                               out_specs=r_spec),
        compiler_params=_cparams(("parallel", "parallel")),
    )(cvec, g, r1)


def _chip_copies(ins, lands, send_sems, recv_sems, kinds):
    x, y, c, chips = _place()
    return [pltpu.make_async_remote_copy(
        src_ref=_block(ins[a], kinds[a], 2 * chip[0] + chip[1]), dst_ref=lands[a].at[j],
        send_sem=send_sems.at[3 * a + j], recv_sem=recv_sems.at[3 * a + j], device_id=(*chip, c), device_id_type=MESH)
        for a in range(len(ins)) for j, chip in enumerate(chips)]


def _land_shape(s, kind):
    return (3,) + (s.shape[1:] if kind == "rows" else (s.shape[0], s.shape[1] // 4))


def _sibling_copies(ins, lands, send_sems, recv_sems, kinds):
    x, y, c, _ = _place()
    cps = []
    for a in range(len(ins)):
        h = lands[a].shape[-2]
        src = ins[a].at[:, pl.ds((1 - c) * h, h), :] if kinds[a] == "rows" else ins[a].at[pl.ds((1 - c) * h, h), :]
        cps.append(pltpu.make_async_remote_copy(src_ref=src, dst_ref=lands[a], send_sem=send_sems.at[a],
                                                recv_sem=recv_sems.at[a], device_id=(x, y, 1 - c), device_id_type=MESH))
    return cps


def _half_shape(g, kind):
    return (4, g.shape[1] // 2, g.shape[2]) if kind == "rows" else (g.shape[0] // 2, g.shape[1])


def _exchange_start(copies, per_array, srcs, land_shapes, kinds, name):
    n = len(srcs)
    lands = [lax.empty(shape, s.dtype) for shape, s in zip(land_shapes, srcs)]

    def body(*refs):
        for cp in copies(refs[:n], refs[n:2 * n], refs[2 * n], refs[2 * n + 1], kinds):
            cp.start()
        refs[-1][...] = jnp.zeros_like(refs[-1])

    return pl.pallas_call(
        body, name=name,
        out_shape=(pltpu.SemaphoreType.DMA((per_array * n,)), pltpu.SemaphoreType.DMA((per_array * n,)),
                   *[_hbm(t) for t in srcs], *[_hbm(t) for t in lands], jax.ShapeDtypeStruct((8, LANES), F32)),
        in_specs=[_IN_HBM] * (2 * n),
        out_specs=(_SEMS, _SEMS, *[_IN_HBM] * (2 * n), pl.BlockSpec(memory_space=pltpu.VMEM)),
        input_output_aliases={a: 2 + a for a in range(2 * n)},
        compiler_params=pltpu.CompilerParams(has_side_effects=_DATAFLOW),
    )(*[pltpu.with_memory_space_constraint(t, pltpu.HBM) for t in list(srcs) + lands])


def _exchange_wait(copies, started, after, kinds, name):
    n = (len(started) - 3) // 2
    bufs = started[2:2 + 2 * n]

    def body(*refs):
        for cp in copies(refs[:n], refs[n:2 * n], refs[2 * n], refs[2 * n + 1], kinds):
            cp.wait_send()
            cp.wait_recv()

    outs = pl.pallas_call(
        body, name=name, out_shape=tuple(_hbm(t) for t in bufs),
        in_specs=[_IN_HBM] * (2 * n) + [_SEMS, _SEMS, ANY], out_specs=tuple([_IN_HBM] * (2 * n)),
        input_output_aliases={a: a for a in range(2 * n)},
        compiler_params=pltpu.CompilerParams(has_side_effects=_DATAFLOW),
    )(*bufs, started[0], started[1], after)
    return outs[:n], outs[n:]


def _add_chips(s1, r2, kcvec, kind, name):
    _, h, C = r2.shape
    tr = _row_tile(h, 512)
    nt = h // tr

    def body(kc_ref, s_ref, r0_ref, r1_ref, r2_ref, o_ref):
        s = s_ref[0] if kind == "rows" else s_ref[...]
        o_ref[...] = ((s.astype(F32) + r0_ref[0].astype(F32)) + r1_ref[0].astype(F32)) + r2_ref[0].astype(F32)

    peer = lambda j: pl.BlockSpec((1, tr, C), lambda i, kc_ref: (j, i, 0))
    if kind == "rows":
        s_spec = pl.BlockSpec((1, tr, C), lambda i, kc_ref: (kc_ref[0], i, 0))
    else:
        s_spec = pl.BlockSpec((tr, C), lambda i, kc_ref: (i, kc_ref[0]))
    return pl.pallas_call(
        body, name=name, out_shape=jax.ShapeDtypeStruct((2 * h, C), F32),
        grid_spec=pltpu.PrefetchScalarGridSpec(
            num_scalar_prefetch=1, grid=(nt,),
            in_specs=[s_spec, peer(0), peer(1), peer(2)],
            out_specs=pl.BlockSpec((tr, C), lambda i, kc_ref: (kc_ref[1] * nt + i, 0))),
        compiler_params=_cparams(("parallel",)),
    )(kcvec, s1, r2, r2, r2)


def _join_sibling_halves(bufs):
    n = len(bufs)

    def body(*refs):
        ins, outs, send_sems, recv_sems = refs[:n], refs[n:2 * n], refs[2 * n], refs[2 * n + 1]
        x, y, c, _ = _place()
        cps = []
        for a in range(n):
            h = ins[a].shape[0] // 2
            cps.append(pltpu.make_async_remote_copy(
                src_ref=ins[a].at[pl.ds(c * h, h), :], dst_ref=outs[a].at[pl.ds(c * h, h), :], send_sem=send_sems.at[a],
                recv_sem=recv_sems.at[a], device_id=(x, y, 1 - c), device_id_type=MESH))
        for cp in cps:
            cp.start()
        for a in range(n):
            h = ins[a].shape[0] // 2
            theirs = outs[a].at[pl.ds((1 - c) * h, h), :]
            pltpu.make_async_remote_copy(src_ref=theirs, dst_ref=theirs, send_sem=send_sems.at[a],
                                         recv_sem=recv_sems.at[a], device_id=(x, y, 1 - c), device_id_type=MESH).wait_recv()
        for cp in cps:
            cp.wait_send()

    return pl.pallas_call(
        body, name="grads_join",
        out_shape=tuple(jax.ShapeDtypeStruct(b.shape, b.dtype) for b in bufs),
        in_specs=[ANY] * n, out_specs=tuple([ANY] * n), input_output_aliases={a: a for a in range(n)},
        scratch_shapes=[pltpu.SemaphoreType.DMA((n,)), pltpu.SemaphoreType.DMA((n,))],
    )(*bufs)


N_DEV = 8


def _allreduce_small(pack):
    P = pack.shape[0]

    def body(p_ref, o_ref, gath, send_sems, recv_sems):
        x, y, c, _ = _place()
        me = 4 * x + 2 * y + c
        cps = []
        for mask in range(1, N_DEV):
            px = 1 - x if mask & 4 else x
            py = 1 - y if mask & 2 else y
            pc = 1 - c if mask & 1 else c
            cps.append((pltpu.make_async_remote_copy(
                src_ref=p_ref, dst_ref=gath.at[me], send_sem=send_sems.at[mask - 1], recv_sem=recv_sems.at[mask - 1],
                device_id=(px, py, pc), device_id_type=MESH), 4 * px + 2 * py + pc, mask))
        for cp, _, _ in cps:
            cp.start()
        gath[me] = p_ref[...]
        for _, peer, mask in cps:
            pltpu.make_async_remote_copy(
                src_ref=p_ref, dst_ref=gath.at[peer], send_sem=send_sems.at[mask - 1], recv_sem=recv_sems.at[mask - 1],
                device_id=(x, y, c), device_id_type=MESH).wait_recv()
        for cp, _, _ in cps:
            cp.wait_send()
        acc = gath[0]
        for i in range(1, N_DEV):
            acc = acc + gath[i]
        o_ref[...] = acc

    return pl.pallas_call(
        body, name="allreduce_small", out_shape=jax.ShapeDtypeStruct((P, LANES), F32),
        in_specs=[pl.BlockSpec(memory_space=pltpu.VMEM)], out_specs=pl.BlockSpec(memory_space=pltpu.VMEM),
        scratch_shapes=[pltpu.VMEM((N_DEV, P, LANES), F32), pltpu.SemaphoreType.DMA((N_DEV - 1,)),
                        pltpu.SemaphoreType.DMA((N_DEV - 1,))],
    )(pack)


def _pack_rows(arrs):
    rows = []
    for a in arrs:
        f = a.reshape(-1)
        f = jnp.pad(f, (0, (-f.shape[0]) % (8 * LANES)))
        rows.append(f.reshape(-1, LANES))
    return jnp.concatenate(rows, axis=0)


def _unpack_rows(pack, shapes):
    out, r = [], 0
    for s in shapes:
        n = math.prod(s)
        out.append(pack[r:r + -(-n // LANES)].reshape(-1)[:n].reshape(s))
        r += 8 * -(-n // (8 * LANES))
    return out


_SMALL = ["norm_mix_pre", "ml_head_norm", "b_gate_a", "b_gate_b", "norm_mix_post", "norm_ffn_pre", "norm_ffn_post",
          "conv_b", "b_ml_i", "b_ml_f", "b_fox_f"]
_BIG = ["w_in", "w_branch_a", "w_branch_b", "w_out", "w_up", "w_down"]
_WEIGHTS = ['norm_mix_pre', 'w_in', 'b_ml_i', 'b_ml_f', 'ml_head_norm', 'b_fox_f', 'b_gate_a', 'b_gate_b', 'w_branch_a',
            'w_branch_b', 'w_out', 'norm_mix_post', 'norm_ffn_pre', 'w_up', 'conv_w', 'conv_b', 'w_down', 'norm_ffn_post']


_KINDS = ["rows", "rows", "rows", "rows", "cols", "rows"]


def kernel(x, norm_mix_pre, w_in, b_ml_i, b_ml_f, ml_head_norm, b_fox_f, b_gate_a, b_gate_b, w_branch_a, w_branch_b, w_out, norm_mix_post, norm_ffn_pre, w_up, conv_w, conv_b, w_down, norm_ffn_post, loss_target, m_norm_mix_pre, m_w_in, m_b_ml_i, m_b_ml_f, m_ml_head_norm, m_b_fox_f, m_b_gate_a, m_b_gate_b, m_w_branch_a, m_w_branch_b, m_w_out, m_norm_mix_post, m_norm_ffn_pre, m_w_up, m_conv_w, m_conv_b, m_w_down, m_norm_ffn_post, v_norm_mix_pre, v_w_in, v_b_ml_i, v_b_ml_f, v_ml_head_norm, v_b_fox_f, v_b_gate_a, v_b_gate_b, v_w_branch_a, v_w_branch_b, v_w_out, v_norm_mix_post, v_norm_ffn_pre, v_w_up, v_conv_w, v_conv_b, v_w_down, v_norm_ffn_post):
    args = dict(locals())
    w = {n: args[n] for n in _WEIGHTS}
    mom = {n: args["m_" + n] for n in _WEIGHTS}
    var = {n: args["v_" + n] for n in _WEIGHTS}
    cx, cy, cc = lax.axis_index("x"), lax.axis_index("y"), lax.axis_index("c")
    kme = 2 * cx + cy
    cvec = jnp.reshape(cc, (1,)).astype(jnp.int32)
    kcvec = jnp.stack([kme, cc]).astype(jnp.int32)
    odd = kme % 2

    tr3 = lambda t: jnp.transpose(t, (0, 2, 1))
    w["w_in"], mom["w_in"], var["w_in"] = tr3(w_in), tr3(m_w_in), tr3(v_w_in)
    w_in_main = lax.dynamic_slice_in_dim(w["w_in"][0], 4 * odd, 2048, axis=0).astype(BF16)
    w_in_gates = lax.dynamic_slice_in_dim(w["w_in"][0], 2048 * (1 - odd), 4, axis=0).astype(BF16)
    (wmain_t,), (g_cw, g_gates) = _gather_weights([w_in_main], _KINDS[:1], [w["conv_w"][0], w_in_gates])
    rest_started = _gather_start([w[n][0].astype(BF16) for n in _BIG[1:]], _KINDS[1:], "gather_rest_start")

    def rest_weights(after):
        bufs = _gather_wait(rest_started, after, _KINDS[1:], "gather_rest_wait")
        g_a, g_b, g_out, wup, g_down = _gather_relay(bufs, _KINDS[1:], "gather_rest_relay")
        return full(g_a), full(g_b), full(g_out), wup, full(g_down)
    gate_rows = g_gates.reshape(16, D_MODEL)
    wsmall_t = jnp.zeros((N_SMALL, D_MODEL), BF16)
    for blk, (lo, hi) in enumerate(((0, 4), (4, 8), (8, 16))):
        wsmall_t = wsmall_t.at[blk * LANES:blk * LANES + hi - lo].set(gate_rows[lo:hi])
    full = lambda g: g.reshape(-1, g.shape[2])
    p = {n: w[n] for n in _SMALL}
    p["conv_w"] = jnp.transpose(g_cw, (1, 0, 2)).reshape(3, -1)

    groups = {}

    def on_grads(group, gs):
        names = list(gs)
        kinds = [_KINDS[_BIG.index(n)] for n in names]
        whole = [g if k == "cols" else g.reshape(4, -1, g.shape[1]) for g, k in zip(gs.values(), kinds)]
        started = _exchange_start(_sibling_copies, 1, whole, [_half_shape(g, k) for g, k in zip(whole, kinds)], kinds,
                                  "grads_to_sibling_start_" + group)
        groups[group] = dict(names=names, kinds=kinds, sibling=started)
        return started[-1]

    def advance(group, after):
        G = groups[group]
        whole, got = _exchange_wait(_sibling_copies, G["sibling"], after, G["kinds"], "grads_to_sibling_wait_" + group)
        sums = [_add_halves(g, r, cvec, k, "add_sibling_" + n) for g, r, k, n in zip(whole, got, G["kinds"], G["names"])]
        G["chips"] = _exchange_start(_chip_copies, 3, sums, [_land_shape(s, k) for s, k in zip(sums, G["kinds"])],
                                     G["kinds"], "grads_to_chips_start_" + group)
        return G["chips"][-1]

    loss_row, grad_x, big, small = _local_step(x[0], loss_target[0], full(wmain_t), wsmall_t, rest_weights, p, on_grads,
                                               advance, rest_started[-1])
    grads = {}
    mine, mine_names = [], []
    for group, G in groups.items():
        sums, got = _exchange_wait(_chip_copies, G["chips"], grad_x, G["kinds"], "grads_to_chips_wait_" + group)
        mine += [_add_chips(s, r, kcvec, k, "add_chips_" + n) for s, r, k, n in zip(sums, got, G["kinds"], G["names"])]
        mine_names += G["names"]
    grads.update(zip(mine_names, _join_sibling_halves(mine)))

    gt = big["wsmall_t"]
    small["w_in_gates"] = jnp.concatenate([gt[0:4], gt[LANES:LANES + 4], gt[2 * LANES:2 * LANES + 8]], axis=0)
    small_names = _SMALL + ["conv_w"]
    packed_names = small_names + ["w_in_gates"]
    pack = _pack_rows([small[n] for n in packed_names] + [loss_row])
    pack = jnp.pad(pack, ((0, (-pack.shape[0]) % 8), (0, 0)))
    full_shapes = [small[n].shape if n in ("conv_w", "w_in_gates") else w[n][0].shape for n in packed_names]
    total = _unpack_rows(_allreduce_small(pack), full_shapes + [loss_row.shape])
    for n, t in zip(packed_names, total):
        grads[n] = t
    loss = total[-1][0, 0]
    grads["conv_w"] = lax.dynamic_slice_in_dim(grads["conv_w"], kme * conv_w.shape[2], conv_w.shape[2], axis=1)
    my_gates = lax.dynamic_slice_in_dim(grads.pop("w_in_gates"), 4 * kme, 4, axis=0)
    g_in = jnp.zeros(w["w_in"].shape[1:], F32)
    g_in = lax.dynamic_update_slice_in_dim(g_in, grads["w_in"], 4 * odd, axis=0)
    grads["w_in"] = lax.dynamic_update_slice_in_dim(g_in, my_gates, 2048 * (1 - odd), axis=0)

    delta, new_m, new_v = {}, {}, {}
    for n in _BIG:
        delta[n], new_m[n], new_v[n] = _adamw(w[n], grads[n], mom[n], var[n], "adamw_" + n)
        grads[n] = grads[n][None]
    for d in (grads, delta, new_m, new_v):
        d["w_in"] = tr3(d["w_in"])
    packs = [_pack_rows([d[n][0] for n in small_names]) for d in (w, mom, var)]
    pad = ((0, (-packs[0].shape[0]) % 8), (0, 0))
    packs = [jnp.pad(t, pad)[None] for t in packs]
    gp = jnp.pad(_pack_rows([grads[n] for n in small_names]), pad)
    shapes = [w[n][0].shape for n in small_names]
    for dst, res in zip((delta, new_m, new_v), _adamw(packs[0], gp, packs[1], packs[2], "adamw_small")):
        for n, t in zip(small_names, _unpack_rows(res[0], shapes)):
            dst[n] = t[None]
    for n in small_names:
        grads[n] = grads[n][None]

    return (loss, grad_x[None], *[grads[n] for n in _WEIGHTS], *[delta[n] for n in _WEIGHTS],
            *[new_m[n] for n in _WEIGHTS], *[new_v[n] for n in _WEIGHTS])
```

```python
import functools
import math

import jax
import jax.numpy as jnp
from jax import lax
from jax.experimental import pallas as pl
from jax.experimental.pallas import tpu as pltpu

F32 = jnp.float32
BF16 = jnp.bfloat16
MESH = pl.DeviceIdType.MESH

D_MODEL = 1024
ML_HEADS = 4
ML_DQK = 128
ML_DV = 256
FOX_HEADS = 8
FOX_DH = 128
D_FF = 2816
GATE_CAP = 15.0
EPS = 1e-6
ADAM_LR, ADAM_B1, ADAM_B2, ADAM_EPS, ADAM_WD, ADAM_STEP = 0.001, 0.9, 0.999, 1e-08, 0.01, 10

LANES = 128
MLC = 256
FOX_TQ = 512
FOX_TQ_FWD = 512
FOX_TK = 512
FOX_TK_FWD = 512
ROW_T = 512
CONV_TC = 1408
VMEM_LIMIT = 56 * 1024 * 1024

C_QM, C_KM, C_VM, C_OM = 0, 512, 1024, 2048
N_ML, N_FOX, N_GATE = 3072, 3072, 2048
N_SMALL = 384


def _cparams(sem=None):
    return pltpu.CompilerParams(dimension_semantics=sem, vmem_limit_bytes=VMEM_LIMIT)


def _tile(n, target):
    if n <= target:
        return n
    best = None
    for t in range(LANES, target + 1, LANES):
        if n % t == 0:
            best = t
    assert best is not None, (n, target)
    return best


def _dot(a, b, dims):
    return lax.dot_general(a, b, (dims, ((), ())), preferred_element_type=F32)


def _dot_nn(a, b):
    return _dot(a, b, ((1,), (0,)))


def _dot_nt(a, b):
    return _dot(a, b, ((1,), (1,)))


def _dot_tn(a, b):
    return _dot(a, b, ((0,), (0,)))


_DOTS = {"nn": _dot_nn, "nt": _dot_nt, "tn": _dot_tn}


def _mm(a, b, mode, out_dtype, name, tm=1024, tn=1408, tk=1408, after=None):
    a_parts = list(a) if isinstance(a, (list, tuple)) else [a]
    b_parts = list(b) if isinstance(b, (list, tuple)) else [b]
    extra = [] if after is None else [after]
    assert len(a_parts) == 1 or len(b_parts) == 1, name
    a_axes = {"nn": "ik", "nt": "ik", "tn": "ki"}[mode]
    b_axes = {"nn": "kj", "nt": "jk", "tn": "kj"}[mode]
    size, target = {}, dict(i=tm, j=tn, k=tk)
    for parts, axes in ((a_parts, a_axes), (b_parts, b_axes)):
        dims = (parts[0].shape[0], parts[0].shape[1] * len(parts))
        for ax, n in zip(axes, dims):
            assert size.setdefault(ax, n) == n, (name, ax, n, size)
    tile = {}
    for parts, axes in ((a_parts, a_axes), (b_parts, b_axes)):
        if len(parts) > 1:
            tile[axes[1]] = _tile(parts[0].shape[1], target[axes[1]])
    for ax in "ijk":
        tile.setdefault(ax, _tile(size[ax], target[ax]))
    M, N, nk = size["i"], size["j"], size["k"] // tile["k"]
    grid_pos = dict(i=0, j=1, k=2)
    dot = _DOTS[mode]

    def specs(parts, axes):
        blk = (tile[axes[0]], tile[axes[1]])
        if len(parts) == 1:
            return [pl.BlockSpec(blk, lambda *g: (g[grid_pos[axes[0]]], g[grid_pos[axes[1]]]))], None
        bpp = parts[0].shape[1] // blk[1]

        def index(p):
            def f(*g):
                g0, g1 = g[grid_pos[axes[0]]], g[grid_pos[axes[1]]]
                on = g1 // bpp == p
                return jnp.where(on, g0, 0), jnp.where(on, g1 % bpp, 0)
            return f

        return [pl.BlockSpec(blk, index(p)) for p in range(len(parts))], (axes[1], bpp)

    a_specs, a_sel = specs(a_parts, a_axes)
    b_specs, b_sel = specs(b_parts, b_axes)
    na, nb = len(a_parts), len(b_parts)

    def body(*refs):
        a_refs, b_refs = refs[:na], refs[na:na + nb]
        o_ref, acc = refs[na + nb + len(extra)], refs[na + nb + len(extra) + 1:]

        def accumulate(part):
            if nk == 1:
                o_ref[...] = part.astype(o_ref.dtype)
                return
            acc_ref, = acc
            k = pl.program_id(2)

            @pl.when(k == 0)
            def _():
                acc_ref[...] = part

            @pl.when(k > 0)
            def _():
                acc_ref[...] += part

            @pl.when(k == nk - 1)
            def _():
                o_ref[...] = acc_ref[...].astype(o_ref.dtype)

        sel = a_sel or b_sel
        if sel is None:
            accumulate(dot(a_refs[0][...], b_refs[0][...]))
        else:
            which = pl.program_id(grid_pos[sel[0]]) // sel[1]
            for p in range(max(na, nb)):
                @pl.when(which == p)
                def _(p=p):
                    accumulate(dot(a_refs[p if a_sel else 0][...], b_refs[p if b_sel else 0][...]))

    return pl.pallas_call(
        body, name=name,
        out_shape=jax.ShapeDtypeStruct((M, N), out_dtype),
        grid=(M // tile["i"], N // tile["j"], nk),
        in_specs=a_specs + b_specs + [pl.BlockSpec(memory_space=pl.ANY)] * len(extra),
        out_specs=pl.BlockSpec((tile["i"], tile["j"]), lambda i, j, k: (i, j)),
        scratch_shapes=[pltpu.VMEM((tile["i"], tile["j"]), F32)] if nk > 1 else [],
        compiler_params=_cparams(("parallel", "parallel", "arbitrary")),
    )(*a_parts, *b_parts, *extra)


def _rstd(x):
    return lax.rsqrt(jnp.mean(x * x, axis=-1, keepdims=True) + EPS)


def _rmsnorm_fwd(x, g, name):
    S, D = x.shape
    T = _tile(S, ROW_T)

    def body(x_ref, g_ref, o_ref):
        xv = x_ref[...]
        o_ref[...] = (xv * _rstd(xv) * g_ref[...]).astype(o_ref.dtype)

    return pl.pallas_call(
        body, name=name, out_shape=jax.ShapeDtypeStruct((S, D), BF16), grid=(S // T,),
        in_specs=[pl.BlockSpec((T, D), lambda i: (i, 0)), pl.BlockSpec((1, D), lambda i: (0, 0))],
        out_specs=pl.BlockSpec((T, D), lambda i: (i, 0)),
        compiler_params=_cparams(("parallel",)),
    )(x, g)


def _resid_norm_fwd(x, z, g, name):
    S, D = x.shape
    T = _tile(S, ROW_T)

    def body(x_ref, z_ref, g_ref, o_ref):
        zv = z_ref[...]
        o_ref[...] = x_ref[...] + zv * _rstd(zv) * g_ref[...]

    row = pl.BlockSpec((T, D), lambda i: (i, 0))
    return pl.pallas_call(
        body, name=name, out_shape=jax.ShapeDtypeStruct((S, D), F32), grid=(S // T,),
        in_specs=[row, row, pl.BlockSpec((1, D), lambda i: (0, 0))],
        out_specs=row, compiler_params=_cparams(("parallel",)),
    )(x, z, g)


def _rmsnorm_bwd_math(dy, xv, g):
    r = _rstd(xv)
    u = dy * g
    dx = r * u - xv * (r * r * r) * jnp.mean(u * xv, axis=-1, keepdims=True)
    return dx, dy * xv * r


def _rmsnorm_bwd(dys, xin, g, resid, out_dtype, name):
    S, D = xin.shape
    T = _tile(S, ROW_T)
    has_resid = resid is not None
    ndy = len(dys)

    def body(*refs):
        dy_refs, (x_ref, g_ref) = refs[:ndy], refs[ndy:ndy + 2]
        dx_ref, dg_ref = refs[-2:]
        dy = dy_refs[0][...]
        for r in dy_refs[1:]:
            dy = dy + r[...]
        dx, dgt = _rmsnorm_bwd_math(dy, x_ref[...], g_ref[...])
        if has_resid:
            dx = dx + refs[ndy + 2][...]
        dx_ref[...] = dx.astype(dx_ref.dtype)

        @pl.when(pl.program_id(0) == 0)
        def _():
            dg_ref[...] = jnp.zeros_like(dg_ref)

        dg_ref[...] += jnp.sum(dgt, axis=0, keepdims=True)

    row = pl.BlockSpec((T, D), lambda i: (i, 0))
    vec = pl.BlockSpec((1, D), lambda i: (0, 0))
    ins = list(dys) + [xin, g] + ([resid] if has_resid else [])
    return pl.pallas_call(
        body, name=name,
        out_shape=(jax.ShapeDtypeStruct((S, D), out_dtype), jax.ShapeDtypeStruct((1, D), F32)),
        grid=(S // T,), in_specs=[row] * ndy + [row, vec] + ([row] if has_resid else []),
        out_specs=(row, vec), compiler_params=_cparams(("arbitrary",)),
    )(*ins)


def _loss_head(x1, d, g, target, name):
    S, D = x1.shape
    T = _tile(S, ROW_T)

    def body(x_ref, d_ref, g_ref, t_ref, loss_ref, dy_ref, dd_ref, dg_ref):
        dv, gv = d_ref[...], g_ref[...]
        y = x_ref[...] + dv * _rstd(dv) * gv
        diff = y - t_ref[...]
        dy = diff * (1.0 / D)
        dy_ref[...] = dy
        dd, dgt = _rmsnorm_bwd_math(dy, dv, gv)
        dd_ref[...] = dd.astype(dd_ref.dtype)

        @pl.when(pl.program_id(0) == 0)
        def _():
            dg_ref[...] = jnp.zeros_like(dg_ref)
            loss_ref[...] = jnp.zeros_like(loss_ref)

        dg_ref[...] += jnp.sum(dgt, axis=0, keepdims=True)
        part = jnp.sum(jnp.sum(diff * diff, axis=1, keepdims=True), axis=0, keepdims=True)
        loss_ref[...] += (0.5 / D) * part

    row = pl.BlockSpec((T, D), lambda i: (i, 0))
    vec = pl.BlockSpec((1, D), lambda i: (0, 0))
    return pl.pallas_call(
        body, name=name,
        out_shape=(jax.ShapeDtypeStruct((1, LANES), F32), jax.ShapeDtypeStruct((S, D), F32),
                   jax.ShapeDtypeStruct((S, D), BF16), jax.ShapeDtypeStruct((1, D), F32)),
        grid=(S // T,), in_specs=[row, row, vec, row],
        out_specs=(pl.BlockSpec((1, LANES), lambda i: (0, 0)), row, row, vec),
        compiler_params=_cparams(("arbitrary",)),
    )(x1, d, g, target)


def _merge_fwd(ya, yb, pm, ba, bb, name):
    S, D = ya.shape
    T = _tile(S, ROW_T)

    def body(ya_ref, yb_ref, ga_ref, gb_ref, ba_ref, bb_ref, o_ref):
        sa = jax.nn.sigmoid(ga_ref[...] + ba_ref[...])
        sb = jax.nn.sigmoid(gb_ref[...] + bb_ref[...])
        o_ref[...] = (sa * ya_ref[...] + sb * yb_ref[...]).astype(o_ref.dtype)

    row = pl.BlockSpec((T, D), lambda i: (i, 0))
    vec = pl.BlockSpec((1, D), lambda i: (0, 0))
    return pl.pallas_call(
        body, name=name, out_shape=jax.ShapeDtypeStruct((S, D), BF16), grid=(S // T,),
        in_specs=[row, row, pl.BlockSpec((T, D), lambda i: (i, 0)),
                  pl.BlockSpec((T, D), lambda i: (i, 1)), vec, vec],
        out_specs=row, compiler_params=_cparams(("parallel",)),
    )(ya, yb, pm, pm, ba, bb)


def _merge_bwd(dmerged, ya, yb, pm, ba, bb, name):
    S, D = ya.shape
    T = _tile(S, ROW_T)

    def body(dm_ref, ya_ref, yb_ref, ga_ref, gb_ref, ba_ref, bb_ref,
             dya_ref, dyb_ref, dga_ref, dgb_ref, dba_ref, dbb_ref):
        dm = dm_ref[...]
        sa = jax.nn.sigmoid(ga_ref[...] + ba_ref[...])
        sb = jax.nn.sigmoid(gb_ref[...] + bb_ref[...])
        dya_ref[...] = (dm * sa).astype(dya_ref.dtype)
        dyb_ref[...] = (dm * sb).astype(dyb_ref.dtype)
        dga = dm * ya_ref[...] * sa * (1.0 - sa)
        dgb = dm * yb_ref[...] * sb * (1.0 - sb)
        dga_ref[...] = dga.astype(dga_ref.dtype)
        dgb_ref[...] = dgb.astype(dgb_ref.dtype)

        @pl.when(pl.program_id(0) == 0)
        def _():
            dba_ref[...] = jnp.zeros_like(dba_ref)
            dbb_ref[...] = jnp.zeros_like(dbb_ref)

        dba_ref[...] += jnp.sum(dga, axis=0, keepdims=True)
        dbb_ref[...] += jnp.sum(dgb, axis=0, keepdims=True)

    row = pl.BlockSpec((T, D), lambda i: (i, 0))
    vec = pl.BlockSpec((1, D), lambda i: (0, 0))
    act = jax.ShapeDtypeStruct((S, D), BF16)
    v1 = jax.ShapeDtypeStruct((1, D), F32)
    return pl.pallas_call(
        body, name=name, out_shape=(act, act, act, act, v1, v1), grid=(S // T,),
        in_specs=[row, row, row, pl.BlockSpec((T, D), lambda i: (i, 0)),
                  pl.BlockSpec((T, D), lambda i: (i, 1)), vec, vec],
        out_specs=(row, row, row, row, vec, vec), compiler_params=_cparams(("arbitrary",)),
    )(dmerged, ya, yb, pm, pm, ba, bb)


_GELU_C = math.sqrt(2.0 / math.pi)


def _gelu(g):
    t = jnp.tanh(_GELU_C * (g + 0.044715 * g * g * g))
    return 0.5 * g * (1.0 + t), t


def _gelu_grad(g, t):
    return 0.5 * (1.0 + t) + 0.5 * g * (1.0 - t * t) * _GELU_C * (1.0 + 3 * 0.044715 * g * g)


def _shift_down(v, halo_ref, first, rows):
    T = v.shape[0]
    keep = jnp.where(first, 0.0, 1.0)
    h7 = halo_ref[7:8, :] * keep
    h6 = halo_ref[6:7, :] * keep
    m1 = jnp.where(rows == 0, h7, pltpu.roll(v, 1, 0))
    m2 = jnp.where(rows == 0, h6, jnp.where(rows == 1, h7, pltpu.roll(v, 2, 0)))
    return m1, m2


def _conv_act_fwd(up, cw, cb, name):
    S, F2 = up.shape
    Fh = F2 // 2
    T = _tile(S, ROW_T)
    tc = _tile(Fh, CONV_TC)
    ncol = Fh // tc
    hb = T // 8

    def body(ua_ref, ug_ref, ha_ref, hg_ref, wa_ref, wg_ref, ba_ref, bg_ref, o_ref, a_ref, g_ref):
        first = pl.program_id(0) == 0
        rows = lax.broadcasted_iota(jnp.int32, (T, tc), 0)

        def conv(u_ref, h_ref, w_ref, b_ref):
            v = u_ref[...]
            m1, m2 = _shift_down(v, h_ref, first, rows)
            return b_ref[...] + w_ref[0:1, :] * m2 + w_ref[1:2, :] * m1 + w_ref[2:3, :] * v

        a = conv(ua_ref, ha_ref, wa_ref, ba_ref)
        g = conv(ug_ref, hg_ref, wg_ref, bg_ref)
        a_ref[...] = a
        g_ref[...] = g
        o_ref[...] = (_gelu(g)[0] * a).astype(o_ref.dtype)

    halo = lambda off: pl.BlockSpec((8, tc), lambda i, j: (jnp.maximum(i * hb - 1, 0), j + off))
    blk = pl.BlockSpec((T, tc), lambda i, j: (i, j))
    f32 = jax.ShapeDtypeStruct((S, Fh), F32)
    return pl.pallas_call(
        body, name=name, out_shape=(jax.ShapeDtypeStruct((S, Fh), BF16), f32, f32), grid=(S // T, ncol),
        in_specs=[blk, pl.BlockSpec((T, tc), lambda i, j: (i, j + ncol)),
                  halo(0), halo(ncol),
                  pl.BlockSpec((3, tc), lambda i, j: (0, j)), pl.BlockSpec((3, tc), lambda i, j: (0, j + ncol)),
                  pl.BlockSpec((1, tc), lambda i, j: (0, j)), pl.BlockSpec((1, tc), lambda i, j: (0, j + ncol))],
        out_specs=(blk, blk, blk),
        compiler_params=_cparams(("parallel", "parallel")),
    )(up, up, up, up, cw, cw, cb, cb)


def _conv_act_bwd(up, a, g, dact, cw, name):
    S, F2 = up.shape
    Fh = F2 // 2
    T = _tile(S, ROW_T)
    tc = _tile(Fh, CONV_TC)
    ncol, nrow, hb, nhb = Fh // tc, S // T, T // 8, S // 8

    def body(ua_ref, ug_ref, a_ref, g_ref, an_ref, gn_ref, wa_ref, wg_ref, da_ref, dn_ref,
             dpa_ref, dpg_ref, dwa_ref, dwg_ref, dba_ref, dbg_ref, dua_n, dug_n):
        i = pl.program_id(1)
        rows = lax.broadcasted_iota(jnp.int32, (T, tc), 0)

        def du_of(a, g, dact_v):
            gel, t = _gelu(g)
            return dact_v * gel, dact_v * a * _gelu_grad(g, t)

        dua, dug = du_of(a_ref[...], g_ref[...], da_ref[...])
        keep = jnp.where(i == nrow - 1, 0.0, 1.0)
        dua_n[...], dug_n[...] = du_of(an_ref[...], gn_ref[...], dn_ref[...] * keep)

        @pl.when(i == 0)
        def _():
            for r in (dwa_ref, dwg_ref, dba_ref, dbg_ref):
                r[...] = jnp.zeros_like(r)

        for du, n_ref, u_ref, w_ref, o_ref, dw_ref, db_ref in ((dua, dua_n, ua_ref, wa_ref, dpa_ref, dwa_ref, dba_ref),
                                                               (dug, dug_n, ug_ref, wg_ref, dpg_ref, dwg_ref, dbg_ref)):
            n0, n1 = n_ref[0:1, :], n_ref[1:2, :]
            du1 = jnp.where(rows == T - 1, n0, pltpu.roll(du, T - 1, 0))
            du2 = jnp.where(rows == T - 2, n0, jnp.where(rows == T - 1, n1, pltpu.roll(du, T - 2, 0)))
            o_ref[...] = (w_ref[2:3, :] * du + w_ref[1:2, :] * du1 + w_ref[0:1, :] * du2).astype(o_ref.dtype)
            u = u_ref[...]
            db_ref[...] += jnp.sum(du, axis=0, keepdims=True)
            for j, d in enumerate((du2, du1, du)):
                dw_ref[j:j + 1, :] += jnp.sum(d * u, axis=0, keepdims=True)

    tile = lambda off: pl.BlockSpec((T, tc), lambda j, i: (i, j + off))
    under = pl.BlockSpec((8, tc), lambda j, i: (jnp.minimum((i + 1) * hb, nhb - 1), j))
    vec = lambda n, off: pl.BlockSpec((n, tc), lambda j, i: (0, j + off))
    act = jax.ShapeDtypeStruct((S, Fh), BF16)
    return pl.pallas_call(
        body, name=name,
        out_shape=(act, act, jax.ShapeDtypeStruct((3, Fh), F32), jax.ShapeDtypeStruct((3, Fh), F32),
                   jax.ShapeDtypeStruct((1, Fh), F32), jax.ShapeDtypeStruct((1, Fh), F32)),
        grid=(ncol, nrow),
        in_specs=[tile(0), tile(ncol), tile(0), tile(0), under, under, vec(3, 0), vec(3, ncol), tile(0), under],
        out_specs=(tile(0), tile(0), vec(3, 0), vec(3, 0), vec(1, 0), vec(1, 0)),
        scratch_shapes=[pltpu.VMEM((8, tc), F32), pltpu.VMEM((8, tc), F32)],
        compiler_params=_cparams(("parallel", "arbitrary")),
    )(up, up, a, g, a, g, cw, cw, dact, dact)


def _split3(x):
    hi = x.astype(BF16)
    r1 = x - hi.astype(F32)
    mid = r1.astype(BF16)
    lo = (r1 - mid.astype(F32)).astype(BF16)
    return hi, mid, lo


def _tri_dot(tri, x):
    hi, mid, lo = _split3(x)
    return _dot_nn(tri, hi) + _dot_nn(tri, mid) + _dot_nn(tri, lo)


def _log_sigmoid(x):
    return jnp.minimum(x, 0.0) - jnp.log(1.0 + jnp.exp(-jnp.abs(x)))


def _tri_mask(n, lower):
    r = lax.broadcasted_iota(jnp.int32, (n, n), 0)
    c = lax.broadcasted_iota(jnp.int32, (n, n), 1)
    return (r >= c) if lower else (r <= c)


def _gates_fwd(ps, bi, bf, bff, name):
    S = ps.shape[0]
    NC = S // MLC

    def body(ps_ref, bi_ref, bf_ref, bff_ref, a_ref, A_ref, wi_ref, em_ref, wk_ref, dec_ref, F_ref, m_scr, f_scr):
        @pl.when(pl.program_id(0) == 0)
        def _():
            m_scr[...] = jnp.zeros_like(m_scr)
            f_scr[...] = jnp.zeros_like(f_scr)

        rows = lax.broadcasted_iota(jnp.int32, (MLC, LANES), 0)
        ltri = _tri_mask(MLC, True).astype(BF16)
        li = GATE_CAP * jnp.tanh((ps_ref[:, 0:LANES] + bi_ref[...]) / GATE_CAP)
        lf = _log_sigmoid(GATE_CAP * jnp.tanh((ps_ref[:, LANES:2 * LANES] + bf_ref[...]) / GATE_CAP))
        b = _tri_dot(ltri, lf)
        a = li - b
        cm = a
        sh = 1
        while sh < MLC:
            cm = jnp.where(rows >= sh, jnp.maximum(cm, pltpu.roll(cm, sh, 0)), cm)
            sh *= 2
        m0 = m_scr[...]
        A = jnp.maximum(cm, m0)
        a_ref[...] = a
        A_ref[...] = A
        A_last = A_ref[MLC - 1:MLC, :]
        wi_ref[...] = jnp.exp(m0 - A)
        em_ref[...] = jnp.exp(-(b + A))
        wk_ref[...] = jnp.exp(a - A_last)
        dec_ref[0] = jnp.exp(m0 - A_last)
        F_ref[...] = b
        m_scr[...] = F_ref[MLC - 1:MLC, :] + A_last
        lfg = _log_sigmoid(ps_ref[:, 2 * LANES:3 * LANES] + bff_ref[...])
        F_ref[...] = _tri_dot(ltri, lfg) + f_scr[...]
        f_scr[...] = F_ref[MLC - 1:MLC, :]

    col = pl.BlockSpec((MLC, LANES), lambda c: (c, 0))
    vec = pl.BlockSpec((1, LANES), lambda c: (0, 0))
    cs = jax.ShapeDtypeStruct((S, LANES), F32)
    return pl.pallas_call(
        body, name=name,
        out_shape=(cs, cs, cs, cs, cs, jax.ShapeDtypeStruct((NC, 1, LANES), F32), cs),
        grid=(NC,), in_specs=[pl.BlockSpec((MLC, N_SMALL), lambda c: (c, 0)), vec, vec, vec],
        out_specs=(col, col, col, col, col, pl.BlockSpec((1, 1, LANES), lambda c: (c, 0, 0)), col),
        scratch_shapes=[pltpu.VMEM((1, LANES), F32), pltpu.VMEM((1, LANES), F32)],
        compiler_params=_cparams(("arbitrary",)),
    )(ps, bi, bf, bff)


def _gates_bwd(ps, bi, bf, bff, rk, kc, tch, dF, name):
    S = ps.shape[0]
    NC = S // MLC

    def body(ps_ref, bi_ref, bf_ref, bff_ref, rk_ref, kc_ref, t_ref, dF_ref, dps_ref, db_ref, carry):
        @pl.when(pl.program_id(0) == 0)
        def _():
            carry[...] = jnp.zeros_like(carry)
            db_ref[...] = jnp.zeros_like(db_ref)

        lanes = lax.broadcasted_iota(jnp.int32, (MLC, LANES), 1)
        utri = _tri_mask(MLC, False).astype(BF16)
        ti = jnp.tanh((ps_ref[:, 0:LANES] + bi_ref[...]) / GATE_CAP)
        t_end, t_start = t_ref[0, 0:1, :], t_ref[0, 1:2, :]
        rk = rk_ref[...]
        rk = rk - (jnp.sum(rk, axis=0, keepdims=True) - (t_start - t_end)) * (1.0 / MLC)
        dpi = jnp.where(lanes < ML_HEADS, (kc_ref[...] - rk) * (1.0 - ti * ti), 0.0)
        tf = jnp.tanh((ps_ref[:, LANES:2 * LANES] + bf_ref[...]) / GATE_CAP)
        dlf = _tri_dot(utri, rk) + t_end
        dpf = jnp.where(lanes < ML_HEADS, dlf * jax.nn.sigmoid(-GATE_CAP * tf) * (1.0 - tf * tf), 0.0)
        dFv = dF_ref[...]
        dlfg = _tri_dot(utri, dFv) + carry[...]
        carry[...] += jnp.sum(dFv, axis=0, keepdims=True)
        dpff = jnp.where(lanes < FOX_HEADS, dlfg * jax.nn.sigmoid(-(ps_ref[:, 2 * LANES:3 * LANES] + bff_ref[...])), 0.0)
        for n, dp in enumerate((dpi, dpf, dpff)):
            dps_ref[:, n * LANES:(n + 1) * LANES] = dp.astype(dps_ref.dtype)
            db_ref[:, n * LANES:(n + 1) * LANES] += jnp.sum(dp, axis=0, keepdims=True)

    rev = lambda c: (NC - 1 - c, 0)
    col = pl.BlockSpec((MLC, LANES), rev)
    vec = pl.BlockSpec((1, LANES), lambda c: (0, 0))
    wide = pl.BlockSpec((MLC, N_SMALL), rev)
    return pl.pallas_call(
        body, name=name,
        out_shape=(jax.ShapeDtypeStruct((S, N_SMALL), BF16), jax.ShapeDtypeStruct((1, N_SMALL), F32)),
        grid=(NC,),
        in_specs=[wide, vec, vec, vec, col, col, pl.BlockSpec((1, 2, LANES), lambda c: (NC - 1 - c, 0, 0)), col],
        out_specs=(wide, pl.BlockSpec((1, N_SMALL), lambda c: (0, 0))),
        scratch_shapes=[pltpu.VMEM((1, LANES), F32)],
        compiler_params=_cparams(("arbitrary",)),
    )(ps, bi, bf, bff, rk, kc, tch, dF)


_ML_SCALE = ML_DQK ** -0.5


def _ml_specs(rev, NC):
    idx = (lambda c: NC - 1 - c) if rev else (lambda c: c)
    qk = lambda blk: pl.BlockSpec((MLC, ML_HEADS * ML_DQK), lambda c: (idx(c), blk))
    wide = lambda blk: pl.BlockSpec((MLC, D_MODEL), lambda c: (idx(c), blk))
    col = pl.BlockSpec((MLC, LANES), lambda c: (idx(c), 0))
    return idx, qk, wide, col


def _ml_intra(q_ref, k_ref, arow_ref, A_ref, h):
    hs = slice(h * ML_DQK, (h + 1) * ML_DQK)
    qf = q_ref[:, hs] * _ML_SCALE
    kf = k_ref[:, hs]
    qb, kb = qf.astype(BF16), kf.astype(BF16)
    qk = _dot_nt(qb, kb)
    logw = arow_ref[h:h + 1, :] - A_ref[:, h:h + 1]
    W = jnp.exp(jnp.where(_tri_mask(MLC, True), logw, -1e30))
    return qb, kb, qf, kf, qk, W


def _mlstm_fwd(pm, a_row, A, wi, em, wk, dec, w_hn, name):
    S = pm.shape[0]
    NC = S // MLC
    _, qk, wide, col = _ml_specs(False, NC)

    def body(q_ref, k_ref, v_ref, o_ref, arow_ref, A_ref, wi_ref, em_ref, wk_ref, dec_ref, whn_ref,
             ha_ref, hp_ref, den_ref, cst_ref, nst_ref, C_scr, n_scr):
        @pl.when(pl.program_id(0) == 0)
        def _():
            C_scr[...] = jnp.zeros_like(C_scr)
            n_scr[...] = jnp.zeros_like(n_scr)

        lanes = lax.broadcasted_iota(jnp.int32, (MLC, LANES), 1)
        den_tile = jnp.zeros((MLC, LANES), F32)
        for h in range(ML_HEADS):
            vs = slice(h * ML_DV, (h + 1) * ML_DV)
            qb, kb, qf, kf, qk_, W = _ml_intra(q_ref, k_ref, arow_ref, A_ref, h)
            vb = v_ref[:, vs].astype(BF16)
            Cf = C_scr[h]
            Cb = Cf.astype(BF16)
            nrow = n_scr[h]
            cst_ref[0, h] = Cb
            nst_ref[0, h] = nrow
            s = qk_ * W
            wic = wi_ref[:, h:h + 1]
            num = _dot_nn(s.astype(BF16), vb) + wic * _dot_nt(qb, Cb)
            den = jnp.sum(s, axis=1, keepdims=True) + wic * jnp.sum(qf * nrow, axis=1, keepdims=True)
            hp = num / jnp.maximum(jnp.abs(den), em_ref[:, h:h + 1])
            hp_ref[:, vs] = hp
            den_tile = jnp.where(lanes == h, den, den_tile)
            hn = hp * _rstd(hp) * whn_ref[:, vs]
            ha_ref[:, vs] = (hn * jax.nn.sigmoid(o_ref[:, vs])).astype(ha_ref.dtype)
            wkc = wk_ref[:, h:h + 1]
            kw = kf * wkc
            d = dec_ref[0, :, h:h + 1]
            C_scr[h] = d * Cf + _dot_tn(vb, kw.astype(BF16))
            n_scr[h] = d * nrow + jnp.sum(kw, axis=0, keepdims=True)
        den_ref[...] = den_tile

    return pl.pallas_call(
        body, name=name,
        out_shape=(jax.ShapeDtypeStruct((S, D_MODEL), BF16), jax.ShapeDtypeStruct((S, D_MODEL), F32),
                   jax.ShapeDtypeStruct((S, LANES), F32),
                   jax.ShapeDtypeStruct((NC, ML_HEADS, ML_DV, ML_DQK), BF16),
                   jax.ShapeDtypeStruct((NC, ML_HEADS, 1, ML_DQK), F32)),
        grid=(NC,),
        in_specs=[qk(C_QM // 512), qk(C_KM // 512), wide(C_VM // D_MODEL), wide(C_OM // D_MODEL),
                  pl.BlockSpec((8, MLC), lambda c: (0, c)), col, col, col, col,
                  pl.BlockSpec((1, 1, LANES), lambda c: (c, 0, 0)), pl.BlockSpec((1, D_MODEL), lambda c: (0, 0))],
        out_specs=(pl.BlockSpec((MLC, D_MODEL), lambda c: (c, 0)), pl.BlockSpec((MLC, D_MODEL), lambda c: (c, 0)),
                   col, pl.BlockSpec((1, ML_HEADS, ML_DV, ML_DQK), lambda c: (c, 0, 0, 0)),
                   pl.BlockSpec((1, ML_HEADS, 1, ML_DQK), lambda c: (c, 0, 0, 0))),
        scratch_shapes=[pltpu.VMEM((ML_HEADS, ML_DV, ML_DQK), F32), pltpu.VMEM((ML_HEADS, 1, ML_DQK), F32)],
        compiler_params=_cparams(("arbitrary",)),
    )(pm, pm, pm, pm, a_row, A, wi, em, wk, dec, w_hn)


def _mlstm_bwd(dha, pm, hp_all, den_all, a_row, A, wi, em, wk, dec, cst, nst, w_hn, name):
    S = pm.shape[0]
    NC = S // MLC
    idx, qk, wide, col = _ml_specs(True, NC)

    def body(dha_ref, q_ref, k_ref, v_ref, o_ref, hp_ref, den_ref, arow_ref, A_ref, wi_ref, em_ref, wk_ref,
             dec_ref, cst_ref, nst_ref, whn_ref,
             dqk_ref, dv_ref, do_ref, rk_ref, kc_ref, t_ref, dwhn_ref, dC_scr, dn_scr, t_scr):
        @pl.when(pl.program_id(0) == 0)
        def _():
            dC_scr[...] = jnp.zeros_like(dC_scr)
            dn_scr[...] = jnp.zeros_like(dn_scr)
            t_scr[...] = jnp.zeros_like(t_scr)
            dwhn_ref[...] = jnp.zeros_like(dwhn_ref)

        lanes = lax.broadcasted_iota(jnp.int32, (MLC, LANES), 1)
        lane1 = lax.broadcasted_iota(jnp.int32, (1, LANES), 1)
        t_ref[0, 0:1, :] = t_scr[...]
        rk_tile = jnp.zeros((MLC, LANES), F32)
        kc_tile = jnp.zeros((MLC, LANES), F32)
        t_new = jnp.zeros((1, LANES), F32)
        for h in range(ML_HEADS):
            hs = slice(h * ML_DQK, (h + 1) * ML_DQK)
            vs = slice(h * ML_DV, (h + 1) * ML_DV)
            hp = hp_ref[:, vs]
            sig = jax.nn.sigmoid(o_ref[:, vs])
            whn = whn_ref[:, vs]
            r = _rstd(hp)
            dga = dha_ref[:, vs]
            do_ref[:, vs] = (dga * (hp * r * whn) * sig * (1.0 - sig)).astype(do_ref.dtype)
            dhn = dga * sig
            dhp, dwt = _rmsnorm_bwd_math(dhn, hp, whn)
            dwhn_ref[:, vs] += jnp.sum(dwt, axis=0, keepdims=True)
            den = den_ref[:, h:h + 1]
            floor = em_ref[:, h:h + 1]
            D = jnp.maximum(jnp.abs(den), floor)
            dnum = dhp / D
            dh_h = jnp.sum(dhp * hp, axis=1, keepdims=True)
            active = jnp.abs(den) >= floor
            dden = -dh_h / D * jnp.where(active, jnp.sign(den), 0.0)
            phi = jnp.where(active, 0.0, dh_h)
            qb, kb, qf, kf, qk_, W = _ml_intra(q_ref, k_ref, arow_ref, A_ref, h)
            vf = v_ref[:, vs]
            vb = vf.astype(BF16)
            Cb = cst_ref[0, h]
            nrow = nst_ref[0, h]
            wic = wi_ref[:, h:h + 1]
            wkc = wk_ref[:, h:h + 1]
            d = dec_ref[0, :, h:h + 1]
            dCn = dC_scr[h]
            dCb = dCn.astype(BF16)
            dnn = dn_scr[h]
            dnumb = dnum.astype(BF16)
            s = qk_ * W
            ds = (_dot_nt(dnumb, vb) + dden) * W
            dsb = ds.astype(BF16)
            dnw = (wic * dnum).astype(BF16)
            wd = wic * dden
            kw = kf * wkc
            dv_state = _dot_nt(kw.astype(BF16), dCb)
            dq = _dot_nn(dsb, kb) + _dot_nn(dnw, Cb) + wd * nrow
            dk_state = wkc * (_dot_nn(vb, dCb) + dnn)
            dk = _dot_tn(dsb, qb) + dk_state
            dv = _dot_tn(s.astype(BF16), dnumb) + dv_state
            dC = d * dCn + _dot_tn(dnw, qb)
            dn = d * dnn + jnp.sum(wd * qf, axis=0, keepdims=True)
            dC_scr[h] = dC
            dn_scr[h] = dn
            dqk_ref[:, hs] = (dq * _ML_SCALE).astype(dqk_ref.dtype)
            dqk_ref[:, C_KM + h * ML_DQK:C_KM + (h + 1) * ML_DQK] = dk.astype(dqk_ref.dtype)
            dv_ref[:, vs] = dv.astype(dv_ref.dtype)
            G = ds * qk_
            inter = _dot_nt(qb, Cb)
            qn = jnp.sum(qf * nrow, axis=1, keepdims=True)
            R = (jnp.sum(G, axis=1, keepdims=True)
                 + wic * (jnp.sum(dnum * inter, axis=1, keepdims=True) + dden * qn))
            K = jnp.sum(G.T, axis=1, keepdims=True) + jnp.sum(kf * dk_state, axis=1, keepdims=True)
            rk_tile = jnp.where(lanes == h, R - K, rk_tile)
            kc_tile = jnp.where(lanes == h, phi, kc_tile)
            tt = (jnp.sum(jnp.sum(dC * Cb.astype(F32), axis=1, keepdims=True), axis=0, keepdims=True)
                  + jnp.sum(dn * nrow, axis=1, keepdims=True))
            t_new = jnp.where(lane1 == h, tt, t_new)
        rk_ref[...] = rk_tile
        kc_ref[...] = kc_tile
        t_ref[0, 1:2, :] = t_new
        t_scr[...] = t_new

    act = lambda n: jax.ShapeDtypeStruct((S, n), BF16)
    cs = jax.ShapeDtypeStruct((S, LANES), F32)
    rowblk = lambda n: pl.BlockSpec((MLC, n), lambda c: (idx(c), 0))
    return pl.pallas_call(
        body, name=name,
        out_shape=(act(D_MODEL), act(D_MODEL), act(D_MODEL), cs, cs,
                   jax.ShapeDtypeStruct((NC, 2, LANES), F32), jax.ShapeDtypeStruct((1, D_MODEL), F32)),
        grid=(NC,),
        in_specs=[rowblk(D_MODEL), qk(C_QM // 512), qk(C_KM // 512), wide(C_VM // D_MODEL), wide(C_OM // D_MODEL),
                  rowblk(D_MODEL), col, pl.BlockSpec((8, MLC), lambda c: (0, idx(c))), col, col, col, col,
                  pl.BlockSpec((1, 1, LANES), lambda c: (idx(c), 0, 0)),
                  pl.BlockSpec((1, ML_HEADS, ML_DV, ML_DQK), lambda c: (idx(c), 0, 0, 0)),
                  pl.BlockSpec((1, ML_HEADS, 1, ML_DQK), lambda c: (idx(c), 0, 0, 0)),
                  pl.BlockSpec((1, D_MODEL), lambda c: (0, 0))],
        out_specs=(rowblk(D_MODEL), rowblk(D_MODEL), rowblk(D_MODEL), col, col,
                   pl.BlockSpec((1, 2, LANES), lambda c: (idx(c), 0, 0)), pl.BlockSpec((1, D_MODEL), lambda c: (0, 0))),
        scratch_shapes=[pltpu.VMEM((ML_HEADS, ML_DV, ML_DQK), F32), pltpu.VMEM((ML_HEADS, 1, ML_DQK), F32),
                        pltpu.VMEM((1, LANES), F32)],
        compiler_params=_cparams(("arbitrary",)),
    )(dha, pm, pm, pm, pm, hp_all, den_all, a_row, A, wi, em, wk, dec, cst, nst, w_hn)


_FOX_SCALE = FOX_DH ** -0.5
_NEG = -1e30
_LOG2E = 1.4426950408889634
_LN2 = 0.6931471805599453
_QF_BLK, _KF_BLK, _VF_BLK = 0, FOX_HEADS, 2 * FOX_HEADS


def _lane_pick(tile, lane):
    lanes = lax.broadcasted_iota(jnp.int32, tile.shape, 1)
    return jnp.sum(jnp.where(lanes == lane, tile, 0.0), axis=1, keepdims=True)


def _col_to_row(col):
    return jnp.max(jnp.broadcast_to(col, (col.shape[0], LANES)).T, axis=0, keepdims=True)


def _causal(q0, k0, shape, q_axis):
    qpos = q0 + lax.broadcasted_iota(jnp.int32, shape, q_axis)
    kpos = k0 + lax.broadcasted_iota(jnp.int32, shape, 1 - q_axis)
    return kpos <= qpos


def _fox_fwd(pf, fc, fk_row, name):
    S = pf.shape[0]
    TQ, TK = FOX_TQ_FWD, FOX_TK_FWD
    nq, nk = S // TQ, S // TK
    c1 = _FOX_SCALE * _LOG2E

    def body(q_ref, k_ref, v_ref, fc_ref, fr_ref, o_ref, lse_ref):
        h, i = pl.program_id(0), pl.program_id(1)
        qb = q_ref[...]
        fq2 = _lane_pick(fc_ref[...], h) * _LOG2E

        def step(j, carry, masked):
            m, l, acc = carry
            off = pl.multiple_of(j * TK, TK)
            t = _dot_nt(qb, k_ref[pl.ds(off, TK), :]) * c1 - fr_ref[0, j] * _LOG2E
            if masked:
                t = jnp.where(_causal(i * TQ, j * TK, (TQ, TK), 0), t, _NEG)
            m_new = jnp.maximum(m, jnp.max(t, axis=1, keepdims=True) + fq2)
            alpha = jnp.exp2(m - m_new)
            p = jnp.exp2(t + (fq2 - m_new))
            l = alpha * l + jnp.sum(p, axis=1, keepdims=True)
            acc = alpha * acc + _dot_nn(p.astype(BF16), v_ref[pl.ds(off, TK), :])
            return m_new, l, acc

        init = (jnp.full((TQ, 1), _NEG, F32), jnp.zeros((TQ, 1), F32), jnp.zeros((TQ, FOX_DH), F32))
        last = (i * TQ) // TK
        carry = lax.fori_loop(0, last, lambda j, c: step(j, c, False), init)
        for d in range(TQ // TK):
            carry = step(last + d, carry, True)
        m, l, acc = carry
        o_ref[...] = (acc / l).astype(o_ref.dtype)
        lse_ref[0, 0] = _col_to_row((m + jnp.log2(l)) * _LN2)

    head = lambda blk: pl.BlockSpec((S, FOX_DH), lambda h, i: (0, blk + h))
    return pl.pallas_call(
        body, name=name,
        out_shape=(jax.ShapeDtypeStruct((S, D_MODEL), BF16), jax.ShapeDtypeStruct((FOX_HEADS, nq, 1, TQ), F32)),
        grid=(FOX_HEADS, nq),
        in_specs=[pl.BlockSpec((TQ, FOX_DH), lambda h, i: (i, _QF_BLK + h)), head(_KF_BLK), head(_VF_BLK),
                  pl.BlockSpec((TQ, LANES), lambda h, i: (i, 0)),
                  pl.BlockSpec((1, nk, 1, TK), lambda h, i: (h, 0, 0, 0))],
        out_specs=(pl.BlockSpec((TQ, FOX_DH), lambda h, i: (i, h)),
                   pl.BlockSpec((1, 1, 1, TQ), lambda h, i: (h, i, 0, 0))),
        compiler_params=_cparams(("parallel", "arbitrary")),
    )(pf, pf, pf, fc, fk_row)


def _fox_bwd(dhb, hb, pf, lse_row, fq_row, fc, name):
    S = pf.shape[0]
    TQ, TK = FOX_TQ, FOX_TK
    nq, nk, r = S // TQ, S // TK, TK // TQ
    c1 = _FOX_SCALE * _LOG2E

    def body(q_ref, k_ref, v_ref, do_ref, o_ref, lse_ref, fq_ref, fc_ref,
             dq_ref, dk_ref, dv_ref, dFk_ref, dFq_ref, dq_acc, qside, delta, dk_acc, dv_acc, cs_acc):
        h, j = pl.program_id(0), pl.program_id(1)

        @pl.when(j == 0)
        def _():
            dq_acc[...] = jnp.zeros_like(dq_acc)
            dFq_ref[...] = jnp.zeros_like(dFq_ref)

            def fill(b, _):
                off = pl.multiple_of(b * TQ, TQ)
                prod = do_ref[pl.ds(off, TQ), :].astype(F32) * o_ref[pl.ds(off, TQ), :].astype(F32)
                delta[b] = jnp.sum(prod.T, axis=0, keepdims=True)
                qside[b] = (fq_ref[0, b] - lse_ref[0, b]) * _LOG2E
                return 0

            lax.fori_loop(0, nq, fill, 0)

        kb = k_ref[...]
        vb = v_ref[...]
        fk2 = _lane_pick(fc_ref[...], h) * _LOG2E
        dk_acc[...] = jnp.zeros_like(dk_acc)
        dv_acc[...] = jnp.zeros_like(dv_acc)
        cs_acc[...] = jnp.zeros_like(cs_acc)

        def step(i, masked):
            off = pl.multiple_of(i * TQ, TQ)
            qb = q_ref[pl.ds(off, TQ), :]
            dob = do_ref[pl.ds(off, TQ), :]
            t = _dot_nt(kb, qb) * c1 + qside[i] - fk2
            if masked:
                t = jnp.where(_causal(i * TQ, j * TK, (TK, TQ), 1), t, _NEG)
            p = jnp.exp2(t)
            dv_acc[...] += _dot_nn(p.astype(BF16), dob)
            ds = p * (_dot_nt(vb, dob) - delta[i])
            dsb = ds.astype(BF16)
            dk_acc[...] += _dot_nn(dsb, qb)
            dq_acc[pl.ds(off, TQ), :] += _dot_tn(dsb, kb)
            cs_acc[...] += jnp.sum(ds, axis=1, keepdims=True)
            dFq_ref[0, i] += jnp.sum(ds, axis=0, keepdims=True)

        for d in range(r):
            step(r * j + d, True)

        def rest(i, _):
            step(i, False)
            return 0

        lax.fori_loop(r * j + r, nq, rest, 0)
        dk_ref[...] = (dk_acc[...] * _FOX_SCALE).astype(dk_ref.dtype)
        dv_ref[...] = dv_acc[...].astype(dv_ref.dtype)
        dFk_ref[0, 0] = -_col_to_row(cs_acc[...])

        @pl.when(j == nk - 1)
        def _():
            dq_ref[...] = (dq_acc[...] * _FOX_SCALE).astype(dq_ref.dtype)

    head = lambda blk: pl.BlockSpec((S, FOX_DH), lambda h, j: (0, blk + h))
    kblk = lambda blk: pl.BlockSpec((TK, FOX_DH), lambda h, j: (j, blk + h))
    qrows = pl.BlockSpec((1, nq, 1, TQ), lambda h, j: (h, 0, 0, 0))
    act = jax.ShapeDtypeStruct((S, D_MODEL), BF16)
    return pl.pallas_call(
        body, name=name,
        out_shape=(act, act, act, jax.ShapeDtypeStruct((FOX_HEADS, nk, 1, TK), F32),
                   jax.ShapeDtypeStruct((FOX_HEADS, nq, 1, TQ), F32)),
        grid=(FOX_HEADS, nk),
        in_specs=[head(_QF_BLK), kblk(_KF_BLK), kblk(_VF_BLK), head(0), head(0), qrows, qrows,
                  pl.BlockSpec((TK, LANES), lambda h, j: (j, 0))],
        out_specs=(head(0), kblk(0), kblk(0), pl.BlockSpec((1, 1, 1, TK), lambda h, j: (h, j, 0, 0)), qrows),
        scratch_shapes=[pltpu.VMEM((S, FOX_DH), F32), pltpu.VMEM((nq, 1, TQ), F32), pltpu.VMEM((nq, 1, TQ), F32),
                        pltpu.VMEM((TK, FOX_DH), F32), pltpu.VMEM((TK, FOX_DH), F32), pltpu.VMEM((TK, 1), F32)],
        compiler_params=_cparams(("parallel", "arbitrary")),
    )(pf, pf, pf, dhb, hb, lse_row, fq_row, fc)


def _pad_lanes(v):
    return jnp.pad(v, ((0, 0), (0, LANES - v.shape[1])))


def _local_step(x, target, wmain_t, wsmall_t, rest_arrived, rest_weights, p, on_grads, advance, token):
    S = x.shape[0]
    bi, bf, bff = _pad_lanes(p["b_ml_i"]), _pad_lanes(p["b_ml_f"]), _pad_lanes(p["b_fox_f"])

    h0 = _rmsnorm_fwd(x, p["norm_mix_pre"] + token[0:1, 0:1], "norm_mix_pre")
    pm = _mm(h0, wmain_t[:N_ML], "nt", F32, "proj_mlstm")
    pf = _mm(h0, wmain_t[N_ML:N_ML + N_FOX], "nt", BF16, "proj_fox")
    pg = _mm(h0, wmain_t[N_ML + N_FOX:], "nt", F32, "proj_merge")
    ps = _mm(h0, wsmall_t, "nt", F32, "proj_gates")
    a, A, wi, em, wk, dec, Fc = _gates_fwd(ps, bi, bf, bff, "gates_fwd")
    a_row = a[:, :8].T
    ha, hp, den, cst, nst = _mlstm_fwd(pm, a_row, A, wi, em, wk, dec, p["ml_head_norm"], "mlstm_fwd")
    ft = Fc[:, :FOX_HEADS].T + rest_arrived(ha)[0, 0]
    fq_row = ft.reshape(FOX_HEADS, S // FOX_TQ, 1, FOX_TQ)
    fk_row = ft.reshape(FOX_HEADS, S // FOX_TK, 1, FOX_TK)
    hb, lse_row = _fox_fwd(pf, Fc, ft.reshape(FOX_HEADS, S // FOX_TK_FWD, 1, FOX_TK_FWD), "fox_fwd")
    wa, wb, wout, wup, wdown = rest_weights(hb)
    ya = _mm(ha, wa, "nn", F32, "branch_a")
    yb = _mm(hb, wb, "nn", F32, "branch_b")
    merged = _merge_fwd(ya, yb, pg, p["b_gate_a"], p["b_gate_b"], "merge_fwd")
    z = _mm(merged, wout, "nn", F32, "out_proj")
    x1 = _resid_norm_fwd(x, z, p["norm_mix_post"], "resid_mix")
    h2 = _rmsnorm_fwd(x1, p["norm_ffn_pre"], "norm_ffn_pre")
    up = _mm(h2, wup, "nn", F32, "ffn_up")
    act, conv_a, conv_g = _conv_act_fwd(up, p["conv_w"], p["conv_b"], "conv_act_fwd")
    d = _mm(act, wdown, "nn", F32, "ffn_down")
    loss_row, dy, dd, g_norm_ffn_post = _loss_head(x1, d, p["norm_ffn_post"], target, "loss_head")
    dact = _mm(dd, wdown, "nt", F32, "d_act")
    g_wdown = _mm(act, dd, "tn", F32, "dw_down", tm=1408)
    dupa, dupg, dcwa, dcwg, dcba, dcbg = _conv_act_bwd(up, conv_a, conv_g, dact, p["conv_w"], "conv_act_bwd")
    g_conv_w = jnp.concatenate([dcwa, dcwg], axis=1)
    g_conv_b = jnp.concatenate([dcba, dcbg], axis=1)
    dh2 = _mm([dupa, dupg], wup, "nt", F32, "d_h2")
    g_wup = _mm(h2, [dupa, dupg], "tn", F32, "dw_up")
    token = on_grads("ffn", dict(w_up=g_wup, w_down=g_wdown))
    dx1, g_norm_ffn_pre = _rmsnorm_bwd([dh2], x1, p["norm_ffn_pre"] + token[0:1, 0:1], dy, F32, "norm_ffn_pre_bwd")
    dz, g_norm_mix_post = _rmsnorm_bwd([dx1], z, p["norm_mix_post"], None, BF16, "norm_mix_post_bwd")
    dmerged = _mm(dz, wout, "nt", F32, "d_merged")
    g_wout = _mm(merged, dz, "tn", F32, "dw_out")
    dya, dyb, dga, dgb, g_b_gate_a, g_b_gate_b = _merge_bwd(dmerged, ya, yb, pg, p["b_gate_a"], p["b_gate_b"], "merge_bwd")
    dha = _mm(dya, wa, "nt", F32, "d_ha")
    g_wa = _mm(ha, dya, "tn", F32, "dw_a")
    dhb = _mm(dyb, wb, "nt", BF16, "d_hb")
    g_wb = _mm(hb, dyb, "tn", F32, "dw_b")
    token = advance("ffn", g_wb) + on_grads("mix", dict(w_out=g_wout, w_branch_a=g_wa, w_branch_b=g_wb))
    dqkm, dvm, dom, rk, kc, tch, g_ml_head_norm = _mlstm_bwd(
        dha, pm, hp, den, a_row, A, wi, em, wk, dec, cst, nst, p["ml_head_norm"] + token[0:1, 0:1], "mlstm_bwd")
    token = advance("mix", dqkm)
    dqf, dkf, dvf, dFk, dFq = _fox_bwd(dhb, hb, pf, lse_row.reshape(fq_row.shape), fq_row + token[0, 0], Fc, "fox_bwd")
    dF = jnp.pad((dFk.reshape(FOX_HEADS, S) + dFq.reshape(FOX_HEADS, S)).T, ((0, 0), (0, LANES - FOX_HEADS)))
    dps, dbias = _gates_bwd(ps, bi, bf, bff, rk, kc, tch, dF, "gates_bwd")
    dpm = [dqkm, dvm, dom, dqf, dkf, dvf, dga, dgb]
    g_wmain_t = _mm(dpm, h0, "tn", F32, "dw_main")
    token = on_grads("in", dict(w_in=g_wmain_t))
    g_wsmall_t = _mm(dps, h0, "tn", F32, "dw_gates")
    dh0s = _mm(dps, wsmall_t + token[0:1, 0:1].astype(BF16), "nn", F32, "d_h0_gates")
    token = advance("in", dh0s)
    dh0 = _mm(dpm, wmain_t, "nn", F32, "d_h0_main", after=token)
    grad_x, g_norm_mix_pre = _rmsnorm_bwd([dh0, dh0s], x, p["norm_mix_pre"], dx1, F32, "norm_mix_pre_bwd")

    big = dict(wsmall_t=g_wsmall_t)
    small = dict(norm_mix_pre=g_norm_mix_pre, ml_head_norm=g_ml_head_norm, b_gate_a=g_b_gate_a, b_gate_b=g_b_gate_b,
                 norm_mix_post=g_norm_mix_post, norm_ffn_pre=g_norm_ffn_pre, norm_ffn_post=g_norm_ffn_post,
                 conv_b=g_conv_b, b_ml_i=dbias[:, 0:ML_HEADS], b_ml_f=dbias[:, LANES:LANES + ML_HEADS],
                 b_fox_f=dbias[:, 2 * LANES:2 * LANES + FOX_HEADS], conv_w=g_conv_w)
    return loss_row, grad_x, big, small


def _row_tile(r, target=256):
    best = None
    for t in range(8, min(r, target) + 1, 8):
        if r % t == 0:
            best = t
    return best if best is not None else r


def _adamw(w, g, m, v, name):
    _, R, C = w.shape
    tr = _row_tile(R)
    tc = C
    if tr == R and R > 256:
        tc = 256

    def body(w_ref, g_ref, m_ref, v_ref, d_ref, mo_ref, vo_ref):
        gv = g_ref[...]
        mn = ADAM_B1 * m_ref[0] + (1.0 - ADAM_B1) * gv
        vn = ADAM_B2 * v_ref[0] + (1.0 - ADAM_B2) * (gv * gv)
        m_hat = mn / (1.0 - ADAM_B1 ** ADAM_STEP)
        v_hat = vn / (1.0 - ADAM_B2 ** ADAM_STEP)
        d_ref[0] = -ADAM_LR * (m_hat / (jnp.sqrt(v_hat) + ADAM_EPS) + ADAM_WD * w_ref[0])
        mo_ref[0] = mn
        vo_ref[0] = vn

    blk = pl.BlockSpec((1, tr, tc), lambda i, j: (0, i, j))
    o = jax.ShapeDtypeStruct((1, R, C), F32)
    return pl.pallas_call(
        body, name=name, out_shape=(o, o, o), grid=(R // tr, C // tc),
        in_specs=[blk, pl.BlockSpec((tr, tc), lambda i, j: (i, j)), blk, blk], out_specs=(blk,) * 3,
        compiler_params=_cparams(("parallel", "parallel")),
    )(w, g, m, v)


ANY = pl.BlockSpec(memory_space=pl.ANY)


def _place():
    x, y, c = lax.axis_index("x"), lax.axis_index("y"), lax.axis_index("c")
    chips = [(1 - x, y), (x, 1 - y), (1 - x, 1 - y)]
    return x, y, c, chips


def _block(ref, kind, k, rows=None):
    if kind == "rows":
        return ref.at[k] if rows is None else ref.at[k, pl.ds(*rows), :]
    cb = ref.shape[1] // 4
    return ref.at[:, pl.ds(k * cb, cb)] if rows is None else ref.at[pl.ds(*rows), pl.ds(k * cb, cb)]


def _gathered_shape(s, kind):
    return (4,) + s.shape if kind == "rows" else (s.shape[0], 4 * s.shape[1])


def _gather_weights(shards, kinds, smalls):
    n, ns = len(shards), len(smalls)

    def body(*refs):
        ins, sm_in = refs[:n], refs[n:n + ns]
        outs, sm_out = refs[n + ns:2 * n + ns], refs[2 * n + ns:2 * (n + ns)]
        send_sems, recv_sems, sm_send, sm_recv, local_sems = refs[2 * (n + ns):]
        x, y, c, chips = _place()
        sibling = (x, y, 1 - c)
        kme = 2 * x + y

        def half(a, k, hc):
            h = ins[a].shape[0] // 2
            return _block(outs[a], kinds[a], k, (hc * h, h))

        def remote(a, slot, src, dst, to):
            return pltpu.make_async_remote_copy(src_ref=src, dst_ref=dst, send_sem=send_sems.at[a * 7 + slot],
                                                recv_sem=recv_sems.at[a * 7 + slot], device_id=to, device_id_type=MESH)

        def sm_copy(b, j, k, to):
            return pltpu.make_async_remote_copy(src_ref=sm_in[b], dst_ref=sm_out[b].at[k], send_sem=sm_send.at[3 * b + j],
                                                recv_sem=sm_recv.at[3 * b + j], device_id=to, device_id_type=MESH)

        local = [pltpu.make_async_copy(sm_in[b], sm_out[b].at[kme], local_sems.at[b]) for b in range(ns)]
        for cp in local:
            cp.start()
        sends = [remote(a, 6, ins[a], _block(outs[a], kinds[a], kme), sibling) for a in range(n)]
        for a in range(n):
            h = ins[a].shape[0] // 2
            for j, chip in enumerate(chips):
                sends.append(remote(a, j, ins[a].at[pl.ds(c * h, h), :], half(a, kme, c), (*chip, c)))
        for b in range(ns):
            for j, chip in enumerate(chips):
                sends.append(sm_copy(b, j, kme, (*chip, c)))
        for cp in sends:
            cp.start()
        for a in range(n):
            for j, chip in enumerate(chips):
                kj = 2 * chip[0] + chip[1]
                remote(a, j, half(a, kj, c), half(a, kj, c), (*chip, c)).wait_recv()
                fwd = remote(a, 3 + j, half(a, kj, c), half(a, kj, c), sibling)
                fwd.start()
                sends.append(fwd)
        for a in range(n):
            for j, chip in enumerate(chips):
                kj = 2 * chip[0] + chip[1]
                remote(a, 3 + j, half(a, kj, 1 - c), half(a, kj, 1 - c), sibling).wait_recv()
        for b in range(ns):
            for j, chip in enumerate(chips):
                sm_copy(b, j, 2 * chip[0] + chip[1], (*chip, c)).wait_recv()
        for a in range(n):
            remote(a, 6, ins[a], _block(outs[a], kinds[a], kme), sibling).wait_recv()
        for cp in sends:
            cp.wait_send()
        for cp in local:
            cp.wait()

    outs = pl.pallas_call(
        body, name="gather_weights",
        out_shape=tuple([jax.ShapeDtypeStruct(_gathered_shape(s, k), s.dtype) for s, k in zip(shards, kinds)]
                        + [jax.ShapeDtypeStruct((4,) + s.shape, s.dtype) for s in smalls]),
        in_specs=[ANY] * (n + ns), out_specs=tuple([ANY] * (n + ns)),
        scratch_shapes=[pltpu.SemaphoreType.DMA((7 * n,)), pltpu.SemaphoreType.DMA((7 * n,)),
                        pltpu.SemaphoreType.DMA((3 * ns,)), pltpu.SemaphoreType.DMA((3 * ns,)),
                        pltpu.SemaphoreType.DMA((ns,))],
    )(*shards, *smalls)
    return outs[:n], outs[n:]


_IN_HBM = pl.BlockSpec(memory_space=pltpu.HBM)
_SEMS = pl.BlockSpec(memory_space=pltpu.SEMAPHORE)
_DATAFLOW = pltpu.SideEffectType.DATAFLOW_SIDE_EFFECTING


def _hbm(t):
    return pltpu.HBM(t.shape, t.dtype)


def _gather_copies(ins, outs, send_sems, recv_sems, kinds):
    x, y, c, chips = _place()
    kme = 2 * x + y
    cps = []
    for a in range(len(ins)):
        h = ins[a].shape[0] // 2
        for j, chip in enumerate(chips + [None]):
            to = (x, y, 1 - c) if chip is None else (*chip, c)
            src = ins[a] if chip is None else ins[a].at[pl.ds(c * h, h), :]
            dst = _block(outs[a], kinds[a], kme, None if chip is None else (c * h, h))
            cps.append(pltpu.make_async_remote_copy(src_ref=src, dst_ref=dst, send_sem=send_sems.at[4 * a + j],
                                                    recv_sem=recv_sems.at[4 * a + j], device_id=to, device_id_type=MESH))
    return cps


def _gather_start(shards, kinds, name):
    n = len(shards)
    outs = [lax.empty(_gathered_shape(s, k), s.dtype) for s, k in zip(shards, kinds)]

    def body(*refs):
        for cp in _gather_copies(refs[:n], refs[n:2 * n], refs[2 * n], refs[2 * n + 1], kinds):
            cp.start()
        refs[-1][...] = jnp.zeros_like(refs[-1])

    return pl.pallas_call(
        body, name=name,
        out_shape=(pltpu.SemaphoreType.DMA((4 * n,)), pltpu.SemaphoreType.DMA((4 * n,)),
                   *[_hbm(t) for t in shards], *[_hbm(t) for t in outs], jax.ShapeDtypeStruct((8, LANES), F32)),
        in_specs=[_IN_HBM] * (2 * n),
        out_specs=(_SEMS, _SEMS, *[_IN_HBM] * (2 * n), pl.BlockSpec(memory_space=pltpu.VMEM)),
        input_output_aliases={a: 2 + a for a in range(2 * n)},
        compiler_params=pltpu.CompilerParams(has_side_effects=_DATAFLOW),
    )(*[pltpu.with_memory_space_constraint(t, pltpu.HBM) for t in list(shards) + outs])


def _gather_wait(started, after, kinds, name):
    n = (len(started) - 3) // 2
    bufs = started[2:2 + 2 * n]

    def body(*refs):
        for cp in _gather_copies(refs[:n], refs[n:2 * n], refs[2 * n], refs[2 * n + 1], kinds):
            cp.wait_send()
            cp.wait_recv()

    outs = pl.pallas_call(
        body, name=name, out_shape=tuple(_hbm(t) for t in bufs),
        in_specs=[_IN_HBM] * (2 * n) + [_SEMS, _SEMS, ANY], out_specs=tuple([_IN_HBM] * (2 * n)),
        input_output_aliases={a: a for a in range(2 * n)},
        compiler_params=pltpu.CompilerParams(has_side_effects=_DATAFLOW),
    )(*bufs, started[0], started[1], after)
    return outs[n:]


def _relay_copies(bufs, send_sems, recv_sems, kinds):
    x, y, c, chips = _place()
    cps = []
    for a in range(len(bufs)):
        h = (bufs[a].shape[1] if kinds[a] == "rows" else bufs[a].shape[0]) // 2
        for j, chip in enumerate(chips):
            part = _block(bufs[a], kinds[a], 2 * chip[0] + chip[1], (c * h, h))
            cps.append(pltpu.make_async_remote_copy(src_ref=part, dst_ref=part, send_sem=send_sems.at[3 * a + j],
                                                    recv_sem=recv_sems.at[3 * a + j], device_id=(x, y, 1 - c),
                                                    device_id_type=MESH))
    return cps


def _relay_start(bufs, kinds, name):
    n = len(bufs)

    def body(*refs):
        for cp in _relay_copies(refs[:n], refs[n], refs[n + 1], kinds):
            cp.start()
        refs[-1][...] = jnp.zeros_like(refs[-1])

    return pl.pallas_call(
        body, name=name,
        out_shape=(pltpu.SemaphoreType.DMA((3 * n,)), pltpu.SemaphoreType.DMA((3 * n,)), *[_hbm(t) for t in bufs],
                   jax.ShapeDtypeStruct((8, LANES), F32)),
        in_specs=[_IN_HBM] * n, out_specs=(_SEMS, _SEMS, *[_IN_HBM] * n, pl.BlockSpec(memory_space=pltpu.VMEM)),
        input_output_aliases={a: 2 + a for a in range(n)},
        compiler_params=pltpu.CompilerParams(has_side_effects=_DATAFLOW),
    )(*[pltpu.with_memory_space_constraint(t, pltpu.HBM) for t in bufs])


def _relay_wait(started, after, kinds, name):
    n = len(started) - 3
    bufs = started[2:2 + n]

    def body(*refs):
        for cp in _relay_copies(refs[:n], refs[n], refs[n + 1], kinds):
            cp.wait_send()
            cp.wait_recv()

    return pl.pallas_call(
        body, name=name, out_shape=tuple(_hbm(t) for t in bufs),
        in_specs=[_IN_HBM] * n + [_SEMS, _SEMS, ANY], out_specs=tuple([_IN_HBM] * n),
        input_output_aliases={a: a for a in range(n)},
        compiler_params=pltpu.CompilerParams(has_side_effects=_DATAFLOW),
    )(*bufs, started[0], started[1], after)


def _add_halves(g, r1, cvec, kind, name):
    def body(c_ref, g_ref, r_ref, o_ref):
        o_ref[...] = (g_ref[...] + r_ref[...]).astype(o_ref.dtype)

    if kind == "rows":
        _, h, C = r1.shape
        tr = _row_tile(h, 512)
        nt = h // tr
        grid = (4, nt)
        g_spec = pl.BlockSpec((1, tr, C), lambda k, i, c_ref: (k, c_ref[0] * nt + i, 0))
        r_spec = pl.BlockSpec((1, tr, C), lambda k, i, c_ref: (k, i, 0))
    else:
        h, C4 = r1.shape
        tr, tc = _row_tile(h, 512), C4 // 4
        nt = h // tr
        grid = (nt, 4)
        g_spec = pl.BlockSpec((tr, tc), lambda i, k, c_ref: (c_ref[0] * nt + i, k))
        r_spec = pl.BlockSpec((tr, tc), lambda i, k, c_ref: (i, k))
    return pl.pallas_call(
        body, name=name, out_shape=jax.ShapeDtypeStruct(r1.shape, BF16),
        grid_spec=pltpu.PrefetchScalarGridSpec(num_scalar_prefetch=1, grid=grid, in_specs=[g_spec, r_spec],
                                               out_specs=r_spec),
        compiler_params=_cparams(("parallel", "parallel")),
    )(cvec, g, r1)


def _chip_copies(ins, lands, send_sems, recv_sems, kinds):
    x, y, c, chips = _place()
    return [pltpu.make_async_remote_copy(
        src_ref=_block(ins[a], kinds[a], 2 * chip[0] + chip[1]), dst_ref=lands[a].at[j],
        send_sem=send_sems.at[3 * a + j], recv_sem=recv_sems.at[3 * a + j], device_id=(*chip, c), device_id_type=MESH)
        for a in range(len(ins)) for j, chip in enumerate(chips)]


def _land_shape(s, kind):
    return (3,) + (s.shape[1:] if kind == "rows" else (s.shape[0], s.shape[1] // 4))


def _sibling_copies(ins, lands, send_sems, recv_sems, kinds):
    x, y, c, _ = _place()
    cps = []
    for a in range(len(ins)):
        h = lands[a].shape[-2]
        src = ins[a].at[:, pl.ds((1 - c) * h, h), :] if kinds[a] == "rows" else ins[a].at[pl.ds((1 - c) * h, h), :]
        cps.append(pltpu.make_async_remote_copy(src_ref=src, dst_ref=lands[a], send_sem=send_sems.at[a],
                                                recv_sem=recv_sems.at[a], device_id=(x, y, 1 - c), device_id_type=MESH))
    return cps


def _half_shape(g, kind):
    return (4, g.shape[1] // 2, g.shape[2]) if kind == "rows" else (g.shape[0] // 2, g.shape[1])


def _exchange_start(copies, per_array, srcs, land_shapes, kinds, name, zeroed=False):
    n = len(srcs)
    lands = [(jnp.zeros if zeroed else lax.empty)(shape, s.dtype) for shape, s in zip(land_shapes, srcs)]

    def body(*refs):
        for cp in copies(refs[:n], refs[n:2 * n], refs[2 * n], refs[2 * n + 1], kinds):
            cp.start()
        refs[-1][...] = jnp.zeros_like(refs[-1])

    return pl.pallas_call(
        body, name=name,
        out_shape=(pltpu.SemaphoreType.DMA((per_array * n,)), pltpu.SemaphoreType.DMA((per_array * n,)),
                   *[_hbm(t) for t in srcs], *[_hbm(t) for t in lands], jax.ShapeDtypeStruct((8, LANES), F32)),
        in_specs=[_IN_HBM] * (2 * n),
        out_specs=(_SEMS, _SEMS, *[_IN_HBM] * (2 * n), pl.BlockSpec(memory_space=pltpu.VMEM)),
        input_output_aliases={a: 2 + a for a in range(2 * n)},
        compiler_params=pltpu.CompilerParams(has_side_effects=_DATAFLOW),
    )(*[pltpu.with_memory_space_constraint(t, pltpu.HBM) for t in list(srcs) + lands])


def _exchange_wait(copies, started, after, kinds, name):
    n = (len(started) - 3) // 2
    bufs = started[2:2 + 2 * n]

    def body(*refs):
        for cp in copies(refs[:n], refs[n:2 * n], refs[2 * n], refs[2 * n + 1], kinds):
            cp.wait_send()
            cp.wait_recv()

    outs = pl.pallas_call(
        body, name=name, out_shape=tuple(_hbm(t) for t in bufs),
        in_specs=[_IN_HBM] * (2 * n) + [_SEMS, _SEMS, ANY], out_specs=tuple([_IN_HBM] * (2 * n)),
        input_output_aliases={a: a for a in range(2 * n)},
        compiler_params=pltpu.CompilerParams(has_side_effects=_DATAFLOW),
    )(*bufs, started[0], started[1], after)
    return outs[:n], outs[n:]


def _add_chips(s1, r2, kcvec, kind, name):
    _, h, C = r2.shape
    tr = _row_tile(h, 512)
    nt = h // tr

    def body(kc_ref, s_ref, r0_ref, r1_ref, r2_ref, o_ref):
        s = s_ref[0] if kind == "rows" else s_ref[...]
        o_ref[...] = ((s.astype(F32) + r0_ref[0].astype(F32)) + r1_ref[0].astype(F32)) + r2_ref[0].astype(F32)

    peer = lambda j: pl.BlockSpec((1, tr, C), lambda i, kc_ref: (j, i, 0))
    if kind == "rows":
        s_spec = pl.BlockSpec((1, tr, C), lambda i, kc_ref: (kc_ref[0], i, 0))
    else:
        s_spec = pl.BlockSpec((tr, C), lambda i, kc_ref: (i, kc_ref[0]))
    return pl.pallas_call(
        body, name=name, out_shape=jax.ShapeDtypeStruct((2 * h, C), F32),
        grid_spec=pltpu.PrefetchScalarGridSpec(
            num_scalar_prefetch=1, grid=(nt,),
            in_specs=[s_spec, peer(0), peer(1), peer(2)],
            out_specs=pl.BlockSpec((tr, C), lambda i, kc_ref: (kc_ref[1] * nt + i, 0))),
        compiler_params=_cparams(("parallel",)),
    )(kcvec, s1, r2, r2, r2)


def _join_sibling_halves(bufs):
    n = len(bufs)

    def body(*refs):
        ins, outs, send_sems, recv_sems = refs[:n], refs[n:2 * n], refs[2 * n], refs[2 * n + 1]
        x, y, c, _ = _place()
        cps = []
        for a in range(n):
            h = ins[a].shape[0] // 2
            cps.append(pltpu.make_async_remote_copy(
                src_ref=ins[a].at[pl.ds(c * h, h), :], dst_ref=outs[a].at[pl.ds(c * h, h), :], send_sem=send_sems.at[a],
                recv_sem=recv_sems.at[a], device_id=(x, y, 1 - c), device_id_type=MESH))
        for cp in cps:
            cp.start()
        for a in range(n):
            h = ins[a].shape[0] // 2
            theirs = outs[a].at[pl.ds((1 - c) * h, h), :]
            pltpu.make_async_remote_copy(src_ref=theirs, dst_ref=theirs, send_sem=send_sems.at[a],
                                         recv_sem=recv_sems.at[a], device_id=(x, y, 1 - c), device_id_type=MESH).wait_recv()
        for cp in cps:
            cp.wait_send()

    return pl.pallas_call(
        body, name="grads_join",
        out_shape=tuple(jax.ShapeDtypeStruct(b.shape, b.dtype) for b in bufs),
        in_specs=[ANY] * n, out_specs=tuple([ANY] * n), input_output_aliases={a: a for a in range(n)},
        scratch_shapes=[pltpu.SemaphoreType.DMA((n,)), pltpu.SemaphoreType.DMA((n,))],
    )(*bufs)


N_DEV = 8


def _spread_copies(packs, lands, send_sems, recv_sems, kinds):
    x, y, c, _ = _place()
    me = 4 * x + 2 * y + c
    return [pltpu.make_async_remote_copy(
        src_ref=packs[0], dst_ref=lands[0].at[me], send_sem=send_sems.at[mask - 1], recv_sem=recv_sems.at[mask - 1],
        device_id=(1 - x if mask & 4 else x, 1 - y if mask & 2 else y, 1 - c if mask & 1 else c), device_id_type=MESH)
        for mask in range(1, N_DEV)]


def _sum_spread(pack, gathered):
    P = pack.shape[0]

    def body(p_ref, g_ref, o_ref):
        x, y, c, _ = _place()
        me = 4 * x + 2 * y + c
        acc = None
        for i in range(N_DEV):
            term = jnp.where(me == i, p_ref[...], g_ref[i])
            acc = term if acc is None else acc + term
        o_ref[...] = acc

    vmem = pl.BlockSpec(memory_space=pltpu.VMEM)
    return pl.pallas_call(body, name="allreduce_sum", out_shape=jax.ShapeDtypeStruct((P, LANES), F32),
                          in_specs=[vmem, vmem], out_specs=vmem)(pack, gathered)


def _pack_rows(arrs):
    rows = []
    for a in arrs:
        f = a.reshape(-1)
        f = jnp.pad(f, (0, (-f.shape[0]) % (8 * LANES)))
        rows.append(f.reshape(-1, LANES))
    return jnp.concatenate(rows, axis=0)


def _unpack_rows(pack, shapes):
    out, r = [], 0
    for s in shapes:
        n = math.prod(s)
        out.append(pack[r:r + -(-n // LANES)].reshape(-1)[:n].reshape(s))
        r += 8 * -(-n // (8 * LANES))
    return out


_SMALL = ["norm_mix_pre", "ml_head_norm", "b_gate_a", "b_gate_b", "norm_mix_post", "norm_ffn_pre", "norm_ffn_post",
          "conv_b", "b_ml_i", "b_ml_f", "b_fox_f"]
_BIG = ["w_in", "w_branch_a", "w_branch_b", "w_out", "w_up", "w_down"]
_WEIGHTS = ['norm_mix_pre', 'w_in', 'b_ml_i', 'b_ml_f', 'ml_head_norm', 'b_fox_f', 'b_gate_a', 'b_gate_b', 'w_branch_a',
            'w_branch_b', 'w_out', 'norm_mix_post', 'norm_ffn_pre', 'w_up', 'conv_w', 'conv_b', 'w_down', 'norm_ffn_post']


_KINDS = ["rows", "rows", "rows", "rows", "cols", "rows"]


def kernel(x, norm_mix_pre, w_in, b_ml_i, b_ml_f, ml_head_norm, b_fox_f, b_gate_a, b_gate_b, w_branch_a, w_branch_b, w_out, norm_mix_post, norm_ffn_pre, w_up, conv_w, conv_b, w_down, norm_ffn_post, loss_target, m_norm_mix_pre, m_w_in, m_b_ml_i, m_b_ml_f, m_ml_head_norm, m_b_fox_f, m_b_gate_a, m_b_gate_b, m_w_branch_a, m_w_branch_b, m_w_out, m_norm_mix_post, m_norm_ffn_pre, m_w_up, m_conv_w, m_conv_b, m_w_down, m_norm_ffn_post, v_norm_mix_pre, v_w_in, v_b_ml_i, v_b_ml_f, v_ml_head_norm, v_b_fox_f, v_b_gate_a, v_b_gate_b, v_w_branch_a, v_w_branch_b, v_w_out, v_norm_mix_post, v_norm_ffn_pre, v_w_up, v_conv_w, v_conv_b, v_w_down, v_norm_ffn_post):
    args = dict(locals())
    w = {n: args[n] for n in _WEIGHTS}
    mom = {n: args["m_" + n] for n in _WEIGHTS}
    var = {n: args["v_" + n] for n in _WEIGHTS}
    cx, cy, cc = lax.axis_index("x"), lax.axis_index("y"), lax.axis_index("c")
    kme = 2 * cx + cy
    cvec = jnp.reshape(cc, (1,)).astype(jnp.int32)
    kcvec = jnp.stack([kme, cc]).astype(jnp.int32)
    odd = kme % 2

    tr3 = lambda t: jnp.transpose(t, (0, 2, 1))
    w["w_in"], mom["w_in"], var["w_in"] = tr3(w_in), tr3(m_w_in), tr3(v_w_in)
    w_in_main = lax.dynamic_slice_in_dim(w["w_in"][0], 4 * odd, 2048, axis=0).astype(BF16)
    w_in_gates = lax.dynamic_slice_in_dim(w["w_in"][0], 2048 * (1 - odd), 4, axis=0).astype(BF16)
    (wmain_t,), (g_cw, g_gates) = _gather_weights([w_in_main], _KINDS[:1], [w["conv_w"][0], w_in_gates])
    rest_started = _gather_start([w[n][0].astype(BF16) for n in _BIG[1:]], _KINDS[1:], "gather_rest_start")

    relay = {}

    def rest_arrived(after):
        bufs = _gather_wait(rest_started, after, _KINDS[1:], "gather_rest_wait")
        relay["started"] = _relay_start(bufs, _KINDS[1:], "gather_rest_relay_start")
        return relay["started"][-1]

    def rest_weights(after):
        g_a, g_b, g_out, wup, g_down = _relay_wait(relay["started"], after, _KINDS[1:], "gather_rest_relay_wait")
        return full(g_a), full(g_b), full(g_out), wup, full(g_down)
    gate_rows = g_gates.reshape(16, D_MODEL)
    wsmall_t = jnp.zeros((N_SMALL, D_MODEL), BF16)
    for blk, (lo, hi) in enumerate(((0, 4), (4, 8), (8, 16))):
        wsmall_t = wsmall_t.at[blk * LANES:blk * LANES + hi - lo].set(gate_rows[lo:hi])
    full = lambda g: g.reshape(-1, g.shape[2])
    p = {n: w[n] for n in _SMALL}
    p["conv_w"] = jnp.transpose(g_cw, (1, 0, 2)).reshape(3, -1)

    groups = {}

    def on_grads(group, gs):
        names = list(gs)
        kinds = [_KINDS[_BIG.index(n)] for n in names]
        whole = [g if k == "cols" else g.reshape(4, -1, g.shape[1]) for g, k in zip(gs.values(), kinds)]
        started = _exchange_start(_sibling_copies, 1, whole, [_half_shape(g, k) for g, k in zip(whole, kinds)], kinds,
                                  "grads_to_sibling_start_" + group)
        groups[group] = dict(names=names, kinds=kinds, sibling=started)
        return started[-1]

    def advance(group, after):
        G = groups[group]
        whole, got = _exchange_wait(_sibling_copies, G["sibling"], after, G["kinds"], "grads_to_sibling_wait_" + group)
        sums = [_add_halves(g, r, cvec, k, "add_sibling_" + n) for g, r, k, n in zip(whole, got, G["kinds"], G["names"])]
        G["chips"] = _exchange_start(_chip_copies, 3, sums, [_land_shape(s, k) for s, k in zip(sums, G["kinds"])],
                                     G["kinds"], "grads_to_chips_start_" + group)
        return G["chips"][-1]

    loss_row, grad_x, big, small = _local_step(x[0], loss_target[0], full(wmain_t), wsmall_t, rest_arrived, rest_weights,
                                               p, on_grads, advance, rest_started[-1])
    gt = big["wsmall_t"]
    small["w_in_gates"] = jnp.concatenate([gt[0:4], gt[LANES:LANES + 4], gt[2 * LANES:2 * LANES + 8]], axis=0)
    small_names = _SMALL + ["conv_w"]
    packed_names = small_names + ["w_in_gates"]
    pack = _pack_rows([small[n] for n in packed_names] + [loss_row])
    spread = _exchange_start(_spread_copies, N_DEV - 1, [pack], [(N_DEV,) + pack.shape], None, "allreduce_start", zeroed=True)

    grads = {}
    mine, mine_names = [], []
    for group, G in groups.items():
        sums, got = _exchange_wait(_chip_copies, G["chips"], spread[-1], G["kinds"], "grads_to_chips_wait_" + group)
        mine += [_add_chips(s, r, kcvec, k, "add_chips_" + n) for s, r, k, n in zip(sums, got, G["kinds"], G["names"])]
        mine_names += G["names"]
    grads.update(zip(mine_names, _join_sibling_halves(mine)))

    delta, new_m, new_v = {}, {}, {}
    for n in _BIG[1:]:
        delta[n], new_m[n], new_v[n] = _adamw(w[n], grads[n], mom[n], var[n], "adamw_" + n)
        grads[n] = grads[n][None]

    (pack,), (gathered,) = _exchange_wait(_spread_copies, spread, delta[_BIG[-1]], None, "allreduce_wait")
    full_shapes = [small[n].shape if n in ("conv_w", "w_in_gates") else w[n][0].shape for n in packed_names]
    total = _unpack_rows(_sum_spread(pack, gathered), full_shapes + [loss_row.shape])
    for n, t in zip(packed_names, total):
        grads[n] = t
    loss = total[-1][0, 0]
    grads["conv_w"] = lax.dynamic_slice_in_dim(grads["conv_w"], kme * conv_w.shape[2], conv_w.shape[2], axis=1)
    my_gates = lax.dynamic_slice_in_dim(grads.pop("w_in_gates"), 4 * kme, 4, axis=0)
    g_in = jnp.zeros(w["w_in"].shape[1:], F32)
    g_in = lax.dynamic_update_slice_in_dim(g_in, grads["w_in"], 4 * odd, axis=0)
    grads["w_in"] = lax.dynamic_update_slice_in_dim(g_in, my_gates, 2048 * (1 - odd), axis=0)
    delta["w_in"], new_m["w_in"], new_v["w_in"] = _adamw(w["w_in"], grads["w_in"], mom["w_in"], var["w_in"], "adamw_w_in")
    grads["w_in"] = grads["w_in"][None]
    for d in (grads, delta, new_m, new_v):
        d["w_in"] = tr3(d["w_in"])
    packs = [_pack_rows([d[n][0] for n in small_names]) for d in (w, mom, var)]
    pad = ((0, (-packs[0].shape[0]) % 8), (0, 0))
    packs = [jnp.pad(t, pad)[None] for t in packs]
    gp = jnp.pad(_pack_rows([grads[n] for n in small_names]), pad)
    shapes = [w[n][0].shape for n in small_names]
    for dst, res in zip((delta, new_m, new_v), _adamw(packs[0], gp, packs[1], packs[2], "adamw_small")):
        for n, t in zip(small_names, _unpack_rows(res[0], shapes)):
            dst[n] = t[None]
    for n in small_names:
        grads[n] = grads[n][None]

    return (loss, grad_x[None], *[grads[n] for n in _WEIGHTS], *[delta[n] for n in _WEIGHTS],
            *[new_m[n] for n in _WEIGHTS], *[new_v[n] for n in _WEIGHTS])
```

```python
import functools
import math

import jax
import jax.numpy as jnp
from jax import lax
from jax.experimental import pallas as pl
from jax.experimental.pallas import tpu as pltpu

F32 = jnp.float32
BF16 = jnp.bfloat16
MESH = pl.DeviceIdType.MESH

D_MODEL = 1024
ML_HEADS = 4
ML_DQK = 128
ML_DV = 256
FOX_HEADS = 8
FOX_DH = 128
D_FF = 2816
GATE_CAP = 15.0
EPS = 1e-6
ADAM_LR, ADAM_B1, ADAM_B2, ADAM_EPS, ADAM_WD, ADAM_STEP = 0.001, 0.9, 0.999, 1e-08, 0.01, 10

LANES = 128
MLC = 256
FOX_TQ = 512
FOX_TQ_FWD = 512
FOX_TK = 512
FOX_TK_FWD = 512
ROW_T = 512
CONV_TC = 1408
VMEM_LIMIT = 56 * 1024 * 1024

C_QM, C_KM, C_VM, C_OM = 0, 512, 1024, 2048
N_ML, N_FOX, N_GATE = 3072, 3072, 2048
N_SMALL = 384


def _cparams(sem=None):
    return pltpu.CompilerParams(dimension_semantics=sem, vmem_limit_bytes=VMEM_LIMIT)


def _tile(n, target):
    if n <= target:
        return n
    best = None
    for t in range(LANES, target + 1, LANES):
        if n % t == 0:
            best = t
    assert best is not None, (n, target)
    return best


def _dot(a, b, dims):
    return lax.dot_general(a, b, (dims, ((), ())), preferred_element_type=F32)


def _dot_nn(a, b):
    return _dot(a, b, ((1,), (0,)))


def _dot_nt(a, b):
    return _dot(a, b, ((1,), (1,)))


def _dot_tn(a, b):
    return _dot(a, b, ((0,), (0,)))


_DOTS = {"nn": _dot_nn, "nt": _dot_nt, "tn": _dot_tn}


def _mm(a, b, mode, out_dtype, name, tm=1024, tn=1408, tk=1408, after=None):
    a_parts = list(a) if isinstance(a, (list, tuple)) else [a]
    b_parts = list(b) if isinstance(b, (list, tuple)) else [b]
    extra = [] if after is None else [after]
    assert len(a_parts) == 1 or len(b_parts) == 1, name
    a_axes = {"nn": "ik", "nt": "ik", "tn": "ki"}[mode]
    b_axes = {"nn": "kj", "nt": "jk", "tn": "kj"}[mode]
    size, target = {}, dict(i=tm, j=tn, k=tk)
    for parts, axes in ((a_parts, a_axes), (b_parts, b_axes)):
        dims = (parts[0].shape[0], parts[0].shape[1] * len(parts))
        for ax, n in zip(axes, dims):
            assert size.setdefault(ax, n) == n, (name, ax, n, size)
    tile = {}
    for parts, axes in ((a_parts, a_axes), (b_parts, b_axes)):
        if len(parts) > 1:
            tile[axes[1]] = _tile(parts[0].shape[1], target[axes[1]])
    for ax in "ijk":
        tile.setdefault(ax, _tile(size[ax], target[ax]))
    M, N, nk = size["i"], size["j"], size["k"] // tile["k"]
    grid_pos = dict(i=0, j=1, k=2)
    dot = _DOTS[mode]

    def specs(parts, axes):
        blk = (tile[axes[0]], tile[axes[1]])
        if len(parts) == 1:
            return [pl.BlockSpec(blk, lambda *g: (g[grid_pos[axes[0]]], g[grid_pos[axes[1]]]))], None
        bpp = parts[0].shape[1] // blk[1]

        def index(p):
            def f(*g):
                g0, g1 = g[grid_pos[axes[0]]], g[grid_pos[axes[1]]]
                on = g1 // bpp == p
                return jnp.where(on, g0, 0), jnp.where(on, g1 % bpp, 0)
            return f

        return [pl.BlockSpec(blk, index(p)) for p in range(len(parts))], (axes[1], bpp)

    a_specs, a_sel = specs(a_parts, a_axes)
    b_specs, b_sel = specs(b_parts, b_axes)
    na, nb = len(a_parts), len(b_parts)

    def body(*refs):
        a_refs, b_refs = refs[:na], refs[na:na + nb]
        o_ref, acc = refs[na + nb + len(extra)], refs[na + nb + len(extra) + 1:]

        def accumulate(part):
            if nk == 1:
                o_ref[...] = part.astype(o_ref.dtype)
                return
            acc_ref, = acc
            k = pl.program_id(2)

            @pl.when(k == 0)
            def _():
                acc_ref[...] = part

            @pl.when(k > 0)
            def _():
                acc_ref[...] += part

            @pl.when(k == nk - 1)
            def _():
                o_ref[...] = acc_ref[...].astype(o_ref.dtype)

        sel = a_sel or b_sel
        if sel is None:
            accumulate(dot(a_refs[0][...], b_refs[0][...]))
        else:
            which = pl.program_id(grid_pos[sel[0]]) // sel[1]
            for p in range(max(na, nb)):
                @pl.when(which == p)
                def _(p=p):
                    accumulate(dot(a_refs[p if a_sel else 0][...], b_refs[p if b_sel else 0][...]))

    return pl.pallas_call(
        body, name=name,
        out_shape=jax.ShapeDtypeStruct((M, N), out_dtype),
        grid=(M // tile["i"], N // tile["j"], nk),
        in_specs=a_specs + b_specs + [pl.BlockSpec(memory_space=pl.ANY)] * len(extra),
        out_specs=pl.BlockSpec((tile["i"], tile["j"]), lambda i, j, k: (i, j)),
        scratch_shapes=[pltpu.VMEM((tile["i"], tile["j"]), F32)] if nk > 1 else [],
        compiler_params=_cparams(("parallel", "parallel", "arbitrary")),
    )(*a_parts, *b_parts, *extra)


def _rstd(x):
    return lax.rsqrt(jnp.mean(x * x, axis=-1, keepdims=True) + EPS)


def _rmsnorm_fwd(x, g, name):
    S, D = x.shape
    T = _tile(S, ROW_T)

    def body(x_ref, g_ref, o_ref):
        xv = x_ref[...]
        o_ref[...] = (xv * _rstd(xv) * g_ref[...]).astype(o_ref.dtype)

    return pl.pallas_call(
        body, name=name, out_shape=jax.ShapeDtypeStruct((S, D), BF16), grid=(S // T,),
        in_specs=[pl.BlockSpec((T, D), lambda i: (i, 0)), pl.BlockSpec((1, D), lambda i: (0, 0))],
        out_specs=pl.BlockSpec((T, D), lambda i: (i, 0)),
        compiler_params=_cparams(("parallel",)),
    )(x, g)


def _resid_norm_fwd(x, z, g, g_next, name):
    S, D = x.shape
    T = _tile(S, ROW_T)

    def body(x_ref, z_ref, g_ref, gn_ref, o_ref, h_ref):
        zv = z_ref[...]
        x1 = x_ref[...] + zv * _rstd(zv) * g_ref[...]
        o_ref[...] = x1
        h_ref[...] = (x1 * _rstd(x1) * gn_ref[...]).astype(h_ref.dtype)

    row = pl.BlockSpec((T, D), lambda i: (i, 0))
    vec = pl.BlockSpec((1, D), lambda i: (0, 0))
    return pl.pallas_call(
        body, name=name, out_shape=(jax.ShapeDtypeStruct((S, D), F32), jax.ShapeDtypeStruct((S, D), BF16)),
        grid=(S // T,), in_specs=[row, row, vec, vec], out_specs=(row, row), compiler_params=_cparams(("parallel",)),
    )(x, z, g, g_next)


def _norm_chain_bwd(dh, xin, g, resid, zin, gz, name):
    S, D = xin.shape
    T = _tile(S, ROW_T)

    def body(dh_ref, x_ref, g_ref, r_ref, z_ref, gz_ref, dx_ref, dz_ref, dg_ref, dgz_ref):
        dx, dgt = _rmsnorm_bwd_math(dh_ref[...], x_ref[...], g_ref[...])
        dx = dx + r_ref[...]
        dx_ref[...] = dx
        dz, dgzt = _rmsnorm_bwd_math(dx, z_ref[...], gz_ref[...])
        dz_ref[...] = dz.astype(dz_ref.dtype)

        @pl.when(pl.program_id(0) == 0)
        def _():
            dg_ref[...] = jnp.zeros_like(dg_ref)
            dgz_ref[...] = jnp.zeros_like(dgz_ref)

        dg_ref[...] += jnp.sum(dgt, axis=0, keepdims=True)
        dgz_ref[...] += jnp.sum(dgzt, axis=0, keepdims=True)

    row = pl.BlockSpec((T, D), lambda i: (i, 0))
    vec = pl.BlockSpec((1, D), lambda i: (0, 0))
    v1 = jax.ShapeDtypeStruct((1, D), F32)
    return pl.pallas_call(
        body, name=name,
        out_shape=(jax.ShapeDtypeStruct((S, D), F32), jax.ShapeDtypeStruct((S, D), BF16), v1, v1),
        grid=(S // T,), in_specs=[row, row, vec, row, row, vec], out_specs=(row, row, vec, vec),
        compiler_params=_cparams(("arbitrary",)),
    )(dh, xin, g, resid, zin, gz)


def _rmsnorm_bwd_math(dy, xv, g):
    r = _rstd(xv)
    u = dy * g
    dx = r * u - xv * (r * r * r) * jnp.mean(u * xv, axis=-1, keepdims=True)
    return dx, dy * xv * r


def _rmsnorm_bwd(dys, xin, g, resid, out_dtype, name):
    S, D = xin.shape
    T = _tile(S, ROW_T)
    has_resid = resid is not None
    ndy = len(dys)

    def body(*refs):
        dy_refs, (x_ref, g_ref) = refs[:ndy], refs[ndy:ndy + 2]
        dx_ref, dg_ref = refs[-2:]
        dy = dy_refs[0][...]
        for r in dy_refs[1:]:
            dy = dy + r[...]
        dx, dgt = _rmsnorm_bwd_math(dy, x_ref[...], g_ref[...])
        if has_resid:
            dx = dx + refs[ndy + 2][...]
        dx_ref[...] = dx.astype(dx_ref.dtype)

        @pl.when(pl.program_id(0) == 0)
        def _():
            dg_ref[...] = jnp.zeros_like(dg_ref)

        dg_ref[...] += jnp.sum(dgt, axis=0, keepdims=True)

    row = pl.BlockSpec((T, D), lambda i: (i, 0))
    vec = pl.BlockSpec((1, D), lambda i: (0, 0))
    ins = list(dys) + [xin, g] + ([resid] if has_resid else [])
    return pl.pallas_call(
        body, name=name,
        out_shape=(jax.ShapeDtypeStruct((S, D), out_dtype), jax.ShapeDtypeStruct((1, D), F32)),
        grid=(S // T,), in_specs=[row] * ndy + [row, vec] + ([row] if has_resid else []),
        out_specs=(row, vec), compiler_params=_cparams(("arbitrary",)),
    )(*ins)


def _loss_head(x1, d, g, target, name):
    S, D = x1.shape
    T = _tile(S, ROW_T)

    def body(x_ref, d_ref, g_ref, t_ref, loss_ref, dy_ref, dd_ref, dg_ref):
        dv, gv = d_ref[...], g_ref[...]
        y = x_ref[...] + dv * _rstd(dv) * gv
        diff = y - t_ref[...]
        dy = diff * (1.0 / D)
        dy_ref[...] = dy
        dd, dgt = _rmsnorm_bwd_math(dy, dv, gv)
        dd_ref[...] = dd.astype(dd_ref.dtype)

        @pl.when(pl.program_id(0) == 0)
        def _():
            dg_ref[...] = jnp.zeros_like(dg_ref)
            loss_ref[...] = jnp.zeros_like(loss_ref)

        dg_ref[...] += jnp.sum(dgt, axis=0, keepdims=True)
        part = jnp.sum(jnp.sum(diff * diff, axis=1, keepdims=True), axis=0, keepdims=True)
        loss_ref[...] += (0.5 / D) * part

    row = pl.BlockSpec((T, D), lambda i: (i, 0))
    vec = pl.BlockSpec((1, D), lambda i: (0, 0))
    return pl.pallas_call(
        body, name=name,
        out_shape=(jax.ShapeDtypeStruct((1, LANES), F32), jax.ShapeDtypeStruct((S, D), F32),
                   jax.ShapeDtypeStruct((S, D), BF16), jax.ShapeDtypeStruct((1, D), F32)),
        grid=(S // T,), in_specs=[row, row, vec, row],
        out_specs=(pl.BlockSpec((1, LANES), lambda i: (0, 0)), row, row, vec),
        compiler_params=_cparams(("arbitrary",)),
    )(x1, d, g, target)


def _merge_fwd(ya, yb, pm, ba, bb, name):
    S, D = ya.shape
    T = _tile(S, ROW_T)

    def body(ya_ref, yb_ref, ga_ref, gb_ref, ba_ref, bb_ref, o_ref):
        sa = jax.nn.sigmoid(ga_ref[...] + ba_ref[...])
        sb = jax.nn.sigmoid(gb_ref[...] + bb_ref[...])
        o_ref[...] = (sa * ya_ref[...] + sb * yb_ref[...]).astype(o_ref.dtype)

    row = pl.BlockSpec((T, D), lambda i: (i, 0))
    vec = pl.BlockSpec((1, D), lambda i: (0, 0))
    return pl.pallas_call(
        body, name=name, out_shape=jax.ShapeDtypeStruct((S, D), BF16), grid=(S // T,),
        in_specs=[row, row, pl.BlockSpec((T, D), lambda i: (i, 0)),
                  pl.BlockSpec((T, D), lambda i: (i, 1)), vec, vec],
        out_specs=row, compiler_params=_cparams(("parallel",)),
    )(ya, yb, pm, pm, ba, bb)


def _merge_bwd(dmerged, ya, yb, pm, ba, bb, name):
    S, D = ya.shape
    T = _tile(S, ROW_T)

    def body(dm_ref, ya_ref, yb_ref, ga_ref, gb_ref, ba_ref, bb_ref,
             dya_ref, dyb_ref, dga_ref, dgb_ref, dba_ref, dbb_ref):
        dm = dm_ref[...]
        sa = jax.nn.sigmoid(ga_ref[...] + ba_ref[...])
        sb = jax.nn.sigmoid(gb_ref[...] + bb_ref[...])
        dya_ref[...] = (dm * sa).astype(dya_ref.dtype)
        dyb_ref[...] = (dm * sb).astype(dyb_ref.dtype)
        dga = dm * ya_ref[...] * sa * (1.0 - sa)
        dgb = dm * yb_ref[...] * sb * (1.0 - sb)
        dga_ref[...] = dga.astype(dga_ref.dtype)
        dgb_ref[...] = dgb.astype(dgb_ref.dtype)

        @pl.when(pl.program_id(0) == 0)
        def _():
            dba_ref[...] = jnp.zeros_like(dba_ref)
            dbb_ref[...] = jnp.zeros_like(dbb_ref)

        dba_ref[...] += jnp.sum(dga, axis=0, keepdims=True)
        dbb_ref[...] += jnp.sum(dgb, axis=0, keepdims=True)

    row = pl.BlockSpec((T, D), lambda i: (i, 0))
    vec = pl.BlockSpec((1, D), lambda i: (0, 0))
    act = jax.ShapeDtypeStruct((S, D), BF16)
    v1 = jax.ShapeDtypeStruct((1, D), F32)
    return pl.pallas_call(
        body, name=name, out_shape=(act, act, act, act, v1, v1), grid=(S // T,),
        in_specs=[row, row, row, pl.BlockSpec((T, D), lambda i: (i, 0)),
                  pl.BlockSpec((T, D), lambda i: (i, 1)), vec, vec],
        out_specs=(row, row, row, row, vec, vec), compiler_params=_cparams(("arbitrary",)),
    )(dmerged, ya, yb, pm, pm, ba, bb)


_GELU_C = math.sqrt(2.0 / math.pi)


def _gelu(g):
    t = jnp.tanh(_GELU_C * (g + 0.044715 * g * g * g))
    return 0.5 * g * (1.0 + t), t


def _gelu_grad(g, t):
    return 0.5 * (1.0 + t) + 0.5 * g * (1.0 - t * t) * _GELU_C * (1.0 + 3 * 0.044715 * g * g)


def _shift_down(v, halo_ref, first, rows):
    T = v.shape[0]
    keep = jnp.where(first, 0.0, 1.0)
    h7 = halo_ref[7:8, :] * keep
    h6 = halo_ref[6:7, :] * keep
    m1 = jnp.where(rows == 0, h7, pltpu.roll(v, 1, 0))
    m2 = jnp.where(rows == 0, h6, jnp.where(rows == 1, h7, pltpu.roll(v, 2, 0)))
    return m1, m2


def _conv_act_fwd(up, cw, cb, name):
    S, F2 = up.shape
    Fh = F2 // 2
    T = _tile(S, ROW_T)
    tc = _tile(Fh, CONV_TC)
    ncol = Fh // tc
    hb = T // 8

    def body(ua_ref, ug_ref, ha_ref, hg_ref, wa_ref, wg_ref, ba_ref, bg_ref, o_ref, a_ref, g_ref):
        first = pl.program_id(0) == 0
        rows = lax.broadcasted_iota(jnp.int32, (T, tc), 0)

        def conv(u_ref, h_ref, w_ref, b_ref):
            v = u_ref[...]
            m1, m2 = _shift_down(v, h_ref, first, rows)
            return b_ref[...] + w_ref[0:1, :] * m2 + w_ref[1:2, :] * m1 + w_ref[2:3, :] * v

        a = conv(ua_ref, ha_ref, wa_ref, ba_ref)
        g = conv(ug_ref, hg_ref, wg_ref, bg_ref)
        a_ref[...] = a
        g_ref[...] = g
        o_ref[...] = (_gelu(g)[0] * a).astype(o_ref.dtype)

    halo = lambda off: pl.BlockSpec((8, tc), lambda i, j: (jnp.maximum(i * hb - 1, 0), j + off))
    blk = pl.BlockSpec((T, tc), lambda i, j: (i, j))
    f32 = jax.ShapeDtypeStruct((S, Fh), F32)
    return pl.pallas_call(
        body, name=name, out_shape=(jax.ShapeDtypeStruct((S, Fh), BF16), f32, f32), grid=(S // T, ncol),
        in_specs=[blk, pl.BlockSpec((T, tc), lambda i, j: (i, j + ncol)),
                  halo(0), halo(ncol),
                  pl.BlockSpec((3, tc), lambda i, j: (0, j)), pl.BlockSpec((3, tc), lambda i, j: (0, j + ncol)),
                  pl.BlockSpec((1, tc), lambda i, j: (0, j)), pl.BlockSpec((1, tc), lambda i, j: (0, j + ncol))],
        out_specs=(blk, blk, blk),
        compiler_params=_cparams(("parallel", "parallel")),
    )(up, up, up, up, cw, cw, cb, cb)


def _conv_act_bwd(up, a, g, dact, cw, name):
    S, F2 = up.shape
    Fh = F2 // 2
    T = _tile(S, ROW_T)
    tc = _tile(Fh, CONV_TC)
    ncol, nrow, hb, nhb = Fh // tc, S // T, T // 8, S // 8

    def body(ua_ref, ug_ref, a_ref, g_ref, an_ref, gn_ref, wa_ref, wg_ref, da_ref, dn_ref,
             dpa_ref, dpg_ref, dwa_ref, dwg_ref, dba_ref, dbg_ref, dua_n, dug_n):
        i = pl.program_id(1)
        rows = lax.broadcasted_iota(jnp.int32, (T, tc), 0)

        def du_of(a, g, dact_v):
            gel, t = _gelu(g)
            return dact_v * gel, dact_v * a * _gelu_grad(g, t)

        dua, dug = du_of(a_ref[...], g_ref[...], da_ref[...])
        keep = jnp.where(i == nrow - 1, 0.0, 1.0)
        dua_n[...], dug_n[...] = du_of(an_ref[...], gn_ref[...], dn_ref[...] * keep)

        @pl.when(i == 0)
        def _():
            for r in (dwa_ref, dwg_ref, dba_ref, dbg_ref):
                r[...] = jnp.zeros_like(r)

        for du, n_ref, u_ref, w_ref, o_ref, dw_ref, db_ref in ((dua, dua_n, ua_ref, wa_ref, dpa_ref, dwa_ref, dba_ref),
                                                               (dug, dug_n, ug_ref, wg_ref, dpg_ref, dwg_ref, dbg_ref)):
            n0, n1 = n_ref[0:1, :], n_ref[1:2, :]
            du1 = jnp.where(rows == T - 1, n0, pltpu.roll(du, T - 1, 0))
            du2 = jnp.where(rows == T - 2, n0, jnp.where(rows == T - 1, n1, pltpu.roll(du, T - 2, 0)))
            o_ref[...] = (w_ref[2:3, :] * du + w_ref[1:2, :] * du1 + w_ref[0:1, :] * du2).astype(o_ref.dtype)
            u = u_ref[...]
            db_ref[...] += jnp.sum(du, axis=0, keepdims=True)
            for j, d in enumerate((du2, du1, du)):
                dw_ref[j:j + 1, :] += jnp.sum(d * u, axis=0, keepdims=True)

    tile = lambda off: pl.BlockSpec((T, tc), lambda j, i: (i, j + off))
    under = pl.BlockSpec((8, tc), lambda j, i: (jnp.minimum((i + 1) * hb, nhb - 1), j))
    vec = lambda n, off: pl.BlockSpec((n, tc), lambda j, i: (0, j + off))
    act = jax.ShapeDtypeStruct((S, Fh), BF16)
    return pl.pallas_call(
        body, name=name,
        out_shape=(act, act, jax.ShapeDtypeStruct((3, Fh), F32), jax.ShapeDtypeStruct((3, Fh), F32),
                   jax.ShapeDtypeStruct((1, Fh), F32), jax.ShapeDtypeStruct((1, Fh), F32)),
        grid=(ncol, nrow),
        in_specs=[tile(0), tile(ncol), tile(0), tile(0), under, under, vec(3, 0), vec(3, ncol), tile(0), under],
        out_specs=(tile(0), tile(0), vec(3, 0), vec(3, 0), vec(1, 0), vec(1, 0)),
        scratch_shapes=[pltpu.VMEM((8, tc), F32), pltpu.VMEM((8, tc), F32)],
        compiler_params=_cparams(("parallel", "arbitrary")),
    )(up, up, a, g, a, g, cw, cw, dact, dact)


def _split3(x):
    hi = x.astype(BF16)
    r1 = x - hi.astype(F32)
    mid = r1.astype(BF16)
    lo = (r1 - mid.astype(F32)).astype(BF16)
    return hi, mid, lo


def _tri_dot(tri, x):
    hi, mid, lo = _split3(x)
    return _dot_nn(tri, hi) + _dot_nn(tri, mid) + _dot_nn(tri, lo)


def _log_sigmoid(x):
    return jnp.minimum(x, 0.0) - jnp.log(1.0 + jnp.exp(-jnp.abs(x)))


def _tri_mask(n, lower):
    r = lax.broadcasted_iota(jnp.int32, (n, n), 0)
    c = lax.broadcasted_iota(jnp.int32, (n, n), 1)
    return (r >= c) if lower else (r <= c)


def _gates_fwd(ps, bi, bf, bff, name):
    S = ps.shape[0]
    NC = S // MLC

    def body(ps_ref, bi_ref, bf_ref, bff_ref, a_ref, A_ref, wi_ref, em_ref, wk_ref, dec_ref, F_ref, m_scr, f_scr):
        @pl.when(pl.program_id(0) == 0)
        def _():
            m_scr[...] = jnp.zeros_like(m_scr)
            f_scr[...] = jnp.zeros_like(f_scr)

        rows = lax.broadcasted_iota(jnp.int32, (MLC, LANES), 0)
        ltri = _tri_mask(MLC, True).astype(BF16)
        li = GATE_CAP * jnp.tanh((ps_ref[:, 0:LANES] + bi_ref[...]) / GATE_CAP)
        lf = _log_sigmoid(GATE_CAP * jnp.tanh((ps_ref[:, LANES:2 * LANES] + bf_ref[...]) / GATE_CAP))
        b = _tri_dot(ltri, lf)
        a = li - b
        cm = a
        sh = 1
        while sh < MLC:
            cm = jnp.where(rows >= sh, jnp.maximum(cm, pltpu.roll(cm, sh, 0)), cm)
            sh *= 2
        m0 = m_scr[...]
        A = jnp.maximum(cm, m0)
        a_ref[...] = a
        A_ref[...] = A
        A_last = A_ref[MLC - 1:MLC, :]
        wi_ref[...] = jnp.exp(m0 - A)
        em_ref[...] = jnp.exp(-(b + A))
        wk_ref[...] = jnp.exp(a - A_last)
        dec_ref[0] = jnp.exp(m0 - A_last)
        F_ref[...] = b
        m_scr[...] = F_ref[MLC - 1:MLC, :] + A_last
        lfg = _log_sigmoid(ps_ref[:, 2 * LANES:3 * LANES] + bff_ref[...])
        F_ref[...] = _tri_dot(ltri, lfg) + f_scr[...]
        f_scr[...] = F_ref[MLC - 1:MLC, :]

    col = pl.BlockSpec((MLC, LANES), lambda c: (c, 0))
    vec = pl.BlockSpec((1, LANES), lambda c: (0, 0))
    cs = jax.ShapeDtypeStruct((S, LANES), F32)
    return pl.pallas_call(
        body, name=name,
        out_shape=(cs, cs, cs, cs, cs, jax.ShapeDtypeStruct((NC, 1, LANES), F32), cs),
        grid=(NC,), in_specs=[pl.BlockSpec((MLC, N_SMALL), lambda c: (c, 0)), vec, vec, vec],
        out_specs=(col, col, col, col, col, pl.BlockSpec((1, 1, LANES), lambda c: (c, 0, 0)), col),
        scratch_shapes=[pltpu.VMEM((1, LANES), F32), pltpu.VMEM((1, LANES), F32)],
        compiler_params=_cparams(("arbitrary",)),
    )(ps, bi, bf, bff)


def _gates_bwd(ps, bi, bf, bff, rk, kc, tch, dF, name):
    S = ps.shape[0]
    NC = S // MLC

    def body(ps_ref, bi_ref, bf_ref, bff_ref, rk_ref, kc_ref, t_ref, dF_ref, dps_ref, db_ref, carry):
        @pl.when(pl.program_id(0) == 0)
        def _():
            carry[...] = jnp.zeros_like(carry)
            db_ref[...] = jnp.zeros_like(db_ref)

        lanes = lax.broadcasted_iota(jnp.int32, (MLC, LANES), 1)
        utri = _tri_mask(MLC, False).astype(BF16)
        ti = jnp.tanh((ps_ref[:, 0:LANES] + bi_ref[...]) / GATE_CAP)
        t_end, t_start = t_ref[0, 0:1, :], t_ref[0, 1:2, :]
        rk = rk_ref[...]
        rk = rk - (jnp.sum(rk, axis=0, keepdims=True) - (t_start - t_end)) * (1.0 / MLC)
        dpi = jnp.where(lanes < ML_HEADS, (kc_ref[...] - rk) * (1.0 - ti * ti), 0.0)
        tf = jnp.tanh((ps_ref[:, LANES:2 * LANES] + bf_ref[...]) / GATE_CAP)
        dlf = _tri_dot(utri, rk) + t_end
        dpf = jnp.where(lanes < ML_HEADS, dlf * jax.nn.sigmoid(-GATE_CAP * tf) * (1.0 - tf * tf), 0.0)
        dFv = dF_ref[...]
        dlfg = _tri_dot(utri, dFv) + carry[...]
        carry[...] += jnp.sum(dFv, axis=0, keepdims=True)
        dpff = jnp.where(lanes < FOX_HEADS, dlfg * jax.nn.sigmoid(-(ps_ref[:, 2 * LANES:3 * LANES] + bff_ref[...])), 0.0)
        for n, dp in enumerate((dpi, dpf, dpff)):
            dps_ref[:, n * LANES:(n + 1) * LANES] = dp.astype(dps_ref.dtype)
            db_ref[:, n * LANES:(n + 1) * LANES] += jnp.sum(dp, axis=0, keepdims=True)

    rev = lambda c: (NC - 1 - c, 0)
    col = pl.BlockSpec((MLC, LANES), rev)
    vec = pl.BlockSpec((1, LANES), lambda c: (0, 0))
    wide = pl.BlockSpec((MLC, N_SMALL), rev)
    return pl.pallas_call(
        body, name=name,
        out_shape=(jax.ShapeDtypeStruct((S, N_SMALL), BF16), jax.ShapeDtypeStruct((1, N_SMALL), F32)),
        grid=(NC,),
        in_specs=[wide, vec, vec, vec, col, col, pl.BlockSpec((1, 2, LANES), lambda c: (NC - 1 - c, 0, 0)), col],
        out_specs=(wide, pl.BlockSpec((1, N_SMALL), lambda c: (0, 0))),
        scratch_shapes=[pltpu.VMEM((1, LANES), F32)],
        compiler_params=_cparams(("arbitrary",)),
    )(ps, bi, bf, bff, rk, kc, tch, dF)


_ML_SCALE = ML_DQK ** -0.5


def _ml_specs(rev, NC):
    idx = (lambda c: NC - 1 - c) if rev else (lambda c: c)
    qk = lambda blk: pl.BlockSpec((MLC, ML_HEADS * ML_DQK), lambda c: (idx(c), blk))
    wide = lambda blk: pl.BlockSpec((MLC, D_MODEL), lambda c: (idx(c), blk))
    col = pl.BlockSpec((MLC, LANES), lambda c: (idx(c), 0))
    return idx, qk, wide, col


def _ml_intra(q_ref, k_ref, arow_ref, A_ref, h):
    hs = slice(h * ML_DQK, (h + 1) * ML_DQK)
    qf = q_ref[:, hs] * _ML_SCALE
    kf = k_ref[:, hs]
    qb, kb = qf.astype(BF16), kf.astype(BF16)
    qk = _dot_nt(qb, kb)
    logw = arow_ref[h:h + 1, :] - A_ref[:, h:h + 1]
    W = jnp.exp(jnp.where(_tri_mask(MLC, True), logw, -1e30))
    return qb, kb, qf, kf, qk, W


def _mlstm_fwd(pm, a_row, A, wi, em, wk, dec, w_hn, name):
    S = pm.shape[0]
    NC = S // MLC
    _, qk, wide, col = _ml_specs(False, NC)

    def body(q_ref, k_ref, v_ref, o_ref, arow_ref, A_ref, wi_ref, em_ref, wk_ref, dec_ref, whn_ref,
             ha_ref, hp_ref, den_ref, cst_ref, nst_ref, C_scr, n_scr):
        @pl.when(pl.program_id(0) == 0)
        def _():
            C_scr[...] = jnp.zeros_like(C_scr)
            n_scr[...] = jnp.zeros_like(n_scr)

        lanes = lax.broadcasted_iota(jnp.int32, (MLC, LANES), 1)
        den_tile = jnp.zeros((MLC, LANES), F32)
        for h in range(ML_HEADS):
            vs = slice(h * ML_DV, (h + 1) * ML_DV)
            qb, kb, qf, kf, qk_, W = _ml_intra(q_ref, k_ref, arow_ref, A_ref, h)
            vb = v_ref[:, vs].astype(BF16)
            Cf = C_scr[h]
            Cb = Cf.astype(BF16)
            nrow = n_scr[h]
            cst_ref[0, h] = Cb
            nst_ref[0, h] = nrow
            s = qk_ * W
            wic = wi_ref[:, h:h + 1]
            num = _dot_nn(s.astype(BF16), vb) + wic * _dot_nt(qb, Cb)
            den = jnp.sum(s, axis=1, keepdims=True) + wic * jnp.sum(qf * nrow, axis=1, keepdims=True)
            hp = num / jnp.maximum(jnp.abs(den), em_ref[:, h:h + 1])
            hp_ref[:, vs] = hp
            den_tile = jnp.where(lanes == h, den, den_tile)
            hn = hp * _rstd(hp) * whn_ref[:, vs]
            ha_ref[:, vs] = (hn * jax.nn.sigmoid(o_ref[:, vs])).astype(ha_ref.dtype)
            wkc = wk_ref[:, h:h + 1]
            kw = kf * wkc
            d = dec_ref[0, :, h:h + 1]
            C_scr[h] = d * Cf + _dot_tn(vb, kw.astype(BF16))
            n_scr[h] = d * nrow + jnp.sum(kw, axis=0, keepdims=True)
        den_ref[...] = den_tile

    return pl.pallas_call(
        body, name=name,
        out_shape=(jax.ShapeDtypeStruct((S, D_MODEL), BF16), jax.ShapeDtypeStruct((S, D_MODEL), F32),
                   jax.ShapeDtypeStruct((S, LANES), F32),
                   jax.ShapeDtypeStruct((NC, ML_HEADS, ML_DV, ML_DQK), BF16),
                   jax.ShapeDtypeStruct((NC, ML_HEADS, 1, ML_DQK), F32)),
        grid=(NC,),
        in_specs=[qk(C_QM // 512), qk(C_KM // 512), wide(C_VM // D_MODEL), wide(C_OM // D_MODEL),
                  pl.BlockSpec((8, MLC), lambda c: (0, c)), col, col, col, col,
                  pl.BlockSpec((1, 1, LANES), lambda c: (c, 0, 0)), pl.BlockSpec((1, D_MODEL), lambda c: (0, 0))],
        out_specs=(pl.BlockSpec((MLC, D_MODEL), lambda c: (c, 0)), pl.BlockSpec((MLC, D_MODEL), lambda c: (c, 0)),
                   col, pl.BlockSpec((1, ML_HEADS, ML_DV, ML_DQK), lambda c: (c, 0, 0, 0)),
                   pl.BlockSpec((1, ML_HEADS, 1, ML_DQK), lambda c: (c, 0, 0, 0))),
        scratch_shapes=[pltpu.VMEM((ML_HEADS, ML_DV, ML_DQK), F32), pltpu.VMEM((ML_HEADS, 1, ML_DQK), F32)],
        compiler_params=_cparams(("arbitrary",)),
    )(pm, pm, pm, pm, a_row, A, wi, em, wk, dec, w_hn)


def _mlstm_bwd(dha, pm, hp_all, den_all, a_row, A, wi, em, wk, dec, cst, nst, w_hn, name):
    S = pm.shape[0]
    NC = S // MLC
    idx, qk, wide, col = _ml_specs(True, NC)

    def body(dha_ref, q_ref, k_ref, v_ref, o_ref, hp_ref, den_ref, arow_ref, A_ref, wi_ref, em_ref, wk_ref,
             dec_ref, cst_ref, nst_ref, whn_ref,
             dqk_ref, dv_ref, do_ref, rk_ref, kc_ref, t_ref, dwhn_ref, dC_scr, dn_scr, t_scr):
        @pl.when(pl.program_id(0) == 0)
        def _():
            dC_scr[...] = jnp.zeros_like(dC_scr)
            dn_scr[...] = jnp.zeros_like(dn_scr)
            t_scr[...] = jnp.zeros_like(t_scr)
            dwhn_ref[...] = jnp.zeros_like(dwhn_ref)

        lanes = lax.broadcasted_iota(jnp.int32, (MLC, LANES), 1)
        lane1 = lax.broadcasted_iota(jnp.int32, (1, LANES), 1)
        t_ref[0, 0:1, :] = t_scr[...]
        rk_tile = jnp.zeros((MLC, LANES), F32)
        kc_tile = jnp.zeros((MLC, LANES), F32)
        t_new = jnp.zeros((1, LANES), F32)
        for h in range(ML_HEADS):
            hs = slice(h * ML_DQK, (h + 1) * ML_DQK)
            vs = slice(h * ML_DV, (h + 1) * ML_DV)
            hp = hp_ref[:, vs]
            sig = jax.nn.sigmoid(o_ref[:, vs])
            whn = whn_ref[:, vs]
            r = _rstd(hp)
            dga = dha_ref[:, vs]
            do_ref[:, vs] = (dga * (hp * r * whn) * sig * (1.0 - sig)).astype(do_ref.dtype)
            dhn = dga * sig
            dhp, dwt = _rmsnorm_bwd_math(dhn, hp, whn)
            dwhn_ref[:, vs] += jnp.sum(dwt, axis=0, keepdims=True)
            den = den_ref[:, h:h + 1]
            floor = em_ref[:, h:h + 1]
            D = jnp.maximum(jnp.abs(den), floor)
            dnum = dhp / D
            dh_h = jnp.sum(dhp * hp, axis=1, keepdims=True)
            active = jnp.abs(den) >= floor
            dden = -dh_h / D * jnp.where(active, jnp.sign(den), 0.0)
            phi = jnp.where(active, 0.0, dh_h)
            qb, kb, qf, kf, qk_, W = _ml_intra(q_ref, k_ref, arow_ref, A_ref, h)
            vf = v_ref[:, vs]
            vb = vf.astype(BF16)
            Cb = cst_ref[0, h]
            nrow = nst_ref[0, h]
            wic = wi_ref[:, h:h + 1]
            wkc = wk_ref[:, h:h + 1]
            d = dec_ref[0, :, h:h + 1]
            dCn = dC_scr[h]
            dCb = dCn.astype(BF16)
            dnn = dn_scr[h]
            dnumb = dnum.astype(BF16)
            s = qk_ * W
            ds = (_dot_nt(dnumb, vb) + dden) * W
            dsb = ds.astype(BF16)
            dnw = (wic * dnum).astype(BF16)
            wd = wic * dden
            kw = kf * wkc
            dv_state = _dot_nt(kw.astype(BF16), dCb)
            dq = _dot_nn(dsb, kb) + _dot_nn(dnw, Cb) + wd * nrow
            dk_state = wkc * (_dot_nn(vb, dCb) + dnn)
            dk = _dot_tn(dsb, qb) + dk_state
            dv = _dot_tn(s.astype(BF16), dnumb) + dv_state
            dC = d * dCn + _dot_tn(dnw, qb)
            dn = d * dnn + jnp.sum(wd * qf, axis=0, keepdims=True)
            dC_scr[h] = dC
            dn_scr[h] = dn
            dqk_ref[:, hs] = (dq * _ML_SCALE).astype(dqk_ref.dtype)
            dqk_ref[:, C_KM + h * ML_DQK:C_KM + (h + 1) * ML_DQK] = dk.astype(dqk_ref.dtype)
            dv_ref[:, vs] = dv.astype(dv_ref.dtype)
            G = ds * qk_
            inter = _dot_nt(qb, Cb)
            qn = jnp.sum(qf * nrow, axis=1, keepdims=True)
            R = (jnp.sum(G, axis=1, keepdims=True)
                 + wic * (jnp.sum(dnum * inter, axis=1, keepdims=True) + dden * qn))
            K = jnp.sum(G.T, axis=1, keepdims=True) + jnp.sum(kf * dk_state, axis=1, keepdims=True)
            rk_tile = jnp.where(lanes == h, R - K, rk_tile)
            kc_tile = jnp.where(lanes == h, phi, kc_tile)
            tt = (jnp.sum(jnp.sum(dC * Cb.astype(F32), axis=1, keepdims=True), axis=0, keepdims=True)
                  + jnp.sum(dn * nrow, axis=1, keepdims=True))
            t_new = jnp.where(lane1 == h, tt, t_new)
        rk_ref[...] = rk_tile
        kc_ref[...] = kc_tile
        t_ref[0, 1:2, :] = t_new
        t_scr[...] = t_new

    act = lambda n: jax.ShapeDtypeStruct((S, n), BF16)
    cs = jax.ShapeDtypeStruct((S, LANES), F32)
    rowblk = lambda n: pl.BlockSpec((MLC, n), lambda c: (idx(c), 0))
    return pl.pallas_call(
        body, name=name,
        out_shape=(act(D_MODEL), act(D_MODEL), act(D_MODEL), cs, cs,
                   jax.ShapeDtypeStruct((NC, 2, LANES), F32), jax.ShapeDtypeStruct((1, D_MODEL), F32)),
        grid=(NC,),
        in_specs=[rowblk(D_MODEL), qk(C_QM // 512), qk(C_KM // 512), wide(C_VM // D_MODEL), wide(C_OM // D_MODEL),
                  rowblk(D_MODEL), col, pl.BlockSpec((8, MLC), lambda c: (0, idx(c))), col, col, col, col,
                  pl.BlockSpec((1, 1, LANES), lambda c: (idx(c), 0, 0)),
                  pl.BlockSpec((1, ML_HEADS, ML_DV, ML_DQK), lambda c: (idx(c), 0, 0, 0)),
                  pl.BlockSpec((1, ML_HEADS, 1, ML_DQK), lambda c: (idx(c), 0, 0, 0)),
                  pl.BlockSpec((1, D_MODEL), lambda c: (0, 0))],
        out_specs=(rowblk(D_MODEL), rowblk(D_MODEL), rowblk(D_MODEL), col, col,
                   pl.BlockSpec((1, 2, LANES), lambda c: (idx(c), 0, 0)), pl.BlockSpec((1, D_MODEL), lambda c: (0, 0))),
        scratch_shapes=[pltpu.VMEM((ML_HEADS, ML_DV, ML_DQK), F32), pltpu.VMEM((ML_HEADS, 1, ML_DQK), F32),
                        pltpu.VMEM((1, LANES), F32)],
        compiler_params=_cparams(("arbitrary",)),
    )(dha, pm, pm, pm, pm, hp_all, den_all, a_row, A, wi, em, wk, dec, cst, nst, w_hn)


_FOX_SCALE = FOX_DH ** -0.5
_NEG = -1e30
_LOG2E = 1.4426950408889634
_LN2 = 0.6931471805599453
_QF_BLK, _KF_BLK, _VF_BLK = 0, FOX_HEADS, 2 * FOX_HEADS


def _lane_pick(tile, lane):
    lanes = lax.broadcasted_iota(jnp.int32, tile.shape, 1)
    return jnp.sum(jnp.where(lanes == lane, tile, 0.0), axis=1, keepdims=True)


def _col_to_row(col):
    return jnp.max(jnp.broadcast_to(col, (col.shape[0], LANES)).T, axis=0, keepdims=True)


def _causal(q0, k0, shape, q_axis):
    qpos = q0 + lax.broadcasted_iota(jnp.int32, shape, q_axis)
    kpos = k0 + lax.broadcasted_iota(jnp.int32, shape, 1 - q_axis)
    return kpos <= qpos


def _fox_fwd(pf, fc, fk_row, name):
    S = pf.shape[0]
    TQ, TK = FOX_TQ_FWD, FOX_TK_FWD
    nq, nk = S // TQ, S // TK
    c1 = _FOX_SCALE * _LOG2E

    def body(q_ref, k_ref, v_ref, fc_ref, fr_ref, o_ref, lse_ref):
        h, i = pl.program_id(0), pl.program_id(1)
        qb = q_ref[...]
        fq2 = _lane_pick(fc_ref[...], h) * _LOG2E

        def step(j, carry, masked):
            m, l, acc = carry
            off = pl.multiple_of(j * TK, TK)
            t = _dot_nt(qb, k_ref[pl.ds(off, TK), :]) * c1 - fr_ref[0, j] * _LOG2E
            if masked:
                t = jnp.where(_causal(i * TQ, j * TK, (TQ, TK), 0), t, _NEG)
            m_new = jnp.maximum(m, jnp.max(t, axis=1, keepdims=True) + fq2)
            alpha = jnp.exp2(m - m_new)
            p = jnp.exp2(t + (fq2 - m_new))
            l = alpha * l + jnp.sum(p, axis=1, keepdims=True)
            acc = alpha * acc + _dot_nn(p.astype(BF16), v_ref[pl.ds(off, TK), :])
            return m_new, l, acc

        init = (jnp.full((TQ, 1), _NEG, F32), jnp.zeros((TQ, 1), F32), jnp.zeros((TQ, FOX_DH), F32))
        last = (i * TQ) // TK
        carry = lax.fori_loop(0, last, lambda j, c: step(j, c, False), init)
        for d in range(TQ // TK):
            carry = step(last + d, carry, True)
        m, l, acc = carry
        o_ref[...] = (acc / l).astype(o_ref.dtype)
        lse_ref[0, 0] = _col_to_row((m + jnp.log2(l)) * _LN2)

    head = lambda blk: pl.BlockSpec((S, FOX_DH), lambda h, i: (0, blk + h))
    return pl.pallas_call(
        body, name=name,
        out_shape=(jax.ShapeDtypeStruct((S, D_MODEL), BF16), jax.ShapeDtypeStruct((FOX_HEADS, nq, 1, TQ), F32)),
        grid=(FOX_HEADS, nq),
        in_specs=[pl.BlockSpec((TQ, FOX_DH), lambda h, i: (i, _QF_BLK + h)), head(_KF_BLK), head(_VF_BLK),
                  pl.BlockSpec((TQ, LANES), lambda h, i: (i, 0)),
                  pl.BlockSpec((1, nk, 1, TK), lambda h, i: (h, 0, 0, 0))],
        out_specs=(pl.BlockSpec((TQ, FOX_DH), lambda h, i: (i, h)),
                   pl.BlockSpec((1, 1, 1, TQ), lambda h, i: (h, i, 0, 0))),
        compiler_params=_cparams(("parallel", "arbitrary")),
    )(pf, pf, pf, fc, fk_row)


def _fox_bwd(dhb, hb, pf, lse_row, fq_row, fc, name):
    S = pf.shape[0]
    TQ, TK = FOX_TQ, FOX_TK
    nq, nk, r = S // TQ, S // TK, TK // TQ
    c1 = _FOX_SCALE * _LOG2E

    def body(q_ref, k_ref, v_ref, do_ref, o_ref, lse_ref, fq_ref, fc_ref,
             dq_ref, dk_ref, dv_ref, dFk_ref, dFq_ref, dq_acc, qside, delta, dk_acc, dv_acc, cs_acc):
        h, j = pl.program_id(0), pl.program_id(1)

        @pl.when(j == 0)
        def _():
            dq_acc[...] = jnp.zeros_like(dq_acc)
            dFq_ref[...] = jnp.zeros_like(dFq_ref)

            def fill(b, _):
                off = pl.multiple_of(b * TQ, TQ)
                prod = do_ref[pl.ds(off, TQ), :].astype(F32) * o_ref[pl.ds(off, TQ), :].astype(F32)
                delta[b] = jnp.sum(prod.T, axis=0, keepdims=True)
                qside[b] = (fq_ref[0, b] - lse_ref[0, b]) * _LOG2E
                return 0

            lax.fori_loop(0, nq, fill, 0)

        kb = k_ref[...]
        vb = v_ref[...]
        fk2 = _lane_pick(fc_ref[...], h) * _LOG2E
        dk_acc[...] = jnp.zeros_like(dk_acc)
        dv_acc[...] = jnp.zeros_like(dv_acc)
        cs_acc[...] = jnp.zeros_like(cs_acc)

        def step(i, masked):
            off = pl.multiple_of(i * TQ, TQ)
            qb = q_ref[pl.ds(off, TQ), :]
            dob = do_ref[pl.ds(off, TQ), :]
            t = _dot_nt(kb, qb) * c1 + qside[i] - fk2
            if masked:
                t = jnp.where(_causal(i * TQ, j * TK, (TK, TQ), 1), t, _NEG)
            p = jnp.exp2(t)
            dv_acc[...] += _dot_nn(p.astype(BF16), dob)
            ds = p * (_dot_nt(vb, dob) - delta[i])
            dsb = ds.astype(BF16)
            dk_acc[...] += _dot_nn(dsb, qb)
            dq_acc[pl.ds(off, TQ), :] += _dot_tn(dsb, kb)
            cs_acc[...] += jnp.sum(ds, axis=1, keepdims=True)
            dFq_ref[0, i] += jnp.sum(ds, axis=0, keepdims=True)

        for d in range(r):
            step(r * j + d, True)

        def rest(i, _):
            step(i, False)
            return 0

        lax.fori_loop(r * j + r, nq, rest, 0)
        dk_ref[...] = (dk_acc[...] * _FOX_SCALE).astype(dk_ref.dtype)
        dv_ref[...] = dv_acc[...].astype(dv_ref.dtype)
        dFk_ref[0, 0] = -_col_to_row(cs_acc[...])

        @pl.when(j == nk - 1)
        def _():
            dq_ref[...] = (dq_acc[...] * _FOX_SCALE).astype(dq_ref.dtype)

    head = lambda blk: pl.BlockSpec((S, FOX_DH), lambda h, j: (0, blk + h))
    kblk = lambda blk: pl.BlockSpec((TK, FOX_DH), lambda h, j: (j, blk + h))
    qrows = pl.BlockSpec((1, nq, 1, TQ), lambda h, j: (h, 0, 0, 0))
    act = jax.ShapeDtypeStruct((S, D_MODEL), BF16)
    return pl.pallas_call(
        body, name=name,
        out_shape=(act, act, act, jax.ShapeDtypeStruct((FOX_HEADS, nk, 1, TK), F32),
                   jax.ShapeDtypeStruct((FOX_HEADS, nq, 1, TQ), F32)),
        grid=(FOX_HEADS, nk),
        in_specs=[head(_QF_BLK), kblk(_KF_BLK), kblk(_VF_BLK), head(0), head(0), qrows, qrows,
                  pl.BlockSpec((TK, LANES), lambda h, j: (j, 0))],
        out_specs=(head(0), kblk(0), kblk(0), pl.BlockSpec((1, 1, 1, TK), lambda h, j: (h, j, 0, 0)), qrows),
        scratch_shapes=[pltpu.VMEM((S, FOX_DH), F32), pltpu.VMEM((nq, 1, TQ), F32), pltpu.VMEM((nq, 1, TQ), F32),
                        pltpu.VMEM((TK, FOX_DH), F32), pltpu.VMEM((TK, FOX_DH), F32), pltpu.VMEM((TK, 1), F32)],
        compiler_params=_cparams(("parallel", "arbitrary")),
    )(pf, pf, pf, dhb, hb, lse_row, fq_row, fc)


def _pad_lanes(v):
    return jnp.pad(v, ((0, 0), (0, LANES - v.shape[1])))


def _local_step(x, target, wmain_t, wsmall_t, rest_arrived, rest_weights, p, on_grads, advance, token):
    S = x.shape[0]
    bi, bf, bff = _pad_lanes(p["b_ml_i"]), _pad_lanes(p["b_ml_f"]), _pad_lanes(p["b_fox_f"])

    h0 = _rmsnorm_fwd(x, p["norm_mix_pre"] + token[0:1, 0:1], "norm_mix_pre")
    pm = _mm(h0, wmain_t[:N_ML], "nt", F32, "proj_mlstm")
    pf = _mm(h0, wmain_t[N_ML:N_ML + N_FOX], "nt", BF16, "proj_fox")
    pg = _mm(h0, wmain_t[N_ML + N_FOX:], "nt", F32, "proj_merge")
    ps = _mm(h0, wsmall_t, "nt", F32, "proj_gates")
    a, A, wi, em, wk, dec, Fc = _gates_fwd(ps, bi, bf, bff, "gates_fwd")
    a_row = a[:, :8].T
    ha, hp, den, cst, nst = _mlstm_fwd(pm, a_row, A, wi, em, wk, dec, p["ml_head_norm"], "mlstm_fwd")
    ft = Fc[:, :FOX_HEADS].T + rest_arrived(ha)[0, 0]
    fq_row = ft.reshape(FOX_HEADS, S // FOX_TQ, 1, FOX_TQ)
    fk_row = ft.reshape(FOX_HEADS, S // FOX_TK, 1, FOX_TK)
    hb, lse_row = _fox_fwd(pf, Fc, ft.reshape(FOX_HEADS, S // FOX_TK_FWD, 1, FOX_TK_FWD), "fox_fwd")
    wa, wb, wout, wup, wdown = rest_weights(hb)
    ya = _mm(ha, wa, "nn", F32, "branch_a")
    yb = _mm(hb, wb, "nn", F32, "branch_b")
    merged = _merge_fwd(ya, yb, pg, p["b_gate_a"], p["b_gate_b"], "merge_fwd")
    z = _mm(merged, wout, "nn", F32, "out_proj")
    x1, h2 = _resid_norm_fwd(x, z, p["norm_mix_post"], p["norm_ffn_pre"], "resid_mix")
    up = _mm(h2, wup, "nn", F32, "ffn_up")
    act, conv_a, conv_g = _conv_act_fwd(up, p["conv_w"], p["conv_b"], "conv_act_fwd")
    d = _mm(act, wdown, "nn", F32, "ffn_down")
    loss_row, dy, dd, g_norm_ffn_post = _loss_head(x1, d, p["norm_ffn_post"], target, "loss_head")
    dact = _mm(dd, wdown, "nt", F32, "d_act")
    g_wdown = _mm(act, dd, "tn", F32, "dw_down", tm=1408)
    dupa, dupg, dcwa, dcwg, dcba, dcbg = _conv_act_bwd(up, conv_a, conv_g, dact, p["conv_w"], "conv_act_bwd")
    g_conv_w = jnp.concatenate([dcwa, dcwg], axis=1)
    g_conv_b = jnp.concatenate([dcba, dcbg], axis=1)
    dh2 = _mm([dupa, dupg], wup, "nt", F32, "d_h2")
    g_wup = _mm(h2, [dupa, dupg], "tn", F32, "dw_up")
    token = on_grads("ffn", dict(w_up=g_wup, w_down=g_wdown))
    dx1, dz, g_norm_ffn_pre, g_norm_mix_post = _norm_chain_bwd(
        dh2, x1, p["norm_ffn_pre"] + token[0:1, 0:1], dy, z, p["norm_mix_post"], "norm_chain_bwd")
    dmerged = _mm(dz, wout, "nt", F32, "d_merged")
    g_wout = _mm(merged, dz, "tn", F32, "dw_out")
    dya, dyb, dga, dgb, g_b_gate_a, g_b_gate_b = _merge_bwd(dmerged, ya, yb, pg, p["b_gate_a"], p["b_gate_b"], "merge_bwd")
    dha = _mm(dya, wa, "nt", F32, "d_ha")
    g_wa = _mm(ha, dya, "tn", F32, "dw_a")
    dhb = _mm(dyb, wb, "nt", BF16, "d_hb")
    g_wb = _mm(hb, dyb, "tn", F32, "dw_b")
    token = advance("ffn", g_wb) + on_grads("mix", dict(w_out=g_wout, w_branch_a=g_wa, w_branch_b=g_wb))
    dqkm, dvm, dom, rk, kc, tch, g_ml_head_norm = _mlstm_bwd(
        dha, pm, hp, den, a_row, A, wi, em, wk, dec, cst, nst, p["ml_head_norm"] + token[0:1, 0:1], "mlstm_bwd")
    token = advance("mix", dqkm)
    dqf, dkf, dvf, dFk, dFq = _fox_bwd(dhb, hb, pf, lse_row.reshape(fq_row.shape), fq_row + token[0, 0], Fc, "fox_bwd")
    dF = jnp.pad((dFk.reshape(FOX_HEADS, S) + dFq.reshape(FOX_HEADS, S)).T, ((0, 0), (0, LANES - FOX_HEADS)))
    dps, dbias = _gates_bwd(ps, bi, bf, bff, rk, kc, tch, dF, "gates_bwd")
    dpm = [dqkm, dvm, dom, dqf, dkf, dvf, dga, dgb]
    g_wmain_t = _mm(dpm, h0, "tn", F32, "dw_main")
    token = on_grads("in", dict(w_in=g_wmain_t))
    g_wsmall_t = _mm(dps, h0, "tn", F32, "dw_gates")
    dh0s = _mm(dps, wsmall_t + token[0:1, 0:1].astype(BF16), "nn", F32, "d_h0_gates")
    token = advance("in", dh0s)
    dh0 = _mm(dpm, wmain_t, "nn", F32, "d_h0_main", after=token)
    grad_x, g_norm_mix_pre = _rmsnorm_bwd([dh0, dh0s], x, p["norm_mix_pre"], dx1, F32, "norm_mix_pre_bwd")

    big = dict(wsmall_t=g_wsmall_t)
    small = dict(norm_mix_pre=g_norm_mix_pre, ml_head_norm=g_ml_head_norm, b_gate_a=g_b_gate_a, b_gate_b=g_b_gate_b,
                 norm_mix_post=g_norm_mix_post, norm_ffn_pre=g_norm_ffn_pre, norm_ffn_post=g_norm_ffn_post,
                 conv_b=g_conv_b, b_ml_i=dbias[:, 0:ML_HEADS], b_ml_f=dbias[:, LANES:LANES + ML_HEADS],
                 b_fox_f=dbias[:, 2 * LANES:2 * LANES + FOX_HEADS], conv_w=g_conv_w)
    return loss_row, grad_x, big, small


def _row_tile(r, target=256):
    best = None
    for t in range(8, min(r, target) + 1, 8):
        if r % t == 0:
            best = t
    return best if best is not None else r


def _adamw(w, g, m, v, name):
    _, R, C = w.shape
    tr = _row_tile(R)
    tc = C
    if tr == R and R > 256:
        tc = 256

    def body(w_ref, g_ref, m_ref, v_ref, d_ref, mo_ref, vo_ref):
        gv = g_ref[...]
        mn = ADAM_B1 * m_ref[0] + (1.0 - ADAM_B1) * gv
        vn = ADAM_B2 * v_ref[0] + (1.0 - ADAM_B2) * (gv * gv)
        m_hat = mn / (1.0 - ADAM_B1 ** ADAM_STEP)
        v_hat = vn / (1.0 - ADAM_B2 ** ADAM_STEP)
        d_ref[0] = -ADAM_LR * (m_hat / (jnp.sqrt(v_hat) + ADAM_EPS) + ADAM_WD * w_ref[0])
        mo_ref[0] = mn
        vo_ref[0] = vn

    blk = pl.BlockSpec((1, tr, tc), lambda i, j: (0, i, j))
    o = jax.ShapeDtypeStruct((1, R, C), F32)
    return pl.pallas_call(
        body, name=name, out_shape=(o, o, o), grid=(R // tr, C // tc),
        in_specs=[blk, pl.BlockSpec((tr, tc), lambda i, j: (i, j)), blk, blk], out_specs=(blk,) * 3,
        compiler_params=_cparams(("parallel", "parallel")),
    )(w, g, m, v)


ANY = pl.BlockSpec(memory_space=pl.ANY)


def _place():
    x, y, c = lax.axis_index("x"), lax.axis_index("y"), lax.axis_index("c")
    chips = [(1 - x, y), (x, 1 - y), (1 - x, 1 - y)]
    return x, y, c, chips


def _block(ref, kind, k, rows=None):
    if kind == "rows":
        return ref.at[k] if rows is None else ref.at[k, pl.ds(*rows), :]
    cb = ref.shape[1] // 4
    return ref.at[:, pl.ds(k * cb, cb)] if rows is None else ref.at[pl.ds(*rows), pl.ds(k * cb, cb)]


def _gathered_shape(s, kind):
    return (4,) + s.shape if kind == "rows" else (s.shape[0], 4 * s.shape[1])


def _gather_weights(shards, kinds, smalls):
    n, ns = len(shards), len(smalls)

    def body(*refs):
        ins, sm_in = refs[:n], refs[n:n + ns]
        outs, sm_out = refs[n + ns:2 * n + ns], refs[2 * n + ns:2 * (n + ns)]
        send_sems, recv_sems, sm_send, sm_recv, local_sems = refs[2 * (n + ns):]
        x, y, c, chips = _place()
        sibling = (x, y, 1 - c)
        kme = 2 * x + y

        def half(a, k, hc):
            h = ins[a].shape[0] // 2
            return _block(outs[a], kinds[a], k, (hc * h, h))

        def remote(a, slot, src, dst, to):
            return pltpu.make_async_remote_copy(src_ref=src, dst_ref=dst, send_sem=send_sems.at[a * 7 + slot],
                                                recv_sem=recv_sems.at[a * 7 + slot], device_id=to, device_id_type=MESH)

        def sm_copy(b, j, k, to):
            return pltpu.make_async_remote_copy(src_ref=sm_in[b], dst_ref=sm_out[b].at[k], send_sem=sm_send.at[3 * b + j],
                                                recv_sem=sm_recv.at[3 * b + j], device_id=to, device_id_type=MESH)

        local = [pltpu.make_async_copy(sm_in[b], sm_out[b].at[kme], local_sems.at[b]) for b in range(ns)]
        for cp in local:
            cp.start()
        sends = [remote(a, 6, ins[a], _block(outs[a], kinds[a], kme), sibling) for a in range(n)]
        for a in range(n):
            h = ins[a].shape[0] // 2
            for j, chip in enumerate(chips):
                sends.append(remote(a, j, ins[a].at[pl.ds(c * h, h), :], half(a, kme, c), (*chip, c)))
        for b in range(ns):
            for j, chip in enumerate(chips):
                sends.append(sm_copy(b, j, kme, (*chip, c)))
        for cp in sends:
            cp.start()
        for a in range(n):
            for j, chip in enumerate(chips):
                kj = 2 * chip[0] + chip[1]
                remote(a, j, half(a, kj, c), half(a, kj, c), (*chip, c)).wait_recv()
                fwd = remote(a, 3 + j, half(a, kj, c), half(a, kj, c), sibling)
                fwd.start()
                sends.append(fwd)
        for a in range(n):
            for j, chip in enumerate(chips):
                kj = 2 * chip[0] + chip[1]
                remote(a, 3 + j, half(a, kj, 1 - c), half(a, kj, 1 - c), sibling).wait_recv()
        for b in range(ns):
            for j, chip in enumerate(chips):
                sm_copy(b, j, 2 * chip[0] + chip[1], (*chip, c)).wait_recv()
        for a in range(n):
            remote(a, 6, ins[a], _block(outs[a], kinds[a], kme), sibling).wait_recv()
        for cp in sends:
            cp.wait_send()
        for cp in local:
            cp.wait()

    outs = pl.pallas_call(
        body, name="gather_weights",
        out_shape=tuple([jax.ShapeDtypeStruct(_gathered_shape(s, k), s.dtype) for s, k in zip(shards, kinds)]
                        + [jax.ShapeDtypeStruct((4,) + s.shape, s.dtype) for s in smalls]),
        in_specs=[ANY] * (n + ns), out_specs=tuple([ANY] * (n + ns)),
        scratch_shapes=[pltpu.SemaphoreType.DMA((7 * n,)), pltpu.SemaphoreType.DMA((7 * n,)),
                        pltpu.SemaphoreType.DMA((3 * ns,)), pltpu.SemaphoreType.DMA((3 * ns,)),
                        pltpu.SemaphoreType.DMA((ns,))],
    )(*shards, *smalls)
    return outs[:n], outs[n:]


_IN_HBM = pl.BlockSpec(memory_space=pltpu.HBM)
_SEMS = pl.BlockSpec(memory_space=pltpu.SEMAPHORE)
_DATAFLOW = pltpu.SideEffectType.DATAFLOW_SIDE_EFFECTING


def _hbm(t):
    return pltpu.HBM(t.shape, t.dtype)


def _gather_copies(ins, outs, send_sems, recv_sems, kinds):
    x, y, c, chips = _place()
    kme = 2 * x + y
    cps = []
    for a in range(len(ins)):
        h = ins[a].shape[0] // 2
        for j, chip in enumerate(chips + [None]):
            to = (x, y, 1 - c) if chip is None else (*chip, c)
            src = ins[a] if chip is None else ins[a].at[pl.ds(c * h, h), :]
            dst = _block(outs[a], kinds[a], kme, None if chip is None else (c * h, h))
            cps.append(pltpu.make_async_remote_copy(src_ref=src, dst_ref=dst, send_sem=send_sems.at[4 * a + j],
                                                    recv_sem=recv_sems.at[4 * a + j], device_id=to, device_id_type=MESH))
    return cps


def _gather_start(shards, kinds, name):
    n = len(shards)
    outs = [lax.empty(_gathered_shape(s, k), s.dtype) for s, k in zip(shards, kinds)]

    def body(*refs):
        for cp in _gather_copies(refs[:n], refs[n:2 * n], refs[2 * n], refs[2 * n + 1], kinds):
            cp.start()
        refs[-1][...] = jnp.zeros_like(refs[-1])

    return pl.pallas_call(
        body, name=name,
        out_shape=(pltpu.SemaphoreType.DMA((4 * n,)), pltpu.SemaphoreType.DMA((4 * n,)),
                   *[_hbm(t) for t in shards], *[_hbm(t) for t in outs], jax.ShapeDtypeStruct((8, LANES), F32)),
        in_specs=[_IN_HBM] * (2 * n),
        out_specs=(_SEMS, _SEMS, *[_IN_HBM] * (2 * n), pl.BlockSpec(memory_space=pltpu.VMEM)),
        input_output_aliases={a: 2 + a for a in range(2 * n)},
        compiler_params=pltpu.CompilerParams(has_side_effects=_DATAFLOW),
    )(*[pltpu.with_memory_space_constraint(t, pltpu.HBM) for t in list(shards) + outs])


def _gather_wait(started, after, kinds, name):
    n = (len(started) - 3) // 2
    bufs = started[2:2 + 2 * n]

    def body(*refs):
        for cp in _gather_copies(refs[:n], refs[n:2 * n], refs[2 * n], refs[2 * n + 1], kinds):
            cp.wait_send()
            cp.wait_recv()

    outs = pl.pallas_call(
        body, name=name, out_shape=tuple(_hbm(t) for t in bufs),
        in_specs=[_IN_HBM] * (2 * n) + [_SEMS, _SEMS, ANY], out_specs=tuple([_IN_HBM] * (2 * n)),
        input_output_aliases={a: a for a in range(2 * n)},
        compiler_params=pltpu.CompilerParams(has_side_effects=_DATAFLOW),
    )(*bufs, started[0], started[1], after)
    return outs[n:]


def _relay_copies(bufs, send_sems, recv_sems, kinds):
    x, y, c, chips = _place()
    cps = []
    for a in range(len(bufs)):
        h = (bufs[a].shape[1] if kinds[a] == "rows" else bufs[a].shape[0]) // 2
        for j, chip in enumerate(chips):
            part = _block(bufs[a], kinds[a], 2 * chip[0] + chip[1], (c * h, h))
            cps.append(pltpu.make_async_remote_copy(src_ref=part, dst_ref=part, send_sem=send_sems.at[3 * a + j],
                                                    recv_sem=recv_sems.at[3 * a + j], device_id=(x, y, 1 - c),
                                                    device_id_type=MESH))
    return cps


def _join_copies(bufs, send_sems, recv_sems, kinds):
    x, y, c, _ = _place()
    cps = []
    for a in range(len(bufs)):
        h = bufs[a].shape[0] // 2
        mine = bufs[a].at[pl.ds(c * h, h), :]
        cps.append(pltpu.make_async_remote_copy(src_ref=mine, dst_ref=mine, send_sem=send_sems.at[a],
                                                recv_sem=recv_sems.at[a], device_id=(x, y, 1 - c), device_id_type=MESH))
    return cps


def _inplace_start(copies, per_array, bufs, kinds, name):
    n = len(bufs)

    def body(*refs):
        for cp in copies(refs[:n], refs[n], refs[n + 1], kinds):
            cp.start()
        refs[-1][...] = jnp.zeros_like(refs[-1])

    return pl.pallas_call(
        body, name=name,
        out_shape=(pltpu.SemaphoreType.DMA((per_array * n,)), pltpu.SemaphoreType.DMA((per_array * n,)),
                   *[_hbm(t) for t in bufs], jax.ShapeDtypeStruct((8, LANES), F32)),
        in_specs=[_IN_HBM] * n, out_specs=(_SEMS, _SEMS, *[_IN_HBM] * n, pl.BlockSpec(memory_space=pltpu.VMEM)),
        input_output_aliases={a: 2 + a for a in range(n)},
        compiler_params=pltpu.CompilerParams(has_side_effects=_DATAFLOW),
    )(*[pltpu.with_memory_space_constraint(t, pltpu.HBM) for t in bufs])


def _inplace_wait(copies, started, after, kinds, name):
    n = len(started) - 3
    bufs = started[2:2 + n]

    def body(*refs):
        for cp in copies(refs[:n], refs[n], refs[n + 1], kinds):
            cp.wait_send()
            cp.wait_recv()

    return pl.pallas_call(
        body, name=name, out_shape=tuple(_hbm(t) for t in bufs),
        in_specs=[_IN_HBM] * n + [_SEMS, _SEMS, ANY], out_specs=tuple([_IN_HBM] * n),
        input_output_aliases={a: a for a in range(n)},
        compiler_params=pltpu.CompilerParams(has_side_effects=_DATAFLOW),
    )(*bufs, started[0], started[1], after)


def _add_halves(g, r1, cvec, kind, name):
    def body(c_ref, g_ref, r_ref, o_ref):
        o_ref[...] = (g_ref[...] + r_ref[...]).astype(o_ref.dtype)

    if kind == "rows":
        _, h, C = r1.shape
        tr = _row_tile(h, 512)
        nt = h // tr
        grid = (4, nt)
        g_spec = pl.BlockSpec((1, tr, C), lambda k, i, c_ref: (k, c_ref[0] * nt + i, 0))
        r_spec = pl.BlockSpec((1, tr, C), lambda k, i, c_ref: (k, i, 0))
    else:
        h, C4 = r1.shape
        tr, tc = _row_tile(h, 512), C4 // 4
        nt = h // tr
        grid = (nt, 4)
        g_spec = pl.BlockSpec((tr, tc), lambda i, k, c_ref: (c_ref[0] * nt + i, k))
        r_spec = pl.BlockSpec((tr, tc), lambda i, k, c_ref: (i, k))
    return pl.pallas_call(
        body, name=name, out_shape=jax.ShapeDtypeStruct(r1.shape, BF16),
        grid_spec=pltpu.PrefetchScalarGridSpec(num_scalar_prefetch=1, grid=grid, in_specs=[g_spec, r_spec],
                                               out_specs=r_spec),
        compiler_params=_cparams(("parallel", "parallel")),
    )(cvec, g, r1)


def _chip_copies(ins, lands, send_sems, recv_sems, kinds):
    x, y, c, chips = _place()
    return [pltpu.make_async_remote_copy(
        src_ref=_block(ins[a], kinds[a], 2 * chip[0] + chip[1]), dst_ref=lands[a].at[j],
        send_sem=send_sems.at[3 * a + j], recv_sem=recv_sems.at[3 * a + j], device_id=(*chip, c), device_id_type=MESH)
        for a in range(len(ins)) for j, chip in enumerate(chips)]


def _land_shape(s, kind):
    return (3,) + (s.shape[1:] if kind == "rows" else (s.shape[0], s.shape[1] // 4))


def _sibling_copies(ins, lands, send_sems, recv_sems, kinds):
    x, y, c, _ = _place()
    cps = []
    for a in range(len(ins)):
        h = lands[a].shape[-2]
        src = ins[a].at[:, pl.ds((1 - c) * h, h), :] if kinds[a] == "rows" else ins[a].at[pl.ds((1 - c) * h, h), :]
        cps.append(pltpu.make_async_remote_copy(src_ref=src, dst_ref=lands[a], send_sem=send_sems.at[a],
                                                recv_sem=recv_sems.at[a], device_id=(x, y, 1 - c), device_id_type=MESH))
    return cps


def _half_shape(g, kind):
    return (4, g.shape[1] // 2, g.shape[2]) if kind == "rows" else (g.shape[0] // 2, g.shape[1])


def _exchange_start(copies, per_array, srcs, land_shapes, kinds, name, zeroed=False):
    n = len(srcs)
    lands = [(jnp.zeros if zeroed else lax.empty)(shape, s.dtype) for shape, s in zip(land_shapes, srcs)]

    def body(*refs):
        for cp in copies(refs[:n], refs[n:2 * n], refs[2 * n], refs[2 * n + 1], kinds):
            cp.start()
        refs[-1][...] = jnp.zeros_like(refs[-1])

    return pl.pallas_call(
        body, name=name,
        out_shape=(pltpu.SemaphoreType.DMA((per_array * n,)), pltpu.SemaphoreType.DMA((per_array * n,)),
                   *[_hbm(t) for t in srcs], *[_hbm(t) for t in lands], jax.ShapeDtypeStruct((8, LANES), F32)),
        in_specs=[_IN_HBM] * (2 * n),
        out_specs=(_SEMS, _SEMS, *[_IN_HBM] * (2 * n), pl.BlockSpec(memory_space=pltpu.VMEM)),
        input_output_aliases={a: 2 + a for a in range(2 * n)},
        compiler_params=pltpu.CompilerParams(has_side_effects=_DATAFLOW),
    )(*[pltpu.with_memory_space_constraint(t, pltpu.HBM) for t in list(srcs) + lands])


def _exchange_wait(copies, started, after, kinds, name):
    n = (len(started) - 3) // 2
    bufs = started[2:2 + 2 * n]

    def body(*refs):
        for cp in copies(refs[:n], refs[n:2 * n], refs[2 * n], refs[2 * n + 1], kinds):
            cp.wait_send()
            cp.wait_recv()

    outs = pl.pallas_call(
        body, name=name, out_shape=tuple(_hbm(t) for t in bufs),
        in_specs=[_IN_HBM] * (2 * n) + [_SEMS, _SEMS, ANY], out_specs=tuple([_IN_HBM] * (2 * n)),
        input_output_aliases={a: a for a in range(2 * n)},
        compiler_params=pltpu.CompilerParams(has_side_effects=_DATAFLOW),
    )(*bufs, started[0], started[1], after)
    return outs[:n], outs[n:]


def _add_chips(s1, r2, kcvec, kind, name):
    _, h, C = r2.shape
    tr = _row_tile(h, 512)
    nt = h // tr

    def body(kc_ref, s_ref, r0_ref, r1_ref, r2_ref, o_ref):
        s = s_ref[0] if kind == "rows" else s_ref[...]
        o_ref[...] = ((s.astype(F32) + r0_ref[0].astype(F32)) + r1_ref[0].astype(F32)) + r2_ref[0].astype(F32)

    peer = lambda j: pl.BlockSpec((1, tr, C), lambda i, kc_ref: (j, i, 0))
    if kind == "rows":
        s_spec = pl.BlockSpec((1, tr, C), lambda i, kc_ref: (kc_ref[0], i, 0))
    else:
        s_spec = pl.BlockSpec((tr, C), lambda i, kc_ref: (i, kc_ref[0]))
    return pl.pallas_call(
        body, name=name, out_shape=jax.ShapeDtypeStruct((2 * h, C), F32),
        grid_spec=pltpu.PrefetchScalarGridSpec(
            num_scalar_prefetch=1, grid=(nt,),
            in_specs=[s_spec, peer(0), peer(1), peer(2)],
            out_specs=pl.BlockSpec((tr, C), lambda i, kc_ref: (kc_ref[1] * nt + i, 0))),
        compiler_params=_cparams(("parallel",)),
    )(kcvec, s1, r2, r2, r2)


N_DEV = 8


def _spread_copies(packs, lands, send_sems, recv_sems, kinds):
    x, y, c, _ = _place()
    me = 4 * x + 2 * y + c
    return [pltpu.make_async_remote_copy(
        src_ref=packs[0], dst_ref=lands[0].at[me], send_sem=send_sems.at[mask - 1], recv_sem=recv_sems.at[mask - 1],
        device_id=(1 - x if mask & 4 else x, 1 - y if mask & 2 else y, 1 - c if mask & 1 else c), device_id_type=MESH)
        for mask in range(1, N_DEV)]


def _sum_spread(pack, gathered):
    P = pack.shape[0]

    def body(p_ref, g_ref, o_ref):
        x, y, c, _ = _place()
        me = 4 * x + 2 * y + c
        acc = None
        for i in range(N_DEV):
            term = jnp.where(me == i, p_ref[...], g_ref[i])
            acc = term if acc is None else acc + term
        o_ref[...] = acc

    vmem = pl.BlockSpec(memory_space=pltpu.VMEM)
    return pl.pallas_call(body, name="allreduce_sum", out_shape=jax.ShapeDtypeStruct((P, LANES), F32),
                          in_specs=[vmem, vmem], out_specs=vmem)(pack, gathered)


def _pack_rows(arrs):
    rows = []
    for a in arrs:
        f = a.reshape(-1)
        f = jnp.pad(f, (0, (-f.shape[0]) % (8 * LANES)))
        rows.append(f.reshape(-1, LANES))
    return jnp.concatenate(rows, axis=0)


def _unpack_rows(pack, shapes):
    out, r = [], 0
    for s in shapes:
        n = math.prod(s)
        out.append(pack[r:r + -(-n // LANES)].reshape(-1)[:n].reshape(s))
        r += 8 * -(-n // (8 * LANES))
    return out


_SMALL = ["norm_mix_pre", "ml_head_norm", "b_gate_a", "b_gate_b", "norm_mix_post", "norm_ffn_pre", "norm_ffn_post",
          "conv_b", "b_ml_i", "b_ml_f", "b_fox_f"]
_BIG = ["w_in", "w_branch_a", "w_branch_b", "w_out", "w_up", "w_down"]
_WEIGHTS = ['norm_mix_pre', 'w_in', 'b_ml_i', 'b_ml_f', 'ml_head_norm', 'b_fox_f', 'b_gate_a', 'b_gate_b', 'w_branch_a',
            'w_branch_b', 'w_out', 'norm_mix_post', 'norm_ffn_pre', 'w_up', 'conv_w', 'conv_b', 'w_down', 'norm_ffn_post']


_KINDS = ["rows", "rows", "rows", "rows", "cols", "rows"]


def kernel(x, norm_mix_pre, w_in, b_ml_i, b_ml_f, ml_head_norm, b_fox_f, b_gate_a, b_gate_b, w_branch_a, w_branch_b, w_out, norm_mix_post, norm_ffn_pre, w_up, conv_w, conv_b, w_down, norm_ffn_post, loss_target, m_norm_mix_pre, m_w_in, m_b_ml_i, m_b_ml_f, m_ml_head_norm, m_b_fox_f, m_b_gate_a, m_b_gate_b, m_w_branch_a, m_w_branch_b, m_w_out, m_norm_mix_post, m_norm_ffn_pre, m_w_up, m_conv_w, m_conv_b, m_w_down, m_norm_ffn_post, v_norm_mix_pre, v_w_in, v_b_ml_i, v_b_ml_f, v_ml_head_norm, v_b_fox_f, v_b_gate_a, v_b_gate_b, v_w_branch_a, v_w_branch_b, v_w_out, v_norm_mix_post, v_norm_ffn_pre, v_w_up, v_conv_w, v_conv_b, v_w_down, v_norm_ffn_post):
    args = dict(locals())
    w = {n: args[n] for n in _WEIGHTS}
    mom = {n: args["m_" + n] for n in _WEIGHTS}
    var = {n: args["v_" + n] for n in _WEIGHTS}
    cx, cy, cc = lax.axis_index("x"), lax.axis_index("y"), lax.axis_index("c")
    kme = 2 * cx + cy
    cvec = jnp.reshape(cc, (1,)).astype(jnp.int32)
    kcvec = jnp.stack([kme, cc]).astype(jnp.int32)
    odd = kme % 2

    tr3 = lambda t: jnp.transpose(t, (0, 2, 1))
    w["w_in"], mom["w_in"], var["w_in"] = tr3(w_in), tr3(m_w_in), tr3(v_w_in)
    w_in_main = lax.dynamic_slice_in_dim(w["w_in"][0], 4 * odd, 2048, axis=0).astype(BF16)
    w_in_gates = lax.dynamic_slice_in_dim(w["w_in"][0], 2048 * (1 - odd), 4, axis=0).astype(BF16)
    (wmain_t,), (g_cw, g_gates) = _gather_weights([w_in_main], _KINDS[:1], [w["conv_w"][0], w_in_gates])
    rest_started = _gather_start([w[n][0].astype(BF16) for n in _BIG[1:]], _KINDS[1:], "gather_rest_start")

    relay = {}

    def rest_arrived(after):
        bufs = _gather_wait(rest_started, after, _KINDS[1:], "gather_rest_wait")
        relay["started"] = _inplace_start(_relay_copies, 3, bufs, _KINDS[1:], "gather_rest_relay_start")
        return relay["started"][-1]

    def rest_weights(after):
        g_a, g_b, g_out, wup, g_down = _inplace_wait(_relay_copies, relay["started"], after, _KINDS[1:],
                                                     "gather_rest_relay_wait")
        return full(g_a), full(g_b), full(g_out), wup, full(g_down)
    gate_rows = g_gates.reshape(16, D_MODEL)
    wsmall_t = jnp.zeros((N_SMALL, D_MODEL), BF16)
    for blk, (lo, hi) in enumerate(((0, 4), (4, 8), (8, 16))):
        wsmall_t = wsmall_t.at[blk * LANES:blk * LANES + hi - lo].set(gate_rows[lo:hi])
    full = lambda g: g.reshape(-1, g.shape[2])
    p = {n: w[n] for n in _SMALL}
    p["conv_w"] = jnp.transpose(g_cw, (1, 0, 2)).reshape(3, -1)

    groups = {}

    def on_grads(group, gs):
        names = list(gs)
        kinds = [_KINDS[_BIG.index(n)] for n in names]
        whole = [g if k == "cols" else g.reshape(4, -1, g.shape[1]) for g, k in zip(gs.values(), kinds)]
        started = _exchange_start(_sibling_copies, 1, whole, [_half_shape(g, k) for g, k in zip(whole, kinds)], kinds,
                                  "grads_to_sibling_start_" + group)
        groups[group] = dict(names=names, kinds=kinds, sibling=started)
        return started[-1]

    def advance(group, after):
        G = groups[group]
        whole, got = _exchange_wait(_sibling_copies, G["sibling"], after, G["kinds"], "grads_to_sibling_wait_" + group)
        sums = [_add_halves(g, r, cvec, k, "add_sibling_" + n) for g, r, k, n in zip(whole, got, G["kinds"], G["names"])]
        G["chips"] = _exchange_start(_chip_copies, 3, sums, [_land_shape(s, k) for s, k in zip(sums, G["kinds"])],
                                     G["kinds"], "grads_to_chips_start_" + group)
        return G["chips"][-1]

    loss_row, grad_x, big, small = _local_step(x[0], loss_target[0], full(wmain_t), wsmall_t, rest_arrived, rest_weights,
                                               p, on_grads, advance, rest_started[-1])
    gt = big["wsmall_t"]
    small["w_in_gates"] = jnp.concatenate([gt[0:4], gt[LANES:LANES + 4], gt[2 * LANES:2 * LANES + 8]], axis=0)
    small_names = _SMALL + ["conv_w"]
    packed_names = small_names + ["w_in_gates"]
    pack = _pack_rows([small[n] for n in packed_names] + [loss_row])
    spread = _exchange_start(_spread_copies, N_DEV - 1, [pack], [(N_DEV,) + pack.shape], None, "allreduce_start", zeroed=True)

    def my_half(group, after):
        G = groups[group]
        sums, got = _exchange_wait(_chip_copies, G["chips"], after, G["kinds"], "grads_to_chips_wait_" + group)
        return [_add_chips(s, r, kcvec, k, "add_chips_" + n) for s, r, k, n in zip(sums, got, G["kinds"], G["names"])]

    first_names = groups["ffn"]["names"] + groups["mix"]["names"]
    join_first = _inplace_start(_join_copies, 1, my_half("ffn", spread[-1]) + my_half("mix", spread[-1]), None,
                                "grads_join_start")
    join_in = _inplace_start(_join_copies, 1, my_half("in", join_first[-1]), None, "grads_join_start_in")
    grads = dict(zip(first_names, _inplace_wait(_join_copies, join_first, join_in[-1], None, "grads_join_wait")))

    delta, new_m, new_v = {}, {}, {}
    for n in _BIG[1:]:
        delta[n], new_m[n], new_v[n] = _adamw(w[n], grads[n], mom[n], var[n], "adamw_" + n)
        grads[n] = grads[n][None]
    grads["w_in"], = _inplace_wait(_join_copies, join_in, delta[_BIG[-1]], None, "grads_join_wait_in")

    (pack,), (gathered,) = _exchange_wait(_spread_copies, spread, delta[_BIG[-1]], None, "allreduce_wait")
    full_shapes = [small[n].shape if n in ("conv_w", "w_in_gates") else w[n][0].shape for n in packed_names]
    total = _unpack_rows(_sum_spread(pack, gathered), full_shapes + [loss_row.shape])
    for n, t in zip(packed_names, total):
        grads[n] = t
    loss = total[-1][0, 0]
    grads["conv_w"] = lax.dynamic_slice_in_dim(grads["conv_w"], kme * conv_w.shape[2], conv_w.shape[2], axis=1)
    my_gates = lax.dynamic_slice_in_dim(grads.pop("w_in_gates"), 4 * kme, 4, axis=0)
    g_in = jnp.zeros(w["w_in"].shape[1:], F32)
    g_in = lax.dynamic_update_slice_in_dim(g_in, grads["w_in"], 4 * odd, axis=0)
    grads["w_in"] = lax.dynamic_update_slice_in_dim(g_in, my_gates, 2048 * (1 - odd), axis=0)
    delta["w_in"], new_m["w_in"], new_v["w_in"] = _adamw(w["w_in"], grads["w_in"], mom["w_in"], var["w_in"], "adamw_w_in")
    grads["w_in"] = grads["w_in"][None]
    for d in (grads, delta, new_m, new_v):
        d["w_in"] = tr3(d["w_in"])
    packs = [_pack_rows([d[n][0] for n in small_names]) for d in (w, mom, var)]
    pad = ((0, (-packs[0].shape[0]) % 8), (0, 0))
    packs = [jnp.pad(t, pad)[None] for t in packs]
    gp = jnp.pad(_pack_rows([grads[n] for n in small_names]), pad)
    shapes = [w[n][0].shape for n in small_names]
    for dst, res in zip((delta, new_m, new_v), _adamw(packs[0], gp, packs[1], packs[2], "adamw_small")):
        for n, t in zip(small_names, _unpack_rows(res[0], shapes)):
            dst[n] = t[None]
    for n in small_names:
        grads[n] = grads[n][None]

    return (loss, grad_x[None], *[grads[n] for n in _WEIGHTS], *[delta[n] for n in _WEIGHTS],
            *[new_m[n] for n in _WEIGHTS], *[new_v[n] for n in _WEIGHTS])
```

```python
import functools
import math

import jax
import jax.numpy as jnp
from jax import lax
from jax.experimental import pallas as pl
from jax.experimental.pallas import tpu as pltpu

F32 = jnp.float32
BF16 = jnp.bfloat16
MESH = pl.DeviceIdType.MESH

D_MODEL = 1024
ML_HEADS = 4
ML_DQK = 128
ML_DV = 256
FOX_HEADS = 8
FOX_DH = 128
D_FF = 2816
GATE_CAP = 15.0
EPS = 1e-6
ADAM_LR, ADAM_B1, ADAM_B2, ADAM_EPS, ADAM_WD, ADAM_STEP = 0.001, 0.9, 0.999, 1e-08, 0.01, 10

LANES = 128
MLC = 256
FOX_TQ = 512
FOX_TQ_FWD = 512
FOX_TK = 512
FOX_TK_FWD = 512
ROW_T = 512
CONV_TC = 1408
VMEM_LIMIT = 56 * 1024 * 1024

C_QM, C_KM, C_VM, C_OM = 0, 512, 1024, 2048
N_ML, N_FOX, N_GATE = 3072, 3072, 2048
N_SMALL = 384


def _cparams(sem=None):
    return pltpu.CompilerParams(dimension_semantics=sem, vmem_limit_bytes=VMEM_LIMIT)


def _tile(n, target):
    if n <= target:
        return n
    best = None
    for t in range(LANES, target + 1, LANES):
        if n % t == 0:
            best = t
    assert best is not None, (n, target)
    return best


def _dot(a, b, dims):
    return lax.dot_general(a, b, (dims, ((), ())), preferred_element_type=F32)


def _dot_nn(a, b):
    return _dot(a, b, ((1,), (0,)))


def _dot_nt(a, b):
    return _dot(a, b, ((1,), (1,)))


def _dot_tn(a, b):
    return _dot(a, b, ((0,), (0,)))


_DOTS = {"nn": _dot_nn, "nt": _dot_nt, "tn": _dot_tn}


def _mm(a, b, mode, out_dtype, name, tm=1024, tn=1408, tk=1408, after=None):
    a_parts = list(a) if isinstance(a, (list, tuple)) else [a]
    b_parts = list(b) if isinstance(b, (list, tuple)) else [b]
    extra = [] if after is None else [after]
    assert len(a_parts) == 1 or len(b_parts) == 1, name
    a_axes = {"nn": "ik", "nt": "ik", "tn": "ki"}[mode]
    b_axes = {"nn": "kj", "nt": "jk", "tn": "kj"}[mode]
    size, target = {}, dict(i=tm, j=tn, k=tk)
    for parts, axes in ((a_parts, a_axes), (b_parts, b_axes)):
        dims = (parts[0].shape[0], parts[0].shape[1] * len(parts))
        for ax, n in zip(axes, dims):
            assert size.setdefault(ax, n) == n, (name, ax, n, size)
    tile = {}
    for parts, axes in ((a_parts, a_axes), (b_parts, b_axes)):
        if len(parts) > 1:
            tile[axes[1]] = _tile(parts[0].shape[1], target[axes[1]])
    for ax in "ijk":
        tile.setdefault(ax, _tile(size[ax], target[ax]))
    M, N, nk = size["i"], size["j"], size["k"] // tile["k"]
    grid_pos = dict(i=0, j=1, k=2)
    dot = _DOTS[mode]

    def specs(parts, axes):
        blk = (tile[axes[0]], tile[axes[1]])
        if len(parts) == 1:
            return [pl.BlockSpec(blk, lambda *g: (g[grid_pos[axes[0]]], g[grid_pos[axes[1]]]))], None
        bpp = parts[0].shape[1] // blk[1]

        def index(p):
            def f(*g):
                g0, g1 = g[grid_pos[axes[0]]], g[grid_pos[axes[1]]]
                on = g1 // bpp == p
                return jnp.where(on, g0, 0), jnp.where(on, g1 % bpp, 0)
            return f

        return [pl.BlockSpec(blk, index(p)) for p in range(len(parts))], (axes[1], bpp)

    a_specs, a_sel = specs(a_parts, a_axes)
    b_specs, b_sel = specs(b_parts, b_axes)
    na, nb = len(a_parts), len(b_parts)

    def body(*refs):
        a_refs, b_refs = refs[:na], refs[na:na + nb]
        o_ref, acc = refs[na + nb + len(extra)], refs[na + nb + len(extra) + 1:]

        def accumulate(part):
            if nk == 1:
                o_ref[...] = part.astype(o_ref.dtype)
                return
            acc_ref, = acc
            k = pl.program_id(2)

            @pl.when(k == 0)
            def _():
                acc_ref[...] = part

            @pl.when(k > 0)
            def _():
                acc_ref[...] += part

            @pl.when(k == nk - 1)
            def _():
                o_ref[...] = acc_ref[...].astype(o_ref.dtype)

        sel = a_sel or b_sel
        if sel is None:
            accumulate(dot(a_refs[0][...], b_refs[0][...]))
        else:
            which = pl.program_id(grid_pos[sel[0]]) // sel[1]
            for p in range(max(na, nb)):
                @pl.when(which == p)
                def _(p=p):
                    accumulate(dot(a_refs[p if a_sel else 0][...], b_refs[p if b_sel else 0][...]))

    return pl.pallas_call(
        body, name=name,
        out_shape=jax.ShapeDtypeStruct((M, N), out_dtype),
        grid=(M // tile["i"], N // tile["j"], nk),
        in_specs=a_specs + b_specs + [pl.BlockSpec(memory_space=pl.ANY)] * len(extra),
        out_specs=pl.BlockSpec((tile["i"], tile["j"]), lambda i, j, k: (i, j)),
        scratch_shapes=[pltpu.VMEM((tile["i"], tile["j"]), F32)] if nk > 1 else [],
        compiler_params=_cparams(("parallel", "parallel", "arbitrary")),
    )(*a_parts, *b_parts, *extra)


def _mm_sum_parts(parts, b, out_dtype, name, trans_b=False, tm=1024, tn=512, after=None):
    M, K = parts[0].shape
    N = b.shape[0] if trans_b else b.shape[1]
    tm, tn = _tile(M, tm), _tile(N, tn)
    n = len(parts)
    extra = [] if after is None else [after]

    def body(*refs):
        b_ref, o_ref = refs[n], refs[n + 1 + len(extra)]
        acc = None
        for p in range(n):
            if trans_b:
                d = _dot_nt(refs[p][...], b_ref[:, p * K:(p + 1) * K])
            else:
                d = _dot_nn(refs[p][...], b_ref[p * K:(p + 1) * K, :])
            acc = d if acc is None else acc + d
        o_ref[...] = acc.astype(o_ref.dtype)

    b_spec = pl.BlockSpec((tn, n * K), lambda i, j: (j, 0)) if trans_b else pl.BlockSpec((n * K, tn), lambda i, j: (0, j))
    return pl.pallas_call(
        body, name=name, out_shape=jax.ShapeDtypeStruct((M, N), out_dtype), grid=(M // tm, N // tn),
        in_specs=[pl.BlockSpec((tm, K), lambda i, j: (i, 0))] * n + [b_spec]
        + [pl.BlockSpec(memory_space=pl.ANY)] * len(extra),
        out_specs=pl.BlockSpec((tm, tn), lambda i, j: (i, j)),
        compiler_params=_cparams(("parallel", "arbitrary")),
    )(*parts, b, *extra)


def _rstd(x):
    return lax.rsqrt(jnp.mean(x * x, axis=-1, keepdims=True) + EPS)


def _rmsnorm_fwd(x, g, name):
    S, D = x.shape
    T = _tile(S, ROW_T)

    def body(x_ref, g_ref, o_ref):
        xv = x_ref[...]
        o_ref[...] = (xv * _rstd(xv) * g_ref[...]).astype(o_ref.dtype)

    return pl.pallas_call(
        body, name=name, out_shape=jax.ShapeDtypeStruct((S, D), BF16), grid=(S // T,),
        in_specs=[pl.BlockSpec((T, D), lambda i: (i, 0)), pl.BlockSpec((1, D), lambda i: (0, 0))],
        out_specs=pl.BlockSpec((T, D), lambda i: (i, 0)),
        compiler_params=_cparams(("parallel",)),
    )(x, g)


def _resid_norm_fwd(x, z, g, g_next, name):
    S, D = x.shape
    T = _tile(S, ROW_T)

    def body(x_ref, z_ref, g_ref, gn_ref, o_ref, h_ref):
        zv = z_ref[...]
        x1 = x_ref[...] + zv * _rstd(zv) * g_ref[...]
        o_ref[...] = x1
        h_ref[...] = (x1 * _rstd(x1) * gn_ref[...]).astype(h_ref.dtype)

    row = pl.BlockSpec((T, D), lambda i: (i, 0))
    vec = pl.BlockSpec((1, D), lambda i: (0, 0))
    return pl.pallas_call(
        body, name=name, out_shape=(jax.ShapeDtypeStruct((S, D), F32), jax.ShapeDtypeStruct((S, D), BF16)),
        grid=(S // T,), in_specs=[row, row, vec, vec], out_specs=(row, row), compiler_params=_cparams(("parallel",)),
    )(x, z, g, g_next)


def _norm_chain_bwd(dh, xin, g, resid, zin, gz, name):
    S, D = xin.shape
    T = _tile(S, ROW_T)

    def body(dh_ref, x_ref, g_ref, r_ref, z_ref, gz_ref, dx_ref, dz_ref, dg_ref, dgz_ref):
        dx, dgt = _rmsnorm_bwd_math(dh_ref[...], x_ref[...], g_ref[...])
        dx = dx + r_ref[...]
        dx_ref[...] = dx
        dz, dgzt = _rmsnorm_bwd_math(dx, z_ref[...], gz_ref[...])
        dz_ref[...] = dz.astype(dz_ref.dtype)

        @pl.when(pl.program_id(0) == 0)
        def _():
            dg_ref[...] = jnp.zeros_like(dg_ref)
            dgz_ref[...] = jnp.zeros_like(dgz_ref)

        dg_ref[...] += jnp.sum(dgt, axis=0, keepdims=True)
        dgz_ref[...] += jnp.sum(dgzt, axis=0, keepdims=True)

    row = pl.BlockSpec((T, D), lambda i: (i, 0))
    vec = pl.BlockSpec((1, D), lambda i: (0, 0))
    v1 = jax.ShapeDtypeStruct((1, D), F32)
    return pl.pallas_call(
        body, name=name,
        out_shape=(jax.ShapeDtypeStruct((S, D), F32), jax.ShapeDtypeStruct((S, D), BF16), v1, v1),
        grid=(S // T,), in_specs=[row, row, vec, row, row, vec], out_specs=(row, row, vec, vec),
        compiler_params=_cparams(("arbitrary",)),
    )(dh, xin, g, resid, zin, gz)


def _rmsnorm_bwd_math(dy, xv, g):
    r = _rstd(xv)
    u = dy * g
    dx = r * u - xv * (r * r * r) * jnp.mean(u * xv, axis=-1, keepdims=True)
    return dx, dy * xv * r


def _rmsnorm_bwd(dys, xin, g, resid, out_dtype, name):
    S, D = xin.shape
    T = _tile(S, ROW_T)
    has_resid = resid is not None
    ndy = len(dys)

    def body(*refs):
        dy_refs, (x_ref, g_ref) = refs[:ndy], refs[ndy:ndy + 2]
        dx_ref, dg_ref = refs[-2:]
        dy = dy_refs[0][...]
        for r in dy_refs[1:]:
            dy = dy + r[...]
        dx, dgt = _rmsnorm_bwd_math(dy, x_ref[...], g_ref[...])
        if has_resid:
            dx = dx + refs[ndy + 2][...]
        dx_ref[...] = dx.astype(dx_ref.dtype)

        @pl.when(pl.program_id(0) == 0)
        def _():
            dg_ref[...] = jnp.zeros_like(dg_ref)

        dg_ref[...] += jnp.sum(dgt, axis=0, keepdims=True)

    row = pl.BlockSpec((T, D), lambda i: (i, 0))
    vec = pl.BlockSpec((1, D), lambda i: (0, 0))
    ins = list(dys) + [xin, g] + ([resid] if has_resid else [])
    return pl.pallas_call(
        body, name=name,
        out_shape=(jax.ShapeDtypeStruct((S, D), out_dtype), jax.ShapeDtypeStruct((1, D), F32)),
        grid=(S // T,), in_specs=[row] * ndy + [row, vec] + ([row] if has_resid else []),
        out_specs=(row, vec), compiler_params=_cparams(("arbitrary",)),
    )(*ins)


def _loss_head(x1, d, g, target, name):
    S, D = x1.shape
    T = _tile(S, ROW_T)

    def body(x_ref, d_ref, g_ref, t_ref, loss_ref, dy_ref, dd_ref, dg_ref):
        dv, gv = d_ref[...], g_ref[...]
        y = x_ref[...] + dv * _rstd(dv) * gv
        diff = y - t_ref[...]
        dy = diff * (1.0 / D)
        dy_ref[...] = dy
        dd, dgt = _rmsnorm_bwd_math(dy, dv, gv)
        dd_ref[...] = dd.astype(dd_ref.dtype)

        @pl.when(pl.program_id(0) == 0)
        def _():
            dg_ref[...] = jnp.zeros_like(dg_ref)
            loss_ref[...] = jnp.zeros_like(loss_ref)

        dg_ref[...] += jnp.sum(dgt, axis=0, keepdims=True)
        part = jnp.sum(jnp.sum(diff * diff, axis=1, keepdims=True), axis=0, keepdims=True)
        loss_ref[...] += (0.5 / D) * part

    row = pl.BlockSpec((T, D), lambda i: (i, 0))
    vec = pl.BlockSpec((1, D), lambda i: (0, 0))
    return pl.pallas_call(
        body, name=name,
        out_shape=(jax.ShapeDtypeStruct((1, LANES), F32), jax.ShapeDtypeStruct((S, D), F32),
                   jax.ShapeDtypeStruct((S, D), BF16), jax.ShapeDtypeStruct((1, D), F32)),
        grid=(S // T,), in_specs=[row, row, vec, row],
        out_specs=(pl.BlockSpec((1, LANES), lambda i: (0, 0)), row, row, vec),
        compiler_params=_cparams(("arbitrary",)),
    )(x1, d, g, target)


def _merge_fwd(ya, yb, pm, ba, bb, name):
    S, D = ya.shape
    T = _tile(S, ROW_T)

    def body(ya_ref, yb_ref, ga_ref, gb_ref, ba_ref, bb_ref, o_ref):
        sa = jax.nn.sigmoid(ga_ref[...] + ba_ref[...])
        sb = jax.nn.sigmoid(gb_ref[...] + bb_ref[...])
        o_ref[...] = (sa * ya_ref[...] + sb * yb_ref[...]).astype(o_ref.dtype)

    row = pl.BlockSpec((T, D), lambda i: (i, 0))
    vec = pl.BlockSpec((1, D), lambda i: (0, 0))
    return pl.pallas_call(
        body, name=name, out_shape=jax.ShapeDtypeStruct((S, D), BF16), grid=(S // T,),
        in_specs=[row, row, pl.BlockSpec((T, D), lambda i: (i, 0)),
                  pl.BlockSpec((T, D), lambda i: (i, 1)), vec, vec],
        out_specs=row, compiler_params=_cparams(("parallel",)),
    )(ya, yb, pm, pm, ba, bb)


def _merge_bwd(dmerged, ya, yb, pm, ba, bb, name):
    S, D = ya.shape
    T = _tile(S, ROW_T)

    def body(dm_ref, ya_ref, yb_ref, ga_ref, gb_ref, ba_ref, bb_ref,
             dya_ref, dyb_ref, dga_ref, dgb_ref, dba_ref, dbb_ref):
        dm = dm_ref[...]
        sa = jax.nn.sigmoid(ga_ref[...] + ba_ref[...])
        sb = jax.nn.sigmoid(gb_ref[...] + bb_ref[...])
        dya_ref[...] = (dm * sa).astype(dya_ref.dtype)
        dyb_ref[...] = (dm * sb).astype(dyb_ref.dtype)
        dga = dm * ya_ref[...] * sa * (1.0 - sa)
        dgb = dm * yb_ref[...] * sb * (1.0 - sb)
        dga_ref[...] = dga.astype(dga_ref.dtype)
        dgb_ref[...] = dgb.astype(dgb_ref.dtype)

        @pl.when(pl.program_id(0) == 0)
        def _():
            dba_ref[...] = jnp.zeros_like(dba_ref)
            dbb_ref[...] = jnp.zeros_like(dbb_ref)

        dba_ref[...] += jnp.sum(dga, axis=0, keepdims=True)
        dbb_ref[...] += jnp.sum(dgb, axis=0, keepdims=True)

    row = pl.BlockSpec((T, D), lambda i: (i, 0))
    vec = pl.BlockSpec((1, D), lambda i: (0, 0))
    act = jax.ShapeDtypeStruct((S, D), BF16)
    v1 = jax.ShapeDtypeStruct((1, D), F32)
    return pl.pallas_call(
        body, name=name, out_shape=(act, act, act, act, v1, v1), grid=(S // T,),
        in_specs=[row, row, row, pl.BlockSpec((T, D), lambda i: (i, 0)),
                  pl.BlockSpec((T, D), lambda i: (i, 1)), vec, vec],
        out_specs=(row, row, row, row, vec, vec), compiler_params=_cparams(("arbitrary",)),
    )(dmerged, ya, yb, pm, pm, ba, bb)


_GELU_C = math.sqrt(2.0 / math.pi)


def _gelu(g):
    t = jnp.tanh(_GELU_C * (g + 0.044715 * g * g * g))
    return 0.5 * g * (1.0 + t), t


def _gelu_grad(g, t):
    return 0.5 * (1.0 + t) + 0.5 * g * (1.0 - t * t) * _GELU_C * (1.0 + 3 * 0.044715 * g * g)


def _shift_down(v, halo_ref, first, rows):
    T = v.shape[0]
    keep = jnp.where(first, 0.0, 1.0)
    h7 = halo_ref[7:8, :] * keep
    h6 = halo_ref[6:7, :] * keep
    m1 = jnp.where(rows == 0, h7, pltpu.roll(v, 1, 0))
    m2 = jnp.where(rows == 0, h6, jnp.where(rows == 1, h7, pltpu.roll(v, 2, 0)))
    return m1, m2


def _conv_act_fwd(up, cw, cb, name):
    S, F2 = up.shape
    Fh = F2 // 2
    T = _tile(S, ROW_T)
    tc = _tile(Fh, CONV_TC)
    ncol = Fh // tc
    hb = T // 8

    def body(ua_ref, ug_ref, ha_ref, hg_ref, wa_ref, wg_ref, ba_ref, bg_ref, o_ref, a_ref, g_ref):
        first = pl.program_id(0) == 0
        rows = lax.broadcasted_iota(jnp.int32, (T, tc), 0)

        def conv(u_ref, h_ref, w_ref, b_ref):
            v = u_ref[...]
            m1, m2 = _shift_down(v, h_ref, first, rows)
            return b_ref[...] + w_ref[0:1, :] * m2 + w_ref[1:2, :] * m1 + w_ref[2:3, :] * v

        a = conv(ua_ref, ha_ref, wa_ref, ba_ref)
        g = conv(ug_ref, hg_ref, wg_ref, bg_ref)
        a_ref[...] = a
        g_ref[...] = g
        o_ref[...] = (_gelu(g)[0] * a).astype(o_ref.dtype)

    halo = lambda off: pl.BlockSpec((8, tc), lambda i, j: (jnp.maximum(i * hb - 1, 0), j + off))
    blk = pl.BlockSpec((T, tc), lambda i, j: (i, j))
    f32 = jax.ShapeDtypeStruct((S, Fh), F32)
    return pl.pallas_call(
        body, name=name, out_shape=(jax.ShapeDtypeStruct((S, Fh), BF16), f32, f32), grid=(S // T, ncol),
        in_specs=[blk, pl.BlockSpec((T, tc), lambda i, j: (i, j + ncol)),
                  halo(0), halo(ncol),
                  pl.BlockSpec((3, tc), lambda i, j: (0, j)), pl.BlockSpec((3, tc), lambda i, j: (0, j + ncol)),
                  pl.BlockSpec((1, tc), lambda i, j: (0, j)), pl.BlockSpec((1, tc), lambda i, j: (0, j + ncol))],
        out_specs=(blk, blk, blk),
        compiler_params=_cparams(("parallel", "parallel")),
    )(up, up, up, up, cw, cw, cb, cb)


def _conv_act_bwd(up, a, g, dact, cw, name):
    S, F2 = up.shape
    Fh = F2 // 2
    T = _tile(S, ROW_T)
    tc = _tile(Fh, CONV_TC)
    ncol, nrow, hb, nhb = Fh // tc, S // T, T // 8, S // 8

    def body(ua_ref, ug_ref, a_ref, g_ref, an_ref, gn_ref, wa_ref, wg_ref, da_ref, dn_ref,
             dpa_ref, dpg_ref, dwa_ref, dwg_ref, dba_ref, dbg_ref, dua_n, dug_n):
        i = pl.program_id(1)
        rows = lax.broadcasted_iota(jnp.int32, (T, tc), 0)

        def du_of(a, g, dact_v):
            gel, t = _gelu(g)
            return dact_v * gel, dact_v * a * _gelu_grad(g, t)

        dua, dug = du_of(a_ref[...], g_ref[...], da_ref[...])
        keep = jnp.where(i == nrow - 1, 0.0, 1.0)
        dua_n[...], dug_n[...] = du_of(an_ref[...], gn_ref[...], dn_ref[...] * keep)

        @pl.when(i == 0)
        def _():
            for r in (dwa_ref, dwg_ref, dba_ref, dbg_ref):
                r[...] = jnp.zeros_like(r)

        for du, n_ref, u_ref, w_ref, o_ref, dw_ref, db_ref in ((dua, dua_n, ua_ref, wa_ref, dpa_ref, dwa_ref, dba_ref),
                                                               (dug, dug_n, ug_ref, wg_ref, dpg_ref, dwg_ref, dbg_ref)):
            n0, n1 = n_ref[0:1, :], n_ref[1:2, :]
            du1 = jnp.where(rows == T - 1, n0, pltpu.roll(du, T - 1, 0))
            du2 = jnp.where(rows == T - 2, n0, jnp.where(rows == T - 1, n1, pltpu.roll(du, T - 2, 0)))
            o_ref[...] = (w_ref[2:3, :] * du + w_ref[1:2, :] * du1 + w_ref[0:1, :] * du2).astype(o_ref.dtype)
            u = u_ref[...]
            db_ref[...] += jnp.sum(du, axis=0, keepdims=True)
            for j, d in enumerate((du2, du1, du)):
                dw_ref[j:j + 1, :] += jnp.sum(d * u, axis=0, keepdims=True)

    tile = lambda off: pl.BlockSpec((T, tc), lambda j, i: (i, j + off))
    under = pl.BlockSpec((8, tc), lambda j, i: (jnp.minimum((i + 1) * hb, nhb - 1), j))
    vec = lambda n, off: pl.BlockSpec((n, tc), lambda j, i: (0, j + off))
    act = jax.ShapeDtypeStruct((S, Fh), BF16)
    return pl.pallas_call(
        body, name=name,
        out_shape=(act, act, jax.ShapeDtypeStruct((3, Fh), F32), jax.ShapeDtypeStruct((3, Fh), F32),
                   jax.ShapeDtypeStruct((1, Fh), F32), jax.ShapeDtypeStruct((1, Fh), F32)),
        grid=(ncol, nrow),
        in_specs=[tile(0), tile(ncol), tile(0), tile(0), under, under, vec(3, 0), vec(3, ncol), tile(0), under],
        out_specs=(tile(0), tile(0), vec(3, 0), vec(3, 0), vec(1, 0), vec(1, 0)),
        scratch_shapes=[pltpu.VMEM((8, tc), F32), pltpu.VMEM((8, tc), F32)],
        compiler_params=_cparams(("parallel", "arbitrary")),
    )(up, up, a, g, a, g, cw, cw, dact, dact)


def _split3(x):
    hi = x.astype(BF16)
    r1 = x - hi.astype(F32)
    mid = r1.astype(BF16)
    lo = (r1 - mid.astype(F32)).astype(BF16)
    return hi, mid, lo


def _tri_dot(tri, x):
    hi, mid, lo = _split3(x)
    return _dot_nn(tri, hi) + _dot_nn(tri, mid) + _dot_nn(tri, lo)


def _log_sigmoid(x):
    return jnp.minimum(x, 0.0) - jnp.log(1.0 + jnp.exp(-jnp.abs(x)))


def _tri_mask(n, lower):
    r = lax.broadcasted_iota(jnp.int32, (n, n), 0)
    c = lax.broadcasted_iota(jnp.int32, (n, n), 1)
    return (r >= c) if lower else (r <= c)


def _gates_fwd(ps, bi, bf, bff, name):
    S = ps.shape[0]
    NC = S // MLC

    def body(ps_ref, bi_ref, bf_ref, bff_ref, a_ref, A_ref, wi_ref, em_ref, wk_ref, dec_ref, F_ref, m_scr, f_scr):
        @pl.when(pl.program_id(0) == 0)
        def _():
            m_scr[...] = jnp.zeros_like(m_scr)
            f_scr[...] = jnp.zeros_like(f_scr)

        rows = lax.broadcasted_iota(jnp.int32, (MLC, LANES), 0)
        ltri = _tri_mask(MLC, True).astype(BF16)
        li = GATE_CAP * jnp.tanh((ps_ref[:, 0:LANES] + bi_ref[...]) / GATE_CAP)
        lf = _log_sigmoid(GATE_CAP * jnp.tanh((ps_ref[:, LANES:2 * LANES] + bf_ref[...]) / GATE_CAP))
        b = _tri_dot(ltri, lf)
        a = li - b
        cm = a
        sh = 1
        while sh < MLC:
            cm = jnp.where(rows >= sh, jnp.maximum(cm, pltpu.roll(cm, sh, 0)), cm)
            sh *= 2
        m0 = m_scr[...]
        A = jnp.maximum(cm, m0)
        a_ref[...] = a
        A_ref[...] = A
        A_last = A_ref[MLC - 1:MLC, :]
        wi_ref[...] = jnp.exp(m0 - A)
        em_ref[...] = jnp.exp(-(b + A))
        wk_ref[...] = jnp.exp(a - A_last)
        dec_ref[0] = jnp.exp(m0 - A_last)
        F_ref[...] = b
        m_scr[...] = F_ref[MLC - 1:MLC, :] + A_last
        lfg = _log_sigmoid(ps_ref[:, 2 * LANES:3 * LANES] + bff_ref[...])
        F_ref[...] = _tri_dot(ltri, lfg) + f_scr[...]
        f_scr[...] = F_ref[MLC - 1:MLC, :]

    col = pl.BlockSpec((MLC, LANES), lambda c: (c, 0))
    vec = pl.BlockSpec((1, LANES), lambda c: (0, 0))
    cs = jax.ShapeDtypeStruct((S, LANES), F32)
    return pl.pallas_call(
        body, name=name,
        out_shape=(cs, cs, cs, cs, cs, jax.ShapeDtypeStruct((NC, 1, LANES), F32), cs),
        grid=(NC,), in_specs=[pl.BlockSpec((MLC, N_SMALL), lambda c: (c, 0)), vec, vec, vec],
        out_specs=(col, col, col, col, col, pl.BlockSpec((1, 1, LANES), lambda c: (c, 0, 0)), col),
        scratch_shapes=[pltpu.VMEM((1, LANES), F32), pltpu.VMEM((1, LANES), F32)],
        compiler_params=_cparams(("arbitrary",)),
    )(ps, bi, bf, bff)


def _gates_bwd(ps, bi, bf, bff, rk, kc, tch, dF, name):
    S = ps.shape[0]
    NC = S // MLC

    def body(ps_ref, bi_ref, bf_ref, bff_ref, rk_ref, kc_ref, t_ref, dF_ref, dps_ref, db_ref, carry):
        @pl.when(pl.program_id(0) == 0)
        def _():
            carry[...] = jnp.zeros_like(carry)
            db_ref[...] = jnp.zeros_like(db_ref)

        lanes = lax.broadcasted_iota(jnp.int32, (MLC, LANES), 1)
        utri = _tri_mask(MLC, False).astype(BF16)
        ti = jnp.tanh((ps_ref[:, 0:LANES] + bi_ref[...]) / GATE_CAP)
        t_end, t_start = t_ref[0, 0:1, :], t_ref[0, 1:2, :]
        rk = rk_ref[...]
        rk = rk - (jnp.sum(rk, axis=0, keepdims=True) - (t_start - t_end)) * (1.0 / MLC)
        dpi = jnp.where(lanes < ML_HEADS, (kc_ref[...] - rk) * (1.0 - ti * ti), 0.0)
        tf = jnp.tanh((ps_ref[:, LANES:2 * LANES] + bf_ref[...]) / GATE_CAP)
        dlf = _tri_dot(utri, rk) + t_end
        dpf = jnp.where(lanes < ML_HEADS, dlf * jax.nn.sigmoid(-GATE_CAP * tf) * (1.0 - tf * tf), 0.0)
        dFv = dF_ref[...]
        dlfg = _tri_dot(utri, dFv) + carry[...]
        carry[...] += jnp.sum(dFv, axis=0, keepdims=True)
        dpff = jnp.where(lanes < FOX_HEADS, dlfg * jax.nn.sigmoid(-(ps_ref[:, 2 * LANES:3 * LANES] + bff_ref[...])), 0.0)
        for n, dp in enumerate((dpi, dpf, dpff)):
            dps_ref[:, n * LANES:(n + 1) * LANES] = dp.astype(dps_ref.dtype)
            db_ref[:, n * LANES:(n + 1) * LANES] += jnp.sum(dp, axis=0, keepdims=True)

    rev = lambda c: (NC - 1 - c, 0)
    col = pl.BlockSpec((MLC, LANES), rev)
    vec = pl.BlockSpec((1, LANES), lambda c: (0, 0))
    wide = pl.BlockSpec((MLC, N_SMALL), rev)
    return pl.pallas_call(
        body, name=name,
        out_shape=(jax.ShapeDtypeStruct((S, N_SMALL), BF16), jax.ShapeDtypeStruct((1, N_SMALL), F32)),
        grid=(NC,),
        in_specs=[wide, vec, vec, vec, col, col, pl.BlockSpec((1, 2, LANES), lambda c: (NC - 1 - c, 0, 0)), col],
        out_specs=(wide, pl.BlockSpec((1, N_SMALL), lambda c: (0, 0))),
        scratch_shapes=[pltpu.VMEM((1, LANES), F32)],
        compiler_params=_cparams(("arbitrary",)),
    )(ps, bi, bf, bff, rk, kc, tch, dF)


_ML_SCALE = ML_DQK ** -0.5


def _ml_specs(rev, NC):
    idx = (lambda c: NC - 1 - c) if rev else (lambda c: c)
    qk = lambda blk: pl.BlockSpec((MLC, ML_HEADS * ML_DQK), lambda c: (idx(c), blk))
    wide = lambda blk: pl.BlockSpec((MLC, D_MODEL), lambda c: (idx(c), blk))
    col = pl.BlockSpec((MLC, LANES), lambda c: (idx(c), 0))
    return idx, qk, wide, col


def _ml_intra(q_ref, k_ref, arow_ref, A_ref, h):
    hs = slice(h * ML_DQK, (h + 1) * ML_DQK)
    qf = q_ref[:, hs] * _ML_SCALE
    kf = k_ref[:, hs]
    qb, kb = qf.astype(BF16), kf.astype(BF16)
    qk = _dot_nt(qb, kb)
    logw = arow_ref[h:h + 1, :] - A_ref[:, h:h + 1]
    W = jnp.exp(jnp.where(_tri_mask(MLC, True), logw, -1e30))
    return qb, kb, qf, kf, qk, W


def _mlstm_fwd(pm, a_row, A, wi, em, wk, dec, w_hn, name):
    S = pm.shape[0]
    NC = S // MLC
    _, qk, wide, col = _ml_specs(False, NC)

    def body(q_ref, k_ref, v_ref, o_ref, arow_ref, A_ref, wi_ref, em_ref, wk_ref, dec_ref, whn_ref,
             ha_ref, hp_ref, den_ref, cst_ref, nst_ref, C_scr, n_scr):
        @pl.when(pl.program_id(0) == 0)
        def _():
            C_scr[...] = jnp.zeros_like(C_scr)
            n_scr[...] = jnp.zeros_like(n_scr)

        lanes = lax.broadcasted_iota(jnp.int32, (MLC, LANES), 1)
        den_tile = jnp.zeros((MLC, LANES), F32)
        for h in range(ML_HEADS):
            vs = slice(h * ML_DV, (h + 1) * ML_DV)
            qb, kb, qf, kf, qk_, W = _ml_intra(q_ref, k_ref, arow_ref, A_ref, h)
            vb = v_ref[:, vs].astype(BF16)
            Cf = C_scr[h]
            Cb = Cf.astype(BF16)
            nrow = n_scr[h]
            cst_ref[0, h] = Cb
            nst_ref[0, h] = nrow
            s = qk_ * W
            wic = wi_ref[:, h:h + 1]
            num = _dot_nn(s.astype(BF16), vb) + wic * _dot_nt(qb, Cb)
            den = jnp.sum(s, axis=1, keepdims=True) + wic * jnp.sum(qf * nrow, axis=1, keepdims=True)
            hp = num / jnp.maximum(jnp.abs(den), em_ref[:, h:h + 1])
            hp_ref[:, vs] = hp
            den_tile = jnp.where(lanes == h, den, den_tile)
            hn = hp * _rstd(hp) * whn_ref[:, vs]
            ha_ref[:, vs] = (hn * jax.nn.sigmoid(o_ref[:, vs])).astype(ha_ref.dtype)
            wkc = wk_ref[:, h:h + 1]
            kw = kf * wkc
            d = dec_ref[0, :, h:h + 1]
            C_scr[h] = d * Cf + _dot_tn(vb, kw.astype(BF16))
            n_scr[h] = d * nrow + jnp.sum(kw, axis=0, keepdims=True)
        den_ref[...] = den_tile

    return pl.pallas_call(
        body, name=name,
        out_shape=(jax.ShapeDtypeStruct((S, D_MODEL), BF16), jax.ShapeDtypeStruct((S, D_MODEL), F32),
                   jax.ShapeDtypeStruct((S, LANES), F32),
                   jax.ShapeDtypeStruct((NC, ML_HEADS, ML_DV, ML_DQK), BF16),
                   jax.ShapeDtypeStruct((NC, ML_HEADS, 1, ML_DQK), F32)),
        grid=(NC,),
        in_specs=[qk(C_QM // 512), qk(C_KM // 512), wide(C_VM // D_MODEL), wide(C_OM // D_MODEL),
                  pl.BlockSpec((8, MLC), lambda c: (0, c)), col, col, col, col,
                  pl.BlockSpec((1, 1, LANES), lambda c: (c, 0, 0)), pl.BlockSpec((1, D_MODEL), lambda c: (0, 0))],
        out_specs=(pl.BlockSpec((MLC, D_MODEL), lambda c: (c, 0)), pl.BlockSpec((MLC, D_MODEL), lambda c: (c, 0)),
                   col, pl.BlockSpec((1, ML_HEADS, ML_DV, ML_DQK), lambda c: (c, 0, 0, 0)),
                   pl.BlockSpec((1, ML_HEADS, 1, ML_DQK), lambda c: (c, 0, 0, 0))),
        scratch_shapes=[pltpu.VMEM((ML_HEADS, ML_DV, ML_DQK), F32), pltpu.VMEM((ML_HEADS, 1, ML_DQK), F32)],
        compiler_params=_cparams(("arbitrary",)),
    )(pm, pm, pm, pm, a_row, A, wi, em, wk, dec, w_hn)


def _mlstm_bwd(dha, pm, hp_all, den_all, a_row, A, wi, em, wk, dec, cst, nst, w_hn, name):
    S = pm.shape[0]
    NC = S // MLC
    idx, qk, wide, col = _ml_specs(True, NC)

    def body(dha_ref, q_ref, k_ref, v_ref, o_ref, hp_ref, den_ref, arow_ref, A_ref, wi_ref, em_ref, wk_ref,
             dec_ref, cst_ref, nst_ref, whn_ref,
             dqk_ref, dv_ref, do_ref, rk_ref, kc_ref, t_ref, dwhn_ref, dC_scr, dn_scr, t_scr):
        @pl.when(pl.program_id(0) == 0)
        def _():
            dC_scr[...] = jnp.zeros_like(dC_scr)
            dn_scr[...] = jnp.zeros_like(dn_scr)
            t_scr[...] = jnp.zeros_like(t_scr)
            dwhn_ref[...] = jnp.zeros_like(dwhn_ref)

        lanes = lax.broadcasted_iota(jnp.int32, (MLC, LANES), 1)
        lane1 = lax.broadcasted_iota(jnp.int32, (1, LANES), 1)
        t_ref[0, 0:1, :] = t_scr[...]
        rk_tile = jnp.zeros((MLC, LANES), F32)
        kc_tile = jnp.zeros((MLC, LANES), F32)
        t_new = jnp.zeros((1, LANES), F32)
        for h in range(ML_HEADS):
            hs = slice(h * ML_DQK, (h + 1) * ML_DQK)
            vs = slice(h * ML_DV, (h + 1) * ML_DV)
            hp = hp_ref[:, vs]
            sig = jax.nn.sigmoid(o_ref[:, vs])
            whn = whn_ref[:, vs]
            r = _rstd(hp)
            dga = dha_ref[:, vs]
            do_ref[:, vs] = (dga * (hp * r * whn) * sig * (1.0 - sig)).astype(do_ref.dtype)
            dhn = dga * sig
            dhp, dwt = _rmsnorm_bwd_math(dhn, hp, whn)
            dwhn_ref[:, vs] += jnp.sum(dwt, axis=0, keepdims=True)
            den = den_ref[:, h:h + 1]
            floor = em_ref[:, h:h + 1]
            D = jnp.maximum(jnp.abs(den), floor)
            dnum = dhp / D
            dh_h = jnp.sum(dhp * hp, axis=1, keepdims=True)
            active = jnp.abs(den) >= floor
            dden = -dh_h / D * jnp.where(active, jnp.sign(den), 0.0)
            phi = jnp.where(active, 0.0, dh_h)
            qb, kb, qf, kf, qk_, W = _ml_intra(q_ref, k_ref, arow_ref, A_ref, h)
            vf = v_ref[:, vs]
            vb = vf.astype(BF16)
            Cb = cst_ref[0, h]
            nrow = nst_ref[0, h]
            wic = wi_ref[:, h:h + 1]
            wkc = wk_ref[:, h:h + 1]
            d = dec_ref[0, :, h:h + 1]
            dCn = dC_scr[h]
            dCb = dCn.astype(BF16)
            dnn = dn_scr[h]
            dnumb = dnum.astype(BF16)
            s = qk_ * W
            ds = (_dot_nt(dnumb, vb) + dden) * W
            dsb = ds.astype(BF16)
            dnw = (wic * dnum).astype(BF16)
            wd = wic * dden
            kw = kf * wkc
            dv_state = _dot_nt(kw.astype(BF16), dCb)
            dq = _dot_nn(dsb, kb) + _dot_nn(dnw, Cb) + wd * nrow
            dk_state = wkc * (_dot_nn(vb, dCb) + dnn)
            dk = _dot_tn(dsb, qb) + dk_state
            dv = _dot_tn(s.astype(BF16), dnumb) + dv_state
            dC = d * dCn + _dot_tn(dnw, qb)
            dn = d * dnn + jnp.sum(wd * qf, axis=0, keepdims=True)
            dC_scr[h] = dC
            dn_scr[h] = dn
            dqk_ref[:, hs] = (dq * _ML_SCALE).astype(dqk_ref.dtype)
            dqk_ref[:, C_KM + h * ML_DQK:C_KM + (h + 1) * ML_DQK] = dk.astype(dqk_ref.dtype)
            dv_ref[:, vs] = dv.astype(dv_ref.dtype)
            G = ds * qk_
            inter = _dot_nt(qb, Cb)
            qn = jnp.sum(qf * nrow, axis=1, keepdims=True)
            R = (jnp.sum(G, axis=1, keepdims=True)
                 + wic * (jnp.sum(dnum * inter, axis=1, keepdims=True) + dden * qn))
            K = jnp.sum(G.T, axis=1, keepdims=True) + jnp.sum(kf * dk_state, axis=1, keepdims=True)
            rk_tile = jnp.where(lanes == h, R - K, rk_tile)
            kc_tile = jnp.where(lanes == h, phi, kc_tile)
            tt = (jnp.sum(jnp.sum(dC * Cb.astype(F32), axis=1, keepdims=True), axis=0, keepdims=True)
                  + jnp.sum(dn * nrow, axis=1, keepdims=True))
            t_new = jnp.where(lane1 == h, tt, t_new)
        rk_ref[...] = rk_tile
        kc_ref[...] = kc_tile
        t_ref[0, 1:2, :] = t_new
        t_scr[...] = t_new

    act = lambda n: jax.ShapeDtypeStruct((S, n), BF16)
    cs = jax.ShapeDtypeStruct((S, LANES), F32)
    rowblk = lambda n: pl.BlockSpec((MLC, n), lambda c: (idx(c), 0))
    return pl.pallas_call(
        body, name=name,
        out_shape=(act(D_MODEL), act(D_MODEL), act(D_MODEL), cs, cs,
                   jax.ShapeDtypeStruct((NC, 2, LANES), F32), jax.ShapeDtypeStruct((1, D_MODEL), F32)),
        grid=(NC,),
        in_specs=[rowblk(D_MODEL), qk(C_QM // 512), qk(C_KM // 512), wide(C_VM // D_MODEL), wide(C_OM // D_MODEL),
                  rowblk(D_MODEL), col, pl.BlockSpec((8, MLC), lambda c: (0, idx(c))), col, col, col, col,
                  pl.BlockSpec((1, 1, LANES), lambda c: (idx(c), 0, 0)),
                  pl.BlockSpec((1, ML_HEADS, ML_DV, ML_DQK), lambda c: (idx(c), 0, 0, 0)),
                  pl.BlockSpec((1, ML_HEADS, 1, ML_DQK), lambda c: (idx(c), 0, 0, 0)),
                  pl.BlockSpec((1, D_MODEL), lambda c: (0, 0))],
        out_specs=(rowblk(D_MODEL), rowblk(D_MODEL), rowblk(D_MODEL), col, col,
                   pl.BlockSpec((1, 2, LANES), lambda c: (idx(c), 0, 0)), pl.BlockSpec((1, D_MODEL), lambda c: (0, 0))),
        scratch_shapes=[pltpu.VMEM((ML_HEADS, ML_DV, ML_DQK), F32), pltpu.VMEM((ML_HEADS, 1, ML_DQK), F32),
                        pltpu.VMEM((1, LANES), F32)],
        compiler_params=_cparams(("arbitrary",)),
    )(dha, pm, pm, pm, pm, hp_all, den_all, a_row, A, wi, em, wk, dec, cst, nst, w_hn)


_FOX_SCALE = FOX_DH ** -0.5
_NEG = -1e30
_LOG2E = 1.4426950408889634
_LN2 = 0.6931471805599453
_QF_BLK, _KF_BLK, _VF_BLK = 0, FOX_HEADS, 2 * FOX_HEADS


def _lane_pick(tile, lane):
    lanes = lax.broadcasted_iota(jnp.int32, tile.shape, 1)
    return jnp.sum(jnp.where(lanes == lane, tile, 0.0), axis=1, keepdims=True)


def _col_to_row(col):
    return jnp.max(jnp.broadcast_to(col, (col.shape[0], LANES)).T, axis=0, keepdims=True)


def _causal(q0, k0, shape, q_axis):
    qpos = q0 + lax.broadcasted_iota(jnp.int32, shape, q_axis)
    kpos = k0 + lax.broadcasted_iota(jnp.int32, shape, 1 - q_axis)
    return kpos <= qpos


def _fox_fwd(pf, fc, fk_row, name):
    S = pf.shape[0]
    TQ, TK = FOX_TQ_FWD, FOX_TK_FWD
    nq, nk = S // TQ, S // TK
    c1 = _FOX_SCALE * _LOG2E

    def body(q_ref, k_ref, v_ref, fc_ref, fr_ref, o_ref, lse_ref):
        h, i = pl.program_id(0), pl.program_id(1)
        qb = q_ref[...]
        fq2 = _lane_pick(fc_ref[...], h) * _LOG2E

        def step(j, carry, masked):
            m, l, acc = carry
            off = pl.multiple_of(j * TK, TK)
            t = _dot_nt(qb, k_ref[pl.ds(off, TK), :]) * c1 - fr_ref[0, j] * _LOG2E
            if masked:
                t = jnp.where(_causal(i * TQ, j * TK, (TQ, TK), 0), t, _NEG)
            m_new = jnp.maximum(m, jnp.max(t, axis=1, keepdims=True) + fq2)
            alpha = jnp.exp2(m - m_new)
            p = jnp.exp2(t + (fq2 - m_new))
            l = alpha * l + jnp.sum(p, axis=1, keepdims=True)
            acc = alpha * acc + _dot_nn(p.astype(BF16), v_ref[pl.ds(off, TK), :])
            return m_new, l, acc

        init = (jnp.full((TQ, 1), _NEG, F32), jnp.zeros((TQ, 1), F32), jnp.zeros((TQ, FOX_DH), F32))
        last = (i * TQ) // TK
        carry = lax.fori_loop(0, last, lambda j, c: step(j, c, False), init)
        for d in range(TQ // TK):
            carry = step(last + d, carry, True)
        m, l, acc = carry
        o_ref[...] = (acc / l).astype(o_ref.dtype)
        lse_ref[0, 0] = _col_to_row((m + jnp.log2(l)) * _LN2)

    head = lambda blk: pl.BlockSpec((S, FOX_DH), lambda h, i: (0, blk + h))
    return pl.pallas_call(
        body, name=name,
        out_shape=(jax.ShapeDtypeStruct((S, D_MODEL), BF16), jax.ShapeDtypeStruct((FOX_HEADS, nq, 1, TQ), F32)),
        grid=(FOX_HEADS, nq),
        in_specs=[pl.BlockSpec((TQ, FOX_DH), lambda h, i: (i, _QF_BLK + h)), head(_KF_BLK), head(_VF_BLK),
                  pl.BlockSpec((TQ, LANES), lambda h, i: (i, 0)),
                  pl.BlockSpec((1, nk, 1, TK), lambda h, i: (h, 0, 0, 0))],
        out_specs=(pl.BlockSpec((TQ, FOX_DH), lambda h, i: (i, h)),
                   pl.BlockSpec((1, 1, 1, TQ), lambda h, i: (h, i, 0, 0))),
        compiler_params=_cparams(("parallel", "arbitrary")),
    )(pf, pf, pf, fc, fk_row)


def _fox_bwd(dhb, hb, pf, lse_row, fq_row, fc, name):
    S = pf.shape[0]
    TQ, TK = FOX_TQ, FOX_TK
    nq, nk, r = S // TQ, S // TK, TK // TQ
    c1 = _FOX_SCALE * _LOG2E

    def body(q_ref, k_ref, v_ref, do_ref, o_ref, lse_ref, fq_ref, fc_ref,
             dq_ref, dk_ref, dv_ref, dFk_ref, dFq_ref, dq_acc, qside, delta, dk_acc, dv_acc, cs_acc):
        h, j = pl.program_id(0), pl.program_id(1)

        @pl.when(j == 0)
        def _():
            dq_acc[...] = jnp.zeros_like(dq_acc)
            dFq_ref[...] = jnp.zeros_like(dFq_ref)

            def fill(b, _):
                off = pl.multiple_of(b * TQ, TQ)
                prod = do_ref[pl.ds(off, TQ), :].astype(F32) * o_ref[pl.ds(off, TQ), :].astype(F32)
                delta[b] = jnp.sum(prod.T, axis=0, keepdims=True)
                qside[b] = (fq_ref[0, b] - lse_ref[0, b]) * _LOG2E
                return 0

            lax.fori_loop(0, nq, fill, 0)

        kb = k_ref[...]
        vb = v_ref[...]
        fk2 = _lane_pick(fc_ref[...], h) * _LOG2E
        dk_acc[...] = jnp.zeros_like(dk_acc)
        dv_acc[...] = jnp.zeros_like(dv_acc)
        cs_acc[...] = jnp.zeros_like(cs_acc)

        def step(i, masked):
            off = pl.multiple_of(i * TQ, TQ)
            qb = q_ref[pl.ds(off, TQ), :]
            dob = do_ref[pl.ds(off, TQ), :]
            t = _dot_nt(kb, qb) * c1 + qside[i] - fk2
            if masked:
                t = jnp.where(_causal(i * TQ, j * TK, (TK, TQ), 1), t, _NEG)
            p = jnp.exp2(t)
            dv_acc[...] += _dot_nn(p.astype(BF16), dob)
            ds = p * (_dot_nt(vb, dob) - delta[i])
            dsb = ds.astype(BF16)
            dk_acc[...] += _dot_nn(dsb, qb)
            dq_acc[pl.ds(off, TQ), :] += _dot_tn(dsb, kb)
            cs_acc[...] += jnp.sum(ds, axis=1, keepdims=True)
            dFq_ref[0, i] += jnp.sum(ds, axis=0, keepdims=True)

        for d in range(r):
            step(r * j + d, True)

        def rest(i, _):
            step(i, False)
            return 0

        lax.fori_loop(r * j + r, nq, rest, 0)
        dk_ref[...] = (dk_acc[...] * _FOX_SCALE).astype(dk_ref.dtype)
        dv_ref[...] = dv_acc[...].astype(dv_ref.dtype)
        dFk_ref[0, 0] = -_col_to_row(cs_acc[...])

        @pl.when(j == nk - 1)
        def _():
            dq_ref[...] = (dq_acc[...] * _FOX_SCALE).astype(dq_ref.dtype)

    head = lambda blk: pl.BlockSpec((S, FOX_DH), lambda h, j: (0, blk + h))
    kblk = lambda blk: pl.BlockSpec((TK, FOX_DH), lambda h, j: (j, blk + h))
    qrows = pl.BlockSpec((1, nq, 1, TQ), lambda h, j: (h, 0, 0, 0))
    act = jax.ShapeDtypeStruct((S, D_MODEL), BF16)
    return pl.pallas_call(
        body, name=name,
        out_shape=(act, act, act, jax.ShapeDtypeStruct((FOX_HEADS, nk, 1, TK), F32),
                   jax.ShapeDtypeStruct((FOX_HEADS, nq, 1, TQ), F32)),
        grid=(FOX_HEADS, nk),
        in_specs=[head(_QF_BLK), kblk(_KF_BLK), kblk(_VF_BLK), head(0), head(0), qrows, qrows,
                  pl.BlockSpec((TK, LANES), lambda h, j: (j, 0))],
        out_specs=(head(0), kblk(0), kblk(0), pl.BlockSpec((1, 1, 1, TK), lambda h, j: (h, j, 0, 0)), qrows),
        scratch_shapes=[pltpu.VMEM((S, FOX_DH), F32), pltpu.VMEM((nq, 1, TQ), F32), pltpu.VMEM((nq, 1, TQ), F32),
                        pltpu.VMEM((TK, FOX_DH), F32), pltpu.VMEM((TK, FOX_DH), F32), pltpu.VMEM((TK, 1), F32)],
        compiler_params=_cparams(("parallel", "arbitrary")),
    )(pf, pf, pf, dhb, hb, lse_row, fq_row, fc)


def _pad_lanes(v):
    return jnp.pad(v, ((0, 0), (0, LANES - v.shape[1])))


def _local_step(x, target, wmain_t, wsmall_t, rest_arrived, rest_weights, p, on_grads, advance, token):
    S = x.shape[0]
    bi, bf, bff = _pad_lanes(p["b_ml_i"]), _pad_lanes(p["b_ml_f"]), _pad_lanes(p["b_fox_f"])

    h0 = _rmsnorm_fwd(x, p["norm_mix_pre"] + token[0:1, 0:1], "norm_mix_pre")
    pm = _mm(h0, wmain_t[:N_ML], "nt", F32, "proj_mlstm")
    pf = _mm(h0, wmain_t[N_ML:N_ML + N_FOX], "nt", BF16, "proj_fox")
    pg = _mm(h0, wmain_t[N_ML + N_FOX:], "nt", F32, "proj_merge")
    ps = _mm(h0, wsmall_t, "nt", F32, "proj_gates")
    a, A, wi, em, wk, dec, Fc = _gates_fwd(ps, bi, bf, bff, "gates_fwd")
    a_row = a[:, :8].T
    ha, hp, den, cst, nst = _mlstm_fwd(pm, a_row, A, wi, em, wk, dec, p["ml_head_norm"], "mlstm_fwd")
    ft = Fc[:, :FOX_HEADS].T + rest_arrived(ha)[0, 0]
    fq_row = ft.reshape(FOX_HEADS, S // FOX_TQ, 1, FOX_TQ)
    fk_row = ft.reshape(FOX_HEADS, S // FOX_TK, 1, FOX_TK)
    hb, lse_row = _fox_fwd(pf, Fc, ft.reshape(FOX_HEADS, S // FOX_TK_FWD, 1, FOX_TK_FWD), "fox_fwd")
    wa, wb, wout, wup, wdown = rest_weights(hb)
    ya = _mm(ha, wa, "nn", F32, "branch_a")
    yb = _mm(hb, wb, "nn", F32, "branch_b")
    merged = _merge_fwd(ya, yb, pg, p["b_gate_a"], p["b_gate_b"], "merge_fwd")
    z = _mm(merged, wout, "nn", F32, "out_proj")
    x1, h2 = _resid_norm_fwd(x, z, p["norm_mix_post"], p["norm_ffn_pre"], "resid_mix")
    up = _mm(h2, wup, "nn", F32, "ffn_up")
    act, conv_a, conv_g = _conv_act_fwd(up, p["conv_w"], p["conv_b"], "conv_act_fwd")
    d = _mm(act, wdown, "nn", F32, "ffn_down", tk=D_FF)
    loss_row, dy, dd, g_norm_ffn_post = _loss_head(x1, d, p["norm_ffn_post"], target, "loss_head")
    dact = _mm(dd, wdown, "nt", F32, "d_act")
    g_wdown = _mm(act, dd, "tn", F32, "dw_down", tm=1408, tk=2048)
    dupa, dupg, dcwa, dcwg, dcba, dcbg = _conv_act_bwd(up, conv_a, conv_g, dact, p["conv_w"], "conv_act_bwd")
    g_conv_w = jnp.concatenate([dcwa, dcwg], axis=1)
    g_conv_b = jnp.concatenate([dcba, dcbg], axis=1)
    dh2 = _mm_sum_parts([dupa, dupg], wup, F32, "d_h2", trans_b=True)
    g_wup = _mm(h2, [dupa, dupg], "tn", F32, "dw_up", tk=2048)
    token = on_grads("ffn", dict(w_up=g_wup, w_down=g_wdown))
    dx1, dz, g_norm_ffn_pre, g_norm_mix_post = _norm_chain_bwd(
        dh2, x1, p["norm_ffn_pre"] + token[0:1, 0:1], dy, z, p["norm_mix_post"], "norm_chain_bwd")
    dmerged = _mm(dz, wout, "nt", F32, "d_merged")
    g_wout = _mm(merged, dz, "tn", F32, "dw_out", tk=4096)
    dya, dyb, dga, dgb, g_b_gate_a, g_b_gate_b = _merge_bwd(dmerged, ya, yb, pg, p["b_gate_a"], p["b_gate_b"], "merge_bwd")
    dha = _mm(dya, wa, "nt", F32, "d_ha")
    g_wa = _mm(ha, dya, "tn", F32, "dw_a", tk=4096)
    dhb = _mm(dyb, wb, "nt", BF16, "d_hb")
    g_wb = _mm(hb, dyb, "tn", F32, "dw_b", tk=4096)
    token = advance("ffn", g_wb) + on_grads("mix", dict(w_out=g_wout, w_branch_a=g_wa, w_branch_b=g_wb))
    dqkm, dvm, dom, rk, kc, tch, g_ml_head_norm = _mlstm_bwd(
        dha, pm, hp, den, a_row, A, wi, em, wk, dec, cst, nst, p["ml_head_norm"] + token[0:1, 0:1], "mlstm_bwd")
    token = advance("mix", dqkm)
    dqf, dkf, dvf, dFk, dFq = _fox_bwd(dhb, hb, pf, lse_row.reshape(fq_row.shape), fq_row + token[0, 0], Fc, "fox_bwd")
    dF = jnp.pad((dFk.reshape(FOX_HEADS, S) + dFq.reshape(FOX_HEADS, S)).T, ((0, 0), (0, LANES - FOX_HEADS)))
    dps, dbias = _gates_bwd(ps, bi, bf, bff, rk, kc, tch, dF, "gates_bwd")
    dpm = [dqkm, dvm, dom, dqf, dkf, dvf, dga, dgb]
    g_wmain_t = _mm(dpm, h0, "tn", F32, "dw_main")
    token = on_grads("in", dict(w_in=g_wmain_t))
    g_wsmall_t = _mm(dps, h0, "tn", F32, "dw_gates")
    dh0s = _mm(dps, wsmall_t + token[0:1, 0:1].astype(BF16), "nn", F32, "d_h0_gates")
    token = advance("in", dh0s)
    dh0 = _mm_sum_parts(dpm, wmain_t, F32, "d_h0_main", after=token)
    grad_x, g_norm_mix_pre = _rmsnorm_bwd([dh0, dh0s], x, p["norm_mix_pre"], dx1, F32, "norm_mix_pre_bwd")

    big = dict(wsmall_t=g_wsmall_t)
    small = dict(norm_mix_pre=g_norm_mix_pre, ml_head_norm=g_ml_head_norm, b_gate_a=g_b_gate_a, b_gate_b=g_b_gate_b,
                 norm_mix_post=g_norm_mix_post, norm_ffn_pre=g_norm_ffn_pre, norm_ffn_post=g_norm_ffn_post,
                 conv_b=g_conv_b, b_ml_i=dbias[:, 0:ML_HEADS], b_ml_f=dbias[:, LANES:LANES + ML_HEADS],
                 b_fox_f=dbias[:, 2 * LANES:2 * LANES + FOX_HEADS], conv_w=g_conv_w)
    return loss_row, grad_x, big, small


def _row_tile(r, target=256):
    best = None
    for t in range(8, min(r, target) + 1, 8):
        if r % t == 0:
            best = t
    return best if best is not None else r


def _adamw(w, g, m, v, name):
    _, R, C = w.shape
    tr = _row_tile(R)
    tc = C
    if tr == R and R > 256:
        tc = 256

    def body(w_ref, g_ref, m_ref, v_ref, d_ref, mo_ref, vo_ref):
        gv = g_ref[...]
        mn = ADAM_B1 * m_ref[0] + (1.0 - ADAM_B1) * gv
        vn = ADAM_B2 * v_ref[0] + (1.0 - ADAM_B2) * (gv * gv)
        m_hat = mn / (1.0 - ADAM_B1 ** ADAM_STEP)
        v_hat = vn / (1.0 - ADAM_B2 ** ADAM_STEP)
        d_ref[0] = -ADAM_LR * (m_hat / (jnp.sqrt(v_hat) + ADAM_EPS) + ADAM_WD * w_ref[0])
        mo_ref[0] = mn
        vo_ref[0] = vn

    blk = pl.BlockSpec((1, tr, tc), lambda i, j: (0, i, j))
    o = jax.ShapeDtypeStruct((1, R, C), F32)
    return pl.pallas_call(
        body, name=name, out_shape=(o, o, o), grid=(R // tr, C // tc),
        in_specs=[blk, pl.BlockSpec((tr, tc), lambda i, j: (i, j)), blk, blk], out_specs=(blk,) * 3,
        compiler_params=_cparams(("parallel", "parallel")),
    )(w, g, m, v)


ANY = pl.BlockSpec(memory_space=pl.ANY)


def _place():
    x, y, c = lax.axis_index("x"), lax.axis_index("y"), lax.axis_index("c")
    chips = [(1 - x, y), (x, 1 - y), (1 - x, 1 - y)]
    return x, y, c, chips


def _block(ref, kind, k, rows=None):
    if kind == "rows":
        return ref.at[k] if rows is None else ref.at[k, pl.ds(*rows), :]
    cb = ref.shape[1] // 4
    return ref.at[:, pl.ds(k * cb, cb)] if rows is None else ref.at[pl.ds(*rows), pl.ds(k * cb, cb)]


def _gathered_shape(s, kind):
    return (4,) + s.shape if kind == "rows" else (s.shape[0], 4 * s.shape[1])


def _gather_weights(shards, kinds, smalls):
    n, ns = len(shards), len(smalls)

    def body(*refs):
        ins, sm_in = refs[:n], refs[n:n + ns]
        outs, sm_out = refs[n + ns:2 * n + ns], refs[2 * n + ns:2 * (n + ns)]
        send_sems, recv_sems, sm_send, sm_recv, local_sems = refs[2 * (n + ns):]
        x, y, c, chips = _place()
        sibling = (x, y, 1 - c)
        kme = 2 * x + y

        def half(a, k, hc):
            h = ins[a].shape[0] // 2
            return _block(outs[a], kinds[a], k, (hc * h, h))

        def remote(a, slot, src, dst, to):
            return pltpu.make_async_remote_copy(src_ref=src, dst_ref=dst, send_sem=send_sems.at[a * 7 + slot],
                                                recv_sem=recv_sems.at[a * 7 + slot], device_id=to, device_id_type=MESH)

        def sm_copy(b, j, k, to):
            return pltpu.make_async_remote_copy(src_ref=sm_in[b], dst_ref=sm_out[b].at[k], send_sem=sm_send.at[3 * b + j],
                                                recv_sem=sm_recv.at[3 * b + j], device_id=to, device_id_type=MESH)

        local = [pltpu.make_async_copy(sm_in[b], sm_out[b].at[kme], local_sems.at[b]) for b in range(ns)]
        for cp in local:
            cp.start()
        sends = [remote(a, 6, ins[a], _block(outs[a], kinds[a], kme), sibling) for a in range(n)]
        for a in range(n):
            h = ins[a].shape[0] // 2
            for j, chip in enumerate(chips):
                sends.append(remote(a, j, ins[a].at[pl.ds(c * h, h), :], half(a, kme, c), (*chip, c)))
        for b in range(ns):
            for j, chip in enumerate(chips):
                sends.append(sm_copy(b, j, kme, (*chip, c)))
        for cp in sends:
            cp.start()
        for a in range(n):
            for j, chip in enumerate(chips):
                kj = 2 * chip[0] + chip[1]
                remote(a, j, half(a, kj, c), half(a, kj, c), (*chip, c)).wait_recv()
                fwd = remote(a, 3 + j, half(a, kj, c), half(a, kj, c), sibling)
                fwd.start()
                sends.append(fwd)
        for a in range(n):
            for j, chip in enumerate(chips):
                kj = 2 * chip[0] + chip[1]
                remote(a, 3 + j, half(a, kj, 1 - c), half(a, kj, 1 - c), sibling).wait_recv()
        for b in range(ns):
            for j, chip in enumerate(chips):
                sm_copy(b, j, 2 * chip[0] + chip[1], (*chip, c)).wait_recv()
        for a in range(n):
            remote(a, 6, ins[a], _block(outs[a], kinds[a], kme), sibling).wait_recv()
        for cp in sends:
            cp.wait_send()
        for cp in local:
            cp.wait()

    outs = pl.pallas_call(
        body, name="gather_weights",
        out_shape=tuple([jax.ShapeDtypeStruct(_gathered_shape(s, k), s.dtype) for s, k in zip(shards, kinds)]
                        + [jax.ShapeDtypeStruct((4,) + s.shape, s.dtype) for s in smalls]),
        in_specs=[ANY] * (n + ns), out_specs=tuple([ANY] * (n + ns)),
        scratch_shapes=[pltpu.SemaphoreType.DMA((7 * n,)), pltpu.SemaphoreType.DMA((7 * n,)),
                        pltpu.SemaphoreType.DMA((3 * ns,)), pltpu.SemaphoreType.DMA((3 * ns,)),
                        pltpu.SemaphoreType.DMA((ns,))],
    )(*shards, *smalls)
    return outs[:n], outs[n:]


_IN_HBM = pl.BlockSpec(memory_space=pltpu.HBM)
_SEMS = pl.BlockSpec(memory_space=pltpu.SEMAPHORE)
_DATAFLOW = pltpu.SideEffectType.DATAFLOW_SIDE_EFFECTING


def _hbm(t):
    return pltpu.HBM(t.shape, t.dtype)


def _gather_copies(ins, outs, send_sems, recv_sems, kinds):
    x, y, c, chips = _place()
    kme = 2 * x + y
    cps = []
    for a in range(len(ins)):
        h = ins[a].shape[0] // 2
        for j, chip in enumerate(chips + [None]):
            to = (x, y, 1 - c) if chip is None else (*chip, c)
            src = ins[a] if chip is None else ins[a].at[pl.ds(c * h, h), :]
            dst = _block(outs[a], kinds[a], kme, None if chip is None else (c * h, h))
            cps.append(pltpu.make_async_remote_copy(src_ref=src, dst_ref=dst, send_sem=send_sems.at[4 * a + j],
                                                    recv_sem=recv_sems.at[4 * a + j], device_id=to, device_id_type=MESH))
    return cps


def _gather_start(shards, kinds, name):
    n = len(shards)
    outs = [lax.empty(_gathered_shape(s, k), s.dtype) for s, k in zip(shards, kinds)]

    def body(*refs):
        for cp in _gather_copies(refs[:n], refs[n:2 * n], refs[2 * n], refs[2 * n + 1], kinds):
            cp.start()
        refs[-1][...] = jnp.zeros_like(refs[-1])

    return pl.pallas_call(
        body, name=name,
        out_shape=(pltpu.SemaphoreType.DMA((4 * n,)), pltpu.SemaphoreType.DMA((4 * n,)),
                   *[_hbm(t) for t in shards], *[_hbm(t) for t in outs], jax.ShapeDtypeStruct((8, LANES), F32)),
        in_specs=[_IN_HBM] * (2 * n),
        out_specs=(_SEMS, _SEMS, *[_IN_HBM] * (2 * n), pl.BlockSpec(memory_space=pltpu.VMEM)),
        input_output_aliases={a: 2 + a for a in range(2 * n)},
        compiler_params=pltpu.CompilerParams(has_side_effects=_DATAFLOW),
    )(*[pltpu.with_memory_space_constraint(t, pltpu.HBM) for t in list(shards) + outs])


def _gather_wait(started, after, kinds, name):
    n = (len(started) - 3) // 2
    bufs = started[2:2 + 2 * n]

    def body(*refs):
        for cp in _gather_copies(refs[:n], refs[n:2 * n], refs[2 * n], refs[2 * n + 1], kinds):
            cp.wait_send()
            cp.wait_recv()

    outs = pl.pallas_call(
        body, name=name, out_shape=tuple(_hbm(t) for t in bufs),
        in_specs=[_IN_HBM] * (2 * n) + [_SEMS, _SEMS, ANY], out_specs=tuple([_IN_HBM] * (2 * n)),
        input_output_aliases={a: a for a in range(2 * n)},
        compiler_params=pltpu.CompilerParams(has_side_effects=_DATAFLOW),
    )(*bufs, started[0], started[1], after)
    return outs[n:]


def _relay_copies(bufs, send_sems, recv_sems, kinds):
    x, y, c, chips = _place()
    cps = []
    for a in range(len(bufs)):
        h = (bufs[a].shape[1] if kinds[a] == "rows" else bufs[a].shape[0]) // 2
        for j, chip in enumerate(chips):
            part = _block(bufs[a], kinds[a], 2 * chip[0] + chip[1], (c * h, h))
            cps.append(pltpu.make_async_remote_copy(src_ref=part, dst_ref=part, send_sem=send_sems.at[3 * a + j],
                                                    recv_sem=recv_sems.at[3 * a + j], device_id=(x, y, 1 - c),
                                                    device_id_type=MESH))
    return cps


def _join_copies(bufs, send_sems, recv_sems, kinds):
    x, y, c, _ = _place()
    cps = []
    for a in range(len(bufs)):
        h = bufs[a].shape[0] // 2
        mine = bufs[a].at[pl.ds(c * h, h), :]
        cps.append(pltpu.make_async_remote_copy(src_ref=mine, dst_ref=mine, send_sem=send_sems.at[a],
                                                recv_sem=recv_sems.at[a], device_id=(x, y, 1 - c), device_id_type=MESH))
    return cps


def _inplace_start(copies, per_array, bufs, kinds, name):
    n = len(bufs)

    def body(*refs):
        for cp in copies(refs[:n], refs[n], refs[n + 1], kinds):
            cp.start()
        refs[-1][...] = jnp.zeros_like(refs[-1])

    return pl.pallas_call(
        body, name=name,
        out_shape=(pltpu.SemaphoreType.DMA((per_array * n,)), pltpu.SemaphoreType.DMA((per_array * n,)),
                   *[_hbm(t) for t in bufs], jax.ShapeDtypeStruct((8, LANES), F32)),
        in_specs=[_IN_HBM] * n, out_specs=(_SEMS, _SEMS, *[_IN_HBM] * n, pl.BlockSpec(memory_space=pltpu.VMEM)),
        input_output_aliases={a: 2 + a for a in range(n)},
        compiler_params=pltpu.CompilerParams(has_side_effects=_DATAFLOW),
    )(*[pltpu.with_memory_space_constraint(t, pltpu.HBM) for t in bufs])


def _inplace_wait(copies, started, after, kinds, name):
    n = len(started) - 3
    bufs = started[2:2 + n]

    def body(*refs):
        for cp in copies(refs[:n], refs[n], refs[n + 1], kinds):
            cp.wait_send()
            cp.wait_recv()

    return pl.pallas_call(
        body, name=name, out_shape=tuple(_hbm(t) for t in bufs),
        in_specs=[_IN_HBM] * n + [_SEMS, _SEMS, ANY], out_specs=tuple([_IN_HBM] * n),
        input_output_aliases={a: a for a in range(n)},
        compiler_params=pltpu.CompilerParams(has_side_effects=_DATAFLOW),
    )(*bufs, started[0], started[1], after)


def _add_halves(g, r1, cvec, kind, name):
    def body(c_ref, g_ref, r_ref, o_ref):
        o_ref[...] = (g_ref[...] + r_ref[...]).astype(o_ref.dtype)

    if kind == "rows":
        _, h, C = r1.shape
        tr = _row_tile(h, 512)
        nt = h // tr
        grid = (4, nt)
        g_spec = pl.BlockSpec((1, tr, C), lambda k, i, c_ref: (k, c_ref[0] * nt + i, 0))
        r_spec = pl.BlockSpec((1, tr, C), lambda k, i, c_ref: (k, i, 0))
    else:
        h, C4 = r1.shape
        tr, tc = _row_tile(h, 512), C4 // 4
        nt = h // tr
        grid = (nt, 4)
        g_spec = pl.BlockSpec((tr, tc), lambda i, k, c_ref: (c_ref[0] * nt + i, k))
        r_spec = pl.BlockSpec((tr, tc), lambda i, k, c_ref: (i, k))
    return pl.pallas_call(
        body, name=name, out_shape=jax.ShapeDtypeStruct(r1.shape, BF16),
        grid_spec=pltpu.PrefetchScalarGridSpec(num_scalar_prefetch=1, grid=grid, in_specs=[g_spec, r_spec],
                                               out_specs=r_spec),
        compiler_params=_cparams(("parallel", "parallel")),
    )(cvec, g, r1)


def _chip_copies(ins, lands, send_sems, recv_sems, kinds):
    x, y, c, chips = _place()
    return [pltpu.make_async_remote_copy(
        src_ref=_block(ins[a], kinds[a], 2 * chip[0] + chip[1]), dst_ref=lands[a].at[j],
        send_sem=send_sems.at[3 * a + j], recv_sem=recv_sems.at[3 * a + j], device_id=(*chip, c), device_id_type=MESH)
        for a in range(len(ins)) for j, chip in enumerate(chips)]


def _land_shape(s, kind):
    return (3,) + (s.shape[1:] if kind == "rows" else (s.shape[0], s.shape[1] // 4))


def _sibling_copies(ins, lands, send_sems, recv_sems, kinds):
    x, y, c, _ = _place()
    cps = []
    for a in range(len(ins)):
        h = lands[a].shape[-2]
        src = ins[a].at[:, pl.ds((1 - c) * h, h), :] if kinds[a] == "rows" else ins[a].at[pl.ds((1 - c) * h, h), :]
        cps.append(pltpu.make_async_remote_copy(src_ref=src, dst_ref=lands[a], send_sem=send_sems.at[a],
                                                recv_sem=recv_sems.at[a], device_id=(x, y, 1 - c), device_id_type=MESH))
    return cps


def _half_shape(g, kind):
    return (4, g.shape[1] // 2, g.shape[2]) if kind == "rows" else (g.shape[0] // 2, g.shape[1])


def _exchange_start(copies, per_array, srcs, land_shapes, kinds, name, zeroed=False):
    n = len(srcs)
    lands = [(jnp.zeros if zeroed else lax.empty)(shape, s.dtype) for shape, s in zip(land_shapes, srcs)]

    def body(*refs):
        for cp in copies(refs[:n], refs[n:2 * n], refs[2 * n], refs[2 * n + 1], kinds):
            cp.start()
        refs[-1][...] = jnp.zeros_like(refs[-1])

    return pl.pallas_call(
        body, name=name,
        out_shape=(pltpu.SemaphoreType.DMA((per_array * n,)), pltpu.SemaphoreType.DMA((per_array * n,)),
                   *[_hbm(t) for t in srcs], *[_hbm(t) for t in lands], jax.ShapeDtypeStruct((8, LANES), F32)),
        in_specs=[_IN_HBM] * (2 * n),
        out_specs=(_SEMS, _SEMS, *[_IN_HBM] * (2 * n), pl.BlockSpec(memory_space=pltpu.VMEM)),
        input_output_aliases={a: 2 + a for a in range(2 * n)},
        compiler_params=pltpu.CompilerParams(has_side_effects=_DATAFLOW),
    )(*[pltpu.with_memory_space_constraint(t, pltpu.HBM) for t in list(srcs) + lands])


def _exchange_wait(copies, started, after, kinds, name):
    n = (len(started) - 3) // 2
    bufs = started[2:2 + 2 * n]

    def body(*refs):
        for cp in copies(refs[:n], refs[n:2 * n], refs[2 * n], refs[2 * n + 1], kinds):
            cp.wait_send()
            cp.wait_recv()

    outs = pl.pallas_call(
        body, name=name, out_shape=tuple(_hbm(t) for t in bufs),
        in_specs=[_IN_HBM] * (2 * n) + [_SEMS, _SEMS, ANY], out_specs=tuple([_IN_HBM] * (2 * n)),
        input_output_aliases={a: a for a in range(2 * n)},
        compiler_params=pltpu.CompilerParams(has_side_effects=_DATAFLOW),
    )(*bufs, started[0], started[1], after)
    return outs[:n], outs[n:]


def _add_chips(s1, r2, kcvec, kind, name):
    _, h, C = r2.shape
    tr = _row_tile(h, 512)
    nt = h // tr

    def body(kc_ref, s_ref, r0_ref, r1_ref, r2_ref, o_ref):
        s = s_ref[0] if kind == "rows" else s_ref[...]
        o_ref[...] = ((s.astype(F32) + r0_ref[0].astype(F32)) + r1_ref[0].astype(F32)) + r2_ref[0].astype(F32)

    peer = lambda j: pl.BlockSpec((1, tr, C), lambda i, kc_ref: (j, i, 0))
    if kind == "rows":
        s_spec = pl.BlockSpec((1, tr, C), lambda i, kc_ref: (kc_ref[0], i, 0))
    else:
        s_spec = pl.BlockSpec((tr, C), lambda i, kc_ref: (i, kc_ref[0]))
    return pl.pallas_call(
        body, name=name, out_shape=jax.ShapeDtypeStruct((2 * h, C), F32),
        grid_spec=pltpu.PrefetchScalarGridSpec(
            num_scalar_prefetch=1, grid=(nt,),
            in_specs=[s_spec, peer(0), peer(1), peer(2)],
            out_specs=pl.BlockSpec((tr, C), lambda i, kc_ref: (kc_ref[1] * nt + i, 0))),
        compiler_params=_cparams(("parallel",)),
    )(kcvec, s1, r2, r2, r2)


N_DEV = 8


def _spread_copies(packs, lands, send_sems, recv_sems, kinds):
    x, y, c, _ = _place()
    me = 4 * x + 2 * y + c
    return [pltpu.make_async_remote_copy(
        src_ref=packs[0], dst_ref=lands[0].at[me], send_sem=send_sems.at[mask - 1], recv_sem=recv_sems.at[mask - 1],
        device_id=(1 - x if mask & 4 else x, 1 - y if mask & 2 else y, 1 - c if mask & 1 else c), device_id_type=MESH)
        for mask in range(1, N_DEV)]


def _sum_spread(pack, gathered):
    P = pack.shape[0]

    def body(p_ref, g_ref, o_ref):
        x, y, c, _ = _place()
        me = 4 * x + 2 * y + c
        acc = None
        for i in range(N_DEV):
            term = jnp.where(me == i, p_ref[...], g_ref[i])
            acc = term if acc is None else acc + term
        o_ref[...] = acc

    vmem = pl.BlockSpec(memory_space=pltpu.VMEM)
    return pl.pallas_call(body, name="allreduce_sum", out_shape=jax.ShapeDtypeStruct((P, LANES), F32),
                          in_specs=[vmem, vmem], out_specs=vmem)(pack, gathered)


def _pack_rows(arrs):
    rows = []
    for a in arrs:
        f = a.reshape(-1)
        f = jnp.pad(f, (0, (-f.shape[0]) % (8 * LANES)))
        rows.append(f.reshape(-1, LANES))
    return jnp.concatenate(rows, axis=0)


def _unpack_rows(pack, shapes):
    out, r = [], 0
    for s in shapes:
        n = math.prod(s)
        out.append(pack[r:r + -(-n // LANES)].reshape(-1)[:n].reshape(s))
        r += 8 * -(-n // (8 * LANES))
    return out


_SMALL = ["norm_mix_pre", "ml_head_norm", "b_gate_a", "b_gate_b", "norm_mix_post", "norm_ffn_pre", "norm_ffn_post",
          "conv_b", "b_ml_i", "b_ml_f", "b_fox_f"]
_BIG = ["w_in", "w_branch_a", "w_branch_b", "w_out", "w_up", "w_down"]
_WEIGHTS = ['norm_mix_pre', 'w_in', 'b_ml_i', 'b_ml_f', 'ml_head_norm', 'b_fox_f', 'b_gate_a', 'b_gate_b', 'w_branch_a',
            'w_branch_b', 'w_out', 'norm_mix_post', 'norm_ffn_pre', 'w_up', 'conv_w', 'conv_b', 'w_down', 'norm_ffn_post']


_KINDS = ["rows", "rows", "rows", "rows", "cols", "rows"]


def kernel(x, norm_mix_pre, w_in, b_ml_i, b_ml_f, ml_head_norm, b_fox_f, b_gate_a, b_gate_b, w_branch_a, w_branch_b, w_out, norm_mix_post, norm_ffn_pre, w_up, conv_w, conv_b, w_down, norm_ffn_post, loss_target, m_norm_mix_pre, m_w_in, m_b_ml_i, m_b_ml_f, m_ml_head_norm, m_b_fox_f, m_b_gate_a, m_b_gate_b, m_w_branch_a, m_w_branch_b, m_w_out, m_norm_mix_post, m_norm_ffn_pre, m_w_up, m_conv_w, m_conv_b, m_w_down, m_norm_ffn_post, v_norm_mix_pre, v_w_in, v_b_ml_i, v_b_ml_f, v_ml_head_norm, v_b_fox_f, v_b_gate_a, v_b_gate_b, v_w_branch_a, v_w_branch_b, v_w_out, v_norm_mix_post, v_norm_ffn_pre, v_w_up, v_conv_w, v_conv_b, v_w_down, v_norm_ffn_post):
    args = dict(locals())
    w = {n: args[n] for n in _WEIGHTS}
    mom = {n: args["m_" + n] for n in _WEIGHTS}
    var = {n: args["v_" + n] for n in _WEIGHTS}
    cx, cy, cc = lax.axis_index("x"), lax.axis_index("y"), lax.axis_index("c")
    kme = 2 * cx + cy
    cvec = jnp.reshape(cc, (1,)).astype(jnp.int32)
    kcvec = jnp.stack([kme, cc]).astype(jnp.int32)
    odd = kme % 2

    tr3 = lambda t: jnp.transpose(t, (0, 2, 1))
    w["w_in"], mom["w_in"], var["w_in"] = tr3(w_in), tr3(m_w_in), tr3(v_w_in)
    w_in_main = lax.dynamic_slice_in_dim(w["w_in"][0], 4 * odd, 2048, axis=0).astype(BF16)
    w_in_gates = lax.dynamic_slice_in_dim(w["w_in"][0], 2048 * (1 - odd), 4, axis=0).astype(BF16)
    (wmain_t,), (g_cw, g_gates) = _gather_weights([w_in_main], _KINDS[:1], [w["conv_w"][0], w_in_gates])
    rest_started = _gather_start([w[n][0].astype(BF16) for n in _BIG[1:]], _KINDS[1:], "gather_rest_start")

    relay = {}

    def rest_arrived(after):
        bufs = _gather_wait(rest_started, after, _KINDS[1:], "gather_rest_wait")
        relay["started"] = _inplace_start(_relay_copies, 3, bufs, _KINDS[1:], "gather_rest_relay_start")
        return relay["started"][-1]

    def rest_weights(after):
        g_a, g_b, g_out, wup, g_down = _inplace_wait(_relay_copies, relay["started"], after, _KINDS[1:],
                                                     "gather_rest_relay_wait")
        return full(g_a), full(g_b), full(g_out), wup, full(g_down)
    gate_rows = g_gates.reshape(16, D_MODEL)
    wsmall_t = jnp.zeros((N_SMALL, D_MODEL), BF16)
    for blk, (lo, hi) in enumerate(((0, 4), (4, 8), (8, 16))):
        wsmall_t = wsmall_t.at[blk * LANES:blk * LANES + hi - lo].set(gate_rows[lo:hi])
    full = lambda g: g.reshape(-1, g.shape[2])
    p = {n: w[n] for n in _SMALL}
    p["conv_w"] = jnp.transpose(g_cw, (1, 0, 2)).reshape(3, -1)

    groups = {}

    def on_grads(group, gs):
        names = list(gs)
        kinds = [_KINDS[_BIG.index(n)] for n in names]
        whole = [g if k == "cols" else g.reshape(4, -1, g.shape[1]) for g, k in zip(gs.values(), kinds)]
        started = _exchange_start(_sibling_copies, 1, whole, [_half_shape(g, k) for g, k in zip(whole, kinds)], kinds,
                                  "grads_to_sibling_start_" + group)
        groups[group] = dict(names=names, kinds=kinds, sibling=started)
        return started[-1]

    def advance(group, after):
        G = groups[group]
        whole, got = _exchange_wait(_sibling_copies, G["sibling"], after, G["kinds"], "grads_to_sibling_wait_" + group)
        sums = [_add_halves(g, r, cvec, k, "add_sibling_" + n) for g, r, k, n in zip(whole, got, G["kinds"], G["names"])]
        G["chips"] = _exchange_start(_chip_copies, 3, sums, [_land_shape(s, k) for s, k in zip(sums, G["kinds"])],
                                     G["kinds"], "grads_to_chips_start_" + group)
        return G["chips"][-1]

    loss_row, grad_x, big, small = _local_step(x[0], loss_target[0], full(wmain_t), wsmall_t, rest_arrived, rest_weights,
                                               p, on_grads, advance, rest_started[-1])
    gt = big["wsmall_t"]
    small["w_in_gates"] = jnp.concatenate([gt[0:4], gt[LANES:LANES + 4], gt[2 * LANES:2 * LANES + 8]], axis=0)
    small_names = _SMALL + ["conv_w"]
    packed_names = small_names + ["w_in_gates"]
    pack = _pack_rows([small[n] for n in packed_names] + [loss_row])
    spread = _exchange_start(_spread_copies, N_DEV - 1, [pack], [(N_DEV,) + pack.shape], None, "allreduce_start", zeroed=True)

    def my_half(group, after):
        G = groups[group]
        sums, got = _exchange_wait(_chip_copies, G["chips"], after, G["kinds"], "grads_to_chips_wait_" + group)
        return [_add_chips(s, r, kcvec, k, "add_chips_" + n) for s, r, k, n in zip(sums, got, G["kinds"], G["names"])]

    first_names = groups["ffn"]["names"] + groups["mix"]["names"]
    join_first = _inplace_start(_join_copies, 1, my_half("ffn", spread[-1]) + my_half("mix", spread[-1]), None,
                                "grads_join_start")
    join_in = _inplace_start(_join_copies, 1, my_half("in", join_first[-1]), None, "grads_join_start_in")
    grads = dict(zip(first_names, _inplace_wait(_join_copies, join_first, join_in[-1], None, "grads_join_wait")))

    delta, new_m, new_v = {}, {}, {}
    for n in _BIG[1:]:
        delta[n], new_m[n], new_v[n] = _adamw(w[n], grads[n], mom[n], var[n], "adamw_" + n)
        grads[n] = grads[n][None]
    grads["w_in"], = _inplace_wait(_join_copies, join_in, delta[_BIG[-1]], None, "grads_join_wait_in")

    (pack,), (gathered,) = _exchange_wait(_spread_copies, spread, delta[_BIG[-1]], None, "allreduce_wait")
    full_shapes = [small[n].shape if n in ("conv_w", "w_in_gates") else w[n][0].shape for n in packed_names]
    total = _unpack_rows(_sum_spread(pack, gathered), full_shapes + [loss_row.shape])
    for n, t in zip(packed_names, total):
        grads[n] = t
    loss = total[-1][0, 0]
    grads["conv_w"] = lax.dynamic_slice_in_dim(grads["conv_w"], kme * conv_w.shape[2], conv_w.shape[2], axis=1)
    my_gates = lax.dynamic_slice_in_dim(grads.pop("w_in_gates"), 4 * kme, 4, axis=0)
    g_in = jnp.zeros(w["w_in"].shape[1:], F32)
    g_in = lax.dynamic_update_slice_in_dim(g_in, grads["w_in"], 4 * odd, axis=0)
    grads["w_in"] = lax.dynamic_update_slice_in_dim(g_in, my_gates, 2048 * (1 - odd), axis=0)
    delta["w_in"], new_m["w_in"], new_v["w_in"] = _adamw(w["w_in"], grads["w_in"], mom["w_in"], var["w_in"], "adamw_w_in")
    grads["w_in"] = grads["w_in"][None]
    for d in (grads, delta, new_m, new_v):
        d["w_in"] = tr3(d["w_in"])
    packs = [_pack_rows([d[n][0] for n in small_names]) for d in (w, mom, var)]
    pad = ((0, (-packs[0].shape[0]) % 8), (0, 0))
    packs = [jnp.pad(t, pad)[None] for t in packs]
    gp = jnp.pad(_pack_rows([grads[n] for n in small_names]), pad)
    shapes = [w[n][0].shape for n in small_names]
    for dst, res in zip((delta, new_m, new_v), _adamw(packs[0], gp, packs[1], packs[2], "adamw_small")):
        for n, t in zip(small_names, _unpack_rows(res[0], shapes)):
            dst[n] = t[None]
    for n in small_names:
        grads[n] = grads[n][None]

    return (loss, grad_x[None], *[grads[n] for n in _WEIGHTS], *[delta[n] for n in _WEIGHTS],
            *[new_m[n] for n in _WEIGHTS], *[new_v[n] for n in _WEIGHTS])
```

```python
import functools
import math

import jax
import jax.numpy as jnp
from jax import lax
from jax.experimental import pallas as pl
from jax.experimental.pallas import tpu as pltpu

F32 = jnp.float32
BF16 = jnp.bfloat16
MESH = pl.DeviceIdType.MESH

D_MODEL = 1024
ML_HEADS = 4
ML_DQK = 128
ML_DV = 256
FOX_HEADS = 8
FOX_DH = 128
D_FF = 2816
GATE_CAP = 15.0
EPS = 1e-6
ADAM_LR, ADAM_B1, ADAM_B2, ADAM_EPS, ADAM_WD, ADAM_STEP = 0.001, 0.9, 0.999, 1e-08, 0.01, 10

LANES = 128
MLC = 256
FOX_TQ = 512
FOX_TQ_FWD = 512
FOX_TK = 512
FOX_TK_FWD = 512
ROW_T = 512
CONV_TC = 1408
VMEM_LIMIT = 56 * 1024 * 1024

C_QM, C_KM, C_VM, C_OM = 0, 512, 1024, 2048
N_ML, N_FOX, N_GATE = 3072, 3072, 2048
N_SMALL = 384


def _cparams(sem=None):
    return pltpu.CompilerParams(dimension_semantics=sem, vmem_limit_bytes=VMEM_LIMIT)


def _tile(n, target):
    if n <= target:
        return n
    best = None
    for t in range(LANES, target + 1, LANES):
        if n % t == 0:
            best = t
    assert best is not None, (n, target)
    return best


def _dot(a, b, dims):
    return lax.dot_general(a, b, (dims, ((), ())), preferred_element_type=F32)


def _dot_nn(a, b):
    return _dot(a, b, ((1,), (0,)))


def _dot_nt(a, b):
    return _dot(a, b, ((1,), (1,)))


def _dot_tn(a, b):
    return _dot(a, b, ((0,), (0,)))


_DOTS = {"nn": _dot_nn, "nt": _dot_nt, "tn": _dot_tn}


def _mm(a, b, mode, out_dtype, name, tm=1024, tn=1408, tk=1408, after=None, b_rows=None):
    a_parts = list(a) if isinstance(a, (list, tuple)) else [a]
    b_parts = list(b) if isinstance(b, (list, tuple)) else [b]
    extra = [] if after is None else [after]
    assert len(a_parts) == 1 or len(b_parts) == 1, name
    a_axes = {"nn": "ik", "nt": "ik", "tn": "ki"}[mode]
    b_axes = {"nn": "kj", "nt": "jk", "tn": "kj"}[mode]
    size, target = {}, dict(i=tm, j=tn, k=tk)
    for parts, axes in ((a_parts, a_axes), (b_parts, b_axes)):
        dims = (parts[0].shape[0], parts[0].shape[1] * len(parts))
        if parts is b_parts and b_rows is not None:
            dims = (b_rows[1], dims[1])
        for ax, n in zip(axes, dims):
            assert size.setdefault(ax, n) == n, (name, ax, n, size)
    tile = {}
    for parts, axes in ((a_parts, a_axes), (b_parts, b_axes)):
        if len(parts) > 1:
            tile[axes[1]] = _tile(parts[0].shape[1], target[axes[1]])
    for ax in "ijk":
        tile.setdefault(ax, _tile(size[ax], target[ax]))
    M, N, nk = size["i"], size["j"], size["k"] // tile["k"]
    grid_pos = dict(i=0, j=1, k=2)
    dot = _DOTS[mode]

    def specs(parts, axes):
        blk = (tile[axes[0]], tile[axes[1]])
        if len(parts) == 1:
            first = 0
            if parts is b_parts and b_rows is not None:
                assert b_rows[0] % blk[0] == 0, (name, b_rows, blk)
                first = b_rows[0] // blk[0]
            return [pl.BlockSpec(blk, lambda *g: (first + g[grid_pos[axes[0]]], g[grid_pos[axes[1]]]))], None
        bpp = parts[0].shape[1] // blk[1]

        def index(p):
            def f(*g):
                g0, g1 = g[grid_pos[axes[0]]], g[grid_pos[axes[1]]]
                on = g1 // bpp == p
                return jnp.where(on, g0, 0), jnp.where(on, g1 % bpp, 0)
            return f

        return [pl.BlockSpec(blk, index(p)) for p in range(len(parts))], (axes[1], bpp)

    a_specs, a_sel = specs(a_parts, a_axes)
    b_specs, b_sel = specs(b_parts, b_axes)
    na, nb = len(a_parts), len(b_parts)

    def body(*refs):
        a_refs, b_refs = refs[:na], refs[na:na + nb]
        o_ref, acc = refs[na + nb + len(extra)], refs[na + nb + len(extra) + 1:]

        def accumulate(part):
            if nk == 1:
                o_ref[...] = part.astype(o_ref.dtype)
                return
            acc_ref, = acc
            k = pl.program_id(2)

            @pl.when(k == 0)
            def _():
                acc_ref[...] = part

            @pl.when(k > 0)
            def _():
                acc_ref[...] += part

            @pl.when(k == nk - 1)
            def _():
                o_ref[...] = acc_ref[...].astype(o_ref.dtype)

        sel = a_sel or b_sel
        if sel is None:
            accumulate(dot(a_refs[0][...], b_refs[0][...]))
        else:
            which = pl.program_id(grid_pos[sel[0]]) // sel[1]
            for p in range(max(na, nb)):
                @pl.when(which == p)
                def _(p=p):
                    accumulate(dot(a_refs[p if a_sel else 0][...], b_refs[p if b_sel else 0][...]))

    return pl.pallas_call(
        body, name=name,
        out_shape=jax.ShapeDtypeStruct((M, N), out_dtype),
        grid=(M // tile["i"], N // tile["j"], nk),
        in_specs=a_specs + b_specs + [pl.BlockSpec(memory_space=pl.ANY)] * len(extra),
        out_specs=pl.BlockSpec((tile["i"], tile["j"]), lambda i, j, k: (i, j)),
        scratch_shapes=[pltpu.VMEM((tile["i"], tile["j"]), F32)] if nk > 1 else [],
        compiler_params=_cparams(("parallel", "parallel", "arbitrary")),
    )(*a_parts, *b_parts, *extra)


def _mm_sum_parts(parts, b, out_dtype, name, trans_b=False, tm=1024, tn=512, after=None):
    M, K = parts[0].shape
    N = b.shape[0] if trans_b else b.shape[1]
    tm, tn = _tile(M, tm), _tile(N, tn)
    n = len(parts)
    extra = [] if after is None else [after]

    def body(*refs):
        b_ref, o_ref = refs[n], refs[n + 1 + len(extra)]
        acc = None
        for p in range(n):
            if trans_b:
                d = _dot_nt(refs[p][...], b_ref[:, p * K:(p + 1) * K])
            else:
                d = _dot_nn(refs[p][...], b_ref[p * K:(p + 1) * K, :])
            acc = d if acc is None else acc + d
        o_ref[...] = acc.astype(o_ref.dtype)

    b_spec = pl.BlockSpec((tn, n * K), lambda i, j: (j, 0)) if trans_b else pl.BlockSpec((n * K, tn), lambda i, j: (0, j))
    return pl.pallas_call(
        body, name=name, out_shape=jax.ShapeDtypeStruct((M, N), out_dtype), grid=(M // tm, N // tn),
        in_specs=[pl.BlockSpec((tm, K), lambda i, j: (i, 0))] * n + [b_spec]
        + [pl.BlockSpec(memory_space=pl.ANY)] * len(extra),
        out_specs=pl.BlockSpec((tm, tn), lambda i, j: (i, j)),
        compiler_params=_cparams(("parallel", "arbitrary")),
    )(*parts, b, *extra)


def _rstd(x):
    return lax.rsqrt(jnp.mean(x * x, axis=-1, keepdims=True) + EPS)


def _rmsnorm_fwd(x, g, name):
    S, D = x.shape
    T = _tile(S, ROW_T)

    def body(x_ref, g_ref, o_ref):
        xv = x_ref[...]
        o_ref[...] = (xv * _rstd(xv) * g_ref[...]).astype(o_ref.dtype)

    return pl.pallas_call(
        body, name=name, out_shape=jax.ShapeDtypeStruct((S, D), BF16), grid=(S // T,),
        in_specs=[pl.BlockSpec((T, D), lambda i: (i, 0)), pl.BlockSpec((1, D), lambda i: (0, 0))],
        out_specs=pl.BlockSpec((T, D), lambda i: (i, 0)),
        compiler_params=_cparams(("parallel",)),
    )(x, g)


def _resid_norm_fwd(x, z, g, g_next, name):
    S, D = x.shape
    T = _tile(S, ROW_T)

    def body(x_ref, z_ref, g_ref, gn_ref, o_ref, h_ref):
        zv = z_ref[...]
        x1 = x_ref[...] + zv * _rstd(zv) * g_ref[...]
        o_ref[...] = x1
        h_ref[...] = (x1 * _rstd(x1) * gn_ref[...]).astype(h_ref.dtype)

    row = pl.BlockSpec((T, D), lambda i: (i, 0))
    vec = pl.BlockSpec((1, D), lambda i: (0, 0))
    return pl.pallas_call(
        body, name=name, out_shape=(jax.ShapeDtypeStruct((S, D), F32), jax.ShapeDtypeStruct((S, D), BF16)),
        grid=(S // T,), in_specs=[row, row, vec, vec], out_specs=(row, row), compiler_params=_cparams(("parallel",)),
    )(x, z, g, g_next)


def _norm_chain_bwd(dh, xin, g, resid, zin, gz, name):
    S, D = xin.shape
    T = _tile(S, ROW_T)

    def body(dh_ref, x_ref, g_ref, r_ref, z_ref, gz_ref, dx_ref, dz_ref, dg_ref, dgz_ref):
        dx, dgt = _rmsnorm_bwd_math(dh_ref[...], x_ref[...], g_ref[...])
        dx = dx + r_ref[...]
        dx_ref[...] = dx
        dz, dgzt = _rmsnorm_bwd_math(dx, z_ref[...], gz_ref[...])
        dz_ref[...] = dz.astype(dz_ref.dtype)

        @pl.when(pl.program_id(0) == 0)
        def _():
            dg_ref[...] = jnp.zeros_like(dg_ref)
            dgz_ref[...] = jnp.zeros_like(dgz_ref)

        dg_ref[...] += jnp.sum(dgt, axis=0, keepdims=True)
        dgz_ref[...] += jnp.sum(dgzt, axis=0, keepdims=True)

    row = pl.BlockSpec((T, D), lambda i: (i, 0))
    vec = pl.BlockSpec((1, D), lambda i: (0, 0))
    v1 = jax.ShapeDtypeStruct((1, D), F32)
    return pl.pallas_call(
        body, name=name,
        out_shape=(jax.ShapeDtypeStruct((S, D), F32), jax.ShapeDtypeStruct((S, D), BF16), v1, v1),
        grid=(S // T,), in_specs=[row, row, vec, row, row, vec], out_specs=(row, row, vec, vec),
        compiler_params=_cparams(("arbitrary",)),
    )(dh, xin, g, resid, zin, gz)


def _rmsnorm_bwd_math(dy, xv, g):
    r = _rstd(xv)
    u = dy * g
    dx = r * u - xv * (r * r * r) * jnp.mean(u * xv, axis=-1, keepdims=True)
    return dx, dy * xv * r


def _rmsnorm_bwd(dys, xin, g, resid, out_dtype, name):
    S, D = xin.shape
    T = _tile(S, ROW_T)
    has_resid = resid is not None
    ndy = len(dys)

    def body(*refs):
        dy_refs, (x_ref, g_ref) = refs[:ndy], refs[ndy:ndy + 2]
        dx_ref, dg_ref = refs[-2:]
        dy = dy_refs[0][...]
        for r in dy_refs[1:]:
            dy = dy + r[...]
        dx, dgt = _rmsnorm_bwd_math(dy, x_ref[...], g_ref[...])
        if has_resid:
            dx = dx + refs[ndy + 2][...]
        dx_ref[...] = dx.astype(dx_ref.dtype)

        @pl.when(pl.program_id(0) == 0)
        def _():
            dg_ref[...] = jnp.zeros_like(dg_ref)

        dg_ref[...] += jnp.sum(dgt, axis=0, keepdims=True)

    row = pl.BlockSpec((T, D), lambda i: (i, 0))
    vec = pl.BlockSpec((1, D), lambda i: (0, 0))
    ins = list(dys) + [xin, g] + ([resid] if has_resid else [])
    return pl.pallas_call(
        body, name=name,
        out_shape=(jax.ShapeDtypeStruct((S, D), out_dtype), jax.ShapeDtypeStruct((1, D), F32)),
        grid=(S // T,), in_specs=[row] * ndy + [row, vec] + ([row] if has_resid else []),
        out_specs=(row, vec), compiler_params=_cparams(("arbitrary",)),
    )(*ins)


def _loss_head(x1, d, g, target, name):
    S, D = x1.shape
    T = _tile(S, ROW_T)

    def body(x_ref, d_ref, g_ref, t_ref, loss_ref, dy_ref, dd_ref, dg_ref):
        dv, gv = d_ref[...], g_ref[...]
        y = x_ref[...] + dv * _rstd(dv) * gv
        diff = y - t_ref[...]
        dy = diff * (1.0 / D)
        dy_ref[...] = dy
        dd, dgt = _rmsnorm_bwd_math(dy, dv, gv)
        dd_ref[...] = dd.astype(dd_ref.dtype)

        @pl.when(pl.program_id(0) == 0)
        def _():
            dg_ref[...] = jnp.zeros_like(dg_ref)
            loss_ref[...] = jnp.zeros_like(loss_ref)

        dg_ref[...] += jnp.sum(dgt, axis=0, keepdims=True)
        part = jnp.sum(jnp.sum(diff * diff, axis=1, keepdims=True), axis=0, keepdims=True)
        loss_ref[...] += (0.5 / D) * part

    row = pl.BlockSpec((T, D), lambda i: (i, 0))
    vec = pl.BlockSpec((1, D), lambda i: (0, 0))
    return pl.pallas_call(
        body, name=name,
        out_shape=(jax.ShapeDtypeStruct((1, LANES), F32), jax.ShapeDtypeStruct((S, D), F32),
                   jax.ShapeDtypeStruct((S, D), BF16), jax.ShapeDtypeStruct((1, D), F32)),
        grid=(S // T,), in_specs=[row, row, vec, row],
        out_specs=(pl.BlockSpec((1, LANES), lambda i: (0, 0)), row, row, vec),
        compiler_params=_cparams(("arbitrary",)),
    )(x1, d, g, target)


def _merge_fwd(ya, yb, pm, ba, bb, name):
    S, D = ya.shape
    T = _tile(S, ROW_T)

    def body(ya_ref, yb_ref, ga_ref, gb_ref, ba_ref, bb_ref, o_ref):
        sa = jax.nn.sigmoid(ga_ref[...] + ba_ref[...])
        sb = jax.nn.sigmoid(gb_ref[...] + bb_ref[...])
        o_ref[...] = (sa * ya_ref[...] + sb * yb_ref[...]).astype(o_ref.dtype)

    row = pl.BlockSpec((T, D), lambda i: (i, 0))
    vec = pl.BlockSpec((1, D), lambda i: (0, 0))
    return pl.pallas_call(
        body, name=name, out_shape=jax.ShapeDtypeStruct((S, D), BF16), grid=(S // T,),
        in_specs=[row, row, pl.BlockSpec((T, D), lambda i: (i, 0)),
                  pl.BlockSpec((T, D), lambda i: (i, 1)), vec, vec],
        out_specs=row, compiler_params=_cparams(("parallel",)),
    )(ya, yb, pm, pm, ba, bb)


def _merge_bwd(dmerged, ya, yb, pm, ba, bb, name):
    S, D = ya.shape
    T = _tile(S, ROW_T)

    def body(dm_ref, ya_ref, yb_ref, ga_ref, gb_ref, ba_ref, bb_ref,
             dya_ref, dyb_ref, dga_ref, dgb_ref, dba_ref, dbb_ref):
        dm = dm_ref[...]
        sa = jax.nn.sigmoid(ga_ref[...] + ba_ref[...])
        sb = jax.nn.sigmoid(gb_ref[...] + bb_ref[...])
        dya_ref[...] = (dm * sa).astype(dya_ref.dtype)
        dyb_ref[...] = (dm * sb).astype(dyb_ref.dtype)
        dga = dm * ya_ref[...] * sa * (1.0 - sa)
        dgb = dm * yb_ref[...] * sb * (1.0 - sb)
        dga_ref[...] = dga.astype(dga_ref.dtype)
        dgb_ref[...] = dgb.astype(dgb_ref.dtype)

        @pl.when(pl.program_id(0) == 0)
        def _():
            dba_ref[...] = jnp.zeros_like(dba_ref)
            dbb_ref[...] = jnp.zeros_like(dbb_ref)

        dba_ref[...] += jnp.sum(dga, axis=0, keepdims=True)
        dbb_ref[...] += jnp.sum(dgb, axis=0, keepdims=True)

    row = pl.BlockSpec((T, D), lambda i: (i, 0))
    vec = pl.BlockSpec((1, D), lambda i: (0, 0))
    act = jax.ShapeDtypeStruct((S, D), BF16)
    v1 = jax.ShapeDtypeStruct((1, D), F32)
    return pl.pallas_call(
        body, name=name, out_shape=(act, act, act, act, v1, v1), grid=(S // T,),
        in_specs=[row, row, row, pl.BlockSpec((T, D), lambda i: (i, 0)),
                  pl.BlockSpec((T, D), lambda i: (i, 1)), vec, vec],
        out_specs=(row, row, row, row, vec, vec), compiler_params=_cparams(("arbitrary",)),
    )(dmerged, ya, yb, pm, pm, ba, bb)


_GELU_C = math.sqrt(2.0 / math.pi)


_GELU_K = 0.044715


def _gelu(g):
    u = 0.5 * jnp.tanh(g * (_GELU_C + (_GELU_C * _GELU_K) * (g * g))) + 0.5
    return g * u, u


def _gelu_grad(g, u):
    return u * (1.0 + g * (1.0 - u) * (2 * _GELU_C + (6 * _GELU_C * _GELU_K) * (g * g)))


def _shift_down(v, halo_ref, first, rows):
    T = v.shape[0]
    keep = jnp.where(first, 0.0, 1.0)
    h7 = halo_ref[7:8, :] * keep
    h6 = halo_ref[6:7, :] * keep
    m1 = jnp.where(rows == 0, h7, pltpu.roll(v, 1, 0))
    m2 = jnp.where(rows == 0, h6, jnp.where(rows == 1, h7, pltpu.roll(v, 2, 0)))
    return m1, m2


def _conv_act_fwd(up, cw, cb, name):
    S, F2 = up.shape
    Fh = F2 // 2
    T = _tile(S, ROW_T)
    tc = _tile(Fh, CONV_TC)
    ncol = Fh // tc
    hb = T // 8

    def body(ua_ref, ug_ref, ha_ref, hg_ref, wa_ref, wg_ref, ba_ref, bg_ref, o_ref, a_ref, g_ref):
        first = pl.program_id(0) == 0
        rows = lax.broadcasted_iota(jnp.int32, (T, tc), 0)

        def conv(u_ref, h_ref, w_ref, b_ref):
            v = u_ref[...]
            m1, m2 = _shift_down(v, h_ref, first, rows)
            return b_ref[...] + w_ref[0:1, :] * m2 + w_ref[1:2, :] * m1 + w_ref[2:3, :] * v

        a = conv(ua_ref, ha_ref, wa_ref, ba_ref)
        g = conv(ug_ref, hg_ref, wg_ref, bg_ref)
        a_ref[...] = a
        g_ref[...] = g
        o_ref[...] = (_gelu(g)[0] * a).astype(o_ref.dtype)

    halo = lambda off: pl.BlockSpec((8, tc), lambda i, j: (jnp.maximum(i * hb - 1, 0), j + off))
    blk = pl.BlockSpec((T, tc), lambda i, j: (i, j))
    f32 = jax.ShapeDtypeStruct((S, Fh), F32)
    return pl.pallas_call(
        body, name=name, out_shape=(jax.ShapeDtypeStruct((S, Fh), BF16), f32, f32), grid=(S // T, ncol),
        in_specs=[blk, pl.BlockSpec((T, tc), lambda i, j: (i, j + ncol)),
                  halo(0), halo(ncol),
                  pl.BlockSpec((3, tc), lambda i, j: (0, j)), pl.BlockSpec((3, tc), lambda i, j: (0, j + ncol)),
                  pl.BlockSpec((1, tc), lambda i, j: (0, j)), pl.BlockSpec((1, tc), lambda i, j: (0, j + ncol))],
        out_specs=(blk, blk, blk),
        compiler_params=_cparams(("parallel", "parallel")),
    )(up, up, up, up, cw, cw, cb, cb)


def _conv_act_bwd(up, a, g, dact, cw, name):
    S, F2 = up.shape
    Fh = F2 // 2
    T = _tile(S, ROW_T)
    tc = _tile(Fh, CONV_TC)
    ncol, nrow, hb, nhb = Fh // tc, S // T, T // 8, S // 8

    def body(ua_ref, ug_ref, a_ref, g_ref, an_ref, gn_ref, wa_ref, wg_ref, da_ref, dn_ref,
             dpa_ref, dpg_ref, dwa_ref, dwg_ref, dba_ref, dbg_ref, dua_n, dug_n):
        i = pl.program_id(1)
        rows = lax.broadcasted_iota(jnp.int32, (T, tc), 0)

        def du_of(a, g, dact_v):
            gel, t = _gelu(g)
            return dact_v * gel, dact_v * a * _gelu_grad(g, t)

        dua, dug = du_of(a_ref[...], g_ref[...], da_ref[...])
        keep = jnp.where(i == nrow - 1, 0.0, 1.0)
        dua_n[...], dug_n[...] = du_of(an_ref[...], gn_ref[...], dn_ref[...] * keep)

        @pl.when(i == 0)
        def _():
            for r in (dwa_ref, dwg_ref, dba_ref, dbg_ref):
                r[...] = jnp.zeros_like(r)

        for du, n_ref, u_ref, w_ref, o_ref, dw_ref, db_ref in ((dua, dua_n, ua_ref, wa_ref, dpa_ref, dwa_ref, dba_ref),
                                                               (dug, dug_n, ug_ref, wg_ref, dpg_ref, dwg_ref, dbg_ref)):
            n0, n1 = n_ref[0:1, :], n_ref[1:2, :]
            du1 = jnp.where(rows == T - 1, n0, pltpu.roll(du, T - 1, 0))
            du2 = jnp.where(rows == T - 2, n0, jnp.where(rows == T - 1, n1, pltpu.roll(du, T - 2, 0)))
            o_ref[...] = (w_ref[2:3, :] * du + w_ref[1:2, :] * du1 + w_ref[0:1, :] * du2).astype(o_ref.dtype)
            u = u_ref[...]
            db_ref[...] += jnp.sum(du, axis=0, keepdims=True)
            for j, d in enumerate((du2, du1, du)):
                dw_ref[j:j + 1, :] += jnp.sum(d * u, axis=0, keepdims=True)

    tile = lambda off: pl.BlockSpec((T, tc), lambda j, i: (i, j + off))
    under = pl.BlockSpec((8, tc), lambda j, i: (jnp.minimum((i + 1) * hb, nhb - 1), j))
    vec = lambda n, off: pl.BlockSpec((n, tc), lambda j, i: (0, j + off))
    act = jax.ShapeDtypeStruct((S, Fh), BF16)
    return pl.pallas_call(
        body, name=name,
        out_shape=(act, act, jax.ShapeDtypeStruct((3, Fh), F32), jax.ShapeDtypeStruct((3, Fh), F32),
                   jax.ShapeDtypeStruct((1, Fh), F32), jax.ShapeDtypeStruct((1, Fh), F32)),
        grid=(ncol, nrow),
        in_specs=[tile(0), tile(ncol), tile(0), tile(0), under, under, vec(3, 0), vec(3, ncol), tile(0), under],
        out_specs=(tile(0), tile(0), vec(3, 0), vec(3, 0), vec(1, 0), vec(1, 0)),
        scratch_shapes=[pltpu.VMEM((8, tc), F32), pltpu.VMEM((8, tc), F32)],
        compiler_params=_cparams(("parallel", "arbitrary")),
    )(up, up, a, g, a, g, cw, cw, dact, dact)


def _split3(x):
    hi = x.astype(BF16)
    r1 = x - hi.astype(F32)
    mid = r1.astype(BF16)
    lo = (r1 - mid.astype(F32)).astype(BF16)
    return hi, mid, lo


def _tri_dot(tri, x):
    hi, mid, lo = _split3(x)
    return _dot_nn(tri, hi) + _dot_nn(tri, mid) + _dot_nn(tri, lo)


def _log_sigmoid(x):
    return jnp.minimum(x, 0.0) - jnp.log(1.0 + jnp.exp(-jnp.abs(x)))


def _tri_mask(n, lower):
    r = lax.broadcasted_iota(jnp.int32, (n, n), 0)
    c = lax.broadcasted_iota(jnp.int32, (n, n), 1)
    return (r >= c) if lower else (r <= c)


def _gates_fwd(ps, bi, bf, bff, name):
    S = ps.shape[0]
    NC = S // MLC

    def body(ps_ref, bi_ref, bf_ref, bff_ref, a_ref, A_ref, wi_ref, em_ref, wk_ref, dec_ref, F_ref, m_scr, f_scr):
        @pl.when(pl.program_id(0) == 0)
        def _():
            m_scr[...] = jnp.zeros_like(m_scr)
            f_scr[...] = jnp.zeros_like(f_scr)

        rows = lax.broadcasted_iota(jnp.int32, (MLC, LANES), 0)
        ltri = _tri_mask(MLC, True).astype(BF16)
        li = GATE_CAP * jnp.tanh((ps_ref[:, 0:LANES] + bi_ref[...]) / GATE_CAP)
        lf = _log_sigmoid(GATE_CAP * jnp.tanh((ps_ref[:, LANES:2 * LANES] + bf_ref[...]) / GATE_CAP))
        b = _tri_dot(ltri, lf)
        a = li - b
        cm = a
        sh = 1
        while sh < MLC:
            cm = jnp.where(rows >= sh, jnp.maximum(cm, pltpu.roll(cm, sh, 0)), cm)
            sh *= 2
        m0 = m_scr[...]
        A = jnp.maximum(cm, m0)
        a_ref[...] = a
        A_ref[...] = A
        A_last = A_ref[MLC - 1:MLC, :]
        wi_ref[...] = jnp.exp(m0 - A)
        em_ref[...] = jnp.exp(-(b + A))
        wk_ref[...] = jnp.exp(a - A_last)
        dec_ref[0] = jnp.exp(m0 - A_last)
        F_ref[...] = b
        m_scr[...] = F_ref[MLC - 1:MLC, :] + A_last
        lfg = _log_sigmoid(ps_ref[:, 2 * LANES:3 * LANES] + bff_ref[...])
        F_ref[...] = _tri_dot(ltri, lfg) + f_scr[...]
        f_scr[...] = F_ref[MLC - 1:MLC, :]

    col = pl.BlockSpec((MLC, LANES), lambda c: (c, 0))
    vec = pl.BlockSpec((1, LANES), lambda c: (0, 0))
    cs = jax.ShapeDtypeStruct((S, LANES), F32)
    return pl.pallas_call(
        body, name=name,
        out_shape=(cs, cs, cs, cs, cs, jax.ShapeDtypeStruct((NC, 1, LANES), F32), cs),
        grid=(NC,), in_specs=[pl.BlockSpec((MLC, N_SMALL), lambda c: (c, 0)), vec, vec, vec],
        out_specs=(col, col, col, col, col, pl.BlockSpec((1, 1, LANES), lambda c: (c, 0, 0)), col),
        scratch_shapes=[pltpu.VMEM((1, LANES), F32), pltpu.VMEM((1, LANES), F32)],
        compiler_params=_cparams(("arbitrary",)),
    )(ps, bi, bf, bff)


def _gates_bwd(ps, bi, bf, bff, rk, kc, tch, dF, name):
    S = ps.shape[0]
    NC = S // MLC

    def body(ps_ref, bi_ref, bf_ref, bff_ref, rk_ref, kc_ref, t_ref, dF_ref, dps_ref, db_ref, carry):
        @pl.when(pl.program_id(0) == 0)
        def _():
            carry[...] = jnp.zeros_like(carry)
            db_ref[...] = jnp.zeros_like(db_ref)

        lanes = lax.broadcasted_iota(jnp.int32, (MLC, LANES), 1)
        utri = _tri_mask(MLC, False).astype(BF16)
        ti = jnp.tanh((ps_ref[:, 0:LANES] + bi_ref[...]) / GATE_CAP)
        t_end, t_start = t_ref[0, 0:1, :], t_ref[0, 1:2, :]
        rk = rk_ref[...]
        rk = rk - (jnp.sum(rk, axis=0, keepdims=True) - (t_start - t_end)) * (1.0 / MLC)
        dpi = jnp.where(lanes < ML_HEADS, (kc_ref[...] - rk) * (1.0 - ti * ti), 0.0)
        tf = jnp.tanh((ps_ref[:, LANES:2 * LANES] + bf_ref[...]) / GATE_CAP)
        dlf = _tri_dot(utri, rk) + t_end
        dpf = jnp.where(lanes < ML_HEADS, dlf * jax.nn.sigmoid(-GATE_CAP * tf) * (1.0 - tf * tf), 0.0)
        dFv = dF_ref[...]
        dlfg = _tri_dot(utri, dFv) + carry[...]
        carry[...] += jnp.sum(dFv, axis=0, keepdims=True)
        dpff = jnp.where(lanes < FOX_HEADS, dlfg * jax.nn.sigmoid(-(ps_ref[:, 2 * LANES:3 * LANES] + bff_ref[...])), 0.0)
        for n, dp in enumerate((dpi, dpf, dpff)):
            dps_ref[:, n * LANES:(n + 1) * LANES] = dp.astype(dps_ref.dtype)
            db_ref[:, n * LANES:(n + 1) * LANES] += jnp.sum(dp, axis=0, keepdims=True)

    rev = lambda c: (NC - 1 - c, 0)
    col = pl.BlockSpec((MLC, LANES), rev)
    vec = pl.BlockSpec((1, LANES), lambda c: (0, 0))
    wide = pl.BlockSpec((MLC, N_SMALL), rev)
    return pl.pallas_call(
        body, name=name,
        out_shape=(jax.ShapeDtypeStruct((S, N_SMALL), BF16), jax.ShapeDtypeStruct((1, N_SMALL), F32)),
        grid=(NC,),
        in_specs=[wide, vec, vec, vec, col, col, pl.BlockSpec((1, 2, LANES), lambda c: (NC - 1 - c, 0, 0)), col],
        out_specs=(wide, pl.BlockSpec((1, N_SMALL), lambda c: (0, 0))),
        scratch_shapes=[pltpu.VMEM((1, LANES), F32)],
        compiler_params=_cparams(("arbitrary",)),
    )(ps, bi, bf, bff, rk, kc, tch, dF)


_ML_SCALE = ML_DQK ** -0.5


def _ml_specs(rev, NC):
    idx = (lambda c: NC - 1 - c) if rev else (lambda c: c)
    qk = lambda blk: pl.BlockSpec((MLC, ML_HEADS * ML_DQK), lambda c: (idx(c), blk))
    wide = lambda blk: pl.BlockSpec((MLC, D_MODEL), lambda c: (idx(c), blk))
    col = pl.BlockSpec((MLC, LANES), lambda c: (idx(c), 0))
    return idx, qk, wide, col


def _ml_intra(q_ref, k_ref, arow_ref, A_ref, h):
    hs = slice(h * ML_DQK, (h + 1) * ML_DQK)
    qf = q_ref[:, hs] * _ML_SCALE
    kf = k_ref[:, hs]
    qb, kb = qf.astype(BF16), kf.astype(BF16)
    qk = _dot_nt(qb, kb)
    logw = arow_ref[h:h + 1, :] - A_ref[:, h:h + 1]
    W = jnp.exp(jnp.where(_tri_mask(MLC, True), logw, -1e30))
    return qb, kb, qf, kf, qk, W


def _mlstm_fwd(pm, a_row, A, wi, em, wk, dec, w_hn, name):
    S = pm.shape[0]
    NC = S // MLC
    _, qk, wide, col = _ml_specs(False, NC)

    def body(q_ref, k_ref, v_ref, o_ref, arow_ref, A_ref, wi_ref, em_ref, wk_ref, dec_ref, whn_ref,
             ha_ref, hp_ref, den_ref, cst_ref, nst_ref, C_scr, n_scr):
        @pl.when(pl.program_id(0) == 0)
        def _():
            C_scr[...] = jnp.zeros_like(C_scr)
            n_scr[...] = jnp.zeros_like(n_scr)

        lanes = lax.broadcasted_iota(jnp.int32, (MLC, LANES), 1)
        den_tile = jnp.zeros((MLC, LANES), F32)
        for h in range(ML_HEADS):
            vs = slice(h * ML_DV, (h + 1) * ML_DV)
            qb, kb, qf, kf, qk_, W = _ml_intra(q_ref, k_ref, arow_ref, A_ref, h)
            vb = v_ref[:, vs].astype(BF16)
            Cf = C_scr[h]
            Cb = Cf.astype(BF16)
            nrow = n_scr[h]
            cst_ref[0, h] = Cb
            nst_ref[0, h] = nrow
            s = qk_ * W
            wic = wi_ref[:, h:h + 1]
            num = _dot_nn(s.astype(BF16), vb) + wic * _dot_nt(qb, Cb)
            den = jnp.sum(s, axis=1, keepdims=True) + wic * jnp.sum(qf * nrow, axis=1, keepdims=True)
            hp = num / jnp.maximum(jnp.abs(den), em_ref[:, h:h + 1])
            hp_ref[:, vs] = hp
            den_tile = jnp.where(lanes == h, den, den_tile)
            hn = hp * _rstd(hp) * whn_ref[:, vs]
            ha_ref[:, vs] = (hn * jax.nn.sigmoid(o_ref[:, vs])).astype(ha_ref.dtype)
            wkc = wk_ref[:, h:h + 1]
            kw = kf * wkc
            d = dec_ref[0, :, h:h + 1]
            C_scr[h] = d * Cf + _dot_tn(vb, kw.astype(BF16))
            n_scr[h] = d * nrow + jnp.sum(kw, axis=0, keepdims=True)
        den_ref[...] = den_tile

    return pl.pallas_call(
        body, name=name,
        out_shape=(jax.ShapeDtypeStruct((S, D_MODEL), BF16), jax.ShapeDtypeStruct((S, D_MODEL), F32),
                   jax.ShapeDtypeStruct((S, LANES), F32),
                   jax.ShapeDtypeStruct((NC, ML_HEADS, ML_DV, ML_DQK), BF16),
                   jax.ShapeDtypeStruct((NC, ML_HEADS, 1, ML_DQK), F32)),
        grid=(NC,),
        in_specs=[qk(C_QM // 512), qk(C_KM // 512), wide(C_VM // D_MODEL), wide(C_OM // D_MODEL),
                  pl.BlockSpec((8, MLC), lambda c: (0, c)), col, col, col, col,
                  pl.BlockSpec((1, 1, LANES), lambda c: (c, 0, 0)), pl.BlockSpec((1, D_MODEL), lambda c: (0, 0))],
        out_specs=(pl.BlockSpec((MLC, D_MODEL), lambda c: (c, 0)), pl.BlockSpec((MLC, D_MODEL), lambda c: (c, 0)),
                   col, pl.BlockSpec((1, ML_HEADS, ML_DV, ML_DQK), lambda c: (c, 0, 0, 0)),
                   pl.BlockSpec((1, ML_HEADS, 1, ML_DQK), lambda c: (c, 0, 0, 0))),
        scratch_shapes=[pltpu.VMEM((ML_HEADS, ML_DV, ML_DQK), F32), pltpu.VMEM((ML_HEADS, 1, ML_DQK), F32)],
        compiler_params=_cparams(("arbitrary",)),
    )(pm, pm, pm, pm, a_row, A, wi, em, wk, dec, w_hn)


def _mlstm_bwd(dha, pm, hp_all, den_all, a_row, A, wi, em, wk, dec, cst, nst, w_hn, name):
    S = pm.shape[0]
    NC = S // MLC
    idx, qk, wide, col = _ml_specs(True, NC)

    def body(dha_ref, q_ref, k_ref, v_ref, o_ref, hp_ref, den_ref, arow_ref, A_ref, wi_ref, em_ref, wk_ref,
             dec_ref, cst_ref, nst_ref, whn_ref,
             dqk_ref, dv_ref, do_ref, rk_ref, kc_ref, t_ref, dwhn_ref, dC_scr, dn_scr, t_scr):
        @pl.when(pl.program_id(0) == 0)
        def _():
            dC_scr[...] = jnp.zeros_like(dC_scr)
            dn_scr[...] = jnp.zeros_like(dn_scr)
            t_scr[...] = jnp.zeros_like(t_scr)
            dwhn_ref[...] = jnp.zeros_like(dwhn_ref)

        lanes = lax.broadcasted_iota(jnp.int32, (MLC, LANES), 1)
        lane1 = lax.broadcasted_iota(jnp.int32, (1, LANES), 1)
        t_ref[0, 0:1, :] = t_scr[...]
        rk_tile = jnp.zeros((MLC, LANES), F32)
        kc_tile = jnp.zeros((MLC, LANES), F32)
        t_new = jnp.zeros((1, LANES), F32)
        for h in range(ML_HEADS):
            hs = slice(h * ML_DQK, (h + 1) * ML_DQK)
            vs = slice(h * ML_DV, (h + 1) * ML_DV)
            hp = hp_ref[:, vs]
            sig = jax.nn.sigmoid(o_ref[:, vs])
            whn = whn_ref[:, vs]
            r = _rstd(hp)
            dga = dha_ref[:, vs]
            do_ref[:, vs] = (dga * (hp * r * whn) * sig * (1.0 - sig)).astype(do_ref.dtype)
            dhn = dga * sig
            dhp, dwt = _rmsnorm_bwd_math(dhn, hp, whn)
            dwhn_ref[:, vs] += jnp.sum(dwt, axis=0, keepdims=True)
            den = den_ref[:, h:h + 1]
            floor = em_ref[:, h:h + 1]
            D = jnp.maximum(jnp.abs(den), floor)
            dnum = dhp / D
            dh_h = jnp.sum(dhp * hp, axis=1, keepdims=True)
            active = jnp.abs(den) >= floor
            dden = -dh_h / D * jnp.where(active, jnp.sign(den), 0.0)
            phi = jnp.where(active, 0.0, dh_h)
            qb, kb, qf, kf, qk_, W = _ml_intra(q_ref, k_ref, arow_ref, A_ref, h)
            vf = v_ref[:, vs]
            vb = vf.astype(BF16)
            Cb = cst_ref[0, h]
            nrow = nst_ref[0, h]
            wic = wi_ref[:, h:h + 1]
            wkc = wk_ref[:, h:h + 1]
            d = dec_ref[0, :, h:h + 1]
            dCn = dC_scr[h]
            dCb = dCn.astype(BF16)
            dnn = dn_scr[h]
            dnumb = dnum.astype(BF16)
            s = qk_ * W
            ds = (_dot_nt(dnumb, vb) + dden) * W
            dsb = ds.astype(BF16)
            dnw = (wic * dnum).astype(BF16)
            wd = wic * dden
            kw = kf * wkc
            dv_state = _dot_nt(kw.astype(BF16), dCb)
            dq = _dot_nn(dsb, kb) + _dot_nn(dnw, Cb) + wd * nrow
            dk_state = wkc * (_dot_nn(vb, dCb) + dnn)
            dk = _dot_tn(dsb, qb) + dk_state
            dv = _dot_tn(s.astype(BF16), dnumb) + dv_state
            dC = d * dCn + _dot_tn(dnw, qb)
            dn = d * dnn + jnp.sum(wd * qf, axis=0, keepdims=True)
            dC_scr[h] = dC
            dn_scr[h] = dn
            dqk_ref[:, hs] = (dq * _ML_SCALE).astype(dqk_ref.dtype)
            dqk_ref[:, C_KM + h * ML_DQK:C_KM + (h + 1) * ML_DQK] = dk.astype(dqk_ref.dtype)
            dv_ref[:, vs] = dv.astype(dv_ref.dtype)
            G = ds * qk_
            inter = _dot_nt(qb, Cb)
            qn = jnp.sum(qf * nrow, axis=1, keepdims=True)
            R = (jnp.sum(G, axis=1, keepdims=True)
                 + wic * (jnp.sum(dnum * inter, axis=1, keepdims=True) + dden * qn))
            K = jnp.sum(G.T, axis=1, keepdims=True) + jnp.sum(kf * dk_state, axis=1, keepdims=True)
            rk_tile = jnp.where(lanes == h, R - K, rk_tile)
            kc_tile = jnp.where(lanes == h, phi, kc_tile)
            tt = (jnp.sum(jnp.sum(dC * Cb.astype(F32), axis=1, keepdims=True), axis=0, keepdims=True)
                  + jnp.sum(dn * nrow, axis=1, keepdims=True))
            t_new = jnp.where(lane1 == h, tt, t_new)
        rk_ref[...] = rk_tile
        kc_ref[...] = kc_tile
        t_ref[0, 1:2, :] = t_new
        t_scr[...] = t_new

    act = lambda n: jax.ShapeDtypeStruct((S, n), BF16)
    cs = jax.ShapeDtypeStruct((S, LANES), F32)
    rowblk = lambda n: pl.BlockSpec((MLC, n), lambda c: (idx(c), 0))
    return pl.pallas_call(
        body, name=name,
        out_shape=(act(D_MODEL), act(D_MODEL), act(D_MODEL), cs, cs,
                   jax.ShapeDtypeStruct((NC, 2, LANES), F32), jax.ShapeDtypeStruct((1, D_MODEL), F32)),
        grid=(NC,),
        in_specs=[rowblk(D_MODEL), qk(C_QM // 512), qk(C_KM // 512), wide(C_VM // D_MODEL), wide(C_OM // D_MODEL),
                  rowblk(D_MODEL), col, pl.BlockSpec((8, MLC), lambda c: (0, idx(c))), col, col, col, col,
                  pl.BlockSpec((1, 1, LANES), lambda c: (idx(c), 0, 0)),
                  pl.BlockSpec((1, ML_HEADS, ML_DV, ML_DQK), lambda c: (idx(c), 0, 0, 0)),
                  pl.BlockSpec((1, ML_HEADS, 1, ML_DQK), lambda c: (idx(c), 0, 0, 0)),
                  pl.BlockSpec((1, D_MODEL), lambda c: (0, 0))],
        out_specs=(rowblk(D_MODEL), rowblk(D_MODEL), rowblk(D_MODEL), col, col,
                   pl.BlockSpec((1, 2, LANES), lambda c: (idx(c), 0, 0)), pl.BlockSpec((1, D_MODEL), lambda c: (0, 0))),
        scratch_shapes=[pltpu.VMEM((ML_HEADS, ML_DV, ML_DQK), F32), pltpu.VMEM((ML_HEADS, 1, ML_DQK), F32),
                        pltpu.VMEM((1, LANES), F32)],
        compiler_params=_cparams(("arbitrary",)),
    )(dha, pm, pm, pm, pm, hp_all, den_all, a_row, A, wi, em, wk, dec, cst, nst, w_hn)


_FOX_SCALE = FOX_DH ** -0.5
_NEG = -1e30
_LOG2E = 1.4426950408889634
_LN2 = 0.6931471805599453
_QF_BLK, _KF_BLK, _VF_BLK = 0, FOX_HEADS, 2 * FOX_HEADS


def _lane_pick(tile, lane):
    lanes = lax.broadcasted_iota(jnp.int32, tile.shape, 1)
    return jnp.sum(jnp.where(lanes == lane, tile, 0.0), axis=1, keepdims=True)


def _col_to_row(col):
    return jnp.max(jnp.broadcast_to(col, (col.shape[0], LANES)).T, axis=0, keepdims=True)


def _causal(q0, k0, shape, q_axis):
    qpos = q0 + lax.broadcasted_iota(jnp.int32, shape, q_axis)
    kpos = k0 + lax.broadcasted_iota(jnp.int32, shape, 1 - q_axis)
    return kpos <= qpos


def _fox_fwd(pf, fc, fk_row, name):
    S = pf.shape[0]
    TQ, TK = FOX_TQ_FWD, FOX_TK_FWD
    nq, nk = S // TQ, S // TK
    c1 = _FOX_SCALE * _LOG2E

    def body(q_ref, k_ref, v_ref, fc_ref, fr_ref, o_ref, lse_ref):
        h, i = pl.program_id(0), pl.program_id(1)
        qb = q_ref[...]
        fq2 = _lane_pick(fc_ref[...], h) * _LOG2E

        def step(j, carry, masked):
            m, l, acc = carry
            off = pl.multiple_of(j * TK, TK)
            t = _dot_nt(qb, k_ref[pl.ds(off, TK), :]) * c1 - fr_ref[0, j] * _LOG2E
            if masked:
                t = jnp.where(_causal(i * TQ, j * TK, (TQ, TK), 0), t, _NEG)
            m_new = jnp.maximum(m, jnp.max(t, axis=1, keepdims=True) + fq2)
            alpha = jnp.exp2(m - m_new)
            p = jnp.exp2(t + (fq2 - m_new))
            l = alpha * l + jnp.sum(p, axis=1, keepdims=True)
            acc = alpha * acc + _dot_nn(p.astype(BF16), v_ref[pl.ds(off, TK), :])
            return m_new, l, acc

        init = (jnp.full((TQ, 1), _NEG, F32), jnp.zeros((TQ, 1), F32), jnp.zeros((TQ, FOX_DH), F32))
        last = (i * TQ) // TK
        carry = lax.fori_loop(0, last, lambda j, c: step(j, c, False), init)
        for d in range(TQ // TK):
            carry = step(last + d, carry, True)
        m, l, acc = carry
        o_ref[...] = (acc / l).astype(o_ref.dtype)
        lse_ref[0, 0] = _col_to_row((m + jnp.log2(l)) * _LN2)

    head = lambda blk: pl.BlockSpec((S, FOX_DH), lambda h, i: (0, blk + h))
    return pl.pallas_call(
        body, name=name,
        out_shape=(jax.ShapeDtypeStruct((S, D_MODEL), BF16), jax.ShapeDtypeStruct((FOX_HEADS, nq, 1, TQ), F32)),
        grid=(FOX_HEADS, nq),
        in_specs=[pl.BlockSpec((TQ, FOX_DH), lambda h, i: (i, _QF_BLK + h)), head(_KF_BLK), head(_VF_BLK),
                  pl.BlockSpec((TQ, LANES), lambda h, i: (i, 0)),
                  pl.BlockSpec((1, nk, 1, TK), lambda h, i: (h, 0, 0, 0))],
        out_specs=(pl.BlockSpec((TQ, FOX_DH), lambda h, i: (i, h)),
                   pl.BlockSpec((1, 1, 1, TQ), lambda h, i: (h, i, 0, 0))),
        compiler_params=_cparams(("parallel", "arbitrary")),
    )(pf, pf, pf, fc, fk_row)


def _fox_bwd(dhb, hb, pf, lse_row, fq_row, fc, name):
    S = pf.shape[0]
    TQ, TK = FOX_TQ, FOX_TK
    nq, nk, r = S // TQ, S // TK, TK // TQ
    c1 = _FOX_SCALE * _LOG2E

    def body(q_ref, k_ref, v_ref, do_ref, o_ref, lse_ref, fq_ref, fc_ref,
             dq_ref, dk_ref, dv_ref, dFk_ref, dFq_ref, dq_acc, qside, delta, dk_acc, dv_acc, cs_acc):
        h, j = pl.program_id(0), pl.program_id(1)

        @pl.when(j == 0)
        def _():
            dq_acc[...] = jnp.zeros_like(dq_acc)
            dFq_ref[...] = jnp.zeros_like(dFq_ref)

            def fill(b, _):
                off = pl.multiple_of(b * TQ, TQ)
                prod = do_ref[pl.ds(off, TQ), :].astype(F32) * o_ref[pl.ds(off, TQ), :].astype(F32)
                delta[b] = jnp.sum(prod.T, axis=0, keepdims=True)
                qside[b] = (fq_ref[0, b] - lse_ref[0, b]) * _LOG2E
                return 0

            lax.fori_loop(0, nq, fill, 0)

        kb = k_ref[...]
        vb = v_ref[...]
        fk2 = _lane_pick(fc_ref[...], h) * _LOG2E
        dk_acc[...] = jnp.zeros_like(dk_acc)
        dv_acc[...] = jnp.zeros_like(dv_acc)
        cs_acc[...] = jnp.zeros_like(cs_acc)

        def step(i, masked):
            off = pl.multiple_of(i * TQ, TQ)
            qb = q_ref[pl.ds(off, TQ), :]
            dob = do_ref[pl.ds(off, TQ), :]
            t = _dot_nt(kb, qb) * c1 + qside[i] - fk2
            if masked:
                t = jnp.where(_causal(i * TQ, j * TK, (TK, TQ), 1), t, _NEG)
            p = jnp.exp2(t)
            dv_acc[...] += _dot_nn(p.astype(BF16), dob)
            ds = p * (_dot_nt(vb, dob) - delta[i])
            dsb = ds.astype(BF16)
            dk_acc[...] += _dot_nn(dsb, qb)
            dq_acc[pl.ds(off, TQ), :] += _dot_tn(dsb, kb)
            cs_acc[...] += jnp.sum(ds, axis=1, keepdims=True)
            dFq_ref[0, i] += jnp.sum(ds, axis=0, keepdims=True)

        for d in range(r):
            step(r * j + d, True)

        def rest(i, _):
            step(i, False)
            return 0

        lax.fori_loop(r * j + r, nq, rest, 0)
        dk_ref[...] = (dk_acc[...] * _FOX_SCALE).astype(dk_ref.dtype)
        dv_ref[...] = dv_acc[...].astype(dv_ref.dtype)
        dFk_ref[0, 0] = -_col_to_row(cs_acc[...])

        @pl.when(j == nk - 1)
        def _():
            dq_ref[...] = (dq_acc[...] * _FOX_SCALE).astype(dq_ref.dtype)

    head = lambda blk: pl.BlockSpec((S, FOX_DH), lambda h, j: (0, blk + h))
    kblk = lambda blk: pl.BlockSpec((TK, FOX_DH), lambda h, j: (j, blk + h))
    qrows = pl.BlockSpec((1, nq, 1, TQ), lambda h, j: (h, 0, 0, 0))
    act = jax.ShapeDtypeStruct((S, D_MODEL), BF16)
    return pl.pallas_call(
        body, name=name,
        out_shape=(act, act, act, jax.ShapeDtypeStruct((FOX_HEADS, nk, 1, TK), F32),
                   jax.ShapeDtypeStruct((FOX_HEADS, nq, 1, TQ), F32)),
        grid=(FOX_HEADS, nk),
        in_specs=[head(_QF_BLK), kblk(_KF_BLK), kblk(_VF_BLK), head(0), head(0), qrows, qrows,
                  pl.BlockSpec((TK, LANES), lambda h, j: (j, 0))],
        out_specs=(head(0), kblk(0), kblk(0), pl.BlockSpec((1, 1, 1, TK), lambda h, j: (h, j, 0, 0)), qrows),
        scratch_shapes=[pltpu.VMEM((S, FOX_DH), F32), pltpu.VMEM((nq, 1, TQ), F32), pltpu.VMEM((nq, 1, TQ), F32),
                        pltpu.VMEM((TK, FOX_DH), F32), pltpu.VMEM((TK, FOX_DH), F32), pltpu.VMEM((TK, 1), F32)],
        compiler_params=_cparams(("parallel", "arbitrary")),
    )(pf, pf, pf, dhb, hb, lse_row, fq_row, fc)


def _pad_lanes(v):
    return jnp.pad(v, ((0, 0), (0, LANES - v.shape[1])))


def _local_step(x, target, wmain_t, wsmall_t, rest_arrived, rest_weights, p, on_grads, advance, token):
    S = x.shape[0]
    bi, bf, bff = _pad_lanes(p["b_ml_i"]), _pad_lanes(p["b_ml_f"]), _pad_lanes(p["b_fox_f"])

    h0 = _rmsnorm_fwd(x, p["norm_mix_pre"] + token[0:1, 0:1], "norm_mix_pre")
    pm = _mm(h0, wmain_t, "nt", F32, "proj_mlstm", b_rows=(0, N_ML))
    pf = _mm(h0, wmain_t, "nt", BF16, "proj_fox", b_rows=(N_ML, N_FOX))
    pg = _mm(h0, wmain_t, "nt", F32, "proj_merge", b_rows=(N_ML + N_FOX, N_GATE))
    ps = _mm(h0, wsmall_t, "nt", F32, "proj_gates")
    a, A, wi, em, wk, dec, Fc = _gates_fwd(ps, bi, bf, bff, "gates_fwd")
    a_row = a[:, :8].T
    ha, hp, den, cst, nst = _mlstm_fwd(pm, a_row, A, wi, em, wk, dec, p["ml_head_norm"], "mlstm_fwd")
    ft = Fc[:, :FOX_HEADS].T + rest_arrived(ha)[0, 0]
    fq_row = ft.reshape(FOX_HEADS, S // FOX_TQ, 1, FOX_TQ)
    fk_row = ft.reshape(FOX_HEADS, S // FOX_TK, 1, FOX_TK)
    hb, lse_row = _fox_fwd(pf, Fc, ft.reshape(FOX_HEADS, S // FOX_TK_FWD, 1, FOX_TK_FWD), "fox_fwd")
    wa, wb, wout, wup, wdown = rest_weights(hb)
    ya = _mm(ha, wa, "nn", F32, "branch_a")
    yb = _mm(hb, wb, "nn", F32, "branch_b")
    merged = _merge_fwd(ya, yb, pg, p["b_gate_a"], p["b_gate_b"], "merge_fwd")
    z = _mm(merged, wout, "nn", F32, "out_proj")
    x1, h2 = _resid_norm_fwd(x, z, p["norm_mix_post"], p["norm_ffn_pre"], "resid_mix")
    up = _mm(h2, wup, "nn", F32, "ffn_up")
    act, conv_a, conv_g = _conv_act_fwd(up, p["conv_w"], p["conv_b"], "conv_act_fwd")
    d = _mm(act, wdown, "nn", F32, "ffn_down", tk=D_FF)
    loss_row, dy, dd, g_norm_ffn_post = _loss_head(x1, d, p["norm_ffn_post"], target, "loss_head")
    dact = _mm(dd, wdown, "nt", F32, "d_act")
    g_wdown = _mm(act, dd, "tn", F32, "dw_down", tm=1408, tk=2048)
    dupa, dupg, dcwa, dcwg, dcba, dcbg = _conv_act_bwd(up, conv_a, conv_g, dact, p["conv_w"], "conv_act_bwd")
    g_conv_w = jnp.concatenate([dcwa, dcwg], axis=1)
    g_conv_b = jnp.concatenate([dcba, dcbg], axis=1)
    dh2 = _mm_sum_parts([dupa, dupg], wup, F32, "d_h2", trans_b=True)
    g_wup = _mm(h2, [dupa, dupg], "tn", F32, "dw_up", tk=2048)
    token = on_grads("ffn", dict(w_up=g_wup, w_down=g_wdown))
    dx1, dz, g_norm_ffn_pre, g_norm_mix_post = _norm_chain_bwd(
        dh2, x1, p["norm_ffn_pre"] + token[0:1, 0:1], dy, z, p["norm_mix_post"], "norm_chain_bwd")
    dmerged = _mm(dz, wout, "nt", F32, "d_merged")
    g_wout = _mm(merged, dz, "tn", F32, "dw_out", tk=2048)
    dya, dyb, dga, dgb, g_b_gate_a, g_b_gate_b = _merge_bwd(dmerged, ya, yb, pg, p["b_gate_a"], p["b_gate_b"], "merge_bwd")
    dha = _mm(dya, wa, "nt", F32, "d_ha")
    g_wa = _mm(ha, dya, "tn", F32, "dw_a", tk=2048)
    dhb = _mm(dyb, wb, "nt", BF16, "d_hb")
    g_wb = _mm(hb, dyb, "tn", F32, "dw_b", tk=2048)
    token = advance("ffn", g_wb) + on_grads("mix", dict(w_out=g_wout, w_branch_a=g_wa, w_branch_b=g_wb))
    dqkm, dvm, dom, rk, kc, tch, g_ml_head_norm = _mlstm_bwd(
        dha, pm, hp, den, a_row, A, wi, em, wk, dec, cst, nst, p["ml_head_norm"] + token[0:1, 0:1], "mlstm_bwd")
    token = advance("mix", dqkm)
    dqf, dkf, dvf, dFk, dFq = _fox_bwd(dhb, hb, pf, lse_row.reshape(fq_row.shape), fq_row + token[0, 0], Fc, "fox_bwd")
    dF = jnp.pad((dFk.reshape(FOX_HEADS, S) + dFq.reshape(FOX_HEADS, S)).T, ((0, 0), (0, LANES - FOX_HEADS)))
    dps, dbias = _gates_bwd(ps, bi, bf, bff, rk, kc, tch, dF, "gates_bwd")
    dpm = [dqkm, dvm, dom, dqf, dkf, dvf, dga, dgb]
    g_wmain_t = _mm(dpm, h0, "tn", F32, "dw_main")
    token = on_grads("in", dict(w_in=g_wmain_t))
    g_wsmall_t = _mm(dps, h0, "tn", F32, "dw_gates")
    dh0s = _mm(dps, wsmall_t + token[0:1, 0:1].astype(BF16), "nn", F32, "d_h0_gates")
    token = advance("in", dh0s)
    dh0 = _mm_sum_parts(dpm, wmain_t, F32, "d_h0_main", after=token)
    grad_x, g_norm_mix_pre = _rmsnorm_bwd([dh0, dh0s], x, p["norm_mix_pre"], dx1, F32, "norm_mix_pre_bwd")

    big = dict(wsmall_t=g_wsmall_t)
    small = dict(norm_mix_pre=g_norm_mix_pre, ml_head_norm=g_ml_head_norm, b_gate_a=g_b_gate_a, b_gate_b=g_b_gate_b,
                 norm_mix_post=g_norm_mix_post, norm_ffn_pre=g_norm_ffn_pre, norm_ffn_post=g_norm_ffn_post,
                 conv_b=g_conv_b, b_ml_i=dbias[:, 0:ML_HEADS], b_ml_f=dbias[:, LANES:LANES + ML_HEADS],
                 b_fox_f=dbias[:, 2 * LANES:2 * LANES + FOX_HEADS], conv_w=g_conv_w)
    return loss_row, grad_x, big, small


def _row_tile(r, target=256):
    best = None
    for t in range(8, min(r, target) + 1, 8):
        if r % t == 0:
            best = t
    return best if best is not None else r


def _adamw(w, g, m, v, name):
    _, R, C = w.shape
    tr = _row_tile(R)
    tc = C
    if tr == R and R > 256:
        tc = 256

    def body(w_ref, g_ref, m_ref, v_ref, d_ref, mo_ref, vo_ref):
        gv = g_ref[...]
        mn = ADAM_B1 * m_ref[0] + (1.0 - ADAM_B1) * gv
        vn = ADAM_B2 * v_ref[0] + (1.0 - ADAM_B2) * (gv * gv)
        m_hat = mn / (1.0 - ADAM_B1 ** ADAM_STEP)
        v_hat = vn / (1.0 - ADAM_B2 ** ADAM_STEP)
        d_ref[0] = -ADAM_LR * (m_hat / (jnp.sqrt(v_hat) + ADAM_EPS) + ADAM_WD * w_ref[0])
        mo_ref[0] = mn
        vo_ref[0] = vn

    blk = pl.BlockSpec((1, tr, tc), lambda i, j: (0, i, j))
    o = jax.ShapeDtypeStruct((1, R, C), F32)
    return pl.pallas_call(
        body, name=name, out_shape=(o, o, o), grid=(R // tr, C // tc),
        in_specs=[blk, pl.BlockSpec((tr, tc), lambda i, j: (i, j)), blk, blk], out_specs=(blk,) * 3,
        compiler_params=_cparams(("parallel", "parallel")),
    )(w, g, m, v)


ANY = pl.BlockSpec(memory_space=pl.ANY)


def _place():
    x, y, c = lax.axis_index("x"), lax.axis_index("y"), lax.axis_index("c")
    chips = [(1 - x, y), (x, 1 - y), (1 - x, 1 - y)]
    return x, y, c, chips


def _block(ref, kind, k, rows=None):
    if kind == "rows":
        return ref.at[k] if rows is None else ref.at[k, pl.ds(*rows), :]
    cb = ref.shape[1] // 4
    return ref.at[:, pl.ds(k * cb, cb)] if rows is None else ref.at[pl.ds(*rows), pl.ds(k * cb, cb)]


def _gathered_shape(s, kind):
    return (4,) + s.shape if kind == "rows" else (s.shape[0], 4 * s.shape[1])


def _gather_weights(shards, kinds, smalls):
    n, ns = len(shards), len(smalls)

    def body(*refs):
        ins, sm_in = refs[:n], refs[n:n + ns]
        outs, sm_out = refs[n + ns:2 * n + ns], refs[2 * n + ns:2 * (n + ns)]
        send_sems, recv_sems, sm_send, sm_recv, local_sems = refs[2 * (n + ns):]
        x, y, c, chips = _place()
        sibling = (x, y, 1 - c)
        kme = 2 * x + y

        def half(a, k, hc):
            h = ins[a].shape[0] // 2
            return _block(outs[a], kinds[a], k, (hc * h, h))

        def remote(a, slot, src, dst, to):
            return pltpu.make_async_remote_copy(src_ref=src, dst_ref=dst, send_sem=send_sems.at[a * 7 + slot],
                                                recv_sem=recv_sems.at[a * 7 + slot], device_id=to, device_id_type=MESH)

        def sm_copy(b, j, k, to):
            return pltpu.make_async_remote_copy(src_ref=sm_in[b], dst_ref=sm_out[b].at[k], send_sem=sm_send.at[3 * b + j],
                                                recv_sem=sm_recv.at[3 * b + j], device_id=to, device_id_type=MESH)

        local = [pltpu.make_async_copy(sm_in[b], sm_out[b].at[kme], local_sems.at[b]) for b in range(ns)]
        for cp in local:
            cp.start()
        sends = [remote(a, 6, ins[a], _block(outs[a], kinds[a], kme), sibling) for a in range(n)]
        for a in range(n):
            h = ins[a].shape[0] // 2
            for j, chip in enumerate(chips):
                sends.append(remote(a, j, ins[a].at[pl.ds(c * h, h), :], half(a, kme, c), (*chip, c)))
        for b in range(ns):
            for j, chip in enumerate(chips):
                sends.append(sm_copy(b, j, kme, (*chip, c)))
        for cp in sends:
            cp.start()
        for a in range(n):
            for j, chip in enumerate(chips):
                kj = 2 * chip[0] + chip[1]
                remote(a, j, half(a, kj, c), half(a, kj, c), (*chip, c)).wait_recv()
                fwd = remote(a, 3 + j, half(a, kj, c), half(a, kj, c), sibling)
                fwd.start()
                sends.append(fwd)
        for a in range(n):
            for j, chip in enumerate(chips):
                kj = 2 * chip[0] + chip[1]
                remote(a, 3 + j, half(a, kj, 1 - c), half(a, kj, 1 - c), sibling).wait_recv()
        for b in range(ns):
            for j, chip in enumerate(chips):
                sm_copy(b, j, 2 * chip[0] + chip[1], (*chip, c)).wait_recv()
        for a in range(n):
            remote(a, 6, ins[a], _block(outs[a], kinds[a], kme), sibling).wait_recv()
        for cp in sends:
            cp.wait_send()
        for cp in local:
            cp.wait()

    outs = pl.pallas_call(
        body, name="gather_weights",
        out_shape=tuple([jax.ShapeDtypeStruct(_gathered_shape(s, k), s.dtype) for s, k in zip(shards, kinds)]
                        + [jax.ShapeDtypeStruct((4,) + s.shape, s.dtype) for s in smalls]),
        in_specs=[ANY] * (n + ns), out_specs=tuple([ANY] * (n + ns)),
        scratch_shapes=[pltpu.SemaphoreType.DMA((7 * n,)), pltpu.SemaphoreType.DMA((7 * n,)),
                        pltpu.SemaphoreType.DMA((3 * ns,)), pltpu.SemaphoreType.DMA((3 * ns,)),
                        pltpu.SemaphoreType.DMA((ns,))],
    )(*shards, *smalls)
    return outs[:n], outs[n:]


_IN_HBM = pl.BlockSpec(memory_space=pltpu.HBM)
_SEMS = pl.BlockSpec(memory_space=pltpu.SEMAPHORE)
_DATAFLOW = pltpu.SideEffectType.DATAFLOW_SIDE_EFFECTING


def _hbm(t):
    return pltpu.HBM(t.shape, t.dtype)


def _gather_copies(ins, outs, send_sems, recv_sems, kinds):
    x, y, c, chips = _place()
    kme = 2 * x + y
    cps = []
    for a in range(len(ins)):
        h = ins[a].shape[0] // 2
        for j, chip in enumerate(chips + [None]):
            to = (x, y, 1 - c) if chip is None else (*chip, c)
            src = ins[a] if chip is None else ins[a].at[pl.ds(c * h, h), :]
            dst = _block(outs[a], kinds[a], kme, None if chip is None else (c * h, h))
            cps.append(pltpu.make_async_remote_copy(src_ref=src, dst_ref=dst, send_sem=send_sems.at[4 * a + j],
                                                    recv_sem=recv_sems.at[4 * a + j], device_id=to, device_id_type=MESH))
    return cps


def _gather_start(shards, kinds, name):
    n = len(shards)
    outs = [lax.empty(_gathered_shape(s, k), s.dtype) for s, k in zip(shards, kinds)]

    def body(*refs):
        for cp in _gather_copies(refs[:n], refs[n:2 * n], refs[2 * n], refs[2 * n + 1], kinds):
            cp.start()
        refs[-1][...] = jnp.zeros_like(refs[-1])

    return pl.pallas_call(
        body, name=name,
        out_shape=(pltpu.SemaphoreType.DMA((4 * n,)), pltpu.SemaphoreType.DMA((4 * n,)),
                   *[_hbm(t) for t in shards], *[_hbm(t) for t in outs], jax.ShapeDtypeStruct((8, LANES), F32)),
        in_specs=[_IN_HBM] * (2 * n),
        out_specs=(_SEMS, _SEMS, *[_IN_HBM] * (2 * n), pl.BlockSpec(memory_space=pltpu.VMEM)),
        input_output_aliases={a: 2 + a for a in range(2 * n)},
        compiler_params=pltpu.CompilerParams(has_side_effects=_DATAFLOW),
    )(*[pltpu.with_memory_space_constraint(t, pltpu.HBM) for t in list(shards) + outs])


def _gather_wait(started, after, kinds, name):
    n = (len(started) - 3) // 2
    bufs = started[2:2 + 2 * n]

    def body(*refs):
        for cp in _gather_copies(refs[:n], refs[n:2 * n], refs[2 * n], refs[2 * n + 1], kinds):
            cp.wait_send()
            cp.wait_recv()

    outs = pl.pallas_call(
        body, name=name, out_shape=tuple(_hbm(t) for t in bufs),
        in_specs=[_IN_HBM] * (2 * n) + [_SEMS, _SEMS, ANY], out_specs=tuple([_IN_HBM] * (2 * n)),
        input_output_aliases={a: a for a in range(2 * n)},
        compiler_params=pltpu.CompilerParams(has_side_effects=_DATAFLOW),
    )(*bufs, started[0], started[1], after)
    return outs[n:]


def _relay_copies(bufs, send_sems, recv_sems, kinds):
    x, y, c, chips = _place()
    cps = []
    for a in range(len(bufs)):
        h = (bufs[a].shape[1] if kinds[a] == "rows" else bufs[a].shape[0]) // 2
        for j, chip in enumerate(chips):
            part = _block(bufs[a], kinds[a], 2 * chip[0] + chip[1], (c * h, h))
            cps.append(pltpu.make_async_remote_copy(src_ref=part, dst_ref=part, send_sem=send_sems.at[3 * a + j],
                                                    recv_sem=recv_sems.at[3 * a + j], device_id=(x, y, 1 - c),
                                                    device_id_type=MESH))
    return cps


def _join_copies(bufs, send_sems, recv_sems, kinds):
    x, y, c, _ = _place()
    cps = []
    for a in range(len(bufs)):
        h = bufs[a].shape[0] // 2
        mine = bufs[a].at[pl.ds(c * h, h), :]
        cps.append(pltpu.make_async_remote_copy(src_ref=mine, dst_ref=mine, send_sem=send_sems.at[a],
                                                recv_sem=recv_sems.at[a], device_id=(x, y, 1 - c), device_id_type=MESH))
    return cps


def _inplace_start(copies, per_array, bufs, kinds, name):
    n = len(bufs)

    def body(*refs):
        for cp in copies(refs[:n], refs[n], refs[n + 1], kinds):
            cp.start()
        refs[-1][...] = jnp.zeros_like(refs[-1])

    return pl.pallas_call(
        body, name=name,
        out_shape=(pltpu.SemaphoreType.DMA((per_array * n,)), pltpu.SemaphoreType.DMA((per_array * n,)),
                   *[_hbm(t) for t in bufs], jax.ShapeDtypeStruct((8, LANES), F32)),
        in_specs=[_IN_HBM] * n, out_specs=(_SEMS, _SEMS, *[_IN_HBM] * n, pl.BlockSpec(memory_space=pltpu.VMEM)),
        input_output_aliases={a: 2 + a for a in range(n)},
        compiler_params=pltpu.CompilerParams(has_side_effects=_DATAFLOW),
    )(*[pltpu.with_memory_space_constraint(t, pltpu.HBM) for t in bufs])


def _inplace_wait(copies, started, after, kinds, name):
    n = len(started) - 3
    bufs = started[2:2 + n]

    def body(*refs):
        for cp in copies(refs[:n], refs[n], refs[n + 1], kinds):
            cp.wait_send()
            cp.wait_recv()

    return pl.pallas_call(
        body, name=name, out_shape=tuple(_hbm(t) for t in bufs),
        in_specs=[_IN_HBM] * n + [_SEMS, _SEMS, ANY], out_specs=tuple([_IN_HBM] * n),
        input_output_aliases={a: a for a in range(n)},
        compiler_params=pltpu.CompilerParams(has_side_effects=_DATAFLOW),
    )(*bufs, started[0], started[1], after)


def _add_halves(g, r1, cvec, kind, name):
    def body(c_ref, g_ref, r_ref, o_ref):
        o_ref[...] = (g_ref[...] + r_ref[...]).astype(o_ref.dtype)

    if kind == "rows":
        _, h, C = r1.shape
        tr = _row_tile(h, 512)
        nt = h // tr
        grid = (4, nt)
        g_spec = pl.BlockSpec((1, tr, C), lambda k, i, c_ref: (k, c_ref[0] * nt + i, 0))
        r_spec = pl.BlockSpec((1, tr, C), lambda k, i, c_ref: (k, i, 0))
    else:
        h, C4 = r1.shape
        tr, tc = _row_tile(h, 512), C4 // 4
        nt = h // tr
        grid = (nt, 4)
        g_spec = pl.BlockSpec((tr, tc), lambda i, k, c_ref: (c_ref[0] * nt + i, k))
        r_spec = pl.BlockSpec((tr, tc), lambda i, k, c_ref: (i, k))
    return pl.pallas_call(
        body, name=name, out_shape=jax.ShapeDtypeStruct(r1.shape, BF16),
        grid_spec=pltpu.PrefetchScalarGridSpec(num_scalar_prefetch=1, grid=grid, in_specs=[g_spec, r_spec],
                                               out_specs=r_spec),
        compiler_params=_cparams(("parallel", "parallel")),
    )(cvec, g, r1)


def _chip_copies(ins, lands, send_sems, recv_sems, kinds):
    x, y, c, chips = _place()
    return [pltpu.make_async_remote_copy(
        src_ref=_block(ins[a], kinds[a], 2 * chip[0] + chip[1]), dst_ref=lands[a].at[j],
        send_sem=send_sems.at[3 * a + j], recv_sem=recv_sems.at[3 * a + j], device_id=(*chip, c), device_id_type=MESH)
        for a in range(len(ins)) for j, chip in enumerate(chips)]


def _land_shape(s, kind):
    return (3,) + (s.shape[1:] if kind == "rows" else (s.shape[0], s.shape[1] // 4))


def _sibling_copies(ins, lands, send_sems, recv_sems, kinds):
    x, y, c, _ = _place()
    cps = []
    for a in range(len(ins)):
        h = lands[a].shape[-2]
        src = ins[a].at[:, pl.ds((1 - c) * h, h), :] if kinds[a] == "rows" else ins[a].at[pl.ds((1 - c) * h, h), :]
        cps.append(pltpu.make_async_remote_copy(src_ref=src, dst_ref=lands[a], send_sem=send_sems.at[a],
                                                recv_sem=recv_sems.at[a], device_id=(x, y, 1 - c), device_id_type=MESH))
    return cps


def _half_shape(g, kind):
    return (4, g.shape[1] // 2, g.shape[2]) if kind == "rows" else (g.shape[0] // 2, g.shape[1])


def _exchange_start(copies, per_array, srcs, land_shapes, kinds, name, zeroed=False):
    n = len(srcs)
    lands = [(jnp.zeros if zeroed else lax.empty)(shape, s.dtype) for shape, s in zip(land_shapes, srcs)]

    def body(*refs):
        for cp in copies(refs[:n], refs[n:2 * n], refs[2 * n], refs[2 * n + 1], kinds):
            cp.start()
        refs[-1][...] = jnp.zeros_like(refs[-1])

    return pl.pallas_call(
        body, name=name,
        out_shape=(pltpu.SemaphoreType.DMA((per_array * n,)), pltpu.SemaphoreType.DMA((per_array * n,)),
                   *[_hbm(t) for t in srcs], *[_hbm(t) for t in lands], jax.ShapeDtypeStruct((8, LANES), F32)),
        in_specs=[_IN_HBM] * (2 * n),
        out_specs=(_SEMS, _SEMS, *[_IN_HBM] * (2 * n), pl.BlockSpec(memory_space=pltpu.VMEM)),
        input_output_aliases={a: 2 + a for a in range(2 * n)},
        compiler_params=pltpu.CompilerParams(has_side_effects=_DATAFLOW),
    )(*[pltpu.with_memory_space_constraint(t, pltpu.HBM) for t in list(srcs) + lands])


def _exchange_wait(copies, started, after, kinds, name):
    n = (len(started) - 3) // 2
    bufs = started[2:2 + 2 * n]

    def body(*refs):
        for cp in copies(refs[:n], refs[n:2 * n], refs[2 * n], refs[2 * n + 1], kinds):
            cp.wait_send()
            cp.wait_recv()

    outs = pl.pallas_call(
        body, name=name, out_shape=tuple(_hbm(t) for t in bufs),
        in_specs=[_IN_HBM] * (2 * n) + [_SEMS, _SEMS, ANY], out_specs=tuple([_IN_HBM] * (2 * n)),
        input_output_aliases={a: a for a in range(2 * n)},
        compiler_params=pltpu.CompilerParams(has_side_effects=_DATAFLOW),
    )(*bufs, started[0], started[1], after)
    return outs[:n], outs[n:]


def _add_chips(s1, r2, kcvec, kind, name):
    _, h, C = r2.shape
    tr = _row_tile(h, 512)
    nt = h // tr

    def body(kc_ref, s_ref, r0_ref, r1_ref, r2_ref, o_ref):
        s = s_ref[0] if kind == "rows" else s_ref[...]
        o_ref[...] = ((s.astype(F32) + r0_ref[0].astype(F32)) + r1_ref[0].astype(F32)) + r2_ref[0].astype(F32)

    peer = lambda j: pl.BlockSpec((1, tr, C), lambda i, kc_ref: (j, i, 0))
    if kind == "rows":
        s_spec = pl.BlockSpec((1, tr, C), lambda i, kc_ref: (kc_ref[0], i, 0))
    else:
        s_spec = pl.BlockSpec((tr, C), lambda i, kc_ref: (i, kc_ref[0]))
    return pl.pallas_call(
        body, name=name, out_shape=jax.ShapeDtypeStruct((2 * h, C), F32),
        grid_spec=pltpu.PrefetchScalarGridSpec(
            num_scalar_prefetch=1, grid=(nt,),
            in_specs=[s_spec, peer(0), peer(1), peer(2)],
            out_specs=pl.BlockSpec((tr, C), lambda i, kc_ref: (kc_ref[1] * nt + i, 0))),
        compiler_params=_cparams(("parallel",)),
    )(kcvec, s1, r2, r2, r2)


N_DEV = 8


def _spread_copies(packs, lands, send_sems, recv_sems, kinds):
    x, y, c, _ = _place()
    me = 4 * x + 2 * y + c
    return [pltpu.make_async_remote_copy(
        src_ref=packs[0], dst_ref=lands[0].at[me], send_sem=send_sems.at[mask - 1], recv_sem=recv_sems.at[mask - 1],
        device_id=(1 - x if mask & 4 else x, 1 - y if mask & 2 else y, 1 - c if mask & 1 else c), device_id_type=MESH)
        for mask in range(1, N_DEV)]


def _sum_spread(pack, gathered):
    P = pack.shape[0]

    def body(p_ref, g_ref, o_ref):
        x, y, c, _ = _place()
        me = 4 * x + 2 * y + c
        acc = None
        for i in range(N_DEV):
            term = jnp.where(me == i, p_ref[...], g_ref[i])
            acc = term if acc is None else acc + term
        o_ref[...] = acc

    vmem = pl.BlockSpec(memory_space=pltpu.VMEM)
    return pl.pallas_call(body, name="allreduce_sum", out_shape=jax.ShapeDtypeStruct((P, LANES), F32),
                          in_specs=[vmem, vmem], out_specs=vmem)(pack, gathered)


def _pack_rows(arrs):
    rows = []
    for a in arrs:
        f = a.reshape(-1)
        f = jnp.pad(f, (0, (-f.shape[0]) % (8 * LANES)))
        rows.append(f.reshape(-1, LANES))
    return jnp.concatenate(rows, axis=0)


def _unpack_rows(pack, shapes):
    out, r = [], 0
    for s in shapes:
        n = math.prod(s)
        out.append(pack[r:r + -(-n // LANES)].reshape(-1)[:n].reshape(s))
        r += 8 * -(-n // (8 * LANES))
    return out


_SMALL = ["norm_mix_pre", "ml_head_norm", "b_gate_a", "b_gate_b", "norm_mix_post", "norm_ffn_pre", "norm_ffn_post",
          "conv_b", "b_ml_i", "b_ml_f", "b_fox_f"]
_BIG = ["w_in", "w_branch_a", "w_branch_b", "w_out", "w_up", "w_down"]
_WEIGHTS = ['norm_mix_pre', 'w_in', 'b_ml_i', 'b_ml_f', 'ml_head_norm', 'b_fox_f', 'b_gate_a', 'b_gate_b', 'w_branch_a',
            'w_branch_b', 'w_out', 'norm_mix_post', 'norm_ffn_pre', 'w_up', 'conv_w', 'conv_b', 'w_down', 'norm_ffn_post']


_KINDS = ["rows", "rows", "rows", "rows", "cols", "rows"]


def kernel(x, norm_mix_pre, w_in, b_ml_i, b_ml_f, ml_head_norm, b_fox_f, b_gate_a, b_gate_b, w_branch_a, w_branch_b, w_out, norm_mix_post, norm_ffn_pre, w_up, conv_w, conv_b, w_down, norm_ffn_post, loss_target, m_norm_mix_pre, m_w_in, m_b_ml_i, m_b_ml_f, m_ml_head_norm, m_b_fox_f, m_b_gate_a, m_b_gate_b, m_w_branch_a, m_w_branch_b, m_w_out, m_norm_mix_post, m_norm_ffn_pre, m_w_up, m_conv_w, m_conv_b, m_w_down, m_norm_ffn_post, v_norm_mix_pre, v_w_in, v_b_ml_i, v_b_ml_f, v_ml_head_norm, v_b_fox_f, v_b_gate_a, v_b_gate_b, v_w_branch_a, v_w_branch_b, v_w_out, v_norm_mix_post, v_norm_ffn_pre, v_w_up, v_conv_w, v_conv_b, v_w_down, v_norm_ffn_post):
    args = dict(locals())
    w = {n: args[n] for n in _WEIGHTS}
    mom = {n: args["m_" + n] for n in _WEIGHTS}
    var = {n: args["v_" + n] for n in _WEIGHTS}
    cx, cy, cc = lax.axis_index("x"), lax.axis_index("y"), lax.axis_index("c")
    kme = 2 * cx + cy
    cvec = jnp.reshape(cc, (1,)).astype(jnp.int32)
    kcvec = jnp.stack([kme, cc]).astype(jnp.int32)
    odd = kme % 2

    tr3 = lambda t: jnp.transpose(t, (0, 2, 1))
    w["w_in"], mom["w_in"], var["w_in"] = tr3(w_in), tr3(m_w_in), tr3(v_w_in)
    w_in_main = lax.dynamic_slice_in_dim(w["w_in"][0], 4 * odd, 2048, axis=0).astype(BF16)
    w_in_gates = lax.dynamic_slice_in_dim(w["w_in"][0], 2048 * (1 - odd), 4, axis=0).astype(BF16)
    (wmain_t,), (g_cw, g_gates) = _gather_weights([w_in_main], _KINDS[:1], [w["conv_w"][0], w_in_gates])
    rest_started = _gather_start([w[n][0].astype(BF16) for n in _BIG[1:]], _KINDS[1:], "gather_rest_start")

    relay = {}

    def rest_arrived(after):
        bufs = _gather_wait(rest_started, after, _KINDS[1:], "gather_rest_wait")
        relay["started"] = _inplace_start(_relay_copies, 3, bufs, _KINDS[1:], "gather_rest_relay_start")
        return relay["started"][-1]

    def rest_weights(after):
        g_a, g_b, g_out, wup, g_down = _inplace_wait(_relay_copies, relay["started"], after, _KINDS[1:],
                                                     "gather_rest_relay_wait")
        return full(g_a), full(g_b), full(g_out), wup, full(g_down)
    gate_rows = g_gates.reshape(16, D_MODEL)
    wsmall_t = jnp.zeros((N_SMALL, D_MODEL), BF16)
    for blk, (lo, hi) in enumerate(((0, 4), (4, 8), (8, 16))):
        wsmall_t = wsmall_t.at[blk * LANES:blk * LANES + hi - lo].set(gate_rows[lo:hi])
    full = lambda g: g.reshape(-1, g.shape[2])
    p = {n: w[n] for n in _SMALL}
    p["conv_w"] = jnp.transpose(g_cw, (1, 0, 2)).reshape(3, -1)

    groups = {}

    def on_grads(group, gs):
        names = list(gs)
        kinds = [_KINDS[_BIG.index(n)] for n in names]
        whole = [g if k == "cols" else g.reshape(4, -1, g.shape[1]) for g, k in zip(gs.values(), kinds)]
        started = _exchange_start(_sibling_copies, 1, whole, [_half_shape(g, k) for g, k in zip(whole, kinds)], kinds,
                                  "grads_to_sibling_start_" + group)
        groups[group] = dict(names=names, kinds=kinds, sibling=started)
        return started[-1]

    def advance(group, after):
        G = groups[group]
        whole, got = _exchange_wait(_sibling_copies, G["sibling"], after, G["kinds"], "grads_to_sibling_wait_" + group)
        sums = [_add_halves(g, r, cvec, k, "add_sibling_" + n) for g, r, k, n in zip(whole, got, G["kinds"], G["names"])]
        G["chips"] = _exchange_start(_chip_copies, 3, sums, [_land_shape(s, k) for s, k in zip(sums, G["kinds"])],
                                     G["kinds"], "grads_to_chips_start_" + group)
        return G["chips"][-1]

    loss_row, grad_x, big, small = _local_step(x[0], loss_target[0], full(wmain_t), wsmall_t, rest_arrived, rest_weights,
                                               p, on_grads, advance, rest_started[-1])
    gt = big["wsmall_t"]
    small["w_in_gates"] = jnp.concatenate([gt[0:4], gt[LANES:LANES + 4], gt[2 * LANES:2 * LANES + 8]], axis=0)
    small_names = _SMALL + ["conv_w"]
    packed_names = small_names + ["w_in_gates"]
    pack = _pack_rows([small[n] for n in packed_names] + [loss_row])
    spread = _exchange_start(_spread_copies, N_DEV - 1, [pack], [(N_DEV,) + pack.shape], None, "allreduce_start", zeroed=True)

    def my_half(group, after):
        G = groups[group]
        sums, got = _exchange_wait(_chip_copies, G["chips"], after, G["kinds"], "grads_to_chips_wait_" + group)
        return [_add_chips(s, r, kcvec, k, "add_chips_" + n) for s, r, k, n in zip(sums, got, G["kinds"], G["names"])]

    first_names = groups["ffn"]["names"] + groups["mix"]["names"]
    join_first = _inplace_start(_join_copies, 1, my_half("ffn", spread[-1]) + my_half("mix", spread[-1]), None,
                                "grads_join_start")
    join_in = _inplace_start(_join_copies, 1, my_half("in", join_first[-1]), None, "grads_join_start_in")
    grads = dict(zip(first_names, _inplace_wait(_join_copies, join_first, join_in[-1], None, "grads_join_wait")))

    delta, new_m, new_v = {}, {}, {}
    for n in _BIG[1:]:
        delta[n], new_m[n], new_v[n] = _adamw(w[n], grads[n], mom[n], var[n], "adamw_" + n)
        grads[n] = grads[n][None]
    grads["w_in"], = _inplace_wait(_join_copies, join_in, delta[_BIG[-1]], None, "grads_join_wait_in")

    (pack,), (gathered,) = _exchange_wait(_spread_copies, spread, delta[_BIG[-1]], None, "allreduce_wait")
    full_shapes = [small[n].shape if n in ("conv_w", "w_in_gates") else w[n][0].shape for n in packed_names]
    total = _unpack_rows(_sum_spread(pack, gathered), full_shapes + [loss_row.shape])
    for n, t in zip(packed_names, total):
        grads[n] = t
    loss = total[-1][0, 0]
    grads["conv_w"] = lax.dynamic_slice_in_dim(grads["conv_w"], kme * conv_w.shape[2], conv_w.shape[2], axis=1)
    my_gates = lax.dynamic_slice_in_dim(grads.pop("w_in_gates"), 4 * kme, 4, axis=0)
    g_in = jnp.zeros(w["w_in"].shape[1:], F32)
    g_in = lax.dynamic_update_slice_in_dim(g_in, grads["w_in"], 4 * odd, axis=0)
    grads["w_in"] = lax.dynamic_update_slice_in_dim(g_in, my_gates, 2048 * (1 - odd), axis=0)
    delta["w_in"], new_m["w_in"], new_v["w_in"] = _adamw(w["w_in"], grads["w_in"], mom["w_in"], var["w_in"], "adamw_w_in")
    grads["w_in"] = grads["w_in"][None]
    for d in (grads, delta, new_m, new_v):
        d["w_in"] = tr3(d["w_in"])
    packs = [_pack_rows([d[n][0] for n in small_names]) for d in (w, mom, var)]
    pad = ((0, (-packs[0].shape[0]) % 8), (0, 0))
    packs = [jnp.pad(t, pad)[None] for t in packs]
    gp = jnp.pad(_pack_rows([grads[n] for n in small_names]), pad)
    shapes = [w[n][0].shape for n in small_names]
    for dst, res in zip((delta, new_m, new_v), _adamw(packs[0], gp, packs[1], packs[2], "adamw_small")):
        for n, t in zip(small_names, _unpack_rows(res[0], shapes)):
            dst[n] = t[None]
    for n in small_names:
        grads[n] = grads[n][None]

    return (loss, grad_x[None], *[grads[n] for n in _WEIGHTS], *[delta[n] for n in _WEIGHTS],
            *[new_m[n] for n in _WEIGHTS], *[new_v[n] for n in _WEIGHTS])
```

```python
import functools
import math

import jax
import jax.numpy as jnp
from jax import lax
from jax.experimental import pallas as pl
from jax.experimental.pallas import tpu as pltpu

F32 = jnp.float32
BF16 = jnp.bfloat16
MESH = pl.DeviceIdType.MESH

D_MODEL = 1024
ML_HEADS = 4
ML_DQK = 128
ML_DV = 256
FOX_HEADS = 8
FOX_DH = 128
D_FF = 2816
GATE_CAP = 15.0
EPS = 1e-6
ADAM_LR, ADAM_B1, ADAM_B2, ADAM_EPS, ADAM_WD, ADAM_STEP = 0.001, 0.9, 0.999, 1e-08, 0.01, 10

LANES = 128
MLC = 256
FOX_TQ = 512
FOX_TQ_FWD = 512
FOX_TK = 512
FOX_TK_FWD = 512
ROW_T = 512
CONV_TC = 1408
VMEM_LIMIT = 56 * 1024 * 1024

C_QM, C_KM, C_VM, C_OM = 0, 512, 1024, 2048
N_ML, N_FOX, N_GATE = 3072, 3072, 2048
N_SMALL = 384


def _cparams(sem=None):
    return pltpu.CompilerParams(dimension_semantics=sem, vmem_limit_bytes=VMEM_LIMIT)


def _tile(n, target):
    if n <= target:
        return n
    best = None
    for t in range(LANES, target + 1, LANES):
        if n % t == 0:
            best = t
    assert best is not None, (n, target)
    return best


def _dot(a, b, dims):
    return lax.dot_general(a, b, (dims, ((), ())), preferred_element_type=F32)


def _dot_nn(a, b):
    return _dot(a, b, ((1,), (0,)))


def _dot_nt(a, b):
    return _dot(a, b, ((1,), (1,)))


def _dot_tn(a, b):
    return _dot(a, b, ((0,), (0,)))


_DOTS = {"nn": _dot_nn, "nt": _dot_nt, "tn": _dot_tn}


def _mm(a, b, mode, out_dtype, name, tm=1024, tn=1408, tk=1408, after=None, b_rows=None):
    a_parts = list(a) if isinstance(a, (list, tuple)) else [a]
    b_parts = list(b) if isinstance(b, (list, tuple)) else [b]
    extra = [] if after is None else [after]
    assert len(a_parts) == 1 or len(b_parts) == 1, name
    a_axes = {"nn": "ik", "nt": "ik", "tn": "ki"}[mode]
    b_axes = {"nn": "kj", "nt": "jk", "tn": "kj"}[mode]
    size, target = {}, dict(i=tm, j=tn, k=tk)
    for parts, axes in ((a_parts, a_axes), (b_parts, b_axes)):
        dims = (parts[0].shape[0], parts[0].shape[1] * len(parts))
        if parts is b_parts and b_rows is not None:
            dims = (b_rows[1], dims[1])
        for ax, n in zip(axes, dims):
            assert size.setdefault(ax, n) == n, (name, ax, n, size)
    tile = {}
    for parts, axes in ((a_parts, a_axes), (b_parts, b_axes)):
        if len(parts) > 1:
            tile[axes[1]] = _tile(parts[0].shape[1], target[axes[1]])
    for ax in "ijk":
        tile.setdefault(ax, _tile(size[ax], target[ax]))
    M, N, nk = size["i"], size["j"], size["k"] // tile["k"]
    grid_pos = dict(i=0, j=1, k=2)
    dot = _DOTS[mode]

    def specs(parts, axes):
        blk = (tile[axes[0]], tile[axes[1]])
        if len(parts) == 1:
            first = 0
            if parts is b_parts and b_rows is not None:
                assert b_rows[0] % blk[0] == 0, (name, b_rows, blk)
                first = b_rows[0] // blk[0]
            return [pl.BlockSpec(blk, lambda *g: (first + g[grid_pos[axes[0]]], g[grid_pos[axes[1]]]))], None
        bpp = parts[0].shape[1] // blk[1]

        def index(p):
            def f(*g):
                g0, g1 = g[grid_pos[axes[0]]], g[grid_pos[axes[1]]]
                on = g1 // bpp == p
                return jnp.where(on, g0, 0), jnp.where(on, g1 % bpp, 0)
            return f

        return [pl.BlockSpec(blk, index(p)) for p in range(len(parts))], (axes[1], bpp)

    a_specs, a_sel = specs(a_parts, a_axes)
    b_specs, b_sel = specs(b_parts, b_axes)
    na, nb = len(a_parts), len(b_parts)

    def body(*refs):
        a_refs, b_refs = refs[:na], refs[na:na + nb]
        o_ref, acc = refs[na + nb + len(extra)], refs[na + nb + len(extra) + 1:]

        def accumulate(part):
            if nk == 1:
                o_ref[...] = part.astype(o_ref.dtype)
                return
            acc_ref, = acc
            k = pl.program_id(2)

            @pl.when(k == 0)
            def _():
                acc_ref[...] = part

            @pl.when(k > 0)
            def _():
                acc_ref[...] += part

            @pl.when(k == nk - 1)
            def _():
                o_ref[...] = acc_ref[...].astype(o_ref.dtype)

        sel = a_sel or b_sel
        if sel is None:
            accumulate(dot(a_refs[0][...], b_refs[0][...]))
        else:
            which = pl.program_id(grid_pos[sel[0]]) // sel[1]
            for p in range(max(na, nb)):
                @pl.when(which == p)
                def _(p=p):
                    accumulate(dot(a_refs[p if a_sel else 0][...], b_refs[p if b_sel else 0][...]))

    return pl.pallas_call(
        body, name=name,
        out_shape=jax.ShapeDtypeStruct((M, N), out_dtype),
        grid=(M // tile["i"], N // tile["j"], nk),
        in_specs=a_specs + b_specs + [pl.BlockSpec(memory_space=pl.ANY)] * len(extra),
        out_specs=pl.BlockSpec((tile["i"], tile["j"]), lambda i, j, k: (i, j)),
        scratch_shapes=[pltpu.VMEM((tile["i"], tile["j"]), F32)] if nk > 1 else [],
        compiler_params=_cparams(("parallel", "parallel", "arbitrary")),
    )(*a_parts, *b_parts, *extra)


def _mm_sum_parts(parts, b, out_dtype, name, trans_b=False, tm=1024, tn=512, after=None):
    M, K = parts[0].shape
    N = b.shape[0] if trans_b else b.shape[1]
    tm, tn = _tile(M, tm), _tile(N, tn)
    n = len(parts)
    extra = [] if after is None else [after]

    def body(*refs):
        b_ref, o_ref = refs[n], refs[n + 1 + len(extra)]
        acc = None
        for p in range(n):
            if trans_b:
                d = _dot_nt(refs[p][...], b_ref[:, p * K:(p + 1) * K])
            else:
                d = _dot_nn(refs[p][...], b_ref[p * K:(p + 1) * K, :])
            acc = d if acc is None else acc + d
        o_ref[...] = acc.astype(o_ref.dtype)

    b_spec = pl.BlockSpec((tn, n * K), lambda i, j: (j, 0)) if trans_b else pl.BlockSpec((n * K, tn), lambda i, j: (0, j))
    return pl.pallas_call(
        body, name=name, out_shape=jax.ShapeDtypeStruct((M, N), out_dtype), grid=(M // tm, N // tn),
        in_specs=[pl.BlockSpec((tm, K), lambda i, j: (i, 0))] * n + [b_spec]
        + [pl.BlockSpec(memory_space=pl.ANY)] * len(extra),
        out_specs=pl.BlockSpec((tm, tn), lambda i, j: (i, j)),
        compiler_params=_cparams(("parallel", "arbitrary")),
    )(*parts, b, *extra)


def _rstd(x):
    return lax.rsqrt(jnp.mean(x * x, axis=-1, keepdims=True) + EPS)


def _rmsnorm_fwd(x, g, name):
    S, D = x.shape
    T = _tile(S, ROW_T)

    def body(x_ref, g_ref, o_ref):
        xv = x_ref[...]
        o_ref[...] = (xv * _rstd(xv) * g_ref[...]).astype(o_ref.dtype)

    return pl.pallas_call(
        body, name=name, out_shape=jax.ShapeDtypeStruct((S, D), BF16), grid=(S // T,),
        in_specs=[pl.BlockSpec((T, D), lambda i: (i, 0)), pl.BlockSpec((1, D), lambda i: (0, 0))],
        out_specs=pl.BlockSpec((T, D), lambda i: (i, 0)),
        compiler_params=_cparams(("parallel",)),
    )(x, g)


def _resid_norm_fwd(x, z, g, g_next, name):
    S, D = x.shape
    T = _tile(S, ROW_T)

    def body(x_ref, z_ref, g_ref, gn_ref, o_ref, h_ref):
        zv = z_ref[...]
        x1 = x_ref[...] + zv * _rstd(zv) * g_ref[...]
        o_ref[...] = x1
        h_ref[...] = (x1 * _rstd(x1) * gn_ref[...]).astype(h_ref.dtype)

    row = pl.BlockSpec((T, D), lambda i: (i, 0))
    vec = pl.BlockSpec((1, D), lambda i: (0, 0))
    return pl.pallas_call(
        body, name=name, out_shape=(jax.ShapeDtypeStruct((S, D), F32), jax.ShapeDtypeStruct((S, D), BF16)),
        grid=(S // T,), in_specs=[row, row, vec, vec], out_specs=(row, row), compiler_params=_cparams(("parallel",)),
    )(x, z, g, g_next)


def _norm_chain_bwd(dh, xin, g, resid, zin, gz, name):
    S, D = xin.shape
    T = _tile(S, ROW_T)

    def body(dh_ref, x_ref, g_ref, r_ref, z_ref, gz_ref, dx_ref, dz_ref, dg_ref, dgz_ref):
        dx, dgt = _rmsnorm_bwd_math(dh_ref[...], x_ref[...], g_ref[...])
        dx = dx + r_ref[...]
        dx_ref[...] = dx
        dz, dgzt = _rmsnorm_bwd_math(dx, z_ref[...], gz_ref[...])
        dz_ref[...] = dz.astype(dz_ref.dtype)

        @pl.when(pl.program_id(0) == 0)
        def _():
            dg_ref[...] = jnp.zeros_like(dg_ref)
            dgz_ref[...] = jnp.zeros_like(dgz_ref)

        dg_ref[...] += jnp.sum(dgt, axis=0, keepdims=True)
        dgz_ref[...] += jnp.sum(dgzt, axis=0, keepdims=True)

    row = pl.BlockSpec((T, D), lambda i: (i, 0))
    vec = pl.BlockSpec((1, D), lambda i: (0, 0))
    v1 = jax.ShapeDtypeStruct((1, D), F32)
    return pl.pallas_call(
        body, name=name,
        out_shape=(jax.ShapeDtypeStruct((S, D), F32), jax.ShapeDtypeStruct((S, D), BF16), v1, v1),
        grid=(S // T,), in_specs=[row, row, vec, row, row, vec], out_specs=(row, row, vec, vec),
        compiler_params=_cparams(("arbitrary",)),
    )(dh, xin, g, resid, zin, gz)


def _rmsnorm_bwd_math(dy, xv, g):
    r = _rstd(xv)
    u = dy * g
    dx = r * u - xv * (r * r * r) * jnp.mean(u * xv, axis=-1, keepdims=True)
    return dx, dy * xv * r


def _rmsnorm_bwd(dys, xin, g, resid, out_dtype, name):
    S, D = xin.shape
    T = _tile(S, ROW_T)
    has_resid = resid is not None
    ndy = len(dys)

    def body(*refs):
        dy_refs, (x_ref, g_ref) = refs[:ndy], refs[ndy:ndy + 2]
        dx_ref, dg_ref = refs[-2:]
        dy = dy_refs[0][...]
        for r in dy_refs[1:]:
            dy = dy + r[...]
        dx, dgt = _rmsnorm_bwd_math(dy, x_ref[...], g_ref[...])
        if has_resid:
            dx = dx + refs[ndy + 2][...]
        dx_ref[...] = dx.astype(dx_ref.dtype)

        @pl.when(pl.program_id(0) == 0)
        def _():
            dg_ref[...] = jnp.zeros_like(dg_ref)

        dg_ref[...] += jnp.sum(dgt, axis=0, keepdims=True)

    row = pl.BlockSpec((T, D), lambda i: (i, 0))
    vec = pl.BlockSpec((1, D), lambda i: (0, 0))
    ins = list(dys) + [xin, g] + ([resid] if has_resid else [])
    return pl.pallas_call(
        body, name=name,
        out_shape=(jax.ShapeDtypeStruct((S, D), out_dtype), jax.ShapeDtypeStruct((1, D), F32)),
        grid=(S // T,), in_specs=[row] * ndy + [row, vec] + ([row] if has_resid else []),
        out_specs=(row, vec), compiler_params=_cparams(("arbitrary",)),
    )(*ins)


def _loss_head(x1, d, g, target, name):
    S, D = x1.shape
    T = _tile(S, ROW_T)

    def body(x_ref, d_ref, g_ref, t_ref, loss_ref, dy_ref, dd_ref, dg_ref):
        dv, gv = d_ref[...], g_ref[...]
        y = x_ref[...] + dv * _rstd(dv) * gv
        diff = y - t_ref[...]
        dy = diff * (1.0 / D)
        dy_ref[...] = dy
        dd, dgt = _rmsnorm_bwd_math(dy, dv, gv)
        dd_ref[...] = dd.astype(dd_ref.dtype)

        @pl.when(pl.program_id(0) == 0)
        def _():
            dg_ref[...] = jnp.zeros_like(dg_ref)
            loss_ref[...] = jnp.zeros_like(loss_ref)

        dg_ref[...] += jnp.sum(dgt, axis=0, keepdims=True)
        part = jnp.sum(jnp.sum(diff * diff, axis=1, keepdims=True), axis=0, keepdims=True)
        loss_ref[...] += (0.5 / D) * part

    row = pl.BlockSpec((T, D), lambda i: (i, 0))
    vec = pl.BlockSpec((1, D), lambda i: (0, 0))
    return pl.pallas_call(
        body, name=name,
        out_shape=(jax.ShapeDtypeStruct((1, LANES), F32), jax.ShapeDtypeStruct((S, D), F32),
                   jax.ShapeDtypeStruct((S, D), BF16), jax.ShapeDtypeStruct((1, D), F32)),
        grid=(S // T,), in_specs=[row, row, vec, row],
        out_specs=(pl.BlockSpec((1, LANES), lambda i: (0, 0)), row, row, vec),
        compiler_params=_cparams(("arbitrary",)),
    )(x1, d, g, target)


def _merge_fwd(ya, yb, pm, ba, bb, name):
    S, D = ya.shape
    T = _tile(S, ROW_T)

    def body(ya_ref, yb_ref, ga_ref, gb_ref, ba_ref, bb_ref, o_ref):
        sa = jax.nn.sigmoid(ga_ref[...] + ba_ref[...])
        sb = jax.nn.sigmoid(gb_ref[...] + bb_ref[...])
        o_ref[...] = (sa * ya_ref[...] + sb * yb_ref[...]).astype(o_ref.dtype)

    row = pl.BlockSpec((T, D), lambda i: (i, 0))
    vec = pl.BlockSpec((1, D), lambda i: (0, 0))
    return pl.pallas_call(
        body, name=name, out_shape=jax.ShapeDtypeStruct((S, D), BF16), grid=(S // T,),
        in_specs=[row, row, pl.BlockSpec((T, D), lambda i: (i, 0)),
                  pl.BlockSpec((T, D), lambda i: (i, 1)), vec, vec],
        out_specs=row, compiler_params=_cparams(("parallel",)),
    )(ya, yb, pm, pm, ba, bb)


def _merge_bwd(dmerged, ya, yb, pm, ba, bb, name):
    S, D = ya.shape
    T = _tile(S, ROW_T)

    def body(dm_ref, ya_ref, yb_ref, ga_ref, gb_ref, ba_ref, bb_ref,
             dya_ref, dyb_ref, dga_ref, dgb_ref, dba_ref, dbb_ref):
        dm = dm_ref[...]
        sa = jax.nn.sigmoid(ga_ref[...] + ba_ref[...])
        sb = jax.nn.sigmoid(gb_ref[...] + bb_ref[...])
        dya_ref[...] = (dm * sa).astype(dya_ref.dtype)
        dyb_ref[...] = (dm * sb).astype(dyb_ref.dtype)
        dga = dm * ya_ref[...] * sa * (1.0 - sa)
        dgb = dm * yb_ref[...] * sb * (1.0 - sb)
        dga_ref[...] = dga.astype(dga_ref.dtype)
        dgb_ref[...] = dgb.astype(dgb_ref.dtype)

        @pl.when(pl.program_id(0) == 0)
        def _():
            dba_ref[...] = jnp.zeros_like(dba_ref)
            dbb_ref[...] = jnp.zeros_like(dbb_ref)

        dba_ref[...] += jnp.sum(dga, axis=0, keepdims=True)
        dbb_ref[...] += jnp.sum(dgb, axis=0, keepdims=True)

    row = pl.BlockSpec((T, D), lambda i: (i, 0))
    vec = pl.BlockSpec((1, D), lambda i: (0, 0))
    act = jax.ShapeDtypeStruct((S, D), BF16)
    v1 = jax.ShapeDtypeStruct((1, D), F32)
    return pl.pallas_call(
        body, name=name, out_shape=(act, act, act, act, v1, v1), grid=(S // T,),
        in_specs=[row, row, row, pl.BlockSpec((T, D), lambda i: (i, 0)),
                  pl.BlockSpec((T, D), lambda i: (i, 1)), vec, vec],
        out_specs=(row, row, row, row, vec, vec), compiler_params=_cparams(("arbitrary",)),
    )(dmerged, ya, yb, pm, pm, ba, bb)


_GELU_C = math.sqrt(2.0 / math.pi)


_GELU_K = 0.044715


def _gelu(g):
    u = 0.5 * jnp.tanh(g * (_GELU_C + (_GELU_C * _GELU_K) * (g * g))) + 0.5
    return g * u, u


def _gelu_grad(g, u):
    return u * (1.0 + g * (1.0 - u) * (2 * _GELU_C + (6 * _GELU_C * _GELU_K) * (g * g)))


def _shift_down(v, halo_ref, first, rows):
    T = v.shape[0]
    keep = jnp.where(first, 0.0, 1.0)
    h7 = halo_ref[7:8, :] * keep
    h6 = halo_ref[6:7, :] * keep
    m1 = jnp.where(rows == 0, h7, pltpu.roll(v, 1, 0))
    m2 = jnp.where(rows == 0, h6, jnp.where(rows == 1, h7, pltpu.roll(v, 2, 0)))
    return m1, m2


def _conv_act_fwd(up, cw, cb, name):
    S, F2 = up.shape
    Fh = F2 // 2
    T = _tile(S, ROW_T)
    tc = _tile(Fh, CONV_TC)
    ncol = Fh // tc
    hb = T // 8

    def body(ua_ref, ug_ref, ha_ref, hg_ref, wa_ref, wg_ref, ba_ref, bg_ref, o_ref, a_ref, g_ref):
        first = pl.program_id(0) == 0
        rows = lax.broadcasted_iota(jnp.int32, (T, tc), 0)

        def conv(u_ref, h_ref, w_ref, b_ref):
            v = u_ref[...]
            m1, m2 = _shift_down(v, h_ref, first, rows)
            return b_ref[...] + w_ref[0:1, :] * m2 + w_ref[1:2, :] * m1 + w_ref[2:3, :] * v

        a = conv(ua_ref, ha_ref, wa_ref, ba_ref)
        g = conv(ug_ref, hg_ref, wg_ref, bg_ref)
        a_ref[...] = a
        g_ref[...] = g
        o_ref[...] = (_gelu(g)[0] * a).astype(o_ref.dtype)

    halo = lambda off: pl.BlockSpec((8, tc), lambda i, j: (jnp.maximum(i * hb - 1, 0), j + off))
    blk = pl.BlockSpec((T, tc), lambda i, j: (i, j))
    f32 = jax.ShapeDtypeStruct((S, Fh), F32)
    return pl.pallas_call(
        body, name=name, out_shape=(jax.ShapeDtypeStruct((S, Fh), BF16), f32, f32), grid=(S // T, ncol),
        in_specs=[blk, pl.BlockSpec((T, tc), lambda i, j: (i, j + ncol)),
                  halo(0), halo(ncol),
                  pl.BlockSpec((3, tc), lambda i, j: (0, j)), pl.BlockSpec((3, tc), lambda i, j: (0, j + ncol)),
                  pl.BlockSpec((1, tc), lambda i, j: (0, j)), pl.BlockSpec((1, tc), lambda i, j: (0, j + ncol))],
        out_specs=(blk, blk, blk),
        compiler_params=_cparams(("parallel", "parallel")),
    )(up, up, up, up, cw, cw, cb, cb)


def _conv_act_bwd(up, a, g, dact, cw, name):
    S, F2 = up.shape
    Fh = F2 // 2
    T = _tile(S, ROW_T // 2)
    tc = _tile(Fh, CONV_TC)
    ncol, nrow, hb, nhb = Fh // tc, S // T, T // 8, S // 8

    def body(ua_ref, ug_ref, a_ref, g_ref, an_ref, gn_ref, wa_ref, wg_ref, da_ref, dn_ref,
             dpa_ref, dpg_ref, dwa_ref, dwg_ref, dba_ref, dbg_ref, dua_n, dug_n):
        i = pl.program_id(1)
        rows = lax.broadcasted_iota(jnp.int32, (T, tc), 0)

        def du_of(a, g, dact_v):
            gel, t = _gelu(g)
            return dact_v * gel, dact_v * a * _gelu_grad(g, t)

        dua, dug = du_of(a_ref[...], g_ref[...], da_ref[...])
        keep = jnp.where(i == nrow - 1, 0.0, 1.0)
        dua_n[...], dug_n[...] = du_of(an_ref[...], gn_ref[...], dn_ref[...] * keep)

        @pl.when(i == 0)
        def _():
            for r in (dwa_ref, dwg_ref, dba_ref, dbg_ref):
                r[...] = jnp.zeros_like(r)

        for du, n_ref, u_ref, w_ref, o_ref, dw_ref, db_ref in ((dua, dua_n, ua_ref, wa_ref, dpa_ref, dwa_ref, dba_ref),
                                                               (dug, dug_n, ug_ref, wg_ref, dpg_ref, dwg_ref, dbg_ref)):
            n0, n1 = n_ref[0:1, :], n_ref[1:2, :]
            du1 = jnp.where(rows == T - 1, n0, pltpu.roll(du, T - 1, 0))
            du2 = jnp.where(rows == T - 2, n0, jnp.where(rows == T - 1, n1, pltpu.roll(du, T - 2, 0)))
            o_ref[...] = (w_ref[2:3, :] * du + w_ref[1:2, :] * du1 + w_ref[0:1, :] * du2).astype(o_ref.dtype)
            u = u_ref[...]
            db_ref[...] += jnp.sum(du, axis=0, keepdims=True)
            for j, d in enumerate((du2, du1, du)):
                dw_ref[j:j + 1, :] += jnp.sum(d * u, axis=0, keepdims=True)

    tile = lambda off: pl.BlockSpec((T, tc), lambda j, i: (i, j + off))
    under = pl.BlockSpec((8, tc), lambda j, i: (jnp.minimum((i + 1) * hb, nhb - 1), j))
    vec = lambda n, off: pl.BlockSpec((n, tc), lambda j, i: (0, j + off))
    act = jax.ShapeDtypeStruct((S, Fh), BF16)
    return pl.pallas_call(
        body, name=name,
        out_shape=(act, act, jax.ShapeDtypeStruct((3, Fh), F32), jax.ShapeDtypeStruct((3, Fh), F32),
                   jax.ShapeDtypeStruct((1, Fh), F32), jax.ShapeDtypeStruct((1, Fh), F32)),
        grid=(ncol, nrow),
        in_specs=[tile(0), tile(ncol), tile(0), tile(0), under, under, vec(3, 0), vec(3, ncol), tile(0), under],
        out_specs=(tile(0), tile(0), vec(3, 0), vec(3, 0), vec(1, 0), vec(1, 0)),
        scratch_shapes=[pltpu.VMEM((8, tc), F32), pltpu.VMEM((8, tc), F32)],
        compiler_params=_cparams(("parallel", "arbitrary")),
    )(up, up, a, g, a, g, cw, cw, dact, dact)


def _split3(x):
    hi = x.astype(BF16)
    r1 = x - hi.astype(F32)
    mid = r1.astype(BF16)
    lo = (r1 - mid.astype(F32)).astype(BF16)
    return hi, mid, lo


def _tri_dot(tri, x):
    hi, mid, lo = _split3(x)
    return _dot_nn(tri, hi) + _dot_nn(tri, mid) + _dot_nn(tri, lo)


def _log_sigmoid(x):
    return jnp.minimum(x, 0.0) - jnp.log(1.0 + jnp.exp(-jnp.abs(x)))


def _tri_mask(n, lower):
    r = lax.broadcasted_iota(jnp.int32, (n, n), 0)
    c = lax.broadcasted_iota(jnp.int32, (n, n), 1)
    return (r >= c) if lower else (r <= c)


def _gates_fwd(ps, bi, bf, bff, name):
    S = ps.shape[0]
    NC = S // MLC

    def body(ps_ref, bi_ref, bf_ref, bff_ref, a_ref, A_ref, wi_ref, em_ref, wk_ref, dec_ref, F_ref, m_scr, f_scr):
        @pl.when(pl.program_id(0) == 0)
        def _():
            m_scr[...] = jnp.zeros_like(m_scr)
            f_scr[...] = jnp.zeros_like(f_scr)

        rows = lax.broadcasted_iota(jnp.int32, (MLC, LANES), 0)
        ltri = _tri_mask(MLC, True).astype(BF16)
        li = GATE_CAP * jnp.tanh((ps_ref[:, 0:LANES] + bi_ref[...]) / GATE_CAP)
        lf = _log_sigmoid(GATE_CAP * jnp.tanh((ps_ref[:, LANES:2 * LANES] + bf_ref[...]) / GATE_CAP))
        b = _tri_dot(ltri, lf)
        a = li - b
        cm = a
        sh = 1
        while sh < MLC:
            cm = jnp.where(rows >= sh, jnp.maximum(cm, pltpu.roll(cm, sh, 0)), cm)
            sh *= 2
        m0 = m_scr[...]
        A = jnp.maximum(cm, m0)
        a_ref[...] = a
        A_ref[...] = A
        A_last = A_ref[MLC - 1:MLC, :]
        wi_ref[...] = jnp.exp(m0 - A)
        em_ref[...] = jnp.exp(-(b + A))
        wk_ref[...] = jnp.exp(a - A_last)
        dec_ref[0] = jnp.exp(m0 - A_last)
        F_ref[...] = b
        m_scr[...] = F_ref[MLC - 1:MLC, :] + A_last
        lfg = _log_sigmoid(ps_ref[:, 2 * LANES:3 * LANES] + bff_ref[...])
        F_ref[...] = _tri_dot(ltri, lfg) + f_scr[...]
        f_scr[...] = F_ref[MLC - 1:MLC, :]

    col = pl.BlockSpec((MLC, LANES), lambda c: (c, 0))
    vec = pl.BlockSpec((1, LANES), lambda c: (0, 0))
    cs = jax.ShapeDtypeStruct((S, LANES), F32)
    return pl.pallas_call(
        body, name=name,
        out_shape=(cs, cs, cs, cs, cs, jax.ShapeDtypeStruct((NC, 1, LANES), F32), cs),
        grid=(NC,), in_specs=[pl.BlockSpec((MLC, N_SMALL), lambda c: (c, 0)), vec, vec, vec],
        out_specs=(col, col, col, col, col, pl.BlockSpec((1, 1, LANES), lambda c: (c, 0, 0)), col),
        scratch_shapes=[pltpu.VMEM((1, LANES), F32), pltpu.VMEM((1, LANES), F32)],
        compiler_params=_cparams(("arbitrary",)),
    )(ps, bi, bf, bff)


def _gates_bwd(ps, bi, bf, bff, rk, kc, tch, dF, name):
    S = ps.shape[0]
    NC = S // MLC

    def body(ps_ref, bi_ref, bf_ref, bff_ref, rk_ref, kc_ref, t_ref, dF_ref, dps_ref, db_ref, carry):
        @pl.when(pl.program_id(0) == 0)
        def _():
            carry[...] = jnp.zeros_like(carry)
            db_ref[...] = jnp.zeros_like(db_ref)

        lanes = lax.broadcasted_iota(jnp.int32, (MLC, LANES), 1)
        utri = _tri_mask(MLC, False).astype(BF16)
        ti = jnp.tanh((ps_ref[:, 0:LANES] + bi_ref[...]) / GATE_CAP)
        t_end, t_start = t_ref[0, 0:1, :], t_ref[0, 1:2, :]
        rk = rk_ref[...]
        rk = rk - (jnp.sum(rk, axis=0, keepdims=True) - (t_start - t_end)) * (1.0 / MLC)
        dpi = jnp.where(lanes < ML_HEADS, (kc_ref[...] - rk) * (1.0 - ti * ti), 0.0)
        tf = jnp.tanh((ps_ref[:, LANES:2 * LANES] + bf_ref[...]) / GATE_CAP)
        dlf = _tri_dot(utri, rk) + t_end
        dpf = jnp.where(lanes < ML_HEADS, dlf * jax.nn.sigmoid(-GATE_CAP * tf) * (1.0 - tf * tf), 0.0)
        dFv = dF_ref[...]
        dlfg = _tri_dot(utri, dFv) + carry[...]
        carry[...] += jnp.sum(dFv, axis=0, keepdims=True)
        dpff = jnp.where(lanes < FOX_HEADS, dlfg * jax.nn.sigmoid(-(ps_ref[:, 2 * LANES:3 * LANES] + bff_ref[...])), 0.0)
        for n, dp in enumerate((dpi, dpf, dpff)):
            dps_ref[:, n * LANES:(n + 1) * LANES] = dp.astype(dps_ref.dtype)
            db_ref[:, n * LANES:(n + 1) * LANES] += jnp.sum(dp, axis=0, keepdims=True)

    rev = lambda c: (NC - 1 - c, 0)
    col = pl.BlockSpec((MLC, LANES), rev)
    vec = pl.BlockSpec((1, LANES), lambda c: (0, 0))
    wide = pl.BlockSpec((MLC, N_SMALL), rev)
    return pl.pallas_call(
        body, name=name,
        out_shape=(jax.ShapeDtypeStruct((S, N_SMALL), BF16), jax.ShapeDtypeStruct((1, N_SMALL), F32)),
        grid=(NC,),
        in_specs=[wide, vec, vec, vec, col, col, pl.BlockSpec((1, 2, LANES), lambda c: (NC - 1 - c, 0, 0)), col],
        out_specs=(wide, pl.BlockSpec((1, N_SMALL), lambda c: (0, 0))),
        scratch_shapes=[pltpu.VMEM((1, LANES), F32)],
        compiler_params=_cparams(("arbitrary",)),
    )(ps, bi, bf, bff, rk, kc, tch, dF)


_ML_SCALE = ML_DQK ** -0.5


def _ml_specs(rev, NC):
    idx = (lambda c: NC - 1 - c) if rev else (lambda c: c)
    qk = lambda blk: pl.BlockSpec((MLC, ML_HEADS * ML_DQK), lambda c: (idx(c), blk))
    wide = lambda blk: pl.BlockSpec((MLC, D_MODEL), lambda c: (idx(c), blk))
    col = pl.BlockSpec((MLC, LANES), lambda c: (idx(c), 0))
    return idx, qk, wide, col


def _ml_intra(q_ref, k_ref, arow_ref, A_ref, h):
    hs = slice(h * ML_DQK, (h + 1) * ML_DQK)
    qf = q_ref[:, hs] * _ML_SCALE
    kf = k_ref[:, hs]
    qb, kb = qf.astype(BF16), kf.astype(BF16)
    qk = _dot_nt(qb, kb)
    logw = arow_ref[h:h + 1, :] - A_ref[:, h:h + 1]
    W = jnp.exp(jnp.where(_tri_mask(MLC, True), logw, -1e30))
    return qb, kb, qf, kf, qk, W


def _mlstm_fwd(pm, a_row, A, wi, em, wk, dec, w_hn, name):
    S = pm.shape[0]
    NC = S // MLC
    _, qk, wide, col = _ml_specs(False, NC)

    def body(q_ref, k_ref, v_ref, o_ref, arow_ref, A_ref, wi_ref, em_ref, wk_ref, dec_ref, whn_ref,
             ha_ref, hp_ref, den_ref, cst_ref, nst_ref, C_scr, n_scr):
        @pl.when(pl.program_id(0) == 0)
        def _():
            C_scr[...] = jnp.zeros_like(C_scr)
            n_scr[...] = jnp.zeros_like(n_scr)

        lanes = lax.broadcasted_iota(jnp.int32, (MLC, LANES), 1)
        den_tile = jnp.zeros((MLC, LANES), F32)
        for h in range(ML_HEADS):
            vs = slice(h * ML_DV, (h + 1) * ML_DV)
            qb, kb, qf, kf, qk_, W = _ml_intra(q_ref, k_ref, arow_ref, A_ref, h)
            vb = v_ref[:, vs].astype(BF16)
            Cf = C_scr[h]
            Cb = Cf.astype(BF16)
            nrow = n_scr[h]
            cst_ref[0, h] = Cb
            nst_ref[0, h] = nrow
            s = qk_ * W
            wic = wi_ref[:, h:h + 1]
            num = _dot_nn(s.astype(BF16), vb) + wic * _dot_nt(qb, Cb)
            den = jnp.sum(s, axis=1, keepdims=True) + wic * jnp.sum(qf * nrow, axis=1, keepdims=True)
            hp = num / jnp.maximum(jnp.abs(den), em_ref[:, h:h + 1])
            hp_ref[:, vs] = hp
            den_tile = jnp.where(lanes == h, den, den_tile)
            hn = hp * _rstd(hp) * whn_ref[:, vs]
            ha_ref[:, vs] = (hn * jax.nn.sigmoid(o_ref[:, vs])).astype(ha_ref.dtype)
            wkc = wk_ref[:, h:h + 1]
            kw = kf * wkc
            d = dec_ref[0, :, h:h + 1]
            C_scr[h] = d * Cf + _dot_tn(vb, kw.astype(BF16))
            n_scr[h] = d * nrow + jnp.sum(kw, axis=0, keepdims=True)
        den_ref[...] = den_tile

    return pl.pallas_call(
        body, name=name,
        out_shape=(jax.ShapeDtypeStruct((S, D_MODEL), BF16), jax.ShapeDtypeStruct((S, D_MODEL), F32),
                   jax.ShapeDtypeStruct((S, LANES), F32),
                   jax.ShapeDtypeStruct((NC, ML_HEADS, ML_DV, ML_DQK), BF16),
                   jax.ShapeDtypeStruct((NC, ML_HEADS, 1, ML_DQK), F32)),
        grid=(NC,),
        in_specs=[qk(C_QM // 512), qk(C_KM // 512), wide(C_VM // D_MODEL), wide(C_OM // D_MODEL),
                  pl.BlockSpec((8, MLC), lambda c: (0, c)), col, col, col, col,
                  pl.BlockSpec((1, 1, LANES), lambda c: (c, 0, 0)), pl.BlockSpec((1, D_MODEL), lambda c: (0, 0))],
        out_specs=(pl.BlockSpec((MLC, D_MODEL), lambda c: (c, 0)), pl.BlockSpec((MLC, D_MODEL), lambda c: (c, 0)),
                   col, pl.BlockSpec((1, ML_HEADS, ML_DV, ML_DQK), lambda c: (c, 0, 0, 0)),
                   pl.BlockSpec((1, ML_HEADS, 1, ML_DQK), lambda c: (c, 0, 0, 0))),
        scratch_shapes=[pltpu.VMEM((ML_HEADS, ML_DV, ML_DQK), F32), pltpu.VMEM((ML_HEADS, 1, ML_DQK), F32)],
        compiler_params=_cparams(("arbitrary",)),
    )(pm, pm, pm, pm, a_row, A, wi, em, wk, dec, w_hn)


def _mlstm_bwd(dha, pm, hp_all, den_all, a_row, A, wi, em, wk, dec, cst, nst, w_hn, name):
    S = pm.shape[0]
    NC = S // MLC
    idx, qk, wide, col = _ml_specs(True, NC)

    def body(dha_ref, q_ref, k_ref, v_ref, o_ref, hp_ref, den_ref, arow_ref, A_ref, wi_ref, em_ref, wk_ref,
             dec_ref, cst_ref, nst_ref, whn_ref,
             dqk_ref, dv_ref, do_ref, rk_ref, kc_ref, t_ref, dwhn_ref, dC_scr, dn_scr, t_scr):
        @pl.when(pl.program_id(0) == 0)
        def _():
            dC_scr[...] = jnp.zeros_like(dC_scr)
            dn_scr[...] = jnp.zeros_like(dn_scr)
            t_scr[...] = jnp.zeros_like(t_scr)
            dwhn_ref[...] = jnp.zeros_like(dwhn_ref)

        lanes = lax.broadcasted_iota(jnp.int32, (MLC, LANES), 1)
        lane1 = lax.broadcasted_iota(jnp.int32, (1, LANES), 1)
        t_ref[0, 0:1, :] = t_scr[...]
        rk_tile = jnp.zeros((MLC, LANES), F32)
        kc_tile = jnp.zeros((MLC, LANES), F32)
        t_new = jnp.zeros((1, LANES), F32)
        for h in range(ML_HEADS):
            hs = slice(h * ML_DQK, (h + 1) * ML_DQK)
            vs = slice(h * ML_DV, (h + 1) * ML_DV)
            hp = hp_ref[:, vs]
            sig = jax.nn.sigmoid(o_ref[:, vs])
            whn = whn_ref[:, vs]
            r = _rstd(hp)
            dga = dha_ref[:, vs]
            do_ref[:, vs] = (dga * (hp * r * whn) * sig * (1.0 - sig)).astype(do_ref.dtype)
            dhn = dga * sig
            dhp, dwt = _rmsnorm_bwd_math(dhn, hp, whn)
            dwhn_ref[:, vs] += jnp.sum(dwt, axis=0, keepdims=True)
            den = den_ref[:, h:h + 1]
            floor = em_ref[:, h:h + 1]
            D = jnp.maximum(jnp.abs(den), floor)
            dnum = dhp / D
            dh_h = jnp.sum(dhp * hp, axis=1, keepdims=True)
            active = jnp.abs(den) >= floor
            dden = -dh_h / D * jnp.where(active, jnp.sign(den), 0.0)
            phi = jnp.where(active, 0.0, dh_h)
            qb, kb, qf, kf, qk_, W = _ml_intra(q_ref, k_ref, arow_ref, A_ref, h)
            vf = v_ref[:, vs]
            vb = vf.astype(BF16)
            Cb = cst_ref[0, h]
            nrow = nst_ref[0, h]
            wic = wi_ref[:, h:h + 1]
            wkc = wk_ref[:, h:h + 1]
            d = dec_ref[0, :, h:h + 1]
            dCn = dC_scr[h]
            dCb = dCn.astype(BF16)
            dnn = dn_scr[h]
            dnumb = dnum.astype(BF16)
            s = qk_ * W
            ds = (_dot_nt(dnumb, vb) + dden) * W
            dsb = ds.astype(BF16)
            dnw = (wic * dnum).astype(BF16)
            wd = wic * dden
            kw = kf * wkc
            dv_state = _dot_nt(kw.astype(BF16), dCb)
            dq = _dot_nn(dsb, kb) + _dot_nn(dnw, Cb) + wd * nrow
            dk_state = wkc * (_dot_nn(vb, dCb) + dnn)
            dk = _dot_tn(dsb, qb) + dk_state
            dv = _dot_tn(s.astype(BF16), dnumb) + dv_state
            dC = d * dCn + _dot_tn(dnw, qb)
            dn = d * dnn + jnp.sum(wd * qf, axis=0, keepdims=True)
            dC_scr[h] = dC
            dn_scr[h] = dn
            dqk_ref[:, hs] = (dq * _ML_SCALE).astype(dqk_ref.dtype)
            dqk_ref[:, C_KM + h * ML_DQK:C_KM + (h + 1) * ML_DQK] = dk.astype(dqk_ref.dtype)
            dv_ref[:, vs] = dv.astype(dv_ref.dtype)
            G = ds * qk_
            inter = _dot_nt(qb, Cb)
            qn = jnp.sum(qf * nrow, axis=1, keepdims=True)
            R = (jnp.sum(G, axis=1, keepdims=True)
                 + wic * (jnp.sum(dnum * inter, axis=1, keepdims=True) + dden * qn))
            K = jnp.sum(G.T, axis=1, keepdims=True) + jnp.sum(kf * dk_state, axis=1, keepdims=True)
            rk_tile = jnp.where(lanes == h, R - K, rk_tile)
            kc_tile = jnp.where(lanes == h, phi, kc_tile)
            tt = (jnp.sum(jnp.sum(dC * Cb.astype(F32), axis=1, keepdims=True), axis=0, keepdims=True)
                  + jnp.sum(dn * nrow, axis=1, keepdims=True))
            t_new = jnp.where(lane1 == h, tt, t_new)
        rk_ref[...] = rk_tile
        kc_ref[...] = kc_tile
        t_ref[0, 1:2, :] = t_new
        t_scr[...] = t_new

    act = lambda n: jax.ShapeDtypeStruct((S, n), BF16)
    cs = jax.ShapeDtypeStruct((S, LANES), F32)
    rowblk = lambda n: pl.BlockSpec((MLC, n), lambda c: (idx(c), 0))
    return pl.pallas_call(
        body, name=name,
        out_shape=(act(D_MODEL), act(D_MODEL), act(D_MODEL), cs, cs,
                   jax.ShapeDtypeStruct((NC, 2, LANES), F32), jax.ShapeDtypeStruct((1, D_MODEL), F32)),
        grid=(NC,),
        in_specs=[rowblk(D_MODEL), qk(C_QM // 512), qk(C_KM // 512), wide(C_VM // D_MODEL), wide(C_OM // D_MODEL),
                  rowblk(D_MODEL), col, pl.BlockSpec((8, MLC), lambda c: (0, idx(c))), col, col, col, col,
                  pl.BlockSpec((1, 1, LANES), lambda c: (idx(c), 0, 0)),
                  pl.BlockSpec((1, ML_HEADS, ML_DV, ML_DQK), lambda c: (idx(c), 0, 0, 0)),
                  pl.BlockSpec((1, ML_HEADS, 1, ML_DQK), lambda c: (idx(c), 0, 0, 0)),
                  pl.BlockSpec((1, D_MODEL), lambda c: (0, 0))],
        out_specs=(rowblk(D_MODEL), rowblk(D_MODEL), rowblk(D_MODEL), col, col,
                   pl.BlockSpec((1, 2, LANES), lambda c: (idx(c), 0, 0)), pl.BlockSpec((1, D_MODEL), lambda c: (0, 0))),
        scratch_shapes=[pltpu.VMEM((ML_HEADS, ML_DV, ML_DQK), F32), pltpu.VMEM((ML_HEADS, 1, ML_DQK), F32),
                        pltpu.VMEM((1, LANES), F32)],
        compiler_params=_cparams(("arbitrary",)),
    )(dha, pm, pm, pm, pm, hp_all, den_all, a_row, A, wi, em, wk, dec, cst, nst, w_hn)


_FOX_SCALE = FOX_DH ** -0.5
_NEG = -1e30
_LOG2E = 1.4426950408889634
_LN2 = 0.6931471805599453
_QF_BLK, _KF_BLK, _VF_BLK = 0, FOX_HEADS, 2 * FOX_HEADS


def _lane_pick(tile, lane):
    lanes = lax.broadcasted_iota(jnp.int32, tile.shape, 1)
    return jnp.sum(jnp.where(lanes == lane, tile, 0.0), axis=1, keepdims=True)


def _col_to_row(col):
    return jnp.max(jnp.broadcast_to(col, (col.shape[0], LANES)).T, axis=0, keepdims=True)


def _causal(q0, k0, shape, q_axis):
    qpos = q0 + lax.broadcasted_iota(jnp.int32, shape, q_axis)
    kpos = k0 + lax.broadcasted_iota(jnp.int32, shape, 1 - q_axis)
    return kpos <= qpos


def _fox_fwd(pf, fc, fk_row, name):
    S = pf.shape[0]
    TQ, TK = FOX_TQ_FWD, FOX_TK_FWD
    nq, nk = S // TQ, S // TK
    c1 = _FOX_SCALE * _LOG2E

    def body(q_ref, k_ref, v_ref, fc_ref, fr_ref, o_ref, lse_ref):
        h, i = pl.program_id(0), pl.program_id(1)
        qb = q_ref[...]
        fq2 = _lane_pick(fc_ref[...], h) * _LOG2E

        def step(j, carry, masked):
            m, l, acc = carry
            off = pl.multiple_of(j * TK, TK)
            t = _dot_nt(qb, k_ref[pl.ds(off, TK), :]) * c1 - fr_ref[0, j] * _LOG2E
            if masked:
                t = jnp.where(_causal(i * TQ, j * TK, (TQ, TK), 0), t, _NEG)
            m_new = jnp.maximum(m, jnp.max(t, axis=1, keepdims=True) + fq2)
            alpha = jnp.exp2(m - m_new)
            p = jnp.exp2(t + (fq2 - m_new))
            l = alpha * l + jnp.sum(p, axis=1, keepdims=True)
            acc = alpha * acc + _dot_nn(p.astype(BF16), v_ref[pl.ds(off, TK), :])
            return m_new, l, acc

        init = (jnp.full((TQ, 1), _NEG, F32), jnp.zeros((TQ, 1), F32), jnp.zeros((TQ, FOX_DH), F32))
        last = (i * TQ) // TK
        carry = lax.fori_loop(0, last, lambda j, c: step(j, c, False), init)
        for d in range(TQ // TK):
            carry = step(last + d, carry, True)
        m, l, acc = carry
        o_ref[...] = (acc / l).astype(o_ref.dtype)
        lse_ref[0, 0] = _col_to_row((m + jnp.log2(l)) * _LN2)

    head = lambda blk: pl.BlockSpec((S, FOX_DH), lambda h, i: (0, blk + h))
    return pl.pallas_call(
        body, name=name,
        out_shape=(jax.ShapeDtypeStruct((S, D_MODEL), BF16), jax.ShapeDtypeStruct((FOX_HEADS, nq, 1, TQ), F32)),
        grid=(FOX_HEADS, nq),
        in_specs=[pl.BlockSpec((TQ, FOX_DH), lambda h, i: (i, _QF_BLK + h)), head(_KF_BLK), head(_VF_BLK),
                  pl.BlockSpec((TQ, LANES), lambda h, i: (i, 0)),
                  pl.BlockSpec((1, nk, 1, TK), lambda h, i: (h, 0, 0, 0))],
        out_specs=(pl.BlockSpec((TQ, FOX_DH), lambda h, i: (i, h)),
                   pl.BlockSpec((1, 1, 1, TQ), lambda h, i: (h, i, 0, 0))),
        compiler_params=_cparams(("parallel", "arbitrary")),
    )(pf, pf, pf, fc, fk_row)


def _fox_bwd(dhb, hb, pf, lse_row, fq_row, fc, name):
    S = pf.shape[0]
    TQ, TK = FOX_TQ, FOX_TK
    nq, nk, r = S // TQ, S // TK, TK // TQ
    c1 = _FOX_SCALE * _LOG2E

    def body(q_ref, k_ref, v_ref, do_ref, o_ref, lse_ref, fq_ref, fc_ref,
             dq_ref, dk_ref, dv_ref, dFk_ref, dFq_ref, dq_acc, qside, delta, dk_acc, dv_acc, cs_acc):
        h, j = pl.program_id(0), pl.program_id(1)

        @pl.when(j == 0)
        def _():
            dq_acc[...] = jnp.zeros_like(dq_acc)
            dFq_ref[...] = jnp.zeros_like(dFq_ref)

            def fill(b, _):
                off = pl.multiple_of(b * TQ, TQ)
                prod = do_ref[pl.ds(off, TQ), :].astype(F32) * o_ref[pl.ds(off, TQ), :].astype(F32)
                delta[b] = jnp.sum(prod.T, axis=0, keepdims=True)
                qside[b] = (fq_ref[0, b] - lse_ref[0, b]) * _LOG2E
                return 0

            lax.fori_loop(0, nq, fill, 0)

        kb = k_ref[...]
        vb = v_ref[...]
        fk2 = _lane_pick(fc_ref[...], h) * _LOG2E
        dk_acc[...] = jnp.zeros_like(dk_acc)
        dv_acc[...] = jnp.zeros_like(dv_acc)
        cs_acc[...] = jnp.zeros_like(cs_acc)

        def step(i, masked):
            off = pl.multiple_of(i * TQ, TQ)
            qb = q_ref[pl.ds(off, TQ), :]
            dob = do_ref[pl.ds(off, TQ), :]
            t = _dot_nt(kb, qb) * c1 + qside[i] - fk2
            if masked:
                t = jnp.where(_causal(i * TQ, j * TK, (TK, TQ), 1), t, _NEG)
            p = jnp.exp2(t)
            dv_acc[...] += _dot_nn(p.astype(BF16), dob)
            ds = p * (_dot_nt(vb, dob) - delta[i])
            dsb = ds.astype(BF16)
            dk_acc[...] += _dot_nn(dsb, qb)
            dq_acc[pl.ds(off, TQ), :] += _dot_tn(dsb, kb)
            cs_acc[...] += jnp.sum(ds, axis=1, keepdims=True)
            dFq_ref[0, i] += jnp.sum(ds, axis=0, keepdims=True)

        for d in range(r):
            step(r * j + d, True)

        def rest(i, _):
            step(i, False)
            return 0

        lax.fori_loop(r * j + r, nq, rest, 0)
        dk_ref[...] = (dk_acc[...] * _FOX_SCALE).astype(dk_ref.dtype)
        dv_ref[...] = dv_acc[...].astype(dv_ref.dtype)
        dFk_ref[0, 0] = -_col_to_row(cs_acc[...])

        @pl.when(j == nk - 1)
        def _():
            dq_ref[...] = (dq_acc[...] * _FOX_SCALE).astype(dq_ref.dtype)

    head = lambda blk: pl.BlockSpec((S, FOX_DH), lambda h, j: (0, blk + h))
    kblk = lambda blk: pl.BlockSpec((TK, FOX_DH), lambda h, j: (j, blk + h))
    qrows = pl.BlockSpec((1, nq, 1, TQ), lambda h, j: (h, 0, 0, 0))
    act = jax.ShapeDtypeStruct((S, D_MODEL), BF16)
    return pl.pallas_call(
        body, name=name,
        out_shape=(act, act, act, jax.ShapeDtypeStruct((FOX_HEADS, nk, 1, TK), F32),
                   jax.ShapeDtypeStruct((FOX_HEADS, nq, 1, TQ), F32)),
        grid=(FOX_HEADS, nk),
        in_specs=[head(_QF_BLK), kblk(_KF_BLK), kblk(_VF_BLK), head(0), head(0), qrows, qrows,
                  pl.BlockSpec((TK, LANES), lambda h, j: (j, 0))],
        out_specs=(head(0), kblk(0), kblk(0), pl.BlockSpec((1, 1, 1, TK), lambda h, j: (h, j, 0, 0)), qrows),
        scratch_shapes=[pltpu.VMEM((S, FOX_DH), F32), pltpu.VMEM((nq, 1, TQ), F32), pltpu.VMEM((nq, 1, TQ), F32),
                        pltpu.VMEM((TK, FOX_DH), F32), pltpu.VMEM((TK, FOX_DH), F32), pltpu.VMEM((TK, 1), F32)],
        compiler_params=_cparams(("parallel", "arbitrary")),
    )(pf, pf, pf, dhb, hb, lse_row, fq_row, fc)


def _pad_lanes(v):
    return jnp.pad(v, ((0, 0), (0, LANES - v.shape[1])))


def _local_step(x, target, wmain_t, wsmall_t, rest_arrived, rest_weights, p, on_grads, advance, token):
    S = x.shape[0]
    bi, bf, bff = _pad_lanes(p["b_ml_i"]), _pad_lanes(p["b_ml_f"]), _pad_lanes(p["b_fox_f"])

    h0 = _rmsnorm_fwd(x, p["norm_mix_pre"] + token[0:1, 0:1], "norm_mix_pre")
    pm = _mm(h0, wmain_t, "nt", F32, "proj_mlstm", b_rows=(0, N_ML))
    pf = _mm(h0, wmain_t, "nt", BF16, "proj_fox", b_rows=(N_ML, N_FOX))
    pg = _mm(h0, wmain_t, "nt", F32, "proj_merge", b_rows=(N_ML + N_FOX, N_GATE))
    ps = _mm(h0, wsmall_t, "nt", F32, "proj_gates")
    a, A, wi, em, wk, dec, Fc = _gates_fwd(ps, bi, bf, bff, "gates_fwd")
    a_row = a[:, :8].T
    ha, hp, den, cst, nst = _mlstm_fwd(pm, a_row, A, wi, em, wk, dec, p["ml_head_norm"], "mlstm_fwd")
    ft = Fc[:, :FOX_HEADS].T + rest_arrived(ha)[0, 0]
    fq_row = ft.reshape(FOX_HEADS, S // FOX_TQ, 1, FOX_TQ)
    fk_row = ft.reshape(FOX_HEADS, S // FOX_TK, 1, FOX_TK)
    hb, lse_row = _fox_fwd(pf, Fc, ft.reshape(FOX_HEADS, S // FOX_TK_FWD, 1, FOX_TK_FWD), "fox_fwd")
    wa, wb, wout, wup, wdown = rest_weights(hb)
    ya = _mm(ha, wa, "nn", F32, "branch_a")
    yb = _mm(hb, wb, "nn", F32, "branch_b")
    merged = _merge_fwd(ya, yb, pg, p["b_gate_a"], p["b_gate_b"], "merge_fwd")
    z = _mm(merged, wout, "nn", F32, "out_proj")
    x1, h2 = _resid_norm_fwd(x, z, p["norm_mix_post"], p["norm_ffn_pre"], "resid_mix")
    up = _mm(h2, wup, "nn", F32, "ffn_up")
    act, conv_a, conv_g = _conv_act_fwd(up, p["conv_w"], p["conv_b"], "conv_act_fwd")
    d = _mm(act, wdown, "nn", F32, "ffn_down", tk=D_FF)
    loss_row, dy, dd, g_norm_ffn_post = _loss_head(x1, d, p["norm_ffn_post"], target, "loss_head")
    dact = _mm(dd, wdown, "nt", F32, "d_act")
    g_wdown = _mm(act, dd, "tn", F32, "dw_down", tm=1408, tk=2048)
    dupa, dupg, dcwa, dcwg, dcba, dcbg = _conv_act_bwd(up, conv_a, conv_g, dact, p["conv_w"], "conv_act_bwd")
    g_conv_w = jnp.concatenate([dcwa, dcwg], axis=1)
    g_conv_b = jnp.concatenate([dcba, dcbg], axis=1)
    dh2 = _mm_sum_parts([dupa, dupg], wup, F32, "d_h2", trans_b=True)
    g_wup = _mm(h2, [dupa, dupg], "tn", F32, "dw_up", tk=2048)
    token = on_grads("ffn", dict(w_up=g_wup, w_down=g_wdown))
    dx1, dz, g_norm_ffn_pre, g_norm_mix_post = _norm_chain_bwd(
        dh2, x1, p["norm_ffn_pre"] + token[0:1, 0:1], dy, z, p["norm_mix_post"], "norm_chain_bwd")
    dmerged = _mm(dz, wout, "nt", F32, "d_merged")
    g_wout = _mm(merged, dz, "tn", F32, "dw_out", tk=2048)
    dya, dyb, dga, dgb, g_b_gate_a, g_b_gate_b = _merge_bwd(dmerged, ya, yb, pg, p["b_gate_a"], p["b_gate_b"], "merge_bwd")
    dha = _mm(dya, wa, "nt", F32, "d_ha")
    g_wa = _mm(ha, dya, "tn", F32, "dw_a", tk=2048)
    dhb = _mm(dyb, wb, "nt", BF16, "d_hb")
    g_wb = _mm(hb, dyb, "tn", F32, "dw_b", tk=2048)
    token = advance("ffn", g_wb) + on_grads("mix", dict(w_out=g_wout, w_branch_a=g_wa, w_branch_b=g_wb))
    dqkm, dvm, dom, rk, kc, tch, g_ml_head_norm = _mlstm_bwd(
        dha, pm, hp, den, a_row, A, wi, em, wk, dec, cst, nst, p["ml_head_norm"] + token[0:1, 0:1], "mlstm_bwd")
    token = advance("mix", dqkm)
    dqf, dkf, dvf, dFk, dFq = _fox_bwd(dhb, hb, pf, lse_row.reshape(fq_row.shape), fq_row + token[0, 0], Fc, "fox_bwd")
    dF = jnp.pad((dFk.reshape(FOX_HEADS, S) + dFq.reshape(FOX_HEADS, S)).T, ((0, 0), (0, LANES - FOX_HEADS)))
    dps, dbias = _gates_bwd(ps, bi, bf, bff, rk, kc, tch, dF, "gates_bwd")
    dpm = [dqkm, dvm, dom, dqf, dkf, dvf, dga, dgb]
    g_wmain_t = _mm(dpm, h0, "tn", F32, "dw_main", tm=512, tk=2048)
    token = on_grads("in", dict(w_in=g_wmain_t))
    g_wsmall_t = _mm(dps, h0, "tn", F32, "dw_gates")
    dh0s = _mm(dps, wsmall_t + token[0:1, 0:1].astype(BF16), "nn", F32, "d_h0_gates")
    token = advance("in", dh0s)
    dh0 = _mm_sum_parts(dpm, wmain_t, F32, "d_h0_main", after=token)
    grad_x, g_norm_mix_pre = _rmsnorm_bwd([dh0, dh0s], x, p["norm_mix_pre"], dx1, F32, "norm_mix_pre_bwd")

    big = dict(wsmall_t=g_wsmall_t)
    small = dict(norm_mix_pre=g_norm_mix_pre, ml_head_norm=g_ml_head_norm, b_gate_a=g_b_gate_a, b_gate_b=g_b_gate_b,
                 norm_mix_post=g_norm_mix_post, norm_ffn_pre=g_norm_ffn_pre, norm_ffn_post=g_norm_ffn_post,
                 conv_b=g_conv_b, b_ml_i=dbias[:, 0:ML_HEADS], b_ml_f=dbias[:, LANES:LANES + ML_HEADS],
                 b_fox_f=dbias[:, 2 * LANES:2 * LANES + FOX_HEADS], conv_w=g_conv_w)
    return loss_row, grad_x, big, small


def _row_tile(r, target=256):
    best = None
    for t in range(8, min(r, target) + 1, 8):
        if r % t == 0:
            best = t
    return best if best is not None else r


def _adamw(w, g, m, v, name):
    _, R, C = w.shape
    tr = _row_tile(R)
    tc = C
    if tr == R and R > 256:
        tc = 256

    def body(w_ref, g_ref, m_ref, v_ref, d_ref, mo_ref, vo_ref):
        gv = g_ref[...]
        mn = ADAM_B1 * m_ref[0] + (1.0 - ADAM_B1) * gv
        vn = ADAM_B2 * v_ref[0] + (1.0 - ADAM_B2) * (gv * gv)
        m_hat = mn / (1.0 - ADAM_B1 ** ADAM_STEP)
        v_hat = vn / (1.0 - ADAM_B2 ** ADAM_STEP)
        d_ref[0] = -ADAM_LR * (m_hat / (jnp.sqrt(v_hat) + ADAM_EPS) + ADAM_WD * w_ref[0])
        mo_ref[0] = mn
        vo_ref[0] = vn

    blk = pl.BlockSpec((1, tr, tc), lambda i, j: (0, i, j))
    o = jax.ShapeDtypeStruct((1, R, C), F32)
    return pl.pallas_call(
        body, name=name, out_shape=(o, o, o), grid=(R // tr, C // tc),
        in_specs=[blk, pl.BlockSpec((tr, tc), lambda i, j: (i, j)), blk, blk], out_specs=(blk,) * 3,
        compiler_params=_cparams(("parallel", "parallel")),
    )(w, g, m, v)


ANY = pl.BlockSpec(memory_space=pl.ANY)


def _place():
    x, y, c = lax.axis_index("x"), lax.axis_index("y"), lax.axis_index("c")
    chips = [(1 - x, y), (x, 1 - y), (1 - x, 1 - y)]
    return x, y, c, chips


def _block(ref, kind, k, rows=None):
    if kind == "rows":
        return ref.at[k] if rows is None else ref.at[k, pl.ds(*rows), :]
    cb = ref.shape[1] // 4
    return ref.at[:, pl.ds(k * cb, cb)] if rows is None else ref.at[pl.ds(*rows), pl.ds(k * cb, cb)]


def _gathered_shape(s, kind):
    return (4,) + s.shape if kind == "rows" else (s.shape[0], 4 * s.shape[1])


def _gather_weights(shards, kinds, smalls):
    n, ns = len(shards), len(smalls)

    def body(*refs):
        ins, sm_in = refs[:n], refs[n:n + ns]
        outs, sm_out = refs[n + ns:2 * n + ns], refs[2 * n + ns:2 * (n + ns)]
        send_sems, recv_sems, sm_send, sm_recv, local_sems = refs[2 * (n + ns):]
        x, y, c, chips = _place()
        sibling = (x, y, 1 - c)
        kme = 2 * x + y

        def half(a, k, hc):
            h = ins[a].shape[0] // 2
            return _block(outs[a], kinds[a], k, (hc * h, h))

        def remote(a, slot, src, dst, to):
            return pltpu.make_async_remote_copy(src_ref=src, dst_ref=dst, send_sem=send_sems.at[a * 7 + slot],
                                                recv_sem=recv_sems.at[a * 7 + slot], device_id=to, device_id_type=MESH)

        def sm_copy(b, j, k, to):
            return pltpu.make_async_remote_copy(src_ref=sm_in[b], dst_ref=sm_out[b].at[k], send_sem=sm_send.at[3 * b + j],
                                                recv_sem=sm_recv.at[3 * b + j], device_id=to, device_id_type=MESH)

        local = [pltpu.make_async_copy(sm_in[b], sm_out[b].at[kme], local_sems.at[b]) for b in range(ns)]
        for cp in local:
            cp.start()
        sends = [remote(a, 6, ins[a], _block(outs[a], kinds[a], kme), sibling) for a in range(n)]
        for a in range(n):
            h = ins[a].shape[0] // 2
            for j, chip in enumerate(chips):
                sends.append(remote(a, j, ins[a].at[pl.ds(c * h, h), :], half(a, kme, c), (*chip, c)))
        for b in range(ns):
            for j, chip in enumerate(chips):
                sends.append(sm_copy(b, j, kme, (*chip, c)))
        for cp in sends:
            cp.start()
        for a in range(n):
            for j, chip in enumerate(chips):
                kj = 2 * chip[0] + chip[1]
                remote(a, j, half(a, kj, c), half(a, kj, c), (*chip, c)).wait_recv()
                fwd = remote(a, 3 + j, half(a, kj, c), half(a, kj, c), sibling)
                fwd.start()
                sends.append(fwd)
        for a in range(n):
            for j, chip in enumerate(chips):
                kj = 2 * chip[0] + chip[1]
                remote(a, 3 + j, half(a, kj, 1 - c), half(a, kj, 1 - c), sibling).wait_recv()
        for b in range(ns):
            for j, chip in enumerate(chips):
                sm_copy(b, j, 2 * chip[0] + chip[1], (*chip, c)).wait_recv()
        for a in range(n):
            remote(a, 6, ins[a], _block(outs[a], kinds[a], kme), sibling).wait_recv()
        for cp in sends:
            cp.wait_send()
        for cp in local:
            cp.wait()

    outs = pl.pallas_call(
        body, name="gather_weights",
        out_shape=tuple([jax.ShapeDtypeStruct(_gathered_shape(s, k), s.dtype) for s, k in zip(shards, kinds)]
                        + [jax.ShapeDtypeStruct((4,) + s.shape, s.dtype) for s in smalls]),
        in_specs=[ANY] * (n + ns), out_specs=tuple([ANY] * (n + ns)),
        scratch_shapes=[pltpu.SemaphoreType.DMA((7 * n,)), pltpu.SemaphoreType.DMA((7 * n,)),
                        pltpu.SemaphoreType.DMA((3 * ns,)), pltpu.SemaphoreType.DMA((3 * ns,)),
                        pltpu.SemaphoreType.DMA((ns,))],
    )(*shards, *smalls)
    return outs[:n], outs[n:]


_IN_HBM = pl.BlockSpec(memory_space=pltpu.HBM)
_SEMS = pl.BlockSpec(memory_space=pltpu.SEMAPHORE)
_DATAFLOW = pltpu.SideEffectType.DATAFLOW_SIDE_EFFECTING


def _hbm(t):
    return pltpu.HBM(t.shape, t.dtype)


def _gather_copies(ins, outs, send_sems, recv_sems, kinds):
    x, y, c, chips = _place()
    kme = 2 * x + y
    cps = []
    for a in range(len(ins)):
        h = ins[a].shape[0] // 2
        for j, chip in enumerate(chips + [None]):
            to = (x, y, 1 - c) if chip is None else (*chip, c)
            src = ins[a] if chip is None else ins[a].at[pl.ds(c * h, h), :]
            dst = _block(outs[a], kinds[a], kme, None if chip is None else (c * h, h))
            cps.append(pltpu.make_async_remote_copy(src_ref=src, dst_ref=dst, send_sem=send_sems.at[4 * a + j],
                                                    recv_sem=recv_sems.at[4 * a + j], device_id=to, device_id_type=MESH))
    return cps


def _gather_start(shards, kinds, name):
    n = len(shards)
    outs = [lax.empty(_gathered_shape(s, k), s.dtype) for s, k in zip(shards, kinds)]

    def body(*refs):
        for cp in _gather_copies(refs[:n], refs[n:2 * n], refs[2 * n], refs[2 * n + 1], kinds):
            cp.start()
        refs[-1][...] = jnp.zeros_like(refs[-1])

    return pl.pallas_call(
        body, name=name,
        out_shape=(pltpu.SemaphoreType.DMA((4 * n,)), pltpu.SemaphoreType.DMA((4 * n,)),
                   *[_hbm(t) for t in shards], *[_hbm(t) for t in outs], jax.ShapeDtypeStruct((8, LANES), F32)),
        in_specs=[_IN_HBM] * (2 * n),
        out_specs=(_SEMS, _SEMS, *[_IN_HBM] * (2 * n), pl.BlockSpec(memory_space=pltpu.VMEM)),
        input_output_aliases={a: 2 + a for a in range(2 * n)},
        compiler_params=pltpu.CompilerParams(has_side_effects=_DATAFLOW),
    )(*[pltpu.with_memory_space_constraint(t, pltpu.HBM) for t in list(shards) + outs])


def _gather_wait(started, after, kinds, name):
    n = (len(started) - 3) // 2
    bufs = started[2:2 + 2 * n]

    def body(*refs):
        for cp in _gather_copies(refs[:n], refs[n:2 * n], refs[2 * n], refs[2 * n + 1], kinds):
            cp.wait_send()
            cp.wait_recv()

    outs = pl.pallas_call(
        body, name=name, out_shape=tuple(_hbm(t) for t in bufs),
        in_specs=[_IN_HBM] * (2 * n) + [_SEMS, _SEMS, ANY], out_specs=tuple([_IN_HBM] * (2 * n)),
        input_output_aliases={a: a for a in range(2 * n)},
        compiler_params=pltpu.CompilerParams(has_side_effects=_DATAFLOW),
    )(*bufs, started[0], started[1], after)
    return outs[n:]


def _relay_copies(bufs, send_sems, recv_sems, kinds):
    x, y, c, chips = _place()
    cps = []
    for a in range(len(bufs)):
        h = (bufs[a].shape[1] if kinds[a] == "rows" else bufs[a].shape[0]) // 2
        for j, chip in enumerate(chips):
            part = _block(bufs[a], kinds[a], 2 * chip[0] + chip[1], (c * h, h))
            cps.append(pltpu.make_async_remote_copy(src_ref=part, dst_ref=part, send_sem=send_sems.at[3 * a + j],
                                                    recv_sem=recv_sems.at[3 * a + j], device_id=(x, y, 1 - c),
                                                    device_id_type=MESH))
    return cps


def _join_copies(bufs, send_sems, recv_sems, kinds):
    x, y, c, _ = _place()
    cps = []
    for a in range(len(bufs)):
        h = bufs[a].shape[0] // 2
        mine = bufs[a].at[pl.ds(c * h, h), :]
        cps.append(pltpu.make_async_remote_copy(src_ref=mine, dst_ref=mine, send_sem=send_sems.at[a],
                                                recv_sem=recv_sems.at[a], device_id=(x, y, 1 - c), device_id_type=MESH))
    return cps


def _inplace_start(copies, per_array, bufs, kinds, name):
    n = len(bufs)

    def body(*refs):
        for cp in copies(refs[:n], refs[n], refs[n + 1], kinds):
            cp.start()
        refs[-1][...] = jnp.zeros_like(refs[-1])

    return pl.pallas_call(
        body, name=name,
        out_shape=(pltpu.SemaphoreType.DMA((per_array * n,)), pltpu.SemaphoreType.DMA((per_array * n,)),
                   *[_hbm(t) for t in bufs], jax.ShapeDtypeStruct((8, LANES), F32)),
        in_specs=[_IN_HBM] * n, out_specs=(_SEMS, _SEMS, *[_IN_HBM] * n, pl.BlockSpec(memory_space=pltpu.VMEM)),
        input_output_aliases={a: 2 + a for a in range(n)},
        compiler_params=pltpu.CompilerParams(has_side_effects=_DATAFLOW),
    )(*[pltpu.with_memory_space_constraint(t, pltpu.HBM) for t in bufs])


def _inplace_wait(copies, started, after, kinds, name):
    n = len(started) - 3
    bufs = started[2:2 + n]

    def body(*refs):
        for cp in copies(refs[:n], refs[n], refs[n + 1], kinds):
            cp.wait_send()
            cp.wait_recv()

    return pl.pallas_call(
        body, name=name, out_shape=tuple(_hbm(t) for t in bufs),
        in_specs=[_IN_HBM] * n + [_SEMS, _SEMS, ANY], out_specs=tuple([_IN_HBM] * n),
        input_output_aliases={a: a for a in range(n)},
        compiler_params=pltpu.CompilerParams(has_side_effects=_DATAFLOW),
    )(*bufs, started[0], started[1], after)


def _add_halves(g, r1, cvec, kind, name):
    def body(c_ref, g_ref, r_ref, o_ref):
        o_ref[...] = (g_ref[...] + r_ref[...]).astype(o_ref.dtype)

    if kind == "rows":
        _, h, C = r1.shape
        tr = _row_tile(h, 512)
        nt = h // tr
        grid = (4, nt)
        g_spec = pl.BlockSpec((1, tr, C), lambda k, i, c_ref: (k, c_ref[0] * nt + i, 0))
        r_spec = pl.BlockSpec((1, tr, C), lambda k, i, c_ref: (k, i, 0))
    else:
        h, C4 = r1.shape
        tr, tc = _row_tile(h, 512), C4 // 4
        nt = h // tr
        grid = (nt, 4)
        g_spec = pl.BlockSpec((tr, tc), lambda i, k, c_ref: (c_ref[0] * nt + i, k))
        r_spec = pl.BlockSpec((tr, tc), lambda i, k, c_ref: (i, k))
    return pl.pallas_call(
        body, name=name, out_shape=jax.ShapeDtypeStruct(r1.shape, BF16),
        grid_spec=pltpu.PrefetchScalarGridSpec(num_scalar_prefetch=1, grid=grid, in_specs=[g_spec, r_spec],
                                               out_specs=r_spec),
        compiler_params=_cparams(("parallel", "parallel")),
    )(cvec, g, r1)


def _chip_copies(ins, lands, send_sems, recv_sems, kinds):
    x, y, c, chips = _place()
    return [pltpu.make_async_remote_copy(
        src_ref=_block(ins[a], kinds[a], 2 * chip[0] + chip[1]), dst_ref=lands[a].at[j],
        send_sem=send_sems.at[3 * a + j], recv_sem=recv_sems.at[3 * a + j], device_id=(*chip, c), device_id_type=MESH)
        for a in range(len(ins)) for j, chip in enumerate(chips)]


def _land_shape(s, kind):
    return (3,) + (s.shape[1:] if kind == "rows" else (s.shape[0], s.shape[1] // 4))


def _sibling_copies(ins, lands, send_sems, recv_sems, kinds):
    x, y, c, _ = _place()
    cps = []
    for a in range(len(ins)):
        h = lands[a].shape[-2]
        src = ins[a].at[:, pl.ds((1 - c) * h, h), :] if kinds[a] == "rows" else ins[a].at[pl.ds((1 - c) * h, h), :]
        cps.append(pltpu.make_async_remote_copy(src_ref=src, dst_ref=lands[a], send_sem=send_sems.at[a],
                                                recv_sem=recv_sems.at[a], device_id=(x, y, 1 - c), device_id_type=MESH))
    return cps


def _half_shape(g, kind):
    return (4, g.shape[1] // 2, g.shape[2]) if kind == "rows" else (g.shape[0] // 2, g.shape[1])


def _exchange_start(copies, per_array, srcs, land_shapes, kinds, name, zeroed=False):
    n = len(srcs)
    lands = [(jnp.zeros if zeroed else lax.empty)(shape, s.dtype) for shape, s in zip(land_shapes, srcs)]

    def body(*refs):
        for cp in copies(refs[:n], refs[n:2 * n], refs[2 * n], refs[2 * n + 1], kinds):
            cp.start()
        refs[-1][...] = jnp.zeros_like(refs[-1])

    return pl.pallas_call(
        body, name=name,
        out_shape=(pltpu.SemaphoreType.DMA((per_array * n,)), pltpu.SemaphoreType.DMA((per_array * n,)),
                   *[_hbm(t) for t in srcs], *[_hbm(t) for t in lands], jax.ShapeDtypeStruct((8, LANES), F32)),
        in_specs=[_IN_HBM] * (2 * n),
        out_specs=(_SEMS, _SEMS, *[_IN_HBM] * (2 * n), pl.BlockSpec(memory_space=pltpu.VMEM)),
        input_output_aliases={a: 2 + a for a in range(2 * n)},
        compiler_params=pltpu.CompilerParams(has_side_effects=_DATAFLOW),
    )(*[pltpu.with_memory_space_constraint(t, pltpu.HBM) for t in list(srcs) + lands])


def _exchange_wait(copies, started, after, kinds, name):
    n = (len(started) - 3) // 2
    bufs = started[2:2 + 2 * n]

    def body(*refs):
        for cp in copies(refs[:n], refs[n:2 * n], refs[2 * n], refs[2 * n + 1], kinds):
            cp.wait_send()
            cp.wait_recv()

    outs = pl.pallas_call(
        body, name=name, out_shape=tuple(_hbm(t) for t in bufs),
        in_specs=[_IN_HBM] * (2 * n) + [_SEMS, _SEMS, ANY], out_specs=tuple([_IN_HBM] * (2 * n)),
        input_output_aliases={a: a for a in range(2 * n)},
        compiler_params=pltpu.CompilerParams(has_side_effects=_DATAFLOW),
    )(*bufs, started[0], started[1], after)
    return outs[:n], outs[n:]


def _add_chips(s1, r2, kcvec, kind, name):
    _, h, C = r2.shape
    tr = _row_tile(h, 512)
    nt = h // tr

    def body(kc_ref, s_ref, r0_ref, r1_ref, r2_ref, o_ref):
        s = s_ref[0] if kind == "rows" else s_ref[...]
        o_ref[...] = ((s.astype(F32) + r0_ref[0].astype(F32)) + r1_ref[0].astype(F32)) + r2_ref[0].astype(F32)

    peer = lambda j: pl.BlockSpec((1, tr, C), lambda i, kc_ref: (j, i, 0))
    if kind == "rows":
        s_spec = pl.BlockSpec((1, tr, C), lambda i, kc_ref: (kc_ref[0], i, 0))
    else:
        s_spec = pl.BlockSpec((tr, C), lambda i, kc_ref: (i, kc_ref[0]))
    return pl.pallas_call(
        body, name=name, out_shape=jax.ShapeDtypeStruct((2 * h, C), F32),
        grid_spec=pltpu.PrefetchScalarGridSpec(
            num_scalar_prefetch=1, grid=(nt,),
            in_specs=[s_spec, peer(0), peer(1), peer(2)],
            out_specs=pl.BlockSpec((tr, C), lambda i, kc_ref: (kc_ref[1] * nt + i, 0))),
        compiler_params=_cparams(("parallel",)),
    )(kcvec, s1, r2, r2, r2)


N_DEV = 8


def _spread_copies(packs, lands, send_sems, recv_sems, kinds):
    x, y, c, _ = _place()
    me = 4 * x + 2 * y + c
    return [pltpu.make_async_remote_copy(
        src_ref=packs[0], dst_ref=lands[0].at[me], send_sem=send_sems.at[mask - 1], recv_sem=recv_sems.at[mask - 1],
        device_id=(1 - x if mask & 4 else x, 1 - y if mask & 2 else y, 1 - c if mask & 1 else c), device_id_type=MESH)
        for mask in range(1, N_DEV)]


def _sum_spread(pack, gathered):
    P = pack.shape[0]

    def body(p_ref, g_ref, o_ref):
        x, y, c, _ = _place()
        me = 4 * x + 2 * y + c
        acc = None
        for i in range(N_DEV):
            term = jnp.where(me == i, p_ref[...], g_ref[i])
            acc = term if acc is None else acc + term
        o_ref[...] = acc

    vmem = pl.BlockSpec(memory_space=pltpu.VMEM)
    return pl.pallas_call(body, name="allreduce_sum", out_shape=jax.ShapeDtypeStruct((P, LANES), F32),
                          in_specs=[vmem, vmem], out_specs=vmem)(pack, gathered)


def _pack_rows(arrs):
    rows = []
    for a in arrs:
        f = a.reshape(-1)
        f = jnp.pad(f, (0, (-f.shape[0]) % (8 * LANES)))
        rows.append(f.reshape(-1, LANES))
    return jnp.concatenate(rows, axis=0)


def _unpack_rows(pack, shapes):
    out, r = [], 0
    for s in shapes:
        n = math.prod(s)
        out.append(pack[r:r + -(-n // LANES)].reshape(-1)[:n].reshape(s))
        r += 8 * -(-n // (8 * LANES))
    return out


_SMALL = ["norm_mix_pre", "ml_head_norm", "b_gate_a", "b_gate_b", "norm_mix_post", "norm_ffn_pre", "norm_ffn_post",
          "conv_b", "b_ml_i", "b_ml_f", "b_fox_f"]
_BIG = ["w_in", "w_branch_a", "w_branch_b", "w_out", "w_up", "w_down"]
_WEIGHTS = ['norm_mix_pre', 'w_in', 'b_ml_i', 'b_ml_f', 'ml_head_norm', 'b_fox_f', 'b_gate_a', 'b_gate_b', 'w_branch_a',
            'w_branch_b', 'w_out', 'norm_mix_post', 'norm_ffn_pre', 'w_up', 'conv_w', 'conv_b', 'w_down', 'norm_ffn_post']


_KINDS = ["rows", "rows", "rows", "rows", "cols", "rows"]


def kernel(x, norm_mix_pre, w_in, b_ml_i, b_ml_f, ml_head_norm, b_fox_f, b_gate_a, b_gate_b, w_branch_a, w_branch_b, w_out, norm_mix_post, norm_ffn_pre, w_up, conv_w, conv_b, w_down, norm_ffn_post, loss_target, m_norm_mix_pre, m_w_in, m_b_ml_i, m_b_ml_f, m_ml_head_norm, m_b_fox_f, m_b_gate_a, m_b_gate_b, m_w_branch_a, m_w_branch_b, m_w_out, m_norm_mix_post, m_norm_ffn_pre, m_w_up, m_conv_w, m_conv_b, m_w_down, m_norm_ffn_post, v_norm_mix_pre, v_w_in, v_b_ml_i, v_b_ml_f, v_ml_head_norm, v_b_fox_f, v_b_gate_a, v_b_gate_b, v_w_branch_a, v_w_branch_b, v_w_out, v_norm_mix_post, v_norm_ffn_pre, v_w_up, v_conv_w, v_conv_b, v_w_down, v_norm_ffn_post):
    args = dict(locals())
    w = {n: args[n] for n in _WEIGHTS}
    mom = {n: args["m_" + n] for n in _WEIGHTS}
    var = {n: args["v_" + n] for n in _WEIGHTS}
    cx, cy, cc = lax.axis_index("x"), lax.axis_index("y"), lax.axis_index("c")
    kme = 2 * cx + cy
    cvec = jnp.reshape(cc, (1,)).astype(jnp.int32)
    kcvec = jnp.stack([kme, cc]).astype(jnp.int32)
    odd = kme % 2

    tr3 = lambda t: jnp.transpose(t, (0, 2, 1))
    w["w_in"], mom["w_in"], var["w_in"] = tr3(w_in), tr3(m_w_in), tr3(v_w_in)
    w_in_main = lax.dynamic_slice_in_dim(w["w_in"][0], 4 * odd, 2048, axis=0).astype(BF16)
    w_in_gates = lax.dynamic_slice_in_dim(w["w_in"][0], 2048 * (1 - odd), 4, axis=0).astype(BF16)
    (wmain_t,), (g_cw, g_gates) = _gather_weights([w_in_main], _KINDS[:1], [w["conv_w"][0], w_in_gates])
    rest_started = _gather_start([w[n][0].astype(BF16) for n in _BIG[1:]], _KINDS[1:], "gather_rest_start")

    relay = {}

    def rest_arrived(after):
        bufs = _gather_wait(rest_started, after, _KINDS[1:], "gather_rest_wait")
        relay["started"] = _inplace_start(_relay_copies, 3, bufs, _KINDS[1:], "gather_rest_relay_start")
        return relay["started"][-1]

    def rest_weights(after):
        g_a, g_b, g_out, wup, g_down = _inplace_wait(_relay_copies, relay["started"], after, _KINDS[1:],
                                                     "gather_rest_relay_wait")
        return full(g_a), full(g_b), full(g_out), wup, full(g_down)
    gate_rows = g_gates.reshape(16, D_MODEL)
    wsmall_t = jnp.zeros((N_SMALL, D_MODEL), BF16)
    for blk, (lo, hi) in enumerate(((0, 4), (4, 8), (8, 16))):
        wsmall_t = wsmall_t.at[blk * LANES:blk * LANES + hi - lo].set(gate_rows[lo:hi])
    full = lambda g: g.reshape(-1, g.shape[2])
    p = {n: w[n] for n in _SMALL}
    p["conv_w"] = jnp.transpose(g_cw, (1, 0, 2)).reshape(3, -1)

    groups = {}

    def on_grads(group, gs):
        names = list(gs)
        kinds = [_KINDS[_BIG.index(n)] for n in names]
        whole = [g if k == "cols" else g.reshape(4, -1, g.shape[1]) for g, k in zip(gs.values(), kinds)]
        started = _exchange_start(_sibling_copies, 1, whole, [_half_shape(g, k) for g, k in zip(whole, kinds)], kinds,
                                  "grads_to_sibling_start_" + group)
        groups[group] = dict(names=names, kinds=kinds, sibling=started)
        return started[-1]

    def advance(group, after):
        G = groups[group]
        whole, got = _exchange_wait(_sibling_copies, G["sibling"], after, G["kinds"], "grads_to_sibling_wait_" + group)
        sums = [_add_halves(g, r, cvec, k, "add_sibling_" + n) for g, r, k, n in zip(whole, got, G["kinds"], G["names"])]
        G["chips"] = _exchange_start(_chip_copies, 3, sums, [_land_shape(s, k) for s, k in zip(sums, G["kinds"])],
                                     G["kinds"], "grads_to_chips_start_" + group)
        return G["chips"][-1]

    loss_row, grad_x, big, small = _local_step(x[0], loss_target[0], full(wmain_t), wsmall_t, rest_arrived, rest_weights,
                                               p, on_grads, advance, rest_started[-1])
    gt = big["wsmall_t"]
    small["w_in_gates"] = jnp.concatenate([gt[0:4], gt[LANES:LANES + 4], gt[2 * LANES:2 * LANES + 8]], axis=0)
    small_names = _SMALL + ["conv_w"]
    packed_names = small_names + ["w_in_gates"]
    pack = _pack_rows([small[n] for n in packed_names] + [loss_row])
    spread = _exchange_start(_spread_copies, N_DEV - 1, [pack], [(N_DEV,) + pack.shape], None, "allreduce_start", zeroed=True)

    def my_half(group, after):
        G = groups[group]
        sums, got = _exchange_wait(_chip_copies, G["chips"], after, G["kinds"], "grads_to_chips_wait_" + group)
        return [_add_chips(s, r, kcvec, k, "add_chips_" + n) for s, r, k, n in zip(sums, got, G["kinds"], G["names"])]

    first_names = groups["ffn"]["names"] + groups["mix"]["names"]
    join_first = _inplace_start(_join_copies, 1, my_half("ffn", spread[-1]) + my_half("mix", spread[-1]), None,
                                "grads_join_start")
    join_in = _inplace_start(_join_copies, 1, my_half("in", join_first[-1]), None, "grads_join_start_in")
    grads = dict(zip(first_names, _inplace_wait(_join_copies, join_first, join_in[-1], None, "grads_join_wait")))

    delta, new_m, new_v = {}, {}, {}
    for n in _BIG[1:]:
        delta[n], new_m[n], new_v[n] = _adamw(w[n], grads[n], mom[n], var[n], "adamw_" + n)
        grads[n] = grads[n][None]
    grads["w_in"], = _inplace_wait(_join_copies, join_in, delta[_BIG[-1]], None, "grads_join_wait_in")

    (pack,), (gathered,) = _exchange_wait(_spread_copies, spread, delta[_BIG[-1]], None, "allreduce_wait")
    full_shapes = [small[n].shape if n in ("conv_w", "w_in_gates") else w[n][0].shape for n in packed_names]
    total = _unpack_rows(_sum_spread(pack, gathered), full_shapes + [loss_row.shape])
    for n, t in zip(packed_names, total):
        grads[n] = t
    loss = total[-1][0, 0]
    grads["conv_w"] = lax.dynamic_slice_in_dim(grads["conv_w"], kme * conv_w.shape[2], conv_w.shape[2], axis=1)
    my_gates = lax.dynamic_slice_in_dim(grads.pop("w_in_gates"), 4 * kme, 4, axis=0)
    g_in = jnp.zeros(w["w_in"].shape[1:], F32)
    g_in = lax.dynamic_update_slice_in_dim(g_in, grads["w_in"], 4 * odd, axis=0)
    grads["w_in"] = lax.dynamic_update_slice_in_dim(g_in, my_gates, 2048 * (1 - odd), axis=0)
    delta["w_in"], new_m["w_in"], new_v["w_in"] = _adamw(w["w_in"], grads["w_in"], mom["w_in"], var["w_in"], "adamw_w_in")
    grads["w_in"] = grads["w_in"][None]
    for d in (grads, delta, new_m, new_v):
        d["w_in"] = tr3(d["w_in"])
    packs = [_pack_rows([d[n][0] for n in small_names]) for d in (w, mom, var)]
    pad = ((0, (-packs[0].shape[0]) % 8), (0, 0))
    packs = [jnp.pad(t, pad)[None] for t in packs]
    gp = jnp.pad(_pack_rows([grads[n] for n in small_names]), pad)
    shapes = [w[n][0].shape for n in small_names]
    for dst, res in zip((delta, new_m, new_v), _adamw(packs[0], gp, packs[1], packs[2], "adamw_small")):
        for n, t in zip(small_names, _unpack_rows(res[0], shapes)):
            dst[n] = t[None]
    for n in small_names:
        grads[n] = grads[n][None]

    return (loss, grad_x[None], *[grads[n] for n in _WEIGHTS], *[delta[n] for n in _WEIGHTS],
            *[new_m[n] for n in _WEIGHTS], *[new_v[n] for n in _WEIGHTS])
```

```python
import functools
import math

import jax
import jax.numpy as jnp
from jax import lax
from jax.experimental import pallas as pl
from jax.experimental.pallas import tpu as pltpu

F32 = jnp.float32
BF16 = jnp.bfloat16
MESH = pl.DeviceIdType.MESH

D_MODEL = 1024
ML_HEADS = 4
ML_DQK = 128
ML_DV = 256
FOX_HEADS = 8
FOX_DH = 128
D_FF = 2816
GATE_CAP = 15.0
EPS = 1e-6
ADAM_LR, ADAM_B1, ADAM_B2, ADAM_EPS, ADAM_WD, ADAM_STEP = 0.001, 0.9, 0.999, 1e-08, 0.01, 10

LANES = 128
MLC = 256
FOX_TQ = 512
FOX_TQ_FWD = 512
FOX_TK = 512
FOX_TK_FWD = 1024
ROW_T = 512
CONV_TC = 1408
VMEM_LIMIT = 56 * 1024 * 1024

C_QM, C_KM, C_VM, C_OM = 0, 512, 1024, 2048
N_ML, N_FOX, N_GATE = 3072, 3072, 2048
N_SMALL = 384


def _cparams(sem=None):
    return pltpu.CompilerParams(dimension_semantics=sem, vmem_limit_bytes=VMEM_LIMIT)


def _tile(n, target):
    if n <= target:
        return n
    best = None
    for t in range(LANES, target + 1, LANES):
        if n % t == 0:
            best = t
    assert best is not None, (n, target)
    return best


def _dot(a, b, dims):
    return lax.dot_general(a, b, (dims, ((), ())), preferred_element_type=F32)


def _dot_nn(a, b):
    return _dot(a, b, ((1,), (0,)))


def _dot_nt(a, b):
    return _dot(a, b, ((1,), (1,)))


def _dot_tn(a, b):
    return _dot(a, b, ((0,), (0,)))


_DOTS = {"nn": _dot_nn, "nt": _dot_nt, "tn": _dot_tn}


def _mm(a, b, mode, out_dtype, name, tm=1024, tn=1408, tk=1408, after=None, b_rows=None):
    a_parts = list(a) if isinstance(a, (list, tuple)) else [a]
    b_parts = list(b) if isinstance(b, (list, tuple)) else [b]
    extra = [] if after is None else [after]
    assert len(a_parts) == 1 or len(b_parts) == 1, name
    a_axes = {"nn": "ik", "nt": "ik", "tn": "ki"}[mode]
    b_axes = {"nn": "kj", "nt": "jk", "tn": "kj"}[mode]
    size, target = {}, dict(i=tm, j=tn, k=tk)
    for parts, axes in ((a_parts, a_axes), (b_parts, b_axes)):
        dims = (parts[0].shape[0], parts[0].shape[1] * len(parts))
        if parts is b_parts and b_rows is not None:
            dims = (b_rows[1], dims[1])
        for ax, n in zip(axes, dims):
            assert size.setdefault(ax, n) == n, (name, ax, n, size)
    tile = {}
    for parts, axes in ((a_parts, a_axes), (b_parts, b_axes)):
        if len(parts) > 1:
            tile[axes[1]] = _tile(parts[0].shape[1], target[axes[1]])
    for ax in "ijk":
        tile.setdefault(ax, _tile(size[ax], target[ax]))
    M, N, nk = size["i"], size["j"], size["k"] // tile["k"]
    grid_pos = dict(i=0, j=1, k=2)
    dot = _DOTS[mode]

    def specs(parts, axes):
        blk = (tile[axes[0]], tile[axes[1]])
        if len(parts) == 1:
            first = 0
            if parts is b_parts and b_rows is not None:
                assert b_rows[0] % blk[0] == 0, (name, b_rows, blk)
                first = b_rows[0] // blk[0]
            return [pl.BlockSpec(blk, lambda *g: (first + g[grid_pos[axes[0]]], g[grid_pos[axes[1]]]))], None
        bpp = parts[0].shape[1] // blk[1]

        def index(p):
            def f(*g):
                g0, g1 = g[grid_pos[axes[0]]], g[grid_pos[axes[1]]]
                on = g1 // bpp == p
                return jnp.where(on, g0, 0), jnp.where(on, g1 % bpp, 0)
            return f

        return [pl.BlockSpec(blk, index(p)) for p in range(len(parts))], (axes[1], bpp)

    a_specs, a_sel = specs(a_parts, a_axes)
    b_specs, b_sel = specs(b_parts, b_axes)
    na, nb = len(a_parts), len(b_parts)

    def body(*refs):
        a_refs, b_refs = refs[:na], refs[na:na + nb]
        o_ref, acc = refs[na + nb + len(extra)], refs[na + nb + len(extra) + 1:]

        def accumulate(part):
            if nk == 1:
                o_ref[...] = part.astype(o_ref.dtype)
                return
            acc_ref, = acc
            k = pl.program_id(2)

            @pl.when(k == 0)
            def _():
                acc_ref[...] = part

            @pl.when(k > 0)
            def _():
                acc_ref[...] += part

            @pl.when(k == nk - 1)
            def _():
                o_ref[...] = acc_ref[...].astype(o_ref.dtype)

        sel = a_sel or b_sel
        if sel is None:
            accumulate(dot(a_refs[0][...], b_refs[0][...]))
        else:
            which = pl.program_id(grid_pos[sel[0]]) // sel[1]
            for p in range(max(na, nb)):
                @pl.when(which == p)
                def _(p=p):
                    accumulate(dot(a_refs[p if a_sel else 0][...], b_refs[p if b_sel else 0][...]))

    return pl.pallas_call(
        body, name=name,
        out_shape=jax.ShapeDtypeStruct((M, N), out_dtype),
        grid=(M // tile["i"], N // tile["j"], nk),
        in_specs=a_specs + b_specs + [pl.BlockSpec(memory_space=pl.ANY)] * len(extra),
        out_specs=pl.BlockSpec((tile["i"], tile["j"]), lambda i, j, k: (i, j)),
        scratch_shapes=[pltpu.VMEM((tile["i"], tile["j"]), F32)] if nk > 1 else [],
        compiler_params=_cparams(("parallel", "parallel", "arbitrary")),
    )(*a_parts, *b_parts, *extra)


def _mm_sum_parts(parts, b, out_dtype, name, trans_b=False, tm=1024, tn=512, after=None):
    M, K = parts[0].shape
    N = b.shape[0] if trans_b else b.shape[1]
    tm, tn = _tile(M, tm), _tile(N, tn)
    n = len(parts)
    extra = [] if after is None else [after]

    def body(*refs):
        b_ref, o_ref = refs[n], refs[n + 1 + len(extra)]
        acc = None
        for p in range(n):
            if trans_b:
                d = _dot_nt(refs[p][...], b_ref[:, p * K:(p + 1) * K])
            else:
                d = _dot_nn(refs[p][...], b_ref[p * K:(p + 1) * K, :])
            acc = d if acc is None else acc + d
        o_ref[...] = acc.astype(o_ref.dtype)

    b_spec = pl.BlockSpec((tn, n * K), lambda i, j: (j, 0)) if trans_b else pl.BlockSpec((n * K, tn), lambda i, j: (0, j))
    return pl.pallas_call(
        body, name=name, out_shape=jax.ShapeDtypeStruct((M, N), out_dtype), grid=(M // tm, N // tn),
        in_specs=[pl.BlockSpec((tm, K), lambda i, j: (i, 0))] * n + [b_spec]
        + [pl.BlockSpec(memory_space=pl.ANY)] * len(extra),
        out_specs=pl.BlockSpec((tm, tn), lambda i, j: (i, j)),
        compiler_params=_cparams(("parallel", "arbitrary")),
    )(*parts, b, *extra)


def _rstd(x):
    return lax.rsqrt(jnp.mean(x * x, axis=-1, keepdims=True) + EPS)


def _rmsnorm_fwd(x, g, name):
    S, D = x.shape
    T = _tile(S, ROW_T)

    def body(x_ref, g_ref, o_ref):
        xv = x_ref[...]
        o_ref[...] = (xv * _rstd(xv) * g_ref[...]).astype(o_ref.dtype)

    return pl.pallas_call(
        body, name=name, out_shape=jax.ShapeDtypeStruct((S, D), BF16), grid=(S // T,),
        in_specs=[pl.BlockSpec((T, D), lambda i: (i, 0)), pl.BlockSpec((1, D), lambda i: (0, 0))],
        out_specs=pl.BlockSpec((T, D), lambda i: (i, 0)),
        compiler_params=_cparams(("parallel",)),
    )(x, g)


def _resid_norm_fwd(x, z, g, g_next, name):
    S, D = x.shape
    T = _tile(S, ROW_T)

    def body(x_ref, z_ref, g_ref, gn_ref, o_ref, h_ref):
        zv = z_ref[...]
        x1 = x_ref[...] + zv * _rstd(zv) * g_ref[...]
        o_ref[...] = x1
        h_ref[...] = (x1 * _rstd(x1) * gn_ref[...]).astype(h_ref.dtype)

    row = pl.BlockSpec((T, D), lambda i: (i, 0))
    vec = pl.BlockSpec((1, D), lambda i: (0, 0))
    return pl.pallas_call(
        body, name=name, out_shape=(jax.ShapeDtypeStruct((S, D), F32), jax.ShapeDtypeStruct((S, D), BF16)),
        grid=(S // T,), in_specs=[row, row, vec, vec], out_specs=(row, row), compiler_params=_cparams(("parallel",)),
    )(x, z, g, g_next)


def _norm_chain_bwd(dh, xin, g, resid, zin, gz, name):
    S, D = xin.shape
    T = _tile(S, ROW_T)

    def body(dh_ref, x_ref, g_ref, r_ref, z_ref, gz_ref, dx_ref, dz_ref, dg_ref, dgz_ref):
        dx, dgt = _rmsnorm_bwd_math(dh_ref[...], x_ref[...], g_ref[...])
        dx = dx + r_ref[...]
        dx_ref[...] = dx
        dz, dgzt = _rmsnorm_bwd_math(dx, z_ref[...], gz_ref[...])
        dz_ref[...] = dz.astype(dz_ref.dtype)

        @pl.when(pl.program_id(0) == 0)
        def _():
            dg_ref[...] = jnp.zeros_like(dg_ref)
            dgz_ref[...] = jnp.zeros_like(dgz_ref)

        dg_ref[...] += jnp.sum(dgt, axis=0, keepdims=True)
        dgz_ref[...] += jnp.sum(dgzt, axis=0, keepdims=True)

    row = pl.BlockSpec((T, D), lambda i: (i, 0))
    vec = pl.BlockSpec((1, D), lambda i: (0, 0))
    v1 = jax.ShapeDtypeStruct((1, D), F32)
    return pl.pallas_call(
        body, name=name,
        out_shape=(jax.ShapeDtypeStruct((S, D), F32), jax.ShapeDtypeStruct((S, D), BF16), v1, v1),
        grid=(S // T,), in_specs=[row, row, vec, row, row, vec], out_specs=(row, row, vec, vec),
        compiler_params=_cparams(("arbitrary",)),
    )(dh, xin, g, resid, zin, gz)


def _rmsnorm_bwd_math(dy, xv, g):
    r = _rstd(xv)
    u = dy * g
    dx = r * u - xv * (r * r * r) * jnp.mean(u * xv, axis=-1, keepdims=True)
    return dx, dy * xv * r


def _rmsnorm_bwd(dys, xin, g, resid, out_dtype, name):
    S, D = xin.shape
    T = _tile(S, ROW_T)
    has_resid = resid is not None
    ndy = len(dys)

    def body(*refs):
        dy_refs, (x_ref, g_ref) = refs[:ndy], refs[ndy:ndy + 2]
        dx_ref, dg_ref = refs[-2:]
        dy = dy_refs[0][...]
        for r in dy_refs[1:]:
            dy = dy + r[...]
        dx, dgt = _rmsnorm_bwd_math(dy, x_ref[...], g_ref[...])
        if has_resid:
            dx = dx + refs[ndy + 2][...]
        dx_ref[...] = dx.astype(dx_ref.dtype)

        @pl.when(pl.program_id(0) == 0)
        def _():
            dg_ref[...] = jnp.zeros_like(dg_ref)

        dg_ref[...] += jnp.sum(dgt, axis=0, keepdims=True)

    row = pl.BlockSpec((T, D), lambda i: (i, 0))
    vec = pl.BlockSpec((1, D), lambda i: (0, 0))
    ins = list(dys) + [xin, g] + ([resid] if has_resid else [])
    return pl.pallas_call(
        body, name=name,
        out_shape=(jax.ShapeDtypeStruct((S, D), out_dtype), jax.ShapeDtypeStruct((1, D), F32)),
        grid=(S // T,), in_specs=[row] * ndy + [row, vec] + ([row] if has_resid else []),
        out_specs=(row, vec), compiler_params=_cparams(("arbitrary",)),
    )(*ins)


def _loss_head(x1, d, g, target, name):
    S, D = x1.shape
    T = _tile(S, ROW_T)

    def body(x_ref, d_ref, g_ref, t_ref, loss_ref, dy_ref, dd_ref, dg_ref):
        dv, gv = d_ref[...], g_ref[...]
        y = x_ref[...] + dv * _rstd(dv) * gv
        diff = y - t_ref[...]
        dy = diff * (1.0 / D)
        dy_ref[...] = dy
        dd, dgt = _rmsnorm_bwd_math(dy, dv, gv)
        dd_ref[...] = dd.astype(dd_ref.dtype)

        @pl.when(pl.program_id(0) == 0)
        def _():
            dg_ref[...] = jnp.zeros_like(dg_ref)
            loss_ref[...] = jnp.zeros_like(loss_ref)

        dg_ref[...] += jnp.sum(dgt, axis=0, keepdims=True)
        part = jnp.sum(jnp.sum(diff * diff, axis=1, keepdims=True), axis=0, keepdims=True)
        loss_ref[...] += (0.5 / D) * part

    row = pl.BlockSpec((T, D), lambda i: (i, 0))
    vec = pl.BlockSpec((1, D), lambda i: (0, 0))
    return pl.pallas_call(
        body, name=name,
        out_shape=(jax.ShapeDtypeStruct((1, LANES), F32), jax.ShapeDtypeStruct((S, D), F32),
                   jax.ShapeDtypeStruct((S, D), BF16), jax.ShapeDtypeStruct((1, D), F32)),
        grid=(S // T,), in_specs=[row, row, vec, row],
        out_specs=(pl.BlockSpec((1, LANES), lambda i: (0, 0)), row, row, vec),
        compiler_params=_cparams(("arbitrary",)),
    )(x1, d, g, target)


def _merge_fwd(ya, yb, pm, ba, bb, name):
    S, D = ya.shape
    T = _tile(S, ROW_T)

    def body(ya_ref, yb_ref, ga_ref, gb_ref, ba_ref, bb_ref, o_ref):
        sa = jax.nn.sigmoid(ga_ref[...] + ba_ref[...])
        sb = jax.nn.sigmoid(gb_ref[...] + bb_ref[...])
        o_ref[...] = (sa * ya_ref[...] + sb * yb_ref[...]).astype(o_ref.dtype)

    row = pl.BlockSpec((T, D), lambda i: (i, 0))
    vec = pl.BlockSpec((1, D), lambda i: (0, 0))
    return pl.pallas_call(
        body, name=name, out_shape=jax.ShapeDtypeStruct((S, D), BF16), grid=(S // T,),
        in_specs=[row, row, pl.BlockSpec((T, D), lambda i: (i, 0)),
                  pl.BlockSpec((T, D), lambda i: (i, 1)), vec, vec],
        out_specs=row, compiler_params=_cparams(("parallel",)),
    )(ya, yb, pm, pm, ba, bb)


def _merge_bwd(dmerged, ya, yb, pm, ba, bb, name):
    S, D = ya.shape
    T = _tile(S, ROW_T)

    def body(dm_ref, ya_ref, yb_ref, ga_ref, gb_ref, ba_ref, bb_ref,
             dya_ref, dyb_ref, dga_ref, dgb_ref, dba_ref, dbb_ref):
        dm = dm_ref[...]
        sa = jax.nn.sigmoid(ga_ref[...] + ba_ref[...])
        sb = jax.nn.sigmoid(gb_ref[...] + bb_ref[...])
        dya_ref[...] = (dm * sa).astype(dya_ref.dtype)
        dyb_ref[...] = (dm * sb).astype(dyb_ref.dtype)
        dga = dm * ya_ref[...] * sa * (1.0 - sa)
        dgb = dm * yb_ref[...] * sb * (1.0 - sb)
        dga_ref[...] = dga.astype(dga_ref.dtype)
        dgb_ref[...] = dgb.astype(dgb_ref.dtype)

        @pl.when(pl.program_id(0) == 0)
        def _():
            dba_ref[...] = jnp.zeros_like(dba_ref)
            dbb_ref[...] = jnp.zeros_like(dbb_ref)

        dba_ref[...] += jnp.sum(dga, axis=0, keepdims=True)
        dbb_ref[...] += jnp.sum(dgb, axis=0, keepdims=True)

    row = pl.BlockSpec((T, D), lambda i: (i, 0))
    vec = pl.BlockSpec((1, D), lambda i: (0, 0))
    act = jax.ShapeDtypeStruct((S, D), BF16)
    v1 = jax.ShapeDtypeStruct((1, D), F32)
    return pl.pallas_call(
        body, name=name, out_shape=(act, act, act, act, v1, v1), grid=(S // T,),
        in_specs=[row, row, row, pl.BlockSpec((T, D), lambda i: (i, 0)),
                  pl.BlockSpec((T, D), lambda i: (i, 1)), vec, vec],
        out_specs=(row, row, row, row, vec, vec), compiler_params=_cparams(("arbitrary",)),
    )(dmerged, ya, yb, pm, pm, ba, bb)


_GELU_C = math.sqrt(2.0 / math.pi)


_GELU_K = 0.044715


def _gelu(g):
    u = 0.5 * jnp.tanh(g * (_GELU_C + (_GELU_C * _GELU_K) * (g * g))) + 0.5
    return g * u, u


def _gelu_grad(g, u):
    return u * (1.0 + g * (1.0 - u) * (2 * _GELU_C + (6 * _GELU_C * _GELU_K) * (g * g)))


def _shift_down(v, halo_ref, first, rows):
    T = v.shape[0]
    keep = jnp.where(first, 0.0, 1.0)
    h7 = halo_ref[7:8, :] * keep
    h6 = halo_ref[6:7, :] * keep
    m1 = jnp.where(rows == 0, h7, pltpu.roll(v, 1, 0))
    m2 = jnp.where(rows == 0, h6, jnp.where(rows == 1, h7, pltpu.roll(v, 2, 0)))
    return m1, m2


def _conv_act_fwd(up, cw, cb, name):
    S, F2 = up.shape
    Fh = F2 // 2
    T = _tile(S, ROW_T)
    tc = _tile(Fh, CONV_TC)
    ncol = Fh // tc
    hb = T // 8

    def body(ua_ref, ug_ref, ha_ref, hg_ref, wa_ref, wg_ref, ba_ref, bg_ref, o_ref, a_ref, g_ref):
        first = pl.program_id(0) == 0
        rows = lax.broadcasted_iota(jnp.int32, (T, tc), 0)

        def conv(u_ref, h_ref, w_ref, b_ref):
            v = u_ref[...]
            m1, m2 = _shift_down(v, h_ref, first, rows)
            return b_ref[...] + w_ref[0:1, :] * m2 + w_ref[1:2, :] * m1 + w_ref[2:3, :] * v

        a = conv(ua_ref, ha_ref, wa_ref, ba_ref)
        g = conv(ug_ref, hg_ref, wg_ref, bg_ref)
        a_ref[...] = a
        g_ref[...] = g
        o_ref[...] = (_gelu(g)[0] * a).astype(o_ref.dtype)

    halo = lambda off: pl.BlockSpec((8, tc), lambda i, j: (jnp.maximum(i * hb - 1, 0), j + off))
    blk = pl.BlockSpec((T, tc), lambda i, j: (i, j))
    f32 = jax.ShapeDtypeStruct((S, Fh), F32)
    return pl.pallas_call(
        body, name=name, out_shape=(jax.ShapeDtypeStruct((S, Fh), BF16), f32, f32), grid=(S // T, ncol),
        in_specs=[blk, pl.BlockSpec((T, tc), lambda i, j: (i, j + ncol)),
                  halo(0), halo(ncol),
                  pl.BlockSpec((3, tc), lambda i, j: (0, j)), pl.BlockSpec((3, tc), lambda i, j: (0, j + ncol)),
                  pl.BlockSpec((1, tc), lambda i, j: (0, j)), pl.BlockSpec((1, tc), lambda i, j: (0, j + ncol))],
        out_specs=(blk, blk, blk),
        compiler_params=_cparams(("parallel", "parallel")),
    )(up, up, up, up, cw, cw, cb, cb)


def _conv_act_bwd(up, a, g, dact, cw, name):
    S, F2 = up.shape
    Fh = F2 // 2
    T = _tile(S, ROW_T)
    tc = _tile(Fh, CONV_TC)
    ncol, nrow, hb, nhb = Fh // tc, S // T, T // 8, S // 8

    def body(ua_ref, ug_ref, a_ref, g_ref, an_ref, gn_ref, wa_ref, wg_ref, da_ref, dn_ref,
             dpa_ref, dpg_ref, dwa_ref, dwg_ref, dba_ref, dbg_ref, dua_n, dug_n):
        i = pl.program_id(1)
        rows = lax.broadcasted_iota(jnp.int32, (T, tc), 0)

        def du_of(a, g, dact_v):
            gel, t = _gelu(g)
            return dact_v * gel, dact_v * a * _gelu_grad(g, t)

        dua, dug = du_of(a_ref[...], g_ref[...], da_ref[...])
        keep = jnp.where(i == nrow - 1, 0.0, 1.0)
        dua_n[...], dug_n[...] = du_of(an_ref[...], gn_ref[...], dn_ref[...] * keep)

        @pl.when(i == 0)
        def _():
            for r in (dwa_ref, dwg_ref, dba_ref, dbg_ref):
                r[...] = jnp.zeros_like(r)

        for du, n_ref, u_ref, w_ref, o_ref, dw_ref, db_ref in ((dua, dua_n, ua_ref, wa_ref, dpa_ref, dwa_ref, dba_ref),
                                                               (dug, dug_n, ug_ref, wg_ref, dpg_ref, dwg_ref, dbg_ref)):
            n0, n1 = n_ref[0:1, :], n_ref[1:2, :]
            du1 = jnp.where(rows == T - 1, n0, pltpu.roll(du, T - 1, 0))
            du2 = jnp.where(rows == T - 2, n0, jnp.where(rows == T - 1, n1, pltpu.roll(du, T - 2, 0)))
            o_ref[...] = (w_ref[2:3, :] * du + w_ref[1:2, :] * du1 + w_ref[0:1, :] * du2).astype(o_ref.dtype)
            u = u_ref[...]
            db_ref[...] += jnp.sum(du, axis=0, keepdims=True)
            for j, d in enumerate((du2, du1, du)):
                dw_ref[j:j + 1, :] += jnp.sum(d * u, axis=0, keepdims=True)

    tile = lambda off: pl.BlockSpec((T, tc), lambda j, i: (i, j + off))
    under = pl.BlockSpec((8, tc), lambda j, i: (jnp.minimum((i + 1) * hb, nhb - 1), j))
    vec = lambda n, off: pl.BlockSpec((n, tc), lambda j, i: (0, j + off))
    act = jax.ShapeDtypeStruct((S, Fh), BF16)
    return pl.pallas_call(
        body, name=name,
        out_shape=(act, act, jax.ShapeDtypeStruct((3, Fh), F32), jax.ShapeDtypeStruct((3, Fh), F32),
                   jax.ShapeDtypeStruct((1, Fh), F32), jax.ShapeDtypeStruct((1, Fh), F32)),
        grid=(ncol, nrow),
        in_specs=[tile(0), tile(ncol), tile(0), tile(0), under, under, vec(3, 0), vec(3, ncol), tile(0), under],
        out_specs=(tile(0), tile(0), vec(3, 0), vec(3, 0), vec(1, 0), vec(1, 0)),
        scratch_shapes=[pltpu.VMEM((8, tc), F32), pltpu.VMEM((8, tc), F32)],
        compiler_params=_cparams(("parallel", "arbitrary")),
    )(up, up, a, g, a, g, cw, cw, dact, dact)


def _split3(x):
    hi = x.astype(BF16)
    r1 = x - hi.astype(F32)
    mid = r1.astype(BF16)
    lo = (r1 - mid.astype(F32)).astype(BF16)
    return hi, mid, lo


def _tri_dot(tri, x):
    hi, mid, lo = _split3(x)
    return _dot_nn(tri, hi) + _dot_nn(tri, mid) + _dot_nn(tri, lo)


def _log_sigmoid(x):
    return jnp.minimum(x, 0.0) - jnp.log(1.0 + jnp.exp(-jnp.abs(x)))


def _tri_mask(n, lower):
    r = lax.broadcasted_iota(jnp.int32, (n, n), 0)
    c = lax.broadcasted_iota(jnp.int32, (n, n), 1)
    return (r >= c) if lower else (r <= c)


def _gates_fwd(ps, bi, bf, bff, name):
    S = ps.shape[0]
    NC = S // MLC

    def body(ps_ref, bi_ref, bf_ref, bff_ref, a_ref, A_ref, wi_ref, em_ref, wk_ref, dec_ref, F_ref, m_scr, f_scr):
        @pl.when(pl.program_id(0) == 0)
        def _():
            m_scr[...] = jnp.zeros_like(m_scr)
            f_scr[...] = jnp.zeros_like(f_scr)

        rows = lax.broadcasted_iota(jnp.int32, (MLC, LANES), 0)
        ltri = _tri_mask(MLC, True).astype(BF16)
        li = GATE_CAP * jnp.tanh((ps_ref[:, 0:LANES] + bi_ref[...]) / GATE_CAP)
        lf = _log_sigmoid(GATE_CAP * jnp.tanh((ps_ref[:, LANES:2 * LANES] + bf_ref[...]) / GATE_CAP))
        b = _tri_dot(ltri, lf)
        a = li - b
        cm = a
        sh = 1
        while sh < MLC:
            cm = jnp.where(rows >= sh, jnp.maximum(cm, pltpu.roll(cm, sh, 0)), cm)
            sh *= 2
        m0 = m_scr[...]
        A = jnp.maximum(cm, m0)
        a_ref[...] = a
        A_ref[...] = A
        A_last = A_ref[MLC - 1:MLC, :]
        wi_ref[...] = jnp.exp(m0 - A)
        em_ref[...] = jnp.exp(-(b + A))
        wk_ref[...] = jnp.exp(a - A_last)
        dec_ref[0] = jnp.exp(m0 - A_last)
        F_ref[...] = b
        m_scr[...] = F_ref[MLC - 1:MLC, :] + A_last
        lfg = _log_sigmoid(ps_ref[:, 2 * LANES:3 * LANES] + bff_ref[...])
        F_ref[...] = _tri_dot(ltri, lfg) + f_scr[...]
        f_scr[...] = F_ref[MLC - 1:MLC, :]

    col = pl.BlockSpec((MLC, LANES), lambda c: (c, 0))
    vec = pl.BlockSpec((1, LANES), lambda c: (0, 0))
    cs = jax.ShapeDtypeStruct((S, LANES), F32)
    return pl.pallas_call(
        body, name=name,
        out_shape=(cs, cs, cs, cs, cs, jax.ShapeDtypeStruct((NC, 1, LANES), F32), cs),
        grid=(NC,), in_specs=[pl.BlockSpec((MLC, N_SMALL), lambda c: (c, 0)), vec, vec, vec],
        out_specs=(col, col, col, col, col, pl.BlockSpec((1, 1, LANES), lambda c: (c, 0, 0)), col),
        scratch_shapes=[pltpu.VMEM((1, LANES), F32), pltpu.VMEM((1, LANES), F32)],
        compiler_params=_cparams(("arbitrary",)),
    )(ps, bi, bf, bff)


def _gates_bwd(ps, bi, bf, bff, rk, kc, tch, dF, name):
    S = ps.shape[0]
    NC = S // MLC

    def body(ps_ref, bi_ref, bf_ref, bff_ref, rk_ref, kc_ref, t_ref, dF_ref, dps_ref, db_ref, carry):
        @pl.when(pl.program_id(0) == 0)
        def _():
            carry[...] = jnp.zeros_like(carry)
            db_ref[...] = jnp.zeros_like(db_ref)

        lanes = lax.broadcasted_iota(jnp.int32, (MLC, LANES), 1)
        utri = _tri_mask(MLC, False).astype(BF16)
        ti = jnp.tanh((ps_ref[:, 0:LANES] + bi_ref[...]) / GATE_CAP)
        t_end, t_start = t_ref[0, 0:1, :], t_ref[0, 1:2, :]
        rk = rk_ref[...]
        rk = rk - (jnp.sum(rk, axis=0, keepdims=True) - (t_start - t_end)) * (1.0 / MLC)
        dpi = jnp.where(lanes < ML_HEADS, (kc_ref[...] - rk) * (1.0 - ti * ti), 0.0)
        tf = jnp.tanh((ps_ref[:, LANES:2 * LANES] + bf_ref[...]) / GATE_CAP)
        dlf = _tri_dot(utri, rk) + t_end
        dpf = jnp.where(lanes < ML_HEADS, dlf * jax.nn.sigmoid(-GATE_CAP * tf) * (1.0 - tf * tf), 0.0)
        dFv = dF_ref[...]
        dlfg = _tri_dot(utri, dFv) + carry[...]
        carry[...] += jnp.sum(dFv, axis=0, keepdims=True)
        dpff = jnp.where(lanes < FOX_HEADS, dlfg * jax.nn.sigmoid(-(ps_ref[:, 2 * LANES:3 * LANES] + bff_ref[...])), 0.0)
        for n, dp in enumerate((dpi, dpf, dpff)):
            dps_ref[:, n * LANES:(n + 1) * LANES] = dp.astype(dps_ref.dtype)
            db_ref[:, n * LANES:(n + 1) * LANES] += jnp.sum(dp, axis=0, keepdims=True)

    rev = lambda c: (NC - 1 - c, 0)
    col = pl.BlockSpec((MLC, LANES), rev)
    vec = pl.BlockSpec((1, LANES), lambda c: (0, 0))
    wide = pl.BlockSpec((MLC, N_SMALL), rev)
    return pl.pallas_call(
        body, name=name,
        out_shape=(jax.ShapeDtypeStruct((S, N_SMALL), BF16), jax.ShapeDtypeStruct((1, N_SMALL), F32)),
        grid=(NC,),
        in_specs=[wide, vec, vec, vec, col, col, pl.BlockSpec((1, 2, LANES), lambda c: (NC - 1 - c, 0, 0)), col],
        out_specs=(wide, pl.BlockSpec((1, N_SMALL), lambda c: (0, 0))),
        scratch_shapes=[pltpu.VMEM((1, LANES), F32)],
        compiler_params=_cparams(("arbitrary",)),
    )(ps, bi, bf, bff, rk, kc, tch, dF)


_ML_SCALE = ML_DQK ** -0.5


def _ml_specs(rev, NC):
    idx = (lambda c: NC - 1 - c) if rev else (lambda c: c)
    qk = lambda blk: pl.BlockSpec((MLC, ML_HEADS * ML_DQK), lambda c: (idx(c), blk))
    wide = lambda blk: pl.BlockSpec((MLC, D_MODEL), lambda c: (idx(c), blk))
    col = pl.BlockSpec((MLC, LANES), lambda c: (idx(c), 0))
    return idx, qk, wide, col


def _ml_intra(q_ref, k_ref, arow_ref, A_ref, h):
    hs = slice(h * ML_DQK, (h + 1) * ML_DQK)
    qf = q_ref[:, hs] * _ML_SCALE
    kf = k_ref[:, hs]
    qb, kb = qf.astype(BF16), kf.astype(BF16)
    qk = _dot_nt(qb, kb)
    logw = arow_ref[h:h + 1, :] - A_ref[:, h:h + 1]
    W = jnp.exp(jnp.where(_tri_mask(MLC, True), logw, -1e30))
    return qb, kb, qf, kf, qk, W


def _mlstm_fwd(pm, a_row, A, wi, em, wk, dec, w_hn, name):
    S = pm.shape[0]
    NC = S // MLC
    _, qk, wide, col = _ml_specs(False, NC)

    def body(q_ref, k_ref, v_ref, o_ref, arow_ref, A_ref, wi_ref, em_ref, wk_ref, dec_ref, whn_ref,
             ha_ref, hp_ref, den_ref, cst_ref, nst_ref, C_scr, n_scr):
        @pl.when(pl.program_id(0) == 0)
        def _():
            C_scr[...] = jnp.zeros_like(C_scr)
            n_scr[...] = jnp.zeros_like(n_scr)

        lanes = lax.broadcasted_iota(jnp.int32, (MLC, LANES), 1)
        den_tile = jnp.zeros((MLC, LANES), F32)
        for h in range(ML_HEADS):
            vs = slice(h * ML_DV, (h + 1) * ML_DV)
            qb, kb, qf, kf, qk_, W = _ml_intra(q_ref, k_ref, arow_ref, A_ref, h)
            vb = v_ref[:, vs].astype(BF16)
            Cf = C_scr[h]
            Cb = Cf.astype(BF16)
            nrow = n_scr[h]
            cst_ref[0, h] = Cb
            nst_ref[0, h] = nrow
            s = qk_ * W
            wic = wi_ref[:, h:h + 1]
            num = _dot_nn(s.astype(BF16), vb) + wic * _dot_nt(qb, Cb)
            den = jnp.sum(s, axis=1, keepdims=True) + wic * jnp.sum(qf * nrow, axis=1, keepdims=True)
            hp = num / jnp.maximum(jnp.abs(den), em_ref[:, h:h + 1])
            hp_ref[:, vs] = hp
            den_tile = jnp.where(lanes == h, den, den_tile)
            hn = hp * _rstd(hp) * whn_ref[:, vs]
            ha_ref[:, vs] = (hn * jax.nn.sigmoid(o_ref[:, vs])).astype(ha_ref.dtype)
            wkc = wk_ref[:, h:h + 1]
            kw = kf * wkc
            d = dec_ref[0, :, h:h + 1]
            C_scr[h] = d * Cf + _dot_tn(vb, kw.astype(BF16))
            n_scr[h] = d * nrow + jnp.sum(kw, axis=0, keepdims=True)
        den_ref[...] = den_tile

    return pl.pallas_call(
        body, name=name,
        out_shape=(jax.ShapeDtypeStruct((S, D_MODEL), BF16), jax.ShapeDtypeStruct((S, D_MODEL), F32),
                   jax.ShapeDtypeStruct((S, LANES), F32),
                   jax.ShapeDtypeStruct((NC, ML_HEADS, ML_DV, ML_DQK), BF16),
                   jax.ShapeDtypeStruct((NC, ML_HEADS, 1, ML_DQK), F32)),
        grid=(NC,),
        in_specs=[qk(C_QM // 512), qk(C_KM // 512), wide(C_VM // D_MODEL), wide(C_OM // D_MODEL),
                  pl.BlockSpec((8, MLC), lambda c: (0, c)), col, col, col, col,
                  pl.BlockSpec((1, 1, LANES), lambda c: (c, 0, 0)), pl.BlockSpec((1, D_MODEL), lambda c: (0, 0))],
        out_specs=(pl.BlockSpec((MLC, D_MODEL), lambda c: (c, 0)), pl.BlockSpec((MLC, D_MODEL), lambda c: (c, 0)),
                   col, pl.BlockSpec((1, ML_HEADS, ML_DV, ML_DQK), lambda c: (c, 0, 0, 0)),
                   pl.BlockSpec((1, ML_HEADS, 1, ML_DQK), lambda c: (c, 0, 0, 0))),
        scratch_shapes=[pltpu.VMEM((ML_HEADS, ML_DV, ML_DQK), F32), pltpu.VMEM((ML_HEADS, 1, ML_DQK), F32)],
        compiler_params=_cparams(("arbitrary",)),
    )(pm, pm, pm, pm, a_row, A, wi, em, wk, dec, w_hn)


def _mlstm_bwd(dha, pm, hp_all, den_all, a_row, A, wi, em, wk, dec, cst, nst, w_hn, name):
    S = pm.shape[0]
    NC = S // MLC
    idx, qk, wide, col = _ml_specs(True, NC)

    def body(dha_ref, q_ref, k_ref, v_ref, o_ref, hp_ref, den_ref, arow_ref, A_ref, wi_ref, em_ref, wk_ref,
             dec_ref, cst_ref, nst_ref, whn_ref,
             dqk_ref, dv_ref, do_ref, rk_ref, kc_ref, t_ref, dwhn_ref, dC_scr, dn_scr, t_scr):
        @pl.when(pl.program_id(0) == 0)
        def _():
            dC_scr[...] = jnp.zeros_like(dC_scr)
            dn_scr[...] = jnp.zeros_like(dn_scr)
            t_scr[...] = jnp.zeros_like(t_scr)
            dwhn_ref[...] = jnp.zeros_like(dwhn_ref)

        lanes = lax.broadcasted_iota(jnp.int32, (MLC, LANES), 1)
        lane1 = lax.broadcasted_iota(jnp.int32, (1, LANES), 1)
        t_ref[0, 0:1, :] = t_scr[...]
        rk_tile = jnp.zeros((MLC, LANES), F32)
        kc_tile = jnp.zeros((MLC, LANES), F32)
        t_new = jnp.zeros((1, LANES), F32)
        for h in range(ML_HEADS):
            hs = slice(h * ML_DQK, (h + 1) * ML_DQK)
            vs = slice(h * ML_DV, (h + 1) * ML_DV)
            hp = hp_ref[:, vs]
            sig = jax.nn.sigmoid(o_ref[:, vs])
            whn = whn_ref[:, vs]
            r = _rstd(hp)
            dga = dha_ref[:, vs]
            do_ref[:, vs] = (dga * (hp * r * whn) * sig * (1.0 - sig)).astype(do_ref.dtype)
            dhn = dga * sig
            dhp, dwt = _rmsnorm_bwd_math(dhn, hp, whn)
            dwhn_ref[:, vs] += jnp.sum(dwt, axis=0, keepdims=True)
            den = den_ref[:, h:h + 1]
            floor = em_ref[:, h:h + 1]
            D = jnp.maximum(jnp.abs(den), floor)
            dnum = dhp / D
            dh_h = jnp.sum(dhp * hp, axis=1, keepdims=True)
            active = jnp.abs(den) >= floor
            dden = -dh_h / D * jnp.where(active, jnp.sign(den), 0.0)
            phi = jnp.where(active, 0.0, dh_h)
            qb, kb, qf, kf, qk_, W = _ml_intra(q_ref, k_ref, arow_ref, A_ref, h)
            vf = v_ref[:, vs]
            vb = vf.astype(BF16)
            Cb = cst_ref[0, h]
            nrow = nst_ref[0, h]
            wic = wi_ref[:, h:h + 1]
            wkc = wk_ref[:, h:h + 1]
            d = dec_ref[0, :, h:h + 1]
            dCn = dC_scr[h]
            dCb = dCn.astype(BF16)
            dnn = dn_scr[h]
            dnumb = dnum.astype(BF16)
            s = qk_ * W
            ds = (_dot_nt(dnumb, vb) + dden) * W
            dsb = ds.astype(BF16)
            dnw = (wic * dnum).astype(BF16)
            wd = wic * dden
            kw = kf * wkc
            dv_state = _dot_nt(kw.astype(BF16), dCb)
            dq = _dot_nn(dsb, kb) + _dot_nn(dnw, Cb) + wd * nrow
            dk_state = wkc * (_dot_nn(vb, dCb) + dnn)
            dk = _dot_tn(dsb, qb) + dk_state
            dv = _dot_tn(s.astype(BF16), dnumb) + dv_state
            dC = d * dCn + _dot_tn(dnw, qb)
            dn = d * dnn + jnp.sum(wd * qf, axis=0, keepdims=True)
            dC_scr[h] = dC
            dn_scr[h] = dn
            dqk_ref[:, hs] = (dq * _ML_SCALE).astype(dqk_ref.dtype)
            dqk_ref[:, C_KM + h * ML_DQK:C_KM + (h + 1) * ML_DQK] = dk.astype(dqk_ref.dtype)
            dv_ref[:, vs] = dv.astype(dv_ref.dtype)
            G = ds * qk_
            inter = _dot_nt(qb, Cb)
            qn = jnp.sum(qf * nrow, axis=1, keepdims=True)
            R = (jnp.sum(G, axis=1, keepdims=True)
                 + wic * (jnp.sum(dnum * inter, axis=1, keepdims=True) + dden * qn))
            K = jnp.sum(G.T, axis=1, keepdims=True) + jnp.sum(kf * dk_state, axis=1, keepdims=True)
            rk_tile = jnp.where(lanes == h, R - K, rk_tile)
            kc_tile = jnp.where(lanes == h, phi, kc_tile)
            tt = (jnp.sum(jnp.sum(dC * Cb.astype(F32), axis=1, keepdims=True), axis=0, keepdims=True)
                  + jnp.sum(dn * nrow, axis=1, keepdims=True))
            t_new = jnp.where(lane1 == h, tt, t_new)
        rk_ref[...] = rk_tile
        kc_ref[...] = kc_tile
        t_ref[0, 1:2, :] = t_new
        t_scr[...] = t_new

    act = lambda n: jax.ShapeDtypeStruct((S, n), BF16)
    cs = jax.ShapeDtypeStruct((S, LANES), F32)
    rowblk = lambda n: pl.BlockSpec((MLC, n), lambda c: (idx(c), 0))
    return pl.pallas_call(
        body, name=name,
        out_shape=(act(D_MODEL), act(D_MODEL), act(D_MODEL), cs, cs,
                   jax.ShapeDtypeStruct((NC, 2, LANES), F32), jax.ShapeDtypeStruct((1, D_MODEL), F32)),
        grid=(NC,),
        in_specs=[rowblk(D_MODEL), qk(C_QM // 512), qk(C_KM // 512), wide(C_VM // D_MODEL), wide(C_OM // D_MODEL),
                  rowblk(D_MODEL), col, pl.BlockSpec((8, MLC), lambda c: (0, idx(c))), col, col, col, col,
                  pl.BlockSpec((1, 1, LANES), lambda c: (idx(c), 0, 0)),
                  pl.BlockSpec((1, ML_HEADS, ML_DV, ML_DQK), lambda c: (idx(c), 0, 0, 0)),
                  pl.BlockSpec((1, ML_HEADS, 1, ML_DQK), lambda c: (idx(c), 0, 0, 0)),
                  pl.BlockSpec((1, D_MODEL), lambda c: (0, 0))],
        out_specs=(rowblk(D_MODEL), rowblk(D_MODEL), rowblk(D_MODEL), col, col,
                   pl.BlockSpec((1, 2, LANES), lambda c: (idx(c), 0, 0)), pl.BlockSpec((1, D_MODEL), lambda c: (0, 0))),
        scratch_shapes=[pltpu.VMEM((ML_HEADS, ML_DV, ML_DQK), F32), pltpu.VMEM((ML_HEADS, 1, ML_DQK), F32),
                        pltpu.VMEM((1, LANES), F32)],
        compiler_params=_cparams(("arbitrary",)),
    )(dha, pm, pm, pm, pm, hp_all, den_all, a_row, A, wi, em, wk, dec, cst, nst, w_hn)


_FOX_SCALE = FOX_DH ** -0.5
_NEG = -1e30
_LOG2E = 1.4426950408889634
_LN2 = 0.6931471805599453
_QF_BLK, _KF_BLK, _VF_BLK = 0, FOX_HEADS, 2 * FOX_HEADS


def _lane_pick(tile, lane):
    lanes = lax.broadcasted_iota(jnp.int32, tile.shape, 1)
    return jnp.sum(jnp.where(lanes == lane, tile, 0.0), axis=1, keepdims=True)


def _col_to_row(col):
    return jnp.max(jnp.broadcast_to(col, (col.shape[0], LANES)).T, axis=0, keepdims=True)


def _causal(q0, k0, shape, q_axis):
    qpos = q0 + lax.broadcasted_iota(jnp.int32, shape, q_axis)
    kpos = k0 + lax.broadcasted_iota(jnp.int32, shape, 1 - q_axis)
    return kpos <= qpos


def _fox_fwd(pf, fc, fk_row, name):
    S = pf.shape[0]
    TQ, TK = FOX_TQ_FWD, FOX_TK_FWD
    nq, nk = S // TQ, S // TK
    c1 = _FOX_SCALE * _LOG2E

    def body(q_ref, k_ref, v_ref, fc_ref, fr_ref, o_ref, lse_ref):
        h, i = pl.program_id(0), pl.program_id(1)
        qb = q_ref[...]
        fq2 = _lane_pick(fc_ref[...], h) * _LOG2E

        def step(j, carry, masked):
            m, l, acc = carry
            off = pl.multiple_of(j * TK, TK)
            t = _dot_nt(qb, k_ref[pl.ds(off, TK), :]) * c1 - fr_ref[0, j] * _LOG2E
            if masked:
                t = jnp.where(_causal(i * TQ, j * TK, (TQ, TK), 0), t, _NEG)
            m_new = jnp.maximum(m, jnp.max(t, axis=1, keepdims=True) + fq2)
            alpha = jnp.exp2(m - m_new)
            p = jnp.exp2(t + (fq2 - m_new))
            l = alpha * l + jnp.sum(p, axis=1, keepdims=True)
            acc = alpha * acc + _dot_nn(p.astype(BF16), v_ref[pl.ds(off, TK), :])
            return m_new, l, acc

        init = (jnp.full((TQ, 1), _NEG, F32), jnp.zeros((TQ, 1), F32), jnp.zeros((TQ, FOX_DH), F32))
        last = (i * TQ) // TK
        carry = lax.fori_loop(0, last, lambda j, c: step(j, c, False), init)
        for d in range(max(1, TQ // TK)):
            carry = step(last + d, carry, True)
        m, l, acc = carry
        o_ref[...] = (acc / l).astype(o_ref.dtype)
        lse_ref[0, 0] = _col_to_row((m + jnp.log2(l)) * _LN2)

    head = lambda blk: pl.BlockSpec((S, FOX_DH), lambda h, i: (0, blk + h))
    return pl.pallas_call(
        body, name=name,
        out_shape=(jax.ShapeDtypeStruct((S, D_MODEL), BF16), jax.ShapeDtypeStruct((FOX_HEADS, nq, 1, TQ), F32)),
        grid=(FOX_HEADS, nq),
        in_specs=[pl.BlockSpec((TQ, FOX_DH), lambda h, i: (i, _QF_BLK + h)), head(_KF_BLK), head(_VF_BLK),
                  pl.BlockSpec((TQ, LANES), lambda h, i: (i, 0)),
                  pl.BlockSpec((1, nk, 1, TK), lambda h, i: (h, 0, 0, 0))],
        out_specs=(pl.BlockSpec((TQ, FOX_DH), lambda h, i: (i, h)),
                   pl.BlockSpec((1, 1, 1, TQ), lambda h, i: (h, i, 0, 0))),
        compiler_params=_cparams(("parallel", "arbitrary")),
    )(pf, pf, pf, fc, fk_row)


def _fox_bwd(dhb, hb, pf, lse_row, fq_row, fc, name):
    S = pf.shape[0]
    TQ, TK = FOX_TQ, FOX_TK
    nq, nk, r = S // TQ, S // TK, TK // TQ
    c1 = _FOX_SCALE * _LOG2E

    def body(q_ref, k_ref, v_ref, do_ref, o_ref, lse_ref, fq_ref, fc_ref,
             dq_ref, dk_ref, dv_ref, dFk_ref, dFq_ref, dq_acc, qside, delta, dk_acc, dv_acc, cs_acc):
        h, j = pl.program_id(0), pl.program_id(1)

        @pl.when(j == 0)
        def _():
            dq_acc[...] = jnp.zeros_like(dq_acc)
            dFq_ref[...] = jnp.zeros_like(dFq_ref)

            def fill(b, _):
                off = pl.multiple_of(b * TQ, TQ)
                prod = do_ref[pl.ds(off, TQ), :].astype(F32) * o_ref[pl.ds(off, TQ), :].astype(F32)
                delta[b] = jnp.sum(prod.T, axis=0, keepdims=True)
                qside[b] = (fq_ref[0, b] - lse_ref[0, b]) * _LOG2E
                return 0

            lax.fori_loop(0, nq, fill, 0)

        kb = k_ref[...]
        vb = v_ref[...]
        fk2 = _lane_pick(fc_ref[...], h) * _LOG2E
        dk_acc[...] = jnp.zeros_like(dk_acc)
        dv_acc[...] = jnp.zeros_like(dv_acc)
        cs_acc[...] = jnp.zeros_like(cs_acc)

        def step(i, masked):
            off = pl.multiple_of(i * TQ, TQ)
            qb = q_ref[pl.ds(off, TQ), :]
            dob = do_ref[pl.ds(off, TQ), :]
            t = _dot_nt(kb, qb) * c1 + qside[i] - fk2
            if masked:
                t = jnp.where(_causal(i * TQ, j * TK, (TK, TQ), 1), t, _NEG)
            p = jnp.exp2(t)
            dv_acc[...] += _dot_nn(p.astype(BF16), dob)
            ds = p * (_dot_nt(vb, dob) - delta[i])
            dsb = ds.astype(BF16)
            dk_acc[...] += _dot_nn(dsb, qb)
            dq_acc[pl.ds(off, TQ), :] += _dot_tn(dsb, kb)
            cs_acc[...] += jnp.sum(ds, axis=1, keepdims=True)
            dFq_ref[0, i] += jnp.sum(ds, axis=0, keepdims=True)

        for d in range(r):
            step(r * j + d, True)

        def rest(i, _):
            step(i, False)
            return 0

        lax.fori_loop(r * j + r, nq, rest, 0)
        dk_ref[...] = (dk_acc[...] * _FOX_SCALE).astype(dk_ref.dtype)
        dv_ref[...] = dv_acc[...].astype(dv_ref.dtype)
        dFk_ref[0, 0] = -_col_to_row(cs_acc[...])

        @pl.when(j == nk - 1)
        def _():
            dq_ref[...] = (dq_acc[...] * _FOX_SCALE).astype(dq_ref.dtype)

    head = lambda blk: pl.BlockSpec((S, FOX_DH), lambda h, j: (0, blk + h))
    kblk = lambda blk: pl.BlockSpec((TK, FOX_DH), lambda h, j: (j, blk + h))
    qrows = pl.BlockSpec((1, nq, 1, TQ), lambda h, j: (h, 0, 0, 0))
    act = jax.ShapeDtypeStruct((S, D_MODEL), BF16)
    return pl.pallas_call(
        body, name=name,
        out_shape=(act, act, act, jax.ShapeDtypeStruct((FOX_HEADS, nk, 1, TK), F32),
                   jax.ShapeDtypeStruct((FOX_HEADS, nq, 1, TQ), F32)),
        grid=(FOX_HEADS, nk),
        in_specs=[head(_QF_BLK), kblk(_KF_BLK), kblk(_VF_BLK), head(0), head(0), qrows, qrows,
                  pl.BlockSpec((TK, LANES), lambda h, j: (j, 0))],
        out_specs=(head(0), kblk(0), kblk(0), pl.BlockSpec((1, 1, 1, TK), lambda h, j: (h, j, 0, 0)), qrows),
        scratch_shapes=[pltpu.VMEM((S, FOX_DH), F32), pltpu.VMEM((nq, 1, TQ), F32), pltpu.VMEM((nq, 1, TQ), F32),
                        pltpu.VMEM((TK, FOX_DH), F32), pltpu.VMEM((TK, FOX_DH), F32), pltpu.VMEM((TK, 1), F32)],
        compiler_params=_cparams(("parallel", "arbitrary")),
    )(pf, pf, pf, dhb, hb, lse_row, fq_row, fc)


def _pad_lanes(v):
    return jnp.pad(v, ((0, 0), (0, LANES - v.shape[1])))


def _local_step(x, target, wmain_t, wsmall_t, rest_arrived, rest_weights, p, on_grads, advance, token):
    S = x.shape[0]
    bi, bf, bff = _pad_lanes(p["b_ml_i"]), _pad_lanes(p["b_ml_f"]), _pad_lanes(p["b_fox_f"])

    h0 = _rmsnorm_fwd(x, p["norm_mix_pre"] + token[0:1, 0:1], "norm_mix_pre")
    pm = _mm(h0, wmain_t, "nt", F32, "proj_mlstm", b_rows=(0, N_ML))
    pf = _mm(h0, wmain_t, "nt", BF16, "proj_fox", b_rows=(N_ML, N_FOX))
    pg = _mm(h0, wmain_t, "nt", F32, "proj_merge", b_rows=(N_ML + N_FOX, N_GATE))
    ps = _mm(h0, wsmall_t, "nt", F32, "proj_gates")
    a, A, wi, em, wk, dec, Fc = _gates_fwd(ps, bi, bf, bff, "gates_fwd")
    a_row = a[:, :8].T
    ha, hp, den, cst, nst = _mlstm_fwd(pm, a_row, A, wi, em, wk, dec, p["ml_head_norm"], "mlstm_fwd")
    ft = Fc[:, :FOX_HEADS].T + rest_arrived(ha)[0, 0]
    fq_row = ft.reshape(FOX_HEADS, S // FOX_TQ, 1, FOX_TQ)
    fk_row = ft.reshape(FOX_HEADS, S // FOX_TK, 1, FOX_TK)
    hb, lse_row = _fox_fwd(pf, Fc, ft.reshape(FOX_HEADS, S // FOX_TK_FWD, 1, FOX_TK_FWD), "fox_fwd")
    wa, wb, wout, wup, wdown = rest_weights(hb)
    ya = _mm(ha, wa, "nn", F32, "branch_a")
    yb = _mm(hb, wb, "nn", F32, "branch_b")
    merged = _merge_fwd(ya, yb, pg, p["b_gate_a"], p["b_gate_b"], "merge_fwd")
    z = _mm(merged, wout, "nn", F32, "out_proj")
    x1, h2 = _resid_norm_fwd(x, z, p["norm_mix_post"], p["norm_ffn_pre"], "resid_mix")
    up = _mm(h2, wup, "nn", F32, "ffn_up")
    act, conv_a, conv_g = _conv_act_fwd(up, p["conv_w"], p["conv_b"], "conv_act_fwd")
    d = _mm(act, wdown, "nn", F32, "ffn_down", tk=D_FF)
    loss_row, dy, dd, g_norm_ffn_post = _loss_head(x1, d, p["norm_ffn_post"], target, "loss_head")
    dact = _mm(dd, wdown, "nt", F32, "d_act")
    g_wdown = _mm(act, dd, "tn", F32, "dw_down", tm=1408, tk=2048)
    dupa, dupg, dcwa, dcwg, dcba, dcbg = _conv_act_bwd(up, conv_a, conv_g, dact, p["conv_w"], "conv_act_bwd")
    g_conv_w = jnp.concatenate([dcwa, dcwg], axis=1)
    g_conv_b = jnp.concatenate([dcba, dcbg], axis=1)
    dh2 = _mm_sum_parts([dupa, dupg], wup, F32, "d_h2", trans_b=True)
    g_wup = _mm(h2, [dupa, dupg], "tn", F32, "dw_up", tk=2048)
    token = on_grads("ffn", dict(w_up=g_wup, w_down=g_wdown))
    dx1, dz, g_norm_ffn_pre, g_norm_mix_post = _norm_chain_bwd(
        dh2, x1, p["norm_ffn_pre"] + token[0:1, 0:1], dy, z, p["norm_mix_post"], "norm_chain_bwd")
    dmerged = _mm(dz, wout, "nt", F32, "d_merged")
    g_wout = _mm(merged, dz, "tn", F32, "dw_out", tk=2048)
    dya, dyb, dga, dgb, g_b_gate_a, g_b_gate_b = _merge_bwd(dmerged, ya, yb, pg, p["b_gate_a"], p["b_gate_b"], "merge_bwd")
    dha = _mm(dya, wa, "nt", F32, "d_ha")
    g_wa = _mm(ha, dya, "tn", F32, "dw_a", tk=2048)
    dhb = _mm(dyb, wb, "nt", BF16, "d_hb")
    g_wb = _mm(hb, dyb, "tn", F32, "dw_b", tk=2048)
    token = advance("ffn", g_wb) + on_grads("mix", dict(w_out=g_wout, w_branch_a=g_wa, w_branch_b=g_wb))
    dqkm, dvm, dom, rk, kc, tch, g_ml_head_norm = _mlstm_bwd(
        dha, pm, hp, den, a_row, A, wi, em, wk, dec, cst, nst, p["ml_head_norm"] + token[0:1, 0:1], "mlstm_bwd")
    token = advance("mix", dqkm)
    dqf, dkf, dvf, dFk, dFq = _fox_bwd(dhb, hb, pf, lse_row.reshape(fq_row.shape), fq_row + token[0, 0], Fc, "fox_bwd")
    dF = jnp.pad((dFk.reshape(FOX_HEADS, S) + dFq.reshape(FOX_HEADS, S)).T, ((0, 0), (0, LANES - FOX_HEADS)))
    dps, dbias = _gates_bwd(ps, bi, bf, bff, rk, kc, tch, dF, "gates_bwd")
    dpm = [dqkm, dvm, dom, dqf, dkf, dvf, dga, dgb]
    g_wmain_t = _mm(dpm, h0, "tn", F32, "dw_main")
    token = on_grads("in", dict(w_in=g_wmain_t))
    g_wsmall_t = _mm(dps, h0, "tn", F32, "dw_gates")
    dh0s = _mm(dps, wsmall_t + token[0:1, 0:1].astype(BF16), "nn", F32, "d_h0_gates")
    token = advance("in", dh0s)
    dh0 = _mm_sum_parts(dpm, wmain_t, F32, "d_h0_main", after=token)
    grad_x, g_norm_mix_pre = _rmsnorm_bwd([dh0, dh0s], x, p["norm_mix_pre"], dx1, F32, "norm_mix_pre_bwd")

    big = dict(wsmall_t=g_wsmall_t)
    small = dict(norm_mix_pre=g_norm_mix_pre, ml_head_norm=g_ml_head_norm, b_gate_a=g_b_gate_a, b_gate_b=g_b_gate_b,
                 norm_mix_post=g_norm_mix_post, norm_ffn_pre=g_norm_ffn_pre, norm_ffn_post=g_norm_ffn_post,
                 conv_b=g_conv_b, b_ml_i=dbias[:, 0:ML_HEADS], b_ml_f=dbias[:, LANES:LANES + ML_HEADS],
                 b_fox_f=dbias[:, 2 * LANES:2 * LANES + FOX_HEADS], conv_w=g_conv_w)
    return loss_row, grad_x, big, small


def _row_tile(r, target=256):
    best = None
    for t in range(8, min(r, target) + 1, 8):
        if r % t == 0:
            best = t
    return best if best is not None else r


def _adamw(w, g, m, v, name):
    _, R, C = w.shape
    tr = _row_tile(R)
    tc = C
    if tr == R and R > 256:
        tc = 256

    def body(w_ref, g_ref, m_ref, v_ref, d_ref, mo_ref, vo_ref):
        gv = g_ref[...]
        mn = ADAM_B1 * m_ref[0] + (1.0 - ADAM_B1) * gv
        vn = ADAM_B2 * v_ref[0] + (1.0 - ADAM_B2) * (gv * gv)
        m_hat = mn / (1.0 - ADAM_B1 ** ADAM_STEP)
        v_hat = vn / (1.0 - ADAM_B2 ** ADAM_STEP)
        d_ref[0] = -ADAM_LR * (m_hat / (jnp.sqrt(v_hat) + ADAM_EPS) + ADAM_WD * w_ref[0])
        mo_ref[0] = mn
        vo_ref[0] = vn

    blk = pl.BlockSpec((1, tr, tc), lambda i, j: (0, i, j))
    o = jax.ShapeDtypeStruct((1, R, C), F32)
    return pl.pallas_call(
        body, name=name, out_shape=(o, o, o), grid=(R // tr, C // tc),
        in_specs=[blk, pl.BlockSpec((tr, tc), lambda i, j: (i, j)), blk, blk], out_specs=(blk,) * 3,
        compiler_params=_cparams(("parallel", "parallel")),
    )(w, g, m, v)


ANY = pl.BlockSpec(memory_space=pl.ANY)


def _place():
    x, y, c = lax.axis_index("x"), lax.axis_index("y"), lax.axis_index("c")
    chips = [(1 - x, y), (x, 1 - y), (1 - x, 1 - y)]
    return x, y, c, chips


def _block(ref, kind, k, rows=None):
    if kind == "rows":
        return ref.at[k] if rows is None else ref.at[k, pl.ds(*rows), :]
    cb = ref.shape[1] // 4
    return ref.at[:, pl.ds(k * cb, cb)] if rows is None else ref.at[pl.ds(*rows), pl.ds(k * cb, cb)]


def _gathered_shape(s, kind):
    return (4,) + s.shape if kind == "rows" else (s.shape[0], 4 * s.shape[1])


def _gather_weights(shards, kinds, smalls):
    n, ns = len(shards), len(smalls)

    def body(*refs):
        ins, sm_in = refs[:n], refs[n:n + ns]
        outs, sm_out = refs[n + ns:2 * n + ns], refs[2 * n + ns:2 * (n + ns)]
        send_sems, recv_sems, sm_send, sm_recv, local_sems = refs[2 * (n + ns):]
        x, y, c, chips = _place()
        sibling = (x, y, 1 - c)
        kme = 2 * x + y

        def half(a, k, hc):
            h = ins[a].shape[0] // 2
            return _block(outs[a], kinds[a], k, (hc * h, h))

        def remote(a, slot, src, dst, to):
            return pltpu.make_async_remote_copy(src_ref=src, dst_ref=dst, send_sem=send_sems.at[a * 7 + slot],
                                                recv_sem=recv_sems.at[a * 7 + slot], device_id=to, device_id_type=MESH)

        def sm_copy(b, j, k, to):
            return pltpu.make_async_remote_copy(src_ref=sm_in[b], dst_ref=sm_out[b].at[k], send_sem=sm_send.at[3 * b + j],
                                                recv_sem=sm_recv.at[3 * b + j], device_id=to, device_id_type=MESH)

        local = [pltpu.make_async_copy(sm_in[b], sm_out[b].at[kme], local_sems.at[b]) for b in range(ns)]
        for cp in local:
            cp.start()
        sends = [remote(a, 6, ins[a], _block(outs[a], kinds[a], kme), sibling) for a in range(n)]
        for a in range(n):
            h = ins[a].shape[0] // 2
            for j, chip in enumerate(chips):
                sends.append(remote(a, j, ins[a].at[pl.ds(c * h, h), :], half(a, kme, c), (*chip, c)))
        for b in range(ns):
            for j, chip in enumerate(chips):
                sends.append(sm_copy(b, j, kme, (*chip, c)))
        for cp in sends:
            cp.start()
        for a in range(n):
            for j, chip in enumerate(chips):
                kj = 2 * chip[0] + chip[1]
                remote(a, j, half(a, kj, c), half(a, kj, c), (*chip, c)).wait_recv()
                fwd = remote(a, 3 + j, half(a, kj, c), half(a, kj, c), sibling)
                fwd.start()
                sends.append(fwd)
        for a in range(n):
            for j, chip in enumerate(chips):
                kj = 2 * chip[0] + chip[1]
                remote(a, 3 + j, half(a, kj, 1 - c), half(a, kj, 1 - c), sibling).wait_recv()
        for b in range(ns):
            for j, chip in enumerate(chips):
                sm_copy(b, j, 2 * chip[0] + chip[1], (*chip, c)).wait_recv()
        for a in range(n):
            remote(a, 6, ins[a], _block(outs[a], kinds[a], kme), sibling).wait_recv()
        for cp in sends:
            cp.wait_send()
        for cp in local:
            cp.wait()

    outs = pl.pallas_call(
        body, name="gather_weights",
        out_shape=tuple([jax.ShapeDtypeStruct(_gathered_shape(s, k), s.dtype) for s, k in zip(shards, kinds)]
                        + [jax.ShapeDtypeStruct((4,) + s.shape, s.dtype) for s in smalls]),
        in_specs=[ANY] * (n + ns), out_specs=tuple([ANY] * (n + ns)),
        scratch_shapes=[pltpu.SemaphoreType.DMA((7 * n,)), pltpu.SemaphoreType.DMA((7 * n,)),
                        pltpu.SemaphoreType.DMA((3 * ns,)), pltpu.SemaphoreType.DMA((3 * ns,)),
                        pltpu.SemaphoreType.DMA((ns,))],
    )(*shards, *smalls)
    return outs[:n], outs[n:]


_IN_HBM = pl.BlockSpec(memory_space=pltpu.HBM)
_SEMS = pl.BlockSpec(memory_space=pltpu.SEMAPHORE)
_DATAFLOW = pltpu.SideEffectType.DATAFLOW_SIDE_EFFECTING


def _hbm(t):
    return pltpu.HBM(t.shape, t.dtype)


def _gather_copies(ins, outs, send_sems, recv_sems, kinds):
    x, y, c, chips = _place()
    kme = 2 * x + y
    cps = []
    for a in range(len(ins)):
        h = ins[a].shape[0] // 2
        for j, chip in enumerate(chips + [None]):
            to = (x, y, 1 - c) if chip is None else (*chip, c)
            src = ins[a] if chip is None else ins[a].at[pl.ds(c * h, h), :]
            dst = _block(outs[a], kinds[a], kme, None if chip is None else (c * h, h))
            cps.append(pltpu.make_async_remote_copy(src_ref=src, dst_ref=dst, send_sem=send_sems.at[4 * a + j],
                                                    recv_sem=recv_sems.at[4 * a + j], device_id=to, device_id_type=MESH))
    return cps


def _gather_start(shards, kinds, name):
    n = len(shards)
    outs = [lax.empty(_gathered_shape(s, k), s.dtype) for s, k in zip(shards, kinds)]

    def body(*refs):
        for cp in _gather_copies(refs[:n], refs[n:2 * n], refs[2 * n], refs[2 * n + 1], kinds):
            cp.start()
        refs[-1][...] = jnp.zeros_like(refs[-1])

    return pl.pallas_call(
        body, name=name,
        out_shape=(pltpu.SemaphoreType.DMA((4 * n,)), pltpu.SemaphoreType.DMA((4 * n,)),
                   *[_hbm(t) for t in shards], *[_hbm(t) for t in outs], jax.ShapeDtypeStruct((8, LANES), F32)),
        in_specs=[_IN_HBM] * (2 * n),
        out_specs=(_SEMS, _SEMS, *[_IN_HBM] * (2 * n), pl.BlockSpec(memory_space=pltpu.VMEM)),
        input_output_aliases={a: 2 + a for a in range(2 * n)},
        compiler_params=pltpu.CompilerParams(has_side_effects=_DATAFLOW),
    )(*[pltpu.with_memory_space_constraint(t, pltpu.HBM) for t in list(shards) + outs])


def _gather_wait(started, after, kinds, name):
    n = (len(started) - 3) // 2
    bufs = started[2:2 + 2 * n]

    def body(*refs):
        for cp in _gather_copies(refs[:n], refs[n:2 * n], refs[2 * n], refs[2 * n + 1], kinds):
            cp.wait_send()
            cp.wait_recv()

    outs = pl.pallas_call(
        body, name=name, out_shape=tuple(_hbm(t) for t in bufs),
        in_specs=[_IN_HBM] * (2 * n) + [_SEMS, _SEMS, ANY], out_specs=tuple([_IN_HBM] * (2 * n)),
        input_output_aliases={a: a for a in range(2 * n)},
        compiler_params=pltpu.CompilerParams(has_side_effects=_DATAFLOW),
    )(*bufs, started[0], started[1], after)
    return outs[n:]


def _relay_copies(bufs, send_sems, recv_sems, kinds):
    x, y, c, chips = _place()
    cps = []
    for a in range(len(bufs)):
        h = (bufs[a].shape[1] if kinds[a] == "rows" else bufs[a].shape[0]) // 2
        for j, chip in enumerate(chips):
            part = _block(bufs[a], kinds[a], 2 * chip[0] + chip[1], (c * h, h))
            cps.append(pltpu.make_async_remote_copy(src_ref=part, dst_ref=part, send_sem=send_sems.at[3 * a + j],
                                                    recv_sem=recv_sems.at[3 * a + j], device_id=(x, y, 1 - c),
                                                    device_id_type=MESH))
    return cps


def _join_copies(bufs, send_sems, recv_sems, kinds):
    x, y, c, _ = _place()
    cps = []
    for a in range(len(bufs)):
        h = bufs[a].shape[0] // 2
        mine = bufs[a].at[pl.ds(c * h, h), :]
        cps.append(pltpu.make_async_remote_copy(src_ref=mine, dst_ref=mine, send_sem=send_sems.at[a],
                                                recv_sem=recv_sems.at[a], device_id=(x, y, 1 - c), device_id_type=MESH))
    return cps


def _inplace_start(copies, per_array, bufs, kinds, name):
    n = len(bufs)

    def body(*refs):
        for cp in copies(refs[:n], refs[n], refs[n + 1], kinds):
            cp.start()
        refs[-1][...] = jnp.zeros_like(refs[-1])

    return pl.pallas_call(
        body, name=name,
        out_shape=(pltpu.SemaphoreType.DMA((per_array * n,)), pltpu.SemaphoreType.DMA((per_array * n,)),
                   *[_hbm(t) for t in bufs], jax.ShapeDtypeStruct((8, LANES), F32)),
        in_specs=[_IN_HBM] * n, out_specs=(_SEMS, _SEMS, *[_IN_HBM] * n, pl.BlockSpec(memory_space=pltpu.VMEM)),
        input_output_aliases={a: 2 + a for a in range(n)},
        compiler_params=pltpu.CompilerParams(has_side_effects=_DATAFLOW),
    )(*[pltpu.with_memory_space_constraint(t, pltpu.HBM) for t in bufs])


def _inplace_wait(copies, started, after, kinds, name):
    n = len(started) - 3
    bufs = started[2:2 + n]

    def body(*refs):
        for cp in copies(refs[:n], refs[n], refs[n + 1], kinds):
            cp.wait_send()
            cp.wait_recv()

    return pl.pallas_call(
        body, name=name, out_shape=tuple(_hbm(t) for t in bufs),
        in_specs=[_IN_HBM] * n + [_SEMS, _SEMS, ANY], out_specs=tuple([_IN_HBM] * n),
        input_output_aliases={a: a for a in range(n)},
        compiler_params=pltpu.CompilerParams(has_side_effects=_DATAFLOW),
    )(*bufs, started[0], started[1], after)


def _add_halves(g, r1, cvec, kind, name):
    def body(c_ref, g_ref, r_ref, o_ref):
        o_ref[...] = (g_ref[...] + r_ref[...]).astype(o_ref.dtype)

    if kind == "rows":
        _, h, C = r1.shape
        tr = _row_tile(h, 512)
        nt = h // tr
        grid = (4, nt)
        g_spec = pl.BlockSpec((1, tr, C), lambda k, i, c_ref: (k, c_ref[0] * nt + i, 0))
        r_spec = pl.BlockSpec((1, tr, C), lambda k, i, c_ref: (k, i, 0))
    else:
        h, C4 = r1.shape
        tr, tc = _row_tile(h, 512), C4 // 4
        nt = h // tr
        grid = (nt, 4)
        g_spec = pl.BlockSpec((tr, tc), lambda i, k, c_ref: (c_ref[0] * nt + i, k))
        r_spec = pl.BlockSpec((tr, tc), lambda i, k, c_ref: (i, k))
    return pl.pallas_call(
        body, name=name, out_shape=jax.ShapeDtypeStruct(r1.shape, BF16),
        grid_spec=pltpu.PrefetchScalarGridSpec(num_scalar_prefetch=1, grid=grid, in_specs=[g_spec, r_spec],
                                               out_specs=r_spec),
        compiler_params=_cparams(("parallel", "parallel")),
    )(cvec, g, r1)


def _chip_copies(ins, lands, send_sems, recv_sems, kinds):
    x, y, c, chips = _place()
    return [pltpu.make_async_remote_copy(
        src_ref=_block(ins[a], kinds[a], 2 * chip[0] + chip[1]), dst_ref=lands[a].at[j],
        send_sem=send_sems.at[3 * a + j], recv_sem=recv_sems.at[3 * a + j], device_id=(*chip, c), device_id_type=MESH)
        for a in range(len(ins)) for j, chip in enumerate(chips)]


def _land_shape(s, kind):
    return (3,) + (s.shape[1:] if kind == "rows" else (s.shape[0], s.shape[1] // 4))


def _sibling_copies(ins, lands, send_sems, recv_sems, kinds):
    x, y, c, _ = _place()
    cps = []
    for a in range(len(ins)):
        h = lands[a].shape[-2]
        src = ins[a].at[:, pl.ds((1 - c) * h, h), :] if kinds[a] == "rows" else ins[a].at[pl.ds((1 - c) * h, h), :]
        cps.append(pltpu.make_async_remote_copy(src_ref=src, dst_ref=lands[a], send_sem=send_sems.at[a],
                                                recv_sem=recv_sems.at[a], device_id=(x, y, 1 - c), device_id_type=MESH))
    return cps


def _half_shape(g, kind):
    return (4, g.shape[1] // 2, g.shape[2]) if kind == "rows" else (g.shape[0] // 2, g.shape[1])


def _exchange_start(copies, per_array, srcs, land_shapes, kinds, name, zeroed=False):
    n = len(srcs)
    lands = [(jnp.zeros if zeroed else lax.empty)(shape, s.dtype) for shape, s in zip(land_shapes, srcs)]

    def body(*refs):
        for cp in copies(refs[:n], refs[n:2 * n], refs[2 * n], refs[2 * n + 1], kinds):
            cp.start()
        refs[-1][...] = jnp.zeros_like(refs[-1])

    return pl.pallas_call(
        body, name=name,
        out_shape=(pltpu.SemaphoreType.DMA((per_array * n,)), pltpu.SemaphoreType.DMA((per_array * n,)),
                   *[_hbm(t) for t in srcs], *[_hbm(t) for t in lands], jax.ShapeDtypeStruct((8, LANES), F32)),
        in_specs=[_IN_HBM] * (2 * n),
        out_specs=(_SEMS, _SEMS, *[_IN_HBM] * (2 * n), pl.BlockSpec(memory_space=pltpu.VMEM)),
        input_output_aliases={a: 2 + a for a in range(2 * n)},
        compiler_params=pltpu.CompilerParams(has_side_effects=_DATAFLOW),
    )(*[pltpu.with_memory_space_constraint(t, pltpu.HBM) for t in list(srcs) + lands])


def _exchange_wait(copies, started, after, kinds, name):
    n = (len(started) - 3) // 2
    bufs = started[2:2 + 2 * n]

    def body(*refs):
        for cp in copies(refs[:n], refs[n:2 * n], refs[2 * n], refs[2 * n + 1], kinds):
            cp.wait_send()
            cp.wait_recv()

    outs = pl.pallas_call(
        body, name=name, out_shape=tuple(_hbm(t) for t in bufs),
        in_specs=[_IN_HBM] * (2 * n) + [_SEMS, _SEMS, ANY], out_specs=tuple([_IN_HBM] * (2 * n)),
        input_output_aliases={a: a for a in range(2 * n)},
        compiler_params=pltpu.CompilerParams(has_side_effects=_DATAFLOW),
    )(*bufs, started[0], started[1], after)
    return outs[:n], outs[n:]


def _add_chips(s1, r2, kcvec, kind, name):
    _, h, C = r2.shape
    tr = _row_tile(h, 512)
    nt = h // tr

    def body(kc_ref, s_ref, r0_ref, r1_ref, r2_ref, o_ref):
        s = s_ref[0] if kind == "rows" else s_ref[...]
        o_ref[...] = ((s.astype(F32) + r0_ref[0].astype(F32)) + r1_ref[0].astype(F32)) + r2_ref[0].astype(F32)

    peer = lambda j: pl.BlockSpec((1, tr, C), lambda i, kc_ref: (j, i, 0))
    if kind == "rows":
        s_spec = pl.BlockSpec((1, tr, C), lambda i, kc_ref: (kc_ref[0], i, 0))
    else:
        s_spec = pl.BlockSpec((tr, C), lambda i, kc_ref: (i, kc_ref[0]))
    return pl.pallas_call(
        body, name=name, out_shape=jax.ShapeDtypeStruct((2 * h, C), F32),
        grid_spec=pltpu.PrefetchScalarGridSpec(
            num_scalar_prefetch=1, grid=(nt,),
            in_specs=[s_spec, peer(0), peer(1), peer(2)],
            out_specs=pl.BlockSpec((tr, C), lambda i, kc_ref: (kc_ref[1] * nt + i, 0))),
        compiler_params=_cparams(("parallel",)),
    )(kcvec, s1, r2, r2, r2)


N_DEV = 8


def _spread_copies(packs, lands, send_sems, recv_sems, kinds):
    x, y, c, _ = _place()
    me = 4 * x + 2 * y + c
    return [pltpu.make_async_remote_copy(
        src_ref=packs[0], dst_ref=lands[0].at[me], send_sem=send_sems.at[mask - 1], recv_sem=recv_sems.at[mask - 1],
        device_id=(1 - x if mask & 4 else x, 1 - y if mask & 2 else y, 1 - c if mask & 1 else c), device_id_type=MESH)
        for mask in range(1, N_DEV)]


def _sum_spread(pack, gathered):
    P = pack.shape[0]

    def body(p_ref, g_ref, o_ref):
        x, y, c, _ = _place()
        me = 4 * x + 2 * y + c
        acc = None
        for i in range(N_DEV):
            term = jnp.where(me == i, p_ref[...], g_ref[i])
            acc = term if acc is None else acc + term
        o_ref[...] = acc

    vmem = pl.BlockSpec(memory_space=pltpu.VMEM)
    return pl.pallas_call(body, name="allreduce_sum", out_shape=jax.ShapeDtypeStruct((P, LANES), F32),
                          in_specs=[vmem, vmem], out_specs=vmem)(pack, gathered)


def _pack_rows(arrs):
    rows = []
    for a in arrs:
        f = a.reshape(-1)
        f = jnp.pad(f, (0, (-f.shape[0]) % (8 * LANES)))
        rows.append(f.reshape(-1, LANES))
    return jnp.concatenate(rows, axis=0)


def _unpack_rows(pack, shapes):
    out, r = [], 0
    for s in shapes:
        n = math.prod(s)
        out.append(pack[r:r + -(-n // LANES)].reshape(-1)[:n].reshape(s))
        r += 8 * -(-n // (8 * LANES))
    return out


_SMALL = ["norm_mix_pre", "ml_head_norm", "b_gate_a", "b_gate_b", "norm_mix_post", "norm_ffn_pre", "norm_ffn_post",
          "conv_b", "b_ml_i", "b_ml_f", "b_fox_f"]
_BIG = ["w_in", "w_branch_a", "w_branch_b", "w_out", "w_up", "w_down"]
_WEIGHTS = ['norm_mix_pre', 'w_in', 'b_ml_i', 'b_ml_f', 'ml_head_norm', 'b_fox_f', 'b_gate_a', 'b_gate_b', 'w_branch_a',
            'w_branch_b', 'w_out', 'norm_mix_post', 'norm_ffn_pre', 'w_up', 'conv_w', 'conv_b', 'w_down', 'norm_ffn_post']


_KINDS = ["rows", "rows", "rows", "rows", "cols", "rows"]


def kernel(x, norm_mix_pre, w_in, b_ml_i, b_ml_f, ml_head_norm, b_fox_f, b_gate_a, b_gate_b, w_branch_a, w_branch_b, w_out, norm_mix_post, norm_ffn_pre, w_up, conv_w, conv_b, w_down, norm_ffn_post, loss_target, m_norm_mix_pre, m_w_in, m_b_ml_i, m_b_ml_f, m_ml_head_norm, m_b_fox_f, m_b_gate_a, m_b_gate_b, m_w_branch_a, m_w_branch_b, m_w_out, m_norm_mix_post, m_norm_ffn_pre, m_w_up, m_conv_w, m_conv_b, m_w_down, m_norm_ffn_post, v_norm_mix_pre, v_w_in, v_b_ml_i, v_b_ml_f, v_ml_head_norm, v_b_fox_f, v_b_gate_a, v_b_gate_b, v_w_branch_a, v_w_branch_b, v_w_out, v_norm_mix_post, v_norm_ffn_pre, v_w_up, v_conv_w, v_conv_b, v_w_down, v_norm_ffn_post):
    args = dict(locals())
    w = {n: args[n] for n in _WEIGHTS}
    mom = {n: args["m_" + n] for n in _WEIGHTS}
    var = {n: args["v_" + n] for n in _WEIGHTS}
    cx, cy, cc = lax.axis_index("x"), lax.axis_index("y"), lax.axis_index("c")
    kme = 2 * cx + cy
    cvec = jnp.reshape(cc, (1,)).astype(jnp.int32)
    kcvec = jnp.stack([kme, cc]).astype(jnp.int32)
    odd = kme % 2

    tr3 = lambda t: jnp.transpose(t, (0, 2, 1))
    w["w_in"], mom["w_in"], var["w_in"] = tr3(w_in), tr3(m_w_in), tr3(v_w_in)
    w_in_main = lax.dynamic_slice_in_dim(w["w_in"][0], 4 * odd, 2048, axis=0).astype(BF16)
    w_in_gates = lax.dynamic_slice_in_dim(w["w_in"][0], 2048 * (1 - odd), 4, axis=0).astype(BF16)
    (wmain_t,), (g_cw, g_gates) = _gather_weights([w_in_main], _KINDS[:1], [w["conv_w"][0], w_in_gates])
    rest_started = _gather_start([w[n][0].astype(BF16) for n in _BIG[1:]], _KINDS[1:], "gather_rest_start")

    relay = {}

    def rest_arrived(after):
        bufs = _gather_wait(rest_started, after, _KINDS[1:], "gather_rest_wait")
        relay["started"] = _inplace_start(_relay_copies, 3, bufs, _KINDS[1:], "gather_rest_relay_start")
        return relay["started"][-1]

    def rest_weights(after):
        g_a, g_b, g_out, wup, g_down = _inplace_wait(_relay_copies, relay["started"], after, _KINDS[1:],
                                                     "gather_rest_relay_wait")
        return full(g_a), full(g_b), full(g_out), wup, full(g_down)
    gate_rows = g_gates.reshape(16, D_MODEL)
    wsmall_t = jnp.zeros((N_SMALL, D_MODEL), BF16)
    for blk, (lo, hi) in enumerate(((0, 4), (4, 8), (8, 16))):
        wsmall_t = wsmall_t.at[blk * LANES:blk * LANES + hi - lo].set(gate_rows[lo:hi])
    full = lambda g: g.reshape(-1, g.shape[2])
    p = {n: w[n] for n in _SMALL}
    p["conv_w"] = jnp.transpose(g_cw, (1, 0, 2)).reshape(3, -1)

    groups = {}

    def on_grads(group, gs):
        names = list(gs)
        kinds = [_KINDS[_BIG.index(n)] for n in names]
        whole = [g if k == "cols" else g.reshape(4, -1, g.shape[1]) for g, k in zip(gs.values(), kinds)]
        started = _exchange_start(_sibling_copies, 1, whole, [_half_shape(g, k) for g, k in zip(whole, kinds)], kinds,
                                  "grads_to_sibling_start_" + group)
        groups[group] = dict(names=names, kinds=kinds, sibling=started)
        return started[-1]

    def advance(group, after):
        G = groups[group]
        whole, got = _exchange_wait(_sibling_copies, G["sibling"], after, G["kinds"], "grads_to_sibling_wait_" + group)
        sums = [_add_halves(g, r, cvec, k, "add_sibling_" + n) for g, r, k, n in zip(whole, got, G["kinds"], G["names"])]
        G["chips"] = _exchange_start(_chip_copies, 3, sums, [_land_shape(s, k) for s, k in zip(sums, G["kinds"])],
                                     G["kinds"], "grads_to_chips_start_" + group)
        return G["chips"][-1]

    loss_row, grad_x, big, small = _local_step(x[0], loss_target[0], full(wmain_t), wsmall_t, rest_arrived, rest_weights,
                                               p, on_grads, advance, rest_started[-1])
    gt = big["wsmall_t"]
    small["w_in_gates"] = jnp.concatenate([gt[0:4], gt[LANES:LANES + 4], gt[2 * LANES:2 * LANES + 8]], axis=0)
    small_names = _SMALL + ["conv_w"]
    packed_names = small_names + ["w_in_gates"]
    pack = _pack_rows([small[n] for n in packed_names] + [loss_row])
    spread = _exchange_start(_spread_copies, N_DEV - 1, [pack], [(N_DEV,) + pack.shape], None, "allreduce_start", zeroed=True)

    def my_half(group, after):
        G = groups[group]
        sums, got = _exchange_wait(_chip_copies, G["chips"], after, G["kinds"], "grads_to_chips_wait_" + group)
        return [_add_chips(s, r, kcvec, k, "add_chips_" + n) for s, r, k, n in zip(sums, got, G["kinds"], G["names"])]

    first_names = groups["ffn"]["names"] + groups["mix"]["names"]
    join_first = _inplace_start(_join_copies, 1, my_half("ffn", spread[-1]) + my_half("mix", spread[-1]), None,
                                "grads_join_start")
    join_in = _inplace_start(_join_copies, 1, my_half("in", join_first[-1]), None, "grads_join_start_in")
    grads = dict(zip(first_names, _inplace_wait(_join_copies, join_first, join_in[-1], None, "grads_join_wait")))

    delta, new_m, new_v = {}, {}, {}
    for n in _BIG[1:]:
        delta[n], new_m[n], new_v[n] = _adamw(w[n], grads[n], mom[n], var[n], "adamw_" + n)
        grads[n] = grads[n][None]
    grads["w_in"], = _inplace_wait(_join_copies, join_in, delta[_BIG[-1]], None, "grads_join_wait_in")

    (pack,), (gathered,) = _exchange_wait(_spread_copies, spread, delta[_BIG[-1]], None, "allreduce_wait")
    full_shapes = [small[n].shape if n in ("conv_w", "w_in_gates") else w[n][0].shape for n in packed_names]
    total = _unpack_rows(_sum_spread(pack, gathered), full_shapes + [loss_row.shape])
    for n, t in zip(packed_names, total):
        grads[n] = t
    loss = total[-1][0, 0]
    grads["conv_w"] = lax.dynamic_slice_in_dim(grads["conv_w"], kme * conv_w.shape[2], conv_w.shape[2], axis=1)
    my_gates = lax.dynamic_slice_in_dim(grads.pop("w_in_gates"), 4 * kme, 4, axis=0)
    g_in = jnp.zeros(w["w_in"].shape[1:], F32)
    g_in = lax.dynamic_update_slice_in_dim(g_in, grads["w_in"], 4 * odd, axis=0)
    grads["w_in"] = lax.dynamic_update_slice_in_dim(g_in, my_gates, 2048 * (1 - odd), axis=0)
    delta["w_in"], new_m["w_in"], new_v["w_in"] = _adamw(w["w_in"], grads["w_in"], mom["w_in"], var["w_in"], "adamw_w_in")
    grads["w_in"] = grads["w_in"][None]
    for d in (grads, delta, new_m, new_v):
        d["w_in"] = tr3(d["w_in"])
    packs = [_pack_rows([d[n][0] for n in small_names]) for d in (w, mom, var)]
    pad = ((0, (-packs[0].shape[0]) % 8), (0, 0))
    packs = [jnp.pad(t, pad)[None] for t in packs]
    gp = jnp.pad(_pack_rows([grads[n] for n in small_names]), pad)
    shapes = [w[n][0].shape for n in small_names]
    for dst, res in zip((delta, new_m, new_v), _adamw(packs[0], gp, packs[1], packs[2], "adamw_small")):
        for n, t in zip(small_names, _unpack_rows(res[0], shapes)):
            dst[n] = t[None]
    for n in small_names:
        grads[n] = grads[n][None]

    return (loss, grad_x[None], *[grads[n] for n in _WEIGHTS], *[delta[n] for n in _WEIGHTS],
            *[new_m[n] for n in _WEIGHTS], *[new_v[n] for n in _WEIGHTS])
```

```python
import functools
import math

import jax
import jax.numpy as jnp
from jax import lax
from jax.experimental import pallas as pl
from jax.experimental.pallas import tpu as pltpu

F32 = jnp.float32
BF16 = jnp.bfloat16
MESH = pl.DeviceIdType.MESH

D_MODEL = 1024
ML_HEADS = 4
ML_DQK = 128
ML_DV = 256
FOX_HEADS = 8
FOX_DH = 128
D_FF = 2816
GATE_CAP = 15.0
EPS = 1e-6
ADAM_LR, ADAM_B1, ADAM_B2, ADAM_EPS, ADAM_WD, ADAM_STEP = 0.001, 0.9, 0.999, 1e-08, 0.01, 10

LANES = 128
MLC = 256
FOX_TQ = 512
FOX_TQ_FWD = 1024
FOX_TK = 512
FOX_TK_FWD = 1024
ROW_T = 512
CONV_TC = 1408
VMEM_LIMIT = 56 * 1024 * 1024

C_QM, C_KM, C_VM, C_OM = 0, 512, 1024, 2048
N_ML, N_FOX, N_GATE = 3072, 3072, 2048
N_SMALL = 384


def _cparams(sem=None):
    return pltpu.CompilerParams(dimension_semantics=sem, vmem_limit_bytes=VMEM_LIMIT)


def _tile(n, target):
    if n <= target:
        return n
    best = None
    for t in range(LANES, target + 1, LANES):
        if n % t == 0:
            best = t
    assert best is not None, (n, target)
    return best


def _dot(a, b, dims):
    return lax.dot_general(a, b, (dims, ((), ())), preferred_element_type=F32)


def _dot_nn(a, b):
    return _dot(a, b, ((1,), (0,)))


def _dot_nt(a, b):
    return _dot(a, b, ((1,), (1,)))


def _dot_tn(a, b):
    return _dot(a, b, ((0,), (0,)))


_DOTS = {"nn": _dot_nn, "nt": _dot_nt, "tn": _dot_tn}


def _mm(a, b, mode, out_dtype, name, tm=1024, tn=1408, tk=1408, after=None, b_rows=None):
    a_parts = list(a) if isinstance(a, (list, tuple)) else [a]
    b_parts = list(b) if isinstance(b, (list, tuple)) else [b]
    extra = [] if after is None else [after]
    assert len(a_parts) == 1 or len(b_parts) == 1, name
    a_axes = {"nn": "ik", "nt": "ik", "tn": "ki"}[mode]
    b_axes = {"nn": "kj", "nt": "jk", "tn": "kj"}[mode]
    size, target = {}, dict(i=tm, j=tn, k=tk)
    for parts, axes in ((a_parts, a_axes), (b_parts, b_axes)):
        dims = (parts[0].shape[0], parts[0].shape[1] * len(parts))
        if parts is b_parts and b_rows is not None:
            dims = (b_rows[1], dims[1])
        for ax, n in zip(axes, dims):
            assert size.setdefault(ax, n) == n, (name, ax, n, size)
    tile = {}
    for parts, axes in ((a_parts, a_axes), (b_parts, b_axes)):
        if len(parts) > 1:
            tile[axes[1]] = _tile(parts[0].shape[1], target[axes[1]])
    for ax in "ijk":
        tile.setdefault(ax, _tile(size[ax], target[ax]))
    M, N, nk = size["i"], size["j"], size["k"] // tile["k"]
    grid_pos = dict(i=0, j=1, k=2)
    dot = _DOTS[mode]

    def specs(parts, axes):
        blk = (tile[axes[0]], tile[axes[1]])
        if len(parts) == 1:
            first = 0
            if parts is b_parts and b_rows is not None:
                assert b_rows[0] % blk[0] == 0, (name, b_rows, blk)
                first = b_rows[0] // blk[0]
            return [pl.BlockSpec(blk, lambda *g: (first + g[grid_pos[axes[0]]], g[grid_pos[axes[1]]]))], None
        bpp = parts[0].shape[1] // blk[1]

        def index(p):
            def f(*g):
                g0, g1 = g[grid_pos[axes[0]]], g[grid_pos[axes[1]]]
                on = g1 // bpp == p
                return jnp.where(on, g0, 0), jnp.where(on, g1 % bpp, 0)
            return f

        return [pl.BlockSpec(blk, index(p)) for p in range(len(parts))], (axes[1], bpp)

    a_specs, a_sel = specs(a_parts, a_axes)
    b_specs, b_sel = specs(b_parts, b_axes)
    na, nb = len(a_parts), len(b_parts)

    def body(*refs):
        a_refs, b_refs = refs[:na], refs[na:na + nb]
        o_ref, acc = refs[na + nb + len(extra)], refs[na + nb + len(extra) + 1:]

        def accumulate(part):
            if nk == 1:
                o_ref[...] = part.astype(o_ref.dtype)
                return
            acc_ref, = acc
            k = pl.program_id(2)

            @pl.when(k == 0)
            def _():
                acc_ref[...] = part

            @pl.when(k > 0)
            def _():
                acc_ref[...] += part

            @pl.when(k == nk - 1)
            def _():
                o_ref[...] = acc_ref[...].astype(o_ref.dtype)

        sel = a_sel or b_sel
        if sel is None:
            accumulate(dot(a_refs[0][...], b_refs[0][...]))
        else:
            which = pl.program_id(grid_pos[sel[0]]) // sel[1]
            for p in range(max(na, nb)):
                @pl.when(which == p)
                def _(p=p):
                    accumulate(dot(a_refs[p if a_sel else 0][...], b_refs[p if b_sel else 0][...]))

    return pl.pallas_call(
        body, name=name,
        out_shape=jax.ShapeDtypeStruct((M, N), out_dtype),
        grid=(M // tile["i"], N // tile["j"], nk),
        in_specs=a_specs + b_specs + [pl.BlockSpec(memory_space=pl.ANY)] * len(extra),
        out_specs=pl.BlockSpec((tile["i"], tile["j"]), lambda i, j, k: (i, j)),
        scratch_shapes=[pltpu.VMEM((tile["i"], tile["j"]), F32)] if nk > 1 else [],
        compiler_params=_cparams(("parallel", "parallel", "arbitrary")),
    )(*a_parts, *b_parts, *extra)


def _mm_sum_parts(parts, b, out_dtype, name, trans_b=False, tm=1024, tn=512, after=None):
    M, K = parts[0].shape
    N = b.shape[0] if trans_b else b.shape[1]
    tm, tn = _tile(M, tm), _tile(N, tn)
    n = len(parts)
    extra = [] if after is None else [after]

    def body(*refs):
        b_ref, o_ref = refs[n], refs[n + 1 + len(extra)]
        acc = None
        for p in range(n):
            if trans_b:
                d = _dot_nt(refs[p][...], b_ref[:, p * K:(p + 1) * K])
            else:
                d = _dot_nn(refs[p][...], b_ref[p * K:(p + 1) * K, :])
            acc = d if acc is None else acc + d
        o_ref[...] = acc.astype(o_ref.dtype)

    b_spec = pl.BlockSpec((tn, n * K), lambda i, j: (j, 0)) if trans_b else pl.BlockSpec((n * K, tn), lambda i, j: (0, j))
    return pl.pallas_call(
        body, name=name, out_shape=jax.ShapeDtypeStruct((M, N), out_dtype), grid=(M // tm, N // tn),
        in_specs=[pl.BlockSpec((tm, K), lambda i, j: (i, 0))] * n + [b_spec]
        + [pl.BlockSpec(memory_space=pl.ANY)] * len(extra),
        out_specs=pl.BlockSpec((tm, tn), lambda i, j: (i, j)),
        compiler_params=_cparams(("parallel", "arbitrary")),
    )(*parts, b, *extra)


def _rstd(x):
    return lax.rsqrt(jnp.mean(x * x, axis=-1, keepdims=True) + EPS)


def _rmsnorm_fwd(x, g, name):
    S, D = x.shape
    T = _tile(S, ROW_T)

    def body(x_ref, g_ref, o_ref):
        xv = x_ref[...]
        o_ref[...] = (xv * _rstd(xv) * g_ref[...]).astype(o_ref.dtype)

    return pl.pallas_call(
        body, name=name, out_shape=jax.ShapeDtypeStruct((S, D), BF16), grid=(S // T,),
        in_specs=[pl.BlockSpec((T, D), lambda i: (i, 0)), pl.BlockSpec((1, D), lambda i: (0, 0))],
        out_specs=pl.BlockSpec((T, D), lambda i: (i, 0)),
        compiler_params=_cparams(("parallel",)),
    )(x, g)


def _resid_norm_fwd(x, z, g, g_next, name):
    S, D = x.shape
    T = _tile(S, ROW_T)

    def body(x_ref, z_ref, g_ref, gn_ref, o_ref, h_ref):
        zv = z_ref[...]
        x1 = x_ref[...] + zv * _rstd(zv) * g_ref[...]
        o_ref[...] = x1
        h_ref[...] = (x1 * _rstd(x1) * gn_ref[...]).astype(h_ref.dtype)

    row = pl.BlockSpec((T, D), lambda i: (i, 0))
    vec = pl.BlockSpec((1, D), lambda i: (0, 0))
    return pl.pallas_call(
        body, name=name, out_shape=(jax.ShapeDtypeStruct((S, D), F32), jax.ShapeDtypeStruct((S, D), BF16)),
        grid=(S // T,), in_specs=[row, row, vec, vec], out_specs=(row, row), compiler_params=_cparams(("parallel",)),
    )(x, z, g, g_next)


def _norm_chain_bwd(dh, xin, g, resid, zin, gz, name):
    S, D = xin.shape
    T = _tile(S, ROW_T)

    def body(dh_ref, x_ref, g_ref, r_ref, z_ref, gz_ref, dx_ref, dz_ref, dg_ref, dgz_ref):
        dx, dgt = _rmsnorm_bwd_math(dh_ref[...], x_ref[...], g_ref[...])
        dx = dx + r_ref[...]
        dx_ref[...] = dx
        dz, dgzt = _rmsnorm_bwd_math(dx, z_ref[...], gz_ref[...])
        dz_ref[...] = dz.astype(dz_ref.dtype)

        @pl.when(pl.program_id(0) == 0)
        def _():
            dg_ref[...] = jnp.zeros_like(dg_ref)
            dgz_ref[...] = jnp.zeros_like(dgz_ref)

        dg_ref[...] += jnp.sum(dgt, axis=0, keepdims=True)
        dgz_ref[...] += jnp.sum(dgzt, axis=0, keepdims=True)

    row = pl.BlockSpec((T, D), lambda i: (i, 0))
    vec = pl.BlockSpec((1, D), lambda i: (0, 0))
    v1 = jax.ShapeDtypeStruct((1, D), F32)
    return pl.pallas_call(
        body, name=name,
        out_shape=(jax.ShapeDtypeStruct((S, D), F32), jax.ShapeDtypeStruct((S, D), BF16), v1, v1),
        grid=(S // T,), in_specs=[row, row, vec, row, row, vec], out_specs=(row, row, vec, vec),
        compiler_params=_cparams(("arbitrary",)),
    )(dh, xin, g, resid, zin, gz)


def _rmsnorm_bwd_math(dy, xv, g):
    r = _rstd(xv)
    u = dy * g
    dx = r * u - xv * (r * r * r) * jnp.mean(u * xv, axis=-1, keepdims=True)
    return dx, dy * xv * r


def _rmsnorm_bwd(dys, xin, g, resid, out_dtype, name):
    S, D = xin.shape
    T = _tile(S, ROW_T)
    has_resid = resid is not None
    ndy = len(dys)

    def body(*refs):
        dy_refs, (x_ref, g_ref) = refs[:ndy], refs[ndy:ndy + 2]
        dx_ref, dg_ref = refs[-2:]
        dy = dy_refs[0][...]
        for r in dy_refs[1:]:
            dy = dy + r[...]
        dx, dgt = _rmsnorm_bwd_math(dy, x_ref[...], g_ref[...])
        if has_resid:
            dx = dx + refs[ndy + 2][...]
        dx_ref[...] = dx.astype(dx_ref.dtype)

        @pl.when(pl.program_id(0) == 0)
        def _():
            dg_ref[...] = jnp.zeros_like(dg_ref)

        dg_ref[...] += jnp.sum(dgt, axis=0, keepdims=True)

    row = pl.BlockSpec((T, D), lambda i: (i, 0))
    vec = pl.BlockSpec((1, D), lambda i: (0, 0))
    ins = list(dys) + [xin, g] + ([resid] if has_resid else [])
    return pl.pallas_call(
        body, name=name,
        out_shape=(jax.ShapeDtypeStruct((S, D), out_dtype), jax.ShapeDtypeStruct((1, D), F32)),
        grid=(S // T,), in_specs=[row] * ndy + [row, vec] + ([row] if has_resid else []),
        out_specs=(row, vec), compiler_params=_cparams(("arbitrary",)),
    )(*ins)


def _loss_head(x1, d, g, target, name):
    S, D = x1.shape
    T = _tile(S, ROW_T)

    def body(x_ref, d_ref, g_ref, t_ref, loss_ref, dy_ref, dd_ref, dg_ref):
        dv, gv = d_ref[...], g_ref[...]
        y = x_ref[...] + dv * _rstd(dv) * gv
        diff = y - t_ref[...]
        dy = diff * (1.0 / D)
        dy_ref[...] = dy
        dd, dgt = _rmsnorm_bwd_math(dy, dv, gv)
        dd_ref[...] = dd.astype(dd_ref.dtype)

        @pl.when(pl.program_id(0) == 0)
        def _():
            dg_ref[...] = jnp.zeros_like(dg_ref)
            loss_ref[...] = jnp.zeros_like(loss_ref)

        dg_ref[...] += jnp.sum(dgt, axis=0, keepdims=True)
        part = jnp.sum(jnp.sum(diff * diff, axis=1, keepdims=True), axis=0, keepdims=True)
        loss_ref[...] += (0.5 / D) * part

    row = pl.BlockSpec((T, D), lambda i: (i, 0))
    vec = pl.BlockSpec((1, D), lambda i: (0, 0))
    return pl.pallas_call(
        body, name=name,
        out_shape=(jax.ShapeDtypeStruct((1, LANES), F32), jax.ShapeDtypeStruct((S, D), F32),
                   jax.ShapeDtypeStruct((S, D), BF16), jax.ShapeDtypeStruct((1, D), F32)),
        grid=(S // T,), in_specs=[row, row, vec, row],
        out_specs=(pl.BlockSpec((1, LANES), lambda i: (0, 0)), row, row, vec),
        compiler_params=_cparams(("arbitrary",)),
    )(x1, d, g, target)


def _merge_fwd(ya, yb, pm, ba, bb, name):
    S, D = ya.shape
    T = _tile(S, ROW_T)

    def body(ya_ref, yb_ref, ga_ref, gb_ref, ba_ref, bb_ref, o_ref):
        sa = jax.nn.sigmoid(ga_ref[...] + ba_ref[...])
        sb = jax.nn.sigmoid(gb_ref[...] + bb_ref[...])
        o_ref[...] = (sa * ya_ref[...] + sb * yb_ref[...]).astype(o_ref.dtype)

    row = pl.BlockSpec((T, D), lambda i: (i, 0))
    vec = pl.BlockSpec((1, D), lambda i: (0, 0))
    return pl.pallas_call(
        body, name=name, out_shape=jax.ShapeDtypeStruct((S, D), BF16), grid=(S // T,),
        in_specs=[row, row, pl.BlockSpec((T, D), lambda i: (i, 0)),
                  pl.BlockSpec((T, D), lambda i: (i, 1)), vec, vec],
        out_specs=row, compiler_params=_cparams(("parallel",)),
    )(ya, yb, pm, pm, ba, bb)


def _merge_bwd(dmerged, ya, yb, pm, ba, bb, name):
    S, D = ya.shape
    T = _tile(S, ROW_T)

    def body(dm_ref, ya_ref, yb_ref, ga_ref, gb_ref, ba_ref, bb_ref,
             dya_ref, dyb_ref, dga_ref, dgb_ref, dba_ref, dbb_ref):
        dm = dm_ref[...]
        sa = jax.nn.sigmoid(ga_ref[...] + ba_ref[...])
        sb = jax.nn.sigmoid(gb_ref[...] + bb_ref[...])
        dya_ref[...] = (dm * sa).astype(dya_ref.dtype)
        dyb_ref[...] = (dm * sb).astype(dyb_ref.dtype)
        dga = dm * ya_ref[...] * sa * (1.0 - sa)
        dgb = dm * yb_ref[...] * sb * (1.0 - sb)
        dga_ref[...] = dga.astype(dga_ref.dtype)
        dgb_ref[...] = dgb.astype(dgb_ref.dtype)

        @pl.when(pl.program_id(0) == 0)
        def _():
            dba_ref[...] = jnp.zeros_like(dba_ref)
            dbb_ref[...] = jnp.zeros_like(dbb_ref)

        dba_ref[...] += jnp.sum(dga, axis=0, keepdims=True)
        dbb_ref[...] += jnp.sum(dgb, axis=0, keepdims=True)

    row = pl.BlockSpec((T, D), lambda i: (i, 0))
    vec = pl.BlockSpec((1, D), lambda i: (0, 0))
    act = jax.ShapeDtypeStruct((S, D), BF16)
    v1 = jax.ShapeDtypeStruct((1, D), F32)
    return pl.pallas_call(
        body, name=name, out_shape=(act, act, act, act, v1, v1), grid=(S // T,),
        in_specs=[row, row, row, pl.BlockSpec((T, D), lambda i: (i, 0)),
                  pl.BlockSpec((T, D), lambda i: (i, 1)), vec, vec],
        out_specs=(row, row, row, row, vec, vec), compiler_params=_cparams(("arbitrary",)),
    )(dmerged, ya, yb, pm, pm, ba, bb)


_GELU_C = math.sqrt(2.0 / math.pi)


_GELU_K = 0.044715


def _gelu(g):
    u = 0.5 * jnp.tanh(g * (_GELU_C + (_GELU_C * _GELU_K) * (g * g))) + 0.5
    return g * u, u


def _gelu_grad(g, u):
    return u * (1.0 + g * (1.0 - u) * (2 * _GELU_C + (6 * _GELU_C * _GELU_K) * (g * g)))


def _shift_down(v, halo_ref, first, rows):
    T = v.shape[0]
    keep = jnp.where(first, 0.0, 1.0)
    h7 = halo_ref[7:8, :] * keep
    h6 = halo_ref[6:7, :] * keep
    m1 = jnp.where(rows == 0, h7, pltpu.roll(v, 1, 0))
    m2 = jnp.where(rows == 0, h6, jnp.where(rows == 1, h7, pltpu.roll(v, 2, 0)))
    return m1, m2


def _conv_act_fwd(up, cw, cb, name):
    S, F2 = up.shape
    Fh = F2 // 2
    T = _tile(S, ROW_T)
    tc = _tile(Fh, CONV_TC)
    ncol = Fh // tc
    hb = T // 8

    def body(ua_ref, ug_ref, ha_ref, hg_ref, wa_ref, wg_ref, ba_ref, bg_ref, o_ref, a_ref, g_ref):
        first = pl.program_id(0) == 0
        rows = lax.broadcasted_iota(jnp.int32, (T, tc), 0)

        def conv(u_ref, h_ref, w_ref, b_ref):
            v = u_ref[...]
            m1, m2 = _shift_down(v, h_ref, first, rows)
            return b_ref[...] + w_ref[0:1, :] * m2 + w_ref[1:2, :] * m1 + w_ref[2:3, :] * v

        a = conv(ua_ref, ha_ref, wa_ref, ba_ref)
        g = conv(ug_ref, hg_ref, wg_ref, bg_ref)
        a_ref[...] = a
        g_ref[...] = g
        o_ref[...] = (_gelu(g)[0] * a).astype(o_ref.dtype)

    halo = lambda off: pl.BlockSpec((8, tc), lambda i, j: (jnp.maximum(i * hb - 1, 0), j + off))
    blk = pl.BlockSpec((T, tc), lambda i, j: (i, j))
    f32 = jax.ShapeDtypeStruct((S, Fh), F32)
    return pl.pallas_call(
        body, name=name, out_shape=(jax.ShapeDtypeStruct((S, Fh), BF16), f32, f32), grid=(S // T, ncol),
        in_specs=[blk, pl.BlockSpec((T, tc), lambda i, j: (i, j + ncol)),
                  halo(0), halo(ncol),
                  pl.BlockSpec((3, tc), lambda i, j: (0, j)), pl.BlockSpec((3, tc), lambda i, j: (0, j + ncol)),
                  pl.BlockSpec((1, tc), lambda i, j: (0, j)), pl.BlockSpec((1, tc), lambda i, j: (0, j + ncol))],
        out_specs=(blk, blk, blk),
        compiler_params=_cparams(("parallel", "parallel")),
    )(up, up, up, up, cw, cw, cb, cb)


def _conv_act_bwd(up, a, g, dact, cw, name):
    S, F2 = up.shape
    Fh = F2 // 2
    T = _tile(S, ROW_T)
    tc = _tile(Fh, CONV_TC)
    ncol, nrow, hb, nhb = Fh // tc, S // T, T // 8, S // 8

    def body(ua_ref, ug_ref, a_ref, g_ref, an_ref, gn_ref, wa_ref, wg_ref, da_ref, dn_ref,
             dpa_ref, dpg_ref, dwa_ref, dwg_ref, dba_ref, dbg_ref, dua_n, dug_n):
        i = pl.program_id(1)
        rows = lax.broadcasted_iota(jnp.int32, (T, tc), 0)

        def du_of(a, g, dact_v):
            gel, t = _gelu(g)
            return dact_v * gel, dact_v * a * _gelu_grad(g, t)

        dua, dug = du_of(a_ref[...], g_ref[...], da_ref[...])
        keep = jnp.where(i == nrow - 1, 0.0, 1.0)
        dua_n[...], dug_n[...] = du_of(an_ref[...], gn_ref[...], dn_ref[...] * keep)

        @pl.when(i == 0)
        def _():
            for r in (dwa_ref, dwg_ref, dba_ref, dbg_ref):
                r[...] = jnp.zeros_like(r)

        for du, n_ref, u_ref, w_ref, o_ref, dw_ref, db_ref in ((dua, dua_n, ua_ref, wa_ref, dpa_ref, dwa_ref, dba_ref),
                                                               (dug, dug_n, ug_ref, wg_ref, dpg_ref, dwg_ref, dbg_ref)):
            n0, n1 = n_ref[0:1, :], n_ref[1:2, :]
            du1 = jnp.where(rows == T - 1, n0, pltpu.roll(du, T - 1, 0))
            du2 = jnp.where(rows == T - 2, n0, jnp.where(rows == T - 1, n1, pltpu.roll(du, T - 2, 0)))
            o_ref[...] = (w_ref[2:3, :] * du + w_ref[1:2, :] * du1 + w_ref[0:1, :] * du2).astype(o_ref.dtype)
            u = u_ref[...]
            db_ref[...] += jnp.sum(du, axis=0, keepdims=True)
            for j, d in enumerate((du2, du1, du)):
                dw_ref[j:j + 1, :] += jnp.sum(d * u, axis=0, keepdims=True)

    tile = lambda off: pl.BlockSpec((T, tc), lambda j, i: (i, j + off))
    under = pl.BlockSpec((8, tc), lambda j, i: (jnp.minimum((i + 1) * hb, nhb - 1), j))
    vec = lambda n, off: pl.BlockSpec((n, tc), lambda j, i: (0, j + off))
    act = jax.ShapeDtypeStruct((S, Fh), BF16)
    return pl.pallas_call(
        body, name=name,
        out_shape=(act, act, jax.ShapeDtypeStruct((3, Fh), F32), jax.ShapeDtypeStruct((3, Fh), F32),
                   jax.ShapeDtypeStruct((1, Fh), F32), jax.ShapeDtypeStruct((1, Fh), F32)),
        grid=(ncol, nrow),
        in_specs=[tile(0), tile(ncol), tile(0), tile(0), under, under, vec(3, 0), vec(3, ncol), tile(0), under],
        out_specs=(tile(0), tile(0), vec(3, 0), vec(3, 0), vec(1, 0), vec(1, 0)),
        scratch_shapes=[pltpu.VMEM((8, tc), F32), pltpu.VMEM((8, tc), F32)],
        compiler_params=_cparams(("parallel", "arbitrary")),
    )(up, up, a, g, a, g, cw, cw, dact, dact)


def _split3(x):
    hi = x.astype(BF16)
    r1 = x - hi.astype(F32)
    mid = r1.astype(BF16)
    lo = (r1 - mid.astype(F32)).astype(BF16)
    return hi, mid, lo


def _tri_dot(tri, x):
    hi, mid, lo = _split3(x)
    return _dot_nn(tri, hi) + _dot_nn(tri, mid) + _dot_nn(tri, lo)


def _log_sigmoid(x):
    return jnp.minimum(x, 0.0) - jnp.log(1.0 + jnp.exp(-jnp.abs(x)))


def _tri_mask(n, lower):
    r = lax.broadcasted_iota(jnp.int32, (n, n), 0)
    c = lax.broadcasted_iota(jnp.int32, (n, n), 1)
    return (r >= c) if lower else (r <= c)


def _gates_fwd(ps, bi, bf, bff, name):
    S = ps.shape[0]
    NC = S // MLC

    def body(ps_ref, bi_ref, bf_ref, bff_ref, a_ref, A_ref, wi_ref, em_ref, wk_ref, dec_ref, F_ref, m_scr, f_scr):
        @pl.when(pl.program_id(0) == 0)
        def _():
            m_scr[...] = jnp.zeros_like(m_scr)
            f_scr[...] = jnp.zeros_like(f_scr)

        rows = lax.broadcasted_iota(jnp.int32, (MLC, LANES), 0)
        ltri = _tri_mask(MLC, True).astype(BF16)
        li = GATE_CAP * jnp.tanh((ps_ref[:, 0:LANES] + bi_ref[...]) / GATE_CAP)
        lf = _log_sigmoid(GATE_CAP * jnp.tanh((ps_ref[:, LANES:2 * LANES] + bf_ref[...]) / GATE_CAP))
        b = _tri_dot(ltri, lf)
        a = li - b
        cm = a
        sh = 1
        while sh < MLC:
            cm = jnp.where(rows >= sh, jnp.maximum(cm, pltpu.roll(cm, sh, 0)), cm)
            sh *= 2
        m0 = m_scr[...]
        A = jnp.maximum(cm, m0)
        a_ref[...] = a
        A_ref[...] = A
        A_last = A_ref[MLC - 1:MLC, :]
        wi_ref[...] = jnp.exp(m0 - A)
        em_ref[...] = jnp.exp(-(b + A))
        wk_ref[...] = jnp.exp(a - A_last)
        dec_ref[0] = jnp.exp(m0 - A_last)
        F_ref[...] = b
        m_scr[...] = F_ref[MLC - 1:MLC, :] + A_last
        lfg = _log_sigmoid(ps_ref[:, 2 * LANES:3 * LANES] + bff_ref[...])
        F_ref[...] = _tri_dot(ltri, lfg) + f_scr[...]
        f_scr[...] = F_ref[MLC - 1:MLC, :]

    col = pl.BlockSpec((MLC, LANES), lambda c: (c, 0))
    vec = pl.BlockSpec((1, LANES), lambda c: (0, 0))
    cs = jax.ShapeDtypeStruct((S, LANES), F32)
    return pl.pallas_call(
        body, name=name,
        out_shape=(cs, cs, cs, cs, cs, jax.ShapeDtypeStruct((NC, 1, LANES), F32), cs),
        grid=(NC,), in_specs=[pl.BlockSpec((MLC, N_SMALL), lambda c: (c, 0)), vec, vec, vec],
        out_specs=(col, col, col, col, col, pl.BlockSpec((1, 1, LANES), lambda c: (c, 0, 0)), col),
        scratch_shapes=[pltpu.VMEM((1, LANES), F32), pltpu.VMEM((1, LANES), F32)],
        compiler_params=_cparams(("arbitrary",)),
    )(ps, bi, bf, bff)


def _gates_bwd(ps, bi, bf, bff, rk, kc, tch, dF, name):
    S = ps.shape[0]
    NC = S // MLC

    def body(ps_ref, bi_ref, bf_ref, bff_ref, rk_ref, kc_ref, t_ref, dF_ref, dps_ref, db_ref, carry):
        @pl.when(pl.program_id(0) == 0)
        def _():
            carry[...] = jnp.zeros_like(carry)
            db_ref[...] = jnp.zeros_like(db_ref)

        lanes = lax.broadcasted_iota(jnp.int32, (MLC, LANES), 1)
        utri = _tri_mask(MLC, False).astype(BF16)
        ti = jnp.tanh((ps_ref[:, 0:LANES] + bi_ref[...]) / GATE_CAP)
        t_end, t_start = t_ref[0, 0:1, :], t_ref[0, 1:2, :]
        rk = rk_ref[...]
        rk = rk - (jnp.sum(rk, axis=0, keepdims=True) - (t_start - t_end)) * (1.0 / MLC)
        dpi = jnp.where(lanes < ML_HEADS, (kc_ref[...] - rk) * (1.0 - ti * ti), 0.0)
        tf = jnp.tanh((ps_ref[:, LANES:2 * LANES] + bf_ref[...]) / GATE_CAP)
        dlf = _tri_dot(utri, rk) + t_end
        dpf = jnp.where(lanes < ML_HEADS, dlf * jax.nn.sigmoid(-GATE_CAP * tf) * (1.0 - tf * tf), 0.0)
        dFv = dF_ref[...]
        dlfg = _tri_dot(utri, dFv) + carry[...]
        carry[...] += jnp.sum(dFv, axis=0, keepdims=True)
        dpff = jnp.where(lanes < FOX_HEADS, dlfg * jax.nn.sigmoid(-(ps_ref[:, 2 * LANES:3 * LANES] + bff_ref[...])), 0.0)
        for n, dp in enumerate((dpi, dpf, dpff)):
            dps_ref[:, n * LANES:(n + 1) * LANES] = dp.astype(dps_ref.dtype)
            db_ref[:, n * LANES:(n + 1) * LANES] += jnp.sum(dp, axis=0, keepdims=True)

    rev = lambda c: (NC - 1 - c, 0)
    col = pl.BlockSpec((MLC, LANES), rev)
    vec = pl.BlockSpec((1, LANES), lambda c: (0, 0))
    wide = pl.BlockSpec((MLC, N_SMALL), rev)
    return pl.pallas_call(
        body, name=name,
        out_shape=(jax.ShapeDtypeStruct((S, N_SMALL), BF16), jax.ShapeDtypeStruct((1, N_SMALL), F32)),
        grid=(NC,),
        in_specs=[wide, vec, vec, vec, col, col, pl.BlockSpec((1, 2, LANES), lambda c: (NC - 1 - c, 0, 0)), col],
        out_specs=(wide, pl.BlockSpec((1, N_SMALL), lambda c: (0, 0))),
        scratch_shapes=[pltpu.VMEM((1, LANES), F32)],
        compiler_params=_cparams(("arbitrary",)),
    )(ps, bi, bf, bff, rk, kc, tch, dF)


_ML_SCALE = ML_DQK ** -0.5


def _ml_specs(rev, NC):
    idx = (lambda c: NC - 1 - c) if rev else (lambda c: c)
    qk = lambda blk: pl.BlockSpec((MLC, ML_HEADS * ML_DQK), lambda c: (idx(c), blk))
    wide = lambda blk: pl.BlockSpec((MLC, D_MODEL), lambda c: (idx(c), blk))
    col = pl.BlockSpec((MLC, LANES), lambda c: (idx(c), 0))
    return idx, qk, wide, col


def _ml_intra(q_ref, k_ref, arow_ref, A_ref, h):
    hs = slice(h * ML_DQK, (h + 1) * ML_DQK)
    qf = q_ref[:, hs] * _ML_SCALE
    kf = k_ref[:, hs]
    qb, kb = qf.astype(BF16), kf.astype(BF16)
    qk = _dot_nt(qb, kb)
    logw = arow_ref[h:h + 1, :] - A_ref[:, h:h + 1]
    W = jnp.exp(jnp.where(_tri_mask(MLC, True), logw, -1e30))
    return qb, kb, qf, kf, qk, W


def _mlstm_fwd(pm, a_row, A, wi, em, wk, dec, w_hn, name):
    S = pm.shape[0]
    NC = S // MLC
    _, qk, wide, col = _ml_specs(False, NC)

    def body(q_ref, k_ref, v_ref, o_ref, arow_ref, A_ref, wi_ref, em_ref, wk_ref, dec_ref, whn_ref,
             ha_ref, hp_ref, den_ref, cst_ref, nst_ref, C_scr, n_scr):
        @pl.when(pl.program_id(0) == 0)
        def _():
            C_scr[...] = jnp.zeros_like(C_scr)
            n_scr[...] = jnp.zeros_like(n_scr)

        lanes = lax.broadcasted_iota(jnp.int32, (MLC, LANES), 1)
        den_tile = jnp.zeros((MLC, LANES), F32)
        for h in range(ML_HEADS):
            vs = slice(h * ML_DV, (h + 1) * ML_DV)
            qb, kb, qf, kf, qk_, W = _ml_intra(q_ref, k_ref, arow_ref, A_ref, h)
            vb = v_ref[:, vs].astype(BF16)
            Cf = C_scr[h]
            Cb = Cf.astype(BF16)
            nrow = n_scr[h]
            cst_ref[0, h] = Cb
            nst_ref[0, h] = nrow
            s = qk_ * W
            wic = wi_ref[:, h:h + 1]
            num = _dot_nn(s.astype(BF16), vb) + wic * _dot_nt(qb, Cb)
            den = jnp.sum(s, axis=1, keepdims=True) + wic * jnp.sum(qf * nrow, axis=1, keepdims=True)
            hp = num / jnp.maximum(jnp.abs(den), em_ref[:, h:h + 1])
            hp_ref[:, vs] = hp
            den_tile = jnp.where(lanes == h, den, den_tile)
            hn = hp * _rstd(hp) * whn_ref[:, vs]
            ha_ref[:, vs] = (hn * jax.nn.sigmoid(o_ref[:, vs])).astype(ha_ref.dtype)
            wkc = wk_ref[:, h:h + 1]
            kw = kf * wkc
            d = dec_ref[0, :, h:h + 1]
            C_scr[h] = d * Cf + _dot_tn(vb, kw.astype(BF16))
            n_scr[h] = d * nrow + jnp.sum(kw, axis=0, keepdims=True)
        den_ref[...] = den_tile

    return pl.pallas_call(
        body, name=name,
        out_shape=(jax.ShapeDtypeStruct((S, D_MODEL), BF16), jax.ShapeDtypeStruct((S, D_MODEL), F32),
                   jax.ShapeDtypeStruct((S, LANES), F32),
                   jax.ShapeDtypeStruct((NC, ML_HEADS, ML_DV, ML_DQK), BF16),
                   jax.ShapeDtypeStruct((NC, ML_HEADS, 1, ML_DQK), F32)),
        grid=(NC,),
        in_specs=[qk(C_QM // 512), qk(C_KM // 512), wide(C_VM // D_MODEL), wide(C_OM // D_MODEL),
                  pl.BlockSpec((8, MLC), lambda c: (0, c)), col, col, col, col,
                  pl.BlockSpec((1, 1, LANES), lambda c: (c, 0, 0)), pl.BlockSpec((1, D_MODEL), lambda c: (0, 0))],
        out_specs=(pl.BlockSpec((MLC, D_MODEL), lambda c: (c, 0)), pl.BlockSpec((MLC, D_MODEL), lambda c: (c, 0)),
                   col, pl.BlockSpec((1, ML_HEADS, ML_DV, ML_DQK), lambda c: (c, 0, 0, 0)),
                   pl.BlockSpec((1, ML_HEADS, 1, ML_DQK), lambda c: (c, 0, 0, 0))),
        scratch_shapes=[pltpu.VMEM((ML_HEADS, ML_DV, ML_DQK), F32), pltpu.VMEM((ML_HEADS, 1, ML_DQK), F32)],
        compiler_params=_cparams(("arbitrary",)),
    )(pm, pm, pm, pm, a_row, A, wi, em, wk, dec, w_hn)


def _mlstm_bwd(dha, pm, hp_all, den_all, a_row, A, wi, em, wk, dec, cst, nst, w_hn, name):
    S = pm.shape[0]
    NC = S // MLC
    idx, qk, wide, col = _ml_specs(True, NC)

    def body(dha_ref, q_ref, k_ref, v_ref, o_ref, hp_ref, den_ref, arow_ref, A_ref, wi_ref, em_ref, wk_ref,
             dec_ref, cst_ref, nst_ref, whn_ref,
             dqk_ref, dv_ref, do_ref, rk_ref, kc_ref, t_ref, dwhn_ref, dC_scr, dn_scr, t_scr):
        @pl.when(pl.program_id(0) == 0)
        def _():
            dC_scr[...] = jnp.zeros_like(dC_scr)
            dn_scr[...] = jnp.zeros_like(dn_scr)
            t_scr[...] = jnp.zeros_like(t_scr)
            dwhn_ref[...] = jnp.zeros_like(dwhn_ref)

        lanes = lax.broadcasted_iota(jnp.int32, (MLC, LANES), 1)
        lane1 = lax.broadcasted_iota(jnp.int32, (1, LANES), 1)
        t_ref[0, 0:1, :] = t_scr[...]
        rk_tile = jnp.zeros((MLC, LANES), F32)
        kc_tile = jnp.zeros((MLC, LANES), F32)
        t_new = jnp.zeros((1, LANES), F32)
        for h in range(ML_HEADS):
            hs = slice(h * ML_DQK, (h + 1) * ML_DQK)
            vs = slice(h * ML_DV, (h + 1) * ML_DV)
            hp = hp_ref[:, vs]
            sig = jax.nn.sigmoid(o_ref[:, vs])
            whn = whn_ref[:, vs]
            r = _rstd(hp)
            dga = dha_ref[:, vs]
            do_ref[:, vs] = (dga * (hp * r * whn) * sig * (1.0 - sig)).astype(do_ref.dtype)
            dhn = dga * sig
            dhp, dwt = _rmsnorm_bwd_math(dhn, hp, whn)
            dwhn_ref[:, vs] += jnp.sum(dwt, axis=0, keepdims=True)
            den = den_ref[:, h:h + 1]
            floor = em_ref[:, h:h + 1]
            D = jnp.maximum(jnp.abs(den), floor)
            dnum = dhp / D
            dh_h = jnp.sum(dhp * hp, axis=1, keepdims=True)
            active = jnp.abs(den) >= floor
            dden = -dh_h / D * jnp.where(active, jnp.sign(den), 0.0)
            phi = jnp.where(active, 0.0, dh_h)
            qb, kb, qf, kf, qk_, W = _ml_intra(q_ref, k_ref, arow_ref, A_ref, h)
            vf = v_ref[:, vs]
            vb = vf.astype(BF16)
            Cb = cst_ref[0, h]
            nrow = nst_ref[0, h]
            wic = wi_ref[:, h:h + 1]
            wkc = wk_ref[:, h:h + 1]
            d = dec_ref[0, :, h:h + 1]
            dCn = dC_scr[h]
            dCb = dCn.astype(BF16)
            dnn = dn_scr[h]
            dnumb = dnum.astype(BF16)
            s = qk_ * W
            ds = (_dot_nt(dnumb, vb) + dden) * W
            dsb = ds.astype(BF16)
            dnw = (wic * dnum).astype(BF16)
            wd = wic * dden
            kw = kf * wkc
            dv_state = _dot_nt(kw.astype(BF16), dCb)
            dq = _dot_nn(dsb, kb) + _dot_nn(dnw, Cb) + wd * nrow
            dk_state = wkc * (_dot_nn(vb, dCb) + dnn)
            dk = _dot_tn(dsb, qb) + dk_state
            dv = _dot_tn(s.astype(BF16), dnumb) + dv_state
            dC = d * dCn + _dot_tn(dnw, qb)
            dn = d * dnn + jnp.sum(wd * qf, axis=0, keepdims=True)
            dC_scr[h] = dC
            dn_scr[h] = dn
            dqk_ref[:, hs] = (dq * _ML_SCALE).astype(dqk_ref.dtype)
            dqk_ref[:, C_KM + h * ML_DQK:C_KM + (h + 1) * ML_DQK] = dk.astype(dqk_ref.dtype)
            dv_ref[:, vs] = dv.astype(dv_ref.dtype)
            G = ds * qk_
            inter = _dot_nt(qb, Cb)
            qn = jnp.sum(qf * nrow, axis=1, keepdims=True)
            R = (jnp.sum(G, axis=1, keepdims=True)
                 + wic * (jnp.sum(dnum * inter, axis=1, keepdims=True) + dden * qn))
            K = jnp.sum(G.T, axis=1, keepdims=True) + jnp.sum(kf * dk_state, axis=1, keepdims=True)
            rk_tile = jnp.where(lanes == h, R - K, rk_tile)
            kc_tile = jnp.where(lanes == h, phi, kc_tile)
            tt = (jnp.sum(jnp.sum(dC * Cb.astype(F32), axis=1, keepdims=True), axis=0, keepdims=True)
                  + jnp.sum(dn * nrow, axis=1, keepdims=True))
            t_new = jnp.where(lane1 == h, tt, t_new)
        rk_ref[...] = rk_tile
        kc_ref[...] = kc_tile
        t_ref[0, 1:2, :] = t_new
        t_scr[...] = t_new

    act = lambda n: jax.ShapeDtypeStruct((S, n), BF16)
    cs = jax.ShapeDtypeStruct((S, LANES), F32)
    rowblk = lambda n: pl.BlockSpec((MLC, n), lambda c: (idx(c), 0))
    return pl.pallas_call(
        body, name=name,
        out_shape=(act(D_MODEL), act(D_MODEL), act(D_MODEL), cs, cs,
                   jax.ShapeDtypeStruct((NC, 2, LANES), F32), jax.ShapeDtypeStruct((1, D_MODEL), F32)),
        grid=(NC,),
        in_specs=[rowblk(D_MODEL), qk(C_QM // 512), qk(C_KM // 512), wide(C_VM // D_MODEL), wide(C_OM // D_MODEL),
                  rowblk(D_MODEL), col, pl.BlockSpec((8, MLC), lambda c: (0, idx(c))), col, col, col, col,
                  pl.BlockSpec((1, 1, LANES), lambda c: (idx(c), 0, 0)),
                  pl.BlockSpec((1, ML_HEADS, ML_DV, ML_DQK), lambda c: (idx(c), 0, 0, 0)),
                  pl.BlockSpec((1, ML_HEADS, 1, ML_DQK), lambda c: (idx(c), 0, 0, 0)),
                  pl.BlockSpec((1, D_MODEL), lambda c: (0, 0))],
        out_specs=(rowblk(D_MODEL), rowblk(D_MODEL), rowblk(D_MODEL), col, col,
                   pl.BlockSpec((1, 2, LANES), lambda c: (idx(c), 0, 0)), pl.BlockSpec((1, D_MODEL), lambda c: (0, 0))),
        scratch_shapes=[pltpu.VMEM((ML_HEADS, ML_DV, ML_DQK), F32), pltpu.VMEM((ML_HEADS, 1, ML_DQK), F32),
                        pltpu.VMEM((1, LANES), F32)],
        compiler_params=_cparams(("arbitrary",)),
    )(dha, pm, pm, pm, pm, hp_all, den_all, a_row, A, wi, em, wk, dec, cst, nst, w_hn)


_FOX_SCALE = FOX_DH ** -0.5
_NEG = -1e30
_LOG2E = 1.4426950408889634
_LN2 = 0.6931471805599453
_QF_BLK, _KF_BLK, _VF_BLK = 0, FOX_HEADS, 2 * FOX_HEADS


def _lane_pick(tile, lane):
    lanes = lax.broadcasted_iota(jnp.int32, tile.shape, 1)
    return jnp.sum(jnp.where(lanes == lane, tile, 0.0), axis=1, keepdims=True)


def _col_to_row(col):
    return jnp.max(jnp.broadcast_to(col, (col.shape[0], LANES)).T, axis=0, keepdims=True)


def _causal(q0, k0, shape, q_axis):
    qpos = q0 + lax.broadcasted_iota(jnp.int32, shape, q_axis)
    kpos = k0 + lax.broadcasted_iota(jnp.int32, shape, 1 - q_axis)
    return kpos <= qpos


def _fox_fwd(pf, fc, fk_row, name):
    S = pf.shape[0]
    TQ, TK = FOX_TQ_FWD, FOX_TK_FWD
    nq, nk = S // TQ, S // TK
    c1 = _FOX_SCALE * _LOG2E

    def body(q_ref, k_ref, v_ref, fc_ref, fr_ref, o_ref, lse_ref):
        h, i = pl.program_id(0), pl.program_id(1)
        qb = q_ref[...]
        fq2 = _lane_pick(fc_ref[...], h) * _LOG2E

        def step(j, carry, masked):
            m, l, acc = carry
            off = pl.multiple_of(j * TK, TK)
            t = _dot_nt(qb, k_ref[pl.ds(off, TK), :]) * c1 - fr_ref[0, j] * _LOG2E
            if masked:
                t = jnp.where(_causal(i * TQ, j * TK, (TQ, TK), 0), t, _NEG)
            m_new = jnp.maximum(m, jnp.max(t, axis=1, keepdims=True) + fq2)
            alpha = jnp.exp2(m - m_new)
            p = jnp.exp2(t + (fq2 - m_new))
            l = alpha * l + jnp.sum(p, axis=1, keepdims=True)
            acc = alpha * acc + _dot_nn(p.astype(BF16), v_ref[pl.ds(off, TK), :])
            return m_new, l, acc

        init = (jnp.full((TQ, 1), _NEG, F32), jnp.zeros((TQ, 1), F32), jnp.zeros((TQ, FOX_DH), F32))
        last = (i * TQ) // TK
        carry = lax.fori_loop(0, last, lambda j, c: step(j, c, False), init)
        for d in range(max(1, TQ // TK)):
            carry = step(last + d, carry, True)
        m, l, acc = carry
        o_ref[...] = (acc / l).astype(o_ref.dtype)
        lse_ref[0, 0] = _col_to_row((m + jnp.log2(l)) * _LN2)

    head = lambda blk: pl.BlockSpec((S, FOX_DH), lambda h, i: (0, blk + h))
    return pl.pallas_call(
        body, name=name,
        out_shape=(jax.ShapeDtypeStruct((S, D_MODEL), BF16), jax.ShapeDtypeStruct((FOX_HEADS, nq, 1, TQ), F32)),
        grid=(FOX_HEADS, nq),
        in_specs=[pl.BlockSpec((TQ, FOX_DH), lambda h, i: (i, _QF_BLK + h)), head(_KF_BLK), head(_VF_BLK),
                  pl.BlockSpec((TQ, LANES), lambda h, i: (i, 0)),
                  pl.BlockSpec((1, nk, 1, TK), lambda h, i: (h, 0, 0, 0))],
        out_specs=(pl.BlockSpec((TQ, FOX_DH), lambda h, i: (i, h)),
                   pl.BlockSpec((1, 1, 1, TQ), lambda h, i: (h, i, 0, 0))),
        compiler_params=_cparams(("parallel", "arbitrary")),
    )(pf, pf, pf, fc, fk_row)


def _fox_bwd(dhb, hb, pf, lse_row, fq_row, fc, name):
    S = pf.shape[0]
    TQ, TK = FOX_TQ, FOX_TK
    nq, nk, r = S // TQ, S // TK, TK // TQ
    c1 = _FOX_SCALE * _LOG2E

    def body(q_ref, k_ref, v_ref, do_ref, o_ref, lse_ref, fq_ref, fc_ref,
             dq_ref, dk_ref, dv_ref, dFk_ref, dFq_ref, dq_acc, qside, delta, dk_acc, dv_acc, cs_acc):
        h, j = pl.program_id(0), pl.program_id(1)

        @pl.when(j == 0)
        def _():
            dq_acc[...] = jnp.zeros_like(dq_acc)
            dFq_ref[...] = jnp.zeros_like(dFq_ref)

            def fill(b, _):
                off = pl.multiple_of(b * TQ, TQ)
                prod = do_ref[pl.ds(off, TQ), :].astype(F32) * o_ref[pl.ds(off, TQ), :].astype(F32)
                delta[b] = jnp.sum(prod.T, axis=0, keepdims=True)
                qside[b] = (fq_ref[0, b] - lse_ref[0, b]) * _LOG2E
                return 0

            lax.fori_loop(0, nq, fill, 0)

        kb = k_ref[...]
        vb = v_ref[...]
        fk2 = _lane_pick(fc_ref[...], h) * _LOG2E
        dk_acc[...] = jnp.zeros_like(dk_acc)
        dv_acc[...] = jnp.zeros_like(dv_acc)
        cs_acc[...] = jnp.zeros_like(cs_acc)

        def step(i, masked):
            off = pl.multiple_of(i * TQ, TQ)
            qb = q_ref[pl.ds(off, TQ), :]
            dob = do_ref[pl.ds(off, TQ), :]
            t = _dot_nt(kb, qb) * c1 + qside[i] - fk2
            if masked:
                t = jnp.where(_causal(i * TQ, j * TK, (TK, TQ), 1), t, _NEG)
            p = jnp.exp2(t)
            dv_acc[...] += _dot_nn(p.astype(BF16), dob)
            ds = p * (_dot_nt(vb, dob) - delta[i])
            dsb = ds.astype(BF16)
            dk_acc[...] += _dot_nn(dsb, qb)
            dq_acc[pl.ds(off, TQ), :] += _dot_tn(dsb, kb)
            cs_acc[...] += jnp.sum(ds, axis=1, keepdims=True)
            dFq_ref[0, i] += jnp.sum(ds, axis=0, keepdims=True)

        for d in range(r):
            step(r * j + d, True)

        def rest(i, _):
            step(i, False)
            return 0

        lax.fori_loop(r * j + r, nq, rest, 0)
        dk_ref[...] = (dk_acc[...] * _FOX_SCALE).astype(dk_ref.dtype)
        dv_ref[...] = dv_acc[...].astype(dv_ref.dtype)
        dFk_ref[0, 0] = -_col_to_row(cs_acc[...])

        @pl.when(j == nk - 1)
        def _():
            dq_ref[...] = (dq_acc[...] * _FOX_SCALE).astype(dq_ref.dtype)

    head = lambda blk: pl.BlockSpec((S, FOX_DH), lambda h, j: (0, blk + h))
    kblk = lambda blk: pl.BlockSpec((TK, FOX_DH), lambda h, j: (j, blk + h))
    qrows = pl.BlockSpec((1, nq, 1, TQ), lambda h, j: (h, 0, 0, 0))
    act = jax.ShapeDtypeStruct((S, D_MODEL), BF16)
    return pl.pallas_call(
        body, name=name,
        out_shape=(act, act, act, jax.ShapeDtypeStruct((FOX_HEADS, nk, 1, TK), F32),
                   jax.ShapeDtypeStruct((FOX_HEADS, nq, 1, TQ), F32)),
        grid=(FOX_HEADS, nk),
        in_specs=[head(_QF_BLK), kblk(_KF_BLK), kblk(_VF_BLK), head(0), head(0), qrows, qrows,
                  pl.BlockSpec((TK, LANES), lambda h, j: (j, 0))],
        out_specs=(head(0), kblk(0), kblk(0), pl.BlockSpec((1, 1, 1, TK), lambda h, j: (h, j, 0, 0)), qrows),
        scratch_shapes=[pltpu.VMEM((S, FOX_DH), F32), pltpu.VMEM((nq, 1, TQ), F32), pltpu.VMEM((nq, 1, TQ), F32),
                        pltpu.VMEM((TK, FOX_DH), F32), pltpu.VMEM((TK, FOX_DH), F32), pltpu.VMEM((TK, 1), F32)],
        compiler_params=_cparams(("parallel", "arbitrary")),
    )(pf, pf, pf, dhb, hb, lse_row, fq_row, fc)


def _pad_lanes(v):
    return jnp.pad(v, ((0, 0), (0, LANES - v.shape[1])))


def _local_step(x, target, wmain_t, wsmall_t, rest_arrived, rest_weights, p, on_grads, advance, token):
    S = x.shape[0]
    bi, bf, bff = _pad_lanes(p["b_ml_i"]), _pad_lanes(p["b_ml_f"]), _pad_lanes(p["b_fox_f"])

    h0 = _rmsnorm_fwd(x, p["norm_mix_pre"] + token[0:1, 0:1], "norm_mix_pre")
    pm = _mm(h0, wmain_t, "nt", F32, "proj_mlstm", b_rows=(0, N_ML))
    pf = _mm(h0, wmain_t, "nt", BF16, "proj_fox", b_rows=(N_ML, N_FOX))
    pg = _mm(h0, wmain_t, "nt", F32, "proj_merge", b_rows=(N_ML + N_FOX, N_GATE))
    ps = _mm(h0, wsmall_t, "nt", F32, "proj_gates")
    a, A, wi, em, wk, dec, Fc = _gates_fwd(ps, bi, bf, bff, "gates_fwd")
    a_row = a[:, :8].T
    ha, hp, den, cst, nst = _mlstm_fwd(pm, a_row, A, wi, em, wk, dec, p["ml_head_norm"], "mlstm_fwd")
    ft = Fc[:, :FOX_HEADS].T + rest_arrived(ha)[0, 0]
    fq_row = ft.reshape(FOX_HEADS, S // FOX_TQ, 1, FOX_TQ)
    fk_row = ft.reshape(FOX_HEADS, S // FOX_TK, 1, FOX_TK)
    hb, lse_row = _fox_fwd(pf, Fc, ft.reshape(FOX_HEADS, S // FOX_TK_FWD, 1, FOX_TK_FWD), "fox_fwd")
    wa, wb, wout, wup, wdown = rest_weights(hb)
    ya = _mm(ha, wa, "nn", F32, "branch_a")
    yb = _mm(hb, wb, "nn", F32, "branch_b")
    merged = _merge_fwd(ya, yb, pg, p["b_gate_a"], p["b_gate_b"], "merge_fwd")
    z = _mm(merged, wout, "nn", F32, "out_proj")
    x1, h2 = _resid_norm_fwd(x, z, p["norm_mix_post"], p["norm_ffn_pre"], "resid_mix")
    up = _mm(h2, wup, "nn", F32, "ffn_up")
    act, conv_a, conv_g = _conv_act_fwd(up, p["conv_w"], p["conv_b"], "conv_act_fwd")
    d = _mm(act, wdown, "nn", F32, "ffn_down", tk=D_FF)
    loss_row, dy, dd, g_norm_ffn_post = _loss_head(x1, d, p["norm_ffn_post"], target, "loss_head")
    dact = _mm(dd, wdown, "nt", F32, "d_act")
    g_wdown = _mm(act, dd, "tn", F32, "dw_down", tm=1408, tk=2048)
    dupa, dupg, dcwa, dcwg, dcba, dcbg = _conv_act_bwd(up, conv_a, conv_g, dact, p["conv_w"], "conv_act_bwd")
    g_conv_w = jnp.concatenate([dcwa, dcwg], axis=1)
    g_conv_b = jnp.concatenate([dcba, dcbg], axis=1)
    dh2 = _mm_sum_parts([dupa, dupg], wup, F32, "d_h2", trans_b=True)
    g_wup = _mm(h2, [dupa, dupg], "tn", F32, "dw_up", tk=2048)
    token = on_grads("ffn", dict(w_up=g_wup, w_down=g_wdown))
    dx1, dz, g_norm_ffn_pre, g_norm_mix_post = _norm_chain_bwd(
        dh2, x1, p["norm_ffn_pre"] + token[0:1, 0:1], dy, z, p["norm_mix_post"], "norm_chain_bwd")
    dmerged = _mm(dz, wout, "nt", F32, "d_merged")
    g_wout = _mm(merged, dz, "tn", F32, "dw_out", tk=2048)
    dya, dyb, dga, dgb, g_b_gate_a, g_b_gate_b = _merge_bwd(dmerged, ya, yb, pg, p["b_gate_a"], p["b_gate_b"], "merge_bwd")
    dha = _mm(dya, wa, "nt", F32, "d_ha")
    g_wa = _mm(ha, dya, "tn", F32, "dw_a", tk=2048)
    dhb = _mm(dyb, wb, "nt", BF16, "d_hb")
    g_wb = _mm(hb, dyb, "tn", F32, "dw_b", tk=2048)
    token = advance("ffn", g_wb) + on_grads("mix", dict(w_out=g_wout, w_branch_a=g_wa, w_branch_b=g_wb))
    dqkm, dvm, dom, rk, kc, tch, g_ml_head_norm = _mlstm_bwd(
        dha, pm, hp, den, a_row, A, wi, em, wk, dec, cst, nst, p["ml_head_norm"] + token[0:1, 0:1], "mlstm_bwd")
    token = advance("mix", dqkm)
    dqf, dkf, dvf, dFk, dFq = _fox_bwd(dhb, hb, pf, lse_row.reshape(fq_row.shape), fq_row + token[0, 0], Fc, "fox_bwd")
    dF = jnp.pad((dFk.reshape(FOX_HEADS, S) + dFq.reshape(FOX_HEADS, S)).T, ((0, 0), (0, LANES - FOX_HEADS)))
    dps, dbias = _gates_bwd(ps, bi, bf, bff, rk, kc, tch, dF, "gates_bwd")
    dpm = [dqkm, dvm, dom, dqf, dkf, dvf, dga, dgb]
    g_wmain_t = _mm(dpm, h0, "tn", F32, "dw_main")
    token = on_grads("in", dict(w_in=g_wmain_t))
    g_wsmall_t = _mm(dps, h0, "tn", F32, "dw_gates")
    dh0s = _mm(dps, wsmall_t + token[0:1, 0:1].astype(BF16), "nn", F32, "d_h0_gates")
    token = advance("in", dh0s)
    dh0 = _mm_sum_parts(dpm, wmain_t, F32, "d_h0_main", after=token)
    grad_x, g_norm_mix_pre = _rmsnorm_bwd([dh0, dh0s], x, p["norm_mix_pre"], dx1, F32, "norm_mix_pre_bwd")

    big = dict(wsmall_t=g_wsmall_t)
    small = dict(norm_mix_pre=g_norm_mix_pre, ml_head_norm=g_ml_head_norm, b_gate_a=g_b_gate_a, b_gate_b=g_b_gate_b,
                 norm_mix_post=g_norm_mix_post, norm_ffn_pre=g_norm_ffn_pre, norm_ffn_post=g_norm_ffn_post,
                 conv_b=g_conv_b, b_ml_i=dbias[:, 0:ML_HEADS], b_ml_f=dbias[:, LANES:LANES + ML_HEADS],
                 b_fox_f=dbias[:, 2 * LANES:2 * LANES + FOX_HEADS], conv_w=g_conv_w)
    return loss_row, grad_x, big, small


def _row_tile(r, target=256):
    best = None
    for t in range(8, min(r, target) + 1, 8):
        if r % t == 0:
            best = t
    return best if best is not None else r


def _adamw(w, g, m, v, name):
    _, R, C = w.shape
    tr = _row_tile(R)
    tc = C
    if tr == R and R > 256:
        tc = 256

    def body(w_ref, g_ref, m_ref, v_ref, d_ref, mo_ref, vo_ref):
        gv = g_ref[...]
        mn = ADAM_B1 * m_ref[0] + (1.0 - ADAM_B1) * gv
        vn = ADAM_B2 * v_ref[0] + (1.0 - ADAM_B2) * (gv * gv)
        m_hat = mn / (1.0 - ADAM_B1 ** ADAM_STEP)
        v_hat = vn / (1.0 - ADAM_B2 ** ADAM_STEP)
        d_ref[0] = -ADAM_LR * (m_hat / (jnp.sqrt(v_hat) + ADAM_EPS) + ADAM_WD * w_ref[0])
        mo_ref[0] = mn
        vo_ref[0] = vn

    blk = pl.BlockSpec((1, tr, tc), lambda i, j: (0, i, j))
    o = jax.ShapeDtypeStruct((1, R, C), F32)
    return pl.pallas_call(
        body, name=name, out_shape=(o, o, o), grid=(R // tr, C // tc),
        in_specs=[blk, pl.BlockSpec((tr, tc), lambda i, j: (i, j)), blk, blk], out_specs=(blk,) * 3,
        compiler_params=_cparams(("parallel", "parallel")),
    )(w, g, m, v)


ANY = pl.BlockSpec(memory_space=pl.ANY)


def _place():
    x, y, c = lax.axis_index("x"), lax.axis_index("y"), lax.axis_index("c")
    chips = [(1 - x, y), (x, 1 - y), (1 - x, 1 - y)]
    return x, y, c, chips


def _block(ref, kind, k, rows=None):
    if kind == "rows":
        return ref.at[k] if rows is None else ref.at[k, pl.ds(*rows), :]
    cb = ref.shape[1] // 4
    return ref.at[:, pl.ds(k * cb, cb)] if rows is None else ref.at[pl.ds(*rows), pl.ds(k * cb, cb)]


def _gathered_shape(s, kind):
    return (4,) + s.shape if kind == "rows" else (s.shape[0], 4 * s.shape[1])


def _gather_weights(shards, kinds, smalls):
    n, ns = len(shards), len(smalls)

    def body(*refs):
        ins, sm_in = refs[:n], refs[n:n + ns]
        outs, sm_out = refs[n + ns:2 * n + ns], refs[2 * n + ns:2 * (n + ns)]
        send_sems, recv_sems, sm_send, sm_recv, local_sems = refs[2 * (n + ns):]
        x, y, c, chips = _place()
        sibling = (x, y, 1 - c)
        kme = 2 * x + y

        def half(a, k, hc):
            h = ins[a].shape[0] // 2
            return _block(outs[a], kinds[a], k, (hc * h, h))

        def remote(a, slot, src, dst, to):
            return pltpu.make_async_remote_copy(src_ref=src, dst_ref=dst, send_sem=send_sems.at[a * 7 + slot],
                                                recv_sem=recv_sems.at[a * 7 + slot], device_id=to, device_id_type=MESH)

        def sm_copy(b, j, k, to):
            return pltpu.make_async_remote_copy(src_ref=sm_in[b], dst_ref=sm_out[b].at[k], send_sem=sm_send.at[3 * b + j],
                                                recv_sem=sm_recv.at[3 * b + j], device_id=to, device_id_type=MESH)

        local = [pltpu.make_async_copy(sm_in[b], sm_out[b].at[kme], local_sems.at[b]) for b in range(ns)]
        for cp in local:
            cp.start()
        sends = [remote(a, 6, ins[a], _block(outs[a], kinds[a], kme), sibling) for a in range(n)]
        for a in range(n):
            h = ins[a].shape[0] // 2
            for j, chip in enumerate(chips):
                sends.append(remote(a, j, ins[a].at[pl.ds(c * h, h), :], half(a, kme, c), (*chip, c)))
        for b in range(ns):
            for j, chip in enumerate(chips):
                sends.append(sm_copy(b, j, kme, (*chip, c)))
        for cp in sends:
            cp.start()
        for a in range(n):
            for j, chip in enumerate(chips):
                kj = 2 * chip[0] + chip[1]
                remote(a, j, half(a, kj, c), half(a, kj, c), (*chip, c)).wait_recv()
                fwd = remote(a, 3 + j, half(a, kj, c), half(a, kj, c), sibling)
                fwd.start()
                sends.append(fwd)
        for a in range(n):
            for j, chip in enumerate(chips):
                kj = 2 * chip[0] + chip[1]
                remote(a, 3 + j, half(a, kj, 1 - c), half(a, kj, 1 - c), sibling).wait_recv()
        for b in range(ns):
            for j, chip in enumerate(chips):
                sm_copy(b, j, 2 * chip[0] + chip[1], (*chip, c)).wait_recv()
        for a in range(n):
            remote(a, 6, ins[a], _block(outs[a], kinds[a], kme), sibling).wait_recv()
        for cp in sends:
            cp.wait_send()
        for cp in local:
            cp.wait()

    outs = pl.pallas_call(
        body, name="gather_weights",
        out_shape=tuple([jax.ShapeDtypeStruct(_gathered_shape(s, k), s.dtype) for s, k in zip(shards, kinds)]
                        + [jax.ShapeDtypeStruct((4,) + s.shape, s.dtype) for s in smalls]),
        in_specs=[ANY] * (n + ns), out_specs=tuple([ANY] * (n + ns)),
        scratch_shapes=[pltpu.SemaphoreType.DMA((7 * n,)), pltpu.SemaphoreType.DMA((7 * n,)),
                        pltpu.SemaphoreType.DMA((3 * ns,)), pltpu.SemaphoreType.DMA((3 * ns,)),
                        pltpu.SemaphoreType.DMA((ns,))],
    )(*shards, *smalls)
    return outs[:n], outs[n:]


_IN_HBM = pl.BlockSpec(memory_space=pltpu.HBM)
_SEMS = pl.BlockSpec(memory_space=pltpu.SEMAPHORE)
_DATAFLOW = pltpu.SideEffectType.DATAFLOW_SIDE_EFFECTING


def _hbm(t):
    return pltpu.HBM(t.shape, t.dtype)


def _gather_copies(ins, outs, send_sems, recv_sems, kinds):
    x, y, c, chips = _place()
    kme = 2 * x + y
    cps = []
    for a in range(len(ins)):
        h = ins[a].shape[0] // 2
        for j, chip in enumerate(chips + [None]):
            to = (x, y, 1 - c) if chip is None else (*chip, c)
            src = ins[a] if chip is None else ins[a].at[pl.ds(c * h, h), :]
            dst = _block(outs[a], kinds[a], kme, None if chip is None else (c * h, h))
            cps.append(pltpu.make_async_remote_copy(src_ref=src, dst_ref=dst, send_sem=send_sems.at[4 * a + j],
                                                    recv_sem=recv_sems.at[4 * a + j], device_id=to, device_id_type=MESH))
    return cps


def _gather_start(shards, kinds, name):
    n = len(shards)
    outs = [lax.empty(_gathered_shape(s, k), s.dtype) for s, k in zip(shards, kinds)]

    def body(*refs):
        for cp in _gather_copies(refs[:n], refs[n:2 * n], refs[2 * n], refs[2 * n + 1], kinds):
            cp.start()
        refs[-1][...] = jnp.zeros_like(refs[-1])

    return pl.pallas_call(
        body, name=name,
        out_shape=(pltpu.SemaphoreType.DMA((4 * n,)), pltpu.SemaphoreType.DMA((4 * n,)),
                   *[_hbm(t) for t in shards], *[_hbm(t) for t in outs], jax.ShapeDtypeStruct((8, LANES), F32)),
        in_specs=[_IN_HBM] * (2 * n),
        out_specs=(_SEMS, _SEMS, *[_IN_HBM] * (2 * n), pl.BlockSpec(memory_space=pltpu.VMEM)),
        input_output_aliases={a: 2 + a for a in range(2 * n)},
        compiler_params=pltpu.CompilerParams(has_side_effects=_DATAFLOW),
    )(*[pltpu.with_memory_space_constraint(t, pltpu.HBM) for t in list(shards) + outs])


def _gather_wait(started, after, kinds, name):
    n = (len(started) - 3) // 2
    bufs = started[2:2 + 2 * n]

    def body(*refs):
        for cp in _gather_copies(refs[:n], refs[n:2 * n], refs[2 * n], refs[2 * n + 1], kinds):
            cp.wait_send()
            cp.wait_recv()

    outs = pl.pallas_call(
        body, name=name, out_shape=tuple(_hbm(t) for t in bufs),
        in_specs=[_IN_HBM] * (2 * n) + [_SEMS, _SEMS, ANY], out_specs=tuple([_IN_HBM] * (2 * n)),
        input_output_aliases={a: a for a in range(2 * n)},
        compiler_params=pltpu.CompilerParams(has_side_effects=_DATAFLOW),
    )(*bufs, started[0], started[1], after)
    return outs[n:]


def _relay_copies(bufs, send_sems, recv_sems, kinds):
    x, y, c, chips = _place()
    cps = []
    for a in range(len(bufs)):
        h = (bufs[a].shape[1] if kinds[a] == "rows" else bufs[a].shape[0]) // 2
        for j, chip in enumerate(chips):
            part = _block(bufs[a], kinds[a], 2 * chip[0] + chip[1], (c * h, h))
            cps.append(pltpu.make_async_remote_copy(src_ref=part, dst_ref=part, send_sem=send_sems.at[3 * a + j],
                                                    recv_sem=recv_sems.at[3 * a + j], device_id=(x, y, 1 - c),
                                                    device_id_type=MESH))
    return cps


def _join_copies(bufs, send_sems, recv_sems, kinds):
    x, y, c, _ = _place()
    cps = []
    for a in range(len(bufs)):
        h = bufs[a].shape[0] // 2
        mine = bufs[a].at[pl.ds(c * h, h), :]
        cps.append(pltpu.make_async_remote_copy(src_ref=mine, dst_ref=mine, send_sem=send_sems.at[a],
                                                recv_sem=recv_sems.at[a], device_id=(x, y, 1 - c), device_id_type=MESH))
    return cps


def _inplace_start(copies, per_array, bufs, kinds, name):
    n = len(bufs)

    def body(*refs):
        for cp in copies(refs[:n], refs[n], refs[n + 1], kinds):
            cp.start()
        refs[-1][...] = jnp.zeros_like(refs[-1])

    return pl.pallas_call(
        body, name=name,
        out_shape=(pltpu.SemaphoreType.DMA((per_array * n,)), pltpu.SemaphoreType.DMA((per_array * n,)),
                   *[_hbm(t) for t in bufs], jax.ShapeDtypeStruct((8, LANES), F32)),
        in_specs=[_IN_HBM] * n, out_specs=(_SEMS, _SEMS, *[_IN_HBM] * n, pl.BlockSpec(memory_space=pltpu.VMEM)),
        input_output_aliases={a: 2 + a for a in range(n)},
        compiler_params=pltpu.CompilerParams(has_side_effects=_DATAFLOW),
    )(*[pltpu.with_memory_space_constraint(t, pltpu.HBM) for t in bufs])


def _inplace_wait(copies, started, after, kinds, name):
    n = len(started) - 3
    bufs = started[2:2 + n]

    def body(*refs):
        for cp in copies(refs[:n], refs[n], refs[n + 1], kinds):
            cp.wait_send()
            cp.wait_recv()

    return pl.pallas_call(
        body, name=name, out_shape=tuple(_hbm(t) for t in bufs),
        in_specs=[_IN_HBM] * n + [_SEMS, _SEMS, ANY], out_specs=tuple([_IN_HBM] * n),
        input_output_aliases={a: a for a in range(n)},
        compiler_params=pltpu.CompilerParams(has_side_effects=_DATAFLOW),
    )(*bufs, started[0], started[1], after)


def _add_halves(g, r1, cvec, kind, name):
    def body(c_ref, g_ref, r_ref, o_ref):
        o_ref[...] = (g_ref[...] + r_ref[...]).astype(o_ref.dtype)

    if kind == "rows":
        _, h, C = r1.shape
        tr = _row_tile(h, 512)
        nt = h // tr
        grid = (4, nt)
        g_spec = pl.BlockSpec((1, tr, C), lambda k, i, c_ref: (k, c_ref[0] * nt + i, 0))
        r_spec = pl.BlockSpec((1, tr, C), lambda k, i, c_ref: (k, i, 0))
    else:
        h, C4 = r1.shape
        tr, tc = _row_tile(h, 512), C4 // 4
        nt = h // tr
        grid = (nt, 4)
        g_spec = pl.BlockSpec((tr, tc), lambda i, k, c_ref: (c_ref[0] * nt + i, k))
        r_spec = pl.BlockSpec((tr, tc), lambda i, k, c_ref: (i, k))
    return pl.pallas_call(
        body, name=name, out_shape=jax.ShapeDtypeStruct(r1.shape, BF16),
        grid_spec=pltpu.PrefetchScalarGridSpec(num_scalar_prefetch=1, grid=grid, in_specs=[g_spec, r_spec],
                                               out_specs=r_spec),
        compiler_params=_cparams(("parallel", "parallel")),
    )(cvec, g, r1)


def _chip_copies(ins, lands, send_sems, recv_sems, kinds):
    x, y, c, chips = _place()
    return [pltpu.make_async_remote_copy(
        src_ref=_block(ins[a], kinds[a], 2 * chip[0] + chip[1]), dst_ref=lands[a].at[j],
        send_sem=send_sems.at[3 * a + j], recv_sem=recv_sems.at[3 * a + j], device_id=(*chip, c), device_id_type=MESH)
        for a in range(len(ins)) for j, chip in enumerate(chips)]


def _land_shape(s, kind):
    return (3,) + (s.shape[1:] if kind == "rows" else (s.shape[0], s.shape[1] // 4))


def _sibling_copies(ins, lands, send_sems, recv_sems, kinds):
    x, y, c, _ = _place()
    cps = []
    for a in range(len(ins)):
        h = lands[a].shape[-2]
        src = ins[a].at[:, pl.ds((1 - c) * h, h), :] if kinds[a] == "rows" else ins[a].at[pl.ds((1 - c) * h, h), :]
        cps.append(pltpu.make_async_remote_copy(src_ref=src, dst_ref=lands[a], send_sem=send_sems.at[a],
                                                recv_sem=recv_sems.at[a], device_id=(x, y, 1 - c), device_id_type=MESH))
    return cps


def _half_shape(g, kind):
    return (4, g.shape[1] // 2, g.shape[2]) if kind == "rows" else (g.shape[0] // 2, g.shape[1])


def _exchange_start(copies, per_array, srcs, land_shapes, kinds, name, zeroed=False):
    n = len(srcs)
    lands = [(jnp.zeros if zeroed else lax.empty)(shape, s.dtype) for shape, s in zip(land_shapes, srcs)]

    def body(*refs):
        for cp in copies(refs[:n], refs[n:2 * n], refs[2 * n], refs[2 * n + 1], kinds):
            cp.start()
        refs[-1][...] = jnp.zeros_like(refs[-1])

    return pl.pallas_call(
        body, name=name,
        out_shape=(pltpu.SemaphoreType.DMA((per_array * n,)), pltpu.SemaphoreType.DMA((per_array * n,)),
                   *[_hbm(t) for t in srcs], *[_hbm(t) for t in lands], jax.ShapeDtypeStruct((8, LANES), F32)),
        in_specs=[_IN_HBM] * (2 * n),
        out_specs=(_SEMS, _SEMS, *[_IN_HBM] * (2 * n), pl.BlockSpec(memory_space=pltpu.VMEM)),
        input_output_aliases={a: 2 + a for a in range(2 * n)},
        compiler_params=pltpu.CompilerParams(has_side_effects=_DATAFLOW),
    )(*[pltpu.with_memory_space_constraint(t, pltpu.HBM) for t in list(srcs) + lands])


def _exchange_wait(copies, started, after, kinds, name):
    n = (len(started) - 3) // 2
    bufs = started[2:2 + 2 * n]

    def body(*refs):
        for cp in copies(refs[:n], refs[n:2 * n], refs[2 * n], refs[2 * n + 1], kinds):
            cp.wait_send()
            cp.wait_recv()

    outs = pl.pallas_call(
        body, name=name, out_shape=tuple(_hbm(t) for t in bufs),
        in_specs=[_IN_HBM] * (2 * n) + [_SEMS, _SEMS, ANY], out_specs=tuple([_IN_HBM] * (2 * n)),
        input_output_aliases={a: a for a in range(2 * n)},
        compiler_params=pltpu.CompilerParams(has_side_effects=_DATAFLOW),
    )(*bufs, started[0], started[1], after)
    return outs[:n], outs[n:]


def _add_chips(s1, r2, kcvec, kind, name):
    _, h, C = r2.shape
    tr = _row_tile(h, 512)
    nt = h // tr

    def body(kc_ref, s_ref, r0_ref, r1_ref, r2_ref, o_ref):
        s = s_ref[0] if kind == "rows" else s_ref[...]
        o_ref[...] = ((s.astype(F32) + r0_ref[0].astype(F32)) + r1_ref[0].astype(F32)) + r2_ref[0].astype(F32)

    peer = lambda j: pl.BlockSpec((1, tr, C), lambda i, kc_ref: (j, i, 0))
    if kind == "rows":
        s_spec = pl.BlockSpec((1, tr, C), lambda i, kc_ref: (kc_ref[0], i, 0))
    else:
        s_spec = pl.BlockSpec((tr, C), lambda i, kc_ref: (i, kc_ref[0]))
    return pl.pallas_call(
        body, name=name, out_shape=jax.ShapeDtypeStruct((2 * h, C), F32),
        grid_spec=pltpu.PrefetchScalarGridSpec(
            num_scalar_prefetch=1, grid=(nt,),
            in_specs=[s_spec, peer(0), peer(1), peer(2)],
            out_specs=pl.BlockSpec((tr, C), lambda i, kc_ref: (kc_ref[1] * nt + i, 0))),
        compiler_params=_cparams(("parallel",)),
    )(kcvec, s1, r2, r2, r2)


N_DEV = 8


def _spread_copies(packs, lands, send_sems, recv_sems, kinds):
    x, y, c, _ = _place()
    me = 4 * x + 2 * y + c
    return [pltpu.make_async_remote_copy(
        src_ref=packs[0], dst_ref=lands[0].at[me], send_sem=send_sems.at[mask - 1], recv_sem=recv_sems.at[mask - 1],
        device_id=(1 - x if mask & 4 else x, 1 - y if mask & 2 else y, 1 - c if mask & 1 else c), device_id_type=MESH)
        for mask in range(1, N_DEV)]


def _sum_spread(pack, gathered):
    P = pack.shape[0]

    def body(p_ref, g_ref, o_ref):
        x, y, c, _ = _place()
        me = 4 * x + 2 * y + c
        acc = None
        for i in range(N_DEV):
            term = jnp.where(me == i, p_ref[...], g_ref[i])
            acc = term if acc is None else acc + term
        o_ref[...] = acc

    vmem = pl.BlockSpec(memory_space=pltpu.VMEM)
    return pl.pallas_call(body, name="allreduce_sum", out_shape=jax.ShapeDtypeStruct((P, LANES), F32),
                          in_specs=[vmem, vmem], out_specs=vmem)(pack, gathered)


def _pack_rows(arrs):
    rows = []
    for a in arrs:
        f = a.reshape(-1)
        f = jnp.pad(f, (0, (-f.shape[0]) % (8 * LANES)))
        rows.append(f.reshape(-1, LANES))
    return jnp.concatenate(rows, axis=0)


def _unpack_rows(pack, shapes):
    out, r = [], 0
    for s in shapes:
        n = math.prod(s)
        out.append(pack[r:r + -(-n // LANES)].reshape(-1)[:n].reshape(s))
        r += 8 * -(-n // (8 * LANES))
    return out


_SMALL = ["norm_mix_pre", "ml_head_norm", "b_gate_a", "b_gate_b", "norm_mix_post", "norm_ffn_pre", "norm_ffn_post",
          "conv_b", "b_ml_i", "b_ml_f", "b_fox_f"]
_BIG = ["w_in", "w_branch_a", "w_branch_b", "w_out", "w_up", "w_down"]
_WEIGHTS = ['norm_mix_pre', 'w_in', 'b_ml_i', 'b_ml_f', 'ml_head_norm', 'b_fox_f', 'b_gate_a', 'b_gate_b', 'w_branch_a',
            'w_branch_b', 'w_out', 'norm_mix_post', 'norm_ffn_pre', 'w_up', 'conv_w', 'conv_b', 'w_down', 'norm_ffn_post']


_KINDS = ["rows", "rows", "rows", "rows", "cols", "rows"]


def kernel(x, norm_mix_pre, w_in, b_ml_i, b_ml_f, ml_head_norm, b_fox_f, b_gate_a, b_gate_b, w_branch_a, w_branch_b, w_out, norm_mix_post, norm_ffn_pre, w_up, conv_w, conv_b, w_down, norm_ffn_post, loss_target, m_norm_mix_pre, m_w_in, m_b_ml_i, m_b_ml_f, m_ml_head_norm, m_b_fox_f, m_b_gate_a, m_b_gate_b, m_w_branch_a, m_w_branch_b, m_w_out, m_norm_mix_post, m_norm_ffn_pre, m_w_up, m_conv_w, m_conv_b, m_w_down, m_norm_ffn_post, v_norm_mix_pre, v_w_in, v_b_ml_i, v_b_ml_f, v_ml_head_norm, v_b_fox_f, v_b_gate_a, v_b_gate_b, v_w_branch_a, v_w_branch_b, v_w_out, v_norm_mix_post, v_norm_ffn_pre, v_w_up, v_conv_w, v_conv_b, v_w_down, v_norm_ffn_post):
    args = dict(locals())
    w = {n: args[n] for n in _WEIGHTS}
    mom = {n: args["m_" + n] for n in _WEIGHTS}
    var = {n: args["v_" + n] for n in _WEIGHTS}
    cx, cy, cc = lax.axis_index("x"), lax.axis_index("y"), lax.axis_index("c")
    kme = 2 * cx + cy
    cvec = jnp.reshape(cc, (1,)).astype(jnp.int32)
    kcvec = jnp.stack([kme, cc]).astype(jnp.int32)
    odd = kme % 2

    tr3 = lambda t: jnp.transpose(t, (0, 2, 1))
    w["w_in"], mom["w_in"], var["w_in"] = tr3(w_in), tr3(m_w_in), tr3(v_w_in)
    w_in_main = lax.dynamic_slice_in_dim(w["w_in"][0], 4 * odd, 2048, axis=0).astype(BF16)
    w_in_gates = lax.dynamic_slice_in_dim(w["w_in"][0], 2048 * (1 - odd), 4, axis=0).astype(BF16)
    (wmain_t,), (g_cw, g_gates) = _gather_weights([w_in_main], _KINDS[:1], [w["conv_w"][0], w_in_gates])
    rest_started = _gather_start([w[n][0].astype(BF16) for n in _BIG[1:]], _KINDS[1:], "gather_rest_start")

    relay = {}

    def rest_arrived(after):
        bufs = _gather_wait(rest_started, after, _KINDS[1:], "gather_rest_wait")
        relay["started"] = _inplace_start(_relay_copies, 3, bufs, _KINDS[1:], "gather_rest_relay_start")
        return relay["started"][-1]

    def rest_weights(after):
        g_a, g_b, g_out, wup, g_down = _inplace_wait(_relay_copies, relay["started"], after, _KINDS[1:],
                                                     "gather_rest_relay_wait")
        return full(g_a), full(g_b), full(g_out), wup, full(g_down)
    gate_rows = g_gates.reshape(16, D_MODEL)
    wsmall_t = jnp.zeros((N_SMALL, D_MODEL), BF16)
    for blk, (lo, hi) in enumerate(((0, 4), (4, 8), (8, 16))):
        wsmall_t = wsmall_t.at[blk * LANES:blk * LANES + hi - lo].set(gate_rows[lo:hi])
    full = lambda g: g.reshape(-1, g.shape[2])
    p = {n: w[n] for n in _SMALL}
    p["conv_w"] = jnp.transpose(g_cw, (1, 0, 2)).reshape(3, -1)

    groups = {}

    def on_grads(group, gs):
        names = list(gs)
        kinds = [_KINDS[_BIG.index(n)] for n in names]
        whole = [g if k == "cols" else g.reshape(4, -1, g.shape[1]) for g, k in zip(gs.values(), kinds)]
        started = _exchange_start(_sibling_copies, 1, whole, [_half_shape(g, k) for g, k in zip(whole, kinds)], kinds,
                                  "grads_to_sibling_start_" + group)
        groups[group] = dict(names=names, kinds=kinds, sibling=started)
        return started[-1]

    def advance(group, after):
        G = groups[group]
        whole, got = _exchange_wait(_sibling_copies, G["sibling"], after, G["kinds"], "grads_to_sibling_wait_" + group)
        sums = [_add_halves(g, r, cvec, k, "add_sibling_" + n) for g, r, k, n in zip(whole, got, G["kinds"], G["names"])]
        G["chips"] = _exchange_start(_chip_copies, 3, sums, [_land_shape(s, k) for s, k in zip(sums, G["kinds"])],
                                     G["kinds"], "grads_to_chips_start_" + group)
        return G["chips"][-1]

    loss_row, grad_x, big, small = _local_step(x[0], loss_target[0], full(wmain_t), wsmall_t, rest_arrived, rest_weights,
                                               p, on_grads, advance, rest_started[-1])
    gt = big["wsmall_t"]
    small["w_in_gates"] = jnp.concatenate([gt[0:4], gt[LANES:LANES + 4], gt[2 * LANES:2 * LANES + 8]], axis=0)
    small_names = _SMALL + ["conv_w"]
    packed_names = small_names + ["w_in_gates"]
    pack = _pack_rows([small[n] for n in packed_names] + [loss_row])
    spread = _exchange_start(_spread_copies, N_DEV - 1, [pack], [(N_DEV,) + pack.shape], None, "allreduce_start", zeroed=True)

    def my_half(group, after):
        G = groups[group]
        sums, got = _exchange_wait(_chip_copies, G["chips"], after, G["kinds"], "grads_to_chips_wait_" + group)
        return [_add_chips(s, r, kcvec, k, "add_chips_" + n) for s, r, k, n in zip(sums, got, G["kinds"], G["names"])]

    first_names = groups["ffn"]["names"] + groups["mix"]["names"]
    join_first = _inplace_start(_join_copies, 1, my_half("ffn", spread[-1]) + my_half("mix", spread[-1]), None,
                                "grads_join_start")
    join_in = _inplace_start(_join_copies, 1, my_half("in", join_first[-1]), None, "grads_join_start_in")
    grads = dict(zip(first_names, _inplace_wait(_join_copies, join_first, join_in[-1], None, "grads_join_wait")))

    delta, new_m, new_v = {}, {}, {}
    for n in _BIG[1:]:
        delta[n], new_m[n], new_v[n] = _adamw(w[n], grads[n], mom[n], var[n], "adamw_" + n)
        grads[n] = grads[n][None]
    grads["w_in"], = _inplace_wait(_join_copies, join_in, delta[_BIG[-1]], None, "grads_join_wait_in")

    (pack,), (gathered,) = _exchange_wait(_spread_copies, spread, delta[_BIG[-1]], None, "allreduce_wait")
    full_shapes = [small[n].shape if n in ("conv_w", "w_in_gates") else w[n][0].shape for n in packed_names]
    total = _unpack_rows(_sum_spread(pack, gathered), full_shapes + [loss_row.shape])
    for n, t in zip(packed_names, total):
        grads[n] = t
    loss = total[-1][0, 0]
    grads["conv_w"] = lax.dynamic_slice_in_dim(grads["conv_w"], kme * conv_w.shape[2], conv_w.shape[2], axis=1)
    my_gates = lax.dynamic_slice_in_dim(grads.pop("w_in_gates"), 4 * kme, 4, axis=0)
    g_in = jnp.zeros(w["w_in"].shape[1:], F32)
    g_in = lax.dynamic_update_slice_in_dim(g_in, grads["w_in"], 4 * odd, axis=0)
    grads["w_in"] = lax.dynamic_update_slice_in_dim(g_in, my_gates, 2048 * (1 - odd), axis=0)
    delta["w_in"], new_m["w_in"], new_v["w_in"] = _adamw(w["w_in"], grads["w_in"], mom["w_in"], var["w_in"], "adamw_w_in")
    grads["w_in"] = grads["w_in"][None]
    for d in (grads, delta, new_m, new_v):
        d["w_in"] = tr3(d["w_in"])
    packs = [_pack_rows([d[n][0] for n in small_names]) for d in (w, mom, var)]
    pad = ((0, (-packs[0].shape[0]) % 8), (0, 0))
    packs = [jnp.pad(t, pad)[None] for t in packs]
    gp = jnp.pad(_pack_rows([grads[n] for n in small_names]), pad)
    shapes = [w[n][0].shape for n in small_names]
    for dst, res in zip((delta, new_m, new_v), _adamw(packs[0], gp, packs[1], packs[2], "adamw_small")):
        for n, t in zip(small_names, _unpack_rows(res[0], shapes)):
            dst[n] = t[None]
    for n in small_names:
        grads[n] = grads[n][None]

    return (loss, grad_x[None], *[grads[n] for n in _WEIGHTS], *[delta[n] for n in _WEIGHTS],
            *[new_m[n] for n in _WEIGHTS], *[new_v[n] for n in _WEIGHTS])
```

```python
import functools
import math

import jax
import jax.numpy as jnp
from jax import lax
from jax.experimental import pallas as pl
from jax.experimental.pallas import tpu as pltpu

F32 = jnp.float32
BF16 = jnp.bfloat16
MESH = pl.DeviceIdType.MESH

D_MODEL = 1024
ML_HEADS = 4
ML_DQK = 128
ML_DV = 256
FOX_HEADS = 8
FOX_DH = 128
D_FF = 2816
GATE_CAP = 15.0
EPS = 1e-6
ADAM_LR, ADAM_B1, ADAM_B2, ADAM_EPS, ADAM_WD, ADAM_STEP = 0.001, 0.9, 0.999, 1e-08, 0.01, 10

LANES = 128
MLC = 256
FOX_TQ = 1024
FOX_TQ_FWD = 1024
FOX_TK = 1024
FOX_TK_FWD = 1024
ROW_T = 512
CONV_TC = 1408
VMEM_LIMIT = 56 * 1024 * 1024

C_QM, C_KM, C_VM, C_OM = 0, 512, 1024, 2048
N_ML, N_FOX, N_GATE = 3072, 3072, 2048
N_SMALL = 384


def _cparams(sem=None):
    return pltpu.CompilerParams(dimension_semantics=sem, vmem_limit_bytes=VMEM_LIMIT)


def _tile(n, target):
    if n <= target:
        return n
    best = None
    for t in range(LANES, target + 1, LANES):
        if n % t == 0:
            best = t
    assert best is not None, (n, target)
    return best


def _dot(a, b, dims):
    return lax.dot_general(a, b, (dims, ((), ())), preferred_element_type=F32)


def _dot_nn(a, b):
    return _dot(a, b, ((1,), (0,)))


def _dot_nt(a, b):
    return _dot(a, b, ((1,), (1,)))


def _dot_tn(a, b):
    return _dot(a, b, ((0,), (0,)))


_DOTS = {"nn": _dot_nn, "nt": _dot_nt, "tn": _dot_tn}


def _mm(a, b, mode, out_dtype, name, tm=1024, tn=1408, tk=1408, after=None, b_rows=None):
    a_parts = list(a) if isinstance(a, (list, tuple)) else [a]
    b_parts = list(b) if isinstance(b, (list, tuple)) else [b]
    extra = [] if after is None else [after]
    assert len(a_parts) == 1 or len(b_parts) == 1, name
    a_axes = {"nn": "ik", "nt": "ik", "tn": "ki"}[mode]
    b_axes = {"nn": "kj", "nt": "jk", "tn": "kj"}[mode]
    size, target = {}, dict(i=tm, j=tn, k=tk)
    for parts, axes in ((a_parts, a_axes), (b_parts, b_axes)):
        dims = (parts[0].shape[0], parts[0].shape[1] * len(parts))
        if parts is b_parts and b_rows is not None:
            dims = (b_rows[1], dims[1])
        for ax, n in zip(axes, dims):
            assert size.setdefault(ax, n) == n, (name, ax, n, size)
    tile = {}
    for parts, axes in ((a_parts, a_axes), (b_parts, b_axes)):
        if len(parts) > 1:
            tile[axes[1]] = _tile(parts[0].shape[1], target[axes[1]])
    for ax in "ijk":
        tile.setdefault(ax, _tile(size[ax], target[ax]))
    M, N, nk = size["i"], size["j"], size["k"] // tile["k"]
    grid_pos = dict(i=0, j=1, k=2)
    dot = _DOTS[mode]

    def specs(parts, axes):
        blk = (tile[axes[0]], tile[axes[1]])
        if len(parts) == 1:
            first = 0
            if parts is b_parts and b_rows is not None:
                assert b_rows[0] % blk[0] == 0, (name, b_rows, blk)
                first = b_rows[0] // blk[0]
            return [pl.BlockSpec(blk, lambda *g: (first + g[grid_pos[axes[0]]], g[grid_pos[axes[1]]]))], None
        bpp = parts[0].shape[1] // blk[1]

        def index(p):
            def f(*g):
                g0, g1 = g[grid_pos[axes[0]]], g[grid_pos[axes[1]]]
                on = g1 // bpp == p
                return jnp.where(on, g0, 0), jnp.where(on, g1 % bpp, 0)
            return f

        return [pl.BlockSpec(blk, index(p)) for p in range(len(parts))], (axes[1], bpp)

    a_specs, a_sel = specs(a_parts, a_axes)
    b_specs, b_sel = specs(b_parts, b_axes)
    na, nb = len(a_parts), len(b_parts)

    def body(*refs):
        a_refs, b_refs = refs[:na], refs[na:na + nb]
        o_ref, acc = refs[na + nb + len(extra)], refs[na + nb + len(extra) + 1:]

        def accumulate(part):
            if nk == 1:
                o_ref[...] = part.astype(o_ref.dtype)
                return
            acc_ref, = acc
            k = pl.program_id(2)

            @pl.when(k == 0)
            def _():
                acc_ref[...] = part

            @pl.when(k > 0)
            def _():
                acc_ref[...] += part

            @pl.when(k == nk - 1)
            def _():
                o_ref[...] = acc_ref[...].astype(o_ref.dtype)

        sel = a_sel or b_sel
        if sel is None:
            accumulate(dot(a_refs[0][...], b_refs[0][...]))
        else:
            which = pl.program_id(grid_pos[sel[0]]) // sel[1]
            for p in range(max(na, nb)):
                @pl.when(which == p)
                def _(p=p):
                    accumulate(dot(a_refs[p if a_sel else 0][...], b_refs[p if b_sel else 0][...]))

    return pl.pallas_call(
        body, name=name,
        out_shape=jax.ShapeDtypeStruct((M, N), out_dtype),
        grid=(M // tile["i"], N // tile["j"], nk),
        in_specs=a_specs + b_specs + [pl.BlockSpec(memory_space=pl.ANY)] * len(extra),
        out_specs=pl.BlockSpec((tile["i"], tile["j"]), lambda i, j, k: (i, j)),
        scratch_shapes=[pltpu.VMEM((tile["i"], tile["j"]), F32)] if nk > 1 else [],
        compiler_params=_cparams(("parallel", "parallel", "arbitrary")),
    )(*a_parts, *b_parts, *extra)


def _mm_sum_parts(parts, b, out_dtype, name, trans_b=False, tm=1024, tn=512, after=None):
    M, K = parts[0].shape
    N = b.shape[0] if trans_b else b.shape[1]
    tm, tn = _tile(M, tm), _tile(N, tn)
    n = len(parts)
    extra = [] if after is None else [after]

    def body(*refs):
        b_ref, o_ref = refs[n], refs[n + 1 + len(extra)]
        acc = None
        for p in range(n):
            if trans_b:
                d = _dot_nt(refs[p][...], b_ref[:, p * K:(p + 1) * K])
            else:
                d = _dot_nn(refs[p][...], b_ref[p * K:(p + 1) * K, :])
            acc = d if acc is None else acc + d
        o_ref[...] = acc.astype(o_ref.dtype)

    b_spec = pl.BlockSpec((tn, n * K), lambda i, j: (j, 0)) if trans_b else pl.BlockSpec((n * K, tn), lambda i, j: (0, j))
    return pl.pallas_call(
        body, name=name, out_shape=jax.ShapeDtypeStruct((M, N), out_dtype), grid=(M // tm, N // tn),
        in_specs=[pl.BlockSpec((tm, K), lambda i, j: (i, 0))] * n + [b_spec]
        + [pl.BlockSpec(memory_space=pl.ANY)] * len(extra),
        out_specs=pl.BlockSpec((tm, tn), lambda i, j: (i, j)),
        compiler_params=_cparams(("parallel", "arbitrary")),
    )(*parts, b, *extra)


def _rstd(x):
    return lax.rsqrt(jnp.mean(x * x, axis=-1, keepdims=True) + EPS)


def _rmsnorm_fwd(x, g, name):
    S, D = x.shape
    T = _tile(S, ROW_T)

    def body(x_ref, g_ref, o_ref):
        xv = x_ref[...]
        o_ref[...] = (xv * _rstd(xv) * g_ref[...]).astype(o_ref.dtype)

    return pl.pallas_call(
        body, name=name, out_shape=jax.ShapeDtypeStruct((S, D), BF16), grid=(S // T,),
        in_specs=[pl.BlockSpec((T, D), lambda i: (i, 0)), pl.BlockSpec((1, D), lambda i: (0, 0))],
        out_specs=pl.BlockSpec((T, D), lambda i: (i, 0)),
        compiler_params=_cparams(("parallel",)),
    )(x, g)


def _resid_norm_fwd(x, z, g, g_next, name):
    S, D = x.shape
    T = _tile(S, ROW_T)

    def body(x_ref, z_ref, g_ref, gn_ref, o_ref, h_ref):
        zv = z_ref[...]
        x1 = x_ref[...] + zv * _rstd(zv) * g_ref[...]
        o_ref[...] = x1
        h_ref[...] = (x1 * _rstd(x1) * gn_ref[...]).astype(h_ref.dtype)

    row = pl.BlockSpec((T, D), lambda i: (i, 0))
    vec = pl.BlockSpec((1, D), lambda i: (0, 0))
    return pl.pallas_call(
        body, name=name, out_shape=(jax.ShapeDtypeStruct((S, D), F32), jax.ShapeDtypeStruct((S, D), BF16)),
        grid=(S // T,), in_specs=[row, row, vec, vec], out_specs=(row, row), compiler_params=_cparams(("parallel",)),
    )(x, z, g, g_next)


def _norm_chain_bwd(dh, xin, g, resid, zin, gz, name):
    S, D = xin.shape
    T = _tile(S, ROW_T)

    def body(dh_ref, x_ref, g_ref, r_ref, z_ref, gz_ref, dx_ref, dz_ref, dg_ref, dgz_ref):
        dx, dgt = _rmsnorm_bwd_math(dh_ref[...], x_ref[...], g_ref[...])
        dx = dx + r_ref[...]
        dx_ref[...] = dx
        dz, dgzt = _rmsnorm_bwd_math(dx, z_ref[...], gz_ref[...])
        dz_ref[...] = dz.astype(dz_ref.dtype)

        @pl.when(pl.program_id(0) == 0)
        def _():
            dg_ref[...] = jnp.zeros_like(dg_ref)
            dgz_ref[...] = jnp.zeros_like(dgz_ref)

        dg_ref[...] += jnp.sum(dgt, axis=0, keepdims=True)
        dgz_ref[...] += jnp.sum(dgzt, axis=0, keepdims=True)

    row = pl.BlockSpec((T, D), lambda i: (i, 0))
    vec = pl.BlockSpec((1, D), lambda i: (0, 0))
    v1 = jax.ShapeDtypeStruct((1, D), F32)
    return pl.pallas_call(
        body, name=name,
        out_shape=(jax.ShapeDtypeStruct((S, D), F32), jax.ShapeDtypeStruct((S, D), BF16), v1, v1),
        grid=(S // T,), in_specs=[row, row, vec, row, row, vec], out_specs=(row, row, vec, vec),
        compiler_params=_cparams(("arbitrary",)),
    )(dh, xin, g, resid, zin, gz)


def _rmsnorm_bwd_math(dy, xv, g):
    r = _rstd(xv)
    u = dy * g
    dx = r * u - xv * (r * r * r) * jnp.mean(u * xv, axis=-1, keepdims=True)
    return dx, dy * xv * r


def _rmsnorm_bwd(dys, xin, g, resid, out_dtype, name):
    S, D = xin.shape
    T = _tile(S, ROW_T)
    has_resid = resid is not None
    ndy = len(dys)

    def body(*refs):
        dy_refs, (x_ref, g_ref) = refs[:ndy], refs[ndy:ndy + 2]
        dx_ref, dg_ref = refs[-2:]
        dy = dy_refs[0][...]
        for r in dy_refs[1:]:
            dy = dy + r[...]
        dx, dgt = _rmsnorm_bwd_math(dy, x_ref[...], g_ref[...])
        if has_resid:
            dx = dx + refs[ndy + 2][...]
        dx_ref[...] = dx.astype(dx_ref.dtype)

        @pl.when(pl.program_id(0) == 0)
        def _():
            dg_ref[...] = jnp.zeros_like(dg_ref)

        dg_ref[...] += jnp.sum(dgt, axis=0, keepdims=True)

    row = pl.BlockSpec((T, D), lambda i: (i, 0))
    vec = pl.BlockSpec((1, D), lambda i: (0, 0))
    ins = list(dys) + [xin, g] + ([resid] if has_resid else [])
    return pl.pallas_call(
        body, name=name,
        out_shape=(jax.ShapeDtypeStruct((S, D), out_dtype), jax.ShapeDtypeStruct((1, D), F32)),
        grid=(S // T,), in_specs=[row] * ndy + [row, vec] + ([row] if has_resid else []),
        out_specs=(row, vec), compiler_params=_cparams(("arbitrary",)),
    )(*ins)


def _loss_head(x1, d, g, target, name):
    S, D = x1.shape
    T = _tile(S, ROW_T)

    def body(x_ref, d_ref, g_ref, t_ref, loss_ref, dy_ref, dd_ref, dg_ref):
        dv, gv = d_ref[...], g_ref[...]
        y = x_ref[...] + dv * _rstd(dv) * gv
        diff = y - t_ref[...]
        dy = diff * (1.0 / D)
        dy_ref[...] = dy
        dd, dgt = _rmsnorm_bwd_math(dy, dv, gv)
        dd_ref[...] = dd.astype(dd_ref.dtype)

        @pl.when(pl.program_id(0) == 0)
        def _():
            dg_ref[...] = jnp.zeros_like(dg_ref)
            loss_ref[...] = jnp.zeros_like(loss_ref)

        dg_ref[...] += jnp.sum(dgt, axis=0, keepdims=True)
        part = jnp.sum(jnp.sum(diff * diff, axis=1, keepdims=True), axis=0, keepdims=True)
        loss_ref[...] += (0.5 / D) * part

    row = pl.BlockSpec((T, D), lambda i: (i, 0))
    vec = pl.BlockSpec((1, D), lambda i: (0, 0))
    return pl.pallas_call(
        body, name=name,
        out_shape=(jax.ShapeDtypeStruct((1, LANES), F32), jax.ShapeDtypeStruct((S, D), F32),
                   jax.ShapeDtypeStruct((S, D), BF16), jax.ShapeDtypeStruct((1, D), F32)),
        grid=(S // T,), in_specs=[row, row, vec, row],
        out_specs=(pl.BlockSpec((1, LANES), lambda i: (0, 0)), row, row, vec),
        compiler_params=_cparams(("arbitrary",)),
    )(x1, d, g, target)


def _merge_fwd(ya, yb, pm, ba, bb, name):
    S, D = ya.shape
    T = _tile(S, ROW_T)

    def body(ya_ref, yb_ref, ga_ref, gb_ref, ba_ref, bb_ref, o_ref):
        sa = jax.nn.sigmoid(ga_ref[...] + ba_ref[...])
        sb = jax.nn.sigmoid(gb_ref[...] + bb_ref[...])
        o_ref[...] = (sa * ya_ref[...] + sb * yb_ref[...]).astype(o_ref.dtype)

    row = pl.BlockSpec((T, D), lambda i: (i, 0))
    vec = pl.BlockSpec((1, D), lambda i: (0, 0))
    return pl.pallas_call(
        body, name=name, out_shape=jax.ShapeDtypeStruct((S, D), BF16), grid=(S // T,),
        in_specs=[row, row, pl.BlockSpec((T, D), lambda i: (i, 0)),
                  pl.BlockSpec((T, D), lambda i: (i, 1)), vec, vec],
        out_specs=row, compiler_params=_cparams(("parallel",)),
    )(ya, yb, pm, pm, ba, bb)


def _merge_bwd(dmerged, ya, yb, pm, ba, bb, name):
    S, D = ya.shape
    T = _tile(S, ROW_T)

    def body(dm_ref, ya_ref, yb_ref, ga_ref, gb_ref, ba_ref, bb_ref,
             dya_ref, dyb_ref, dga_ref, dgb_ref, dba_ref, dbb_ref):
        dm = dm_ref[...]
        sa = jax.nn.sigmoid(ga_ref[...] + ba_ref[...])
        sb = jax.nn.sigmoid(gb_ref[...] + bb_ref[...])
        dya_ref[...] = (dm * sa).astype(dya_ref.dtype)
        dyb_ref[...] = (dm * sb).astype(dyb_ref.dtype)
        dga = dm * ya_ref[...] * sa * (1.0 - sa)
        dgb = dm * yb_ref[...] * sb * (1.0 - sb)
        dga_ref[...] = dga.astype(dga_ref.dtype)
        dgb_ref[...] = dgb.astype(dgb_ref.dtype)

        @pl.when(pl.program_id(0) == 0)
        def _():
            dba_ref[...] = jnp.zeros_like(dba_ref)
            dbb_ref[...] = jnp.zeros_like(dbb_ref)

        dba_ref[...] += jnp.sum(dga, axis=0, keepdims=True)
        dbb_ref[...] += jnp.sum(dgb, axis=0, keepdims=True)

    row = pl.BlockSpec((T, D), lambda i: (i, 0))
    vec = pl.BlockSpec((1, D), lambda i: (0, 0))
    act = jax.ShapeDtypeStruct((S, D), BF16)
    v1 = jax.ShapeDtypeStruct((1, D), F32)
    return pl.pallas_call(
        body, name=name, out_shape=(act, act, act, act, v1, v1), grid=(S // T,),
        in_specs=[row, row, row, pl.BlockSpec((T, D), lambda i: (i, 0)),
                  pl.BlockSpec((T, D), lambda i: (i, 1)), vec, vec],
        out_specs=(row, row, row, row, vec, vec), compiler_params=_cparams(("arbitrary",)),
    )(dmerged, ya, yb, pm, pm, ba, bb)


_GELU_C = math.sqrt(2.0 / math.pi)


_GELU_K = 0.044715


def _gelu(g):
    u = 0.5 * jnp.tanh(g * (_GELU_C + (_GELU_C * _GELU_K) * (g * g))) + 0.5
    return g * u, u


def _gelu_grad(g, u):
    return u * (1.0 + g * (1.0 - u) * (2 * _GELU_C + (6 * _GELU_C * _GELU_K) * (g * g)))


def _shift_down(v, halo_ref, first, rows):
    T = v.shape[0]
    keep = jnp.where(first, 0.0, 1.0)
    h7 = halo_ref[7:8, :] * keep
    h6 = halo_ref[6:7, :] * keep
    m1 = jnp.where(rows == 0, h7, pltpu.roll(v, 1, 0))
    m2 = jnp.where(rows == 0, h6, jnp.where(rows == 1, h7, pltpu.roll(v, 2, 0)))
    return m1, m2


def _conv_act_fwd(up, cw, cb, name):
    S, F2 = up.shape
    Fh = F2 // 2
    T = _tile(S, ROW_T)
    tc = _tile(Fh, CONV_TC)
    ncol = Fh // tc
    hb = T // 8

    def body(ua_ref, ug_ref, ha_ref, hg_ref, wa_ref, wg_ref, ba_ref, bg_ref, o_ref, a_ref, g_ref):
        first = pl.program_id(0) == 0
        rows = lax.broadcasted_iota(jnp.int32, (T, tc), 0)

        def conv(u_ref, h_ref, w_ref, b_ref):
            v = u_ref[...]
            m1, m2 = _shift_down(v, h_ref, first, rows)
            return b_ref[...] + w_ref[0:1, :] * m2 + w_ref[1:2, :] * m1 + w_ref[2:3, :] * v

        a = conv(ua_ref, ha_ref, wa_ref, ba_ref)
        g = conv(ug_ref, hg_ref, wg_ref, bg_ref)
        a_ref[...] = a
        g_ref[...] = g
        o_ref[...] = (_gelu(g)[0] * a).astype(o_ref.dtype)

    halo = lambda off: pl.BlockSpec((8, tc), lambda i, j: (jnp.maximum(i * hb - 1, 0), j + off))
    blk = pl.BlockSpec((T, tc), lambda i, j: (i, j))
    f32 = jax.ShapeDtypeStruct((S, Fh), F32)
    return pl.pallas_call(
        body, name=name, out_shape=(jax.ShapeDtypeStruct((S, Fh), BF16), f32, f32), grid=(S // T, ncol),
        in_specs=[blk, pl.BlockSpec((T, tc), lambda i, j: (i, j + ncol)),
                  halo(0), halo(ncol),
                  pl.BlockSpec((3, tc), lambda i, j: (0, j)), pl.BlockSpec((3, tc), lambda i, j: (0, j + ncol)),
                  pl.BlockSpec((1, tc), lambda i, j: (0, j)), pl.BlockSpec((1, tc), lambda i, j: (0, j + ncol))],
        out_specs=(blk, blk, blk),
        compiler_params=_cparams(("parallel", "parallel")),
    )(up, up, up, up, cw, cw, cb, cb)


def _conv_act_bwd(up, a, g, dact, cw, name):
    S, F2 = up.shape
    Fh = F2 // 2
    T = _tile(S, ROW_T)
    tc = _tile(Fh, CONV_TC)
    ncol, nrow, hb, nhb = Fh // tc, S // T, T // 8, S // 8

    def body(ua_ref, ug_ref, a_ref, g_ref, an_ref, gn_ref, wa_ref, wg_ref, da_ref, dn_ref,
             dpa_ref, dpg_ref, dwa_ref, dwg_ref, dba_ref, dbg_ref, dua_n, dug_n):
        i = pl.program_id(1)
        rows = lax.broadcasted_iota(jnp.int32, (T, tc), 0)

        def du_of(a, g, dact_v):
            gel, t = _gelu(g)
            return dact_v * gel, dact_v * a * _gelu_grad(g, t)

        dua, dug = du_of(a_ref[...], g_ref[...], da_ref[...])
        keep = jnp.where(i == nrow - 1, 0.0, 1.0)
        dua_n[...], dug_n[...] = du_of(an_ref[...], gn_ref[...], dn_ref[...] * keep)

        @pl.when(i == 0)
        def _():
            for r in (dwa_ref, dwg_ref, dba_ref, dbg_ref):
                r[...] = jnp.zeros_like(r)

        for du, n_ref, u_ref, w_ref, o_ref, dw_ref, db_ref in ((dua, dua_n, ua_ref, wa_ref, dpa_ref, dwa_ref, dba_ref),
                                                               (dug, dug_n, ug_ref, wg_ref, dpg_ref, dwg_ref, dbg_ref)):
            n0, n1 = n_ref[0:1, :], n_ref[1:2, :]
            du1 = jnp.where(rows == T - 1, n0, pltpu.roll(du, T - 1, 0))
            du2 = jnp.where(rows == T - 2, n0, jnp.where(rows == T - 1, n1, pltpu.roll(du, T - 2, 0)))
            o_ref[...] = (w_ref[2:3, :] * du + w_ref[1:2, :] * du1 + w_ref[0:1, :] * du2).astype(o_ref.dtype)
            u = u_ref[...]
            db_ref[...] += jnp.sum(du, axis=0, keepdims=True)
            for j, d in enumerate((du2, du1, du)):
                dw_ref[j:j + 1, :] += jnp.sum(d * u, axis=0, keepdims=True)

    tile = lambda off: pl.BlockSpec((T, tc), lambda j, i: (i, j + off))
    under = pl.BlockSpec((8, tc), lambda j, i: (jnp.minimum((i + 1) * hb, nhb - 1), j))
    vec = lambda n, off: pl.BlockSpec((n, tc), lambda j, i: (0, j + off))
    act = jax.ShapeDtypeStruct((S, Fh), BF16)
    return pl.pallas_call(
        body, name=name,
        out_shape=(act, act, jax.ShapeDtypeStruct((3, Fh), F32), jax.ShapeDtypeStruct((3, Fh), F32),
                   jax.ShapeDtypeStruct((1, Fh), F32), jax.ShapeDtypeStruct((1, Fh), F32)),
        grid=(ncol, nrow),
        in_specs=[tile(0), tile(ncol), tile(0), tile(0), under, under, vec(3, 0), vec(3, ncol), tile(0), under],
        out_specs=(tile(0), tile(0), vec(3, 0), vec(3, 0), vec(1, 0), vec(1, 0)),
        scratch_shapes=[pltpu.VMEM((8, tc), F32), pltpu.VMEM((8, tc), F32)],
        compiler_params=_cparams(("parallel", "arbitrary")),
    )(up, up, a, g, a, g, cw, cw, dact, dact)


def _split3(x):
    hi = x.astype(BF16)
    r1 = x - hi.astype(F32)
    mid = r1.astype(BF16)
    lo = (r1 - mid.astype(F32)).astype(BF16)
    return hi, mid, lo


def _tri_dot(tri, x):
    hi, mid, lo = _split3(x)
    return _dot_nn(tri, hi) + _dot_nn(tri, mid) + _dot_nn(tri, lo)


def _log_sigmoid(x):
    return jnp.minimum(x, 0.0) - jnp.log(1.0 + jnp.exp(-jnp.abs(x)))


def _tri_mask(n, lower):
    r = lax.broadcasted_iota(jnp.int32, (n, n), 0)
    c = lax.broadcasted_iota(jnp.int32, (n, n), 1)
    return (r >= c) if lower else (r <= c)


def _gates_fwd(ps, bi, bf, bff, name):
    S = ps.shape[0]
    NC = S // MLC

    def body(ps_ref, bi_ref, bf_ref, bff_ref, a_ref, A_ref, wi_ref, em_ref, wk_ref, dec_ref, F_ref, m_scr, f_scr):
        @pl.when(pl.program_id(0) == 0)
        def _():
            m_scr[...] = jnp.zeros_like(m_scr)
            f_scr[...] = jnp.zeros_like(f_scr)

        rows = lax.broadcasted_iota(jnp.int32, (MLC, LANES), 0)
        ltri = _tri_mask(MLC, True).astype(BF16)
        li = GATE_CAP * jnp.tanh((ps_ref[:, 0:LANES] + bi_ref[...]) / GATE_CAP)
        lf = _log_sigmoid(GATE_CAP * jnp.tanh((ps_ref[:, LANES:2 * LANES] + bf_ref[...]) / GATE_CAP))
        b = _tri_dot(ltri, lf)
        a = li - b
        cm = a
        sh = 1
        while sh < MLC:
            cm = jnp.where(rows >= sh, jnp.maximum(cm, pltpu.roll(cm, sh, 0)), cm)
            sh *= 2
        m0 = m_scr[...]
        A = jnp.maximum(cm, m0)
        a_ref[...] = a
        A_ref[...] = A
        A_last = A_ref[MLC - 1:MLC, :]
        wi_ref[...] = jnp.exp(m0 - A)
        em_ref[...] = jnp.exp(-(b + A))
        wk_ref[...] = jnp.exp(a - A_last)
        dec_ref[0] = jnp.exp(m0 - A_last)
        F_ref[...] = b
        m_scr[...] = F_ref[MLC - 1:MLC, :] + A_last
        lfg = _log_sigmoid(ps_ref[:, 2 * LANES:3 * LANES] + bff_ref[...])
        F_ref[...] = _tri_dot(ltri, lfg) + f_scr[...]
        f_scr[...] = F_ref[MLC - 1:MLC, :]

    col = pl.BlockSpec((MLC, LANES), lambda c: (c, 0))
    vec = pl.BlockSpec((1, LANES), lambda c: (0, 0))
    cs = jax.ShapeDtypeStruct((S, LANES), F32)
    return pl.pallas_call(
        body, name=name,
        out_shape=(cs, cs, cs, cs, cs, jax.ShapeDtypeStruct((NC, 1, LANES), F32), cs),
        grid=(NC,), in_specs=[pl.BlockSpec((MLC, N_SMALL), lambda c: (c, 0)), vec, vec, vec],
        out_specs=(col, col, col, col, col, pl.BlockSpec((1, 1, LANES), lambda c: (c, 0, 0)), col),
        scratch_shapes=[pltpu.VMEM((1, LANES), F32), pltpu.VMEM((1, LANES), F32)],
        compiler_params=_cparams(("arbitrary",)),
    )(ps, bi, bf, bff)


def _gates_bwd(ps, bi, bf, bff, rk, kc, tch, dF, name):
    S = ps.shape[0]
    NC = S // MLC

    def body(ps_ref, bi_ref, bf_ref, bff_ref, rk_ref, kc_ref, t_ref, dF_ref, dps_ref, db_ref, carry):
        @pl.when(pl.program_id(0) == 0)
        def _():
            carry[...] = jnp.zeros_like(carry)
            db_ref[...] = jnp.zeros_like(db_ref)

        lanes = lax.broadcasted_iota(jnp.int32, (MLC, LANES), 1)
        utri = _tri_mask(MLC, False).astype(BF16)
        ti = jnp.tanh((ps_ref[:, 0:LANES] + bi_ref[...]) / GATE_CAP)
        t_end, t_start = t_ref[0, 0:1, :], t_ref[0, 1:2, :]
        rk = rk_ref[...]
        rk = rk - (jnp.sum(rk, axis=0, keepdims=True) - (t_start - t_end)) * (1.0 / MLC)
        dpi = jnp.where(lanes < ML_HEADS, (kc_ref[...] - rk) * (1.0 - ti * ti), 0.0)
        tf = jnp.tanh((ps_ref[:, LANES:2 * LANES] + bf_ref[...]) / GATE_CAP)
        dlf = _tri_dot(utri, rk) + t_end
        dpf = jnp.where(lanes < ML_HEADS, dlf * jax.nn.sigmoid(-GATE_CAP * tf) * (1.0 - tf * tf), 0.0)
        dFv = dF_ref[...]
        dlfg = _tri_dot(utri, dFv) + carry[...]
        carry[...] += jnp.sum(dFv, axis=0, keepdims=True)
        dpff = jnp.where(lanes < FOX_HEADS, dlfg * jax.nn.sigmoid(-(ps_ref[:, 2 * LANES:3 * LANES] + bff_ref[...])), 0.0)
        for n, dp in enumerate((dpi, dpf, dpff)):
            dps_ref[:, n * LANES:(n + 1) * LANES] = dp.astype(dps_ref.dtype)
            db_ref[:, n * LANES:(n + 1) * LANES] += jnp.sum(dp, axis=0, keepdims=True)

    rev = lambda c: (NC - 1 - c, 0)
    col = pl.BlockSpec((MLC, LANES), rev)
    vec = pl.BlockSpec((1, LANES), lambda c: (0, 0))
    wide = pl.BlockSpec((MLC, N_SMALL), rev)
    return pl.pallas_call(
        body, name=name,
        out_shape=(jax.ShapeDtypeStruct((S, N_SMALL), BF16), jax.ShapeDtypeStruct((1, N_SMALL), F32)),
        grid=(NC,),
        in_specs=[wide, vec, vec, vec, col, col, pl.BlockSpec((1, 2, LANES), lambda c: (NC - 1 - c, 0, 0)), col],
        out_specs=(wide, pl.BlockSpec((1, N_SMALL), lambda c: (0, 0))),
        scratch_shapes=[pltpu.VMEM((1, LANES), F32)],
        compiler_params=_cparams(("arbitrary",)),
    )(ps, bi, bf, bff, rk, kc, tch, dF)


_ML_SCALE = ML_DQK ** -0.5


def _ml_specs(rev, NC):
    idx = (lambda c: NC - 1 - c) if rev else (lambda c: c)
    qk = lambda blk: pl.BlockSpec((MLC, ML_HEADS * ML_DQK), lambda c: (idx(c), blk))
    wide = lambda blk: pl.BlockSpec((MLC, D_MODEL), lambda c: (idx(c), blk))
    col = pl.BlockSpec((MLC, LANES), lambda c: (idx(c), 0))
    return idx, qk, wide, col


def _ml_intra(q_ref, k_ref, arow_ref, A_ref, h):
    hs = slice(h * ML_DQK, (h + 1) * ML_DQK)
    qf = q_ref[:, hs] * _ML_SCALE
    kf = k_ref[:, hs]
    qb, kb = qf.astype(BF16), kf.astype(BF16)
    qk = _dot_nt(qb, kb)
    logw = arow_ref[h:h + 1, :] - A_ref[:, h:h + 1]
    W = jnp.exp(jnp.where(_tri_mask(MLC, True), logw, -1e30))
    return qb, kb, qf, kf, qk, W


def _mlstm_fwd(pm, a_row, A, wi, em, wk, dec, w_hn, name):
    S = pm.shape[0]
    NC = S // MLC
    _, qk, wide, col = _ml_specs(False, NC)

    def body(q_ref, k_ref, v_ref, o_ref, arow_ref, A_ref, wi_ref, em_ref, wk_ref, dec_ref, whn_ref,
             ha_ref, hp_ref, den_ref, cst_ref, nst_ref, C_scr, n_scr):
        @pl.when(pl.program_id(0) == 0)
        def _():
            C_scr[...] = jnp.zeros_like(C_scr)
            n_scr[...] = jnp.zeros_like(n_scr)

        lanes = lax.broadcasted_iota(jnp.int32, (MLC, LANES), 1)
        den_tile = jnp.zeros((MLC, LANES), F32)
        for h in range(ML_HEADS):
            vs = slice(h * ML_DV, (h + 1) * ML_DV)
            qb, kb, qf, kf, qk_, W = _ml_intra(q_ref, k_ref, arow_ref, A_ref, h)
            vb = v_ref[:, vs].astype(BF16)
            Cf = C_scr[h]
            Cb = Cf.astype(BF16)
            nrow = n_scr[h]
            cst_ref[0, h] = Cb
            nst_ref[0, h] = nrow
            s = qk_ * W
            wic = wi_ref[:, h:h + 1]
            num = _dot_nn(s.astype(BF16), vb) + wic * _dot_nt(qb, Cb)
            den = jnp.sum(s, axis=1, keepdims=True) + wic * jnp.sum(qf * nrow, axis=1, keepdims=True)
            hp = num / jnp.maximum(jnp.abs(den), em_ref[:, h:h + 1])
            hp_ref[:, vs] = hp
            den_tile = jnp.where(lanes == h, den, den_tile)
            hn = hp * _rstd(hp) * whn_ref[:, vs]
            ha_ref[:, vs] = (hn * jax.nn.sigmoid(o_ref[:, vs])).astype(ha_ref.dtype)
            wkc = wk_ref[:, h:h + 1]
            kw = kf * wkc
            d = dec_ref[0, :, h:h + 1]
            C_scr[h] = d * Cf + _dot_tn(vb, kw.astype(BF16))
            n_scr[h] = d * nrow + jnp.sum(kw, axis=0, keepdims=True)
        den_ref[...] = den_tile

    return pl.pallas_call(
        body, name=name,
        out_shape=(jax.ShapeDtypeStruct((S, D_MODEL), BF16), jax.ShapeDtypeStruct((S, D_MODEL), F32),
                   jax.ShapeDtypeStruct((S, LANES), F32),
                   jax.ShapeDtypeStruct((NC, ML_HEADS, ML_DV, ML_DQK), BF16),
                   jax.ShapeDtypeStruct((NC, ML_HEADS, 1, ML_DQK), F32)),
        grid=(NC,),
        in_specs=[qk(C_QM // 512), qk(C_KM // 512), wide(C_VM // D_MODEL), wide(C_OM // D_MODEL),
                  pl.BlockSpec((8, MLC), lambda c: (0, c)), col, col, col, col,
                  pl.BlockSpec((1, 1, LANES), lambda c: (c, 0, 0)), pl.BlockSpec((1, D_MODEL), lambda c: (0, 0))],
        out_specs=(pl.BlockSpec((MLC, D_MODEL), lambda c: (c, 0)), pl.BlockSpec((MLC, D_MODEL), lambda c: (c, 0)),
                   col, pl.BlockSpec((1, ML_HEADS, ML_DV, ML_DQK), lambda c: (c, 0, 0, 0)),
                   pl.BlockSpec((1, ML_HEADS, 1, ML_DQK), lambda c: (c, 0, 0, 0))),
        scratch_shapes=[pltpu.VMEM((ML_HEADS, ML_DV, ML_DQK), F32), pltpu.VMEM((ML_HEADS, 1, ML_DQK), F32)],
        compiler_params=_cparams(("arbitrary",)),
    )(pm, pm, pm, pm, a_row, A, wi, em, wk, dec, w_hn)


def _mlstm_bwd(dha, pm, hp_all, den_all, a_row, A, wi, em, wk, dec, cst, nst, w_hn, name):
    S = pm.shape[0]
    NC = S // MLC
    idx, qk, wide, col = _ml_specs(True, NC)

    def body(dha_ref, q_ref, k_ref, v_ref, o_ref, hp_ref, den_ref, arow_ref, A_ref, wi_ref, em_ref, wk_ref,
             dec_ref, cst_ref, nst_ref, whn_ref,
             dqk_ref, dv_ref, do_ref, rk_ref, kc_ref, t_ref, dwhn_ref, dC_scr, dn_scr, t_scr):
        @pl.when(pl.program_id(0) == 0)
        def _():
            dC_scr[...] = jnp.zeros_like(dC_scr)
            dn_scr[...] = jnp.zeros_like(dn_scr)
            t_scr[...] = jnp.zeros_like(t_scr)
            dwhn_ref[...] = jnp.zeros_like(dwhn_ref)

        lanes = lax.broadcasted_iota(jnp.int32, (MLC, LANES), 1)
        lane1 = lax.broadcasted_iota(jnp.int32, (1, LANES), 1)
        t_ref[0, 0:1, :] = t_scr[...]
        rk_tile = jnp.zeros((MLC, LANES), F32)
        kc_tile = jnp.zeros((MLC, LANES), F32)
        t_new = jnp.zeros((1, LANES), F32)
        for h in range(ML_HEADS):
            hs = slice(h * ML_DQK, (h + 1) * ML_DQK)
            vs = slice(h * ML_DV, (h + 1) * ML_DV)
            hp = hp_ref[:, vs]
            sig = jax.nn.sigmoid(o_ref[:, vs])
            whn = whn_ref[:, vs]
            r = _rstd(hp)
            dga = dha_ref[:, vs]
            do_ref[:, vs] = (dga * (hp * r * whn) * sig * (1.0 - sig)).astype(do_ref.dtype)
            dhn = dga * sig
            dhp, dwt = _rmsnorm_bwd_math(dhn, hp, whn)
            dwhn_ref[:, vs] += jnp.sum(dwt, axis=0, keepdims=True)
            den = den_ref[:, h:h + 1]
            floor = em_ref[:, h:h + 1]
            D = jnp.maximum(jnp.abs(den), floor)
            dnum = dhp / D
            dh_h = jnp.sum(dhp * hp, axis=1, keepdims=True)
            active = jnp.abs(den) >= floor
            dden = -dh_h / D * jnp.where(active, jnp.sign(den), 0.0)
            phi = jnp.where(active, 0.0, dh_h)
            qb, kb, qf, kf, qk_, W = _ml_intra(q_ref, k_ref, arow_ref, A_ref, h)
            vf = v_ref[:, vs]
            vb = vf.astype(BF16)
            Cb = cst_ref[0, h]
            nrow = nst_ref[0, h]
            wic = wi_ref[:, h:h + 1]
            wkc = wk_ref[:, h:h + 1]
            d = dec_ref[0, :, h:h + 1]
            dCn = dC_scr[h]
            dCb = dCn.astype(BF16)
            dnn = dn_scr[h]
            dnumb = dnum.astype(BF16)
            s = qk_ * W
            ds = (_dot_nt(dnumb, vb) + dden) * W
            dsb = ds.astype(BF16)
            dnw = (wic * dnum).astype(BF16)
            wd = wic * dden
            kw = kf * wkc
            dv_state = _dot_nt(kw.astype(BF16), dCb)
            dq = _dot_nn(dsb, kb) + _dot_nn(dnw, Cb) + wd * nrow
            dk_state = wkc * (_dot_nn(vb, dCb) + dnn)
            dk = _dot_tn(dsb, qb) + dk_state
            dv = _dot_tn(s.astype(BF16), dnumb) + dv_state
            dC = d * dCn + _dot_tn(dnw, qb)
            dn = d * dnn + jnp.sum(wd * qf, axis=0, keepdims=True)
            dC_scr[h] = dC
            dn_scr[h] = dn
            dqk_ref[:, hs] = (dq * _ML_SCALE).astype(dqk_ref.dtype)
            dqk_ref[:, C_KM + h * ML_DQK:C_KM + (h + 1) * ML_DQK] = dk.astype(dqk_ref.dtype)
            dv_ref[:, vs] = dv.astype(dv_ref.dtype)
            G = ds * qk_
            inter = _dot_nt(qb, Cb)
            qn = jnp.sum(qf * nrow, axis=1, keepdims=True)
            R = (jnp.sum(G, axis=1, keepdims=True)
                 + wic * (jnp.sum(dnum * inter, axis=1, keepdims=True) + dden * qn))
            K = jnp.sum(G.T, axis=1, keepdims=True) + jnp.sum(kf * dk_state, axis=1, keepdims=True)
            rk_tile = jnp.where(lanes == h, R - K, rk_tile)
            kc_tile = jnp.where(lanes == h, phi, kc_tile)
            tt = (jnp.sum(jnp.sum(dC * Cb.astype(F32), axis=1, keepdims=True), axis=0, keepdims=True)
                  + jnp.sum(dn * nrow, axis=1, keepdims=True))
            t_new = jnp.where(lane1 == h, tt, t_new)
        rk_ref[...] = rk_tile
        kc_ref[...] = kc_tile
        t_ref[0, 1:2, :] = t_new
        t_scr[...] = t_new

    act = lambda n: jax.ShapeDtypeStruct((S, n), BF16)
    cs = jax.ShapeDtypeStruct((S, LANES), F32)
    rowblk = lambda n: pl.BlockSpec((MLC, n), lambda c: (idx(c), 0))
    return pl.pallas_call(
        body, name=name,
        out_shape=(act(D_MODEL), act(D_MODEL), act(D_MODEL), cs, cs,
                   jax.ShapeDtypeStruct((NC, 2, LANES), F32), jax.ShapeDtypeStruct((1, D_MODEL), F32)),
        grid=(NC,),
        in_specs=[rowblk(D_MODEL), qk(C_QM // 512), qk(C_KM // 512), wide(C_VM // D_MODEL), wide(C_OM // D_MODEL),
                  rowblk(D_MODEL), col, pl.BlockSpec((8, MLC), lambda c: (0, idx(c))), col, col, col, col,
                  pl.BlockSpec((1, 1, LANES), lambda c: (idx(c), 0, 0)),
                  pl.BlockSpec((1, ML_HEADS, ML_DV, ML_DQK), lambda c: (idx(c), 0, 0, 0)),
                  pl.BlockSpec((1, ML_HEADS, 1, ML_DQK), lambda c: (idx(c), 0, 0, 0)),
                  pl.BlockSpec((1, D_MODEL), lambda c: (0, 0))],
        out_specs=(rowblk(D_MODEL), rowblk(D_MODEL), rowblk(D_MODEL), col, col,
                   pl.BlockSpec((1, 2, LANES), lambda c: (idx(c), 0, 0)), pl.BlockSpec((1, D_MODEL), lambda c: (0, 0))),
        scratch_shapes=[pltpu.VMEM((ML_HEADS, ML_DV, ML_DQK), F32), pltpu.VMEM((ML_HEADS, 1, ML_DQK), F32),
                        pltpu.VMEM((1, LANES), F32)],
        compiler_params=_cparams(("arbitrary",)),
    )(dha, pm, pm, pm, pm, hp_all, den_all, a_row, A, wi, em, wk, dec, cst, nst, w_hn)


_FOX_SCALE = FOX_DH ** -0.5
_NEG = -1e30
_LOG2E = 1.4426950408889634
_LN2 = 0.6931471805599453
_QF_BLK, _KF_BLK, _VF_BLK = 0, FOX_HEADS, 2 * FOX_HEADS


def _lane_pick(tile, lane):
    lanes = lax.broadcasted_iota(jnp.int32, tile.shape, 1)
    return jnp.sum(jnp.where(lanes == lane, tile, 0.0), axis=1, keepdims=True)


def _col_to_row(col):
    return jnp.max(jnp.broadcast_to(col, (col.shape[0], LANES)).T, axis=0, keepdims=True)


def _causal(q0, k0, shape, q_axis):
    qpos = q0 + lax.broadcasted_iota(jnp.int32, shape, q_axis)
    kpos = k0 + lax.broadcasted_iota(jnp.int32, shape, 1 - q_axis)
    return kpos <= qpos


def _fox_fwd(pf, fc, fk_row, name):
    S = pf.shape[0]
    TQ, TK = FOX_TQ_FWD, FOX_TK_FWD
    nq, nk = S // TQ, S // TK
    c1 = _FOX_SCALE * _LOG2E

    def body(q_ref, k_ref, v_ref, fc_ref, fr_ref, o_ref, lse_ref):
        h, i = pl.program_id(0), pl.program_id(1)
        qb = q_ref[...]
        fq2 = _lane_pick(fc_ref[...], h) * _LOG2E

        def step(j, carry, masked):
            m, l, acc = carry
            off = pl.multiple_of(j * TK, TK)
            t = _dot_nt(qb, k_ref[pl.ds(off, TK), :]) * c1 - fr_ref[0, j] * _LOG2E
            if masked:
                t = jnp.where(_causal(i * TQ, j * TK, (TQ, TK), 0), t, _NEG)
            m_new = jnp.maximum(m, jnp.max(t, axis=1, keepdims=True) + fq2)
            alpha = jnp.exp2(m - m_new)
            p = jnp.exp2(t + (fq2 - m_new))
            l = alpha * l + jnp.sum(p, axis=1, keepdims=True)
            acc = alpha * acc + _dot_nn(p.astype(BF16), v_ref[pl.ds(off, TK), :])
            return m_new, l, acc

        init = (jnp.full((TQ, 1), _NEG, F32), jnp.zeros((TQ, 1), F32), jnp.zeros((TQ, FOX_DH), F32))
        last = (i * TQ) // TK
        carry = lax.fori_loop(0, last, lambda j, c: step(j, c, False), init)
        for d in range(max(1, TQ // TK)):
            carry = step(last + d, carry, True)
        m, l, acc = carry
        o_ref[...] = (acc / l).astype(o_ref.dtype)
        lse_ref[0, 0] = _col_to_row((m + jnp.log2(l)) * _LN2)

    head = lambda blk: pl.BlockSpec((S, FOX_DH), lambda h, i: (0, blk + h))
    return pl.pallas_call(
        body, name=name,
        out_shape=(jax.ShapeDtypeStruct((S, D_MODEL), BF16), jax.ShapeDtypeStruct((FOX_HEADS, nq, 1, TQ), F32)),
        grid=(FOX_HEADS, nq),
        in_specs=[pl.BlockSpec((TQ, FOX_DH), lambda h, i: (i, _QF_BLK + h)), head(_KF_BLK), head(_VF_BLK),
                  pl.BlockSpec((TQ, LANES), lambda h, i: (i, 0)),
                  pl.BlockSpec((1, nk, 1, TK), lambda h, i: (h, 0, 0, 0))],
        out_specs=(pl.BlockSpec((TQ, FOX_DH), lambda h, i: (i, h)),
                   pl.BlockSpec((1, 1, 1, TQ), lambda h, i: (h, i, 0, 0))),
        compiler_params=_cparams(("parallel", "arbitrary")),
    )(pf, pf, pf, fc, fk_row)


def _fox_bwd(dhb, hb, pf, lse_row, fq_row, fc, name):
    S = pf.shape[0]
    TQ, TK = FOX_TQ, FOX_TK
    nq, nk, r = S // TQ, S // TK, TK // TQ
    c1 = _FOX_SCALE * _LOG2E

    def body(q_ref, k_ref, v_ref, do_ref, o_ref, lse_ref, fq_ref, fc_ref,
             dq_ref, dk_ref, dv_ref, dFk_ref, dFq_ref, dq_acc, qside, delta, dk_acc, dv_acc, cs_acc):
        h, j = pl.program_id(0), pl.program_id(1)

        @pl.when(j == 0)
        def _():
            dq_acc[...] = jnp.zeros_like(dq_acc)
            dFq_ref[...] = jnp.zeros_like(dFq_ref)

            def fill(b, _):
                off = pl.multiple_of(b * TQ, TQ)
                prod = do_ref[pl.ds(off, TQ), :].astype(F32) * o_ref[pl.ds(off, TQ), :].astype(F32)
                delta[b] = jnp.sum(prod.T, axis=0, keepdims=True)
                qside[b] = (fq_ref[0, b] - lse_ref[0, b]) * _LOG2E
                return 0

            lax.fori_loop(0, nq, fill, 0)

        kb = k_ref[...]
        vb = v_ref[...]
        fk2 = _lane_pick(fc_ref[...], h) * _LOG2E
        dk_acc[...] = jnp.zeros_like(dk_acc)
        dv_acc[...] = jnp.zeros_like(dv_acc)
        cs_acc[...] = jnp.zeros_like(cs_acc)

        def step(i, masked):
            off = pl.multiple_of(i * TQ, TQ)
            qb = q_ref[pl.ds(off, TQ), :]
            dob = do_ref[pl.ds(off, TQ), :]
            t = _dot_nt(kb, qb) * c1 + qside[i] - fk2
            if masked:
                t = jnp.where(_causal(i * TQ, j * TK, (TK, TQ), 1), t, _NEG)
            p = jnp.exp2(t)
            dv_acc[...] += _dot_nn(p.astype(BF16), dob)
            ds = p * (_dot_nt(vb, dob) - delta[i])
            dsb = ds.astype(BF16)
            dk_acc[...] += _dot_nn(dsb, qb)
            dq_acc[pl.ds(off, TQ), :] += _dot_tn(dsb, kb)
            cs_acc[...] += jnp.sum(ds, axis=1, keepdims=True)
            dFq_ref[0, i] += jnp.sum(ds, axis=0, keepdims=True)

        for d in range(r):
            step(r * j + d, True)

        def rest(i, _):
            step(i, False)
            return 0

        lax.fori_loop(r * j + r, nq, rest, 0)
        dk_ref[...] = (dk_acc[...] * _FOX_SCALE).astype(dk_ref.dtype)
        dv_ref[...] = dv_acc[...].astype(dv_ref.dtype)
        dFk_ref[0, 0] = -_col_to_row(cs_acc[...])

        @pl.when(j == nk - 1)
        def _():
            dq_ref[...] = (dq_acc[...] * _FOX_SCALE).astype(dq_ref.dtype)

    head = lambda blk: pl.BlockSpec((S, FOX_DH), lambda h, j: (0, blk + h))
    kblk = lambda blk: pl.BlockSpec((TK, FOX_DH), lambda h, j: (j, blk + h))
    qrows = pl.BlockSpec((1, nq, 1, TQ), lambda h, j: (h, 0, 0, 0))
    act = jax.ShapeDtypeStruct((S, D_MODEL), BF16)
    return pl.pallas_call(
        body, name=name,
        out_shape=(act, act, act, jax.ShapeDtypeStruct((FOX_HEADS, nk, 1, TK), F32),
                   jax.ShapeDtypeStruct((FOX_HEADS, nq, 1, TQ), F32)),
        grid=(FOX_HEADS, nk),
        in_specs=[head(_QF_BLK), kblk(_KF_BLK), kblk(_VF_BLK), head(0), head(0), qrows, qrows,
                  pl.BlockSpec((TK, LANES), lambda h, j: (j, 0))],
        out_specs=(head(0), kblk(0), kblk(0), pl.BlockSpec((1, 1, 1, TK), lambda h, j: (h, j, 0, 0)), qrows),
        scratch_shapes=[pltpu.VMEM((S, FOX_DH), F32), pltpu.VMEM((nq, 1, TQ), F32), pltpu.VMEM((nq, 1, TQ), F32),
                        pltpu.VMEM((TK, FOX_DH), F32), pltpu.VMEM((TK, FOX_DH), F32), pltpu.VMEM((TK, 1), F32)],
        compiler_params=_cparams(("parallel", "arbitrary")),
    )(pf, pf, pf, dhb, hb, lse_row, fq_row, fc)


def _pad_lanes(v):
    return jnp.pad(v, ((0, 0), (0, LANES - v.shape[1])))


def _local_step(x, target, wmain_t, wsmall_t, rest_arrived, rest_weights, p, on_grads, advance, token):
    S = x.shape[0]
    bi, bf, bff = _pad_lanes(p["b_ml_i"]), _pad_lanes(p["b_ml_f"]), _pad_lanes(p["b_fox_f"])

    h0 = _rmsnorm_fwd(x, p["norm_mix_pre"] + token[0:1, 0:1], "norm_mix_pre")
    pm = _mm(h0, wmain_t, "nt", F32, "proj_mlstm", b_rows=(0, N_ML))
    pf = _mm(h0, wmain_t, "nt", BF16, "proj_fox", b_rows=(N_ML, N_FOX))
    pg = _mm(h0, wmain_t, "nt", F32, "proj_merge", b_rows=(N_ML + N_FOX, N_GATE))
    ps = _mm(h0, wsmall_t, "nt", F32, "proj_gates")
    a, A, wi, em, wk, dec, Fc = _gates_fwd(ps, bi, bf, bff, "gates_fwd")
    a_row = a[:, :8].T
    ha, hp, den, cst, nst = _mlstm_fwd(pm, a_row, A, wi, em, wk, dec, p["ml_head_norm"], "mlstm_fwd")
    ft = Fc[:, :FOX_HEADS].T + rest_arrived(ha)[0, 0]
    fq_row = ft.reshape(FOX_HEADS, S // FOX_TQ, 1, FOX_TQ)
    fk_row = ft.reshape(FOX_HEADS, S // FOX_TK, 1, FOX_TK)
    hb, lse_row = _fox_fwd(pf, Fc, ft.reshape(FOX_HEADS, S // FOX_TK_FWD, 1, FOX_TK_FWD), "fox_fwd")
    wa, wb, wout, wup, wdown = rest_weights(hb)
    ya = _mm(ha, wa, "nn", F32, "branch_a")
    yb = _mm(hb, wb, "nn", F32, "branch_b")
    merged = _merge_fwd(ya, yb, pg, p["b_gate_a"], p["b_gate_b"], "merge_fwd")
    z = _mm(merged, wout, "nn", F32, "out_proj")
    x1, h2 = _resid_norm_fwd(x, z, p["norm_mix_post"], p["norm_ffn_pre"], "resid_mix")
    up = _mm(h2, wup, "nn", F32, "ffn_up")
    act, conv_a, conv_g = _conv_act_fwd(up, p["conv_w"], p["conv_b"], "conv_act_fwd")
    d = _mm(act, wdown, "nn", F32, "ffn_down", tk=D_FF)
    loss_row, dy, dd, g_norm_ffn_post = _loss_head(x1, d, p["norm_ffn_post"], target, "loss_head")
    dact = _mm(dd, wdown, "nt", F32, "d_act")
    g_wdown = _mm(act, dd, "tn", F32, "dw_down", tm=1408, tk=2048)
    dupa, dupg, dcwa, dcwg, dcba, dcbg = _conv_act_bwd(up, conv_a, conv_g, dact, p["conv_w"], "conv_act_bwd")
    g_conv_w = jnp.concatenate([dcwa, dcwg], axis=1)
    g_conv_b = jnp.concatenate([dcba, dcbg], axis=1)
    dh2 = _mm_sum_parts([dupa, dupg], wup, F32, "d_h2", trans_b=True)
    g_wup = _mm(h2, [dupa, dupg], "tn", F32, "dw_up", tk=2048)
    token = on_grads("ffn", dict(w_up=g_wup, w_down=g_wdown))
    dx1, dz, g_norm_ffn_pre, g_norm_mix_post = _norm_chain_bwd(
        dh2, x1, p["norm_ffn_pre"] + token[0:1, 0:1], dy, z, p["norm_mix_post"], "norm_chain_bwd")
    dmerged = _mm(dz, wout, "nt", F32, "d_merged")
    g_wout = _mm(merged, dz, "tn", F32, "dw_out", tk=2048)
    dya, dyb, dga, dgb, g_b_gate_a, g_b_gate_b = _merge_bwd(dmerged, ya, yb, pg, p["b_gate_a"], p["b_gate_b"], "merge_bwd")
    dha = _mm(dya, wa, "nt", F32, "d_ha")
    g_wa = _mm(ha, dya, "tn", F32, "dw_a", tk=2048)
    dhb = _mm(dyb, wb, "nt", BF16, "d_hb")
    g_wb = _mm(hb, dyb, "tn", F32, "dw_b", tk=2048)
    token = advance("ffn", g_wb) + on_grads("mix", dict(w_out=g_wout, w_branch_a=g_wa, w_branch_b=g_wb))
    dqkm, dvm, dom, rk, kc, tch, g_ml_head_norm = _mlstm_bwd(
        dha, pm, hp, den, a_row, A, wi, em, wk, dec, cst, nst, p["ml_head_norm"] + token[0:1, 0:1], "mlstm_bwd")
    token = advance("mix", dqkm)
    dqf, dkf, dvf, dFk, dFq = _fox_bwd(dhb, hb, pf, lse_row.reshape(fq_row.shape), fq_row + token[0, 0], Fc, "fox_bwd")
    dF = jnp.pad((dFk.reshape(FOX_HEADS, S) + dFq.reshape(FOX_HEADS, S)).T, ((0, 0), (0, LANES - FOX_HEADS)))
    dps, dbias = _gates_bwd(ps, bi, bf, bff, rk, kc, tch, dF, "gates_bwd")
    dpm = [dqkm, dvm, dom, dqf, dkf, dvf, dga, dgb]
    g_wmain_t = _mm(dpm, h0, "tn", F32, "dw_main")
    token = on_grads("in", dict(w_in=g_wmain_t))
    g_wsmall_t = _mm(dps, h0, "tn", F32, "dw_gates")
    dh0s = _mm(dps, wsmall_t + token[0:1, 0:1].astype(BF16), "nn", F32, "d_h0_gates")
    token = advance("in", dh0s)
    dh0 = _mm_sum_parts(dpm, wmain_t, F32, "d_h0_main", after=token)
    grad_x, g_norm_mix_pre = _rmsnorm_bwd([dh0, dh0s], x, p["norm_mix_pre"], dx1, F32, "norm_mix_pre_bwd")

    big = dict(wsmall_t=g_wsmall_t)
    small = dict(norm_mix_pre=g_norm_mix_pre, ml_head_norm=g_ml_head_norm, b_gate_a=g_b_gate_a, b_gate_b=g_b_gate_b,
                 norm_mix_post=g_norm_mix_post, norm_ffn_pre=g_norm_ffn_pre, norm_ffn_post=g_norm_ffn_post,
                 conv_b=g_conv_b, b_ml_i=dbias[:, 0:ML_HEADS], b_ml_f=dbias[:, LANES:LANES + ML_HEADS],
                 b_fox_f=dbias[:, 2 * LANES:2 * LANES + FOX_HEADS], conv_w=g_conv_w)
    return loss_row, grad_x, big, small


def _row_tile(r, target=256):
    best = None
    for t in range(8, min(r, target) + 1, 8):
        if r % t == 0:
            best = t
    return best if best is not None else r


def _adamw(w, g, m, v, name):
    _, R, C = w.shape
    tr = _row_tile(R)
    tc = C
    if tr == R and R > 256:
        tc = 256

    def body(w_ref, g_ref, m_ref, v_ref, d_ref, mo_ref, vo_ref):
        gv = g_ref[...]
        mn = ADAM_B1 * m_ref[0] + (1.0 - ADAM_B1) * gv
        vn = ADAM_B2 * v_ref[0] + (1.0 - ADAM_B2) * (gv * gv)
        m_hat = mn / (1.0 - ADAM_B1 ** ADAM_STEP)
        v_hat = vn / (1.0 - ADAM_B2 ** ADAM_STEP)
        d_ref[0] = -ADAM_LR * (m_hat / (jnp.sqrt(v_hat) + ADAM_EPS) + ADAM_WD * w_ref[0])
        mo_ref[0] = mn
        vo_ref[0] = vn

    blk = pl.BlockSpec((1, tr, tc), lambda i, j: (0, i, j))
    o = jax.ShapeDtypeStruct((1, R, C), F32)
    return pl.pallas_call(
        body, name=name, out_shape=(o, o, o), grid=(R // tr, C // tc),
        in_specs=[blk, pl.BlockSpec((tr, tc), lambda i, j: (i, j)), blk, blk], out_specs=(blk,) * 3,
        compiler_params=_cparams(("parallel", "parallel")),
    )(w, g, m, v)


ANY = pl.BlockSpec(memory_space=pl.ANY)


def _place():
    x, y, c = lax.axis_index("x"), lax.axis_index("y"), lax.axis_index("c")
    chips = [(1 - x, y), (x, 1 - y), (1 - x, 1 - y)]
    return x, y, c, chips


def _block(ref, kind, k, rows=None):
    if kind == "rows":
        return ref.at[k] if rows is None else ref.at[k, pl.ds(*rows), :]
    cb = ref.shape[1] // 4
    return ref.at[:, pl.ds(k * cb, cb)] if rows is None else ref.at[pl.ds(*rows), pl.ds(k * cb, cb)]


def _gathered_shape(s, kind):
    return (4,) + s.shape if kind == "rows" else (s.shape[0], 4 * s.shape[1])


def _gather_weights(shards, kinds, smalls):
    n, ns = len(shards), len(smalls)

    def body(*refs):
        ins, sm_in = refs[:n], refs[n:n + ns]
        outs, sm_out = refs[n + ns:2 * n + ns], refs[2 * n + ns:2 * (n + ns)]
        send_sems, recv_sems, sm_send, sm_recv, local_sems = refs[2 * (n + ns):]
        x, y, c, chips = _place()
        sibling = (x, y, 1 - c)
        kme = 2 * x + y

        def half(a, k, hc):
            h = ins[a].shape[0] // 2
            return _block(outs[a], kinds[a], k, (hc * h, h))

        def remote(a, slot, src, dst, to):
            return pltpu.make_async_remote_copy(src_ref=src, dst_ref=dst, send_sem=send_sems.at[a * 7 + slot],
                                                recv_sem=recv_sems.at[a * 7 + slot], device_id=to, device_id_type=MESH)

        def sm_copy(b, j, k, to):
            return pltpu.make_async_remote_copy(src_ref=sm_in[b], dst_ref=sm_out[b].at[k], send_sem=sm_send.at[3 * b + j],
                                                recv_sem=sm_recv.at[3 * b + j], device_id=to, device_id_type=MESH)

        local = [pltpu.make_async_copy(sm_in[b], sm_out[b].at[kme], local_sems.at[b]) for b in range(ns)]
        for cp in local:
            cp.start()
        sends = [remote(a, 6, ins[a], _block(outs[a], kinds[a], kme), sibling) for a in range(n)]
        for a in range(n):
            h = ins[a].shape[0] // 2
            for j, chip in enumerate(chips):
                sends.append(remote(a, j, ins[a].at[pl.ds(c * h, h), :], half(a, kme, c), (*chip, c)))
        for b in range(ns):
            for j, chip in enumerate(chips):
                sends.append(sm_copy(b, j, kme, (*chip, c)))
        for cp in sends:
            cp.start()
        for a in range(n):
            for j, chip in enumerate(chips):
                kj = 2 * chip[0] + chip[1]
                remote(a, j, half(a, kj, c), half(a, kj, c), (*chip, c)).wait_recv()
                fwd = remote(a, 3 + j, half(a, kj, c), half(a, kj, c), sibling)
                fwd.start()
                sends.append(fwd)
        for a in range(n):
            for j, chip in enumerate(chips):
                kj = 2 * chip[0] + chip[1]
                remote(a, 3 + j, half(a, kj, 1 - c), half(a, kj, 1 - c), sibling).wait_recv()
        for b in range(ns):
            for j, chip in enumerate(chips):
                sm_copy(b, j, 2 * chip[0] + chip[1], (*chip, c)).wait_recv()
        for a in range(n):
            remote(a, 6, ins[a], _block(outs[a], kinds[a], kme), sibling).wait_recv()
        for cp in sends:
            cp.wait_send()
        for cp in local:
            cp.wait()

    outs = pl.pallas_call(
        body, name="gather_weights",
        out_shape=tuple([jax.ShapeDtypeStruct(_gathered_shape(s, k), s.dtype) for s, k in zip(shards, kinds)]
                        + [jax.ShapeDtypeStruct((4,) + s.shape, s.dtype) for s in smalls]),
        in_specs=[ANY] * (n + ns), out_specs=tuple([ANY] * (n + ns)),
        scratch_shapes=[pltpu.SemaphoreType.DMA((7 * n,)), pltpu.SemaphoreType.DMA((7 * n,)),
                        pltpu.SemaphoreType.DMA((3 * ns,)), pltpu.SemaphoreType.DMA((3 * ns,)),
                        pltpu.SemaphoreType.DMA((ns,))],
    )(*shards, *smalls)
    return outs[:n], outs[n:]


_IN_HBM = pl.BlockSpec(memory_space=pltpu.HBM)
_SEMS = pl.BlockSpec(memory_space=pltpu.SEMAPHORE)
_DATAFLOW = pltpu.SideEffectType.DATAFLOW_SIDE_EFFECTING


def _hbm(t):
    return pltpu.HBM(t.shape, t.dtype)


def _gather_copies(ins, outs, send_sems, recv_sems, kinds):
    x, y, c, chips = _place()
    kme = 2 * x + y
    cps = []
    for a in range(len(ins)):
        h = ins[a].shape[0] // 2
        for j, chip in enumerate(chips + [None]):
            to = (x, y, 1 - c) if chip is None else (*chip, c)
            src = ins[a] if chip is None else ins[a].at[pl.ds(c * h, h), :]
            dst = _block(outs[a], kinds[a], kme, None if chip is None else (c * h, h))
            cps.append(pltpu.make_async_remote_copy(src_ref=src, dst_ref=dst, send_sem=send_sems.at[4 * a + j],
                                                    recv_sem=recv_sems.at[4 * a + j], device_id=to, device_id_type=MESH))
    return cps


def _gather_start(shards, kinds, name):
    n = len(shards)
    outs = [lax.empty(_gathered_shape(s, k), s.dtype) for s, k in zip(shards, kinds)]

    def body(*refs):
        for cp in _gather_copies(refs[:n], refs[n:2 * n], refs[2 * n], refs[2 * n + 1], kinds):
            cp.start()
        refs[-1][...] = jnp.zeros_like(refs[-1])

    return pl.pallas_call(
        body, name=name,
        out_shape=(pltpu.SemaphoreType.DMA((4 * n,)), pltpu.SemaphoreType.DMA((4 * n,)),
                   *[_hbm(t) for t in shards], *[_hbm(t) for t in outs], jax.ShapeDtypeStruct((8, LANES), F32)),
        in_specs=[_IN_HBM] * (2 * n),
        out_specs=(_SEMS, _SEMS, *[_IN_HBM] * (2 * n), pl.BlockSpec(memory_space=pltpu.VMEM)),
        input_output_aliases={a: 2 + a for a in range(2 * n)},
        compiler_params=pltpu.CompilerParams(has_side_effects=_DATAFLOW),
    )(*[pltpu.with_memory_space_constraint(t, pltpu.HBM) for t in list(shards) + outs])


def _gather_wait(started, after, kinds, name):
    n = (len(started) - 3) // 2
    bufs = started[2:2 + 2 * n]

    def body(*refs):
        for cp in _gather_copies(refs[:n], refs[n:2 * n], refs[2 * n], refs[2 * n + 1], kinds):
            cp.wait_send()
            cp.wait_recv()

    outs = pl.pallas_call(
        body, name=name, out_shape=tuple(_hbm(t) for t in bufs),
        in_specs=[_IN_HBM] * (2 * n) + [_SEMS, _SEMS, ANY], out_specs=tuple([_IN_HBM] * (2 * n)),
        input_output_aliases={a: a for a in range(2 * n)},
        compiler_params=pltpu.CompilerParams(has_side_effects=_DATAFLOW),
    )(*bufs, started[0], started[1], after)
    return outs[n:]


def _relay_copies(bufs, send_sems, recv_sems, kinds):
    x, y, c, chips = _place()
    cps = []
    for a in range(len(bufs)):
        h = (bufs[a].shape[1] if kinds[a] == "rows" else bufs[a].shape[0]) // 2
        for j, chip in enumerate(chips):
            part = _block(bufs[a], kinds[a], 2 * chip[0] + chip[1], (c * h, h))
            cps.append(pltpu.make_async_remote_copy(src_ref=part, dst_ref=part, send_sem=send_sems.at[3 * a + j],
                                                    recv_sem=recv_sems.at[3 * a + j], device_id=(x, y, 1 - c),
                                                    device_id_type=MESH))
    return cps


def _join_copies(bufs, send_sems, recv_sems, kinds):
    x, y, c, _ = _place()
    cps = []
    for a in range(len(bufs)):
        h = bufs[a].shape[0] // 2
        mine = bufs[a].at[pl.ds(c * h, h), :]
        cps.append(pltpu.make_async_remote_copy(src_ref=mine, dst_ref=mine, send_sem=send_sems.at[a],
                                                recv_sem=recv_sems.at[a], device_id=(x, y, 1 - c), device_id_type=MESH))
    return cps


def _inplace_start(copies, per_array, bufs, kinds, name):
    n = len(bufs)

    def body(*refs):
        for cp in copies(refs[:n], refs[n], refs[n + 1], kinds):
            cp.start()
        refs[-1][...] = jnp.zeros_like(refs[-1])

    return pl.pallas_call(
        body, name=name,
        out_shape=(pltpu.SemaphoreType.DMA((per_array * n,)), pltpu.SemaphoreType.DMA((per_array * n,)),
                   *[_hbm(t) for t in bufs], jax.ShapeDtypeStruct((8, LANES), F32)),
        in_specs=[_IN_HBM] * n, out_specs=(_SEMS, _SEMS, *[_IN_HBM] * n, pl.BlockSpec(memory_space=pltpu.VMEM)),
        input_output_aliases={a: 2 + a for a in range(n)},
        compiler_params=pltpu.CompilerParams(has_side_effects=_DATAFLOW),
    )(*[pltpu.with_memory_space_constraint(t, pltpu.HBM) for t in bufs])


def _inplace_wait(copies, started, after, kinds, name):
    n = len(started) - 3
    bufs = started[2:2 + n]

    def body(*refs):
        for cp in copies(refs[:n], refs[n], refs[n + 1], kinds):
            cp.wait_send()
            cp.wait_recv()

    return pl.pallas_call(
        body, name=name, out_shape=tuple(_hbm(t) for t in bufs),
        in_specs=[_IN_HBM] * n + [_SEMS, _SEMS, ANY], out_specs=tuple([_IN_HBM] * n),
        input_output_aliases={a: a for a in range(n)},
        compiler_params=pltpu.CompilerParams(has_side_effects=_DATAFLOW),
    )(*bufs, started[0], started[1], after)


def _add_halves(g, r1, cvec, kind, name):
    def body(c_ref, g_ref, r_ref, o_ref):
        o_ref[...] = (g_ref[...] + r_ref[...]).astype(o_ref.dtype)

    if kind == "rows":
        _, h, C = r1.shape
        tr = _row_tile(h, 512)
        nt = h // tr
        grid = (4, nt)
        g_spec = pl.BlockSpec((1, tr, C), lambda k, i, c_ref: (k, c_ref[0] * nt + i, 0))
        r_spec = pl.BlockSpec((1, tr, C), lambda k, i, c_ref: (k, i, 0))
    else:
        h, C4 = r1.shape
        tr, tc = _row_tile(h, 512), C4 // 4
        nt = h // tr
        grid = (nt, 4)
        g_spec = pl.BlockSpec((tr, tc), lambda i, k, c_ref: (c_ref[0] * nt + i, k))
        r_spec = pl.BlockSpec((tr, tc), lambda i, k, c_ref: (i, k))
    return pl.pallas_call(
        body, name=name, out_shape=jax.ShapeDtypeStruct(r1.shape, BF16),
        grid_spec=pltpu.PrefetchScalarGridSpec(num_scalar_prefetch=1, grid=grid, in_specs=[g_spec, r_spec],
                                               out_specs=r_spec),
        compiler_params=_cparams(("parallel", "parallel")),
    )(cvec, g, r1)


def _chip_copies(ins, lands, send_sems, recv_sems, kinds):
    x, y, c, chips = _place()
    return [pltpu.make_async_remote_copy(
        src_ref=_block(ins[a], kinds[a], 2 * chip[0] + chip[1]), dst_ref=lands[a].at[j],
        send_sem=send_sems.at[3 * a + j], recv_sem=recv_sems.at[3 * a + j], device_id=(*chip, c), device_id_type=MESH)
        for a in range(len(ins)) for j, chip in enumerate(chips)]


def _land_shape(s, kind):
    return (3,) + (s.shape[1:] if kind == "rows" else (s.shape[0], s.shape[1] // 4))


def _sibling_copies(ins, lands, send_sems, recv_sems, kinds):
    x, y, c, _ = _place()
    cps = []
    for a in range(len(ins)):
        h = lands[a].shape[-2]
        src = ins[a].at[:, pl.ds((1 - c) * h, h), :] if kinds[a] == "rows" else ins[a].at[pl.ds((1 - c) * h, h), :]
        cps.append(pltpu.make_async_remote_copy(src_ref=src, dst_ref=lands[a], send_sem=send_sems.at[a],
                                                recv_sem=recv_sems.at[a], device_id=(x, y, 1 - c), device_id_type=MESH))
    return cps


def _half_shape(g, kind):
    return (4, g.shape[1] // 2, g.shape[2]) if kind == "rows" else (g.shape[0] // 2, g.shape[1])


def _exchange_start(copies, per_array, srcs, land_shapes, kinds, name, zeroed=False):
    n = len(srcs)
    lands = [(jnp.zeros if zeroed else lax.empty)(shape, s.dtype) for shape, s in zip(land_shapes, srcs)]

    def body(*refs):
        for cp in copies(refs[:n], refs[n:2 * n], refs[2 * n], refs[2 * n + 1], kinds):
            cp.start()
        refs[-1][...] = jnp.zeros_like(refs[-1])

    return pl.pallas_call(
        body, name=name,
        out_shape=(pltpu.SemaphoreType.DMA((per_array * n,)), pltpu.SemaphoreType.DMA((per_array * n,)),
                   *[_hbm(t) for t in srcs], *[_hbm(t) for t in lands], jax.ShapeDtypeStruct((8, LANES), F32)),
        in_specs=[_IN_HBM] * (2 * n),
        out_specs=(_SEMS, _SEMS, *[_IN_HBM] * (2 * n), pl.BlockSpec(memory_space=pltpu.VMEM)),
        input_output_aliases={a: 2 + a for a in range(2 * n)},
        compiler_params=pltpu.CompilerParams(has_side_effects=_DATAFLOW),
    )(*[pltpu.with_memory_space_constraint(t, pltpu.HBM) for t in list(srcs) + lands])


def _exchange_wait(copies, started, after, kinds, name):
    n = (len(started) - 3) // 2
    bufs = started[2:2 + 2 * n]

    def body(*refs):
        for cp in copies(refs[:n], refs[n:2 * n], refs[2 * n], refs[2 * n + 1], kinds):
            cp.wait_send()
            cp.wait_recv()

    outs = pl.pallas_call(
        body, name=name, out_shape=tuple(_hbm(t) for t in bufs),
        in_specs=[_IN_HBM] * (2 * n) + [_SEMS, _SEMS, ANY], out_specs=tuple([_IN_HBM] * (2 * n)),
        input_output_aliases={a: a for a in range(2 * n)},
        compiler_params=pltpu.CompilerParams(has_side_effects=_DATAFLOW),
    )(*bufs, started[0], started[1], after)
    return outs[:n], outs[n:]


def _add_chips(s1, r2, kcvec, kind, name):
    _, h, C = r2.shape
    tr = _row_tile(h, 512)
    nt = h // tr

    def body(kc_ref, s_ref, r0_ref, r1_ref, r2_ref, o_ref):
        s = s_ref[0] if kind == "rows" else s_ref[...]
        o_ref[...] = ((s.astype(F32) + r0_ref[0].astype(F32)) + r1_ref[0].astype(F32)) + r2_ref[0].astype(F32)

    peer = lambda j: pl.BlockSpec((1, tr, C), lambda i, kc_ref: (j, i, 0))
    if kind == "rows":
        s_spec = pl.BlockSpec((1, tr, C), lambda i, kc_ref: (kc_ref[0], i, 0))
    else:
        s_spec = pl.BlockSpec((tr, C), lambda i, kc_ref: (i, kc_ref[0]))
    return pl.pallas_call(
        body, name=name, out_shape=jax.ShapeDtypeStruct((2 * h, C), F32),
        grid_spec=pltpu.PrefetchScalarGridSpec(
            num_scalar_prefetch=1, grid=(nt,),
            in_specs=[s_spec, peer(0), peer(1), peer(2)],
            out_specs=pl.BlockSpec((tr, C), lambda i, kc_ref: (kc_ref[1] * nt + i, 0))),
        compiler_params=_cparams(("parallel",)),
    )(kcvec, s1, r2, r2, r2)


N_DEV = 8


def _spread_copies(packs, lands, send_sems, recv_sems, kinds):
    x, y, c, _ = _place()
    me = 4 * x + 2 * y + c
    return [pltpu.make_async_remote_copy(
        src_ref=packs[0], dst_ref=lands[0].at[me], send_sem=send_sems.at[mask - 1], recv_sem=recv_sems.at[mask - 1],
        device_id=(1 - x if mask & 4 else x, 1 - y if mask & 2 else y, 1 - c if mask & 1 else c), device_id_type=MESH)
        for mask in range(1, N_DEV)]


def _sum_spread(pack, gathered):
    P = pack.shape[0]

    def body(p_ref, g_ref, o_ref):
        x, y, c, _ = _place()
        me = 4 * x + 2 * y + c
        acc = None
        for i in range(N_DEV):
            term = jnp.where(me == i, p_ref[...], g_ref[i])
            acc = term if acc is None else acc + term
        o_ref[...] = acc

    vmem = pl.BlockSpec(memory_space=pltpu.VMEM)
    return pl.pallas_call(body, name="allreduce_sum", out_shape=jax.ShapeDtypeStruct((P, LANES), F32),
                          in_specs=[vmem, vmem], out_specs=vmem)(pack, gathered)


def _pack_rows(arrs):
    rows = []
    for a in arrs:
        f = a.reshape(-1)
        f = jnp.pad(f, (0, (-f.shape[0]) % (8 * LANES)))
        rows.append(f.reshape(-1, LANES))
    return jnp.concatenate(rows, axis=0)


def _unpack_rows(pack, shapes):
    out, r = [], 0
    for s in shapes:
        n = math.prod(s)
        out.append(pack[r:r + -(-n // LANES)].reshape(-1)[:n].reshape(s))
        r += 8 * -(-n // (8 * LANES))
    return out


_SMALL = ["norm_mix_pre", "ml_head_norm", "b_gate_a", "b_gate_b", "norm_mix_post", "norm_ffn_pre", "norm_ffn_post",
          "conv_b", "b_ml_i", "b_ml_f", "b_fox_f"]
_BIG = ["w_in", "w_branch_a", "w_branch_b", "w_out", "w_up", "w_down"]
_WEIGHTS = ['norm_mix_pre', 'w_in', 'b_ml_i', 'b_ml_f', 'ml_head_norm', 'b_fox_f', 'b_gate_a', 'b_gate_b', 'w_branch_a',
            'w_branch_b', 'w_out', 'norm_mix_post', 'norm_ffn_pre', 'w_up', 'conv_w', 'conv_b', 'w_down', 'norm_ffn_post']


_KINDS = ["rows", "rows", "rows", "rows", "cols", "rows"]


def kernel(x, norm_mix_pre, w_in, b_ml_i, b_ml_f, ml_head_norm, b_fox_f, b_gate_a, b_gate_b, w_branch_a, w_branch_b, w_out, norm_mix_post, norm_ffn_pre, w_up, conv_w, conv_b, w_down, norm_ffn_post, loss_target, m_norm_mix_pre, m_w_in, m_b_ml_i, m_b_ml_f, m_ml_head_norm, m_b_fox_f, m_b_gate_a, m_b_gate_b, m_w_branch_a, m_w_branch_b, m_w_out, m_norm_mix_post, m_norm_ffn_pre, m_w_up, m_conv_w, m_conv_b, m_w_down, m_norm_ffn_post, v_norm_mix_pre, v_w_in, v_b_ml_i, v_b_ml_f, v_ml_head_norm, v_b_fox_f, v_b_gate_a, v_b_gate_b, v_w_branch_a, v_w_branch_b, v_w_out, v_norm_mix_post, v_norm_ffn_pre, v_w_up, v_conv_w, v_conv_b, v_w_down, v_norm_ffn_post):
    args = dict(locals())
    w = {n: args[n] for n in _WEIGHTS}
    mom = {n: args["m_" + n] for n in _WEIGHTS}
    var = {n: args["v_" + n] for n in _WEIGHTS}
    cx, cy, cc = lax.axis_index("x"), lax.axis_index("y"), lax.axis_index("c")
    kme = 2 * cx + cy
    cvec = jnp.reshape(cc, (1,)).astype(jnp.int32)
    kcvec = jnp.stack([kme, cc]).astype(jnp.int32)
    odd = kme % 2

    tr3 = lambda t: jnp.transpose(t, (0, 2, 1))
    w["w_in"], mom["w_in"], var["w_in"] = tr3(w_in), tr3(m_w_in), tr3(v_w_in)
    w_in_main = lax.dynamic_slice_in_dim(w["w_in"][0], 4 * odd, 2048, axis=0).astype(BF16)
    w_in_gates = lax.dynamic_slice_in_dim(w["w_in"][0], 2048 * (1 - odd), 4, axis=0).astype(BF16)
    (wmain_t,), (g_cw, g_gates) = _gather_weights([w_in_main], _KINDS[:1], [w["conv_w"][0], w_in_gates])
    rest_started = _gather_start([w[n][0].astype(BF16) for n in _BIG[1:]], _KINDS[1:], "gather_rest_start")

    relay = {}

    def rest_arrived(after):
        bufs = _gather_wait(rest_started, after, _KINDS[1:], "gather_rest_wait")
        relay["started"] = _inplace_start(_relay_copies, 3, bufs, _KINDS[1:], "gather_rest_relay_start")
        return relay["started"][-1]

    def rest_weights(after):
        g_a, g_b, g_out, wup, g_down = _inplace_wait(_relay_copies, relay["started"], after, _KINDS[1:],
                                                     "gather_rest_relay_wait")
        return full(g_a), full(g_b), full(g_out), wup, full(g_down)
    gate_rows = g_gates.reshape(16, D_MODEL)
    wsmall_t = jnp.zeros((N_SMALL, D_MODEL), BF16)
    for blk, (lo, hi) in enumerate(((0, 4), (4, 8), (8, 16))):
        wsmall_t = wsmall_t.at[blk * LANES:blk * LANES + hi - lo].set(gate_rows[lo:hi])
    full = lambda g: g.reshape(-1, g.shape[2])
    p = {n: w[n] for n in _SMALL}
    p["conv_w"] = jnp.transpose(g_cw, (1, 0, 2)).reshape(3, -1)

    groups = {}

    def on_grads(group, gs):
        names = list(gs)
        kinds = [_KINDS[_BIG.index(n)] for n in names]
        whole = [g if k == "cols" else g.reshape(4, -1, g.shape[1]) for g, k in zip(gs.values(), kinds)]
        started = _exchange_start(_sibling_copies, 1, whole, [_half_shape(g, k) for g, k in zip(whole, kinds)], kinds,
                                  "grads_to_sibling_start_" + group)
        groups[group] = dict(names=names, kinds=kinds, sibling=started)
        return started[-1]

    def advance(group, after):
        G = groups[group]
        whole, got = _exchange_wait(_sibling_copies, G["sibling"], after, G["kinds"], "grads_to_sibling_wait_" + group)
        sums = [_add_halves(g, r, cvec, k, "add_sibling_" + n) for g, r, k, n in zip(whole, got, G["kinds"], G["names"])]
        G["chips"] = _exchange_start(_chip_copies, 3, sums, [_land_shape(s, k) for s, k in zip(sums, G["kinds"])],
                                     G["kinds"], "grads_to_chips_start_" + group)
        return G["chips"][-1]

    loss_row, grad_x, big, small = _local_step(x[0], loss_target[0], full(wmain_t), wsmall_t, rest_arrived, rest_weights,
                                               p, on_grads, advance, rest_started[-1])
    gt = big["wsmall_t"]
    small["w_in_gates"] = jnp.concatenate([gt[0:4], gt[LANES:LANES + 4], gt[2 * LANES:2 * LANES + 8]], axis=0)
    small_names = _SMALL + ["conv_w"]
    packed_names = small_names + ["w_in_gates"]
    pack = _pack_rows([small[n] for n in packed_names] + [loss_row])
    spread = _exchange_start(_spread_copies, N_DEV - 1, [pack], [(N_DEV,) + pack.shape], None, "allreduce_start", zeroed=True)

    def my_half(group, after):
        G = groups[group]
        sums, got = _exchange_wait(_chip_copies, G["chips"], after, G["kinds"], "grads_to_chips_wait_" + group)
        return [_add_chips(s, r, kcvec, k, "add_chips_" + n) for s, r, k, n in zip(sums, got, G["kinds"], G["names"])]

    first_names = groups["ffn"]["names"] + groups["mix"]["names"]
    join_first = _inplace_start(_join_copies, 1, my_half("ffn", spread[-1]) + my_half("mix", spread[-1]), None,
                                "grads_join_start")
    join_in = _inplace_start(_join_copies, 1, my_half("in", join_first[-1]), None, "grads_join_start_in")
    grads = dict(zip(first_names, _inplace_wait(_join_copies, join_first, join_in[-1], None, "grads_join_wait")))

    delta, new_m, new_v = {}, {}, {}
    for n in _BIG[1:]:
        delta[n], new_m[n], new_v[n] = _adamw(w[n], grads[n], mom[n], var[n], "adamw_" + n)
        grads[n] = grads[n][None]
    grads["w_in"], = _inplace_wait(_join_copies, join_in, delta[_BIG[-1]], None, "grads_join_wait_in")

    (pack,), (gathered,) = _exchange_wait(_spread_copies, spread, delta[_BIG[-1]], None, "allreduce_wait")
    full_shapes = [small[n].shape if n in ("conv_w", "w_in_gates") else w[n][0].shape for n in packed_names]
    total = _unpack_rows(_sum_spread(pack, gathered), full_shapes + [loss_row.shape])
    for n, t in zip(packed_names, total):
        grads[n] = t
    loss = total[-1][0, 0]
    grads["conv_w"] = lax.dynamic_slice_in_dim(grads["conv_w"], kme * conv_w.shape[2], conv_w.shape[2], axis=1)
    my_gates = lax.dynamic_slice_in_dim(grads.pop("w_in_gates"), 4 * kme, 4, axis=0)
    g_in = jnp.zeros(w["w_in"].shape[1:], F32)
    g_in = lax.dynamic_update_slice_in_dim(g_in, grads["w_in"], 4 * odd, axis=0)
    grads["w_in"] = lax.dynamic_update_slice_in_dim(g_in, my_gates, 2048 * (1 - odd), axis=0)
    delta["w_in"], new_m["w_in"], new_v["w_in"] = _adamw(w["w_in"], grads["w_in"], mom["w_in"], var["w_in"], "adamw_w_in")
    grads["w_in"] = grads["w_in"][None]
    for d in (grads, delta, new_m, new_v):
        d["w_in"] = tr3(d["w_in"])
    packs = [_pack_rows([d[n][0] for n in small_names]) for d in (w, mom, var)]
    pad = ((0, (-packs[0].shape[0]) % 8), (0, 0))
    packs = [jnp.pad(t, pad)[None] for t in packs]
    gp = jnp.pad(_pack_rows([grads[n] for n in small_names]), pad)
    shapes = [w[n][0].shape for n in small_names]
    for dst, res in zip((delta, new_m, new_v), _adamw(packs[0], gp, packs[1], packs[2], "adamw_small")):
        for n, t in zip(small_names, _unpack_rows(res[0], shapes)):
            dst[n] = t[None]
    for n in small_names:
        grads[n] = grads[n][None]

    return (loss, grad_x[None], *[grads[n] for n in _WEIGHTS], *[delta[n] for n in _WEIGHTS],
            *[new_m[n] for n in _WEIGHTS], *[new_v[n] for n in _WEIGHTS])
```

```python
import functools
import math

import jax
import jax.numpy as jnp
from jax import lax
from jax.experimental import pallas as pl
from jax.experimental.pallas import tpu as pltpu

F32 = jnp.float32
BF16 = jnp.bfloat16
MESH = pl.DeviceIdType.MESH

D_MODEL = 1024
ML_HEADS = 4
ML_DQK = 128
ML_DV = 256
FOX_HEADS = 8
FOX_DH = 128
D_FF = 2816
GATE_CAP = 15.0
EPS = 1e-6
ADAM_LR, ADAM_B1, ADAM_B2, ADAM_EPS, ADAM_WD, ADAM_STEP = 0.001, 0.9, 0.999, 1e-08, 0.01, 10

LANES = 128
MLC = 256
FOX_TQ = 1024
FOX_TQ_FWD = 1024
FOX_TK = 1024
FOX_TK_FWD = 1024
ROW_T = 512
CONV_TC = 1408
VMEM_LIMIT = 56 * 1024 * 1024

C_QM, C_KM, C_VM, C_OM = 0, 512, 1024, 2048
N_ML, N_FOX, N_GATE = 3072, 3072, 2048
N_SMALL = 384


def _cparams(sem=None):
    return pltpu.CompilerParams(dimension_semantics=sem, vmem_limit_bytes=VMEM_LIMIT)


def _tile(n, target):
    if n <= target:
        return n
    best = None
    for t in range(LANES, target + 1, LANES):
        if n % t == 0:
            best = t
    assert best is not None, (n, target)
    return best


def _dot(a, b, dims):
    return lax.dot_general(a, b, (dims, ((), ())), preferred_element_type=F32)


def _dot_nn(a, b):
    return _dot(a, b, ((1,), (0,)))


def _dot_nt(a, b):
    return _dot(a, b, ((1,), (1,)))


def _dot_tn(a, b):
    return _dot(a, b, ((0,), (0,)))


_DOTS = {"nn": _dot_nn, "nt": _dot_nt, "tn": _dot_tn}


def _mm(a, b, mode, out_dtype, name, tm=1024, tn=1408, tk=1408, after=None, b_rows=None):
    a_parts = list(a) if isinstance(a, (list, tuple)) else [a]
    b_parts = list(b) if isinstance(b, (list, tuple)) else [b]
    extra = [] if after is None else [after]
    assert len(a_parts) == 1 or len(b_parts) == 1, name
    a_axes = {"nn": "ik", "nt": "ik", "tn": "ki"}[mode]
    b_axes = {"nn": "kj", "nt": "jk", "tn": "kj"}[mode]
    size, target = {}, dict(i=tm, j=tn, k=tk)
    for parts, axes in ((a_parts, a_axes), (b_parts, b_axes)):
        dims = (parts[0].shape[0], parts[0].shape[1] * len(parts))
        if parts is b_parts and b_rows is not None:
            dims = (b_rows[1], dims[1])
        for ax, n in zip(axes, dims):
            assert size.setdefault(ax, n) == n, (name, ax, n, size)
    tile = {}
    for parts, axes in ((a_parts, a_axes), (b_parts, b_axes)):
        if len(parts) > 1:
            tile[axes[1]] = _tile(parts[0].shape[1], target[axes[1]])
    for ax in "ijk":
        tile.setdefault(ax, _tile(size[ax], target[ax]))
    M, N, nk = size["i"], size["j"], size["k"] // tile["k"]
    grid_pos = dict(i=0, j=1, k=2)
    dot = _DOTS[mode]

    def specs(parts, axes):
        blk = (tile[axes[0]], tile[axes[1]])
        if len(parts) == 1:
            first = 0
            if parts is b_parts and b_rows is not None:
                assert b_rows[0] % blk[0] == 0, (name, b_rows, blk)
                first = b_rows[0] // blk[0]
            return [pl.BlockSpec(blk, lambda *g: (first + g[grid_pos[axes[0]]], g[grid_pos[axes[1]]]))], None
        bpp = parts[0].shape[1] // blk[1]

        def index(p):
            def f(*g):
                g0, g1 = g[grid_pos[axes[0]]], g[grid_pos[axes[1]]]
                on = g1 // bpp == p
                return jnp.where(on, g0, 0), jnp.where(on, g1 % bpp, 0)
            return f

        return [pl.BlockSpec(blk, index(p)) for p in range(len(parts))], (axes[1], bpp)

    a_specs, a_sel = specs(a_parts, a_axes)
    b_specs, b_sel = specs(b_parts, b_axes)
    na, nb = len(a_parts), len(b_parts)

    def body(*refs):
        a_refs, b_refs = refs[:na], refs[na:na + nb]
        o_ref, acc = refs[na + nb + len(extra)], refs[na + nb + len(extra) + 1:]

        def accumulate(part):
            if nk == 1:
                o_ref[...] = part.astype(o_ref.dtype)
                return
            acc_ref, = acc
            k = pl.program_id(2)

            @pl.when(k == 0)
            def _():
                acc_ref[...] = part

            @pl.when(k > 0)
            def _():
                acc_ref[...] += part

            @pl.when(k == nk - 1)
            def _():
                o_ref[...] = acc_ref[...].astype(o_ref.dtype)

        sel = a_sel or b_sel
        if sel is None:
            accumulate(dot(a_refs[0][...], b_refs[0][...]))
        else:
            which = pl.program_id(grid_pos[sel[0]]) // sel[1]
            for p in range(max(na, nb)):
                @pl.when(which == p)
                def _(p=p):
                    accumulate(dot(a_refs[p if a_sel else 0][...], b_refs[p if b_sel else 0][...]))

    return pl.pallas_call(
        body, name=name,
        out_shape=jax.ShapeDtypeStruct((M, N), out_dtype),
        grid=(M // tile["i"], N // tile["j"], nk),
        in_specs=a_specs + b_specs + [pl.BlockSpec(memory_space=pl.ANY)] * len(extra),
        out_specs=pl.BlockSpec((tile["i"], tile["j"]), lambda i, j, k: (i, j)),
        scratch_shapes=[pltpu.VMEM((tile["i"], tile["j"]), F32)] if nk > 1 else [],
        compiler_params=_cparams(("parallel", "parallel", "arbitrary")),
    )(*a_parts, *b_parts, *extra)


def _mm_sum_parts(parts, b, out_dtype, name, trans_b=False, tm=1024, tn=512, after=None):
    M, K = parts[0].shape
    N = b.shape[0] if trans_b else b.shape[1]
    tm, tn = _tile(M, tm), _tile(N, tn)
    n = len(parts)
    extra = [] if after is None else [after]

    def body(*refs):
        b_ref, o_ref = refs[n], refs[n + 1 + len(extra)]
        acc = None
        for p in range(n):
            if trans_b:
                d = _dot_nt(refs[p][...], b_ref[:, p * K:(p + 1) * K])
            else:
                d = _dot_nn(refs[p][...], b_ref[p * K:(p + 1) * K, :])
            acc = d if acc is None else acc + d
        o_ref[...] = acc.astype(o_ref.dtype)

    b_spec = pl.BlockSpec((tn, n * K), lambda i, j: (j, 0)) if trans_b else pl.BlockSpec((n * K, tn), lambda i, j: (0, j))
    return pl.pallas_call(
        body, name=name, out_shape=jax.ShapeDtypeStruct((M, N), out_dtype), grid=(M // tm, N // tn),
        in_specs=[pl.BlockSpec((tm, K), lambda i, j: (i, 0))] * n + [b_spec]
        + [pl.BlockSpec(memory_space=pl.ANY)] * len(extra),
        out_specs=pl.BlockSpec((tm, tn), lambda i, j: (i, j)),
        compiler_params=_cparams(("parallel", "arbitrary")),
    )(*parts, b, *extra)


def _rstd(x):
    return lax.rsqrt(jnp.mean(x * x, axis=-1, keepdims=True) + EPS)


def _rmsnorm_fwd(x, g, name):
    S, D = x.shape
    T = _tile(S, ROW_T)

    def body(x_ref, g_ref, o_ref):
        xv = x_ref[...]
        o_ref[...] = (xv * _rstd(xv) * g_ref[...]).astype(o_ref.dtype)

    return pl.pallas_call(
        body, name=name, out_shape=jax.ShapeDtypeStruct((S, D), BF16), grid=(S // T,),
        in_specs=[pl.BlockSpec((T, D), lambda i: (i, 0)), pl.BlockSpec((1, D), lambda i: (0, 0))],
        out_specs=pl.BlockSpec((T, D), lambda i: (i, 0)),
        compiler_params=_cparams(("parallel",)),
    )(x, g)


def _resid_norm_fwd(x, z, g, g_next, name):
    S, D = x.shape
    T = _tile(S, ROW_T)

    def body(x_ref, z_ref, g_ref, gn_ref, o_ref, h_ref):
        zv = z_ref[...]
        x1 = x_ref[...] + zv * _rstd(zv) * g_ref[...]
        o_ref[...] = x1
        h_ref[...] = (x1 * _rstd(x1) * gn_ref[...]).astype(h_ref.dtype)

    row = pl.BlockSpec((T, D), lambda i: (i, 0))
    vec = pl.BlockSpec((1, D), lambda i: (0, 0))
    return pl.pallas_call(
        body, name=name, out_shape=(jax.ShapeDtypeStruct((S, D), F32), jax.ShapeDtypeStruct((S, D), BF16)),
        grid=(S // T,), in_specs=[row, row, vec, vec], out_specs=(row, row), compiler_params=_cparams(("parallel",)),
    )(x, z, g, g_next)


def _norm_chain_bwd(dh, xin, g, resid, zin, gz, name):
    S, D = xin.shape
    T = _tile(S, ROW_T)

    def body(dh_ref, x_ref, g_ref, r_ref, z_ref, gz_ref, dx_ref, dz_ref, dg_ref, dgz_ref):
        dx, dgt = _rmsnorm_bwd_math(dh_ref[...], x_ref[...], g_ref[...])
        dx = dx + r_ref[...]
        dx_ref[...] = dx
        dz, dgzt = _rmsnorm_bwd_math(dx, z_ref[...], gz_ref[...])
        dz_ref[...] = dz.astype(dz_ref.dtype)

        @pl.when(pl.program_id(0) == 0)
        def _():
            dg_ref[...] = jnp.zeros_like(dg_ref)
            dgz_ref[...] = jnp.zeros_like(dgz_ref)

        dg_ref[...] += jnp.sum(dgt, axis=0, keepdims=True)
        dgz_ref[...] += jnp.sum(dgzt, axis=0, keepdims=True)

    row = pl.BlockSpec((T, D), lambda i: (i, 0))
    vec = pl.BlockSpec((1, D), lambda i: (0, 0))
    v1 = jax.ShapeDtypeStruct((1, D), F32)
    return pl.pallas_call(
        body, name=name,
        out_shape=(jax.ShapeDtypeStruct((S, D), F32), jax.ShapeDtypeStruct((S, D), BF16), v1, v1),
        grid=(S // T,), in_specs=[row, row, vec, row, row, vec], out_specs=(row, row, vec, vec),
        compiler_params=_cparams(("arbitrary",)),
    )(dh, xin, g, resid, zin, gz)


def _rmsnorm_bwd_math(dy, xv, g):
    r = _rstd(xv)
    u = dy * g
    dx = r * u - xv * (r * r * r) * jnp.mean(u * xv, axis=-1, keepdims=True)
    return dx, dy * xv * r


def _rmsnorm_bwd(dys, xin, g, resid, out_dtype, name):
    S, D = xin.shape
    T = _tile(S, ROW_T)
    has_resid = resid is not None
    ndy = len(dys)

    def body(*refs):
        dy_refs, (x_ref, g_ref) = refs[:ndy], refs[ndy:ndy + 2]
        dx_ref, dg_ref = refs[-2:]
        dy = dy_refs[0][...]
        for r in dy_refs[1:]:
            dy = dy + r[...]
        dx, dgt = _rmsnorm_bwd_math(dy, x_ref[...], g_ref[...])
        if has_resid:
            dx = dx + refs[ndy + 2][...]
        dx_ref[...] = dx.astype(dx_ref.dtype)

        @pl.when(pl.program_id(0) == 0)
        def _():
            dg_ref[...] = jnp.zeros_like(dg_ref)

        dg_ref[...] += jnp.sum(dgt, axis=0, keepdims=True)

    row = pl.BlockSpec((T, D), lambda i: (i, 0))
    vec = pl.BlockSpec((1, D), lambda i: (0, 0))
    ins = list(dys) + [xin, g] + ([resid] if has_resid else [])
    return pl.pallas_call(
        body, name=name,
        out_shape=(jax.ShapeDtypeStruct((S, D), out_dtype), jax.ShapeDtypeStruct((1, D), F32)),
        grid=(S // T,), in_specs=[row] * ndy + [row, vec] + ([row] if has_resid else []),
        out_specs=(row, vec), compiler_params=_cparams(("arbitrary",)),
    )(*ins)


def _loss_head(x1, d, g, target, name):
    S, D = x1.shape
    T = _tile(S, ROW_T)

    def body(x_ref, d_ref, g_ref, t_ref, loss_ref, dy_ref, dd_ref, dg_ref):
        dv, gv = d_ref[...], g_ref[...]
        y = x_ref[...] + dv * _rstd(dv) * gv
        diff = y - t_ref[...]
        dy = diff * (1.0 / D)
        dy_ref[...] = dy
        dd, dgt = _rmsnorm_bwd_math(dy, dv, gv)
        dd_ref[...] = dd.astype(dd_ref.dtype)

        @pl.when(pl.program_id(0) == 0)
        def _():
            dg_ref[...] = jnp.zeros_like(dg_ref)
            loss_ref[...] = jnp.zeros_like(loss_ref)

        dg_ref[...] += jnp.sum(dgt, axis=0, keepdims=True)
        part = jnp.sum(jnp.sum(diff * diff, axis=1, keepdims=True), axis=0, keepdims=True)
        loss_ref[...] += (0.5 / D) * part

    row = pl.BlockSpec((T, D), lambda i: (i, 0))
    vec = pl.BlockSpec((1, D), lambda i: (0, 0))
    return pl.pallas_call(
        body, name=name,
        out_shape=(jax.ShapeDtypeStruct((1, LANES), F32), jax.ShapeDtypeStruct((S, D), F32),
                   jax.ShapeDtypeStruct((S, D), BF16), jax.ShapeDtypeStruct((1, D), F32)),
        grid=(S // T,), in_specs=[row, row, vec, row],
        out_specs=(pl.BlockSpec((1, LANES), lambda i: (0, 0)), row, row, vec),
        compiler_params=_cparams(("arbitrary",)),
    )(x1, d, g, target)


def _merge_fwd(ya, yb, pm, ba, bb, name):
    S, D = ya.shape
    T = _tile(S, ROW_T)

    def body(ya_ref, yb_ref, ga_ref, gb_ref, ba_ref, bb_ref, o_ref):
        sa = jax.nn.sigmoid(ga_ref[...] + ba_ref[...])
        sb = jax.nn.sigmoid(gb_ref[...] + bb_ref[...])
        o_ref[...] = (sa * ya_ref[...] + sb * yb_ref[...]).astype(o_ref.dtype)

    row = pl.BlockSpec((T, D), lambda i: (i, 0))
    vec = pl.BlockSpec((1, D), lambda i: (0, 0))
    return pl.pallas_call(
        body, name=name, out_shape=jax.ShapeDtypeStruct((S, D), BF16), grid=(S // T,),
        in_specs=[row, row, pl.BlockSpec((T, D), lambda i: (i, 0)),
                  pl.BlockSpec((T, D), lambda i: (i, 1)), vec, vec],
        out_specs=row, compiler_params=_cparams(("parallel",)),
    )(ya, yb, pm, pm, ba, bb)


def _merge_bwd(dmerged, ya, yb, pm, ba, bb, name):
    S, D = ya.shape
    T = _tile(S, ROW_T)

    def body(dm_ref, ya_ref, yb_ref, ga_ref, gb_ref, ba_ref, bb_ref,
             dya_ref, dyb_ref, dga_ref, dgb_ref, dba_ref, dbb_ref):
        dm = dm_ref[...]
        sa = jax.nn.sigmoid(ga_ref[...] + ba_ref[...])
        sb = jax.nn.sigmoid(gb_ref[...] + bb_ref[...])
        dya_ref[...] = (dm * sa).astype(dya_ref.dtype)
        dyb_ref[...] = (dm * sb).astype(dyb_ref.dtype)
        dga = dm * ya_ref[...] * sa * (1.0 - sa)
        dgb = dm * yb_ref[...] * sb * (1.0 - sb)
        dga_ref[...] = dga.astype(dga_ref.dtype)
        dgb_ref[...] = dgb.astype(dgb_ref.dtype)

        @pl.when(pl.program_id(0) == 0)
        def _():
            dba_ref[...] = jnp.zeros_like(dba_ref)
            dbb_ref[...] = jnp.zeros_like(dbb_ref)

        dba_ref[...] += jnp.sum(dga, axis=0, keepdims=True)
        dbb_ref[...] += jnp.sum(dgb, axis=0, keepdims=True)

    row = pl.BlockSpec((T, D), lambda i: (i, 0))
    vec = pl.BlockSpec((1, D), lambda i: (0, 0))
    act = jax.ShapeDtypeStruct((S, D), BF16)
    v1 = jax.ShapeDtypeStruct((1, D), F32)
    return pl.pallas_call(
        body, name=name, out_shape=(act, act, act, act, v1, v1), grid=(S // T,),
        in_specs=[row, row, row, pl.BlockSpec((T, D), lambda i: (i, 0)),
                  pl.BlockSpec((T, D), lambda i: (i, 1)), vec, vec],
        out_specs=(row, row, row, row, vec, vec), compiler_params=_cparams(("arbitrary",)),
    )(dmerged, ya, yb, pm, pm, ba, bb)


_GELU_C = math.sqrt(2.0 / math.pi)


_GELU_K = 0.044715


def _gelu(g):
    u = 0.5 * jnp.tanh(g * (_GELU_C + (_GELU_C * _GELU_K) * (g * g))) + 0.5
    return g * u, u


def _gelu_grad(g, u):
    return u * (1.0 + g * (1.0 - u) * (2 * _GELU_C + (6 * _GELU_C * _GELU_K) * (g * g)))


def _shift_down(v, halo_ref, first, rows):
    T = v.shape[0]
    keep = jnp.where(first, 0.0, 1.0)
    h7 = halo_ref[7:8, :] * keep
    h6 = halo_ref[6:7, :] * keep
    m1 = jnp.where(rows == 0, h7, pltpu.roll(v, 1, 0))
    m2 = jnp.where(rows == 0, h6, jnp.where(rows == 1, h7, pltpu.roll(v, 2, 0)))
    return m1, m2


def _conv_act_fwd(up, cw, cb, name):
    S, F2 = up.shape
    Fh = F2 // 2
    T = _tile(S, ROW_T)
    tc = _tile(Fh, CONV_TC)
    ncol = Fh // tc
    hb = T // 8

    def body(ua_ref, ug_ref, ha_ref, hg_ref, wa_ref, wg_ref, ba_ref, bg_ref, o_ref, a_ref, g_ref):
        first = pl.program_id(0) == 0
        rows = lax.broadcasted_iota(jnp.int32, (T, tc), 0)

        def conv(u_ref, h_ref, w_ref, b_ref):
            v = u_ref[...]
            m1, m2 = _shift_down(v, h_ref, first, rows)
            return b_ref[...] + w_ref[0:1, :] * m2 + w_ref[1:2, :] * m1 + w_ref[2:3, :] * v

        a = conv(ua_ref, ha_ref, wa_ref, ba_ref)
        g = conv(ug_ref, hg_ref, wg_ref, bg_ref)
        a_ref[...] = a
        g_ref[...] = g
        o_ref[...] = (_gelu(g)[0] * a).astype(o_ref.dtype)

    halo = lambda off: pl.BlockSpec((8, tc), lambda i, j: (jnp.maximum(i * hb - 1, 0), j + off))
    blk = pl.BlockSpec((T, tc), lambda i, j: (i, j))
    f32 = jax.ShapeDtypeStruct((S, Fh), F32)
    return pl.pallas_call(
        body, name=name, out_shape=(jax.ShapeDtypeStruct((S, Fh), BF16), f32, f32), grid=(S // T, ncol),
        in_specs=[blk, pl.BlockSpec((T, tc), lambda i, j: (i, j + ncol)),
                  halo(0), halo(ncol),
                  pl.BlockSpec((3, tc), lambda i, j: (0, j)), pl.BlockSpec((3, tc), lambda i, j: (0, j + ncol)),
                  pl.BlockSpec((1, tc), lambda i, j: (0, j)), pl.BlockSpec((1, tc), lambda i, j: (0, j + ncol))],
        out_specs=(blk, blk, blk),
        compiler_params=_cparams(("parallel", "parallel")),
    )(up, up, up, up, cw, cw, cb, cb)


def _conv_act_bwd(up, a, g, dact, cw, name):
    S, F2 = up.shape
    Fh = F2 // 2
    T = _tile(S, ROW_T)
    tc = _tile(Fh, CONV_TC)
    ncol, nrow, hb, nhb = Fh // tc, S // T, T // 8, S // 8

    def body(ua_ref, ug_ref, a_ref, g_ref, an_ref, gn_ref, wa_ref, wg_ref, da_ref, dn_ref,
             dpa_ref, dpg_ref, dwa_ref, dwg_ref, dba_ref, dbg_ref, dua_n, dug_n):
        i = pl.program_id(1)
        rows = lax.broadcasted_iota(jnp.int32, (T, tc), 0)

        def du_of(a, g, dact_v):
            gel, t = _gelu(g)
            return dact_v * gel, dact_v * a * _gelu_grad(g, t)

        dua, dug = du_of(a_ref[...], g_ref[...], da_ref[...])
        keep = jnp.where(i == nrow - 1, 0.0, 1.0)
        dua_n[...], dug_n[...] = du_of(an_ref[...], gn_ref[...], dn_ref[...] * keep)

        @pl.when(i == 0)
        def _():
            for r in (dwa_ref, dwg_ref, dba_ref, dbg_ref):
                r[...] = jnp.zeros_like(r)

        for du, n_ref, u_ref, w_ref, o_ref, dw_ref, db_ref in ((dua, dua_n, ua_ref, wa_ref, dpa_ref, dwa_ref, dba_ref),
                                                               (dug, dug_n, ug_ref, wg_ref, dpg_ref, dwg_ref, dbg_ref)):
            n0, n1 = n_ref[0:1, :], n_ref[1:2, :]
            du1 = jnp.where(rows == T - 1, n0, pltpu.roll(du, T - 1, 0))
            du2 = jnp.where(rows == T - 2, n0, jnp.where(rows == T - 1, n1, pltpu.roll(du, T - 2, 0)))
            o_ref[...] = (w_ref[2:3, :] * du + w_ref[1:2, :] * du1 + w_ref[0:1, :] * du2).astype(o_ref.dtype)
            u = u_ref[...]
            db_ref[...] += jnp.sum(du, axis=0, keepdims=True)
            for j, d in enumerate((du2, du1, du)):
                dw_ref[j:j + 1, :] += jnp.sum(d * u, axis=0, keepdims=True)

    tile = lambda off: pl.BlockSpec((T, tc), lambda j, i: (i, j + off))
    under = pl.BlockSpec((8, tc), lambda j, i: (jnp.minimum((i + 1) * hb, nhb - 1), j))
    vec = lambda n, off: pl.BlockSpec((n, tc), lambda j, i: (0, j + off))
    act = jax.ShapeDtypeStruct((S, Fh), BF16)
    return pl.pallas_call(
        body, name=name,
        out_shape=(act, act, jax.ShapeDtypeStruct((3, Fh), F32), jax.ShapeDtypeStruct((3, Fh), F32),
                   jax.ShapeDtypeStruct((1, Fh), F32), jax.ShapeDtypeStruct((1, Fh), F32)),
        grid=(ncol, nrow),
        in_specs=[tile(0), tile(ncol), tile(0), tile(0), under, under, vec(3, 0), vec(3, ncol), tile(0), under],
        out_specs=(tile(0), tile(0), vec(3, 0), vec(3, 0), vec(1, 0), vec(1, 0)),
        scratch_shapes=[pltpu.VMEM((8, tc), F32), pltpu.VMEM((8, tc), F32)],
        compiler_params=_cparams(("parallel", "arbitrary")),
    )(up, up, a, g, a, g, cw, cw, dact, dact)


def _split3(x):
    hi = x.astype(BF16)
    r1 = x - hi.astype(F32)
    mid = r1.astype(BF16)
    lo = (r1 - mid.astype(F32)).astype(BF16)
    return hi, mid, lo


def _tri_dot(tri, x):
    hi, mid, lo = _split3(x)
    return _dot_nn(tri, hi) + _dot_nn(tri, mid) + _dot_nn(tri, lo)


def _log_sigmoid(x):
    return jnp.minimum(x, 0.0) - jnp.log(1.0 + jnp.exp(-jnp.abs(x)))


def _tri_mask(n, lower):
    r = lax.broadcasted_iota(jnp.int32, (n, n), 0)
    c = lax.broadcasted_iota(jnp.int32, (n, n), 1)
    return (r >= c) if lower else (r <= c)


def _gates_fwd(ps, bi, bf, bff, name):
    S = ps.shape[0]
    NC = S // MLC

    def body(ps_ref, bi_ref, bf_ref, bff_ref, a_ref, A_ref, wi_ref, em_ref, wk_ref, dec_ref, F_ref, m_scr, f_scr):
        @pl.when(pl.program_id(0) == 0)
        def _():
            m_scr[...] = jnp.zeros_like(m_scr)
            f_scr[...] = jnp.zeros_like(f_scr)

        rows = lax.broadcasted_iota(jnp.int32, (MLC, LANES), 0)
        ltri = _tri_mask(MLC, True).astype(BF16)
        li = GATE_CAP * jnp.tanh((ps_ref[:, 0:LANES] + bi_ref[...]) / GATE_CAP)
        lf = _log_sigmoid(GATE_CAP * jnp.tanh((ps_ref[:, LANES:2 * LANES] + bf_ref[...]) / GATE_CAP))
        b = _tri_dot(ltri, lf)
        a = li - b
        cm = a
        sh = 1
        while sh < MLC:
            cm = jnp.where(rows >= sh, jnp.maximum(cm, pltpu.roll(cm, sh, 0)), cm)
            sh *= 2
        m0 = m_scr[...]
        A = jnp.maximum(cm, m0)
        a_ref[...] = a
        A_ref[...] = A
        A_last = A_ref[MLC - 1:MLC, :]
        wi_ref[...] = jnp.exp(m0 - A)
        em_ref[...] = jnp.exp(-(b + A))
        wk_ref[...] = jnp.exp(a - A_last)
        dec_ref[0] = jnp.exp(m0 - A_last)
        F_ref[...] = b
        m_scr[...] = F_ref[MLC - 1:MLC, :] + A_last
        lfg = _log_sigmoid(ps_ref[:, 2 * LANES:3 * LANES] + bff_ref[...])
        F_ref[...] = _tri_dot(ltri, lfg) + f_scr[...]
        f_scr[...] = F_ref[MLC - 1:MLC, :]

    col = pl.BlockSpec((MLC, LANES), lambda c: (c, 0))
    vec = pl.BlockSpec((1, LANES), lambda c: (0, 0))
    cs = jax.ShapeDtypeStruct((S, LANES), F32)
    return pl.pallas_call(
        body, name=name,
        out_shape=(cs, cs, cs, cs, cs, jax.ShapeDtypeStruct((NC, 1, LANES), F32), cs),
        grid=(NC,), in_specs=[pl.BlockSpec((MLC, N_SMALL), lambda c: (c, 0)), vec, vec, vec],
        out_specs=(col, col, col, col, col, pl.BlockSpec((1, 1, LANES), lambda c: (c, 0, 0)), col),
        scratch_shapes=[pltpu.VMEM((1, LANES), F32), pltpu.VMEM((1, LANES), F32)],
        compiler_params=_cparams(("arbitrary",)),
    )(ps, bi, bf, bff)


def _gates_bwd(ps, bi, bf, bff, rk, kc, tch, dF, name):
    S = ps.shape[0]
    NC = S // MLC

    def body(ps_ref, bi_ref, bf_ref, bff_ref, rk_ref, kc_ref, t_ref, dF_ref, dps_ref, db_ref, carry):
        @pl.when(pl.program_id(0) == 0)
        def _():
            carry[...] = jnp.zeros_like(carry)
            db_ref[...] = jnp.zeros_like(db_ref)

        lanes = lax.broadcasted_iota(jnp.int32, (MLC, LANES), 1)
        utri = _tri_mask(MLC, False).astype(BF16)
        ti = jnp.tanh((ps_ref[:, 0:LANES] + bi_ref[...]) / GATE_CAP)
        t_end, t_start = t_ref[0, 0:1, :], t_ref[0, 1:2, :]
        rk = rk_ref[...]
        rk = rk - (jnp.sum(rk, axis=0, keepdims=True) - (t_start - t_end)) * (1.0 / MLC)
        dpi = jnp.where(lanes < ML_HEADS, (kc_ref[...] - rk) * (1.0 - ti * ti), 0.0)
        tf = jnp.tanh((ps_ref[:, LANES:2 * LANES] + bf_ref[...]) / GATE_CAP)
        dlf = _tri_dot(utri, rk) + t_end
        dpf = jnp.where(lanes < ML_HEADS, dlf * jax.nn.sigmoid(-GATE_CAP * tf) * (1.0 - tf * tf), 0.0)
        dFv = dF_ref[...]
        dlfg = _tri_dot(utri, dFv) + carry[...]
        carry[...] += jnp.sum(dFv, axis=0, keepdims=True)
        dpff = jnp.where(lanes < FOX_HEADS, dlfg * jax.nn.sigmoid(-(ps_ref[:, 2 * LANES:3 * LANES] + bff_ref[...])), 0.0)
        for n, dp in enumerate((dpi, dpf, dpff)):
            dps_ref[:, n * LANES:(n + 1) * LANES] = dp.astype(dps_ref.dtype)
            db_ref[:, n * LANES:(n + 1) * LANES] += jnp.sum(dp, axis=0, keepdims=True)

    rev = lambda c: (NC - 1 - c, 0)
    col = pl.BlockSpec((MLC, LANES), rev)
    vec = pl.BlockSpec((1, LANES), lambda c: (0, 0))
    wide = pl.BlockSpec((MLC, N_SMALL), rev)
    return pl.pallas_call(
        body, name=name,
        out_shape=(jax.ShapeDtypeStruct((S, N_SMALL), BF16), jax.ShapeDtypeStruct((1, N_SMALL), F32)),
        grid=(NC,),
        in_specs=[wide, vec, vec, vec, col, col, pl.BlockSpec((1, 2, LANES), lambda c: (NC - 1 - c, 0, 0)), col],
        out_specs=(wide, pl.BlockSpec((1, N_SMALL), lambda c: (0, 0))),
        scratch_shapes=[pltpu.VMEM((1, LANES), F32)],
        compiler_params=_cparams(("arbitrary",)),
    )(ps, bi, bf, bff, rk, kc, tch, dF)


_ML_SCALE = ML_DQK ** -0.5


def _ml_specs(rev, NC):
    idx = (lambda c: NC - 1 - c) if rev else (lambda c: c)
    qk = lambda blk: pl.BlockSpec((MLC, ML_HEADS * ML_DQK), lambda c: (idx(c), blk))
    wide = lambda blk: pl.BlockSpec((MLC, D_MODEL), lambda c: (idx(c), blk))
    col = pl.BlockSpec((MLC, LANES), lambda c: (idx(c), 0))
    return idx, qk, wide, col


def _ml_intra(q_ref, k_ref, arow_ref, A_ref, h):
    hs = slice(h * ML_DQK, (h + 1) * ML_DQK)
    qf = q_ref[:, hs] * _ML_SCALE
    kf = k_ref[:, hs]
    qb, kb = qf.astype(BF16), kf.astype(BF16)
    qk = _dot_nt(qb, kb)
    logw = arow_ref[h:h + 1, :] - A_ref[:, h:h + 1]
    W = jnp.exp(jnp.where(_tri_mask(MLC, True), logw, -1e30))
    return qb, kb, qf, kf, qk, W


def _mlstm_fwd(pm, a_row, A, wi, em, wk, dec, w_hn, name):
    S = pm.shape[0]
    NC = S // MLC
    _, qk, wide, col = _ml_specs(False, NC)

    def body(q_ref, k_ref, v_ref, o_ref, arow_ref, A_ref, wi_ref, em_ref, wk_ref, dec_ref, whn_ref,
             ha_ref, hp_ref, den_ref, cst_ref, nst_ref, C_scr, n_scr):
        @pl.when(pl.program_id(0) == 0)
        def _():
            C_scr[...] = jnp.zeros_like(C_scr)
            n_scr[...] = jnp.zeros_like(n_scr)

        lanes = lax.broadcasted_iota(jnp.int32, (MLC, LANES), 1)
        den_tile = jnp.zeros((MLC, LANES), F32)
        for h in range(ML_HEADS):
            vs = slice(h * ML_DV, (h + 1) * ML_DV)
            qb, kb, qf, kf, qk_, W = _ml_intra(q_ref, k_ref, arow_ref, A_ref, h)
            vb = v_ref[:, vs].astype(BF16)
            Cf = C_scr[h]
            Cb = Cf.astype(BF16)
            nrow = n_scr[h]
            cst_ref[0, h] = Cb
            nst_ref[0, h] = nrow
            s = qk_ * W
            wic = wi_ref[:, h:h + 1]
            num = _dot_nn(s.astype(BF16), vb) + wic * _dot_nt(qb, Cb)
            den = jnp.sum(s, axis=1, keepdims=True) + wic * jnp.sum(qf * nrow, axis=1, keepdims=True)
            hp = num / jnp.maximum(jnp.abs(den), em_ref[:, h:h + 1])
            hp_ref[:, vs] = hp
            den_tile = jnp.where(lanes == h, den, den_tile)
            hn = hp * _rstd(hp) * whn_ref[:, vs]
            ha_ref[:, vs] = (hn * jax.nn.sigmoid(o_ref[:, vs])).astype(ha_ref.dtype)
            wkc = wk_ref[:, h:h + 1]
            kw = kf * wkc
            d = dec_ref[0, :, h:h + 1]
            C_scr[h] = d * Cf + _dot_tn(vb, kw.astype(BF16))
            n_scr[h] = d * nrow + jnp.sum(kw, axis=0, keepdims=True)
        den_ref[...] = den_tile

    return pl.pallas_call(
        body, name=name,
        out_shape=(jax.ShapeDtypeStruct((S, D_MODEL), BF16), jax.ShapeDtypeStruct((S, D_MODEL), F32),
                   jax.ShapeDtypeStruct((S, LANES), F32),
                   jax.ShapeDtypeStruct((NC, ML_HEADS, ML_DV, ML_DQK), BF16),
                   jax.ShapeDtypeStruct((NC, ML_HEADS, 1, ML_DQK), F32)),
        grid=(NC,),
        in_specs=[qk(C_QM // 512), qk(C_KM // 512), wide(C_VM // D_MODEL), wide(C_OM // D_MODEL),
                  pl.BlockSpec((8, MLC), lambda c: (0, c)), col, col, col, col,
                  pl.BlockSpec((1, 1, LANES), lambda c: (c, 0, 0)), pl.BlockSpec((1, D_MODEL), lambda c: (0, 0))],
        out_specs=(pl.BlockSpec((MLC, D_MODEL), lambda c: (c, 0)), pl.BlockSpec((MLC, D_MODEL), lambda c: (c, 0)),
                   col, pl.BlockSpec((1, ML_HEADS, ML_DV, ML_DQK), lambda c: (c, 0, 0, 0)),
                   pl.BlockSpec((1, ML_HEADS, 1, ML_DQK), lambda c: (c, 0, 0, 0))),
        scratch_shapes=[pltpu.VMEM((ML_HEADS, ML_DV, ML_DQK), F32), pltpu.VMEM((ML_HEADS, 1, ML_DQK), F32)],
        compiler_params=_cparams(("arbitrary",)),
    )(pm, pm, pm, pm, a_row, A, wi, em, wk, dec, w_hn)


def _mlstm_bwd(dha, pm, hp_all, den_all, a_row, A, wi, em, wk, dec, cst, nst, w_hn, name):
    S = pm.shape[0]
    NC = S // MLC
    idx, qk, wide, col = _ml_specs(True, NC)

    def body(dha_ref, q_ref, k_ref, v_ref, o_ref, hp_ref, den_ref, arow_ref, A_ref, wi_ref, em_ref, wk_ref,
             dec_ref, cst_ref, nst_ref, whn_ref,
             dqk_ref, dv_ref, do_ref, rk_ref, kc_ref, t_ref, dwhn_ref, dC_scr, dn_scr, t_scr):
        @pl.when(pl.program_id(0) == 0)
        def _():
            dC_scr[...] = jnp.zeros_like(dC_scr)
            dn_scr[...] = jnp.zeros_like(dn_scr)
            t_scr[...] = jnp.zeros_like(t_scr)
            dwhn_ref[...] = jnp.zeros_like(dwhn_ref)

        lanes = lax.broadcasted_iota(jnp.int32, (MLC, LANES), 1)
        lane1 = lax.broadcasted_iota(jnp.int32, (1, LANES), 1)
        t_ref[0, 0:1, :] = t_scr[...]
        rk_tile = jnp.zeros((MLC, LANES), F32)
        kc_tile = jnp.zeros((MLC, LANES), F32)
        t_new = jnp.zeros((1, LANES), F32)
        for h in range(ML_HEADS):
            hs = slice(h * ML_DQK, (h + 1) * ML_DQK)
            vs = slice(h * ML_DV, (h + 1) * ML_DV)
            hp = hp_ref[:, vs]
            sig = jax.nn.sigmoid(o_ref[:, vs])
            whn = whn_ref[:, vs]
            r = _rstd(hp)
            dga = dha_ref[:, vs]
            do_ref[:, vs] = (dga * (hp * r * whn) * sig * (1.0 - sig)).astype(do_ref.dtype)
            dhn = dga * sig
            dhp, dwt = _rmsnorm_bwd_math(dhn, hp, whn)
            dwhn_ref[:, vs] += jnp.sum(dwt, axis=0, keepdims=True)
            den = den_ref[:, h:h + 1]
            floor = em_ref[:, h:h + 1]
            D = jnp.maximum(jnp.abs(den), floor)
            dnum = dhp / D
            dh_h = jnp.sum(dhp * hp, axis=1, keepdims=True)
            active = jnp.abs(den) >= floor
            dden = -dh_h / D * jnp.where(active, jnp.sign(den), 0.0)
            phi = jnp.where(active, 0.0, dh_h)
            qb, kb, qf, kf, qk_, W = _ml_intra(q_ref, k_ref, arow_ref, A_ref, h)
            vf = v_ref[:, vs]
            vb = vf.astype(BF16)
            Cb = cst_ref[0, h]
            nrow = nst_ref[0, h]
            wic = wi_ref[:, h:h + 1]
            wkc = wk_ref[:, h:h + 1]
            d = dec_ref[0, :, h:h + 1]
            dCn = dC_scr[h]
            dCb = dCn.astype(BF16)
            dnn = dn_scr[h]
            dnumb = dnum.astype(BF16)
            s = qk_ * W
            ds = (_dot_nt(dnumb, vb) + dden) * W
            dsb = ds.astype(BF16)
            dnw = (wic * dnum).astype(BF16)
            wd = wic * dden
            kw = kf * wkc
            dv_state = _dot_nt(kw.astype(BF16), dCb)
            dq = _dot_nn(dsb, kb) + _dot_nn(dnw, Cb) + wd * nrow
            dk_state = wkc * (_dot_nn(vb, dCb) + dnn)
            dk = _dot_tn(dsb, qb) + dk_state
            dv = _dot_tn(s.astype(BF16), dnumb) + dv_state
            dC = d * dCn + _dot_tn(dnw, qb)
            dn = d * dnn + jnp.sum(wd * qf, axis=0, keepdims=True)
            dC_scr[h] = dC
            dn_scr[h] = dn
            dqk_ref[:, hs] = (dq * _ML_SCALE).astype(dqk_ref.dtype)
            dqk_ref[:, C_KM + h * ML_DQK:C_KM + (h + 1) * ML_DQK] = dk.astype(dqk_ref.dtype)
            dv_ref[:, vs] = dv.astype(dv_ref.dtype)
            G = ds * qk_
            inter = _dot_nt(qb, Cb)
            qn = jnp.sum(qf * nrow, axis=1, keepdims=True)
            R = (jnp.sum(G, axis=1, keepdims=True)
                 + wic * (jnp.sum(dnum * inter, axis=1, keepdims=True) + dden * qn))
            K = jnp.sum(G.T, axis=1, keepdims=True) + jnp.sum(kf * dk_state, axis=1, keepdims=True)
            rk_tile = jnp.where(lanes == h, R - K, rk_tile)
            kc_tile = jnp.where(lanes == h, phi, kc_tile)
            tt = (jnp.sum(jnp.sum(dC * Cb.astype(F32), axis=1, keepdims=True), axis=0, keepdims=True)
                  + jnp.sum(dn * nrow, axis=1, keepdims=True))
            t_new = jnp.where(lane1 == h, tt, t_new)
        rk_ref[...] = rk_tile
        kc_ref[...] = kc_tile
        t_ref[0, 1:2, :] = t_new
        t_scr[...] = t_new

    act = lambda n: jax.ShapeDtypeStruct((S, n), BF16)
    cs = jax.ShapeDtypeStruct((S, LANES), F32)
    rowblk = lambda n: pl.BlockSpec((MLC, n), lambda c: (idx(c), 0))
    return pl.pallas_call(
        body, name=name,
        out_shape=(act(D_MODEL), act(D_MODEL), act(D_MODEL), cs, cs,
                   jax.ShapeDtypeStruct((NC, 2, LANES), F32), jax.ShapeDtypeStruct((1, D_MODEL), F32)),
        grid=(NC,),
        in_specs=[rowblk(D_MODEL), qk(C_QM // 512), qk(C_KM // 512), wide(C_VM // D_MODEL), wide(C_OM // D_MODEL),
                  rowblk(D_MODEL), col, pl.BlockSpec((8, MLC), lambda c: (0, idx(c))), col, col, col, col,
                  pl.BlockSpec((1, 1, LANES), lambda c: (idx(c), 0, 0)),
                  pl.BlockSpec((1, ML_HEADS, ML_DV, ML_DQK), lambda c: (idx(c), 0, 0, 0)),
                  pl.BlockSpec((1, ML_HEADS, 1, ML_DQK), lambda c: (idx(c), 0, 0, 0)),
                  pl.BlockSpec((1, D_MODEL), lambda c: (0, 0))],
        out_specs=(rowblk(D_MODEL), rowblk(D_MODEL), rowblk(D_MODEL), col, col,
                   pl.BlockSpec((1, 2, LANES), lambda c: (idx(c), 0, 0)), pl.BlockSpec((1, D_MODEL), lambda c: (0, 0))),
        scratch_shapes=[pltpu.VMEM((ML_HEADS, ML_DV, ML_DQK), F32), pltpu.VMEM((ML_HEADS, 1, ML_DQK), F32),
                        pltpu.VMEM((1, LANES), F32)],
        compiler_params=_cparams(("arbitrary",)),
    )(dha, pm, pm, pm, pm, hp_all, den_all, a_row, A, wi, em, wk, dec, cst, nst, w_hn)


_FOX_SCALE = FOX_DH ** -0.5
_NEG = -1e30
_LOG2E = 1.4426950408889634
_LN2 = 0.6931471805599453
_QF_BLK, _KF_BLK, _VF_BLK = 0, FOX_HEADS, 2 * FOX_HEADS


def _lane_pick(tile, lane):
    lanes = lax.broadcasted_iota(jnp.int32, tile.shape, 1)
    return jnp.sum(jnp.where(lanes == lane, tile, 0.0), axis=1, keepdims=True)


def _col_to_row(col):
    return jnp.max(jnp.broadcast_to(col, (col.shape[0], LANES)).T, axis=0, keepdims=True)


def _causal(q0, k0, shape, q_axis):
    qpos = q0 + lax.broadcasted_iota(jnp.int32, shape, q_axis)
    kpos = k0 + lax.broadcasted_iota(jnp.int32, shape, 1 - q_axis)
    return kpos <= qpos


def _fox_fwd(pf, fc, fk_row, name):
    S = pf.shape[0]
    TQ, TK = FOX_TQ_FWD, FOX_TK_FWD
    nq, nk = S // TQ, S // TK
    c1 = _FOX_SCALE * _LOG2E

    def body(q_ref, k_ref, v_ref, fc_ref, fr_ref, o_ref, lse_ref):
        h, i = pl.program_id(0), pl.program_id(1)
        qb = q_ref[...]
        fq2 = _lane_pick(fc_ref[...], h) * _LOG2E

        def step(j, carry, masked):
            m, l, acc = carry
            off = pl.multiple_of(j * TK, TK)
            t = _dot_nt(qb, k_ref[pl.ds(off, TK), :]) * c1 - fr_ref[0, j] * _LOG2E
            if masked:
                t = jnp.where(_causal(i * TQ, j * TK, (TQ, TK), 0), t, _NEG)
            m_new = jnp.maximum(m, jnp.max(t, axis=1, keepdims=True) + fq2)
            alpha = jnp.exp2(m - m_new)
            p = jnp.exp2(t + (fq2 - m_new))
            l = alpha * l + jnp.sum(p, axis=1, keepdims=True)
            acc = alpha * acc + _dot_nn(p.astype(BF16), v_ref[pl.ds(off, TK), :])
            return m_new, l, acc

        init = (jnp.full((TQ, 1), _NEG, F32), jnp.zeros((TQ, 1), F32), jnp.zeros((TQ, FOX_DH), F32))
        last = (i * TQ) // TK
        carry = lax.fori_loop(0, last, lambda j, c: step(j, c, False), init)
        for d in range(max(1, TQ // TK)):
            carry = step(last + d, carry, True)
        m, l, acc = carry
        o_ref[...] = (acc / l).astype(o_ref.dtype)
        lse_ref[0, 0] = _col_to_row((m + jnp.log2(l)) * _LN2)

    head = lambda blk: pl.BlockSpec((S, FOX_DH), lambda h, i: (0, blk + h))
    return pl.pallas_call(
        body, name=name,
        out_shape=(jax.ShapeDtypeStruct((S, D_MODEL), BF16), jax.ShapeDtypeStruct((FOX_HEADS, nq, 1, TQ), F32)),
        grid=(FOX_HEADS, nq),
        in_specs=[pl.BlockSpec((TQ, FOX_DH), lambda h, i: (i, _QF_BLK + h)), head(_KF_BLK), head(_VF_BLK),
                  pl.BlockSpec((TQ, LANES), lambda h, i: (i, 0)),
                  pl.BlockSpec((1, nk, 1, TK), lambda h, i: (h, 0, 0, 0))],
        out_specs=(pl.BlockSpec((TQ, FOX_DH), lambda h, i: (i, h)),
                   pl.BlockSpec((1, 1, 1, TQ), lambda h, i: (h, i, 0, 0))),
        compiler_params=_cparams(("parallel", "arbitrary")),
    )(pf, pf, pf, fc, fk_row)


def _fox_bwd(dhb, hb, pf, lse_row, fq_row, fc, name):
    S = pf.shape[0]
    TQ, TK = FOX_TQ, FOX_TK
    nq, nk, r = S // TQ, S // TK, TK // TQ
    c1 = _FOX_SCALE * _LOG2E

    def body(q_ref, k_ref, v_ref, do_ref, o_ref, lse_ref, fq_ref, fc_ref,
             dq_ref, dk_ref, dv_ref, dFk_ref, dFq_ref, dq_acc, qside, delta, dk_acc, dv_acc, cs_acc):
        h, j = pl.program_id(0), pl.program_id(1)

        @pl.when(j == 0)
        def _():
            dq_acc[...] = jnp.zeros_like(dq_acc)
            dFq_ref[...] = jnp.zeros_like(dFq_ref)

            def fill(b, _):
                off = pl.multiple_of(b * TQ, TQ)
                prod = do_ref[pl.ds(off, TQ), :].astype(F32) * o_ref[pl.ds(off, TQ), :].astype(F32)
                delta[b] = jnp.sum(prod.T, axis=0, keepdims=True)
                qside[b] = (fq_ref[0, b] - lse_ref[0, b]) * _LOG2E
                return 0

            lax.fori_loop(0, nq, fill, 0)

        kb = k_ref[...]
        vb = v_ref[...]
        fk2 = _lane_pick(fc_ref[...], h) * _LOG2E
        dk_acc[...] = jnp.zeros_like(dk_acc)
        dv_acc[...] = jnp.zeros_like(dv_acc)
        cs_acc[...] = jnp.zeros_like(cs_acc)

        def step(i, masked):
            off = pl.multiple_of(i * TQ, TQ)
            qb = q_ref[pl.ds(off, TQ), :]
            dob = do_ref[pl.ds(off, TQ), :]
            t = _dot_nt(kb, qb) * c1 + qside[i] - fk2
            if masked:
                t = jnp.where(_causal(i * TQ, j * TK, (TK, TQ), 1), t, _NEG)
            p = jnp.exp2(t)
            dv_acc[...] += _dot_nn(p.astype(BF16), dob)
            ds = p * (_dot_nt(vb, dob) - delta[i])
            dsb = ds.astype(BF16)
            dk_acc[...] += _dot_nn(dsb, qb)
            dq_acc[pl.ds(off, TQ), :] += _dot_tn(dsb, kb)
            cs_acc[...] += jnp.sum(ds, axis=1, keepdims=True)
            dFq_ref[0, i] += jnp.sum(ds, axis=0, keepdims=True)

        for d in range(r):
            step(r * j + d, True)

        def rest(i, _):
            step(i, False)
            return 0

        lax.fori_loop(r * j + r, nq, rest, 0)
        dk_ref[...] = (dk_acc[...] * _FOX_SCALE).astype(dk_ref.dtype)
        dv_ref[...] = dv_acc[...].astype(dv_ref.dtype)
        dFk_ref[0, 0] = -_col_to_row(cs_acc[...])

        @pl.when(j == nk - 1)
        def _():
            dq_ref[...] = (dq_acc[...] * _FOX_SCALE).astype(dq_ref.dtype)

    head = lambda blk: pl.BlockSpec((S, FOX_DH), lambda h, j: (0, blk + h))
    kblk = lambda blk: pl.BlockSpec((TK, FOX_DH), lambda h, j: (j, blk + h))
    qrows = pl.BlockSpec((1, nq, 1, TQ), lambda h, j: (h, 0, 0, 0))
    act = jax.ShapeDtypeStruct((S, D_MODEL), BF16)
    return pl.pallas_call(
        body, name=name,
        out_shape=(act, act, act, jax.ShapeDtypeStruct((FOX_HEADS, nk, 1, TK), F32),
                   jax.ShapeDtypeStruct((FOX_HEADS, nq, 1, TQ), F32)),
        grid=(FOX_HEADS, nk),
        in_specs=[head(_QF_BLK), kblk(_KF_BLK), kblk(_VF_BLK), head(0), head(0), qrows, qrows,
                  pl.BlockSpec((TK, LANES), lambda h, j: (j, 0))],
        out_specs=(head(0), kblk(0), kblk(0), pl.BlockSpec((1, 1, 1, TK), lambda h, j: (h, j, 0, 0)), qrows),
        scratch_shapes=[pltpu.VMEM((S, FOX_DH), F32), pltpu.VMEM((nq, 1, TQ), F32), pltpu.VMEM((nq, 1, TQ), F32),
                        pltpu.VMEM((TK, FOX_DH), F32), pltpu.VMEM((TK, FOX_DH), F32), pltpu.VMEM((TK, 1), F32)],
        compiler_params=_cparams(("parallel", "arbitrary")),
    )(pf, pf, pf, dhb, hb, lse_row, fq_row, fc)


def _pad_lanes(v):
    return jnp.pad(v, ((0, 0), (0, LANES - v.shape[1])))


def _local_step(x, target, wmain_t, wsmall_t, rest_arrived, rest_weights, p, on_grads, advance, token):
    S = x.shape[0]
    bi, bf, bff = _pad_lanes(p["b_ml_i"]), _pad_lanes(p["b_ml_f"]), _pad_lanes(p["b_fox_f"])

    h0 = _rmsnorm_fwd(x, p["norm_mix_pre"] + token[0:1, 0:1], "norm_mix_pre")
    pm = _mm(h0, wmain_t, "nt", F32, "proj_mlstm", tm=2048, b_rows=(0, N_ML))
    pf = _mm(h0, wmain_t, "nt", BF16, "proj_fox", tm=2048, b_rows=(N_ML, N_FOX))
    pg = _mm(h0, wmain_t, "nt", F32, "proj_merge", tm=2048, b_rows=(N_ML + N_FOX, N_GATE))
    ps = _mm(h0, wsmall_t, "nt", F32, "proj_gates")
    a, A, wi, em, wk, dec, Fc = _gates_fwd(ps, bi, bf, bff, "gates_fwd")
    a_row = a[:, :8].T
    ha, hp, den, cst, nst = _mlstm_fwd(pm, a_row, A, wi, em, wk, dec, p["ml_head_norm"], "mlstm_fwd")
    ft = Fc[:, :FOX_HEADS].T + rest_arrived(ha)[0, 0]
    fq_row = ft.reshape(FOX_HEADS, S // FOX_TQ, 1, FOX_TQ)
    fk_row = ft.reshape(FOX_HEADS, S // FOX_TK, 1, FOX_TK)
    hb, lse_row = _fox_fwd(pf, Fc, ft.reshape(FOX_HEADS, S // FOX_TK_FWD, 1, FOX_TK_FWD), "fox_fwd")
    wa, wb, wout, wup, wdown = rest_weights(hb)
    ya = _mm(ha, wa, "nn", F32, "branch_a")
    yb = _mm(hb, wb, "nn", F32, "branch_b")
    merged = _merge_fwd(ya, yb, pg, p["b_gate_a"], p["b_gate_b"], "merge_fwd")
    z = _mm(merged, wout, "nn", F32, "out_proj")
    x1, h2 = _resid_norm_fwd(x, z, p["norm_mix_post"], p["norm_ffn_pre"], "resid_mix")
    up = _mm(h2, wup, "nn", F32, "ffn_up", tm=2048)
    act, conv_a, conv_g = _conv_act_fwd(up, p["conv_w"], p["conv_b"], "conv_act_fwd")
    d = _mm(act, wdown, "nn", F32, "ffn_down", tk=D_FF)
    loss_row, dy, dd, g_norm_ffn_post = _loss_head(x1, d, p["norm_ffn_post"], target, "loss_head")
    dact = _mm(dd, wdown, "nt", F32, "d_act", tm=2048)
    g_wdown = _mm(act, dd, "tn", F32, "dw_down", tm=1408, tk=2048)
    dupa, dupg, dcwa, dcwg, dcba, dcbg = _conv_act_bwd(up, conv_a, conv_g, dact, p["conv_w"], "conv_act_bwd")
    g_conv_w = jnp.concatenate([dcwa, dcwg], axis=1)
    g_conv_b = jnp.concatenate([dcba, dcbg], axis=1)
    dh2 = _mm_sum_parts([dupa, dupg], wup, F32, "d_h2", trans_b=True)
    g_wup = _mm(h2, [dupa, dupg], "tn", F32, "dw_up", tk=2048)
    token = on_grads("ffn", dict(w_up=g_wup, w_down=g_wdown))
    dx1, dz, g_norm_ffn_pre, g_norm_mix_post = _norm_chain_bwd(
        dh2, x1, p["norm_ffn_pre"] + token[0:1, 0:1], dy, z, p["norm_mix_post"], "norm_chain_bwd")
    dmerged = _mm(dz, wout, "nt", F32, "d_merged")
    g_wout = _mm(merged, dz, "tn", F32, "dw_out", tk=2048)
    dya, dyb, dga, dgb, g_b_gate_a, g_b_gate_b = _merge_bwd(dmerged, ya, yb, pg, p["b_gate_a"], p["b_gate_b"], "merge_bwd")
    dha = _mm(dya, wa, "nt", F32, "d_ha")
    g_wa = _mm(ha, dya, "tn", F32, "dw_a", tk=2048)
    dhb = _mm(dyb, wb, "nt", BF16, "d_hb")
    g_wb = _mm(hb, dyb, "tn", F32, "dw_b", tk=2048)
    token = advance("ffn", g_wb) + on_grads("mix", dict(w_out=g_wout, w_branch_a=g_wa, w_branch_b=g_wb))
    dqkm, dvm, dom, rk, kc, tch, g_ml_head_norm = _mlstm_bwd(
        dha, pm, hp, den, a_row, A, wi, em, wk, dec, cst, nst, p["ml_head_norm"] + token[0:1, 0:1], "mlstm_bwd")
    token = advance("mix", dqkm)
    dqf, dkf, dvf, dFk, dFq = _fox_bwd(dhb, hb, pf, lse_row.reshape(fq_row.shape), fq_row + token[0, 0], Fc, "fox_bwd")
    dF = jnp.pad((dFk.reshape(FOX_HEADS, S) + dFq.reshape(FOX_HEADS, S)).T, ((0, 0), (0, LANES - FOX_HEADS)))
    dps, dbias = _gates_bwd(ps, bi, bf, bff, rk, kc, tch, dF, "gates_bwd")
    dpm = [dqkm, dvm, dom, dqf, dkf, dvf, dga, dgb]
    g_wmain_t = _mm(dpm, h0, "tn", F32, "dw_main")
    token = on_grads("in", dict(w_in=g_wmain_t))
    g_wsmall_t = _mm(dps, h0, "tn", F32, "dw_gates")
    dh0s = _mm(dps, wsmall_t + token[0:1, 0:1].astype(BF16), "nn", F32, "d_h0_gates")
    token = advance("in", dh0s)
    dh0 = _mm_sum_parts(dpm, wmain_t, F32, "d_h0_main", after=token)
    grad_x, g_norm_mix_pre = _rmsnorm_bwd([dh0, dh0s], x, p["norm_mix_pre"], dx1, F32, "norm_mix_pre_bwd")

    big = dict(wsmall_t=g_wsmall_t)
    small = dict(norm_mix_pre=g_norm_mix_pre, ml_head_norm=g_ml_head_norm, b_gate_a=g_b_gate_a, b_gate_b=g_b_gate_b,
                 norm_mix_post=g_norm_mix_post, norm_ffn_pre=g_norm_ffn_pre, norm_ffn_post=g_norm_ffn_post,
                 conv_b=g_conv_b, b_ml_i=dbias[:, 0:ML_HEADS], b_ml_f=dbias[:, LANES:LANES + ML_HEADS],
                 b_fox_f=dbias[:, 2 * LANES:2 * LANES + FOX_HEADS], conv_w=g_conv_w)
    return loss_row, grad_x, big, small


def _row_tile(r, target=256):
    best = None
    for t in range(8, min(r, target) + 1, 8):
        if r % t == 0:
            best = t
    return best if best is not None else r


def _adamw(w, g, m, v, name):
    _, R, C = w.shape
    tr = _row_tile(R)
    tc = C
    if tr == R and R > 256:
        tc = 256

    def body(w_ref, g_ref, m_ref, v_ref, d_ref, mo_ref, vo_ref):
        gv = g_ref[...]
        mn = ADAM_B1 * m_ref[0] + (1.0 - ADAM_B1) * gv
        vn = ADAM_B2 * v_ref[0] + (1.0 - ADAM_B2) * (gv * gv)
        m_hat = mn / (1.0 - ADAM_B1 ** ADAM_STEP)
        v_hat = vn / (1.0 - ADAM_B2 ** ADAM_STEP)
        d_ref[0] = -ADAM_LR * (m_hat / (jnp.sqrt(v_hat) + ADAM_EPS) + ADAM_WD * w_ref[0])
        mo_ref[0] = mn
        vo_ref[0] = vn

    blk = pl.BlockSpec((1, tr, tc), lambda i, j: (0, i, j))
    o = jax.ShapeDtypeStruct((1, R, C), F32)
    return pl.pallas_call(
        body, name=name, out_shape=(o, o, o), grid=(R // tr, C // tc),
        in_specs=[blk, pl.BlockSpec((tr, tc), lambda i, j: (i, j)), blk, blk], out_specs=(blk,) * 3,
        compiler_params=_cparams(("parallel", "parallel")),
    )(w, g, m, v)


ANY = pl.BlockSpec(memory_space=pl.ANY)


def _place():
    x, y, c = lax.axis_index("x"), lax.axis_index("y"), lax.axis_index("c")
    chips = [(1 - x, y), (x, 1 - y), (1 - x, 1 - y)]
    return x, y, c, chips


def _block(ref, kind, k, rows=None):
    if kind == "rows":
        return ref.at[k] if rows is None else ref.at[k, pl.ds(*rows), :]
    cb = ref.shape[1] // 4
    return ref.at[:, pl.ds(k * cb, cb)] if rows is None else ref.at[pl.ds(*rows), pl.ds(k * cb, cb)]


def _gathered_shape(s, kind):
    return (4,) + s.shape if kind == "rows" else (s.shape[0], 4 * s.shape[1])


def _gather_weights(shards, kinds, smalls):
    n, ns = len(shards), len(smalls)

    def body(*refs):
        ins, sm_in = refs[:n], refs[n:n + ns]
        outs, sm_out = refs[n + ns:2 * n + ns], refs[2 * n + ns:2 * (n + ns)]
        send_sems, recv_sems, sm_send, sm_recv, local_sems = refs[2 * (n + ns):]
        x, y, c, chips = _place()
        sibling = (x, y, 1 - c)
        kme = 2 * x + y

        def half(a, k, hc):
            h = ins[a].shape[0] // 2
            return _block(outs[a], kinds[a], k, (hc * h, h))

        def remote(a, slot, src, dst, to):
            return pltpu.make_async_remote_copy(src_ref=src, dst_ref=dst, send_sem=send_sems.at[a * 7 + slot],
                                                recv_sem=recv_sems.at[a * 7 + slot], device_id=to, device_id_type=MESH)

        def sm_copy(b, j, k, to):
            return pltpu.make_async_remote_copy(src_ref=sm_in[b], dst_ref=sm_out[b].at[k], send_sem=sm_send.at[3 * b + j],
                                                recv_sem=sm_recv.at[3 * b + j], device_id=to, device_id_type=MESH)

        local = [pltpu.make_async_copy(sm_in[b], sm_out[b].at[kme], local_sems.at[b]) for b in range(ns)]
        for cp in local:
            cp.start()
        sends = [remote(a, 6, ins[a], _block(outs[a], kinds[a], kme), sibling) for a in range(n)]
        for a in range(n):
            h = ins[a].shape[0] // 2
            for j, chip in enumerate(chips):
                sends.append(remote(a, j, ins[a].at[pl.ds(c * h, h), :], half(a, kme, c), (*chip, c)))
        for b in range(ns):
            for j, chip in enumerate(chips):
                sends.append(sm_copy(b, j, kme, (*chip, c)))
        for cp in sends:
            cp.start()
        for a in range(n):
            for j, chip in enumerate(chips):
                kj = 2 * chip[0] + chip[1]
                remote(a, j, half(a, kj, c), half(a, kj, c), (*chip, c)).wait_recv()
                fwd = remote(a, 3 + j, half(a, kj, c), half(a, kj, c), sibling)
                fwd.start()
                sends.append(fwd)
        for a in range(n):
            for j, chip in enumerate(chips):
                kj = 2 * chip[0] + chip[1]
                remote(a, 3 + j, half(a, kj, 1 - c), half(a, kj, 1 - c), sibling).wait_recv()
        for b in range(ns):
            for j, chip in enumerate(chips):
                sm_copy(b, j, 2 * chip[0] + chip[1], (*chip, c)).wait_recv()
        for a in range(n):
            remote(a, 6, ins[a], _block(outs[a], kinds[a], kme), sibling).wait_recv()
        for cp in sends:
            cp.wait_send()
        for cp in local:
            cp.wait()

    outs = pl.pallas_call(
        body, name="gather_weights",
        out_shape=tuple([jax.ShapeDtypeStruct(_gathered_shape(s, k), s.dtype) for s, k in zip(shards, kinds)]
                        + [jax.ShapeDtypeStruct((4,) + s.shape, s.dtype) for s in smalls]),
        in_specs=[ANY] * (n + ns), out_specs=tuple([ANY] * (n + ns)),
        scratch_shapes=[pltpu.SemaphoreType.DMA((7 * n,)), pltpu.SemaphoreType.DMA((7 * n,)),
                        pltpu.SemaphoreType.DMA((3 * ns,)), pltpu.SemaphoreType.DMA((3 * ns,)),
                        pltpu.SemaphoreType.DMA((ns,))],
    )(*shards, *smalls)
    return outs[:n], outs[n:]


_IN_HBM = pl.BlockSpec(memory_space=pltpu.HBM)
_SEMS = pl.BlockSpec(memory_space=pltpu.SEMAPHORE)
_DATAFLOW = pltpu.SideEffectType.DATAFLOW_SIDE_EFFECTING


def _hbm(t):
    return pltpu.HBM(t.shape, t.dtype)


def _gather_copies(ins, outs, send_sems, recv_sems, kinds):
    x, y, c, chips = _place()
    kme = 2 * x + y
    cps = []
    for a in range(len(ins)):
        h = ins[a].shape[0] // 2
        for j, chip in enumerate(chips + [None]):
            to = (x, y, 1 - c) if chip is None else (*chip, c)
            src = ins[a] if chip is None else ins[a].at[pl.ds(c * h, h), :]
            dst = _block(outs[a], kinds[a], kme, None if chip is None else (c * h, h))
            cps.append(pltpu.make_async_remote_copy(src_ref=src, dst_ref=dst, send_sem=send_sems.at[4 * a + j],
                                                    recv_sem=recv_sems.at[4 * a + j], device_id=to, device_id_type=MESH))
    return cps


def _gather_start(shards, kinds, name):
    n = len(shards)
    outs = [lax.empty(_gathered_shape(s, k), s.dtype) for s, k in zip(shards, kinds)]

    def body(*refs):
        for cp in _gather_copies(refs[:n], refs[n:2 * n], refs[2 * n], refs[2 * n + 1], kinds):
            cp.start()
        refs[-1][...] = jnp.zeros_like(refs[-1])

    return pl.pallas_call(
        body, name=name,
        out_shape=(pltpu.SemaphoreType.DMA((4 * n,)), pltpu.SemaphoreType.DMA((4 * n,)),
                   *[_hbm(t) for t in shards], *[_hbm(t) for t in outs], jax.ShapeDtypeStruct((8, LANES), F32)),
        in_specs=[_IN_HBM] * (2 * n),
        out_specs=(_SEMS, _SEMS, *[_IN_HBM] * (2 * n), pl.BlockSpec(memory_space=pltpu.VMEM)),
        input_output_aliases={a: 2 + a for a in range(2 * n)},
        compiler_params=pltpu.CompilerParams(has_side_effects=_DATAFLOW),
    )(*[pltpu.with_memory_space_constraint(t, pltpu.HBM) for t in list(shards) + outs])


def _gather_wait(started, after, kinds, name):
    n = (len(started) - 3) // 2
    bufs = started[2:2 + 2 * n]

    def body(*refs):
        for cp in _gather_copies(refs[:n], refs[n:2 * n], refs[2 * n], refs[2 * n + 1], kinds):
            cp.wait_send()
            cp.wait_recv()

    outs = pl.pallas_call(
        body, name=name, out_shape=tuple(_hbm(t) for t in bufs),
        in_specs=[_IN_HBM] * (2 * n) + [_SEMS, _SEMS, ANY], out_specs=tuple([_IN_HBM] * (2 * n)),
        input_output_aliases={a: a for a in range(2 * n)},
        compiler_params=pltpu.CompilerParams(has_side_effects=_DATAFLOW),
    )(*bufs, started[0], started[1], after)
    return outs[n:]


def _relay_copies(bufs, send_sems, recv_sems, kinds):
    x, y, c, chips = _place()
    cps = []
    for a in range(len(bufs)):
        h = (bufs[a].shape[1] if kinds[a] == "rows" else bufs[a].shape[0]) // 2
        for j, chip in enumerate(chips):
            part = _block(bufs[a], kinds[a], 2 * chip[0] + chip[1], (c * h, h))
            cps.append(pltpu.make_async_remote_copy(src_ref=part, dst_ref=part, send_sem=send_sems.at[3 * a + j],
                                                    recv_sem=recv_sems.at[3 * a + j], device_id=(x, y, 1 - c),
                                                    device_id_type=MESH))
    return cps


def _join_copies(bufs, send_sems, recv_sems, kinds):
    x, y, c, _ = _place()
    cps = []
    for a in range(len(bufs)):
        h = bufs[a].shape[0] // 2
        mine = bufs[a].at[pl.ds(c * h, h), :]
        cps.append(pltpu.make_async_remote_copy(src_ref=mine, dst_ref=mine, send_sem=send_sems.at[a],
                                                recv_sem=recv_sems.at[a], device_id=(x, y, 1 - c), device_id_type=MESH))
    return cps


def _inplace_start(copies, per_array, bufs, kinds, name):
    n = len(bufs)

    def body(*refs):
        for cp in copies(refs[:n], refs[n], refs[n + 1], kinds):
            cp.start()
        refs[-1][...] = jnp.zeros_like(refs[-1])

    return pl.pallas_call(
        body, name=name,
        out_shape=(pltpu.SemaphoreType.DMA((per_array * n,)), pltpu.SemaphoreType.DMA((per_array * n,)),
                   *[_hbm(t) for t in bufs], jax.ShapeDtypeStruct((8, LANES), F32)),
        in_specs=[_IN_HBM] * n, out_specs=(_SEMS, _SEMS, *[_IN_HBM] * n, pl.BlockSpec(memory_space=pltpu.VMEM)),
        input_output_aliases={a: 2 + a for a in range(n)},
        compiler_params=pltpu.CompilerParams(has_side_effects=_DATAFLOW),
    )(*[pltpu.with_memory_space_constraint(t, pltpu.HBM) for t in bufs])


def _inplace_wait(copies, started, after, kinds, name):
    n = len(started) - 3
    bufs = started[2:2 + n]

    def body(*refs):
        for cp in copies(refs[:n], refs[n], refs[n + 1], kinds):
            cp.wait_send()
            cp.wait_recv()

    return pl.pallas_call(
        body, name=name, out_shape=tuple(_hbm(t) for t in bufs),
        in_specs=[_IN_HBM] * n + [_SEMS, _SEMS, ANY], out_specs=tuple([_IN_HBM] * n),
        input_output_aliases={a: a for a in range(n)},
        compiler_params=pltpu.CompilerParams(has_side_effects=_DATAFLOW),
    )(*bufs, started[0], started[1], after)


def _add_halves(g, r1, cvec, kind, name):
    def body(c_ref, g_ref, r_ref, o_ref):
        o_ref[...] = (g_ref[...] + r_ref[...]).astype(o_ref.dtype)

    if kind == "rows":
        _, h, C = r1.shape
        tr = _row_tile(h, 512)
        nt = h // tr
        grid = (4, nt)
        g_spec = pl.BlockSpec((1, tr, C), lambda k, i, c_ref: (k, c_ref[0] * nt + i, 0))
        r_spec = pl.BlockSpec((1, tr, C), lambda k, i, c_ref: (k, i, 0))
    else:
        h, C4 = r1.shape
        tr, tc = _row_tile(h, 512), C4 // 4
        nt = h // tr
        grid = (nt, 4)
        g_spec = pl.BlockSpec((tr, tc), lambda i, k, c_ref: (c_ref[0] * nt + i, k))
        r_spec = pl.BlockSpec((tr, tc), lambda i, k, c_ref: (i, k))
    return pl.pallas_call(
        body, name=name, out_shape=jax.ShapeDtypeStruct(r1.shape, BF16),
        grid_spec=pltpu.PrefetchScalarGridSpec(num_scalar_prefetch=1, grid=grid, in_specs=[g_spec, r_spec],
                                               out_specs=r_spec),
        compiler_params=_cparams(("parallel", "parallel")),
    )(cvec, g, r1)


def _chip_copies(ins, lands, send_sems, recv_sems, kinds):
    x, y, c, chips = _place()
    return [pltpu.make_async_remote_copy(
        src_ref=_block(ins[a], kinds[a], 2 * chip[0] + chip[1]), dst_ref=lands[a].at[j],
        send_sem=send_sems.at[3 * a + j], recv_sem=recv_sems.at[3 * a + j], device_id=(*chip, c), device_id_type=MESH)
        for a in range(len(ins)) for j, chip in enumerate(chips)]


def _land_shape(s, kind):
    return (3,) + (s.shape[1:] if kind == "rows" else (s.shape[0], s.shape[1] // 4))


def _sibling_copies(ins, lands, send_sems, recv_sems, kinds):
    x, y, c, _ = _place()
    cps = []
    for a in range(len(ins)):
        h = lands[a].shape[-2]
        src = ins[a].at[:, pl.ds((1 - c) * h, h), :] if kinds[a] == "rows" else ins[a].at[pl.ds((1 - c) * h, h), :]
        cps.append(pltpu.make_async_remote_copy(src_ref=src, dst_ref=lands[a], send_sem=send_sems.at[a],
                                                recv_sem=recv_sems.at[a], device_id=(x, y, 1 - c), device_id_type=MESH))
    return cps


def _half_shape(g, kind):
    return (4, g.shape[1] // 2, g.shape[2]) if kind == "rows" else (g.shape[0] // 2, g.shape[1])


def _exchange_start(copies, per_array, srcs, land_shapes, kinds, name, zeroed=False):
    n = len(srcs)
    lands = [(jnp.zeros if zeroed else lax.empty)(shape, s.dtype) for shape, s in zip(land_shapes, srcs)]

    def body(*refs):
        for cp in copies(refs[:n], refs[n:2 * n], refs[2 * n], refs[2 * n + 1], kinds):
            cp.start()
        refs[-1][...] = jnp.zeros_like(refs[-1])

    return pl.pallas_call(
        body, name=name,
        out_shape=(pltpu.SemaphoreType.DMA((per_array * n,)), pltpu.SemaphoreType.DMA((per_array * n,)),
                   *[_hbm(t) for t in srcs], *[_hbm(t) for t in lands], jax.ShapeDtypeStruct((8, LANES), F32)),
        in_specs=[_IN_HBM] * (2 * n),
        out_specs=(_SEMS, _SEMS, *[_IN_HBM] * (2 * n), pl.BlockSpec(memory_space=pltpu.VMEM)),
        input_output_aliases={a: 2 + a for a in range(2 * n)},
        compiler_params=pltpu.CompilerParams(has_side_effects=_DATAFLOW),
    )(*[pltpu.with_memory_space_constraint(t, pltpu.HBM) for t in list(srcs) + lands])


def _exchange_wait(copies, started, after, kinds, name):
    n = (len(started) - 3) // 2
    bufs = started[2:2 + 2 * n]

    def body(*refs):
        for cp in copies(refs[:n], refs[n:2 * n], refs[2 * n], refs[2 * n + 1], kinds):
            cp.wait_send()
            cp.wait_recv()

    outs = pl.pallas_call(
        body, name=name, out_shape=tuple(_hbm(t) for t in bufs),
        in_specs=[_IN_HBM] * (2 * n) + [_SEMS, _SEMS, ANY], out_specs=tuple([_IN_HBM] * (2 * n)),
        input_output_aliases={a: a for a in range(2 * n)},
        compiler_params=pltpu.CompilerParams(has_side_effects=_DATAFLOW),
    )(*bufs, started[0], started[1], after)
    return outs[:n], outs[n:]


def _add_chips(s1, r2, kcvec, kind, name):
    _, h, C = r2.shape
    tr = _row_tile(h, 512)
    nt = h // tr

    def body(kc_ref, s_ref, r0_ref, r1_ref, r2_ref, o_ref):
        s = s_ref[0] if kind == "rows" else s_ref[...]
        o_ref[...] = ((s.astype(F32) + r0_ref[0].astype(F32)) + r1_ref[0].astype(F32)) + r2_ref[0].astype(F32)

    peer = lambda j: pl.BlockSpec((1, tr, C), lambda i, kc_ref: (j, i, 0))
    if kind == "rows":
        s_spec = pl.BlockSpec((1, tr, C), lambda i, kc_ref: (kc_ref[0], i, 0))
    else:
        s_spec = pl.BlockSpec((tr, C), lambda i, kc_ref: (i, kc_ref[0]))
    return pl.pallas_call(
        body, name=name, out_shape=jax.ShapeDtypeStruct((2 * h, C), F32),
        grid_spec=pltpu.PrefetchScalarGridSpec(
            num_scalar_prefetch=1, grid=(nt,),
            in_specs=[s_spec, peer(0), peer(1), peer(2)],
            out_specs=pl.BlockSpec((tr, C), lambda i, kc_ref: (kc_ref[1] * nt + i, 0))),
        compiler_params=_cparams(("parallel",)),
    )(kcvec, s1, r2, r2, r2)


N_DEV = 8


def _spread_copies(packs, lands, send_sems, recv_sems, kinds):
    x, y, c, _ = _place()
    me = 4 * x + 2 * y + c
    return [pltpu.make_async_remote_copy(
        src_ref=packs[0], dst_ref=lands[0].at[me], send_sem=send_sems.at[mask - 1], recv_sem=recv_sems.at[mask - 1],
        device_id=(1 - x if mask & 4 else x, 1 - y if mask & 2 else y, 1 - c if mask & 1 else c), device_id_type=MESH)
        for mask in range(1, N_DEV)]


def _sum_spread(pack, gathered):
    P = pack.shape[0]

    def body(p_ref, g_ref, o_ref):
        x, y, c, _ = _place()
        me = 4 * x + 2 * y + c
        acc = None
        for i in range(N_DEV):
            term = jnp.where(me == i, p_ref[...], g_ref[i])
            acc = term if acc is None else acc + term
        o_ref[...] = acc

    vmem = pl.BlockSpec(memory_space=pltpu.VMEM)
    return pl.pallas_call(body, name="allreduce_sum", out_shape=jax.ShapeDtypeStruct((P, LANES), F32),
                          in_specs=[vmem, vmem], out_specs=vmem)(pack, gathered)


def _pack_rows(arrs):
    rows = []
    for a in arrs:
        f = a.reshape(-1)
        f = jnp.pad(f, (0, (-f.shape[0]) % (8 * LANES)))
        rows.append(f.reshape(-1, LANES))
    return jnp.concatenate(rows, axis=0)


def _unpack_rows(pack, shapes):
    out, r = [], 0
    for s in shapes:
        n = math.prod(s)
        out.append(pack[r:r + -(-n // LANES)].reshape(-1)[:n].reshape(s))
        r += 8 * -(-n // (8 * LANES))
    return out


_SMALL = ["norm_mix_pre", "ml_head_norm", "b_gate_a", "b_gate_b", "norm_mix_post", "norm_ffn_pre", "norm_ffn_post",
          "conv_b", "b_ml_i", "b_ml_f", "b_fox_f"]
_BIG = ["w_in", "w_branch_a", "w_branch_b", "w_out", "w_up", "w_down"]
_WEIGHTS = ['norm_mix_pre', 'w_in', 'b_ml_i', 'b_ml_f', 'ml_head_norm', 'b_fox_f', 'b_gate_a', 'b_gate_b', 'w_branch_a',
            'w_branch_b', 'w_out', 'norm_mix_post', 'norm_ffn_pre', 'w_up', 'conv_w', 'conv_b', 'w_down', 'norm_ffn_post']


_KINDS = ["rows", "rows", "rows", "rows", "cols", "rows"]


def kernel(x, norm_mix_pre, w_in, b_ml_i, b_ml_f, ml_head_norm, b_fox_f, b_gate_a, b_gate_b, w_branch_a, w_branch_b, w_out, norm_mix_post, norm_ffn_pre, w_up, conv_w, conv_b, w_down, norm_ffn_post, loss_target, m_norm_mix_pre, m_w_in, m_b_ml_i, m_b_ml_f, m_ml_head_norm, m_b_fox_f, m_b_gate_a, m_b_gate_b, m_w_branch_a, m_w_branch_b, m_w_out, m_norm_mix_post, m_norm_ffn_pre, m_w_up, m_conv_w, m_conv_b, m_w_down, m_norm_ffn_post, v_norm_mix_pre, v_w_in, v_b_ml_i, v_b_ml_f, v_ml_head_norm, v_b_fox_f, v_b_gate_a, v_b_gate_b, v_w_branch_a, v_w_branch_b, v_w_out, v_norm_mix_post, v_norm_ffn_pre, v_w_up, v_conv_w, v_conv_b, v_w_down, v_norm_ffn_post):
    args = dict(locals())
    w = {n: args[n] for n in _WEIGHTS}
    mom = {n: args["m_" + n] for n in _WEIGHTS}
    var = {n: args["v_" + n] for n in _WEIGHTS}
    cx, cy, cc = lax.axis_index("x"), lax.axis_index("y"), lax.axis_index("c")
    kme = 2 * cx + cy
    cvec = jnp.reshape(cc, (1,)).astype(jnp.int32)
    kcvec = jnp.stack([kme, cc]).astype(jnp.int32)
    odd = kme % 2

    tr3 = lambda t: jnp.transpose(t, (0, 2, 1))
    w["w_in"], mom["w_in"], var["w_in"] = tr3(w_in), tr3(m_w_in), tr3(v_w_in)
    w_in_main = lax.dynamic_slice_in_dim(w["w_in"][0], 4 * odd, 2048, axis=0).astype(BF16)
    w_in_gates = lax.dynamic_slice_in_dim(w["w_in"][0], 2048 * (1 - odd), 4, axis=0).astype(BF16)
    (wmain_t,), (g_cw, g_gates) = _gather_weights([w_in_main], _KINDS[:1], [w["conv_w"][0], w_in_gates])
    rest_started = _gather_start([w[n][0].astype(BF16) for n in _BIG[1:]], _KINDS[1:], "gather_rest_start")

    relay = {}

    def rest_arrived(after):
        bufs = _gather_wait(rest_started, after, _KINDS[1:], "gather_rest_wait")
        relay["started"] = _inplace_start(_relay_copies, 3, bufs, _KINDS[1:], "gather_rest_relay_start")
        return relay["started"][-1]

    def rest_weights(after):
        g_a, g_b, g_out, wup, g_down = _inplace_wait(_relay_copies, relay["started"], after, _KINDS[1:],
                                                     "gather_rest_relay_wait")
        return full(g_a), full(g_b), full(g_out), wup, full(g_down)
    gate_rows = g_gates.reshape(16, D_MODEL)
    wsmall_t = jnp.zeros((N_SMALL, D_MODEL), BF16)
    for blk, (lo, hi) in enumerate(((0, 4), (4, 8), (8, 16))):
        wsmall_t = wsmall_t.at[blk * LANES:blk * LANES + hi - lo].set(gate_rows[lo:hi])
    full = lambda g: g.reshape(-1, g.shape[2])
    p = {n: w[n] for n in _SMALL}
    p["conv_w"] = jnp.transpose(g_cw, (1, 0, 2)).reshape(3, -1)

    groups = {}

    def on_grads(group, gs):
        names = list(gs)
        kinds = [_KINDS[_BIG.index(n)] for n in names]
        whole = [g if k == "cols" else g.reshape(4, -1, g.shape[1]) for g, k in zip(gs.values(), kinds)]
        started = _exchange_start(_sibling_copies, 1, whole, [_half_shape(g, k) for g, k in zip(whole, kinds)], kinds,
                                  "grads_to_sibling_start_" + group)
        groups[group] = dict(names=names, kinds=kinds, sibling=started)
        return started[-1]

    def advance(group, after):
        G = groups[group]
        whole, got = _exchange_wait(_sibling_copies, G["sibling"], after, G["kinds"], "grads_to_sibling_wait_" + group)
        sums = [_add_halves(g, r, cvec, k, "add_sibling_" + n) for g, r, k, n in zip(whole, got, G["kinds"], G["names"])]
        G["chips"] = _exchange_start(_chip_copies, 3, sums, [_land_shape(s, k) for s, k in zip(sums, G["kinds"])],
                                     G["kinds"], "grads_to_chips_start_" + group)
        return G["chips"][-1]

    loss_row, grad_x, big, small = _local_step(x[0], loss_target[0], full(wmain_t), wsmall_t, rest_arrived, rest_weights,
                                               p, on_grads, advance, rest_started[-1])
    gt = big["wsmall_t"]
    small["w_in_gates"] = jnp.concatenate([gt[0:4], gt[LANES:LANES + 4], gt[2 * LANES:2 * LANES + 8]], axis=0)
    small_names = _SMALL + ["conv_w"]
    packed_names = small_names + ["w_in_gates"]
    pack = _pack_rows([small[n] for n in packed_names] + [loss_row])
    spread = _exchange_start(_spread_copies, N_DEV - 1, [pack], [(N_DEV,) + pack.shape], None, "allreduce_start", zeroed=True)

    def my_half(group, after):
        G = groups[group]
        sums, got = _exchange_wait(_chip_copies, G["chips"], after, G["kinds"], "grads_to_chips_wait_" + group)
        return [_add_chips(s, r, kcvec, k, "add_chips_" + n) for s, r, k, n in zip(sums, got, G["kinds"], G["names"])]

    first_names = groups["ffn"]["names"] + groups["mix"]["names"]
    join_first = _inplace_start(_join_copies, 1, my_half("ffn", spread[-1]) + my_half("mix", spread[-1]), None,
                                "grads_join_start")
    join_in = _inplace_start(_join_copies, 1, my_half("in", join_first[-1]), None, "grads_join_start_in")
    grads = dict(zip(first_names, _inplace_wait(_join_copies, join_first, join_in[-1], None, "grads_join_wait")))

    delta, new_m, new_v = {}, {}, {}
    for n in _BIG[1:]:
        delta[n], new_m[n], new_v[n] = _adamw(w[n], grads[n], mom[n], var[n], "adamw_" + n)
        grads[n] = grads[n][None]
    grads["w_in"], = _inplace_wait(_join_copies, join_in, delta[_BIG[-1]], None, "grads_join_wait_in")

    (pack,), (gathered,) = _exchange_wait(_spread_copies, spread, delta[_BIG[-1]], None, "allreduce_wait")
    full_shapes = [small[n].shape if n in ("conv_w", "w_in_gates") else w[n][0].shape for n in packed_names]
    total = _unpack_rows(_sum_spread(pack, gathered), full_shapes + [loss_row.shape])
    for n, t in zip(packed_names, total):
        grads[n] = t
    loss = total[-1][0, 0]
    grads["conv_w"] = lax.dynamic_slice_in_dim(grads["conv_w"], kme * conv_w.shape[2], conv_w.shape[2], axis=1)
    my_gates = lax.dynamic_slice_in_dim(grads.pop("w_in_gates"), 4 * kme, 4, axis=0)
    g_in = jnp.zeros(w["w_in"].shape[1:], F32)
    g_in = lax.dynamic_update_slice_in_dim(g_in, grads["w_in"], 4 * odd, axis=0)
    grads["w_in"] = lax.dynamic_update_slice_in_dim(g_in, my_gates, 2048 * (1 - odd), axis=0)
    delta["w_in"], new_m["w_in"], new_v["w_in"] = _adamw(w["w_in"], grads["w_in"], mom["w_in"], var["w_in"], "adamw_w_in")
    grads["w_in"] = grads["w_in"][None]
    for d in (grads, delta, new_m, new_v):
        d["w_in"] = tr3(d["w_in"])
    packs = [_pack_rows([d[n][0] for n in small_names]) for d in (w, mom, var)]
    pad = ((0, (-packs[0].shape[0]) % 8), (0, 0))
    packs = [jnp.pad(t, pad)[None] for t in packs]
    gp = jnp.pad(_pack_rows([grads[n] for n in small_names]), pad)
    shapes = [w[n][0].shape for n in small_names]
    for dst, res in zip((delta, new_m, new_v), _adamw(packs[0], gp, packs[1], packs[2], "adamw_small")):
        for n, t in zip(small_names, _unpack_rows(res[0], shapes)):
            dst[n] = t[None]
    for n in small_names:
        grads[n] = grads[n][None]

    return (loss, grad_x[None], *[grads[n] for n in _WEIGHTS], *[delta[n] for n in _WEIGHTS],
            *[new_m[n] for n in _WEIGHTS], *[new_v[n] for n in _WEIGHTS])
```
